```python
import math
import jax, jax.numpy as jnp
from jax import lax
import numpy as np

D_MODEL = 1024
BATCH = 8
SEQ = 2048
DEPTH = 2

N_MEM = 256
HEAD_DIM = 128
DN_HEADS = D_MODEL // HEAD_DIM
FOX_HEADS = D_MODEL // HEAD_DIM
MEM_HEADS = 4
MIX_WIDTH = D_MODEL
MEM_WIDTH = MEM_HEADS * HEAD_DIM
D_FF = 4 * D_MODEL
CONV_WIDTH = 4
CHUNK = 64
Q_BLOCK = 128
EPS = 1e-6
N_DN = (DEPTH + 1) // 2
N_FOX = DEPTH // 2
DN_IN = 4 * MIX_WIDTH + 2 * DN_HEADS + MEM_WIDTH
FOX_IN = 4 * MIX_WIDTH + FOX_HEADS + MEM_WIDTH
OUT_IN = MIX_WIDTH + MEM_WIDTH

kernel_name = "hybrid_deltanet_fox_memory_decoder"


def rms_norm(x, w):
    xf = x.astype(jnp.float32)
    y = xf * lax.rsqrt(jnp.mean(xf * xf, axis=-1, keepdims=True) + EPS)
    return (y * w.astype(jnp.float32)).astype(x.dtype)


def l2_norm(x):
    xf = x.astype(jnp.float32)
    return xf * lax.rsqrt(jnp.sum(xf * xf, axis=-1, keepdims=True) + EPS)


def split_heads(t, heads):
    return t.reshape(t.shape[:-1] + (heads, HEAD_DIM))


def causal_depthwise_conv(x, w):
    c = x.shape[-1]
    return lax.conv_general_dilated(
        x, w[:, None, :].astype(x.dtype), window_strides=(1,),
        padding=[(CONV_WIDTH - 1, 0)], dimension_numbers=("NWC", "WIO", "NWC"),
        feature_group_count=c)


def chunked_gated_delta_rule(q, k, v, g, beta):
    bsz, seq, heads, _ = q.shape
    n = seq // CHUNK

    def chunks(t):
        return t.reshape(bsz, n, CHUNK, heads, -1).transpose(1, 0, 3, 2, 4)

    qc, kc, vc = chunks(q), chunks(k), chunks(v)
    gc = chunks(g[..., None])[..., 0]
    bc = chunks(beta[..., None])[..., 0]
    gcum = jnp.cumsum(gc, axis=-1)
    causal = jnp.tril(jnp.ones((CHUNK, CHUNK), dtype=bool))
    strict = jnp.tril(jnp.ones((CHUNK, CHUNK), dtype=bool), -1)
    diff = gcum[..., :, None] - gcum[..., None, :]
    decay = jnp.where(causal, jnp.exp(jnp.where(causal, diff, 0.0)), 0.0)
    kb = kc * bc[..., None]
    a_mat = jnp.where(strict, jnp.einsum("nbhik,nbhjk->nbhij", kb, kc) * decay, 0.0)
    eye = jnp.eye(CHUNK, dtype=jnp.float32)
    t_mat = lax.linalg.triangular_solve(
        eye + a_mat, jnp.broadcast_to(eye, a_mat.shape), left_side=True, lower=True)
    u = jnp.einsum("nbhij,nbhjv->nbhiv", t_mat, vc * bc[..., None])
    w = jnp.einsum("nbhij,nbhjk->nbhik", t_mat, kb * jnp.exp(gcum)[..., None])
    qk = jnp.where(causal, jnp.einsum("nbhik,nbhjk->nbhij", qc, kc) * decay, 0.0)

    def step(state, inp):
        q_i, k_i, u_i, w_i, qk_i, g_i = inp
        v_new = u_i - jnp.einsum("bhck,bhkv->bhcv", w_i, state)
        out = (jnp.einsum("bhck,bhkv->bhcv", q_i * jnp.exp(g_i)[..., None], state)
               + jnp.einsum("bhij,bhjv->bhiv", qk_i, v_new))
        g_last = g_i[..., -1:]
        k_dec = k_i * jnp.exp(g_last - g_i)[..., None]
        state = state * jnp.exp(g_last)[..., None] + jnp.einsum("bhck,bhcv->bhkv", k_dec, v_new)
        return state, out

    state0 = jnp.zeros((bsz, heads, q.shape[-1], v.shape[-1]), jnp.float32)
    _, o = lax.scan(step, state0, (qc, kc, u, w, qk, gcum))
    return o.transpose(1, 0, 3, 2, 4).reshape(bsz, seq, heads, -1)


def gated_deltanet(h, w_in, conv_w, a_log, dt_bias, o_norm_w):
    bsz, seq, _ = h.shape
    proj = h @ w_in
    qkv = proj[..., : 3 * MIX_WIDTH]
    z = proj[..., 3 * MIX_WIDTH: 4 * MIX_WIDTH]
    a = proj[..., 4 * MIX_WIDTH: 4 * MIX_WIDTH + DN_HEADS]
    b = proj[..., 4 * MIX_WIDTH + DN_HEADS: 4 * MIX_WIDTH + 2 * DN_HEADS]
    q_mem = proj[..., 4 * MIX_WIDTH + 2 * DN_HEADS:]
    qkv = jax.nn.silu(causal_depthwise_conv(qkv, conv_w))
    q = l2_norm(split_heads(qkv[..., :MIX_WIDTH], DN_HEADS)) * (HEAD_DIM ** -0.5)
    k = l2_norm(split_heads(qkv[..., MIX_WIDTH: 2 * MIX_WIDTH], DN_HEADS))
    v = split_heads(qkv[..., 2 * MIX_WIDTH:], DN_HEADS).astype(jnp.float32)
    beta = jax.nn.sigmoid(b.astype(jnp.float32))
    g = -jnp.exp(a_log.astype(jnp.float32)) * jax.nn.softplus(
        a.astype(jnp.float32) + dt_bias.astype(jnp.float32))
    o = chunked_gated_delta_rule(q, k, v, g, beta)
    o = rms_norm(o, o_norm_w) * jax.nn.silu(split_heads(z, DN_HEADS).astype(jnp.float32))
    return o.reshape(bsz, seq, MIX_WIDTH).astype(h.dtype), q_mem


def forgetting_attention(h, w_in, f_bias, q_norm_w, k_norm_w):
    bsz, seq, _ = h.shape
    proj = h @ w_in
    q = split_heads(proj[..., :MIX_WIDTH], FOX_HEADS)
    k = split_heads(proj[..., MIX_WIDTH: 2 * MIX_WIDTH], FOX_HEADS)
    v = split_heads(proj[..., 2 * MIX_WIDTH: 3 * MIX_WIDTH], FOX_HEADS)
    gate = proj[..., 3 * MIX_WIDTH: 4 * MIX_WIDTH]
    f_logit = proj[..., 4 * MIX_WIDTH: 4 * MIX_WIDTH + FOX_HEADS]
    q_mem = proj[..., 4 * MIX_WIDTH + FOX_HEADS:]
    q = rms_norm(q, q_norm_w).astype(jnp.float32) * (HEAD_DIM ** -0.5)
    k = rms_norm(k, k_norm_w).astype(jnp.float32)
    log_f = jax.nn.log_sigmoid(f_logit.astype(jnp.float32) + f_bias.astype(jnp.float32))
    f_cum = jnp.cumsum(log_f, axis=1).transpose(0, 2, 1)
    nb = seq // Q_BLOCK
    q_blocks = q.reshape(bsz, nb, Q_BLOCK, FOX_HEADS, HEAD_DIM).transpose(1, 0, 2, 3, 4)
    f_blocks = f_cum.reshape(bsz, FOX_HEADS, nb, Q_BLOCK).transpose(2, 0, 1, 3)
    k_pos = jnp.arange(seq)

    def block(args):
        qb, fb, i = args
        s = jnp.einsum("bqhd,bkhd->bhqk", qb, k)
        bias = fb[..., :, None] - f_cum[:, :, None, :]
        q_pos = i * Q_BLOCK + jnp.arange(Q_BLOCK)
        mask = k_pos[None, :] <= q_pos[:, None]
        p = jax.nn.softmax(jnp.where(mask, s + bias, -jnp.inf), axis=-1)
        return jnp.einsum("bhqk,bkhd->bqhd", p.astype(v.dtype), v)

    o = lax.map(block, (q_blocks, f_blocks, jnp.arange(nb)))
    o = o.transpose(1, 0, 2, 3, 4).reshape(bsz, seq, MIX_WIDTH)
    o = o.astype(jnp.float32) * jax.nn.sigmoid(gate.astype(jnp.float32))
    return o.astype(h.dtype), q_mem


def memory_attention(q_mem, mem_k, mem_v, q_norm_w):
    bsz, seq, _ = q_mem.shape
    q = rms_norm(split_heads(q_mem, MEM_HEADS), q_norm_w).astype(jnp.float32) * (HEAD_DIM ** -0.5)
    p = jax.nn.softmax(jnp.einsum("bthd,bmhd->bhtm", q, mem_k), axis=-1)
    o = jnp.einsum("bhtm,bmhd->bthd", p.astype(mem_v.dtype), mem_v)
    return o.reshape(bsz, seq, MEM_WIDTH).astype(q_mem.dtype)


def _fwd_setup_inputs(seed: int = 0) -> dict:
    key = jax.random.key(seed)
    ks = jax.random.split(key, 20)
    f32 = jnp.float32

    def dense(k, shape, fan_in):
        return jax.random.normal(k, shape, f32) * fan_in ** -0.5

    def gain(k, shape):
        return 1.0 + 0.02 * jax.random.normal(k, shape, f32)

    dt = jnp.exp(jax.random.uniform(ks[9], (N_DN, DN_HEADS), f32, math.log(1e-3), math.log(1e-1)))
    return {
        "x": jax.random.normal(ks[0], (BATCH, SEQ, D_MODEL), f32),
        "mem": jax.random.normal(ks[1], (BATCH, N_MEM, D_MODEL), f32),
        "mem_norm_w": gain(ks[2], (D_MODEL,)),
        "w_mem_kv": dense(ks[3], (D_MODEL, 2 * MEM_WIDTH), D_MODEL),
        "mem_k_norm_w": gain(ks[4], (HEAD_DIM,)),
        "norm1_w": gain(ks[5], (DEPTH, D_MODEL)),
        "dn_w_in": dense(ks[6], (N_DN, D_MODEL, DN_IN), D_MODEL),
        "dn_conv_w": dense(ks[7], (N_DN, CONV_WIDTH, 3 * MIX_WIDTH), CONV_WIDTH),
        "dn_a_log": jnp.log(jax.random.uniform(ks[8], (N_DN, DN_HEADS), f32, 1.0, 16.0)),
        "dn_dt_bias": dt + jnp.log(-jnp.expm1(-dt)),
        "dn_o_norm_w": gain(ks[10], (N_DN, HEAD_DIM)),
        "fox_w_in": dense(ks[11], (N_FOX, D_MODEL, FOX_IN), D_MODEL),
        "fox_f_bias": jax.random.uniform(ks[12], (N_FOX, FOX_HEADS), f32, 1.0, 4.0),
        "fox_q_norm_w": gain(ks[13], (N_FOX, HEAD_DIM)),
        "fox_k_norm_w": gain(ks[14], (N_FOX, HEAD_DIM)),
        "memq_norm_w": gain(ks[15], (DEPTH, HEAD_DIM)),
        "w_out": dense(ks[16], (DEPTH, OUT_IN, D_MODEL), OUT_IN),
        "norm2_w": gain(ks[17], (DEPTH, D_MODEL)),
        "w_mlp1": dense(ks[18], (DEPTH, D_MODEL, D_FF), D_MODEL),
        "w_mlp2": dense(ks[19], (DEPTH, D_FF, D_MODEL), D_FF),
    }


def _fwd_reference(x, mem, mem_norm_w, w_mem_kv, mem_k_norm_w, norm1_w, dn_w_in, dn_conv_w,
              dn_a_log, dn_dt_bias, dn_o_norm_w, fox_w_in, fox_f_bias, fox_q_norm_w,
              fox_k_norm_w, memq_norm_w, w_out, norm2_w, w_mlp1, w_mlp2):
    mem_kv = rms_norm(mem, mem_norm_w) @ w_mem_kv
    mem_k = rms_norm(split_heads(mem_kv[..., :MEM_WIDTH], MEM_HEADS), mem_k_norm_w).astype(jnp.float32)
    mem_v = split_heads(mem_kv[..., MEM_WIDTH:], MEM_HEADS)
    for i in range(DEPTH):
        h = rms_norm(x, norm1_w[i])
        j = i // 2
        if i % 2 == 0:
            mix, q_mem = gated_deltanet(h, dn_w_in[j], dn_conv_w[j], dn_a_log[j],
                                        dn_dt_bias[j], dn_o_norm_w[j])
        else:
            mix, q_mem = forgetting_attention(h, fox_w_in[j], fox_f_bias[j],
                                              fox_q_norm_w[j], fox_k_norm_w[j])
        mem_out = memory_attention(q_mem, mem_k, mem_v, memq_norm_w[i])
        x = x + jnp.concatenate([mix, mem_out], axis=-1) @ w_out[i]
        h = rms_norm(x, norm2_w[i])
        x = x + jnp.square(jax.nn.relu(h @ w_mlp1[i])) @ w_mlp2[i]
    return x


import jax as _jax
import jax.numpy as _jnp

TWIN_FORMAT = 'train_step'
FWD_PARAMS = ['x', 'mem', 'mem_norm_w', 'w_mem_kv', 'mem_k_norm_w', 'norm1_w', 'dn_w_in', 'dn_conv_w', 'dn_a_log', 'dn_dt_bias', 'dn_o_norm_w', 'fox_w_in', 'fox_f_bias', 'fox_q_norm_w', 'fox_k_norm_w', 'memq_norm_w', 'w_out', 'norm2_w', 'w_mlp1', 'w_mlp2']
TWIN_WEIGHTS = ['mem_norm_w', 'w_mem_kv', 'mem_k_norm_w', 'norm1_w', 'dn_w_in', 'dn_conv_w', 'dn_a_log', 'dn_dt_bias', 'dn_o_norm_w', 'fox_w_in', 'fox_f_bias', 'fox_q_norm_w', 'fox_k_norm_w', 'memq_norm_w', 'w_out', 'norm2_w', 'w_mlp1', 'w_mlp2']
TWIN_DIFF_INPUT = 'x'
TWIN_INPUTS = ['x', 'mem', 'mem_norm_w', 'w_mem_kv', 'mem_k_norm_w', 'norm1_w', 'dn_w_in', 'dn_conv_w', 'dn_a_log', 'dn_dt_bias', 'dn_o_norm_w', 'fox_w_in', 'fox_f_bias', 'fox_q_norm_w', 'fox_k_norm_w', 'memq_norm_w', 'w_out', 'norm2_w', 'w_mlp1', 'w_mlp2', 'loss_target', 'm_mem_norm_w', 'm_w_mem_kv', 'm_mem_k_norm_w', 'm_norm1_w', 'm_dn_w_in', 'm_dn_conv_w', 'm_dn_a_log', 'm_dn_dt_bias', 'm_dn_o_norm_w', 'm_fox_w_in', 'm_fox_f_bias', 'm_fox_q_norm_w', 'm_fox_k_norm_w', 'm_memq_norm_w', 'm_w_out', 'm_norm2_w', 'm_w_mlp1', 'm_w_mlp2', 'v_mem_norm_w', 'v_w_mem_kv', 'v_mem_k_norm_w', 'v_norm1_w', 'v_dn_w_in', 'v_dn_conv_w', 'v_dn_a_log', 'v_dn_dt_bias', 'v_dn_o_norm_w', 'v_fox_w_in', 'v_fox_f_bias', 'v_fox_q_norm_w', 'v_fox_k_norm_w', 'v_memq_norm_w', 'v_w_out', 'v_norm2_w', 'v_w_mlp1', 'v_w_mlp2']
TWIN_OUTPUTS = ['loss', 'grad_x', 'grad_mem_norm_w', 'grad_w_mem_kv', 'grad_mem_k_norm_w', 'grad_norm1_w', 'grad_dn_w_in', 'grad_dn_conv_w', 'grad_dn_a_log', 'grad_dn_dt_bias', 'grad_dn_o_norm_w', 'grad_fox_w_in', 'grad_fox_f_bias', 'grad_fox_q_norm_w', 'grad_fox_k_norm_w', 'grad_memq_norm_w', 'grad_w_out', 'grad_norm2_w', 'grad_w_mlp1', 'grad_w_mlp2', 'delta_mem_norm_w', 'delta_w_mem_kv', 'delta_mem_k_norm_w', 'delta_norm1_w', 'delta_dn_w_in', 'delta_dn_conv_w', 'delta_dn_a_log', 'delta_dn_dt_bias', 'delta_dn_o_norm_w', 'delta_fox_w_in', 'delta_fox_f_bias', 'delta_fox_q_norm_w', 'delta_fox_k_norm_w', 'delta_memq_norm_w', 'delta_w_out', 'delta_norm2_w', 'delta_w_mlp1', 'delta_w_mlp2', 'new_m_mem_norm_w', 'new_m_w_mem_kv', 'new_m_mem_k_norm_w', 'new_m_norm1_w', 'new_m_dn_w_in', 'new_m_dn_conv_w', 'new_m_dn_a_log', 'new_m_dn_dt_bias', 'new_m_dn_o_norm_w', 'new_m_fox_w_in', 'new_m_fox_f_bias', 'new_m_fox_q_norm_w', 'new_m_fox_k_norm_w', 'new_m_memq_norm_w', 'new_m_w_out', 'new_m_norm2_w', 'new_m_w_mlp1', 'new_m_w_mlp2', 'new_v_mem_norm_w', 'new_v_w_mem_kv', 'new_v_mem_k_norm_w', 'new_v_norm1_w', 'new_v_dn_w_in', 'new_v_dn_conv_w', 'new_v_dn_a_log', 'new_v_dn_dt_bias', 'new_v_dn_o_norm_w', 'new_v_fox_w_in', 'new_v_fox_f_bias', 'new_v_fox_q_norm_w', 'new_v_fox_k_norm_w', 'new_v_memq_norm_w', 'new_v_w_out', 'new_v_norm2_w', 'new_v_w_mlp1', 'new_v_w_mlp2']
TWIN_LEAF_KINDS = {'loss': 'loss', 'grad_x': 'grad_x', 'grad_mem_norm_w': 'grad_w', 'grad_w_mem_kv': 'grad_w', 'grad_mem_k_norm_w': 'grad_w', 'grad_norm1_w': 'grad_w', 'grad_dn_w_in': 'grad_w', 'grad_dn_conv_w': 'grad_w', 'grad_dn_a_log': 'grad_w', 'grad_dn_dt_bias': 'grad_w', 'grad_dn_o_norm_w': 'grad_w', 'grad_fox_w_in': 'grad_w', 'grad_fox_f_bias': 'grad_w', 'grad_fox_q_norm_w': 'grad_w', 'grad_fox_k_norm_w': 'grad_w', 'grad_memq_norm_w': 'grad_w', 'grad_w_out': 'grad_w', 'grad_norm2_w': 'grad_w', 'grad_w_mlp1': 'grad_w', 'grad_w_mlp2': 'grad_w', 'delta_mem_norm_w': 'delta_w', 'delta_w_mem_kv': 'delta_w', 'delta_mem_k_norm_w': 'delta_w', 'delta_norm1_w': 'delta_w', 'delta_dn_w_in': 'delta_w', 'delta_dn_conv_w': 'delta_w', 'delta_dn_a_log': 'delta_w', 'delta_dn_dt_bias': 'delta_w', 'delta_dn_o_norm_w': 'delta_w', 'delta_fox_w_in': 'delta_w', 'delta_fox_f_bias': 'delta_w', 'delta_fox_q_norm_w': 'delta_w', 'delta_fox_k_norm_w': 'delta_w', 'delta_memq_norm_w': 'delta_w', 'delta_w_out': 'delta_w', 'delta_norm2_w': 'delta_w', 'delta_w_mlp1': 'delta_w', 'delta_w_mlp2': 'delta_w', 'new_m_mem_norm_w': 'new_m', 'new_m_w_mem_kv': 'new_m', 'new_m_mem_k_norm_w': 'new_m', 'new_m_norm1_w': 'new_m', 'new_m_dn_w_in': 'new_m', 'new_m_dn_conv_w': 'new_m', 'new_m_dn_a_log': 'new_m', 'new_m_dn_dt_bias': 'new_m', 'new_m_dn_o_norm_w': 'new_m', 'new_m_fox_w_in': 'new_m', 'new_m_fox_f_bias': 'new_m', 'new_m_fox_q_norm_w': 'new_m', 'new_m_fox_k_norm_w': 'new_m', 'new_m_memq_norm_w': 'new_m', 'new_m_w_out': 'new_m', 'new_m_norm2_w': 'new_m', 'new_m_w_mlp1': 'new_m', 'new_m_w_mlp2': 'new_m', 'new_v_mem_norm_w': 'new_v', 'new_v_w_mem_kv': 'new_v', 'new_v_mem_k_norm_w': 'new_v', 'new_v_norm1_w': 'new_v', 'new_v_dn_w_in': 'new_v', 'new_v_dn_conv_w': 'new_v', 'new_v_dn_a_log': 'new_v', 'new_v_dn_dt_bias': 'new_v', 'new_v_dn_o_norm_w': 'new_v', 'new_v_fox_w_in': 'new_v', 'new_v_fox_f_bias': 'new_v', 'new_v_fox_q_norm_w': 'new_v', 'new_v_fox_k_norm_w': 'new_v', 'new_v_memq_norm_w': 'new_v', 'new_v_w_out': 'new_v', 'new_v_norm2_w': 'new_v', 'new_v_w_mlp1': 'new_v', 'new_v_w_mlp2': 'new_v'}


def _forward(args):
    return _fwd_reference(*[args[k] for k in FWD_PARAMS])


def _output_shape():
    out = _jax.eval_shape(lambda: _forward(_fwd_setup_inputs(0)))
    return out.shape, out.dtype

N_MICROBATCH = 1
ADAM_LR = 0.001
ADAM_B1 = 0.9
ADAM_B2 = 0.999
ADAM_EPS = 1e-08
ADAM_WD = 0.01
ADAM_STEP = 10
PER_EXAMPLE_BATCH_AXIS = {'x': 0, 'mem': 0, 'loss_target': 0}
SHARED_INPUTS = []
_WEIGHT_DTYPES = {'mem_norm_w': _jnp.float32, 'w_mem_kv': _jnp.float32, 'mem_k_norm_w': _jnp.float32, 'norm1_w': _jnp.float32, 'dn_w_in': _jnp.float32, 'dn_conv_w': _jnp.float32, 'dn_a_log': _jnp.float32, 'dn_dt_bias': _jnp.float32, 'dn_o_norm_w': _jnp.float32, 'fox_w_in': _jnp.float32, 'fox_f_bias': _jnp.float32, 'fox_q_norm_w': _jnp.float32, 'fox_k_norm_w': _jnp.float32, 'memq_norm_w': _jnp.float32, 'w_out': _jnp.float32, 'norm2_w': _jnp.float32, 'w_mlp1': _jnp.float32, 'w_mlp2': _jnp.float32}
MOMENT_SCALE = {'mem_norm_w': 5.337895e-01, 'w_mem_kv': 5.263799e-01, 'mem_k_norm_w': 9.521328e-01, 'norm1_w': 3.274016e+00, 'dn_w_in': 3.410761e-01, 'dn_conv_w': 9.370427e-01, 'dn_a_log': 9.223378e+00, 'dn_dt_bias': 8.308548e+00, 'dn_o_norm_w': 3.006983e+01, 'fox_w_in': 1.024006e+00, 'fox_f_bias': 1.603028e+01, 'fox_q_norm_w': 1.135419e+00, 'fox_k_norm_w': 1.137243e+00, 'memq_norm_w': 5.529302e-01, 'w_out': 2.235956e+00, 'norm2_w': 4.802871e+01, 'w_mlp1': 2.095646e+00, 'w_mlp2': 7.857531e+00}


def _to_microbatches(a, axis):
    t = _jnp.moveaxis(a, axis, 0)
    t = t.reshape((N_MICROBATCH, t.shape[0] // N_MICROBATCH) + t.shape[1:])
    return _jnp.moveaxis(t, 1, axis + 1)


def setup_inputs(seed: int = 0) -> dict:
    inp = _fwd_setup_inputs(seed)
    key = _jax.random.fold_in(_jax.random.key(seed), 7919)
    shape, _ = _output_shape()
    out = dict(inp)
    out["loss_target"] = _jax.random.normal(_jax.random.fold_in(key, 0), shape, _jnp.float32)
    for i, name in enumerate(TWIN_WEIGHTS):
        w = inp[name].astype(_jnp.float32)
        if MOMENT_SCALE is None:
            s = _jnp.sqrt(_jnp.mean(_jnp.square(w)) + 1e-30)
        else:
            s = MOMENT_SCALE[name]
        km, kv = _jax.random.split(_jax.random.fold_in(key, i + 1))
        out[name] = w
        out["m_" + name] = s * _jax.random.normal(km, w.shape, _jnp.float32)
        out["v_" + name] = (s * s) * _jax.random.uniform(kv, w.shape, _jnp.float32, 0.5, 1.5)
    if N_MICROBATCH > 1:
        for name, axis in PER_EXAMPLE_BATCH_AXIS.items():
            out[name] = _to_microbatches(out[name], axis)
    return {'x': out['x'], 'mem': out['mem'], 'mem_norm_w': out['mem_norm_w'], 'w_mem_kv': out['w_mem_kv'], 'mem_k_norm_w': out['mem_k_norm_w'], 'norm1_w': out['norm1_w'], 'dn_w_in': out['dn_w_in'], 'dn_conv_w': out['dn_conv_w'], 'dn_a_log': out['dn_a_log'], 'dn_dt_bias': out['dn_dt_bias'], 'dn_o_norm_w': out['dn_o_norm_w'], 'fox_w_in': out['fox_w_in'], 'fox_f_bias': out['fox_f_bias'], 'fox_q_norm_w': out['fox_q_norm_w'], 'fox_k_norm_w': out['fox_k_norm_w'], 'memq_norm_w': out['memq_norm_w'], 'w_out': out['w_out'], 'norm2_w': out['norm2_w'], 'w_mlp1': out['w_mlp1'], 'w_mlp2': out['w_mlp2'], 'loss_target': out['loss_target'], 'm_mem_norm_w': out['m_mem_norm_w'], 'm_w_mem_kv': out['m_w_mem_kv'], 'm_mem_k_norm_w': out['m_mem_k_norm_w'], 'm_norm1_w': out['m_norm1_w'], 'm_dn_w_in': out['m_dn_w_in'], 'm_dn_conv_w': out['m_dn_conv_w'], 'm_dn_a_log': out['m_dn_a_log'], 'm_dn_dt_bias': out['m_dn_dt_bias'], 'm_dn_o_norm_w': out['m_dn_o_norm_w'], 'm_fox_w_in': out['m_fox_w_in'], 'm_fox_f_bias': out['m_fox_f_bias'], 'm_fox_q_norm_w': out['m_fox_q_norm_w'], 'm_fox_k_norm_w': out['m_fox_k_norm_w'], 'm_memq_norm_w': out['m_memq_norm_w'], 'm_w_out': out['m_w_out'], 'm_norm2_w': out['m_norm2_w'], 'm_w_mlp1': out['m_w_mlp1'], 'm_w_mlp2': out['m_w_mlp2'], 'v_mem_norm_w': out['v_mem_norm_w'], 'v_w_mem_kv': out['v_w_mem_kv'], 'v_mem_k_norm_w': out['v_mem_k_norm_w'], 'v_norm1_w': out['v_norm1_w'], 'v_dn_w_in': out['v_dn_w_in'], 'v_dn_conv_w': out['v_dn_conv_w'], 'v_dn_a_log': out['v_dn_a_log'], 'v_dn_dt_bias': out['v_dn_dt_bias'], 'v_dn_o_norm_w': out['v_dn_o_norm_w'], 'v_fox_w_in': out['v_fox_w_in'], 'v_fox_f_bias': out['v_fox_f_bias'], 'v_fox_q_norm_w': out['v_fox_q_norm_w'], 'v_fox_k_norm_w': out['v_fox_k_norm_w'], 'v_memq_norm_w': out['v_memq_norm_w'], 'v_w_out': out['v_w_out'], 'v_norm2_w': out['v_norm2_w'], 'v_w_mlp1': out['v_w_mlp1'], 'v_w_mlp2': out['v_w_mlp2']}


def _loss(weights, diff, rest, loss_target):
    with _jax.named_scope("forward"):
        args = {**rest, TWIN_DIFF_INPUT: diff, **{k: w.astype(_WEIGHT_DTYPES[k]) for k, w in weights.items()}}
        y = _forward(args)
    with _jax.named_scope("loss_head"):
        err = _jnp.square(y.astype(_jnp.float32) - loss_target)
        return 0.5 * _jnp.sum(_jnp.mean(err, axis=-1)) if err.ndim else 0.5 * err


def _adamw(w, g, m, v):
    m = ADAM_B1 * m + (1.0 - ADAM_B1) * g
    v = ADAM_B2 * v + (1.0 - ADAM_B2) * _jnp.square(g)
    m_hat = m / (1.0 - ADAM_B1 ** ADAM_STEP)
    v_hat = v / (1.0 - ADAM_B2 ** ADAM_STEP)
    delta = -ADAM_LR * (m_hat / (_jnp.sqrt(v_hat) + ADAM_EPS) + ADAM_WD * w)
    return delta, m, v


def reference(x, mem, mem_norm_w, w_mem_kv, mem_k_norm_w, norm1_w, dn_w_in, dn_conv_w, dn_a_log, dn_dt_bias, dn_o_norm_w, fox_w_in, fox_f_bias, fox_q_norm_w, fox_k_norm_w, memq_norm_w, w_out, norm2_w, w_mlp1, w_mlp2, loss_target, m_mem_norm_w, m_w_mem_kv, m_mem_k_norm_w, m_norm1_w, m_dn_w_in, m_dn_conv_w, m_dn_a_log, m_dn_dt_bias, m_dn_o_norm_w, m_fox_w_in, m_fox_f_bias, m_fox_q_norm_w, m_fox_k_norm_w, m_memq_norm_w, m_w_out, m_norm2_w, m_w_mlp1, m_w_mlp2, v_mem_norm_w, v_w_mem_kv, v_mem_k_norm_w, v_norm1_w, v_dn_w_in, v_dn_conv_w, v_dn_a_log, v_dn_dt_bias, v_dn_o_norm_w, v_fox_w_in, v_fox_f_bias, v_fox_q_norm_w, v_fox_k_norm_w, v_memq_norm_w, v_w_out, v_norm2_w, v_w_mlp1, v_w_mlp2):
    given = dict(x=x, mem=mem, mem_norm_w=mem_norm_w, w_mem_kv=w_mem_kv, mem_k_norm_w=mem_k_norm_w, norm1_w=norm1_w, dn_w_in=dn_w_in, dn_conv_w=dn_conv_w, dn_a_log=dn_a_log, dn_dt_bias=dn_dt_bias, dn_o_norm_w=dn_o_norm_w, fox_w_in=fox_w_in, fox_f_bias=fox_f_bias, fox_q_norm_w=fox_q_norm_w, fox_k_norm_w=fox_k_norm_w, memq_norm_w=memq_norm_w, w_out=w_out, norm2_w=norm2_w, w_mlp1=w_mlp1, w_mlp2=w_mlp2, loss_target=loss_target, m_mem_norm_w=m_mem_norm_w, m_w_mem_kv=m_w_mem_kv, m_mem_k_norm_w=m_mem_k_norm_w, m_norm1_w=m_norm1_w, m_dn_w_in=m_dn_w_in, m_dn_conv_w=m_dn_conv_w, m_dn_a_log=m_dn_a_log, m_dn_dt_bias=m_dn_dt_bias, m_dn_o_norm_w=m_dn_o_norm_w, m_fox_w_in=m_fox_w_in, m_fox_f_bias=m_fox_f_bias, m_fox_q_norm_w=m_fox_q_norm_w, m_fox_k_norm_w=m_fox_k_norm_w, m_memq_norm_w=m_memq_norm_w, m_w_out=m_w_out, m_norm2_w=m_norm2_w, m_w_mlp1=m_w_mlp1, m_w_mlp2=m_w_mlp2, v_mem_norm_w=v_mem_norm_w, v_w_mem_kv=v_w_mem_kv, v_mem_k_norm_w=v_mem_k_norm_w, v_norm1_w=v_norm1_w, v_dn_w_in=v_dn_w_in, v_dn_conv_w=v_dn_conv_w, v_dn_a_log=v_dn_a_log, v_dn_dt_bias=v_dn_dt_bias, v_dn_o_norm_w=v_dn_o_norm_w, v_fox_w_in=v_fox_w_in, v_fox_f_bias=v_fox_f_bias, v_fox_q_norm_w=v_fox_q_norm_w, v_fox_k_norm_w=v_fox_k_norm_w, v_memq_norm_w=v_memq_norm_w, v_w_out=v_w_out, v_norm2_w=v_norm2_w, v_w_mlp1=v_w_mlp1, v_w_mlp2=v_w_mlp2)
    weights = {n: given[n] for n in TWIN_WEIGHTS}
    shared = {n: given[n] for n in SHARED_INPUTS}
    per_example = {n: given[n] for n in ['x', 'mem']}
    grad_fn = _jax.value_and_grad(_loss, argnums=(0, 1))

    def one_microbatch(ex, loss_target):
        ex = dict(ex)
        diff = ex.pop(TWIN_DIFF_INPUT)
        return grad_fn(weights, diff, {**shared, **ex}, loss_target)

    if N_MICROBATCH == 1:
        loss, (grad_w, grad_x) = one_microbatch(per_example, given["loss_target"])
    else:
        def body(carry, xs):
            loss_sum, grad_sum = carry
            l_k, (gw_k, gx_k) = one_microbatch(xs[0], xs[1])
            with _jax.named_scope("update"):
                return (loss_sum + l_k, _jax.tree.map(_jnp.add, grad_sum, gw_k)), gx_k

        init = (_jnp.zeros((), _jnp.float32), _jax.tree.map(_jnp.zeros_like, weights))
        (loss, grad_w), grad_x = _jax.lax.scan(body, init, (per_example, given["loss_target"]))
    with _jax.named_scope("update"):
        delta_w, new_m, new_v = {}, {}, {}
        for n in TWIN_WEIGHTS:
            delta_w[n], new_m[n], new_v[n] = _adamw(weights[n], grad_w[n], given["m_" + n], given["v_" + n])
    return (loss, grad_x, *[grad_w[n] for n in TWIN_WEIGHTS], *[delta_w[n] for n in TWIN_WEIGHTS],
            *[new_m[n] for n in TWIN_WEIGHTS], *[new_v[n] for n in TWIN_WEIGHTS])
```

```python
import functools
import math

import jax
import jax.numpy as jnp
from jax import lax
from jax.experimental import pallas as pl
from jax.experimental.pallas import tpu as pltpu

F32 = jnp.float32
BF16 = jnp.bfloat16
HIGHEST = lax.Precision.HIGHEST

D_MODEL = 1024
HEAD_DIM = 128
N_HEADS = 8
MEM_HEADS = 4
MEM_WIDTH = MEM_HEADS * HEAD_DIM
D_FF = 4 * D_MODEL
CONV_WIDTH = 4
CHUNK = 64
Q_BLOCK = 128
EPS = 1e-6
SCALE = HEAD_DIM ** -0.5
MAIN_WIDTH = 4 * D_MODEL + MEM_WIDTH
LANES = 128
N_DEV = 8
PACK_W = 512

ADAM_LR = 0.001
ADAM_B1 = 0.9
ADAM_B2 = 0.999
ADAM_EPS = 1e-08
ADAM_WD = 0.01
ADAM_STEP = 10

VMEM_LIMIT = 56 * 2 ** 20
MESH = pl.DeviceIdType.MESH


def _bdot(a, b, dims):
    return lax.dot_general(a.astype(BF16), b.astype(BF16), (dims, ((), ())), preferred_element_type=F32)


@jax.custom_vjp
def mm(a, b):
    return _bdot(a, b, ((1,), (0,)))


@jax.custom_vjp
def mm_nt(a, b):
    return _bdot(a, b, ((1,), (1,)))


@jax.custom_vjp
def mm_tn(a, b):
    return _bdot(a, b, ((0,), (0,)))


mm.defvjp(lambda a, b: (mm(a, b), (a, b)), lambda r, g: (mm_nt(g, r[1]), mm_tn(r[0], g)))
mm_nt.defvjp(lambda a, b: (mm_nt(a, b), (a, b)), lambda r, g: (mm(g, r[1]), mm_tn(g, r[0])))
mm_tn.defvjp(lambda a, b: (mm_tn(a, b), (a, b)), lambda r, g: (mm_nt(r[1], g), mm(r[0], g)))


def hdot(a, b):
    return jnp.dot(a, b, precision=HIGHEST, preferred_element_type=F32)


def rms(x, w):
    return x * lax.rsqrt(jnp.mean(x * x, axis=-1, keepdims=True) + EPS) * w


def l2n(x):
    return x * lax.rsqrt(jnp.sum(x * x, axis=-1, keepdims=True) + EPS)


def _iota2(n, m):
    return lax.broadcasted_iota(jnp.int32, (n, m), 0), lax.broadcasted_iota(jnp.int32, (n, m), 1)


def _lower_ones(n):
    r, c = _iota2(n, n)
    return jnp.where(r >= c, 1.0, 0.0).astype(F32)


def _last_row(x):
    r = lax.broadcasted_iota(jnp.int32, x.shape, 0)
    return jnp.sum(jnp.where(r == x.shape[0] - 1, x, 0.0), axis=0, keepdims=True)


def _softmax_rows(z):
    m = lax.stop_gradient(jnp.max(z, axis=-1, keepdims=True))
    e = jnp.exp(z - m)
    return e / jnp.sum(e, axis=-1, keepdims=True)


def inv_unit_lower(a):
    n = a.shape[0]
    r, c = _iota2(n, n)
    p = jnp.where(r == c, 1.0, 0.0).astype(F32) - a
    ak = a
    for _ in range(int(math.log2(n)) - 1):
        ak = hdot(ak, ak)
        p = p + hdot(p, ak)
    return p


def delta_chunk(q, k, v, gc, beta, s):
    c = q.shape[0]
    r, cc = _iota2(c, c)
    causal = r >= cc
    strict = r > cc
    gi = jnp.broadcast_to(gc, (c, c))
    gj = gi.T
    decay = jnp.where(causal, jnp.exp(jnp.where(causal, gi - gj, 0.0)), 0.0)
    kb = k * beta
    a = jnp.where(strict, mm_nt(kb, k) * decay, 0.0)
    t = inv_unit_lower(a)
    u = mm(t, v * beta)
    w = mm(t, kb * jnp.exp(gc))
    qk = jnp.where(causal, mm_nt(q, k) * decay, 0.0)
    v_new = u - mm(w, s)
    out = mm(q * jnp.exp(gc), s) + mm(qk, v_new)
    g_last = _last_row(gc)
    k_dec = k * jnp.exp(g_last - gc)
    s_new = s * jnp.exp(g_last) + mm_tn(k_dec, v_new)
    return out, s_new


def fox_block(q, k, v, fq, fk, qpos0):
    s = mm_nt(q, k)
    r, c = _iota2(s.shape[0], s.shape[1])
    mask = c <= (r + qpos0)
    p = _softmax_rows(jnp.where(mask, s + (fq - fk), -jnp.inf))
    return mm(p, v)


def mem_head(qm, wq, mk, mv):
    p = _softmax_rows(mm_nt(rms(qm, wq) * SCALE, mk))
    return mm(p, mv)


def _heads(x, n):
    return [x[:, h * HEAD_DIM:(h + 1) * HEAD_DIM] for h in range(n)]


def memkv_fn(mem, mnw, wkv, mknw):
    kv = mm(rms(mem, mnw), wkv)
    mk = jnp.concatenate([rms(kh, mknw) for kh in _heads(kv[:, :MEM_WIDTH], MEM_HEADS)], axis=1)
    return mk, kv[:, MEM_WIDTH:]


def dn_gates_fn(ab, alog, dtb):
    g = -jnp.exp(alog) * jax.nn.softplus(ab + dtb)
    low = _lower_ones(CHUNK)
    gc = jnp.concatenate([hdot(low, g[i * CHUNK:(i + 1) * CHUNK]) for i in range(ab.shape[0] // CHUNK)], axis=0)
    lane = lax.broadcasted_iota(jnp.int32, ab.shape, 1)
    return jnp.where(lane < N_HEADS, gc, jax.nn.sigmoid(ab))


def fox_fcum_fn(fp, fbias):
    lf = jax.nn.log_sigmoid(fp + fbias)
    low = _lower_ones(LANES)
    carry = jnp.zeros((1, fp.shape[1]), F32)
    outs = []
    for i in range(fp.shape[0] // LANES):
        cs = hdot(low, lf[i * LANES:(i + 1) * LANES]) + carry
        carry = _last_row(cs)
        outs.append(cs)
    return jnp.concatenate(outs, axis=0)


def fox_qk_fn(qraw, kraw, qnw, knw):
    q = jnp.concatenate([rms(x, qnw) * SCALE for x in _heads(qraw, N_HEADS)], axis=1)
    k = jnp.concatenate([rms(x, knw) for x in _heads(kraw, N_HEADS)], axis=1)
    return q, k


def _mem_out(qm, mqw, mk, mv):
    return [mem_head(a, mqw, b, c) for a, b, c in zip(_heads(qm, MEM_HEADS), _heads(mk, MEM_HEADS), _heads(mv, MEM_HEADS))]


def dn_out_fn(o, z, qm, onw, mqw, mk, mv):
    mix = [rms(a, onw) * jax.nn.silu(b) for a, b in zip(_heads(o, N_HEADS), _heads(z, N_HEADS))]
    return jnp.concatenate(mix + _mem_out(qm, mqw, mk, mv), axis=1)


def fox_out_fn(o, gate, qm, mqw, mk, mv):
    return jnp.concatenate([o * jax.nn.sigmoid(gate)] + _mem_out(qm, mqw, mk, mv), axis=1)


def _pick(n, cands):
    for c in cands:
        if n % c == 0:
            return c
    return n


def _params(sem):
    return pltpu.CompilerParams(dimension_semantics=sem, vmem_limit_bytes=VMEM_LIMIT)


def matmul(a, b, *, name, ta=False, tb=False, add=None, out_dtype=F32):
    (k, m) = a.shape if ta else a.shape[::-1]
    (kb, n) = b.shape[::-1] if tb else b.shape
    assert k == kb, (a.shape, b.shape, ta, tb)
    tm = _pick(m, (512, 256, 128))
    tn = _pick(n, (512, 256, 128))
    tk = _pick(k, (1024, 512, 256, 128))
    nk = k // tk
    dims = ((0,) if ta else (1,), (1,) if tb else (0,))

    def body(*refs):
        if add is None:
            a_ref, b_ref, o_ref, acc = refs
        else:
            a_ref, b_ref, add_ref, o_ref, acc = refs
        kk = pl.program_id(2)

        @pl.when(kk == 0)
        def _():
            acc[...] = jnp.zeros_like(acc)

        acc[...] += _bdot(a_ref[...], b_ref[...], dims)

        @pl.when(kk == nk - 1)
        def _():
            r = acc[...]
            if add is not None:
                r = r + add_ref[...]
            o_ref[...] = r.astype(out_dtype)

    a_spec = pl.BlockSpec((tk, tm), lambda i, j, kk: (kk, i)) if ta else pl.BlockSpec((tm, tk), lambda i, j, kk: (i, kk))
    b_spec = pl.BlockSpec((tn, tk), lambda i, j, kk: (j, kk)) if tb else pl.BlockSpec((tk, tn), lambda i, j, kk: (kk, j))
    o_spec = pl.BlockSpec((tm, tn), lambda i, j, kk: (i, j))
    ins, specs = [a, b], [a_spec, b_spec]
    if add is not None:
        ins.append(add)
        specs.append(o_spec)
    return pl.pallas_call(
        body, name=name, grid=(m // tm, n // tn, nk), in_specs=specs, out_specs=o_spec,
        out_shape=jax.ShapeDtypeStruct((m, n), out_dtype),
        scratch_shapes=[pltpu.VMEM((tm, tn), F32)],
        compiler_params=_params(("parallel", "parallel", "arbitrary")),
    )(*ins)


def rows_call(fn, row_ins, full_ins, row_outs, acc_outs, *, tm, name):
    row_ins = [r if isinstance(r, tuple) else (r, r.shape[1], 0) for r in row_ins]
    t = row_ins[0][0].shape[0]
    tm = min(tm, t)
    n_in = len(row_ins) + len(full_ins)
    n_row = len(row_outs)

    def body(*refs):
        res = fn(*[r[...] for r in refs[:n_in]])
        res = res if isinstance(res, (tuple, list)) else (res,)
        outs = refs[n_in:]
        for ref, val in zip(outs[:n_row], res[:n_row]):
            ref[...] = val.astype(ref.dtype)
        first = pl.program_id(0) == 0
        for ref, val in zip(outs[n_row:], res[n_row:]):
            @pl.when(first)
            def _(ref=ref, val=val):
                ref[...] = val

            @pl.when(jnp.logical_not(first))
            def _(ref=ref, val=val):
                ref[...] += val

    def full_spec(shape):
        return pl.BlockSpec(shape, lambda i, nd=len(shape): (0,) * nd)

    in_specs = [pl.BlockSpec((tm, w), lambda i, cb=cb: (i, cb)) for (_, w, cb) in row_ins]
    in_specs += [full_spec(f.shape) for f in full_ins]
    out_specs = [pl.BlockSpec((tm, c), lambda i: (i, 0)) for c, _ in row_outs] + [full_spec(s) for s in acc_outs]
    out_shape = [jax.ShapeDtypeStruct((t, c), dt) for c, dt in row_outs] + [jax.ShapeDtypeStruct(s, F32) for s in acc_outs]
    res = pl.pallas_call(
        body, name=name, grid=(t // tm,), in_specs=in_specs, out_specs=out_specs, out_shape=out_shape,
        compiler_params=_params(("arbitrary",)),
    )(*[r[0] for r in row_ins], *full_ins)
    return res


def vjp_rows(fn, n_diff_row, row_diff_full):
    def bwd(*args, n_row, n_ct):
        prim_rows = args[:n_row]
        cts = args[n_row:n_row + n_ct]
        fulls = args[n_row + n_ct:]
        _, vjp = jax.vjp(fn, *prim_rows, *fulls)
        g = vjp(cts[0] if n_ct == 1 else tuple(cts))
        out = list(g[:n_diff_row])
        out += [gf for gf, d in zip(g[n_row:], row_diff_full) if d]
        return tuple(out)
    return bwd


def _shift_down(x, s):
    if s == 0:
        return x
    t = lax.broadcasted_iota(jnp.int32, x.shape, 0)
    return jnp.where(t >= s, pltpu.roll(x, s, 0), 0.0)


def _shift_up(x, s):
    if s == 0:
        return x
    n = x.shape[0]
    t = lax.broadcasted_iota(jnp.int32, x.shape, 0)
    return jnp.where(t < n - s, pltpu.roll(x, n - s, 0), 0.0)


def _conv(x, w_ref):
    return sum(w_ref[pl.ds(j, 1), :] * _shift_down(x, CONV_WIDTH - 1 - j) for j in range(CONV_WIDTH))


_DN_POST = (lambda c: l2n(jax.nn.silu(c)) * SCALE, lambda c: l2n(jax.nn.silu(c)), jax.nn.silu)


def dn_prep_fwd(proj, conv_w, *, name):
    t = proj.shape[0]

    def body(xq, xk, xv, wq, wk, wv, oq, ok, ov):
        for x_ref, w_ref, o_ref, post in zip((xq, xk, xv), (wq, wk, wv), (oq, ok, ov), _DN_POST):
            o_ref[...] = post(_conv(x_ref[...], w_ref))

    x_specs = [pl.BlockSpec((t, HEAD_DIM), lambda h, g=g: (0, g * N_HEADS + h)) for g in range(3)]
    w_specs = [pl.BlockSpec((CONV_WIDTH, HEAD_DIM), lambda h, g=g: (0, g * N_HEADS + h)) for g in range(3)]
    o_spec = pl.BlockSpec((t, HEAD_DIM), lambda h: (0, h))
    return pl.pallas_call(
        body, name=name, grid=(N_HEADS,), in_specs=x_specs + w_specs, out_specs=[o_spec] * 3,
        out_shape=[jax.ShapeDtypeStruct((t, D_MODEL), F32)] * 3, compiler_params=_params(("parallel",)),
    )(proj, proj, proj, conv_w, conv_w, conv_w)


def dn_prep_bwd(proj, conv_w, dq, dk, dv, *, name):
    t = proj.shape[0]

    def body(xq, xk, xv, wq, wk, wv, gq, gk, gv, dxq, dxk, dxv, dwq, dwk, dwv):
        for x_ref, w_ref, g_ref, dx_ref, dw_ref, post in zip(
                (xq, xk, xv), (wq, wk, wv), (gq, gk, gv), (dxq, dxk, dxv), (dwq, dwk, dwv), _DN_POST):
            x = x_ref[...]
            _, vjp = jax.vjp(post, _conv(x, w_ref))
            dc, = vjp(g_ref[...])
            dx_ref[...] = sum(w_ref[pl.ds(j, 1), :] * _shift_up(dc, CONV_WIDTH - 1 - j) for j in range(CONV_WIDTH))
            for j in range(CONV_WIDTH):
                dw_ref[pl.ds(j, 1), :] = jnp.sum(dc * _shift_down(x, CONV_WIDTH - 1 - j), axis=0, keepdims=True)

    x_specs = [pl.BlockSpec((t, HEAD_DIM), lambda h, g=g: (0, g * N_HEADS + h)) for g in range(3)]
    w_specs = [pl.BlockSpec((CONV_WIDTH, HEAD_DIM), lambda h, g=g: (0, g * N_HEADS + h)) for g in range(3)]
    g_spec = pl.BlockSpec((t, HEAD_DIM), lambda h: (0, h))
    dw_spec = pl.BlockSpec((CONV_WIDTH, HEAD_DIM), lambda h: (0, h))
    return pl.pallas_call(
        body, name=name, grid=(N_HEADS,), in_specs=x_specs + w_specs + [g_spec] * 3, out_specs=[g_spec] * 3 + [dw_spec] * 3,
        out_shape=[jax.ShapeDtypeStruct((t, D_MODEL), F32)] * 3 + [jax.ShapeDtypeStruct((CONV_WIDTH, D_MODEL), F32)] * 3,
        compiler_params=_params(("parallel",)),
    )(proj, proj, proj, conv_w, conv_w, conv_w, dq, dk, dv)


def delta_fwd(q, k, v, gc, beta, *, name):
    t = q.shape[0]
    nc = t // CHUNK

    def body(q_ref, k_ref, v_ref, g_ref, b_ref, o_ref, s0_ref, s_ref):
        @pl.when(pl.program_id(1) == 0)
        def _():
            s_ref[...] = jnp.zeros_like(s_ref)

        s = s_ref[...]
        s0_ref[...] = s
        o, s_new = delta_chunk(q_ref[...], k_ref[...], v_ref[...], g_ref[...], b_ref[...], s)
        o_ref[...] = o
        s_ref[...] = s_new

    x_spec = pl.BlockSpec((CHUNK, HEAD_DIM), lambda h, c: (c, h))
    g_spec = pl.BlockSpec((None, CHUNK, 1), lambda h, c: (h, c, 0))
    s_spec = pl.BlockSpec((None, None, HEAD_DIM, HEAD_DIM), lambda h, c: (h, c, 0, 0))
    return pl.pallas_call(
        body, name=name, grid=(N_HEADS, nc), in_specs=[x_spec] * 3 + [g_spec] * 2, out_specs=[x_spec, s_spec],
        out_shape=[jax.ShapeDtypeStruct((t, D_MODEL), F32), jax.ShapeDtypeStruct((N_HEADS, nc, HEAD_DIM, HEAD_DIM), F32)],
        scratch_shapes=[pltpu.VMEM((HEAD_DIM, HEAD_DIM), F32)],
        compiler_params=_params(("parallel", "arbitrary")),
    )(q, k, v, gc, beta)


def delta_bwd(q, k, v, gc, beta, s0, do, *, name):
    t = q.shape[0]
    nc = t // CHUNK

    def body(q_ref, k_ref, v_ref, g_ref, b_ref, s0_ref, do_ref, dq_ref, dk_ref, dv_ref, dg_ref, db_ref, ds_ref):
        @pl.when(pl.program_id(1) == 0)
        def _():
            ds_ref[...] = jnp.zeros_like(ds_ref)

        _, vjp = jax.vjp(delta_chunk, q_ref[...], k_ref[...], v_ref[...], g_ref[...], b_ref[...], s0_ref[...])
        dq, dk, dv, dg, db, ds = vjp((do_ref[...], ds_ref[...]))
        dq_ref[...] = dq
        dk_ref[...] = dk
        dv_ref[...] = dv
        dg_ref[...] = dg
        db_ref[...] = db
        ds_ref[...] = ds

    x_spec = pl.BlockSpec((CHUNK, HEAD_DIM), lambda h, c: (nc - 1 - c, h))
    g_spec = pl.BlockSpec((None, CHUNK, 1), lambda h, c: (h, nc - 1 - c, 0))
    s_spec = pl.BlockSpec((None, None, HEAD_DIM, HEAD_DIM), lambda h, c: (h, nc - 1 - c, 0, 0))
    return pl.pallas_call(
        body, name=name, grid=(N_HEADS, nc), in_specs=[x_spec] * 3 + [g_spec] * 2 + [s_spec, x_spec],
        out_specs=[x_spec] * 3 + [g_spec] * 2,
        out_shape=[jax.ShapeDtypeStruct((t, D_MODEL), F32)] * 3 + [jax.ShapeDtypeStruct((N_HEADS, t, 1), F32)] * 2,
        scratch_shapes=[pltpu.VMEM((HEAD_DIM, HEAD_DIM), F32)],
        compiler_params=_params(("parallel", "arbitrary")),
    )(q, k, v, gc, beta, s0, do)


_V_BLOCK = 2 * N_HEADS


def fox_attn_fwd(q, k, proj, fq, fk, *, name):
    t = q.shape[0]

    def body(q_ref, k_ref, v_ref, fq_ref, fk_ref, o_ref):
        qpos0 = pl.program_id(1) * Q_BLOCK
        o_ref[...] = fox_block(q_ref[...], k_ref[...], v_ref[...], fq_ref[...], fk_ref[...], qpos0)

    q_spec = pl.BlockSpec((Q_BLOCK, HEAD_DIM), lambda h, i: (i, h))
    k_spec = pl.BlockSpec((t, HEAD_DIM), lambda h, i: (0, h))
    v_spec = pl.BlockSpec((t, HEAD_DIM), lambda h, i: (0, _V_BLOCK + h))
    fq_spec = pl.BlockSpec((None, Q_BLOCK, 1), lambda h, i: (h, i, 0))
    fk_spec = pl.BlockSpec((None, 1, t), lambda h, i: (h, 0, 0))
    return pl.pallas_call(
        body, name=name, grid=(N_HEADS, t // Q_BLOCK), in_specs=[q_spec, k_spec, v_spec, fq_spec, fk_spec], out_specs=q_spec,
        out_shape=jax.ShapeDtypeStruct((t, D_MODEL), F32), compiler_params=_params(("parallel", "arbitrary")),
    )(q, k, proj, fq, fk)


def fox_attn_bwd(q, k, proj, fq, fk, do, *, name):
    t = q.shape[0]

    def body(q_ref, k_ref, v_ref, fq_ref, fk_ref, do_ref, dq_ref, dk_ref, dv_ref, dfq_ref, dfk_ref):
        i = pl.program_id(1)
        f = functools.partial(fox_block, qpos0=i * Q_BLOCK)
        _, vjp = jax.vjp(f, q_ref[...], k_ref[...], v_ref[...], fq_ref[...], fk_ref[...])
        dq, dk, dv, dfq, dfk = vjp(do_ref[...])
        dq_ref[...] = dq
        dfq_ref[...] = dfq

        @pl.when(i == 0)
        def _():
            dk_ref[...] = dk
            dv_ref[...] = dv
            dfk_ref[...] = dfk

        @pl.when(i > 0)
        def _():
            dk_ref[...] += dk
            dv_ref[...] += dv
            dfk_ref[...] += dfk

    q_spec = pl.BlockSpec((Q_BLOCK, HEAD_DIM), lambda h, i: (i, h))
    k_spec = pl.BlockSpec((t, HEAD_DIM), lambda h, i: (0, h))
    v_spec = pl.BlockSpec((t, HEAD_DIM), lambda h, i: (0, _V_BLOCK + h))
    fq_spec = pl.BlockSpec((None, Q_BLOCK, 1), lambda h, i: (h, i, 0))
    fk_spec = pl.BlockSpec((None, 1, t), lambda h, i: (h, 0, 0))
    return pl.pallas_call(
        body, name=name, grid=(N_HEADS, t // Q_BLOCK), in_specs=[q_spec, k_spec, v_spec, fq_spec, fk_spec, q_spec],
        out_specs=[q_spec, k_spec, k_spec, fq_spec, fk_spec],
        out_shape=[jax.ShapeDtypeStruct((t, D_MODEL), F32)] * 3
        + [jax.ShapeDtypeStruct((N_HEADS, t, 1), F32), jax.ShapeDtypeStruct((N_HEADS, 1, t), F32)],
        compiler_params=_params(("parallel", "arbitrary")),
    )(q, k, proj, fq, fk, do)


def memkv_fwd(mem, mnw, wkv, mknw, *, name):
    n = mem.shape[0]

    def body(mem_ref, mnw_ref, w_ref, mknw_ref, mk_ref, mv_ref):
        mk, mv = memkv_fn(mem_ref[...], mnw_ref[...], w_ref[...], mknw_ref[...])
        mk_ref[...] = mk
        mv_ref[...] = mv

    return pl.pallas_call(
        body, name=name, out_shape=[jax.ShapeDtypeStruct((n, MEM_WIDTH), F32)] * 2,
        compiler_params=pltpu.CompilerParams(vmem_limit_bytes=VMEM_LIMIT),
    )(mem, mnw, wkv, mknw)


def memkv_bwd(mem, mnw, wkv, mknw, dmk, dmv, *, name):
    def body(mem_ref, mnw_ref, w_ref, mknw_ref, dmk_ref, dmv_ref, dmnw_ref, dw_ref, dmknw_ref):
        f = functools.partial(memkv_fn, mem_ref[...])
        _, vjp = jax.vjp(f, mnw_ref[...], w_ref[...].astype(F32), mknw_ref[...])
        dmnw, dw, dmknw = vjp((dmk_ref[...], dmv_ref[...]))
        dmnw_ref[...] = dmnw
        dw_ref[...] = dw
        dmknw_ref[...] = dmknw

    return pl.pallas_call(
        body, name=name,
        out_shape=[jax.ShapeDtypeStruct(mnw.shape, F32), jax.ShapeDtypeStruct(wkv.shape, F32), jax.ShapeDtypeStruct(mknw.shape, F32)],
        compiler_params=pltpu.CompilerParams(vmem_limit_bytes=VMEM_LIMIT),
    )(mem, mnw, wkv, mknw, dmk, dmv)


def _row(v, width=None):
    v = v.reshape(1, -1)
    if width is not None and v.shape[1] < width:
        v = jnp.pad(v, ((0, 0), (0, width - v.shape[1])))
    return v


def _head_cols(a):
    return a[:, :N_HEADS].T[:, :, None]


def _lanes_from_heads(*cols):
    t = cols[0].shape[1]
    parts = [c[:, :, 0].T for c in cols]
    parts.append(jnp.zeros((t, LANES - N_HEADS * len(cols)), F32))
    return jnp.concatenate(parts, axis=1)


def _norm_fwd(x, w, name):
    return rows_call(lambda x, w: rms(x, w), [x], [w], [(D_MODEL, BF16)], [], tm=512, name=name)[0]


def _norm_bwd(x, w, dh, dx_in, name):
    def fn(x, dh, dx_in, w):
        _, vjp = jax.vjp(rms, x, w)
        dx, dw = vjp(dh)
        return dx + dx_in, dw
    return rows_call(fn, [x, dh, dx_in], [w], [(D_MODEL, F32)], [(1, D_MODEL)], tm=512, name=name)


def _mlp_fwd(x, n2w, w1, w2, tag):
    h2 = _norm_fwd(x, n2w, f"norm2_fwd_{tag}")
    u = matmul(h2, w1, name=f"mlp1_fwd_{tag}")
    a1 = rows_call(lambda u: jnp.square(jnp.maximum(u, 0.0)), [u], [], [(D_FF, BF16)], [], tm=256, name=f"act_fwd_{tag}")[0]
    y = matmul(a1, w2, add=x, name=f"mlp2_fwd_{tag}")
    return y, (x, h2, u, a1)


def _mlp_bwd(dy, res, n2w, w1, w2, tag):
    x, h2, u, a1 = res
    da1 = matmul(dy, w2, tb=True, name=f"mlp2_dx_{tag}")
    dw2 = matmul(a1, dy, ta=True, name=f"mlp2_dw_{tag}")
    du = rows_call(lambda u, da: da * (2.0 * jnp.maximum(u, 0.0)), [u, da1], [], [(D_FF, BF16)], [], tm=256,
                   name=f"act_bwd_{tag}")[0]
    dh2 = matmul(du, w1, tb=True, name=f"mlp1_dx_{tag}")
    dw1 = matmul(h2, du, ta=True, name=f"mlp1_dw_{tag}")
    dx, dn2w = _norm_bwd(x, n2w, dh2, dy, f"norm2_bwd_{tag}")
    return dx, dw1, dw2, dn2w


def _in_proj_bwd(h, dmain, dsmall, w_main, w_small, tag):
    dh = matmul(dmain, w_main, tb=True, name=f"inproj_dx_main_{tag}")
    dh = matmul(dsmall, w_small, tb=True, add=dh, name=f"inproj_dx_small_{tag}")
    dw_main = matmul(h, dmain, ta=True, name=f"inproj_dw_main_{tag}")
    dw_small = matmul(h, dsmall, ta=True, name=f"inproj_dw_small_{tag}")
    return dh, dw_main, dw_small


def local_step(x, mem, target, w):
    t = x.shape[0]
    n_mem = mem.shape[0]
    g = {}

    mnw, mknw = _row(w["mem_norm_w"]), _row(w["mem_k_norm_w"])
    mk, mv = memkv_fwd(mem, mnw, w["w_mem_kv"], mknw, name="memkv_fwd")

    n1w0, n2w0 = _row(w["norm1_w"][0]), _row(w["norm2_w"][0])
    alog, dtb = _row(w["dn_a_log"][0], LANES), _row(w["dn_dt_bias"][0], LANES)
    onw, mqw0 = _row(w["dn_o_norm_w"][0]), _row(w["memq_norm_w"][0])
    x0 = x
    h0 = _norm_fwd(x0, n1w0, "norm1_fwd_0")
    pm0 = matmul(h0, w["dn_main"], name="inproj_main_0")
    ps0 = matmul(h0, w["dn_ab"], name="inproj_small_0")
    gates = rows_call(dn_gates_fn, [ps0], [alog, dtb], [(LANES, F32)], [], tm=512, name="dn_gates_fwd")[0]
    gc, beta = _head_cols(gates), _head_cols(gates[:, N_HEADS:])
    q0, k0, v0 = dn_prep_fwd(pm0, w["dn_conv_w"], name="dn_prep_fwd")
    o0, s_start = delta_fwd(q0, k0, v0, gc, beta, name="delta_fwd")
    cat0 = rows_call(dn_out_fn, [o0, (pm0, D_MODEL, 3), (pm0, MEM_WIDTH, 8)], [onw, mqw0, mk, mv],
                     [(D_MODEL + MEM_WIDTH, BF16)], [], tm=256, name="dn_out_fwd")[0]
    x1 = matmul(cat0, w["w_out"][0], add=x0, name="wout_fwd_0")
    x2, mlp_res0 = _mlp_fwd(x1, n2w0, w["w_mlp1"][0], w["w_mlp2"][0], "0")

    n1w1, n2w1 = _row(w["norm1_w"][1]), _row(w["norm2_w"][1])
    fbias = _row(w["fox_f_bias"][0], LANES)
    qnw, knw, mqw1 = _row(w["fox_q_norm_w"][0]), _row(w["fox_k_norm_w"][0]), _row(w["memq_norm_w"][1])
    h1 = _norm_fwd(x2, n1w1, "norm1_fwd_1")
    pm1 = matmul(h1, w["fox_main"], name="inproj_main_1")
    ps1 = matmul(h1, w["fox_f"], name="inproj_small_1")
    fcum = rows_call(fox_fcum_fn, [ps1], [fbias], [(LANES, F32)], [], tm=t, name="fox_fcum_fwd")[0]
    fq = _head_cols(fcum)
    fk = jnp.swapaxes(fq, 1, 2)
    q1, k1 = rows_call(fox_qk_fn, [(pm1, D_MODEL, 0), (pm1, D_MODEL, 1)], [qnw, knw], [(D_MODEL, F32)] * 2, [], tm=256,
                       name="fox_qk_fwd")
    o1 = fox_attn_fwd(q1, k1, pm1, fq, fk, name="fox_attn_fwd")
    cat1 = rows_call(fox_out_fn, [o1, (pm1, D_MODEL, 3), (pm1, MEM_WIDTH, 8)], [mqw1, mk, mv],
                     [(D_MODEL + MEM_WIDTH, BF16)], [], tm=256, name="fox_out_fwd")[0]
    x3 = matmul(cat1, w["w_out"][1], add=x2, name="wout_fwd_1")
    y, mlp_res1 = _mlp_fwd(x3, n2w1, w["w_mlp1"][1], w["w_mlp2"][1], "1")

    def loss_fn(y, tgt):
        e = y - tgt
        return e * (1.0 / D_MODEL), jnp.sum(jnp.sum(e * e, axis=1, keepdims=True), axis=0, keepdims=True)
    dy, sq = rows_call(loss_fn, [y, target], [], [(D_MODEL, F32)], [(1, 1)], tm=512, name="loss")
    loss = sq[0, 0] * (0.5 / D_MODEL)

    dx3, dw1_1, dw2_1, dn2w1 = _mlp_bwd(dy, mlp_res1, n2w1, w["w_mlp1"][1], w["w_mlp2"][1], "1")
    dcat1 = matmul(dx3, w["w_out"][1], tb=True, name="wout_dx_1")
    dwo1 = matmul(cat1, dx3, ta=True, name="wout_dw_1")
    do1, dgate1, dqm1, dmqw1, dmk1, dmv1 = rows_call(
        functools.partial(vjp_rows(fox_out_fn, 3, (True, True, True)), n_row=3, n_ct=1),
        [o1, (pm1, D_MODEL, 3), (pm1, MEM_WIDTH, 8), dcat1], [mqw1, mk, mv],
        [(D_MODEL, F32), (D_MODEL, F32), (MEM_WIDTH, F32)], [(1, HEAD_DIM), (n_mem, MEM_WIDTH), (n_mem, MEM_WIDTH)],
        tm=256, name="fox_out_bwd")
    dq1, dk1, dv1, dfq, dfk = fox_attn_bwd(q1, k1, pm1, fq, fk, do1, name="fox_attn_bwd")
    dqraw1, dkraw1, dqnw, dknw = rows_call(
        functools.partial(vjp_rows(fox_qk_fn, 2, (True, True)), n_row=2, n_ct=2),
        [(pm1, D_MODEL, 0), (pm1, D_MODEL, 1), dq1, dk1], [qnw, knw],
        [(D_MODEL, F32)] * 2, [(1, HEAD_DIM)] * 2, tm=256, name="fox_qk_bwd")
    dfcum = _lanes_from_heads(dfq + jnp.swapaxes(dfk, 1, 2))
    dps1, dfbias = rows_call(
        functools.partial(vjp_rows(fox_fcum_fn, 1, (True,)), n_row=1, n_ct=1),
        [ps1, dfcum], [fbias], [(LANES, F32)], [(1, LANES)], tm=t, name="fox_fcum_bwd")
    dpm1 = jnp.concatenate([dqraw1, dkraw1, dv1, dgate1, dqm1], axis=1)
    dh1, dwmain1, dwsmall1 = _in_proj_bwd(h1, dpm1, dps1, w["fox_main"], w["fox_f"], "1")
    dx2, dn1w1 = _norm_bwd(x2, n1w1, dh1, dx3, "norm1_bwd_1")

    dx1, dw1_0, dw2_0, dn2w0 = _mlp_bwd(dx2, mlp_res0, n2w0, w["w_mlp1"][0], w["w_mlp2"][0], "0")
    dcat0 = matmul(dx1, w["w_out"][0], tb=True, name="wout_dx_0")
    dwo0 = matmul(cat0, dx1, ta=True, name="wout_dw_0")
    do0, dz0, dqm0, donw, dmqw0, dmk0, dmv0 = rows_call(
        functools.partial(vjp_rows(dn_out_fn, 3, (True, True, True, True)), n_row=3, n_ct=1),
        [o0, (pm0, D_MODEL, 3), (pm0, MEM_WIDTH, 8), dcat0], [onw, mqw0, mk, mv],
        [(D_MODEL, F32), (D_MODEL, F32), (MEM_WIDTH, F32)],
        [(1, HEAD_DIM), (1, HEAD_DIM), (n_mem, MEM_WIDTH), (n_mem, MEM_WIDTH)], tm=256, name="dn_out_bwd")
    dq0, dk0, dv0, dgc, dbeta = delta_bwd(q0, k0, v0, gc, beta, s_start, do0, name="delta_bwd")
    dxq, dxk, dxv, dcq, dck, dcv = dn_prep_bwd(pm0, w["dn_conv_w"], dq0, dk0, dv0, name="dn_prep_bwd")
    dconv = jnp.concatenate([dcq, dck, dcv], axis=1)
    dps0, dalog, ddtb = rows_call(
        functools.partial(vjp_rows(dn_gates_fn, 1, (True, True)), n_row=1, n_ct=1),
        [ps0, _lanes_from_heads(dgc, dbeta)], [alog, dtb], [(LANES, F32)], [(1, LANES)] * 2, tm=512, name="dn_gates_bwd")
    dpm0 = jnp.concatenate([dxq, dxk, dxv, dz0, dqm0], axis=1)
    dh0, dwmain0, dwsmall0 = _in_proj_bwd(h0, dpm0, dps0, w["dn_main"], w["dn_ab"], "0")
    grad_x, dn1w0 = _norm_bwd(x0, n1w0, dh0, dx1, "norm1_bwd_0")

    dmnw, dwkv, dmknw = memkv_bwd(mem, mnw, w["w_mem_kv"], mknw, dmk0 + dmk1, dmv0 + dmv1, name="memkv_bwd")

    g["mem_norm_w"] = dmnw[0]
    g["w_mem_kv"] = dwkv
    g["mem_k_norm_w"] = dmknw[0]
    g["norm1_w"] = jnp.concatenate([dn1w0, dn1w1], axis=0)
    g["dn_main"], g["dn_ab"] = dwmain0, dwsmall0
    g["dn_conv_w"] = dconv
    g["dn_a_log"] = dalog[:, :N_HEADS]
    g["dn_dt_bias"] = ddtb[:, :N_HEADS]
    g["dn_o_norm_w"] = donw
    g["fox_main"], g["fox_f"] = dwmain1, dwsmall1
    g["fox_f_bias"] = dfbias[:, :N_HEADS]
    g["fox_q_norm_w"] = dqnw
    g["fox_k_norm_w"] = dknw
    g["memq_norm_w"] = jnp.concatenate([dmqw0, dmqw1], axis=0)
    g["w_out"] = jnp.stack([dwo0, dwo1])
    g["norm2_w"] = jnp.concatenate([dn2w0, dn2w1], axis=0)
    g["w_mlp1"] = jnp.stack([dw1_0, dw1_1])
    g["w_mlp2"] = jnp.stack([dw2_0, dw2_1])
    return loss, grad_x, g


WEIGHTS = ["mem_norm_w", "w_mem_kv", "mem_k_norm_w", "norm1_w", "dn_w_in", "dn_conv_w", "dn_a_log", "dn_dt_bias",
           "dn_o_norm_w", "fox_w_in", "fox_f_bias", "fox_q_norm_w", "fox_k_norm_w", "memq_norm_w", "w_out", "norm2_w",
           "w_mlp1", "w_mlp2"]
DN_IN = 4 * D_MODEL + 2 * N_HEADS + MEM_WIDTH
FOX_IN = 4 * D_MODEL + N_HEADS + MEM_WIDTH
IN_PAD = 640
SMALL = [("mem_norm_w", (D_MODEL,), D_MODEL), ("mem_k_norm_w", (HEAD_DIM,), HEAD_DIM), ("norm1_w", (2, D_MODEL), 2 * D_MODEL),
         ("dn_a_log", (1, N_HEADS), LANES), ("dn_dt_bias", (1, N_HEADS), LANES), ("dn_o_norm_w", (1, HEAD_DIM), HEAD_DIM),
         ("fox_f_bias", (1, N_HEADS), LANES), ("fox_q_norm_w", (1, HEAD_DIM), HEAD_DIM), ("fox_k_norm_w", (1, HEAD_DIM), HEAD_DIM),
         ("memq_norm_w", (2, HEAD_DIM), 2 * HEAD_DIM), ("norm2_w", (2, D_MODEL), 2 * D_MODEL)]
SMALL_ROWS = 16
CONV_ROWS = 8
CONV_WIRE_ROWS = 16
SEG_ROWS = [("w_mem_kv", 256), ("dn_w_in", 1280), ("fox_w_in", 1280), ("w_out", 768), ("w_mlp1", 2048), ("w_mlp2", 2048)]
BIG_ROWS = sum(r for _, r in SEG_ROWS)
PACK_ROWS = BIG_ROWS + CONV_ROWS + SMALL_ROWS
WIRE_ROWS = BIG_ROWS + CONV_WIRE_ROWS
PACK_TILE = 856


def _seg_offsets():
    off, out = 0, {}
    for n, r in SEG_ROWS:
        out[n] = (off, off + r)
        off += r
    return out


SEG = _seg_offsets()


def _flat_rows(a, rows):
    a = a.reshape(-1)
    return jnp.pad(a, (0, rows * PACK_W - a.shape[0])).reshape(rows, PACK_W)


def _pack_big(p):
    def in_w(a):
        a = a[0]
        return jnp.pad(a, ((0, 0), (0, IN_PAD - a.shape[1]))).reshape(-1, PACK_W)
    return [p["w_mem_kv"].reshape(-1, PACK_W), in_w(p["dn_w_in"]), in_w(p["fox_w_in"]), p["w_out"].reshape(-1, PACK_W),
            p["w_mlp1"].reshape(-1, PACK_W), p["w_mlp2"].reshape(-1, PACK_W)]


def pack_shard(p):
    small = jnp.concatenate([jnp.pad(p[n].reshape(-1), (0, ln - math.prod(sh))) for n, sh, ln in SMALL])
    return jnp.concatenate(_pack_big(p) + [_flat_rows(p["dn_conv_w"], CONV_ROWS), _flat_rows(small, SMALL_ROWS)], axis=0)


def unpack_shard(pk):
    out = {}
    seg = {n: pk[a:b] for n, (a, b) in SEG.items()}
    out["w_mem_kv"] = seg["w_mem_kv"].reshape(D_MODEL // N_DEV, D_MODEL)
    out["dn_w_in"] = seg["dn_w_in"].reshape(D_MODEL, IN_PAD)[None, :, :DN_IN // N_DEV]
    out["fox_w_in"] = seg["fox_w_in"].reshape(D_MODEL, IN_PAD)[None, :, :FOX_IN // N_DEV]
    out["w_out"] = seg["w_out"].reshape(2, (D_MODEL + MEM_WIDTH) // N_DEV, D_MODEL)
    out["w_mlp1"] = seg["w_mlp1"].reshape(2, D_MODEL, D_FF // N_DEV)
    out["w_mlp2"] = seg["w_mlp2"].reshape(2, D_FF // N_DEV, D_MODEL)
    n_conv = CONV_WIDTH * 3 * D_MODEL // N_DEV
    out["dn_conv_w"] = pk[BIG_ROWS:BIG_ROWS + CONV_ROWS].reshape(-1)[:n_conv].reshape(1, CONV_WIDTH, 3 * D_MODEL // N_DEV)
    small = pk[BIG_ROWS + CONV_ROWS:].reshape(-1)
    off = 0
    for n, sh, ln in SMALL:
        out[n] = small[off:off + math.prod(sh)].reshape(sh)
        off += ln
    return out


def pack_wire(p):
    cw = p["dn_conv_w"].reshape(-1)
    hi = cw.astype(BF16)
    lo = (cw - hi.astype(F32)).astype(BF16)
    big = [a.astype(BF16) for a in _pack_big(p)]
    return jnp.concatenate(big + [_flat_rows(jnp.concatenate([hi, lo]), CONV_WIRE_ROWS)], axis=0)


def unpack_gathered(gw, p):
    w = {n: p[n] for n, _, _ in SMALL}
    seg = {n: gw[:, a:b] for n, (a, b) in SEG.items()}
    w["w_mem_kv"] = seg["w_mem_kv"].reshape(D_MODEL, D_MODEL)

    def in_w(s, width):
        return s.reshape(N_DEV, D_MODEL, IN_PAD)[:, :, :width // N_DEV].transpose(1, 0, 2).reshape(D_MODEL, width)

    dn = in_w(seg["dn_w_in"], DN_IN)
    fox = in_w(seg["fox_w_in"], FOX_IN)
    gate_end = 4 * D_MODEL
    w["dn_main"] = jnp.concatenate([dn[:, :gate_end], dn[:, gate_end + 2 * N_HEADS:]], axis=1)
    w["dn_ab"] = jnp.pad(dn[:, gate_end:gate_end + 2 * N_HEADS], ((0, 0), (0, LANES - 2 * N_HEADS)))
    w["fox_main"] = jnp.concatenate([fox[:, :gate_end], fox[:, gate_end + N_HEADS:]], axis=1)
    w["fox_f"] = jnp.pad(fox[:, gate_end:gate_end + N_HEADS], ((0, 0), (0, LANES - N_HEADS)))
    w["w_out"] = seg["w_out"].reshape(N_DEV, 2, -1, D_MODEL).transpose(1, 0, 2, 3).reshape(2, D_MODEL + MEM_WIDTH, D_MODEL)
    w["w_mlp1"] = seg["w_mlp1"].reshape(N_DEV, 2, D_MODEL, -1).transpose(1, 2, 0, 3).reshape(2, D_MODEL, D_FF)
    w["w_mlp2"] = seg["w_mlp2"].reshape(N_DEV, 2, -1, D_MODEL).transpose(1, 0, 2, 3).reshape(2, D_FF, D_MODEL)
    n_conv = CONV_WIDTH * 3 * D_MODEL // N_DEV
    cw = gw[:, BIG_ROWS:].reshape(N_DEV, -1)[:, :2 * n_conv].astype(F32)
    cw = cw[:, :n_conv] + cw[:, n_conv:]
    w["dn_conv_w"] = cw.reshape(N_DEV, CONV_WIDTH, -1).transpose(1, 0, 2).reshape(CONV_WIDTH, 3 * D_MODEL)
    return w


def pack_grads(g):
    gate_end = 4 * D_MODEL

    def in_w(main, small, n_small, width):
        full = jnp.concatenate([main[:, :gate_end], small[:, :n_small], main[:, gate_end:]], axis=1)
        s = full.reshape(D_MODEL, N_DEV, width // N_DEV).transpose(1, 0, 2)
        return jnp.pad(s, ((0, 0), (0, 0), (0, IN_PAD - width // N_DEV))).reshape(N_DEV, -1, PACK_W)

    small = jnp.concatenate([jnp.pad(g[n].reshape(-1), (0, ln - math.prod(sh))) for n, sh, ln in SMALL])
    parts = [
        g["w_mem_kv"].reshape(N_DEV, -1, PACK_W),
        in_w(g["dn_main"], g["dn_ab"], 2 * N_HEADS, DN_IN),
        in_w(g["fox_main"], g["fox_f"], N_HEADS, FOX_IN),
        g["w_out"].reshape(2, N_DEV, -1, D_MODEL).transpose(1, 0, 2, 3).reshape(N_DEV, -1, PACK_W),
        g["w_mlp1"].reshape(2, D_MODEL, N_DEV, -1).transpose(2, 0, 1, 3).reshape(N_DEV, -1, PACK_W),
        g["w_mlp2"].reshape(2, N_DEV, -1, D_MODEL).transpose(1, 0, 2, 3).reshape(N_DEV, -1, PACK_W),
        jnp.stack([_flat_rows(c, CONV_ROWS) for c in g["dn_conv_w"].reshape(CONV_WIDTH, N_DEV, -1).transpose(1, 0, 2)]),
        jnp.broadcast_to(_flat_rows(small, SMALL_ROWS), (N_DEV, SMALL_ROWS, PACK_W)),
    ]
    return jnp.concatenate(parts, axis=1)


_HBM = pl.BlockSpec(memory_space=pltpu.HBM)


def _place():
    return lax.axis_index("x"), lax.axis_index("y"), lax.axis_index("c")


def all_gather(xs, *, name):
    def body(x_ref, out_ref, send_sems, recv_sems, local_sem):
        x, y, c = _place()
        me, sibling = (x, y, c), (x, y, 1 - c)
        chips = [(1 - x, y), (x, 1 - y), (1 - x, 1 - y)]

        def slot(px, py, pc):
            return out_ref.at[4 * px + 2 * py + pc]

        def copy(k, block, to, src=None):
            return pltpu.make_async_remote_copy(
                src_ref=slot(*block) if src is None else src, dst_ref=slot(*block),
                send_sem=send_sems.at[k], recv_sem=recv_sems.at[k], device_id=to, device_id_type=MESH)

        mine = pltpu.make_async_copy(x_ref, slot(*me), local_sem)
        mine.start()
        first = [copy(0, me, sibling, src=x_ref)]
        first += [copy(1 + j, me, (*chip, c), src=x_ref) for j, chip in enumerate(chips)]
        for cp in first:
            cp.start()
        passed = [copy(4 + j, (*chip, c), sibling) for j, chip in enumerate(chips)]
        for j, chip in enumerate(chips):
            copy(1 + j, (*chip, c), me).wait_recv()
            passed[j].start()
        copy(0, sibling, me).wait_recv()
        for j, chip in enumerate(chips):
            copy(4 + j, (*chip, 1 - c), me).wait_recv()
        for cp in first + passed:
            cp.wait_send()
        mine.wait()

    return pl.pallas_call(
        body, name=name, out_shape=jax.ShapeDtypeStruct((N_DEV,) + xs.shape, xs.dtype), in_specs=[_HBM], out_specs=_HBM,
        scratch_shapes=[pltpu.SemaphoreType.DMA((7,)), pltpu.SemaphoreType.DMA((7,)), pltpu.SemaphoreType.DMA],
    )(xs)


def sibling_exchange(g, *, name):
    def body(g_ref, out_ref, send_sems, recv_sems):
        x, y, c = _place()
        copies = [pltpu.make_async_remote_copy(
            src_ref=g_ref.at[2 * k + 1 - c], dst_ref=out_ref.at[k], send_sem=send_sems.at[k], recv_sem=recv_sems.at[k],
            device_id=(x, y, 1 - c), device_id_type=MESH) for k in range(4)]
        for cp in copies:
            cp.start()
        for cp in copies:
            cp.wait_recv()
        for cp in copies:
            cp.wait_send()

    return pl.pallas_call(
        body, name=name, out_shape=jax.ShapeDtypeStruct((4,) + g.shape[1:], g.dtype), in_specs=[_HBM], out_specs=_HBM,
        scratch_shapes=[pltpu.SemaphoreType.DMA((4,)), pltpu.SemaphoreType.DMA((4,))],
    )(g)


def pair_sum(g, got, *, name):
    rows = g.shape[1]
    c = lax.axis_index("c").astype(jnp.int32).reshape(1)

    def body(c_ref, a_ref, b_ref, o_ref):
        o_ref[...] = a_ref[...] + b_ref[...]

    grid_spec = pltpu.PrefetchScalarGridSpec(
        num_scalar_prefetch=1, grid=(4, rows // PACK_TILE),
        in_specs=[pl.BlockSpec((None, PACK_TILE, PACK_W), lambda k, i, c_ref: (2 * k + c_ref[0], i, 0)),
                  pl.BlockSpec((None, PACK_TILE, PACK_W), lambda k, i, c_ref: (k, i, 0))],
        out_specs=pl.BlockSpec((None, PACK_TILE, PACK_W), lambda k, i, c_ref: (k, i, 0)))
    return pl.pallas_call(
        body, name=name, grid_spec=grid_spec, out_shape=jax.ShapeDtypeStruct((4,) + g.shape[1:], F32),
        compiler_params=_params(("parallel", "parallel")),
    )(c, g, got)


def chip_exchange(h, *, name):
    def body(h_ref, out_ref, send_sems, recv_sems, local_sem):
        x, y, c = _place()
        mine = 2 * x + y
        chips = [(1 - x, y), (x, 1 - y), (1 - x, 1 - y)]
        keep = pltpu.make_async_copy(h_ref.at[mine], out_ref.at[mine], local_sem)
        keep.start()
        sends = [pltpu.make_async_remote_copy(
            src_ref=h_ref.at[2 * qx + qy], dst_ref=out_ref.at[mine], send_sem=send_sems.at[j], recv_sem=recv_sems.at[j],
            device_id=(qx, qy, c), device_id_type=MESH) for j, (qx, qy) in enumerate(chips)]
        for cp in sends:
            cp.start()
        for j, (qx, qy) in enumerate(chips):
            pltpu.make_async_remote_copy(
                src_ref=h_ref.at[mine], dst_ref=out_ref.at[2 * qx + qy], send_sem=send_sems.at[j], recv_sem=recv_sems.at[j],
                device_id=(qx, qy, c), device_id_type=MESH).wait_recv()
        for cp in sends:
            cp.wait_send()
        keep.wait()

    return pl.pallas_call(
        body, name=name, out_shape=jax.ShapeDtypeStruct(h.shape, h.dtype), in_specs=[_HBM], out_specs=_HBM,
        scratch_shapes=[pltpu.SemaphoreType.DMA((3,)), pltpu.SemaphoreType.DMA((3,)), pltpu.SemaphoreType.DMA],
    )(h)


def adamw(parts, w, m, v, *, name):
    rows = w.shape[0]

    def body(p_ref, w_ref, m_ref, v_ref, g_ref, d_ref, mo_ref, vo_ref):
        g = ((p_ref[0] + p_ref[1]) + p_ref[2]) + p_ref[3]
        m_new = ADAM_B1 * m_ref[...] + (1.0 - ADAM_B1) * g
        v_new = ADAM_B2 * v_ref[...] + (1.0 - ADAM_B2) * jnp.square(g)
        m_hat = m_new / (1.0 - ADAM_B1 ** ADAM_STEP)
        v_hat = v_new / (1.0 - ADAM_B2 ** ADAM_STEP)
        g_ref[...] = g
        d_ref[...] = -ADAM_LR * (m_hat / (jnp.sqrt(v_hat) + ADAM_EPS) + ADAM_WD * w_ref[...])
        mo_ref[...] = m_new
        vo_ref[...] = v_new

    spec = pl.BlockSpec((PACK_TILE, PACK_W), lambda i: (i, 0))
    return pl.pallas_call(
        body, name=name, grid=(rows // PACK_TILE,),
        in_specs=[pl.BlockSpec((4, PACK_TILE, PACK_W), lambda i: (0, i, 0)), spec, spec, spec], out_specs=[spec] * 4,
        out_shape=[jax.ShapeDtypeStruct((rows, PACK_W), F32)] * 4, compiler_params=_params(("parallel",)),
    )(parts, w, m, v)


def kernel(x, mem, mem_norm_w, w_mem_kv, mem_k_norm_w, norm1_w, dn_w_in, dn_conv_w, dn_a_log, dn_dt_bias, dn_o_norm_w, fox_w_in, fox_f_bias, fox_q_norm_w, fox_k_norm_w, memq_norm_w, w_out, norm2_w, w_mlp1, w_mlp2, loss_target, m_mem_norm_w, m_w_mem_kv, m_mem_k_norm_w, m_norm1_w, m_dn_w_in, m_dn_conv_w, m_dn_a_log, m_dn_dt_bias, m_dn_o_norm_w, m_fox_w_in, m_fox_f_bias, m_fox_q_norm_w, m_fox_k_norm_w, m_memq_norm_w, m_w_out, m_norm2_w, m_w_mlp1, m_w_mlp2, v_mem_norm_w, v_w_mem_kv, v_mem_k_norm_w, v_norm1_w, v_dn_w_in, v_dn_conv_w, v_dn_a_log, v_dn_dt_bias, v_dn_o_norm_w, v_fox_w_in, v_fox_f_bias, v_fox_q_norm_w, v_fox_k_norm_w, v_memq_norm_w, v_w_out, v_norm2_w, v_w_mlp1, v_w_mlp2):
    p = dict(mem_norm_w=mem_norm_w, w_mem_kv=w_mem_kv, mem_k_norm_w=mem_k_norm_w, norm1_w=norm1_w, dn_w_in=dn_w_in,
             dn_conv_w=dn_conv_w, dn_a_log=dn_a_log, dn_dt_bias=dn_dt_bias, dn_o_norm_w=dn_o_norm_w, fox_w_in=fox_w_in,
             fox_f_bias=fox_f_bias, fox_q_norm_w=fox_q_norm_w, fox_k_norm_w=fox_k_norm_w, memq_norm_w=memq_norm_w,
             w_out=w_out, norm2_w=norm2_w, w_mlp1=w_mlp1, w_mlp2=w_mlp2)
    pm = dict(mem_norm_w=m_mem_norm_w, w_mem_kv=m_w_mem_kv, mem_k_norm_w=m_mem_k_norm_w, norm1_w=m_norm1_w,
              dn_w_in=m_dn_w_in, dn_conv_w=m_dn_conv_w, dn_a_log=m_dn_a_log, dn_dt_bias=m_dn_dt_bias,
              dn_o_norm_w=m_dn_o_norm_w, fox_w_in=m_fox_w_in, fox_f_bias=m_fox_f_bias, fox_q_norm_w=m_fox_q_norm_w,
              fox_k_norm_w=m_fox_k_norm_w, memq_norm_w=m_memq_norm_w, w_out=m_w_out, norm2_w=m_norm2_w, w_mlp1=m_w_mlp1,
              w_mlp2=m_w_mlp2)
    pv = dict(mem_norm_w=v_mem_norm_w, w_mem_kv=v_w_mem_kv, mem_k_norm_w=v_mem_k_norm_w, norm1_w=v_norm1_w,
              dn_w_in=v_dn_w_in, dn_conv_w=v_dn_conv_w, dn_a_log=v_dn_a_log, dn_dt_bias=v_dn_dt_bias,
              dn_o_norm_w=v_dn_o_norm_w, fox_w_in=v_fox_w_in, fox_f_bias=v_fox_f_bias, fox_q_norm_w=v_fox_q_norm_w,
              fox_k_norm_w=v_fox_k_norm_w, memq_norm_w=v_memq_norm_w, w_out=v_w_out, norm2_w=v_norm2_w, w_mlp1=v_w_mlp1,
              w_mlp2=v_w_mlp2)

    gathered = all_gather(pack_wire(p), name="weights_all_gather")
    loss, grad_x, g = local_step(x[0], mem[0], loss_target[0], unpack_gathered(gathered, p))
    loss = lax.psum(loss, ("x", "y", "c"))

    gp = pack_grads(g)
    chip_sums = pair_sum(gp, sibling_exchange(gp, name="grads_to_sibling"), name="grads_pair_sum")
    parts = chip_exchange(chip_sums, name="grads_to_chips")
    outs = adamw(parts, pack_shard(p), pack_shard(pm), pack_shard(pv), name="adamw")
    groups = [unpack_shard(o) for o in outs]
    return (loss, grad_x[None], *[grp[n] for grp in groups for n in WEIGHTS])
```

```python
import functools
import math

import jax
import jax.numpy as jnp
from jax import lax
from jax.experimental import pallas as pl
from jax.experimental.pallas import tpu as pltpu

F32 = jnp.float32
BF16 = jnp.bfloat16
HIGHEST = lax.Precision.HIGHEST

D_MODEL = 1024
HEAD_DIM = 128
N_HEADS = 8
MEM_HEADS = 4
MEM_WIDTH = MEM_HEADS * HEAD_DIM
D_FF = 4 * D_MODEL
CONV_WIDTH = 4
CHUNK = 64
Q_BLOCK = 128
EPS = 1e-6
SCALE = HEAD_DIM ** -0.5
MAIN_WIDTH = 4 * D_MODEL + MEM_WIDTH
LANES = 128
N_DEV = 8
PACK_W = 512

ADAM_LR = 0.001
ADAM_B1 = 0.9
ADAM_B2 = 0.999
ADAM_EPS = 1e-08
ADAM_WD = 0.01
ADAM_STEP = 10

VMEM_LIMIT = 56 * 2 ** 20
MESH = pl.DeviceIdType.MESH


def _bdot(a, b, dims):
    return lax.dot_general(a.astype(BF16), b.astype(BF16), (dims, ((), ())), preferred_element_type=F32)


@jax.custom_vjp
def mm(a, b):
    return _bdot(a, b, ((1,), (0,)))


@jax.custom_vjp
def mm_nt(a, b):
    return _bdot(a, b, ((1,), (1,)))


@jax.custom_vjp
def mm_tn(a, b):
    return _bdot(a, b, ((0,), (0,)))


mm.defvjp(lambda a, b: (mm(a, b), (a, b)), lambda r, g: (mm_nt(g, r[1]), mm_tn(r[0], g)))
mm_nt.defvjp(lambda a, b: (mm_nt(a, b), (a, b)), lambda r, g: (mm(g, r[1]), mm_tn(g, r[0])))
mm_tn.defvjp(lambda a, b: (mm_tn(a, b), (a, b)), lambda r, g: (mm_nt(r[1], g), mm(r[0], g)))


def hdot(a, b):
    return jnp.dot(a, b, precision=HIGHEST, preferred_element_type=F32)


def rms(x, w):
    return x * lax.rsqrt(jnp.mean(x * x, axis=-1, keepdims=True) + EPS) * w


def l2n(x):
    return x * lax.rsqrt(jnp.sum(x * x, axis=-1, keepdims=True) + EPS)


def _iota2(n, m):
    return lax.broadcasted_iota(jnp.int32, (n, m), 0), lax.broadcasted_iota(jnp.int32, (n, m), 1)


def _lower_ones(n):
    r, c = _iota2(n, n)
    return jnp.where(r >= c, 1.0, 0.0).astype(F32)


def _last_row(x):
    r = lax.broadcasted_iota(jnp.int32, x.shape, 0)
    return jnp.sum(jnp.where(r == x.shape[0] - 1, x, 0.0), axis=0, keepdims=True)


def _softmax_rows(z):
    m = lax.stop_gradient(jnp.max(z, axis=-1, keepdims=True))
    e = jnp.exp(z - m)
    return e / jnp.sum(e, axis=-1, keepdims=True)


def inv_unit_lower(a):
    n = a.shape[0]
    r, c = _iota2(n, n)
    p = jnp.where(r == c, 1.0, 0.0).astype(F32) - a
    ak = a
    for _ in range(int(math.log2(n)) - 1):
        ak = hdot(ak, ak)
        p = p + hdot(p, ak)
    return p


def delta_chunk(q, k, v, gc, beta, s):
    c = q.shape[0]
    r, cc = _iota2(c, c)
    causal = r >= cc
    strict = r > cc
    gi = jnp.broadcast_to(gc, (c, c))
    gj = gi.T
    decay = jnp.where(causal, jnp.exp(jnp.where(causal, gi - gj, 0.0)), 0.0)
    kb = k * beta
    a = jnp.where(strict, mm_nt(kb, k) * decay, 0.0)
    t = inv_unit_lower(a)
    u = mm(t, v * beta)
    w = mm(t, kb * jnp.exp(gc))
    qk = jnp.where(causal, mm_nt(q, k) * decay, 0.0)
    v_new = u - mm(w, s)
    out = mm(q * jnp.exp(gc), s) + mm(qk, v_new)
    g_last = _last_row(gc)
    k_dec = k * jnp.exp(g_last - gc)
    s_new = s * jnp.exp(g_last) + mm_tn(k_dec, v_new)
    return out, s_new


def fox_block(q, k, v, fq, fk, qpos0):
    s = mm_nt(q, k)
    r, c = _iota2(s.shape[0], s.shape[1])
    mask = c <= (r + qpos0)
    p = _softmax_rows(jnp.where(mask, s + (fq - fk), -jnp.inf))
    return mm(p, v)


def mem_head(qm, wq, mk, mv):
    p = _softmax_rows(mm_nt(rms(qm, wq) * SCALE, mk))
    return mm(p, mv)


def _heads(x, n):
    return [x[:, h * HEAD_DIM:(h + 1) * HEAD_DIM] for h in range(n)]


def memkv_fn(mem, mnw, wkv, mknw):
    kv = mm(rms(mem, mnw), wkv)
    mk = jnp.concatenate([rms(kh, mknw) for kh in _heads(kv[:, :MEM_WIDTH], MEM_HEADS)], axis=1)
    return mk, kv[:, MEM_WIDTH:]


def dn_gates_fn(ab, alog, dtb):
    g = -jnp.exp(alog) * jax.nn.softplus(ab + dtb)
    low = _lower_ones(CHUNK)
    gc = jnp.concatenate([hdot(low, g[i * CHUNK:(i + 1) * CHUNK]) for i in range(ab.shape[0] // CHUNK)], axis=0)
    lane = lax.broadcasted_iota(jnp.int32, ab.shape, 1)
    return jnp.where(lane < N_HEADS, gc, jax.nn.sigmoid(ab))


def fox_fcum_fn(fp, fbias):
    lf = jax.nn.log_sigmoid(fp + fbias)
    low = _lower_ones(LANES)
    carry = jnp.zeros((1, fp.shape[1]), F32)
    outs = []
    for i in range(fp.shape[0] // LANES):
        cs = hdot(low, lf[i * LANES:(i + 1) * LANES]) + carry
        carry = _last_row(cs)
        outs.append(cs)
    return jnp.concatenate(outs, axis=0)


def fox_qk_fn(qraw, kraw, qnw, knw):
    q = jnp.concatenate([rms(x, qnw) * SCALE for x in _heads(qraw, N_HEADS)], axis=1)
    k = jnp.concatenate([rms(x, knw) for x in _heads(kraw, N_HEADS)], axis=1)
    return q, k


def _mem_out(qm, mqw, mk, mv):
    return [mem_head(a, mqw, b, c) for a, b, c in zip(_heads(qm, MEM_HEADS), _heads(mk, MEM_HEADS), _heads(mv, MEM_HEADS))]


def dn_out_fn(o, z, qm, onw, mqw, mk, mv):
    mix = [rms(a, onw) * jax.nn.silu(b) for a, b in zip(_heads(o, N_HEADS), _heads(z, N_HEADS))]
    return jnp.concatenate(mix + _mem_out(qm, mqw, mk, mv), axis=1)


def fox_out_fn(o, gate, qm, mqw, mk, mv):
    return jnp.concatenate([o * jax.nn.sigmoid(gate)] + _mem_out(qm, mqw, mk, mv), axis=1)


def _pick(n, cands):
    for c in cands:
        if n % c == 0:
            return c
    return n


def _params(sem):
    return pltpu.CompilerParams(dimension_semantics=sem, vmem_limit_bytes=VMEM_LIMIT)


def matmul(a, b, *, name, ta=False, tb=False, add=None, out_dtype=F32):
    (k, m) = a.shape if ta else a.shape[::-1]
    (kb, n) = b.shape[::-1] if tb else b.shape
    assert k == kb, (a.shape, b.shape, ta, tb)
    tm = _pick(m, (512, 256, 128))
    tn = _pick(n, (512, 256, 128))
    tk = _pick(k, (1024, 512, 256, 128))
    nk = k // tk
    dims = ((0,) if ta else (1,), (1,) if tb else (0,))

    def body(*refs):
        if add is None:
            a_ref, b_ref, o_ref, acc = refs
        else:
            a_ref, b_ref, add_ref, o_ref, acc = refs
        kk = pl.program_id(2)

        @pl.when(kk == 0)
        def _():
            acc[...] = jnp.zeros_like(acc)

        acc[...] += _bdot(a_ref[...], b_ref[...], dims)

        @pl.when(kk == nk - 1)
        def _():
            r = acc[...]
            if add is not None:
                r = r + add_ref[...]
            o_ref[...] = r.astype(out_dtype)

    a_spec = pl.BlockSpec((tk, tm), lambda i, j, kk: (kk, i)) if ta else pl.BlockSpec((tm, tk), lambda i, j, kk: (i, kk))
    b_spec = pl.BlockSpec((tn, tk), lambda i, j, kk: (j, kk)) if tb else pl.BlockSpec((tk, tn), lambda i, j, kk: (kk, j))
    o_spec = pl.BlockSpec((tm, tn), lambda i, j, kk: (i, j))
    ins, specs = [a, b], [a_spec, b_spec]
    if add is not None:
        ins.append(add)
        specs.append(o_spec)
    return pl.pallas_call(
        body, name=name, grid=(m // tm, n // tn, nk), in_specs=specs, out_specs=o_spec,
        out_shape=jax.ShapeDtypeStruct((m, n), out_dtype),
        scratch_shapes=[pltpu.VMEM((tm, tn), F32)],
        compiler_params=_params(("parallel", "parallel", "arbitrary")),
    )(*ins)


def rows_call(fn, row_ins, full_ins, row_outs, acc_outs, *, tm, name):
    row_ins = [r if isinstance(r, tuple) else (r, r.shape[1], 0) for r in row_ins]
    t = row_ins[0][0].shape[0]
    tm = min(tm, t)
    n_in = len(row_ins) + len(full_ins)
    n_row = len(row_outs)

    def body(*refs):
        res = fn(*[r[...] for r in refs[:n_in]])
        res = res if isinstance(res, (tuple, list)) else (res,)
        outs = refs[n_in:]
        for ref, val in zip(outs[:n_row], res[:n_row]):
            ref[...] = val.astype(ref.dtype)
        first = pl.program_id(0) == 0
        for ref, val in zip(outs[n_row:], res[n_row:]):
            @pl.when(first)
            def _(ref=ref, val=val):
                ref[...] = val

            @pl.when(jnp.logical_not(first))
            def _(ref=ref, val=val):
                ref[...] += val

    def full_spec(shape):
        return pl.BlockSpec(shape, lambda i, nd=len(shape): (0,) * nd)

    in_specs = [pl.BlockSpec((tm, w), lambda i, cb=cb: (i, cb)) for (_, w, cb) in row_ins]
    in_specs += [full_spec(f.shape) for f in full_ins]
    out_specs = [pl.BlockSpec((tm, c), lambda i: (i, 0)) for c, _ in row_outs] + [full_spec(s) for s in acc_outs]
    out_shape = [jax.ShapeDtypeStruct((t, c), dt) for c, dt in row_outs] + [jax.ShapeDtypeStruct(s, F32) for s in acc_outs]
    res = pl.pallas_call(
        body, name=name, grid=(t // tm,), in_specs=in_specs, out_specs=out_specs, out_shape=out_shape,
        compiler_params=_params(("arbitrary",)),
    )(*[r[0] for r in row_ins], *full_ins)
    return res


def vjp_rows(fn, n_diff_row, row_diff_full):
    def bwd(*args, n_row, n_ct):
        prim_rows = args[:n_row]
        cts = args[n_row:n_row + n_ct]
        fulls = args[n_row + n_ct:]
        _, vjp = jax.vjp(fn, *prim_rows, *fulls)
        g = vjp(cts[0] if n_ct == 1 else tuple(cts))
        out = list(g[:n_diff_row])
        out += [gf for gf, d in zip(g[n_row:], row_diff_full) if d]
        return tuple(out)
    return bwd


def _shift_down(x, s):
    if s == 0:
        return x
    t = lax.broadcasted_iota(jnp.int32, x.shape, 0)
    return jnp.where(t >= s, pltpu.roll(x, s, 0), 0.0)


def _shift_up(x, s):
    if s == 0:
        return x
    n = x.shape[0]
    t = lax.broadcasted_iota(jnp.int32, x.shape, 0)
    return jnp.where(t < n - s, pltpu.roll(x, n - s, 0), 0.0)


def _conv(x, w_ref):
    return sum(w_ref[pl.ds(j, 1), :] * _shift_down(x, CONV_WIDTH - 1 - j) for j in range(CONV_WIDTH))


_DN_POST = (lambda c: l2n(jax.nn.silu(c)) * SCALE, lambda c: l2n(jax.nn.silu(c)), jax.nn.silu)


def dn_prep_fwd(proj, conv_w, *, name):
    t = proj.shape[0]

    def body(xq, xk, xv, wq, wk, wv, oq, ok, ov):
        for x_ref, w_ref, o_ref, post in zip((xq, xk, xv), (wq, wk, wv), (oq, ok, ov), _DN_POST):
            o_ref[...] = post(_conv(x_ref[...], w_ref))

    x_specs = [pl.BlockSpec((t, HEAD_DIM), lambda h, g=g: (0, g * N_HEADS + h)) for g in range(3)]
    w_specs = [pl.BlockSpec((CONV_WIDTH, HEAD_DIM), lambda h, g=g: (0, g * N_HEADS + h)) for g in range(3)]
    o_spec = pl.BlockSpec((t, HEAD_DIM), lambda h: (0, h))
    return pl.pallas_call(
        body, name=name, grid=(N_HEADS,), in_specs=x_specs + w_specs, out_specs=[o_spec] * 3,
        out_shape=[jax.ShapeDtypeStruct((t, D_MODEL), F32)] * 3, compiler_params=_params(("parallel",)),
    )(proj, proj, proj, conv_w, conv_w, conv_w)


def dn_prep_bwd(proj, conv_w, dq, dk, dv, *, name):
    t = proj.shape[0]

    def body(xq, xk, xv, wq, wk, wv, gq, gk, gv, dxq, dxk, dxv, dwq, dwk, dwv):
        for x_ref, w_ref, g_ref, dx_ref, dw_ref, post in zip(
                (xq, xk, xv), (wq, wk, wv), (gq, gk, gv), (dxq, dxk, dxv), (dwq, dwk, dwv), _DN_POST):
            x = x_ref[...]
            _, vjp = jax.vjp(post, _conv(x, w_ref))
            dc, = vjp(g_ref[...])
            dx_ref[...] = sum(w_ref[pl.ds(j, 1), :] * _shift_up(dc, CONV_WIDTH - 1 - j) for j in range(CONV_WIDTH))
            for j in range(CONV_WIDTH):
                dw_ref[pl.ds(j, 1), :] = jnp.sum(dc * _shift_down(x, CONV_WIDTH - 1 - j), axis=0, keepdims=True)

    x_specs = [pl.BlockSpec((t, HEAD_DIM), lambda h, g=g: (0, g * N_HEADS + h)) for g in range(3)]
    w_specs = [pl.BlockSpec((CONV_WIDTH, HEAD_DIM), lambda h, g=g: (0, g * N_HEADS + h)) for g in range(3)]
    g_spec = pl.BlockSpec((t, HEAD_DIM), lambda h: (0, h))
    dw_spec = pl.BlockSpec((CONV_WIDTH, HEAD_DIM), lambda h: (0, h))
    return pl.pallas_call(
        body, name=name, grid=(N_HEADS,), in_specs=x_specs + w_specs + [g_spec] * 3, out_specs=[g_spec] * 3 + [dw_spec] * 3,
        out_shape=[jax.ShapeDtypeStruct((t, D_MODEL), F32)] * 3 + [jax.ShapeDtypeStruct((CONV_WIDTH, D_MODEL), F32)] * 3,
        compiler_params=_params(("parallel",)),
    )(proj, proj, proj, conv_w, conv_w, conv_w, dq, dk, dv)


def _head_slice(h):
    return slice(h * HEAD_DIM, (h + 1) * HEAD_DIM)


def delta_fwd(q, k, v, gc, beta, *, name):
    t = q.shape[0]
    nc = t // CHUNK

    def body(q_ref, k_ref, v_ref, g_ref, b_ref, o_ref, s0_ref, s_ref):
        @pl.when(pl.program_id(0) == 0)
        def _():
            s_ref[...] = jnp.zeros_like(s_ref)

        for h in range(N_HEADS):
            sl = _head_slice(h)
            s = s_ref[h]
            s0_ref[h] = s
            o, s_new = delta_chunk(q_ref[:, sl], k_ref[:, sl], v_ref[:, sl], g_ref[h], b_ref[h], s)
            o_ref[:, sl] = o
            s_ref[h] = s_new

    x_spec = pl.BlockSpec((CHUNK, D_MODEL), lambda c: (c, 0))
    g_spec = pl.BlockSpec((N_HEADS, CHUNK, 1), lambda c: (0, c, 0))
    s_spec = pl.BlockSpec((N_HEADS, None, HEAD_DIM, HEAD_DIM), lambda c: (0, c, 0, 0))
    return pl.pallas_call(
        body, name=name, grid=(nc,), in_specs=[x_spec] * 3 + [g_spec] * 2, out_specs=[x_spec, s_spec],
        out_shape=[jax.ShapeDtypeStruct((t, D_MODEL), F32), jax.ShapeDtypeStruct((N_HEADS, nc, HEAD_DIM, HEAD_DIM), F32)],
        scratch_shapes=[pltpu.VMEM((N_HEADS, HEAD_DIM, HEAD_DIM), F32)],
        compiler_params=_params(("arbitrary",)),
    )(q, k, v, gc, beta)


def delta_bwd(q, k, v, gc, beta, s0, do, *, name):
    t = q.shape[0]
    nc = t // CHUNK

    def body(q_ref, k_ref, v_ref, g_ref, b_ref, s0_ref, do_ref, dq_ref, dk_ref, dv_ref, dg_ref, db_ref, ds_ref):
        @pl.when(pl.program_id(0) == 0)
        def _():
            ds_ref[...] = jnp.zeros_like(ds_ref)

        for h in range(N_HEADS):
            sl = _head_slice(h)
            _, vjp = jax.vjp(delta_chunk, q_ref[:, sl], k_ref[:, sl], v_ref[:, sl], g_ref[h], b_ref[h], s0_ref[h])
            dq, dk, dv, dg, db, ds = vjp((do_ref[:, sl], ds_ref[h]))
            dq_ref[:, sl] = dq
            dk_ref[:, sl] = dk
            dv_ref[:, sl] = dv
            dg_ref[h] = dg
            db_ref[h] = db
            ds_ref[h] = ds

    x_spec = pl.BlockSpec((CHUNK, D_MODEL), lambda c: (nc - 1 - c, 0))
    g_spec = pl.BlockSpec((N_HEADS, CHUNK, 1), lambda c: (0, nc - 1 - c, 0))
    s_spec = pl.BlockSpec((N_HEADS, None, HEAD_DIM, HEAD_DIM), lambda c: (0, nc - 1 - c, 0, 0))
    return pl.pallas_call(
        body, name=name, grid=(nc,), in_specs=[x_spec] * 3 + [g_spec] * 2 + [s_spec, x_spec],
        out_specs=[x_spec] * 3 + [g_spec] * 2,
        out_shape=[jax.ShapeDtypeStruct((t, D_MODEL), F32)] * 3 + [jax.ShapeDtypeStruct((N_HEADS, t, 1), F32)] * 2,
        scratch_shapes=[pltpu.VMEM((N_HEADS, HEAD_DIM, HEAD_DIM), F32)],
        compiler_params=_params(("arbitrary",)),
    )(q, k, v, gc, beta, s0, do)


_V_BLOCK = 2 * N_HEADS
FOX_GROUPS = 8


def _fox_groups(t):
    nq = t // Q_BLOCK
    per = max(1, nq // FOX_GROUPS)
    return [(g0, per, (g0 + per) * Q_BLOCK) for g0 in range(0, nq, per)]


def fox_attn_fwd(q, k, proj, fq, fk, *, name):
    t = q.shape[0]

    def body(q_ref, k_ref, v_ref, fq_ref, fk_ref, o_ref):
        for g0, per, keys in _fox_groups(t):
            def block(j, carry, g0=g0, keys=keys):
                rows = pl.ds(pl.multiple_of((g0 + j) * Q_BLOCK, Q_BLOCK), Q_BLOCK)
                o_ref[rows, :] = fox_block(q_ref[rows, :], k_ref[0:keys, :], v_ref[0:keys, :], fq_ref[rows, :],
                                           fk_ref[:, 0:keys], (g0 + j) * Q_BLOCK)
                return carry
            lax.fori_loop(0, per, block, 0)

    x_spec = pl.BlockSpec((t, HEAD_DIM), lambda h: (0, h))
    v_spec = pl.BlockSpec((t, HEAD_DIM), lambda h: (0, _V_BLOCK + h))
    fq_spec = pl.BlockSpec((None, t, 1), lambda h: (h, 0, 0))
    fk_spec = pl.BlockSpec((None, 1, t), lambda h: (h, 0, 0))
    return pl.pallas_call(
        body, name=name, grid=(N_HEADS,), in_specs=[x_spec, x_spec, v_spec, fq_spec, fk_spec], out_specs=x_spec,
        out_shape=jax.ShapeDtypeStruct((t, D_MODEL), F32), compiler_params=_params(("parallel",)),
    )(q, k, proj, fq, fk)


def fox_attn_bwd(q, k, proj, fq, fk, do, *, name):
    t = q.shape[0]

    def body(q_ref, k_ref, v_ref, fq_ref, fk_ref, do_ref, dq_ref, dk_ref, dv_ref, dfq_ref, dfk_ref):
        dk_ref[...] = jnp.zeros_like(dk_ref)
        dv_ref[...] = jnp.zeros_like(dv_ref)
        dfk_ref[...] = jnp.zeros_like(dfk_ref)
        for g0, per, keys in _fox_groups(t):
            def block(j, carry, g0=g0, keys=keys):
                rows = pl.ds(pl.multiple_of((g0 + j) * Q_BLOCK, Q_BLOCK), Q_BLOCK)
                f = functools.partial(fox_block, qpos0=(g0 + j) * Q_BLOCK)
                _, vjp = jax.vjp(f, q_ref[rows, :], k_ref[0:keys, :], v_ref[0:keys, :], fq_ref[rows, :], fk_ref[:, 0:keys])
                dq, dk, dv, dfq, dfk = vjp(do_ref[rows, :])
                dq_ref[rows, :] = dq
                dfq_ref[rows, :] = dfq
                dk_ref[0:keys, :] += dk
                dv_ref[0:keys, :] += dv
                dfk_ref[:, 0:keys] += dfk
                return carry
            lax.fori_loop(0, per, block, 0)

    x_spec = pl.BlockSpec((t, HEAD_DIM), lambda h: (0, h))
    v_spec = pl.BlockSpec((t, HEAD_DIM), lambda h: (0, _V_BLOCK + h))
    fq_spec = pl.BlockSpec((None, t, 1), lambda h: (h, 0, 0))
    fk_spec = pl.BlockSpec((None, 1, t), lambda h: (h, 0, 0))
    return pl.pallas_call(
        body, name=name, grid=(N_HEADS,), in_specs=[x_spec, x_spec, v_spec, fq_spec, fk_spec, x_spec],
        out_specs=[x_spec, x_spec, x_spec, fq_spec, fk_spec],
        out_shape=[jax.ShapeDtypeStruct((t, D_MODEL), F32)] * 3
        + [jax.ShapeDtypeStruct((N_HEADS, t, 1), F32), jax.ShapeDtypeStruct((N_HEADS, 1, t), F32)],
        compiler_params=_params(("parallel",)),
    )(q, k, proj, fq, fk, do)


def memkv_fwd(mem, mnw, wkv, mknw, *, name):
    n = mem.shape[0]

    def body(mem_ref, mnw_ref, w_ref, mknw_ref, mk_ref, mv_ref):
        mk, mv = memkv_fn(mem_ref[...], mnw_ref[...], w_ref[...], mknw_ref[...])
        mk_ref[...] = mk
        mv_ref[...] = mv

    return pl.pallas_call(
        body, name=name, out_shape=[jax.ShapeDtypeStruct((n, MEM_WIDTH), F32)] * 2,
        compiler_params=pltpu.CompilerParams(vmem_limit_bytes=VMEM_LIMIT),
    )(mem, mnw, wkv, mknw)


def memkv_bwd(mem, mnw, wkv, mknw, dmk, dmv, *, name):
    def body(mem_ref, mnw_ref, w_ref, mknw_ref, dmk_ref, dmv_ref, dmnw_ref, dw_ref, dmknw_ref):
        f = functools.partial(memkv_fn, mem_ref[...])
        _, vjp = jax.vjp(f, mnw_ref[...], w_ref[...].astype(F32), mknw_ref[...])
        dmnw, dw, dmknw = vjp((dmk_ref[...], dmv_ref[...]))
        dmnw_ref[...] = dmnw
        dw_ref[...] = dw
        dmknw_ref[...] = dmknw

    return pl.pallas_call(
        body, name=name,
        out_shape=[jax.ShapeDtypeStruct(mnw.shape, F32), jax.ShapeDtypeStruct(wkv.shape, F32), jax.ShapeDtypeStruct(mknw.shape, F32)],
        compiler_params=pltpu.CompilerParams(vmem_limit_bytes=VMEM_LIMIT),
    )(mem, mnw, wkv, mknw, dmk, dmv)


def _row(v, width=None):
    v = v.reshape(1, -1)
    if width is not None and v.shape[1] < width:
        v = jnp.pad(v, ((0, 0), (0, width - v.shape[1])))
    return v


def _head_cols(a):
    return a[:, :N_HEADS].T[:, :, None]


def _lanes_from_heads(*cols):
    t = cols[0].shape[1]
    parts = [c[:, :, 0].T for c in cols]
    parts.append(jnp.zeros((t, LANES - N_HEADS * len(cols)), F32))
    return jnp.concatenate(parts, axis=1)


def _norm_fwd(x, w, name):
    return rows_call(lambda x, w: rms(x, w), [x], [w], [(D_MODEL, BF16)], [], tm=512, name=name)[0]


def _norm_bwd(x, w, dh, dx_in, name):
    def fn(x, dh, dx_in, w):
        _, vjp = jax.vjp(rms, x, w)
        dx, dw = vjp(dh)
        return dx + dx_in, dw
    return rows_call(fn, [x, dh, dx_in], [w], [(D_MODEL, F32)], [(1, D_MODEL)], tm=512, name=name)


def _mlp_fwd(x, n2w, w1, w2, tag):
    h2 = _norm_fwd(x, n2w, f"norm2_fwd_{tag}")
    u = matmul(h2, w1, name=f"mlp1_fwd_{tag}")
    a1 = rows_call(lambda u: jnp.square(jnp.maximum(u, 0.0)), [u], [], [(D_FF, BF16)], [], tm=256, name=f"act_fwd_{tag}")[0]
    y = matmul(a1, w2, add=x, name=f"mlp2_fwd_{tag}")
    return y, (x, h2, u, a1)


def _mlp_bwd(dy, res, n2w, w1, w2, tag):
    x, h2, u, a1 = res
    da1 = matmul(dy, w2, tb=True, name=f"mlp2_dx_{tag}")
    dw2 = matmul(a1, dy, ta=True, name=f"mlp2_dw_{tag}")
    du = rows_call(lambda u, da: da * (2.0 * jnp.maximum(u, 0.0)), [u, da1], [], [(D_FF, BF16)], [], tm=256,
                   name=f"act_bwd_{tag}")[0]
    dh2 = matmul(du, w1, tb=True, name=f"mlp1_dx_{tag}")
    dw1 = matmul(h2, du, ta=True, name=f"mlp1_dw_{tag}")
    dx, dn2w = _norm_bwd(x, n2w, dh2, dy, f"norm2_bwd_{tag}")
    return dx, dw1, dw2, dn2w


def _in_proj_bwd(h, dmain, dsmall, w_main, w_small, tag):
    dh = matmul(dmain, w_main, tb=True, name=f"inproj_dx_main_{tag}")
    dh = matmul(dsmall, w_small, tb=True, add=dh, name=f"inproj_dx_small_{tag}")
    dw_main = matmul(h, dmain, ta=True, name=f"inproj_dw_main_{tag}")
    dw_small = matmul(h, dsmall, ta=True, name=f"inproj_dw_small_{tag}")
    return dh, dw_main, dw_small


def local_step(x, mem, target, w):
    t = x.shape[0]
    n_mem = mem.shape[0]
    g = {}

    mnw, mknw = _row(w["mem_norm_w"]), _row(w["mem_k_norm_w"])
    mk, mv = memkv_fwd(mem, mnw, w["w_mem_kv"], mknw, name="memkv_fwd")

    n1w0, n2w0 = _row(w["norm1_w"][0]), _row(w["norm2_w"][0])
    alog, dtb = _row(w["dn_a_log"][0], LANES), _row(w["dn_dt_bias"][0], LANES)
    onw, mqw0 = _row(w["dn_o_norm_w"][0]), _row(w["memq_norm_w"][0])
    x0 = x
    h0 = _norm_fwd(x0, n1w0, "norm1_fwd_0")
    pm0 = matmul(h0, w["dn_main"], name="inproj_main_0")
    ps0 = matmul(h0, w["dn_ab"], name="inproj_small_0")
    gates = rows_call(dn_gates_fn, [ps0], [alog, dtb], [(LANES, F32)], [], tm=512, name="dn_gates_fwd")[0]
    gc, beta = _head_cols(gates), _head_cols(gates[:, N_HEADS:])
    q0, k0, v0 = dn_prep_fwd(pm0, w["dn_conv_w"], name="dn_prep_fwd")
    o0, s_start = delta_fwd(q0, k0, v0, gc, beta, name="delta_fwd")
    cat0 = rows_call(dn_out_fn, [o0, (pm0, D_MODEL, 3), (pm0, MEM_WIDTH, 8)], [onw, mqw0, mk, mv],
                     [(D_MODEL + MEM_WIDTH, BF16)], [], tm=256, name="dn_out_fwd")[0]
    x1 = matmul(cat0, w["w_out"][0], add=x0, name="wout_fwd_0")
    x2, mlp_res0 = _mlp_fwd(x1, n2w0, w["w_mlp1"][0], w["w_mlp2"][0], "0")

    n1w1, n2w1 = _row(w["norm1_w"][1]), _row(w["norm2_w"][1])
    fbias = _row(w["fox_f_bias"][0], LANES)
    qnw, knw, mqw1 = _row(w["fox_q_norm_w"][0]), _row(w["fox_k_norm_w"][0]), _row(w["memq_norm_w"][1])
    h1 = _norm_fwd(x2, n1w1, "norm1_fwd_1")
    pm1 = matmul(h1, w["fox_main"], name="inproj_main_1")
    ps1 = matmul(h1, w["fox_f"], name="inproj_small_1")
    fcum = rows_call(fox_fcum_fn, [ps1], [fbias], [(LANES, F32)], [], tm=t, name="fox_fcum_fwd")[0]
    fq = _head_cols(fcum)
    fk = jnp.swapaxes(fq, 1, 2)
    q1, k1 = rows_call(fox_qk_fn, [(pm1, D_MODEL, 0), (pm1, D_MODEL, 1)], [qnw, knw], [(D_MODEL, F32)] * 2, [], tm=256,
                       name="fox_qk_fwd")
    o1 = fox_attn_fwd(q1, k1, pm1, fq, fk, name="fox_attn_fwd")
    cat1 = rows_call(fox_out_fn, [o1, (pm1, D_MODEL, 3), (pm1, MEM_WIDTH, 8)], [mqw1, mk, mv],
                     [(D_MODEL + MEM_WIDTH, BF16)], [], tm=256, name="fox_out_fwd")[0]
    x3 = matmul(cat1, w["w_out"][1], add=x2, name="wout_fwd_1")
    y, mlp_res1 = _mlp_fwd(x3, n2w1, w["w_mlp1"][1], w["w_mlp2"][1], "1")

    def loss_fn(y, tgt):
        e = y - tgt
        return e * (1.0 / D_MODEL), jnp.sum(jnp.sum(e * e, axis=1, keepdims=True), axis=0, keepdims=True)
    dy, sq = rows_call(loss_fn, [y, target], [], [(D_MODEL, F32)], [(1, 1)], tm=512, name="loss")
    loss = sq[0, 0] * (0.5 / D_MODEL)

    dx3, dw1_1, dw2_1, dn2w1 = _mlp_bwd(dy, mlp_res1, n2w1, w["w_mlp1"][1], w["w_mlp2"][1], "1")
    dcat1 = matmul(dx3, w["w_out"][1], tb=True, name="wout_dx_1")
    dwo1 = matmul(cat1, dx3, ta=True, name="wout_dw_1")
    do1, dgate1, dqm1, dmqw1, dmk1, dmv1 = rows_call(
        functools.partial(vjp_rows(fox_out_fn, 3, (True, True, True)), n_row=3, n_ct=1),
        [o1, (pm1, D_MODEL, 3), (pm1, MEM_WIDTH, 8), dcat1], [mqw1, mk, mv],
        [(D_MODEL, F32), (D_MODEL, F32), (MEM_WIDTH, F32)], [(1, HEAD_DIM), (n_mem, MEM_WIDTH), (n_mem, MEM_WIDTH)],
        tm=256, name="fox_out_bwd")
    dq1, dk1, dv1, dfq, dfk = fox_attn_bwd(q1, k1, pm1, fq, fk, do1, name="fox_attn_bwd")
    dqraw1, dkraw1, dqnw, dknw = rows_call(
        functools.partial(vjp_rows(fox_qk_fn, 2, (True, True)), n_row=2, n_ct=2),
        [(pm1, D_MODEL, 0), (pm1, D_MODEL, 1), dq1, dk1], [qnw, knw],
        [(D_MODEL, F32)] * 2, [(1, HEAD_DIM)] * 2, tm=256, name="fox_qk_bwd")
    dfcum = _lanes_from_heads(dfq + jnp.swapaxes(dfk, 1, 2))
    dps1, dfbias = rows_call(
        functools.partial(vjp_rows(fox_fcum_fn, 1, (True,)), n_row=1, n_ct=1),
        [ps1, dfcum], [fbias], [(LANES, F32)], [(1, LANES)], tm=t, name="fox_fcum_bwd")
    dpm1 = jnp.concatenate([dqraw1, dkraw1, dv1, dgate1, dqm1], axis=1)
    dh1, dwmain1, dwsmall1 = _in_proj_bwd(h1, dpm1, dps1, w["fox_main"], w["fox_f"], "1")
    dx2, dn1w1 = _norm_bwd(x2, n1w1, dh1, dx3, "norm1_bwd_1")

    dx1, dw1_0, dw2_0, dn2w0 = _mlp_bwd(dx2, mlp_res0, n2w0, w["w_mlp1"][0], w["w_mlp2"][0], "0")
    dcat0 = matmul(dx1, w["w_out"][0], tb=True, name="wout_dx_0")
    dwo0 = matmul(cat0, dx1, ta=True, name="wout_dw_0")
    do0, dz0, dqm0, donw, dmqw0, dmk0, dmv0 = rows_call(
        functools.partial(vjp_rows(dn_out_fn, 3, (True, True, True, True)), n_row=3, n_ct=1),
        [o0, (pm0, D_MODEL, 3), (pm0, MEM_WIDTH, 8), dcat0], [onw, mqw0, mk, mv],
        [(D_MODEL, F32), (D_MODEL, F32), (MEM_WIDTH, F32)],
        [(1, HEAD_DIM), (1, HEAD_DIM), (n_mem, MEM_WIDTH), (n_mem, MEM_WIDTH)], tm=256, name="dn_out_bwd")
    dq0, dk0, dv0, dgc, dbeta = delta_bwd(q0, k0, v0, gc, beta, s_start, do0, name="delta_bwd")
    dxq, dxk, dxv, dcq, dck, dcv = dn_prep_bwd(pm0, w["dn_conv_w"], dq0, dk0, dv0, name="dn_prep_bwd")
    dconv = jnp.concatenate([dcq, dck, dcv], axis=1)
    dps0, dalog, ddtb = rows_call(
        functools.partial(vjp_rows(dn_gates_fn, 1, (True, True)), n_row=1, n_ct=1),
        [ps0, _lanes_from_heads(dgc, dbeta)], [alog, dtb], [(LANES, F32)], [(1, LANES)] * 2, tm=512, name="dn_gates_bwd")
    dpm0 = jnp.concatenate([dxq, dxk, dxv, dz0, dqm0], axis=1)
    dh0, dwmain0, dwsmall0 = _in_proj_bwd(h0, dpm0, dps0, w["dn_main"], w["dn_ab"], "0")
    grad_x, dn1w0 = _norm_bwd(x0, n1w0, dh0, dx1, "norm1_bwd_0")

    dmnw, dwkv, dmknw = memkv_bwd(mem, mnw, w["w_mem_kv"], mknw, dmk0 + dmk1, dmv0 + dmv1, name="memkv_bwd")

    g["mem_norm_w"] = dmnw[0]
    g["w_mem_kv"] = dwkv
    g["mem_k_norm_w"] = dmknw[0]
    g["norm1_w"] = jnp.concatenate([dn1w0, dn1w1], axis=0)
    g["dn_main"], g["dn_ab"] = dwmain0, dwsmall0
    g["dn_conv_w"] = dconv
    g["dn_a_log"] = dalog[:, :N_HEADS]
    g["dn_dt_bias"] = ddtb[:, :N_HEADS]
    g["dn_o_norm_w"] = donw
    g["fox_main"], g["fox_f"] = dwmain1, dwsmall1
    g["fox_f_bias"] = dfbias[:, :N_HEADS]
    g["fox_q_norm_w"] = dqnw
    g["fox_k_norm_w"] = dknw
    g["memq_norm_w"] = jnp.concatenate([dmqw0, dmqw1], axis=0)
    g["w_out"] = jnp.stack([dwo0, dwo1])
    g["norm2_w"] = jnp.concatenate([dn2w0, dn2w1], axis=0)
    g["w_mlp1"] = jnp.stack([dw1_0, dw1_1])
    g["w_mlp2"] = jnp.stack([dw2_0, dw2_1])
    return loss, grad_x, g


WEIGHTS = ["mem_norm_w", "w_mem_kv", "mem_k_norm_w", "norm1_w", "dn_w_in", "dn_conv_w", "dn_a_log", "dn_dt_bias",
           "dn_o_norm_w", "fox_w_in", "fox_f_bias", "fox_q_norm_w", "fox_k_norm_w", "memq_norm_w", "w_out", "norm2_w",
           "w_mlp1", "w_mlp2"]
DN_IN = 4 * D_MODEL + 2 * N_HEADS + MEM_WIDTH
FOX_IN = 4 * D_MODEL + N_HEADS + MEM_WIDTH
IN_PAD = 640
SMALL = [("mem_norm_w", (D_MODEL,), D_MODEL), ("mem_k_norm_w", (HEAD_DIM,), HEAD_DIM), ("norm1_w", (2, D_MODEL), 2 * D_MODEL),
         ("dn_a_log", (1, N_HEADS), LANES), ("dn_dt_bias", (1, N_HEADS), LANES), ("dn_o_norm_w", (1, HEAD_DIM), HEAD_DIM),
         ("fox_f_bias", (1, N_HEADS), LANES), ("fox_q_norm_w", (1, HEAD_DIM), HEAD_DIM), ("fox_k_norm_w", (1, HEAD_DIM), HEAD_DIM),
         ("memq_norm_w", (2, HEAD_DIM), 2 * HEAD_DIM), ("norm2_w", (2, D_MODEL), 2 * D_MODEL)]
SMALL_ROWS = 16
CONV_ROWS = 16
SEG_ROWS = [("w_mem_kv", 256), ("dn_w_in", 1280), ("fox_w_in", 1280), ("w_out", 768), ("w_mlp1", 2048), ("w_mlp2", 2048)]
BIG_ROWS = sum(r for _, r in SEG_ROWS)
PACK_ROWS = BIG_ROWS + CONV_ROWS
PACK_TILE = 592


def _seg_offsets():
    off, out = 0, {}
    for n, r in SEG_ROWS:
        out[n] = (off, off + r)
        off += r
    return out


SEG = _seg_offsets()


def _flat_rows(a, rows):
    a = a.reshape(-1)
    return jnp.pad(a, (0, rows * PACK_W - a.shape[0])).reshape(rows, PACK_W)


def _pack_big(p):
    def in_w(a):
        a = a[0]
        return jnp.pad(a, ((0, 0), (0, IN_PAD - a.shape[1]))).reshape(-1, PACK_W)
    return [p["w_mem_kv"].reshape(-1, PACK_W), in_w(p["dn_w_in"]), in_w(p["fox_w_in"]), p["w_out"].reshape(-1, PACK_W),
            p["w_mlp1"].reshape(-1, PACK_W), p["w_mlp2"].reshape(-1, PACK_W)]


def _pack_small(p):
    small = jnp.concatenate([jnp.pad(p[n].reshape(-1), (0, ln - math.prod(sh))) for n, sh, ln in SMALL])
    return _flat_rows(small, SMALL_ROWS)


def pack_shard(p):
    return jnp.concatenate(_pack_big(p) + [_flat_rows(p["dn_conv_w"], CONV_ROWS)], axis=0), _pack_small(p)


def unpack_shard(pk, small):
    out = {}
    seg = {n: pk[a:b] for n, (a, b) in SEG.items()}
    out["w_mem_kv"] = seg["w_mem_kv"].reshape(D_MODEL // N_DEV, D_MODEL)
    out["dn_w_in"] = seg["dn_w_in"].reshape(D_MODEL, IN_PAD)[None, :, :DN_IN // N_DEV]
    out["fox_w_in"] = seg["fox_w_in"].reshape(D_MODEL, IN_PAD)[None, :, :FOX_IN // N_DEV]
    out["w_out"] = seg["w_out"].reshape(2, (D_MODEL + MEM_WIDTH) // N_DEV, D_MODEL)
    out["w_mlp1"] = seg["w_mlp1"].reshape(2, D_MODEL, D_FF // N_DEV)
    out["w_mlp2"] = seg["w_mlp2"].reshape(2, D_FF // N_DEV, D_MODEL)
    n_conv = CONV_WIDTH * 3 * D_MODEL // N_DEV
    out["dn_conv_w"] = pk[BIG_ROWS:].reshape(-1)[:n_conv].reshape(1, CONV_WIDTH, 3 * D_MODEL // N_DEV)
    small = small.reshape(-1)
    off = 0
    for n, sh, ln in SMALL:
        out[n] = small[off:off + math.prod(sh)].reshape(sh)
        off += ln
    return out


def pack_wire(p):
    cw = p["dn_conv_w"].reshape(-1)
    hi = cw.astype(BF16)
    lo = (cw - hi.astype(F32)).astype(BF16)
    big = [a.astype(BF16) for a in _pack_big(p)]
    return jnp.concatenate(big + [_flat_rows(jnp.concatenate([hi, lo]), CONV_ROWS)], axis=0)


def unpack_gathered(gw, p):
    w = {n: p[n] for n, _, _ in SMALL}
    seg = {n: gw[:, a:b] for n, (a, b) in SEG.items()}
    w["w_mem_kv"] = seg["w_mem_kv"].reshape(D_MODEL, D_MODEL)

    def in_w(s, width):
        return s.reshape(N_DEV, D_MODEL, IN_PAD)[:, :, :width // N_DEV].transpose(1, 0, 2).reshape(D_MODEL, width)

    dn = in_w(seg["dn_w_in"], DN_IN)
    fox = in_w(seg["fox_w_in"], FOX_IN)
    gate_end = 4 * D_MODEL
    w["dn_main"] = jnp.concatenate([dn[:, :gate_end], dn[:, gate_end + 2 * N_HEADS:]], axis=1)
    w["dn_ab"] = jnp.pad(dn[:, gate_end:gate_end + 2 * N_HEADS], ((0, 0), (0, LANES - 2 * N_HEADS)))
    w["fox_main"] = jnp.concatenate([fox[:, :gate_end], fox[:, gate_end + N_HEADS:]], axis=1)
    w["fox_f"] = jnp.pad(fox[:, gate_end:gate_end + N_HEADS], ((0, 0), (0, LANES - N_HEADS)))
    w["w_out"] = seg["w_out"].reshape(N_DEV, 2, -1, D_MODEL).transpose(1, 0, 2, 3).reshape(2, D_MODEL + MEM_WIDTH, D_MODEL)
    w["w_mlp1"] = seg["w_mlp1"].reshape(N_DEV, 2, D_MODEL, -1).transpose(1, 2, 0, 3).reshape(2, D_MODEL, D_FF)
    w["w_mlp2"] = seg["w_mlp2"].reshape(N_DEV, 2, -1, D_MODEL).transpose(1, 0, 2, 3).reshape(2, D_FF, D_MODEL)
    n_conv = CONV_WIDTH * 3 * D_MODEL // N_DEV
    cw = gw[:, BIG_ROWS:].reshape(N_DEV, -1)[:, :2 * n_conv].astype(F32)
    cw = cw[:, :n_conv] + cw[:, n_conv:]
    w["dn_conv_w"] = cw.reshape(N_DEV, CONV_WIDTH, -1).transpose(1, 0, 2).reshape(CONV_WIDTH, 3 * D_MODEL)
    return w


def pack_grads(g):
    gate_end = 4 * D_MODEL

    def in_w(main, small, n_small, width):
        full = jnp.concatenate([main[:, :gate_end], small[:, :n_small], main[:, gate_end:]], axis=1)
        s = full.reshape(D_MODEL, N_DEV, width // N_DEV).transpose(1, 0, 2)
        return jnp.pad(s, ((0, 0), (0, 0), (0, IN_PAD - width // N_DEV))).reshape(N_DEV, -1, PACK_W)

    parts = [
        g["w_mem_kv"].reshape(N_DEV, -1, PACK_W),
        in_w(g["dn_main"], g["dn_ab"], 2 * N_HEADS, DN_IN),
        in_w(g["fox_main"], g["fox_f"], N_HEADS, FOX_IN),
        g["w_out"].reshape(2, N_DEV, -1, D_MODEL).transpose(1, 0, 2, 3).reshape(N_DEV, -1, PACK_W),
        g["w_mlp1"].reshape(2, D_MODEL, N_DEV, -1).transpose(2, 0, 1, 3).reshape(N_DEV, -1, PACK_W),
        g["w_mlp2"].reshape(2, N_DEV, -1, D_MODEL).transpose(1, 0, 2, 3).reshape(N_DEV, -1, PACK_W),
        jnp.stack([_flat_rows(c, CONV_ROWS) for c in g["dn_conv_w"].reshape(CONV_WIDTH, N_DEV, -1).transpose(1, 0, 2)]),
    ]
    return jnp.concatenate([a.astype(BF16) for a in parts], axis=1), _pack_small(g)


_HBM = pl.BlockSpec(memory_space=pltpu.HBM)


def _place():
    return lax.axis_index("x"), lax.axis_index("y"), lax.axis_index("c")


def all_gather(xs, *, name):
    def body(x_ref, out_ref, send_sems, recv_sems, local_sem):
        x, y, c = _place()
        me, sibling = (x, y, c), (x, y, 1 - c)
        chips = [(1 - x, y), (x, 1 - y), (1 - x, 1 - y)]

        def slot(px, py, pc):
            return out_ref.at[4 * px + 2 * py + pc]

        def copy(k, block, to, src=None):
            return pltpu.make_async_remote_copy(
                src_ref=slot(*block) if src is None else src, dst_ref=slot(*block),
                send_sem=send_sems.at[k], recv_sem=recv_sems.at[k], device_id=to, device_id_type=MESH)

        mine = pltpu.make_async_copy(x_ref, slot(*me), local_sem)
        mine.start()
        first = [copy(0, me, sibling, src=x_ref)]
        first += [copy(1 + j, me, (*chip, c), src=x_ref) for j, chip in enumerate(chips)]
        for cp in first:
            cp.start()
        passed = [copy(4 + j, (*chip, c), sibling) for j, chip in enumerate(chips)]
        for j, chip in enumerate(chips):
            copy(1 + j, (*chip, c), me).wait_recv()
            passed[j].start()
        copy(0, sibling, me).wait_recv()
        for j, chip in enumerate(chips):
            copy(4 + j, (*chip, 1 - c), me).wait_recv()
        for cp in first + passed:
            cp.wait_send()
        mine.wait()

    return pl.pallas_call(
        body, name=name, out_shape=jax.ShapeDtypeStruct((N_DEV,) + xs.shape, xs.dtype), in_specs=[_HBM], out_specs=_HBM,
        scratch_shapes=[pltpu.SemaphoreType.DMA((7,)), pltpu.SemaphoreType.DMA((7,)), pltpu.SemaphoreType.DMA],
    )(xs)


def sibling_exchange(g, *, name):
    def body(g_ref, out_ref, send_sems, recv_sems):
        x, y, c = _place()
        copies = [pltpu.make_async_remote_copy(
            src_ref=g_ref.at[2 * k + 1 - c], dst_ref=out_ref.at[k], send_sem=send_sems.at[k], recv_sem=recv_sems.at[k],
            device_id=(x, y, 1 - c), device_id_type=MESH) for k in range(4)]
        for cp in copies:
            cp.start()
        for cp in copies:
            cp.wait_recv()
        for cp in copies:
            cp.wait_send()

    return pl.pallas_call(
        body, name=name, out_shape=jax.ShapeDtypeStruct((4,) + g.shape[1:], g.dtype), in_specs=[_HBM], out_specs=_HBM,
        scratch_shapes=[pltpu.SemaphoreType.DMA((4,)), pltpu.SemaphoreType.DMA((4,))],
    )(g)


def pair_sum(g, got, *, name):
    rows = g.shape[1]
    c = lax.axis_index("c").astype(jnp.int32).reshape(1)

    def body(c_ref, a_ref, b_ref, o_ref):
        o_ref[...] = (a_ref[...].astype(F32) + b_ref[...].astype(F32)).astype(o_ref.dtype)

    grid_spec = pltpu.PrefetchScalarGridSpec(
        num_scalar_prefetch=1, grid=(4, rows // PACK_TILE),
        in_specs=[pl.BlockSpec((None, PACK_TILE, PACK_W), lambda k, i, c_ref: (2 * k + c_ref[0], i, 0)),
                  pl.BlockSpec((None, PACK_TILE, PACK_W), lambda k, i, c_ref: (k, i, 0))],
        out_specs=pl.BlockSpec((None, PACK_TILE, PACK_W), lambda k, i, c_ref: (k, i, 0)))
    return pl.pallas_call(
        body, name=name, grid_spec=grid_spec, out_shape=jax.ShapeDtypeStruct((4,) + g.shape[1:], g.dtype),
        compiler_params=_params(("parallel", "parallel")),
    )(c, g, got)


def chip_exchange(h, *, name):
    def body(h_ref, out_ref, send_sems, recv_sems, local_sem):
        x, y, c = _place()
        mine = 2 * x + y
        chips = [(1 - x, y), (x, 1 - y), (1 - x, 1 - y)]
        keep = pltpu.make_async_copy(h_ref.at[mine], out_ref.at[mine], local_sem)
        keep.start()
        sends = [pltpu.make_async_remote_copy(
            src_ref=h_ref.at[2 * qx + qy], dst_ref=out_ref.at[mine], send_sem=send_sems.at[j], recv_sem=recv_sems.at[j],
            device_id=(qx, qy, c), device_id_type=MESH) for j, (qx, qy) in enumerate(chips)]
        for cp in sends:
            cp.start()
        for j, (qx, qy) in enumerate(chips):
            pltpu.make_async_remote_copy(
                src_ref=h_ref.at[mine], dst_ref=out_ref.at[2 * qx + qy], send_sem=send_sems.at[j], recv_sem=recv_sems.at[j],
                device_id=(qx, qy, c), device_id_type=MESH).wait_recv()
        for cp in sends:
            cp.wait_send()
        keep.wait()

    return pl.pallas_call(
        body, name=name, out_shape=jax.ShapeDtypeStruct(h.shape, h.dtype), in_specs=[_HBM], out_specs=_HBM,
        scratch_shapes=[pltpu.SemaphoreType.DMA((3,)), pltpu.SemaphoreType.DMA((3,)), pltpu.SemaphoreType.DMA],
    )(h)


def adamw(parts, w, m, v, *, tile, name):
    n, rows, _ = parts.shape

    def body(p_ref, w_ref, m_ref, v_ref, g_ref, d_ref, mo_ref, vo_ref):
        g = p_ref[0].astype(F32)
        for i in range(1, n):
            g = g + p_ref[i].astype(F32)
        m_new = ADAM_B1 * m_ref[...] + (1.0 - ADAM_B1) * g
        v_new = ADAM_B2 * v_ref[...] + (1.0 - ADAM_B2) * jnp.square(g)
        m_hat = m_new / (1.0 - ADAM_B1 ** ADAM_STEP)
        v_hat = v_new / (1.0 - ADAM_B2 ** ADAM_STEP)
        g_ref[...] = g
        d_ref[...] = -ADAM_LR * (m_hat / (jnp.sqrt(v_hat) + ADAM_EPS) + ADAM_WD * w_ref[...])
        mo_ref[...] = m_new
        vo_ref[...] = v_new

    spec = pl.BlockSpec((tile, PACK_W), lambda i: (i, 0))
    return pl.pallas_call(
        body, name=name, grid=(rows // tile,),
        in_specs=[pl.BlockSpec((n, tile, PACK_W), lambda i: (0, i, 0)), spec, spec, spec], out_specs=[spec] * 4,
        out_shape=[jax.ShapeDtypeStruct((rows, PACK_W), F32)] * 4, compiler_params=_params(("parallel",)),
    )(parts, w, m, v)


def kernel(x, mem, mem_norm_w, w_mem_kv, mem_k_norm_w, norm1_w, dn_w_in, dn_conv_w, dn_a_log, dn_dt_bias, dn_o_norm_w, fox_w_in, fox_f_bias, fox_q_norm_w, fox_k_norm_w, memq_norm_w, w_out, norm2_w, w_mlp1, w_mlp2, loss_target, m_mem_norm_w, m_w_mem_kv, m_mem_k_norm_w, m_norm1_w, m_dn_w_in, m_dn_conv_w, m_dn_a_log, m_dn_dt_bias, m_dn_o_norm_w, m_fox_w_in, m_fox_f_bias, m_fox_q_norm_w, m_fox_k_norm_w, m_memq_norm_w, m_w_out, m_norm2_w, m_w_mlp1, m_w_mlp2, v_mem_norm_w, v_w_mem_kv, v_mem_k_norm_w, v_norm1_w, v_dn_w_in, v_dn_conv_w, v_dn_a_log, v_dn_dt_bias, v_dn_o_norm_w, v_fox_w_in, v_fox_f_bias, v_fox_q_norm_w, v_fox_k_norm_w, v_memq_norm_w, v_w_out, v_norm2_w, v_w_mlp1, v_w_mlp2):
    p = dict(mem_norm_w=mem_norm_w, w_mem_kv=w_mem_kv, mem_k_norm_w=mem_k_norm_w, norm1_w=norm1_w, dn_w_in=dn_w_in,
             dn_conv_w=dn_conv_w, dn_a_log=dn_a_log, dn_dt_bias=dn_dt_bias, dn_o_norm_w=dn_o_norm_w, fox_w_in=fox_w_in,
             fox_f_bias=fox_f_bias, fox_q_norm_w=fox_q_norm_w, fox_k_norm_w=fox_k_norm_w, memq_norm_w=memq_norm_w,
             w_out=w_out, norm2_w=norm2_w, w_mlp1=w_mlp1, w_mlp2=w_mlp2)
    pm = dict(mem_norm_w=m_mem_norm_w, w_mem_kv=m_w_mem_kv, mem_k_norm_w=m_mem_k_norm_w, norm1_w=m_norm1_w,
              dn_w_in=m_dn_w_in, dn_conv_w=m_dn_conv_w, dn_a_log=m_dn_a_log, dn_dt_bias=m_dn_dt_bias,
              dn_o_norm_w=m_dn_o_norm_w, fox_w_in=m_fox_w_in, fox_f_bias=m_fox_f_bias, fox_q_norm_w=m_fox_q_norm_w,
              fox_k_norm_w=m_fox_k_norm_w, memq_norm_w=m_memq_norm_w, w_out=m_w_out, norm2_w=m_norm2_w, w_mlp1=m_w_mlp1,
              w_mlp2=m_w_mlp2)
    pv = dict(mem_norm_w=v_mem_norm_w, w_mem_kv=v_w_mem_kv, mem_k_norm_w=v_mem_k_norm_w, norm1_w=v_norm1_w,
              dn_w_in=v_dn_w_in, dn_conv_w=v_dn_conv_w, dn_a_log=v_dn_a_log, dn_dt_bias=v_dn_dt_bias,
              dn_o_norm_w=v_dn_o_norm_w, fox_w_in=v_fox_w_in, fox_f_bias=v_fox_f_bias, fox_q_norm_w=v_fox_q_norm_w,
              fox_k_norm_w=v_fox_k_norm_w, memq_norm_w=v_memq_norm_w, w_out=v_w_out, norm2_w=v_norm2_w, w_mlp1=v_w_mlp1,
              w_mlp2=v_w_mlp2)

    gathered = all_gather(pack_wire(p), name="weights_all_gather")
    loss, grad_x, g = local_step(x[0], mem[0], loss_target[0], unpack_gathered(gathered, p))
    loss = lax.psum(loss, ("x", "y", "c"))

    gp, gsmall = pack_grads(g)
    chip_sums = pair_sum(gp, sibling_exchange(gp, name="grads_to_sibling"), name="grads_pair_sum")
    parts = chip_exchange(chip_sums, name="grads_to_chips")
    small_parts = all_gather(gsmall, name="small_grads_all_gather")
    (w_big, w_small), (m_big, m_small), (v_big, v_small) = pack_shard(p), pack_shard(pm), pack_shard(pv)
    big = adamw(parts, w_big, m_big, v_big, tile=PACK_TILE, name="adamw")
    small = adamw(small_parts, w_small, m_small, v_small, tile=SMALL_ROWS, name="adamw_small")
    groups = [unpack_shard(b, sm) for b, sm in zip(big, small)]
    return (loss, grad_x[None], *[grp[n] for grp in groups for n in WEIGHTS])
```

```python
import functools
import math

import jax
import jax.numpy as jnp
from jax import lax
from jax.experimental import pallas as pl
from jax.experimental.pallas import tpu as pltpu

F32 = jnp.float32
BF16 = jnp.bfloat16
HIGHEST = lax.Precision.HIGHEST

D_MODEL = 1024
HEAD_DIM = 128
N_HEADS = 8
MEM_HEADS = 4
MEM_WIDTH = MEM_HEADS * HEAD_DIM
D_FF = 4 * D_MODEL
CONV_WIDTH = 4
CHUNK = 64
Q_BLOCK = 128
EPS = 1e-6
SCALE = HEAD_DIM ** -0.5
MAIN_WIDTH = 4 * D_MODEL + MEM_WIDTH
LANES = 128
N_DEV = 8
PACK_W = 512

ADAM_LR = 0.001
ADAM_B1 = 0.9
ADAM_B2 = 0.999
ADAM_EPS = 1e-08
ADAM_WD = 0.01
ADAM_STEP = 10

VMEM_LIMIT = 56 * 2 ** 20
MESH = pl.DeviceIdType.MESH


def _bdot(a, b, dims):
    return lax.dot_general(a.astype(BF16), b.astype(BF16), (dims, ((), ())), preferred_element_type=F32)


@jax.custom_vjp
def mm(a, b):
    return _bdot(a, b, ((1,), (0,)))


@jax.custom_vjp
def mm_nt(a, b):
    return _bdot(a, b, ((1,), (1,)))


@jax.custom_vjp
def mm_tn(a, b):
    return _bdot(a, b, ((0,), (0,)))


mm.defvjp(lambda a, b: (mm(a, b), (a, b)), lambda r, g: (mm_nt(g, r[1]), mm_tn(r[0], g)))
mm_nt.defvjp(lambda a, b: (mm_nt(a, b), (a, b)), lambda r, g: (mm(g, r[1]), mm_tn(g, r[0])))
mm_tn.defvjp(lambda a, b: (mm_tn(a, b), (a, b)), lambda r, g: (mm_nt(r[1], g), mm(r[0], g)))


def hdot(a, b):
    return jnp.dot(a, b, precision=HIGHEST, preferred_element_type=F32)


def rms(x, w):
    return x * lax.rsqrt(jnp.mean(x * x, axis=-1, keepdims=True) + EPS) * w


def l2n(x):
    return x * lax.rsqrt(jnp.sum(x * x, axis=-1, keepdims=True) + EPS)


def _iota2(n, m):
    return lax.broadcasted_iota(jnp.int32, (n, m), 0), lax.broadcasted_iota(jnp.int32, (n, m), 1)


def _lower_ones(n):
    r, c = _iota2(n, n)
    return jnp.where(r >= c, 1.0, 0.0).astype(F32)


def _last_row(x):
    r = lax.broadcasted_iota(jnp.int32, x.shape, 0)
    return jnp.sum(jnp.where(r == x.shape[0] - 1, x, 0.0), axis=0, keepdims=True)


def _softmax_rows(z):
    m = lax.stop_gradient(jnp.max(z, axis=-1, keepdims=True))
    e = jnp.exp(z - m)
    return e / jnp.sum(e, axis=-1, keepdims=True)


_BNN = (((2,), (1,)), ((0,), (0,)))
_BNT = (((2,), (2,)), ((0,), (0,)))
_BTN = (((1,), (1,)), ((0,), (0,)))


def _bbdot(a, b, dims):
    return lax.dot_general(a.astype(BF16), b.astype(BF16), dims, preferred_element_type=F32)


@jax.custom_vjp
def bmm(a, b):
    return _bbdot(a, b, _BNN)


@jax.custom_vjp
def bmm_nt(a, b):
    return _bbdot(a, b, _BNT)


@jax.custom_vjp
def bmm_tn(a, b):
    return _bbdot(a, b, _BTN)


@jax.custom_vjp
def bmm_high(a, b):
    return lax.dot_general(a, b, _BNN, precision=lax.Precision.HIGH, preferred_element_type=F32)


bmm.defvjp(lambda a, b: (bmm(a, b), (a, b)), lambda r, g: (bmm_nt(g, r[1]), bmm_tn(r[0], g)))
bmm_nt.defvjp(lambda a, b: (bmm_nt(a, b), (a, b)), lambda r, g: (bmm(g, r[1]), bmm_tn(g, r[0])))
bmm_tn.defvjp(lambda a, b: (bmm_tn(a, b), (a, b)), lambda r, g: (bmm_nt(r[1], g), bmm(r[0], g)))
bmm_high.defvjp(lambda a, b: (bmm_high(a, b), (a, b)), lambda r, g: (bmm_nt(g, r[1]), bmm_tn(r[0], g)))

NEUMANN_HIGH_LEVELS = 2


def inv_unit_lower(a):
    n = a.shape[-1]
    r, c = _iota2(n, n)
    p = jnp.where(r == c, 1.0, 0.0).astype(F32) - a
    ak = a
    for level in range(int(math.log2(n)) - 1):
        dot = bmm_high if level < NEUMANN_HIGH_LEVELS else bmm
        ak = dot(ak, ak)
        p = p + dot(p, ak)
    return p


def delta_intra(q, k, v, gc, beta):
    b, c, _ = q.shape
    r, cc = _iota2(c, c)
    causal = r >= cc
    strict = r > cc
    gi = jnp.broadcast_to(gc, (b, c, c))
    gj = jnp.swapaxes(gi, 1, 2)
    decay = jnp.where(causal, jnp.exp(jnp.where(causal, gi - gj, 0.0)), 0.0)
    kb = k * beta
    a = jnp.where(strict, bmm_nt(kb, k) * decay, 0.0)
    t = inv_unit_lower(a)
    u = bmm(t, v * beta)
    w = bmm(t, kb * jnp.exp(gc))
    qk = jnp.where(causal, bmm_nt(q, k) * decay, 0.0)
    return u, w, qk


def delta_step(s, q, k, gc, u, w, qk):
    v_new = u - bmm(w, s)
    out = bmm(q * jnp.exp(gc), s) + bmm(qk, v_new)
    r = lax.broadcasted_iota(jnp.int32, gc.shape, 1)
    g_last = jnp.sum(jnp.where(r == gc.shape[1] - 1, gc, 0.0), axis=1, keepdims=True)
    k_dec = k * jnp.exp(g_last - gc)
    s_new = s * jnp.exp(g_last) + bmm_tn(k_dec, v_new)
    return out, s_new


def fox_block(q, k, v, fq, fk, qpos0):
    s = mm_nt(q, k)
    r, c = _iota2(s.shape[0], s.shape[1])
    mask = c <= (r + qpos0)
    p = _softmax_rows(jnp.where(mask, s + (fq - fk), -jnp.inf))
    return mm(p, v)


def mem_head(qm, wq, mk, mv):
    p = _softmax_rows(mm_nt(rms(qm, wq) * SCALE, mk))
    return mm(p, mv)


def _heads(x, n):
    return [x[:, h * HEAD_DIM:(h + 1) * HEAD_DIM] for h in range(n)]


def memkv_fn(mem, mnw, wkv, mknw):
    kv = mm(rms(mem, mnw), wkv)
    mk = jnp.concatenate([rms(kh, mknw) for kh in _heads(kv[:, :MEM_WIDTH], MEM_HEADS)], axis=1)
    return mk, kv[:, MEM_WIDTH:]


def dn_gates_fn(ab, alog, dtb):
    g = -jnp.exp(alog) * jax.nn.softplus(ab + dtb)
    low = _lower_ones(CHUNK)
    gc = jnp.concatenate([hdot(low, g[i * CHUNK:(i + 1) * CHUNK]) for i in range(ab.shape[0] // CHUNK)], axis=0)
    lane = lax.broadcasted_iota(jnp.int32, ab.shape, 1)
    return jnp.where(lane < N_HEADS, gc, jax.nn.sigmoid(ab))


def fox_fcum_fn(fp, fbias):
    lf = jax.nn.log_sigmoid(fp + fbias)
    low = _lower_ones(LANES)
    carry = jnp.zeros((1, fp.shape[1]), F32)
    outs = []
    for i in range(fp.shape[0] // LANES):
        cs = hdot(low, lf[i * LANES:(i + 1) * LANES]) + carry
        carry = _last_row(cs)
        outs.append(cs)
    return jnp.concatenate(outs, axis=0)


def fox_qk_fn(qraw, kraw, qnw, knw):
    q = jnp.concatenate([rms(x, qnw) * SCALE for x in _heads(qraw, N_HEADS)], axis=1)
    k = jnp.concatenate([rms(x, knw) for x in _heads(kraw, N_HEADS)], axis=1)
    return q, k


def _mem_out(qm, mqw, mk, mv):
    return [mem_head(a, mqw, b, c) for a, b, c in zip(_heads(qm, MEM_HEADS), _heads(mk, MEM_HEADS), _heads(mv, MEM_HEADS))]


def dn_out_fn(o, z, qm, onw, mqw, mk, mv):
    mix = [rms(a, onw) * jax.nn.silu(b) for a, b in zip(o, _heads(z, N_HEADS))]
    return jnp.concatenate(mix + _mem_out(qm, mqw, mk, mv), axis=1)


def fox_out_fn(o, gate, qm, mqw, mk, mv):
    return jnp.concatenate([o * jax.nn.sigmoid(gate)] + _mem_out(qm, mqw, mk, mv), axis=1)


def _pick(n, cands):
    for c in cands:
        if n % c == 0:
            return c
    return n


def _params(sem):
    return pltpu.CompilerParams(dimension_semantics=sem, vmem_limit_bytes=VMEM_LIMIT)


def matmul(a, b, *, name, ta=False, tb=False, add=None, out_dtype=F32):
    (k, m) = a.shape if ta else a.shape[::-1]
    (kb, n) = b.shape[::-1] if tb else b.shape
    assert k == kb, (a.shape, b.shape, ta, tb)
    tm = _pick(m, (512, 256, 128))
    tn = _pick(n, (512, 256, 128))
    tk = _pick(k, (1024, 512, 256, 128))
    nk = k // tk
    dims = ((0,) if ta else (1,), (1,) if tb else (0,))

    def body(*refs):
        if add is None:
            a_ref, b_ref, o_ref, acc = refs
        else:
            a_ref, b_ref, add_ref, o_ref, acc = refs
        kk = pl.program_id(2)

        @pl.when(kk == 0)
        def _():
            acc[...] = jnp.zeros_like(acc)

        acc[...] += _bdot(a_ref[...], b_ref[...], dims)

        @pl.when(kk == nk - 1)
        def _():
            r = acc[...]
            if add is not None:
                r = r + add_ref[...]
            o_ref[...] = r.astype(out_dtype)

    a_spec = pl.BlockSpec((tk, tm), lambda i, j, kk: (kk, i)) if ta else pl.BlockSpec((tm, tk), lambda i, j, kk: (i, kk))
    b_spec = pl.BlockSpec((tn, tk), lambda i, j, kk: (j, kk)) if tb else pl.BlockSpec((tk, tn), lambda i, j, kk: (kk, j))
    o_spec = pl.BlockSpec((tm, tn), lambda i, j, kk: (i, j))
    ins, specs = [a, b], [a_spec, b_spec]
    if add is not None:
        ins.append(add)
        specs.append(o_spec)
    return pl.pallas_call(
        body, name=name, grid=(m // tm, n // tn, nk), in_specs=specs, out_specs=o_spec,
        out_shape=jax.ShapeDtypeStruct((m, n), out_dtype),
        scratch_shapes=[pltpu.VMEM((tm, tn), F32)],
        compiler_params=_params(("parallel", "parallel", "arbitrary")),
    )(*ins)


def rows_call(fn, row_ins, full_ins, row_outs, acc_outs, *, tm, name):
    row_ins = [r if isinstance(r, tuple) else (r, r.shape[-1], 0) for r in row_ins]
    t = row_ins[0][0].shape[-2]
    tm = min(tm, t)
    n_in = len(row_ins) + len(full_ins)
    n_row = len(row_outs)

    def body(*refs):
        res = fn(*[[r[h] for h in range(r.shape[0])] if (i < len(row_ins) and len(r.shape) == 3) else r[...]
                   for i, r in enumerate(refs[:n_in])])
        res = res if isinstance(res, (tuple, list)) else (res,)
        outs = refs[n_in:]
        for ref, val in zip(outs[:n_row], res[:n_row]):
            if len(ref.shape) == 3:
                for h, vh in enumerate(val):
                    ref[h] = vh.astype(ref.dtype)
            else:
                ref[...] = val.astype(ref.dtype)
        first = pl.program_id(0) == 0
        for ref, val in zip(outs[n_row:], res[n_row:]):
            @pl.when(first)
            def _(ref=ref, val=val):
                ref[...] = val

            @pl.when(jnp.logical_not(first))
            def _(ref=ref, val=val):
                ref[...] += val

    def full_spec(shape):
        return pl.BlockSpec(shape, lambda i, nd=len(shape): (0,) * nd)

    def row_spec(lead, w, cb):
        if lead is None:
            return pl.BlockSpec((tm, w), lambda i: (i, cb))
        return pl.BlockSpec((lead, tm, w), lambda i: (0, i, cb))

    def lead_cols(c):
        return c if isinstance(c, tuple) else (None, c)

    in_specs = [row_spec(a.shape[0] if a.ndim == 3 else None, w, cb) for (a, w, cb) in row_ins]
    in_specs += [full_spec(f.shape) for f in full_ins]
    out_specs = [row_spec(*lead_cols(c), 0) for c, _ in row_outs] + [full_spec(s) for s in acc_outs]
    out_shape = [jax.ShapeDtypeStruct(tuple(d for d in (lead_cols(c)[0], t, lead_cols(c)[1]) if d is not None), dt)
                 for c, dt in row_outs] + [jax.ShapeDtypeStruct(s, F32) for s in acc_outs]
    res = pl.pallas_call(
        body, name=name, grid=(t // tm,), in_specs=in_specs, out_specs=out_specs, out_shape=out_shape,
        compiler_params=_params(("arbitrary",)),
    )(*[r[0] for r in row_ins], *full_ins)
    return res


def vjp_rows(fn, n_diff_row, row_diff_full):
    def bwd(*args, n_row, n_ct):
        prim_rows = args[:n_row]
        cts = args[n_row:n_row + n_ct]
        fulls = args[n_row + n_ct:]
        _, vjp = jax.vjp(fn, *prim_rows, *fulls)
        g = vjp(cts[0] if n_ct == 1 else tuple(cts))
        out = list(g[:n_diff_row])
        out += [gf for gf, d in zip(g[n_row:], row_diff_full) if d]
        return tuple(out)
    return bwd


def _shift_down(x, s):
    if s == 0:
        return x
    t = lax.broadcasted_iota(jnp.int32, x.shape, 0)
    return jnp.where(t >= s, pltpu.roll(x, s, 0), 0.0)


def _shift_up(x, s):
    if s == 0:
        return x
    n = x.shape[0]
    t = lax.broadcasted_iota(jnp.int32, x.shape, 0)
    return jnp.where(t < n - s, pltpu.roll(x, n - s, 0), 0.0)


def _conv(x, w_ref):
    return sum(w_ref[pl.ds(j, 1), :] * _shift_down(x, CONV_WIDTH - 1 - j) for j in range(CONV_WIDTH))


_DN_POST = (lambda c: l2n(jax.nn.silu(c)) * SCALE, lambda c: l2n(jax.nn.silu(c)), jax.nn.silu)


def dn_prep_fwd(proj, conv_w, *, name):
    t = proj.shape[0]

    def body(xq, xk, xv, wq, wk, wv, oq, ok, ov):
        for x_ref, w_ref, o_ref, post in zip((xq, xk, xv), (wq, wk, wv), (oq, ok, ov), _DN_POST):
            o_ref[...] = post(_conv(x_ref[...], w_ref))

    x_specs = [pl.BlockSpec((t, HEAD_DIM), lambda h, g=g: (0, g * N_HEADS + h)) for g in range(3)]
    w_specs = [pl.BlockSpec((CONV_WIDTH, HEAD_DIM), lambda h, g=g: (0, g * N_HEADS + h)) for g in range(3)]
    o_spec = pl.BlockSpec((None, t, HEAD_DIM), lambda h: (h, 0, 0))
    return pl.pallas_call(
        body, name=name, grid=(N_HEADS,), in_specs=x_specs + w_specs, out_specs=[o_spec] * 3,
        out_shape=[jax.ShapeDtypeStruct((N_HEADS, t, HEAD_DIM), F32)] * 3, compiler_params=_params(("parallel",)),
    )(proj, proj, proj, conv_w, conv_w, conv_w)


def dn_prep_bwd(proj, conv_w, dq, dk, dv, *, name):
    t = proj.shape[0]

    def body(xq, xk, xv, wq, wk, wv, gq, gk, gv, dxq, dxk, dxv, dwq, dwk, dwv):
        for x_ref, w_ref, g_ref, dx_ref, dw_ref, post in zip(
                (xq, xk, xv), (wq, wk, wv), (gq, gk, gv), (dxq, dxk, dxv), (dwq, dwk, dwv), _DN_POST):
            x = x_ref[...]
            _, vjp = jax.vjp(post, _conv(x, w_ref))
            dc, = vjp(g_ref[...])
            dx_ref[...] = sum(w_ref[pl.ds(j, 1), :] * _shift_up(dc, CONV_WIDTH - 1 - j) for j in range(CONV_WIDTH))
            for j in range(CONV_WIDTH):
                dw_ref[pl.ds(j, 1), :] = jnp.sum(dc * _shift_down(x, CONV_WIDTH - 1 - j), axis=0, keepdims=True)

    x_specs = [pl.BlockSpec((t, HEAD_DIM), lambda h, g=g: (0, g * N_HEADS + h)) for g in range(3)]
    w_specs = [pl.BlockSpec((CONV_WIDTH, HEAD_DIM), lambda h, g=g: (0, g * N_HEADS + h)) for g in range(3)]
    g_spec = pl.BlockSpec((None, t, HEAD_DIM), lambda h: (h, 0, 0))
    dx_spec = pl.BlockSpec((t, HEAD_DIM), lambda h: (0, h))
    dw_spec = pl.BlockSpec((CONV_WIDTH, HEAD_DIM), lambda h: (0, h))
    return pl.pallas_call(
        body, name=name, grid=(N_HEADS,), in_specs=x_specs + w_specs + [g_spec] * 3, out_specs=[dx_spec] * 3 + [dw_spec] * 3,
        out_shape=[jax.ShapeDtypeStruct((t, D_MODEL), F32)] * 3 + [jax.ShapeDtypeStruct((CONV_WIDTH, D_MODEL), F32)] * 3,
        compiler_params=_params(("parallel",)),
    )(proj, proj, proj, conv_w, conv_w, conv_w, dq, dk, dv)


INTRA_CHUNKS = 4


def _delta_specs(t):
    def spec(rows, w, index):
        return pl.BlockSpec((N_HEADS, rows, w), lambda i: (0, index(i), 0))
    return spec


def delta_intra_fwd(q, k, v, gc, beta, *, name):
    t = q.shape[1]
    per = min(INTRA_CHUNKS, t // CHUNK)
    rows, nb = per * CHUNK, N_HEADS * per

    def body(q_ref, k_ref, v_ref, g_ref, b_ref, u_ref, w_ref, qk_ref):
        ins = [r[...].reshape(nb, CHUNK, r.shape[-1]) for r in (q_ref, k_ref, v_ref, g_ref, b_ref)]
        for ref, val in zip((u_ref, w_ref, qk_ref), delta_intra(*ins)):
            ref[...] = val.reshape(ref.shape)

    spec = _delta_specs(t)
    x_spec, g_spec, qk_spec = (spec(rows, w, lambda i: i) for w in (HEAD_DIM, 1, CHUNK))
    return pl.pallas_call(
        body, name=name, grid=(t // rows,), in_specs=[x_spec] * 3 + [g_spec] * 2, out_specs=[x_spec, x_spec, qk_spec],
        out_shape=[jax.ShapeDtypeStruct((N_HEADS, t, HEAD_DIM), F32)] * 2 + [jax.ShapeDtypeStruct((N_HEADS, t, CHUNK), F32)],
        compiler_params=_params(("parallel",)),
    )(q, k, v, gc, beta)


def delta_seq_fwd(q, k, gc, u, w, qk, *, name):
    t = q.shape[1]
    nc = t // CHUNK

    def body(q_ref, k_ref, g_ref, u_ref, w_ref, qk_ref, o_ref, s0_ref, s_ref):
        @pl.when(pl.program_id(0) == 0)
        def _():
            s_ref[...] = jnp.zeros_like(s_ref)

        s = s_ref[...]
        s0_ref[...] = s
        o, s_new = delta_step(s, q_ref[...], k_ref[...], g_ref[...], u_ref[...], w_ref[...], qk_ref[...])
        o_ref[...] = o
        s_ref[...] = s_new

    spec = _delta_specs(t)
    x_spec, g_spec, qk_spec = (spec(CHUNK, w, lambda c: c) for w in (HEAD_DIM, 1, CHUNK))
    s_spec = pl.BlockSpec((N_HEADS, None, HEAD_DIM, HEAD_DIM), lambda c: (0, c, 0, 0))
    return pl.pallas_call(
        body, name=name, grid=(nc,), in_specs=[x_spec, x_spec, g_spec, x_spec, x_spec, qk_spec], out_specs=[x_spec, s_spec],
        out_shape=[jax.ShapeDtypeStruct((N_HEADS, t, HEAD_DIM), F32),
                   jax.ShapeDtypeStruct((N_HEADS, nc, HEAD_DIM, HEAD_DIM), F32)],
        scratch_shapes=[pltpu.VMEM((N_HEADS, HEAD_DIM, HEAD_DIM), F32)],
        compiler_params=_params(("arbitrary",)),
    )(q, k, gc, u, w, qk)


def delta_seq_bwd(q, k, gc, u, w, qk, s0, do, *, name):
    t = q.shape[1]
    nc = t // CHUNK

    def body(q_ref, k_ref, g_ref, u_ref, w_ref, qk_ref, s0_ref, do_ref,
             dq_ref, dk_ref, dg_ref, du_ref, dw_ref, dqk_ref, ds_ref):
        @pl.when(pl.program_id(0) == 0)
        def _():
            ds_ref[...] = jnp.zeros_like(ds_ref)

        _, vjp = jax.vjp(delta_step, s0_ref[...], q_ref[...], k_ref[...], g_ref[...], u_ref[...], w_ref[...], qk_ref[...])
        ds, dq, dk, dg, du, dw, dqk = vjp((do_ref[...], ds_ref[...]))
        for ref, val in zip((ds_ref, dq_ref, dk_ref, dg_ref, du_ref, dw_ref, dqk_ref), (ds, dq, dk, dg, du, dw, dqk)):
            ref[...] = val

    spec = _delta_specs(t)
    x_spec, g_spec, qk_spec = (spec(CHUNK, w, lambda c: nc - 1 - c) for w in (HEAD_DIM, 1, CHUNK))
    s_spec = pl.BlockSpec((N_HEADS, None, HEAD_DIM, HEAD_DIM), lambda c: (0, nc - 1 - c, 0, 0))
    return pl.pallas_call(
        body, name=name, grid=(nc,), in_specs=[x_spec, x_spec, g_spec, x_spec, x_spec, qk_spec, s_spec, x_spec],
        out_specs=[x_spec, x_spec, g_spec, x_spec, x_spec, qk_spec],
        out_shape=[jax.ShapeDtypeStruct((N_HEADS, t, w_), F32) for w_ in (HEAD_DIM, HEAD_DIM, 1, HEAD_DIM, HEAD_DIM, CHUNK)],
        scratch_shapes=[pltpu.VMEM((N_HEADS, HEAD_DIM, HEAD_DIM), F32)],
        compiler_params=_params(("arbitrary",)),
    )(q, k, gc, u, w, qk, s0, do)


def delta_intra_bwd(q, k, v, gc, beta, du, dw, dqk, dq_s, dk_s, dg_s, *, name):
    t = q.shape[1]
    per = min(INTRA_CHUNKS, t // CHUNK)
    rows, nb = per * CHUNK, N_HEADS * per

    def body(q_ref, k_ref, v_ref, g_ref, b_ref, du_ref, dw_ref, dqk_ref, dqs_ref, dks_ref, dgs_ref,
             dq_ref, dk_ref, dv_ref, dg_ref, db_ref):
        def chunks(r):
            return r[...].reshape(nb, CHUNK, r.shape[-1])
        _, vjp = jax.vjp(delta_intra, *[chunks(r) for r in (q_ref, k_ref, v_ref, g_ref, b_ref)])
        dq, dk, dv, dg, db = vjp(tuple(chunks(r) for r in (du_ref, dw_ref, dqk_ref)))
        dq_ref[...] = dq.reshape(dq_ref.shape) + dqs_ref[...]
        dk_ref[...] = dk.reshape(dk_ref.shape) + dks_ref[...]
        dv_ref[...] = dv.reshape(dv_ref.shape)
        dg_ref[...] = dg.reshape(dg_ref.shape) + dgs_ref[...]
        db_ref[...] = db.reshape(db_ref.shape)

    spec = _delta_specs(t)
    x_spec, g_spec, qk_spec = (spec(rows, w, lambda i: i) for w in (HEAD_DIM, 1, CHUNK))
    return pl.pallas_call(
        body, name=name, grid=(t // rows,),
        in_specs=[x_spec] * 3 + [g_spec] * 2 + [x_spec, x_spec, qk_spec, x_spec, x_spec, g_spec],
        out_specs=[x_spec] * 3 + [g_spec] * 2,
        out_shape=[jax.ShapeDtypeStruct((N_HEADS, t, HEAD_DIM), F32)] * 3 + [jax.ShapeDtypeStruct((N_HEADS, t, 1), F32)] * 2,
        compiler_params=_params(("parallel",)),
    )(q, k, v, gc, beta, du, dw, dqk, dq_s, dk_s, dg_s)


_V_BLOCK = 2 * N_HEADS
FOX_GROUPS = 8


def _fox_groups(t):
    nq = t // Q_BLOCK
    per = max(1, nq // FOX_GROUPS)
    return [(g0, per, (g0 + per) * Q_BLOCK) for g0 in range(0, nq, per)]


def fox_attn_fwd(q, k, proj, fq, fk, *, name):
    t = q.shape[0]

    def body(q_ref, k_ref, v_ref, fq_ref, fk_ref, o_ref):
        for g0, per, keys in _fox_groups(t):
            def block(j, carry, g0=g0, keys=keys):
                rows = pl.ds(pl.multiple_of((g0 + j) * Q_BLOCK, Q_BLOCK), Q_BLOCK)
                o_ref[rows, :] = fox_block(q_ref[rows, :], k_ref[0:keys, :], v_ref[0:keys, :], fq_ref[rows, :],
                                           fk_ref[:, 0:keys], (g0 + j) * Q_BLOCK)
                return carry
            lax.fori_loop(0, per, block, 0)

    x_spec = pl.BlockSpec((t, HEAD_DIM), lambda h: (0, h))
    v_spec = pl.BlockSpec((t, HEAD_DIM), lambda h: (0, _V_BLOCK + h))
    fq_spec = pl.BlockSpec((None, t, 1), lambda h: (h, 0, 0))
    fk_spec = pl.BlockSpec((None, 1, t), lambda h: (h, 0, 0))
    return pl.pallas_call(
        body, name=name, grid=(N_HEADS,), in_specs=[x_spec, x_spec, v_spec, fq_spec, fk_spec], out_specs=x_spec,
        out_shape=jax.ShapeDtypeStruct((t, D_MODEL), F32), compiler_params=_params(("parallel",)),
    )(q, k, proj, fq, fk)


def fox_attn_bwd(q, k, proj, fq, fk, do, *, name):
    t = q.shape[0]

    def body(q_ref, k_ref, v_ref, fq_ref, fk_ref, do_ref, dq_ref, dk_ref, dv_ref, dfq_ref, dfk_ref):
        dk_ref[...] = jnp.zeros_like(dk_ref)
        dv_ref[...] = jnp.zeros_like(dv_ref)
        dfk_ref[...] = jnp.zeros_like(dfk_ref)
        for g0, per, keys in _fox_groups(t):
            def block(j, carry, g0=g0, keys=keys):
                rows = pl.ds(pl.multiple_of((g0 + j) * Q_BLOCK, Q_BLOCK), Q_BLOCK)
                f = functools.partial(fox_block, qpos0=(g0 + j) * Q_BLOCK)
                _, vjp = jax.vjp(f, q_ref[rows, :], k_ref[0:keys, :], v_ref[0:keys, :], fq_ref[rows, :], fk_ref[:, 0:keys])
                dq, dk, dv, dfq, dfk = vjp(do_ref[rows, :])
                dq_ref[rows, :] = dq
                dfq_ref[rows, :] = dfq
                dk_ref[0:keys, :] += dk
                dv_ref[0:keys, :] += dv
                dfk_ref[:, 0:keys] += dfk
                return carry
            lax.fori_loop(0, per, block, 0)

    x_spec = pl.BlockSpec((t, HEAD_DIM), lambda h: (0, h))
    v_spec = pl.BlockSpec((t, HEAD_DIM), lambda h: (0, _V_BLOCK + h))
    fq_spec = pl.BlockSpec((None, t, 1), lambda h: (h, 0, 0))
    fk_spec = pl.BlockSpec((None, 1, t), lambda h: (h, 0, 0))
    return pl.pallas_call(
        body, name=name, grid=(N_HEADS,), in_specs=[x_spec, x_spec, v_spec, fq_spec, fk_spec, x_spec],
        out_specs=[x_spec, x_spec, x_spec, fq_spec, fk_spec],
        out_shape=[jax.ShapeDtypeStruct((t, D_MODEL), F32)] * 3
        + [jax.ShapeDtypeStruct((N_HEADS, t, 1), F32), jax.ShapeDtypeStruct((N_HEADS, 1, t), F32)],
        compiler_params=_params(("parallel",)),
    )(q, k, proj, fq, fk, do)


def memkv_fwd(mem, mnw, wkv, mknw, *, name):
    n = mem.shape[0]

    def body(mem_ref, mnw_ref, w_ref, mknw_ref, mk_ref, mv_ref):
        mk, mv = memkv_fn(mem_ref[...], mnw_ref[...], w_ref[...], mknw_ref[...])
        mk_ref[...] = mk
        mv_ref[...] = mv

    return pl.pallas_call(
        body, name=name, out_shape=[jax.ShapeDtypeStruct((n, MEM_WIDTH), F32)] * 2,
        compiler_params=pltpu.CompilerParams(vmem_limit_bytes=VMEM_LIMIT),
    )(mem, mnw, wkv, mknw)


def memkv_bwd(mem, mnw, wkv, mknw, dmk, dmv, *, name):
    def body(mem_ref, mnw_ref, w_ref, mknw_ref, dmk_ref, dmv_ref, dmnw_ref, dw_ref, dmknw_ref):
        f = functools.partial(memkv_fn, mem_ref[...])
        _, vjp = jax.vjp(f, mnw_ref[...], w_ref[...].astype(F32), mknw_ref[...])
        dmnw, dw, dmknw = vjp((dmk_ref[...], dmv_ref[...]))
        dmnw_ref[...] = dmnw
        dw_ref[...] = dw
        dmknw_ref[...] = dmknw

    return pl.pallas_call(
        body, name=name,
        out_shape=[jax.ShapeDtypeStruct(mnw.shape, F32), jax.ShapeDtypeStruct(wkv.shape, F32), jax.ShapeDtypeStruct(mknw.shape, F32)],
        compiler_params=pltpu.CompilerParams(vmem_limit_bytes=VMEM_LIMIT),
    )(mem, mnw, wkv, mknw, dmk, dmv)


def _row(v, width=None):
    v = v.reshape(1, -1)
    if width is not None and v.shape[1] < width:
        v = jnp.pad(v, ((0, 0), (0, width - v.shape[1])))
    return v


def _head_cols(a):
    return a[:, :N_HEADS].T[:, :, None]


def _lanes_from_heads(*cols):
    t = cols[0].shape[1]
    parts = [c[:, :, 0].T for c in cols]
    parts.append(jnp.zeros((t, LANES - N_HEADS * len(cols)), F32))
    return jnp.concatenate(parts, axis=1)


def _norm_fwd(x, w, name):
    return rows_call(lambda x, w: rms(x, w), [x], [w], [(D_MODEL, BF16)], [], tm=512, name=name)[0]


def _norm_bwd(x, w, dh, dx_in, name):
    def fn(x, dh, dx_in, w):
        _, vjp = jax.vjp(rms, x, w)
        dx, dw = vjp(dh)
        return dx + dx_in, dw
    return rows_call(fn, [x, dh, dx_in], [w], [(D_MODEL, F32)], [(1, D_MODEL)], tm=512, name=name)


def _mlp_fwd(x, n2w, w1, w2, tag):
    h2 = _norm_fwd(x, n2w, f"norm2_fwd_{tag}")
    u = matmul(h2, w1, name=f"mlp1_fwd_{tag}")
    a1 = rows_call(lambda u: jnp.square(jnp.maximum(u, 0.0)), [u], [], [(D_FF, BF16)], [], tm=256, name=f"act_fwd_{tag}")[0]
    y = matmul(a1, w2, add=x, name=f"mlp2_fwd_{tag}")
    return y, (x, h2, u, a1)


def _mlp_bwd(dy, res, n2w, w1, w2, tag):
    x, h2, u, a1 = res
    da1 = matmul(dy, w2, tb=True, name=f"mlp2_dx_{tag}")
    dw2 = matmul(a1, dy, ta=True, name=f"mlp2_dw_{tag}")
    du = rows_call(lambda u, da: da * (2.0 * jnp.maximum(u, 0.0)), [u, da1], [], [(D_FF, BF16)], [], tm=256,
                   name=f"act_bwd_{tag}")[0]
    dh2 = matmul(du, w1, tb=True, name=f"mlp1_dx_{tag}")
    dw1 = matmul(h2, du, ta=True, name=f"mlp1_dw_{tag}")
    dx, dn2w = _norm_bwd(x, n2w, dh2, dy, f"norm2_bwd_{tag}")
    return dx, dw1, dw2, dn2w


def _in_proj_bwd(h, dmain, dsmall, w_main, w_small, tag):
    dh = matmul(dmain, w_main, tb=True, name=f"inproj_dx_main_{tag}")
    dh = matmul(dsmall, w_small, tb=True, add=dh, name=f"inproj_dx_small_{tag}")
    dw_main = matmul(h, dmain, ta=True, name=f"inproj_dw_main_{tag}")
    dw_small = matmul(h, dsmall, ta=True, name=f"inproj_dw_small_{tag}")
    return dh, dw_main, dw_small


def local_step(x, mem, target, w):
    t = x.shape[0]
    n_mem = mem.shape[0]
    g = {}

    mnw, mknw = _row(w["mem_norm_w"]), _row(w["mem_k_norm_w"])
    mk, mv = memkv_fwd(mem, mnw, w["w_mem_kv"], mknw, name="memkv_fwd")

    n1w0, n2w0 = _row(w["norm1_w"][0]), _row(w["norm2_w"][0])
    alog, dtb = _row(w["dn_a_log"][0], LANES), _row(w["dn_dt_bias"][0], LANES)
    onw, mqw0 = _row(w["dn_o_norm_w"][0]), _row(w["memq_norm_w"][0])
    x0 = x
    h0 = _norm_fwd(x0, n1w0, "norm1_fwd_0")
    pm0 = matmul(h0, w["dn_main"], name="inproj_main_0")
    ps0 = matmul(h0, w["dn_ab"], name="inproj_small_0")
    gates = rows_call(dn_gates_fn, [ps0], [alog, dtb], [(LANES, F32)], [], tm=512, name="dn_gates_fwd")[0]
    gc, beta = _head_cols(gates), _head_cols(gates[:, N_HEADS:])
    q0, k0, v0 = dn_prep_fwd(pm0, w["dn_conv_w"], name="dn_prep_fwd")
    u0, w0, qk0 = delta_intra_fwd(q0, k0, v0, gc, beta, name="delta_intra_fwd")
    o0, s_start = delta_seq_fwd(q0, k0, gc, u0, w0, qk0, name="delta_seq_fwd")
    cat0 = rows_call(dn_out_fn, [o0, (pm0, D_MODEL, 3), (pm0, MEM_WIDTH, 8)], [onw, mqw0, mk, mv],
                     [(D_MODEL + MEM_WIDTH, BF16)], [], tm=256, name="dn_out_fwd")[0]
    x1 = matmul(cat0, w["w_out"][0], add=x0, name="wout_fwd_0")
    x2, mlp_res0 = _mlp_fwd(x1, n2w0, w["w_mlp1"][0], w["w_mlp2"][0], "0")

    n1w1, n2w1 = _row(w["norm1_w"][1]), _row(w["norm2_w"][1])
    fbias = _row(w["fox_f_bias"][0], LANES)
    qnw, knw, mqw1 = _row(w["fox_q_norm_w"][0]), _row(w["fox_k_norm_w"][0]), _row(w["memq_norm_w"][1])
    h1 = _norm_fwd(x2, n1w1, "norm1_fwd_1")
    pm1 = matmul(h1, w["fox_main"], name="inproj_main_1")
    ps1 = matmul(h1, w["fox_f"], name="inproj_small_1")
    fcum = rows_call(fox_fcum_fn, [ps1], [fbias], [(LANES, F32)], [], tm=t, name="fox_fcum_fwd")[0]
    fq = _head_cols(fcum)
    fk = jnp.swapaxes(fq, 1, 2)
    q1, k1 = rows_call(fox_qk_fn, [(pm1, D_MODEL, 0), (pm1, D_MODEL, 1)], [qnw, knw], [(D_MODEL, F32)] * 2, [], tm=256,
                       name="fox_qk_fwd")
    o1 = fox_attn_fwd(q1, k1, pm1, fq, fk, name="fox_attn_fwd")
    cat1 = rows_call(fox_out_fn, [o1, (pm1, D_MODEL, 3), (pm1, MEM_WIDTH, 8)], [mqw1, mk, mv],
                     [(D_MODEL + MEM_WIDTH, BF16)], [], tm=256, name="fox_out_fwd")[0]
    x3 = matmul(cat1, w["w_out"][1], add=x2, name="wout_fwd_1")
    y, mlp_res1 = _mlp_fwd(x3, n2w1, w["w_mlp1"][1], w["w_mlp2"][1], "1")

    def loss_fn(y, tgt):
        e = y - tgt
        return e * (1.0 / D_MODEL), jnp.sum(jnp.sum(e * e, axis=1, keepdims=True), axis=0, keepdims=True)
    dy, sq = rows_call(loss_fn, [y, target], [], [(D_MODEL, F32)], [(1, 1)], tm=512, name="loss")
    loss = sq[0, 0] * (0.5 / D_MODEL)

    dx3, dw1_1, dw2_1, dn2w1 = _mlp_bwd(dy, mlp_res1, n2w1, w["w_mlp1"][1], w["w_mlp2"][1], "1")
    dcat1 = matmul(dx3, w["w_out"][1], tb=True, name="wout_dx_1")
    dwo1 = matmul(cat1, dx3, ta=True, name="wout_dw_1")
    do1, dgate1, dqm1, dmqw1, dmk1, dmv1 = rows_call(
        functools.partial(vjp_rows(fox_out_fn, 3, (True, True, True)), n_row=3, n_ct=1),
        [o1, (pm1, D_MODEL, 3), (pm1, MEM_WIDTH, 8), dcat1], [mqw1, mk, mv],
        [(D_MODEL, F32), (D_MODEL, F32), (MEM_WIDTH, F32)], [(1, HEAD_DIM), (n_mem, MEM_WIDTH), (n_mem, MEM_WIDTH)],
        tm=256, name="fox_out_bwd")
    dq1, dk1, dv1, dfq, dfk = fox_attn_bwd(q1, k1, pm1, fq, fk, do1, name="fox_attn_bwd")
    dqraw1, dkraw1, dqnw, dknw = rows_call(
        functools.partial(vjp_rows(fox_qk_fn, 2, (True, True)), n_row=2, n_ct=2),
        [(pm1, D_MODEL, 0), (pm1, D_MODEL, 1), dq1, dk1], [qnw, knw],
        [(D_MODEL, F32)] * 2, [(1, HEAD_DIM)] * 2, tm=256, name="fox_qk_bwd")
    dfcum = _lanes_from_heads(dfq + jnp.swapaxes(dfk, 1, 2))
    dps1, dfbias = rows_call(
        functools.partial(vjp_rows(fox_fcum_fn, 1, (True,)), n_row=1, n_ct=1),
        [ps1, dfcum], [fbias], [(LANES, F32)], [(1, LANES)], tm=t, name="fox_fcum_bwd")
    dpm1 = jnp.concatenate([dqraw1, dkraw1, dv1, dgate1, dqm1], axis=1)
    dh1, dwmain1, dwsmall1 = _in_proj_bwd(h1, dpm1, dps1, w["fox_main"], w["fox_f"], "1")
    dx2, dn1w1 = _norm_bwd(x2, n1w1, dh1, dx3, "norm1_bwd_1")

    dx1, dw1_0, dw2_0, dn2w0 = _mlp_bwd(dx2, mlp_res0, n2w0, w["w_mlp1"][0], w["w_mlp2"][0], "0")
    dcat0 = matmul(dx1, w["w_out"][0], tb=True, name="wout_dx_0")
    dwo0 = matmul(cat0, dx1, ta=True, name="wout_dw_0")
    do0, dz0, dqm0, donw, dmqw0, dmk0, dmv0 = rows_call(
        functools.partial(vjp_rows(dn_out_fn, 3, (True, True, True, True)), n_row=3, n_ct=1),
        [o0, (pm0, D_MODEL, 3), (pm0, MEM_WIDTH, 8), dcat0], [onw, mqw0, mk, mv],
        [((N_HEADS, HEAD_DIM), F32), (D_MODEL, F32), (MEM_WIDTH, F32)],
        [(1, HEAD_DIM), (1, HEAD_DIM), (n_mem, MEM_WIDTH), (n_mem, MEM_WIDTH)], tm=256, name="dn_out_bwd")
    dq_s, dk_s, dg_s, du0, dw0, dqk0 = delta_seq_bwd(q0, k0, gc, u0, w0, qk0, s_start, do0, name="delta_seq_bwd")
    dq0, dk0, dv0, dgc, dbeta = delta_intra_bwd(q0, k0, v0, gc, beta, du0, dw0, dqk0, dq_s, dk_s, dg_s,
                                                name="delta_intra_bwd")
    dxq, dxk, dxv, dcq, dck, dcv = dn_prep_bwd(pm0, w["dn_conv_w"], dq0, dk0, dv0, name="dn_prep_bwd")
    dconv = jnp.concatenate([dcq, dck, dcv], axis=1)
    dps0, dalog, ddtb = rows_call(
        functools.partial(vjp_rows(dn_gates_fn, 1, (True, True)), n_row=1, n_ct=1),
        [ps0, _lanes_from_heads(dgc, dbeta)], [alog, dtb], [(LANES, F32)], [(1, LANES)] * 2, tm=512, name="dn_gates_bwd")
    dpm0 = jnp.concatenate([dxq, dxk, dxv, dz0, dqm0], axis=1)
    dh0, dwmain0, dwsmall0 = _in_proj_bwd(h0, dpm0, dps0, w["dn_main"], w["dn_ab"], "0")
    grad_x, dn1w0 = _norm_bwd(x0, n1w0, dh0, dx1, "norm1_bwd_0")

    dmnw, dwkv, dmknw = memkv_bwd(mem, mnw, w["w_mem_kv"], mknw, dmk0 + dmk1, dmv0 + dmv1, name="memkv_bwd")

    g["mem_norm_w"] = dmnw[0]
    g["w_mem_kv"] = dwkv
    g["mem_k_norm_w"] = dmknw[0]
    g["norm1_w"] = jnp.concatenate([dn1w0, dn1w1], axis=0)
    g["dn_main"], g["dn_ab"] = dwmain0, dwsmall0
    g["dn_conv_w"] = dconv
    g["dn_a_log"] = dalog[:, :N_HEADS]
    g["dn_dt_bias"] = ddtb[:, :N_HEADS]
    g["dn_o_norm_w"] = donw
    g["fox_main"], g["fox_f"] = dwmain1, dwsmall1
    g["fox_f_bias"] = dfbias[:, :N_HEADS]
    g["fox_q_norm_w"] = dqnw
    g["fox_k_norm_w"] = dknw
    g["memq_norm_w"] = jnp.concatenate([dmqw0, dmqw1], axis=0)
    g["w_out"] = jnp.stack([dwo0, dwo1])
    g["norm2_w"] = jnp.concatenate([dn2w0, dn2w1], axis=0)
    g["w_mlp1"] = jnp.stack([dw1_0, dw1_1])
    g["w_mlp2"] = jnp.stack([dw2_0, dw2_1])
    return loss, grad_x, g


WEIGHTS = ["mem_norm_w", "w_mem_kv", "mem_k_norm_w", "norm1_w", "dn_w_in", "dn_conv_w", "dn_a_log", "dn_dt_bias",
           "dn_o_norm_w", "fox_w_in", "fox_f_bias", "fox_q_norm_w", "fox_k_norm_w", "memq_norm_w", "w_out", "norm2_w",
           "w_mlp1", "w_mlp2"]
DN_IN = 4 * D_MODEL + 2 * N_HEADS + MEM_WIDTH
FOX_IN = 4 * D_MODEL + N_HEADS + MEM_WIDTH
IN_PAD = 640
SMALL = [("mem_norm_w", (D_MODEL,), D_MODEL), ("mem_k_norm_w", (HEAD_DIM,), HEAD_DIM), ("norm1_w", (2, D_MODEL), 2 * D_MODEL),
         ("dn_a_log", (1, N_HEADS), LANES), ("dn_dt_bias", (1, N_HEADS), LANES), ("dn_o_norm_w", (1, HEAD_DIM), HEAD_DIM),
         ("fox_f_bias", (1, N_HEADS), LANES), ("fox_q_norm_w", (1, HEAD_DIM), HEAD_DIM), ("fox_k_norm_w", (1, HEAD_DIM), HEAD_DIM),
         ("memq_norm_w", (2, HEAD_DIM), 2 * HEAD_DIM), ("norm2_w", (2, D_MODEL), 2 * D_MODEL)]
SMALL_ROWS = 16
CONV_ROWS = 16
SEG_ROWS = [("w_mem_kv", 256), ("dn_w_in", 1280), ("fox_w_in", 1280), ("w_out", 768), ("w_mlp1", 2048), ("w_mlp2", 2048)]
BIG_ROWS = sum(r for _, r in SEG_ROWS)
PACK_ROWS = BIG_ROWS + CONV_ROWS
PACK_TILE = 592


def _seg_offsets():
    off, out = 0, {}
    for n, r in SEG_ROWS:
        out[n] = (off, off + r)
        off += r
    return out


SEG = _seg_offsets()


def _flat_rows(a, rows):
    a = a.reshape(-1)
    return jnp.pad(a, (0, rows * PACK_W - a.shape[0])).reshape(rows, PACK_W)


def _pack_big(p):
    def in_w(a):
        a = a[0]
        return jnp.pad(a, ((0, 0), (0, IN_PAD - a.shape[1]))).reshape(-1, PACK_W)
    return [p["w_mem_kv"].reshape(-1, PACK_W), in_w(p["dn_w_in"]), in_w(p["fox_w_in"]), p["w_out"].reshape(-1, PACK_W),
            p["w_mlp1"].reshape(-1, PACK_W), p["w_mlp2"].reshape(-1, PACK_W)]


def _pack_small(p):
    small = jnp.concatenate([jnp.pad(p[n].reshape(-1), (0, ln - math.prod(sh))) for n, sh, ln in SMALL])
    return _flat_rows(small, SMALL_ROWS)


def pack_shard(p):
    return jnp.concatenate(_pack_big(p) + [_flat_rows(p["dn_conv_w"], CONV_ROWS)], axis=0), _pack_small(p)


def unpack_shard(pk, small):
    out = {}
    seg = {n: pk[a:b] for n, (a, b) in SEG.items()}
    out["w_mem_kv"] = seg["w_mem_kv"].reshape(D_MODEL // N_DEV, D_MODEL)
    out["dn_w_in"] = seg["dn_w_in"].reshape(D_MODEL, IN_PAD)[None, :, :DN_IN // N_DEV]
    out["fox_w_in"] = seg["fox_w_in"].reshape(D_MODEL, IN_PAD)[None, :, :FOX_IN // N_DEV]
    out["w_out"] = seg["w_out"].reshape(2, (D_MODEL + MEM_WIDTH) // N_DEV, D_MODEL)
    out["w_mlp1"] = seg["w_mlp1"].reshape(2, D_MODEL, D_FF // N_DEV)
    out["w_mlp2"] = seg["w_mlp2"].reshape(2, D_FF // N_DEV, D_MODEL)
    n_conv = CONV_WIDTH * 3 * D_MODEL // N_DEV
    out["dn_conv_w"] = pk[BIG_ROWS:].reshape(-1)[:n_conv].reshape(1, CONV_WIDTH, 3 * D_MODEL // N_DEV)
    small = small.reshape(-1)
    off = 0
    for n, sh, ln in SMALL:
        out[n] = small[off:off + math.prod(sh)].reshape(sh)
        off += ln
    return out


def pack_wire(p):
    cw = p["dn_conv_w"].reshape(-1)
    hi = cw.astype(BF16)
    lo = (cw - hi.astype(F32)).astype(BF16)
    big = [a.astype(BF16) for a in _pack_big(p)]
    return jnp.concatenate(big + [_flat_rows(jnp.concatenate([hi, lo]), CONV_ROWS)], axis=0)


def unpack_gathered(gw, p):
    w = {n: p[n] for n, _, _ in SMALL}
    seg = {n: gw[:, a:b] for n, (a, b) in SEG.items()}
    w["w_mem_kv"] = seg["w_mem_kv"].reshape(D_MODEL, D_MODEL)

    def in_w(s, width):
        return s.reshape(N_DEV, D_MODEL, IN_PAD)[:, :, :width // N_DEV].transpose(1, 0, 2).reshape(D_MODEL, width)

    dn = in_w(seg["dn_w_in"], DN_IN)
    fox = in_w(seg["fox_w_in"], FOX_IN)
    gate_end = 4 * D_MODEL
    w["dn_main"] = jnp.concatenate([dn[:, :gate_end], dn[:, gate_end + 2 * N_HEADS:]], axis=1)
    w["dn_ab"] = jnp.pad(dn[:, gate_end:gate_end + 2 * N_HEADS], ((0, 0), (0, LANES - 2 * N_HEADS)))
    w["fox_main"] = jnp.concatenate([fox[:, :gate_end], fox[:, gate_end + N_HEADS:]], axis=1)
    w["fox_f"] = jnp.pad(fox[:, gate_end:gate_end + N_HEADS], ((0, 0), (0, LANES - N_HEADS)))
    w["w_out"] = seg["w_out"].reshape(N_DEV, 2, -1, D_MODEL).transpose(1, 0, 2, 3).reshape(2, D_MODEL + MEM_WIDTH, D_MODEL)
    w["w_mlp1"] = seg["w_mlp1"].reshape(N_DEV, 2, D_MODEL, -1).transpose(1, 2, 0, 3).reshape(2, D_MODEL, D_FF)
    w["w_mlp2"] = seg["w_mlp2"].reshape(N_DEV, 2, -1, D_MODEL).transpose(1, 0, 2, 3).reshape(2, D_FF, D_MODEL)
    n_conv = CONV_WIDTH * 3 * D_MODEL // N_DEV
    cw = gw[:, BIG_ROWS:].reshape(N_DEV, -1)[:, :2 * n_conv].astype(F32)
    cw = cw[:, :n_conv] + cw[:, n_conv:]
    w["dn_conv_w"] = cw.reshape(N_DEV, CONV_WIDTH, -1).transpose(1, 0, 2).reshape(CONV_WIDTH, 3 * D_MODEL)
    return w


def pack_grads(g):
    gate_end = 4 * D_MODEL

    def in_w(main, small, n_small, width):
        full = jnp.concatenate([main[:, :gate_end], small[:, :n_small], main[:, gate_end:]], axis=1)
        s = full.reshape(D_MODEL, N_DEV, width // N_DEV).transpose(1, 0, 2)
        return jnp.pad(s, ((0, 0), (0, 0), (0, IN_PAD - width // N_DEV))).reshape(N_DEV, -1, PACK_W)

    parts = [
        g["w_mem_kv"].reshape(N_DEV, -1, PACK_W),
        in_w(g["dn_main"], g["dn_ab"], 2 * N_HEADS, DN_IN),
        in_w(g["fox_main"], g["fox_f"], N_HEADS, FOX_IN),
        g["w_out"].reshape(2, N_DEV, -1, D_MODEL).transpose(1, 0, 2, 3).reshape(N_DEV, -1, PACK_W),
        g["w_mlp1"].reshape(2, D_MODEL, N_DEV, -1).transpose(2, 0, 1, 3).reshape(N_DEV, -1, PACK_W),
        g["w_mlp2"].reshape(2, N_DEV, -1, D_MODEL).transpose(1, 0, 2, 3).reshape(N_DEV, -1, PACK_W),
        jnp.stack([_flat_rows(c, CONV_ROWS) for c in g["dn_conv_w"].reshape(CONV_WIDTH, N_DEV, -1).transpose(1, 0, 2)]),
    ]
    return jnp.concatenate([a.astype(BF16) for a in parts], axis=1), _pack_small(g)


_HBM = pl.BlockSpec(memory_space=pltpu.HBM)


def _place():
    return lax.axis_index("x"), lax.axis_index("y"), lax.axis_index("c")


def all_gather(xs, *, name):
    def body(x_ref, out_ref, send_sems, recv_sems, local_sem):
        x, y, c = _place()
        me, sibling = (x, y, c), (x, y, 1 - c)
        chips = [(1 - x, y), (x, 1 - y), (1 - x, 1 - y)]

        def slot(px, py, pc):
            return out_ref.at[4 * px + 2 * py + pc]

        def copy(k, block, to, src=None):
            return pltpu.make_async_remote_copy(
                src_ref=slot(*block) if src is None else src, dst_ref=slot(*block),
                send_sem=send_sems.at[k], recv_sem=recv_sems.at[k], device_id=to, device_id_type=MESH)

        mine = pltpu.make_async_copy(x_ref, slot(*me), local_sem)
        mine.start()
        first = [copy(0, me, sibling, src=x_ref)]
        first += [copy(1 + j, me, (*chip, c), src=x_ref) for j, chip in enumerate(chips)]
        for cp in first:
            cp.start()
        passed = [copy(4 + j, (*chip, c), sibling) for j, chip in enumerate(chips)]
        for j, chip in enumerate(chips):
            copy(1 + j, (*chip, c), me).wait_recv()
            passed[j].start()
        copy(0, sibling, me).wait_recv()
        for j, chip in enumerate(chips):
            copy(4 + j, (*chip, 1 - c), me).wait_recv()
        for cp in first + passed:
            cp.wait_send()
        mine.wait()

    return pl.pallas_call(
        body, name=name, out_shape=jax.ShapeDtypeStruct((N_DEV,) + xs.shape, xs.dtype), in_specs=[_HBM], out_specs=_HBM,
        scratch_shapes=[pltpu.SemaphoreType.DMA((7,)), pltpu.SemaphoreType.DMA((7,)), pltpu.SemaphoreType.DMA],
    )(xs)


def sibling_exchange(g, *, name):
    def body(g_ref, out_ref, send_sems, recv_sems):
        x, y, c = _place()
        copies = [pltpu.make_async_remote_copy(
            src_ref=g_ref.at[2 * k + 1 - c], dst_ref=out_ref.at[k], send_sem=send_sems.at[k], recv_sem=recv_sems.at[k],
            device_id=(x, y, 1 - c), device_id_type=MESH) for k in range(4)]
        for cp in copies:
            cp.start()
        for cp in copies:
            cp.wait_recv()
        for cp in copies:
            cp.wait_send()

    return pl.pallas_call(
        body, name=name, out_shape=jax.ShapeDtypeStruct((4,) + g.shape[1:], g.dtype), in_specs=[_HBM], out_specs=_HBM,
        scratch_shapes=[pltpu.SemaphoreType.DMA((4,)), pltpu.SemaphoreType.DMA((4,))],
    )(g)


def pair_sum(g, got, *, name):
    rows = g.shape[1]
    c = lax.axis_index("c").astype(jnp.int32).reshape(1)

    def body(c_ref, a_ref, b_ref, o_ref):
        o_ref[...] = (a_ref[...].astype(F32) + b_ref[...].astype(F32)).astype(o_ref.dtype)

    grid_spec = pltpu.PrefetchScalarGridSpec(
        num_scalar_prefetch=1, grid=(4, rows // PACK_TILE),
        in_specs=[pl.BlockSpec((None, PACK_TILE, PACK_W), lambda k, i, c_ref: (2 * k + c_ref[0], i, 0)),
                  pl.BlockSpec((None, PACK_TILE, PACK_W), lambda k, i, c_ref: (k, i, 0))],
        out_specs=pl.BlockSpec((None, PACK_TILE, PACK_W), lambda k, i, c_ref: (k, i, 0)))
    return pl.pallas_call(
        body, name=name, grid_spec=grid_spec, out_shape=jax.ShapeDtypeStruct((4,) + g.shape[1:], g.dtype),
        compiler_params=_params(("parallel", "parallel")),
    )(c, g, got)


def chip_exchange(h, *, name):
    def body(h_ref, out_ref, send_sems, recv_sems, local_sem):
        x, y, c = _place()
        mine = 2 * x + y
        chips = [(1 - x, y), (x, 1 - y), (1 - x, 1 - y)]
        keep = pltpu.make_async_copy(h_ref.at[mine], out_ref.at[mine], local_sem)
        keep.start()
        sends = [pltpu.make_async_remote_copy(
            src_ref=h_ref.at[2 * qx + qy], dst_ref=out_ref.at[mine], send_sem=send_sems.at[j], recv_sem=recv_sems.at[j],
            device_id=(qx, qy, c), device_id_type=MESH) for j, (qx, qy) in enumerate(chips)]
        for cp in sends:
            cp.start()
        for j, (qx, qy) in enumerate(chips):
            pltpu.make_async_remote_copy(
                src_ref=h_ref.at[mine], dst_ref=out_ref.at[2 * qx + qy], send_sem=send_sems.at[j], recv_sem=recv_sems.at[j],
                device_id=(qx, qy, c), device_id_type=MESH).wait_recv()
        for cp in sends:
            cp.wait_send()
        keep.wait()

    return pl.pallas_call(
        body, name=name, out_shape=jax.ShapeDtypeStruct(h.shape, h.dtype), in_specs=[_HBM], out_specs=_HBM,
        scratch_shapes=[pltpu.SemaphoreType.DMA((3,)), pltpu.SemaphoreType.DMA((3,)), pltpu.SemaphoreType.DMA],
    )(h)


def adamw(parts, w, m, v, *, tile, name):
    n, rows, _ = parts.shape

    def body(p_ref, w_ref, m_ref, v_ref, g_ref, d_ref, mo_ref, vo_ref):
        g = p_ref[0].astype(F32)
        for i in range(1, n):
            g = g + p_ref[i].astype(F32)
        m_new = ADAM_B1 * m_ref[...] + (1.0 - ADAM_B1) * g
        v_new = ADAM_B2 * v_ref[...] + (1.0 - ADAM_B2) * jnp.square(g)
        m_hat = m_new / (1.0 - ADAM_B1 ** ADAM_STEP)
        v_hat = v_new / (1.0 - ADAM_B2 ** ADAM_STEP)
        g_ref[...] = g
        d_ref[...] = -ADAM_LR * (m_hat / (jnp.sqrt(v_hat) + ADAM_EPS) + ADAM_WD * w_ref[...])
        mo_ref[...] = m_new
        vo_ref[...] = v_new

    spec = pl.BlockSpec((tile, PACK_W), lambda i: (i, 0))
    return pl.pallas_call(
        body, name=name, grid=(rows // tile,),
        in_specs=[pl.BlockSpec((n, tile, PACK_W), lambda i: (0, i, 0)), spec, spec, spec], out_specs=[spec] * 4,
        out_shape=[jax.ShapeDtypeStruct((rows, PACK_W), F32)] * 4, compiler_params=_params(("parallel",)),
    )(parts, w, m, v)


def kernel(x, mem, mem_norm_w, w_mem_kv, mem_k_norm_w, norm1_w, dn_w_in, dn_conv_w, dn_a_log, dn_dt_bias, dn_o_norm_w, fox_w_in, fox_f_bias, fox_q_norm_w, fox_k_norm_w, memq_norm_w, w_out, norm2_w, w_mlp1, w_mlp2, loss_target, m_mem_norm_w, m_w_mem_kv, m_mem_k_norm_w, m_norm1_w, m_dn_w_in, m_dn_conv_w, m_dn_a_log, m_dn_dt_bias, m_dn_o_norm_w, m_fox_w_in, m_fox_f_bias, m_fox_q_norm_w, m_fox_k_norm_w, m_memq_norm_w, m_w_out, m_norm2_w, m_w_mlp1, m_w_mlp2, v_mem_norm_w, v_w_mem_kv, v_mem_k_norm_w, v_norm1_w, v_dn_w_in, v_dn_conv_w, v_dn_a_log, v_dn_dt_bias, v_dn_o_norm_w, v_fox_w_in, v_fox_f_bias, v_fox_q_norm_w, v_fox_k_norm_w, v_memq_norm_w, v_w_out, v_norm2_w, v_w_mlp1, v_w_mlp2):
    p = dict(mem_norm_w=mem_norm_w, w_mem_kv=w_mem_kv, mem_k_norm_w=mem_k_norm_w, norm1_w=norm1_w, dn_w_in=dn_w_in,
             dn_conv_w=dn_conv_w, dn_a_log=dn_a_log, dn_dt_bias=dn_dt_bias, dn_o_norm_w=dn_o_norm_w, fox_w_in=fox_w_in,
             fox_f_bias=fox_f_bias, fox_q_norm_w=fox_q_norm_w, fox_k_norm_w=fox_k_norm_w, memq_norm_w=memq_norm_w,
             w_out=w_out, norm2_w=norm2_w, w_mlp1=w_mlp1, w_mlp2=w_mlp2)
    pm = dict(mem_norm_w=m_mem_norm_w, w_mem_kv=m_w_mem_kv, mem_k_norm_w=m_mem_k_norm_w, norm1_w=m_norm1_w,
              dn_w_in=m_dn_w_in, dn_conv_w=m_dn_conv_w, dn_a_log=m_dn_a_log, dn_dt_bias=m_dn_dt_bias,
              dn_o_norm_w=m_dn_o_norm_w, fox_w_in=m_fox_w_in, fox_f_bias=m_fox_f_bias, fox_q_norm_w=m_fox_q_norm_w,
              fox_k_norm_w=m_fox_k_norm_w, memq_norm_w=m_memq_norm_w, w_out=m_w_out, norm2_w=m_norm2_w, w_mlp1=m_w_mlp1,
              w_mlp2=m_w_mlp2)
    pv = dict(mem_norm_w=v_mem_norm_w, w_mem_kv=v_w_mem_kv, mem_k_norm_w=v_mem_k_norm_w, norm1_w=v_norm1_w,
              dn_w_in=v_dn_w_in, dn_conv_w=v_dn_conv_w, dn_a_log=v_dn_a_log, dn_dt_bias=v_dn_dt_bias,
              dn_o_norm_w=v_dn_o_norm_w, fox_w_in=v_fox_w_in, fox_f_bias=v_fox_f_bias, fox_q_norm_w=v_fox_q_norm_w,
              fox_k_norm_w=v_fox_k_norm_w, memq_norm_w=v_memq_norm_w, w_out=v_w_out, norm2_w=v_norm2_w, w_mlp1=v_w_mlp1,
              w_mlp2=v_w_mlp2)

    gathered = all_gather(pack_wire(p), name="weights_all_gather")
    loss, grad_x, g = local_step(x[0], mem[0], loss_target[0], unpack_gathered(gathered, p))
    loss = lax.psum(loss, ("x", "y", "c"))

    gp, gsmall = pack_grads(g)
    chip_sums = pair_sum(gp, sibling_exchange(gp, name="grads_to_sibling"), name="grads_pair_sum")
    parts = chip_exchange(chip_sums, name="grads_to_chips")
    small_parts = all_gather(gsmall, name="small_grads_all_gather")
    (w_big, w_small), (m_big, m_small), (v_big, v_small) = pack_shard(p), pack_shard(pm), pack_shard(pv)
    big = adamw(parts, w_big, m_big, v_big, tile=PACK_TILE, name="adamw")
    small = adamw(small_parts, w_small, m_small, v_small, tile=SMALL_ROWS, name="adamw_small")
    groups = [unpack_shard(b, sm) for b, sm in zip(big, small)]
    return (loss, grad_x[None], *[grp[n] for grp in groups for n in WEIGHTS])
```

```python
import functools
import math

import jax
import jax.numpy as jnp
from jax import lax
from jax.experimental import pallas as pl
from jax.experimental.pallas import tpu as pltpu

F32 = jnp.float32
BF16 = jnp.bfloat16
HIGHEST = lax.Precision.HIGHEST

D_MODEL = 1024
HEAD_DIM = 128
N_HEADS = 8
MEM_HEADS = 4
MEM_WIDTH = MEM_HEADS * HEAD_DIM
D_FF = 4 * D_MODEL
CONV_WIDTH = 4
CHUNK = 64
Q_BLOCK = 128
EPS = 1e-6
SCALE = HEAD_DIM ** -0.5
MAIN_WIDTH = 4 * D_MODEL + MEM_WIDTH
LANES = 128
N_DEV = 8
PACK_W = 512

ADAM_LR = 0.001
ADAM_B1 = 0.9
ADAM_B2 = 0.999
ADAM_EPS = 1e-08
ADAM_WD = 0.01
ADAM_STEP = 10

VMEM_LIMIT = 56 * 2 ** 20
MESH = pl.DeviceIdType.MESH


def _bdot(a, b, dims):
    return lax.dot_general(a.astype(BF16), b.astype(BF16), (dims, ((), ())), preferred_element_type=F32)


@jax.custom_vjp
def mm(a, b):
    return _bdot(a, b, ((1,), (0,)))


@jax.custom_vjp
def mm_nt(a, b):
    return _bdot(a, b, ((1,), (1,)))


@jax.custom_vjp
def mm_tn(a, b):
    return _bdot(a, b, ((0,), (0,)))


mm.defvjp(lambda a, b: (mm(a, b), (a, b)), lambda r, g: (mm_nt(g, r[1]), mm_tn(r[0], g)))
mm_nt.defvjp(lambda a, b: (mm_nt(a, b), (a, b)), lambda r, g: (mm(g, r[1]), mm_tn(g, r[0])))
mm_tn.defvjp(lambda a, b: (mm_tn(a, b), (a, b)), lambda r, g: (mm_nt(r[1], g), mm(r[0], g)))


def hdot(a, b):
    return jnp.dot(a, b, precision=HIGHEST, preferred_element_type=F32)


def rms(x, w):
    return x * lax.rsqrt(jnp.mean(x * x, axis=-1, keepdims=True) + EPS) * w


def l2n(x):
    return x * lax.rsqrt(jnp.sum(x * x, axis=-1, keepdims=True) + EPS)


def _iota2(n, m):
    return lax.broadcasted_iota(jnp.int32, (n, m), 0), lax.broadcasted_iota(jnp.int32, (n, m), 1)


def _lower_ones(n):
    r, c = _iota2(n, n)
    return jnp.where(r >= c, 1.0, 0.0).astype(F32)


def _last_row(x):
    r = lax.broadcasted_iota(jnp.int32, x.shape, 0)
    return jnp.sum(jnp.where(r == x.shape[0] - 1, x, 0.0), axis=0, keepdims=True)


def _softmax_rows(z):
    m = lax.stop_gradient(jnp.max(z, axis=-1, keepdims=True))
    e = jnp.exp(z - m)
    return e / jnp.sum(e, axis=-1, keepdims=True)


_BNN = (((2,), (1,)), ((0,), (0,)))
_BNT = (((2,), (2,)), ((0,), (0,)))
_BTN = (((1,), (1,)), ((0,), (0,)))


def _bbdot(a, b, dims):
    return lax.dot_general(a.astype(BF16), b.astype(BF16), dims, preferred_element_type=F32)


@jax.custom_vjp
def bmm(a, b):
    return _bbdot(a, b, _BNN)


@jax.custom_vjp
def bmm_nt(a, b):
    return _bbdot(a, b, _BNT)


@jax.custom_vjp
def bmm_tn(a, b):
    return _bbdot(a, b, _BTN)


@jax.custom_vjp
def bmm_high(a, b):
    return lax.dot_general(a, b, _BNN, precision=lax.Precision.HIGH, preferred_element_type=F32)


bmm.defvjp(lambda a, b: (bmm(a, b), (a, b)), lambda r, g: (bmm_nt(g, r[1]), bmm_tn(r[0], g)))
bmm_nt.defvjp(lambda a, b: (bmm_nt(a, b), (a, b)), lambda r, g: (bmm(g, r[1]), bmm_tn(g, r[0])))
bmm_tn.defvjp(lambda a, b: (bmm_tn(a, b), (a, b)), lambda r, g: (bmm_nt(r[1], g), bmm(r[0], g)))
bmm_high.defvjp(lambda a, b: (bmm_high(a, b), (a, b)), lambda r, g: (bmm_nt(g, r[1]), bmm_tn(r[0], g)))

NEUMANN_HIGH_LEVELS = 2


def inv_unit_lower(a):
    n = a.shape[-1]
    r, c = _iota2(n, n)
    p = jnp.where(r == c, 1.0, 0.0).astype(F32) - a
    ak = a
    for level in range(int(math.log2(n)) - 1):
        dot = bmm_high if level < NEUMANN_HIGH_LEVELS else bmm
        ak = dot(ak, ak)
        p = p + dot(p, ak)
    return p


def delta_intra(q, k, v, gc, beta):
    b, c, _ = q.shape
    r, cc = _iota2(c, c)
    causal = r >= cc
    strict = r > cc
    gi = jnp.broadcast_to(gc, (b, c, c))
    gj = jnp.swapaxes(gi, 1, 2)
    decay = jnp.where(causal, jnp.exp(jnp.where(causal, gi - gj, 0.0)), 0.0)
    kb = k * beta
    a = jnp.where(strict, bmm_nt(kb, k) * decay, 0.0)
    t = inv_unit_lower(a)
    u = bmm(t, v * beta)
    w = bmm(t, kb * jnp.exp(gc))
    qk = jnp.where(causal, bmm_nt(q, k) * decay, 0.0)
    return u, w, qk


def delta_step(s, q, k, gc, u, w, qk):
    v_new = u - bmm(w, s)
    out = bmm(q * jnp.exp(gc), s) + bmm(qk, v_new)
    r = lax.broadcasted_iota(jnp.int32, gc.shape, 1)
    g_last = jnp.sum(jnp.where(r == gc.shape[1] - 1, gc, 0.0), axis=1, keepdims=True)
    k_dec = k * jnp.exp(g_last - gc)
    s_new = s * jnp.exp(g_last) + bmm_tn(k_dec, v_new)
    return out, s_new


def fox_block(q, k, v, fq, fk, qpos0):
    s = mm_nt(q, k)
    r, c = _iota2(s.shape[0], s.shape[1])
    mask = c <= (r + qpos0)
    p = _softmax_rows(jnp.where(mask, s + (fq - fk), -jnp.inf))
    return mm(p, v)


def mem_head(qm, wq, mk, mv):
    p = _softmax_rows(mm_nt(rms(qm, wq) * SCALE, mk))
    return mm(p, mv)


def _heads(x, n):
    return [x[:, h * HEAD_DIM:(h + 1) * HEAD_DIM] for h in range(n)]


def memkv_fn(mem, mnw, wkv, mknw):
    kv = mm(rms(mem, mnw), wkv)
    mk = jnp.concatenate([rms(kh, mknw) for kh in _heads(kv[:, :MEM_WIDTH], MEM_HEADS)], axis=1)
    return mk, kv[:, MEM_WIDTH:]


def dn_gates_fn(ab, alog, dtb):
    g = -jnp.exp(alog) * jax.nn.softplus(ab + dtb)
    low = _lower_ones(CHUNK)
    gc = jnp.concatenate([hdot(low, g[i * CHUNK:(i + 1) * CHUNK]) for i in range(ab.shape[0] // CHUNK)], axis=0)
    lane = lax.broadcasted_iota(jnp.int32, ab.shape, 1)
    return jnp.where(lane < N_HEADS, gc, jax.nn.sigmoid(ab))


def fox_fcum_fn(fp, fbias):
    lf = jax.nn.log_sigmoid(fp + fbias)
    low = _lower_ones(LANES)
    carry = jnp.zeros((1, fp.shape[1]), F32)
    outs = []
    for i in range(fp.shape[0] // LANES):
        cs = hdot(low, lf[i * LANES:(i + 1) * LANES]) + carry
        carry = _last_row(cs)
        outs.append(cs)
    return jnp.concatenate(outs, axis=0)


def fox_qk_fn(qraw, kraw, qnw, knw):
    q = jnp.concatenate([rms(x, qnw) * SCALE for x in _heads(qraw, N_HEADS)], axis=1)
    k = jnp.concatenate([rms(x, knw) for x in _heads(kraw, N_HEADS)], axis=1)
    return q, k


def _mem_out(qm, mqw, mk, mv):
    return [mem_head(a, mqw, b, c) for a, b, c in zip(_heads(qm, MEM_HEADS), _heads(mk, MEM_HEADS), _heads(mv, MEM_HEADS))]


def dn_out_fn(o, z, qm, onw, mqw, mk, mv):
    mix = [rms(a, onw) * jax.nn.silu(b) for a, b in zip(o, _heads(z, N_HEADS))]
    return jnp.concatenate(mix + _mem_out(qm, mqw, mk, mv), axis=1)


def fox_out_fn(o, gate, qm, mqw, mk, mv):
    return jnp.concatenate([o * jax.nn.sigmoid(gate)] + _mem_out(qm, mqw, mk, mv), axis=1)


def _pick(n, cands):
    for c in cands:
        if n % c == 0:
            return c
    return n


def _params(sem):
    return pltpu.CompilerParams(dimension_semantics=sem, vmem_limit_bytes=VMEM_LIMIT)


def matmul(a, b, *, name, ta=False, tb=False, add=None, out_dtype=F32, b_view=None, out_view=None):
    (k, m) = a.shape if ta else a.shape[::-1]
    tm = _pick(m, (512, 256, 128))
    if b_view is None:
        (kb, n) = b.shape[::-1] if tb else b.shape
        tn = _pick(n, (512, 256, 128))
        tk = _pick(k, (1024, 512, 256, 128))
        b_spec = pl.BlockSpec((tn, tk), lambda i, j, kk: (j, kk)) if tb else pl.BlockSpec((tk, tn), lambda i, j, kk: (kk, j))
    else:
        kb, n, tk, tn, b_spec = b_view
    assert k == kb, (a.shape, b.shape, ta, tb)
    o_spec = pl.BlockSpec((tm, tn), lambda i, j, kk: (i, j))
    o_shape, into = (m, n), None
    if out_view is not None:
        assert add is None
        o_shape, tm_o, tn_o, o_spec, into = out_view
        assert (tm_o, tn_o) == (tm, tn), (tm, tn, out_view)
    nk = k // tk
    dims = ((0,) if ta else (1,), (1,) if tb else (0,))

    def body(*refs):
        a_ref, b_ref = refs[:2]
        add_ref = refs[2] if add is not None else None
        o_ref, acc = refs[-2:]
        kk = pl.program_id(2)

        @pl.when(kk == 0)
        def _():
            acc[...] = jnp.zeros_like(acc)

        acc[...] += _bdot(a_ref[...], b_ref[...], dims)

        @pl.when(kk == nk - 1)
        def _():
            r = acc[...]
            if add is not None:
                r = r + add_ref[...]
            o_ref[...] = r.astype(out_dtype)

    a_spec = pl.BlockSpec((tk, tm), lambda i, j, kk: (kk, i)) if ta else pl.BlockSpec((tm, tk), lambda i, j, kk: (i, kk))
    ins, specs, aliases = [a, b], [a_spec, b_spec], {}
    if add is not None:
        ins.append(add)
        specs.append(o_spec)
    if into is not None:
        aliases = {len(ins): 0}
        ins.append(into)
        specs.append(pl.BlockSpec(memory_space=pl.ANY))
    return pl.pallas_call(
        body, name=name, grid=(m // tm, n // tn, nk), in_specs=specs, out_specs=o_spec,
        out_shape=jax.ShapeDtypeStruct(o_shape, out_dtype), input_output_aliases=aliases,
        scratch_shapes=[pltpu.VMEM((tm, tn), F32)],
        compiler_params=_params(("parallel", "parallel", "arbitrary")),
    )(*ins)


def rows_call(fn, row_ins, full_ins, row_outs, acc_outs, *, tm, name):
    row_ins = [r if isinstance(r, tuple) else (r, r.shape[-1], 0) for r in row_ins]
    t = row_ins[0][0].shape[-2]
    tm = min(tm, t)
    n_in = len(row_ins) + len(full_ins)
    n_row = len(row_outs)

    def body(*refs):
        res = fn(*[[r[h] for h in range(r.shape[0])] if (i < len(row_ins) and len(r.shape) == 3) else r[...]
                   for i, r in enumerate(refs[:n_in])])
        res = res if isinstance(res, (tuple, list)) else (res,)
        outs = refs[n_in:]
        for ref, val in zip(outs[:n_row], res[:n_row]):
            if len(ref.shape) == 3:
                for h, vh in enumerate(val):
                    ref[h] = vh.astype(ref.dtype)
            else:
                ref[...] = val.astype(ref.dtype)
        first = pl.program_id(0) == 0
        for ref, val in zip(outs[n_row:], res[n_row:]):
            @pl.when(first)
            def _(ref=ref, val=val):
                ref[...] = val

            @pl.when(jnp.logical_not(first))
            def _(ref=ref, val=val):
                ref[...] += val

    def full_spec(shape):
        return pl.BlockSpec(shape, lambda i, nd=len(shape): (0,) * nd)

    def row_spec(lead, w, cb):
        if lead is None:
            return pl.BlockSpec((tm, w), lambda i: (i, cb))
        return pl.BlockSpec((lead, tm, w), lambda i: (0, i, cb))

    def lead_cols(c):
        return c if isinstance(c, tuple) else (None, c)

    in_specs = [row_spec(a.shape[0] if a.ndim == 3 else None, w, cb) for (a, w, cb) in row_ins]
    in_specs += [full_spec(f.shape) for f in full_ins]
    out_specs = [row_spec(*lead_cols(c), 0) for c, _ in row_outs] + [full_spec(s) for s in acc_outs]
    out_shape = [jax.ShapeDtypeStruct(tuple(d for d in (lead_cols(c)[0], t, lead_cols(c)[1]) if d is not None), dt)
                 for c, dt in row_outs] + [jax.ShapeDtypeStruct(s, F32) for s in acc_outs]
    res = pl.pallas_call(
        body, name=name, grid=(t // tm,), in_specs=in_specs, out_specs=out_specs, out_shape=out_shape,
        compiler_params=_params(("arbitrary",)),
    )(*[r[0] for r in row_ins], *full_ins)
    return res


def vjp_rows(fn, n_diff_row, row_diff_full):
    def bwd(*args, n_row, n_ct):
        prim_rows = args[:n_row]
        cts = args[n_row:n_row + n_ct]
        fulls = args[n_row + n_ct:]
        _, vjp = jax.vjp(fn, *prim_rows, *fulls)
        g = vjp(cts[0] if n_ct == 1 else tuple(cts))
        out = list(g[:n_diff_row])
        out += [gf for gf, d in zip(g[n_row:], row_diff_full) if d]
        return tuple(out)
    return bwd


def _shift_down(x, s):
    if s == 0:
        return x
    t = lax.broadcasted_iota(jnp.int32, x.shape, 0)
    return jnp.where(t >= s, pltpu.roll(x, s, 0), 0.0)


def _shift_up(x, s):
    if s == 0:
        return x
    n = x.shape[0]
    t = lax.broadcasted_iota(jnp.int32, x.shape, 0)
    return jnp.where(t < n - s, pltpu.roll(x, n - s, 0), 0.0)


def _conv(x, w_ref):
    return sum(w_ref[pl.ds(j, 1), :] * _shift_down(x, CONV_WIDTH - 1 - j) for j in range(CONV_WIDTH))


_DN_POST = (lambda c: l2n(jax.nn.silu(c)) * SCALE, lambda c: l2n(jax.nn.silu(c)), jax.nn.silu)


def dn_prep_fwd(proj, conv_w, *, name):
    t = proj.shape[0]

    def body(xq, xk, xv, wq, wk, wv, oq, ok, ov):
        for x_ref, w_ref, o_ref, post in zip((xq, xk, xv), (wq, wk, wv), (oq, ok, ov), _DN_POST):
            o_ref[...] = post(_conv(x_ref[...], w_ref))

    x_specs = [pl.BlockSpec((t, HEAD_DIM), lambda h, g=g: (0, g * N_HEADS + h)) for g in range(3)]
    w_specs = [pl.BlockSpec((CONV_WIDTH, HEAD_DIM), lambda h, g=g: (0, g * N_HEADS + h)) for g in range(3)]
    o_spec = pl.BlockSpec((None, t, HEAD_DIM), lambda h: (h, 0, 0))
    return pl.pallas_call(
        body, name=name, grid=(N_HEADS,), in_specs=x_specs + w_specs, out_specs=[o_spec] * 3,
        out_shape=[jax.ShapeDtypeStruct((N_HEADS, t, HEAD_DIM), F32)] * 3, compiler_params=_params(("parallel",)),
    )(proj, proj, proj, conv_w, conv_w, conv_w)


def dn_prep_bwd(proj, conv_w, dq, dk, dv, *, name):
    t = proj.shape[0]

    def body(xq, xk, xv, wq, wk, wv, gq, gk, gv, dxq, dxk, dxv, dwq, dwk, dwv):
        for x_ref, w_ref, g_ref, dx_ref, dw_ref, post in zip(
                (xq, xk, xv), (wq, wk, wv), (gq, gk, gv), (dxq, dxk, dxv), (dwq, dwk, dwv), _DN_POST):
            x = x_ref[...]
            _, vjp = jax.vjp(post, _conv(x, w_ref))
            dc, = vjp(g_ref[...])
            dx_ref[...] = sum(w_ref[pl.ds(j, 1), :] * _shift_up(dc, CONV_WIDTH - 1 - j) for j in range(CONV_WIDTH))
            for j in range(CONV_WIDTH):
                dw_ref[pl.ds(j, 1), :] = jnp.sum(dc * _shift_down(x, CONV_WIDTH - 1 - j), axis=0, keepdims=True)

    x_specs = [pl.BlockSpec((t, HEAD_DIM), lambda h, g=g: (0, g * N_HEADS + h)) for g in range(3)]
    w_specs = [pl.BlockSpec((CONV_WIDTH, HEAD_DIM), lambda h, g=g: (0, g * N_HEADS + h)) for g in range(3)]
    g_spec = pl.BlockSpec((None, t, HEAD_DIM), lambda h: (h, 0, 0))
    dx_spec = pl.BlockSpec((t, HEAD_DIM), lambda h: (0, h))
    dw_spec = pl.BlockSpec((CONV_WIDTH, HEAD_DIM), lambda h: (0, h))
    return pl.pallas_call(
        body, name=name, grid=(N_HEADS,), in_specs=x_specs + w_specs + [g_spec] * 3, out_specs=[dx_spec] * 3 + [dw_spec] * 3,
        out_shape=[jax.ShapeDtypeStruct((t, D_MODEL), F32)] * 3 + [jax.ShapeDtypeStruct((CONV_WIDTH, D_MODEL), F32)] * 3,
        compiler_params=_params(("parallel",)),
    )(proj, proj, proj, conv_w, conv_w, conv_w, dq, dk, dv)


INTRA_CHUNKS = 4


def _delta_specs(t):
    def spec(rows, w, index):
        return pl.BlockSpec((N_HEADS, rows, w), lambda i: (0, index(i), 0))
    return spec


def delta_intra_fwd(q, k, v, gc, beta, *, name):
    t = q.shape[1]
    per = min(INTRA_CHUNKS, t // CHUNK)
    rows, nb = per * CHUNK, N_HEADS * per

    def body(q_ref, k_ref, v_ref, g_ref, b_ref, u_ref, w_ref, qk_ref):
        ins = [r[...].reshape(nb, CHUNK, r.shape[-1]) for r in (q_ref, k_ref, v_ref, g_ref, b_ref)]
        for ref, val in zip((u_ref, w_ref, qk_ref), delta_intra(*ins)):
            ref[...] = val.reshape(ref.shape)

    spec = _delta_specs(t)
    x_spec, g_spec, qk_spec = (spec(rows, w, lambda i: i) for w in (HEAD_DIM, 1, CHUNK))
    return pl.pallas_call(
        body, name=name, grid=(t // rows,), in_specs=[x_spec] * 3 + [g_spec] * 2, out_specs=[x_spec, x_spec, qk_spec],
        out_shape=[jax.ShapeDtypeStruct((N_HEADS, t, HEAD_DIM), F32)] * 2 + [jax.ShapeDtypeStruct((N_HEADS, t, CHUNK), F32)],
        compiler_params=_params(("parallel",)),
    )(q, k, v, gc, beta)


def delta_seq_fwd(q, k, gc, u, w, qk, *, name):
    t = q.shape[1]
    nc = t // CHUNK

    def body(q_ref, k_ref, g_ref, u_ref, w_ref, qk_ref, o_ref, s0_ref, s_ref):
        @pl.when(pl.program_id(0) == 0)
        def _():
            s_ref[...] = jnp.zeros_like(s_ref)

        s = s_ref[...]
        s0_ref[...] = s
        o, s_new = delta_step(s, q_ref[...], k_ref[...], g_ref[...], u_ref[...], w_ref[...], qk_ref[...])
        o_ref[...] = o
        s_ref[...] = s_new

    spec = _delta_specs(t)
    x_spec, g_spec, qk_spec = (spec(CHUNK, w, lambda c: c) for w in (HEAD_DIM, 1, CHUNK))
    s_spec = pl.BlockSpec((N_HEADS, None, HEAD_DIM, HEAD_DIM), lambda c: (0, c, 0, 0))
    return pl.pallas_call(
        body, name=name, grid=(nc,), in_specs=[x_spec, x_spec, g_spec, x_spec, x_spec, qk_spec], out_specs=[x_spec, s_spec],
        out_shape=[jax.ShapeDtypeStruct((N_HEADS, t, HEAD_DIM), F32),
                   jax.ShapeDtypeStruct((N_HEADS, nc, HEAD_DIM, HEAD_DIM), F32)],
        scratch_shapes=[pltpu.VMEM((N_HEADS, HEAD_DIM, HEAD_DIM), F32)],
        compiler_params=_params(("arbitrary",)),
    )(q, k, gc, u, w, qk)


def delta_seq_bwd(q, k, gc, u, w, qk, s0, do, *, name):
    t = q.shape[1]
    nc = t // CHUNK

    def body(q_ref, k_ref, g_ref, u_ref, w_ref, qk_ref, s0_ref, do_ref,
             dq_ref, dk_ref, dg_ref, du_ref, dw_ref, dqk_ref, ds_ref):
        @pl.when(pl.program_id(0) == 0)
        def _():
            ds_ref[...] = jnp.zeros_like(ds_ref)

        _, vjp = jax.vjp(delta_step, s0_ref[...], q_ref[...], k_ref[...], g_ref[...], u_ref[...], w_ref[...], qk_ref[...])
        ds, dq, dk, dg, du, dw, dqk = vjp((do_ref[...], ds_ref[...]))
        for ref, val in zip((ds_ref, dq_ref, dk_ref, dg_ref, du_ref, dw_ref, dqk_ref), (ds, dq, dk, dg, du, dw, dqk)):
            ref[...] = val

    spec = _delta_specs(t)
    x_spec, g_spec, qk_spec = (spec(CHUNK, w, lambda c: nc - 1 - c) for w in (HEAD_DIM, 1, CHUNK))
    s_spec = pl.BlockSpec((N_HEADS, None, HEAD_DIM, HEAD_DIM), lambda c: (0, nc - 1 - c, 0, 0))
    return pl.pallas_call(
        body, name=name, grid=(nc,), in_specs=[x_spec, x_spec, g_spec, x_spec, x_spec, qk_spec, s_spec, x_spec],
        out_specs=[x_spec, x_spec, g_spec, x_spec, x_spec, qk_spec],
        out_shape=[jax.ShapeDtypeStruct((N_HEADS, t, w_), F32) for w_ in (HEAD_DIM, HEAD_DIM, 1, HEAD_DIM, HEAD_DIM, CHUNK)],
        scratch_shapes=[pltpu.VMEM((N_HEADS, HEAD_DIM, HEAD_DIM), F32)],
        compiler_params=_params(("arbitrary",)),
    )(q, k, gc, u, w, qk, s0, do)


def delta_intra_bwd(q, k, v, gc, beta, du, dw, dqk, dq_s, dk_s, dg_s, *, name):
    t = q.shape[1]
    per = min(INTRA_CHUNKS, t // CHUNK)
    rows, nb = per * CHUNK, N_HEADS * per

    def body(q_ref, k_ref, v_ref, g_ref, b_ref, du_ref, dw_ref, dqk_ref, dqs_ref, dks_ref, dgs_ref,
             dq_ref, dk_ref, dv_ref, dg_ref, db_ref):
        def chunks(r):
            return r[...].reshape(nb, CHUNK, r.shape[-1])
        _, vjp = jax.vjp(delta_intra, *[chunks(r) for r in (q_ref, k_ref, v_ref, g_ref, b_ref)])
        dq, dk, dv, dg, db = vjp(tuple(chunks(r) for r in (du_ref, dw_ref, dqk_ref)))
        dq_ref[...] = dq.reshape(dq_ref.shape) + dqs_ref[...]
        dk_ref[...] = dk.reshape(dk_ref.shape) + dks_ref[...]
        dv_ref[...] = dv.reshape(dv_ref.shape)
        dg_ref[...] = dg.reshape(dg_ref.shape) + dgs_ref[...]
        db_ref[...] = db.reshape(db_ref.shape)

    spec = _delta_specs(t)
    x_spec, g_spec, qk_spec = (spec(rows, w, lambda i: i) for w in (HEAD_DIM, 1, CHUNK))
    return pl.pallas_call(
        body, name=name, grid=(t // rows,),
        in_specs=[x_spec] * 3 + [g_spec] * 2 + [x_spec, x_spec, qk_spec, x_spec, x_spec, g_spec],
        out_specs=[x_spec] * 3 + [g_spec] * 2,
        out_shape=[jax.ShapeDtypeStruct((N_HEADS, t, HEAD_DIM), F32)] * 3 + [jax.ShapeDtypeStruct((N_HEADS, t, 1), F32)] * 2,
        compiler_params=_params(("parallel",)),
    )(q, k, v, gc, beta, du, dw, dqk, dq_s, dk_s, dg_s)


_V_BLOCK = 2 * N_HEADS
FOX_GROUPS = 8


def _fox_groups(t):
    nq = t // Q_BLOCK
    per = max(1, nq // FOX_GROUPS)
    return [(g0, per, (g0 + per) * Q_BLOCK) for g0 in range(0, nq, per)]


def fox_attn_fwd(q, k, proj, fq, fk, *, name):
    t = q.shape[0]

    def body(q_ref, k_ref, v_ref, fq_ref, fk_ref, o_ref):
        for g0, per, keys in _fox_groups(t):
            def block(j, carry, g0=g0, keys=keys):
                rows = pl.ds(pl.multiple_of((g0 + j) * Q_BLOCK, Q_BLOCK), Q_BLOCK)
                o_ref[rows, :] = fox_block(q_ref[rows, :], k_ref[0:keys, :], v_ref[0:keys, :], fq_ref[rows, :],
                                           fk_ref[:, 0:keys], (g0 + j) * Q_BLOCK)
                return carry
            lax.fori_loop(0, per, block, 0)

    x_spec = pl.BlockSpec((t, HEAD_DIM), lambda h: (0, h))
    v_spec = pl.BlockSpec((t, HEAD_DIM), lambda h: (0, _V_BLOCK + h))
    fq_spec = pl.BlockSpec((None, t, 1), lambda h: (h, 0, 0))
    fk_spec = pl.BlockSpec((None, 1, t), lambda h: (h, 0, 0))
    return pl.pallas_call(
        body, name=name, grid=(N_HEADS,), in_specs=[x_spec, x_spec, v_spec, fq_spec, fk_spec], out_specs=x_spec,
        out_shape=jax.ShapeDtypeStruct((t, D_MODEL), F32), compiler_params=_params(("parallel",)),
    )(q, k, proj, fq, fk)


def fox_attn_bwd(q, k, proj, fq, fk, do, *, name):
    t = q.shape[0]

    def body(q_ref, k_ref, v_ref, fq_ref, fk_ref, do_ref, dq_ref, dk_ref, dv_ref, dfq_ref, dfk_ref):
        dk_ref[...] = jnp.zeros_like(dk_ref)
        dv_ref[...] = jnp.zeros_like(dv_ref)
        dfk_ref[...] = jnp.zeros_like(dfk_ref)
        for g0, per, keys in _fox_groups(t):
            def block(j, carry, g0=g0, keys=keys):
                rows = pl.ds(pl.multiple_of((g0 + j) * Q_BLOCK, Q_BLOCK), Q_BLOCK)
                f = functools.partial(fox_block, qpos0=(g0 + j) * Q_BLOCK)
                _, vjp = jax.vjp(f, q_ref[rows, :], k_ref[0:keys, :], v_ref[0:keys, :], fq_ref[rows, :], fk_ref[:, 0:keys])
                dq, dk, dv, dfq, dfk = vjp(do_ref[rows, :])
                dq_ref[rows, :] = dq
                dfq_ref[rows, :] = dfq
                dk_ref[0:keys, :] += dk
                dv_ref[0:keys, :] += dv
                dfk_ref[:, 0:keys] += dfk
                return carry
            lax.fori_loop(0, per, block, 0)

    x_spec = pl.BlockSpec((t, HEAD_DIM), lambda h: (0, h))
    v_spec = pl.BlockSpec((t, HEAD_DIM), lambda h: (0, _V_BLOCK + h))
    fq_spec = pl.BlockSpec((None, t, 1), lambda h: (h, 0, 0))
    fk_spec = pl.BlockSpec((None, 1, t), lambda h: (h, 0, 0))
    return pl.pallas_call(
        body, name=name, grid=(N_HEADS,), in_specs=[x_spec, x_spec, v_spec, fq_spec, fk_spec, x_spec],
        out_specs=[x_spec, x_spec, x_spec, fq_spec, fk_spec],
        out_shape=[jax.ShapeDtypeStruct((t, D_MODEL), F32)] * 3
        + [jax.ShapeDtypeStruct((N_HEADS, t, 1), F32), jax.ShapeDtypeStruct((N_HEADS, 1, t), F32)],
        compiler_params=_params(("parallel",)),
    )(q, k, proj, fq, fk, do)


def memkv_fwd(mem, mnw, wkv, mknw, *, name):
    n = mem.shape[0]

    def body(mem_ref, mnw_ref, w_ref, mknw_ref, mk_ref, mv_ref):
        mk, mv = memkv_fn(mem_ref[...], mnw_ref[...], w_ref[...], mknw_ref[...])
        mk_ref[...] = mk
        mv_ref[...] = mv

    return pl.pallas_call(
        body, name=name, out_shape=[jax.ShapeDtypeStruct((n, MEM_WIDTH), F32)] * 2,
        compiler_params=pltpu.CompilerParams(vmem_limit_bytes=VMEM_LIMIT),
    )(mem, mnw, wkv, mknw)


def memkv_bwd(mem, mnw, wkv, mknw, dmk, dmv, *, name):
    def body(mem_ref, mnw_ref, w_ref, mknw_ref, dmk_ref, dmv_ref, dmnw_ref, dw_ref, dmknw_ref):
        f = functools.partial(memkv_fn, mem_ref[...])
        _, vjp = jax.vjp(f, mnw_ref[...], w_ref[...].astype(F32), mknw_ref[...])
        dmnw, dw, dmknw = vjp((dmk_ref[...], dmv_ref[...]))
        dmnw_ref[...] = dmnw
        dw_ref[...] = dw.astype(dw_ref.dtype)
        dmknw_ref[...] = dmknw

    return pl.pallas_call(
        body, name=name,
        out_shape=[jax.ShapeDtypeStruct(mnw.shape, F32), jax.ShapeDtypeStruct(wkv.shape, BF16), jax.ShapeDtypeStruct(mknw.shape, F32)],
        compiler_params=pltpu.CompilerParams(vmem_limit_bytes=VMEM_LIMIT),
    )(mem, mnw, wkv, mknw, dmk, dmv)


def _row(v, width=None):
    v = v.reshape(1, -1)
    if width is not None and v.shape[1] < width:
        v = jnp.pad(v, ((0, 0), (0, width - v.shape[1])))
    return v


def _head_cols(a):
    return a[:, :N_HEADS].T[:, :, None]


def _lanes_from_heads(*cols):
    t = cols[0].shape[1]
    parts = [c[:, :, 0].T for c in cols]
    parts.append(jnp.zeros((t, LANES - N_HEADS * len(cols)), F32))
    return jnp.concatenate(parts, axis=1)


def _norm_fwd(x, w, name):
    return rows_call(lambda x, w: rms(x, w), [x], [w], [(D_MODEL, BF16)], [], tm=512, name=name)[0]


def _norm_bwd(x, w, dh, dx_in, name):
    def fn(x, dh, dx_in, w):
        _, vjp = jax.vjp(rms, x, w)
        dx, dw = vjp(dh)
        return dx + dx_in, dw
    return rows_call(fn, [x, dh, dx_in], [w], [(D_MODEL, F32)], [(1, D_MODEL)], tm=512, name=name)


FF_PIECE = D_FF // N_DEV
_TILE = 512


def _piece_spec(shape, index):
    return pl.BlockSpec((None, None) + shape, index)


def _mlp_fwd(x, n2w, w1, w2, layer):
    h2 = _norm_fwd(x, n2w, f"norm2_fwd_{layer}")
    u = matmul(h2, w1, name=f"mlp1_fwd_{layer}", b_view=(
        D_MODEL, D_FF, D_MODEL, FF_PIECE, _piece_spec((D_MODEL, FF_PIECE), lambda i, j, kk: (j, layer, kk, 0))))
    a1 = rows_call(lambda u: jnp.square(jnp.maximum(u, 0.0)), [u], [], [(D_FF, BF16)], [], tm=256, name=f"act_fwd_{layer}")[0]
    y = matmul(a1, w2, add=x, name=f"mlp2_fwd_{layer}", b_view=(
        D_FF, D_MODEL, FF_PIECE, _TILE, _piece_spec((FF_PIECE, _TILE), lambda i, j, kk: (kk, layer, 0, j))))
    return y, (x, h2, u, a1)


def _mlp_bwd(dy, res, n2w, w1, w2, layer, dw1_into, dw2_into):
    x, h2, u, a1 = res
    da1 = matmul(dy, w2, tb=True, name=f"mlp2_dx_{layer}", b_view=(
        D_MODEL, D_FF, D_MODEL, FF_PIECE, _piece_spec((FF_PIECE, D_MODEL), lambda i, j, kk: (j, layer, 0, kk))))
    dw2 = matmul(a1, dy, ta=True, name=f"mlp2_dw_{layer}", out_dtype=BF16, out_view=(
        w2.shape, FF_PIECE, _TILE, _piece_spec((FF_PIECE, _TILE), lambda i, j, kk: (i, layer, 0, j)), dw2_into))
    du = rows_call(lambda u, da: da * (2.0 * jnp.maximum(u, 0.0)), [u, da1], [], [(D_FF, BF16)], [], tm=256,
                   name=f"act_bwd_{layer}")[0]
    dh2 = matmul(du, w1, tb=True, name=f"mlp1_dx_{layer}", b_view=(
        D_FF, D_MODEL, FF_PIECE, _TILE, _piece_spec((_TILE, FF_PIECE), lambda i, j, kk: (kk, layer, j, 0))))
    dw1 = matmul(h2, du, ta=True, name=f"mlp1_dw_{layer}", out_dtype=BF16, out_view=(
        w1.shape, _TILE, FF_PIECE, _piece_spec((_TILE, FF_PIECE), lambda i, j, kk: (j, layer, i, 0)), dw1_into))
    dx, dn2w = _norm_bwd(x, n2w, dh2, dy, f"norm2_bwd_{layer}")
    return dx, dw1, dw2, dn2w


def _wout_dw(cat, dx, layer, into):
    return matmul(cat, dx, ta=True, name=f"wout_dw_{layer}", out_dtype=BF16, out_view=(
        (2, D_MODEL + MEM_WIDTH, D_MODEL), _TILE, _TILE, pl.BlockSpec((None, _TILE, _TILE), lambda i, j, kk: (layer, i, j)),
        into))


def _in_proj_bwd(h, dmain, dsmall, w_main, w_small, tag):
    dh = matmul(dmain, w_main, tb=True, name=f"inproj_dx_main_{tag}")
    dh = matmul(dsmall, w_small, tb=True, add=dh, name=f"inproj_dx_small_{tag}")
    dw_main = matmul(h, dmain, ta=True, out_dtype=BF16, name=f"inproj_dw_main_{tag}")
    dw_small = matmul(h, dsmall, ta=True, out_dtype=BF16, name=f"inproj_dw_small_{tag}")
    return dh, dw_main, dw_small


def local_step(x, mem, target, w):
    t = x.shape[0]
    n_mem = mem.shape[0]
    g = {}

    mnw, mknw = _row(w["mem_norm_w"]), _row(w["mem_k_norm_w"])
    mk, mv = memkv_fwd(mem, mnw, w["w_mem_kv"], mknw, name="memkv_fwd")

    n1w0, n2w0 = _row(w["norm1_w"][0]), _row(w["norm2_w"][0])
    alog, dtb = _row(w["dn_a_log"][0], LANES), _row(w["dn_dt_bias"][0], LANES)
    onw, mqw0 = _row(w["dn_o_norm_w"][0]), _row(w["memq_norm_w"][0])
    x0 = x
    h0 = _norm_fwd(x0, n1w0, "norm1_fwd_0")
    pm0 = matmul(h0, w["dn_main"], name="inproj_main_0")
    ps0 = matmul(h0, w["dn_ab"], name="inproj_small_0")
    gates = rows_call(dn_gates_fn, [ps0], [alog, dtb], [(LANES, F32)], [], tm=512, name="dn_gates_fwd")[0]
    gc, beta = _head_cols(gates), _head_cols(gates[:, N_HEADS:])
    q0, k0, v0 = dn_prep_fwd(pm0, w["dn_conv_w"], name="dn_prep_fwd")
    u0, w0, qk0 = delta_intra_fwd(q0, k0, v0, gc, beta, name="delta_intra_fwd")
    o0, s_start = delta_seq_fwd(q0, k0, gc, u0, w0, qk0, name="delta_seq_fwd")
    cat0 = rows_call(dn_out_fn, [o0, (pm0, D_MODEL, 3), (pm0, MEM_WIDTH, 8)], [onw, mqw0, mk, mv],
                     [(D_MODEL + MEM_WIDTH, BF16)], [], tm=256, name="dn_out_fwd")[0]
    x1 = matmul(cat0, w["w_out"][0], add=x0, name="wout_fwd_0")
    x2, mlp_res0 = _mlp_fwd(x1, n2w0, w["w_mlp1"], w["w_mlp2"], 0)

    n1w1, n2w1 = _row(w["norm1_w"][1]), _row(w["norm2_w"][1])
    fbias = _row(w["fox_f_bias"][0], LANES)
    qnw, knw, mqw1 = _row(w["fox_q_norm_w"][0]), _row(w["fox_k_norm_w"][0]), _row(w["memq_norm_w"][1])
    h1 = _norm_fwd(x2, n1w1, "norm1_fwd_1")
    pm1 = matmul(h1, w["fox_main"], name="inproj_main_1")
    ps1 = matmul(h1, w["fox_f"], name="inproj_small_1")
    fcum = rows_call(fox_fcum_fn, [ps1], [fbias], [(LANES, F32)], [], tm=t, name="fox_fcum_fwd")[0]
    fq = _head_cols(fcum)
    fk = jnp.swapaxes(fq, 1, 2)
    q1, k1 = rows_call(fox_qk_fn, [(pm1, D_MODEL, 0), (pm1, D_MODEL, 1)], [qnw, knw], [(D_MODEL, F32)] * 2, [], tm=256,
                       name="fox_qk_fwd")
    o1 = fox_attn_fwd(q1, k1, pm1, fq, fk, name="fox_attn_fwd")
    cat1 = rows_call(fox_out_fn, [o1, (pm1, D_MODEL, 3), (pm1, MEM_WIDTH, 8)], [mqw1, mk, mv],
                     [(D_MODEL + MEM_WIDTH, BF16)], [], tm=256, name="fox_out_fwd")[0]
    x3 = matmul(cat1, w["w_out"][1], add=x2, name="wout_fwd_1")
    y, mlp_res1 = _mlp_fwd(x3, n2w1, w["w_mlp1"], w["w_mlp2"], 1)

    def loss_fn(y, tgt):
        e = y - tgt
        return e * (1.0 / D_MODEL), jnp.sum(jnp.sum(e * e, axis=1, keepdims=True), axis=0, keepdims=True)
    dy, sq = rows_call(loss_fn, [y, target], [], [(D_MODEL, F32)], [(1, 1)], tm=512, name="loss")
    loss = sq[0, 0] * (0.5 / D_MODEL)

    dx3, dw1, dw2, dn2w1 = _mlp_bwd(dy, mlp_res1, n2w1, w["w_mlp1"], w["w_mlp2"], 1, None, None)
    dcat1 = matmul(dx3, w["w_out"][1], tb=True, name="wout_dx_1")
    dwo = _wout_dw(cat1, dx3, 1, None)
    do1, dgate1, dqm1, dmqw1, dmk1, dmv1 = rows_call(
        functools.partial(vjp_rows(fox_out_fn, 3, (True, True, True)), n_row=3, n_ct=1),
        [o1, (pm1, D_MODEL, 3), (pm1, MEM_WIDTH, 8), dcat1], [mqw1, mk, mv],
        [(D_MODEL, F32), (D_MODEL, F32), (MEM_WIDTH, F32)], [(1, HEAD_DIM), (n_mem, MEM_WIDTH), (n_mem, MEM_WIDTH)],
        tm=256, name="fox_out_bwd")
    dq1, dk1, dv1, dfq, dfk = fox_attn_bwd(q1, k1, pm1, fq, fk, do1, name="fox_attn_bwd")
    dqraw1, dkraw1, dqnw, dknw = rows_call(
        functools.partial(vjp_rows(fox_qk_fn, 2, (True, True)), n_row=2, n_ct=2),
        [(pm1, D_MODEL, 0), (pm1, D_MODEL, 1), dq1, dk1], [qnw, knw],
        [(D_MODEL, F32)] * 2, [(1, HEAD_DIM)] * 2, tm=256, name="fox_qk_bwd")
    dfcum = _lanes_from_heads(dfq + jnp.swapaxes(dfk, 1, 2))
    dps1, dfbias = rows_call(
        functools.partial(vjp_rows(fox_fcum_fn, 1, (True,)), n_row=1, n_ct=1),
        [ps1, dfcum], [fbias], [(LANES, F32)], [(1, LANES)], tm=t, name="fox_fcum_bwd")
    dpm1 = jnp.concatenate([dqraw1, dkraw1, dv1, dgate1, dqm1], axis=1)
    dh1, dwmain1, dwsmall1 = _in_proj_bwd(h1, dpm1, dps1, w["fox_main"], w["fox_f"], "1")
    dx2, dn1w1 = _norm_bwd(x2, n1w1, dh1, dx3, "norm1_bwd_1")

    dx1, dw1, dw2, dn2w0 = _mlp_bwd(dx2, mlp_res0, n2w0, w["w_mlp1"], w["w_mlp2"], 0, dw1, dw2)
    dcat0 = matmul(dx1, w["w_out"][0], tb=True, name="wout_dx_0")
    dwo = _wout_dw(cat0, dx1, 0, dwo)
    do0, dz0, dqm0, donw, dmqw0, dmk0, dmv0 = rows_call(
        functools.partial(vjp_rows(dn_out_fn, 3, (True, True, True, True)), n_row=3, n_ct=1),
        [o0, (pm0, D_MODEL, 3), (pm0, MEM_WIDTH, 8), dcat0], [onw, mqw0, mk, mv],
        [((N_HEADS, HEAD_DIM), F32), (D_MODEL, F32), (MEM_WIDTH, F32)],
        [(1, HEAD_DIM), (1, HEAD_DIM), (n_mem, MEM_WIDTH), (n_mem, MEM_WIDTH)], tm=256, name="dn_out_bwd")
    dq_s, dk_s, dg_s, du0, dw0, dqk0 = delta_seq_bwd(q0, k0, gc, u0, w0, qk0, s_start, do0, name="delta_seq_bwd")
    dq0, dk0, dv0, dgc, dbeta = delta_intra_bwd(q0, k0, v0, gc, beta, du0, dw0, dqk0, dq_s, dk_s, dg_s,
                                                name="delta_intra_bwd")
    dxq, dxk, dxv, dcq, dck, dcv = dn_prep_bwd(pm0, w["dn_conv_w"], dq0, dk0, dv0, name="dn_prep_bwd")
    dconv = jnp.concatenate([dcq, dck, dcv], axis=1)
    dps0, dalog, ddtb = rows_call(
        functools.partial(vjp_rows(dn_gates_fn, 1, (True, True)), n_row=1, n_ct=1),
        [ps0, _lanes_from_heads(dgc, dbeta)], [alog, dtb], [(LANES, F32)], [(1, LANES)] * 2, tm=512, name="dn_gates_bwd")
    dpm0 = jnp.concatenate([dxq, dxk, dxv, dz0, dqm0], axis=1)
    dh0, dwmain0, dwsmall0 = _in_proj_bwd(h0, dpm0, dps0, w["dn_main"], w["dn_ab"], "0")
    grad_x, dn1w0 = _norm_bwd(x0, n1w0, dh0, dx1, "norm1_bwd_0")

    dmnw, dwkv, dmknw = memkv_bwd(mem, mnw, w["w_mem_kv"], mknw, dmk0 + dmk1, dmv0 + dmv1, name="memkv_bwd")

    g["mem_norm_w"] = dmnw[0]
    g["w_mem_kv"] = dwkv
    g["mem_k_norm_w"] = dmknw[0]
    g["norm1_w"] = jnp.concatenate([dn1w0, dn1w1], axis=0)
    g["dn_main"], g["dn_ab"] = dwmain0, dwsmall0
    g["dn_conv_w"] = dconv
    g["dn_a_log"] = dalog[:, :N_HEADS]
    g["dn_dt_bias"] = ddtb[:, :N_HEADS]
    g["dn_o_norm_w"] = donw
    g["fox_main"], g["fox_f"] = dwmain1, dwsmall1
    g["fox_f_bias"] = dfbias[:, :N_HEADS]
    g["fox_q_norm_w"] = dqnw
    g["fox_k_norm_w"] = dknw
    g["memq_norm_w"] = jnp.concatenate([dmqw0, dmqw1], axis=0)
    g["w_out"] = dwo
    g["norm2_w"] = jnp.concatenate([dn2w0, dn2w1], axis=0)
    g["w_mlp1"] = dw1
    g["w_mlp2"] = dw2
    return loss, grad_x, g


WEIGHTS = ["mem_norm_w", "w_mem_kv", "mem_k_norm_w", "norm1_w", "dn_w_in", "dn_conv_w", "dn_a_log", "dn_dt_bias",
           "dn_o_norm_w", "fox_w_in", "fox_f_bias", "fox_q_norm_w", "fox_k_norm_w", "memq_norm_w", "w_out", "norm2_w",
           "w_mlp1", "w_mlp2"]
DN_IN = 4 * D_MODEL + 2 * N_HEADS + MEM_WIDTH
FOX_IN = 4 * D_MODEL + N_HEADS + MEM_WIDTH
GATE_END = 4 * D_MODEL
OUT_IN = D_MODEL + MEM_WIDTH
BIG = [("w_mem_kv", D_MODEL // N_DEV, D_MODEL), ("dn_w_in", D_MODEL, DN_IN // N_DEV), ("fox_w_in", D_MODEL, FOX_IN // N_DEV),
       ("dn_conv_w", CONV_WIDTH, 3 * D_MODEL // N_DEV), ("w_out", 2 * OUT_IN // N_DEV, D_MODEL),
       ("w_mlp1", 2 * D_MODEL, FF_PIECE), ("w_mlp2", 2 * FF_PIECE, D_MODEL)]
SMALL = [("mem_norm_w", (D_MODEL,), D_MODEL), ("mem_k_norm_w", (HEAD_DIM,), HEAD_DIM), ("norm1_w", (2, D_MODEL), 2 * D_MODEL),
         ("dn_a_log", (1, N_HEADS), LANES), ("dn_dt_bias", (1, N_HEADS), LANES), ("dn_o_norm_w", (1, HEAD_DIM), HEAD_DIM),
         ("fox_f_bias", (1, N_HEADS), LANES), ("fox_q_norm_w", (1, HEAD_DIM), HEAD_DIM), ("fox_k_norm_w", (1, HEAD_DIM), HEAD_DIM),
         ("memq_norm_w", (2, HEAD_DIM), 2 * HEAD_DIM), ("norm2_w", (2, D_MODEL), 2 * D_MODEL)]
SMALL_ROWS = 16


def pack_small(p):
    flat = jnp.concatenate([jnp.pad(p[n].reshape(-1), (0, ln - math.prod(sh))) for n, sh, ln in SMALL])
    return jnp.pad(flat, (0, SMALL_ROWS * PACK_W - flat.shape[0])).reshape(SMALL_ROWS, PACK_W)


def unpack_small(pk):
    flat, off, out = pk.reshape(-1), 0, {}
    for n, sh, ln in SMALL:
        out[n] = flat[off:off + math.prod(sh)].reshape(sh)
        off += ln
    return out


def shard_matrices(p):
    return [p[n].reshape(r, c) for n, r, c in BIG]


def step_weights(gathered, p):
    w = {n: p[n] for n, _, _ in SMALL}
    g = {n: a for (n, _, _), a in zip(BIG, gathered)}
    w["w_mem_kv"] = g["w_mem_kv"].reshape(D_MODEL, D_MODEL)
    dn = g["dn_w_in"].transpose(1, 0, 2).reshape(D_MODEL, DN_IN)
    fox = g["fox_w_in"].transpose(1, 0, 2).reshape(D_MODEL, FOX_IN)
    w["dn_main"] = jnp.concatenate([dn[:, :GATE_END], dn[:, GATE_END + 2 * N_HEADS:]], axis=1)
    w["dn_ab"] = jnp.pad(dn[:, GATE_END:GATE_END + 2 * N_HEADS], ((0, 0), (0, LANES - 2 * N_HEADS)))
    w["fox_main"] = jnp.concatenate([fox[:, :GATE_END], fox[:, GATE_END + N_HEADS:]], axis=1)
    w["fox_f"] = jnp.pad(fox[:, GATE_END:GATE_END + N_HEADS], ((0, 0), (0, LANES - N_HEADS)))
    w["dn_conv_w"] = g["dn_conv_w"].transpose(1, 0, 2).reshape(CONV_WIDTH, 3 * D_MODEL)
    w["w_out"] = g["w_out"].reshape(N_DEV, 2, OUT_IN // N_DEV, D_MODEL).transpose(1, 0, 2, 3).reshape(2, OUT_IN, D_MODEL)
    w["w_mlp1"] = g["w_mlp1"].reshape(N_DEV, 2, D_MODEL, FF_PIECE)
    w["w_mlp2"] = g["w_mlp2"].reshape(N_DEV, 2, FF_PIECE, D_MODEL)
    return w


def grad_pieces(g):
    def in_w(main, small, n_small, width):
        full = jnp.concatenate([main[:, :GATE_END], small[:, :n_small], main[:, GATE_END:]], axis=1)
        return full.reshape(D_MODEL, N_DEV, width // N_DEV).transpose(1, 0, 2)

    out = {
        "w_mem_kv": g["w_mem_kv"].reshape(N_DEV, D_MODEL // N_DEV, D_MODEL),
        "dn_w_in": in_w(g["dn_main"], g["dn_ab"], 2 * N_HEADS, DN_IN),
        "fox_w_in": in_w(g["fox_main"], g["fox_f"], N_HEADS, FOX_IN),
        "dn_conv_w": g["dn_conv_w"].reshape(CONV_WIDTH, N_DEV, -1).transpose(1, 0, 2).astype(BF16),
        "w_out": g["w_out"].reshape(2, N_DEV, OUT_IN // N_DEV, D_MODEL).transpose(1, 0, 2, 3).reshape(N_DEV, -1, D_MODEL),
        "w_mlp1": g["w_mlp1"].reshape(N_DEV, 2 * D_MODEL, FF_PIECE),
        "w_mlp2": g["w_mlp2"].reshape(N_DEV, 2 * FF_PIECE, D_MODEL),
    }
    return [out[n] for n, _, _ in BIG]


_HBM = pl.BlockSpec(memory_space=pltpu.HBM)


def _place():
    return lax.axis_index("x"), lax.axis_index("y"), lax.axis_index("c")


def all_gather(xs, *, name):
    n = len(xs)

    def body(*refs):
        x_refs, out_refs = refs[:n], refs[n:2 * n]
        send_sems, recv_sems, local_sems = refs[2 * n:]
        x, y, c = _place()
        me, sibling = (x, y, c), (x, y, 1 - c)
        chips = [(1 - x, y), (x, 1 - y), (1 - x, 1 - y)]

        def copy(a, k, block, to, src=None):
            px, py, pc = block
            dst = out_refs[a].at[4 * px + 2 * py + pc]
            return pltpu.make_async_remote_copy(
                src_ref=dst if src is None else src, dst_ref=dst,
                send_sem=send_sems.at[a, k], recv_sem=recv_sems.at[a, k], device_id=to, device_id_type=MESH)

        mine = [pltpu.make_async_copy(x_refs[a], out_refs[a].at[4 * x + 2 * y + c], local_sems.at[a]) for a in range(n)]
        first = [copy(a, 0, me, sibling, src=x_refs[a]) for a in range(n)]
        first += [copy(a, 1 + j, me, (*chip, c), src=x_refs[a]) for j, chip in enumerate(chips) for a in range(n)]
        for cp in mine + first:
            cp.start()
        passed = []
        for j, chip in enumerate(chips):
            for a in range(n):
                copy(a, 1 + j, (*chip, c), me).wait_recv()
                passed.append(copy(a, 4 + j, (*chip, c), sibling))
                passed[-1].start()
        for a in range(n):
            copy(a, 0, sibling, me).wait_recv()
        for j, chip in enumerate(chips):
            for a in range(n):
                copy(a, 4 + j, (*chip, 1 - c), me).wait_recv()
        for cp in first + passed:
            cp.wait_send()
        for cp in mine:
            cp.wait()

    return pl.pallas_call(
        body, name=name, out_shape=[jax.ShapeDtypeStruct((N_DEV,) + a.shape, a.dtype) for a in xs],
        in_specs=[_HBM] * n, out_specs=[_HBM] * n,
        scratch_shapes=[pltpu.SemaphoreType.DMA((n, 7)), pltpu.SemaphoreType.DMA((n, 7)), pltpu.SemaphoreType.DMA((n,))],
    )(*xs)


def sibling_exchange(gs, *, name):
    n = len(gs)

    def body(*refs):
        g_refs, out_refs = refs[:n], refs[n:2 * n]
        send_sems, recv_sems = refs[2 * n:]
        x, y, c = _place()
        copies = [pltpu.make_async_remote_copy(
            src_ref=g_refs[a].at[2 * k + 1 - c], dst_ref=out_refs[a].at[k], send_sem=send_sems.at[a, k],
            recv_sem=recv_sems.at[a, k], device_id=(x, y, 1 - c), device_id_type=MESH) for a in range(n) for k in range(4)]
        for cp in copies:
            cp.start()
        for cp in copies:
            cp.wait_recv()
        for cp in copies:
            cp.wait_send()

    return pl.pallas_call(
        body, name=name, out_shape=[jax.ShapeDtypeStruct((4,) + g.shape[1:], g.dtype) for g in gs],
        in_specs=[_HBM] * n, out_specs=[_HBM] * n,
        scratch_shapes=[pltpu.SemaphoreType.DMA((n, 4)), pltpu.SemaphoreType.DMA((n, 4))],
    )(*gs)


def _row_tile(rows):
    return _pick(rows, (512, 256, 128))


def pair_sum(g, got, *, name):
    _, rows, cols = g.shape
    tile = _row_tile(rows)
    c = lax.axis_index("c").astype(jnp.int32).reshape(1)

    def body(c_ref, a_ref, b_ref, o_ref):
        o_ref[...] = (a_ref[...].astype(F32) + b_ref[...].astype(F32)).astype(o_ref.dtype)

    grid_spec = pltpu.PrefetchScalarGridSpec(
        num_scalar_prefetch=1, grid=(4, rows // tile),
        in_specs=[pl.BlockSpec((None, tile, cols), lambda k, i, c_ref: (2 * k + c_ref[0], i, 0)),
                  pl.BlockSpec((None, tile, cols), lambda k, i, c_ref: (k, i, 0))],
        out_specs=pl.BlockSpec((None, tile, cols), lambda k, i, c_ref: (k, i, 0)))
    return pl.pallas_call(
        body, name=name, grid_spec=grid_spec, out_shape=jax.ShapeDtypeStruct((4, rows, cols), g.dtype),
        compiler_params=_params(("parallel", "parallel")),
    )(c, g, got)


def chip_exchange(hs, *, name):
    n = len(hs)

    def body(*refs):
        h_refs, out_refs = refs[:n], refs[n:2 * n]
        send_sems, recv_sems, local_sems = refs[2 * n:]
        x, y, c = _place()
        mine = 2 * x + y
        chips = [(1 - x, y), (x, 1 - y), (1 - x, 1 - y)]
        keep = [pltpu.make_async_copy(h_refs[a].at[mine], out_refs[a].at[mine], local_sems.at[a]) for a in range(n)]
        sends = [pltpu.make_async_remote_copy(
            src_ref=h_refs[a].at[2 * qx + qy], dst_ref=out_refs[a].at[mine], send_sem=send_sems.at[a, j],
            recv_sem=recv_sems.at[a, j], device_id=(qx, qy, c), device_id_type=MESH)
            for j, (qx, qy) in enumerate(chips) for a in range(n)]
        for cp in keep + sends:
            cp.start()
        for j, (qx, qy) in enumerate(chips):
            for a in range(n):
                pltpu.make_async_remote_copy(
                    src_ref=h_refs[a].at[mine], dst_ref=out_refs[a].at[2 * qx + qy], send_sem=send_sems.at[a, j],
                    recv_sem=recv_sems.at[a, j], device_id=(qx, qy, c), device_id_type=MESH).wait_recv()
        for cp in sends:
            cp.wait_send()
        for cp in keep:
            cp.wait()

    return pl.pallas_call(
        body, name=name, out_shape=[jax.ShapeDtypeStruct(h.shape, h.dtype) for h in hs],
        in_specs=[_HBM] * n, out_specs=[_HBM] * n,
        scratch_shapes=[pltpu.SemaphoreType.DMA((n, 3)), pltpu.SemaphoreType.DMA((n, 3)), pltpu.SemaphoreType.DMA((n,))],
    )(*hs)


def adamw(parts, w, m, v, *, name):
    n, rows, cols = parts.shape
    tile = _row_tile(rows)

    def body(p_ref, w_ref, m_ref, v_ref, g_ref, d_ref, mo_ref, vo_ref):
        g = p_ref[0].astype(F32)
        for i in range(1, n):
            g = g + p_ref[i].astype(F32)
        m_new = ADAM_B1 * m_ref[...] + (1.0 - ADAM_B1) * g
        v_new = ADAM_B2 * v_ref[...] + (1.0 - ADAM_B2) * jnp.square(g)
        m_hat = m_new / (1.0 - ADAM_B1 ** ADAM_STEP)
        v_hat = v_new / (1.0 - ADAM_B2 ** ADAM_STEP)
        g_ref[...] = g
        d_ref[...] = -ADAM_LR * (m_hat / (jnp.sqrt(v_hat) + ADAM_EPS) + ADAM_WD * w_ref[...])
        mo_ref[...] = m_new
        vo_ref[...] = v_new

    spec = pl.BlockSpec((tile, cols), lambda i: (i, 0))
    return pl.pallas_call(
        body, name=name, grid=(rows // tile,),
        in_specs=[pl.BlockSpec((n, tile, cols), lambda i: (0, i, 0)), spec, spec, spec], out_specs=[spec] * 4,
        out_shape=[jax.ShapeDtypeStruct((rows, cols), F32)] * 4, compiler_params=_params(("parallel",)),
    )(parts, w, m, v)


def kernel(x, mem, mem_norm_w, w_mem_kv, mem_k_norm_w, norm1_w, dn_w_in, dn_conv_w, dn_a_log, dn_dt_bias, dn_o_norm_w, fox_w_in, fox_f_bias, fox_q_norm_w, fox_k_norm_w, memq_norm_w, w_out, norm2_w, w_mlp1, w_mlp2, loss_target, m_mem_norm_w, m_w_mem_kv, m_mem_k_norm_w, m_norm1_w, m_dn_w_in, m_dn_conv_w, m_dn_a_log, m_dn_dt_bias, m_dn_o_norm_w, m_fox_w_in, m_fox_f_bias, m_fox_q_norm_w, m_fox_k_norm_w, m_memq_norm_w, m_w_out, m_norm2_w, m_w_mlp1, m_w_mlp2, v_mem_norm_w, v_w_mem_kv, v_mem_k_norm_w, v_norm1_w, v_dn_w_in, v_dn_conv_w, v_dn_a_log, v_dn_dt_bias, v_dn_o_norm_w, v_fox_w_in, v_fox_f_bias, v_fox_q_norm_w, v_fox_k_norm_w, v_memq_norm_w, v_w_out, v_norm2_w, v_w_mlp1, v_w_mlp2):
    p = dict(mem_norm_w=mem_norm_w, w_mem_kv=w_mem_kv, mem_k_norm_w=mem_k_norm_w, norm1_w=norm1_w, dn_w_in=dn_w_in,
             dn_conv_w=dn_conv_w, dn_a_log=dn_a_log, dn_dt_bias=dn_dt_bias, dn_o_norm_w=dn_o_norm_w, fox_w_in=fox_w_in,
             fox_f_bias=fox_f_bias, fox_q_norm_w=fox_q_norm_w, fox_k_norm_w=fox_k_norm_w, memq_norm_w=memq_norm_w,
             w_out=w_out, norm2_w=norm2_w, w_mlp1=w_mlp1, w_mlp2=w_mlp2)
    pm = dict(mem_norm_w=m_mem_norm_w, w_mem_kv=m_w_mem_kv, mem_k_norm_w=m_mem_k_norm_w, norm1_w=m_norm1_w,
              dn_w_in=m_dn_w_in, dn_conv_w=m_dn_conv_w, dn_a_log=m_dn_a_log, dn_dt_bias=m_dn_dt_bias,
              dn_o_norm_w=m_dn_o_norm_w, fox_w_in=m_fox_w_in, fox_f_bias=m_fox_f_bias, fox_q_norm_w=m_fox_q_norm_w,
              fox_k_norm_w=m_fox_k_norm_w, memq_norm_w=m_memq_norm_w, w_out=m_w_out, norm2_w=m_norm2_w, w_mlp1=m_w_mlp1,
              w_mlp2=m_w_mlp2)
    pv = dict(mem_norm_w=v_mem_norm_w, w_mem_kv=v_w_mem_kv, mem_k_norm_w=v_mem_k_norm_w, norm1_w=v_norm1_w,
              dn_w_in=v_dn_w_in, dn_conv_w=v_dn_conv_w, dn_a_log=v_dn_a_log, dn_dt_bias=v_dn_dt_bias,
              dn_o_norm_w=v_dn_o_norm_w, fox_w_in=v_fox_w_in, fox_f_bias=v_fox_f_bias, fox_q_norm_w=v_fox_q_norm_w,
              fox_k_norm_w=v_fox_k_norm_w, memq_norm_w=v_memq_norm_w, w_out=v_w_out, norm2_w=v_norm2_w, w_mlp1=v_w_mlp1,
              w_mlp2=v_w_mlp2)

    wire = [a if n == "dn_conv_w" else a.astype(BF16) for (n, _, _), a in zip(BIG, shard_matrices(p))]
    gathered = all_gather(wire, name="weights_all_gather")
    loss, grad_x, g = local_step(x[0], mem[0], loss_target[0], step_weights(gathered, p))
    loss = lax.psum(loss, ("x", "y", "c"))

    pieces = grad_pieces(g)
    got = sibling_exchange(pieces, name="grads_to_sibling")
    chip_sums = [pair_sum(a, b, name=f"grads_pair_sum_{n}") for (n, _, _), a, b in zip(BIG, pieces, got)]
    parts = chip_exchange(chip_sums, name="grads_to_chips")
    small_parts, = all_gather([pack_small(g)], name="small_grads_all_gather")

    results = {}
    for (n, _, _), part, w_, m_, v_ in zip(BIG, parts, shard_matrices(p), shard_matrices(pm), shard_matrices(pv)):
        results[n] = [o.reshape(p[n].shape) for o in adamw(part, w_, m_, v_, name=f"adamw_{n}")]
    small = [unpack_small(o) for o in adamw(small_parts, pack_small(p), pack_small(pm), pack_small(pv), name="adamw_small")]
    groups = [{**small[i], **{n: r[i] for n, r in results.items()}} for i in range(4)]
    return (loss, grad_x[None], *[grp[n] for grp in groups for n in WEIGHTS])
```

```python
import functools
import math

import jax
import jax.numpy as jnp
from jax import lax
from jax.experimental import pallas as pl
from jax.experimental.pallas import tpu as pltpu

F32 = jnp.float32
BF16 = jnp.bfloat16
HIGHEST = lax.Precision.HIGHEST

D_MODEL = 1024
HEAD_DIM = 128
N_HEADS = 8
MEM_HEADS = 4
MEM_WIDTH = MEM_HEADS * HEAD_DIM
D_FF = 4 * D_MODEL
CONV_WIDTH = 4
CHUNK = 64
Q_BLOCK = 128
EPS = 1e-6
SCALE = HEAD_DIM ** -0.5
MAIN_WIDTH = 4 * D_MODEL + MEM_WIDTH
LANES = 128
N_DEV = 8
PACK_W = 512

ADAM_LR = 0.001
ADAM_B1 = 0.9
ADAM_B2 = 0.999
ADAM_EPS = 1e-08
ADAM_WD = 0.01
ADAM_STEP = 10

VMEM_LIMIT = 56 * 2 ** 20
MESH = pl.DeviceIdType.MESH


def _bdot(a, b, dims):
    return lax.dot_general(a.astype(BF16), b.astype(BF16), (dims, ((), ())), preferred_element_type=F32)


@jax.custom_vjp
def mm(a, b):
    return _bdot(a, b, ((1,), (0,)))


@jax.custom_vjp
def mm_nt(a, b):
    return _bdot(a, b, ((1,), (1,)))


@jax.custom_vjp
def mm_tn(a, b):
    return _bdot(a, b, ((0,), (0,)))


mm.defvjp(lambda a, b: (mm(a, b), (a, b)), lambda r, g: (mm_nt(g, r[1]), mm_tn(r[0], g)))
mm_nt.defvjp(lambda a, b: (mm_nt(a, b), (a, b)), lambda r, g: (mm(g, r[1]), mm_tn(g, r[0])))
mm_tn.defvjp(lambda a, b: (mm_tn(a, b), (a, b)), lambda r, g: (mm_nt(r[1], g), mm(r[0], g)))


def hdot(a, b):
    return jnp.dot(a, b, precision=HIGHEST, preferred_element_type=F32)


def rms(x, w):
    return x * lax.rsqrt(jnp.mean(x * x, axis=-1, keepdims=True) + EPS) * w


def l2n(x):
    return x * lax.rsqrt(jnp.sum(x * x, axis=-1, keepdims=True) + EPS)


def _iota2(n, m):
    return lax.broadcasted_iota(jnp.int32, (n, m), 0), lax.broadcasted_iota(jnp.int32, (n, m), 1)


def _lower_ones(n):
    r, c = _iota2(n, n)
    return jnp.where(r >= c, 1.0, 0.0).astype(F32)


def _last_row(x):
    r = lax.broadcasted_iota(jnp.int32, x.shape, 0)
    return jnp.sum(jnp.where(r == x.shape[0] - 1, x, 0.0), axis=0, keepdims=True)


def _softmax_rows(z):
    m = lax.stop_gradient(jnp.max(z, axis=-1, keepdims=True))
    e = jnp.exp(z - m)
    return e / jnp.sum(e, axis=-1, keepdims=True)


_BNN = (((2,), (1,)), ((0,), (0,)))
_BNT = (((2,), (2,)), ((0,), (0,)))
_BTN = (((1,), (1,)), ((0,), (0,)))


def _bbdot(a, b, dims):
    return lax.dot_general(a.astype(BF16), b.astype(BF16), dims, preferred_element_type=F32)


@jax.custom_vjp
def bmm(a, b):
    return _bbdot(a, b, _BNN)


@jax.custom_vjp
def bmm_nt(a, b):
    return _bbdot(a, b, _BNT)


@jax.custom_vjp
def bmm_tn(a, b):
    return _bbdot(a, b, _BTN)


@jax.custom_vjp
def bmm_high(a, b):
    return lax.dot_general(a, b, _BNN, precision=lax.Precision.HIGH, preferred_element_type=F32)


bmm.defvjp(lambda a, b: (bmm(a, b), (a, b)), lambda r, g: (bmm_nt(g, r[1]), bmm_tn(r[0], g)))
bmm_nt.defvjp(lambda a, b: (bmm_nt(a, b), (a, b)), lambda r, g: (bmm(g, r[1]), bmm_tn(g, r[0])))
bmm_tn.defvjp(lambda a, b: (bmm_tn(a, b), (a, b)), lambda r, g: (bmm_nt(r[1], g), bmm(r[0], g)))
bmm_high.defvjp(lambda a, b: (bmm_high(a, b), (a, b)), lambda r, g: (bmm_nt(g, r[1]), bmm_tn(r[0], g)))

NEUMANN_HIGH_LEVELS = 2


def inv_unit_lower(a):
    n = a.shape[-1]
    r, c = _iota2(n, n)
    p = jnp.where(r == c, 1.0, 0.0).astype(F32) - a
    ak = a
    for level in range(int(math.log2(n)) - 1):
        dot = bmm_high if level < NEUMANN_HIGH_LEVELS else bmm
        ak = dot(ak, ak)
        p = p + dot(p, ak)
    return p


def delta_intra(q, k, v, gc, beta):
    b, c, _ = q.shape
    r, cc = _iota2(c, c)
    causal = r >= cc
    strict = r > cc
    gi = jnp.broadcast_to(gc, (b, c, c))
    gj = jnp.swapaxes(gi, 1, 2)
    decay = jnp.where(causal, jnp.exp(jnp.where(causal, gi - gj, 0.0)), 0.0)
    kb = k * beta
    a = jnp.where(strict, bmm_nt(kb, k) * decay, 0.0)
    t = inv_unit_lower(a)
    u = bmm(t, v * beta)
    w = bmm(t, kb * jnp.exp(gc))
    qk = jnp.where(causal, bmm_nt(q, k) * decay, 0.0)
    return u, w, qk


def delta_step(s, q, k, gc, u, w, qk):
    v_new = u - bmm(w, s)
    out = bmm(q * jnp.exp(gc), s) + bmm(qk, v_new)
    r = lax.broadcasted_iota(jnp.int32, gc.shape, 1)
    g_last = jnp.sum(jnp.where(r == gc.shape[1] - 1, gc, 0.0), axis=1, keepdims=True)
    k_dec = k * jnp.exp(g_last - gc)
    s_new = s * jnp.exp(g_last) + bmm_tn(k_dec, v_new)
    return out, s_new


def fox_block(q, k, v, fq, fk, qpos0):
    s = mm_nt(q, k)
    r, c = _iota2(s.shape[0], s.shape[1])
    mask = c <= (r + qpos0)
    p = _softmax_rows(jnp.where(mask, s + (fq - fk), -jnp.inf))
    return mm(p, v)


def mem_head(qm, wq, mk, mv):
    p = _softmax_rows(mm_nt(rms(qm, wq) * SCALE, mk))
    return mm(p, mv)


def _heads(x, n):
    return [x[:, h * HEAD_DIM:(h + 1) * HEAD_DIM] for h in range(n)]


def memkv_fn(mem, mnw, wkv, mknw):
    kv = mm(rms(mem, mnw), wkv)
    mk = jnp.concatenate([rms(kh, mknw) for kh in _heads(kv[:, :MEM_WIDTH], MEM_HEADS)], axis=1)
    return mk, kv[:, MEM_WIDTH:]


def dn_gates_fn(ab, alog, dtb):
    g = -jnp.exp(alog) * jax.nn.softplus(ab + dtb)
    low = _lower_ones(CHUNK)
    gc = jnp.concatenate([hdot(low, g[i * CHUNK:(i + 1) * CHUNK]) for i in range(ab.shape[0] // CHUNK)], axis=0)
    lane = lax.broadcasted_iota(jnp.int32, ab.shape, 1)
    return jnp.where(lane < N_HEADS, gc, jax.nn.sigmoid(ab))


def fox_fcum_fn(fp, fbias):
    lf = jax.nn.log_sigmoid(fp + fbias)
    low = _lower_ones(LANES)
    carry = jnp.zeros((1, fp.shape[1]), F32)
    outs = []
    for i in range(fp.shape[0] // LANES):
        cs = hdot(low, lf[i * LANES:(i + 1) * LANES]) + carry
        carry = _last_row(cs)
        outs.append(cs)
    return jnp.concatenate(outs, axis=0)


def fox_qk_fn(qraw, kraw, qnw, knw):
    q = jnp.concatenate([rms(x, qnw) * SCALE for x in _heads(qraw, N_HEADS)], axis=1)
    k = jnp.concatenate([rms(x, knw) for x in _heads(kraw, N_HEADS)], axis=1)
    return q, k


def _mem_out(qm, mqw, mk, mv):
    return [mem_head(a, mqw, b, c) for a, b, c in zip(_heads(qm, MEM_HEADS), _heads(mk, MEM_HEADS), _heads(mv, MEM_HEADS))]


def dn_out_fn(o, z, qm, onw, mqw, mk, mv):
    mix = [rms(a, onw) * jax.nn.silu(b) for a, b in zip(o, _heads(z, N_HEADS))]
    return jnp.concatenate(mix + _mem_out(qm, mqw, mk, mv), axis=1)


def fox_out_fn(o, gate, qm, mqw, mk, mv):
    return jnp.concatenate([o * jax.nn.sigmoid(gate)] + _mem_out(qm, mqw, mk, mv), axis=1)


def _pick(n, cands):
    for c in cands:
        if n % c == 0:
            return c
    return n


def _params(sem):
    return pltpu.CompilerParams(dimension_semantics=sem, vmem_limit_bytes=VMEM_LIMIT)


MATMUL_VMEM_BUDGET = 40 * 2 ** 20


def _matmul_tiles(m, n, k, bytes_a, bytes_b, bytes_mn, fixed):
    tm, tn, tk = fixed if fixed is not None else (None, None, None)
    tm = tm or _pick(m, (1024, 512, 256, 128))
    tn = tn or _pick(n, (512, 256, 128))
    if tk is None:
        for tk in [c for c in (2048, 1536, 1024, 512, 256, 128) if k % c == 0] + [k]:
            if 2 * (tm * tk * bytes_a + tk * tn * bytes_b + tm * tn * bytes_mn) + tm * tn * 4 <= MATMUL_VMEM_BUDGET:
                break
    return tm, tn, tk


def matmul(a, b, *, name, ta=False, tb=False, post=None, post_ins=(), extra_out=None, out_dtype=F32, tiles=None,
           b_view=None, out_view=None):
    (k, m) = a.shape if ta else a.shape[::-1]
    (kb, n) = b_view[:2] if b_view is not None else (b.shape[::-1] if tb else b.shape)
    assert k == kb, (a.shape, b.shape, ta, tb)
    bytes_mn = sum(p.dtype.itemsize for p in post_ins) + jnp.dtype(out_dtype).itemsize
    bytes_mn += jnp.dtype(extra_out[1]).itemsize if extra_out else 0
    tm, tn, tk = _matmul_tiles(m, n, k, a.dtype.itemsize, b.dtype.itemsize, bytes_mn, tiles)
    nk = k // tk
    dims = ((0,) if ta else (1,), (1,) if tb else (0,))
    n_post = len(post_ins)
    n_out = 2 if extra_out else 1

    def body(*refs):
        a_ref, b_ref = refs[:2]
        post_refs = refs[2:2 + n_post]
        o_refs, acc = refs[-1 - n_out:-1], refs[-1]
        kk = pl.program_id(2)

        @pl.when(kk == 0)
        def _():
            acc[...] = jnp.zeros_like(acc)

        b_tile = b_ref[...]
        acc[...] += _bdot(a_ref[...], b_tile.reshape(-1, b_tile.shape[-1]), dims)

        @pl.when(kk == nk - 1)
        def _():
            r = acc[...]
            if post is not None:
                r = post(r, *[p[...] for p in post_refs])
            o_refs[0][...] = r.astype(out_dtype)
            if extra_out:
                o_refs[1][...] = extra_out[0](r).astype(extra_out[1])

    a_spec = pl.BlockSpec((tk, tm), lambda i, j, kk: (kk, i)) if ta else pl.BlockSpec((tm, tk), lambda i, j, kk: (i, kk))
    if b_view is not None:
        b_spec = b_view[2]
    else:
        b_spec = pl.BlockSpec((tn, tk), lambda i, j, kk: (j, kk)) if tb else pl.BlockSpec((tk, tn), lambda i, j, kk: (kk, j))
    mn_spec = pl.BlockSpec((tm, tn), lambda i, j, kk: (i, j))
    o_shape, o_spec, into = ((m, n), mn_spec, None) if out_view is None else out_view
    ins, specs, aliases = [a, b, *post_ins], [a_spec, b_spec] + [mn_spec] * n_post, {}
    if into is not None:
        aliases = {len(ins): 0}
        ins.append(into)
        specs.append(pl.BlockSpec(memory_space=pl.ANY))
    out_shape = [jax.ShapeDtypeStruct(o_shape, out_dtype)]
    out_specs = [o_spec]
    if extra_out:
        out_shape.append(jax.ShapeDtypeStruct((m, n), extra_out[1]))
        out_specs.append(mn_spec)
    res = pl.pallas_call(
        body, name=name, grid=(m // tm, n // tn, nk), in_specs=specs, out_specs=out_specs, out_shape=out_shape,
        input_output_aliases=aliases, scratch_shapes=[pltpu.VMEM((tm, tn), F32)],
        compiler_params=_params(("parallel", "parallel", "arbitrary")),
    )(*ins)
    return res if extra_out else res[0]


def rows_call(fn, row_ins, full_ins, row_outs, acc_outs, *, tm, name):
    row_ins = [r if isinstance(r, tuple) else (r, r.shape[-1], 0) for r in row_ins]
    t = row_ins[0][0].shape[-2]
    tm = min(tm, t)
    n_in = len(row_ins) + len(full_ins)
    n_row = len(row_outs)

    def body(*refs):
        res = fn(*[[r[h] for h in range(r.shape[0])] if (i < len(row_ins) and len(r.shape) == 3) else r[...]
                   for i, r in enumerate(refs[:n_in])])
        res = res if isinstance(res, (tuple, list)) else (res,)
        outs = refs[n_in:]
        for ref, val in zip(outs[:n_row], res[:n_row]):
            if len(ref.shape) == 3:
                for h, vh in enumerate(val):
                    ref[h] = vh.astype(ref.dtype)
            else:
                ref[...] = val.astype(ref.dtype)
        first = pl.program_id(0) == 0
        for ref, val in zip(outs[n_row:], res[n_row:]):
            @pl.when(first)
            def _(ref=ref, val=val):
                ref[...] = val

            @pl.when(jnp.logical_not(first))
            def _(ref=ref, val=val):
                ref[...] += val

    def full_spec(shape):
        return pl.BlockSpec(shape, lambda i, nd=len(shape): (0,) * nd)

    def row_spec(lead, w, cb):
        if lead is None:
            return pl.BlockSpec((tm, w), lambda i: (i, cb))
        return pl.BlockSpec((lead, tm, w), lambda i: (0, i, cb))

    def lead_cols(c):
        return c if isinstance(c, tuple) else (None, c)

    in_specs = [row_spec(a.shape[0] if a.ndim == 3 else None, w, cb) for (a, w, cb) in row_ins]
    in_specs += [full_spec(f.shape) for f in full_ins]
    out_specs = [row_spec(*lead_cols(c), 0) for c, _ in row_outs] + [full_spec(s) for s in acc_outs]
    out_shape = [jax.ShapeDtypeStruct(tuple(d for d in (lead_cols(c)[0], t, lead_cols(c)[1]) if d is not None), dt)
                 for c, dt in row_outs] + [jax.ShapeDtypeStruct(s, F32) for s in acc_outs]
    res = pl.pallas_call(
        body, name=name, grid=(t // tm,), in_specs=in_specs, out_specs=out_specs, out_shape=out_shape,
        compiler_params=_params(("arbitrary",)),
    )(*[r[0] for r in row_ins], *full_ins)
    return res


def vjp_rows(fn, n_diff_row, row_diff_full):
    def bwd(*args, n_row, n_ct):
        prim_rows = args[:n_row]
        cts = args[n_row:n_row + n_ct]
        fulls = args[n_row + n_ct:]
        _, vjp = jax.vjp(fn, *prim_rows, *fulls)
        g = vjp(cts[0] if n_ct == 1 else tuple(cts))
        out = list(g[:n_diff_row])
        out += [gf for gf, d in zip(g[n_row:], row_diff_full) if d]
        return tuple(out)
    return bwd


def _shift_down(x, s):
    if s == 0:
        return x
    t = lax.broadcasted_iota(jnp.int32, x.shape, 0)
    return jnp.where(t >= s, pltpu.roll(x, s, 0), 0.0)


def _shift_up(x, s):
    if s == 0:
        return x
    n = x.shape[0]
    t = lax.broadcasted_iota(jnp.int32, x.shape, 0)
    return jnp.where(t < n - s, pltpu.roll(x, n - s, 0), 0.0)


def _conv(x, w_ref):
    return sum(w_ref[pl.ds(j, 1), :] * _shift_down(x, CONV_WIDTH - 1 - j) for j in range(CONV_WIDTH))


_DN_POST = (lambda c: l2n(jax.nn.silu(c)) * SCALE, lambda c: l2n(jax.nn.silu(c)), jax.nn.silu)


def dn_prep_fwd(proj, conv_w, *, name):
    t = proj.shape[0]

    def body(xq, xk, xv, wq, wk, wv, oq, ok, ov):
        for x_ref, w_ref, o_ref, post in zip((xq, xk, xv), (wq, wk, wv), (oq, ok, ov), _DN_POST):
            o_ref[...] = post(_conv(x_ref[...], w_ref))

    x_specs = [pl.BlockSpec((t, HEAD_DIM), lambda h, g=g: (0, g * N_HEADS + h)) for g in range(3)]
    w_specs = [pl.BlockSpec((CONV_WIDTH, HEAD_DIM), lambda h, g=g: (0, g * N_HEADS + h)) for g in range(3)]
    o_spec = pl.BlockSpec((None, t, HEAD_DIM), lambda h: (h, 0, 0))
    return pl.pallas_call(
        body, name=name, grid=(N_HEADS,), in_specs=x_specs + w_specs, out_specs=[o_spec] * 3,
        out_shape=[jax.ShapeDtypeStruct((N_HEADS, t, HEAD_DIM), F32)] * 3, compiler_params=_params(("parallel",)),
    )(proj, proj, proj, conv_w, conv_w, conv_w)


def dn_prep_bwd(proj, conv_w, dq, dk, dv, *, name):
    t = proj.shape[0]

    def body(xq, xk, xv, wq, wk, wv, gq, gk, gv, dxq, dxk, dxv, dwq, dwk, dwv):
        for x_ref, w_ref, g_ref, dx_ref, dw_ref, post in zip(
                (xq, xk, xv), (wq, wk, wv), (gq, gk, gv), (dxq, dxk, dxv), (dwq, dwk, dwv), _DN_POST):
            x = x_ref[...]
            _, vjp = jax.vjp(post, _conv(x, w_ref))
            dc, = vjp(g_ref[...])
            dx_ref[...] = sum(w_ref[pl.ds(j, 1), :] * _shift_up(dc, CONV_WIDTH - 1 - j) for j in range(CONV_WIDTH))
            for j in range(CONV_WIDTH):
                dw_ref[pl.ds(j, 1), :] = jnp.sum(dc * _shift_down(x, CONV_WIDTH - 1 - j), axis=0, keepdims=True)

    x_specs = [pl.BlockSpec((t, HEAD_DIM), lambda h, g=g: (0, g * N_HEADS + h)) for g in range(3)]
    w_specs = [pl.BlockSpec((CONV_WIDTH, HEAD_DIM), lambda h, g=g: (0, g * N_HEADS + h)) for g in range(3)]
    g_spec = pl.BlockSpec((None, t, HEAD_DIM), lambda h: (h, 0, 0))
    dx_spec = pl.BlockSpec((t, HEAD_DIM), lambda h: (0, h))
    dw_spec = pl.BlockSpec((CONV_WIDTH, HEAD_DIM), lambda h: (0, h))
    return pl.pallas_call(
        body, name=name, grid=(N_HEADS,), in_specs=x_specs + w_specs + [g_spec] * 3, out_specs=[dx_spec] * 3 + [dw_spec] * 3,
        out_shape=[jax.ShapeDtypeStruct((t, D_MODEL), F32)] * 3 + [jax.ShapeDtypeStruct((CONV_WIDTH, D_MODEL), F32)] * 3,
        compiler_params=_params(("parallel",)),
    )(proj, proj, proj, conv_w, conv_w, conv_w, dq, dk, dv)


INTRA_CHUNKS = 4


def _delta_specs(t):
    def spec(rows, w, index):
        return pl.BlockSpec((N_HEADS, rows, w), lambda i: (0, index(i), 0))
    return spec


def delta_intra_fwd(q, k, v, gc, beta, *, name):
    t = q.shape[1]
    per = min(INTRA_CHUNKS, t // CHUNK)
    rows, nb = per * CHUNK, N_HEADS * per

    def body(q_ref, k_ref, v_ref, g_ref, b_ref, u_ref, w_ref, qk_ref):
        ins = [r[...].reshape(nb, CHUNK, r.shape[-1]) for r in (q_ref, k_ref, v_ref, g_ref, b_ref)]
        for ref, val in zip((u_ref, w_ref, qk_ref), delta_intra(*ins)):
            ref[...] = val.reshape(ref.shape)

    spec = _delta_specs(t)
    x_spec, g_spec, qk_spec = (spec(rows, w, lambda i: i) for w in (HEAD_DIM, 1, CHUNK))
    return pl.pallas_call(
        body, name=name, grid=(t // rows,), in_specs=[x_spec] * 3 + [g_spec] * 2, out_specs=[x_spec, x_spec, qk_spec],
        out_shape=[jax.ShapeDtypeStruct((N_HEADS, t, HEAD_DIM), F32)] * 2 + [jax.ShapeDtypeStruct((N_HEADS, t, CHUNK), F32)],
        compiler_params=_params(("parallel",)),
    )(q, k, v, gc, beta)


def delta_seq_fwd(q, k, gc, u, w, qk, *, name):
    t = q.shape[1]
    nc = t // CHUNK

    def body(q_ref, k_ref, g_ref, u_ref, w_ref, qk_ref, o_ref, s0_ref, s_ref):
        @pl.when(pl.program_id(0) == 0)
        def _():
            s_ref[...] = jnp.zeros_like(s_ref)

        s = s_ref[...]
        s0_ref[...] = s
        o, s_new = delta_step(s, q_ref[...], k_ref[...], g_ref[...], u_ref[...], w_ref[...], qk_ref[...])
        o_ref[...] = o
        s_ref[...] = s_new

    spec = _delta_specs(t)
    x_spec, g_spec, qk_spec = (spec(CHUNK, w, lambda c: c) for w in (HEAD_DIM, 1, CHUNK))
    s_spec = pl.BlockSpec((N_HEADS, None, HEAD_DIM, HEAD_DIM), lambda c: (0, c, 0, 0))
    return pl.pallas_call(
        body, name=name, grid=(nc,), in_specs=[x_spec, x_spec, g_spec, x_spec, x_spec, qk_spec], out_specs=[x_spec, s_spec],
        out_shape=[jax.ShapeDtypeStruct((N_HEADS, t, HEAD_DIM), F32),
                   jax.ShapeDtypeStruct((N_HEADS, nc, HEAD_DIM, HEAD_DIM), F32)],
        scratch_shapes=[pltpu.VMEM((N_HEADS, HEAD_DIM, HEAD_DIM), F32)],
        compiler_params=_params(("arbitrary",)),
    )(q, k, gc, u, w, qk)


def delta_seq_bwd(q, k, gc, u, w, qk, s0, do, *, name):
    t = q.shape[1]
    nc = t // CHUNK

    def body(q_ref, k_ref, g_ref, u_ref, w_ref, qk_ref, s0_ref, do_ref,
             dq_ref, dk_ref, dg_ref, du_ref, dw_ref, dqk_ref, ds_ref):
        @pl.when(pl.program_id(0) == 0)
        def _():
            ds_ref[...] = jnp.zeros_like(ds_ref)

        _, vjp = jax.vjp(delta_step, s0_ref[...], q_ref[...], k_ref[...], g_ref[...], u_ref[...], w_ref[...], qk_ref[...])
        ds, dq, dk, dg, du, dw, dqk = vjp((do_ref[...], ds_ref[...]))
        for ref, val in zip((ds_ref, dq_ref, dk_ref, dg_ref, du_ref, dw_ref, dqk_ref), (ds, dq, dk, dg, du, dw, dqk)):
            ref[...] = val

    spec = _delta_specs(t)
    x_spec, g_spec, qk_spec = (spec(CHUNK, w, lambda c: nc - 1 - c) for w in (HEAD_DIM, 1, CHUNK))
    s_spec = pl.BlockSpec((N_HEADS, None, HEAD_DIM, HEAD_DIM), lambda c: (0, nc - 1 - c, 0, 0))
    return pl.pallas_call(
        body, name=name, grid=(nc,), in_specs=[x_spec, x_spec, g_spec, x_spec, x_spec, qk_spec, s_spec, x_spec],
        out_specs=[x_spec, x_spec, g_spec, x_spec, x_spec, qk_spec],
        out_shape=[jax.ShapeDtypeStruct((N_HEADS, t, w_), F32) for w_ in (HEAD_DIM, HEAD_DIM, 1, HEAD_DIM, HEAD_DIM, CHUNK)],
        scratch_shapes=[pltpu.VMEM((N_HEADS, HEAD_DIM, HEAD_DIM), F32)],
        compiler_params=_params(("arbitrary",)),
    )(q, k, gc, u, w, qk, s0, do)


def delta_intra_bwd(q, k, v, gc, beta, du, dw, dqk, dq_s, dk_s, dg_s, *, name):
    t = q.shape[1]
    per = min(INTRA_CHUNKS, t // CHUNK)
    rows, nb = per * CHUNK, N_HEADS * per

    def body(q_ref, k_ref, v_ref, g_ref, b_ref, du_ref, dw_ref, dqk_ref, dqs_ref, dks_ref, dgs_ref,
             dq_ref, dk_ref, dv_ref, dg_ref, db_ref):
        def chunks(r):
            return r[...].reshape(nb, CHUNK, r.shape[-1])
        _, vjp = jax.vjp(delta_intra, *[chunks(r) for r in (q_ref, k_ref, v_ref, g_ref, b_ref)])
        dq, dk, dv, dg, db = vjp(tuple(chunks(r) for r in (du_ref, dw_ref, dqk_ref)))
        dq_ref[...] = dq.reshape(dq_ref.shape) + dqs_ref[...]
        dk_ref[...] = dk.reshape(dk_ref.shape) + dks_ref[...]
        dv_ref[...] = dv.reshape(dv_ref.shape)
        dg_ref[...] = dg.reshape(dg_ref.shape) + dgs_ref[...]
        db_ref[...] = db.reshape(db_ref.shape)

    spec = _delta_specs(t)
    x_spec, g_spec, qk_spec = (spec(rows, w, lambda i: i) for w in (HEAD_DIM, 1, CHUNK))
    return pl.pallas_call(
        body, name=name, grid=(t // rows,),
        in_specs=[x_spec] * 3 + [g_spec] * 2 + [x_spec, x_spec, qk_spec, x_spec, x_spec, g_spec],
        out_specs=[x_spec] * 3 + [g_spec] * 2,
        out_shape=[jax.ShapeDtypeStruct((N_HEADS, t, HEAD_DIM), F32)] * 3 + [jax.ShapeDtypeStruct((N_HEADS, t, 1), F32)] * 2,
        compiler_params=_params(("parallel",)),
    )(q, k, v, gc, beta, du, dw, dqk, dq_s, dk_s, dg_s)


_V_BLOCK = 2 * N_HEADS
FOX_GROUPS = 8


def _fox_groups(t):
    nq = t // Q_BLOCK
    per = max(1, nq // FOX_GROUPS)
    return [(g0, per, (g0 + per) * Q_BLOCK) for g0 in range(0, nq, per)]


def fox_attn_fwd(q, k, proj, fq, fk, *, name):
    t = q.shape[0]

    def body(q_ref, k_ref, v_ref, fq_ref, fk_ref, o_ref):
        for g0, per, keys in _fox_groups(t):
            def block(j, carry, g0=g0, keys=keys):
                rows = pl.ds(pl.multiple_of((g0 + j) * Q_BLOCK, Q_BLOCK), Q_BLOCK)
                o_ref[rows, :] = fox_block(q_ref[rows, :], k_ref[0:keys, :], v_ref[0:keys, :], fq_ref[rows, :],
                                           fk_ref[:, 0:keys], (g0 + j) * Q_BLOCK)
                return carry
            lax.fori_loop(0, per, block, 0)

    x_spec = pl.BlockSpec((t, HEAD_DIM), lambda h: (0, h))
    v_spec = pl.BlockSpec((t, HEAD_DIM), lambda h: (0, _V_BLOCK + h))
    fq_spec = pl.BlockSpec((None, t, 1), lambda h: (h, 0, 0))
    fk_spec = pl.BlockSpec((None, 1, t), lambda h: (h, 0, 0))
    return pl.pallas_call(
        body, name=name, grid=(N_HEADS,), in_specs=[x_spec, x_spec, v_spec, fq_spec, fk_spec], out_specs=x_spec,
        out_shape=jax.ShapeDtypeStruct((t, D_MODEL), F32), compiler_params=_params(("parallel",)),
    )(q, k, proj, fq, fk)


def fox_attn_bwd(q, k, proj, fq, fk, do, *, name):
    t = q.shape[0]

    def body(q_ref, k_ref, v_ref, fq_ref, fk_ref, do_ref, dq_ref, dk_ref, dv_ref, dfq_ref, dfk_ref):
        dk_ref[...] = jnp.zeros_like(dk_ref)
        dv_ref[...] = jnp.zeros_like(dv_ref)
        dfk_ref[...] = jnp.zeros_like(dfk_ref)
        for g0, per, keys in _fox_groups(t):
            def block(j, carry, g0=g0, keys=keys):
                rows = pl.ds(pl.multiple_of((g0 + j) * Q_BLOCK, Q_BLOCK), Q_BLOCK)
                f = functools.partial(fox_block, qpos0=(g0 + j) * Q_BLOCK)
                _, vjp = jax.vjp(f, q_ref[rows, :], k_ref[0:keys, :], v_ref[0:keys, :], fq_ref[rows, :], fk_ref[:, 0:keys])
                dq, dk, dv, dfq, dfk = vjp(do_ref[rows, :])
                dq_ref[rows, :] = dq
                dfq_ref[rows, :] = dfq
                dk_ref[0:keys, :] += dk
                dv_ref[0:keys, :] += dv
                dfk_ref[:, 0:keys] += dfk
                return carry
            lax.fori_loop(0, per, block, 0)

    x_spec = pl.BlockSpec((t, HEAD_DIM), lambda h: (0, h))
    v_spec = pl.BlockSpec((t, HEAD_DIM), lambda h: (0, _V_BLOCK + h))
    fq_spec = pl.BlockSpec((None, t, 1), lambda h: (h, 0, 0))
    fk_spec = pl.BlockSpec((None, 1, t), lambda h: (h, 0, 0))
    return pl.pallas_call(
        body, name=name, grid=(N_HEADS,), in_specs=[x_spec, x_spec, v_spec, fq_spec, fk_spec, x_spec],
        out_specs=[x_spec, x_spec, x_spec, fq_spec, fk_spec],
        out_shape=[jax.ShapeDtypeStruct((t, D_MODEL), F32)] * 3
        + [jax.ShapeDtypeStruct((N_HEADS, t, 1), F32), jax.ShapeDtypeStruct((N_HEADS, 1, t), F32)],
        compiler_params=_params(("parallel",)),
    )(q, k, proj, fq, fk, do)


def memkv_fwd(mem, mnw, wkv, mknw, *, name):
    n = mem.shape[0]

    def body(mem_ref, mnw_ref, w_ref, mknw_ref, mk_ref, mv_ref):
        mk, mv = memkv_fn(mem_ref[...], mnw_ref[...], w_ref[...], mknw_ref[...])
        mk_ref[...] = mk
        mv_ref[...] = mv

    return pl.pallas_call(
        body, name=name, out_shape=[jax.ShapeDtypeStruct((n, MEM_WIDTH), F32)] * 2,
        compiler_params=pltpu.CompilerParams(vmem_limit_bytes=VMEM_LIMIT),
    )(mem, mnw, wkv, mknw)


def memkv_bwd(mem, mnw, wkv, mknw, dmk, dmv, *, name):
    def body(mem_ref, mnw_ref, w_ref, mknw_ref, dmk_ref, dmv_ref, dmnw_ref, dw_ref, dmknw_ref):
        f = functools.partial(memkv_fn, mem_ref[...])
        _, vjp = jax.vjp(f, mnw_ref[...], w_ref[...].astype(F32), mknw_ref[...])
        dmnw, dw, dmknw = vjp((dmk_ref[...], dmv_ref[...]))
        dmnw_ref[...] = dmnw
        dw_ref[...] = dw.astype(dw_ref.dtype)
        dmknw_ref[...] = dmknw

    return pl.pallas_call(
        body, name=name,
        out_shape=[jax.ShapeDtypeStruct(mnw.shape, F32), jax.ShapeDtypeStruct(wkv.shape, BF16), jax.ShapeDtypeStruct(mknw.shape, F32)],
        compiler_params=pltpu.CompilerParams(vmem_limit_bytes=VMEM_LIMIT),
    )(mem, mnw, wkv, mknw, dmk, dmv)


def _row(v, width=None):
    v = v.reshape(1, -1)
    if width is not None and v.shape[1] < width:
        v = jnp.pad(v, ((0, 0), (0, width - v.shape[1])))
    return v


def _head_cols(a):
    return a[:, :N_HEADS].T[:, :, None]


def _lanes_from_heads(*cols):
    t = cols[0].shape[1]
    parts = [c[:, :, 0].T for c in cols]
    parts.append(jnp.zeros((t, LANES - N_HEADS * len(cols)), F32))
    return jnp.concatenate(parts, axis=1)


def _norm_fwd(x, w, name):
    return rows_call(lambda x, w: rms(x, w), [x], [w], [(D_MODEL, BF16)], [], tm=512, name=name)[0]


def _norm_bwd(x, w, dh, dx_in, name):
    def fn(x, dh, dx_in, w):
        _, vjp = jax.vjp(rms, x, w)
        dx, dw = vjp(dh)
        return dx + dx_in, dw
    return rows_call(fn, [x, dh, dx_in], [w], [(D_MODEL, F32)], [(1, D_MODEL)], tm=512, name=name)


FF_PIECE = D_FF // N_DEV


def _piece_spec(shape, index):
    return pl.BlockSpec(shape[:-2] + (None,) + shape[-2:], index)


def _add(r, x):
    return r + x


def _mlp_fwd(x, n2w, w1, w2, layer):
    h2 = _norm_fwd(x, n2w, f"norm2_fwd_{layer}")
    u, a1 = matmul(h2, w1, name=f"mlp1_fwd_{layer}", tiles=(None, FF_PIECE, D_MODEL),
                   extra_out=(lambda u: jnp.square(jnp.maximum(u, 0.0)), BF16),
                   b_view=(D_MODEL, D_FF, _piece_spec((None, D_MODEL, FF_PIECE), lambda i, j, kk: (j, layer, 0, 0))))
    y = matmul(a1, w2, name=f"mlp2_fwd_{layer}", post=_add, post_ins=[x], tiles=(None, D_MODEL, 2 * FF_PIECE),
               b_view=(D_FF, D_MODEL, _piece_spec((2, FF_PIECE, D_MODEL), lambda i, j, kk: (kk, layer, 0, 0))))
    return y, (x, h2, u, a1)


def _mlp_bwd(dy, res, n2w, w1, w2, layer, dw1_into, dw2_into):
    x, h2, u, a1 = res
    du = matmul(dy, w2, tb=True, name=f"mlp2_dx_{layer}", out_dtype=BF16, tiles=(None, 2 * FF_PIECE, D_MODEL),
                post=lambda r, u: r * (2.0 * jnp.maximum(u, 0.0)), post_ins=[u],
                b_view=(D_MODEL, D_FF, _piece_spec((2, FF_PIECE, D_MODEL), lambda i, j, kk: (j, layer, 0, 0))))
    dw2 = matmul(a1, dy, ta=True, name=f"mlp2_dw_{layer}", out_dtype=BF16, tiles=(FF_PIECE, D_MODEL, None), out_view=(
        w2.shape, _piece_spec((None, FF_PIECE, D_MODEL), lambda i, j, kk: (i, layer, 0, 0)), dw2_into))
    dh2 = matmul(du, w1, tb=True, name=f"mlp1_dx_{layer}", tiles=(None, D_MODEL, FF_PIECE),
                 b_view=(D_FF, D_MODEL, _piece_spec((None, D_MODEL, FF_PIECE), lambda i, j, kk: (kk, layer, 0, 0))))
    dw1 = matmul(h2, du, ta=True, name=f"mlp1_dw_{layer}", out_dtype=BF16, tiles=(D_MODEL, FF_PIECE, None), out_view=(
        w1.shape, _piece_spec((None, D_MODEL, FF_PIECE), lambda i, j, kk: (j, layer, 0, 0)), dw1_into))
    dx, dn2w = _norm_bwd(x, n2w, dh2, dy, f"norm2_bwd_{layer}")
    return dx, dw1, dw2, dn2w


def _wout_dw(cat, dx, layer, into):
    tm = 512
    return matmul(cat, dx, ta=True, name=f"wout_dw_{layer}", out_dtype=BF16, tiles=(tm, D_MODEL, None), out_view=(
        (2, D_MODEL + MEM_WIDTH, D_MODEL), pl.BlockSpec((None, tm, D_MODEL), lambda i, j, kk: (layer, i, 0)), into))


def _in_proj_bwd(h, dmain, dsmall, w_main, w_small, tag):
    dh = matmul(dmain, w_main, tb=True, name=f"inproj_dx_main_{tag}")
    dh = matmul(dsmall, w_small, tb=True, post=_add, post_ins=[dh], name=f"inproj_dx_small_{tag}")
    dw_main = matmul(h, dmain, ta=True, out_dtype=BF16, name=f"inproj_dw_main_{tag}")
    dw_small = matmul(h, dsmall, ta=True, out_dtype=BF16, name=f"inproj_dw_small_{tag}")
    return dh, dw_main, dw_small


def local_step(x, mem, target, w):
    t = x.shape[0]
    n_mem = mem.shape[0]
    g = {}

    mnw, mknw = _row(w["mem_norm_w"]), _row(w["mem_k_norm_w"])
    mk, mv = memkv_fwd(mem, mnw, w["w_mem_kv"], mknw, name="memkv_fwd")

    n1w0, n2w0 = _row(w["norm1_w"][0]), _row(w["norm2_w"][0])
    alog, dtb = _row(w["dn_a_log"][0], LANES), _row(w["dn_dt_bias"][0], LANES)
    onw, mqw0 = _row(w["dn_o_norm_w"][0]), _row(w["memq_norm_w"][0])
    x0 = x
    h0 = _norm_fwd(x0, n1w0, "norm1_fwd_0")
    pm0 = matmul(h0, w["dn_main"], name="inproj_main_0")
    ps0 = matmul(h0, w["dn_ab"], name="inproj_small_0")
    gates = rows_call(dn_gates_fn, [ps0], [alog, dtb], [(LANES, F32)], [], tm=512, name="dn_gates_fwd")[0]
    gc, beta = _head_cols(gates), _head_cols(gates[:, N_HEADS:])
    q0, k0, v0 = dn_prep_fwd(pm0, w["dn_conv_w"], name="dn_prep_fwd")
    u0, w0, qk0 = delta_intra_fwd(q0, k0, v0, gc, beta, name="delta_intra_fwd")
    o0, s_start = delta_seq_fwd(q0, k0, gc, u0, w0, qk0, name="delta_seq_fwd")
    cat0 = rows_call(dn_out_fn, [o0, (pm0, D_MODEL, 3), (pm0, MEM_WIDTH, 8)], [onw, mqw0, mk, mv],
                     [(D_MODEL + MEM_WIDTH, BF16)], [], tm=256, name="dn_out_fwd")[0]
    x1 = matmul(cat0, w["w_out"][0], post=_add, post_ins=[x0], name="wout_fwd_0")
    x2, mlp_res0 = _mlp_fwd(x1, n2w0, w["w_mlp1"], w["w_mlp2"], 0)

    n1w1, n2w1 = _row(w["norm1_w"][1]), _row(w["norm2_w"][1])
    fbias = _row(w["fox_f_bias"][0], LANES)
    qnw, knw, mqw1 = _row(w["fox_q_norm_w"][0]), _row(w["fox_k_norm_w"][0]), _row(w["memq_norm_w"][1])
    h1 = _norm_fwd(x2, n1w1, "norm1_fwd_1")
    pm1 = matmul(h1, w["fox_main"], name="inproj_main_1")
    ps1 = matmul(h1, w["fox_f"], name="inproj_small_1")
    fcum = rows_call(fox_fcum_fn, [ps1], [fbias], [(LANES, F32)], [], tm=t, name="fox_fcum_fwd")[0]
    fq = _head_cols(fcum)
    fk = jnp.swapaxes(fq, 1, 2)
    q1, k1 = rows_call(fox_qk_fn, [(pm1, D_MODEL, 0), (pm1, D_MODEL, 1)], [qnw, knw], [(D_MODEL, F32)] * 2, [], tm=256,
                       name="fox_qk_fwd")
    o1 = fox_attn_fwd(q1, k1, pm1, fq, fk, name="fox_attn_fwd")
    cat1 = rows_call(fox_out_fn, [o1, (pm1, D_MODEL, 3), (pm1, MEM_WIDTH, 8)], [mqw1, mk, mv],
                     [(D_MODEL + MEM_WIDTH, BF16)], [], tm=256, name="fox_out_fwd")[0]
    x3 = matmul(cat1, w["w_out"][1], post=_add, post_ins=[x2], name="wout_fwd_1")
    y, mlp_res1 = _mlp_fwd(x3, n2w1, w["w_mlp1"], w["w_mlp2"], 1)

    def loss_fn(y, tgt):
        e = y - tgt
        return e * (1.0 / D_MODEL), jnp.sum(jnp.sum(e * e, axis=1, keepdims=True), axis=0, keepdims=True)
    dy, sq = rows_call(loss_fn, [y, target], [], [(D_MODEL, F32)], [(1, 1)], tm=512, name="loss")
    loss = sq[0, 0] * (0.5 / D_MODEL)

    dx3, dw1, dw2, dn2w1 = _mlp_bwd(dy, mlp_res1, n2w1, w["w_mlp1"], w["w_mlp2"], 1, None, None)
    dcat1 = matmul(dx3, w["w_out"][1], tb=True, name="wout_dx_1")
    dwo = _wout_dw(cat1, dx3, 1, None)
    do1, dgate1, dqm1, dmqw1, dmk1, dmv1 = rows_call(
        functools.partial(vjp_rows(fox_out_fn, 3, (True, True, True)), n_row=3, n_ct=1),
        [o1, (pm1, D_MODEL, 3), (pm1, MEM_WIDTH, 8), dcat1], [mqw1, mk, mv],
        [(D_MODEL, F32), (D_MODEL, F32), (MEM_WIDTH, F32)], [(1, HEAD_DIM), (n_mem, MEM_WIDTH), (n_mem, MEM_WIDTH)],
        tm=256, name="fox_out_bwd")
    dq1, dk1, dv1, dfq, dfk = fox_attn_bwd(q1, k1, pm1, fq, fk, do1, name="fox_attn_bwd")
    dqraw1, dkraw1, dqnw, dknw = rows_call(
        functools.partial(vjp_rows(fox_qk_fn, 2, (True, True)), n_row=2, n_ct=2),
        [(pm1, D_MODEL, 0), (pm1, D_MODEL, 1), dq1, dk1], [qnw, knw],
        [(D_MODEL, F32)] * 2, [(1, HEAD_DIM)] * 2, tm=256, name="fox_qk_bwd")
    dfcum = _lanes_from_heads(dfq + jnp.swapaxes(dfk, 1, 2))
    dps1, dfbias = rows_call(
        functools.partial(vjp_rows(fox_fcum_fn, 1, (True,)), n_row=1, n_ct=1),
        [ps1, dfcum], [fbias], [(LANES, F32)], [(1, LANES)], tm=t, name="fox_fcum_bwd")
    dpm1 = jnp.concatenate([dqraw1, dkraw1, dv1, dgate1, dqm1], axis=1)
    dh1, dwmain1, dwsmall1 = _in_proj_bwd(h1, dpm1, dps1, w["fox_main"], w["fox_f"], "1")
    dx2, dn1w1 = _norm_bwd(x2, n1w1, dh1, dx3, "norm1_bwd_1")

    dx1, dw1, dw2, dn2w0 = _mlp_bwd(dx2, mlp_res0, n2w0, w["w_mlp1"], w["w_mlp2"], 0, dw1, dw2)
    dcat0 = matmul(dx1, w["w_out"][0], tb=True, name="wout_dx_0")
    dwo = _wout_dw(cat0, dx1, 0, dwo)
    do0, dz0, dqm0, donw, dmqw0, dmk0, dmv0 = rows_call(
        functools.partial(vjp_rows(dn_out_fn, 3, (True, True, True, True)), n_row=3, n_ct=1),
        [o0, (pm0, D_MODEL, 3), (pm0, MEM_WIDTH, 8), dcat0], [onw, mqw0, mk, mv],
        [((N_HEADS, HEAD_DIM), F32), (D_MODEL, F32), (MEM_WIDTH, F32)],
        [(1, HEAD_DIM), (1, HEAD_DIM), (n_mem, MEM_WIDTH), (n_mem, MEM_WIDTH)], tm=256, name="dn_out_bwd")
    dq_s, dk_s, dg_s, du0, dw0, dqk0 = delta_seq_bwd(q0, k0, gc, u0, w0, qk0, s_start, do0, name="delta_seq_bwd")
    dq0, dk0, dv0, dgc, dbeta = delta_intra_bwd(q0, k0, v0, gc, beta, du0, dw0, dqk0, dq_s, dk_s, dg_s,
                                                name="delta_intra_bwd")
    dxq, dxk, dxv, dcq, dck, dcv = dn_prep_bwd(pm0, w["dn_conv_w"], dq0, dk0, dv0, name="dn_prep_bwd")
    dconv = jnp.concatenate([dcq, dck, dcv], axis=1)
    dps0, dalog, ddtb = rows_call(
        functools.partial(vjp_rows(dn_gates_fn, 1, (True, True)), n_row=1, n_ct=1),
        [ps0, _lanes_from_heads(dgc, dbeta)], [alog, dtb], [(LANES, F32)], [(1, LANES)] * 2, tm=512, name="dn_gates_bwd")
    dpm0 = jnp.concatenate([dxq, dxk, dxv, dz0, dqm0], axis=1)
    dh0, dwmain0, dwsmall0 = _in_proj_bwd(h0, dpm0, dps0, w["dn_main"], w["dn_ab"], "0")
    grad_x, dn1w0 = _norm_bwd(x0, n1w0, dh0, dx1, "norm1_bwd_0")

    dmnw, dwkv, dmknw = memkv_bwd(mem, mnw, w["w_mem_kv"], mknw, dmk0 + dmk1, dmv0 + dmv1, name="memkv_bwd")

    g["mem_norm_w"] = dmnw[0]
    g["w_mem_kv"] = dwkv
    g["mem_k_norm_w"] = dmknw[0]
    g["norm1_w"] = jnp.concatenate([dn1w0, dn1w1], axis=0)
    g["dn_main"], g["dn_ab"] = dwmain0, dwsmall0
    g["dn_conv_w"] = dconv
    g["dn_a_log"] = dalog[:, :N_HEADS]
    g["dn_dt_bias"] = ddtb[:, :N_HEADS]
    g["dn_o_norm_w"] = donw
    g["fox_main"], g["fox_f"] = dwmain1, dwsmall1
    g["fox_f_bias"] = dfbias[:, :N_HEADS]
    g["fox_q_norm_w"] = dqnw
    g["fox_k_norm_w"] = dknw
    g["memq_norm_w"] = jnp.concatenate([dmqw0, dmqw1], axis=0)
    g["w_out"] = dwo
    g["norm2_w"] = jnp.concatenate([dn2w0, dn2w1], axis=0)
    g["w_mlp1"] = dw1
    g["w_mlp2"] = dw2
    return loss, grad_x, g


WEIGHTS = ["mem_norm_w", "w_mem_kv", "mem_k_norm_w", "norm1_w", "dn_w_in", "dn_conv_w", "dn_a_log", "dn_dt_bias",
           "dn_o_norm_w", "fox_w_in", "fox_f_bias", "fox_q_norm_w", "fox_k_norm_w", "memq_norm_w", "w_out", "norm2_w",
           "w_mlp1", "w_mlp2"]
DN_IN = 4 * D_MODEL + 2 * N_HEADS + MEM_WIDTH
FOX_IN = 4 * D_MODEL + N_HEADS + MEM_WIDTH
GATE_END = 4 * D_MODEL
OUT_IN = D_MODEL + MEM_WIDTH
BIG = [("w_mem_kv", D_MODEL // N_DEV, D_MODEL), ("dn_w_in", D_MODEL, DN_IN // N_DEV), ("fox_w_in", D_MODEL, FOX_IN // N_DEV),
       ("dn_conv_w", CONV_WIDTH, 3 * D_MODEL // N_DEV), ("w_out", 2 * OUT_IN // N_DEV, D_MODEL),
       ("w_mlp1", 2 * D_MODEL, FF_PIECE), ("w_mlp2", 2 * FF_PIECE, D_MODEL)]
SMALL = [("mem_norm_w", (D_MODEL,), D_MODEL), ("mem_k_norm_w", (HEAD_DIM,), HEAD_DIM), ("norm1_w", (2, D_MODEL), 2 * D_MODEL),
         ("dn_a_log", (1, N_HEADS), LANES), ("dn_dt_bias", (1, N_HEADS), LANES), ("dn_o_norm_w", (1, HEAD_DIM), HEAD_DIM),
         ("fox_f_bias", (1, N_HEADS), LANES), ("fox_q_norm_w", (1, HEAD_DIM), HEAD_DIM), ("fox_k_norm_w", (1, HEAD_DIM), HEAD_DIM),
         ("memq_norm_w", (2, HEAD_DIM), 2 * HEAD_DIM), ("norm2_w", (2, D_MODEL), 2 * D_MODEL)]
SMALL_ROWS = 16


def pack_small(p):
    flat = jnp.concatenate([jnp.pad(p[n].reshape(-1), (0, ln - math.prod(sh))) for n, sh, ln in SMALL])
    return jnp.pad(flat, (0, SMALL_ROWS * PACK_W - flat.shape[0])).reshape(SMALL_ROWS, PACK_W)


def unpack_small(pk):
    flat, off, out = pk.reshape(-1), 0, {}
    for n, sh, ln in SMALL:
        out[n] = flat[off:off + math.prod(sh)].reshape(sh)
        off += ln
    return out


def shard_matrices(p):
    return [p[n].reshape(r, c) for n, r, c in BIG]


def step_weights(gathered, p):
    w = {n: p[n] for n, _, _ in SMALL}
    g = {n: a for (n, _, _), a in zip(BIG, gathered)}
    w["w_mem_kv"] = g["w_mem_kv"].reshape(D_MODEL, D_MODEL)
    dn = g["dn_w_in"].transpose(1, 0, 2).reshape(D_MODEL, DN_IN)
    fox = g["fox_w_in"].transpose(1, 0, 2).reshape(D_MODEL, FOX_IN)
    w["dn_main"] = jnp.concatenate([dn[:, :GATE_END], dn[:, GATE_END + 2 * N_HEADS:]], axis=1)
    w["dn_ab"] = jnp.pad(dn[:, GATE_END:GATE_END + 2 * N_HEADS], ((0, 0), (0, LANES - 2 * N_HEADS)))
    w["fox_main"] = jnp.concatenate([fox[:, :GATE_END], fox[:, GATE_END + N_HEADS:]], axis=1)
    w["fox_f"] = jnp.pad(fox[:, GATE_END:GATE_END + N_HEADS], ((0, 0), (0, LANES - N_HEADS)))
    w["dn_conv_w"] = g["dn_conv_w"].transpose(1, 0, 2).reshape(CONV_WIDTH, 3 * D_MODEL)
    w["w_out"] = g["w_out"].reshape(N_DEV, 2, OUT_IN // N_DEV, D_MODEL).transpose(1, 0, 2, 3).reshape(2, OUT_IN, D_MODEL)
    w["w_mlp1"] = g["w_mlp1"].reshape(N_DEV, 2, D_MODEL, FF_PIECE)
    w["w_mlp2"] = g["w_mlp2"].reshape(N_DEV, 2, FF_PIECE, D_MODEL)
    return w


def grad_pieces(g):
    def in_w(main, small, n_small, width):
        full = jnp.concatenate([main[:, :GATE_END], small[:, :n_small], main[:, GATE_END:]], axis=1)
        return full.reshape(D_MODEL, N_DEV, width // N_DEV).transpose(1, 0, 2)

    out = {
        "w_mem_kv": g["w_mem_kv"].reshape(N_DEV, D_MODEL // N_DEV, D_MODEL),
        "dn_w_in": in_w(g["dn_main"], g["dn_ab"], 2 * N_HEADS, DN_IN),
        "fox_w_in": in_w(g["fox_main"], g["fox_f"], N_HEADS, FOX_IN),
        "dn_conv_w": g["dn_conv_w"].reshape(CONV_WIDTH, N_DEV, -1).transpose(1, 0, 2).astype(BF16),
        "w_out": g["w_out"].reshape(2, N_DEV, OUT_IN // N_DEV, D_MODEL).transpose(1, 0, 2, 3).reshape(N_DEV, -1, D_MODEL),
        "w_mlp1": g["w_mlp1"].reshape(N_DEV, 2 * D_MODEL, FF_PIECE),
        "w_mlp2": g["w_mlp2"].reshape(N_DEV, 2 * FF_PIECE, D_MODEL),
    }
    return [out[n] for n, _, _ in BIG]


_HBM = pl.BlockSpec(memory_space=pltpu.HBM)


def _place():
    return lax.axis_index("x"), lax.axis_index("y"), lax.axis_index("c")


def all_gather(xs, *, name):
    n = len(xs)

    def body(*refs):
        x_refs, out_refs = refs[:n], refs[n:2 * n]
        send_sems, recv_sems, local_sems = refs[2 * n:]
        x, y, c = _place()
        me, sibling = (x, y, c), (x, y, 1 - c)
        chips = [(1 - x, y), (x, 1 - y), (1 - x, 1 - y)]

        def copy(a, k, block, to, src=None):
            px, py, pc = block
            dst = out_refs[a].at[4 * px + 2 * py + pc]
            return pltpu.make_async_remote_copy(
                src_ref=dst if src is None else src, dst_ref=dst,
                send_sem=send_sems.at[a, k], recv_sem=recv_sems.at[a, k], device_id=to, device_id_type=MESH)

        mine = [pltpu.make_async_copy(x_refs[a], out_refs[a].at[4 * x + 2 * y + c], local_sems.at[a]) for a in range(n)]
        first = [copy(a, 0, me, sibling, src=x_refs[a]) for a in range(n)]
        first += [copy(a, 1 + j, me, (*chip, c), src=x_refs[a]) for j, chip in enumerate(chips) for a in range(n)]
        for cp in mine + first:
            cp.start()
        passed = []
        for j, chip in enumerate(chips):
            for a in range(n):
                copy(a, 1 + j, (*chip, c), me).wait_recv()
                passed.append(copy(a, 4 + j, (*chip, c), sibling))
                passed[-1].start()
        for a in range(n):
            copy(a, 0, sibling, me).wait_recv()
        for j, chip in enumerate(chips):
            for a in range(n):
                copy(a, 4 + j, (*chip, 1 - c), me).wait_recv()
        for cp in first + passed:
            cp.wait_send()
        for cp in mine:
            cp.wait()

    return pl.pallas_call(
        body, name=name, out_shape=[jax.ShapeDtypeStruct((N_DEV,) + a.shape, a.dtype) for a in xs],
        in_specs=[_HBM] * n, out_specs=[_HBM] * n,
        scratch_shapes=[pltpu.SemaphoreType.DMA((n, 7)), pltpu.SemaphoreType.DMA((n, 7)), pltpu.SemaphoreType.DMA((n,))],
    )(*xs)


def sibling_exchange(gs, *, name):
    n = len(gs)

    def body(*refs):
        g_refs, out_refs = refs[:n], refs[n:2 * n]
        send_sems, recv_sems = refs[2 * n:]
        x, y, c = _place()
        copies = [pltpu.make_async_remote_copy(
            src_ref=g_refs[a].at[2 * k + 1 - c], dst_ref=out_refs[a].at[k], send_sem=send_sems.at[a, k],
            recv_sem=recv_sems.at[a, k], device_id=(x, y, 1 - c), device_id_type=MESH) for a in range(n) for k in range(4)]
        for cp in copies:
            cp.start()
        for cp in copies:
            cp.wait_recv()
        for cp in copies:
            cp.wait_send()

    return pl.pallas_call(
        body, name=name, out_shape=[jax.ShapeDtypeStruct((4,) + g.shape[1:], g.dtype) for g in gs],
        in_specs=[_HBM] * n, out_specs=[_HBM] * n,
        scratch_shapes=[pltpu.SemaphoreType.DMA((n, 4)), pltpu.SemaphoreType.DMA((n, 4))],
    )(*gs)


def _row_tile(rows):
    return _pick(rows, (512, 256, 128))


def pair_sum(g, got, *, name):
    _, rows, cols = g.shape
    tile = _row_tile(rows)
    c = lax.axis_index("c").astype(jnp.int32).reshape(1)

    def body(c_ref, a_ref, b_ref, o_ref):
        o_ref[...] = (a_ref[...].astype(F32) + b_ref[...].astype(F32)).astype(o_ref.dtype)

    grid_spec = pltpu.PrefetchScalarGridSpec(
        num_scalar_prefetch=1, grid=(4, rows // tile),
        in_specs=[pl.BlockSpec((None, tile, cols), lambda k, i, c_ref: (2 * k + c_ref[0], i, 0)),
                  pl.BlockSpec((None, tile, cols), lambda k, i, c_ref: (k, i, 0))],
        out_specs=pl.BlockSpec((None, tile, cols), lambda k, i, c_ref: (k, i, 0)))
    return pl.pallas_call(
        body, name=name, grid_spec=grid_spec, out_shape=jax.ShapeDtypeStruct((4, rows, cols), g.dtype),
        compiler_params=_params(("parallel", "parallel")),
    )(c, g, got)


def chip_exchange(hs, *, name):
    n = len(hs)

    def body(*refs):
        h_refs, out_refs = refs[:n], refs[n:2 * n]
        send_sems, recv_sems, local_sems = refs[2 * n:]
        x, y, c = _place()
        mine = 2 * x + y
        chips = [(1 - x, y), (x, 1 - y), (1 - x, 1 - y)]
        keep = [pltpu.make_async_copy(h_refs[a].at[mine], out_refs[a].at[mine], local_sems.at[a]) for a in range(n)]
        sends = [pltpu.make_async_remote_copy(
            src_ref=h_refs[a].at[2 * qx + qy], dst_ref=out_refs[a].at[mine], send_sem=send_sems.at[a, j],
            recv_sem=recv_sems.at[a, j], device_id=(qx, qy, c), device_id_type=MESH)
            for j, (qx, qy) in enumerate(chips) for a in range(n)]
        for cp in keep + sends:
            cp.start()
        for j, (qx, qy) in enumerate(chips):
            for a in range(n):
                pltpu.make_async_remote_copy(
                    src_ref=h_refs[a].at[mine], dst_ref=out_refs[a].at[2 * qx + qy], send_sem=send_sems.at[a, j],
                    recv_sem=recv_sems.at[a, j], device_id=(qx, qy, c), device_id_type=MESH).wait_recv()
        for cp in sends:
            cp.wait_send()
        for cp in keep:
            cp.wait()

    return pl.pallas_call(
        body, name=name, out_shape=[jax.ShapeDtypeStruct(h.shape, h.dtype) for h in hs],
        in_specs=[_HBM] * n, out_specs=[_HBM] * n,
        scratch_shapes=[pltpu.SemaphoreType.DMA((n, 3)), pltpu.SemaphoreType.DMA((n, 3)), pltpu.SemaphoreType.DMA((n,))],
    )(*hs)


def adamw(parts, w, m, v, *, name):
    n, rows, cols = parts.shape
    tile = _row_tile(rows)

    def body(p_ref, w_ref, m_ref, v_ref, g_ref, d_ref, mo_ref, vo_ref):
        g = p_ref[0].astype(F32)
        for i in range(1, n):
            g = g + p_ref[i].astype(F32)
        m_new = ADAM_B1 * m_ref[...] + (1.0 - ADAM_B1) * g
        v_new = ADAM_B2 * v_ref[...] + (1.0 - ADAM_B2) * jnp.square(g)
        m_hat = m_new / (1.0 - ADAM_B1 ** ADAM_STEP)
        v_hat = v_new / (1.0 - ADAM_B2 ** ADAM_STEP)
        g_ref[...] = g
        d_ref[...] = -ADAM_LR * (m_hat / (jnp.sqrt(v_hat) + ADAM_EPS) + ADAM_WD * w_ref[...])
        mo_ref[...] = m_new
        vo_ref[...] = v_new

    spec = pl.BlockSpec((tile, cols), lambda i: (i, 0))
    return pl.pallas_call(
        body, name=name, grid=(rows // tile,),
        in_specs=[pl.BlockSpec((n, tile, cols), lambda i: (0, i, 0)), spec, spec, spec], out_specs=[spec] * 4,
        out_shape=[jax.ShapeDtypeStruct((rows, cols), F32)] * 4, compiler_params=_params(("parallel",)),
    )(parts, w, m, v)


def kernel(x, mem, mem_norm_w, w_mem_kv, mem_k_norm_w, norm1_w, dn_w_in, dn_conv_w, dn_a_log, dn_dt_bias, dn_o_norm_w, fox_w_in, fox_f_bias, fox_q_norm_w, fox_k_norm_w, memq_norm_w, w_out, norm2_w, w_mlp1, w_mlp2, loss_target, m_mem_norm_w, m_w_mem_kv, m_mem_k_norm_w, m_norm1_w, m_dn_w_in, m_dn_conv_w, m_dn_a_log, m_dn_dt_bias, m_dn_o_norm_w, m_fox_w_in, m_fox_f_bias, m_fox_q_norm_w, m_fox_k_norm_w, m_memq_norm_w, m_w_out, m_norm2_w, m_w_mlp1, m_w_mlp2, v_mem_norm_w, v_w_mem_kv, v_mem_k_norm_w, v_norm1_w, v_dn_w_in, v_dn_conv_w, v_dn_a_log, v_dn_dt_bias, v_dn_o_norm_w, v_fox_w_in, v_fox_f_bias, v_fox_q_norm_w, v_fox_k_norm_w, v_memq_norm_w, v_w_out, v_norm2_w, v_w_mlp1, v_w_mlp2):
    p = dict(mem_norm_w=mem_norm_w, w_mem_kv=w_mem_kv, mem_k_norm_w=mem_k_norm_w, norm1_w=norm1_w, dn_w_in=dn_w_in,
             dn_conv_w=dn_conv_w, dn_a_log=dn_a_log, dn_dt_bias=dn_dt_bias, dn_o_norm_w=dn_o_norm_w, fox_w_in=fox_w_in,
             fox_f_bias=fox_f_bias, fox_q_norm_w=fox_q_norm_w, fox_k_norm_w=fox_k_norm_w, memq_norm_w=memq_norm_w,
             w_out=w_out, norm2_w=norm2_w, w_mlp1=w_mlp1, w_mlp2=w_mlp2)
    pm = dict(mem_norm_w=m_mem_norm_w, w_mem_kv=m_w_mem_kv, mem_k_norm_w=m_mem_k_norm_w, norm1_w=m_norm1_w,
              dn_w_in=m_dn_w_in, dn_conv_w=m_dn_conv_w, dn_a_log=m_dn_a_log, dn_dt_bias=m_dn_dt_bias,
              dn_o_norm_w=m_dn_o_norm_w, fox_w_in=m_fox_w_in, fox_f_bias=m_fox_f_bias, fox_q_norm_w=m_fox_q_norm_w,
              fox_k_norm_w=m_fox_k_norm_w, memq_norm_w=m_memq_norm_w, w_out=m_w_out, norm2_w=m_norm2_w, w_mlp1=m_w_mlp1,
              w_mlp2=m_w_mlp2)
    pv = dict(mem_norm_w=v_mem_norm_w, w_mem_kv=v_w_mem_kv, mem_k_norm_w=v_mem_k_norm_w, norm1_w=v_norm1_w,
              dn_w_in=v_dn_w_in, dn_conv_w=v_dn_conv_w, dn_a_log=v_dn_a_log, dn_dt_bias=v_dn_dt_bias,
              dn_o_norm_w=v_dn_o_norm_w, fox_w_in=v_fox_w_in, fox_f_bias=v_fox_f_bias, fox_q_norm_w=v_fox_q_norm_w,
              fox_k_norm_w=v_fox_k_norm_w, memq_norm_w=v_memq_norm_w, w_out=v_w_out, norm2_w=v_norm2_w, w_mlp1=v_w_mlp1,
              w_mlp2=v_w_mlp2)

    wire = [a if n == "dn_conv_w" else a.astype(BF16) for (n, _, _), a in zip(BIG, shard_matrices(p))]
    gathered = all_gather(wire, name="weights_all_gather")
    loss, grad_x, g = local_step(x[0], mem[0], loss_target[0], step_weights(gathered, p))
    loss = lax.psum(loss, ("x", "y", "c"))

    pieces = grad_pieces(g)
    got = sibling_exchange(pieces, name="grads_to_sibling")
    chip_sums = [pair_sum(a, b, name=f"grads_pair_sum_{n}") for (n, _, _), a, b in zip(BIG, pieces, got)]
    parts = chip_exchange(chip_sums, name="grads_to_chips")
    small_parts, = all_gather([pack_small(g)], name="small_grads_all_gather")

    results = {}
    for (n, _, _), part, w_, m_, v_ in zip(BIG, parts, shard_matrices(p), shard_matrices(pm), shard_matrices(pv)):
        results[n] = [o.reshape(p[n].shape) for o in adamw(part, w_, m_, v_, name=f"adamw_{n}")]
    small = [unpack_small(o) for o in adamw(small_parts, pack_small(p), pack_small(pm), pack_small(pv), name="adamw_small")]
    groups = [{**small[i], **{n: r[i] for n, r in results.items()}} for i in range(4)]
    return (loss, grad_x[None], *[grp[n] for grp in groups for n in WEIGHTS])
```

```python
import functools
import math

import jax
import jax.numpy as jnp
from jax import lax
from jax.experimental import pallas as pl
from jax.experimental.pallas import tpu as pltpu

F32 = jnp.float32
BF16 = jnp.bfloat16
HIGHEST = lax.Precision.HIGHEST

D_MODEL = 1024
HEAD_DIM = 128
N_HEADS = 8
MEM_HEADS = 4
MEM_WIDTH = MEM_HEADS * HEAD_DIM
D_FF = 4 * D_MODEL
CONV_WIDTH = 4
CHUNK = 64
Q_BLOCK = 128
EPS = 1e-6
SCALE = HEAD_DIM ** -0.5
MAIN_WIDTH = 4 * D_MODEL + MEM_WIDTH
LANES = 128
N_DEV = 8
PACK_W = 512

ADAM_LR = 0.001
ADAM_B1 = 0.9
ADAM_B2 = 0.999
ADAM_EPS = 1e-08
ADAM_WD = 0.01
ADAM_STEP = 10

VMEM_LIMIT = 56 * 2 ** 20
MESH = pl.DeviceIdType.MESH


def _bdot(a, b, dims):
    return lax.dot_general(a.astype(BF16), b.astype(BF16), (dims, ((), ())), preferred_element_type=F32)


@jax.custom_vjp
def mm(a, b):
    return _bdot(a, b, ((1,), (0,)))


@jax.custom_vjp
def mm_nt(a, b):
    return _bdot(a, b, ((1,), (1,)))


@jax.custom_vjp
def mm_tn(a, b):
    return _bdot(a, b, ((0,), (0,)))


mm.defvjp(lambda a, b: (mm(a, b), (a, b)), lambda r, g: (mm_nt(g, r[1]), mm_tn(r[0], g)))
mm_nt.defvjp(lambda a, b: (mm_nt(a, b), (a, b)), lambda r, g: (mm(g, r[1]), mm_tn(g, r[0])))
mm_tn.defvjp(lambda a, b: (mm_tn(a, b), (a, b)), lambda r, g: (mm_nt(r[1], g), mm(r[0], g)))


def hdot(a, b):
    return jnp.dot(a, b, precision=HIGHEST, preferred_element_type=F32)


def rms(x, w):
    return x * lax.rsqrt(jnp.mean(x * x, axis=-1, keepdims=True) + EPS) * w


def l2n(x):
    return x * lax.rsqrt(jnp.sum(x * x, axis=-1, keepdims=True) + EPS)


def _iota2(n, m):
    return lax.broadcasted_iota(jnp.int32, (n, m), 0), lax.broadcasted_iota(jnp.int32, (n, m), 1)


def _lower_ones(n):
    r, c = _iota2(n, n)
    return jnp.where(r >= c, 1.0, 0.0).astype(F32)


def _last_row(x):
    r = lax.broadcasted_iota(jnp.int32, x.shape, 0)
    return jnp.sum(jnp.where(r == x.shape[0] - 1, x, 0.0), axis=0, keepdims=True)


def _softmax_rows(z):
    m = lax.stop_gradient(jnp.max(z, axis=-1, keepdims=True))
    e = jnp.exp(z - m)
    return e / jnp.sum(e, axis=-1, keepdims=True)


_BNN = (((2,), (1,)), ((0,), (0,)))
_BNT = (((2,), (2,)), ((0,), (0,)))
_BTN = (((1,), (1,)), ((0,), (0,)))


def _bbdot(a, b, dims):
    return lax.dot_general(a.astype(BF16), b.astype(BF16), dims, preferred_element_type=F32)


@jax.custom_vjp
def bmm(a, b):
    return _bbdot(a, b, _BNN)


@jax.custom_vjp
def bmm_nt(a, b):
    return _bbdot(a, b, _BNT)


@jax.custom_vjp
def bmm_tn(a, b):
    return _bbdot(a, b, _BTN)


@jax.custom_vjp
def bmm_high(a, b):
    return lax.dot_general(a, b, _BNN, precision=lax.Precision.HIGH, preferred_element_type=F32)


bmm.defvjp(lambda a, b: (bmm(a, b), (a, b)), lambda r, g: (bmm_nt(g, r[1]), bmm_tn(r[0], g)))
bmm_nt.defvjp(lambda a, b: (bmm_nt(a, b), (a, b)), lambda r, g: (bmm(g, r[1]), bmm_tn(g, r[0])))
bmm_tn.defvjp(lambda a, b: (bmm_tn(a, b), (a, b)), lambda r, g: (bmm_nt(r[1], g), bmm(r[0], g)))
bmm_high.defvjp(lambda a, b: (bmm_high(a, b), (a, b)), lambda r, g: (bmm_nt(g, r[1]), bmm_tn(r[0], g)))

NEUMANN_HIGH_LEVELS = 2


def inv_unit_lower(a):
    n = a.shape[-1]
    r, c = _iota2(n, n)
    p = jnp.where(r == c, 1.0, 0.0).astype(F32) - a
    ak = a
    for level in range(int(math.log2(n)) - 1):
        dot = bmm_high if level < NEUMANN_HIGH_LEVELS else bmm
        ak = dot(ak, ak)
        p = p + dot(p, ak)
    return p


def delta_intra(q, k, v, gc, beta):
    b, c, _ = q.shape
    r, cc = _iota2(c, c)
    causal = r >= cc
    strict = r > cc
    gi = jnp.broadcast_to(gc, (b, c, c))
    gj = jnp.swapaxes(gi, 1, 2)
    decay = jnp.where(causal, jnp.exp(jnp.where(causal, gi - gj, 0.0)), 0.0)
    kb = k * beta
    a = jnp.where(strict, bmm_nt(kb, k) * decay, 0.0)
    t = inv_unit_lower(a)
    u = bmm(t, v * beta)
    w = bmm(t, kb * jnp.exp(gc))
    qk = jnp.where(causal, bmm_nt(q, k) * decay, 0.0)
    return u, w, qk


def delta_step(s, q, k, gc, u, w, qk):
    v_new = u - bmm(w, s)
    out = bmm(q * jnp.exp(gc), s) + bmm(qk, v_new)
    r = lax.broadcasted_iota(jnp.int32, gc.shape, 1)
    g_last = jnp.sum(jnp.where(r == gc.shape[1] - 1, gc, 0.0), axis=1, keepdims=True)
    k_dec = k * jnp.exp(g_last - gc)
    s_new = s * jnp.exp(g_last) + bmm_tn(k_dec, v_new)
    return out, s_new


def fox_block(q, k, v, fq, fk, qpos0):
    s = mm_nt(q, k)
    r, c = _iota2(s.shape[0], s.shape[1])
    mask = c <= (r + qpos0)
    p = _softmax_rows(jnp.where(mask, s + (fq - fk), -jnp.inf))
    return mm(p, v)


def mem_head(qm, wq, mk, mv):
    p = _softmax_rows(mm_nt(rms(qm, wq) * SCALE, mk))
    return mm(p, mv)


def _heads(x, n):
    return [x[:, h * HEAD_DIM:(h + 1) * HEAD_DIM] for h in range(n)]


def memkv_fn(mem, mnw, wkv, mknw):
    kv = mm(rms(mem, mnw), wkv)
    mk = jnp.concatenate([rms(kh, mknw) for kh in _heads(kv[:, :MEM_WIDTH], MEM_HEADS)], axis=1)
    return mk, kv[:, MEM_WIDTH:]


def dn_gates_fn(ab, alog, dtb):
    g = -jnp.exp(alog) * jax.nn.softplus(ab + dtb)
    low = _lower_ones(CHUNK)
    gc = jnp.concatenate([hdot(low, g[i * CHUNK:(i + 1) * CHUNK]) for i in range(ab.shape[0] // CHUNK)], axis=0)
    lane = lax.broadcasted_iota(jnp.int32, ab.shape, 1)
    return jnp.where(lane < N_HEADS, gc, jax.nn.sigmoid(ab))


def fox_fcum_fn(fp, fbias):
    lf = jax.nn.log_sigmoid(fp + fbias)
    low = _lower_ones(LANES)
    carry = jnp.zeros((1, fp.shape[1]), F32)
    outs = []
    for i in range(fp.shape[0] // LANES):
        cs = hdot(low, lf[i * LANES:(i + 1) * LANES]) + carry
        carry = _last_row(cs)
        outs.append(cs)
    return jnp.concatenate(outs, axis=0)


def fox_qk_fn(qraw, kraw, qnw, knw):
    q = jnp.concatenate([rms(x, qnw) * SCALE for x in _heads(qraw, N_HEADS)], axis=1)
    k = jnp.concatenate([rms(x, knw) for x in _heads(kraw, N_HEADS)], axis=1)
    return q, k


def _mem_out(qm, mqw, mk, mv):
    return [mem_head(a, mqw, b, c) for a, b, c in zip(_heads(qm, MEM_HEADS), _heads(mk, MEM_HEADS), _heads(mv, MEM_HEADS))]


def dn_out_fn(o, z, qm, onw, mqw, mk, mv):
    mix = [rms(a, onw) * jax.nn.silu(b) for a, b in zip(o, _heads(z, N_HEADS))]
    return jnp.concatenate(mix + _mem_out(qm, mqw, mk, mv), axis=1)


def fox_out_fn(o, gate, qm, mqw, mk, mv):
    return jnp.concatenate([o * jax.nn.sigmoid(gate)] + _mem_out(qm, mqw, mk, mv), axis=1)


_HBM = pl.BlockSpec(memory_space=pltpu.HBM)


def _place():
    return lax.axis_index("x"), lax.axis_index("y"), lax.axis_index("c")


class Rider:
    def __init__(self, ins, out_shape, scratch, start, finish):
        self.ins, self.out_shape, self.scratch, self.start, self.finish = list(ins), list(out_shape), list(scratch), start, finish
        self.results = None


def gather_rider(xs):
    n = len(xs)

    def plan(x_refs, out_refs, sems):
        send_sems, recv_sems, local_sems = sems
        x, y, c = _place()
        me, sibling = (x, y, c), (x, y, 1 - c)
        chips = [(1 - x, y), (x, 1 - y), (1 - x, 1 - y)]

        def copy(a, k, block, to, src=None):
            px, py, pc = block
            dst = out_refs[a].at[4 * px + 2 * py + pc]
            return pltpu.make_async_remote_copy(
                src_ref=dst if src is None else src, dst_ref=dst,
                send_sem=send_sems.at[a, k], recv_sem=recv_sems.at[a, k], device_id=to, device_id_type=MESH)

        mine = [pltpu.make_async_copy(x_refs[a], out_refs[a].at[4 * x + 2 * y + c], local_sems.at[a]) for a in range(n)]
        first = [copy(a, 0, me, sibling, src=x_refs[a]) for a in range(n)]
        first += [copy(a, 1 + j, me, (*chip, c), src=x_refs[a]) for j, chip in enumerate(chips) for a in range(n)]
        return copy, me, sibling, chips, mine, first

    def start(x_refs, out_refs, sems):
        _, _, _, _, mine, first = plan(x_refs, out_refs, sems)
        for cp in mine + first:
            cp.start()

    def finish(x_refs, out_refs, sems):
        copy, me, sibling, chips, mine, first = plan(x_refs, out_refs, sems)
        _, _, c = me
        passed = []
        for j, chip in enumerate(chips):
            for a in range(n):
                copy(a, 1 + j, (*chip, c), me).wait_recv()
                passed.append(copy(a, 4 + j, (*chip, c), sibling))
                passed[-1].start()
        for a in range(n):
            copy(a, 0, sibling, me).wait_recv()
        for j, chip in enumerate(chips):
            for a in range(n):
                copy(a, 4 + j, (*chip, 1 - c), me).wait_recv()
        for cp in first + passed:
            cp.wait_send()
        for cp in mine:
            cp.wait()

    return Rider(xs, [jax.ShapeDtypeStruct((N_DEV,) + a.shape, a.dtype) for a in xs],
                 [pltpu.SemaphoreType.DMA((n, 7)), pltpu.SemaphoreType.DMA((n, 7)), pltpu.SemaphoreType.DMA((n,))], start, finish)


def sibling_rider(gs):
    n = len(gs)

    def plan(g_refs, out_refs, sems):
        send_sems, recv_sems = sems
        x, y, c = _place()
        return [pltpu.make_async_remote_copy(
            src_ref=g_refs[a].at[2 * k + 1 - c], dst_ref=out_refs[a].at[k], send_sem=send_sems.at[a, k],
            recv_sem=recv_sems.at[a, k], device_id=(x, y, 1 - c), device_id_type=MESH) for a in range(n) for k in range(4)]

    def start(g_refs, out_refs, sems):
        for cp in plan(g_refs, out_refs, sems):
            cp.start()

    def finish(g_refs, out_refs, sems):
        copies = plan(g_refs, out_refs, sems)
        for cp in copies:
            cp.wait_recv()
        for cp in copies:
            cp.wait_send()

    return Rider(gs, [jax.ShapeDtypeStruct((4,) + g.shape[1:], g.dtype) for g in gs],
                 [pltpu.SemaphoreType.DMA((n, 4)), pltpu.SemaphoreType.DMA((n, 4))], start, finish)


def chips_rider(hs):
    n = len(hs)

    def plan(h_refs, out_refs, sems):
        send_sems, recv_sems, local_sems = sems
        x, y, c = _place()
        mine = 2 * x + y
        chips = [(1 - x, y), (x, 1 - y), (1 - x, 1 - y)]
        keep = [pltpu.make_async_copy(h_refs[a].at[mine], out_refs[a].at[mine], local_sems.at[a]) for a in range(n)]
        sends = [pltpu.make_async_remote_copy(
            src_ref=h_refs[a].at[2 * qx + qy], dst_ref=out_refs[a].at[mine], send_sem=send_sems.at[a, j],
            recv_sem=recv_sems.at[a, j], device_id=(qx, qy, c), device_id_type=MESH)
            for j, (qx, qy) in enumerate(chips) for a in range(n)]
        recvs = [pltpu.make_async_remote_copy(
            src_ref=h_refs[a].at[mine], dst_ref=out_refs[a].at[2 * qx + qy], send_sem=send_sems.at[a, j],
            recv_sem=recv_sems.at[a, j], device_id=(qx, qy, c), device_id_type=MESH)
            for j, (qx, qy) in enumerate(chips) for a in range(n)]
        return keep, sends, recvs

    def start(h_refs, out_refs, sems):
        keep, sends, _ = plan(h_refs, out_refs, sems)
        for cp in keep + sends:
            cp.start()

    def finish(h_refs, out_refs, sems):
        keep, sends, recvs = plan(h_refs, out_refs, sems)
        for cp in recvs:
            cp.wait_recv()
        for cp in sends:
            cp.wait_send()
        for cp in keep:
            cp.wait()

    return Rider(hs, [jax.ShapeDtypeStruct(h.shape, h.dtype) for h in hs],
                 [pltpu.SemaphoreType.DMA((n, 3)), pltpu.SemaphoreType.DMA((n, 3)), pltpu.SemaphoreType.DMA((n,))], start, finish)


def hosted_call(riders, body, *, out_shape, in_specs, out_specs, grid=(), scratch_shapes=(), **kw):
    riders = tuple(riders or ())
    if not riders:
        return pl.pallas_call(body, out_shape=out_shape, in_specs=in_specs, out_specs=out_specs, grid=grid,
                              scratch_shapes=scratch_shapes, **kw)
    single = not isinstance(out_shape, (list, tuple))
    k_out_shape = [out_shape] if single else list(out_shape)
    k_out_specs = [out_specs] if single else list(out_specs)
    n_in, n_out, n_scr = len(in_specs), len(k_out_shape), len(scratch_shapes)
    r_ins = [a for r in riders for a in r.ins]
    r_outs = [s for r in riders for s in r.out_shape]
    r_scr = [s for r in riders for s in r.scratch]

    def full_body(*refs):
        ins = refs[:n_in + len(r_ins)]
        outs = refs[n_in + len(r_ins):n_in + len(r_ins) + n_out + len(r_outs)]
        scr = refs[n_in + len(r_ins) + n_out + len(r_outs):]
        ids = [pl.program_id(d) for d in range(len(grid))]
        first = functools.reduce(jnp.logical_and, [i == 0 for i in ids]) if ids else None
        last = functools.reduce(jnp.logical_and, [i == g - 1 for i, g in zip(ids, grid)]) if ids else None

        def each(method):
            i0, o0, s0 = n_in, n_out, n_scr
            for r in riders:
                getattr(r, method)(ins[i0:i0 + len(r.ins)], outs[o0:o0 + len(r.out_shape)], scr[s0:s0 + len(r.scratch)])
                i0, o0, s0 = i0 + len(r.ins), o0 + len(r.out_shape), s0 + len(r.scratch)

        if first is None:
            each("start")
        else:
            pl.when(first)(lambda: each("start"))
        body(*ins[:n_in], *outs[:n_out], *scr[:n_scr])
        if last is None:
            each("finish")
        else:
            pl.when(last)(lambda: each("finish"))

    call = pl.pallas_call(
        full_body, out_shape=k_out_shape + r_outs, in_specs=list(in_specs) + [_HBM] * len(r_ins),
        out_specs=k_out_specs + [_HBM] * len(r_outs), grid=grid, scratch_shapes=list(scratch_shapes) + r_scr, **kw)

    def run(*args):
        res = call(*args, *r_ins)
        o0 = n_out
        for r in riders:
            r.results = list(res[o0:o0 + len(r.out_shape)])
            o0 += len(r.out_shape)
        return res[0] if single else list(res[:n_out])

    return run


def run_riders(riders, *, name):
    hosted_call(riders, lambda: None, name=name, out_shape=[], in_specs=[], out_specs=[])()
    return [r.results for r in riders]


def _pick(n, cands):
    for c in cands:
        if n % c == 0:
            return c
    return n


def _params(sem):
    return pltpu.CompilerParams(dimension_semantics=sem, vmem_limit_bytes=VMEM_LIMIT)


MATMUL_VMEM_BUDGET = 40 * 2 ** 20


def _matmul_tiles(m, n, k, bytes_a, bytes_b, bytes_mn, fixed):
    tm, tn, tk = fixed if fixed is not None else (None, None, None)
    tm = tm or _pick(m, (1024, 512, 256, 128))
    tn = tn or _pick(n, (512, 256, 128))
    if tk is None:
        for tk in [c for c in (2048, 1536, 1024, 512, 256, 128) if k % c == 0] + [k]:
            if 2 * (tm * tk * bytes_a + tk * tn * bytes_b + tm * tn * bytes_mn) + tm * tn * 4 <= MATMUL_VMEM_BUDGET:
                break
    return tm, tn, tk


def matmul(a, b, *, name, ta=False, tb=False, post=None, post_ins=(), extra_out=None, out_dtype=F32, tiles=None,
           b_view=None, out_view=None, riders=()):
    (k, m) = a.shape if ta else a.shape[::-1]
    (kb, n) = b_view[:2] if b_view is not None else (b.shape[::-1] if tb else b.shape)
    assert k == kb, (a.shape, b.shape, ta, tb)
    bytes_mn = sum(p.dtype.itemsize for p in post_ins) + jnp.dtype(out_dtype).itemsize
    bytes_mn += jnp.dtype(extra_out[1]).itemsize if extra_out else 0
    tm, tn, tk = _matmul_tiles(m, n, k, a.dtype.itemsize, b.dtype.itemsize, bytes_mn, tiles)
    nk = k // tk
    dims = ((0,) if ta else (1,), (1,) if tb else (0,))
    n_post = len(post_ins)
    n_out = 2 if extra_out else 1

    def body(*refs):
        a_ref, b_ref = refs[:2]
        post_refs = refs[2:2 + n_post]
        o_refs, acc = refs[-1 - n_out:-1], refs[-1]
        kk = pl.program_id(2)

        @pl.when(kk == 0)
        def _():
            acc[...] = jnp.zeros_like(acc)

        b_tile = b_ref[...]
        acc[...] += _bdot(a_ref[...], b_tile.reshape(-1, b_tile.shape[-1]), dims)

        @pl.when(kk == nk - 1)
        def _():
            r = acc[...]
            if post is not None:
                r = post(r, *[p[...] for p in post_refs])
            o_refs[0][...] = r.astype(out_dtype)
            if extra_out:
                o_refs[1][...] = extra_out[0](r).astype(extra_out[1])

    a_spec = pl.BlockSpec((tk, tm), lambda i, j, kk: (kk, i)) if ta else pl.BlockSpec((tm, tk), lambda i, j, kk: (i, kk))
    if b_view is not None:
        b_spec = b_view[2]
    else:
        b_spec = pl.BlockSpec((tn, tk), lambda i, j, kk: (j, kk)) if tb else pl.BlockSpec((tk, tn), lambda i, j, kk: (kk, j))
    mn_spec = pl.BlockSpec((tm, tn), lambda i, j, kk: (i, j))
    o_shape, o_spec, into = ((m, n), mn_spec, None) if out_view is None else out_view
    ins, specs, aliases = [a, b, *post_ins], [a_spec, b_spec] + [mn_spec] * n_post, {}
    if into is not None:
        aliases = {len(ins): 0}
        ins.append(into)
        specs.append(pl.BlockSpec(memory_space=pl.ANY))
    out_shape = [jax.ShapeDtypeStruct(o_shape, out_dtype)]
    out_specs = [o_spec]
    if extra_out:
        out_shape.append(jax.ShapeDtypeStruct((m, n), extra_out[1]))
        out_specs.append(mn_spec)
    res = hosted_call(
        riders, body, name=name, grid=(m // tm, n // tn, nk), in_specs=specs, out_specs=out_specs, out_shape=out_shape,
        input_output_aliases=aliases, scratch_shapes=[pltpu.VMEM((tm, tn), F32)],
        compiler_params=_params(("parallel", "parallel", "arbitrary")),
    )(*ins)
    return res if extra_out else res[0]


def rows_call(fn, row_ins, full_ins, row_outs, acc_outs, *, tm, name, riders=()):
    row_ins = [r if isinstance(r, tuple) else (r, r.shape[-1], 0) for r in row_ins]
    t = row_ins[0][0].shape[-2]
    tm = min(tm, t)
    n_in = len(row_ins) + len(full_ins)
    n_row = len(row_outs)

    def body(*refs):
        res = fn(*[[r[h] for h in range(r.shape[0])] if (i < len(row_ins) and len(r.shape) == 3) else r[...]
                   for i, r in enumerate(refs[:n_in])])
        res = res if isinstance(res, (tuple, list)) else (res,)
        outs = refs[n_in:]
        for ref, val in zip(outs[:n_row], res[:n_row]):
            if len(ref.shape) == 3:
                for h, vh in enumerate(val):
                    ref[h] = vh.astype(ref.dtype)
            else:
                ref[...] = val.astype(ref.dtype)
        first = pl.program_id(0) == 0
        for ref, val in zip(outs[n_row:], res[n_row:]):
            @pl.when(first)
            def _(ref=ref, val=val):
                ref[...] = val

            @pl.when(jnp.logical_not(first))
            def _(ref=ref, val=val):
                ref[...] += val

    def full_spec(shape):
        return pl.BlockSpec(shape, lambda i, nd=len(shape): (0,) * nd)

    def row_spec(lead, w, cb):
        if lead is None:
            return pl.BlockSpec((tm, w), lambda i: (i, cb))
        return pl.BlockSpec((lead, tm, w), lambda i: (0, i, cb))

    def lead_cols(c):
        return c if isinstance(c, tuple) else (None, c)

    in_specs = [row_spec(a.shape[0] if a.ndim == 3 else None, w, cb) for (a, w, cb) in row_ins]
    in_specs += [full_spec(f.shape) for f in full_ins]
    out_specs = [row_spec(*lead_cols(c), 0) for c, _ in row_outs] + [full_spec(s) for s in acc_outs]
    out_shape = [jax.ShapeDtypeStruct(tuple(d for d in (lead_cols(c)[0], t, lead_cols(c)[1]) if d is not None), dt)
                 for c, dt in row_outs] + [jax.ShapeDtypeStruct(s, F32) for s in acc_outs]
    res = hosted_call(
        riders, body, name=name, grid=(t // tm,), in_specs=in_specs, out_specs=out_specs, out_shape=out_shape,
        compiler_params=_params(("arbitrary",)),
    )(*[r[0] for r in row_ins], *full_ins)
    return res


def vjp_rows(fn, n_diff_row, row_diff_full):
    def bwd(*args, n_row, n_ct):
        prim_rows = args[:n_row]
        cts = args[n_row:n_row + n_ct]
        fulls = args[n_row + n_ct:]
        _, vjp = jax.vjp(fn, *prim_rows, *fulls)
        g = vjp(cts[0] if n_ct == 1 else tuple(cts))
        out = list(g[:n_diff_row])
        out += [gf for gf, d in zip(g[n_row:], row_diff_full) if d]
        return tuple(out)
    return bwd


def _shift_down(x, s):
    if s == 0:
        return x
    t = lax.broadcasted_iota(jnp.int32, x.shape, 0)
    return jnp.where(t >= s, pltpu.roll(x, s, 0), 0.0)


def _shift_up(x, s):
    if s == 0:
        return x
    n = x.shape[0]
    t = lax.broadcasted_iota(jnp.int32, x.shape, 0)
    return jnp.where(t < n - s, pltpu.roll(x, n - s, 0), 0.0)


def _conv(x, w_ref):
    return sum(w_ref[pl.ds(j, 1), :] * _shift_down(x, CONV_WIDTH - 1 - j) for j in range(CONV_WIDTH))


_DN_POST = (lambda c: l2n(jax.nn.silu(c)) * SCALE, lambda c: l2n(jax.nn.silu(c)), jax.nn.silu)


def dn_prep_fwd(proj, conv_w, *, name, riders=()):
    t = proj.shape[0]

    def body(xq, xk, xv, wq, wk, wv, oq, ok, ov):
        for x_ref, w_ref, o_ref, post in zip((xq, xk, xv), (wq, wk, wv), (oq, ok, ov), _DN_POST):
            o_ref[...] = post(_conv(x_ref[...], w_ref))

    x_specs = [pl.BlockSpec((t, HEAD_DIM), lambda h, g=g: (0, g * N_HEADS + h)) for g in range(3)]
    w_specs = [pl.BlockSpec((CONV_WIDTH, HEAD_DIM), lambda h, g=g: (0, g * N_HEADS + h)) for g in range(3)]
    o_spec = pl.BlockSpec((None, t, HEAD_DIM), lambda h: (h, 0, 0))
    return hosted_call(
        riders, body, name=name, grid=(N_HEADS,), in_specs=x_specs + w_specs, out_specs=[o_spec] * 3,
        out_shape=[jax.ShapeDtypeStruct((N_HEADS, t, HEAD_DIM), F32)] * 3, compiler_params=_params(("parallel",)),
    )(proj, proj, proj, conv_w, conv_w, conv_w)


def dn_prep_bwd(proj, conv_w, dq, dk, dv, *, name, riders=()):
    t = proj.shape[0]

    def body(xq, xk, xv, wq, wk, wv, gq, gk, gv, dxq, dxk, dxv, dwq, dwk, dwv):
        for x_ref, w_ref, g_ref, dx_ref, dw_ref, post in zip(
                (xq, xk, xv), (wq, wk, wv), (gq, gk, gv), (dxq, dxk, dxv), (dwq, dwk, dwv), _DN_POST):
            x = x_ref[...]
            _, vjp = jax.vjp(post, _conv(x, w_ref))
            dc, = vjp(g_ref[...])
            dx_ref[...] = sum(w_ref[pl.ds(j, 1), :] * _shift_up(dc, CONV_WIDTH - 1 - j) for j in range(CONV_WIDTH))
            for j in range(CONV_WIDTH):
                dw_ref[pl.ds(j, 1), :] = jnp.sum(dc * _shift_down(x, CONV_WIDTH - 1 - j), axis=0, keepdims=True)

    x_specs = [pl.BlockSpec((t, HEAD_DIM), lambda h, g=g: (0, g * N_HEADS + h)) for g in range(3)]
    w_specs = [pl.BlockSpec((CONV_WIDTH, HEAD_DIM), lambda h, g=g: (0, g * N_HEADS + h)) for g in range(3)]
    g_spec = pl.BlockSpec((None, t, HEAD_DIM), lambda h: (h, 0, 0))
    dx_spec = pl.BlockSpec((t, HEAD_DIM), lambda h: (0, h))
    dw_spec = pl.BlockSpec((CONV_WIDTH, HEAD_DIM), lambda h: (0, h))
    return hosted_call(
        riders, body, name=name, grid=(N_HEADS,), in_specs=x_specs + w_specs + [g_spec] * 3, out_specs=[dx_spec] * 3 + [dw_spec] * 3,
        out_shape=[jax.ShapeDtypeStruct((t, D_MODEL), F32)] * 3 + [jax.ShapeDtypeStruct((CONV_WIDTH, D_MODEL), F32)] * 3,
        compiler_params=_params(("parallel",)),
    )(proj, proj, proj, conv_w, conv_w, conv_w, dq, dk, dv)


INTRA_CHUNKS = 4


def _delta_specs(t):
    def spec(rows, w, index):
        return pl.BlockSpec((N_HEADS, rows, w), lambda i: (0, index(i), 0))
    return spec


def delta_intra_fwd(q, k, v, gc, beta, *, name, riders=()):
    t = q.shape[1]
    per = min(INTRA_CHUNKS, t // CHUNK)
    rows, nb = per * CHUNK, N_HEADS * per

    def body(q_ref, k_ref, v_ref, g_ref, b_ref, u_ref, w_ref, qk_ref):
        ins = [r[...].reshape(nb, CHUNK, r.shape[-1]) for r in (q_ref, k_ref, v_ref, g_ref, b_ref)]
        for ref, val in zip((u_ref, w_ref, qk_ref), delta_intra(*ins)):
            ref[...] = val.reshape(ref.shape)

    spec = _delta_specs(t)
    x_spec, g_spec, qk_spec = (spec(rows, w, lambda i: i) for w in (HEAD_DIM, 1, CHUNK))
    return hosted_call(
        riders, body, name=name, grid=(t // rows,), in_specs=[x_spec] * 3 + [g_spec] * 2, out_specs=[x_spec, x_spec, qk_spec],
        out_shape=[jax.ShapeDtypeStruct((N_HEADS, t, HEAD_DIM), F32)] * 2 + [jax.ShapeDtypeStruct((N_HEADS, t, CHUNK), F32)],
        compiler_params=_params(("parallel",)),
    )(q, k, v, gc, beta)


def delta_seq_fwd(q, k, gc, u, w, qk, *, name, riders=()):
    t = q.shape[1]
    nc = t // CHUNK

    def body(q_ref, k_ref, g_ref, u_ref, w_ref, qk_ref, o_ref, s0_ref, s_ref):
        @pl.when(pl.program_id(0) == 0)
        def _():
            s_ref[...] = jnp.zeros_like(s_ref)

        s = s_ref[...]
        s0_ref[...] = s
        o, s_new = delta_step(s, q_ref[...], k_ref[...], g_ref[...], u_ref[...], w_ref[...], qk_ref[...])
        o_ref[...] = o
        s_ref[...] = s_new

    spec = _delta_specs(t)
    x_spec, g_spec, qk_spec = (spec(CHUNK, w, lambda c: c) for w in (HEAD_DIM, 1, CHUNK))
    s_spec = pl.BlockSpec((N_HEADS, None, HEAD_DIM, HEAD_DIM), lambda c: (0, c, 0, 0))
    return hosted_call(
        riders, body, name=name, grid=(nc,), in_specs=[x_spec, x_spec, g_spec, x_spec, x_spec, qk_spec], out_specs=[x_spec, s_spec],
        out_shape=[jax.ShapeDtypeStruct((N_HEADS, t, HEAD_DIM), F32),
                   jax.ShapeDtypeStruct((N_HEADS, nc, HEAD_DIM, HEAD_DIM), F32)],
        scratch_shapes=[pltpu.VMEM((N_HEADS, HEAD_DIM, HEAD_DIM), F32)],
        compiler_params=_params(("arbitrary",)),
    )(q, k, gc, u, w, qk)


def delta_seq_bwd(q, k, gc, u, w, qk, s0, do, *, name, riders=()):
    t = q.shape[1]
    nc = t // CHUNK

    def body(q_ref, k_ref, g_ref, u_ref, w_ref, qk_ref, s0_ref, do_ref,
             dq_ref, dk_ref, dg_ref, du_ref, dw_ref, dqk_ref, ds_ref):
        @pl.when(pl.program_id(0) == 0)
        def _():
            ds_ref[...] = jnp.zeros_like(ds_ref)

        _, vjp = jax.vjp(delta_step, s0_ref[...], q_ref[...], k_ref[...], g_ref[...], u_ref[...], w_ref[...], qk_ref[...])
        ds, dq, dk, dg, du, dw, dqk = vjp((do_ref[...], ds_ref[...]))
        for ref, val in zip((ds_ref, dq_ref, dk_ref, dg_ref, du_ref, dw_ref, dqk_ref), (ds, dq, dk, dg, du, dw, dqk)):
            ref[...] = val

    spec = _delta_specs(t)
    x_spec, g_spec, qk_spec = (spec(CHUNK, w, lambda c: nc - 1 - c) for w in (HEAD_DIM, 1, CHUNK))
    s_spec = pl.BlockSpec((N_HEADS, None, HEAD_DIM, HEAD_DIM), lambda c: (0, nc - 1 - c, 0, 0))
    return hosted_call(
        riders, body, name=name, grid=(nc,), in_specs=[x_spec, x_spec, g_spec, x_spec, x_spec, qk_spec, s_spec, x_spec],
        out_specs=[x_spec, x_spec, g_spec, x_spec, x_spec, qk_spec],
        out_shape=[jax.ShapeDtypeStruct((N_HEADS, t, w_), F32) for w_ in (HEAD_DIM, HEAD_DIM, 1, HEAD_DIM, HEAD_DIM, CHUNK)],
        scratch_shapes=[pltpu.VMEM((N_HEADS, HEAD_DIM, HEAD_DIM), F32)],
        compiler_params=_params(("arbitrary",)),
    )(q, k, gc, u, w, qk, s0, do)


def delta_intra_bwd(q, k, v, gc, beta, du, dw, dqk, dq_s, dk_s, dg_s, *, name, riders=()):
    t = q.shape[1]
    per = min(INTRA_CHUNKS, t // CHUNK)
    rows, nb = per * CHUNK, N_HEADS * per

    def body(q_ref, k_ref, v_ref, g_ref, b_ref, du_ref, dw_ref, dqk_ref, dqs_ref, dks_ref, dgs_ref,
             dq_ref, dk_ref, dv_ref, dg_ref, db_ref):
        def chunks(r):
            return r[...].reshape(nb, CHUNK, r.shape[-1])
        _, vjp = jax.vjp(delta_intra, *[chunks(r) for r in (q_ref, k_ref, v_ref, g_ref, b_ref)])
        dq, dk, dv, dg, db = vjp(tuple(chunks(r) for r in (du_ref, dw_ref, dqk_ref)))
        dq_ref[...] = dq.reshape(dq_ref.shape) + dqs_ref[...]
        dk_ref[...] = dk.reshape(dk_ref.shape) + dks_ref[...]
        dv_ref[...] = dv.reshape(dv_ref.shape)
        dg_ref[...] = dg.reshape(dg_ref.shape) + dgs_ref[...]
        db_ref[...] = db.reshape(db_ref.shape)

    spec = _delta_specs(t)
    x_spec, g_spec, qk_spec = (spec(rows, w, lambda i: i) for w in (HEAD_DIM, 1, CHUNK))
    return hosted_call(
        riders, body, name=name, grid=(t // rows,),
        in_specs=[x_spec] * 3 + [g_spec] * 2 + [x_spec, x_spec, qk_spec, x_spec, x_spec, g_spec],
        out_specs=[x_spec] * 3 + [g_spec] * 2,
        out_shape=[jax.ShapeDtypeStruct((N_HEADS, t, HEAD_DIM), F32)] * 3 + [jax.ShapeDtypeStruct((N_HEADS, t, 1), F32)] * 2,
        compiler_params=_params(("parallel",)),
    )(q, k, v, gc, beta, du, dw, dqk, dq_s, dk_s, dg_s)


_V_BLOCK = 2 * N_HEADS
FOX_GROUPS = 8


def _fox_groups(t):
    nq = t // Q_BLOCK
    per = max(1, nq // FOX_GROUPS)
    return [(g0, per, (g0 + per) * Q_BLOCK) for g0 in range(0, nq, per)]


def fox_attn_fwd(q, k, proj, fq, fk, *, name, riders=()):
    t = q.shape[0]

    def body(q_ref, k_ref, v_ref, fq_ref, fk_ref, o_ref):
        for g0, per, keys in _fox_groups(t):
            def block(j, carry, g0=g0, keys=keys):
                rows = pl.ds(pl.multiple_of((g0 + j) * Q_BLOCK, Q_BLOCK), Q_BLOCK)
                o_ref[rows, :] = fox_block(q_ref[rows, :], k_ref[0:keys, :], v_ref[0:keys, :], fq_ref[rows, :],
                                           fk_ref[:, 0:keys], (g0 + j) * Q_BLOCK)
                return carry
            lax.fori_loop(0, per, block, 0)

    x_spec = pl.BlockSpec((t, HEAD_DIM), lambda h: (0, h))
    v_spec = pl.BlockSpec((t, HEAD_DIM), lambda h: (0, _V_BLOCK + h))
    fq_spec = pl.BlockSpec((None, t, 1), lambda h: (h, 0, 0))
    fk_spec = pl.BlockSpec((None, 1, t), lambda h: (h, 0, 0))
    return hosted_call(
        riders, body, name=name, grid=(N_HEADS,), in_specs=[x_spec, x_spec, v_spec, fq_spec, fk_spec], out_specs=x_spec,
        out_shape=jax.ShapeDtypeStruct((t, D_MODEL), F32), compiler_params=_params(("parallel",)),
    )(q, k, proj, fq, fk)


def fox_attn_bwd(q, k, proj, fq, fk, do, *, name, riders=()):
    t = q.shape[0]

    def body(q_ref, k_ref, v_ref, fq_ref, fk_ref, do_ref, dq_ref, dk_ref, dv_ref, dfq_ref, dfk_ref):
        dk_ref[...] = jnp.zeros_like(dk_ref)
        dv_ref[...] = jnp.zeros_like(dv_ref)
        dfk_ref[...] = jnp.zeros_like(dfk_ref)
        for g0, per, keys in _fox_groups(t):
            def block(j, carry, g0=g0, keys=keys):
                rows = pl.ds(pl.multiple_of((g0 + j) * Q_BLOCK, Q_BLOCK), Q_BLOCK)
                f = functools.partial(fox_block, qpos0=(g0 + j) * Q_BLOCK)
                _, vjp = jax.vjp(f, q_ref[rows, :], k_ref[0:keys, :], v_ref[0:keys, :], fq_ref[rows, :], fk_ref[:, 0:keys])
                dq, dk, dv, dfq, dfk = vjp(do_ref[rows, :])
                dq_ref[rows, :] = dq
                dfq_ref[rows, :] = dfq
                dk_ref[0:keys, :] += dk
                dv_ref[0:keys, :] += dv
                dfk_ref[:, 0:keys] += dfk
                return carry
            lax.fori_loop(0, per, block, 0)

    x_spec = pl.BlockSpec((t, HEAD_DIM), lambda h: (0, h))
    v_spec = pl.BlockSpec((t, HEAD_DIM), lambda h: (0, _V_BLOCK + h))
    fq_spec = pl.BlockSpec((None, t, 1), lambda h: (h, 0, 0))
    fk_spec = pl.BlockSpec((None, 1, t), lambda h: (h, 0, 0))
    return hosted_call(
        riders, body, name=name, grid=(N_HEADS,), in_specs=[x_spec, x_spec, v_spec, fq_spec, fk_spec, x_spec],
        out_specs=[x_spec, x_spec, x_spec, fq_spec, fk_spec],
        out_shape=[jax.ShapeDtypeStruct((t, D_MODEL), F32)] * 3
        + [jax.ShapeDtypeStruct((N_HEADS, t, 1), F32), jax.ShapeDtypeStruct((N_HEADS, 1, t), F32)],
        compiler_params=_params(("parallel",)),
    )(q, k, proj, fq, fk, do)


def memkv_fwd(mem, mnw, wkv, mknw, *, name):
    n = mem.shape[0]

    def body(mem_ref, mnw_ref, w_ref, mknw_ref, mk_ref, mv_ref):
        mk, mv = memkv_fn(mem_ref[...], mnw_ref[...], w_ref[...], mknw_ref[...])
        mk_ref[...] = mk
        mv_ref[...] = mv

    return pl.pallas_call(
        body, name=name, out_shape=[jax.ShapeDtypeStruct((n, MEM_WIDTH), F32)] * 2,
        compiler_params=pltpu.CompilerParams(vmem_limit_bytes=VMEM_LIMIT),
    )(mem, mnw, wkv, mknw)


def memkv_bwd(mem, mnw, wkv, mknw, dmk, dmv, *, name):
    def body(mem_ref, mnw_ref, w_ref, mknw_ref, dmk_ref, dmv_ref, dmnw_ref, dw_ref, dmknw_ref):
        f = functools.partial(memkv_fn, mem_ref[...])
        _, vjp = jax.vjp(f, mnw_ref[...], w_ref[...].astype(F32), mknw_ref[...])
        dmnw, dw, dmknw = vjp((dmk_ref[...], dmv_ref[...]))
        dmnw_ref[...] = dmnw
        dw_ref[...] = dw.astype(dw_ref.dtype)
        dmknw_ref[...] = dmknw

    return pl.pallas_call(
        body, name=name,
        out_shape=[jax.ShapeDtypeStruct(mnw.shape, F32), jax.ShapeDtypeStruct(wkv.shape, BF16), jax.ShapeDtypeStruct(mknw.shape, F32)],
        compiler_params=pltpu.CompilerParams(vmem_limit_bytes=VMEM_LIMIT),
    )(mem, mnw, wkv, mknw, dmk, dmv)


def _row(v, width=None):
    v = v.reshape(1, -1)
    if width is not None and v.shape[1] < width:
        v = jnp.pad(v, ((0, 0), (0, width - v.shape[1])))
    return v


def _head_cols(a):
    return a[:, :N_HEADS].T[:, :, None]


def _lanes_from_heads(*cols):
    t = cols[0].shape[1]
    parts = [c[:, :, 0].T for c in cols]
    parts.append(jnp.zeros((t, LANES - N_HEADS * len(cols)), F32))
    return jnp.concatenate(parts, axis=1)


def _norm_fwd(x, w, name, riders=()):
    return rows_call(lambda x, w: rms(x, w), [x], [w], [(D_MODEL, BF16)], [], tm=512, name=name, riders=riders)[0]


def _norm_bwd(x, w, dh, dx_in, name, riders=()):
    def fn(x, dh, dx_in, w):
        _, vjp = jax.vjp(rms, x, w)
        dx, dw = vjp(dh)
        return dx + dx_in, dw
    return rows_call(fn, [x, dh, dx_in], [w], [(D_MODEL, F32)], [(1, D_MODEL)], tm=512, name=name, riders=riders)


FF_PIECE = D_FF // N_DEV


def _add(r, x):
    return r + x


def _piece(rows, cols, index):
    return pl.BlockSpec((None, rows, cols), lambda i, j, kk: (index(i, j, kk), 0, 0))


def _two_pieces(rows, cols, index):
    return pl.BlockSpec((2, rows, cols), lambda i, j, kk: (index(i, j, kk), 0, 0))


def _mlp_fwd(x, n2w, w1, w2, layer, riders=()):
    riders = list(riders) + [None, None]
    h2 = _norm_fwd(x, n2w, f"norm2_fwd_{layer}")
    u, a1 = matmul(h2, w1, name=f"mlp1_fwd_{layer}", tiles=(None, FF_PIECE, D_MODEL),
                   extra_out=(lambda u: jnp.square(jnp.maximum(u, 0.0)), BF16),
                   b_view=(D_MODEL, D_FF, _piece(D_MODEL, FF_PIECE, lambda i, j, kk: j)), riders=riders[0])
    y = matmul(a1, w2, name=f"mlp2_fwd_{layer}", post=_add, post_ins=[x], tiles=(None, D_MODEL, 2 * FF_PIECE),
               b_view=(D_FF, D_MODEL, _two_pieces(FF_PIECE, D_MODEL, lambda i, j, kk: kk)), riders=riders[1])
    return y, (x, h2, u, a1)


def pair_sum(g, got, *, name):
    _, rows, cols = g.shape
    tile = _pick(rows, (512, 256, 128))
    c = lax.axis_index("c").astype(jnp.int32).reshape(1)

    def body(c_ref, a_ref, b_ref, o_ref):
        o_ref[...] = (a_ref[...].astype(F32) + b_ref[...].astype(F32)).astype(o_ref.dtype)

    grid_spec = pltpu.PrefetchScalarGridSpec(
        num_scalar_prefetch=1, grid=(4, rows // tile),
        in_specs=[pl.BlockSpec((None, tile, cols), lambda k, i, c_ref: (2 * k + c_ref[0], i, 0)),
                  pl.BlockSpec((None, tile, cols), lambda k, i, c_ref: (k, i, 0))],
        out_specs=pl.BlockSpec((None, tile, cols), lambda k, i, c_ref: (k, i, 0)))
    return pl.pallas_call(
        body, name=name, grid_spec=grid_spec, out_shape=jax.ShapeDtypeStruct((4, rows, cols), g.dtype),
        compiler_params=_params(("parallel", "parallel")),
    )(c, g, got)


def chip_sums(names, pieces, gots):
    return [pair_sum(a, got, name=f"grads_pair_sum_{n}") for n, a, got in zip(names, pieces, gots)]


def _mlp_bwd(dy, res, n2w, w1, w2, layer, riders=()):
    x, h2, u, a1 = res
    du = matmul(dy, w2, tb=True, name=f"mlp2_dx_{layer}", out_dtype=BF16, tiles=(None, 2 * FF_PIECE, D_MODEL),
                post=lambda r, u: r * (2.0 * jnp.maximum(u, 0.0)), post_ins=[u],
                b_view=(D_MODEL, D_FF, _two_pieces(FF_PIECE, D_MODEL, lambda i, j, kk: j)), riders=riders)
    dw2 = matmul(a1, dy, ta=True, name=f"mlp2_dw_{layer}", out_dtype=BF16, tiles=(FF_PIECE, D_MODEL, None), out_view=(
        w2.shape, _piece(FF_PIECE, D_MODEL, lambda i, j, kk: i), None))
    sib2 = sibling_rider([dw2])
    dh2 = matmul(du, w1, tb=True, name=f"mlp1_dx_{layer}", tiles=(None, D_MODEL, FF_PIECE),
                 b_view=(D_FF, D_MODEL, _piece(D_MODEL, FF_PIECE, lambda i, j, kk: kk)), riders=[sib2])
    dw1 = matmul(h2, du, ta=True, name=f"mlp1_dw_{layer}", out_dtype=BF16, tiles=(D_MODEL, FF_PIECE, None), out_view=(
        w1.shape, _piece(D_MODEL, FF_PIECE, lambda i, j, kk: j), None))
    sib1 = sibling_rider([dw1])
    dx, dn2w = _norm_bwd(x, n2w, dh2, dy, f"norm2_bwd_{layer}", riders=[sib1])
    return dx, dw1, dw2, dn2w, sib1, sib2


def _in_proj_bwd(h, dmain, dsmall, w_main, w_small, tag):
    dh = matmul(dmain, w_main, tb=True, name=f"inproj_dx_main_{tag}")
    dh = matmul(dsmall, w_small, tb=True, post=_add, post_ins=[dh], name=f"inproj_dx_small_{tag}")
    dw_main = matmul(h, dmain, ta=True, out_dtype=BF16, name=f"inproj_dw_main_{tag}")
    dw_small = matmul(h, dsmall, ta=True, out_dtype=BF16, name=f"inproj_dw_small_{tag}")
    return dh, dw_main, dw_small


def local_step(x, mem, target, w):
    t = x.shape[0]
    n_mem = mem.shape[0]
    g = {}

    def wire(a):
        return a.astype(BF16)

    dn_g, conv_g, kv_g = run_riders(
        [gather_rider([wire(w["dn_w_in"][0]), w["dn_conv_w"][0], wire(w["w_mem_kv"])])], name="weights_gather_first")[0]
    dn_main, dn_ab = in_proj_weights(dn_g, DN_IN, 2 * N_HEADS)
    conv_w = conv_g.transpose(1, 0, 2).reshape(CONV_WIDTH, 3 * D_MODEL)
    w_kv = kv_g.reshape(D_MODEL, D_MODEL)
    ride_mlp1_0 = gather_rider([wire(w["w_mlp1"][0])])
    ride_out = gather_rider([wire(w["w_out"][0]), wire(w["w_out"][1])])
    ride_mlp2_0 = gather_rider([wire(w["w_mlp2"][0])])
    ride_fox = gather_rider([wire(w["fox_w_in"][0])])
    ride_mlp_1 = gather_rider([wire(w["w_mlp1"][1]), wire(w["w_mlp2"][1])])

    mnw, mknw = _row(w["mem_norm_w"]), _row(w["mem_k_norm_w"])
    mk, mv = memkv_fwd(mem, mnw, w_kv, mknw, name="memkv_fwd")

    n1w0, n2w0 = _row(w["norm1_w"][0]), _row(w["norm2_w"][0])
    alog, dtb = _row(w["dn_a_log"][0], LANES), _row(w["dn_dt_bias"][0], LANES)
    onw, mqw0 = _row(w["dn_o_norm_w"][0]), _row(w["memq_norm_w"][0])
    x0 = x
    h0 = _norm_fwd(x0, n1w0, "norm1_fwd_0")
    pm0 = matmul(h0, dn_main, name="inproj_main_0", riders=[ride_mlp1_0])
    ps0 = matmul(h0, dn_ab, name="inproj_small_0")
    gates = rows_call(dn_gates_fn, [ps0], [alog, dtb], [(LANES, F32)], [], tm=512, name="dn_gates_fwd")[0]
    gc, beta = _head_cols(gates), _head_cols(gates[:, N_HEADS:])
    q0, k0, v0 = dn_prep_fwd(pm0, conv_w, name="dn_prep_fwd", riders=[ride_out])
    u0, w0, qk0 = delta_intra_fwd(q0, k0, v0, gc, beta, name="delta_intra_fwd", riders=[ride_mlp2_0])
    o0, s_start = delta_seq_fwd(q0, k0, gc, u0, w0, qk0, name="delta_seq_fwd", riders=[ride_fox])
    cat0 = rows_call(dn_out_fn, [o0, (pm0, D_MODEL, 3), (pm0, MEM_WIDTH, 8)], [onw, mqw0, mk, mv],
                     [(D_MODEL + MEM_WIDTH, BF16)], [], tm=256, name="dn_out_fwd")[0]
    w_out0, w_out1 = (a.reshape(OUT_IN, D_MODEL) for a in ride_out.results)
    (w1_0,), (w2_0,) = ride_mlp1_0.results, ride_mlp2_0.results
    fox_main, fox_f = in_proj_weights(ride_fox.results[0], FOX_IN, N_HEADS)
    x1 = matmul(cat0, w_out0, post=_add, post_ins=[x0], name="wout_fwd_0")
    x2, mlp_res0 = _mlp_fwd(x1, n2w0, w1_0, w2_0, 0)

    n1w1, n2w1 = _row(w["norm1_w"][1]), _row(w["norm2_w"][1])
    fbias = _row(w["fox_f_bias"][0], LANES)
    qnw, knw, mqw1 = _row(w["fox_q_norm_w"][0]), _row(w["fox_k_norm_w"][0]), _row(w["memq_norm_w"][1])
    h1 = _norm_fwd(x2, n1w1, "norm1_fwd_1")
    pm1 = matmul(h1, fox_main, name="inproj_main_1")
    ps1 = matmul(h1, fox_f, name="inproj_small_1")
    fcum = rows_call(fox_fcum_fn, [ps1], [fbias], [(LANES, F32)], [], tm=t, name="fox_fcum_fwd")[0]
    fq = _head_cols(fcum)
    fk = jnp.swapaxes(fq, 1, 2)
    q1, k1 = rows_call(fox_qk_fn, [(pm1, D_MODEL, 0), (pm1, D_MODEL, 1)], [qnw, knw], [(D_MODEL, F32)] * 2, [], tm=256,
                       name="fox_qk_fwd")
    o1 = fox_attn_fwd(q1, k1, pm1, fq, fk, name="fox_attn_fwd", riders=[ride_mlp_1])
    cat1 = rows_call(fox_out_fn, [o1, (pm1, D_MODEL, 3), (pm1, MEM_WIDTH, 8)], [mqw1, mk, mv],
                     [(D_MODEL + MEM_WIDTH, BF16)], [], tm=256, name="fox_out_fwd")[0]
    w1_1, w2_1 = ride_mlp_1.results
    x3 = matmul(cat1, w_out1, post=_add, post_ins=[x2], name="wout_fwd_1")
    y, mlp_res1 = _mlp_fwd(x3, n2w1, w1_1, w2_1, 1)

    def loss_fn(y, tgt):
        e = y - tgt
        return e * (1.0 / D_MODEL), jnp.sum(jnp.sum(e * e, axis=1, keepdims=True), axis=0, keepdims=True)
    dy, sq = rows_call(loss_fn, [y, target], [], [(D_MODEL, F32)], [(1, 1)], tm=512, name="loss")
    loss = sq[0, 0] * (0.5 / D_MODEL)

    dx3, dw1_1, dw2_1, dn2w1, sib1, sib2 = _mlp_bwd(dy, mlp_res1, n2w1, w1_1, w2_1, 1)
    dcat1 = matmul(dx3, w_out1, tb=True, name="wout_dx_1")
    dwo_1 = matmul(cat1, dx3, ta=True, out_dtype=BF16, name="wout_dw_1").reshape(N_DEV, OUT_IN // N_DEV, D_MODEL)
    sibo = sibling_rider([dwo_1])
    do1, dgate1, dqm1, dmqw1, dmk1, dmv1 = rows_call(
        functools.partial(vjp_rows(fox_out_fn, 3, (True, True, True)), n_row=3, n_ct=1),
        [o1, (pm1, D_MODEL, 3), (pm1, MEM_WIDTH, 8), dcat1], [mqw1, mk, mv],
        [(D_MODEL, F32), (D_MODEL, F32), (MEM_WIDTH, F32)], [(1, HEAD_DIM), (n_mem, MEM_WIDTH), (n_mem, MEM_WIDTH)],
        tm=256, name="fox_out_bwd", riders=[sibo])
    ride_l1 = chips_rider(chip_sums(["w_mlp2_1", "w_mlp1_1", "w_out_1"], [dw2_1, dw1_1, dwo_1],
                                    sib2.results + sib1.results + sibo.results))
    dq1, dk1, dv1, dfq, dfk = fox_attn_bwd(q1, k1, pm1, fq, fk, do1, name="fox_attn_bwd", riders=[ride_l1])
    dqraw1, dkraw1, dqnw, dknw = rows_call(
        functools.partial(vjp_rows(fox_qk_fn, 2, (True, True)), n_row=2, n_ct=2),
        [(pm1, D_MODEL, 0), (pm1, D_MODEL, 1), dq1, dk1], [qnw, knw],
        [(D_MODEL, F32)] * 2, [(1, HEAD_DIM)] * 2, tm=256, name="fox_qk_bwd")
    dfcum = _lanes_from_heads(dfq + jnp.swapaxes(dfk, 1, 2))
    dps1, dfbias = rows_call(
        functools.partial(vjp_rows(fox_fcum_fn, 1, (True,)), n_row=1, n_ct=1),
        [ps1, dfcum], [fbias], [(LANES, F32)], [(1, LANES)], tm=t, name="fox_fcum_bwd")
    dpm1 = jnp.concatenate([dqraw1, dkraw1, dv1, dgate1, dqm1], axis=1)
    dh1, dwmain1, dwsmall1 = _in_proj_bwd(h1, dpm1, dps1, fox_main, fox_f, "1")
    g_fox = in_proj_pieces(dwmain1, dwsmall1, N_HEADS, FOX_IN)
    sibf = sibling_rider([g_fox])
    dx2, dn1w1 = _norm_bwd(x2, n1w1, dh1, dx3, "norm1_bwd_1", riders=[sibf])
    ride_fox_g = chips_rider(chip_sums(["fox_w_in"], [g_fox], sibf.results))

    dx1, dw1_0, dw2_0, dn2w0, sib1, sib2 = _mlp_bwd(dx2, mlp_res0, n2w0, w1_0, w2_0, 0)
    dcat0 = matmul(dx1, w_out0, tb=True, name="wout_dx_0")
    dwo_0 = matmul(cat0, dx1, ta=True, out_dtype=BF16, name="wout_dw_0").reshape(N_DEV, OUT_IN // N_DEV, D_MODEL)
    sibo = sibling_rider([dwo_0])
    do0, dz0, dqm0, donw, dmqw0, dmk0, dmv0 = rows_call(
        functools.partial(vjp_rows(dn_out_fn, 3, (True, True, True, True)), n_row=3, n_ct=1),
        [o0, (pm0, D_MODEL, 3), (pm0, MEM_WIDTH, 8), dcat0], [onw, mqw0, mk, mv],
        [((N_HEADS, HEAD_DIM), F32), (D_MODEL, F32), (MEM_WIDTH, F32)],
        [(1, HEAD_DIM), (1, HEAD_DIM), (n_mem, MEM_WIDTH), (n_mem, MEM_WIDTH)], tm=256, name="dn_out_bwd", riders=[sibo])
    h_l0 = chip_sums(["w_mlp2_0", "w_mlp1_0", "w_out_0"], [dw2_0, dw1_0, dwo_0], sib2.results + sib1.results + sibo.results)
    ride_l0_mlp, ride_l0_out = chips_rider(h_l0[:2]), chips_rider(h_l0[2:])
    dq_s, dk_s, dg_s, du0, dw0, dqk0 = delta_seq_bwd(q0, k0, gc, u0, w0, qk0, s_start, do0, name="delta_seq_bwd",
                                                     riders=[ride_fox_g])
    dq0, dk0, dv0, dgc, dbeta = delta_intra_bwd(q0, k0, v0, gc, beta, du0, dw0, dqk0, dq_s, dk_s, dg_s,
                                                name="delta_intra_bwd", riders=[ride_l0_mlp])
    dxq, dxk, dxv, dcq, dck, dcv = dn_prep_bwd(pm0, conv_w, dq0, dk0, dv0, name="dn_prep_bwd", riders=[ride_l0_out])
    dconv = jnp.concatenate([dcq, dck, dcv], axis=1)
    dps0, dalog, ddtb = rows_call(
        functools.partial(vjp_rows(dn_gates_fn, 1, (True, True)), n_row=1, n_ct=1),
        [ps0, _lanes_from_heads(dgc, dbeta)], [alog, dtb], [(LANES, F32)], [(1, LANES)] * 2, tm=512, name="dn_gates_bwd")
    dpm0 = jnp.concatenate([dxq, dxk, dxv, dz0, dqm0], axis=1)
    dh0, dwmain0, dwsmall0 = _in_proj_bwd(h0, dpm0, dps0, dn_main, dn_ab, "0")
    g_dn = in_proj_pieces(dwmain0, dwsmall0, 2 * N_HEADS, DN_IN)
    g_conv = dconv.reshape(CONV_WIDTH, N_DEV, -1).transpose(1, 0, 2).astype(BF16)
    sibd = sibling_rider([g_dn, g_conv])
    grad_x, dn1w0 = _norm_bwd(x0, n1w0, dh0, dx1, "norm1_bwd_0", riders=[sibd])

    dmnw, dwkv, dmknw = memkv_bwd(mem, mnw, w_kv, mknw, dmk0 + dmk1, dmv0 + dmv1, name="memkv_bwd")
    g_kv = dwkv.reshape(N_DEV, D_MODEL // N_DEV, D_MODEL)
    sibk = sibling_rider([g_kv])
    run_riders([sibk], name="grads_to_sibling_last")

    g["mem_norm_w"] = dmnw[0]
    g["mem_k_norm_w"] = dmknw[0]
    g["norm1_w"] = jnp.concatenate([dn1w0, dn1w1], axis=0)
    g["dn_a_log"] = dalog[:, :N_HEADS]
    g["dn_dt_bias"] = ddtb[:, :N_HEADS]
    g["dn_o_norm_w"] = donw
    g["fox_f_bias"] = dfbias[:, :N_HEADS]
    g["fox_q_norm_w"] = dqnw
    g["fox_k_norm_w"] = dknw
    g["memq_norm_w"] = jnp.concatenate([dmqw0, dmqw1], axis=0)
    g["norm2_w"] = jnp.concatenate([dn2w0, dn2w1], axis=0)

    ride_last = chips_rider(chip_sums(["dn_w_in", "dn_conv_w", "w_mem_kv"], [g_dn, g_conv, g_kv], sibd.results + sibk.results))
    ride_small = gather_rider([pack_small(g)])
    run_riders([ride_last, ride_small], name="grads_to_chips_last")

    def layers(l0, l1):
        return jnp.stack([l0, l1], axis=1).reshape(4, -1, l0.shape[-1])

    parts = {
        "w_mlp2": layers(ride_l0_mlp.results[0], ride_l1.results[0]),
        "w_mlp1": layers(ride_l0_mlp.results[1], ride_l1.results[1]),
        "w_out": layers(ride_l0_out.results[0], ride_l1.results[2]),
        "fox_w_in": ride_fox_g.results[0],
        "dn_w_in": ride_last.results[0], "dn_conv_w": ride_last.results[1], "w_mem_kv": ride_last.results[2],
    }
    return loss, grad_x, parts, ride_small.results[0]


WEIGHTS = ["mem_norm_w", "w_mem_kv", "mem_k_norm_w", "norm1_w", "dn_w_in", "dn_conv_w", "dn_a_log", "dn_dt_bias",
           "dn_o_norm_w", "fox_w_in", "fox_f_bias", "fox_q_norm_w", "fox_k_norm_w", "memq_norm_w", "w_out", "norm2_w",
           "w_mlp1", "w_mlp2"]
DN_IN = 4 * D_MODEL + 2 * N_HEADS + MEM_WIDTH
FOX_IN = 4 * D_MODEL + N_HEADS + MEM_WIDTH
GATE_END = 4 * D_MODEL
OUT_IN = D_MODEL + MEM_WIDTH
BIG = [("w_mem_kv", D_MODEL // N_DEV, D_MODEL), ("dn_w_in", D_MODEL, DN_IN // N_DEV), ("fox_w_in", D_MODEL, FOX_IN // N_DEV),
       ("dn_conv_w", CONV_WIDTH, 3 * D_MODEL // N_DEV), ("w_out", 2 * OUT_IN // N_DEV, D_MODEL),
       ("w_mlp1", 2 * D_MODEL, FF_PIECE), ("w_mlp2", 2 * FF_PIECE, D_MODEL)]
SMALL = [("mem_norm_w", (D_MODEL,), D_MODEL), ("mem_k_norm_w", (HEAD_DIM,), HEAD_DIM), ("norm1_w", (2, D_MODEL), 2 * D_MODEL),
         ("dn_a_log", (1, N_HEADS), LANES), ("dn_dt_bias", (1, N_HEADS), LANES), ("dn_o_norm_w", (1, HEAD_DIM), HEAD_DIM),
         ("fox_f_bias", (1, N_HEADS), LANES), ("fox_q_norm_w", (1, HEAD_DIM), HEAD_DIM), ("fox_k_norm_w", (1, HEAD_DIM), HEAD_DIM),
         ("memq_norm_w", (2, HEAD_DIM), 2 * HEAD_DIM), ("norm2_w", (2, D_MODEL), 2 * D_MODEL)]
SMALL_ROWS = 16


def pack_small(p):
    flat = jnp.concatenate([jnp.pad(p[n].reshape(-1), (0, ln - math.prod(sh))) for n, sh, ln in SMALL])
    return jnp.pad(flat, (0, SMALL_ROWS * PACK_W - flat.shape[0])).reshape(SMALL_ROWS, PACK_W)


def unpack_small(pk):
    flat, off, out = pk.reshape(-1), 0, {}
    for n, sh, ln in SMALL:
        out[n] = flat[off:off + math.prod(sh)].reshape(sh)
        off += ln
    return out


def shard_matrices(p):
    return [p[n].reshape(r, c) for n, r, c in BIG]


def in_proj_weights(gathered, width, n_small):
    full = gathered.transpose(1, 0, 2).reshape(D_MODEL, width)
    main = jnp.concatenate([full[:, :GATE_END], full[:, GATE_END + n_small:]], axis=1)
    return main, jnp.pad(full[:, GATE_END:GATE_END + n_small], ((0, 0), (0, LANES - n_small)))


def in_proj_pieces(d_main, d_small, n_small, width):
    full = jnp.concatenate([d_main[:, :GATE_END], d_small[:, :n_small], d_main[:, GATE_END:]], axis=1)
    return full.reshape(D_MODEL, N_DEV, width // N_DEV).transpose(1, 0, 2)


def adamw(parts, w, m, v, *, name):
    n, rows, cols = parts.shape
    tile = _pick(rows, (512, 256, 128))

    def body(p_ref, w_ref, m_ref, v_ref, g_ref, d_ref, mo_ref, vo_ref):
        g = p_ref[0].astype(F32)
        for i in range(1, n):
            g = g + p_ref[i].astype(F32)
        m_new = ADAM_B1 * m_ref[...] + (1.0 - ADAM_B1) * g
        v_new = ADAM_B2 * v_ref[...] + (1.0 - ADAM_B2) * jnp.square(g)
        m_hat = m_new / (1.0 - ADAM_B1 ** ADAM_STEP)
        v_hat = v_new / (1.0 - ADAM_B2 ** ADAM_STEP)
        g_ref[...] = g
        d_ref[...] = -ADAM_LR * (m_hat / (jnp.sqrt(v_hat) + ADAM_EPS) + ADAM_WD * w_ref[...])
        mo_ref[...] = m_new
        vo_ref[...] = v_new

    spec = pl.BlockSpec((tile, cols), lambda i: (i, 0))
    return pl.pallas_call(
        body, name=name, grid=(rows // tile,),
        in_specs=[pl.BlockSpec((n, tile, cols), lambda i: (0, i, 0)), spec, spec, spec], out_specs=[spec] * 4,
        out_shape=[jax.ShapeDtypeStruct((rows, cols), F32)] * 4, compiler_params=_params(("parallel",)),
    )(parts, w, m, v)


def kernel(x, mem, mem_norm_w, w_mem_kv, mem_k_norm_w, norm1_w, dn_w_in, dn_conv_w, dn_a_log, dn_dt_bias, dn_o_norm_w, fox_w_in, fox_f_bias, fox_q_norm_w, fox_k_norm_w, memq_norm_w, w_out, norm2_w, w_mlp1, w_mlp2, loss_target, m_mem_norm_w, m_w_mem_kv, m_mem_k_norm_w, m_norm1_w, m_dn_w_in, m_dn_conv_w, m_dn_a_log, m_dn_dt_bias, m_dn_o_norm_w, m_fox_w_in, m_fox_f_bias, m_fox_q_norm_w, m_fox_k_norm_w, m_memq_norm_w, m_w_out, m_norm2_w, m_w_mlp1, m_w_mlp2, v_mem_norm_w, v_w_mem_kv, v_mem_k_norm_w, v_norm1_w, v_dn_w_in, v_dn_conv_w, v_dn_a_log, v_dn_dt_bias, v_dn_o_norm_w, v_fox_w_in, v_fox_f_bias, v_fox_q_norm_w, v_fox_k_norm_w, v_memq_norm_w, v_w_out, v_norm2_w, v_w_mlp1, v_w_mlp2):
    p = dict(mem_norm_w=mem_norm_w, w_mem_kv=w_mem_kv, mem_k_norm_w=mem_k_norm_w, norm1_w=norm1_w, dn_w_in=dn_w_in,
             dn_conv_w=dn_conv_w, dn_a_log=dn_a_log, dn_dt_bias=dn_dt_bias, dn_o_norm_w=dn_o_norm_w, fox_w_in=fox_w_in,
             fox_f_bias=fox_f_bias, fox_q_norm_w=fox_q_norm_w, fox_k_norm_w=fox_k_norm_w, memq_norm_w=memq_norm_w,
             w_out=w_out, norm2_w=norm2_w, w_mlp1=w_mlp1, w_mlp2=w_mlp2)
    pm = dict(mem_norm_w=m_mem_norm_w, w_mem_kv=m_w_mem_kv, mem_k_norm_w=m_mem_k_norm_w, norm1_w=m_norm1_w,
              dn_w_in=m_dn_w_in, dn_conv_w=m_dn_conv_w, dn_a_log=m_dn_a_log, dn_dt_bias=m_dn_dt_bias,
              dn_o_norm_w=m_dn_o_norm_w, fox_w_in=m_fox_w_in, fox_f_bias=m_fox_f_bias, fox_q_norm_w=m_fox_q_norm_w,
              fox_k_norm_w=m_fox_k_norm_w, memq_norm_w=m_memq_norm_w, w_out=m_w_out, norm2_w=m_norm2_w, w_mlp1=m_w_mlp1,
              w_mlp2=m_w_mlp2)
    pv = dict(mem_norm_w=v_mem_norm_w, w_mem_kv=v_w_mem_kv, mem_k_norm_w=v_mem_k_norm_w, norm1_w=v_norm1_w,
              dn_w_in=v_dn_w_in, dn_conv_w=v_dn_conv_w, dn_a_log=v_dn_a_log, dn_dt_bias=v_dn_dt_bias,
              dn_o_norm_w=v_dn_o_norm_w, fox_w_in=v_fox_w_in, fox_f_bias=v_fox_f_bias, fox_q_norm_w=v_fox_q_norm_w,
              fox_k_norm_w=v_fox_k_norm_w, memq_norm_w=v_memq_norm_w, w_out=v_w_out, norm2_w=v_norm2_w, w_mlp1=v_w_mlp1,
              w_mlp2=v_w_mlp2)

    loss, grad_x, parts, small_parts = local_step(x[0], mem[0], loss_target[0], p)
    loss = lax.psum(loss, ("x", "y", "c"))

    results = {}
    for (n, _, _), w_, m_, v_ in zip(BIG, shard_matrices(p), shard_matrices(pm), shard_matrices(pv)):
        results[n] = [o.reshape(p[n].shape) for o in adamw(parts[n], w_, m_, v_, name=f"adamw_{n}")]
    small = [unpack_small(o) for o in adamw(small_parts, pack_small(p), pack_small(pm), pack_small(pv), name="adamw_small")]
    groups = [{**small[i], **{n: r[i] for n, r in results.items()}} for i in range(4)]
    return (loss, grad_x[None], *[grp[n] for grp in groups for n in WEIGHTS])
```

```python
import functools
import math

import jax
import jax.numpy as jnp
from jax import lax
from jax.experimental import pallas as pl
from jax.experimental.pallas import tpu as pltpu

F32 = jnp.float32
BF16 = jnp.bfloat16
HIGHEST = lax.Precision.HIGHEST

D_MODEL = 1024
HEAD_DIM = 128
N_HEADS = 8
MEM_HEADS = 4
MEM_WIDTH = MEM_HEADS * HEAD_DIM
D_FF = 4 * D_MODEL
CONV_WIDTH = 4
CHUNK = 64
Q_BLOCK = 128
EPS = 1e-6
SCALE = HEAD_DIM ** -0.5
MAIN_WIDTH = 4 * D_MODEL + MEM_WIDTH
LANES = 128
N_DEV = 8
PACK_W = 512

ADAM_LR = 0.001
ADAM_B1 = 0.9
ADAM_B2 = 0.999
ADAM_EPS = 1e-08
ADAM_WD = 0.01
ADAM_STEP = 10

VMEM_LIMIT = 56 * 2 ** 20
MESH = pl.DeviceIdType.MESH


def _bdot(a, b, dims):
    return lax.dot_general(a.astype(BF16), b.astype(BF16), (dims, ((), ())), preferred_element_type=F32)


@jax.custom_vjp
def mm(a, b):
    return _bdot(a, b, ((1,), (0,)))


@jax.custom_vjp
def mm_nt(a, b):
    return _bdot(a, b, ((1,), (1,)))


@jax.custom_vjp
def mm_tn(a, b):
    return _bdot(a, b, ((0,), (0,)))


mm.defvjp(lambda a, b: (mm(a, b), (a, b)), lambda r, g: (mm_nt(g, r[1]), mm_tn(r[0], g)))
mm_nt.defvjp(lambda a, b: (mm_nt(a, b), (a, b)), lambda r, g: (mm(g, r[1]), mm_tn(g, r[0])))
mm_tn.defvjp(lambda a, b: (mm_tn(a, b), (a, b)), lambda r, g: (mm_nt(r[1], g), mm(r[0], g)))


def hdot(a, b):
    return jnp.dot(a, b, precision=HIGHEST, preferred_element_type=F32)


def rms(x, w):
    return x * lax.rsqrt(jnp.mean(x * x, axis=-1, keepdims=True) + EPS) * w


def l2n(x):
    return x * lax.rsqrt(jnp.sum(x * x, axis=-1, keepdims=True) + EPS)


def _iota2(n, m):
    return lax.broadcasted_iota(jnp.int32, (n, m), 0), lax.broadcasted_iota(jnp.int32, (n, m), 1)


def _lower_ones(n):
    r, c = _iota2(n, n)
    return jnp.where(r >= c, 1.0, 0.0).astype(F32)


def _last_row(x):
    r = lax.broadcasted_iota(jnp.int32, x.shape, 0)
    return jnp.sum(jnp.where(r == x.shape[0] - 1, x, 0.0), axis=0, keepdims=True)


def _softmax_rows(z):
    m = lax.stop_gradient(jnp.max(z, axis=-1, keepdims=True))
    e = jnp.exp(z - m)
    return e * (1.0 / jnp.sum(e, axis=-1, keepdims=True))


_BNN = (((2,), (1,)), ((0,), (0,)))
_BNT = (((2,), (2,)), ((0,), (0,)))
_BTN = (((1,), (1,)), ((0,), (0,)))


def _bbdot(a, b, dims):
    return lax.dot_general(a.astype(BF16), b.astype(BF16), dims, preferred_element_type=F32)


@jax.custom_vjp
def bmm(a, b):
    return _bbdot(a, b, _BNN)


@jax.custom_vjp
def bmm_nt(a, b):
    return _bbdot(a, b, _BNT)


@jax.custom_vjp
def bmm_tn(a, b):
    return _bbdot(a, b, _BTN)


@jax.custom_vjp
def bmm_high(a, b):
    return lax.dot_general(a, b, _BNN, precision=lax.Precision.HIGH, preferred_element_type=F32)


bmm.defvjp(lambda a, b: (bmm(a, b), (a, b)), lambda r, g: (bmm_nt(g, r[1]), bmm_tn(r[0], g)))
bmm_nt.defvjp(lambda a, b: (bmm_nt(a, b), (a, b)), lambda r, g: (bmm(g, r[1]), bmm_tn(g, r[0])))
bmm_tn.defvjp(lambda a, b: (bmm_tn(a, b), (a, b)), lambda r, g: (bmm_nt(r[1], g), bmm(r[0], g)))
bmm_high.defvjp(lambda a, b: (bmm_high(a, b), (a, b)), lambda r, g: (bmm_nt(g, r[1]), bmm_tn(r[0], g)))

NEUMANN_HIGH_LEVELS = 2


def inv_unit_lower(a):
    n = a.shape[-1]
    r, c = _iota2(n, n)
    p = jnp.where(r == c, 1.0, 0.0).astype(F32) - a
    ak = a
    for level in range(int(math.log2(n)) - 1):
        dot = bmm_high if level < NEUMANN_HIGH_LEVELS else bmm
        ak = dot(ak, ak)
        p = p + dot(p, ak)
    return p


def delta_intra(q, k, v, gc, beta):
    b, c, _ = q.shape
    r, cc = _iota2(c, c)
    causal = r >= cc
    strict = r > cc
    gi = jnp.broadcast_to(gc, (b, c, c))
    gj = jnp.swapaxes(gi, 1, 2)
    decay = jnp.where(causal, jnp.exp(jnp.where(causal, gi - gj, 0.0)), 0.0)
    kb = k * beta
    a = jnp.where(strict, bmm_nt(kb, k) * decay, 0.0)
    t = inv_unit_lower(a)
    u = bmm(t, v * beta)
    w = bmm(t, kb * jnp.exp(gc))
    qk = jnp.where(causal, bmm_nt(q, k) * decay, 0.0)
    return u, w, qk


def delta_step(s, q, k, gc, u, w, qk):
    v_new = u - bmm(w, s)
    out = bmm(q * jnp.exp(gc), s) + bmm(qk, v_new)
    r = lax.broadcasted_iota(jnp.int32, gc.shape, 1)
    g_last = jnp.sum(jnp.where(r == gc.shape[1] - 1, gc, 0.0), axis=1, keepdims=True)
    k_dec = k * jnp.exp(g_last - gc)
    s_new = s * jnp.exp(g_last) + bmm_tn(k_dec, v_new)
    return out, s_new


def fox_probs(q, k, fq, fk, qpos0):
    s = lax.dot_general(q, k, (((1,), (1,)), ((), ())), preferred_element_type=F32)
    r, c = _iota2(s.shape[0], s.shape[1])
    return _softmax_rows(jnp.where(c <= (r + qpos0), s + (fq - fk), -jnp.inf))


def mem_head(qm, wq, mk, mv):
    p = _softmax_rows(mm_nt(rms(qm, wq) * SCALE, mk))
    return mm(p, mv)


def _heads(x, n):
    return [x[:, h * HEAD_DIM:(h + 1) * HEAD_DIM] for h in range(n)]


def memkv_fn(mem, mnw, wkv, mknw):
    kv = mm(rms(mem, mnw), wkv)
    mk = jnp.concatenate([rms(kh, mknw) for kh in _heads(kv[:, :MEM_WIDTH], MEM_HEADS)], axis=1)
    return mk, kv[:, MEM_WIDTH:]


def dn_gates_fn(ab, alog, dtb):
    g = -jnp.exp(alog) * jax.nn.softplus(ab + dtb)
    low = _lower_ones(CHUNK)
    gc = jnp.concatenate([hdot(low, g[i * CHUNK:(i + 1) * CHUNK]) for i in range(ab.shape[0] // CHUNK)], axis=0)
    lane = lax.broadcasted_iota(jnp.int32, ab.shape, 1)
    return jnp.where(lane < N_HEADS, gc, jax.nn.sigmoid(ab))


def fox_fcum_fn(fp, fbias):
    lf = jax.nn.log_sigmoid(fp + fbias)
    low = _lower_ones(LANES)
    carry = jnp.zeros((1, fp.shape[1]), F32)
    outs = []
    for i in range(fp.shape[0] // LANES):
        cs = hdot(low, lf[i * LANES:(i + 1) * LANES]) + carry
        carry = _last_row(cs)
        outs.append(cs)
    return jnp.concatenate(outs, axis=0)


def fox_qk_fn(qraw, kraw, qnw, knw):
    q = jnp.concatenate([rms(x, qnw) * SCALE for x in _heads(qraw, N_HEADS)], axis=1)
    k = jnp.concatenate([rms(x, knw) for x in _heads(kraw, N_HEADS)], axis=1)
    return q, k


def _mem_out(qm, mqw, mk, mv):
    return [mem_head(a, mqw, b, c) for a, b, c in zip(_heads(qm, MEM_HEADS), _heads(mk, MEM_HEADS), _heads(mv, MEM_HEADS))]


def dn_out_fn(o, z, qm, onw, mqw, mk, mv):
    mix = [rms(a, onw) * jax.nn.silu(b) for a, b in zip(o, _heads(z, N_HEADS))]
    return jnp.concatenate(mix + _mem_out(qm, mqw, mk, mv), axis=1)


def fox_out_fn(o, gate, qm, mqw, mk, mv):
    return jnp.concatenate([o * jax.nn.sigmoid(gate)] + _mem_out(qm, mqw, mk, mv), axis=1)


_HBM = pl.BlockSpec(memory_space=pltpu.HBM)


def _place():
    return lax.axis_index("x"), lax.axis_index("y"), lax.axis_index("c")


class Rider:
    def __init__(self, ins, out_shape, scratch, start, finish):
        self.ins, self.out_shape, self.scratch, self.start, self.finish = list(ins), list(out_shape), list(scratch), start, finish
        self.results = None


def gather_rider(xs):
    n = len(xs)

    def plan(x_refs, out_refs, sems):
        send_sems, recv_sems, local_sems = sems
        x, y, c = _place()
        me, sibling = (x, y, c), (x, y, 1 - c)
        chips = [(1 - x, y), (x, 1 - y), (1 - x, 1 - y)]

        def copy(a, k, block, to, src=None):
            px, py, pc = block
            dst = out_refs[a].at[4 * px + 2 * py + pc]
            return pltpu.make_async_remote_copy(
                src_ref=dst if src is None else src, dst_ref=dst,
                send_sem=send_sems.at[a, k], recv_sem=recv_sems.at[a, k], device_id=to, device_id_type=MESH)

        mine = [pltpu.make_async_copy(x_refs[a], out_refs[a].at[4 * x + 2 * y + c], local_sems.at[a]) for a in range(n)]
        first = [copy(a, 0, me, sibling, src=x_refs[a]) for a in range(n)]
        first += [copy(a, 1 + j, me, (*chip, c), src=x_refs[a]) for j, chip in enumerate(chips) for a in range(n)]
        return copy, me, sibling, chips, mine, first

    def start(x_refs, out_refs, sems):
        _, _, _, _, mine, first = plan(x_refs, out_refs, sems)
        for cp in mine + first:
            cp.start()

    def finish(x_refs, out_refs, sems):
        copy, me, sibling, chips, mine, first = plan(x_refs, out_refs, sems)
        _, _, c = me
        passed = []
        for j, chip in enumerate(chips):
            for a in range(n):
                copy(a, 1 + j, (*chip, c), me).wait_recv()
                passed.append(copy(a, 4 + j, (*chip, c), sibling))
                passed[-1].start()
        for a in range(n):
            copy(a, 0, sibling, me).wait_recv()
        for j, chip in enumerate(chips):
            for a in range(n):
                copy(a, 4 + j, (*chip, 1 - c), me).wait_recv()
        for cp in first + passed:
            cp.wait_send()
        for cp in mine:
            cp.wait()

    return Rider(xs, [jax.ShapeDtypeStruct((N_DEV,) + a.shape, a.dtype) for a in xs],
                 [pltpu.SemaphoreType.DMA((n, 7)), pltpu.SemaphoreType.DMA((n, 7)), pltpu.SemaphoreType.DMA((n,))], start, finish)


def sibling_rider(gs):
    n = len(gs)

    def plan(g_refs, out_refs, sems):
        send_sems, recv_sems = sems
        x, y, c = _place()
        return [pltpu.make_async_remote_copy(
            src_ref=g_refs[a].at[2 * k + 1 - c], dst_ref=out_refs[a].at[k], send_sem=send_sems.at[a, k],
            recv_sem=recv_sems.at[a, k], device_id=(x, y, 1 - c), device_id_type=MESH) for a in range(n) for k in range(4)]

    def start(g_refs, out_refs, sems):
        for cp in plan(g_refs, out_refs, sems):
            cp.start()

    def finish(g_refs, out_refs, sems):
        copies = plan(g_refs, out_refs, sems)
        for cp in copies:
            cp.wait_recv()
        for cp in copies:
            cp.wait_send()

    return Rider(gs, [jax.ShapeDtypeStruct((4,) + g.shape[1:], g.dtype) for g in gs],
                 [pltpu.SemaphoreType.DMA((n, 4)), pltpu.SemaphoreType.DMA((n, 4))], start, finish)


def chips_rider(hs):
    n = len(hs)

    def plan(h_refs, out_refs, sems):
        send_sems, recv_sems, local_sems = sems
        x, y, c = _place()
        mine = 2 * x + y
        chips = [(1 - x, y), (x, 1 - y), (1 - x, 1 - y)]
        keep = [pltpu.make_async_copy(h_refs[a].at[mine], out_refs[a].at[mine], local_sems.at[a]) for a in range(n)]
        sends = [pltpu.make_async_remote_copy(
            src_ref=h_refs[a].at[2 * qx + qy], dst_ref=out_refs[a].at[mine], send_sem=send_sems.at[a, j],
            recv_sem=recv_sems.at[a, j], device_id=(qx, qy, c), device_id_type=MESH)
            for j, (qx, qy) in enumerate(chips) for a in range(n)]
        recvs = [pltpu.make_async_remote_copy(
            src_ref=h_refs[a].at[mine], dst_ref=out_refs[a].at[2 * qx + qy], send_sem=send_sems.at[a, j],
            recv_sem=recv_sems.at[a, j], device_id=(qx, qy, c), device_id_type=MESH)
            for j, (qx, qy) in enumerate(chips) for a in range(n)]
        return keep, sends, recvs

    def start(h_refs, out_refs, sems):
        keep, sends, _ = plan(h_refs, out_refs, sems)
        for cp in keep + sends:
            cp.start()

    def finish(h_refs, out_refs, sems):
        keep, sends, recvs = plan(h_refs, out_refs, sems)
        for cp in recvs:
            cp.wait_recv()
        for cp in sends:
            cp.wait_send()
        for cp in keep:
            cp.wait()

    return Rider(hs, [jax.ShapeDtypeStruct(h.shape, h.dtype) for h in hs],
                 [pltpu.SemaphoreType.DMA((n, 3)), pltpu.SemaphoreType.DMA((n, 3)), pltpu.SemaphoreType.DMA((n,))], start, finish)


def hosted_call(riders, body, *, out_shape, in_specs, out_specs, grid=(), scratch_shapes=(), **kw):
    riders = tuple(riders or ())
    if not riders:
        return pl.pallas_call(body, out_shape=out_shape, in_specs=in_specs, out_specs=out_specs, grid=grid,
                              scratch_shapes=scratch_shapes, **kw)
    single = not isinstance(out_shape, (list, tuple))
    k_out_shape = [out_shape] if single else list(out_shape)
    k_out_specs = [out_specs] if single else list(out_specs)
    n_in, n_out, n_scr = len(in_specs), len(k_out_shape), len(scratch_shapes)
    r_ins = [a for r in riders for a in r.ins]
    r_outs = [s for r in riders for s in r.out_shape]
    r_scr = [s for r in riders for s in r.scratch]

    def full_body(*refs):
        ins = refs[:n_in + len(r_ins)]
        outs = refs[n_in + len(r_ins):n_in + len(r_ins) + n_out + len(r_outs)]
        scr = refs[n_in + len(r_ins) + n_out + len(r_outs):]
        ids = [pl.program_id(d) for d in range(len(grid))]
        first = functools.reduce(jnp.logical_and, [i == 0 for i in ids]) if ids else None
        last = functools.reduce(jnp.logical_and, [i == g - 1 for i, g in zip(ids, grid)]) if ids else None

        def each(method):
            i0, o0, s0 = n_in, n_out, n_scr
            for r in riders:
                getattr(r, method)(ins[i0:i0 + len(r.ins)], outs[o0:o0 + len(r.out_shape)], scr[s0:s0 + len(r.scratch)])
                i0, o0, s0 = i0 + len(r.ins), o0 + len(r.out_shape), s0 + len(r.scratch)

        if first is None:
            each("start")
        else:
            pl.when(first)(lambda: each("start"))
        body(*ins[:n_in], *outs[:n_out], *scr[:n_scr])
        if last is None:
            each("finish")
        else:
            pl.when(last)(lambda: each("finish"))

    call = pl.pallas_call(
        full_body, out_shape=k_out_shape + r_outs, in_specs=list(in_specs) + [_HBM] * len(r_ins),
        out_specs=k_out_specs + [_HBM] * len(r_outs), grid=grid, scratch_shapes=list(scratch_shapes) + r_scr, **kw)

    def run(*args):
        res = call(*args, *r_ins)
        o0 = n_out
        for r in riders:
            r.results = list(res[o0:o0 + len(r.out_shape)])
            o0 += len(r.out_shape)
        return res[0] if single else list(res[:n_out])

    return run


def run_riders(riders, *, name):
    hosted_call(riders, lambda: None, name=name, out_shape=[], in_specs=[], out_specs=[])()
    return [r.results for r in riders]


def _pick(n, cands):
    for c in cands:
        if n % c == 0:
            return c
    return n


def _params(sem):
    return pltpu.CompilerParams(dimension_semantics=sem, vmem_limit_bytes=VMEM_LIMIT)


MATMUL_VMEM_BUDGET = 40 * 2 ** 20


def _matmul_tiles(m, n, k, bytes_a, bytes_b, bytes_mn, fixed):
    tm, tn, tk = fixed if fixed is not None else (None, None, None)
    tm = tm or _pick(m, (1024, 512, 256, 128))
    tn = tn or _pick(n, (512, 256, 128))
    if tk is None:
        for tk in [c for c in (2048, 1536, 1024, 512, 256, 128) if k % c == 0] + [k]:
            if 2 * (tm * tk * bytes_a + tk * tn * bytes_b + tm * tn * bytes_mn) + tm * tn * 4 <= MATMUL_VMEM_BUDGET:
                break
    return tm, tn, tk


def matmul(a, b, *, name, ta=False, tb=False, post=None, post_ins=(), extra_out=None, out_dtype=F32, tiles=None,
           b_view=None, out_view=None, riders=()):
    (k, m) = a.shape if ta else a.shape[::-1]
    (kb, n) = b_view[:2] if b_view is not None else (b.shape[::-1] if tb else b.shape)
    assert k == kb, (a.shape, b.shape, ta, tb)
    bytes_mn = sum(p.dtype.itemsize for p in post_ins) + jnp.dtype(out_dtype).itemsize
    bytes_mn += jnp.dtype(extra_out[1]).itemsize if extra_out else 0
    tm, tn, tk = _matmul_tiles(m, n, k, a.dtype.itemsize, b.dtype.itemsize, bytes_mn, tiles)
    nk = k // tk
    dims = ((0,) if ta else (1,), (1,) if tb else (0,))
    n_post = len(post_ins)
    n_out = 2 if extra_out else 1

    def body(*refs):
        a_ref, b_ref = refs[:2]
        post_refs = refs[2:2 + n_post]
        o_refs, acc = refs[-1 - n_out:-1], refs[-1]
        kk = pl.program_id(2)

        @pl.when(kk == 0)
        def _():
            acc[...] = jnp.zeros_like(acc)

        b_tile = b_ref[...]
        acc[...] += _bdot(a_ref[...], b_tile.reshape(-1, b_tile.shape[-1]), dims)

        @pl.when(kk == nk - 1)
        def _():
            r = acc[...]
            if post is not None:
                r = post(r, *[p[...] for p in post_refs])
            o_refs[0][...] = r.astype(out_dtype)
            if extra_out:
                o_refs[1][...] = extra_out[0](r).astype(extra_out[1])

    a_spec = pl.BlockSpec((tk, tm), lambda i, j, kk: (kk, i)) if ta else pl.BlockSpec((tm, tk), lambda i, j, kk: (i, kk))
    if b_view is not None:
        b_spec = b_view[2]
    else:
        b_spec = pl.BlockSpec((tn, tk), lambda i, j, kk: (j, kk)) if tb else pl.BlockSpec((tk, tn), lambda i, j, kk: (kk, j))
    mn_spec = pl.BlockSpec((tm, tn), lambda i, j, kk: (i, j))
    o_shape, o_spec, into = ((m, n), mn_spec, None) if out_view is None else out_view
    ins, specs, aliases = [a, b, *post_ins], [a_spec, b_spec] + [mn_spec] * n_post, {}
    if into is not None:
        aliases = {len(ins): 0}
        ins.append(into)
        specs.append(pl.BlockSpec(memory_space=pl.ANY))
    out_shape = [jax.ShapeDtypeStruct(o_shape, out_dtype)]
    out_specs = [o_spec]
    if extra_out:
        out_shape.append(jax.ShapeDtypeStruct((m, n), extra_out[1]))
        out_specs.append(mn_spec)
    res = hosted_call(
        riders, body, name=name, grid=(m // tm, n // tn, nk), in_specs=specs, out_specs=out_specs, out_shape=out_shape,
        input_output_aliases=aliases, scratch_shapes=[pltpu.VMEM((tm, tn), F32)],
        compiler_params=_params(("parallel", "parallel", "arbitrary")),
    )(*ins)
    return res if extra_out else res[0]


def rows_call(fn, row_ins, full_ins, row_outs, acc_outs, *, tm, name, riders=()):
    row_ins = [r if isinstance(r, tuple) else (r, r.shape[-1], 0) for r in row_ins]
    t = row_ins[0][0].shape[-2]
    tm = min(tm, t)
    n_in = len(row_ins) + len(full_ins)
    n_row = len(row_outs)

    def body(*refs):
        res = fn(*[[r[h] for h in range(r.shape[0])] if (i < len(row_ins) and len(r.shape) == 3) else r[...]
                   for i, r in enumerate(refs[:n_in])])
        res = res if isinstance(res, (tuple, list)) else (res,)
        outs = refs[n_in:]
        for ref, val in zip(outs[:n_row], res[:n_row]):
            if len(ref.shape) == 3:
                for h, vh in enumerate(val):
                    ref[h] = vh.astype(ref.dtype)
            else:
                ref[...] = val.astype(ref.dtype)
        first = pl.program_id(0) == 0
        for ref, val in zip(outs[n_row:], res[n_row:]):
            @pl.when(first)
            def _(ref=ref, val=val):
                ref[...] = val

            @pl.when(jnp.logical_not(first))
            def _(ref=ref, val=val):
                ref[...] += val

    def full_spec(shape):
        return pl.BlockSpec(shape, lambda i, nd=len(shape): (0,) * nd)

    def row_spec(lead, w, cb):
        if lead is None:
            return pl.BlockSpec((tm, w), lambda i: (i, cb))
        return pl.BlockSpec((lead, tm, w), lambda i: (0, i, cb))

    def lead_cols(c):
        return c if isinstance(c, tuple) else (None, c)

    in_specs = [row_spec(a.shape[0] if a.ndim == 3 else None, w, cb) for (a, w, cb) in row_ins]
    in_specs += [full_spec(f.shape) for f in full_ins]
    out_specs = [row_spec(*lead_cols(c), 0) for c, _ in row_outs] + [full_spec(s) for s in acc_outs]
    out_shape = [jax.ShapeDtypeStruct(tuple(d for d in (lead_cols(c)[0], t, lead_cols(c)[1]) if d is not None), dt)
                 for c, dt in row_outs] + [jax.ShapeDtypeStruct(s, F32) for s in acc_outs]
    res = hosted_call(
        riders, body, name=name, grid=(t // tm,), in_specs=in_specs, out_specs=out_specs, out_shape=out_shape,
        compiler_params=_params(("arbitrary",)),
    )(*[r[0] for r in row_ins], *full_ins)
    return res


def vjp_rows(fn, n_diff_row, row_diff_full):
    def bwd(*args, n_row, n_ct):
        prim_rows = args[:n_row]
        cts = args[n_row:n_row + n_ct]
        fulls = args[n_row + n_ct:]
        _, vjp = jax.vjp(fn, *prim_rows, *fulls)
        g = vjp(cts[0] if n_ct == 1 else tuple(cts))
        out = list(g[:n_diff_row])
        out += [gf for gf, d in zip(g[n_row:], row_diff_full) if d]
        return tuple(out)
    return bwd


def _shift_down(x, s):
    if s == 0:
        return x
    t = lax.broadcasted_iota(jnp.int32, x.shape, 0)
    return jnp.where(t >= s, pltpu.roll(x, s, 0), 0.0)


def _shift_up(x, s):
    if s == 0:
        return x
    n = x.shape[0]
    t = lax.broadcasted_iota(jnp.int32, x.shape, 0)
    return jnp.where(t < n - s, pltpu.roll(x, n - s, 0), 0.0)


def _conv(x, w_ref):
    return sum(w_ref[pl.ds(j, 1), :] * _shift_down(x, CONV_WIDTH - 1 - j) for j in range(CONV_WIDTH))


_DN_POST = (lambda c: l2n(jax.nn.silu(c)) * SCALE, lambda c: l2n(jax.nn.silu(c)), jax.nn.silu)


def dn_prep_fwd(proj, conv_w, *, name, riders=()):
    t = proj.shape[0]

    def body(xq, xk, xv, wq, wk, wv, oq, ok, ov):
        for x_ref, w_ref, o_ref, post in zip((xq, xk, xv), (wq, wk, wv), (oq, ok, ov), _DN_POST):
            o_ref[...] = post(_conv(x_ref[...], w_ref))

    x_specs = [pl.BlockSpec((t, HEAD_DIM), lambda h, g=g: (0, g * N_HEADS + h)) for g in range(3)]
    w_specs = [pl.BlockSpec((CONV_WIDTH, HEAD_DIM), lambda h, g=g: (0, g * N_HEADS + h)) for g in range(3)]
    o_spec = pl.BlockSpec((None, t, HEAD_DIM), lambda h: (h, 0, 0))
    return hosted_call(
        riders, body, name=name, grid=(N_HEADS,), in_specs=x_specs + w_specs, out_specs=[o_spec] * 3,
        out_shape=[jax.ShapeDtypeStruct((N_HEADS, t, HEAD_DIM), F32)] * 3, compiler_params=_params(("parallel",)),
    )(proj, proj, proj, conv_w, conv_w, conv_w)


def dn_prep_bwd(proj, conv_w, dq, dk, dv, *, name, riders=()):
    t = proj.shape[0]

    def body(xq, xk, xv, wq, wk, wv, gq, gk, gv, dxq, dxk, dxv, dwq, dwk, dwv):
        for x_ref, w_ref, g_ref, dx_ref, dw_ref, post in zip(
                (xq, xk, xv), (wq, wk, wv), (gq, gk, gv), (dxq, dxk, dxv), (dwq, dwk, dwv), _DN_POST):
            x = x_ref[...]
            _, vjp = jax.vjp(post, _conv(x, w_ref))
            dc, = vjp(g_ref[...])
            dx_ref[...] = sum(w_ref[pl.ds(j, 1), :] * _shift_up(dc, CONV_WIDTH - 1 - j) for j in range(CONV_WIDTH))
            for j in range(CONV_WIDTH):
                dw_ref[pl.ds(j, 1), :] = jnp.sum(dc * _shift_down(x, CONV_WIDTH - 1 - j), axis=0, keepdims=True)

    x_specs = [pl.BlockSpec((t, HEAD_DIM), lambda h, g=g: (0, g * N_HEADS + h)) for g in range(3)]
    w_specs = [pl.BlockSpec((CONV_WIDTH, HEAD_DIM), lambda h, g=g: (0, g * N_HEADS + h)) for g in range(3)]
    g_spec = pl.BlockSpec((None, t, HEAD_DIM), lambda h: (h, 0, 0))
    dx_spec = pl.BlockSpec((t, HEAD_DIM), lambda h: (0, h))
    dw_spec = pl.BlockSpec((CONV_WIDTH, HEAD_DIM), lambda h: (0, h))
    return hosted_call(
        riders, body, name=name, grid=(N_HEADS,), in_specs=x_specs + w_specs + [g_spec] * 3, out_specs=[dx_spec] * 3 + [dw_spec] * 3,
        out_shape=[jax.ShapeDtypeStruct((t, D_MODEL), F32)] * 3 + [jax.ShapeDtypeStruct((CONV_WIDTH, D_MODEL), F32)] * 3,
        compiler_params=_params(("parallel",)),
    )(proj, proj, proj, conv_w, conv_w, conv_w, dq, dk, dv)


INTRA_CHUNKS = 4


def _delta_specs(t):
    def spec(rows, w, index):
        return pl.BlockSpec((N_HEADS, rows, w), lambda i: (0, index(i), 0))
    return spec


def delta_intra_fwd(q, k, v, gc, beta, *, name, riders=()):
    t = q.shape[1]
    per = min(INTRA_CHUNKS, t // CHUNK)
    rows, nb = per * CHUNK, N_HEADS * per

    def body(q_ref, k_ref, v_ref, g_ref, b_ref, u_ref, w_ref, qk_ref):
        ins = [r[...].reshape(nb, CHUNK, r.shape[-1]) for r in (q_ref, k_ref, v_ref, g_ref, b_ref)]
        for ref, val in zip((u_ref, w_ref, qk_ref), delta_intra(*ins)):
            ref[...] = val.reshape(ref.shape)

    spec = _delta_specs(t)
    x_spec, g_spec, qk_spec = (spec(rows, w, lambda i: i) for w in (HEAD_DIM, 1, CHUNK))
    return hosted_call(
        riders, body, name=name, grid=(t // rows,), in_specs=[x_spec] * 3 + [g_spec] * 2, out_specs=[x_spec, x_spec, qk_spec],
        out_shape=[jax.ShapeDtypeStruct((N_HEADS, t, HEAD_DIM), F32)] * 2 + [jax.ShapeDtypeStruct((N_HEADS, t, CHUNK), F32)],
        compiler_params=_params(("parallel",)),
    )(q, k, v, gc, beta)


def delta_seq_fwd(q, k, gc, u, w, qk, *, name, riders=()):
    t = q.shape[1]
    nc = t // CHUNK

    def body(q_ref, k_ref, g_ref, u_ref, w_ref, qk_ref, o_ref, s0_ref, s_ref):
        @pl.when(pl.program_id(0) == 0)
        def _():
            s_ref[...] = jnp.zeros_like(s_ref)

        s = s_ref[...]
        s0_ref[...] = s
        o, s_new = delta_step(s, q_ref[...], k_ref[...], g_ref[...], u_ref[...], w_ref[...], qk_ref[...])
        o_ref[...] = o
        s_ref[...] = s_new

    spec = _delta_specs(t)
    x_spec, g_spec, qk_spec = (spec(CHUNK, w, lambda c: c) for w in (HEAD_DIM, 1, CHUNK))
    s_spec = pl.BlockSpec((N_HEADS, None, HEAD_DIM, HEAD_DIM), lambda c: (0, c, 0, 0))
    return hosted_call(
        riders, body, name=name, grid=(nc,), in_specs=[x_spec, x_spec, g_spec, x_spec, x_spec, qk_spec], out_specs=[x_spec, s_spec],
        out_shape=[jax.ShapeDtypeStruct((N_HEADS, t, HEAD_DIM), F32),
                   jax.ShapeDtypeStruct((N_HEADS, nc, HEAD_DIM, HEAD_DIM), F32)],
        scratch_shapes=[pltpu.VMEM((N_HEADS, HEAD_DIM, HEAD_DIM), F32)],
        compiler_params=_params(("arbitrary",)),
    )(q, k, gc, u, w, qk)


def delta_seq_bwd(q, k, gc, u, w, qk, s0, do, *, name, riders=()):
    t = q.shape[1]
    nc = t // CHUNK

    def body(q_ref, k_ref, g_ref, u_ref, w_ref, qk_ref, s0_ref, do_ref,
             dq_ref, dk_ref, dg_ref, du_ref, dw_ref, dqk_ref, ds_ref):
        @pl.when(pl.program_id(0) == 0)
        def _():
            ds_ref[...] = jnp.zeros_like(ds_ref)

        _, vjp = jax.vjp(delta_step, s0_ref[...], q_ref[...], k_ref[...], g_ref[...], u_ref[...], w_ref[...], qk_ref[...])
        ds, dq, dk, dg, du, dw, dqk = vjp((do_ref[...], ds_ref[...]))
        for ref, val in zip((ds_ref, dq_ref, dk_ref, dg_ref, du_ref, dw_ref, dqk_ref), (ds, dq, dk, dg, du, dw, dqk)):
            ref[...] = val

    spec = _delta_specs(t)
    x_spec, g_spec, qk_spec = (spec(CHUNK, w, lambda c: nc - 1 - c) for w in (HEAD_DIM, 1, CHUNK))
    s_spec = pl.BlockSpec((N_HEADS, None, HEAD_DIM, HEAD_DIM), lambda c: (0, nc - 1 - c, 0, 0))
    return hosted_call(
        riders, body, name=name, grid=(nc,), in_specs=[x_spec, x_spec, g_spec, x_spec, x_spec, qk_spec, s_spec, x_spec],
        out_specs=[x_spec, x_spec, g_spec, x_spec, x_spec, qk_spec],
        out_shape=[jax.ShapeDtypeStruct((N_HEADS, t, w_), F32) for w_ in (HEAD_DIM, HEAD_DIM, 1, HEAD_DIM, HEAD_DIM, CHUNK)],
        scratch_shapes=[pltpu.VMEM((N_HEADS, HEAD_DIM, HEAD_DIM), F32)],
        compiler_params=_params(("arbitrary",)),
    )(q, k, gc, u, w, qk, s0, do)


def delta_intra_bwd(q, k, v, gc, beta, du, dw, dqk, dq_s, dk_s, dg_s, *, name, riders=()):
    t = q.shape[1]
    per = min(INTRA_CHUNKS, t // CHUNK)
    rows, nb = per * CHUNK, N_HEADS * per

    def body(q_ref, k_ref, v_ref, g_ref, b_ref, du_ref, dw_ref, dqk_ref, dqs_ref, dks_ref, dgs_ref,
             dq_ref, dk_ref, dv_ref, dg_ref, db_ref):
        def chunks(r):
            return r[...].reshape(nb, CHUNK, r.shape[-1])
        _, vjp = jax.vjp(delta_intra, *[chunks(r) for r in (q_ref, k_ref, v_ref, g_ref, b_ref)])
        dq, dk, dv, dg, db = vjp(tuple(chunks(r) for r in (du_ref, dw_ref, dqk_ref)))
        dq_ref[...] = dq.reshape(dq_ref.shape) + dqs_ref[...]
        dk_ref[...] = dk.reshape(dk_ref.shape) + dks_ref[...]
        dv_ref[...] = dv.reshape(dv_ref.shape)
        dg_ref[...] = dg.reshape(dg_ref.shape) + dgs_ref[...]
        db_ref[...] = db.reshape(db_ref.shape)

    spec = _delta_specs(t)
    x_spec, g_spec, qk_spec = (spec(rows, w, lambda i: i) for w in (HEAD_DIM, 1, CHUNK))
    return hosted_call(
        riders, body, name=name, grid=(t // rows,),
        in_specs=[x_spec] * 3 + [g_spec] * 2 + [x_spec, x_spec, qk_spec, x_spec, x_spec, g_spec],
        out_specs=[x_spec] * 3 + [g_spec] * 2,
        out_shape=[jax.ShapeDtypeStruct((N_HEADS, t, HEAD_DIM), F32)] * 3 + [jax.ShapeDtypeStruct((N_HEADS, t, 1), F32)] * 2,
        compiler_params=_params(("parallel",)),
    )(q, k, v, gc, beta, du, dw, dqk, dq_s, dk_s, dg_s)


_V_BLOCK = 2 * N_HEADS
FOX_GROUPS = 8


def _fox_groups(t):
    nq = t // Q_BLOCK
    per = max(1, nq // FOX_GROUPS)
    return [(g0, per, (g0 + per) * Q_BLOCK) for g0 in range(0, nq, per)]


def fox_attn_fwd(q, k, proj, fq, fk, *, name, riders=()):
    t = q.shape[0]

    def body(q_ref, k_ref, v_ref, fq_ref, fk_ref, o_ref, kb_ref, vb_ref):
        kb_ref[...] = k_ref[...].astype(BF16)
        vb_ref[...] = v_ref[...].astype(BF16)
        for g0, per, keys in _fox_groups(t):
            def block(j, carry, g0=g0, keys=keys):
                rows = pl.ds(pl.multiple_of((g0 + j) * Q_BLOCK, Q_BLOCK), Q_BLOCK)
                p = fox_probs(q_ref[rows, :].astype(BF16), kb_ref[0:keys, :], fq_ref[rows, :], fk_ref[:, 0:keys],
                              (g0 + j) * Q_BLOCK)
                o_ref[rows, :] = jnp.dot(p.astype(BF16), vb_ref[0:keys, :], preferred_element_type=F32)
                return carry
            lax.fori_loop(0, per, block, 0)

    x_spec = pl.BlockSpec((t, HEAD_DIM), lambda h: (0, h))
    v_spec = pl.BlockSpec((t, HEAD_DIM), lambda h: (0, _V_BLOCK + h))
    fq_spec = pl.BlockSpec((None, t, 1), lambda h: (h, 0, 0))
    fk_spec = pl.BlockSpec((None, 1, t), lambda h: (h, 0, 0))
    return hosted_call(
        riders, body, name=name, grid=(N_HEADS,), in_specs=[x_spec, x_spec, v_spec, fq_spec, fk_spec], out_specs=x_spec,
        out_shape=jax.ShapeDtypeStruct((t, D_MODEL), F32), scratch_shapes=[pltpu.VMEM((t, HEAD_DIM), BF16)] * 2,
        compiler_params=_params(("parallel",)),
    )(q, k, proj, fq, fk)


def fox_attn_bwd(q, k, proj, fq, fk, do, *, name, riders=()):
    t = q.shape[0]

    def body(q_ref, k_ref, v_ref, fq_ref, fk_ref, do_ref, dq_ref, dk_ref, dv_ref, dfq_ref, dfk_ref, kb_ref, vb_ref):
        kb_ref[...] = k_ref[...].astype(BF16)
        vb_ref[...] = v_ref[...].astype(BF16)
        dk_ref[...] = jnp.zeros_like(dk_ref)
        dv_ref[...] = jnp.zeros_like(dv_ref)
        dfk_ref[...] = jnp.zeros_like(dfk_ref)
        nt = (((1,), (1,)), ((), ()))
        tn = (((0,), (0,)), ((), ()))
        for g0, per, keys in _fox_groups(t):
            def block(j, carry, g0=g0, keys=keys):
                rows = pl.ds(pl.multiple_of((g0 + j) * Q_BLOCK, Q_BLOCK), Q_BLOCK)
                qb, dob = q_ref[rows, :].astype(BF16), do_ref[rows, :].astype(BF16)
                kb, vb = kb_ref[0:keys, :], vb_ref[0:keys, :]
                p = fox_probs(qb, kb, fq_ref[rows, :], fk_ref[:, 0:keys], (g0 + j) * Q_BLOCK)
                dp = lax.dot_general(dob, vb, nt, preferred_element_type=F32)
                dz = p * (dp - jnp.sum(dp * p, axis=-1, keepdims=True))
                pb, dzb = p.astype(BF16), dz.astype(BF16)
                dq_ref[rows, :] = jnp.dot(dzb, kb, preferred_element_type=F32)
                dfq_ref[rows, :] = jnp.sum(dz, axis=-1, keepdims=True)
                dk_ref[0:keys, :] += lax.dot_general(dzb, qb, tn, preferred_element_type=F32)
                dv_ref[0:keys, :] += lax.dot_general(pb, dob, tn, preferred_element_type=F32)
                dfk_ref[:, 0:keys] -= jnp.sum(dz, axis=0, keepdims=True)
                return carry
            lax.fori_loop(0, per, block, 0)

    x_spec = pl.BlockSpec((t, HEAD_DIM), lambda h: (0, h))
    v_spec = pl.BlockSpec((t, HEAD_DIM), lambda h: (0, _V_BLOCK + h))
    fq_spec = pl.BlockSpec((None, t, 1), lambda h: (h, 0, 0))
    fk_spec = pl.BlockSpec((None, 1, t), lambda h: (h, 0, 0))
    return hosted_call(
        riders, body, name=name, grid=(N_HEADS,), in_specs=[x_spec, x_spec, v_spec, fq_spec, fk_spec, x_spec],
        out_specs=[x_spec, x_spec, x_spec, fq_spec, fk_spec],
        out_shape=[jax.ShapeDtypeStruct((t, D_MODEL), F32)] * 3
        + [jax.ShapeDtypeStruct((N_HEADS, t, 1), F32), jax.ShapeDtypeStruct((N_HEADS, 1, t), F32)],
        scratch_shapes=[pltpu.VMEM((t, HEAD_DIM), BF16)] * 2, compiler_params=_params(("parallel",)),
    )(q, k, proj, fq, fk, do)


def memkv_fwd(mem, mnw, wkv, mknw, *, name):
    n = mem.shape[0]

    def body(mem_ref, mnw_ref, w_ref, mknw_ref, mk_ref, mv_ref):
        mk, mv = memkv_fn(mem_ref[...], mnw_ref[...], w_ref[...], mknw_ref[...])
        mk_ref[...] = mk
        mv_ref[...] = mv

    return pl.pallas_call(
        body, name=name, out_shape=[jax.ShapeDtypeStruct((n, MEM_WIDTH), F32)] * 2,
        compiler_params=pltpu.CompilerParams(vmem_limit_bytes=VMEM_LIMIT),
    )(mem, mnw, wkv, mknw)


def memkv_bwd(mem, mnw, wkv, mknw, dmk, dmv, *, name):
    def body(mem_ref, mnw_ref, w_ref, mknw_ref, dmk_ref, dmv_ref, dmnw_ref, dw_ref, dmknw_ref):
        f = functools.partial(memkv_fn, mem_ref[...])
        _, vjp = jax.vjp(f, mnw_ref[...], w_ref[...].astype(F32), mknw_ref[...])
        dmnw, dw, dmknw = vjp((dmk_ref[...], dmv_ref[...]))
        dmnw_ref[...] = dmnw
        dw_ref[...] = dw.astype(dw_ref.dtype)
        dmknw_ref[...] = dmknw

    return pl.pallas_call(
        body, name=name,
        out_shape=[jax.ShapeDtypeStruct(mnw.shape, F32), jax.ShapeDtypeStruct(wkv.shape, BF16), jax.ShapeDtypeStruct(mknw.shape, F32)],
        compiler_params=pltpu.CompilerParams(vmem_limit_bytes=VMEM_LIMIT),
    )(mem, mnw, wkv, mknw, dmk, dmv)


def _row(v, width=None):
    v = v.reshape(1, -1)
    if width is not None and v.shape[1] < width:
        v = jnp.pad(v, ((0, 0), (0, width - v.shape[1])))
    return v


def _head_cols(a):
    return a[:, :N_HEADS].T[:, :, None]


def _lanes_from_heads(*cols):
    t = cols[0].shape[1]
    parts = [c[:, :, 0].T for c in cols]
    parts.append(jnp.zeros((t, LANES - N_HEADS * len(cols)), F32))
    return jnp.concatenate(parts, axis=1)


def _norm_fwd(x, w, name, riders=()):
    return rows_call(lambda x, w: rms(x, w), [x], [w], [(D_MODEL, BF16)], [], tm=512, name=name, riders=riders)[0]


def _norm_bwd(x, w, dh, dx_in, name, riders=()):
    def fn(x, dh, dx_in, w):
        _, vjp = jax.vjp(rms, x, w)
        dx, dw = vjp(dh)
        return dx + dx_in, dw
    return rows_call(fn, [x, dh, dx_in], [w], [(D_MODEL, F32)], [(1, D_MODEL)], tm=512, name=name, riders=riders)


FF_PIECE = D_FF // N_DEV


def _add(r, x):
    return r + x


def _piece(rows, cols, index):
    return pl.BlockSpec((None, rows, cols), lambda i, j, kk: (index(i, j, kk), 0, 0))


def _two_pieces(rows, cols, index):
    return pl.BlockSpec((2, rows, cols), lambda i, j, kk: (index(i, j, kk), 0, 0))


def _mlp_fwd(x, n2w, w1, w2, layer, riders=()):
    riders = list(riders) + [None, None]
    h2 = _norm_fwd(x, n2w, f"norm2_fwd_{layer}")
    u, a1 = matmul(h2, w1, name=f"mlp1_fwd_{layer}", tiles=(None, FF_PIECE, D_MODEL),
                   extra_out=(lambda u: jnp.square(jnp.maximum(u, 0.0)), BF16),
                   b_view=(D_MODEL, D_FF, _piece(D_MODEL, FF_PIECE, lambda i, j, kk: j)), riders=riders[0])
    y = matmul(a1, w2, name=f"mlp2_fwd_{layer}", post=_add, post_ins=[x], tiles=(None, D_MODEL, 2 * FF_PIECE),
               b_view=(D_FF, D_MODEL, _two_pieces(FF_PIECE, D_MODEL, lambda i, j, kk: kk)), riders=riders[1])
    return y, (x, h2, u, a1)


def pair_sum(g, got, *, name):
    _, rows, cols = g.shape
    tile = _pick(rows, (512, 256, 128))
    c = lax.axis_index("c").astype(jnp.int32).reshape(1)

    def body(c_ref, a_ref, b_ref, o_ref):
        o_ref[...] = (a_ref[...].astype(F32) + b_ref[...].astype(F32)).astype(o_ref.dtype)

    grid_spec = pltpu.PrefetchScalarGridSpec(
        num_scalar_prefetch=1, grid=(4, rows // tile),
        in_specs=[pl.BlockSpec((None, tile, cols), lambda k, i, c_ref: (2 * k + c_ref[0], i, 0)),
                  pl.BlockSpec((None, tile, cols), lambda k, i, c_ref: (k, i, 0))],
        out_specs=pl.BlockSpec((None, tile, cols), lambda k, i, c_ref: (k, i, 0)))
    return pl.pallas_call(
        body, name=name, grid_spec=grid_spec, out_shape=jax.ShapeDtypeStruct((4, rows, cols), g.dtype),
        compiler_params=_params(("parallel", "parallel")),
    )(c, g, got)


def chip_sums(names, pieces, gots):
    return [pair_sum(a, got, name=f"grads_pair_sum_{n}") for n, a, got in zip(names, pieces, gots)]


def _mlp_bwd(dy, res, n2w, w1, w2, layer, riders=()):
    x, h2, u, a1 = res
    du = matmul(dy, w2, tb=True, name=f"mlp2_dx_{layer}", out_dtype=BF16, tiles=(None, 2 * FF_PIECE, D_MODEL),
                post=lambda r, u: r * (2.0 * jnp.maximum(u, 0.0)), post_ins=[u],
                b_view=(D_MODEL, D_FF, _two_pieces(FF_PIECE, D_MODEL, lambda i, j, kk: j)), riders=riders)
    dw2 = matmul(a1, dy, ta=True, name=f"mlp2_dw_{layer}", out_dtype=BF16, tiles=(FF_PIECE, D_MODEL, None), out_view=(
        w2.shape, _piece(FF_PIECE, D_MODEL, lambda i, j, kk: i), None))
    sib2 = sibling_rider([dw2])
    dh2 = matmul(du, w1, tb=True, name=f"mlp1_dx_{layer}", tiles=(None, D_MODEL, FF_PIECE),
                 b_view=(D_FF, D_MODEL, _piece(D_MODEL, FF_PIECE, lambda i, j, kk: kk)), riders=[sib2])
    dw1 = matmul(h2, du, ta=True, name=f"mlp1_dw_{layer}", out_dtype=BF16, tiles=(D_MODEL, FF_PIECE, None), out_view=(
        w1.shape, _piece(D_MODEL, FF_PIECE, lambda i, j, kk: j), None))
    sib1 = sibling_rider([dw1])
    dx, dn2w = _norm_bwd(x, n2w, dh2, dy, f"norm2_bwd_{layer}", riders=[sib1])
    return dx, dw1, dw2, dn2w, sib1, sib2


def _in_proj_bwd(h, dmain, dsmall, w_main, w_small, tag):
    dh = matmul(dmain, w_main, tb=True, name=f"inproj_dx_main_{tag}")
    dh = matmul(dsmall, w_small, tb=True, post=_add, post_ins=[dh], name=f"inproj_dx_small_{tag}")
    dw_main = matmul(h, dmain, ta=True, out_dtype=BF16, name=f"inproj_dw_main_{tag}")
    dw_small = matmul(h, dsmall, ta=True, out_dtype=BF16, name=f"inproj_dw_small_{tag}")
    return dh, dw_main, dw_small


def local_step(x, mem, target, w, m, v):
    t = x.shape[0]
    n_mem = mem.shape[0]
    g = {}

    def wire(a):
        return a.astype(BF16)

    (dn_g,), = run_riders([gather_rider([wire(w["dn_w_in"][0])])], name="weights_gather_first")
    dn_main, dn_ab = in_proj_weights(dn_g, DN_IN, 2 * N_HEADS)
    fox_w = wire(w["fox_w_in"][0])
    ride_out = gather_rider([wire(w["w_out"][0]), wire(w["w_out"][1]), w["dn_conv_w"][0]])
    ride_kv = gather_rider([wire(w["w_mem_kv"])])
    ride_mlp1_0 = gather_rider([wire(w["w_mlp1"][0])])
    ride_mlp2_0 = gather_rider([wire(w["w_mlp2"][0])])
    ride_fox_a, ride_fox_b = gather_rider([fox_w[:D_MODEL // 2]]), gather_rider([fox_w[D_MODEL // 2:]])
    ride_mlp_1 = gather_rider([wire(w["w_mlp1"][1]), wire(w["w_mlp2"][1])])
    mnw, mknw = _row(w["mem_norm_w"]), _row(w["mem_k_norm_w"])

    n1w0, n2w0 = _row(w["norm1_w"][0]), _row(w["norm2_w"][0])
    alog, dtb = _row(w["dn_a_log"][0], LANES), _row(w["dn_dt_bias"][0], LANES)
    onw, mqw0 = _row(w["dn_o_norm_w"][0]), _row(w["memq_norm_w"][0])
    x0 = x
    h0 = _norm_fwd(x0, n1w0, "norm1_fwd_0")
    pm0 = matmul(h0, dn_main, name="inproj_main_0", riders=[ride_out])
    w_out0, w_out1 = (a.reshape(OUT_IN, D_MODEL) for a in ride_out.results[:2])
    conv_w = ride_out.results[2].transpose(1, 0, 2).reshape(CONV_WIDTH, 3 * D_MODEL)
    ps0 = matmul(h0, dn_ab, name="inproj_small_0")
    gates = rows_call(dn_gates_fn, [ps0], [alog, dtb], [(LANES, F32)], [], tm=512, name="dn_gates_fwd")[0]
    gc, beta = _head_cols(gates), _head_cols(gates[:, N_HEADS:])
    q0, k0, v0 = dn_prep_fwd(pm0, conv_w, name="dn_prep_fwd", riders=[ride_kv])
    w_kv = ride_kv.results[0].reshape(D_MODEL, D_MODEL)
    mk, mv = memkv_fwd(mem, mnw, w_kv, mknw, name="memkv_fwd")
    u0, w0, qk0 = delta_intra_fwd(q0, k0, v0, gc, beta, name="delta_intra_fwd", riders=[ride_mlp1_0])
    o0, s_start = delta_seq_fwd(q0, k0, gc, u0, w0, qk0, name="delta_seq_fwd", riders=[ride_mlp2_0])
    cat0 = rows_call(dn_out_fn, [o0, (pm0, D_MODEL, 3), (pm0, MEM_WIDTH, 8)], [onw, mqw0, mk, mv],
                     [(D_MODEL + MEM_WIDTH, BF16)], [], tm=256, name="dn_out_fwd")[0]
    (w1_0,), (w2_0,) = ride_mlp1_0.results, ride_mlp2_0.results
    x1 = matmul(cat0, w_out0, post=_add, post_ins=[x0], name="wout_fwd_0")
    x2, mlp_res0 = _mlp_fwd(x1, n2w0, w1_0, w2_0, 0, riders=[[ride_fox_a], [ride_fox_b]])
    fox_main, fox_f = in_proj_weights(
        jnp.concatenate([ride_fox_a.results[0], ride_fox_b.results[0]], axis=1), FOX_IN, N_HEADS)

    n1w1, n2w1 = _row(w["norm1_w"][1]), _row(w["norm2_w"][1])
    fbias = _row(w["fox_f_bias"][0], LANES)
    qnw, knw, mqw1 = _row(w["fox_q_norm_w"][0]), _row(w["fox_k_norm_w"][0]), _row(w["memq_norm_w"][1])
    h1 = _norm_fwd(x2, n1w1, "norm1_fwd_1")
    pm1 = matmul(h1, fox_main, name="inproj_main_1")
    ps1 = matmul(h1, fox_f, name="inproj_small_1")
    fcum = rows_call(fox_fcum_fn, [ps1], [fbias], [(LANES, F32)], [], tm=t, name="fox_fcum_fwd")[0]
    fq = _head_cols(fcum)
    fk = jnp.swapaxes(fq, 1, 2)
    q1, k1 = rows_call(fox_qk_fn, [(pm1, D_MODEL, 0), (pm1, D_MODEL, 1)], [qnw, knw], [(D_MODEL, F32)] * 2, [], tm=256,
                       name="fox_qk_fwd")
    o1 = fox_attn_fwd(q1, k1, pm1, fq, fk, name="fox_attn_fwd", riders=[ride_mlp_1])
    cat1 = rows_call(fox_out_fn, [o1, (pm1, D_MODEL, 3), (pm1, MEM_WIDTH, 8)], [mqw1, mk, mv],
                     [(D_MODEL + MEM_WIDTH, BF16)], [], tm=256, name="fox_out_fwd")[0]
    w1_1, w2_1 = ride_mlp_1.results
    x3 = matmul(cat1, w_out1, post=_add, post_ins=[x2], name="wout_fwd_1")
    y, mlp_res1 = _mlp_fwd(x3, n2w1, w1_1, w2_1, 1)

    def loss_fn(y, tgt):
        e = y - tgt
        return e * (1.0 / D_MODEL), jnp.sum(jnp.sum(e * e, axis=1, keepdims=True), axis=0, keepdims=True)
    dy, sq = rows_call(loss_fn, [y, target], [], [(D_MODEL, F32)], [(1, 1)], tm=512, name="loss")
    loss = sq[0, 0] * (0.5 / D_MODEL)

    dx3, dw1_1, dw2_1, dn2w1, sib1, sib2 = _mlp_bwd(dy, mlp_res1, n2w1, w1_1, w2_1, 1)
    dcat1 = matmul(dx3, w_out1, tb=True, name="wout_dx_1")
    dwo_1 = matmul(cat1, dx3, ta=True, out_dtype=BF16, name="wout_dw_1").reshape(N_DEV, OUT_IN // N_DEV, D_MODEL)
    sibo = sibling_rider([dwo_1])
    do1, dgate1, dqm1, dmqw1, dmk1, dmv1 = rows_call(
        functools.partial(vjp_rows(fox_out_fn, 3, (True, True, True)), n_row=3, n_ct=1),
        [o1, (pm1, D_MODEL, 3), (pm1, MEM_WIDTH, 8), dcat1], [mqw1, mk, mv],
        [(D_MODEL, F32), (D_MODEL, F32), (MEM_WIDTH, F32)], [(1, HEAD_DIM), (n_mem, MEM_WIDTH), (n_mem, MEM_WIDTH)],
        tm=256, name="fox_out_bwd", riders=[sibo])
    ride_l1 = chips_rider(chip_sums(["w_mlp2_1", "w_mlp1_1", "w_out_1"], [dw2_1, dw1_1, dwo_1],
                                    sib2.results + sib1.results + sibo.results))
    dq1, dk1, dv1, dfq, dfk = fox_attn_bwd(q1, k1, pm1, fq, fk, do1, name="fox_attn_bwd", riders=[ride_l1])
    dqraw1, dkraw1, dqnw, dknw = rows_call(
        functools.partial(vjp_rows(fox_qk_fn, 2, (True, True)), n_row=2, n_ct=2),
        [(pm1, D_MODEL, 0), (pm1, D_MODEL, 1), dq1, dk1], [qnw, knw],
        [(D_MODEL, F32)] * 2, [(1, HEAD_DIM)] * 2, tm=256, name="fox_qk_bwd")
    dfcum = _lanes_from_heads(dfq + jnp.swapaxes(dfk, 1, 2))
    dps1, dfbias = rows_call(
        functools.partial(vjp_rows(fox_fcum_fn, 1, (True,)), n_row=1, n_ct=1),
        [ps1, dfcum], [fbias], [(LANES, F32)], [(1, LANES)], tm=t, name="fox_fcum_bwd")
    dpm1 = jnp.concatenate([dqraw1, dkraw1, dv1, dgate1, dqm1], axis=1)
    dh1, dwmain1, dwsmall1 = _in_proj_bwd(h1, dpm1, dps1, fox_main, fox_f, "1")
    g_fox = in_proj_pieces(dwmain1, dwsmall1, N_HEADS, FOX_IN)
    sibf = sibling_rider([g_fox])
    dx2, dn1w1 = _norm_bwd(x2, n1w1, dh1, dx3, "norm1_bwd_1", riders=[sibf])
    ride_fox_g = chips_rider(chip_sums(["fox_w_in"], [g_fox], sibf.results))

    dx1, dw1_0, dw2_0, dn2w0, sib1, sib2 = _mlp_bwd(dx2, mlp_res0, n2w0, w1_0, w2_0, 0)
    dcat0 = matmul(dx1, w_out0, tb=True, name="wout_dx_0")
    dwo_0 = matmul(cat0, dx1, ta=True, out_dtype=BF16, name="wout_dw_0").reshape(N_DEV, OUT_IN // N_DEV, D_MODEL)
    sibo = sibling_rider([dwo_0])
    do0, dz0, dqm0, donw, dmqw0, dmk0, dmv0 = rows_call(
        functools.partial(vjp_rows(dn_out_fn, 3, (True, True, True, True)), n_row=3, n_ct=1),
        [o0, (pm0, D_MODEL, 3), (pm0, MEM_WIDTH, 8), dcat0], [onw, mqw0, mk, mv],
        [((N_HEADS, HEAD_DIM), F32), (D_MODEL, F32), (MEM_WIDTH, F32)],
        [(1, HEAD_DIM), (1, HEAD_DIM), (n_mem, MEM_WIDTH), (n_mem, MEM_WIDTH)], tm=256, name="dn_out_bwd", riders=[sibo])
    h_l0 = chip_sums(["w_mlp2_0", "w_mlp1_0", "w_out_0"], [dw2_0, dw1_0, dwo_0], sib2.results + sib1.results + sibo.results)
    ride_l0_mlp, ride_l0_out = chips_rider(h_l0[:2]), chips_rider(h_l0[2:])
    dq_s, dk_s, dg_s, du0, dw0, dqk0 = delta_seq_bwd(q0, k0, gc, u0, w0, qk0, s_start, do0, name="delta_seq_bwd",
                                                     riders=[ride_fox_g])
    dq0, dk0, dv0, dgc, dbeta = delta_intra_bwd(q0, k0, v0, gc, beta, du0, dw0, dqk0, dq_s, dk_s, dg_s,
                                                name="delta_intra_bwd", riders=[ride_l0_mlp])
    dxq, dxk, dxv, dcq, dck, dcv = dn_prep_bwd(pm0, conv_w, dq0, dk0, dv0, name="dn_prep_bwd", riders=[ride_l0_out])
    dconv = jnp.concatenate([dcq, dck, dcv], axis=1)
    dps0, dalog, ddtb = rows_call(
        functools.partial(vjp_rows(dn_gates_fn, 1, (True, True)), n_row=1, n_ct=1),
        [ps0, _lanes_from_heads(dgc, dbeta)], [alog, dtb], [(LANES, F32)], [(1, LANES)] * 2, tm=512, name="dn_gates_bwd")
    dpm0 = jnp.concatenate([dxq, dxk, dxv, dz0, dqm0], axis=1)
    dh0, dwmain0, dwsmall0 = _in_proj_bwd(h0, dpm0, dps0, dn_main, dn_ab, "0")
    g_dn = in_proj_pieces(dwmain0, dwsmall0, 2 * N_HEADS, DN_IN)
    g_conv = dconv.reshape(CONV_WIDTH, N_DEV, -1).transpose(1, 0, 2).astype(BF16)
    sibd = sibling_rider([g_dn, g_conv])
    grad_x, dn1w0 = _norm_bwd(x0, n1w0, dh0, dx1, "norm1_bwd_0", riders=[sibd])

    dmnw, dwkv, dmknw = memkv_bwd(mem, mnw, w_kv, mknw, dmk0 + dmk1, dmv0 + dmv1, name="memkv_bwd")
    g_kv = dwkv.reshape(N_DEV, D_MODEL // N_DEV, D_MODEL)

    g["mem_norm_w"] = dmnw[0]
    g["mem_k_norm_w"] = dmknw[0]
    g["norm1_w"] = jnp.concatenate([dn1w0, dn1w1], axis=0)
    g["dn_a_log"] = dalog[:, :N_HEADS]
    g["dn_dt_bias"] = ddtb[:, :N_HEADS]
    g["dn_o_norm_w"] = donw
    g["fox_f_bias"] = dfbias[:, :N_HEADS]
    g["fox_q_norm_w"] = dqnw
    g["fox_k_norm_w"] = dknw
    g["memq_norm_w"] = jnp.concatenate([dmqw0, dmqw1], axis=0)
    g["norm2_w"] = jnp.concatenate([dn2w0, dn2w1], axis=0)

    def layers(l0, l1):
        return jnp.stack([l0, l1], axis=1).reshape(4, -1, l0.shape[-1])

    shards = {n: [d[n].reshape(r, c) for d in (w, m, v)] for n, r, c in BIG}
    ride_dn = chips_rider(chip_sums(["dn_w_in", "dn_conv_w"], [g_dn, g_conv], sibd.results))
    sibk = sibling_rider([g_kv])
    ride_small = gather_rider([pack_small(g)])
    out = {}
    out["w_mlp1"] = adamw(layers(ride_l0_mlp.results[1], ride_l1.results[1]), *shards["w_mlp1"], name="adamw_w_mlp1",
                          riders=[ride_dn, sibk, ride_small])
    ride_kv_g = chips_rider(chip_sums(["w_mem_kv"], [g_kv], sibk.results))
    out["w_mlp2"] = adamw(layers(ride_l0_mlp.results[0], ride_l1.results[0]), *shards["w_mlp2"], name="adamw_w_mlp2",
                          riders=[ride_kv_g])
    out["w_out"] = adamw(layers(ride_l0_out.results[0], ride_l1.results[2]), *shards["w_out"], name="adamw_w_out")
    out["fox_w_in"] = adamw(ride_fox_g.results[0], *shards["fox_w_in"], name="adamw_fox_w_in")
    out["dn_w_in"] = adamw(ride_dn.results[0], *shards["dn_w_in"], name="adamw_dn_w_in")
    out["dn_conv_w"] = adamw(ride_dn.results[1], *shards["dn_conv_w"], name="adamw_dn_conv_w")
    out["w_mem_kv"] = adamw(ride_kv_g.results[0], *shards["w_mem_kv"], name="adamw_w_mem_kv")
    small = adamw(ride_small.results[0], pack_small(w), pack_small(m), pack_small(v), name="adamw_small")
    return loss, grad_x, {n: [o.reshape(w[n].shape) for o in outs] for n, outs in out.items()}, small


WEIGHTS = ["mem_norm_w", "w_mem_kv", "mem_k_norm_w", "norm1_w", "dn_w_in", "dn_conv_w", "dn_a_log", "dn_dt_bias",
           "dn_o_norm_w", "fox_w_in", "fox_f_bias", "fox_q_norm_w", "fox_k_norm_w", "memq_norm_w", "w_out", "norm2_w",
           "w_mlp1", "w_mlp2"]
DN_IN = 4 * D_MODEL + 2 * N_HEADS + MEM_WIDTH
FOX_IN = 4 * D_MODEL + N_HEADS + MEM_WIDTH
GATE_END = 4 * D_MODEL
OUT_IN = D_MODEL + MEM_WIDTH
BIG = [("w_mem_kv", D_MODEL // N_DEV, D_MODEL), ("dn_w_in", D_MODEL, DN_IN // N_DEV), ("fox_w_in", D_MODEL, FOX_IN // N_DEV),
       ("dn_conv_w", CONV_WIDTH, 3 * D_MODEL // N_DEV), ("w_out", 2 * OUT_IN // N_DEV, D_MODEL),
       ("w_mlp1", 2 * D_MODEL, FF_PIECE), ("w_mlp2", 2 * FF_PIECE, D_MODEL)]
SMALL = [("mem_norm_w", (D_MODEL,), D_MODEL), ("mem_k_norm_w", (HEAD_DIM,), HEAD_DIM), ("norm1_w", (2, D_MODEL), 2 * D_MODEL),
         ("dn_a_log", (1, N_HEADS), LANES), ("dn_dt_bias", (1, N_HEADS), LANES), ("dn_o_norm_w", (1, HEAD_DIM), HEAD_DIM),
         ("fox_f_bias", (1, N_HEADS), LANES), ("fox_q_norm_w", (1, HEAD_DIM), HEAD_DIM), ("fox_k_norm_w", (1, HEAD_DIM), HEAD_DIM),
         ("memq_norm_w", (2, HEAD_DIM), 2 * HEAD_DIM), ("norm2_w", (2, D_MODEL), 2 * D_MODEL)]
SMALL_ROWS = 16


def pack_small(p):
    flat = jnp.concatenate([jnp.pad(p[n].reshape(-1), (0, ln - math.prod(sh))) for n, sh, ln in SMALL])
    return jnp.pad(flat, (0, SMALL_ROWS * PACK_W - flat.shape[0])).reshape(SMALL_ROWS, PACK_W)


def unpack_small(pk):
    flat, off, out = pk.reshape(-1), 0, {}
    for n, sh, ln in SMALL:
        out[n] = flat[off:off + math.prod(sh)].reshape(sh)
        off += ln
    return out


def in_proj_weights(gathered, width, n_small):
    full = gathered.transpose(1, 0, 2).reshape(D_MODEL, width)
    main = jnp.concatenate([full[:, :GATE_END], full[:, GATE_END + n_small:]], axis=1)
    return main, jnp.pad(full[:, GATE_END:GATE_END + n_small], ((0, 0), (0, LANES - n_small)))


def in_proj_pieces(d_main, d_small, n_small, width):
    full = jnp.concatenate([d_main[:, :GATE_END], d_small[:, :n_small], d_main[:, GATE_END:]], axis=1)
    return full.reshape(D_MODEL, N_DEV, width // N_DEV).transpose(1, 0, 2)


def adamw(parts, w, m, v, *, name, riders=()):
    n, rows, cols = parts.shape
    tile = _pick(rows, (512, 256, 128))

    def body(p_ref, w_ref, m_ref, v_ref, g_ref, d_ref, mo_ref, vo_ref):
        g = p_ref[0].astype(F32)
        for i in range(1, n):
            g = g + p_ref[i].astype(F32)
        m_new = ADAM_B1 * m_ref[...] + (1.0 - ADAM_B1) * g
        v_new = ADAM_B2 * v_ref[...] + (1.0 - ADAM_B2) * jnp.square(g)
        m_hat = m_new / (1.0 - ADAM_B1 ** ADAM_STEP)
        v_hat = v_new / (1.0 - ADAM_B2 ** ADAM_STEP)
        g_ref[...] = g
        d_ref[...] = -ADAM_LR * (m_hat / (jnp.sqrt(v_hat) + ADAM_EPS) + ADAM_WD * w_ref[...])
        mo_ref[...] = m_new
        vo_ref[...] = v_new

    spec = pl.BlockSpec((tile, cols), lambda i: (i, 0))
    return hosted_call(
        riders, body, name=name, grid=(rows // tile,),
        in_specs=[pl.BlockSpec((n, tile, cols), lambda i: (0, i, 0)), spec, spec, spec], out_specs=[spec] * 4,
        out_shape=[jax.ShapeDtypeStruct((rows, cols), F32)] * 4, compiler_params=_params(("parallel",)),
    )(parts, w, m, v)


def kernel(x, mem, mem_norm_w, w_mem_kv, mem_k_norm_w, norm1_w, dn_w_in, dn_conv_w, dn_a_log, dn_dt_bias, dn_o_norm_w, fox_w_in, fox_f_bias, fox_q_norm_w, fox_k_norm_w, memq_norm_w, w_out, norm2_w, w_mlp1, w_mlp2, loss_target, m_mem_norm_w, m_w_mem_kv, m_mem_k_norm_w, m_norm1_w, m_dn_w_in, m_dn_conv_w, m_dn_a_log, m_dn_dt_bias, m_dn_o_norm_w, m_fox_w_in, m_fox_f_bias, m_fox_q_norm_w, m_fox_k_norm_w, m_memq_norm_w, m_w_out, m_norm2_w, m_w_mlp1, m_w_mlp2, v_mem_norm_w, v_w_mem_kv, v_mem_k_norm_w, v_norm1_w, v_dn_w_in, v_dn_conv_w, v_dn_a_log, v_dn_dt_bias, v_dn_o_norm_w, v_fox_w_in, v_fox_f_bias, v_fox_q_norm_w, v_fox_k_norm_w, v_memq_norm_w, v_w_out, v_norm2_w, v_w_mlp1, v_w_mlp2):
    p = dict(mem_norm_w=mem_norm_w, w_mem_kv=w_mem_kv, mem_k_norm_w=mem_k_norm_w, norm1_w=norm1_w, dn_w_in=dn_w_in,
             dn_conv_w=dn_conv_w, dn_a_log=dn_a_log, dn_dt_bias=dn_dt_bias, dn_o_norm_w=dn_o_norm_w, fox_w_in=fox_w_in,
             fox_f_bias=fox_f_bias, fox_q_norm_w=fox_q_norm_w, fox_k_norm_w=fox_k_norm_w, memq_norm_w=memq_norm_w,
             w_out=w_out, norm2_w=norm2_w, w_mlp1=w_mlp1, w_mlp2=w_mlp2)
    pm = dict(mem_norm_w=m_mem_norm_w, w_mem_kv=m_w_mem_kv, mem_k_norm_w=m_mem_k_norm_w, norm1_w=m_norm1_w,
              dn_w_in=m_dn_w_in, dn_conv_w=m_dn_conv_w, dn_a_log=m_dn_a_log, dn_dt_bias=m_dn_dt_bias,
              dn_o_norm_w=m_dn_o_norm_w, fox_w_in=m_fox_w_in, fox_f_bias=m_fox_f_bias, fox_q_norm_w=m_fox_q_norm_w,
              fox_k_norm_w=m_fox_k_norm_w, memq_norm_w=m_memq_norm_w, w_out=m_w_out, norm2_w=m_norm2_w, w_mlp1=m_w_mlp1,
              w_mlp2=m_w_mlp2)
    pv = dict(mem_norm_w=v_mem_norm_w, w_mem_kv=v_w_mem_kv, mem_k_norm_w=v_mem_k_norm_w, norm1_w=v_norm1_w,
              dn_w_in=v_dn_w_in, dn_conv_w=v_dn_conv_w, dn_a_log=v_dn_a_log, dn_dt_bias=v_dn_dt_bias,
              dn_o_norm_w=v_dn_o_norm_w, fox_w_in=v_fox_w_in, fox_f_bias=v_fox_f_bias, fox_q_norm_w=v_fox_q_norm_w,
              fox_k_norm_w=v_fox_k_norm_w, memq_norm_w=v_memq_norm_w, w_out=v_w_out, norm2_w=v_norm2_w, w_mlp1=v_w_mlp1,
              w_mlp2=v_w_mlp2)

    loss, grad_x, results, small = local_step(x[0], mem[0], loss_target[0], p, pm, pv)
    loss = lax.psum(loss, ("x", "y", "c"))
    small = [unpack_small(o) for o in small]
    groups = [{**small[i], **{n: r[i] for n, r in results.items()}} for i in range(4)]
    return (loss, grad_x[None], *[grp[n] for grp in groups for n in WEIGHTS])
```

```python
import functools
import math

import jax
import jax.numpy as jnp
from jax import lax
from jax.experimental import pallas as pl
from jax.experimental.pallas import tpu as pltpu

F32 = jnp.float32
BF16 = jnp.bfloat16
HIGHEST = lax.Precision.HIGHEST

D_MODEL = 1024
HEAD_DIM = 128
N_HEADS = 8
MEM_HEADS = 4
MEM_WIDTH = MEM_HEADS * HEAD_DIM
D_FF = 4 * D_MODEL
CONV_WIDTH = 4
CHUNK = 64
Q_BLOCK = 128
EPS = 1e-6
SCALE = HEAD_DIM ** -0.5
MAIN_WIDTH = 4 * D_MODEL + MEM_WIDTH
LANES = 128
N_DEV = 8
PACK_W = 512

ADAM_LR = 0.001
ADAM_B1 = 0.9
ADAM_B2 = 0.999
ADAM_EPS = 1e-08
ADAM_WD = 0.01
ADAM_STEP = 10

VMEM_LIMIT = 56 * 2 ** 20
MESH = pl.DeviceIdType.MESH


def _bdot(a, b, dims):
    return lax.dot_general(a.astype(BF16), b.astype(BF16), (dims, ((), ())), preferred_element_type=F32)


@jax.custom_vjp
def mm(a, b):
    return _bdot(a, b, ((1,), (0,)))


@jax.custom_vjp
def mm_nt(a, b):
    return _bdot(a, b, ((1,), (1,)))


@jax.custom_vjp
def mm_tn(a, b):
    return _bdot(a, b, ((0,), (0,)))


mm.defvjp(lambda a, b: (mm(a, b), (a, b)), lambda r, g: (mm_nt(g, r[1]), mm_tn(r[0], g)))
mm_nt.defvjp(lambda a, b: (mm_nt(a, b), (a, b)), lambda r, g: (mm(g, r[1]), mm_tn(g, r[0])))
mm_tn.defvjp(lambda a, b: (mm_tn(a, b), (a, b)), lambda r, g: (mm_nt(r[1], g), mm(r[0], g)))


def hdot(a, b):
    return jnp.dot(a, b, precision=HIGHEST, preferred_element_type=F32)


def rms(x, w):
    return x * lax.rsqrt(jnp.mean(x * x, axis=-1, keepdims=True) + EPS) * w


def l2n(x):
    return x * lax.rsqrt(jnp.sum(x * x, axis=-1, keepdims=True) + EPS)


def _iota2(n, m):
    return lax.broadcasted_iota(jnp.int32, (n, m), 0), lax.broadcasted_iota(jnp.int32, (n, m), 1)


def _lower_ones(n):
    r, c = _iota2(n, n)
    return jnp.where(r >= c, 1.0, 0.0).astype(F32)


def _last_row(x):
    r = lax.broadcasted_iota(jnp.int32, x.shape, 0)
    return jnp.sum(jnp.where(r == x.shape[0] - 1, x, 0.0), axis=0, keepdims=True)


def _softmax_rows(z):
    m = lax.stop_gradient(jnp.max(z, axis=-1, keepdims=True))
    e = jnp.exp(z - m)
    return e * (1.0 / jnp.sum(e, axis=-1, keepdims=True))


_BNN = (((2,), (1,)), ((0,), (0,)))
_BNT = (((2,), (2,)), ((0,), (0,)))
_BTN = (((1,), (1,)), ((0,), (0,)))


def _bbdot(a, b, dims):
    return lax.dot_general(a.astype(BF16), b.astype(BF16), dims, preferred_element_type=F32)


@jax.custom_vjp
def bmm(a, b):
    return _bbdot(a, b, _BNN)


@jax.custom_vjp
def bmm_nt(a, b):
    return _bbdot(a, b, _BNT)


@jax.custom_vjp
def bmm_tn(a, b):
    return _bbdot(a, b, _BTN)


@jax.custom_vjp
def bmm_high(a, b):
    return lax.dot_general(a, b, _BNN, precision=lax.Precision.HIGH, preferred_element_type=F32)


bmm.defvjp(lambda a, b: (bmm(a, b), (a, b)), lambda r, g: (bmm_nt(g, r[1]), bmm_tn(r[0], g)))
bmm_nt.defvjp(lambda a, b: (bmm_nt(a, b), (a, b)), lambda r, g: (bmm(g, r[1]), bmm_tn(g, r[0])))
bmm_tn.defvjp(lambda a, b: (bmm_tn(a, b), (a, b)), lambda r, g: (bmm_nt(r[1], g), bmm(r[0], g)))
bmm_high.defvjp(lambda a, b: (bmm_high(a, b), (a, b)), lambda r, g: (bmm_nt(g, r[1]), bmm_tn(r[0], g)))

NEUMANN_HIGH_LEVELS = 2


def inv_unit_lower(a):
    n = a.shape[-1]
    r, c = _iota2(n, n)
    p = jnp.where(r == c, 1.0, 0.0).astype(F32) - a
    ak = a
    for level in range(int(math.log2(n)) - 1):
        dot = bmm_high if level < NEUMANN_HIGH_LEVELS else bmm
        ak = dot(ak, ak)
        p = p + dot(p, ak)
    return p


def delta_intra(q, k, v, gc, beta):
    b, c, _ = q.shape
    r, cc = _iota2(c, c)
    causal = r >= cc
    strict = r > cc
    gi = jnp.broadcast_to(gc, (b, c, c))
    gj = jnp.swapaxes(gi, 1, 2)
    decay = jnp.where(causal, jnp.exp(jnp.where(causal, gi - gj, 0.0)), 0.0)
    kb = k * beta
    a = jnp.where(strict, bmm_nt(kb, k) * decay, 0.0)
    t = inv_unit_lower(a)
    u = bmm(t, v * beta)
    w = bmm(t, kb * jnp.exp(gc))
    qk = jnp.where(causal, bmm_nt(q, k) * decay, 0.0)
    return u, w, qk


def delta_step(s, q, k, gc, u, w, qk):
    v_new = u - bmm(w, s)
    out = bmm(q * jnp.exp(gc), s) + bmm(qk, v_new)
    r = lax.broadcasted_iota(jnp.int32, gc.shape, 1)
    g_last = jnp.sum(jnp.where(r == gc.shape[1] - 1, gc, 0.0), axis=1, keepdims=True)
    k_dec = k * jnp.exp(g_last - gc)
    s_new = s * jnp.exp(g_last) + bmm_tn(k_dec, v_new)
    return out, s_new


def fox_probs(q, k, fq, fk, qpos0):
    s = lax.dot_general(q, k, (((1,), (1,)), ((), ())), preferred_element_type=F32)
    r, c = _iota2(s.shape[0], s.shape[1])
    return _softmax_rows(jnp.where(c <= (r + qpos0), s + (fq - fk), -jnp.inf))


def mem_head(qm, wq, mk, mv):
    p = _softmax_rows(mm_nt(rms(qm, wq) * SCALE, mk))
    return mm(p, mv)


def _heads(x, n):
    return [x[:, h * HEAD_DIM:(h + 1) * HEAD_DIM] for h in range(n)]


def memkv_fn(mem, mnw, wkv, mknw):
    kv = mm(rms(mem, mnw), wkv)
    mk = jnp.concatenate([rms(kh, mknw) for kh in _heads(kv[:, :MEM_WIDTH], MEM_HEADS)], axis=1)
    return mk, kv[:, MEM_WIDTH:]


def dn_gates_fn(ab, alog, dtb):
    g = -jnp.exp(alog) * jax.nn.softplus(ab + dtb)
    low = _lower_ones(CHUNK)
    gc = jnp.concatenate([hdot(low, g[i * CHUNK:(i + 1) * CHUNK]) for i in range(ab.shape[0] // CHUNK)], axis=0)
    lane = lax.broadcasted_iota(jnp.int32, ab.shape, 1)
    return jnp.where(lane < N_HEADS, gc, jax.nn.sigmoid(ab))


def fox_fcum_fn(fp, fbias):
    lf = jax.nn.log_sigmoid(fp + fbias)
    low = _lower_ones(LANES)
    carry = jnp.zeros((1, fp.shape[1]), F32)
    outs = []
    for i in range(fp.shape[0] // LANES):
        cs = hdot(low, lf[i * LANES:(i + 1) * LANES]) + carry
        carry = _last_row(cs)
        outs.append(cs)
    return jnp.concatenate(outs, axis=0)


def fox_qk_fn(qraw, kraw, qnw, knw):
    q = jnp.concatenate([rms(x, qnw) * SCALE for x in _heads(qraw, N_HEADS)], axis=1)
    k = jnp.concatenate([rms(x, knw) for x in _heads(kraw, N_HEADS)], axis=1)
    return q, k


def _mem_out(qm, mqw, mk, mv):
    return [mem_head(a, mqw, b, c) for a, b, c in zip(_heads(qm, MEM_HEADS), _heads(mk, MEM_HEADS), _heads(mv, MEM_HEADS))]


def dn_out_fn(o, z, qm, onw, mqw, mk, mv):
    mix = [rms(a, onw) * jax.nn.silu(b) for a, b in zip(o, _heads(z, N_HEADS))]
    return jnp.concatenate(mix + _mem_out(qm, mqw, mk, mv), axis=1)


def fox_out_fn(o, gate, qm, mqw, mk, mv):
    return jnp.concatenate([o * jax.nn.sigmoid(gate)] + _mem_out(qm, mqw, mk, mv), axis=1)


_HBM = pl.BlockSpec(memory_space=pltpu.HBM)


def _place():
    return lax.axis_index("x"), lax.axis_index("y"), lax.axis_index("c")


class Rider:
    def __init__(self, ins, out_shape, scratch, start, finish):
        self.ins, self.out_shape, self.scratch, self.start, self.finish = list(ins), list(out_shape), list(scratch), start, finish
        self.results = None


def gather_rider(xs):
    n = len(xs)

    def plan(x_refs, out_refs, sems):
        send_sems, recv_sems, local_sems = sems
        x, y, c = _place()
        me, sibling = (x, y, c), (x, y, 1 - c)
        chips = [(1 - x, y), (x, 1 - y), (1 - x, 1 - y)]

        def copy(a, k, block, to, src=None):
            px, py, pc = block
            dst = out_refs[a].at[4 * px + 2 * py + pc]
            return pltpu.make_async_remote_copy(
                src_ref=dst if src is None else src, dst_ref=dst,
                send_sem=send_sems.at[a, k], recv_sem=recv_sems.at[a, k], device_id=to, device_id_type=MESH)

        mine = [pltpu.make_async_copy(x_refs[a], out_refs[a].at[4 * x + 2 * y + c], local_sems.at[a]) for a in range(n)]
        first = [copy(a, 0, me, sibling, src=x_refs[a]) for a in range(n)]
        first += [copy(a, 1 + j, me, (*chip, c), src=x_refs[a]) for j, chip in enumerate(chips) for a in range(n)]
        return copy, me, sibling, chips, mine, first

    def start(x_refs, out_refs, sems):
        _, _, _, _, mine, first = plan(x_refs, out_refs, sems)
        for cp in mine + first:
            cp.start()

    def finish(x_refs, out_refs, sems):
        copy, me, sibling, chips, mine, first = plan(x_refs, out_refs, sems)
        _, _, c = me
        passed = []
        for j, chip in enumerate(chips):
            for a in range(n):
                copy(a, 1 + j, (*chip, c), me).wait_recv()
                passed.append(copy(a, 4 + j, (*chip, c), sibling))
                passed[-1].start()
        for a in range(n):
            copy(a, 0, sibling, me).wait_recv()
        for j, chip in enumerate(chips):
            for a in range(n):
                copy(a, 4 + j, (*chip, 1 - c), me).wait_recv()
        for cp in first + passed:
            cp.wait_send()
        for cp in mine:
            cp.wait()

    return Rider(xs, [jax.ShapeDtypeStruct((N_DEV,) + a.shape, a.dtype) for a in xs],
                 [pltpu.SemaphoreType.DMA((n, 7)), pltpu.SemaphoreType.DMA((n, 7)), pltpu.SemaphoreType.DMA((n,))], start, finish)


def sibling_rider(gs):
    n = len(gs)

    def plan(g_refs, out_refs, sems):
        send_sems, recv_sems = sems
        x, y, c = _place()
        return [pltpu.make_async_remote_copy(
            src_ref=g_refs[a].at[2 * k + 1 - c], dst_ref=out_refs[a].at[k], send_sem=send_sems.at[a, k],
            recv_sem=recv_sems.at[a, k], device_id=(x, y, 1 - c), device_id_type=MESH) for a in range(n) for k in range(4)]

    def start(g_refs, out_refs, sems):
        for cp in plan(g_refs, out_refs, sems):
            cp.start()

    def finish(g_refs, out_refs, sems):
        copies = plan(g_refs, out_refs, sems)
        for cp in copies:
            cp.wait_recv()
        for cp in copies:
            cp.wait_send()

    return Rider(gs, [jax.ShapeDtypeStruct((4,) + g.shape[1:], g.dtype) for g in gs],
                 [pltpu.SemaphoreType.DMA((n, 4)), pltpu.SemaphoreType.DMA((n, 4))], start, finish)


def chips_rider(hs):
    n = len(hs)

    def plan(h_refs, out_refs, sems):
        send_sems, recv_sems, local_sems = sems
        x, y, c = _place()
        mine = 2 * x + y
        chips = [(1 - x, y), (x, 1 - y), (1 - x, 1 - y)]
        keep = [pltpu.make_async_copy(h_refs[a].at[mine], out_refs[a].at[mine], local_sems.at[a]) for a in range(n)]
        sends = [pltpu.make_async_remote_copy(
            src_ref=h_refs[a].at[2 * qx + qy], dst_ref=out_refs[a].at[mine], send_sem=send_sems.at[a, j],
            recv_sem=recv_sems.at[a, j], device_id=(qx, qy, c), device_id_type=MESH)
            for j, (qx, qy) in enumerate(chips) for a in range(n)]
        recvs = [pltpu.make_async_remote_copy(
            src_ref=h_refs[a].at[mine], dst_ref=out_refs[a].at[2 * qx + qy], send_sem=send_sems.at[a, j],
            recv_sem=recv_sems.at[a, j], device_id=(qx, qy, c), device_id_type=MESH)
            for j, (qx, qy) in enumerate(chips) for a in range(n)]
        return keep, sends, recvs

    def start(h_refs, out_refs, sems):
        keep, sends, _ = plan(h_refs, out_refs, sems)
        for cp in keep + sends:
            cp.start()

    def finish(h_refs, out_refs, sems):
        keep, sends, recvs = plan(h_refs, out_refs, sems)
        for cp in recvs:
            cp.wait_recv()
        for cp in sends:
            cp.wait_send()
        for cp in keep:
            cp.wait()

    return Rider(hs, [jax.ShapeDtypeStruct(h.shape, h.dtype) for h in hs],
                 [pltpu.SemaphoreType.DMA((n, 3)), pltpu.SemaphoreType.DMA((n, 3)), pltpu.SemaphoreType.DMA((n,))], start, finish)


def hosted_call(riders, body, *, out_shape, in_specs, out_specs, grid=(), scratch_shapes=(), **kw):
    riders = tuple(riders or ())
    if not riders:
        return pl.pallas_call(body, out_shape=out_shape, in_specs=in_specs, out_specs=out_specs, grid=grid,
                              scratch_shapes=scratch_shapes, **kw)
    single = not isinstance(out_shape, (list, tuple))
    k_out_shape = [out_shape] if single else list(out_shape)
    k_out_specs = [out_specs] if single else list(out_specs)
    n_in, n_out, n_scr = len(in_specs), len(k_out_shape), len(scratch_shapes)
    r_ins = [a for r in riders for a in r.ins]
    r_outs = [s for r in riders for s in r.out_shape]
    r_scr = [s for r in riders for s in r.scratch]

    def full_body(*refs):
        ins = refs[:n_in + len(r_ins)]
        outs = refs[n_in + len(r_ins):n_in + len(r_ins) + n_out + len(r_outs)]
        scr = refs[n_in + len(r_ins) + n_out + len(r_outs):]
        ids = [pl.program_id(d) for d in range(len(grid))]
        first = functools.reduce(jnp.logical_and, [i == 0 for i in ids]) if ids else None
        last = functools.reduce(jnp.logical_and, [i == g - 1 for i, g in zip(ids, grid)]) if ids else None

        def each(method):
            i0, o0, s0 = n_in, n_out, n_scr
            for r in riders:
                getattr(r, method)(ins[i0:i0 + len(r.ins)], outs[o0:o0 + len(r.out_shape)], scr[s0:s0 + len(r.scratch)])
                i0, o0, s0 = i0 + len(r.ins), o0 + len(r.out_shape), s0 + len(r.scratch)

        if first is None:
            each("start")
        else:
            pl.when(first)(lambda: each("start"))
        body(*ins[:n_in], *outs[:n_out], *scr[:n_scr])
        if last is None:
            each("finish")
        else:
            pl.when(last)(lambda: each("finish"))

    call = pl.pallas_call(
        full_body, out_shape=k_out_shape + r_outs, in_specs=list(in_specs) + [_HBM] * len(r_ins),
        out_specs=k_out_specs + [_HBM] * len(r_outs), grid=grid, scratch_shapes=list(scratch_shapes) + r_scr, **kw)

    def run(*args):
        res = call(*args, *r_ins)
        o0 = n_out
        for r in riders:
            r.results = list(res[o0:o0 + len(r.out_shape)])
            o0 += len(r.out_shape)
        return res[0] if single else list(res[:n_out])

    return run


def run_riders(riders, *, name):
    hosted_call(riders, lambda: None, name=name, out_shape=[], in_specs=[], out_specs=[])()
    return [r.results for r in riders]


def _pick(n, cands):
    for c in cands:
        if n % c == 0:
            return c
    return n


def _params(sem):
    return pltpu.CompilerParams(dimension_semantics=sem, vmem_limit_bytes=VMEM_LIMIT)


MATMUL_VMEM_BUDGET = 40 * 2 ** 20


def _matmul_tiles(m, n, k, bytes_a, bytes_b, bytes_mn, fixed):
    tm, tn, tk = fixed if fixed is not None else (None, None, None)
    tm = tm or _pick(m, (1024, 512, 256, 128))
    tn = tn or _pick(n, (512, 256, 128))
    if tk is None:
        for tk in [c for c in (2048, 1536, 1024, 512, 256, 128) if k % c == 0] + [k]:
            if 2 * (tm * tk * bytes_a + tk * tn * bytes_b + tm * tn * bytes_mn) + tm * tn * 4 <= MATMUL_VMEM_BUDGET:
                break
    return tm, tn, tk


def matmul(a, b, *, name, ta=False, tb=False, post=None, post_ins=(), extra_out=None, out_dtype=F32, tiles=None,
           b_view=None, out_view=None, riders=()):
    (k, m) = a.shape if ta else a.shape[::-1]
    (kb, n) = b_view[:2] if b_view is not None else (b.shape[::-1] if tb else b.shape)
    assert k == kb, (a.shape, b.shape, ta, tb)
    bytes_mn = sum(p.dtype.itemsize for p in post_ins) + jnp.dtype(out_dtype).itemsize
    bytes_mn += jnp.dtype(extra_out[1]).itemsize if extra_out else 0
    tm, tn, tk = _matmul_tiles(m, n, k, a.dtype.itemsize, b.dtype.itemsize, bytes_mn, tiles)
    nk = k // tk
    dims = ((0,) if ta else (1,), (1,) if tb else (0,))
    n_post = len(post_ins)
    n_out = 2 if extra_out else 1

    def body(*refs):
        a_ref, b_ref = refs[:2]
        post_refs = refs[2:2 + n_post]
        o_refs, acc = refs[-1 - n_out:-1], refs[-1]
        kk = pl.program_id(2)

        @pl.when(kk == 0)
        def _():
            acc[...] = jnp.zeros_like(acc)

        b_tile = b_ref[...]
        acc[...] += _bdot(a_ref[...], b_tile.reshape(-1, b_tile.shape[-1]), dims)

        @pl.when(kk == nk - 1)
        def _():
            r = acc[...]
            if post is not None:
                r = post(r, *[p[...] for p in post_refs])
            o_refs[0][...] = r.astype(out_dtype)
            if extra_out:
                o_refs[1][...] = extra_out[0](r).astype(extra_out[1])

    a_spec = pl.BlockSpec((tk, tm), lambda i, j, kk: (kk, i)) if ta else pl.BlockSpec((tm, tk), lambda i, j, kk: (i, kk))
    if b_view is not None:
        b_spec = b_view[2]
    else:
        b_spec = pl.BlockSpec((tn, tk), lambda i, j, kk: (j, kk)) if tb else pl.BlockSpec((tk, tn), lambda i, j, kk: (kk, j))
    mn_spec = pl.BlockSpec((tm, tn), lambda i, j, kk: (i, j))
    o_shape, o_spec, into = ((m, n), mn_spec, None) if out_view is None else out_view
    ins, specs, aliases = [a, b, *post_ins], [a_spec, b_spec] + [mn_spec] * n_post, {}
    if into is not None:
        aliases = {len(ins): 0}
        ins.append(into)
        specs.append(pl.BlockSpec(memory_space=pl.ANY))
    out_shape = [jax.ShapeDtypeStruct(o_shape, out_dtype)]
    out_specs = [o_spec]
    if extra_out:
        out_shape.append(jax.ShapeDtypeStruct((m, n), extra_out[1]))
        out_specs.append(mn_spec)
    res = hosted_call(
        riders, body, name=name, grid=(m // tm, n // tn, nk), in_specs=specs, out_specs=out_specs, out_shape=out_shape,
        input_output_aliases=aliases, scratch_shapes=[pltpu.VMEM((tm, tn), F32)],
        compiler_params=_params(("parallel", "parallel", "arbitrary")),
    )(*ins)
    return res if extra_out else res[0]


def rows_call(fn, row_ins, full_ins, row_outs, acc_outs, *, tm, name, riders=()):
    row_ins = [r if isinstance(r, tuple) else (r, r.shape[-1], 0) for r in row_ins]
    t = row_ins[0][0].shape[-2]
    tm = min(tm, t)
    n_in = len(row_ins) + len(full_ins)
    n_row = len(row_outs)

    def body(*refs):
        res = fn(*[[r[h] for h in range(r.shape[0])] if (i < len(row_ins) and len(r.shape) == 3) else r[...]
                   for i, r in enumerate(refs[:n_in])])
        res = res if isinstance(res, (tuple, list)) else (res,)
        outs = refs[n_in:]
        for ref, val in zip(outs[:n_row], res[:n_row]):
            if len(ref.shape) == 3:
                for h, vh in enumerate(val):
                    ref[h] = vh.astype(ref.dtype)
            else:
                ref[...] = val.astype(ref.dtype)
        first = pl.program_id(0) == 0
        for ref, val in zip(outs[n_row:], res[n_row:]):
            @pl.when(first)
            def _(ref=ref, val=val):
                ref[...] = val

            @pl.when(jnp.logical_not(first))
            def _(ref=ref, val=val):
                ref[...] += val

    def full_spec(shape):
        return pl.BlockSpec(shape, lambda i, nd=len(shape): (0,) * nd)

    def row_spec(lead, w, cb):
        if lead is None:
            return pl.BlockSpec((tm, w), lambda i: (i, cb))
        return pl.BlockSpec((lead, tm, w), lambda i: (0, i, cb))

    def lead_cols(c):
        return c if isinstance(c, tuple) else (None, c)

    in_specs = [row_spec(a.shape[0] if a.ndim == 3 else None, w, cb) for (a, w, cb) in row_ins]
    in_specs += [full_spec(f.shape) for f in full_ins]
    out_specs = [row_spec(*lead_cols(c), 0) for c, _ in row_outs] + [full_spec(s) for s in acc_outs]
    out_shape = [jax.ShapeDtypeStruct(tuple(d for d in (lead_cols(c)[0], t, lead_cols(c)[1]) if d is not None), dt)
                 for c, dt in row_outs] + [jax.ShapeDtypeStruct(s, F32) for s in acc_outs]
    res = hosted_call(
        riders, body, name=name, grid=(t // tm,), in_specs=in_specs, out_specs=out_specs, out_shape=out_shape,
        compiler_params=_params(("arbitrary",)),
    )(*[r[0] for r in row_ins], *full_ins)
    return res


def vjp_rows(fn, n_diff_row, row_diff_full):
    def bwd(*args, n_row, n_ct):
        prim_rows = args[:n_row]
        cts = args[n_row:n_row + n_ct]
        fulls = args[n_row + n_ct:]
        _, vjp = jax.vjp(fn, *prim_rows, *fulls)
        g = vjp(cts[0] if n_ct == 1 else tuple(cts))
        out = list(g[:n_diff_row])
        out += [gf for gf, d in zip(g[n_row:], row_diff_full) if d]
        return tuple(out)
    return bwd


def _shift_down(x, s):
    if s == 0:
        return x
    t = lax.broadcasted_iota(jnp.int32, x.shape, 0)
    return jnp.where(t >= s, pltpu.roll(x, s, 0), 0.0)


def _shift_up(x, s):
    if s == 0:
        return x
    n = x.shape[0]
    t = lax.broadcasted_iota(jnp.int32, x.shape, 0)
    return jnp.where(t < n - s, pltpu.roll(x, n - s, 0), 0.0)


def _conv(x, w_ref):
    return sum(w_ref[pl.ds(j, 1), :] * _shift_down(x, CONV_WIDTH - 1 - j) for j in range(CONV_WIDTH))


_DN_POST = (lambda c: l2n(jax.nn.silu(c)) * SCALE, lambda c: l2n(jax.nn.silu(c)), jax.nn.silu)


def dn_prep_fwd(proj, conv_w, *, name, riders=()):
    t = proj.shape[0]

    def body(xq, xk, xv, wq, wk, wv, oq, ok, ov):
        for x_ref, w_ref, o_ref, post in zip((xq, xk, xv), (wq, wk, wv), (oq, ok, ov), _DN_POST):
            o_ref[...] = post(_conv(x_ref[...], w_ref))

    x_specs = [pl.BlockSpec((t, HEAD_DIM), lambda h, g=g: (0, g * N_HEADS + h)) for g in range(3)]
    w_specs = [pl.BlockSpec((CONV_WIDTH, HEAD_DIM), lambda h, g=g: (0, g * N_HEADS + h)) for g in range(3)]
    o_spec = pl.BlockSpec((None, t, HEAD_DIM), lambda h: (h, 0, 0))
    return hosted_call(
        riders, body, name=name, grid=(N_HEADS,), in_specs=x_specs + w_specs, out_specs=[o_spec] * 3,
        out_shape=[jax.ShapeDtypeStruct((N_HEADS, t, HEAD_DIM), F32)] * 3, compiler_params=_params(("parallel",)),
    )(proj, proj, proj, conv_w, conv_w, conv_w)


def dn_prep_bwd(proj, conv_w, dq, dk, dv, *, name, riders=()):
    t = proj.shape[0]

    def body(xq, xk, xv, wq, wk, wv, gq, gk, gv, dxq, dxk, dxv, dwq, dwk, dwv):
        for x_ref, w_ref, g_ref, dx_ref, dw_ref, post in zip(
                (xq, xk, xv), (wq, wk, wv), (gq, gk, gv), (dxq, dxk, dxv), (dwq, dwk, dwv), _DN_POST):
            x = x_ref[...]
            _, vjp = jax.vjp(post, _conv(x, w_ref))
            dc, = vjp(g_ref[...])
            dx_ref[...] = sum(w_ref[pl.ds(j, 1), :] * _shift_up(dc, CONV_WIDTH - 1 - j) for j in range(CONV_WIDTH))
            for j in range(CONV_WIDTH):
                dw_ref[pl.ds(j, 1), :] = jnp.sum(dc * _shift_down(x, CONV_WIDTH - 1 - j), axis=0, keepdims=True)

    x_specs = [pl.BlockSpec((t, HEAD_DIM), lambda h, g=g: (0, g * N_HEADS + h)) for g in range(3)]
    w_specs = [pl.BlockSpec((CONV_WIDTH, HEAD_DIM), lambda h, g=g: (0, g * N_HEADS + h)) for g in range(3)]
    g_spec = pl.BlockSpec((None, t, HEAD_DIM), lambda h: (h, 0, 0))
    dx_spec = pl.BlockSpec((t, HEAD_DIM), lambda h: (0, h))
    dw_spec = pl.BlockSpec((CONV_WIDTH, HEAD_DIM), lambda h: (0, h))
    return hosted_call(
        riders, body, name=name, grid=(N_HEADS,), in_specs=x_specs + w_specs + [g_spec] * 3, out_specs=[dx_spec] * 3 + [dw_spec] * 3,
        out_shape=[jax.ShapeDtypeStruct((t, D_MODEL), F32)] * 3 + [jax.ShapeDtypeStruct((CONV_WIDTH, D_MODEL), F32)] * 3,
        compiler_params=_params(("parallel",)),
    )(proj, proj, proj, conv_w, conv_w, conv_w, dq, dk, dv)


INTRA_CHUNKS = 4


def _lane_column(x, lane_index):
    lane = lax.broadcasted_iota(jnp.int32, x.shape, 1)
    return jnp.sum(jnp.where(lane == lane_index, x, 0.0), axis=1, keepdims=True)


def _head_columns(g, first_lane):
    return jnp.concatenate([_lane_column(g, first_lane + h)[None] for h in range(N_HEADS)], axis=0)


def _intra_of_gates(q, k, v, gates):
    nb = N_HEADS * (gates.shape[0] // CHUNK)

    def chunks(x):
        return x.reshape(nb, CHUNK, x.shape[-1])

    res = delta_intra(chunks(q), chunks(k), chunks(v), chunks(_head_columns(gates, 0)), chunks(_head_columns(gates, N_HEADS)))
    return tuple(x.reshape(N_HEADS, -1, x.shape[-1]) for x in res)


def _step_of_gates(s, q, k, gates, u, w, qk):
    return delta_step(s, q, k, _head_columns(gates, 0), u, w, qk)


def _head_major(rows, w, index):
    return pl.BlockSpec((N_HEADS, rows, w), lambda i: (0, index(i), 0))


def delta_intra_fwd(q, k, v, gates, *, name, riders=()):
    t = q.shape[1]
    rows = min(INTRA_CHUNKS, t // CHUNK) * CHUNK

    def body(q_ref, k_ref, v_ref, g_ref, u_ref, w_ref, qk_ref):
        for ref, val in zip((u_ref, w_ref, qk_ref), _intra_of_gates(q_ref[...], k_ref[...], v_ref[...], g_ref[...])):
            ref[...] = val

    x_spec, qk_spec = (_head_major(rows, w, lambda i: i) for w in (HEAD_DIM, CHUNK))
    g_spec = pl.BlockSpec((rows, LANES), lambda i: (i, 0))
    return hosted_call(
        riders, body, name=name, grid=(t // rows,), in_specs=[x_spec] * 3 + [g_spec], out_specs=[x_spec, x_spec, qk_spec],
        out_shape=[jax.ShapeDtypeStruct((N_HEADS, t, HEAD_DIM), F32)] * 2 + [jax.ShapeDtypeStruct((N_HEADS, t, CHUNK), F32)],
        compiler_params=_params(("parallel",)),
    )(q, k, v, gates)


def delta_seq_fwd(q, k, gates, u, w, qk, *, name, riders=()):
    t = q.shape[1]
    nc = t // CHUNK

    def body(q_ref, k_ref, g_ref, u_ref, w_ref, qk_ref, o_ref, s0_ref, s_ref):
        @pl.when(pl.program_id(0) == 0)
        def _():
            s_ref[...] = jnp.zeros_like(s_ref)

        s = s_ref[...]
        s0_ref[...] = s
        o, s_new = _step_of_gates(s, q_ref[...], k_ref[...], g_ref[...], u_ref[...], w_ref[...], qk_ref[...])
        o_ref[...] = o
        s_ref[...] = s_new

    x_spec, qk_spec = (_head_major(CHUNK, w, lambda c: c) for w in (HEAD_DIM, CHUNK))
    g_spec = pl.BlockSpec((CHUNK, LANES), lambda c: (c, 0))
    s_spec = pl.BlockSpec((N_HEADS, None, HEAD_DIM, HEAD_DIM), lambda c: (0, c, 0, 0))
    return hosted_call(
        riders, body, name=name, grid=(nc,), in_specs=[x_spec, x_spec, g_spec, x_spec, x_spec, qk_spec], out_specs=[x_spec, s_spec],
        out_shape=[jax.ShapeDtypeStruct((N_HEADS, t, HEAD_DIM), F32),
                   jax.ShapeDtypeStruct((N_HEADS, nc, HEAD_DIM, HEAD_DIM), F32)],
        scratch_shapes=[pltpu.VMEM((N_HEADS, HEAD_DIM, HEAD_DIM), F32)],
        compiler_params=_params(("arbitrary",)),
    )(q, k, gates, u, w, qk)


def delta_seq_bwd(q, k, gates, u, w, qk, s0, do, *, name, riders=()):
    t = q.shape[1]
    nc = t // CHUNK

    def body(q_ref, k_ref, g_ref, u_ref, w_ref, qk_ref, s0_ref, do_ref,
             dq_ref, dk_ref, dg_ref, du_ref, dw_ref, dqk_ref, ds_ref):
        @pl.when(pl.program_id(0) == 0)
        def _():
            ds_ref[...] = jnp.zeros_like(ds_ref)

        _, vjp = jax.vjp(_step_of_gates, s0_ref[...], q_ref[...], k_ref[...], g_ref[...], u_ref[...], w_ref[...], qk_ref[...])
        ds, dq, dk, dg, du, dw, dqk = vjp((do_ref[...], ds_ref[...]))
        for ref, val in zip((ds_ref, dq_ref, dk_ref, dg_ref, du_ref, dw_ref, dqk_ref), (ds, dq, dk, dg, du, dw, dqk)):
            ref[...] = val

    x_spec, qk_spec = (_head_major(CHUNK, w, lambda c: nc - 1 - c) for w in (HEAD_DIM, CHUNK))
    g_spec = pl.BlockSpec((CHUNK, LANES), lambda c: (nc - 1 - c, 0))
    s_spec = pl.BlockSpec((N_HEADS, None, HEAD_DIM, HEAD_DIM), lambda c: (0, nc - 1 - c, 0, 0))
    head_shape = [jax.ShapeDtypeStruct((N_HEADS, t, w_), F32) for w_ in (HEAD_DIM, HEAD_DIM, HEAD_DIM, HEAD_DIM, CHUNK)]
    return hosted_call(
        riders, body, name=name, grid=(nc,), in_specs=[x_spec, x_spec, g_spec, x_spec, x_spec, qk_spec, s_spec, x_spec],
        out_specs=[x_spec, x_spec, g_spec, x_spec, x_spec, qk_spec],
        out_shape=head_shape[:2] + [jax.ShapeDtypeStruct((t, LANES), F32)] + head_shape[2:],
        scratch_shapes=[pltpu.VMEM((N_HEADS, HEAD_DIM, HEAD_DIM), F32)],
        compiler_params=_params(("arbitrary",)),
    )(q, k, gates, u, w, qk, s0, do)


def delta_intra_bwd(q, k, v, gates, du, dw, dqk, dq_s, dk_s, dg_s, *, name, riders=()):
    t = q.shape[1]
    rows = min(INTRA_CHUNKS, t // CHUNK) * CHUNK

    def body(q_ref, k_ref, v_ref, g_ref, du_ref, dw_ref, dqk_ref, dqs_ref, dks_ref, dgs_ref, dq_ref, dk_ref, dv_ref, dg_ref):
        _, vjp = jax.vjp(_intra_of_gates, q_ref[...], k_ref[...], v_ref[...], g_ref[...])
        dq, dk, dv, dg = vjp((du_ref[...], dw_ref[...], dqk_ref[...]))
        dq_ref[...] = dq + dqs_ref[...]
        dk_ref[...] = dk + dks_ref[...]
        dv_ref[...] = dv
        dg_ref[...] = dg + dgs_ref[...]

    x_spec, qk_spec = (_head_major(rows, w, lambda i: i) for w in (HEAD_DIM, CHUNK))
    g_spec = pl.BlockSpec((rows, LANES), lambda i: (i, 0))
    return hosted_call(
        riders, body, name=name, grid=(t // rows,),
        in_specs=[x_spec] * 3 + [g_spec, x_spec, x_spec, qk_spec, x_spec, x_spec, g_spec],
        out_specs=[x_spec] * 3 + [g_spec],
        out_shape=[jax.ShapeDtypeStruct((N_HEADS, t, HEAD_DIM), F32)] * 3 + [jax.ShapeDtypeStruct((t, LANES), F32)],
        compiler_params=_params(("parallel",)),
    )(q, k, v, gates, du, dw, dqk, dq_s, dk_s, dg_s)


_V_BLOCK = 2 * N_HEADS
FOX_GROUPS = 8


def _fox_groups(t):
    nq = t // Q_BLOCK
    per = max(1, nq // FOX_GROUPS)
    return [(g0, per, (g0 + per) * Q_BLOCK) for g0 in range(0, nq, per)]


def fox_attn_fwd(q, k, proj, fq, fk, *, name, riders=()):
    t = q.shape[0]

    def body(q_ref, k_ref, v_ref, fq_ref, fk_ref, o_ref, kb_ref, vb_ref):
        head = pl.program_id(0)
        kb_ref[...] = k_ref[...].astype(BF16)
        vb_ref[...] = v_ref[...].astype(BF16)
        for g0, per, keys in _fox_groups(t):
            def block(j, carry, g0=g0, keys=keys):
                rows = pl.ds(pl.multiple_of((g0 + j) * Q_BLOCK, Q_BLOCK), Q_BLOCK)
                p = fox_probs(q_ref[rows, :].astype(BF16), kb_ref[0:keys, :], _lane_column(fq_ref[rows, :], head),
                              fk_ref[:, 0:keys], (g0 + j) * Q_BLOCK)
                o_ref[rows, :] = jnp.dot(p.astype(BF16), vb_ref[0:keys, :], preferred_element_type=F32)
                return carry
            lax.fori_loop(0, per, block, 0)

    x_spec = pl.BlockSpec((t, HEAD_DIM), lambda h: (0, h))
    v_spec = pl.BlockSpec((t, HEAD_DIM), lambda h: (0, _V_BLOCK + h))
    fq_spec = pl.BlockSpec((t, LANES), lambda h: (0, 0))
    fk_spec = pl.BlockSpec((None, 1, t), lambda h: (h, 0, 0))
    return hosted_call(
        riders, body, name=name, grid=(N_HEADS,), in_specs=[x_spec, x_spec, v_spec, fq_spec, fk_spec], out_specs=x_spec,
        out_shape=jax.ShapeDtypeStruct((t, D_MODEL), F32), scratch_shapes=[pltpu.VMEM((t, HEAD_DIM), BF16)] * 2,
        compiler_params=_params(("parallel",)),
    )(q, k, proj, fq, fk)


def fox_attn_bwd(q, k, proj, fq, fk, do, *, name, riders=()):
    t = q.shape[0]

    def body(q_ref, k_ref, v_ref, fq_ref, fk_ref, do_ref, dq_ref, dk_ref, dv_ref, dfq_ref, dfk_ref, kb_ref, vb_ref):
        head = pl.program_id(0)

        @pl.when(head == 0)
        def _():
            dfq_ref[...] = jnp.zeros_like(dfq_ref)

        kb_ref[...] = k_ref[...].astype(BF16)
        vb_ref[...] = v_ref[...].astype(BF16)
        dk_ref[...] = jnp.zeros_like(dk_ref)
        dv_ref[...] = jnp.zeros_like(dv_ref)
        dfk_ref[...] = jnp.zeros_like(dfk_ref)
        nt = (((1,), (1,)), ((), ()))
        tn = (((0,), (0,)), ((), ()))
        for g0, per, keys in _fox_groups(t):
            def block(j, carry, g0=g0, keys=keys):
                rows = pl.ds(pl.multiple_of((g0 + j) * Q_BLOCK, Q_BLOCK), Q_BLOCK)
                qb, dob = q_ref[rows, :].astype(BF16), do_ref[rows, :].astype(BF16)
                kb, vb = kb_ref[0:keys, :], vb_ref[0:keys, :]
                p = fox_probs(qb, kb, _lane_column(fq_ref[rows, :], head), fk_ref[:, 0:keys], (g0 + j) * Q_BLOCK)
                dp = lax.dot_general(dob, vb, nt, preferred_element_type=F32)
                dz = p * (dp - jnp.sum(dp * p, axis=-1, keepdims=True))
                pb, dzb = p.astype(BF16), dz.astype(BF16)
                dq_ref[rows, :] = jnp.dot(dzb, kb, preferred_element_type=F32)
                lane = lax.broadcasted_iota(jnp.int32, (Q_BLOCK, LANES), 1)
                dfq_ref[rows, :] += jnp.where(lane == head, jnp.sum(dz, axis=-1, keepdims=True), 0.0)
                dk_ref[0:keys, :] += lax.dot_general(dzb, qb, tn, preferred_element_type=F32)
                dv_ref[0:keys, :] += lax.dot_general(pb, dob, tn, preferred_element_type=F32)
                dfk_ref[:, 0:keys] -= jnp.sum(dz, axis=0, keepdims=True)
                return carry
            lax.fori_loop(0, per, block, 0)

    x_spec = pl.BlockSpec((t, HEAD_DIM), lambda h: (0, h))
    v_spec = pl.BlockSpec((t, HEAD_DIM), lambda h: (0, _V_BLOCK + h))
    fq_spec = pl.BlockSpec((t, LANES), lambda h: (0, 0))
    fk_spec = pl.BlockSpec((None, 1, t), lambda h: (h, 0, 0))
    return hosted_call(
        riders, body, name=name, grid=(N_HEADS,), in_specs=[x_spec, x_spec, v_spec, fq_spec, fk_spec, x_spec],
        out_specs=[x_spec, x_spec, x_spec, fq_spec, fk_spec],
        out_shape=[jax.ShapeDtypeStruct((t, D_MODEL), F32)] * 3
        + [jax.ShapeDtypeStruct((t, LANES), F32), jax.ShapeDtypeStruct((N_HEADS, 1, t), F32)],
        scratch_shapes=[pltpu.VMEM((t, HEAD_DIM), BF16)] * 2, compiler_params=_params(("arbitrary",)),
    )(q, k, proj, fq, fk, do)


def memkv_fwd(mem, mnw, wkv, mknw, *, name):
    n = mem.shape[0]

    def body(mem_ref, mnw_ref, w_ref, mknw_ref, mk_ref, mv_ref):
        mk, mv = memkv_fn(mem_ref[...], mnw_ref[...], w_ref[...], mknw_ref[...])
        mk_ref[...] = mk
        mv_ref[...] = mv

    return pl.pallas_call(
        body, name=name, out_shape=[jax.ShapeDtypeStruct((n, MEM_WIDTH), F32)] * 2,
        compiler_params=pltpu.CompilerParams(vmem_limit_bytes=VMEM_LIMIT),
    )(mem, mnw, wkv, mknw)


def memkv_bwd(mem, mnw, wkv, mknw, dmk, dmv, *, name):
    def body(mem_ref, mnw_ref, w_ref, mknw_ref, dmk_ref, dmv_ref, dmnw_ref, dw_ref, dmknw_ref):
        f = functools.partial(memkv_fn, mem_ref[...])
        _, vjp = jax.vjp(f, mnw_ref[...], w_ref[...].astype(F32), mknw_ref[...])
        dmnw, dw, dmknw = vjp((dmk_ref[...], dmv_ref[...]))
        dmnw_ref[...] = dmnw
        dw_ref[...] = dw.astype(dw_ref.dtype)
        dmknw_ref[...] = dmknw

    return pl.pallas_call(
        body, name=name,
        out_shape=[jax.ShapeDtypeStruct(mnw.shape, F32), jax.ShapeDtypeStruct(wkv.shape, BF16), jax.ShapeDtypeStruct(mknw.shape, F32)],
        compiler_params=pltpu.CompilerParams(vmem_limit_bytes=VMEM_LIMIT),
    )(mem, mnw, wkv, mknw, dmk, dmv)


def _row(v, width=None):
    v = v.reshape(1, -1)
    if width is not None and v.shape[1] < width:
        v = jnp.pad(v, ((0, 0), (0, width - v.shape[1])))
    return v


def _norm_fwd(x, w, name, riders=()):
    return rows_call(lambda x, w: rms(x, w), [x], [w], [(D_MODEL, BF16)], [], tm=512, name=name, riders=riders)[0]


def _norm_bwd(x, w, dh, dx_in, name, riders=()):
    def fn(x, dh, dx_in, w):
        _, vjp = jax.vjp(rms, x, w)
        dx, dw = vjp(dh)
        return dx + dx_in, dw
    return rows_call(fn, [x, dh, dx_in], [w], [(D_MODEL, F32)], [(1, D_MODEL)], tm=512, name=name, riders=riders)


FF_PIECE = D_FF // N_DEV


def _add(r, x):
    return r + x


def _piece(rows, cols, index):
    return pl.BlockSpec((None, rows, cols), lambda i, j, kk: (index(i, j, kk), 0, 0))


def _two_pieces(rows, cols, index):
    return pl.BlockSpec((2, rows, cols), lambda i, j, kk: (index(i, j, kk), 0, 0))


def _mlp_fwd(x, n2w, w1, w2, layer, riders=()):
    riders = list(riders) + [None, None]
    h2 = _norm_fwd(x, n2w, f"norm2_fwd_{layer}")
    u, a1 = matmul(h2, w1, name=f"mlp1_fwd_{layer}", tiles=(None, FF_PIECE, D_MODEL),
                   extra_out=(lambda u: jnp.square(jnp.maximum(u, 0.0)), BF16),
                   b_view=(D_MODEL, D_FF, _piece(D_MODEL, FF_PIECE, lambda i, j, kk: j)), riders=riders[0])
    y = matmul(a1, w2, name=f"mlp2_fwd_{layer}", post=_add, post_ins=[x], tiles=(None, D_MODEL, 2 * FF_PIECE),
               b_view=(D_FF, D_MODEL, _two_pieces(FF_PIECE, D_MODEL, lambda i, j, kk: kk)), riders=riders[1])
    return y, (x, h2, u, a1)


def pair_sum(g, got, *, name):
    _, rows, cols = g.shape
    tile = _pick(rows, (512, 256, 128))
    c = lax.axis_index("c").astype(jnp.int32).reshape(1)

    def body(c_ref, a_ref, b_ref, o_ref):
        o_ref[...] = (a_ref[...].astype(F32) + b_ref[...].astype(F32)).astype(o_ref.dtype)

    grid_spec = pltpu.PrefetchScalarGridSpec(
        num_scalar_prefetch=1, grid=(4, rows // tile),
        in_specs=[pl.BlockSpec((None, tile, cols), lambda k, i, c_ref: (2 * k + c_ref[0], i, 0)),
                  pl.BlockSpec((None, tile, cols), lambda k, i, c_ref: (k, i, 0))],
        out_specs=pl.BlockSpec((None, tile, cols), lambda k, i, c_ref: (k, i, 0)))
    return pl.pallas_call(
        body, name=name, grid_spec=grid_spec, out_shape=jax.ShapeDtypeStruct((4, rows, cols), g.dtype),
        compiler_params=_params(("parallel", "parallel")),
    )(c, g, got)


def chip_sums(names, pieces, gots):
    return [pair_sum(a, got, name=f"grads_pair_sum_{n}") for n, a, got in zip(names, pieces, gots)]


def _mlp_bwd(dy, res, n2w, w1, w2, layer, riders=()):
    x, h2, u, a1 = res
    du = matmul(dy, w2, tb=True, name=f"mlp2_dx_{layer}", out_dtype=BF16, tiles=(None, 2 * FF_PIECE, D_MODEL),
                post=lambda r, u: r * (2.0 * jnp.maximum(u, 0.0)), post_ins=[u],
                b_view=(D_MODEL, D_FF, _two_pieces(FF_PIECE, D_MODEL, lambda i, j, kk: j)), riders=riders)
    dw2 = matmul(a1, dy, ta=True, name=f"mlp2_dw_{layer}", out_dtype=BF16, tiles=(FF_PIECE, D_MODEL, None), out_view=(
        w2.shape, _piece(FF_PIECE, D_MODEL, lambda i, j, kk: i), None))
    sib2 = sibling_rider([dw2])
    dh2 = matmul(du, w1, tb=True, name=f"mlp1_dx_{layer}", tiles=(None, D_MODEL, FF_PIECE),
                 b_view=(D_FF, D_MODEL, _piece(D_MODEL, FF_PIECE, lambda i, j, kk: kk)), riders=[sib2])
    dw1 = matmul(h2, du, ta=True, name=f"mlp1_dw_{layer}", out_dtype=BF16, tiles=(D_MODEL, FF_PIECE, None), out_view=(
        w1.shape, _piece(D_MODEL, FF_PIECE, lambda i, j, kk: j), None))
    sib1 = sibling_rider([dw1])
    dx, dn2w = _norm_bwd(x, n2w, dh2, dy, f"norm2_bwd_{layer}", riders=[sib1])
    return dx, dw1, dw2, dn2w, sib1, sib2


def _in_proj_bwd(h, dmain, dsmall, w_main, w_small, tag):
    dh = matmul(dmain, w_main, tb=True, name=f"inproj_dx_main_{tag}")
    dh = matmul(dsmall, w_small, tb=True, post=_add, post_ins=[dh], name=f"inproj_dx_small_{tag}")
    dw_main = matmul(h, dmain, ta=True, out_dtype=BF16, name=f"inproj_dw_main_{tag}")
    dw_small = matmul(h, dsmall, ta=True, out_dtype=BF16, name=f"inproj_dw_small_{tag}")
    return dh, dw_main, dw_small


def local_step(x, mem, target, w, m, v):
    t = x.shape[0]
    n_mem = mem.shape[0]
    g = {}

    def wire(a):
        return a.astype(BF16)

    (dn_g,), = run_riders([gather_rider([wire(w["dn_w_in"][0])])], name="weights_gather_first")
    dn_main, dn_ab = in_proj_weights(dn_g, DN_IN, 2 * N_HEADS)
    fox_w = wire(w["fox_w_in"][0])
    ride_out = gather_rider([wire(w["w_out"][0]), wire(w["w_out"][1]), w["dn_conv_w"][0]])
    ride_kv = gather_rider([wire(w["w_mem_kv"])])
    ride_mlp1_0 = gather_rider([wire(w["w_mlp1"][0])])
    ride_mlp2_0 = gather_rider([wire(w["w_mlp2"][0])])
    ride_fox_a, ride_fox_b = gather_rider([fox_w[:D_MODEL // 2]]), gather_rider([fox_w[D_MODEL // 2:]])
    ride_mlp_1 = gather_rider([wire(w["w_mlp1"][1]), wire(w["w_mlp2"][1])])
    mnw, mknw = _row(w["mem_norm_w"]), _row(w["mem_k_norm_w"])

    n1w0, n2w0 = _row(w["norm1_w"][0]), _row(w["norm2_w"][0])
    alog, dtb = _row(w["dn_a_log"][0], LANES), _row(w["dn_dt_bias"][0], LANES)
    onw, mqw0 = _row(w["dn_o_norm_w"][0]), _row(w["memq_norm_w"][0])
    x0 = x
    h0 = _norm_fwd(x0, n1w0, "norm1_fwd_0")
    pm0 = matmul(h0, dn_main, name="inproj_main_0", riders=[ride_out])
    w_out0, w_out1 = (a.reshape(OUT_IN, D_MODEL) for a in ride_out.results[:2])
    conv_w = ride_out.results[2].transpose(1, 0, 2).reshape(CONV_WIDTH, 3 * D_MODEL)
    ps0 = matmul(h0, dn_ab, name="inproj_small_0")
    gates = rows_call(dn_gates_fn, [ps0], [alog, dtb], [(LANES, F32)], [], tm=512, name="dn_gates_fwd")[0]
    q0, k0, v0 = dn_prep_fwd(pm0, conv_w, name="dn_prep_fwd", riders=[ride_kv])
    w_kv = ride_kv.results[0].reshape(D_MODEL, D_MODEL)
    mk, mv = memkv_fwd(mem, mnw, w_kv, mknw, name="memkv_fwd")
    u0, w0, qk0 = delta_intra_fwd(q0, k0, v0, gates, name="delta_intra_fwd", riders=[ride_mlp1_0])
    o0, s_start = delta_seq_fwd(q0, k0, gates, u0, w0, qk0, name="delta_seq_fwd", riders=[ride_mlp2_0])
    cat0 = rows_call(dn_out_fn, [o0, (pm0, D_MODEL, 3), (pm0, MEM_WIDTH, 8)], [onw, mqw0, mk, mv],
                     [(D_MODEL + MEM_WIDTH, BF16)], [], tm=256, name="dn_out_fwd")[0]
    (w1_0,), (w2_0,) = ride_mlp1_0.results, ride_mlp2_0.results
    x1 = matmul(cat0, w_out0, post=_add, post_ins=[x0], name="wout_fwd_0")
    x2, mlp_res0 = _mlp_fwd(x1, n2w0, w1_0, w2_0, 0, riders=[[ride_fox_a], [ride_fox_b]])
    fox_main, fox_f = in_proj_weights(
        jnp.concatenate([ride_fox_a.results[0], ride_fox_b.results[0]], axis=1), FOX_IN, N_HEADS)

    n1w1, n2w1 = _row(w["norm1_w"][1]), _row(w["norm2_w"][1])
    fbias = _row(w["fox_f_bias"][0], LANES)
    qnw, knw, mqw1 = _row(w["fox_q_norm_w"][0]), _row(w["fox_k_norm_w"][0]), _row(w["memq_norm_w"][1])
    h1 = _norm_fwd(x2, n1w1, "norm1_fwd_1")
    pm1 = matmul(h1, fox_main, name="inproj_main_1")
    ps1 = matmul(h1, fox_f, name="inproj_small_1")
    fq = rows_call(fox_fcum_fn, [ps1], [fbias], [(LANES, F32)], [], tm=t, name="fox_fcum_fwd")[0]
    fk = fq[:, :N_HEADS].T[:, None, :]
    q1, k1 = rows_call(fox_qk_fn, [(pm1, D_MODEL, 0), (pm1, D_MODEL, 1)], [qnw, knw], [(D_MODEL, F32)] * 2, [], tm=256,
                       name="fox_qk_fwd")
    o1 = fox_attn_fwd(q1, k1, pm1, fq, fk, name="fox_attn_fwd", riders=[ride_mlp_1])
    cat1 = rows_call(fox_out_fn, [o1, (pm1, D_MODEL, 3), (pm1, MEM_WIDTH, 8)], [mqw1, mk, mv],
                     [(D_MODEL + MEM_WIDTH, BF16)], [], tm=256, name="fox_out_fwd")[0]
    w1_1, w2_1 = ride_mlp_1.results
    x3 = matmul(cat1, w_out1, post=_add, post_ins=[x2], name="wout_fwd_1")
    y, mlp_res1 = _mlp_fwd(x3, n2w1, w1_1, w2_1, 1)

    def loss_fn(y, tgt):
        e = y - tgt
        return e * (1.0 / D_MODEL), jnp.sum(jnp.sum(e * e, axis=1, keepdims=True), axis=0, keepdims=True)
    dy, sq = rows_call(loss_fn, [y, target], [], [(D_MODEL, F32)], [(1, 1)], tm=512, name="loss")
    loss = sq[0, 0] * (0.5 / D_MODEL)

    dx3, dw1_1, dw2_1, dn2w1, sib1, sib2 = _mlp_bwd(dy, mlp_res1, n2w1, w1_1, w2_1, 1)
    dcat1 = matmul(dx3, w_out1, tb=True, name="wout_dx_1")
    dwo_1 = matmul(cat1, dx3, ta=True, out_dtype=BF16, name="wout_dw_1").reshape(N_DEV, OUT_IN // N_DEV, D_MODEL)
    sibo = sibling_rider([dwo_1])
    do1, dgate1, dqm1, dmqw1, dmk1, dmv1 = rows_call(
        functools.partial(vjp_rows(fox_out_fn, 3, (True, True, True)), n_row=3, n_ct=1),
        [o1, (pm1, D_MODEL, 3), (pm1, MEM_WIDTH, 8), dcat1], [mqw1, mk, mv],
        [(D_MODEL, F32), (D_MODEL, F32), (MEM_WIDTH, F32)], [(1, HEAD_DIM), (n_mem, MEM_WIDTH), (n_mem, MEM_WIDTH)],
        tm=256, name="fox_out_bwd", riders=[sibo])
    ride_l1 = chips_rider(chip_sums(["w_mlp2_1", "w_mlp1_1", "w_out_1"], [dw2_1, dw1_1, dwo_1],
                                    sib2.results + sib1.results + sibo.results))
    dq1, dk1, dv1, dfq, dfk = fox_attn_bwd(q1, k1, pm1, fq, fk, do1, name="fox_attn_bwd", riders=[ride_l1])
    dqraw1, dkraw1, dqnw, dknw = rows_call(
        functools.partial(vjp_rows(fox_qk_fn, 2, (True, True)), n_row=2, n_ct=2),
        [(pm1, D_MODEL, 0), (pm1, D_MODEL, 1), dq1, dk1], [qnw, knw],
        [(D_MODEL, F32)] * 2, [(1, HEAD_DIM)] * 2, tm=256, name="fox_qk_bwd")
    dfcum = dfq + jnp.pad(dfk[:, 0, :].T, ((0, 0), (0, LANES - N_HEADS)))
    dps1, dfbias = rows_call(
        functools.partial(vjp_rows(fox_fcum_fn, 1, (True,)), n_row=1, n_ct=1),
        [ps1, dfcum], [fbias], [(LANES, F32)], [(1, LANES)], tm=t, name="fox_fcum_bwd")
    dpm1 = jnp.concatenate([dqraw1, dkraw1, dv1, dgate1, dqm1], axis=1)
    dh1, dwmain1, dwsmall1 = _in_proj_bwd(h1, dpm1, dps1, fox_main, fox_f, "1")
    g_fox = in_proj_pieces(dwmain1, dwsmall1, N_HEADS, FOX_IN)
    sibf = sibling_rider([g_fox])
    dx2, dn1w1 = _norm_bwd(x2, n1w1, dh1, dx3, "norm1_bwd_1", riders=[sibf])
    ride_fox_g = chips_rider(chip_sums(["fox_w_in"], [g_fox], sibf.results))

    dx1, dw1_0, dw2_0, dn2w0, sib1, sib2 = _mlp_bwd(dx2, mlp_res0, n2w0, w1_0, w2_0, 0)
    dcat0 = matmul(dx1, w_out0, tb=True, name="wout_dx_0")
    dwo_0 = matmul(cat0, dx1, ta=True, out_dtype=BF16, name="wout_dw_0").reshape(N_DEV, OUT_IN // N_DEV, D_MODEL)
    sibo = sibling_rider([dwo_0])
    do0, dz0, dqm0, donw, dmqw0, dmk0, dmv0 = rows_call(
        functools.partial(vjp_rows(dn_out_fn, 3, (True, True, True, True)), n_row=3, n_ct=1),
        [o0, (pm0, D_MODEL, 3), (pm0, MEM_WIDTH, 8), dcat0], [onw, mqw0, mk, mv],
        [((N_HEADS, HEAD_DIM), F32), (D_MODEL, F32), (MEM_WIDTH, F32)],
        [(1, HEAD_DIM), (1, HEAD_DIM), (n_mem, MEM_WIDTH), (n_mem, MEM_WIDTH)], tm=256, name="dn_out_bwd", riders=[sibo])
    h_l0 = chip_sums(["w_mlp2_0", "w_mlp1_0", "w_out_0"], [dw2_0, dw1_0, dwo_0], sib2.results + sib1.results + sibo.results)
    ride_l0_mlp, ride_l0_out = chips_rider(h_l0[:2]), chips_rider(h_l0[2:])
    dq_s, dk_s, dg_s, du0, dw0, dqk0 = delta_seq_bwd(q0, k0, gates, u0, w0, qk0, s_start, do0, name="delta_seq_bwd",
                                                     riders=[ride_fox_g])
    dq0, dk0, dv0, dgates = delta_intra_bwd(q0, k0, v0, gates, du0, dw0, dqk0, dq_s, dk_s, dg_s,
                                            name="delta_intra_bwd", riders=[ride_l0_mlp])
    dxq, dxk, dxv, dcq, dck, dcv = dn_prep_bwd(pm0, conv_w, dq0, dk0, dv0, name="dn_prep_bwd", riders=[ride_l0_out])
    dconv = jnp.concatenate([dcq, dck, dcv], axis=1)
    dps0, dalog, ddtb = rows_call(
        functools.partial(vjp_rows(dn_gates_fn, 1, (True, True)), n_row=1, n_ct=1),
        [ps0, dgates], [alog, dtb], [(LANES, F32)], [(1, LANES)] * 2, tm=512, name="dn_gates_bwd")
    dpm0 = jnp.concatenate([dxq, dxk, dxv, dz0, dqm0], axis=1)
    dh0, dwmain0, dwsmall0 = _in_proj_bwd(h0, dpm0, dps0, dn_main, dn_ab, "0")
    g_dn = in_proj_pieces(dwmain0, dwsmall0, 2 * N_HEADS, DN_IN)
    g_conv = dconv.reshape(CONV_WIDTH, N_DEV, -1).transpose(1, 0, 2).astype(BF16)
    sibd = sibling_rider([g_dn, g_conv])
    grad_x, dn1w0 = _norm_bwd(x0, n1w0, dh0, dx1, "norm1_bwd_0", riders=[sibd])

    dmnw, dwkv, dmknw = memkv_bwd(mem, mnw, w_kv, mknw, dmk0 + dmk1, dmv0 + dmv1, name="memkv_bwd")
    g_kv = dwkv.reshape(N_DEV, D_MODEL // N_DEV, D_MODEL)

    g["mem_norm_w"] = dmnw[0]
    g["mem_k_norm_w"] = dmknw[0]
    g["norm1_w"] = jnp.concatenate([dn1w0, dn1w1], axis=0)
    g["dn_a_log"] = dalog[:, :N_HEADS]
    g["dn_dt_bias"] = ddtb[:, :N_HEADS]
    g["dn_o_norm_w"] = donw
    g["fox_f_bias"] = dfbias[:, :N_HEADS]
    g["fox_q_norm_w"] = dqnw
    g["fox_k_norm_w"] = dknw
    g["memq_norm_w"] = jnp.concatenate([dmqw0, dmqw1], axis=0)
    g["norm2_w"] = jnp.concatenate([dn2w0, dn2w1], axis=0)

    def layers(l0, l1):
        return jnp.stack([l0, l1], axis=1).reshape(4, -1, l0.shape[-1])

    shards = {n: [d[n].reshape(r, c) for d in (w, m, v)] for n, r, c in BIG}
    ride_dn = chips_rider(chip_sums(["dn_w_in", "dn_conv_w"], [g_dn, g_conv], sibd.results))
    sibk = sibling_rider([g_kv])
    ride_small = gather_rider([pack_small(g)])
    out = {}
    out["w_mlp1"] = adamw(layers(ride_l0_mlp.results[1], ride_l1.results[1]), *shards["w_mlp1"], name="adamw_w_mlp1",
                          riders=[ride_dn, sibk, ride_small])
    ride_kv_g = chips_rider(chip_sums(["w_mem_kv"], [g_kv], sibk.results))
    out["w_mlp2"] = adamw(layers(ride_l0_mlp.results[0], ride_l1.results[0]), *shards["w_mlp2"], name="adamw_w_mlp2",
                          riders=[ride_kv_g])
    out["w_out"] = adamw(layers(ride_l0_out.results[0], ride_l1.results[2]), *shards["w_out"], name="adamw_w_out")
    out["fox_w_in"] = adamw(ride_fox_g.results[0], *shards["fox_w_in"], name="adamw_fox_w_in")
    out["dn_w_in"] = adamw(ride_dn.results[0], *shards["dn_w_in"], name="adamw_dn_w_in")
    out["dn_conv_w"] = adamw(ride_dn.results[1], *shards["dn_conv_w"], name="adamw_dn_conv_w")
    out["w_mem_kv"] = adamw(ride_kv_g.results[0], *shards["w_mem_kv"], name="adamw_w_mem_kv")
    small = adamw(ride_small.results[0], pack_small(w), pack_small(m), pack_small(v), name="adamw_small")
    return loss, grad_x, {n: [o.reshape(w[n].shape) for o in outs] for n, outs in out.items()}, small


WEIGHTS = ["mem_norm_w", "w_mem_kv", "mem_k_norm_w", "norm1_w", "dn_w_in", "dn_conv_w", "dn_a_log", "dn_dt_bias",
           "dn_o_norm_w", "fox_w_in", "fox_f_bias", "fox_q_norm_w", "fox_k_norm_w", "memq_norm_w", "w_out", "norm2_w",
           "w_mlp1", "w_mlp2"]
DN_IN = 4 * D_MODEL + 2 * N_HEADS + MEM_WIDTH
FOX_IN = 4 * D_MODEL + N_HEADS + MEM_WIDTH
GATE_END = 4 * D_MODEL
OUT_IN = D_MODEL + MEM_WIDTH
BIG = [("w_mem_kv", D_MODEL // N_DEV, D_MODEL), ("dn_w_in", D_MODEL, DN_IN // N_DEV), ("fox_w_in", D_MODEL, FOX_IN // N_DEV),
       ("dn_conv_w", CONV_WIDTH, 3 * D_MODEL // N_DEV), ("w_out", 2 * OUT_IN // N_DEV, D_MODEL),
       ("w_mlp1", 2 * D_MODEL, FF_PIECE), ("w_mlp2", 2 * FF_PIECE, D_MODEL)]
SMALL = [("mem_norm_w", (D_MODEL,), D_MODEL), ("mem_k_norm_w", (HEAD_DIM,), HEAD_DIM), ("norm1_w", (2, D_MODEL), 2 * D_MODEL),
         ("dn_a_log", (1, N_HEADS), LANES), ("dn_dt_bias", (1, N_HEADS), LANES), ("dn_o_norm_w", (1, HEAD_DIM), HEAD_DIM),
         ("fox_f_bias", (1, N_HEADS), LANES), ("fox_q_norm_w", (1, HEAD_DIM), HEAD_DIM), ("fox_k_norm_w", (1, HEAD_DIM), HEAD_DIM),
         ("memq_norm_w", (2, HEAD_DIM), 2 * HEAD_DIM), ("norm2_w", (2, D_MODEL), 2 * D_MODEL)]
SMALL_ROWS = 16


def pack_small(p):
    flat = jnp.concatenate([jnp.pad(p[n].reshape(-1), (0, ln - math.prod(sh))) for n, sh, ln in SMALL])
    return jnp.pad(flat, (0, SMALL_ROWS * PACK_W - flat.shape[0])).reshape(SMALL_ROWS, PACK_W)


def unpack_small(pk):
    flat, off, out = pk.reshape(-1), 0, {}
    for n, sh, ln in SMALL:
        out[n] = flat[off:off + math.prod(sh)].reshape(sh)
        off += ln
    return out


def in_proj_weights(gathered, width, n_small):
    full = gathered.transpose(1, 0, 2).reshape(D_MODEL, width)
    main = jnp.concatenate([full[:, :GATE_END], full[:, GATE_END + n_small:]], axis=1)
    return main, jnp.pad(full[:, GATE_END:GATE_END + n_small], ((0, 0), (0, LANES - n_small)))


def in_proj_pieces(d_main, d_small, n_small, width):
    full = jnp.concatenate([d_main[:, :GATE_END], d_small[:, :n_small], d_main[:, GATE_END:]], axis=1)
    return full.reshape(D_MODEL, N_DEV, width // N_DEV).transpose(1, 0, 2)


def adamw(parts, w, m, v, *, name, riders=()):
    n, rows, cols = parts.shape
    tile = _pick(rows, (512, 256, 128))

    def body(p_ref, w_ref, m_ref, v_ref, g_ref, d_ref, mo_ref, vo_ref):
        g = p_ref[0].astype(F32)
        for i in range(1, n):
            g = g + p_ref[i].astype(F32)
        m_new = ADAM_B1 * m_ref[...] + (1.0 - ADAM_B1) * g
        v_new = ADAM_B2 * v_ref[...] + (1.0 - ADAM_B2) * jnp.square(g)
        m_hat = m_new / (1.0 - ADAM_B1 ** ADAM_STEP)
        v_hat = v_new / (1.0 - ADAM_B2 ** ADAM_STEP)
        g_ref[...] = g
        d_ref[...] = -ADAM_LR * (m_hat / (jnp.sqrt(v_hat) + ADAM_EPS) + ADAM_WD * w_ref[...])
        mo_ref[...] = m_new
        vo_ref[...] = v_new

    spec = pl.BlockSpec((tile, cols), lambda i: (i, 0))
    return hosted_call(
        riders, body, name=name, grid=(rows // tile,),
        in_specs=[pl.BlockSpec((n, tile, cols), lambda i: (0, i, 0)), spec, spec, spec], out_specs=[spec] * 4,
        out_shape=[jax.ShapeDtypeStruct((rows, cols), F32)] * 4, compiler_params=_params(("parallel",)),
    )(parts, w, m, v)


def kernel(x, mem, mem_norm_w, w_mem_kv, mem_k_norm_w, norm1_w, dn_w_in, dn_conv_w, dn_a_log, dn_dt_bias, dn_o_norm_w, fox_w_in, fox_f_bias, fox_q_norm_w, fox_k_norm_w, memq_norm_w, w_out, norm2_w, w_mlp1, w_mlp2, loss_target, m_mem_norm_w, m_w_mem_kv, m_mem_k_norm_w, m_norm1_w, m_dn_w_in, m_dn_conv_w, m_dn_a_log, m_dn_dt_bias, m_dn_o_norm_w, m_fox_w_in, m_fox_f_bias, m_fox_q_norm_w, m_fox_k_norm_w, m_memq_norm_w, m_w_out, m_norm2_w, m_w_mlp1, m_w_mlp2, v_mem_norm_w, v_w_mem_kv, v_mem_k_norm_w, v_norm1_w, v_dn_w_in, v_dn_conv_w, v_dn_a_log, v_dn_dt_bias, v_dn_o_norm_w, v_fox_w_in, v_fox_f_bias, v_fox_q_norm_w, v_fox_k_norm_w, v_memq_norm_w, v_w_out, v_norm2_w, v_w_mlp1, v_w_mlp2):
    p = dict(mem_norm_w=mem_norm_w, w_mem_kv=w_mem_kv, mem_k_norm_w=mem_k_norm_w, norm1_w=norm1_w, dn_w_in=dn_w_in,
             dn_conv_w=dn_conv_w, dn_a_log=dn_a_log, dn_dt_bias=dn_dt_bias, dn_o_norm_w=dn_o_norm_w, fox_w_in=fox_w_in,
             fox_f_bias=fox_f_bias, fox_q_norm_w=fox_q_norm_w, fox_k_norm_w=fox_k_norm_w, memq_norm_w=memq_norm_w,
             w_out=w_out, norm2_w=norm2_w, w_mlp1=w_mlp1, w_mlp2=w_mlp2)
    pm = dict(mem_norm_w=m_mem_norm_w, w_mem_kv=m_w_mem_kv, mem_k_norm_w=m_mem_k_norm_w, norm1_w=m_norm1_w,
              dn_w_in=m_dn_w_in, dn_conv_w=m_dn_conv_w, dn_a_log=m_dn_a_log, dn_dt_bias=m_dn_dt_bias,
              dn_o_norm_w=m_dn_o_norm_w, fox_w_in=m_fox_w_in, fox_f_bias=m_fox_f_bias, fox_q_norm_w=m_fox_q_norm_w,
              fox_k_norm_w=m_fox_k_norm_w, memq_norm_w=m_memq_norm_w, w_out=m_w_out, norm2_w=m_norm2_w, w_mlp1=m_w_mlp1,
              w_mlp2=m_w_mlp2)
    pv = dict(mem_norm_w=v_mem_norm_w, w_mem_kv=v_w_mem_kv, mem_k_norm_w=v_mem_k_norm_w, norm1_w=v_norm1_w,
              dn_w_in=v_dn_w_in, dn_conv_w=v_dn_conv_w, dn_a_log=v_dn_a_log, dn_dt_bias=v_dn_dt_bias,
              dn_o_norm_w=v_dn_o_norm_w, fox_w_in=v_fox_w_in, fox_f_bias=v_fox_f_bias, fox_q_norm_w=v_fox_q_norm_w,
              fox_k_norm_w=v_fox_k_norm_w, memq_norm_w=v_memq_norm_w, w_out=v_w_out, norm2_w=v_norm2_w, w_mlp1=v_w_mlp1,
              w_mlp2=v_w_mlp2)

    loss, grad_x, results, small = local_step(x[0], mem[0], loss_target[0], p, pm, pv)
    loss = lax.psum(loss, ("x", "y", "c"))
    small = [unpack_small(o) for o in small]
    groups = [{**small[i], **{n: r[i] for n, r in results.items()}} for i in range(4)]
    return (loss, grad_x[None], *[grp[n] for grp in groups for n in WEIGHTS])
```

```python
import functools
import math

import jax
import jax.numpy as jnp
from jax import lax
from jax.experimental import pallas as pl
from jax.experimental.pallas import tpu as pltpu

F32 = jnp.float32
BF16 = jnp.bfloat16
HIGHEST = lax.Precision.HIGHEST

D_MODEL = 1024
HEAD_DIM = 128
N_HEADS = 8
MEM_HEADS = 4
MEM_WIDTH = MEM_HEADS * HEAD_DIM
D_FF = 4 * D_MODEL
CONV_WIDTH = 4
CHUNK = 64
Q_BLOCK = 128
EPS = 1e-6
SCALE = HEAD_DIM ** -0.5
MAIN_WIDTH = 4 * D_MODEL + MEM_WIDTH
LANES = 128
N_DEV = 8
PACK_W = 512

ADAM_LR = 0.001
ADAM_B1 = 0.9
ADAM_B2 = 0.999
ADAM_EPS = 1e-08
ADAM_WD = 0.01
ADAM_STEP = 10

VMEM_LIMIT = 56 * 2 ** 20
MESH = pl.DeviceIdType.MESH


def _bdot(a, b, dims):
    return lax.dot_general(a.astype(BF16), b.astype(BF16), (dims, ((), ())), preferred_element_type=F32)


@jax.custom_vjp
def mm(a, b):
    return _bdot(a, b, ((1,), (0,)))


@jax.custom_vjp
def mm_nt(a, b):
    return _bdot(a, b, ((1,), (1,)))


@jax.custom_vjp
def mm_tn(a, b):
    return _bdot(a, b, ((0,), (0,)))


mm.defvjp(lambda a, b: (mm(a, b), (a, b)), lambda r, g: (mm_nt(g, r[1]), mm_tn(r[0], g)))
mm_nt.defvjp(lambda a, b: (mm_nt(a, b), (a, b)), lambda r, g: (mm(g, r[1]), mm_tn(g, r[0])))
mm_tn.defvjp(lambda a, b: (mm_tn(a, b), (a, b)), lambda r, g: (mm_nt(r[1], g), mm(r[0], g)))


def hdot(a, b):
    return jnp.dot(a, b, precision=HIGHEST, preferred_element_type=F32)


def rms(x, w):
    return x * lax.rsqrt(jnp.mean(x * x, axis=-1, keepdims=True) + EPS) * w


def l2n(x):
    return x * lax.rsqrt(jnp.sum(x * x, axis=-1, keepdims=True) + EPS)


def _iota2(n, m):
    return lax.broadcasted_iota(jnp.int32, (n, m), 0), lax.broadcasted_iota(jnp.int32, (n, m), 1)


def _lower_ones(n):
    r, c = _iota2(n, n)
    return jnp.where(r >= c, 1.0, 0.0).astype(F32)


def _last_row(x):
    r = lax.broadcasted_iota(jnp.int32, x.shape, 0)
    return jnp.sum(jnp.where(r == x.shape[0] - 1, x, 0.0), axis=0, keepdims=True)


def _softmax_rows(z):
    m = lax.stop_gradient(jnp.max(z, axis=-1, keepdims=True))
    e = jnp.exp(z - m)
    return e * (1.0 / jnp.sum(e, axis=-1, keepdims=True))


_BNN = (((2,), (1,)), ((0,), (0,)))
_BNT = (((2,), (2,)), ((0,), (0,)))
_BTN = (((1,), (1,)), ((0,), (0,)))


def _bbdot(a, b, dims):
    return lax.dot_general(a.astype(BF16), b.astype(BF16), dims, preferred_element_type=F32)


@jax.custom_vjp
def bmm(a, b):
    return _bbdot(a, b, _BNN)


@jax.custom_vjp
def bmm_nt(a, b):
    return _bbdot(a, b, _BNT)


@jax.custom_vjp
def bmm_tn(a, b):
    return _bbdot(a, b, _BTN)


@jax.custom_vjp
def bmm_high(a, b):
    return lax.dot_general(a, b, _BNN, precision=lax.Precision.HIGH, preferred_element_type=F32)


bmm.defvjp(lambda a, b: (bmm(a, b), (a, b)), lambda r, g: (bmm_nt(g, r[1]), bmm_tn(r[0], g)))
bmm_nt.defvjp(lambda a, b: (bmm_nt(a, b), (a, b)), lambda r, g: (bmm(g, r[1]), bmm_tn(g, r[0])))
bmm_tn.defvjp(lambda a, b: (bmm_tn(a, b), (a, b)), lambda r, g: (bmm_nt(r[1], g), bmm(r[0], g)))
bmm_high.defvjp(lambda a, b: (bmm_high(a, b), (a, b)), lambda r, g: (bmm_nt(g, r[1]), bmm_tn(r[0], g)))

NEUMANN_HIGH_LEVELS = 2


def inv_unit_lower(a):
    n = a.shape[-1]
    r, c = _iota2(n, n)
    p = jnp.where(r == c, 1.0, 0.0).astype(F32) - a
    ak = a
    for level in range(int(math.log2(n)) - 1):
        dot = bmm_high if level < NEUMANN_HIGH_LEVELS else bmm
        ak = dot(ak, ak)
        p = p + dot(p, ak)
    return p


def delta_intra(q, k, v, gc, beta):
    b, c, _ = q.shape
    r, cc = _iota2(c, c)
    causal = r >= cc
    strict = r > cc
    gi = jnp.broadcast_to(gc, (b, c, c))
    gj = jnp.swapaxes(gi, 1, 2)
    decay = jnp.where(causal, jnp.exp(jnp.where(causal, gi - gj, 0.0)), 0.0)
    kb = k * beta
    a = jnp.where(strict, bmm_nt(kb, k) * decay, 0.0)
    t = inv_unit_lower(a)
    u = bmm(t, v * beta)
    w = bmm(t, kb * jnp.exp(gc))
    qk = jnp.where(causal, bmm_nt(q, k) * decay, 0.0)
    return u, w, qk


def delta_step(s, q, k, gc, u, w, qk):
    v_new = u - bmm(w, s)
    out = bmm(q * jnp.exp(gc), s) + bmm(qk, v_new)
    r = lax.broadcasted_iota(jnp.int32, gc.shape, 1)
    g_last = jnp.sum(jnp.where(r == gc.shape[1] - 1, gc, 0.0), axis=1, keepdims=True)
    k_dec = k * jnp.exp(g_last - gc)
    s_new = s * jnp.exp(g_last) + bmm_tn(k_dec, v_new)
    return out, s_new


def fox_probs(q, k, fq, fk, qpos0):
    s = lax.dot_general(q, k, (((1,), (1,)), ((), ())), preferred_element_type=F32)
    r, c = _iota2(s.shape[0], s.shape[1])
    return _softmax_rows(jnp.where(c <= (r + qpos0), s + (fq - fk), -jnp.inf))


def mem_head(qm, wq, mk, mv):
    p = _softmax_rows(mm_nt(rms(qm, wq) * SCALE, mk))
    return mm(p, mv)


def _heads(x, n):
    return [x[:, h * HEAD_DIM:(h + 1) * HEAD_DIM] for h in range(n)]


def memkv_fn(mem, mnw, wkv, mknw):
    kv = mm(rms(mem, mnw), wkv)
    mk = jnp.concatenate([rms(kh, mknw) for kh in _heads(kv[:, :MEM_WIDTH], MEM_HEADS)], axis=1)
    return mk, kv[:, MEM_WIDTH:]


def dn_gates_fn(ab, alog, dtb):
    g = -jnp.exp(alog) * jax.nn.softplus(ab + dtb)
    low = _lower_ones(CHUNK)
    gc = jnp.concatenate([hdot(low, g[i * CHUNK:(i + 1) * CHUNK]) for i in range(ab.shape[0] // CHUNK)], axis=0)
    lane = lax.broadcasted_iota(jnp.int32, ab.shape, 1)
    return jnp.where(lane < N_HEADS, gc, jax.nn.sigmoid(ab))


def fox_fcum_fn(fp, fbias):
    lf = jax.nn.log_sigmoid(fp + fbias)
    low = _lower_ones(LANES)
    carry = jnp.zeros((1, fp.shape[1]), F32)
    outs = []
    for i in range(fp.shape[0] // LANES):
        cs = hdot(low, lf[i * LANES:(i + 1) * LANES]) + carry
        carry = _last_row(cs)
        outs.append(cs)
    return jnp.concatenate(outs, axis=0)


def fox_qk_fn(qraw, kraw, qnw, knw):
    q = jnp.concatenate([rms(x, qnw) * SCALE for x in _heads(qraw, N_HEADS)], axis=1)
    k = jnp.concatenate([rms(x, knw) for x in _heads(kraw, N_HEADS)], axis=1)
    return q, k


def _mem_out(qm, mqw, mk, mv):
    return [mem_head(a, mqw, b, c) for a, b, c in zip(_heads(qm, MEM_HEADS), _heads(mk, MEM_HEADS), _heads(mv, MEM_HEADS))]


def dn_out_fn(o, z, qm, onw, mqw, mk, mv):
    mix = [rms(a, onw) * jax.nn.silu(b) for a, b in zip(o, _heads(z, N_HEADS))]
    return jnp.concatenate(mix + _mem_out(qm, mqw, mk, mv), axis=1)


def fox_out_fn(o, gate, qm, mqw, mk, mv):
    return jnp.concatenate([o * jax.nn.sigmoid(gate)] + _mem_out(qm, mqw, mk, mv), axis=1)


_HBM = pl.BlockSpec(memory_space=pltpu.HBM)


def _place():
    return lax.axis_index("x"), lax.axis_index("y"), lax.axis_index("c")


class Rider:
    def __init__(self, ins, out_shape, scratch, start, finish):
        self.ins, self.out_shape, self.scratch, self.start, self.finish = list(ins), list(out_shape), list(scratch), start, finish
        self.results = None


def gather_rider(xs):
    n = len(xs)

    def plan(x_refs, out_refs, sems):
        send_sems, recv_sems, local_sems = sems
        x, y, c = _place()
        me, sibling = (x, y, c), (x, y, 1 - c)
        chips = [(1 - x, y), (x, 1 - y), (1 - x, 1 - y)]

        def copy(a, k, block, to, src=None):
            px, py, pc = block
            dst = out_refs[a].at[4 * px + 2 * py + pc]
            return pltpu.make_async_remote_copy(
                src_ref=dst if src is None else src, dst_ref=dst,
                send_sem=send_sems.at[a, k], recv_sem=recv_sems.at[a, k], device_id=to, device_id_type=MESH)

        mine = [pltpu.make_async_copy(x_refs[a], out_refs[a].at[4 * x + 2 * y + c], local_sems.at[a]) for a in range(n)]
        first = [copy(a, 0, me, sibling, src=x_refs[a]) for a in range(n)]
        first += [copy(a, 1 + j, me, (*chip, c), src=x_refs[a]) for j, chip in enumerate(chips) for a in range(n)]
        return copy, me, sibling, chips, mine, first

    def start(x_refs, out_refs, sems):
        _, _, _, _, mine, first = plan(x_refs, out_refs, sems)
        for cp in mine + first:
            cp.start()

    def finish(x_refs, out_refs, sems):
        copy, me, sibling, chips, mine, first = plan(x_refs, out_refs, sems)
        _, _, c = me
        passed = []
        for j, chip in enumerate(chips):
            for a in range(n):
                copy(a, 1 + j, (*chip, c), me).wait_recv()
                passed.append(copy(a, 4 + j, (*chip, c), sibling))
                passed[-1].start()
        for a in range(n):
            copy(a, 0, sibling, me).wait_recv()
        for j, chip in enumerate(chips):
            for a in range(n):
                copy(a, 4 + j, (*chip, 1 - c), me).wait_recv()
        for cp in first + passed:
            cp.wait_send()
        for cp in mine:
            cp.wait()

    return Rider(xs, [jax.ShapeDtypeStruct((N_DEV,) + a.shape, a.dtype) for a in xs],
                 [pltpu.SemaphoreType.DMA((n, 7)), pltpu.SemaphoreType.DMA((n, 7)), pltpu.SemaphoreType.DMA((n,))], start, finish)


def sibling_rider(gs):
    n = len(gs)

    def plan(g_refs, out_refs, sems):
        send_sems, recv_sems = sems
        x, y, c = _place()
        return [pltpu.make_async_remote_copy(
            src_ref=g_refs[a].at[2 * k + 1 - c], dst_ref=out_refs[a].at[k], send_sem=send_sems.at[a, k],
            recv_sem=recv_sems.at[a, k], device_id=(x, y, 1 - c), device_id_type=MESH) for a in range(n) for k in range(4)]

    def start(g_refs, out_refs, sems):
        for cp in plan(g_refs, out_refs, sems):
            cp.start()

    def finish(g_refs, out_refs, sems):
        copies = plan(g_refs, out_refs, sems)
        for cp in copies:
            cp.wait_recv()
        for cp in copies:
            cp.wait_send()

    return Rider(gs, [jax.ShapeDtypeStruct((4,) + g.shape[1:], g.dtype) for g in gs],
                 [pltpu.SemaphoreType.DMA((n, 4)), pltpu.SemaphoreType.DMA((n, 4))], start, finish)


def chips_rider(hs):
    n = len(hs)

    def plan(h_refs, out_refs, sems):
        send_sems, recv_sems, local_sems = sems
        x, y, c = _place()
        mine = 2 * x + y
        chips = [(1 - x, y), (x, 1 - y), (1 - x, 1 - y)]
        keep = [pltpu.make_async_copy(h_refs[a].at[mine], out_refs[a].at[mine], local_sems.at[a]) for a in range(n)]
        sends = [pltpu.make_async_remote_copy(
            src_ref=h_refs[a].at[2 * qx + qy], dst_ref=out_refs[a].at[mine], send_sem=send_sems.at[a, j],
            recv_sem=recv_sems.at[a, j], device_id=(qx, qy, c), device_id_type=MESH)
            for j, (qx, qy) in enumerate(chips) for a in range(n)]
        recvs = [pltpu.make_async_remote_copy(
            src_ref=h_refs[a].at[mine], dst_ref=out_refs[a].at[2 * qx + qy], send_sem=send_sems.at[a, j],
            recv_sem=recv_sems.at[a, j], device_id=(qx, qy, c), device_id_type=MESH)
            for j, (qx, qy) in enumerate(chips) for a in range(n)]
        return keep, sends, recvs

    def start(h_refs, out_refs, sems):
        keep, sends, _ = plan(h_refs, out_refs, sems)
        for cp in keep + sends:
            cp.start()

    def finish(h_refs, out_refs, sems):
        keep, sends, recvs = plan(h_refs, out_refs, sems)
        for cp in recvs:
            cp.wait_recv()
        for cp in sends:
            cp.wait_send()
        for cp in keep:
            cp.wait()

    return Rider(hs, [jax.ShapeDtypeStruct(h.shape, h.dtype) for h in hs],
                 [pltpu.SemaphoreType.DMA((n, 3)), pltpu.SemaphoreType.DMA((n, 3)), pltpu.SemaphoreType.DMA((n,))], start, finish)


def hosted_call(riders, body, *, out_shape, in_specs, out_specs, grid=(), scratch_shapes=(), **kw):
    riders = tuple(riders or ())
    if not riders:
        return pl.pallas_call(body, out_shape=out_shape, in_specs=in_specs, out_specs=out_specs, grid=grid,
                              scratch_shapes=scratch_shapes, **kw)
    single = not isinstance(out_shape, (list, tuple))
    k_out_shape = [out_shape] if single else list(out_shape)
    k_out_specs = [out_specs] if single else list(out_specs)
    n_in, n_out, n_scr = len(in_specs), len(k_out_shape), len(scratch_shapes)
    r_ins = [a for r in riders for a in r.ins]
    r_outs = [s for r in riders for s in r.out_shape]
    r_scr = [s for r in riders for s in r.scratch]

    def full_body(*refs):
        ins = refs[:n_in + len(r_ins)]
        outs = refs[n_in + len(r_ins):n_in + len(r_ins) + n_out + len(r_outs)]
        scr = refs[n_in + len(r_ins) + n_out + len(r_outs):]
        ids = [pl.program_id(d) for d in range(len(grid))]
        first = functools.reduce(jnp.logical_and, [i == 0 for i in ids]) if ids else None
        last = functools.reduce(jnp.logical_and, [i == g - 1 for i, g in zip(ids, grid)]) if ids else None

        def each(method):
            i0, o0, s0 = n_in, n_out, n_scr
            for r in riders:
                getattr(r, method)(ins[i0:i0 + len(r.ins)], outs[o0:o0 + len(r.out_shape)], scr[s0:s0 + len(r.scratch)])
                i0, o0, s0 = i0 + len(r.ins), o0 + len(r.out_shape), s0 + len(r.scratch)

        if first is None:
            each("start")
        else:
            pl.when(first)(lambda: each("start"))
        body(*ins[:n_in], *outs[:n_out], *scr[:n_scr])
        if last is None:
            each("finish")
        else:
            pl.when(last)(lambda: each("finish"))

    call = pl.pallas_call(
        full_body, out_shape=k_out_shape + r_outs, in_specs=list(in_specs) + [_HBM] * len(r_ins),
        out_specs=k_out_specs + [_HBM] * len(r_outs), grid=grid, scratch_shapes=list(scratch_shapes) + r_scr, **kw)

    def run(*args):
        res = call(*args, *r_ins)
        o0 = n_out
        for r in riders:
            r.results = list(res[o0:o0 + len(r.out_shape)])
            o0 += len(r.out_shape)
        return res[0] if single else list(res[:n_out])

    return run


def run_riders(riders, *, name):
    hosted_call(riders, lambda: None, name=name, out_shape=[], in_specs=[], out_specs=[])()
    return [r.results for r in riders]


def _pick(n, cands):
    for c in cands:
        if n % c == 0:
            return c
    return n


def _params(sem):
    return pltpu.CompilerParams(dimension_semantics=sem, vmem_limit_bytes=VMEM_LIMIT)


MATMUL_VMEM_BUDGET = 40 * 2 ** 20


def _matmul_tiles(m, n, k, bytes_a, bytes_b, bytes_mn, fixed):
    tm, tn, tk = fixed if fixed is not None else (None, None, None)
    tm = tm or _pick(m, (1024, 512, 256, 128))
    tn = tn or _pick(n, (512, 256, 128))
    if tk is None:
        for tk in [c for c in (2048, 1536, 1024, 512, 256, 128) if k % c == 0] + [k]:
            if 2 * (tm * tk * bytes_a + tk * tn * bytes_b + tm * tn * bytes_mn) + tm * tn * 4 <= MATMUL_VMEM_BUDGET:
                break
    return tm, tn, tk


def matmul(a, b, *, name, ta=False, tb=False, post=None, post_ins=(), extra_out=None, out_dtype=F32, tiles=None,
           b_view=None, out_view=None, riders=()):
    (k, m) = a.shape if ta else a.shape[::-1]
    (kb, n) = b_view[:2] if b_view is not None else (b.shape[::-1] if tb else b.shape)
    assert k == kb, (a.shape, b.shape, ta, tb)
    bytes_mn = sum(p.dtype.itemsize for p in post_ins) + jnp.dtype(out_dtype).itemsize
    bytes_mn += jnp.dtype(extra_out[1]).itemsize if extra_out else 0
    tm, tn, tk = _matmul_tiles(m, n, k, a.dtype.itemsize, b.dtype.itemsize, bytes_mn, tiles)
    nk = k // tk
    dims = ((0,) if ta else (1,), (1,) if tb else (0,))
    n_post = len(post_ins)
    n_out = 2 if extra_out else 1

    def body(*refs):
        a_ref, b_ref = refs[:2]
        post_refs = refs[2:2 + n_post]
        o_refs, acc = refs[-1 - n_out:-1], refs[-1]
        kk = pl.program_id(2)

        @pl.when(kk == 0)
        def _():
            acc[...] = jnp.zeros_like(acc)

        b_tile = b_ref[...]
        acc[...] += _bdot(a_ref[...], b_tile.reshape(-1, b_tile.shape[-1]), dims)

        @pl.when(kk == nk - 1)
        def _():
            r = acc[...]
            if post is not None:
                r = post(r, *[p[...] for p in post_refs])
            o_refs[0][...] = r.astype(out_dtype)
            if extra_out:
                o_refs[1][...] = extra_out[0](r).astype(extra_out[1])

    a_spec = pl.BlockSpec((tk, tm), lambda i, j, kk: (kk, i)) if ta else pl.BlockSpec((tm, tk), lambda i, j, kk: (i, kk))
    if b_view is not None:
        b_spec = b_view[2]
    else:
        b_spec = pl.BlockSpec((tn, tk), lambda i, j, kk: (j, kk)) if tb else pl.BlockSpec((tk, tn), lambda i, j, kk: (kk, j))
    mn_spec = pl.BlockSpec((tm, tn), lambda i, j, kk: (i, j))
    o_shape, o_spec, into = ((m, n), mn_spec, None) if out_view is None else out_view
    ins, specs, aliases = [a, b, *post_ins], [a_spec, b_spec] + [mn_spec] * n_post, {}
    if into is not None:
        aliases = {len(ins): 0}
        ins.append(into)
        specs.append(pl.BlockSpec(memory_space=pl.ANY))
    out_shape = [jax.ShapeDtypeStruct(o_shape, out_dtype)]
    out_specs = [o_spec]
    if extra_out:
        out_shape.append(jax.ShapeDtypeStruct((m, n), extra_out[1]))
        out_specs.append(mn_spec)
    res = hosted_call(
        riders, body, name=name, grid=(m // tm, n // tn, nk), in_specs=specs, out_specs=out_specs, out_shape=out_shape,
        input_output_aliases=aliases, scratch_shapes=[pltpu.VMEM((tm, tn), F32)],
        compiler_params=_params(("parallel", "parallel", "arbitrary")),
    )(*ins)
    return res if extra_out else res[0]


def rows_call(fn, row_ins, full_ins, row_outs, acc_outs, *, tm, name, riders=()):
    row_ins = [r if isinstance(r, tuple) else (r, r.shape[-1], 0) for r in row_ins]
    t = row_ins[0][0].shape[-2]
    tm = min(tm, t)
    n_in = len(row_ins) + len(full_ins)
    n_row = len(row_outs)

    def body(*refs):
        res = fn(*[[r[h] for h in range(r.shape[0])] if (i < len(row_ins) and len(r.shape) == 3) else r[...]
                   for i, r in enumerate(refs[:n_in])])
        res = res if isinstance(res, (tuple, list)) else (res,)
        outs = refs[n_in:]
        for ref, val in zip(outs[:n_row], res[:n_row]):
            if len(ref.shape) == 3:
                for h, vh in enumerate(val):
                    ref[h] = vh.astype(ref.dtype)
            else:
                ref[...] = val.astype(ref.dtype)
        first = pl.program_id(0) == 0
        for ref, val in zip(outs[n_row:], res[n_row:]):
            @pl.when(first)
            def _(ref=ref, val=val):
                ref[...] = val

            @pl.when(jnp.logical_not(first))
            def _(ref=ref, val=val):
                ref[...] += val

    def full_spec(shape):
        return pl.BlockSpec(shape, lambda i, nd=len(shape): (0,) * nd)

    def row_spec(lead, w, cb):
        if lead is None:
            return pl.BlockSpec((tm, w), lambda i: (i, cb))
        return pl.BlockSpec((lead, tm, w), lambda i: (0, i, cb))

    def lead_cols(c):
        return c if isinstance(c, tuple) else (None, c)

    in_specs = [row_spec(a.shape[0] if a.ndim == 3 else None, w, cb) for (a, w, cb) in row_ins]
    in_specs += [full_spec(f.shape) for f in full_ins]
    out_specs = [row_spec(*lead_cols(c), 0) for c, _ in row_outs] + [full_spec(s) for s in acc_outs]
    out_shape = [jax.ShapeDtypeStruct(tuple(d for d in (lead_cols(c)[0], t, lead_cols(c)[1]) if d is not None), dt)
                 for c, dt in row_outs] + [jax.ShapeDtypeStruct(s, F32) for s in acc_outs]
    res = hosted_call(
        riders, body, name=name, grid=(t // tm,), in_specs=in_specs, out_specs=out_specs, out_shape=out_shape,
        compiler_params=_params(("arbitrary",)),
    )(*[r[0] for r in row_ins], *full_ins)
    return res


def vjp_rows(fn, n_diff_row, row_diff_full):
    def bwd(*args, n_row, n_ct):
        prim_rows = args[:n_row]
        cts = args[n_row:n_row + n_ct]
        fulls = args[n_row + n_ct:]
        _, vjp = jax.vjp(fn, *prim_rows, *fulls)
        g = vjp(cts[0] if n_ct == 1 else tuple(cts))
        out = list(g[:n_diff_row])
        out += [gf for gf, d in zip(g[n_row:], row_diff_full) if d]
        return tuple(out)
    return bwd


def _shift_down(x, s):
    if s == 0:
        return x
    t = lax.broadcasted_iota(jnp.int32, x.shape, 0)
    return jnp.where(t >= s, pltpu.roll(x, s, 0), 0.0)


def _shift_up(x, s):
    if s == 0:
        return x
    n = x.shape[0]
    t = lax.broadcasted_iota(jnp.int32, x.shape, 0)
    return jnp.where(t < n - s, pltpu.roll(x, n - s, 0), 0.0)


def _conv(x, w_ref):
    return sum(w_ref[pl.ds(j, 1), :] * _shift_down(x, CONV_WIDTH - 1 - j) for j in range(CONV_WIDTH))


_DN_POST = (lambda c: l2n(jax.nn.silu(c)) * SCALE, lambda c: l2n(jax.nn.silu(c)), jax.nn.silu)


def dn_prep_fwd(proj, conv_w, *, name, riders=()):
    t = proj.shape[0]

    def body(xq, xk, xv, wq, wk, wv, oq, ok, ov):
        for x_ref, w_ref, o_ref, post in zip((xq, xk, xv), (wq, wk, wv), (oq, ok, ov), _DN_POST):
            o_ref[...] = post(_conv(x_ref[...], w_ref))

    x_specs = [pl.BlockSpec((t, HEAD_DIM), lambda h, g=g: (0, g * N_HEADS + h)) for g in range(3)]
    w_specs = [pl.BlockSpec((CONV_WIDTH, HEAD_DIM), lambda h, g=g: (0, g * N_HEADS + h)) for g in range(3)]
    o_spec = pl.BlockSpec((None, t, HEAD_DIM), lambda h: (h, 0, 0))
    return hosted_call(
        riders, body, name=name, grid=(N_HEADS,), in_specs=x_specs + w_specs, out_specs=[o_spec] * 3,
        out_shape=[jax.ShapeDtypeStruct((N_HEADS, t, HEAD_DIM), F32)] * 3, compiler_params=_params(("parallel",)),
    )(proj, proj, proj, conv_w, conv_w, conv_w)


def dn_prep_bwd(proj, conv_w, dq, dk, dv, *, name, riders=()):
    t = proj.shape[0]

    def body(xq, xk, xv, wq, wk, wv, gq, gk, gv, dxq, dxk, dxv, dwq, dwk, dwv):
        for x_ref, w_ref, g_ref, dx_ref, dw_ref, post in zip(
                (xq, xk, xv), (wq, wk, wv), (gq, gk, gv), (dxq, dxk, dxv), (dwq, dwk, dwv), _DN_POST):
            x = x_ref[...]
            _, vjp = jax.vjp(post, _conv(x, w_ref))
            dc, = vjp(g_ref[...])
            dx = sum(w_ref[pl.ds(j, 1), :] * _shift_up(dc, CONV_WIDTH - 1 - j) for j in range(CONV_WIDTH))
            dx_ref[...] = dx.astype(dx_ref.dtype)
            for j in range(CONV_WIDTH):
                dw_ref[pl.ds(j, 1), :] = jnp.sum(dc * _shift_down(x, CONV_WIDTH - 1 - j), axis=0, keepdims=True)

    x_specs = [pl.BlockSpec((t, HEAD_DIM), lambda h, g=g: (0, g * N_HEADS + h)) for g in range(3)]
    w_specs = [pl.BlockSpec((CONV_WIDTH, HEAD_DIM), lambda h, g=g: (0, g * N_HEADS + h)) for g in range(3)]
    g_spec = pl.BlockSpec((None, t, HEAD_DIM), lambda h: (h, 0, 0))
    dx_spec = pl.BlockSpec((t, HEAD_DIM), lambda h: (0, h))
    dw_spec = pl.BlockSpec((CONV_WIDTH, HEAD_DIM), lambda h: (0, h))
    return hosted_call(
        riders, body, name=name, grid=(N_HEADS,), in_specs=x_specs + w_specs + [g_spec] * 3, out_specs=[dx_spec] * 3 + [dw_spec] * 3,
        out_shape=[jax.ShapeDtypeStruct((t, D_MODEL), BF16)] * 3 + [jax.ShapeDtypeStruct((CONV_WIDTH, D_MODEL), F32)] * 3,
        compiler_params=_params(("parallel",)),
    )(proj, proj, proj, conv_w, conv_w, conv_w, dq, dk, dv)


INTRA_CHUNKS = 4


def _lane_column(x, lane_index):
    lane = lax.broadcasted_iota(jnp.int32, x.shape, 1)
    return jnp.sum(jnp.where(lane == lane_index, x, 0.0), axis=1, keepdims=True)


def _head_columns(g, first_lane):
    return jnp.concatenate([_lane_column(g, first_lane + h)[None] for h in range(N_HEADS)], axis=0)


def _intra_of_gates(q, k, v, gates):
    nb = N_HEADS * (gates.shape[0] // CHUNK)

    def chunks(x):
        return x.reshape(nb, CHUNK, x.shape[-1])

    res = delta_intra(chunks(q), chunks(k), chunks(v), chunks(_head_columns(gates, 0)), chunks(_head_columns(gates, N_HEADS)))
    return tuple(x.reshape(N_HEADS, -1, x.shape[-1]) for x in res)


def _step_of_gates(s, q, k, gates, u, w, qk):
    return delta_step(s, q, k, _head_columns(gates, 0), u, w, qk)


def _head_major(rows, w, index):
    return pl.BlockSpec((N_HEADS, rows, w), lambda i: (0, index(i), 0))


def delta_intra_fwd(q, k, v, gates, *, name, riders=()):
    t = q.shape[1]
    rows = min(INTRA_CHUNKS, t // CHUNK) * CHUNK

    def body(q_ref, k_ref, v_ref, g_ref, u_ref, w_ref, qk_ref):
        for ref, val in zip((u_ref, w_ref, qk_ref), _intra_of_gates(q_ref[...], k_ref[...], v_ref[...], g_ref[...])):
            ref[...] = val

    x_spec, qk_spec = (_head_major(rows, w, lambda i: i) for w in (HEAD_DIM, CHUNK))
    g_spec = pl.BlockSpec((rows, LANES), lambda i: (i, 0))
    return hosted_call(
        riders, body, name=name, grid=(t // rows,), in_specs=[x_spec] * 3 + [g_spec], out_specs=[x_spec, x_spec, qk_spec],
        out_shape=[jax.ShapeDtypeStruct((N_HEADS, t, HEAD_DIM), F32)] * 2 + [jax.ShapeDtypeStruct((N_HEADS, t, CHUNK), F32)],
        compiler_params=_params(("parallel",)),
    )(q, k, v, gates)


def delta_seq_fwd(q, k, gates, u, w, qk, *, name, riders=()):
    t = q.shape[1]
    nc = t // CHUNK

    def body(q_ref, k_ref, g_ref, u_ref, w_ref, qk_ref, o_ref, s0_ref, s_ref):
        @pl.when(pl.program_id(0) == 0)
        def _():
            s_ref[...] = jnp.zeros_like(s_ref)

        s = s_ref[...]
        s0_ref[...] = s
        o, s_new = _step_of_gates(s, q_ref[...], k_ref[...], g_ref[...], u_ref[...], w_ref[...], qk_ref[...])
        o_ref[...] = o
        s_ref[...] = s_new

    x_spec, qk_spec = (_head_major(CHUNK, w, lambda c: c) for w in (HEAD_DIM, CHUNK))
    g_spec = pl.BlockSpec((CHUNK, LANES), lambda c: (c, 0))
    s_spec = pl.BlockSpec((N_HEADS, None, HEAD_DIM, HEAD_DIM), lambda c: (0, c, 0, 0))
    return hosted_call(
        riders, body, name=name, grid=(nc,), in_specs=[x_spec, x_spec, g_spec, x_spec, x_spec, qk_spec], out_specs=[x_spec, s_spec],
        out_shape=[jax.ShapeDtypeStruct((N_HEADS, t, HEAD_DIM), F32),
                   jax.ShapeDtypeStruct((N_HEADS, nc, HEAD_DIM, HEAD_DIM), F32)],
        scratch_shapes=[pltpu.VMEM((N_HEADS, HEAD_DIM, HEAD_DIM), F32)],
        compiler_params=_params(("arbitrary",)),
    )(q, k, gates, u, w, qk)


def delta_seq_bwd(q, k, gates, u, w, qk, s0, do, *, name, riders=()):
    t = q.shape[1]
    nc = t // CHUNK

    def body(q_ref, k_ref, g_ref, u_ref, w_ref, qk_ref, s0_ref, do_ref,
             dq_ref, dk_ref, dg_ref, du_ref, dw_ref, dqk_ref, ds_ref):
        @pl.when(pl.program_id(0) == 0)
        def _():
            ds_ref[...] = jnp.zeros_like(ds_ref)

        _, vjp = jax.vjp(_step_of_gates, s0_ref[...], q_ref[...], k_ref[...], g_ref[...], u_ref[...], w_ref[...], qk_ref[...])
        ds, dq, dk, dg, du, dw, dqk = vjp((do_ref[...], ds_ref[...]))
        for ref, val in zip((ds_ref, dq_ref, dk_ref, dg_ref, du_ref, dw_ref, dqk_ref), (ds, dq, dk, dg, du, dw, dqk)):
            ref[...] = val

    x_spec, qk_spec = (_head_major(CHUNK, w, lambda c: nc - 1 - c) for w in (HEAD_DIM, CHUNK))
    g_spec = pl.BlockSpec((CHUNK, LANES), lambda c: (nc - 1 - c, 0))
    s_spec = pl.BlockSpec((N_HEADS, None, HEAD_DIM, HEAD_DIM), lambda c: (0, nc - 1 - c, 0, 0))
    head_shape = [jax.ShapeDtypeStruct((N_HEADS, t, w_), F32) for w_ in (HEAD_DIM, HEAD_DIM, HEAD_DIM, HEAD_DIM, CHUNK)]
    return hosted_call(
        riders, body, name=name, grid=(nc,), in_specs=[x_spec, x_spec, g_spec, x_spec, x_spec, qk_spec, s_spec, x_spec],
        out_specs=[x_spec, x_spec, g_spec, x_spec, x_spec, qk_spec],
        out_shape=head_shape[:2] + [jax.ShapeDtypeStruct((t, LANES), F32)] + head_shape[2:],
        scratch_shapes=[pltpu.VMEM((N_HEADS, HEAD_DIM, HEAD_DIM), F32)],
        compiler_params=_params(("arbitrary",)),
    )(q, k, gates, u, w, qk, s0, do)


def delta_intra_bwd(q, k, v, gates, du, dw, dqk, dq_s, dk_s, dg_s, *, name, riders=()):
    t = q.shape[1]
    rows = min(INTRA_CHUNKS, t // CHUNK) * CHUNK

    def body(q_ref, k_ref, v_ref, g_ref, du_ref, dw_ref, dqk_ref, dqs_ref, dks_ref, dgs_ref, dq_ref, dk_ref, dv_ref, dg_ref):
        _, vjp = jax.vjp(_intra_of_gates, q_ref[...], k_ref[...], v_ref[...], g_ref[...])
        dq, dk, dv, dg = vjp((du_ref[...], dw_ref[...], dqk_ref[...]))
        dq_ref[...] = dq + dqs_ref[...]
        dk_ref[...] = dk + dks_ref[...]
        dv_ref[...] = dv
        dg_ref[...] = dg + dgs_ref[...]

    x_spec, qk_spec = (_head_major(rows, w, lambda i: i) for w in (HEAD_DIM, CHUNK))
    g_spec = pl.BlockSpec((rows, LANES), lambda i: (i, 0))
    return hosted_call(
        riders, body, name=name, grid=(t // rows,),
        in_specs=[x_spec] * 3 + [g_spec, x_spec, x_spec, qk_spec, x_spec, x_spec, g_spec],
        out_specs=[x_spec] * 3 + [g_spec],
        out_shape=[jax.ShapeDtypeStruct((N_HEADS, t, HEAD_DIM), F32)] * 3 + [jax.ShapeDtypeStruct((t, LANES), F32)],
        compiler_params=_params(("parallel",)),
    )(q, k, v, gates, du, dw, dqk, dq_s, dk_s, dg_s)


_V_BLOCK = 2 * N_HEADS
FOX_GROUPS = 8


def _fox_groups(t):
    nq = t // Q_BLOCK
    per = max(1, nq // FOX_GROUPS)
    return [(g0, per, (g0 + per) * Q_BLOCK) for g0 in range(0, nq, per)]


def fox_attn_fwd(q, k, proj, fq, fk, *, name, riders=()):
    t = q.shape[0]

    def body(q_ref, k_ref, v_ref, fq_ref, fk_ref, o_ref, kb_ref, vb_ref):
        head = pl.program_id(0)
        kb_ref[...] = k_ref[...].astype(BF16)
        vb_ref[...] = v_ref[...].astype(BF16)
        for g0, per, keys in _fox_groups(t):
            def block(j, carry, g0=g0, keys=keys):
                rows = pl.ds(pl.multiple_of((g0 + j) * Q_BLOCK, Q_BLOCK), Q_BLOCK)
                p = fox_probs(q_ref[rows, :].astype(BF16), kb_ref[0:keys, :], _lane_column(fq_ref[rows, :], head),
                              fk_ref[:, 0:keys], (g0 + j) * Q_BLOCK)
                o_ref[rows, :] = jnp.dot(p.astype(BF16), vb_ref[0:keys, :], preferred_element_type=F32)
                return carry
            lax.fori_loop(0, per, block, 0)

    x_spec = pl.BlockSpec((t, HEAD_DIM), lambda h: (0, h))
    v_spec = pl.BlockSpec((t, HEAD_DIM), lambda h: (0, _V_BLOCK + h))
    fq_spec = pl.BlockSpec((t, LANES), lambda h: (0, 0))
    fk_spec = pl.BlockSpec((None, 1, t), lambda h: (h, 0, 0))
    return hosted_call(
        riders, body, name=name, grid=(N_HEADS,), in_specs=[x_spec, x_spec, v_spec, fq_spec, fk_spec], out_specs=x_spec,
        out_shape=jax.ShapeDtypeStruct((t, D_MODEL), F32), scratch_shapes=[pltpu.VMEM((t, HEAD_DIM), BF16)] * 2,
        compiler_params=_params(("parallel",)),
    )(q, k, proj, fq, fk)


def fox_attn_bwd(q, k, proj, fq, fk, do, *, name, riders=()):
    t = q.shape[0]

    def body(q_ref, k_ref, v_ref, fq_ref, fk_ref, do_ref, dq_ref, dk_ref, dv_out_ref, dfq_ref, dfk_ref, kb_ref, vb_ref, dv_ref):
        head = pl.program_id(0)

        @pl.when(head == 0)
        def _():
            dfq_ref[...] = jnp.zeros_like(dfq_ref)

        kb_ref[...] = k_ref[...].astype(BF16)
        vb_ref[...] = v_ref[...].astype(BF16)
        dk_ref[...] = jnp.zeros_like(dk_ref)
        dv_ref[...] = jnp.zeros_like(dv_ref)
        dfk_ref[...] = jnp.zeros_like(dfk_ref)
        nt = (((1,), (1,)), ((), ()))
        tn = (((0,), (0,)), ((), ()))
        for g0, per, keys in _fox_groups(t):
            def block(j, carry, g0=g0, keys=keys):
                rows = pl.ds(pl.multiple_of((g0 + j) * Q_BLOCK, Q_BLOCK), Q_BLOCK)
                qb, dob = q_ref[rows, :].astype(BF16), do_ref[rows, :].astype(BF16)
                kb, vb = kb_ref[0:keys, :], vb_ref[0:keys, :]
                p = fox_probs(qb, kb, _lane_column(fq_ref[rows, :], head), fk_ref[:, 0:keys], (g0 + j) * Q_BLOCK)
                dp = lax.dot_general(dob, vb, nt, preferred_element_type=F32)
                dz = p * (dp - jnp.sum(dp * p, axis=-1, keepdims=True))
                pb, dzb = p.astype(BF16), dz.astype(BF16)
                dq_ref[rows, :] = jnp.dot(dzb, kb, preferred_element_type=F32)
                lane = lax.broadcasted_iota(jnp.int32, (Q_BLOCK, LANES), 1)
                dfq_ref[rows, :] += jnp.where(lane == head, jnp.sum(dz, axis=-1, keepdims=True), 0.0)
                dk_ref[0:keys, :] += lax.dot_general(dzb, qb, tn, preferred_element_type=F32)
                dv_ref[0:keys, :] += lax.dot_general(pb, dob, tn, preferred_element_type=F32)
                dfk_ref[:, 0:keys] -= jnp.sum(dz, axis=0, keepdims=True)
                return carry
            lax.fori_loop(0, per, block, 0)
        dv_out_ref[...] = dv_ref[...].astype(dv_out_ref.dtype)

    x_spec = pl.BlockSpec((t, HEAD_DIM), lambda h: (0, h))
    v_spec = pl.BlockSpec((t, HEAD_DIM), lambda h: (0, _V_BLOCK + h))
    fq_spec = pl.BlockSpec((t, LANES), lambda h: (0, 0))
    fk_spec = pl.BlockSpec((None, 1, t), lambda h: (h, 0, 0))
    return hosted_call(
        riders, body, name=name, grid=(N_HEADS,), in_specs=[x_spec, x_spec, v_spec, fq_spec, fk_spec, x_spec],
        out_specs=[x_spec, x_spec, x_spec, fq_spec, fk_spec],
        out_shape=[jax.ShapeDtypeStruct((t, D_MODEL), F32)] * 2 + [jax.ShapeDtypeStruct((t, D_MODEL), BF16)]
        + [jax.ShapeDtypeStruct((t, LANES), F32), jax.ShapeDtypeStruct((N_HEADS, 1, t), F32)],
        scratch_shapes=[pltpu.VMEM((t, HEAD_DIM), BF16)] * 2 + [pltpu.VMEM((t, HEAD_DIM), F32)],
        compiler_params=_params(("arbitrary",)),
    )(q, k, proj, fq, fk, do)


def memkv_fwd(mem, mnw, wkv, mknw, *, name):
    n = mem.shape[0]

    def body(mem_ref, mnw_ref, w_ref, mknw_ref, mk_ref, mv_ref):
        mk, mv = memkv_fn(mem_ref[...], mnw_ref[...], w_ref[...], mknw_ref[...])
        mk_ref[...] = mk
        mv_ref[...] = mv

    return pl.pallas_call(
        body, name=name, out_shape=[jax.ShapeDtypeStruct((n, MEM_WIDTH), F32)] * 2,
        compiler_params=pltpu.CompilerParams(vmem_limit_bytes=VMEM_LIMIT),
    )(mem, mnw, wkv, mknw)


def memkv_bwd(mem, mnw, wkv, mknw, dmk, dmv, *, name):
    def body(mem_ref, mnw_ref, w_ref, mknw_ref, dmk_ref, dmv_ref, dmnw_ref, dw_ref, dmknw_ref):
        f = functools.partial(memkv_fn, mem_ref[...])
        _, vjp = jax.vjp(f, mnw_ref[...], w_ref[...].astype(F32), mknw_ref[...])
        dmnw, dw, dmknw = vjp((dmk_ref[...], dmv_ref[...]))
        dmnw_ref[...] = dmnw
        dw_ref[...] = dw.astype(dw_ref.dtype)
        dmknw_ref[...] = dmknw

    return pl.pallas_call(
        body, name=name,
        out_shape=[jax.ShapeDtypeStruct(mnw.shape, F32), jax.ShapeDtypeStruct(wkv.shape, BF16), jax.ShapeDtypeStruct(mknw.shape, F32)],
        compiler_params=pltpu.CompilerParams(vmem_limit_bytes=VMEM_LIMIT),
    )(mem, mnw, wkv, mknw, dmk, dmv)


def _row(v, width=None):
    v = v.reshape(1, -1)
    if width is not None and v.shape[1] < width:
        v = jnp.pad(v, ((0, 0), (0, width - v.shape[1])))
    return v


def _norm_fwd(x, w, name, riders=()):
    return rows_call(lambda x, w: rms(x, w), [x], [w], [(D_MODEL, BF16)], [], tm=512, name=name, riders=riders)[0]


def _norm_bwd(x, w, dh, dx_in, name, riders=()):
    def fn(x, dh, dx_in, w):
        _, vjp = jax.vjp(rms, x, w)
        dx, dw = vjp(dh)
        return dx + dx_in, dw
    return rows_call(fn, [x, dh, dx_in], [w], [(D_MODEL, F32)], [(1, D_MODEL)], tm=512, name=name, riders=riders)


FF_PIECE = D_FF // N_DEV


def _add(r, x):
    return r + x


def _piece(rows, cols, index):
    return pl.BlockSpec((None, rows, cols), lambda i, j, kk: (index(i, j, kk), 0, 0))


def _two_pieces(rows, cols, index):
    return pl.BlockSpec((2, rows, cols), lambda i, j, kk: (index(i, j, kk), 0, 0))


def _mlp_fwd(x, n2w, w1, w2, layer, riders=()):
    riders = list(riders) + [None, None]
    h2 = _norm_fwd(x, n2w, f"norm2_fwd_{layer}")
    u, a1 = matmul(h2, w1, name=f"mlp1_fwd_{layer}", tiles=(None, FF_PIECE, D_MODEL),
                   extra_out=(lambda u: jnp.square(jnp.maximum(u, 0.0)), BF16),
                   b_view=(D_MODEL, D_FF, _piece(D_MODEL, FF_PIECE, lambda i, j, kk: j)), riders=riders[0])
    y = matmul(a1, w2, name=f"mlp2_fwd_{layer}", post=_add, post_ins=[x], tiles=(None, D_MODEL, 2 * FF_PIECE),
               b_view=(D_FF, D_MODEL, _two_pieces(FF_PIECE, D_MODEL, lambda i, j, kk: kk)), riders=riders[1])
    return y, (x, h2, u, a1)


def pair_sum(g, got, *, name):
    _, rows, cols = g.shape
    tile = _pick(rows, (512, 256, 128))
    c = lax.axis_index("c").astype(jnp.int32).reshape(1)

    def body(c_ref, a_ref, b_ref, o_ref):
        o_ref[...] = (a_ref[...].astype(F32) + b_ref[...].astype(F32)).astype(o_ref.dtype)

    grid_spec = pltpu.PrefetchScalarGridSpec(
        num_scalar_prefetch=1, grid=(4, rows // tile),
        in_specs=[pl.BlockSpec((None, tile, cols), lambda k, i, c_ref: (2 * k + c_ref[0], i, 0)),
                  pl.BlockSpec((None, tile, cols), lambda k, i, c_ref: (k, i, 0))],
        out_specs=pl.BlockSpec((None, tile, cols), lambda k, i, c_ref: (k, i, 0)))
    return pl.pallas_call(
        body, name=name, grid_spec=grid_spec, out_shape=jax.ShapeDtypeStruct((4, rows, cols), g.dtype),
        compiler_params=_params(("parallel", "parallel")),
    )(c, g, got)


def chip_sums(names, pieces, gots):
    return [pair_sum(a, got, name=f"grads_pair_sum_{n}") for n, a, got in zip(names, pieces, gots)]


def _mlp_bwd(dy, res, n2w, w1, w2, layer, riders=()):
    x, h2, u, a1 = res
    du = matmul(dy, w2, tb=True, name=f"mlp2_dx_{layer}", out_dtype=BF16, tiles=(None, 2 * FF_PIECE, D_MODEL),
                post=lambda r, u: r * (2.0 * jnp.maximum(u, 0.0)), post_ins=[u],
                b_view=(D_MODEL, D_FF, _two_pieces(FF_PIECE, D_MODEL, lambda i, j, kk: j)), riders=riders)
    dw2 = matmul(a1, dy, ta=True, name=f"mlp2_dw_{layer}", out_dtype=BF16, tiles=(FF_PIECE, D_MODEL, None), out_view=(
        w2.shape, _piece(FF_PIECE, D_MODEL, lambda i, j, kk: i), None))
    sib2 = sibling_rider([dw2])
    dh2 = matmul(du, w1, tb=True, name=f"mlp1_dx_{layer}", tiles=(None, D_MODEL, FF_PIECE),
                 b_view=(D_FF, D_MODEL, _piece(D_MODEL, FF_PIECE, lambda i, j, kk: kk)), riders=[sib2])
    dw1 = matmul(h2, du, ta=True, name=f"mlp1_dw_{layer}", out_dtype=BF16, tiles=(D_MODEL, FF_PIECE, None), out_view=(
        w1.shape, _piece(D_MODEL, FF_PIECE, lambda i, j, kk: j), None))
    sib1 = sibling_rider([dw1])
    dx, dn2w = _norm_bwd(x, n2w, dh2, dy, f"norm2_bwd_{layer}", riders=[sib1])
    return dx, dw1, dw2, dn2w, sib1, sib2


def _in_proj_bwd(h, dmain, dsmall, w_main, w_small, tag):
    dh = matmul(dmain, w_main, tb=True, name=f"inproj_dx_main_{tag}")
    dh = matmul(dsmall, w_small, tb=True, post=_add, post_ins=[dh], name=f"inproj_dx_small_{tag}")
    dw_main = matmul(h, dmain, ta=True, out_dtype=BF16, name=f"inproj_dw_main_{tag}")
    dw_small = matmul(h, dsmall, ta=True, out_dtype=BF16, name=f"inproj_dw_small_{tag}")
    return dh, dw_main, dw_small


def local_step(x, mem, target, w, m, v):
    t = x.shape[0]
    n_mem = mem.shape[0]
    g = {}

    def wire(a):
        return a.astype(BF16)

    (dn_g,), = run_riders([gather_rider([wire(w["dn_w_in"][0])])], name="weights_gather_first")
    dn_main, dn_ab = in_proj_weights(dn_g, DN_IN, 2 * N_HEADS)
    fox_w = wire(w["fox_w_in"][0])
    ride_out = gather_rider([wire(w["w_out"][0]), wire(w["w_out"][1]), w["dn_conv_w"][0]])
    ride_kv = gather_rider([wire(w["w_mem_kv"])])
    ride_mlp1_0 = gather_rider([wire(w["w_mlp1"][0])])
    ride_mlp2_0 = gather_rider([wire(w["w_mlp2"][0])])
    ride_fox_a, ride_fox_b = gather_rider([fox_w[:D_MODEL // 2]]), gather_rider([fox_w[D_MODEL // 2:]])
    ride_mlp_1 = gather_rider([wire(w["w_mlp1"][1]), wire(w["w_mlp2"][1])])
    mnw, mknw = _row(w["mem_norm_w"]), _row(w["mem_k_norm_w"])

    n1w0, n2w0 = _row(w["norm1_w"][0]), _row(w["norm2_w"][0])
    alog, dtb = _row(w["dn_a_log"][0], LANES), _row(w["dn_dt_bias"][0], LANES)
    onw, mqw0 = _row(w["dn_o_norm_w"][0]), _row(w["memq_norm_w"][0])
    x0 = x
    h0 = _norm_fwd(x0, n1w0, "norm1_fwd_0")
    pm0 = matmul(h0, dn_main, name="inproj_main_0", riders=[ride_out])
    w_out0, w_out1 = (a.reshape(OUT_IN, D_MODEL) for a in ride_out.results[:2])
    conv_w = ride_out.results[2].transpose(1, 0, 2).reshape(CONV_WIDTH, 3 * D_MODEL)
    ps0 = matmul(h0, dn_ab, name="inproj_small_0")
    gates = rows_call(dn_gates_fn, [ps0], [alog, dtb], [(LANES, F32)], [], tm=512, name="dn_gates_fwd")[0]
    q0, k0, v0 = dn_prep_fwd(pm0, conv_w, name="dn_prep_fwd", riders=[ride_kv])
    w_kv = ride_kv.results[0].reshape(D_MODEL, D_MODEL)
    mk, mv = memkv_fwd(mem, mnw, w_kv, mknw, name="memkv_fwd")
    u0, w0, qk0 = delta_intra_fwd(q0, k0, v0, gates, name="delta_intra_fwd", riders=[ride_mlp1_0])
    o0, s_start = delta_seq_fwd(q0, k0, gates, u0, w0, qk0, name="delta_seq_fwd", riders=[ride_mlp2_0])
    cat0 = rows_call(dn_out_fn, [o0, (pm0, D_MODEL, 3), (pm0, MEM_WIDTH, 8)], [onw, mqw0, mk, mv],
                     [(D_MODEL + MEM_WIDTH, BF16)], [], tm=256, name="dn_out_fwd")[0]
    (w1_0,), (w2_0,) = ride_mlp1_0.results, ride_mlp2_0.results
    x1 = matmul(cat0, w_out0, post=_add, post_ins=[x0], name="wout_fwd_0")
    x2, mlp_res0 = _mlp_fwd(x1, n2w0, w1_0, w2_0, 0, riders=[[ride_fox_a], [ride_fox_b]])
    fox_main, fox_f = in_proj_weights(
        jnp.concatenate([ride_fox_a.results[0], ride_fox_b.results[0]], axis=1), FOX_IN, N_HEADS)

    n1w1, n2w1 = _row(w["norm1_w"][1]), _row(w["norm2_w"][1])
    fbias = _row(w["fox_f_bias"][0], LANES)
    qnw, knw, mqw1 = _row(w["fox_q_norm_w"][0]), _row(w["fox_k_norm_w"][0]), _row(w["memq_norm_w"][1])
    h1 = _norm_fwd(x2, n1w1, "norm1_fwd_1")
    pm1 = matmul(h1, fox_main, name="inproj_main_1")
    ps1 = matmul(h1, fox_f, name="inproj_small_1")
    fq = rows_call(fox_fcum_fn, [ps1], [fbias], [(LANES, F32)], [], tm=t, name="fox_fcum_fwd")[0]
    fk = fq[:, :N_HEADS].T[:, None, :]
    q1, k1 = rows_call(fox_qk_fn, [(pm1, D_MODEL, 0), (pm1, D_MODEL, 1)], [qnw, knw], [(D_MODEL, F32)] * 2, [], tm=256,
                       name="fox_qk_fwd")
    o1 = fox_attn_fwd(q1, k1, pm1, fq, fk, name="fox_attn_fwd", riders=[ride_mlp_1])
    cat1 = rows_call(fox_out_fn, [o1, (pm1, D_MODEL, 3), (pm1, MEM_WIDTH, 8)], [mqw1, mk, mv],
                     [(D_MODEL + MEM_WIDTH, BF16)], [], tm=256, name="fox_out_fwd")[0]
    w1_1, w2_1 = ride_mlp_1.results
    x3 = matmul(cat1, w_out1, post=_add, post_ins=[x2], name="wout_fwd_1")
    y, mlp_res1 = _mlp_fwd(x3, n2w1, w1_1, w2_1, 1)

    def loss_fn(y, tgt):
        e = y - tgt
        return e * (1.0 / D_MODEL), jnp.sum(jnp.sum(e * e, axis=1, keepdims=True), axis=0, keepdims=True)
    dy, sq = rows_call(loss_fn, [y, target], [], [(D_MODEL, F32)], [(1, 1)], tm=512, name="loss")
    loss = sq[0, 0] * (0.5 / D_MODEL)

    dx3, dw1_1, dw2_1, dn2w1, sib1, sib2 = _mlp_bwd(dy, mlp_res1, n2w1, w1_1, w2_1, 1)
    dcat1 = matmul(dx3, w_out1, tb=True, name="wout_dx_1")
    dwo_1 = matmul(cat1, dx3, ta=True, out_dtype=BF16, name="wout_dw_1").reshape(N_DEV, OUT_IN // N_DEV, D_MODEL)
    sibo = sibling_rider([dwo_1])
    do1, dgate1, dqm1, dmqw1, dmk1, dmv1 = rows_call(
        functools.partial(vjp_rows(fox_out_fn, 3, (True, True, True)), n_row=3, n_ct=1),
        [o1, (pm1, D_MODEL, 3), (pm1, MEM_WIDTH, 8), dcat1], [mqw1, mk, mv],
        [(D_MODEL, F32), (D_MODEL, BF16), (MEM_WIDTH, BF16)], [(1, HEAD_DIM), (n_mem, MEM_WIDTH), (n_mem, MEM_WIDTH)],
        tm=256, name="fox_out_bwd", riders=[sibo])
    ride_l1 = chips_rider(chip_sums(["w_mlp2_1", "w_mlp1_1", "w_out_1"], [dw2_1, dw1_1, dwo_1],
                                    sib2.results + sib1.results + sibo.results))
    dq1, dk1, dv1, dfq, dfk = fox_attn_bwd(q1, k1, pm1, fq, fk, do1, name="fox_attn_bwd", riders=[ride_l1])
    dqraw1, dkraw1, dqnw, dknw = rows_call(
        functools.partial(vjp_rows(fox_qk_fn, 2, (True, True)), n_row=2, n_ct=2),
        [(pm1, D_MODEL, 0), (pm1, D_MODEL, 1), dq1, dk1], [qnw, knw],
        [(D_MODEL, BF16)] * 2, [(1, HEAD_DIM)] * 2, tm=256, name="fox_qk_bwd")
    dfcum = dfq + jnp.pad(dfk[:, 0, :].T, ((0, 0), (0, LANES - N_HEADS)))
    dps1, dfbias = rows_call(
        functools.partial(vjp_rows(fox_fcum_fn, 1, (True,)), n_row=1, n_ct=1),
        [ps1, dfcum], [fbias], [(LANES, F32)], [(1, LANES)], tm=t, name="fox_fcum_bwd")
    dpm1 = jnp.concatenate([dqraw1, dkraw1, dv1, dgate1, dqm1], axis=1)
    dh1, dwmain1, dwsmall1 = _in_proj_bwd(h1, dpm1, dps1, fox_main, fox_f, "1")
    g_fox = in_proj_pieces(dwmain1, dwsmall1, N_HEADS, FOX_IN)
    sibf = sibling_rider([g_fox])
    dx2, dn1w1 = _norm_bwd(x2, n1w1, dh1, dx3, "norm1_bwd_1", riders=[sibf])
    ride_fox_g = chips_rider(chip_sums(["fox_w_in"], [g_fox], sibf.results))

    dx1, dw1_0, dw2_0, dn2w0, sib1, sib2 = _mlp_bwd(dx2, mlp_res0, n2w0, w1_0, w2_0, 0)
    dcat0 = matmul(dx1, w_out0, tb=True, name="wout_dx_0")
    dwo_0 = matmul(cat0, dx1, ta=True, out_dtype=BF16, name="wout_dw_0").reshape(N_DEV, OUT_IN // N_DEV, D_MODEL)
    sibo = sibling_rider([dwo_0])
    do0, dz0, dqm0, donw, dmqw0, dmk0, dmv0 = rows_call(
        functools.partial(vjp_rows(dn_out_fn, 3, (True, True, True, True)), n_row=3, n_ct=1),
        [o0, (pm0, D_MODEL, 3), (pm0, MEM_WIDTH, 8), dcat0], [onw, mqw0, mk, mv],
        [((N_HEADS, HEAD_DIM), F32), (D_MODEL, BF16), (MEM_WIDTH, BF16)],
        [(1, HEAD_DIM), (1, HEAD_DIM), (n_mem, MEM_WIDTH), (n_mem, MEM_WIDTH)], tm=256, name="dn_out_bwd", riders=[sibo])
    h_l0 = chip_sums(["w_mlp2_0", "w_mlp1_0", "w_out_0"], [dw2_0, dw1_0, dwo_0], sib2.results + sib1.results + sibo.results)
    ride_l0_mlp, ride_l0_out = chips_rider(h_l0[:2]), chips_rider(h_l0[2:])
    dq_s, dk_s, dg_s, du0, dw0, dqk0 = delta_seq_bwd(q0, k0, gates, u0, w0, qk0, s_start, do0, name="delta_seq_bwd",
                                                     riders=[ride_fox_g])
    dq0, dk0, dv0, dgates = delta_intra_bwd(q0, k0, v0, gates, du0, dw0, dqk0, dq_s, dk_s, dg_s,
                                            name="delta_intra_bwd", riders=[ride_l0_mlp])
    dxq, dxk, dxv, dcq, dck, dcv = dn_prep_bwd(pm0, conv_w, dq0, dk0, dv0, name="dn_prep_bwd", riders=[ride_l0_out])
    dconv = jnp.concatenate([dcq, dck, dcv], axis=1)
    dps0, dalog, ddtb = rows_call(
        functools.partial(vjp_rows(dn_gates_fn, 1, (True, True)), n_row=1, n_ct=1),
        [ps0, dgates], [alog, dtb], [(LANES, F32)], [(1, LANES)] * 2, tm=512, name="dn_gates_bwd")
    dpm0 = jnp.concatenate([dxq, dxk, dxv, dz0, dqm0], axis=1)
    dh0, dwmain0, dwsmall0 = _in_proj_bwd(h0, dpm0, dps0, dn_main, dn_ab, "0")
    g_dn = in_proj_pieces(dwmain0, dwsmall0, 2 * N_HEADS, DN_IN)
    g_conv = dconv.reshape(CONV_WIDTH, N_DEV, -1).transpose(1, 0, 2).astype(BF16)
    sibd = sibling_rider([g_dn, g_conv])
    grad_x, dn1w0 = _norm_bwd(x0, n1w0, dh0, dx1, "norm1_bwd_0", riders=[sibd])

    dmnw, dwkv, dmknw = memkv_bwd(mem, mnw, w_kv, mknw, dmk0 + dmk1, dmv0 + dmv1, name="memkv_bwd")
    g_kv = dwkv.reshape(N_DEV, D_MODEL // N_DEV, D_MODEL)

    g["mem_norm_w"] = dmnw[0]
    g["mem_k_norm_w"] = dmknw[0]
    g["norm1_w"] = jnp.concatenate([dn1w0, dn1w1], axis=0)
    g["dn_a_log"] = dalog[:, :N_HEADS]
    g["dn_dt_bias"] = ddtb[:, :N_HEADS]
    g["dn_o_norm_w"] = donw
    g["fox_f_bias"] = dfbias[:, :N_HEADS]
    g["fox_q_norm_w"] = dqnw
    g["fox_k_norm_w"] = dknw
    g["memq_norm_w"] = jnp.concatenate([dmqw0, dmqw1], axis=0)
    g["norm2_w"] = jnp.concatenate([dn2w0, dn2w1], axis=0)

    def layers(l0, l1):
        return jnp.stack([l0, l1], axis=1).reshape(4, -1, l0.shape[-1])

    shards = {n: [d[n].reshape(r, c) for d in (w, m, v)] for n, r, c in BIG}
    h_dn, h_conv = chip_sums(["dn_w_in", "dn_conv_w"], [g_dn, g_conv], sibd.results)
    quarter = D_MODEL // 4
    ride_dn = [chips_rider([h_dn[:, i * quarter:(i + 1) * quarter]]) for i in range(4)]
    ride_conv, sibk, ride_small = chips_rider([h_conv]), sibling_rider([g_kv]), gather_rider([pack_small(g)])
    out = {}
    out["w_mlp1"] = adamw(layers(ride_l0_mlp.results[1], ride_l1.results[1]), *shards["w_mlp1"], name="adamw_w_mlp1",
                          riders=[ride_dn[0], ride_conv, sibk, ride_small])
    ride_kv_g = chips_rider(chip_sums(["w_mem_kv"], [g_kv], sibk.results))
    out["w_mlp2"] = adamw(layers(ride_l0_mlp.results[0], ride_l1.results[0]), *shards["w_mlp2"], name="adamw_w_mlp2",
                          riders=[ride_dn[1], ride_kv_g])
    out["w_out"] = adamw(layers(ride_l0_out.results[0], ride_l1.results[2]), *shards["w_out"], name="adamw_w_out",
                         riders=[ride_dn[2]])
    out["fox_w_in"] = adamw(ride_fox_g.results[0], *shards["fox_w_in"], name="adamw_fox_w_in", riders=[ride_dn[3]])
    out["dn_w_in"] = adamw(jnp.concatenate([r.results[0] for r in ride_dn], axis=1), *shards["dn_w_in"], name="adamw_dn_w_in")
    out["dn_conv_w"] = adamw(ride_conv.results[0], *shards["dn_conv_w"], name="adamw_dn_conv_w")
    out["w_mem_kv"] = adamw(ride_kv_g.results[0], *shards["w_mem_kv"], name="adamw_w_mem_kv")
    small = adamw(ride_small.results[0], pack_small(w), pack_small(m), pack_small(v), name="adamw_small")
    return loss, grad_x, {n: [o.reshape(w[n].shape) for o in outs] for n, outs in out.items()}, small


WEIGHTS = ["mem_norm_w", "w_mem_kv", "mem_k_norm_w", "norm1_w", "dn_w_in", "dn_conv_w", "dn_a_log", "dn_dt_bias",
           "dn_o_norm_w", "fox_w_in", "fox_f_bias", "fox_q_norm_w", "fox_k_norm_w", "memq_norm_w", "w_out", "norm2_w",
           "w_mlp1", "w_mlp2"]
DN_IN = 4 * D_MODEL + 2 * N_HEADS + MEM_WIDTH
FOX_IN = 4 * D_MODEL + N_HEADS + MEM_WIDTH
GATE_END = 4 * D_MODEL
OUT_IN = D_MODEL + MEM_WIDTH
BIG = [("w_mem_kv", D_MODEL // N_DEV, D_MODEL), ("dn_w_in", D_MODEL, DN_IN // N_DEV), ("fox_w_in", D_MODEL, FOX_IN // N_DEV),
       ("dn_conv_w", CONV_WIDTH, 3 * D_MODEL // N_DEV), ("w_out", 2 * OUT_IN // N_DEV, D_MODEL),
       ("w_mlp1", 2 * D_MODEL, FF_PIECE), ("w_mlp2", 2 * FF_PIECE, D_MODEL)]
SMALL = [("mem_norm_w", (D_MODEL,), D_MODEL), ("mem_k_norm_w", (HEAD_DIM,), HEAD_DIM), ("norm1_w", (2, D_MODEL), 2 * D_MODEL),
         ("dn_a_log", (1, N_HEADS), LANES), ("dn_dt_bias", (1, N_HEADS), LANES), ("dn_o_norm_w", (1, HEAD_DIM), HEAD_DIM),
         ("fox_f_bias", (1, N_HEADS), LANES), ("fox_q_norm_w", (1, HEAD_DIM), HEAD_DIM), ("fox_k_norm_w", (1, HEAD_DIM), HEAD_DIM),
         ("memq_norm_w", (2, HEAD_DIM), 2 * HEAD_DIM), ("norm2_w", (2, D_MODEL), 2 * D_MODEL)]
SMALL_ROWS = 16


def pack_small(p):
    flat = jnp.concatenate([jnp.pad(p[n].reshape(-1), (0, ln - math.prod(sh))) for n, sh, ln in SMALL])
    return jnp.pad(flat, (0, SMALL_ROWS * PACK_W - flat.shape[0])).reshape(SMALL_ROWS, PACK_W)


def unpack_small(pk):
    flat, off, out = pk.reshape(-1), 0, {}
    for n, sh, ln in SMALL:
        out[n] = flat[off:off + math.prod(sh)].reshape(sh)
        off += ln
    return out


def in_proj_weights(gathered, width, n_small):
    full = gathered.transpose(1, 0, 2).reshape(D_MODEL, width)
    main = jnp.concatenate([full[:, :GATE_END], full[:, GATE_END + n_small:]], axis=1)
    return main, jnp.pad(full[:, GATE_END:GATE_END + n_small], ((0, 0), (0, LANES - n_small)))


def in_proj_pieces(d_main, d_small, n_small, width):
    full = jnp.concatenate([d_main[:, :GATE_END], d_small[:, :n_small], d_main[:, GATE_END:]], axis=1)
    return full.reshape(D_MODEL, N_DEV, width // N_DEV).transpose(1, 0, 2)


def adamw(parts, w, m, v, *, name, riders=()):
    n, rows, cols = parts.shape
    tile = _pick(rows, (512, 256, 128))

    def body(p_ref, w_ref, m_ref, v_ref, g_ref, d_ref, mo_ref, vo_ref):
        g = p_ref[0].astype(F32)
        for i in range(1, n):
            g = g + p_ref[i].astype(F32)
        m_new = ADAM_B1 * m_ref[...] + (1.0 - ADAM_B1) * g
        v_new = ADAM_B2 * v_ref[...] + (1.0 - ADAM_B2) * jnp.square(g)
        m_hat = m_new / (1.0 - ADAM_B1 ** ADAM_STEP)
        v_hat = v_new / (1.0 - ADAM_B2 ** ADAM_STEP)
        g_ref[...] = g
        d_ref[...] = -ADAM_LR * (m_hat / (jnp.sqrt(v_hat) + ADAM_EPS) + ADAM_WD * w_ref[...])
        mo_ref[...] = m_new
        vo_ref[...] = v_new

    spec = pl.BlockSpec((tile, cols), lambda i: (i, 0))
    return hosted_call(
        riders, body, name=name, grid=(rows // tile,),
        in_specs=[pl.BlockSpec((n, tile, cols), lambda i: (0, i, 0)), spec, spec, spec], out_specs=[spec] * 4,
        out_shape=[jax.ShapeDtypeStruct((rows, cols), F32)] * 4, compiler_params=_params(("parallel",)),
    )(parts, w, m, v)


def kernel(x, mem, mem_norm_w, w_mem_kv, mem_k_norm_w, norm1_w, dn_w_in, dn_conv_w, dn_a_log, dn_dt_bias, dn_o_norm_w, fox_w_in, fox_f_bias, fox_q_norm_w, fox_k_norm_w, memq_norm_w, w_out, norm2_w, w_mlp1, w_mlp2, loss_target, m_mem_norm_w, m_w_mem_kv, m_mem_k_norm_w, m_norm1_w, m_dn_w_in, m_dn_conv_w, m_dn_a_log, m_dn_dt_bias, m_dn_o_norm_w, m_fox_w_in, m_fox_f_bias, m_fox_q_norm_w, m_fox_k_norm_w, m_memq_norm_w, m_w_out, m_norm2_w, m_w_mlp1, m_w_mlp2, v_mem_norm_w, v_w_mem_kv, v_mem_k_norm_w, v_norm1_w, v_dn_w_in, v_dn_conv_w, v_dn_a_log, v_dn_dt_bias, v_dn_o_norm_w, v_fox_w_in, v_fox_f_bias, v_fox_q_norm_w, v_fox_k_norm_w, v_memq_norm_w, v_w_out, v_norm2_w, v_w_mlp1, v_w_mlp2):
    p = dict(mem_norm_w=mem_norm_w, w_mem_kv=w_mem_kv, mem_k_norm_w=mem_k_norm_w, norm1_w=norm1_w, dn_w_in=dn_w_in,
             dn_conv_w=dn_conv_w, dn_a_log=dn_a_log, dn_dt_bias=dn_dt_bias, dn_o_norm_w=dn_o_norm_w, fox_w_in=fox_w_in,
             fox_f_bias=fox_f_bias, fox_q_norm_w=fox_q_norm_w, fox_k_norm_w=fox_k_norm_w, memq_norm_w=memq_norm_w,
             w_out=w_out, norm2_w=norm2_w, w_mlp1=w_mlp1, w_mlp2=w_mlp2)
    pm = dict(mem_norm_w=m_mem_norm_w, w_mem_kv=m_w_mem_kv, mem_k_norm_w=m_mem_k_norm_w, norm1_w=m_norm1_w,
              dn_w_in=m_dn_w_in, dn_conv_w=m_dn_conv_w, dn_a_log=m_dn_a_log, dn_dt_bias=m_dn_dt_bias,
              dn_o_norm_w=m_dn_o_norm_w, fox_w_in=m_fox_w_in, fox_f_bias=m_fox_f_bias, fox_q_norm_w=m_fox_q_norm_w,
              fox_k_norm_w=m_fox_k_norm_w, memq_norm_w=m_memq_norm_w, w_out=m_w_out, norm2_w=m_norm2_w, w_mlp1=m_w_mlp1,
              w_mlp2=m_w_mlp2)
    pv = dict(mem_norm_w=v_mem_norm_w, w_mem_kv=v_w_mem_kv, mem_k_norm_w=v_mem_k_norm_w, norm1_w=v_norm1_w,
              dn_w_in=v_dn_w_in, dn_conv_w=v_dn_conv_w, dn_a_log=v_dn_a_log, dn_dt_bias=v_dn_dt_bias,
              dn_o_norm_w=v_dn_o_norm_w, fox_w_in=v_fox_w_in, fox_f_bias=v_fox_f_bias, fox_q_norm_w=v_fox_q_norm_w,
              fox_k_norm_w=v_fox_k_norm_w, memq_norm_w=v_memq_norm_w, w_out=v_w_out, norm2_w=v_norm2_w, w_mlp1=v_w_mlp1,
              w_mlp2=v_w_mlp2)

    loss, grad_x, results, small = local_step(x[0], mem[0], loss_target[0], p, pm, pv)
    loss = lax.psum(loss, ("x", "y", "c"))
    small = [unpack_small(o) for o in small]
    groups = [{**small[i], **{n: r[i] for n, r in results.items()}} for i in range(4)]
    return (loss, grad_x[None], *[grp[n] for grp in groups for n in WEIGHTS])
```

```python
import functools
import math

import jax
import jax.numpy as jnp
from jax import lax
from jax.experimental import pallas as pl
from jax.experimental.pallas import tpu as pltpu

F32 = jnp.float32
BF16 = jnp.bfloat16
HIGHEST = lax.Precision.HIGHEST

D_MODEL = 1024
HEAD_DIM = 128
N_HEADS = 8
MEM_HEADS = 4
MEM_WIDTH = MEM_HEADS * HEAD_DIM
D_FF = 4 * D_MODEL
CONV_WIDTH = 4
CHUNK = 64
Q_BLOCK = 128
EPS = 1e-6
SCALE = HEAD_DIM ** -0.5
MAIN_WIDTH = 4 * D_MODEL + MEM_WIDTH
LANES = 128
N_DEV = 8
PACK_W = 512

ADAM_LR = 0.001
ADAM_B1 = 0.9
ADAM_B2 = 0.999
ADAM_EPS = 1e-08
ADAM_WD = 0.01
ADAM_STEP = 10

VMEM_LIMIT = 56 * 2 ** 20
MESH = pl.DeviceIdType.MESH


def _bdot(a, b, dims):
    return lax.dot_general(a.astype(BF16), b.astype(BF16), (dims, ((), ())), preferred_element_type=F32)


@jax.custom_vjp
def mm(a, b):
    return _bdot(a, b, ((1,), (0,)))


@jax.custom_vjp
def mm_nt(a, b):
    return _bdot(a, b, ((1,), (1,)))


@jax.custom_vjp
def mm_tn(a, b):
    return _bdot(a, b, ((0,), (0,)))


mm.defvjp(lambda a, b: (mm(a, b), (a, b)), lambda r, g: (mm_nt(g, r[1]), mm_tn(r[0], g)))
mm_nt.defvjp(lambda a, b: (mm_nt(a, b), (a, b)), lambda r, g: (mm(g, r[1]), mm_tn(g, r[0])))
mm_tn.defvjp(lambda a, b: (mm_tn(a, b), (a, b)), lambda r, g: (mm_nt(r[1], g), mm(r[0], g)))


def hdot(a, b):
    return jnp.dot(a, b, precision=HIGHEST, preferred_element_type=F32)


def rms(x, w):
    return x * lax.rsqrt(jnp.mean(x * x, axis=-1, keepdims=True) + EPS) * w


def l2n(x):
    return x * lax.rsqrt(jnp.sum(x * x, axis=-1, keepdims=True) + EPS)


def _iota2(n, m):
    return lax.broadcasted_iota(jnp.int32, (n, m), 0), lax.broadcasted_iota(jnp.int32, (n, m), 1)


def _lower_ones(n):
    r, c = _iota2(n, n)
    return jnp.where(r >= c, 1.0, 0.0).astype(F32)


def _last_row(x):
    r = lax.broadcasted_iota(jnp.int32, x.shape, 0)
    return jnp.sum(jnp.where(r == x.shape[0] - 1, x, 0.0), axis=0, keepdims=True)


def _softmax_rows(z):
    m = lax.stop_gradient(jnp.max(z, axis=-1, keepdims=True))
    e = jnp.exp(z - m)
    return e * (1.0 / jnp.sum(e, axis=-1, keepdims=True))


_BNN = (((2,), (1,)), ((0,), (0,)))
_BNT = (((2,), (2,)), ((0,), (0,)))
_BTN = (((1,), (1,)), ((0,), (0,)))


def _bbdot(a, b, dims):
    return lax.dot_general(a.astype(BF16), b.astype(BF16), dims, preferred_element_type=F32)


@jax.custom_vjp
def bmm(a, b):
    return _bbdot(a, b, _BNN)


@jax.custom_vjp
def bmm_nt(a, b):
    return _bbdot(a, b, _BNT)


@jax.custom_vjp
def bmm_tn(a, b):
    return _bbdot(a, b, _BTN)


@jax.custom_vjp
def bmm_high(a, b):
    return lax.dot_general(a, b, _BNN, precision=lax.Precision.HIGH, preferred_element_type=F32)


bmm.defvjp(lambda a, b: (bmm(a, b), (a, b)), lambda r, g: (bmm_nt(g, r[1]), bmm_tn(r[0], g)))
bmm_nt.defvjp(lambda a, b: (bmm_nt(a, b), (a, b)), lambda r, g: (bmm(g, r[1]), bmm_tn(g, r[0])))
bmm_tn.defvjp(lambda a, b: (bmm_tn(a, b), (a, b)), lambda r, g: (bmm_nt(r[1], g), bmm(r[0], g)))
bmm_high.defvjp(lambda a, b: (bmm_high(a, b), (a, b)), lambda r, g: (bmm_nt(g, r[1]), bmm_tn(r[0], g)))

NEUMANN_HIGH_LEVELS = 2


def inv_unit_lower(a):
    n = a.shape[-1]
    r, c = _iota2(n, n)
    p = jnp.where(r == c, 1.0, 0.0).astype(F32) - a
    ak = a
    for level in range(int(math.log2(n)) - 1):
        dot = bmm_high if level < NEUMANN_HIGH_LEVELS else bmm
        ak = dot(ak, ak)
        p = p + dot(p, ak)
    return p


def delta_intra(q, k, v, gc, beta):
    b, c, _ = q.shape
    r, cc = _iota2(c, c)
    causal = r >= cc
    strict = r > cc
    gi = jnp.broadcast_to(gc, (b, c, c))
    gj = jnp.swapaxes(gi, 1, 2)
    decay = jnp.where(causal, jnp.exp(jnp.where(causal, gi - gj, 0.0)), 0.0)
    kb = k * beta
    a = jnp.where(strict, bmm_nt(kb, k) * decay, 0.0)
    t = inv_unit_lower(a)
    u = bmm(t, v * beta)
    w = bmm(t, kb * jnp.exp(gc))
    qk = jnp.where(causal, bmm_nt(q, k) * decay, 0.0)
    return u, w, qk


def delta_step(s, q, k, gc, u, w, qk):
    v_new = u - bmm(w, s)
    out = bmm(q * jnp.exp(gc), s) + bmm(qk, v_new)
    r = lax.broadcasted_iota(jnp.int32, gc.shape, 1)
    g_last = jnp.sum(jnp.where(r == gc.shape[1] - 1, gc, 0.0), axis=1, keepdims=True)
    k_dec = k * jnp.exp(g_last - gc)
    s_new = s * jnp.exp(g_last) + bmm_tn(k_dec, v_new)
    return out, s_new


def fox_probs(q, k, fq, fk, qpos0):
    s = lax.dot_general(q, k, (((1,), (1,)), ((), ())), preferred_element_type=F32)
    r, c = _iota2(s.shape[0], s.shape[1])
    return _softmax_rows(jnp.where(c <= (r + qpos0), s + (fq - fk), -jnp.inf))


def mem_head(qm, wq, mk, mv):
    p = _softmax_rows(mm_nt(rms(qm, wq) * SCALE, mk))
    return mm(p, mv)


def _heads(x, n):
    return [x[:, h * HEAD_DIM:(h + 1) * HEAD_DIM] for h in range(n)]


def memkv_fn(mem, mnw, wkv, mknw):
    kv = mm(rms(mem, mnw), wkv)
    mk = jnp.concatenate([rms(kh, mknw) for kh in _heads(kv[:, :MEM_WIDTH], MEM_HEADS)], axis=1)
    return mk, kv[:, MEM_WIDTH:]


def dn_gates_fn(ab, alog, dtb):
    g = -jnp.exp(alog) * jax.nn.softplus(ab + dtb)
    low = _lower_ones(CHUNK)
    gc = jnp.concatenate([hdot(low, g[i * CHUNK:(i + 1) * CHUNK]) for i in range(ab.shape[0] // CHUNK)], axis=0)
    lane = lax.broadcasted_iota(jnp.int32, ab.shape, 1)
    return jnp.where(lane < N_HEADS, gc, jax.nn.sigmoid(ab))


def fox_fcum_fn(fp, fbias):
    lf = jax.nn.log_sigmoid(fp + fbias)
    low = _lower_ones(LANES)
    carry = jnp.zeros((1, fp.shape[1]), F32)
    outs = []
    for i in range(fp.shape[0] // LANES):
        cs = hdot(low, lf[i * LANES:(i + 1) * LANES]) + carry
        carry = _last_row(cs)
        outs.append(cs)
    return jnp.concatenate(outs, axis=0)


def fox_qk_fn(qraw, kraw, qnw, knw):
    q = jnp.concatenate([rms(x, qnw) * SCALE for x in _heads(qraw, N_HEADS)], axis=1)
    k = jnp.concatenate([rms(x, knw) for x in _heads(kraw, N_HEADS)], axis=1)
    return q, k


def _mem_out(qm, mqw, mk, mv):
    return [mem_head(a, mqw, b, c) for a, b, c in zip(_heads(qm, MEM_HEADS), _heads(mk, MEM_HEADS), _heads(mv, MEM_HEADS))]


def dn_out_fn(o, z, qm, onw, mqw, mk, mv):
    mix = [rms(a, onw) * jax.nn.silu(b) for a, b in zip(o, _heads(z, N_HEADS))]
    return jnp.concatenate(mix + _mem_out(qm, mqw, mk, mv), axis=1)


def fox_out_fn(o, gate, qm, mqw, mk, mv):
    return jnp.concatenate([o * jax.nn.sigmoid(gate)] + _mem_out(qm, mqw, mk, mv), axis=1)


_HBM = pl.BlockSpec(memory_space=pltpu.HBM)


def _place():
    return lax.axis_index("x"), lax.axis_index("y"), lax.axis_index("c")


class Rider:
    def __init__(self, ins, out_shape, scratch, start, finish):
        self.ins, self.out_shape, self.scratch, self.start, self.finish = list(ins), list(out_shape), list(scratch), start, finish
        self.results = None


def gather_rider(xs):
    n = len(xs)

    def plan(x_refs, out_refs, sems):
        send_sems, recv_sems, local_sems = sems
        x, y, c = _place()
        me, sibling = (x, y, c), (x, y, 1 - c)
        chips = [(1 - x, y), (x, 1 - y), (1 - x, 1 - y)]

        def copy(a, k, block, to, src=None):
            px, py, pc = block
            dst = out_refs[a].at[4 * px + 2 * py + pc]
            return pltpu.make_async_remote_copy(
                src_ref=dst if src is None else src, dst_ref=dst,
                send_sem=send_sems.at[a, k], recv_sem=recv_sems.at[a, k], device_id=to, device_id_type=MESH)

        mine = [pltpu.make_async_copy(x_refs[a], out_refs[a].at[4 * x + 2 * y + c], local_sems.at[a]) for a in range(n)]
        first = [copy(a, 0, me, sibling, src=x_refs[a]) for a in range(n)]
        first += [copy(a, 1 + j, me, (*chip, c), src=x_refs[a]) for j, chip in enumerate(chips) for a in range(n)]
        return copy, me, sibling, chips, mine, first

    def start(x_refs, out_refs, sems):
        _, _, _, _, mine, first = plan(x_refs, out_refs, sems)
        for cp in mine + first:
            cp.start()

    def finish(x_refs, out_refs, sems):
        copy, me, sibling, chips, mine, first = plan(x_refs, out_refs, sems)
        _, _, c = me
        passed = []
        for j, chip in enumerate(chips):
            for a in range(n):
                copy(a, 1 + j, (*chip, c), me).wait_recv()
                passed.append(copy(a, 4 + j, (*chip, c), sibling))
                passed[-1].start()
        for a in range(n):
            copy(a, 0, sibling, me).wait_recv()
        for j, chip in enumerate(chips):
            for a in range(n):
                copy(a, 4 + j, (*chip, 1 - c), me).wait_recv()
        for cp in first + passed:
            cp.wait_send()
        for cp in mine:
            cp.wait()

    return Rider(xs, [jax.ShapeDtypeStruct((N_DEV,) + a.shape, a.dtype) for a in xs],
                 [pltpu.SemaphoreType.DMA((n, 7)), pltpu.SemaphoreType.DMA((n, 7)), pltpu.SemaphoreType.DMA((n,))], start, finish)


def sibling_rider(gs):
    n = len(gs)

    def plan(g_refs, out_refs, sems):
        send_sems, recv_sems = sems
        x, y, c = _place()
        return [pltpu.make_async_remote_copy(
            src_ref=g_refs[a].at[2 * k + 1 - c], dst_ref=out_refs[a].at[k], send_sem=send_sems.at[a, k],
            recv_sem=recv_sems.at[a, k], device_id=(x, y, 1 - c), device_id_type=MESH) for a in range(n) for k in range(4)]

    def start(g_refs, out_refs, sems):
        for cp in plan(g_refs, out_refs, sems):
            cp.start()

    def finish(g_refs, out_refs, sems):
        copies = plan(g_refs, out_refs, sems)
        for cp in copies:
            cp.wait_recv()
        for cp in copies:
            cp.wait_send()

    return Rider(gs, [jax.ShapeDtypeStruct((4,) + g.shape[1:], g.dtype) for g in gs],
                 [pltpu.SemaphoreType.DMA((n, 4)), pltpu.SemaphoreType.DMA((n, 4))], start, finish)


def chips_rider(hs):
    n = len(hs)

    def plan(h_refs, out_refs, sems):
        send_sems, recv_sems, local_sems = sems
        x, y, c = _place()
        mine = 2 * x + y
        chips = [(1 - x, y), (x, 1 - y), (1 - x, 1 - y)]
        keep = [pltpu.make_async_copy(h_refs[a].at[mine], out_refs[a].at[mine], local_sems.at[a]) for a in range(n)]
        sends = [pltpu.make_async_remote_copy(
            src_ref=h_refs[a].at[2 * qx + qy], dst_ref=out_refs[a].at[mine], send_sem=send_sems.at[a, j],
            recv_sem=recv_sems.at[a, j], device_id=(qx, qy, c), device_id_type=MESH)
            for j, (qx, qy) in enumerate(chips) for a in range(n)]
        recvs = [pltpu.make_async_remote_copy(
            src_ref=h_refs[a].at[mine], dst_ref=out_refs[a].at[2 * qx + qy], send_sem=send_sems.at[a, j],
            recv_sem=recv_sems.at[a, j], device_id=(qx, qy, c), device_id_type=MESH)
            for j, (qx, qy) in enumerate(chips) for a in range(n)]
        return keep, sends, recvs

    def start(h_refs, out_refs, sems):
        keep, sends, _ = plan(h_refs, out_refs, sems)
        for cp in keep + sends:
            cp.start()

    def finish(h_refs, out_refs, sems):
        keep, sends, recvs = plan(h_refs, out_refs, sems)
        for cp in recvs:
            cp.wait_recv()
        for cp in sends:
            cp.wait_send()
        for cp in keep:
            cp.wait()

    return Rider(hs, [jax.ShapeDtypeStruct(h.shape, h.dtype) for h in hs],
                 [pltpu.SemaphoreType.DMA((n, 3)), pltpu.SemaphoreType.DMA((n, 3)), pltpu.SemaphoreType.DMA((n,))], start, finish)


def hosted_call(riders, body, *, out_shape, in_specs, out_specs, grid=(), scratch_shapes=(), **kw):
    riders = tuple(riders or ())
    if not riders:
        return pl.pallas_call(body, out_shape=out_shape, in_specs=in_specs, out_specs=out_specs, grid=grid,
                              scratch_shapes=scratch_shapes, **kw)
    single = not isinstance(out_shape, (list, tuple))
    k_out_shape = [out_shape] if single else list(out_shape)
    k_out_specs = [out_specs] if single else list(out_specs)
    n_in, n_out, n_scr = len(in_specs), len(k_out_shape), len(scratch_shapes)
    r_ins = [a for r in riders for a in r.ins]
    r_outs = [s for r in riders for s in r.out_shape]
    r_scr = [s for r in riders for s in r.scratch]

    def full_body(*refs):
        ins = refs[:n_in + len(r_ins)]
        outs = refs[n_in + len(r_ins):n_in + len(r_ins) + n_out + len(r_outs)]
        scr = refs[n_in + len(r_ins) + n_out + len(r_outs):]
        ids = [pl.program_id(d) for d in range(len(grid))]
        first = functools.reduce(jnp.logical_and, [i == 0 for i in ids]) if ids else None
        last = functools.reduce(jnp.logical_and, [i == g - 1 for i, g in zip(ids, grid)]) if ids else None

        def each(method):
            i0, o0, s0 = n_in, n_out, n_scr
            for r in riders:
                getattr(r, method)(ins[i0:i0 + len(r.ins)], outs[o0:o0 + len(r.out_shape)], scr[s0:s0 + len(r.scratch)])
                i0, o0, s0 = i0 + len(r.ins), o0 + len(r.out_shape), s0 + len(r.scratch)

        if first is None:
            each("start")
        else:
            pl.when(first)(lambda: each("start"))
        body(*ins[:n_in], *outs[:n_out], *scr[:n_scr])
        if last is None:
            each("finish")
        else:
            pl.when(last)(lambda: each("finish"))

    call = pl.pallas_call(
        full_body, out_shape=k_out_shape + r_outs, in_specs=list(in_specs) + [_HBM] * len(r_ins),
        out_specs=k_out_specs + [_HBM] * len(r_outs), grid=grid, scratch_shapes=list(scratch_shapes) + r_scr, **kw)

    def run(*args):
        res = call(*args, *r_ins)
        o0 = n_out
        for r in riders:
            r.results = list(res[o0:o0 + len(r.out_shape)])
            o0 += len(r.out_shape)
        return res[0] if single else list(res[:n_out])

    return run


def run_riders(riders, *, name):
    hosted_call(riders, lambda: None, name=name, out_shape=[], in_specs=[], out_specs=[])()
    return [r.results for r in riders]


def _pick(n, cands):
    for c in cands:
        if n % c == 0:
            return c
    return n


def _params(sem):
    return pltpu.CompilerParams(dimension_semantics=sem, vmem_limit_bytes=VMEM_LIMIT)


MATMUL_VMEM_BUDGET = 40 * 2 ** 20


def _matmul_tiles(m, n, k, bytes_a, bytes_b, bytes_mn, fixed):
    fm, fn, fk = fixed if fixed is not None else (None, None, None)

    def options(given, size, cands):
        return [given] if given else ([c for c in cands if size % c == 0] or [size])

    best = None
    for tm in options(fm, m, (2048, 1024, 512, 256, 128)):
        for tn in options(fn, n, (512, 256, 128)):
            for tk in options(fk, k, (2048, 1536, 1024, 512, 256, 128)):
                if 2 * (tm * tk * bytes_a + tk * tn * bytes_b + tm * tn * bytes_mn) + tm * tn * 4 > MATMUL_VMEM_BUDGET:
                    continue
                key = ((m // tm) * (n // tn) * (k // tk), -tk)
                if best is None or key < best[0]:
                    best = (key, (tm, tn, tk))
    assert best is not None, (m, n, k, fixed)
    return best[1]


def matmul(a, b, *, name, ta=False, tb=False, post=None, post_ins=(), extra_out=None, out_dtype=F32, tiles=None,
           b_view=None, out_view=None, riders=()):
    (k, m) = a.shape if ta else a.shape[::-1]
    (kb, n) = b_view[:2] if b_view is not None else (b.shape[::-1] if tb else b.shape)
    assert k == kb, (a.shape, b.shape, ta, tb)
    bytes_mn = sum(p.dtype.itemsize for p in post_ins) + jnp.dtype(out_dtype).itemsize
    bytes_mn += jnp.dtype(extra_out[1]).itemsize if extra_out else 0
    tm, tn, tk = _matmul_tiles(m, n, k, a.dtype.itemsize, b.dtype.itemsize, bytes_mn, tiles)
    nk = k // tk
    dims = ((0,) if ta else (1,), (1,) if tb else (0,))
    n_post = len(post_ins)
    n_out = 2 if extra_out else 1

    def body(*refs):
        a_ref, b_ref = refs[:2]
        post_refs = refs[2:2 + n_post]
        o_refs, acc = refs[-1 - n_out:-1], refs[-1]
        kk = pl.program_id(2)

        @pl.when(kk == 0)
        def _():
            acc[...] = jnp.zeros_like(acc)

        b_tile = b_ref[...]
        acc[...] += _bdot(a_ref[...], b_tile.reshape(-1, b_tile.shape[-1]), dims)

        @pl.when(kk == nk - 1)
        def _():
            r = acc[...]
            if post is not None:
                r = post(r, *[p[...] for p in post_refs])
            o_refs[0][...] = r.astype(out_dtype)
            if extra_out:
                o_refs[1][...] = extra_out[0](r).astype(extra_out[1])

    a_spec = pl.BlockSpec((tk, tm), lambda i, j, kk: (kk, i)) if ta else pl.BlockSpec((tm, tk), lambda i, j, kk: (i, kk))
    if b_view is not None:
        b_spec = b_view[2]
    else:
        b_spec = pl.BlockSpec((tn, tk), lambda i, j, kk: (j, kk)) if tb else pl.BlockSpec((tk, tn), lambda i, j, kk: (kk, j))
    mn_spec = pl.BlockSpec((tm, tn), lambda i, j, kk: (i, j))
    o_shape, o_spec, into = ((m, n), mn_spec, None) if out_view is None else out_view
    ins, specs, aliases = [a, b, *post_ins], [a_spec, b_spec] + [mn_spec] * n_post, {}
    if into is not None:
        aliases = {len(ins): 0}
        ins.append(into)
        specs.append(pl.BlockSpec(memory_space=pl.ANY))
    out_shape = [jax.ShapeDtypeStruct(o_shape, out_dtype)]
    out_specs = [o_spec]
    if extra_out:
        out_shape.append(jax.ShapeDtypeStruct((m, n), extra_out[1]))
        out_specs.append(mn_spec)
    res = hosted_call(
        riders, body, name=name, grid=(m // tm, n // tn, nk), in_specs=specs, out_specs=out_specs, out_shape=out_shape,
        input_output_aliases=aliases, scratch_shapes=[pltpu.VMEM((tm, tn), F32)],
        compiler_params=_params(("parallel", "parallel", "arbitrary")),
    )(*ins)
    return res if extra_out else res[0]


def rows_call(fn, row_ins, full_ins, row_outs, acc_outs, *, tm, name, riders=()):
    row_ins = [r if isinstance(r, tuple) else (r, r.shape[-1], 0) for r in row_ins]
    t = row_ins[0][0].shape[-2]
    tm = min(tm, t)
    n_in = len(row_ins) + len(full_ins)
    n_row = len(row_outs)

    def body(*refs):
        res = fn(*[[r[h] for h in range(r.shape[0])] if (i < len(row_ins) and len(r.shape) == 3) else r[...]
                   for i, r in enumerate(refs[:n_in])])
        res = res if isinstance(res, (tuple, list)) else (res,)
        outs = refs[n_in:]
        for ref, val in zip(outs[:n_row], res[:n_row]):
            if len(ref.shape) == 3:
                for h, vh in enumerate(val):
                    ref[h] = vh.astype(ref.dtype)
            else:
                ref[...] = val.astype(ref.dtype)
        first = pl.program_id(0) == 0
        for ref, val in zip(outs[n_row:], res[n_row:]):
            @pl.when(first)
            def _(ref=ref, val=val):
                ref[...] = val

            @pl.when(jnp.logical_not(first))
            def _(ref=ref, val=val):
                ref[...] += val

    def full_spec(shape):
        return pl.BlockSpec(shape, lambda i, nd=len(shape): (0,) * nd)

    def row_spec(lead, w, cb):
        if lead is None:
            return pl.BlockSpec((tm, w), lambda i: (i, cb))
        return pl.BlockSpec((lead, tm, w), lambda i: (0, i, cb))

    def lead_cols(c):
        return c if isinstance(c, tuple) else (None, c)

    in_specs = [row_spec(a.shape[0] if a.ndim == 3 else None, w, cb) for (a, w, cb) in row_ins]
    in_specs += [full_spec(f.shape) for f in full_ins]
    out_specs = [row_spec(*lead_cols(c), 0) for c, _ in row_outs] + [full_spec(s) for s in acc_outs]
    out_shape = [jax.ShapeDtypeStruct(tuple(d for d in (lead_cols(c)[0], t, lead_cols(c)[1]) if d is not None), dt)
                 for c, dt in row_outs] + [jax.ShapeDtypeStruct(s, F32) for s in acc_outs]
    res = hosted_call(
        riders, body, name=name, grid=(t // tm,), in_specs=in_specs, out_specs=out_specs, out_shape=out_shape,
        compiler_params=_params(("arbitrary",)),
    )(*[r[0] for r in row_ins], *full_ins)
    return res


def vjp_rows(fn, n_diff_row, row_diff_full):
    def bwd(*args, n_row, n_ct):
        prim_rows = args[:n_row]
        cts = args[n_row:n_row + n_ct]
        fulls = args[n_row + n_ct:]
        _, vjp = jax.vjp(fn, *prim_rows, *fulls)
        g = vjp(cts[0] if n_ct == 1 else tuple(cts))
        out = list(g[:n_diff_row])
        out += [gf for gf, d in zip(g[n_row:], row_diff_full) if d]
        return tuple(out)
    return bwd


def _shift_down(x, s):
    if s == 0:
        return x
    t = lax.broadcasted_iota(jnp.int32, x.shape, 0)
    return jnp.where(t >= s, pltpu.roll(x, s, 0), 0.0)


def _shift_up(x, s):
    if s == 0:
        return x
    n = x.shape[0]
    t = lax.broadcasted_iota(jnp.int32, x.shape, 0)
    return jnp.where(t < n - s, pltpu.roll(x, n - s, 0), 0.0)


def _conv(x, w_ref):
    return sum(w_ref[pl.ds(j, 1), :] * _shift_down(x, CONV_WIDTH - 1 - j) for j in range(CONV_WIDTH))


_DN_POST = (lambda c: l2n(jax.nn.silu(c)) * SCALE, lambda c: l2n(jax.nn.silu(c)), jax.nn.silu)


def dn_prep_fwd(proj, conv_w, *, name, riders=()):
    t = proj.shape[0]

    def body(xq, xk, xv, wq, wk, wv, oq, ok, ov):
        for x_ref, w_ref, o_ref, post in zip((xq, xk, xv), (wq, wk, wv), (oq, ok, ov), _DN_POST):
            o_ref[...] = post(_conv(x_ref[...], w_ref))

    x_specs = [pl.BlockSpec((t, HEAD_DIM), lambda h, g=g: (0, g * N_HEADS + h)) for g in range(3)]
    w_specs = [pl.BlockSpec((CONV_WIDTH, HEAD_DIM), lambda h, g=g: (0, g * N_HEADS + h)) for g in range(3)]
    o_spec = pl.BlockSpec((None, t, HEAD_DIM), lambda h: (h, 0, 0))
    return hosted_call(
        riders, body, name=name, grid=(N_HEADS,), in_specs=x_specs + w_specs, out_specs=[o_spec] * 3,
        out_shape=[jax.ShapeDtypeStruct((N_HEADS, t, HEAD_DIM), F32)] * 3, compiler_params=_params(("parallel",)),
    )(proj, proj, proj, conv_w, conv_w, conv_w)


def dn_prep_bwd(proj, conv_w, dq, dk, dv, *, name, riders=()):
    t = proj.shape[0]

    def body(xq, xk, xv, wq, wk, wv, gq, gk, gv, dxq, dxk, dxv, dwq, dwk, dwv):
        for x_ref, w_ref, g_ref, dx_ref, dw_ref, post in zip(
                (xq, xk, xv), (wq, wk, wv), (gq, gk, gv), (dxq, dxk, dxv), (dwq, dwk, dwv), _DN_POST):
            x = x_ref[...]
            _, vjp = jax.vjp(post, _conv(x, w_ref))
            dc, = vjp(g_ref[...])
            dx = sum(w_ref[pl.ds(j, 1), :] * _shift_up(dc, CONV_WIDTH - 1 - j) for j in range(CONV_WIDTH))
            dx_ref[...] = dx.astype(dx_ref.dtype)
            for j in range(CONV_WIDTH):
                dw_ref[pl.ds(j, 1), :] = jnp.sum(dc * _shift_down(x, CONV_WIDTH - 1 - j), axis=0, keepdims=True)

    x_specs = [pl.BlockSpec((t, HEAD_DIM), lambda h, g=g: (0, g * N_HEADS + h)) for g in range(3)]
    w_specs = [pl.BlockSpec((CONV_WIDTH, HEAD_DIM), lambda h, g=g: (0, g * N_HEADS + h)) for g in range(3)]
    g_spec = pl.BlockSpec((None, t, HEAD_DIM), lambda h: (h, 0, 0))
    dx_spec = pl.BlockSpec((t, HEAD_DIM), lambda h: (0, h))
    dw_spec = pl.BlockSpec((CONV_WIDTH, HEAD_DIM), lambda h: (0, h))
    return hosted_call(
        riders, body, name=name, grid=(N_HEADS,), in_specs=x_specs + w_specs + [g_spec] * 3, out_specs=[dx_spec] * 3 + [dw_spec] * 3,
        out_shape=[jax.ShapeDtypeStruct((t, D_MODEL), BF16)] * 3 + [jax.ShapeDtypeStruct((CONV_WIDTH, D_MODEL), F32)] * 3,
        compiler_params=_params(("parallel",)),
    )(proj, proj, proj, conv_w, conv_w, conv_w, dq, dk, dv)


INTRA_CHUNKS = 4


def _lane_column(x, lane_index):
    lane = lax.broadcasted_iota(jnp.int32, x.shape, 1)
    return jnp.sum(jnp.where(lane == lane_index, x, 0.0), axis=1, keepdims=True)


def _head_columns(g, first_lane):
    return jnp.concatenate([_lane_column(g, first_lane + h)[None] for h in range(N_HEADS)], axis=0)


def _intra_of_gates(q, k, v, gates):
    nb = N_HEADS * (gates.shape[0] // CHUNK)

    def chunks(x):
        return x.reshape(nb, CHUNK, x.shape[-1])

    res = delta_intra(chunks(q), chunks(k), chunks(v), chunks(_head_columns(gates, 0)), chunks(_head_columns(gates, N_HEADS)))
    return tuple(x.reshape(N_HEADS, -1, x.shape[-1]) for x in res)


def _step_of_gates(s, q, k, gates, u, w, qk):
    return delta_step(s, q, k, _head_columns(gates, 0), u, w, qk)


def _head_major(rows, w, index):
    return pl.BlockSpec((N_HEADS, rows, w), lambda i: (0, index(i), 0))


def delta_intra_fwd(q, k, v, gates, *, name, riders=()):
    t = q.shape[1]
    rows = min(INTRA_CHUNKS, t // CHUNK) * CHUNK

    def body(q_ref, k_ref, v_ref, g_ref, u_ref, w_ref, qk_ref):
        for ref, val in zip((u_ref, w_ref, qk_ref), _intra_of_gates(q_ref[...], k_ref[...], v_ref[...], g_ref[...])):
            ref[...] = val

    x_spec, qk_spec = (_head_major(rows, w, lambda i: i) for w in (HEAD_DIM, CHUNK))
    g_spec = pl.BlockSpec((rows, LANES), lambda i: (i, 0))
    return hosted_call(
        riders, body, name=name, grid=(t // rows,), in_specs=[x_spec] * 3 + [g_spec], out_specs=[x_spec, x_spec, qk_spec],
        out_shape=[jax.ShapeDtypeStruct((N_HEADS, t, HEAD_DIM), F32)] * 2 + [jax.ShapeDtypeStruct((N_HEADS, t, CHUNK), F32)],
        compiler_params=_params(("parallel",)),
    )(q, k, v, gates)


def delta_seq_fwd(q, k, gates, u, w, qk, *, name, riders=()):
    t = q.shape[1]
    nc = t // CHUNK

    def body(q_ref, k_ref, g_ref, u_ref, w_ref, qk_ref, o_ref, s0_ref, s_ref):
        @pl.when(pl.program_id(0) == 0)
        def _():
            s_ref[...] = jnp.zeros_like(s_ref)

        s = s_ref[...]
        s0_ref[...] = s
        o, s_new = _step_of_gates(s, q_ref[...], k_ref[...], g_ref[...], u_ref[...], w_ref[...], qk_ref[...])
        o_ref[...] = o
        s_ref[...] = s_new

    x_spec, qk_spec = (_head_major(CHUNK, w, lambda c: c) for w in (HEAD_DIM, CHUNK))
    g_spec = pl.BlockSpec((CHUNK, LANES), lambda c: (c, 0))
    s_spec = pl.BlockSpec((N_HEADS, None, HEAD_DIM, HEAD_DIM), lambda c: (0, c, 0, 0))
    return hosted_call(
        riders, body, name=name, grid=(nc,), in_specs=[x_spec, x_spec, g_spec, x_spec, x_spec, qk_spec], out_specs=[x_spec, s_spec],
        out_shape=[jax.ShapeDtypeStruct((N_HEADS, t, HEAD_DIM), F32),
                   jax.ShapeDtypeStruct((N_HEADS, nc, HEAD_DIM, HEAD_DIM), F32)],
        scratch_shapes=[pltpu.VMEM((N_HEADS, HEAD_DIM, HEAD_DIM), F32)],
        compiler_params=_params(("arbitrary",)),
    )(q, k, gates, u, w, qk)


def delta_seq_bwd(q, k, gates, u, w, qk, s0, do, *, name, riders=()):
    t = q.shape[1]
    nc = t // CHUNK

    def body(q_ref, k_ref, g_ref, u_ref, w_ref, qk_ref, s0_ref, do_ref,
             dq_ref, dk_ref, dg_ref, du_ref, dw_ref, dqk_ref, ds_ref):
        @pl.when(pl.program_id(0) == 0)
        def _():
            ds_ref[...] = jnp.zeros_like(ds_ref)

        _, vjp = jax.vjp(_step_of_gates, s0_ref[...], q_ref[...], k_ref[...], g_ref[...], u_ref[...], w_ref[...], qk_ref[...])
        ds, dq, dk, dg, du, dw, dqk = vjp((do_ref[...], ds_ref[...]))
        for ref, val in zip((ds_ref, dq_ref, dk_ref, dg_ref, du_ref, dw_ref, dqk_ref), (ds, dq, dk, dg, du, dw, dqk)):
            ref[...] = val

    x_spec, qk_spec = (_head_major(CHUNK, w, lambda c: nc - 1 - c) for w in (HEAD_DIM, CHUNK))
    g_spec = pl.BlockSpec((CHUNK, LANES), lambda c: (nc - 1 - c, 0))
    s_spec = pl.BlockSpec((N_HEADS, None, HEAD_DIM, HEAD_DIM), lambda c: (0, nc - 1 - c, 0, 0))
    head_shape = [jax.ShapeDtypeStruct((N_HEADS, t, w_), F32) for w_ in (HEAD_DIM, HEAD_DIM, HEAD_DIM, HEAD_DIM, CHUNK)]
    return hosted_call(
        riders, body, name=name, grid=(nc,), in_specs=[x_spec, x_spec, g_spec, x_spec, x_spec, qk_spec, s_spec, x_spec],
        out_specs=[x_spec, x_spec, g_spec, x_spec, x_spec, qk_spec],
        out_shape=head_shape[:2] + [jax.ShapeDtypeStruct((t, LANES), F32)] + head_shape[2:],
        scratch_shapes=[pltpu.VMEM((N_HEADS, HEAD_DIM, HEAD_DIM), F32)],
        compiler_params=_params(("arbitrary",)),
    )(q, k, gates, u, w, qk, s0, do)


def delta_intra_bwd(q, k, v, gates, du, dw, dqk, dq_s, dk_s, dg_s, *, name, riders=()):
    t = q.shape[1]
    rows = min(INTRA_CHUNKS, t // CHUNK) * CHUNK

    def body(q_ref, k_ref, v_ref, g_ref, du_ref, dw_ref, dqk_ref, dqs_ref, dks_ref, dgs_ref, dq_ref, dk_ref, dv_ref, dg_ref):
        _, vjp = jax.vjp(_intra_of_gates, q_ref[...], k_ref[...], v_ref[...], g_ref[...])
        dq, dk, dv, dg = vjp((du_ref[...], dw_ref[...], dqk_ref[...]))
        dq_ref[...] = dq + dqs_ref[...]
        dk_ref[...] = dk + dks_ref[...]
        dv_ref[...] = dv
        dg_ref[...] = dg + dgs_ref[...]

    x_spec, qk_spec = (_head_major(rows, w, lambda i: i) for w in (HEAD_DIM, CHUNK))
    g_spec = pl.BlockSpec((rows, LANES), lambda i: (i, 0))
    return hosted_call(
        riders, body, name=name, grid=(t // rows,),
        in_specs=[x_spec] * 3 + [g_spec, x_spec, x_spec, qk_spec, x_spec, x_spec, g_spec],
        out_specs=[x_spec] * 3 + [g_spec],
        out_shape=[jax.ShapeDtypeStruct((N_HEADS, t, HEAD_DIM), F32)] * 3 + [jax.ShapeDtypeStruct((t, LANES), F32)],
        compiler_params=_params(("parallel",)),
    )(q, k, v, gates, du, dw, dqk, dq_s, dk_s, dg_s)


_V_BLOCK = 2 * N_HEADS
FOX_GROUPS = 8


def _fox_groups(t):
    nq = t // Q_BLOCK
    per = max(1, nq // FOX_GROUPS)
    return [(g0, per, (g0 + per) * Q_BLOCK) for g0 in range(0, nq, per)]


def fox_attn_fwd(q, k, proj, fq, fk, *, name, riders=()):
    t = q.shape[0]

    def body(q_ref, k_ref, v_ref, fq_ref, fk_ref, o_ref, kb_ref, vb_ref):
        head = pl.program_id(0)
        kb_ref[...] = k_ref[...].astype(BF16)
        vb_ref[...] = v_ref[...].astype(BF16)
        for g0, per, keys in _fox_groups(t):
            def block(j, carry, g0=g0, keys=keys):
                rows = pl.ds(pl.multiple_of((g0 + j) * Q_BLOCK, Q_BLOCK), Q_BLOCK)
                p = fox_probs(q_ref[rows, :].astype(BF16), kb_ref[0:keys, :], _lane_column(fq_ref[rows, :], head),
                              fk_ref[:, 0:keys], (g0 + j) * Q_BLOCK)
                o_ref[rows, :] = jnp.dot(p.astype(BF16), vb_ref[0:keys, :], preferred_element_type=F32)
                return carry
            lax.fori_loop(0, per, block, 0)

    x_spec = pl.BlockSpec((t, HEAD_DIM), lambda h: (0, h))
    v_spec = pl.BlockSpec((t, HEAD_DIM), lambda h: (0, _V_BLOCK + h))
    fq_spec = pl.BlockSpec((t, LANES), lambda h: (0, 0))
    fk_spec = pl.BlockSpec((None, 1, t), lambda h: (h, 0, 0))
    return hosted_call(
        riders, body, name=name, grid=(N_HEADS,), in_specs=[x_spec, x_spec, v_spec, fq_spec, fk_spec], out_specs=x_spec,
        out_shape=jax.ShapeDtypeStruct((t, D_MODEL), F32), scratch_shapes=[pltpu.VMEM((t, HEAD_DIM), BF16)] * 2,
        compiler_params=_params(("parallel",)),
    )(q, k, proj, fq, fk)


def fox_attn_bwd(q, k, proj, fq, fk, do, *, name, riders=()):
    t = q.shape[0]

    def body(q_ref, k_ref, v_ref, fq_ref, fk_ref, do_ref, dq_ref, dk_ref, dv_out_ref, dfq_ref, dfk_ref, kb_ref, vb_ref, dv_ref):
        head = pl.program_id(0)

        @pl.when(head == 0)
        def _():
            dfq_ref[...] = jnp.zeros_like(dfq_ref)

        kb_ref[...] = k_ref[...].astype(BF16)
        vb_ref[...] = v_ref[...].astype(BF16)
        dk_ref[...] = jnp.zeros_like(dk_ref)
        dv_ref[...] = jnp.zeros_like(dv_ref)
        dfk_ref[...] = jnp.zeros_like(dfk_ref)
        nt = (((1,), (1,)), ((), ()))
        tn = (((0,), (0,)), ((), ()))
        for g0, per, keys in _fox_groups(t):
            def block(j, carry, g0=g0, keys=keys):
                rows = pl.ds(pl.multiple_of((g0 + j) * Q_BLOCK, Q_BLOCK), Q_BLOCK)
                qb, dob = q_ref[rows, :].astype(BF16), do_ref[rows, :].astype(BF16)
                kb, vb = kb_ref[0:keys, :], vb_ref[0:keys, :]
                p = fox_probs(qb, kb, _lane_column(fq_ref[rows, :], head), fk_ref[:, 0:keys], (g0 + j) * Q_BLOCK)
                dp = lax.dot_general(dob, vb, nt, preferred_element_type=F32)
                dz = p * (dp - jnp.sum(dp * p, axis=-1, keepdims=True))
                pb, dzb = p.astype(BF16), dz.astype(BF16)
                dq_ref[rows, :] = jnp.dot(dzb, kb, preferred_element_type=F32)
                lane = lax.broadcasted_iota(jnp.int32, (Q_BLOCK, LANES), 1)
                dfq_ref[rows, :] += jnp.where(lane == head, jnp.sum(dz, axis=-1, keepdims=True), 0.0)
                dk_ref[0:keys, :] += lax.dot_general(dzb, qb, tn, preferred_element_type=F32)
                dv_ref[0:keys, :] += lax.dot_general(pb, dob, tn, preferred_element_type=F32)
                dfk_ref[:, 0:keys] -= jnp.sum(dz, axis=0, keepdims=True)
                return carry
            lax.fori_loop(0, per, block, 0)
        dv_out_ref[...] = dv_ref[...].astype(dv_out_ref.dtype)

    x_spec = pl.BlockSpec((t, HEAD_DIM), lambda h: (0, h))
    v_spec = pl.BlockSpec((t, HEAD_DIM), lambda h: (0, _V_BLOCK + h))
    fq_spec = pl.BlockSpec((t, LANES), lambda h: (0, 0))
    fk_spec = pl.BlockSpec((None, 1, t), lambda h: (h, 0, 0))
    return hosted_call(
        riders, body, name=name, grid=(N_HEADS,), in_specs=[x_spec, x_spec, v_spec, fq_spec, fk_spec, x_spec],
        out_specs=[x_spec, x_spec, x_spec, fq_spec, fk_spec],
        out_shape=[jax.ShapeDtypeStruct((t, D_MODEL), F32)] * 2 + [jax.ShapeDtypeStruct((t, D_MODEL), BF16)]
        + [jax.ShapeDtypeStruct((t, LANES), F32), jax.ShapeDtypeStruct((N_HEADS, 1, t), F32)],
        scratch_shapes=[pltpu.VMEM((t, HEAD_DIM), BF16)] * 2 + [pltpu.VMEM((t, HEAD_DIM), F32)],
        compiler_params=_params(("arbitrary",)),
    )(q, k, proj, fq, fk, do)


def memkv_fwd(mem, mnw, wkv, mknw, *, name):
    n = mem.shape[0]

    def body(mem_ref, mnw_ref, w_ref, mknw_ref, mk_ref, mv_ref):
        mk, mv = memkv_fn(mem_ref[...], mnw_ref[...], w_ref[...], mknw_ref[...])
        mk_ref[...] = mk
        mv_ref[...] = mv

    return pl.pallas_call(
        body, name=name, out_shape=[jax.ShapeDtypeStruct((n, MEM_WIDTH), F32)] * 2,
        compiler_params=pltpu.CompilerParams(vmem_limit_bytes=VMEM_LIMIT),
    )(mem, mnw, wkv, mknw)


def memkv_bwd(mem, mnw, wkv, mknw, dmk, dmv, *, name):
    def body(mem_ref, mnw_ref, w_ref, mknw_ref, dmk_ref, dmv_ref, dmnw_ref, dw_ref, dmknw_ref):
        f = functools.partial(memkv_fn, mem_ref[...])
        _, vjp = jax.vjp(f, mnw_ref[...], w_ref[...].astype(F32), mknw_ref[...])
        dmnw, dw, dmknw = vjp((dmk_ref[...], dmv_ref[...]))
        dmnw_ref[...] = dmnw
        dw_ref[...] = dw.astype(dw_ref.dtype)
        dmknw_ref[...] = dmknw

    return pl.pallas_call(
        body, name=name,
        out_shape=[jax.ShapeDtypeStruct(mnw.shape, F32), jax.ShapeDtypeStruct(wkv.shape, BF16), jax.ShapeDtypeStruct(mknw.shape, F32)],
        compiler_params=pltpu.CompilerParams(vmem_limit_bytes=VMEM_LIMIT),
    )(mem, mnw, wkv, mknw, dmk, dmv)


def _row(v, width=None):
    v = v.reshape(1, -1)
    if width is not None and v.shape[1] < width:
        v = jnp.pad(v, ((0, 0), (0, width - v.shape[1])))
    return v


def _norm_fwd(x, w, name, riders=()):
    return rows_call(lambda x, w: rms(x, w), [x], [w], [(D_MODEL, BF16)], [], tm=512, name=name, riders=riders)[0]


def _norm_bwd(x, w, dh, dx_in, name, riders=()):
    def fn(x, dh, dx_in, w):
        _, vjp = jax.vjp(rms, x, w)
        dx, dw = vjp(dh)
        return dx + dx_in, dw
    return rows_call(fn, [x, dh, dx_in], [w], [(D_MODEL, F32)], [(1, D_MODEL)], tm=512, name=name, riders=riders)


FF_PIECE = D_FF // N_DEV


def _add(r, x):
    return r + x


def _piece(rows, cols, index):
    return pl.BlockSpec((None, rows, cols), lambda i, j, kk: (index(i, j, kk), 0, 0))


def _two_pieces(rows, cols, index):
    return pl.BlockSpec((2, rows, cols), lambda i, j, kk: (index(i, j, kk), 0, 0))


def _mlp_fwd(x, n2w, w1, w2, layer, riders=()):
    riders = list(riders) + [None, None]
    h2 = _norm_fwd(x, n2w, f"norm2_fwd_{layer}")
    u, a1 = matmul(h2, w1, name=f"mlp1_fwd_{layer}", tiles=(None, FF_PIECE, D_MODEL),
                   extra_out=(lambda u: jnp.square(jnp.maximum(u, 0.0)), BF16),
                   b_view=(D_MODEL, D_FF, _piece(D_MODEL, FF_PIECE, lambda i, j, kk: j)), riders=riders[0])
    y = matmul(a1, w2, name=f"mlp2_fwd_{layer}", post=_add, post_ins=[x], tiles=(None, D_MODEL, 2 * FF_PIECE),
               b_view=(D_FF, D_MODEL, _two_pieces(FF_PIECE, D_MODEL, lambda i, j, kk: kk)), riders=riders[1])
    return y, (x, h2, u, a1)


def pair_sum(g, got, *, name):
    _, rows, cols = g.shape
    tile = _pick(rows, (512, 256, 128))
    c = lax.axis_index("c").astype(jnp.int32).reshape(1)

    def body(c_ref, a_ref, b_ref, o_ref):
        o_ref[...] = (a_ref[...].astype(F32) + b_ref[...].astype(F32)).astype(o_ref.dtype)

    grid_spec = pltpu.PrefetchScalarGridSpec(
        num_scalar_prefetch=1, grid=(4, rows // tile),
        in_specs=[pl.BlockSpec((None, tile, cols), lambda k, i, c_ref: (2 * k + c_ref[0], i, 0)),
                  pl.BlockSpec((None, tile, cols), lambda k, i, c_ref: (k, i, 0))],
        out_specs=pl.BlockSpec((None, tile, cols), lambda k, i, c_ref: (k, i, 0)))
    return pl.pallas_call(
        body, name=name, grid_spec=grid_spec, out_shape=jax.ShapeDtypeStruct((4, rows, cols), g.dtype),
        compiler_params=_params(("parallel", "parallel")),
    )(c, g, got)


def chip_sums(names, pieces, gots):
    return [pair_sum(a, got, name=f"grads_pair_sum_{n}") for n, a, got in zip(names, pieces, gots)]


def _mlp_bwd(dy, res, n2w, w1, w2, layer, riders=()):
    x, h2, u, a1 = res
    du = matmul(dy, w2, tb=True, name=f"mlp2_dx_{layer}", out_dtype=BF16, tiles=(None, 2 * FF_PIECE, D_MODEL),
                post=lambda r, u: r * (2.0 * jnp.maximum(u, 0.0)), post_ins=[u],
                b_view=(D_MODEL, D_FF, _two_pieces(FF_PIECE, D_MODEL, lambda i, j, kk: j)), riders=riders)
    dw2 = matmul(a1, dy, ta=True, name=f"mlp2_dw_{layer}", out_dtype=BF16, tiles=(FF_PIECE, D_MODEL, None), out_view=(
        w2.shape, _piece(FF_PIECE, D_MODEL, lambda i, j, kk: i), None))
    sib2 = sibling_rider([dw2])
    dh2 = matmul(du, w1, tb=True, name=f"mlp1_dx_{layer}", tiles=(None, D_MODEL, FF_PIECE),
                 b_view=(D_FF, D_MODEL, _piece(D_MODEL, FF_PIECE, lambda i, j, kk: kk)), riders=[sib2])
    dw1 = matmul(h2, du, ta=True, name=f"mlp1_dw_{layer}", out_dtype=BF16, tiles=(D_MODEL, FF_PIECE, None), out_view=(
        w1.shape, _piece(D_MODEL, FF_PIECE, lambda i, j, kk: j), None))
    sib1 = sibling_rider([dw1])
    dx, dn2w = _norm_bwd(x, n2w, dh2, dy, f"norm2_bwd_{layer}", riders=[sib1])
    return dx, dw1, dw2, dn2w, sib1, sib2


def _in_proj_bwd(h, dmain, dsmall, w_main, w_small, tag):
    dh = matmul(dmain, w_main, tb=True, name=f"inproj_dx_main_{tag}")
    dh = matmul(dsmall, w_small, tb=True, post=_add, post_ins=[dh], name=f"inproj_dx_small_{tag}")
    dw_main = matmul(h, dmain, ta=True, out_dtype=BF16, name=f"inproj_dw_main_{tag}")
    dw_small = matmul(h, dsmall, ta=True, out_dtype=BF16, name=f"inproj_dw_small_{tag}")
    return dh, dw_main, dw_small


def local_step(x, mem, target, w, m, v):
    t = x.shape[0]
    n_mem = mem.shape[0]
    g = {}

    def wire(a):
        return a.astype(BF16)

    (dn_g,), = run_riders([gather_rider([wire(w["dn_w_in"][0])])], name="weights_gather_first")
    dn_main, dn_ab = in_proj_weights(dn_g, DN_IN, 2 * N_HEADS)
    fox_w = wire(w["fox_w_in"][0])
    ride_out = gather_rider([wire(w["w_out"][0]), wire(w["w_out"][1]), w["dn_conv_w"][0]])
    ride_kv = gather_rider([wire(w["w_mem_kv"])])
    ride_mlp1_0 = gather_rider([wire(w["w_mlp1"][0])])
    ride_mlp2_0 = gather_rider([wire(w["w_mlp2"][0])])
    ride_fox_a, ride_fox_b = gather_rider([fox_w[:D_MODEL // 2]]), gather_rider([fox_w[D_MODEL // 2:]])
    ride_mlp_1 = gather_rider([wire(w["w_mlp1"][1]), wire(w["w_mlp2"][1])])
    mnw, mknw = _row(w["mem_norm_w"]), _row(w["mem_k_norm_w"])

    n1w0, n2w0 = _row(w["norm1_w"][0]), _row(w["norm2_w"][0])
    alog, dtb = _row(w["dn_a_log"][0], LANES), _row(w["dn_dt_bias"][0], LANES)
    onw, mqw0 = _row(w["dn_o_norm_w"][0]), _row(w["memq_norm_w"][0])
    x0 = x
    h0 = _norm_fwd(x0, n1w0, "norm1_fwd_0")
    pm0 = matmul(h0, dn_main, name="inproj_main_0", riders=[ride_out])
    w_out0, w_out1 = (a.reshape(OUT_IN, D_MODEL) for a in ride_out.results[:2])
    conv_w = ride_out.results[2].transpose(1, 0, 2).reshape(CONV_WIDTH, 3 * D_MODEL)
    ps0 = matmul(h0, dn_ab, name="inproj_small_0")
    gates = rows_call(dn_gates_fn, [ps0], [alog, dtb], [(LANES, F32)], [], tm=512, name="dn_gates_fwd")[0]
    q0, k0, v0 = dn_prep_fwd(pm0, conv_w, name="dn_prep_fwd", riders=[ride_kv])
    w_kv = ride_kv.results[0].reshape(D_MODEL, D_MODEL)
    mk, mv = memkv_fwd(mem, mnw, w_kv, mknw, name="memkv_fwd")
    u0, w0, qk0 = delta_intra_fwd(q0, k0, v0, gates, name="delta_intra_fwd", riders=[ride_mlp1_0])
    o0, s_start = delta_seq_fwd(q0, k0, gates, u0, w0, qk0, name="delta_seq_fwd", riders=[ride_mlp2_0])
    cat0 = rows_call(dn_out_fn, [o0, (pm0, D_MODEL, 3), (pm0, MEM_WIDTH, 8)], [onw, mqw0, mk, mv],
                     [(D_MODEL + MEM_WIDTH, BF16)], [], tm=256, name="dn_out_fwd")[0]
    (w1_0,), (w2_0,) = ride_mlp1_0.results, ride_mlp2_0.results
    x1 = matmul(cat0, w_out0, post=_add, post_ins=[x0], name="wout_fwd_0")
    x2, mlp_res0 = _mlp_fwd(x1, n2w0, w1_0, w2_0, 0, riders=[[ride_fox_a], [ride_fox_b]])
    fox_main, fox_f = in_proj_weights(
        jnp.concatenate([ride_fox_a.results[0], ride_fox_b.results[0]], axis=1), FOX_IN, N_HEADS)

    n1w1, n2w1 = _row(w["norm1_w"][1]), _row(w["norm2_w"][1])
    fbias = _row(w["fox_f_bias"][0], LANES)
    qnw, knw, mqw1 = _row(w["fox_q_norm_w"][0]), _row(w["fox_k_norm_w"][0]), _row(w["memq_norm_w"][1])
    h1 = _norm_fwd(x2, n1w1, "norm1_fwd_1")
    pm1 = matmul(h1, fox_main, name="inproj_main_1")
    ps1 = matmul(h1, fox_f, name="inproj_small_1")
    fq = rows_call(fox_fcum_fn, [ps1], [fbias], [(LANES, F32)], [], tm=t, name="fox_fcum_fwd")[0]
    fk = fq[:, :N_HEADS].T[:, None, :]
    q1, k1 = rows_call(fox_qk_fn, [(pm1, D_MODEL, 0), (pm1, D_MODEL, 1)], [qnw, knw], [(D_MODEL, F32)] * 2, [], tm=256,
                       name="fox_qk_fwd")
    o1 = fox_attn_fwd(q1, k1, pm1, fq, fk, name="fox_attn_fwd", riders=[ride_mlp_1])
    cat1 = rows_call(fox_out_fn, [o1, (pm1, D_MODEL, 3), (pm1, MEM_WIDTH, 8)], [mqw1, mk, mv],
                     [(D_MODEL + MEM_WIDTH, BF16)], [], tm=256, name="fox_out_fwd")[0]
    w1_1, w2_1 = ride_mlp_1.results
    x3 = matmul(cat1, w_out1, post=_add, post_ins=[x2], name="wout_fwd_1")
    y, mlp_res1 = _mlp_fwd(x3, n2w1, w1_1, w2_1, 1)

    def loss_fn(y, tgt):
        e = y - tgt
        return e * (1.0 / D_MODEL), jnp.sum(jnp.sum(e * e, axis=1, keepdims=True), axis=0, keepdims=True)
    dy, sq = rows_call(loss_fn, [y, target], [], [(D_MODEL, F32)], [(1, 1)], tm=512, name="loss")
    loss = sq[0, 0] * (0.5 / D_MODEL)

    dx3, dw1_1, dw2_1, dn2w1, sib1, sib2 = _mlp_bwd(dy, mlp_res1, n2w1, w1_1, w2_1, 1)
    dcat1 = matmul(dx3, w_out1, tb=True, name="wout_dx_1")
    dwo_1 = matmul(cat1, dx3, ta=True, out_dtype=BF16, name="wout_dw_1").reshape(N_DEV, OUT_IN // N_DEV, D_MODEL)
    sibo = sibling_rider([dwo_1])
    do1, dgate1, dqm1, dmqw1, dmk1, dmv1 = rows_call(
        functools.partial(vjp_rows(fox_out_fn, 3, (True, True, True)), n_row=3, n_ct=1),
        [o1, (pm1, D_MODEL, 3), (pm1, MEM_WIDTH, 8), dcat1], [mqw1, mk, mv],
        [(D_MODEL, F32), (D_MODEL, BF16), (MEM_WIDTH, BF16)], [(1, HEAD_DIM), (n_mem, MEM_WIDTH), (n_mem, MEM_WIDTH)],
        tm=256, name="fox_out_bwd", riders=[sibo])
    ride_l1 = chips_rider(chip_sums(["w_mlp2_1", "w_mlp1_1", "w_out_1"], [dw2_1, dw1_1, dwo_1],
                                    sib2.results + sib1.results + sibo.results))
    dq1, dk1, dv1, dfq, dfk = fox_attn_bwd(q1, k1, pm1, fq, fk, do1, name="fox_attn_bwd", riders=[ride_l1])
    dqraw1, dkraw1, dqnw, dknw = rows_call(
        functools.partial(vjp_rows(fox_qk_fn, 2, (True, True)), n_row=2, n_ct=2),
        [(pm1, D_MODEL, 0), (pm1, D_MODEL, 1), dq1, dk1], [qnw, knw],
        [(D_MODEL, BF16)] * 2, [(1, HEAD_DIM)] * 2, tm=256, name="fox_qk_bwd")
    dfcum = dfq + jnp.pad(dfk[:, 0, :].T, ((0, 0), (0, LANES - N_HEADS)))
    dps1, dfbias = rows_call(
        functools.partial(vjp_rows(fox_fcum_fn, 1, (True,)), n_row=1, n_ct=1),
        [ps1, dfcum], [fbias], [(LANES, F32)], [(1, LANES)], tm=t, name="fox_fcum_bwd")
    dpm1 = jnp.concatenate([dqraw1, dkraw1, dv1, dgate1, dqm1], axis=1)
    dh1, dwmain1, dwsmall1 = _in_proj_bwd(h1, dpm1, dps1, fox_main, fox_f, "1")
    g_fox = in_proj_pieces(dwmain1, dwsmall1, N_HEADS, FOX_IN)
    sibf = sibling_rider([g_fox])
    dx2, dn1w1 = _norm_bwd(x2, n1w1, dh1, dx3, "norm1_bwd_1", riders=[sibf])
    ride_fox_g = chips_rider(chip_sums(["fox_w_in"], [g_fox], sibf.results))

    dx1, dw1_0, dw2_0, dn2w0, sib1, sib2 = _mlp_bwd(dx2, mlp_res0, n2w0, w1_0, w2_0, 0)
    dcat0 = matmul(dx1, w_out0, tb=True, name="wout_dx_0")
    dwo_0 = matmul(cat0, dx1, ta=True, out_dtype=BF16, name="wout_dw_0").reshape(N_DEV, OUT_IN // N_DEV, D_MODEL)
    sibo = sibling_rider([dwo_0])
    do0, dz0, dqm0, donw, dmqw0, dmk0, dmv0 = rows_call(
        functools.partial(vjp_rows(dn_out_fn, 3, (True, True, True, True)), n_row=3, n_ct=1),
        [o0, (pm0, D_MODEL, 3), (pm0, MEM_WIDTH, 8), dcat0], [onw, mqw0, mk, mv],
        [((N_HEADS, HEAD_DIM), F32), (D_MODEL, BF16), (MEM_WIDTH, BF16)],
        [(1, HEAD_DIM), (1, HEAD_DIM), (n_mem, MEM_WIDTH), (n_mem, MEM_WIDTH)], tm=256, name="dn_out_bwd", riders=[sibo])
    h_l0 = chip_sums(["w_mlp2_0", "w_mlp1_0", "w_out_0"], [dw2_0, dw1_0, dwo_0], sib2.results + sib1.results + sibo.results)
    ride_l0_mlp, ride_l0_out = chips_rider(h_l0[:2]), chips_rider(h_l0[2:])
    dq_s, dk_s, dg_s, du0, dw0, dqk0 = delta_seq_bwd(q0, k0, gates, u0, w0, qk0, s_start, do0, name="delta_seq_bwd",
                                                     riders=[ride_fox_g])
    dq0, dk0, dv0, dgates = delta_intra_bwd(q0, k0, v0, gates, du0, dw0, dqk0, dq_s, dk_s, dg_s,
                                            name="delta_intra_bwd", riders=[ride_l0_mlp])
    dxq, dxk, dxv, dcq, dck, dcv = dn_prep_bwd(pm0, conv_w, dq0, dk0, dv0, name="dn_prep_bwd", riders=[ride_l0_out])
    dconv = jnp.concatenate([dcq, dck, dcv], axis=1)
    dps0, dalog, ddtb = rows_call(
        functools.partial(vjp_rows(dn_gates_fn, 1, (True, True)), n_row=1, n_ct=1),
        [ps0, dgates], [alog, dtb], [(LANES, F32)], [(1, LANES)] * 2, tm=512, name="dn_gates_bwd")
    dpm0 = jnp.concatenate([dxq, dxk, dxv, dz0, dqm0], axis=1)
    dh0, dwmain0, dwsmall0 = _in_proj_bwd(h0, dpm0, dps0, dn_main, dn_ab, "0")
    g_dn = in_proj_pieces(dwmain0, dwsmall0, 2 * N_HEADS, DN_IN)
    g_conv = dconv.reshape(CONV_WIDTH, N_DEV, -1).transpose(1, 0, 2).astype(BF16)
    sibd = sibling_rider([g_dn, g_conv])
    grad_x, dn1w0 = _norm_bwd(x0, n1w0, dh0, dx1, "norm1_bwd_0", riders=[sibd])

    dmnw, dwkv, dmknw = memkv_bwd(mem, mnw, w_kv, mknw, dmk0 + dmk1, dmv0 + dmv1, name="memkv_bwd")
    g_kv = dwkv.reshape(N_DEV, D_MODEL // N_DEV, D_MODEL)

    g["mem_norm_w"] = dmnw[0]
    g["mem_k_norm_w"] = dmknw[0]
    g["norm1_w"] = jnp.concatenate([dn1w0, dn1w1], axis=0)
    g["dn_a_log"] = dalog[:, :N_HEADS]
    g["dn_dt_bias"] = ddtb[:, :N_HEADS]
    g["dn_o_norm_w"] = donw
    g["fox_f_bias"] = dfbias[:, :N_HEADS]
    g["fox_q_norm_w"] = dqnw
    g["fox_k_norm_w"] = dknw
    g["memq_norm_w"] = jnp.concatenate([dmqw0, dmqw1], axis=0)
    g["norm2_w"] = jnp.concatenate([dn2w0, dn2w1], axis=0)

    def layers(l0, l1):
        return jnp.stack([l0, l1], axis=1).reshape(4, -1, l0.shape[-1])

    shards = {n: [d[n].reshape(r, c) for d in (w, m, v)] for n, r, c in BIG}
    h_dn, h_conv = chip_sums(["dn_w_in", "dn_conv_w"], [g_dn, g_conv], sibd.results)
    quarter = D_MODEL // 4
    ride_dn = [chips_rider([h_dn[:, i * quarter:(i + 1) * quarter]]) for i in range(4)]
    ride_conv, sibk, ride_small = chips_rider([h_conv]), sibling_rider([g_kv]), gather_rider([pack_small(g, last=loss)])
    out = {}
    out["w_mlp1"] = adamw(layers(ride_l0_mlp.results[1], ride_l1.results[1]), *shards["w_mlp1"], name="adamw_w_mlp1",
                          riders=[ride_dn[0], ride_conv, sibk, ride_small])
    ride_kv_g = chips_rider(chip_sums(["w_mem_kv"], [g_kv], sibk.results))
    out["w_mlp2"] = adamw(layers(ride_l0_mlp.results[0], ride_l1.results[0]), *shards["w_mlp2"], name="adamw_w_mlp2",
                          riders=[ride_dn[1], ride_kv_g])
    out["w_out"] = adamw(layers(ride_l0_out.results[0], ride_l1.results[2]), *shards["w_out"], name="adamw_w_out",
                         riders=[ride_dn[2]])
    out["fox_w_in"] = adamw(ride_fox_g.results[0], *shards["fox_w_in"], name="adamw_fox_w_in", riders=[ride_dn[3]])
    out["dn_w_in"] = adamw(jnp.concatenate([r.results[0] for r in ride_dn], axis=1), *shards["dn_w_in"], name="adamw_dn_w_in")
    out["dn_conv_w"] = adamw(ride_conv.results[0], *shards["dn_conv_w"], name="adamw_dn_conv_w")
    out["w_mem_kv"] = adamw(ride_kv_g.results[0], *shards["w_mem_kv"], name="adamw_w_mem_kv")
    small = adamw(ride_small.results[0], pack_small(w), pack_small(m), pack_small(v), name="adamw_small")
    loss = small[0][-1, -1]
    return loss, grad_x, {n: [o.reshape(w[n].shape) for o in outs] for n, outs in out.items()}, small


WEIGHTS = ["mem_norm_w", "w_mem_kv", "mem_k_norm_w", "norm1_w", "dn_w_in", "dn_conv_w", "dn_a_log", "dn_dt_bias",
           "dn_o_norm_w", "fox_w_in", "fox_f_bias", "fox_q_norm_w", "fox_k_norm_w", "memq_norm_w", "w_out", "norm2_w",
           "w_mlp1", "w_mlp2"]
DN_IN = 4 * D_MODEL + 2 * N_HEADS + MEM_WIDTH
FOX_IN = 4 * D_MODEL + N_HEADS + MEM_WIDTH
GATE_END = 4 * D_MODEL
OUT_IN = D_MODEL + MEM_WIDTH
BIG = [("w_mem_kv", D_MODEL // N_DEV, D_MODEL), ("dn_w_in", D_MODEL, DN_IN // N_DEV), ("fox_w_in", D_MODEL, FOX_IN // N_DEV),
       ("dn_conv_w", CONV_WIDTH, 3 * D_MODEL // N_DEV), ("w_out", 2 * OUT_IN // N_DEV, D_MODEL),
       ("w_mlp1", 2 * D_MODEL, FF_PIECE), ("w_mlp2", 2 * FF_PIECE, D_MODEL)]
SMALL = [("mem_norm_w", (D_MODEL,), D_MODEL), ("mem_k_norm_w", (HEAD_DIM,), HEAD_DIM), ("norm1_w", (2, D_MODEL), 2 * D_MODEL),
         ("dn_a_log", (1, N_HEADS), LANES), ("dn_dt_bias", (1, N_HEADS), LANES), ("dn_o_norm_w", (1, HEAD_DIM), HEAD_DIM),
         ("fox_f_bias", (1, N_HEADS), LANES), ("fox_q_norm_w", (1, HEAD_DIM), HEAD_DIM), ("fox_k_norm_w", (1, HEAD_DIM), HEAD_DIM),
         ("memq_norm_w", (2, HEAD_DIM), 2 * HEAD_DIM), ("norm2_w", (2, D_MODEL), 2 * D_MODEL)]
SMALL_ROWS = 16


def pack_small(p, last=None):
    flat = jnp.concatenate([jnp.pad(p[n].reshape(-1), (0, ln - math.prod(sh))) for n, sh, ln in SMALL])
    tail = jnp.zeros((SMALL_ROWS * PACK_W - flat.shape[0],), F32)
    if last is not None:
        tail = jnp.concatenate([tail[:-1], last.reshape(1)])
    return jnp.concatenate([flat, tail]).reshape(SMALL_ROWS, PACK_W)


def unpack_small(pk):
    flat, off, out = pk.reshape(-1), 0, {}
    for n, sh, ln in SMALL:
        out[n] = flat[off:off + math.prod(sh)].reshape(sh)
        off += ln
    return out


def in_proj_weights(gathered, width, n_small):
    full = gathered.transpose(1, 0, 2).reshape(D_MODEL, width)
    main = jnp.concatenate([full[:, :GATE_END], full[:, GATE_END + n_small:]], axis=1)
    return main, jnp.pad(full[:, GATE_END:GATE_END + n_small], ((0, 0), (0, LANES - n_small)))


def in_proj_pieces(d_main, d_small, n_small, width):
    full = jnp.concatenate([d_main[:, :GATE_END], d_small[:, :n_small], d_main[:, GATE_END:]], axis=1)
    return full.reshape(D_MODEL, N_DEV, width // N_DEV).transpose(1, 0, 2)


def adamw(parts, w, m, v, *, name, riders=()):
    n, rows, cols = parts.shape
    tile = _pick(rows, (512, 256, 128))

    def body(p_ref, w_ref, m_ref, v_ref, g_ref, d_ref, mo_ref, vo_ref):
        g = p_ref[0].astype(F32)
        for i in range(1, n):
            g = g + p_ref[i].astype(F32)
        m_new = ADAM_B1 * m_ref[...] + (1.0 - ADAM_B1) * g
        v_new = ADAM_B2 * v_ref[...] + (1.0 - ADAM_B2) * jnp.square(g)
        m_hat = m_new / (1.0 - ADAM_B1 ** ADAM_STEP)
        v_hat = v_new / (1.0 - ADAM_B2 ** ADAM_STEP)
        g_ref[...] = g
        d_ref[...] = -ADAM_LR * (m_hat / (jnp.sqrt(v_hat) + ADAM_EPS) + ADAM_WD * w_ref[...])
        mo_ref[...] = m_new
        vo_ref[...] = v_new

    spec = pl.BlockSpec((tile, cols), lambda i: (i, 0))
    return hosted_call(
        riders, body, name=name, grid=(rows // tile,),
        in_specs=[pl.BlockSpec((n, tile, cols), lambda i: (0, i, 0)), spec, spec, spec], out_specs=[spec] * 4,
        out_shape=[jax.ShapeDtypeStruct((rows, cols), F32)] * 4, compiler_params=_params(("parallel",)),
    )(parts, w, m, v)


def kernel(x, mem, mem_norm_w, w_mem_kv, mem_k_norm_w, norm1_w, dn_w_in, dn_conv_w, dn_a_log, dn_dt_bias, dn_o_norm_w, fox_w_in, fox_f_bias, fox_q_norm_w, fox_k_norm_w, memq_norm_w, w_out, norm2_w, w_mlp1, w_mlp2, loss_target, m_mem_norm_w, m_w_mem_kv, m_mem_k_norm_w, m_norm1_w, m_dn_w_in, m_dn_conv_w, m_dn_a_log, m_dn_dt_bias, m_dn_o_norm_w, m_fox_w_in, m_fox_f_bias, m_fox_q_norm_w, m_fox_k_norm_w, m_memq_norm_w, m_w_out, m_norm2_w, m_w_mlp1, m_w_mlp2, v_mem_norm_w, v_w_mem_kv, v_mem_k_norm_w, v_norm1_w, v_dn_w_in, v_dn_conv_w, v_dn_a_log, v_dn_dt_bias, v_dn_o_norm_w, v_fox_w_in, v_fox_f_bias, v_fox_q_norm_w, v_fox_k_norm_w, v_memq_norm_w, v_w_out, v_norm2_w, v_w_mlp1, v_w_mlp2):
    p = dict(mem_norm_w=mem_norm_w, w_mem_kv=w_mem_kv, mem_k_norm_w=mem_k_norm_w, norm1_w=norm1_w, dn_w_in=dn_w_in,
             dn_conv_w=dn_conv_w, dn_a_log=dn_a_log, dn_dt_bias=dn_dt_bias, dn_o_norm_w=dn_o_norm_w, fox_w_in=fox_w_in,
             fox_f_bias=fox_f_bias, fox_q_norm_w=fox_q_norm_w, fox_k_norm_w=fox_k_norm_w, memq_norm_w=memq_norm_w,
             w_out=w_out, norm2_w=norm2_w, w_mlp1=w_mlp1, w_mlp2=w_mlp2)
    pm = dict(mem_norm_w=m_mem_norm_w, w_mem_kv=m_w_mem_kv, mem_k_norm_w=m_mem_k_norm_w, norm1_w=m_norm1_w,
              dn_w_in=m_dn_w_in, dn_conv_w=m_dn_conv_w, dn_a_log=m_dn_a_log, dn_dt_bias=m_dn_dt_bias,
              dn_o_norm_w=m_dn_o_norm_w, fox_w_in=m_fox_w_in, fox_f_bias=m_fox_f_bias, fox_q_norm_w=m_fox_q_norm_w,
              fox_k_norm_w=m_fox_k_norm_w, memq_norm_w=m_memq_norm_w, w_out=m_w_out, norm2_w=m_norm2_w, w_mlp1=m_w_mlp1,
              w_mlp2=m_w_mlp2)
    pv = dict(mem_norm_w=v_mem_norm_w, w_mem_kv=v_w_mem_kv, mem_k_norm_w=v_mem_k_norm_w, norm1_w=v_norm1_w,
              dn_w_in=v_dn_w_in, dn_conv_w=v_dn_conv_w, dn_a_log=v_dn_a_log, dn_dt_bias=v_dn_dt_bias,
              dn_o_norm_w=v_dn_o_norm_w, fox_w_in=v_fox_w_in, fox_f_bias=v_fox_f_bias, fox_q_norm_w=v_fox_q_norm_w,
              fox_k_norm_w=v_fox_k_norm_w, memq_norm_w=v_memq_norm_w, w_out=v_w_out, norm2_w=v_norm2_w, w_mlp1=v_w_mlp1,
              w_mlp2=v_w_mlp2)

    loss, grad_x, results, small = local_step(x[0], mem[0], loss_target[0], p, pm, pv)
    small = [unpack_small(o) for o in small]
    groups = [{**small[i], **{n: r[i] for n, r in results.items()}} for i in range(4)]
    return (loss, grad_x[None], *[grp[n] for grp in groups for n in WEIGHTS])
```

```python
import functools
import math

import jax
import jax.numpy as jnp
from jax import lax
from jax.experimental import pallas as pl
from jax.experimental.pallas import tpu as pltpu

F32 = jnp.float32
BF16 = jnp.bfloat16
HIGHEST = lax.Precision.HIGHEST

D_MODEL = 1024
HEAD_DIM = 128
N_HEADS = 8
MEM_HEADS = 4
MEM_WIDTH = MEM_HEADS * HEAD_DIM
D_FF = 4 * D_MODEL
CONV_WIDTH = 4
CHUNK = 64
Q_BLOCK = 128
EPS = 1e-6
SCALE = HEAD_DIM ** -0.5
MAIN_WIDTH = 4 * D_MODEL + MEM_WIDTH
LANES = 128
N_DEV = 8
PACK_W = 512

ADAM_LR = 0.001
ADAM_B1 = 0.9
ADAM_B2 = 0.999
ADAM_EPS = 1e-08
ADAM_WD = 0.01
ADAM_STEP = 10

VMEM_LIMIT = 56 * 2 ** 20
MESH = pl.DeviceIdType.MESH


def _bdot(a, b, dims):
    return lax.dot_general(a.astype(BF16), b.astype(BF16), (dims, ((), ())), preferred_element_type=F32)


@jax.custom_vjp
def mm(a, b):
    return _bdot(a, b, ((1,), (0,)))


@jax.custom_vjp
def mm_nt(a, b):
    return _bdot(a, b, ((1,), (1,)))


@jax.custom_vjp
def mm_tn(a, b):
    return _bdot(a, b, ((0,), (0,)))


mm.defvjp(lambda a, b: (mm(a, b), (a, b)), lambda r, g: (mm_nt(g, r[1]), mm_tn(r[0], g)))
mm_nt.defvjp(lambda a, b: (mm_nt(a, b), (a, b)), lambda r, g: (mm(g, r[1]), mm_tn(g, r[0])))
mm_tn.defvjp(lambda a, b: (mm_tn(a, b), (a, b)), lambda r, g: (mm_nt(r[1], g), mm(r[0], g)))


def hdot(a, b):
    return jnp.dot(a, b, precision=HIGHEST, preferred_element_type=F32)


def rms(x, w):
    return x * lax.rsqrt(jnp.mean(x * x, axis=-1, keepdims=True) + EPS) * w


def l2n(x):
    return x * lax.rsqrt(jnp.sum(x * x, axis=-1, keepdims=True) + EPS)


def _iota2(n, m):
    return lax.broadcasted_iota(jnp.int32, (n, m), 0), lax.broadcasted_iota(jnp.int32, (n, m), 1)


def _lower_ones(n):
    r, c = _iota2(n, n)
    return jnp.where(r >= c, 1.0, 0.0).astype(F32)


def _last_row(x):
    r = lax.broadcasted_iota(jnp.int32, x.shape, 0)
    return jnp.sum(jnp.where(r == x.shape[0] - 1, x, 0.0), axis=0, keepdims=True)


def _softmax_rows(z):
    m = lax.stop_gradient(jnp.max(z, axis=-1, keepdims=True))
    e = jnp.exp(z - m)
    return e * (1.0 / jnp.sum(e, axis=-1, keepdims=True))


_BNN = (((2,), (1,)), ((0,), (0,)))
_BNT = (((2,), (2,)), ((0,), (0,)))
_BTN = (((1,), (1,)), ((0,), (0,)))


def _bbdot(a, b, dims):
    return lax.dot_general(a.astype(BF16), b.astype(BF16), dims, preferred_element_type=F32)


@jax.custom_vjp
def bmm(a, b):
    return _bbdot(a, b, _BNN)


@jax.custom_vjp
def bmm_nt(a, b):
    return _bbdot(a, b, _BNT)


@jax.custom_vjp
def bmm_tn(a, b):
    return _bbdot(a, b, _BTN)


@jax.custom_vjp
def bmm_high(a, b):
    return lax.dot_general(a, b, _BNN, precision=lax.Precision.HIGH, preferred_element_type=F32)


bmm.defvjp(lambda a, b: (bmm(a, b), (a, b)), lambda r, g: (bmm_nt(g, r[1]), bmm_tn(r[0], g)))
bmm_nt.defvjp(lambda a, b: (bmm_nt(a, b), (a, b)), lambda r, g: (bmm(g, r[1]), bmm_tn(g, r[0])))
bmm_tn.defvjp(lambda a, b: (bmm_tn(a, b), (a, b)), lambda r, g: (bmm_nt(r[1], g), bmm(r[0], g)))
bmm_high.defvjp(lambda a, b: (bmm_high(a, b), (a, b)), lambda r, g: (bmm_nt(g, r[1]), bmm_tn(r[0], g)))

NEUMANN_HIGH_LEVELS = 2


def inv_unit_lower(a):
    n = a.shape[-1]
    r, c = _iota2(n, n)
    p = jnp.where(r == c, 1.0, 0.0).astype(F32) - a
    ak = a
    for level in range(int(math.log2(n)) - 1):
        dot = bmm_high if level < NEUMANN_HIGH_LEVELS else bmm
        ak = dot(ak, ak)
        p = p + dot(p, ak)
    return p


def delta_intra(q, k, v, gc, beta):
    b, c, _ = q.shape
    r, cc = _iota2(c, c)
    causal = r >= cc
    strict = r > cc
    gi = jnp.broadcast_to(gc, (b, c, c))
    gj = jnp.swapaxes(gi, 1, 2)
    decay = jnp.where(causal, jnp.exp(jnp.where(causal, gi - gj, 0.0)), 0.0)
    kb = k * beta
    a = jnp.where(strict, bmm_nt(kb, k) * decay, 0.0)
    t = inv_unit_lower(a)
    u = bmm(t, v * beta)
    w = bmm(t, kb * jnp.exp(gc))
    qk = jnp.where(causal, bmm_nt(q, k) * decay, 0.0)
    return u, w, qk


def delta_step(s, q, k, gc, u, w, qk):
    v_new = u - bmm(w, s)
    out = bmm(q * jnp.exp(gc), s) + bmm(qk, v_new)
    r = lax.broadcasted_iota(jnp.int32, gc.shape, 1)
    g_last = jnp.sum(jnp.where(r == gc.shape[1] - 1, gc, 0.0), axis=1, keepdims=True)
    k_dec = k * jnp.exp(g_last - gc)
    s_new = s * jnp.exp(g_last) + bmm_tn(k_dec, v_new)
    return out, s_new


def fox_probs(q, k, fq, fk, qpos0):
    s = lax.dot_general(q, k, (((1,), (1,)), ((), ())), preferred_element_type=F32)
    r, c = _iota2(s.shape[0], s.shape[1])
    return _softmax_rows(jnp.where(c <= (r + qpos0), s + (fq - fk), -jnp.inf))


def mem_head(qm, wq, mk, mv):
    p = _softmax_rows(mm_nt(rms(qm, wq) * SCALE, mk))
    return mm(p, mv)


def _heads(x, n):
    return [x[:, h * HEAD_DIM:(h + 1) * HEAD_DIM] for h in range(n)]


def memkv_fn(mem, mnw, wkv, mknw):
    kv = mm(rms(mem, mnw), wkv)
    mk = jnp.concatenate([rms(kh, mknw) for kh in _heads(kv[:, :MEM_WIDTH], MEM_HEADS)], axis=1)
    return mk, kv[:, MEM_WIDTH:]


def dn_gates_fn(ab, alog, dtb):
    g = -jnp.exp(alog) * jax.nn.softplus(ab + dtb)
    low = _lower_ones(CHUNK)
    gc = jnp.concatenate([hdot(low, g[i * CHUNK:(i + 1) * CHUNK]) for i in range(ab.shape[0] // CHUNK)], axis=0)
    lane = lax.broadcasted_iota(jnp.int32, ab.shape, 1)
    return jnp.where(lane < N_HEADS, gc, jax.nn.sigmoid(ab))


def fox_fcum_fn(fp, fbias):
    lf = jax.nn.log_sigmoid(fp + fbias)
    low = _lower_ones(LANES)
    carry = jnp.zeros((1, fp.shape[1]), F32)
    outs = []
    for i in range(fp.shape[0] // LANES):
        cs = hdot(low, lf[i * LANES:(i + 1) * LANES]) + carry
        carry = _last_row(cs)
        outs.append(cs)
    return jnp.concatenate(outs, axis=0)


def fox_qk_fn(qraw, kraw, qnw, knw):
    q = jnp.concatenate([rms(x, qnw) * SCALE for x in _heads(qraw, N_HEADS)], axis=1)
    k = jnp.concatenate([rms(x, knw) for x in _heads(kraw, N_HEADS)], axis=1)
    return q, k


def _mem_out(qm, mqw, mk, mv):
    return [mem_head(a, mqw, b, c) for a, b, c in zip(_heads(qm, MEM_HEADS), _heads(mk, MEM_HEADS), _heads(mv, MEM_HEADS))]


def dn_out_fn(o, z, qm, onw, mqw, mk, mv):
    mix = [rms(a, onw) * jax.nn.silu(b) for a, b in zip(o, _heads(z, N_HEADS))]
    return jnp.concatenate(mix + _mem_out(qm, mqw, mk, mv), axis=1)


def fox_out_fn(o, gate, qm, mqw, mk, mv):
    return jnp.concatenate([o * jax.nn.sigmoid(gate)] + _mem_out(qm, mqw, mk, mv), axis=1)


_HBM = pl.BlockSpec(memory_space=pltpu.HBM)


def _place():
    return lax.axis_index("x"), lax.axis_index("y"), lax.axis_index("c")


class Rider:
    def __init__(self, ins, out_shape, scratch, start, finish):
        self.ins, self.out_shape, self.scratch, self.start, self.finish = list(ins), list(out_shape), list(scratch), start, finish
        self.results = None


def gather_rider(xs):
    n = len(xs)

    def plan(x_refs, out_refs, sems):
        send_sems, recv_sems, local_sems = sems
        x, y, c = _place()
        me, sibling = (x, y, c), (x, y, 1 - c)
        chips = [(1 - x, y), (x, 1 - y), (1 - x, 1 - y)]

        def copy(a, k, block, to, src=None):
            px, py, pc = block
            dst = out_refs[a].at[4 * px + 2 * py + pc]
            return pltpu.make_async_remote_copy(
                src_ref=dst if src is None else src, dst_ref=dst,
                send_sem=send_sems.at[a, k], recv_sem=recv_sems.at[a, k], device_id=to, device_id_type=MESH)

        mine = [pltpu.make_async_copy(x_refs[a], out_refs[a].at[4 * x + 2 * y + c], local_sems.at[a]) for a in range(n)]
        first = [copy(a, 0, me, sibling, src=x_refs[a]) for a in range(n)]
        first += [copy(a, 1 + j, me, (*chip, c), src=x_refs[a]) for j, chip in enumerate(chips) for a in range(n)]
        return copy, me, sibling, chips, mine, first

    def start(x_refs, out_refs, sems):
        _, _, _, _, mine, first = plan(x_refs, out_refs, sems)
        for cp in mine + first:
            cp.start()

    def finish(x_refs, out_refs, sems):
        copy, me, sibling, chips, mine, first = plan(x_refs, out_refs, sems)
        _, _, c = me
        passed = []
        for j, chip in enumerate(chips):
            for a in range(n):
                copy(a, 1 + j, (*chip, c), me).wait_recv()
                passed.append(copy(a, 4 + j, (*chip, c), sibling))
                passed[-1].start()
        for a in range(n):
            copy(a, 0, sibling, me).wait_recv()
        for j, chip in enumerate(chips):
            for a in range(n):
                copy(a, 4 + j, (*chip, 1 - c), me).wait_recv()
        for cp in first + passed:
            cp.wait_send()
        for cp in mine:
            cp.wait()

    return Rider(xs, [jax.ShapeDtypeStruct((N_DEV,) + a.shape, a.dtype) for a in xs],
                 [pltpu.SemaphoreType.DMA((n, 7)), pltpu.SemaphoreType.DMA((n, 7)), pltpu.SemaphoreType.DMA((n,))], start, finish)


def sibling_rider(gs):
    n = len(gs)

    def plan(g_refs, out_refs, sems):
        send_sems, recv_sems = sems
        x, y, c = _place()
        return [pltpu.make_async_remote_copy(
            src_ref=g_refs[a].at[2 * k + 1 - c], dst_ref=out_refs[a].at[k], send_sem=send_sems.at[a, k],
            recv_sem=recv_sems.at[a, k], device_id=(x, y, 1 - c), device_id_type=MESH) for a in range(n) for k in range(4)]

    def start(g_refs, out_refs, sems):
        for cp in plan(g_refs, out_refs, sems):
            cp.start()

    def finish(g_refs, out_refs, sems):
        copies = plan(g_refs, out_refs, sems)
        for cp in copies:
            cp.wait_recv()
        for cp in copies:
            cp.wait_send()

    return Rider(gs, [jax.ShapeDtypeStruct((4,) + g.shape[1:], g.dtype) for g in gs],
                 [pltpu.SemaphoreType.DMA((n, 4)), pltpu.SemaphoreType.DMA((n, 4))], start, finish)


def chips_rider(hs):
    n = len(hs)

    def plan(h_refs, out_refs, sems):
        send_sems, recv_sems, local_sems = sems
        x, y, c = _place()
        mine = 2 * x + y
        chips = [(1 - x, y), (x, 1 - y), (1 - x, 1 - y)]
        keep = [pltpu.make_async_copy(h_refs[a].at[mine], out_refs[a].at[mine], local_sems.at[a]) for a in range(n)]
        sends = [pltpu.make_async_remote_copy(
            src_ref=h_refs[a].at[2 * qx + qy], dst_ref=out_refs[a].at[mine], send_sem=send_sems.at[a, j],
            recv_sem=recv_sems.at[a, j], device_id=(qx, qy, c), device_id_type=MESH)
            for j, (qx, qy) in enumerate(chips) for a in range(n)]
        recvs = [pltpu.make_async_remote_copy(
            src_ref=h_refs[a].at[mine], dst_ref=out_refs[a].at[2 * qx + qy], send_sem=send_sems.at[a, j],
            recv_sem=recv_sems.at[a, j], device_id=(qx, qy, c), device_id_type=MESH)
            for j, (qx, qy) in enumerate(chips) for a in range(n)]
        return keep, sends, recvs

    def start(h_refs, out_refs, sems):
        keep, sends, _ = plan(h_refs, out_refs, sems)
        for cp in keep + sends:
            cp.start()

    def finish(h_refs, out_refs, sems):
        keep, sends, recvs = plan(h_refs, out_refs, sems)
        for cp in recvs:
            cp.wait_recv()
        for cp in sends:
            cp.wait_send()
        for cp in keep:
            cp.wait()

    return Rider(hs, [jax.ShapeDtypeStruct(h.shape, h.dtype) for h in hs],
                 [pltpu.SemaphoreType.DMA((n, 3)), pltpu.SemaphoreType.DMA((n, 3)), pltpu.SemaphoreType.DMA((n,))], start, finish)


def hosted_call(riders, body, *, out_shape, in_specs, out_specs, grid=(), scratch_shapes=(), **kw):
    riders = tuple(riders or ())
    if not riders:
        return pl.pallas_call(body, out_shape=out_shape, in_specs=in_specs, out_specs=out_specs, grid=grid,
                              scratch_shapes=scratch_shapes, **kw)
    single = not isinstance(out_shape, (list, tuple))
    k_out_shape = [out_shape] if single else list(out_shape)
    k_out_specs = [out_specs] if single else list(out_specs)
    n_in, n_out, n_scr = len(in_specs), len(k_out_shape), len(scratch_shapes)
    r_ins = [a for r in riders for a in r.ins]
    r_outs = [s for r in riders for s in r.out_shape]
    r_scr = [s for r in riders for s in r.scratch]

    def full_body(*refs):
        ins = refs[:n_in + len(r_ins)]
        outs = refs[n_in + len(r_ins):n_in + len(r_ins) + n_out + len(r_outs)]
        scr = refs[n_in + len(r_ins) + n_out + len(r_outs):]
        ids = [pl.program_id(d) for d in range(len(grid))]
        first = functools.reduce(jnp.logical_and, [i == 0 for i in ids]) if ids else None
        last = functools.reduce(jnp.logical_and, [i == g - 1 for i, g in zip(ids, grid)]) if ids else None

        def each(method):
            i0, o0, s0 = n_in, n_out, n_scr
            for r in riders:
                getattr(r, method)(ins[i0:i0 + len(r.ins)], outs[o0:o0 + len(r.out_shape)], scr[s0:s0 + len(r.scratch)])
                i0, o0, s0 = i0 + len(r.ins), o0 + len(r.out_shape), s0 + len(r.scratch)

        if first is None:
            each("start")
        else:
            pl.when(first)(lambda: each("start"))
        body(*ins[:n_in], *outs[:n_out], *scr[:n_scr])
        if last is None:
            each("finish")
        else:
            pl.when(last)(lambda: each("finish"))

    call = pl.pallas_call(
        full_body, out_shape=k_out_shape + r_outs, in_specs=list(in_specs) + [_HBM] * len(r_ins),
        out_specs=k_out_specs + [_HBM] * len(r_outs), grid=grid, scratch_shapes=list(scratch_shapes) + r_scr, **kw)

    def run(*args):
        res = call(*args, *r_ins)
        o0 = n_out
        for r in riders:
            r.results = list(res[o0:o0 + len(r.out_shape)])
            o0 += len(r.out_shape)
        return res[0] if single else list(res[:n_out])

    return run


def run_riders(riders, *, name):
    hosted_call(riders, lambda: None, name=name, out_shape=[], in_specs=[], out_specs=[])()
    return [r.results for r in riders]


def _pick(n, cands):
    for c in cands:
        if n % c == 0:
            return c
    return n


def _params(sem):
    return pltpu.CompilerParams(dimension_semantics=sem, vmem_limit_bytes=VMEM_LIMIT)


MATMUL_VMEM_BUDGET = 40 * 2 ** 20


def _matmul_tiles(m, n, k, bytes_a, bytes_b, bytes_mn, fixed):
    fm, fn, fk = fixed if fixed is not None else (None, None, None)

    def options(given, size, cands):
        return [given] if given else ([c for c in cands if size % c == 0] or [size])

    best = None
    for tm in options(fm, m, (2048, 1024, 512, 256, 128)):
        for tn in options(fn, n, (512, 256, 128)):
            for tk in options(fk, k, (2048, 1536, 1024, 512, 256, 128)):
                if 2 * (tm * tk * bytes_a + tk * tn * bytes_b + tm * tn * bytes_mn) + tm * tn * 4 > MATMUL_VMEM_BUDGET:
                    continue
                key = ((m // tm) * (n // tn) * (k // tk), -tk)
                if best is None or key < best[0]:
                    best = (key, (tm, tn, tk))
    assert best is not None, (m, n, k, fixed)
    return best[1]


def matmul(a, b, *, name, ta=False, tb=False, post=None, post_ins=(), extra_out=None, out_dtype=F32, tiles=None,
           b_view=None, out_view=None, riders=()):
    (k, m) = a.shape if ta else a.shape[::-1]
    (kb, n) = b_view[:2] if b_view is not None else (b.shape[::-1] if tb else b.shape)
    assert k == kb, (a.shape, b.shape, ta, tb)
    bytes_mn = sum(p.dtype.itemsize for p in post_ins) + jnp.dtype(out_dtype).itemsize
    bytes_mn += jnp.dtype(extra_out[1]).itemsize if extra_out else 0
    tm, tn, tk = _matmul_tiles(m, n, k, a.dtype.itemsize, b.dtype.itemsize, bytes_mn, tiles)
    nk = k // tk
    dims = ((0,) if ta else (1,), (1,) if tb else (0,))
    n_post = len(post_ins)
    n_out = 2 if extra_out else 1

    def body(*refs):
        a_ref, b_ref = refs[:2]
        post_refs = refs[2:2 + n_post]
        o_refs, acc = refs[-1 - n_out:-1], refs[-1]
        kk = pl.program_id(2)

        @pl.when(kk == 0)
        def _():
            acc[...] = jnp.zeros_like(acc)

        b_tile = b_ref[...]
        acc[...] += _bdot(a_ref[...], b_tile.reshape(-1, b_tile.shape[-1]), dims)

        @pl.when(kk == nk - 1)
        def _():
            r = acc[...]
            if post is not None:
                r = post(r, *[p[...] for p in post_refs])
            o_refs[0][...] = r.astype(out_dtype)
            if extra_out:
                o_refs[1][...] = extra_out[0](r).astype(extra_out[1])

    a_spec = pl.BlockSpec((tk, tm), lambda i, j, kk: (kk, i)) if ta else pl.BlockSpec((tm, tk), lambda i, j, kk: (i, kk))
    if b_view is not None:
        b_spec = b_view[2]
    else:
        b_spec = pl.BlockSpec((tn, tk), lambda i, j, kk: (j, kk)) if tb else pl.BlockSpec((tk, tn), lambda i, j, kk: (kk, j))
    mn_spec = pl.BlockSpec((tm, tn), lambda i, j, kk: (i, j))
    o_shape, o_spec, into = ((m, n), mn_spec, None) if out_view is None else out_view
    ins, specs, aliases = [a, b, *post_ins], [a_spec, b_spec] + [mn_spec] * n_post, {}
    if into is not None:
        aliases = {len(ins): 0}
        ins.append(into)
        specs.append(pl.BlockSpec(memory_space=pl.ANY))
    out_shape = [jax.ShapeDtypeStruct(o_shape, out_dtype)]
    out_specs = [o_spec]
    if extra_out:
        out_shape.append(jax.ShapeDtypeStruct((m, n), extra_out[1]))
        out_specs.append(mn_spec)
    res = hosted_call(
        riders, body, name=name, grid=(m // tm, n // tn, nk), in_specs=specs, out_specs=out_specs, out_shape=out_shape,
        input_output_aliases=aliases, scratch_shapes=[pltpu.VMEM((tm, tn), F32)],
        compiler_params=_params(("parallel", "parallel", "arbitrary")),
    )(*ins)
    return res if extra_out else res[0]


def rows_call(fn, row_ins, full_ins, row_outs, acc_outs, *, tm, name, riders=()):
    row_ins = [r if isinstance(r, tuple) else (r, r.shape[-1], 0) for r in row_ins]
    t = row_ins[0][0].shape[-2]
    tm = min(tm, t)
    n_in = len(row_ins) + len(full_ins)
    n_row = len(row_outs)

    def body(*refs):
        res = fn(*[[r[h] for h in range(r.shape[0])] if (i < len(row_ins) and len(r.shape) == 3) else r[...]
                   for i, r in enumerate(refs[:n_in])])
        res = res if isinstance(res, (tuple, list)) else (res,)
        outs = refs[n_in:]
        for ref, val in zip(outs[:n_row], res[:n_row]):
            if len(ref.shape) == 3:
                for h, vh in enumerate(val):
                    ref[h] = vh.astype(ref.dtype)
            else:
                ref[...] = val.astype(ref.dtype)
        first = pl.program_id(0) == 0
        for ref, val in zip(outs[n_row:], res[n_row:]):
            @pl.when(first)
            def _(ref=ref, val=val):
                ref[...] = val

            @pl.when(jnp.logical_not(first))
            def _(ref=ref, val=val):
                ref[...] += val

    def full_spec(shape):
        return pl.BlockSpec(shape, lambda i, nd=len(shape): (0,) * nd)

    def row_spec(lead, w, cb):
        if lead is None:
            return pl.BlockSpec((tm, w), lambda i: (i, cb))
        return pl.BlockSpec((lead, tm, w), lambda i: (0, i, cb))

    def lead_cols(c):
        return c if isinstance(c, tuple) else (None, c)

    in_specs = [row_spec(a.shape[0] if a.ndim == 3 else None, w, cb) for (a, w, cb) in row_ins]
    in_specs += [full_spec(f.shape) for f in full_ins]
    out_specs = [row_spec(*lead_cols(c), 0) for c, _ in row_outs] + [full_spec(s) for s in acc_outs]
    out_shape = [jax.ShapeDtypeStruct(tuple(d for d in (lead_cols(c)[0], t, lead_cols(c)[1]) if d is not None), dt)
                 for c, dt in row_outs] + [jax.ShapeDtypeStruct(s, F32) for s in acc_outs]
    res = hosted_call(
        riders, body, name=name, grid=(t // tm,), in_specs=in_specs, out_specs=out_specs, out_shape=out_shape,
        compiler_params=_params(("arbitrary",)),
    )(*[r[0] for r in row_ins], *full_ins)
    return res


def vjp_rows(fn, n_diff_row, row_diff_full):
    def bwd(*args, n_row, n_ct):
        prim_rows = args[:n_row]
        cts = args[n_row:n_row + n_ct]
        fulls = args[n_row + n_ct:]
        _, vjp = jax.vjp(fn, *prim_rows, *fulls)
        g = vjp(cts[0] if n_ct == 1 else tuple(cts))
        out = list(g[:n_diff_row])
        out += [gf for gf, d in zip(g[n_row:], row_diff_full) if d]
        return tuple(out)
    return bwd


def _shift_down(x, s):
    if s == 0:
        return x
    t = lax.broadcasted_iota(jnp.int32, x.shape, 0)
    return jnp.where(t >= s, pltpu.roll(x, s, 0), 0.0)


def _shift_up(x, s):
    if s == 0:
        return x
    n = x.shape[0]
    t = lax.broadcasted_iota(jnp.int32, x.shape, 0)
    return jnp.where(t < n - s, pltpu.roll(x, n - s, 0), 0.0)


def _conv(x, w_ref):
    return sum(w_ref[pl.ds(j, 1), :] * _shift_down(x, CONV_WIDTH - 1 - j) for j in range(CONV_WIDTH))


_DN_POST = (lambda c: l2n(jax.nn.silu(c)) * SCALE, lambda c: l2n(jax.nn.silu(c)), jax.nn.silu)


def dn_prep_fwd(proj, conv_w, *, name, riders=()):
    t = proj.shape[0]

    def body(xq, xk, xv, wq, wk, wv, oq, ok, ov):
        for x_ref, w_ref, o_ref, post in zip((xq, xk, xv), (wq, wk, wv), (oq, ok, ov), _DN_POST):
            o_ref[...] = post(_conv(x_ref[...], w_ref))

    x_specs = [pl.BlockSpec((t, HEAD_DIM), lambda h, g=g: (0, g * N_HEADS + h)) for g in range(3)]
    w_specs = [pl.BlockSpec((CONV_WIDTH, HEAD_DIM), lambda h, g=g: (0, g * N_HEADS + h)) for g in range(3)]
    o_spec = pl.BlockSpec((None, t, HEAD_DIM), lambda h: (h, 0, 0))
    return hosted_call(
        riders, body, name=name, grid=(N_HEADS,), in_specs=x_specs + w_specs, out_specs=[o_spec] * 3,
        out_shape=[jax.ShapeDtypeStruct((N_HEADS, t, HEAD_DIM), F32)] * 3, compiler_params=_params(("parallel",)),
    )(proj, proj, proj, conv_w, conv_w, conv_w)


def dn_prep_bwd(proj, conv_w, dq, dk, dv, *, name, riders=()):
    t = proj.shape[0]

    def body(xq, xk, xv, wq, wk, wv, gq, gk, gv, dxq, dxk, dxv, dwq, dwk, dwv):
        for x_ref, w_ref, g_ref, dx_ref, dw_ref, post in zip(
                (xq, xk, xv), (wq, wk, wv), (gq, gk, gv), (dxq, dxk, dxv), (dwq, dwk, dwv), _DN_POST):
            x = x_ref[...]
            _, vjp = jax.vjp(post, _conv(x, w_ref))
            dc, = vjp(g_ref[...])
            dx = sum(w_ref[pl.ds(j, 1), :] * _shift_up(dc, CONV_WIDTH - 1 - j) for j in range(CONV_WIDTH))
            dx_ref[...] = dx.astype(dx_ref.dtype)
            for j in range(CONV_WIDTH):
                dw_ref[pl.ds(j, 1), :] = jnp.sum(dc * _shift_down(x, CONV_WIDTH - 1 - j), axis=0, keepdims=True)

    x_specs = [pl.BlockSpec((t, HEAD_DIM), lambda h, g=g: (0, g * N_HEADS + h)) for g in range(3)]
    w_specs = [pl.BlockSpec((CONV_WIDTH, HEAD_DIM), lambda h, g=g: (0, g * N_HEADS + h)) for g in range(3)]
    g_spec = pl.BlockSpec((None, t, HEAD_DIM), lambda h: (h, 0, 0))
    dx_spec = pl.BlockSpec((t, HEAD_DIM), lambda h: (0, h))
    dw_spec = pl.BlockSpec((CONV_WIDTH, HEAD_DIM), lambda h: (0, h))
    return hosted_call(
        riders, body, name=name, grid=(N_HEADS,), in_specs=x_specs + w_specs + [g_spec] * 3, out_specs=[dx_spec] * 3 + [dw_spec] * 3,
        out_shape=[jax.ShapeDtypeStruct((t, D_MODEL), BF16)] * 3 + [jax.ShapeDtypeStruct((CONV_WIDTH, D_MODEL), F32)] * 3,
        compiler_params=_params(("parallel",)),
    )(proj, proj, proj, conv_w, conv_w, conv_w, dq, dk, dv)


INTRA_CHUNKS = 4


def _lane_column(x, lane_index):
    lane = lax.broadcasted_iota(jnp.int32, x.shape, 1)
    return jnp.sum(jnp.where(lane == lane_index, x, 0.0), axis=1, keepdims=True)


def _head_columns(g, first_lane):
    return jnp.concatenate([_lane_column(g, first_lane + h)[None] for h in range(N_HEADS)], axis=0)


def _intra_of_gates(q, k, v, gates):
    nb = N_HEADS * (gates.shape[0] // CHUNK)

    def chunks(x):
        return x.reshape(nb, CHUNK, x.shape[-1])

    res = delta_intra(chunks(q), chunks(k), chunks(v), chunks(_head_columns(gates, 0)), chunks(_head_columns(gates, N_HEADS)))
    return tuple(x.reshape(N_HEADS, -1, x.shape[-1]) for x in res)


def _step_of_gates(s, q, k, gates, u, w, qk):
    return delta_step(s, q, k, _head_columns(gates, 0), u, w, qk)


def _head_major(rows, w, index):
    return pl.BlockSpec((N_HEADS, rows, w), lambda i: (0, index(i), 0))


def delta_intra_fwd(q, k, v, gates, *, name, riders=()):
    t = q.shape[1]
    rows = min(INTRA_CHUNKS, t // CHUNK) * CHUNK

    def body(q_ref, k_ref, v_ref, g_ref, u_ref, w_ref, qk_ref):
        for ref, val in zip((u_ref, w_ref, qk_ref), _intra_of_gates(q_ref[...], k_ref[...], v_ref[...], g_ref[...])):
            ref[...] = val

    x_spec, qk_spec = (_head_major(rows, w, lambda i: i) for w in (HEAD_DIM, CHUNK))
    g_spec = pl.BlockSpec((rows, LANES), lambda i: (i, 0))
    return hosted_call(
        riders, body, name=name, grid=(t // rows,), in_specs=[x_spec] * 3 + [g_spec], out_specs=[x_spec, x_spec, qk_spec],
        out_shape=[jax.ShapeDtypeStruct((N_HEADS, t, HEAD_DIM), F32)] * 2 + [jax.ShapeDtypeStruct((N_HEADS, t, CHUNK), F32)],
        compiler_params=_params(("parallel",)),
    )(q, k, v, gates)


def delta_seq_fwd(q, k, gates, u, w, qk, *, name, riders=()):
    t = q.shape[1]
    nc = t // CHUNK

    def body(q_ref, k_ref, g_ref, u_ref, w_ref, qk_ref, o_ref, s0_ref, s_ref):
        @pl.when(pl.program_id(0) == 0)
        def _():
            s_ref[...] = jnp.zeros_like(s_ref)

        s = s_ref[...]
        s0_ref[...] = s
        o, s_new = _step_of_gates(s, q_ref[...], k_ref[...], g_ref[...], u_ref[...], w_ref[...], qk_ref[...])
        o_ref[...] = o
        s_ref[...] = s_new

    x_spec, qk_spec = (_head_major(CHUNK, w, lambda c: c) for w in (HEAD_DIM, CHUNK))
    g_spec = pl.BlockSpec((CHUNK, LANES), lambda c: (c, 0))
    s_spec = pl.BlockSpec((N_HEADS, None, HEAD_DIM, HEAD_DIM), lambda c: (0, c, 0, 0))
    return hosted_call(
        riders, body, name=name, grid=(nc,), in_specs=[x_spec, x_spec, g_spec, x_spec, x_spec, qk_spec], out_specs=[x_spec, s_spec],
        out_shape=[jax.ShapeDtypeStruct((N_HEADS, t, HEAD_DIM), F32),
                   jax.ShapeDtypeStruct((N_HEADS, nc, HEAD_DIM, HEAD_DIM), F32)],
        scratch_shapes=[pltpu.VMEM((N_HEADS, HEAD_DIM, HEAD_DIM), F32)],
        compiler_params=_params(("arbitrary",)),
    )(q, k, gates, u, w, qk)


def delta_seq_bwd(q, k, gates, u, w, qk, s0, do, *, name, riders=()):
    t = q.shape[1]
    nc = t // CHUNK

    def body(q_ref, k_ref, g_ref, u_ref, w_ref, qk_ref, s0_ref, do_ref,
             dq_ref, dk_ref, dg_ref, du_ref, dw_ref, dqk_ref, ds_ref):
        @pl.when(pl.program_id(0) == 0)
        def _():
            ds_ref[...] = jnp.zeros_like(ds_ref)

        _, vjp = jax.vjp(_step_of_gates, s0_ref[...], q_ref[...], k_ref[...], g_ref[...], u_ref[...], w_ref[...], qk_ref[...])
        ds, dq, dk, dg, du, dw, dqk = vjp((do_ref[...], ds_ref[...]))
        for ref, val in zip((ds_ref, dq_ref, dk_ref, dg_ref, du_ref, dw_ref, dqk_ref), (ds, dq, dk, dg, du, dw, dqk)):
            ref[...] = val

    x_spec, qk_spec = (_head_major(CHUNK, w, lambda c: nc - 1 - c) for w in (HEAD_DIM, CHUNK))
    g_spec = pl.BlockSpec((CHUNK, LANES), lambda c: (nc - 1 - c, 0))
    s_spec = pl.BlockSpec((N_HEADS, None, HEAD_DIM, HEAD_DIM), lambda c: (0, nc - 1 - c, 0, 0))
    head_shape = [jax.ShapeDtypeStruct((N_HEADS, t, w_), F32) for w_ in (HEAD_DIM, HEAD_DIM, HEAD_DIM, HEAD_DIM, CHUNK)]
    return hosted_call(
        riders, body, name=name, grid=(nc,), in_specs=[x_spec, x_spec, g_spec, x_spec, x_spec, qk_spec, s_spec, x_spec],
        out_specs=[x_spec, x_spec, g_spec, x_spec, x_spec, qk_spec],
        out_shape=head_shape[:2] + [jax.ShapeDtypeStruct((t, LANES), F32)] + head_shape[2:],
        scratch_shapes=[pltpu.VMEM((N_HEADS, HEAD_DIM, HEAD_DIM), F32)],
        compiler_params=_params(("arbitrary",)),
    )(q, k, gates, u, w, qk, s0, do)


def delta_intra_bwd(q, k, v, gates, du, dw, dqk, dq_s, dk_s, dg_s, *, name, riders=()):
    t = q.shape[1]
    rows = min(INTRA_CHUNKS, t // CHUNK) * CHUNK

    def body(q_ref, k_ref, v_ref, g_ref, du_ref, dw_ref, dqk_ref, dqs_ref, dks_ref, dgs_ref, dq_ref, dk_ref, dv_ref, dg_ref):
        _, vjp = jax.vjp(_intra_of_gates, q_ref[...], k_ref[...], v_ref[...], g_ref[...])
        dq, dk, dv, dg = vjp((du_ref[...], dw_ref[...], dqk_ref[...]))
        dq_ref[...] = dq + dqs_ref[...]
        dk_ref[...] = dk + dks_ref[...]
        dv_ref[...] = dv
        dg_ref[...] = dg + dgs_ref[...]

    x_spec, qk_spec = (_head_major(rows, w, lambda i: i) for w in (HEAD_DIM, CHUNK))
    g_spec = pl.BlockSpec((rows, LANES), lambda i: (i, 0))
    return hosted_call(
        riders, body, name=name, grid=(t // rows,),
        in_specs=[x_spec] * 3 + [g_spec, x_spec, x_spec, qk_spec, x_spec, x_spec, g_spec],
        out_specs=[x_spec] * 3 + [g_spec],
        out_shape=[jax.ShapeDtypeStruct((N_HEADS, t, HEAD_DIM), F32)] * 3 + [jax.ShapeDtypeStruct((t, LANES), F32)],
        compiler_params=_params(("parallel",)),
    )(q, k, v, gates, du, dw, dqk, dq_s, dk_s, dg_s)


_V_BLOCK = 2 * N_HEADS
FOX_GROUPS = 8


def _fox_groups(t):
    nq = t // Q_BLOCK
    per = max(1, nq // FOX_GROUPS)
    return [(g0, per, (g0 + per) * Q_BLOCK) for g0 in range(0, nq, per)]


def fox_attn_fwd(q, k, proj, fq, fk, *, name, riders=()):
    t = q.shape[0]

    def body(q_ref, k_ref, v_ref, fq_ref, fk_ref, o_ref, kb_ref, vb_ref):
        head = pl.program_id(0)
        kb_ref[...] = k_ref[...].astype(BF16)
        vb_ref[...] = v_ref[...].astype(BF16)
        for g0, per, keys in _fox_groups(t):
            def block(j, carry, g0=g0, keys=keys):
                rows = pl.ds(pl.multiple_of((g0 + j) * Q_BLOCK, Q_BLOCK), Q_BLOCK)
                p = fox_probs(q_ref[rows, :].astype(BF16), kb_ref[0:keys, :], _lane_column(fq_ref[rows, :], head),
                              fk_ref[:, 0:keys], (g0 + j) * Q_BLOCK)
                o_ref[rows, :] = jnp.dot(p.astype(BF16), vb_ref[0:keys, :], preferred_element_type=F32)
                return carry
            lax.fori_loop(0, per, block, 0)

    x_spec = pl.BlockSpec((t, HEAD_DIM), lambda h: (0, h))
    v_spec = pl.BlockSpec((t, HEAD_DIM), lambda h: (0, _V_BLOCK + h))
    fq_spec = pl.BlockSpec((t, LANES), lambda h: (0, 0))
    fk_spec = pl.BlockSpec((None, 1, t), lambda h: (h, 0, 0))
    return hosted_call(
        riders, body, name=name, grid=(N_HEADS,), in_specs=[x_spec, x_spec, v_spec, fq_spec, fk_spec], out_specs=x_spec,
        out_shape=jax.ShapeDtypeStruct((t, D_MODEL), F32), scratch_shapes=[pltpu.VMEM((t, HEAD_DIM), BF16)] * 2,
        compiler_params=_params(("parallel",)),
    )(q, k, proj, fq, fk)


def fox_attn_bwd(q, k, proj, fq, fk, do, *, name, riders=()):
    t = q.shape[0]

    def body(q_ref, k_ref, v_ref, fq_ref, fk_ref, do_ref, dq_ref, dk_ref, dv_out_ref, dfq_ref, dfk_ref, kb_ref, vb_ref, dv_ref):
        head = pl.program_id(0)

        @pl.when(head == 0)
        def _():
            dfq_ref[...] = jnp.zeros_like(dfq_ref)

        kb_ref[...] = k_ref[...].astype(BF16)
        vb_ref[...] = v_ref[...].astype(BF16)
        dk_ref[...] = jnp.zeros_like(dk_ref)
        dv_ref[...] = jnp.zeros_like(dv_ref)
        dfk_ref[...] = jnp.zeros_like(dfk_ref)
        nt = (((1,), (1,)), ((), ()))
        tn = (((0,), (0,)), ((), ()))
        for g0, per, keys in _fox_groups(t):
            def block(j, carry, g0=g0, keys=keys):
                rows = pl.ds(pl.multiple_of((g0 + j) * Q_BLOCK, Q_BLOCK), Q_BLOCK)
                qb, dob = q_ref[rows, :].astype(BF16), do_ref[rows, :].astype(BF16)
                kb, vb = kb_ref[0:keys, :], vb_ref[0:keys, :]
                p = fox_probs(qb, kb, _lane_column(fq_ref[rows, :], head), fk_ref[:, 0:keys], (g0 + j) * Q_BLOCK)
                dp = lax.dot_general(dob, vb, nt, preferred_element_type=F32)
                dz = p * (dp - jnp.sum(dp * p, axis=-1, keepdims=True))
                pb, dzb = p.astype(BF16), dz.astype(BF16)
                dq_ref[rows, :] = jnp.dot(dzb, kb, preferred_element_type=F32)
                lane = lax.broadcasted_iota(jnp.int32, (Q_BLOCK, LANES), 1)
                dfq_ref[rows, :] += jnp.where(lane == head, jnp.sum(dz, axis=-1, keepdims=True), 0.0)
                dk_ref[0:keys, :] += lax.dot_general(dzb, qb, tn, preferred_element_type=F32)
                dv_ref[0:keys, :] += lax.dot_general(pb, dob, tn, preferred_element_type=F32)
                dfk_ref[:, 0:keys] -= jnp.sum(dz, axis=0, keepdims=True)
                return carry
            lax.fori_loop(0, per, block, 0)
        dv_out_ref[...] = dv_ref[...].astype(dv_out_ref.dtype)

    x_spec = pl.BlockSpec((t, HEAD_DIM), lambda h: (0, h))
    v_spec = pl.BlockSpec((t, HEAD_DIM), lambda h: (0, _V_BLOCK + h))
    fq_spec = pl.BlockSpec((t, LANES), lambda h: (0, 0))
    fk_spec = pl.BlockSpec((None, 1, t), lambda h: (h, 0, 0))
    return hosted_call(
        riders, body, name=name, grid=(N_HEADS,), in_specs=[x_spec, x_spec, v_spec, fq_spec, fk_spec, x_spec],
        out_specs=[x_spec, x_spec, x_spec, fq_spec, fk_spec],
        out_shape=[jax.ShapeDtypeStruct((t, D_MODEL), F32)] * 2 + [jax.ShapeDtypeStruct((t, D_MODEL), BF16)]
        + [jax.ShapeDtypeStruct((t, LANES), F32), jax.ShapeDtypeStruct((N_HEADS, 1, t), F32)],
        scratch_shapes=[pltpu.VMEM((t, HEAD_DIM), BF16)] * 2 + [pltpu.VMEM((t, HEAD_DIM), F32)],
        compiler_params=_params(("arbitrary",)),
    )(q, k, proj, fq, fk, do)


def memkv_fwd(mem, mnw, wkv, mknw, *, name):
    n = mem.shape[0]

    def body(mem_ref, mnw_ref, w_ref, mknw_ref, mk_ref, mv_ref):
        mk, mv = memkv_fn(mem_ref[...], mnw_ref[...], w_ref[...], mknw_ref[...])
        mk_ref[...] = mk
        mv_ref[...] = mv

    return pl.pallas_call(
        body, name=name, out_shape=[jax.ShapeDtypeStruct((n, MEM_WIDTH), F32)] * 2,
        compiler_params=pltpu.CompilerParams(vmem_limit_bytes=VMEM_LIMIT),
    )(mem, mnw, wkv, mknw)


def memkv_bwd(mem, mnw, wkv, mknw, dmk, dmv, *, name):
    def body(mem_ref, mnw_ref, w_ref, mknw_ref, dmk_ref, dmv_ref, dmnw_ref, dw_ref, dmknw_ref):
        f = functools.partial(memkv_fn, mem_ref[...])
        _, vjp = jax.vjp(f, mnw_ref[...], w_ref[...].astype(F32), mknw_ref[...])
        dmnw, dw, dmknw = vjp((dmk_ref[...], dmv_ref[...]))
        dmnw_ref[...] = dmnw
        dw_ref[...] = dw.astype(dw_ref.dtype)
        dmknw_ref[...] = dmknw

    return pl.pallas_call(
        body, name=name,
        out_shape=[jax.ShapeDtypeStruct(mnw.shape, F32), jax.ShapeDtypeStruct(wkv.shape, BF16), jax.ShapeDtypeStruct(mknw.shape, F32)],
        compiler_params=pltpu.CompilerParams(vmem_limit_bytes=VMEM_LIMIT),
    )(mem, mnw, wkv, mknw, dmk, dmv)


def _row(v, width=None):
    v = v.reshape(1, -1)
    if width is not None and v.shape[1] < width:
        v = jnp.pad(v, ((0, 0), (0, width - v.shape[1])))
    return v


def _norm_fwd(x, w, name, riders=()):
    return rows_call(lambda x, w: rms(x, w), [x], [w], [(D_MODEL, BF16)], [], tm=512, name=name, riders=riders)[0]


def _norm_bwd(x, w, dh, dx_in, name, riders=()):
    def fn(x, dh, dx_in, w):
        _, vjp = jax.vjp(rms, x, w)
        dx, dw = vjp(dh)
        return dx + dx_in, dw
    return rows_call(fn, [x, dh, dx_in], [w], [(D_MODEL, F32)], [(1, D_MODEL)], tm=512, name=name, riders=riders)


FF_PIECE = D_FF // N_DEV


def _add(r, x):
    return r + x


def _piece(rows, cols, index):
    return pl.BlockSpec((None, rows, cols), lambda i, j, kk: (index(i, j, kk), 0, 0))


def _two_pieces(rows, cols, index):
    return pl.BlockSpec((2, rows, cols), lambda i, j, kk: (index(i, j, kk), 0, 0))


def _mlp_fwd(x, n2w, w1, w2, layer, riders=()):
    riders = list(riders) + [None, None]
    h2 = _norm_fwd(x, n2w, f"norm2_fwd_{layer}")
    u, a1 = matmul(h2, w1, name=f"mlp1_fwd_{layer}", tiles=(None, FF_PIECE, D_MODEL),
                   extra_out=(lambda u: jnp.square(jnp.maximum(u, 0.0)), BF16),
                   b_view=(D_MODEL, D_FF, _piece(D_MODEL, FF_PIECE, lambda i, j, kk: j)), riders=riders[0])
    y = matmul(a1, w2, name=f"mlp2_fwd_{layer}", post=_add, post_ins=[x], tiles=(None, D_MODEL, 2 * FF_PIECE),
               b_view=(D_FF, D_MODEL, _two_pieces(FF_PIECE, D_MODEL, lambda i, j, kk: kk)), riders=riders[1])
    return y, (x, h2, u, a1)


def pair_sum(g, got, *, name):
    _, rows, cols = g.shape
    tile = _pick(rows, (512, 256, 128))
    c = lax.axis_index("c").astype(jnp.int32).reshape(1)

    def body(c_ref, a_ref, b_ref, o_ref):
        o_ref[...] = (a_ref[...].astype(F32) + b_ref[...].astype(F32)).astype(o_ref.dtype)

    grid_spec = pltpu.PrefetchScalarGridSpec(
        num_scalar_prefetch=1, grid=(4, rows // tile),
        in_specs=[pl.BlockSpec((None, tile, cols), lambda k, i, c_ref: (2 * k + c_ref[0], i, 0)),
                  pl.BlockSpec((None, tile, cols), lambda k, i, c_ref: (k, i, 0))],
        out_specs=pl.BlockSpec((None, tile, cols), lambda k, i, c_ref: (k, i, 0)))
    return pl.pallas_call(
        body, name=name, grid_spec=grid_spec, out_shape=jax.ShapeDtypeStruct((4, rows, cols), g.dtype),
        compiler_params=_params(("parallel", "parallel")),
    )(c, g, got)


def chip_sums(names, pieces, gots):
    return [pair_sum(a, got, name=f"grads_pair_sum_{n}") for n, a, got in zip(names, pieces, gots)]


def _mlp_bwd(dy, res, n2w, w1, w2, layer, riders=()):
    x, h2, u, a1 = res
    du = matmul(dy, w2, tb=True, name=f"mlp2_dx_{layer}", out_dtype=BF16, tiles=(None, 2 * FF_PIECE, D_MODEL),
                post=lambda r, u: r * (2.0 * jnp.maximum(u, 0.0)), post_ins=[u],
                b_view=(D_MODEL, D_FF, _two_pieces(FF_PIECE, D_MODEL, lambda i, j, kk: j)), riders=riders)
    dw2 = matmul(a1, dy, ta=True, name=f"mlp2_dw_{layer}", out_dtype=BF16, tiles=(FF_PIECE, D_MODEL, None), out_view=(
        w2.shape, _piece(FF_PIECE, D_MODEL, lambda i, j, kk: i), None))
    sib2 = sibling_rider([dw2])
    dh2 = matmul(du, w1, tb=True, name=f"mlp1_dx_{layer}", tiles=(None, D_MODEL, FF_PIECE),
                 b_view=(D_FF, D_MODEL, _piece(D_MODEL, FF_PIECE, lambda i, j, kk: kk)), riders=[sib2])
    dw1 = matmul(h2, du, ta=True, name=f"mlp1_dw_{layer}", out_dtype=BF16, tiles=(D_MODEL, FF_PIECE, None), out_view=(
        w1.shape, _piece(D_MODEL, FF_PIECE, lambda i, j, kk: j), None))
    sib1 = sibling_rider([dw1])
    dx, dn2w = _norm_bwd(x, n2w, dh2, dy, f"norm2_bwd_{layer}", riders=[sib1])
    return dx, dw1, dw2, dn2w, sib1, sib2


def _in_proj_bwd(h, dmain, dsmall, w_main, w_small, tag):
    dh = matmul(dmain, w_main, tb=True, name=f"inproj_dx_main_{tag}")
    dh = matmul(dsmall, w_small, tb=True, post=_add, post_ins=[dh], name=f"inproj_dx_small_{tag}")
    dw_main = matmul(h, dmain, ta=True, out_dtype=BF16, name=f"inproj_dw_main_{tag}")
    dw_small = matmul(h, dsmall, ta=True, out_dtype=BF16, name=f"inproj_dw_small_{tag}")
    return dh, dw_main, dw_small


def local_step(x, mem, target, w, m, v):
    t = x.shape[0]
    n_mem = mem.shape[0]
    g = {}

    def wire(a):
        return a.astype(BF16)

    (dn_g,), = run_riders([gather_rider([wire(w["dn_w_in"][0])])], name="weights_gather_first")
    dn_main, dn_ab = in_proj_weights(dn_g, DN_IN, 2 * N_HEADS)
    fox_w = wire(w["fox_w_in"][0])
    ride_out = gather_rider([wire(w["w_out"][0]), wire(w["w_out"][1]), w["dn_conv_w"][0]])
    ride_kv = gather_rider([wire(w["w_mem_kv"])])
    ride_mlp1_0 = gather_rider([wire(w["w_mlp1"][0])])
    ride_mlp2_0 = gather_rider([wire(w["w_mlp2"][0])])
    ride_fox_a, ride_fox_b = gather_rider([fox_w[:D_MODEL // 2]]), gather_rider([fox_w[D_MODEL // 2:]])
    ride_mlp_1 = gather_rider([wire(w["w_mlp1"][1]), wire(w["w_mlp2"][1])])
    mnw, mknw = _row(w["mem_norm_w"]), _row(w["mem_k_norm_w"])

    n1w0, n2w0 = _row(w["norm1_w"][0]), _row(w["norm2_w"][0])
    alog, dtb = _row(w["dn_a_log"][0], LANES), _row(w["dn_dt_bias"][0], LANES)
    onw, mqw0 = _row(w["dn_o_norm_w"][0]), _row(w["memq_norm_w"][0])
    x0 = x
    h0 = _norm_fwd(x0, n1w0, "norm1_fwd_0")
    pm0 = matmul(h0, dn_main, name="inproj_main_0", riders=[ride_out])
    w_out0, w_out1 = (a.reshape(OUT_IN, D_MODEL) for a in ride_out.results[:2])
    conv_w = ride_out.results[2].transpose(1, 0, 2).reshape(CONV_WIDTH, 3 * D_MODEL)
    ps0 = matmul(h0, dn_ab, name="inproj_small_0")
    gates = rows_call(dn_gates_fn, [ps0], [alog, dtb], [(LANES, F32)], [], tm=512, name="dn_gates_fwd")[0]
    q0, k0, v0 = dn_prep_fwd(pm0, conv_w, name="dn_prep_fwd", riders=[ride_kv])
    w_kv = ride_kv.results[0].reshape(D_MODEL, D_MODEL)
    mk, mv = memkv_fwd(mem, mnw, w_kv, mknw, name="memkv_fwd")
    u0, w0, qk0 = delta_intra_fwd(q0, k0, v0, gates, name="delta_intra_fwd", riders=[ride_mlp1_0])
    o0, s_start = delta_seq_fwd(q0, k0, gates, u0, w0, qk0, name="delta_seq_fwd", riders=[ride_mlp2_0])
    cat0 = rows_call(dn_out_fn, [o0, (pm0, D_MODEL, 3), (pm0, MEM_WIDTH, 8)], [onw, mqw0, mk, mv],
                     [(D_MODEL + MEM_WIDTH, BF16)], [], tm=256, name="dn_out_fwd")[0]
    (w1_0,), (w2_0,) = ride_mlp1_0.results, ride_mlp2_0.results
    x1 = matmul(cat0, w_out0, post=_add, post_ins=[x0], name="wout_fwd_0")
    x2, mlp_res0 = _mlp_fwd(x1, n2w0, w1_0, w2_0, 0, riders=[[ride_fox_a], [ride_fox_b]])
    fox_main, fox_f = in_proj_weights(
        jnp.concatenate([ride_fox_a.results[0], ride_fox_b.results[0]], axis=1), FOX_IN, N_HEADS)

    n1w1, n2w1 = _row(w["norm1_w"][1]), _row(w["norm2_w"][1])
    fbias = _row(w["fox_f_bias"][0], LANES)
    qnw, knw, mqw1 = _row(w["fox_q_norm_w"][0]), _row(w["fox_k_norm_w"][0]), _row(w["memq_norm_w"][1])
    h1 = _norm_fwd(x2, n1w1, "norm1_fwd_1")
    pm1 = matmul(h1, fox_main, name="inproj_main_1")
    ps1 = matmul(h1, fox_f, name="inproj_small_1")
    fq = rows_call(fox_fcum_fn, [ps1], [fbias], [(LANES, F32)], [], tm=t, name="fox_fcum_fwd")[0]
    fk = fq[:, :N_HEADS].T[:, None, :]
    q1, k1 = rows_call(fox_qk_fn, [(pm1, D_MODEL, 0), (pm1, D_MODEL, 1)], [qnw, knw], [(D_MODEL, F32)] * 2, [], tm=256,
                       name="fox_qk_fwd")
    o1 = fox_attn_fwd(q1, k1, pm1, fq, fk, name="fox_attn_fwd", riders=[ride_mlp_1])
    cat1 = rows_call(fox_out_fn, [o1, (pm1, D_MODEL, 3), (pm1, MEM_WIDTH, 8)], [mqw1, mk, mv],
                     [(D_MODEL + MEM_WIDTH, BF16)], [], tm=256, name="fox_out_fwd")[0]
    w1_1, w2_1 = ride_mlp_1.results
    x3 = matmul(cat1, w_out1, post=_add, post_ins=[x2], name="wout_fwd_1")
    y, mlp_res1 = _mlp_fwd(x3, n2w1, w1_1, w2_1, 1)

    def loss_fn(y, tgt):
        e = y - tgt
        return e * (1.0 / D_MODEL), jnp.sum(jnp.sum(e * e, axis=1, keepdims=True), axis=0, keepdims=True)
    dy, sq = rows_call(loss_fn, [y, target], [], [(D_MODEL, F32)], [(1, 1)], tm=512, name="loss")
    loss = sq[0, 0] * (0.5 / D_MODEL)

    dx3, dw1_1, dw2_1, dn2w1, sib1, sib2 = _mlp_bwd(dy, mlp_res1, n2w1, w1_1, w2_1, 1)
    dcat1 = matmul(dx3, w_out1, tb=True, name="wout_dx_1")
    dwo_1 = matmul(cat1, dx3, ta=True, out_dtype=BF16, name="wout_dw_1").reshape(N_DEV, OUT_IN // N_DEV, D_MODEL)
    sibo = sibling_rider([dwo_1])
    do1, dgate1, dqm1, dmqw1, dmk1, dmv1 = rows_call(
        functools.partial(vjp_rows(fox_out_fn, 3, (True, True, True)), n_row=3, n_ct=1),
        [o1, (pm1, D_MODEL, 3), (pm1, MEM_WIDTH, 8), dcat1], [mqw1, mk, mv],
        [(D_MODEL, F32), (D_MODEL, BF16), (MEM_WIDTH, BF16)], [(1, HEAD_DIM), (n_mem, MEM_WIDTH), (n_mem, MEM_WIDTH)],
        tm=256, name="fox_out_bwd", riders=[sibo])
    ride_l1 = chips_rider(chip_sums(["w_mlp2_1", "w_mlp1_1", "w_out_1"], [dw2_1, dw1_1, dwo_1],
                                    sib2.results + sib1.results + sibo.results))
    dq1, dk1, dv1, dfq, dfk = fox_attn_bwd(q1, k1, pm1, fq, fk, do1, name="fox_attn_bwd", riders=[ride_l1])
    dqraw1, dkraw1, dqnw, dknw = rows_call(
        functools.partial(vjp_rows(fox_qk_fn, 2, (True, True)), n_row=2, n_ct=2),
        [(pm1, D_MODEL, 0), (pm1, D_MODEL, 1), dq1, dk1], [qnw, knw],
        [(D_MODEL, BF16)] * 2, [(1, HEAD_DIM)] * 2, tm=256, name="fox_qk_bwd")
    dfcum = dfq + jnp.pad(dfk[:, 0, :].T, ((0, 0), (0, LANES - N_HEADS)))
    dps1, dfbias = rows_call(
        functools.partial(vjp_rows(fox_fcum_fn, 1, (True,)), n_row=1, n_ct=1),
        [ps1, dfcum], [fbias], [(LANES, F32)], [(1, LANES)], tm=t, name="fox_fcum_bwd")
    dpm1 = jnp.concatenate([dqraw1, dkraw1, dv1, dgate1, dqm1], axis=1)
    dh1, dwmain1, dwsmall1 = _in_proj_bwd(h1, dpm1, dps1, fox_main, fox_f, "1")
    g_fox = in_proj_pieces(dwmain1, dwsmall1, N_HEADS, FOX_IN)
    sibf = sibling_rider([g_fox])
    dx2, dn1w1 = _norm_bwd(x2, n1w1, dh1, dx3, "norm1_bwd_1", riders=[sibf])
    ride_fox_g = chips_rider(chip_sums(["fox_w_in"], [g_fox], sibf.results))

    dx1, dw1_0, dw2_0, dn2w0, sib1, sib2 = _mlp_bwd(dx2, mlp_res0, n2w0, w1_0, w2_0, 0)
    dcat0 = matmul(dx1, w_out0, tb=True, name="wout_dx_0")
    dwo_0 = matmul(cat0, dx1, ta=True, out_dtype=BF16, name="wout_dw_0").reshape(N_DEV, OUT_IN // N_DEV, D_MODEL)
    sibo = sibling_rider([dwo_0])
    do0, dz0, dqm0, donw, dmqw0, dmk0, dmv0 = rows_call(
        functools.partial(vjp_rows(dn_out_fn, 3, (True, True, True, True)), n_row=3, n_ct=1),
        [o0, (pm0, D_MODEL, 3), (pm0, MEM_WIDTH, 8), dcat0], [onw, mqw0, mk, mv],
        [((N_HEADS, HEAD_DIM), F32), (D_MODEL, BF16), (MEM_WIDTH, BF16)],
        [(1, HEAD_DIM), (1, HEAD_DIM), (n_mem, MEM_WIDTH), (n_mem, MEM_WIDTH)], tm=256, name="dn_out_bwd", riders=[sibo])
    h_l0 = chip_sums(["w_mlp2_0", "w_mlp1_0", "w_out_0"], [dw2_0, dw1_0, dwo_0], sib2.results + sib1.results + sibo.results)
    ride_l0_mlp, ride_l0_out = chips_rider(h_l0[:2]), chips_rider(h_l0[2:])
    dmnw, dwkv, dmknw = memkv_bwd(mem, mnw, w_kv, mknw, dmk0 + dmk1, dmv0 + dmv1, name="memkv_bwd")
    g_kv = dwkv.reshape(N_DEV, D_MODEL // N_DEV, D_MODEL)
    sibk = sibling_rider([g_kv])
    dq_s, dk_s, dg_s, du0, dw0, dqk0 = delta_seq_bwd(q0, k0, gates, u0, w0, qk0, s_start, do0, name="delta_seq_bwd",
                                                     riders=[ride_fox_g, sibk])
    ride_kv_g = chips_rider(chip_sums(["w_mem_kv"], [g_kv], sibk.results))
    dq0, dk0, dv0, dgates = delta_intra_bwd(q0, k0, v0, gates, du0, dw0, dqk0, dq_s, dk_s, dg_s,
                                            name="delta_intra_bwd", riders=[ride_l0_mlp, ride_kv_g])
    dxq, dxk, dxv, dcq, dck, dcv = dn_prep_bwd(pm0, conv_w, dq0, dk0, dv0, name="dn_prep_bwd", riders=[ride_l0_out])
    dconv = jnp.concatenate([dcq, dck, dcv], axis=1)
    dps0, dalog, ddtb = rows_call(
        functools.partial(vjp_rows(dn_gates_fn, 1, (True, True)), n_row=1, n_ct=1),
        [ps0, dgates], [alog, dtb], [(LANES, F32)], [(1, LANES)] * 2, tm=512, name="dn_gates_bwd")
    dpm0 = jnp.concatenate([dxq, dxk, dxv, dz0, dqm0], axis=1)
    dh0, dwmain0, dwsmall0 = _in_proj_bwd(h0, dpm0, dps0, dn_main, dn_ab, "0")
    grad_x, dn1w0 = _norm_bwd(x0, n1w0, dh0, dx1, "norm1_bwd_0")
    g_dn = in_proj_pieces(dwmain0, dwsmall0, 2 * N_HEADS, DN_IN)
    g_conv = dconv.reshape(CONV_WIDTH, N_DEV, -1).transpose(1, 0, 2).astype(BF16)

    g["mem_norm_w"] = dmnw[0]
    g["mem_k_norm_w"] = dmknw[0]
    g["norm1_w"] = jnp.concatenate([dn1w0, dn1w1], axis=0)
    g["dn_a_log"] = dalog[:, :N_HEADS]
    g["dn_dt_bias"] = ddtb[:, :N_HEADS]
    g["dn_o_norm_w"] = donw
    g["fox_f_bias"] = dfbias[:, :N_HEADS]
    g["fox_q_norm_w"] = dqnw
    g["fox_k_norm_w"] = dknw
    g["memq_norm_w"] = jnp.concatenate([dmqw0, dmqw1], axis=0)
    g["norm2_w"] = jnp.concatenate([dn2w0, dn2w1], axis=0)

    sibd = sibling_rider([g_dn, g_conv])
    run_riders([sibd], name="grads_to_sibling_last")
    ride_last = chips_rider(chip_sums(["dn_w_in", "dn_conv_w"], [g_dn, g_conv], sibd.results))
    ride_small = gather_rider([pack_small(g, last=loss)])
    run_riders([ride_last, ride_small], name="grads_to_chips_last")

    def layers(l0, l1):
        return jnp.stack([l0, l1], axis=1).reshape(4, -1, l0.shape[-1])

    parts = {
        "w_mlp1": layers(ride_l0_mlp.results[1], ride_l1.results[1]),
        "w_mlp2": layers(ride_l0_mlp.results[0], ride_l1.results[0]),
        "w_out": layers(ride_l0_out.results[0], ride_l1.results[2]),
        "fox_w_in": ride_fox_g.results[0], "w_mem_kv": ride_kv_g.results[0],
        "dn_w_in": ride_last.results[0], "dn_conv_w": ride_last.results[1],
    }
    out = {n: adamw(parts[n], w[n], m[n], v[n], name=f"adamw_{n}") for n, _, _ in BIG}
    small = adamw(ride_small.results[0], pack_small(w), pack_small(m), pack_small(v), name="adamw_small")
    loss = small[0][-1, -1]
    return loss, grad_x, out, small


WEIGHTS = ["mem_norm_w", "w_mem_kv", "mem_k_norm_w", "norm1_w", "dn_w_in", "dn_conv_w", "dn_a_log", "dn_dt_bias",
           "dn_o_norm_w", "fox_w_in", "fox_f_bias", "fox_q_norm_w", "fox_k_norm_w", "memq_norm_w", "w_out", "norm2_w",
           "w_mlp1", "w_mlp2"]
DN_IN = 4 * D_MODEL + 2 * N_HEADS + MEM_WIDTH
FOX_IN = 4 * D_MODEL + N_HEADS + MEM_WIDTH
GATE_END = 4 * D_MODEL
OUT_IN = D_MODEL + MEM_WIDTH
BIG = [("w_mem_kv", D_MODEL // N_DEV, D_MODEL), ("dn_w_in", D_MODEL, DN_IN // N_DEV), ("fox_w_in", D_MODEL, FOX_IN // N_DEV),
       ("dn_conv_w", CONV_WIDTH, 3 * D_MODEL // N_DEV), ("w_out", 2 * OUT_IN // N_DEV, D_MODEL),
       ("w_mlp1", 2 * D_MODEL, FF_PIECE), ("w_mlp2", 2 * FF_PIECE, D_MODEL)]
SMALL = [("mem_norm_w", (D_MODEL,), D_MODEL), ("mem_k_norm_w", (HEAD_DIM,), HEAD_DIM), ("norm1_w", (2, D_MODEL), 2 * D_MODEL),
         ("dn_a_log", (1, N_HEADS), LANES), ("dn_dt_bias", (1, N_HEADS), LANES), ("dn_o_norm_w", (1, HEAD_DIM), HEAD_DIM),
         ("fox_f_bias", (1, N_HEADS), LANES), ("fox_q_norm_w", (1, HEAD_DIM), HEAD_DIM), ("fox_k_norm_w", (1, HEAD_DIM), HEAD_DIM),
         ("memq_norm_w", (2, HEAD_DIM), 2 * HEAD_DIM), ("norm2_w", (2, D_MODEL), 2 * D_MODEL)]
SMALL_ROWS = 16


def pack_small(p, last=None):
    flat = jnp.concatenate([jnp.pad(p[n].reshape(-1), (0, ln - math.prod(sh))) for n, sh, ln in SMALL])
    tail = jnp.zeros((SMALL_ROWS * PACK_W - flat.shape[0],), F32)
    if last is not None:
        tail = jnp.concatenate([tail[:-1], last.reshape(1)])
    return jnp.concatenate([flat, tail]).reshape(SMALL_ROWS, PACK_W)


def unpack_small(pk):
    flat, off, out = pk.reshape(-1), 0, {}
    for n, sh, ln in SMALL:
        out[n] = flat[off:off + math.prod(sh)].reshape(sh)
        off += ln
    return out


def in_proj_weights(gathered, width, n_small):
    full = gathered.transpose(1, 0, 2).reshape(D_MODEL, width)
    main = jnp.concatenate([full[:, :GATE_END], full[:, GATE_END + n_small:]], axis=1)
    return main, jnp.pad(full[:, GATE_END:GATE_END + n_small], ((0, 0), (0, LANES - n_small)))


def in_proj_pieces(d_main, d_small, n_small, width):
    full = jnp.concatenate([d_main[:, :GATE_END], d_small[:, :n_small], d_main[:, GATE_END:]], axis=1)
    return full.reshape(D_MODEL, N_DEV, width // N_DEV).transpose(1, 0, 2)


def adamw(parts, w, m, v, *, name):
    n, _, cols = parts.shape
    layers = w.shape[0] if w.ndim == 3 else 1
    rows = w.shape[-2]
    tile = _pick(rows, (512, 256, 128))
    steps = rows // tile

    def body(p_ref, w_ref, m_ref, v_ref, g_ref, d_ref, mo_ref, vo_ref):
        g = p_ref[0].astype(F32)
        for i in range(1, n):
            g = g + p_ref[i].astype(F32)
        m_new = ADAM_B1 * m_ref[...] + (1.0 - ADAM_B1) * g
        v_new = ADAM_B2 * v_ref[...] + (1.0 - ADAM_B2) * jnp.square(g)
        m_hat = m_new / (1.0 - ADAM_B1 ** ADAM_STEP)
        v_hat = v_new / (1.0 - ADAM_B2 ** ADAM_STEP)
        g_ref[...] = g
        d_ref[...] = -ADAM_LR * (m_hat / (jnp.sqrt(v_hat) + ADAM_EPS) + ADAM_WD * w_ref[...])
        mo_ref[...] = m_new
        vo_ref[...] = v_new

    if w.ndim == 3:
        spec = pl.BlockSpec((None, tile, cols), lambda l, i: (l, i, 0))
    else:
        spec = pl.BlockSpec((tile, cols), lambda l, i: (i, 0))
    return pl.pallas_call(
        body, name=name, grid=(layers, steps),
        in_specs=[pl.BlockSpec((n, tile, cols), lambda l, i: (0, l * steps + i, 0)), spec, spec, spec], out_specs=[spec] * 4,
        out_shape=[jax.ShapeDtypeStruct(w.shape, F32)] * 4, compiler_params=_params(("parallel", "parallel")),
    )(parts, w, m, v)


def kernel(x, mem, mem_norm_w, w_mem_kv, mem_k_norm_w, norm1_w, dn_w_in, dn_conv_w, dn_a_log, dn_dt_bias, dn_o_norm_w, fox_w_in, fox_f_bias, fox_q_norm_w, fox_k_norm_w, memq_norm_w, w_out, norm2_w, w_mlp1, w_mlp2, loss_target, m_mem_norm_w, m_w_mem_kv, m_mem_k_norm_w, m_norm1_w, m_dn_w_in, m_dn_conv_w, m_dn_a_log, m_dn_dt_bias, m_dn_o_norm_w, m_fox_w_in, m_fox_f_bias, m_fox_q_norm_w, m_fox_k_norm_w, m_memq_norm_w, m_w_out, m_norm2_w, m_w_mlp1, m_w_mlp2, v_mem_norm_w, v_w_mem_kv, v_mem_k_norm_w, v_norm1_w, v_dn_w_in, v_dn_conv_w, v_dn_a_log, v_dn_dt_bias, v_dn_o_norm_w, v_fox_w_in, v_fox_f_bias, v_fox_q_norm_w, v_fox_k_norm_w, v_memq_norm_w, v_w_out, v_norm2_w, v_w_mlp1, v_w_mlp2):
    p = dict(mem_norm_w=mem_norm_w, w_mem_kv=w_mem_kv, mem_k_norm_w=mem_k_norm_w, norm1_w=norm1_w, dn_w_in=dn_w_in,
             dn_conv_w=dn_conv_w, dn_a_log=dn_a_log, dn_dt_bias=dn_dt_bias, dn_o_norm_w=dn_o_norm_w, fox_w_in=fox_w_in,
             fox_f_bias=fox_f_bias, fox_q_norm_w=fox_q_norm_w, fox_k_norm_w=fox_k_norm_w, memq_norm_w=memq_norm_w,
             w_out=w_out, norm2_w=norm2_w, w_mlp1=w_mlp1, w_mlp2=w_mlp2)
    pm = dict(mem_norm_w=m_mem_norm_w, w_mem_kv=m_w_mem_kv, mem_k_norm_w=m_mem_k_norm_w, norm1_w=m_norm1_w,
              dn_w_in=m_dn_w_in, dn_conv_w=m_dn_conv_w, dn_a_log=m_dn_a_log, dn_dt_bias=m_dn_dt_bias,
              dn_o_norm_w=m_dn_o_norm_w, fox_w_in=m_fox_w_in, fox_f_bias=m_fox_f_bias, fox_q_norm_w=m_fox_q_norm_w,
              fox_k_norm_w=m_fox_k_norm_w, memq_norm_w=m_memq_norm_w, w_out=m_w_out, norm2_w=m_norm2_w, w_mlp1=m_w_mlp1,
              w_mlp2=m_w_mlp2)
    pv = dict(mem_norm_w=v_mem_norm_w, w_mem_kv=v_w_mem_kv, mem_k_norm_w=v_mem_k_norm_w, norm1_w=v_norm1_w,
              dn_w_in=v_dn_w_in, dn_conv_w=v_dn_conv_w, dn_a_log=v_dn_a_log, dn_dt_bias=v_dn_dt_bias,
              dn_o_norm_w=v_dn_o_norm_w, fox_w_in=v_fox_w_in, fox_f_bias=v_fox_f_bias, fox_q_norm_w=v_fox_q_norm_w,
              fox_k_norm_w=v_fox_k_norm_w, memq_norm_w=v_memq_norm_w, w_out=v_w_out, norm2_w=v_norm2_w, w_mlp1=v_w_mlp1,
              w_mlp2=v_w_mlp2)

    loss, grad_x, results, small = local_step(x[0], mem[0], loss_target[0], p, pm, pv)
    small = [unpack_small(o) for o in small]
    groups = [{**small[i], **{n: r[i] for n, r in results.items()}} for i in range(4)]
    return (loss, grad_x[None], *[grp[n] for grp in groups for n in WEIGHTS])
```

```python
import functools
import math

import jax
import jax.numpy as jnp
from jax import lax
from jax.experimental import pallas as pl
from jax.experimental.pallas import tpu as pltpu

F32 = jnp.float32
BF16 = jnp.bfloat16
HIGHEST = lax.Precision.HIGHEST

D_MODEL = 1024
HEAD_DIM = 128
N_HEADS = 8
MEM_HEADS = 4
MEM_WIDTH = MEM_HEADS * HEAD_DIM
D_FF = 4 * D_MODEL
CONV_WIDTH = 4
CHUNK = 64
Q_BLOCK = 128
EPS = 1e-6
SCALE = HEAD_DIM ** -0.5
MAIN_WIDTH = 4 * D_MODEL + MEM_WIDTH
LANES = 128
N_DEV = 8

ADAM_LR = 0.001
ADAM_B1 = 0.9
ADAM_B2 = 0.999
ADAM_EPS = 1e-08
ADAM_WD = 0.01
ADAM_STEP = 10

VMEM_LIMIT = 56 * 2 ** 20
MESH = pl.DeviceIdType.MESH


def _bdot(a, b, dims):
    return lax.dot_general(a.astype(BF16), b.astype(BF16), (dims, ((), ())), preferred_element_type=F32)


@jax.custom_vjp
def mm(a, b):
    return _bdot(a, b, ((1,), (0,)))


@jax.custom_vjp
def mm_nt(a, b):
    return _bdot(a, b, ((1,), (1,)))


@jax.custom_vjp
def mm_tn(a, b):
    return _bdot(a, b, ((0,), (0,)))


mm.defvjp(lambda a, b: (mm(a, b), (a, b)), lambda r, g: (mm_nt(g, r[1]), mm_tn(r[0], g)))
mm_nt.defvjp(lambda a, b: (mm_nt(a, b), (a, b)), lambda r, g: (mm(g, r[1]), mm_tn(g, r[0])))
mm_tn.defvjp(lambda a, b: (mm_tn(a, b), (a, b)), lambda r, g: (mm_nt(r[1], g), mm(r[0], g)))


def hdot(a, b):
    return jnp.dot(a, b, precision=HIGHEST, preferred_element_type=F32)


def rms(x, w):
    return x * lax.rsqrt(jnp.mean(x * x, axis=-1, keepdims=True) + EPS) * w


def l2n(x):
    return x * lax.rsqrt(jnp.sum(x * x, axis=-1, keepdims=True) + EPS)


def _iota2(n, m):
    return lax.broadcasted_iota(jnp.int32, (n, m), 0), lax.broadcasted_iota(jnp.int32, (n, m), 1)


def _lower_ones(n):
    r, c = _iota2(n, n)
    return jnp.where(r >= c, 1.0, 0.0).astype(F32)


def _last_row(x):
    r = lax.broadcasted_iota(jnp.int32, x.shape, 0)
    return jnp.sum(jnp.where(r == x.shape[0] - 1, x, 0.0), axis=0, keepdims=True)


def _softmax_rows(z):
    m = lax.stop_gradient(jnp.max(z, axis=-1, keepdims=True))
    e = jnp.exp(z - m)
    return e * (1.0 / jnp.sum(e, axis=-1, keepdims=True))


_BNN = (((2,), (1,)), ((0,), (0,)))
_BNT = (((2,), (2,)), ((0,), (0,)))
_BTN = (((1,), (1,)), ((0,), (0,)))


def _bbdot(a, b, dims):
    return lax.dot_general(a.astype(BF16), b.astype(BF16), dims, preferred_element_type=F32)


@jax.custom_vjp
def bmm(a, b):
    return _bbdot(a, b, _BNN)


@jax.custom_vjp
def bmm_nt(a, b):
    return _bbdot(a, b, _BNT)


@jax.custom_vjp
def bmm_tn(a, b):
    return _bbdot(a, b, _BTN)


@jax.custom_vjp
def bmm_high(a, b):
    return lax.dot_general(a, b, _BNN, precision=lax.Precision.HIGH, preferred_element_type=F32)


bmm.defvjp(lambda a, b: (bmm(a, b), (a, b)), lambda r, g: (bmm_nt(g, r[1]), bmm_tn(r[0], g)))
bmm_nt.defvjp(lambda a, b: (bmm_nt(a, b), (a, b)), lambda r, g: (bmm(g, r[1]), bmm_tn(g, r[0])))
bmm_tn.defvjp(lambda a, b: (bmm_tn(a, b), (a, b)), lambda r, g: (bmm_nt(r[1], g), bmm(r[0], g)))
bmm_high.defvjp(lambda a, b: (bmm_high(a, b), (a, b)), lambda r, g: (bmm_nt(g, r[1]), bmm_tn(r[0], g)))

NEUMANN_HIGH_LEVELS = 2


def inv_unit_lower(a):
    n = a.shape[-1]
    r, c = _iota2(n, n)
    p = jnp.where(r == c, 1.0, 0.0).astype(F32) - a
    ak = a
    for level in range(int(math.log2(n)) - 1):
        dot = bmm_high if level < NEUMANN_HIGH_LEVELS else bmm
        ak = dot(ak, ak)
        p = p + dot(p, ak)
    return p


def delta_intra(q, k, v, gc, beta):
    b, c, _ = q.shape
    r, cc = _iota2(c, c)
    causal = r >= cc
    strict = r > cc
    gi = jnp.broadcast_to(gc, (b, c, c))
    gj = jnp.swapaxes(gi, 1, 2)
    decay = jnp.where(causal, jnp.exp(jnp.where(causal, gi - gj, 0.0)), 0.0)
    kb = k * beta
    a = jnp.where(strict, bmm_nt(kb, k) * decay, 0.0)
    t = inv_unit_lower(a)
    u = bmm(t, v * beta)
    w = bmm(t, kb * jnp.exp(gc))
    qk = jnp.where(causal, bmm_nt(q, k) * decay, 0.0)
    return u, w, qk


def delta_step(s, q, k, gc, u, w, qk):
    v_new = u - bmm(w, s)
    out = bmm(q * jnp.exp(gc), s) + bmm(qk, v_new)
    r = lax.broadcasted_iota(jnp.int32, gc.shape, 1)
    g_last = jnp.sum(jnp.where(r == gc.shape[1] - 1, gc, 0.0), axis=1, keepdims=True)
    k_dec = k * jnp.exp(g_last - gc)
    s_new = s * jnp.exp(g_last) + bmm_tn(k_dec, v_new)
    return out, s_new


def fox_probs(q, k, fq, fk, qpos0):
    s = lax.dot_general(q, k, (((1,), (1,)), ((), ())), preferred_element_type=F32)
    r, c = _iota2(s.shape[0], s.shape[1])
    return _softmax_rows(jnp.where(c <= (r + qpos0), s + (fq - fk), -jnp.inf))


def mem_head(qm, wq, mk, mv):
    p = _softmax_rows(mm_nt(rms(qm, wq) * SCALE, mk))
    return mm(p, mv)


def _heads(x, n):
    return [x[:, h * HEAD_DIM:(h + 1) * HEAD_DIM] for h in range(n)]


def memkv_fn(mem, mnw, wkv, mknw):
    kv = mm(rms(mem, mnw), wkv)
    mk = jnp.concatenate([rms(kh, mknw) for kh in _heads(kv[:, :MEM_WIDTH], MEM_HEADS)], axis=1)
    return mk, kv[:, MEM_WIDTH:]


def dn_gates_fn(ab, alog, dtb):
    g = -jnp.exp(alog) * jax.nn.softplus(ab + dtb)
    low = _lower_ones(CHUNK)
    gc = jnp.concatenate([hdot(low, g[i * CHUNK:(i + 1) * CHUNK]) for i in range(ab.shape[0] // CHUNK)], axis=0)
    lane = lax.broadcasted_iota(jnp.int32, ab.shape, 1)
    return jnp.where(lane < N_HEADS, gc, jax.nn.sigmoid(ab))


def fox_fcum_fn(fp, fbias):
    lf = jax.nn.log_sigmoid(fp + fbias)
    low = _lower_ones(LANES)
    carry = jnp.zeros((1, fp.shape[1]), F32)
    outs = []
    for i in range(fp.shape[0] // LANES):
        cs = hdot(low, lf[i * LANES:(i + 1) * LANES]) + carry
        carry = _last_row(cs)
        outs.append(cs)
    return jnp.concatenate(outs, axis=0)


def fox_qk_fn(qraw, kraw, qnw, knw):
    q = jnp.concatenate([rms(x, qnw) * SCALE for x in _heads(qraw, N_HEADS)], axis=1)
    k = jnp.concatenate([rms(x, knw) for x in _heads(kraw, N_HEADS)], axis=1)
    return q, k


def _mem_out(qm, mqw, mk, mv):
    return [mem_head(a, mqw, b, c) for a, b, c in zip(_heads(qm, MEM_HEADS), _heads(mk, MEM_HEADS), _heads(mv, MEM_HEADS))]


def dn_out_fn(o, z, qm, onw, mqw, mk, mv):
    mix = [rms(a, onw) * jax.nn.silu(b) for a, b in zip(o, _heads(z, N_HEADS))]
    return jnp.concatenate(mix + _mem_out(qm, mqw, mk, mv), axis=1)


def fox_out_fn(o, gate, qm, mqw, mk, mv):
    return jnp.concatenate([o * jax.nn.sigmoid(gate)] + _mem_out(qm, mqw, mk, mv), axis=1)


_HBM = pl.BlockSpec(memory_space=pltpu.HBM)


def _place():
    return lax.axis_index("x"), lax.axis_index("y"), lax.axis_index("c")


class Rider:
    def __init__(self, ins, out_shape, scratch, start, finish):
        self.ins, self.out_shape, self.scratch, self.start, self.finish = list(ins), list(out_shape), list(scratch), start, finish
        self.results = None


def gather_rider(xs):
    n = len(xs)

    def plan(x_refs, out_refs, sems):
        send_sems, recv_sems, local_sems = sems
        x, y, c = _place()
        me, sibling = (x, y, c), (x, y, 1 - c)
        chips = [(1 - x, y), (x, 1 - y), (1 - x, 1 - y)]

        def copy(a, k, block, to, src=None):
            px, py, pc = block
            dst = out_refs[a].at[4 * px + 2 * py + pc]
            return pltpu.make_async_remote_copy(
                src_ref=dst if src is None else src, dst_ref=dst,
                send_sem=send_sems.at[a, k], recv_sem=recv_sems.at[a, k], device_id=to, device_id_type=MESH)

        mine = [pltpu.make_async_copy(x_refs[a], out_refs[a].at[4 * x + 2 * y + c], local_sems.at[a]) for a in range(n)]
        first = [copy(a, 0, me, sibling, src=x_refs[a]) for a in range(n)]
        first += [copy(a, 1 + j, me, (*chip, c), src=x_refs[a]) for j, chip in enumerate(chips) for a in range(n)]
        return copy, me, sibling, chips, mine, first

    def start(x_refs, out_refs, sems):
        _, _, _, _, mine, first = plan(x_refs, out_refs, sems)
        for cp in mine + first:
            cp.start()

    def finish(x_refs, out_refs, sems):
        copy, me, sibling, chips, mine, first = plan(x_refs, out_refs, sems)
        _, _, c = me
        passed = []
        for j, chip in enumerate(chips):
            for a in range(n):
                copy(a, 1 + j, (*chip, c), me).wait_recv()
                passed.append(copy(a, 4 + j, (*chip, c), sibling))
                passed[-1].start()
        for a in range(n):
            copy(a, 0, sibling, me).wait_recv()
        for j, chip in enumerate(chips):
            for a in range(n):
                copy(a, 4 + j, (*chip, 1 - c), me).wait_recv()
        for cp in first + passed:
            cp.wait_send()
        for cp in mine:
            cp.wait()

    return Rider(xs, [jax.ShapeDtypeStruct((N_DEV,) + a.shape, a.dtype) for a in xs],
                 [pltpu.SemaphoreType.DMA((n, 7)), pltpu.SemaphoreType.DMA((n, 7)), pltpu.SemaphoreType.DMA((n,))], start, finish)


def sibling_rider(gs):
    n = len(gs)

    def plan(g_refs, out_refs, sems):
        send_sems, recv_sems = sems
        x, y, c = _place()
        return [pltpu.make_async_remote_copy(
            src_ref=g_refs[a].at[2 * k + 1 - c], dst_ref=out_refs[a].at[k], send_sem=send_sems.at[a, k],
            recv_sem=recv_sems.at[a, k], device_id=(x, y, 1 - c), device_id_type=MESH) for a in range(n) for k in range(4)]

    def start(g_refs, out_refs, sems):
        for cp in plan(g_refs, out_refs, sems):
            cp.start()

    def finish(g_refs, out_refs, sems):
        copies = plan(g_refs, out_refs, sems)
        for cp in copies:
            cp.wait_recv()
        for cp in copies:
            cp.wait_send()

    return Rider(gs, [jax.ShapeDtypeStruct((4,) + g.shape[1:], g.dtype) for g in gs],
                 [pltpu.SemaphoreType.DMA((n, 4)), pltpu.SemaphoreType.DMA((n, 4))], start, finish)


def chips_rider(hs):
    n = len(hs)

    def plan(h_refs, out_refs, sems):
        send_sems, recv_sems, local_sems = sems
        x, y, c = _place()
        mine = 2 * x + y
        chips = [(1 - x, y), (x, 1 - y), (1 - x, 1 - y)]
        keep = [pltpu.make_async_copy(h_refs[a].at[mine], out_refs[a].at[mine], local_sems.at[a]) for a in range(n)]
        sends = [pltpu.make_async_remote_copy(
            src_ref=h_refs[a].at[2 * qx + qy], dst_ref=out_refs[a].at[mine], send_sem=send_sems.at[a, j],
            recv_sem=recv_sems.at[a, j], device_id=(qx, qy, c), device_id_type=MESH)
            for j, (qx, qy) in enumerate(chips) for a in range(n)]
        recvs = [pltpu.make_async_remote_copy(
            src_ref=h_refs[a].at[mine], dst_ref=out_refs[a].at[2 * qx + qy], send_sem=send_sems.at[a, j],
            recv_sem=recv_sems.at[a, j], device_id=(qx, qy, c), device_id_type=MESH)
            for j, (qx, qy) in enumerate(chips) for a in range(n)]
        return keep, sends, recvs

    def start(h_refs, out_refs, sems):
        keep, sends, _ = plan(h_refs, out_refs, sems)
        for cp in keep + sends:
            cp.start()

    def finish(h_refs, out_refs, sems):
        keep, sends, recvs = plan(h_refs, out_refs, sems)
        for cp in recvs:
            cp.wait_recv()
        for cp in sends:
            cp.wait_send()
        for cp in keep:
            cp.wait()

    return Rider(hs, [jax.ShapeDtypeStruct(h.shape, h.dtype) for h in hs],
                 [pltpu.SemaphoreType.DMA((n, 3)), pltpu.SemaphoreType.DMA((n, 3)), pltpu.SemaphoreType.DMA((n,))], start, finish)


def hosted_call(riders, body, *, out_shape, in_specs, out_specs, grid=(), scratch_shapes=(), **kw):
    riders = tuple(riders or ())
    if not riders:
        return pl.pallas_call(body, out_shape=out_shape, in_specs=in_specs, out_specs=out_specs, grid=grid,
                              scratch_shapes=scratch_shapes, **kw)
    single = not isinstance(out_shape, (list, tuple))
    k_out_shape = [out_shape] if single else list(out_shape)
    k_out_specs = [out_specs] if single else list(out_specs)
    n_in, n_out, n_scr = len(in_specs), len(k_out_shape), len(scratch_shapes)
    r_ins = [a for r in riders for a in r.ins]
    r_outs = [s for r in riders for s in r.out_shape]
    r_scr = [s for r in riders for s in r.scratch]

    def full_body(*refs):
        ins = refs[:n_in + len(r_ins)]
        outs = refs[n_in + len(r_ins):n_in + len(r_ins) + n_out + len(r_outs)]
        scr = refs[n_in + len(r_ins) + n_out + len(r_outs):]
        ids = [pl.program_id(d) for d in range(len(grid))]
        first = functools.reduce(jnp.logical_and, [i == 0 for i in ids]) if ids else None
        last = functools.reduce(jnp.logical_and, [i == g - 1 for i, g in zip(ids, grid)]) if ids else None

        def each(method):
            i0, o0, s0 = n_in, n_out, n_scr
            for r in riders:
                getattr(r, method)(ins[i0:i0 + len(r.ins)], outs[o0:o0 + len(r.out_shape)], scr[s0:s0 + len(r.scratch)])
                i0, o0, s0 = i0 + len(r.ins), o0 + len(r.out_shape), s0 + len(r.scratch)

        if first is None:
            each("start")
        else:
            pl.when(first)(lambda: each("start"))
        body(*ins[:n_in], *outs[:n_out], *scr[:n_scr])
        if last is None:
            each("finish")
        else:
            pl.when(last)(lambda: each("finish"))

    call = pl.pallas_call(
        full_body, out_shape=k_out_shape + r_outs, in_specs=list(in_specs) + [_HBM] * len(r_ins),
        out_specs=k_out_specs + [_HBM] * len(r_outs), grid=grid, scratch_shapes=list(scratch_shapes) + r_scr, **kw)

    def run(*args):
        res = call(*args, *r_ins)
        o0 = n_out
        for r in riders:
            r.results = list(res[o0:o0 + len(r.out_shape)])
            o0 += len(r.out_shape)
        return res[0] if single else list(res[:n_out])

    return run


def run_riders(riders, *, name):
    hosted_call(riders, lambda: None, name=name, out_shape=[], in_specs=[], out_specs=[])()
    return [r.results for r in riders]


def _pick(n, cands):
    for c in cands:
        if n % c == 0:
            return c
    return n


def _params(sem):
    return pltpu.CompilerParams(dimension_semantics=sem, vmem_limit_bytes=VMEM_LIMIT)


MATMUL_VMEM_BUDGET = 40 * 2 ** 20


def _matmul_tiles(m, n, k, bytes_a, bytes_b, bytes_mn, fixed):
    fm, fn, fk = fixed if fixed is not None else (None, None, None)

    def options(given, size, cands):
        return [given] if given else ([c for c in cands if size % c == 0] or [size])

    best = None
    for tm in options(fm, m, (2048, 1024, 512, 256, 128)):
        for tn in options(fn, n, (512, 256, 128)):
            for tk in options(fk, k, (2048, 1536, 1024, 512, 256, 128)):
                if 2 * (tm * tk * bytes_a + tk * tn * bytes_b + tm * tn * bytes_mn) + tm * tn * 4 > MATMUL_VMEM_BUDGET:
                    continue
                key = ((m // tm) * (n // tn) * (k // tk), -tk)
                if best is None or key < best[0]:
                    best = (key, (tm, tn, tk))
    assert best is not None, (m, n, k, fixed)
    return best[1]


def matmul(a, b, *, name, ta=False, tb=False, post=None, post_ins=(), extra_out=None, out_dtype=F32, tiles=None,
           b_view=None, out_view=None, riders=()):
    (k, m) = a.shape if ta else a.shape[::-1]
    (kb, n) = b_view[:2] if b_view is not None else (b.shape[::-1] if tb else b.shape)
    assert k == kb, (a.shape, b.shape, ta, tb)
    bytes_mn = sum(p.dtype.itemsize for p in post_ins) + jnp.dtype(out_dtype).itemsize
    bytes_mn += jnp.dtype(extra_out[1]).itemsize if extra_out else 0
    tm, tn, tk = _matmul_tiles(m, n, k, a.dtype.itemsize, b.dtype.itemsize, bytes_mn, tiles)
    nk = k // tk
    dims = ((0,) if ta else (1,), (1,) if tb else (0,))
    n_post = len(post_ins)
    n_out = 2 if extra_out else 1

    def body(*refs):
        a_ref, b_ref = refs[:2]
        post_refs = refs[2:2 + n_post]
        o_refs, acc = refs[-1 - n_out:-1], refs[-1]
        kk = pl.program_id(2)

        @pl.when(kk == 0)
        def _():
            acc[...] = jnp.zeros_like(acc)

        b_tile = b_ref[...]
        acc[...] += _bdot(a_ref[...], b_tile.reshape(-1, b_tile.shape[-1]), dims)

        @pl.when(kk == nk - 1)
        def _():
            r = acc[...]
            if post is not None:
                r = post(r, *[p[...] for p in post_refs])
            o_refs[0][...] = r.astype(out_dtype)
            if extra_out:
                o_refs[1][...] = extra_out[0](r).astype(extra_out[1])

    a_spec = pl.BlockSpec((tk, tm), lambda i, j, kk: (kk, i)) if ta else pl.BlockSpec((tm, tk), lambda i, j, kk: (i, kk))
    if b_view is not None:
        b_spec = b_view[2]
    else:
        b_spec = pl.BlockSpec((tn, tk), lambda i, j, kk: (j, kk)) if tb else pl.BlockSpec((tk, tn), lambda i, j, kk: (kk, j))
    mn_spec = pl.BlockSpec((tm, tn), lambda i, j, kk: (i, j))
    o_shape, o_spec, into = ((m, n), mn_spec, None) if out_view is None else out_view
    ins, specs, aliases = [a, b, *post_ins], [a_spec, b_spec] + [mn_spec] * n_post, {}
    if into is not None:
        aliases = {len(ins): 0}
        ins.append(into)
        specs.append(pl.BlockSpec(memory_space=pl.ANY))
    out_shape = [jax.ShapeDtypeStruct(o_shape, out_dtype)]
    out_specs = [o_spec]
    if extra_out:
        out_shape.append(jax.ShapeDtypeStruct((m, n), extra_out[1]))
        out_specs.append(mn_spec)
    res = hosted_call(
        riders, body, name=name, grid=(m // tm, n // tn, nk), in_specs=specs, out_specs=out_specs, out_shape=out_shape,
        input_output_aliases=aliases, scratch_shapes=[pltpu.VMEM((tm, tn), F32)],
        compiler_params=_params(("parallel", "parallel", "arbitrary")),
    )(*ins)
    return res if extra_out else res[0]


def rows_call(fn, row_ins, full_ins, row_outs, acc_outs, *, tm, name, riders=()):
    row_ins = [r if isinstance(r, tuple) else (r, r.shape[-1], 0) for r in row_ins]
    t = row_ins[0][0].shape[-2]
    tm = min(tm, t)
    n_in = len(row_ins) + len(full_ins)
    n_row = len(row_outs)

    def body(*refs):
        res = fn(*[[r[h] for h in range(r.shape[0])] if (i < len(row_ins) and len(r.shape) == 3) else r[...]
                   for i, r in enumerate(refs[:n_in])])
        res = res if isinstance(res, (tuple, list)) else (res,)
        outs = refs[n_in:]
        for ref, val in zip(outs[:n_row], res[:n_row]):
            if len(ref.shape) == 3:
                for h, vh in enumerate(val):
                    ref[h] = vh.astype(ref.dtype)
            else:
                ref[...] = val.astype(ref.dtype)
        first = pl.program_id(0) == 0
        for ref, val in zip(outs[n_row:], res[n_row:]):
            @pl.when(first)
            def _(ref=ref, val=val):
                ref[...] = val

            @pl.when(jnp.logical_not(first))
            def _(ref=ref, val=val):
                ref[...] += val

    def full_spec(shape):
        return pl.BlockSpec(shape, lambda i, nd=len(shape): (0,) * nd)

    def row_spec(lead, w, cb):
        if lead is None:
            return pl.BlockSpec((tm, w), lambda i: (i, cb))
        return pl.BlockSpec((lead, tm, w), lambda i: (0, i, cb))

    def lead_cols(c):
        return c if isinstance(c, tuple) else (None, c)

    in_specs = [row_spec(a.shape[0] if a.ndim == 3 else None, w, cb) for (a, w, cb) in row_ins]
    in_specs += [full_spec(f.shape) for f in full_ins]
    out_specs = [row_spec(*lead_cols(c), 0) for c, _ in row_outs] + [full_spec(s) for s in acc_outs]
    out_shape = [jax.ShapeDtypeStruct(tuple(d for d in (lead_cols(c)[0], t, lead_cols(c)[1]) if d is not None), dt)
                 for c, dt in row_outs] + [jax.ShapeDtypeStruct(s, F32) for s in acc_outs]
    res = hosted_call(
        riders, body, name=name, grid=(t // tm,), in_specs=in_specs, out_specs=out_specs, out_shape=out_shape,
        compiler_params=_params(("arbitrary",)),
    )(*[r[0] for r in row_ins], *full_ins)
    return res


def vjp_rows(fn, n_diff_row, row_diff_full):
    def bwd(*args, n_row, n_ct):
        prim_rows = args[:n_row]
        cts = args[n_row:n_row + n_ct]
        fulls = args[n_row + n_ct:]
        _, vjp = jax.vjp(fn, *prim_rows, *fulls)
        g = vjp(cts[0] if n_ct == 1 else tuple(cts))
        out = list(g[:n_diff_row])
        out += [gf for gf, d in zip(g[n_row:], row_diff_full) if d]
        return tuple(out)
    return bwd


def _shift_down(x, s):
    if s == 0:
        return x
    t = lax.broadcasted_iota(jnp.int32, x.shape, 0)
    return jnp.where(t >= s, pltpu.roll(x, s, 0), 0.0)


def _shift_up(x, s):
    if s == 0:
        return x
    n = x.shape[0]
    t = lax.broadcasted_iota(jnp.int32, x.shape, 0)
    return jnp.where(t < n - s, pltpu.roll(x, n - s, 0), 0.0)


def _conv(x, w_ref):
    return sum(w_ref[pl.ds(j, 1), :] * _shift_down(x, CONV_WIDTH - 1 - j) for j in range(CONV_WIDTH))


_DN_POST = (lambda c: l2n(jax.nn.silu(c)) * SCALE, lambda c: l2n(jax.nn.silu(c)), jax.nn.silu)


def dn_prep_fwd(proj, conv_w, *, name, riders=()):
    t = proj.shape[0]

    def body(xq, xk, xv, wq, wk, wv, oq, ok, ov):
        for x_ref, w_ref, o_ref, post in zip((xq, xk, xv), (wq, wk, wv), (oq, ok, ov), _DN_POST):
            o_ref[...] = post(_conv(x_ref[...], w_ref))

    x_specs = [pl.BlockSpec((t, HEAD_DIM), lambda h, g=g: (0, g * N_HEADS + h)) for g in range(3)]
    w_specs = [pl.BlockSpec((CONV_WIDTH, HEAD_DIM), lambda h, g=g: (0, g * N_HEADS + h)) for g in range(3)]
    o_spec = pl.BlockSpec((None, t, HEAD_DIM), lambda h: (h, 0, 0))
    return hosted_call(
        riders, body, name=name, grid=(N_HEADS,), in_specs=x_specs + w_specs, out_specs=[o_spec] * 3,
        out_shape=[jax.ShapeDtypeStruct((N_HEADS, t, HEAD_DIM), F32)] * 3, compiler_params=_params(("parallel",)),
    )(proj, proj, proj, conv_w, conv_w, conv_w)


def dn_prep_bwd(proj, conv_w, dq, dk, dv, *, name, riders=()):
    t = proj.shape[0]

    def body(xq, xk, xv, wq, wk, wv, gq, gk, gv, dxq, dxk, dxv, dwq, dwk, dwv):
        for x_ref, w_ref, g_ref, dx_ref, dw_ref, post in zip(
                (xq, xk, xv), (wq, wk, wv), (gq, gk, gv), (dxq, dxk, dxv), (dwq, dwk, dwv), _DN_POST):
            x = x_ref[...]
            _, vjp = jax.vjp(post, _conv(x, w_ref))
            dc, = vjp(g_ref[...])
            dx = sum(w_ref[pl.ds(j, 1), :] * _shift_up(dc, CONV_WIDTH - 1 - j) for j in range(CONV_WIDTH))
            dx_ref[...] = dx.astype(dx_ref.dtype)
            for j in range(CONV_WIDTH):
                dw_ref[pl.ds(j, 1), :] = jnp.sum(dc * _shift_down(x, CONV_WIDTH - 1 - j), axis=0, keepdims=True)

    x_specs = [pl.BlockSpec((t, HEAD_DIM), lambda h, g=g: (0, g * N_HEADS + h)) for g in range(3)]
    w_specs = [pl.BlockSpec((CONV_WIDTH, HEAD_DIM), lambda h, g=g: (0, g * N_HEADS + h)) for g in range(3)]
    g_spec = pl.BlockSpec((None, t, HEAD_DIM), lambda h: (h, 0, 0))
    dx_spec = pl.BlockSpec((t, HEAD_DIM), lambda h: (0, h))
    dw_spec = pl.BlockSpec((CONV_WIDTH, HEAD_DIM), lambda h: (0, h))
    return hosted_call(
        riders, body, name=name, grid=(N_HEADS,), in_specs=x_specs + w_specs + [g_spec] * 3, out_specs=[dx_spec] * 3 + [dw_spec] * 3,
        out_shape=[jax.ShapeDtypeStruct((t, D_MODEL), BF16)] * 3 + [jax.ShapeDtypeStruct((CONV_WIDTH, D_MODEL), F32)] * 3,
        compiler_params=_params(("parallel",)),
    )(proj, proj, proj, conv_w, conv_w, conv_w, dq, dk, dv)


INTRA_CHUNKS = 4


def _lane_column(x, lane_index):
    lane = lax.broadcasted_iota(jnp.int32, x.shape, 1)
    return jnp.sum(jnp.where(lane == lane_index, x, 0.0), axis=1, keepdims=True)


def _head_columns(g, first_lane):
    return jnp.concatenate([_lane_column(g, first_lane + h)[None] for h in range(N_HEADS)], axis=0)


def _intra_of_gates(q, k, v, gates):
    nb = N_HEADS * (gates.shape[0] // CHUNK)

    def chunks(x):
        return x.reshape(nb, CHUNK, x.shape[-1])

    res = delta_intra(chunks(q), chunks(k), chunks(v), chunks(_head_columns(gates, 0)), chunks(_head_columns(gates, N_HEADS)))
    return tuple(x.reshape(N_HEADS, -1, x.shape[-1]) for x in res)


def _step_of_gates(s, q, k, gates, u, w, qk):
    return delta_step(s, q, k, _head_columns(gates, 0), u, w, qk)


def _head_major(rows, w, index):
    return pl.BlockSpec((N_HEADS, rows, w), lambda i: (0, index(i), 0))


def delta_intra_fwd(q, k, v, gates, *, name, riders=()):
    t = q.shape[1]
    rows = min(INTRA_CHUNKS, t // CHUNK) * CHUNK

    def body(q_ref, k_ref, v_ref, g_ref, u_ref, w_ref, qk_ref):
        for ref, val in zip((u_ref, w_ref, qk_ref), _intra_of_gates(q_ref[...], k_ref[...], v_ref[...], g_ref[...])):
            ref[...] = val

    x_spec, qk_spec = (_head_major(rows, w, lambda i: i) for w in (HEAD_DIM, CHUNK))
    g_spec = pl.BlockSpec((rows, LANES), lambda i: (i, 0))
    return hosted_call(
        riders, body, name=name, grid=(t // rows,), in_specs=[x_spec] * 3 + [g_spec], out_specs=[x_spec, x_spec, qk_spec],
        out_shape=[jax.ShapeDtypeStruct((N_HEADS, t, HEAD_DIM), F32)] * 2 + [jax.ShapeDtypeStruct((N_HEADS, t, CHUNK), F32)],
        compiler_params=_params(("parallel",)),
    )(q, k, v, gates)


def delta_seq_fwd(q, k, gates, u, w, qk, *, name, riders=()):
    t = q.shape[1]
    nc = t // CHUNK

    def body(q_ref, k_ref, g_ref, u_ref, w_ref, qk_ref, o_ref, s0_ref, s_ref):
        @pl.when(pl.program_id(0) == 0)
        def _():
            s_ref[...] = jnp.zeros_like(s_ref)

        s = s_ref[...]
        s0_ref[...] = s
        o, s_new = _step_of_gates(s, q_ref[...], k_ref[...], g_ref[...], u_ref[...], w_ref[...], qk_ref[...])
        o_ref[...] = o
        s_ref[...] = s_new

    x_spec, qk_spec = (_head_major(CHUNK, w, lambda c: c) for w in (HEAD_DIM, CHUNK))
    g_spec = pl.BlockSpec((CHUNK, LANES), lambda c: (c, 0))
    s_spec = pl.BlockSpec((N_HEADS, None, HEAD_DIM, HEAD_DIM), lambda c: (0, c, 0, 0))
    return hosted_call(
        riders, body, name=name, grid=(nc,), in_specs=[x_spec, x_spec, g_spec, x_spec, x_spec, qk_spec], out_specs=[x_spec, s_spec],
        out_shape=[jax.ShapeDtypeStruct((N_HEADS, t, HEAD_DIM), F32),
                   jax.ShapeDtypeStruct((N_HEADS, nc, HEAD_DIM, HEAD_DIM), F32)],
        scratch_shapes=[pltpu.VMEM((N_HEADS, HEAD_DIM, HEAD_DIM), F32)],
        compiler_params=_params(("arbitrary",)),
    )(q, k, gates, u, w, qk)


def delta_seq_bwd(q, k, gates, u, w, qk, s0, do, *, name, riders=()):
    t = q.shape[1]
    nc = t // CHUNK

    def body(q_ref, k_ref, g_ref, u_ref, w_ref, qk_ref, s0_ref, do_ref,
             dq_ref, dk_ref, dg_ref, du_ref, dw_ref, dqk_ref, ds_ref):
        @pl.when(pl.program_id(0) == 0)
        def _():
            ds_ref[...] = jnp.zeros_like(ds_ref)

        _, vjp = jax.vjp(_step_of_gates, s0_ref[...], q_ref[...], k_ref[...], g_ref[...], u_ref[...], w_ref[...], qk_ref[...])
        ds, dq, dk, dg, du, dw, dqk = vjp((do_ref[...], ds_ref[...]))
        for ref, val in zip((ds_ref, dq_ref, dk_ref, dg_ref, du_ref, dw_ref, dqk_ref), (ds, dq, dk, dg, du, dw, dqk)):
            ref[...] = val

    x_spec, qk_spec = (_head_major(CHUNK, w, lambda c: nc - 1 - c) for w in (HEAD_DIM, CHUNK))
    g_spec = pl.BlockSpec((CHUNK, LANES), lambda c: (nc - 1 - c, 0))
    s_spec = pl.BlockSpec((N_HEADS, None, HEAD_DIM, HEAD_DIM), lambda c: (0, nc - 1 - c, 0, 0))
    head_shape = [jax.ShapeDtypeStruct((N_HEADS, t, w_), F32) for w_ in (HEAD_DIM, HEAD_DIM, HEAD_DIM, HEAD_DIM, CHUNK)]
    return hosted_call(
        riders, body, name=name, grid=(nc,), in_specs=[x_spec, x_spec, g_spec, x_spec, x_spec, qk_spec, s_spec, x_spec],
        out_specs=[x_spec, x_spec, g_spec, x_spec, x_spec, qk_spec],
        out_shape=head_shape[:2] + [jax.ShapeDtypeStruct((t, LANES), F32)] + head_shape[2:],
        scratch_shapes=[pltpu.VMEM((N_HEADS, HEAD_DIM, HEAD_DIM), F32)],
        compiler_params=_params(("arbitrary",)),
    )(q, k, gates, u, w, qk, s0, do)


def delta_intra_bwd(q, k, v, gates, du, dw, dqk, dq_s, dk_s, dg_s, *, name, riders=()):
    t = q.shape[1]
    rows = min(INTRA_CHUNKS, t // CHUNK) * CHUNK

    def body(q_ref, k_ref, v_ref, g_ref, du_ref, dw_ref, dqk_ref, dqs_ref, dks_ref, dgs_ref, dq_ref, dk_ref, dv_ref, dg_ref):
        _, vjp = jax.vjp(_intra_of_gates, q_ref[...], k_ref[...], v_ref[...], g_ref[...])
        dq, dk, dv, dg = vjp((du_ref[...], dw_ref[...], dqk_ref[...]))
        dq_ref[...] = dq + dqs_ref[...]
        dk_ref[...] = dk + dks_ref[...]
        dv_ref[...] = dv
        dg_ref[...] = dg + dgs_ref[...]

    x_spec, qk_spec = (_head_major(rows, w, lambda i: i) for w in (HEAD_DIM, CHUNK))
    g_spec = pl.BlockSpec((rows, LANES), lambda i: (i, 0))
    return hosted_call(
        riders, body, name=name, grid=(t // rows,),
        in_specs=[x_spec] * 3 + [g_spec, x_spec, x_spec, qk_spec, x_spec, x_spec, g_spec],
        out_specs=[x_spec] * 3 + [g_spec],
        out_shape=[jax.ShapeDtypeStruct((N_HEADS, t, HEAD_DIM), F32)] * 3 + [jax.ShapeDtypeStruct((t, LANES), F32)],
        compiler_params=_params(("parallel",)),
    )(q, k, v, gates, du, dw, dqk, dq_s, dk_s, dg_s)


_V_BLOCK = 2 * N_HEADS
FOX_GROUPS = 8


def _fox_groups(t):
    nq = t // Q_BLOCK
    per = max(1, nq // FOX_GROUPS)
    return [(g0, per, (g0 + per) * Q_BLOCK) for g0 in range(0, nq, per)]


def fox_attn_fwd(q, k, proj, fq, fk, *, name, riders=()):
    t = q.shape[0]

    def body(q_ref, k_ref, v_ref, fq_ref, fk_ref, o_ref, kb_ref, vb_ref):
        head = pl.program_id(0)
        kb_ref[...] = k_ref[...].astype(BF16)
        vb_ref[...] = v_ref[...].astype(BF16)
        for g0, per, keys in _fox_groups(t):
            def block(j, carry, g0=g0, keys=keys):
                rows = pl.ds(pl.multiple_of((g0 + j) * Q_BLOCK, Q_BLOCK), Q_BLOCK)
                p = fox_probs(q_ref[rows, :].astype(BF16), kb_ref[0:keys, :], _lane_column(fq_ref[rows, :], head),
                              fk_ref[:, 0:keys], (g0 + j) * Q_BLOCK)
                o_ref[rows, :] = jnp.dot(p.astype(BF16), vb_ref[0:keys, :], preferred_element_type=F32)
                return carry
            lax.fori_loop(0, per, block, 0)

    x_spec = pl.BlockSpec((t, HEAD_DIM), lambda h: (0, h))
    v_spec = pl.BlockSpec((t, HEAD_DIM), lambda h: (0, _V_BLOCK + h))
    fq_spec = pl.BlockSpec((t, LANES), lambda h: (0, 0))
    fk_spec = pl.BlockSpec((None, 1, t), lambda h: (h, 0, 0))
    return hosted_call(
        riders, body, name=name, grid=(N_HEADS,), in_specs=[x_spec, x_spec, v_spec, fq_spec, fk_spec], out_specs=x_spec,
        out_shape=jax.ShapeDtypeStruct((t, D_MODEL), F32), scratch_shapes=[pltpu.VMEM((t, HEAD_DIM), BF16)] * 2,
        compiler_params=_params(("parallel",)),
    )(q, k, proj, fq, fk)


def fox_attn_bwd(q, k, proj, fq, fk, do, *, name, riders=()):
    t = q.shape[0]

    def body(q_ref, k_ref, v_ref, fq_ref, fk_ref, do_ref, dq_ref, dk_ref, dv_out_ref, dfq_ref, dfk_ref, kb_ref, vb_ref, dv_ref):
        head = pl.program_id(0)

        @pl.when(head == 0)
        def _():
            dfq_ref[...] = jnp.zeros_like(dfq_ref)

        kb_ref[...] = k_ref[...].astype(BF16)
        vb_ref[...] = v_ref[...].astype(BF16)
        dk_ref[...] = jnp.zeros_like(dk_ref)
        dv_ref[...] = jnp.zeros_like(dv_ref)
        dfk_ref[...] = jnp.zeros_like(dfk_ref)
        nt = (((1,), (1,)), ((), ()))
        tn = (((0,), (0,)), ((), ()))
        for g0, per, keys in _fox_groups(t):
            def block(j, carry, g0=g0, keys=keys):
                rows = pl.ds(pl.multiple_of((g0 + j) * Q_BLOCK, Q_BLOCK), Q_BLOCK)
                qb, dob = q_ref[rows, :].astype(BF16), do_ref[rows, :].astype(BF16)
                kb, vb = kb_ref[0:keys, :], vb_ref[0:keys, :]
                p = fox_probs(qb, kb, _lane_column(fq_ref[rows, :], head), fk_ref[:, 0:keys], (g0 + j) * Q_BLOCK)
                dp = lax.dot_general(dob, vb, nt, preferred_element_type=F32)
                dz = p * (dp - jnp.sum(dp * p, axis=-1, keepdims=True))
                pb, dzb = p.astype(BF16), dz.astype(BF16)
                dq_ref[rows, :] = jnp.dot(dzb, kb, preferred_element_type=F32)
                lane = lax.broadcasted_iota(jnp.int32, (Q_BLOCK, LANES), 1)
                dfq_ref[rows, :] += jnp.where(lane == head, jnp.sum(dz, axis=-1, keepdims=True), 0.0)
                dk_ref[0:keys, :] += lax.dot_general(dzb, qb, tn, preferred_element_type=F32)
                dv_ref[0:keys, :] += lax.dot_general(pb, dob, tn, preferred_element_type=F32)
                dfk_ref[:, 0:keys] -= jnp.sum(dz, axis=0, keepdims=True)
                return carry
            lax.fori_loop(0, per, block, 0)
        dv_out_ref[...] = dv_ref[...].astype(dv_out_ref.dtype)

    x_spec = pl.BlockSpec((t, HEAD_DIM), lambda h: (0, h))
    v_spec = pl.BlockSpec((t, HEAD_DIM), lambda h: (0, _V_BLOCK + h))
    fq_spec = pl.BlockSpec((t, LANES), lambda h: (0, 0))
    fk_spec = pl.BlockSpec((None, 1, t), lambda h: (h, 0, 0))
    return hosted_call(
        riders, body, name=name, grid=(N_HEADS,), in_specs=[x_spec, x_spec, v_spec, fq_spec, fk_spec, x_spec],
        out_specs=[x_spec, x_spec, x_spec, fq_spec, fk_spec],
        out_shape=[jax.ShapeDtypeStruct((t, D_MODEL), F32)] * 2 + [jax.ShapeDtypeStruct((t, D_MODEL), BF16)]
        + [jax.ShapeDtypeStruct((t, LANES), F32), jax.ShapeDtypeStruct((N_HEADS, 1, t), F32)],
        scratch_shapes=[pltpu.VMEM((t, HEAD_DIM), BF16)] * 2 + [pltpu.VMEM((t, HEAD_DIM), F32)],
        compiler_params=_params(("arbitrary",)),
    )(q, k, proj, fq, fk, do)


def memkv_fwd(mem, mnw, wkv, mknw, *, name):
    n = mem.shape[0]

    def body(mem_ref, mnw_ref, w_ref, mknw_ref, mk_ref, mv_ref):
        mk, mv = memkv_fn(mem_ref[...], mnw_ref[...], w_ref[...], mknw_ref[...])
        mk_ref[...] = mk
        mv_ref[...] = mv

    return pl.pallas_call(
        body, name=name, out_shape=[jax.ShapeDtypeStruct((n, MEM_WIDTH), F32)] * 2,
        compiler_params=pltpu.CompilerParams(vmem_limit_bytes=VMEM_LIMIT),
    )(mem, mnw, wkv, mknw)


def memkv_bwd(mem, mnw, wkv, mknw, dmk, dmv, *, name):
    def body(mem_ref, mnw_ref, w_ref, mknw_ref, dmk_ref, dmv_ref, dmnw_ref, dw_ref, dmknw_ref):
        f = functools.partial(memkv_fn, mem_ref[...])
        _, vjp = jax.vjp(f, mnw_ref[...], w_ref[...].astype(F32), mknw_ref[...])
        dmnw, dw, dmknw = vjp((dmk_ref[...], dmv_ref[...]))
        dmnw_ref[...] = dmnw
        dw_ref[...] = dw.astype(dw_ref.dtype)
        dmknw_ref[...] = dmknw

    return pl.pallas_call(
        body, name=name,
        out_shape=[jax.ShapeDtypeStruct(mnw.shape, F32), jax.ShapeDtypeStruct(wkv.shape, BF16), jax.ShapeDtypeStruct(mknw.shape, F32)],
        compiler_params=pltpu.CompilerParams(vmem_limit_bytes=VMEM_LIMIT),
    )(mem, mnw, wkv, mknw, dmk, dmv)


def _row(v, width=None):
    v = v.reshape(1, -1)
    if width is not None and v.shape[1] < width:
        v = jnp.pad(v, ((0, 0), (0, width - v.shape[1])))
    return v


def _norm_fwd(x, w, name, riders=()):
    return rows_call(lambda x, w: rms(x, w), [x], [w], [(D_MODEL, BF16)], [], tm=512, name=name, riders=riders)[0]


def _norm_bwd(x, w, dh, dx_in, name, riders=()):
    def fn(x, dh, dx_in, w):
        _, vjp = jax.vjp(rms, x, w)
        dx, dw = vjp(dh)
        return dx + dx_in, dw
    return rows_call(fn, [x, dh, dx_in], [w], [(D_MODEL, F32)], [(1, D_MODEL)], tm=512, name=name, riders=riders)


FF_PIECE = D_FF // N_DEV


def _add(r, x):
    return r + x


def _piece(rows, cols, index):
    return pl.BlockSpec((None, rows, cols), lambda i, j, kk: (index(i, j, kk), 0, 0))


def _two_pieces(rows, cols, index):
    return pl.BlockSpec((2, rows, cols), lambda i, j, kk: (index(i, j, kk), 0, 0))


def _mlp_fwd(x, n2w, w1, w2, layer, riders=()):
    riders = list(riders) + [None, None]
    h2 = _norm_fwd(x, n2w, f"norm2_fwd_{layer}")
    u, a1 = matmul(h2, w1, name=f"mlp1_fwd_{layer}", tiles=(None, FF_PIECE, D_MODEL),
                   extra_out=(lambda u: jnp.square(jnp.maximum(u, 0.0)), BF16),
                   b_view=(D_MODEL, D_FF, _piece(D_MODEL, FF_PIECE, lambda i, j, kk: j)), riders=riders[0])
    y = matmul(a1, w2, name=f"mlp2_fwd_{layer}", post=_add, post_ins=[x], tiles=(None, D_MODEL, 2 * FF_PIECE),
               b_view=(D_FF, D_MODEL, _two_pieces(FF_PIECE, D_MODEL, lambda i, j, kk: kk)), riders=riders[1])
    return y, (x, h2, u, a1)


def pair_sum(g, got, *, name):
    _, rows, cols = g.shape
    tile = _pick(rows, (512, 256, 128))
    c = lax.axis_index("c").astype(jnp.int32).reshape(1)

    def body(c_ref, a_ref, b_ref, o_ref):
        o_ref[...] = (a_ref[...].astype(F32) + b_ref[...].astype(F32)).astype(o_ref.dtype)

    grid_spec = pltpu.PrefetchScalarGridSpec(
        num_scalar_prefetch=1, grid=(4, rows // tile),
        in_specs=[pl.BlockSpec((None, tile, cols), lambda k, i, c_ref: (2 * k + c_ref[0], i, 0)),
                  pl.BlockSpec((None, tile, cols), lambda k, i, c_ref: (k, i, 0))],
        out_specs=pl.BlockSpec((None, tile, cols), lambda k, i, c_ref: (k, i, 0)))
    return pl.pallas_call(
        body, name=name, grid_spec=grid_spec, out_shape=jax.ShapeDtypeStruct((4, rows, cols), g.dtype),
        compiler_params=_params(("parallel", "parallel")),
    )(c, g, got)


def chip_sums(names, pieces, gots):
    return [pair_sum(a, got, name=f"grads_pair_sum_{n}") for n, a, got in zip(names, pieces, gots)]


def _mlp_bwd(dy, res, n2w, w1, w2, layer, riders=()):
    x, h2, u, a1 = res
    du = matmul(dy, w2, tb=True, name=f"mlp2_dx_{layer}", out_dtype=BF16, tiles=(None, 2 * FF_PIECE, D_MODEL),
                post=lambda r, u: r * (2.0 * jnp.maximum(u, 0.0)), post_ins=[u],
                b_view=(D_MODEL, D_FF, _two_pieces(FF_PIECE, D_MODEL, lambda i, j, kk: j)), riders=riders)
    dw2 = matmul(a1, dy, ta=True, name=f"mlp2_dw_{layer}", out_dtype=BF16, tiles=(FF_PIECE, D_MODEL, None), out_view=(
        w2.shape, _piece(FF_PIECE, D_MODEL, lambda i, j, kk: i), None))
    sib2 = sibling_rider([dw2])
    dh2 = matmul(du, w1, tb=True, name=f"mlp1_dx_{layer}", tiles=(None, D_MODEL, FF_PIECE),
                 b_view=(D_FF, D_MODEL, _piece(D_MODEL, FF_PIECE, lambda i, j, kk: kk)), riders=[sib2])
    dw1 = matmul(h2, du, ta=True, name=f"mlp1_dw_{layer}", out_dtype=BF16, tiles=(D_MODEL, FF_PIECE, None), out_view=(
        w1.shape, _piece(D_MODEL, FF_PIECE, lambda i, j, kk: j), None))
    sib1 = sibling_rider([dw1])
    dx, dn2w = _norm_bwd(x, n2w, dh2, dy, f"norm2_bwd_{layer}", riders=[sib1])
    return dx, dw1, dw2, dn2w, sib1, sib2


def _in_proj_bwd(h, dmain, dsmall, w_main, w_small, tag):
    dh = matmul(dmain, w_main, tb=True, name=f"inproj_dx_main_{tag}")
    dh = matmul(dsmall, w_small, tb=True, post=_add, post_ins=[dh], name=f"inproj_dx_small_{tag}")
    dw_main = matmul(h, dmain, ta=True, out_dtype=BF16, name=f"inproj_dw_main_{tag}")
    dw_small = matmul(h, dsmall, ta=True, out_dtype=BF16, name=f"inproj_dw_small_{tag}")
    return dh, dw_main, dw_small


def local_step(x, mem, target, w, m, v):
    t = x.shape[0]
    n_mem = mem.shape[0]
    g = {}

    def wire(a):
        return a.astype(BF16)

    (dn_g,), = run_riders([gather_rider([wire(w["dn_w_in"][0])])], name="weights_gather_first")
    dn_main, dn_ab = in_proj_weights(dn_g, DN_IN, 2 * N_HEADS)
    fox_w = wire(w["fox_w_in"][0])
    ride_out = gather_rider([wire(w["w_out"][0]), wire(w["w_out"][1]), w["dn_conv_w"][0]])
    ride_kv = gather_rider([wire(w["w_mem_kv"])])
    ride_mlp1_0 = gather_rider([wire(w["w_mlp1"][0])])
    ride_mlp2_0 = gather_rider([wire(w["w_mlp2"][0])])
    ride_fox_a, ride_fox_b = gather_rider([fox_w[:D_MODEL // 2]]), gather_rider([fox_w[D_MODEL // 2:]])
    ride_mlp_1 = gather_rider([wire(w["w_mlp1"][1]), wire(w["w_mlp2"][1])])
    mnw, mknw = _row(w["mem_norm_w"]), _row(w["mem_k_norm_w"])

    n1w0, n2w0 = _row(w["norm1_w"][0]), _row(w["norm2_w"][0])
    alog, dtb = _row(w["dn_a_log"][0], LANES), _row(w["dn_dt_bias"][0], LANES)
    onw, mqw0 = _row(w["dn_o_norm_w"][0]), _row(w["memq_norm_w"][0])
    x0 = x
    h0 = _norm_fwd(x0, n1w0, "norm1_fwd_0")
    pm0 = matmul(h0, dn_main, name="inproj_main_0", riders=[ride_out])
    w_out0, w_out1 = (a.reshape(OUT_IN, D_MODEL) for a in ride_out.results[:2])
    conv_w = ride_out.results[2].transpose(1, 0, 2).reshape(CONV_WIDTH, 3 * D_MODEL)
    ps0 = matmul(h0, dn_ab, name="inproj_small_0")
    gates = rows_call(dn_gates_fn, [ps0], [alog, dtb], [(LANES, F32)], [], tm=512, name="dn_gates_fwd")[0]
    q0, k0, v0 = dn_prep_fwd(pm0, conv_w, name="dn_prep_fwd", riders=[ride_kv])
    w_kv = ride_kv.results[0].reshape(D_MODEL, D_MODEL)
    mk, mv = memkv_fwd(mem, mnw, w_kv, mknw, name="memkv_fwd")
    u0, w0, qk0 = delta_intra_fwd(q0, k0, v0, gates, name="delta_intra_fwd", riders=[ride_mlp1_0])
    o0, s_start = delta_seq_fwd(q0, k0, gates, u0, w0, qk0, name="delta_seq_fwd", riders=[ride_mlp2_0])
    cat0 = rows_call(dn_out_fn, [o0, (pm0, D_MODEL, 3), (pm0, MEM_WIDTH, 8)], [onw, mqw0, mk, mv],
                     [(D_MODEL + MEM_WIDTH, BF16)], [], tm=256, name="dn_out_fwd")[0]
    (w1_0,), (w2_0,) = ride_mlp1_0.results, ride_mlp2_0.results
    x1 = matmul(cat0, w_out0, post=_add, post_ins=[x0], name="wout_fwd_0")
    x2, mlp_res0 = _mlp_fwd(x1, n2w0, w1_0, w2_0, 0, riders=[[ride_fox_a], [ride_fox_b]])
    fox_main, fox_f = in_proj_weights(
        jnp.concatenate([ride_fox_a.results[0], ride_fox_b.results[0]], axis=1), FOX_IN, N_HEADS)

    n1w1, n2w1 = _row(w["norm1_w"][1]), _row(w["norm2_w"][1])
    fbias = _row(w["fox_f_bias"][0], LANES)
    qnw, knw, mqw1 = _row(w["fox_q_norm_w"][0]), _row(w["fox_k_norm_w"][0]), _row(w["memq_norm_w"][1])
    h1 = _norm_fwd(x2, n1w1, "norm1_fwd_1")
    pm1 = matmul(h1, fox_main, name="inproj_main_1")
    ps1 = matmul(h1, fox_f, name="inproj_small_1")
    fq = rows_call(fox_fcum_fn, [ps1], [fbias], [(LANES, F32)], [], tm=t, name="fox_fcum_fwd")[0]
    fk = fq[:, :N_HEADS].T[:, None, :]
    q1, k1 = rows_call(fox_qk_fn, [(pm1, D_MODEL, 0), (pm1, D_MODEL, 1)], [qnw, knw], [(D_MODEL, F32)] * 2, [], tm=256,
                       name="fox_qk_fwd")
    o1 = fox_attn_fwd(q1, k1, pm1, fq, fk, name="fox_attn_fwd", riders=[ride_mlp_1])
    cat1 = rows_call(fox_out_fn, [o1, (pm1, D_MODEL, 3), (pm1, MEM_WIDTH, 8)], [mqw1, mk, mv],
                     [(D_MODEL + MEM_WIDTH, BF16)], [], tm=256, name="fox_out_fwd")[0]
    w1_1, w2_1 = ride_mlp_1.results
    x3 = matmul(cat1, w_out1, post=_add, post_ins=[x2], name="wout_fwd_1")
    y, mlp_res1 = _mlp_fwd(x3, n2w1, w1_1, w2_1, 1)

    def loss_fn(y, tgt):
        e = y - tgt
        return e * (1.0 / D_MODEL), jnp.sum(jnp.sum(e * e, axis=1, keepdims=True), axis=0, keepdims=True)
    dy, sq = rows_call(loss_fn, [y, target], [], [(D_MODEL, F32)], [(1, 1)], tm=512, name="loss")
    loss = sq[0, 0] * (0.5 / D_MODEL)

    dx3, dw1_1, dw2_1, dn2w1, sib1, sib2 = _mlp_bwd(dy, mlp_res1, n2w1, w1_1, w2_1, 1)
    dcat1 = matmul(dx3, w_out1, tb=True, name="wout_dx_1")
    dwo_1 = matmul(cat1, dx3, ta=True, out_dtype=BF16, name="wout_dw_1").reshape(N_DEV, OUT_IN // N_DEV, D_MODEL)
    sibo = sibling_rider([dwo_1])
    do1, dgate1, dqm1, dmqw1, dmk1, dmv1 = rows_call(
        functools.partial(vjp_rows(fox_out_fn, 3, (True, True, True)), n_row=3, n_ct=1),
        [o1, (pm1, D_MODEL, 3), (pm1, MEM_WIDTH, 8), dcat1], [mqw1, mk, mv],
        [(D_MODEL, F32), (D_MODEL, BF16), (MEM_WIDTH, BF16)], [(1, HEAD_DIM), (n_mem, MEM_WIDTH), (n_mem, MEM_WIDTH)],
        tm=256, name="fox_out_bwd", riders=[sibo])
    ride_l1 = chips_rider(chip_sums(["w_mlp2_1", "w_mlp1_1", "w_out_1"], [dw2_1, dw1_1, dwo_1],
                                    sib2.results + sib1.results + sibo.results))
    dq1, dk1, dv1, dfq, dfk = fox_attn_bwd(q1, k1, pm1, fq, fk, do1, name="fox_attn_bwd", riders=[ride_l1])
    dqraw1, dkraw1, dqnw, dknw = rows_call(
        functools.partial(vjp_rows(fox_qk_fn, 2, (True, True)), n_row=2, n_ct=2),
        [(pm1, D_MODEL, 0), (pm1, D_MODEL, 1), dq1, dk1], [qnw, knw],
        [(D_MODEL, BF16)] * 2, [(1, HEAD_DIM)] * 2, tm=256, name="fox_qk_bwd")
    dfcum = dfq + jnp.pad(dfk[:, 0, :].T, ((0, 0), (0, LANES - N_HEADS)))
    dps1, dfbias = rows_call(
        functools.partial(vjp_rows(fox_fcum_fn, 1, (True,)), n_row=1, n_ct=1),
        [ps1, dfcum], [fbias], [(LANES, F32)], [(1, LANES)], tm=t, name="fox_fcum_bwd")
    dpm1 = jnp.concatenate([dqraw1, dkraw1, dv1, dgate1, dqm1], axis=1)
    dh1, dwmain1, dwsmall1 = _in_proj_bwd(h1, dpm1, dps1, fox_main, fox_f, "1")
    g_fox = in_proj_pieces(dwmain1, dwsmall1, N_HEADS, FOX_IN)
    sibf = sibling_rider([g_fox])
    dx2, dn1w1 = _norm_bwd(x2, n1w1, dh1, dx3, "norm1_bwd_1", riders=[sibf])
    ride_fox_g = chips_rider(chip_sums(["fox_w_in"], [g_fox], sibf.results))

    dx1, dw1_0, dw2_0, dn2w0, sib1, sib2 = _mlp_bwd(dx2, mlp_res0, n2w0, w1_0, w2_0, 0)
    dcat0 = matmul(dx1, w_out0, tb=True, name="wout_dx_0")
    dwo_0 = matmul(cat0, dx1, ta=True, out_dtype=BF16, name="wout_dw_0").reshape(N_DEV, OUT_IN // N_DEV, D_MODEL)
    sibo = sibling_rider([dwo_0])
    do0, dz0, dqm0, donw, dmqw0, dmk0, dmv0 = rows_call(
        functools.partial(vjp_rows(dn_out_fn, 3, (True, True, True, True)), n_row=3, n_ct=1),
        [o0, (pm0, D_MODEL, 3), (pm0, MEM_WIDTH, 8), dcat0], [onw, mqw0, mk, mv],
        [((N_HEADS, HEAD_DIM), F32), (D_MODEL, BF16), (MEM_WIDTH, BF16)],
        [(1, HEAD_DIM), (1, HEAD_DIM), (n_mem, MEM_WIDTH), (n_mem, MEM_WIDTH)], tm=256, name="dn_out_bwd", riders=[sibo])
    h_l0 = chip_sums(["w_mlp2_0", "w_mlp1_0", "w_out_0"], [dw2_0, dw1_0, dwo_0], sib2.results + sib1.results + sibo.results)
    ride_l0_mlp, ride_l0_out = chips_rider(h_l0[:2]), chips_rider(h_l0[2:])
    dmnw, dwkv, dmknw = memkv_bwd(mem, mnw, w_kv, mknw, dmk0 + dmk1, dmv0 + dmv1, name="memkv_bwd")
    g_kv = dwkv.reshape(N_DEV, D_MODEL // N_DEV, D_MODEL)
    sibk = sibling_rider([g_kv])
    dq_s, dk_s, dg_s, du0, dw0, dqk0 = delta_seq_bwd(q0, k0, gates, u0, w0, qk0, s_start, do0, name="delta_seq_bwd",
                                                     riders=[ride_fox_g, sibk])
    ride_kv_g = chips_rider(chip_sums(["w_mem_kv"], [g_kv], sibk.results))
    dq0, dk0, dv0, dgates = delta_intra_bwd(q0, k0, v0, gates, du0, dw0, dqk0, dq_s, dk_s, dg_s,
                                            name="delta_intra_bwd", riders=[ride_l0_mlp, ride_kv_g])
    dxq, dxk, dxv, dcq, dck, dcv = dn_prep_bwd(pm0, conv_w, dq0, dk0, dv0, name="dn_prep_bwd", riders=[ride_l0_out])
    dconv = jnp.concatenate([dcq, dck, dcv], axis=1)
    dps0, dalog, ddtb = rows_call(
        functools.partial(vjp_rows(dn_gates_fn, 1, (True, True)), n_row=1, n_ct=1),
        [ps0, dgates], [alog, dtb], [(LANES, F32)], [(1, LANES)] * 2, tm=512, name="dn_gates_bwd")
    dpm0 = jnp.concatenate([dxq, dxk, dxv, dz0, dqm0], axis=1)
    dh0, dwmain0, dwsmall0 = _in_proj_bwd(h0, dpm0, dps0, dn_main, dn_ab, "0")
    grad_x, dn1w0 = _norm_bwd(x0, n1w0, dh0, dx1, "norm1_bwd_0")
    g_dn = in_proj_pieces(dwmain0, dwsmall0, 2 * N_HEADS, DN_IN)
    g_conv = dconv.reshape(CONV_WIDTH, N_DEV, -1).transpose(1, 0, 2).astype(BF16)

    g["mem_norm_w"] = dmnw[0]
    g["mem_k_norm_w"] = dmknw[0]
    g["norm1_w"] = jnp.concatenate([dn1w0, dn1w1], axis=0)
    g["dn_a_log"] = dalog[:, :N_HEADS]
    g["dn_dt_bias"] = ddtb[:, :N_HEADS]
    g["dn_o_norm_w"] = donw
    g["fox_f_bias"] = dfbias[:, :N_HEADS]
    g["fox_q_norm_w"] = dqnw
    g["fox_k_norm_w"] = dknw
    g["memq_norm_w"] = jnp.concatenate([dmqw0, dmqw1], axis=0)
    g["norm2_w"] = jnp.concatenate([dn2w0, dn2w1], axis=0)

    sibd = sibling_rider([g_dn, g_conv])
    run_riders([sibd], name="grads_to_sibling_last")
    ride_last = chips_rider(chip_sums(["dn_w_in", "dn_conv_w"], [g_dn, g_conv], sibd.results))
    ride_small = gather_rider([pack_small(g, last=loss)])
    run_riders([ride_last, ride_small], name="grads_to_chips_last")

    def layers(l0, l1):
        return jnp.stack([l0, l1], axis=1).reshape(4, -1, l0.shape[-1])

    parts = {
        "w_mlp1": layers(ride_l0_mlp.results[1], ride_l1.results[1]),
        "w_mlp2": layers(ride_l0_mlp.results[0], ride_l1.results[0]),
        "w_out": layers(ride_l0_out.results[0], ride_l1.results[2]),
        "fox_w_in": ride_fox_g.results[0], "w_mem_kv": ride_kv_g.results[0],
        "dn_w_in": ride_last.results[0], "dn_conv_w": ride_last.results[1],
    }
    out = {n: adamw(parts[n], w[n], m[n], v[n], name=f"adamw_{n}") for n, _, _ in BIG}
    small = adamw(ride_small.results[0], pack_small(w), pack_small(m), pack_small(v), name="adamw_small")
    loss = small[0][-1, -1]
    return loss, grad_x, out, small


WEIGHTS = ["mem_norm_w", "w_mem_kv", "mem_k_norm_w", "norm1_w", "dn_w_in", "dn_conv_w", "dn_a_log", "dn_dt_bias",
           "dn_o_norm_w", "fox_w_in", "fox_f_bias", "fox_q_norm_w", "fox_k_norm_w", "memq_norm_w", "w_out", "norm2_w",
           "w_mlp1", "w_mlp2"]
DN_IN = 4 * D_MODEL + 2 * N_HEADS + MEM_WIDTH
FOX_IN = 4 * D_MODEL + N_HEADS + MEM_WIDTH
GATE_END = 4 * D_MODEL
OUT_IN = D_MODEL + MEM_WIDTH
BIG = [("w_mem_kv", D_MODEL // N_DEV, D_MODEL), ("dn_w_in", D_MODEL, DN_IN // N_DEV), ("fox_w_in", D_MODEL, FOX_IN // N_DEV),
       ("dn_conv_w", CONV_WIDTH, 3 * D_MODEL // N_DEV), ("w_out", 2 * OUT_IN // N_DEV, D_MODEL),
       ("w_mlp1", 2 * D_MODEL, FF_PIECE), ("w_mlp2", 2 * FF_PIECE, D_MODEL)]
SMALL_TILE = 8 * LANES
SMALL = [(name, shape, -(-math.prod(shape) // SMALL_TILE) * SMALL_TILE) for name, shape in [
    ("mem_norm_w", (D_MODEL,)), ("mem_k_norm_w", (HEAD_DIM,)), ("norm1_w", (2, D_MODEL)), ("dn_a_log", (1, N_HEADS)),
    ("dn_dt_bias", (1, N_HEADS)), ("dn_o_norm_w", (1, HEAD_DIM)), ("fox_f_bias", (1, N_HEADS)),
    ("fox_q_norm_w", (1, HEAD_DIM)), ("fox_k_norm_w", (1, HEAD_DIM)), ("memq_norm_w", (2, HEAD_DIM)), ("norm2_w", (2, D_MODEL))]]
SMALL_ROWS = sum(ln for _, _, ln in SMALL) // LANES + 8


def pack_small(p, last=None):
    def rows(a, ln):
        a = a.reshape(-1)
        return (a if a.shape[0] == ln else jnp.pad(a, (0, ln - a.shape[0]))).reshape(-1, LANES)

    used = sum(ln for _, _, ln in SMALL) // LANES
    tail = jnp.zeros(((SMALL_ROWS - used) * LANES,), F32)
    if last is not None:
        tail = jnp.concatenate([tail[:-1], last.reshape(1)])
    return jnp.concatenate([rows(p[n], ln) for n, _, ln in SMALL] + [tail.reshape(-1, LANES)], axis=0)


def unpack_small(pk):
    row, out = 0, {}
    for n, sh, ln in SMALL:
        out[n] = pk[row:row + ln // LANES].reshape(-1)[:math.prod(sh)].reshape(sh)
        row += ln // LANES
    return out


def in_proj_weights(gathered, width, n_small):
    full = gathered.transpose(1, 0, 2).reshape(D_MODEL, width)
    main = jnp.concatenate([full[:, :GATE_END], full[:, GATE_END + n_small:]], axis=1)
    return main, jnp.pad(full[:, GATE_END:GATE_END + n_small], ((0, 0), (0, LANES - n_small)))


def in_proj_pieces(d_main, d_small, n_small, width):
    full = jnp.concatenate([d_main[:, :GATE_END], d_small[:, :n_small], d_main[:, GATE_END:]], axis=1)
    return full.reshape(D_MODEL, N_DEV, width // N_DEV).transpose(1, 0, 2)


def adamw(parts, w, m, v, *, name):
    n, _, cols = parts.shape
    layers = w.shape[0] if w.ndim == 3 else 1
    rows = w.shape[-2]
    tile = _pick(rows, (512, 256, 128))
    steps = rows // tile

    def body(p_ref, w_ref, m_ref, v_ref, g_ref, d_ref, mo_ref, vo_ref):
        g = p_ref[0].astype(F32)
        for i in range(1, n):
            g = g + p_ref[i].astype(F32)
        m_new = ADAM_B1 * m_ref[...] + (1.0 - ADAM_B1) * g
        v_new = ADAM_B2 * v_ref[...] + (1.0 - ADAM_B2) * jnp.square(g)
        m_hat = m_new / (1.0 - ADAM_B1 ** ADAM_STEP)
        v_hat = v_new / (1.0 - ADAM_B2 ** ADAM_STEP)
        g_ref[...] = g
        d_ref[...] = -ADAM_LR * (m_hat / (jnp.sqrt(v_hat) + ADAM_EPS) + ADAM_WD * w_ref[...])
        mo_ref[...] = m_new
        vo_ref[...] = v_new

    if w.ndim == 3:
        spec = pl.BlockSpec((None, tile, cols), lambda l, i: (l, i, 0))
    else:
        spec = pl.BlockSpec((tile, cols), lambda l, i: (i, 0))
    return pl.pallas_call(
        body, name=name, grid=(layers, steps),
        in_specs=[pl.BlockSpec((n, tile, cols), lambda l, i: (0, l * steps + i, 0)), spec, spec, spec], out_specs=[spec] * 4,
        out_shape=[jax.ShapeDtypeStruct(w.shape, F32)] * 4, compiler_params=_params(("parallel", "parallel")),
    )(parts, w, m, v)


def kernel(x, mem, mem_norm_w, w_mem_kv, mem_k_norm_w, norm1_w, dn_w_in, dn_conv_w, dn_a_log, dn_dt_bias, dn_o_norm_w, fox_w_in, fox_f_bias, fox_q_norm_w, fox_k_norm_w, memq_norm_w, w_out, norm2_w, w_mlp1, w_mlp2, loss_target, m_mem_norm_w, m_w_mem_kv, m_mem_k_norm_w, m_norm1_w, m_dn_w_in, m_dn_conv_w, m_dn_a_log, m_dn_dt_bias, m_dn_o_norm_w, m_fox_w_in, m_fox_f_bias, m_fox_q_norm_w, m_fox_k_norm_w, m_memq_norm_w, m_w_out, m_norm2_w, m_w_mlp1, m_w_mlp2, v_mem_norm_w, v_w_mem_kv, v_mem_k_norm_w, v_norm1_w, v_dn_w_in, v_dn_conv_w, v_dn_a_log, v_dn_dt_bias, v_dn_o_norm_w, v_fox_w_in, v_fox_f_bias, v_fox_q_norm_w, v_fox_k_norm_w, v_memq_norm_w, v_w_out, v_norm2_w, v_w_mlp1, v_w_mlp2):
    p = dict(mem_norm_w=mem_norm_w, w_mem_kv=w_mem_kv, mem_k_norm_w=mem_k_norm_w, norm1_w=norm1_w, dn_w_in=dn_w_in,
             dn_conv_w=dn_conv_w, dn_a_log=dn_a_log, dn_dt_bias=dn_dt_bias, dn_o_norm_w=dn_o_norm_w, fox_w_in=fox_w_in,
             fox_f_bias=fox_f_bias, fox_q_norm_w=fox_q_norm_w, fox_k_norm_w=fox_k_norm_w, memq_norm_w=memq_norm_w,
             w_out=w_out, norm2_w=norm2_w, w_mlp1=w_mlp1, w_mlp2=w_mlp2)
    pm = dict(mem_norm_w=m_mem_norm_w, w_mem_kv=m_w_mem_kv, mem_k_norm_w=m_mem_k_norm_w, norm1_w=m_norm1_w,
              dn_w_in=m_dn_w_in, dn_conv_w=m_dn_conv_w, dn_a_log=m_dn_a_log, dn_dt_bias=m_dn_dt_bias,
              dn_o_norm_w=m_dn_o_norm_w, fox_w_in=m_fox_w_in, fox_f_bias=m_fox_f_bias, fox_q_norm_w=m_fox_q_norm_w,
              fox_k_norm_w=m_fox_k_norm_w, memq_norm_w=m_memq_norm_w, w_out=m_w_out, norm2_w=m_norm2_w, w_mlp1=m_w_mlp1,
              w_mlp2=m_w_mlp2)
    pv = dict(mem_norm_w=v_mem_norm_w, w_mem_kv=v_w_mem_kv, mem_k_norm_w=v_mem_k_norm_w, norm1_w=v_norm1_w,
              dn_w_in=v_dn_w_in, dn_conv_w=v_dn_conv_w, dn_a_log=v_dn_a_log, dn_dt_bias=v_dn_dt_bias,
              dn_o_norm_w=v_dn_o_norm_w, fox_w_in=v_fox_w_in, fox_f_bias=v_fox_f_bias, fox_q_norm_w=v_fox_q_norm_w,
              fox_k_norm_w=v_fox_k_norm_w, memq_norm_w=v_memq_norm_w, w_out=v_w_out, norm2_w=v_norm2_w, w_mlp1=v_w_mlp1,
              w_mlp2=v_w_mlp2)

    loss, grad_x, results, small = local_step(x[0], mem[0], loss_target[0], p, pm, pv)
    small = [unpack_small(o) for o in small]
    groups = [{**small[i], **{n: r[i] for n, r in results.items()}} for i in range(4)]
    return (loss, grad_x[None], *[grp[n] for grp in groups for n in WEIGHTS])
```

```python
import functools
import math

import jax
import jax.numpy as jnp
from jax import lax
from jax.experimental import pallas as pl
from jax.experimental.pallas import tpu as pltpu

F32 = jnp.float32
BF16 = jnp.bfloat16
HIGHEST = lax.Precision.HIGHEST

D_MODEL = 1024
HEAD_DIM = 128
N_HEADS = 8
MEM_HEADS = 4
MEM_WIDTH = MEM_HEADS * HEAD_DIM
D_FF = 4 * D_MODEL
CONV_WIDTH = 4
CHUNK = 64
Q_BLOCK = 128
EPS = 1e-6
SCALE = HEAD_DIM ** -0.5
MAIN_WIDTH = 4 * D_MODEL + MEM_WIDTH
LANES = 128
N_DEV = 8

ADAM_LR = 0.001
ADAM_B1 = 0.9
ADAM_B2 = 0.999
ADAM_EPS = 1e-08
ADAM_WD = 0.01
ADAM_STEP = 10

VMEM_LIMIT = 56 * 2 ** 20
MESH = pl.DeviceIdType.MESH


def _bdot(a, b, dims):
    return lax.dot_general(a.astype(BF16), b.astype(BF16), (dims, ((), ())), preferred_element_type=F32)


@jax.custom_vjp
def mm(a, b):
    return _bdot(a, b, ((1,), (0,)))


@jax.custom_vjp
def mm_nt(a, b):
    return _bdot(a, b, ((1,), (1,)))


@jax.custom_vjp
def mm_tn(a, b):
    return _bdot(a, b, ((0,), (0,)))


mm.defvjp(lambda a, b: (mm(a, b), (a, b)), lambda r, g: (mm_nt(g, r[1]), mm_tn(r[0], g)))
mm_nt.defvjp(lambda a, b: (mm_nt(a, b), (a, b)), lambda r, g: (mm(g, r[1]), mm_tn(g, r[0])))
mm_tn.defvjp(lambda a, b: (mm_tn(a, b), (a, b)), lambda r, g: (mm_nt(r[1], g), mm(r[0], g)))


def hdot(a, b):
    return jnp.dot(a, b, precision=HIGHEST, preferred_element_type=F32)


def rms(x, w):
    return x * lax.rsqrt(jnp.mean(x * x, axis=-1, keepdims=True) + EPS) * w


def l2n(x):
    return x * lax.rsqrt(jnp.sum(x * x, axis=-1, keepdims=True) + EPS)


def _iota2(n, m):
    return lax.broadcasted_iota(jnp.int32, (n, m), 0), lax.broadcasted_iota(jnp.int32, (n, m), 1)


def _lower_ones(n):
    r, c = _iota2(n, n)
    return jnp.where(r >= c, 1.0, 0.0).astype(F32)


def _last_row(x):
    r = lax.broadcasted_iota(jnp.int32, x.shape, 0)
    return jnp.sum(jnp.where(r == x.shape[0] - 1, x, 0.0), axis=0, keepdims=True)


def _softmax_rows(z):
    m = lax.stop_gradient(jnp.max(z, axis=-1, keepdims=True))
    e = jnp.exp(z - m)
    return e * (1.0 / jnp.sum(e, axis=-1, keepdims=True))


_BNN = (((2,), (1,)), ((0,), (0,)))
_BNT = (((2,), (2,)), ((0,), (0,)))
_BTN = (((1,), (1,)), ((0,), (0,)))


def _bbdot(a, b, dims):
    return lax.dot_general(a.astype(BF16), b.astype(BF16), dims, preferred_element_type=F32)


@jax.custom_vjp
def bmm(a, b):
    return _bbdot(a, b, _BNN)


@jax.custom_vjp
def bmm_nt(a, b):
    return _bbdot(a, b, _BNT)


@jax.custom_vjp
def bmm_tn(a, b):
    return _bbdot(a, b, _BTN)


@jax.custom_vjp
def bmm_high(a, b):
    return lax.dot_general(a, b, _BNN, precision=lax.Precision.HIGH, preferred_element_type=F32)


bmm.defvjp(lambda a, b: (bmm(a, b), (a, b)), lambda r, g: (bmm_nt(g, r[1]), bmm_tn(r[0], g)))
bmm_nt.defvjp(lambda a, b: (bmm_nt(a, b), (a, b)), lambda r, g: (bmm(g, r[1]), bmm_tn(g, r[0])))
bmm_tn.defvjp(lambda a, b: (bmm_tn(a, b), (a, b)), lambda r, g: (bmm_nt(r[1], g), bmm(r[0], g)))
bmm_high.defvjp(lambda a, b: (bmm_high(a, b), (a, b)), lambda r, g: (bmm_nt(g, r[1]), bmm_tn(r[0], g)))

NEUMANN_HIGH_LEVELS = 2


def inv_unit_lower(a):
    n = a.shape[-1]
    r, c = _iota2(n, n)
    p = jnp.where(r == c, 1.0, 0.0).astype(F32) - a
    ak = a
    for level in range(int(math.log2(n)) - 1):
        dot = bmm_high if level < NEUMANN_HIGH_LEVELS else bmm
        ak = dot(ak, ak)
        p = p + dot(p, ak)
    return p


def delta_intra(q, k, v, gc, beta):
    b, c, _ = q.shape
    r, cc = _iota2(c, c)
    causal = r >= cc
    strict = r > cc
    gi = jnp.broadcast_to(gc, (b, c, c))
    gj = jnp.swapaxes(gi, 1, 2)
    decay = jnp.where(causal, jnp.exp(jnp.where(causal, gi - gj, 0.0)), 0.0)
    kb = k * beta
    a = jnp.where(strict, bmm_nt(kb, k) * decay, 0.0)
    t = inv_unit_lower(a)
    u = bmm(t, v * beta)
    w = bmm(t, kb * jnp.exp(gc))
    qk = jnp.where(causal, bmm_nt(q, k) * decay, 0.0)
    return u, w, qk


def delta_step(s, q, k, gc, u, w, qk):
    v_new = u - bmm(w, s)
    out = bmm(q * jnp.exp(gc), s) + bmm(qk, v_new)
    r = lax.broadcasted_iota(jnp.int32, gc.shape, 1)
    g_last = jnp.sum(jnp.where(r == gc.shape[1] - 1, gc, 0.0), axis=1, keepdims=True)
    k_dec = k * jnp.exp(g_last - gc)
    s_new = s * jnp.exp(g_last) + bmm_tn(k_dec, v_new)
    return out, s_new


def fox_probs(q, k, fq, fk, qpos0):
    s = lax.dot_general(q, k, (((1,), (1,)), ((), ())), preferred_element_type=F32)
    r, c = _iota2(s.shape[0], s.shape[1])
    return _softmax_rows(jnp.where(c <= (r + qpos0), s + (fq - fk), -jnp.inf))


def mem_head(qm, wq, mk, mv):
    p = _softmax_rows(mm_nt(rms(qm, wq) * SCALE, mk))
    return mm(p, mv)


def _heads(x, n):
    return [x[:, h * HEAD_DIM:(h + 1) * HEAD_DIM] for h in range(n)]


def memkv_fn(mem, mnw, wkv, mknw):
    kv = mm(rms(mem, mnw), wkv)
    mk = jnp.concatenate([rms(kh, mknw) for kh in _heads(kv[:, :MEM_WIDTH], MEM_HEADS)], axis=1)
    return mk, kv[:, MEM_WIDTH:]


def dn_gates_fn(ab, alog, dtb):
    g = -jnp.exp(alog) * jax.nn.softplus(ab + dtb)
    low = _lower_ones(CHUNK)
    gc = jnp.concatenate([hdot(low, g[i * CHUNK:(i + 1) * CHUNK]) for i in range(ab.shape[0] // CHUNK)], axis=0)
    lane = lax.broadcasted_iota(jnp.int32, ab.shape, 1)
    return jnp.where(lane < N_HEADS, gc, jax.nn.sigmoid(ab))


def fox_fcum_fn(fp, fbias):
    lf = jax.nn.log_sigmoid(fp + fbias)
    low = _lower_ones(LANES)
    carry = jnp.zeros((1, fp.shape[1]), F32)
    outs = []
    for i in range(fp.shape[0] // LANES):
        cs = hdot(low, lf[i * LANES:(i + 1) * LANES]) + carry
        carry = _last_row(cs)
        outs.append(cs)
    return jnp.concatenate(outs, axis=0)


def fox_qk_fn(qraw, kraw, qnw, knw):
    q = jnp.concatenate([rms(x, qnw) * SCALE for x in _heads(qraw, N_HEADS)], axis=1)
    k = jnp.concatenate([rms(x, knw) for x in _heads(kraw, N_HEADS)], axis=1)
    return q, k


def _mem_out(qm, mqw, mk, mv):
    return [mem_head(a, mqw, b, c) for a, b, c in zip(_heads(qm, MEM_HEADS), _heads(mk, MEM_HEADS), _heads(mv, MEM_HEADS))]


def dn_out_fn(o, z, qm, onw, mqw, mk, mv):
    mix = [rms(a, onw) * jax.nn.silu(b) for a, b in zip(o, _heads(z, N_HEADS))]
    return jnp.concatenate(mix + _mem_out(qm, mqw, mk, mv), axis=1)


def fox_out_fn(o, gate, qm, mqw, mk, mv):
    return jnp.concatenate([o * jax.nn.sigmoid(gate)] + _mem_out(qm, mqw, mk, mv), axis=1)


_HBM = pl.BlockSpec(memory_space=pltpu.HBM)


def _place():
    return lax.axis_index("x"), lax.axis_index("y"), lax.axis_index("c")


class Rider:
    def __init__(self, ins, out_shape, scratch, start, finish):
        self.ins, self.out_shape, self.scratch, self.start, self.finish = list(ins), list(out_shape), list(scratch), start, finish
        self.results = None


def gather_rider(xs):
    n = len(xs)

    def plan(x_refs, out_refs, sems):
        send_sems, recv_sems, local_sems = sems
        x, y, c = _place()
        me, sibling = (x, y, c), (x, y, 1 - c)
        chips = [(1 - x, y), (x, 1 - y), (1 - x, 1 - y)]

        def copy(a, k, block, to, src=None):
            px, py, pc = block
            dst = out_refs[a].at[4 * px + 2 * py + pc]
            return pltpu.make_async_remote_copy(
                src_ref=dst if src is None else src, dst_ref=dst,
                send_sem=send_sems.at[a, k], recv_sem=recv_sems.at[a, k], device_id=to, device_id_type=MESH)

        mine = [pltpu.make_async_copy(x_refs[a], out_refs[a].at[4 * x + 2 * y + c], local_sems.at[a]) for a in range(n)]
        first = [copy(a, 0, me, sibling, src=x_refs[a]) for a in range(n)]
        first += [copy(a, 1 + j, me, (*chip, c), src=x_refs[a]) for j, chip in enumerate(chips) for a in range(n)]
        return copy, me, sibling, chips, mine, first

    def start(x_refs, out_refs, sems):
        _, _, _, _, mine, first = plan(x_refs, out_refs, sems)
        for cp in mine + first:
            cp.start()

    def finish(x_refs, out_refs, sems):
        copy, me, sibling, chips, mine, first = plan(x_refs, out_refs, sems)
        _, _, c = me
        passed = []
        for j, chip in enumerate(chips):
            for a in range(n):
                copy(a, 1 + j, (*chip, c), me).wait_recv()
                passed.append(copy(a, 4 + j, (*chip, c), sibling))
                passed[-1].start()
        for a in range(n):
            copy(a, 0, sibling, me).wait_recv()
        for j, chip in enumerate(chips):
            for a in range(n):
                copy(a, 4 + j, (*chip, 1 - c), me).wait_recv()
        for cp in first + passed:
            cp.wait_send()
        for cp in mine:
            cp.wait()

    return Rider(xs, [jax.ShapeDtypeStruct((N_DEV,) + a.shape, a.dtype) for a in xs],
                 [pltpu.SemaphoreType.DMA((n, 7)), pltpu.SemaphoreType.DMA((n, 7)), pltpu.SemaphoreType.DMA((n,))], start, finish)


def sibling_rider(gs):
    n = len(gs)

    def plan(g_refs, out_refs, sems):
        send_sems, recv_sems = sems
        x, y, c = _place()
        return [pltpu.make_async_remote_copy(
            src_ref=g_refs[a].at[2 * k + 1 - c], dst_ref=out_refs[a].at[k], send_sem=send_sems.at[a, k],
            recv_sem=recv_sems.at[a, k], device_id=(x, y, 1 - c), device_id_type=MESH) for a in range(n) for k in range(4)]

    def start(g_refs, out_refs, sems):
        for cp in plan(g_refs, out_refs, sems):
            cp.start()

    def finish(g_refs, out_refs, sems):
        copies = plan(g_refs, out_refs, sems)
        for cp in copies:
            cp.wait_recv()
        for cp in copies:
            cp.wait_send()

    return Rider(gs, [jax.ShapeDtypeStruct((4,) + g.shape[1:], g.dtype) for g in gs],
                 [pltpu.SemaphoreType.DMA((n, 4)), pltpu.SemaphoreType.DMA((n, 4))], start, finish)


def chips_rider(hs):
    n = len(hs)

    def plan(h_refs, out_refs, sems):
        send_sems, recv_sems, local_sems = sems
        x, y, c = _place()
        mine = 2 * x + y
        chips = [(1 - x, y), (x, 1 - y), (1 - x, 1 - y)]
        keep = [pltpu.make_async_copy(h_refs[a].at[mine], out_refs[a].at[mine], local_sems.at[a]) for a in range(n)]
        sends = [pltpu.make_async_remote_copy(
            src_ref=h_refs[a].at[2 * qx + qy], dst_ref=out_refs[a].at[mine], send_sem=send_sems.at[a, j],
            recv_sem=recv_sems.at[a, j], device_id=(qx, qy, c), device_id_type=MESH)
            for j, (qx, qy) in enumerate(chips) for a in range(n)]
        recvs = [pltpu.make_async_remote_copy(
            src_ref=h_refs[a].at[mine], dst_ref=out_refs[a].at[2 * qx + qy], send_sem=send_sems.at[a, j],
            recv_sem=recv_sems.at[a, j], device_id=(qx, qy, c), device_id_type=MESH)
            for j, (qx, qy) in enumerate(chips) for a in range(n)]
        return keep, sends, recvs

    def start(h_refs, out_refs, sems):
        keep, sends, _ = plan(h_refs, out_refs, sems)
        for cp in keep + sends:
            cp.start()

    def finish(h_refs, out_refs, sems):
        keep, sends, recvs = plan(h_refs, out_refs, sems)
        for cp in recvs:
            cp.wait_recv()
        for cp in sends:
            cp.wait_send()
        for cp in keep:
            cp.wait()

    return Rider(hs, [jax.ShapeDtypeStruct(h.shape, h.dtype) for h in hs],
                 [pltpu.SemaphoreType.DMA((n, 3)), pltpu.SemaphoreType.DMA((n, 3)), pltpu.SemaphoreType.DMA((n,))], start, finish)


def hosted_call(riders, body, *, out_shape, in_specs, out_specs, grid=(), scratch_shapes=(), **kw):
    riders = tuple(riders or ())
    if not riders:
        return pl.pallas_call(body, out_shape=out_shape, in_specs=in_specs, out_specs=out_specs, grid=grid,
                              scratch_shapes=scratch_shapes, **kw)
    single = not isinstance(out_shape, (list, tuple))
    k_out_shape = [out_shape] if single else list(out_shape)
    k_out_specs = [out_specs] if single else list(out_specs)
    n_in, n_out, n_scr = len(in_specs), len(k_out_shape), len(scratch_shapes)
    r_ins = [a for r in riders for a in r.ins]
    r_outs = [s for r in riders for s in r.out_shape]
    r_scr = [s for r in riders for s in r.scratch]

    def full_body(*refs):
        ins = refs[:n_in + len(r_ins)]
        outs = refs[n_in + len(r_ins):n_in + len(r_ins) + n_out + len(r_outs)]
        scr = refs[n_in + len(r_ins) + n_out + len(r_outs):]
        ids = [pl.program_id(d) for d in range(len(grid))]
        first = functools.reduce(jnp.logical_and, [i == 0 for i in ids]) if ids else None
        last = functools.reduce(jnp.logical_and, [i == g - 1 for i, g in zip(ids, grid)]) if ids else None

        def each(method):
            i0, o0, s0 = n_in, n_out, n_scr
            for r in riders:
                getattr(r, method)(ins[i0:i0 + len(r.ins)], outs[o0:o0 + len(r.out_shape)], scr[s0:s0 + len(r.scratch)])
                i0, o0, s0 = i0 + len(r.ins), o0 + len(r.out_shape), s0 + len(r.scratch)

        if first is None:
            each("start")
        else:
            pl.when(first)(lambda: each("start"))
        body(*ins[:n_in], *outs[:n_out], *scr[:n_scr])
        if last is None:
            each("finish")
        else:
            pl.when(last)(lambda: each("finish"))

    call = pl.pallas_call(
        full_body, out_shape=k_out_shape + r_outs, in_specs=list(in_specs) + [_HBM] * len(r_ins),
        out_specs=k_out_specs + [_HBM] * len(r_outs), grid=grid, scratch_shapes=list(scratch_shapes) + r_scr, **kw)

    def run(*args):
        res = call(*args, *r_ins)
        o0 = n_out
        for r in riders:
            r.results = list(res[o0:o0 + len(r.out_shape)])
            o0 += len(r.out_shape)
        return res[0] if single else list(res[:n_out])

    return run


def run_riders(riders, *, name):
    hosted_call(riders, lambda: None, name=name, out_shape=[], in_specs=[], out_specs=[])()
    return [r.results for r in riders]


def _pick(n, cands):
    for c in cands:
        if n % c == 0:
            return c
    return n


def _params(sem):
    return pltpu.CompilerParams(dimension_semantics=sem, vmem_limit_bytes=VMEM_LIMIT)


MATMUL_VMEM_BUDGET = 40 * 2 ** 20


def _matmul_tiles(m, n, k, bytes_a, bytes_b, bytes_mn, fixed):
    fm, fn, fk = fixed if fixed is not None else (None, None, None)

    def options(given, size, cands):
        return [given] if given else ([c for c in cands if size % c == 0] or [size])

    best = None
    for tm in options(fm, m, (2048, 1024, 512, 256, 128)):
        for tn in options(fn, n, (512, 256, 128)):
            for tk in options(fk, k, (2048, 1536, 1024, 512, 256, 128)):
                if 2 * (tm * tk * bytes_a + tk * tn * bytes_b + tm * tn * bytes_mn) + tm * tn * 4 > MATMUL_VMEM_BUDGET:
                    continue
                key = ((m // tm) * (n // tn) * (k // tk), -tk)
                if best is None or key < best[0]:
                    best = (key, (tm, tn, tk))
    assert best is not None, (m, n, k, fixed)
    return best[1]


def matmul(a, b, *, name, ta=False, tb=False, post=None, post_ins=(), row_ins=(), acc=False, extra_out=None,
           out_dtype=F32, tiles=None, b_view=None, out_view=None, riders=()):
    (k, m) = a.shape if ta else a.shape[::-1]
    (kb, n) = b_view[:2] if b_view is not None else (b.shape[::-1] if tb else b.shape)
    assert k == kb, (a.shape, b.shape, ta, tb)
    bytes_mn = sum(p.dtype.itemsize for p in post_ins) + jnp.dtype(out_dtype).itemsize
    bytes_mn += jnp.dtype(extra_out[1]).itemsize if extra_out else 0
    tm, tn, tk = _matmul_tiles(m, n, k, a.dtype.itemsize, b.dtype.itemsize, bytes_mn, tiles)
    assert not acc or tn == n, (name, tn, n)
    nk = k // tk
    dims = ((0,) if ta else (1,), (1,) if tb else (0,))
    n_post, n_row = len(post_ins), len(row_ins)
    n_out = 1 + bool(extra_out) + bool(acc)

    def body(*refs):
        a_ref, b_ref = refs[:2]
        post_refs = refs[2:2 + n_post + n_row]
        o_refs, acc_ref = refs[-1 - n_out:-1], refs[-1]
        first_rows, kk = pl.program_id(0) == 0, pl.program_id(2)

        @pl.when(kk == 0)
        def _():
            acc_ref[...] = jnp.zeros_like(acc_ref)

        b_tile = b_ref[...]
        acc_ref[...] += _bdot(a_ref[...], b_tile.reshape(-1, b_tile.shape[-1]), dims)

        @pl.when(kk == nk - 1)
        def _():
            r = acc_ref[...]
            rows = [p[...] for p in post_refs[n_post:]]
            if post is not None:
                r = post(r, *[p[...] for p in post_refs[:n_post]], *rows)
            if acc:
                r, s = r
                sum_ref = o_refs[-1]

                @pl.when(first_rows)
                def _():
                    sum_ref[...] = s

                @pl.when(jnp.logical_not(first_rows))
                def _():
                    sum_ref[...] += s

            o_refs[0][...] = r.astype(out_dtype)
            if extra_out:
                o_refs[1][...] = extra_out[0](r, *rows).astype(extra_out[1])

    a_spec = pl.BlockSpec((tk, tm), lambda i, j, kk: (kk, i)) if ta else pl.BlockSpec((tm, tk), lambda i, j, kk: (i, kk))
    if b_view is not None:
        b_spec = b_view[2]
    else:
        b_spec = pl.BlockSpec((tn, tk), lambda i, j, kk: (j, kk)) if tb else pl.BlockSpec((tk, tn), lambda i, j, kk: (kk, j))
    mn_spec = pl.BlockSpec((tm, tn), lambda i, j, kk: (i, j))
    row_spec = pl.BlockSpec((1, tn), lambda i, j, kk: (0, j))
    o_shape, o_spec = ((m, n), mn_spec) if out_view is None else out_view
    out_shape = [jax.ShapeDtypeStruct(o_shape, out_dtype)]
    out_specs = [o_spec]
    if extra_out:
        out_shape.append(jax.ShapeDtypeStruct((m, n), extra_out[1]))
        out_specs.append(mn_spec)
    if acc:
        out_shape.append(jax.ShapeDtypeStruct((1, n), F32))
        out_specs.append(row_spec)
    res = hosted_call(
        riders, body, name=name, grid=(m // tm, n // tn, nk),
        in_specs=[a_spec, b_spec] + [mn_spec] * n_post + [row_spec] * n_row, out_specs=out_specs, out_shape=out_shape,
        scratch_shapes=[pltpu.VMEM((tm, tn), F32)],
        compiler_params=_params(("arbitrary" if acc else "parallel", "parallel", "arbitrary")),
    )(a, b, *post_ins, *row_ins)
    return res if n_out > 1 else res[0]


def rows_call(fn, row_ins, full_ins, row_outs, acc_outs, *, tm, name, riders=()):
    row_ins = [r if isinstance(r, tuple) else (r, r.shape[-1], 0) for r in row_ins]
    t = row_ins[0][0].shape[-2]
    tm = min(tm, t)
    n_in = len(row_ins) + len(full_ins)
    n_row = len(row_outs)

    def body(*refs):
        res = fn(*[[r[h] for h in range(r.shape[0])] if (i < len(row_ins) and len(r.shape) == 3) else r[...]
                   for i, r in enumerate(refs[:n_in])])
        res = res if isinstance(res, (tuple, list)) else (res,)
        outs = refs[n_in:]
        for ref, val in zip(outs[:n_row], res[:n_row]):
            if len(ref.shape) == 3:
                for h, vh in enumerate(val):
                    ref[h] = vh.astype(ref.dtype)
            else:
                ref[...] = val.astype(ref.dtype)
        first = pl.program_id(0) == 0
        for ref, val in zip(outs[n_row:], res[n_row:]):
            @pl.when(first)
            def _(ref=ref, val=val):
                ref[...] = val

            @pl.when(jnp.logical_not(first))
            def _(ref=ref, val=val):
                ref[...] += val

    def full_spec(shape):
        return pl.BlockSpec(shape, lambda i, nd=len(shape): (0,) * nd)

    def row_spec(lead, w, cb):
        if lead is None:
            return pl.BlockSpec((tm, w), lambda i: (i, cb))
        return pl.BlockSpec((lead, tm, w), lambda i: (0, i, cb))

    def lead_cols(c):
        return c if isinstance(c, tuple) else (None, c)

    in_specs = [row_spec(a.shape[0] if a.ndim == 3 else None, w, cb) for (a, w, cb) in row_ins]
    in_specs += [full_spec(f.shape) for f in full_ins]
    out_specs = [row_spec(*lead_cols(c), 0) for c, _ in row_outs] + [full_spec(s) for s in acc_outs]
    out_shape = [jax.ShapeDtypeStruct(tuple(d for d in (lead_cols(c)[0], t, lead_cols(c)[1]) if d is not None), dt)
                 for c, dt in row_outs] + [jax.ShapeDtypeStruct(s, F32) for s in acc_outs]
    res = hosted_call(
        riders, body, name=name, grid=(t // tm,), in_specs=in_specs, out_specs=out_specs, out_shape=out_shape,
        compiler_params=_params(("arbitrary",)),
    )(*[r[0] for r in row_ins], *full_ins)
    return res


def vjp_rows(fn, n_diff_row, row_diff_full):
    def bwd(*args, n_row, n_ct):
        prim_rows = args[:n_row]
        cts = args[n_row:n_row + n_ct]
        fulls = args[n_row + n_ct:]
        _, vjp = jax.vjp(fn, *prim_rows, *fulls)
        g = vjp(cts[0] if n_ct == 1 else tuple(cts))
        out = list(g[:n_diff_row])
        out += [gf for gf, d in zip(g[n_row:], row_diff_full) if d]
        return tuple(out)
    return bwd


def _shift_down(x, s):
    if s == 0:
        return x
    t = lax.broadcasted_iota(jnp.int32, x.shape, 0)
    return jnp.where(t >= s, pltpu.roll(x, s, 0), 0.0)


def _shift_up(x, s):
    if s == 0:
        return x
    n = x.shape[0]
    t = lax.broadcasted_iota(jnp.int32, x.shape, 0)
    return jnp.where(t < n - s, pltpu.roll(x, n - s, 0), 0.0)


def _conv(x, w_ref):
    return sum(w_ref[pl.ds(j, 1), :] * _shift_down(x, CONV_WIDTH - 1 - j) for j in range(CONV_WIDTH))


_DN_POST = (lambda c: l2n(jax.nn.silu(c)) * SCALE, lambda c: l2n(jax.nn.silu(c)), jax.nn.silu)


def dn_prep_fwd(proj, conv_w, *, name, riders=()):
    t = proj.shape[0]

    def body(xq, xk, xv, wq, wk, wv, oq, ok, ov):
        for x_ref, w_ref, o_ref, post in zip((xq, xk, xv), (wq, wk, wv), (oq, ok, ov), _DN_POST):
            o_ref[...] = post(_conv(x_ref[...], w_ref))

    x_specs = [pl.BlockSpec((t, HEAD_DIM), lambda h, g=g: (0, g * N_HEADS + h)) for g in range(3)]
    w_specs = [pl.BlockSpec((CONV_WIDTH, HEAD_DIM), lambda h, g=g: (0, g * N_HEADS + h)) for g in range(3)]
    o_spec = pl.BlockSpec((None, t, HEAD_DIM), lambda h: (h, 0, 0))
    return hosted_call(
        riders, body, name=name, grid=(N_HEADS,), in_specs=x_specs + w_specs, out_specs=[o_spec] * 3,
        out_shape=[jax.ShapeDtypeStruct((N_HEADS, t, HEAD_DIM), F32)] * 3, compiler_params=_params(("parallel",)),
    )(proj, proj, proj, conv_w, conv_w, conv_w)


def dn_prep_bwd(proj, conv_w, dq, dk, dv, *, name, riders=()):
    t = proj.shape[0]

    def body(xq, xk, xv, wq, wk, wv, gq, gk, gv, dxq, dxk, dxv, dwq, dwk, dwv):
        for x_ref, w_ref, g_ref, dx_ref, dw_ref, post in zip(
                (xq, xk, xv), (wq, wk, wv), (gq, gk, gv), (dxq, dxk, dxv), (dwq, dwk, dwv), _DN_POST):
            x = x_ref[...]
            _, vjp = jax.vjp(post, _conv(x, w_ref))
            dc, = vjp(g_ref[...])
            dx = sum(w_ref[pl.ds(j, 1), :] * _shift_up(dc, CONV_WIDTH - 1 - j) for j in range(CONV_WIDTH))
            dx_ref[...] = dx.astype(dx_ref.dtype)
            for j in range(CONV_WIDTH):
                dw_ref[pl.ds(j, 1), :] = jnp.sum(dc * _shift_down(x, CONV_WIDTH - 1 - j), axis=0, keepdims=True)

    x_specs = [pl.BlockSpec((t, HEAD_DIM), lambda h, g=g: (0, g * N_HEADS + h)) for g in range(3)]
    w_specs = [pl.BlockSpec((CONV_WIDTH, HEAD_DIM), lambda h, g=g: (0, g * N_HEADS + h)) for g in range(3)]
    g_spec = pl.BlockSpec((None, t, HEAD_DIM), lambda h: (h, 0, 0))
    dx_spec = pl.BlockSpec((t, HEAD_DIM), lambda h: (0, h))
    dw_spec = pl.BlockSpec((CONV_WIDTH, HEAD_DIM), lambda h: (0, h))
    return hosted_call(
        riders, body, name=name, grid=(N_HEADS,), in_specs=x_specs + w_specs + [g_spec] * 3, out_specs=[dx_spec] * 3 + [dw_spec] * 3,
        out_shape=[jax.ShapeDtypeStruct((t, D_MODEL), BF16)] * 3 + [jax.ShapeDtypeStruct((CONV_WIDTH, D_MODEL), F32)] * 3,
        compiler_params=_params(("parallel",)),
    )(proj, proj, proj, conv_w, conv_w, conv_w, dq, dk, dv)


INTRA_CHUNKS = 4


def _lane_column(x, lane_index):
    lane = lax.broadcasted_iota(jnp.int32, x.shape, 1)
    return jnp.sum(jnp.where(lane == lane_index, x, 0.0), axis=1, keepdims=True)


def _head_columns(g, first_lane):
    return jnp.concatenate([_lane_column(g, first_lane + h)[None] for h in range(N_HEADS)], axis=0)


def _intra_of_gates(q, k, v, gates):
    nb = N_HEADS * (gates.shape[0] // CHUNK)

    def chunks(x):
        return x.reshape(nb, CHUNK, x.shape[-1])

    res = delta_intra(chunks(q), chunks(k), chunks(v), chunks(_head_columns(gates, 0)), chunks(_head_columns(gates, N_HEADS)))
    return tuple(x.reshape(N_HEADS, -1, x.shape[-1]) for x in res)


def _step_of_gates(s, q, k, gates, u, w, qk):
    return delta_step(s, q, k, _head_columns(gates, 0), u, w, qk)


def _head_major(rows, w, index):
    return pl.BlockSpec((N_HEADS, rows, w), lambda i: (0, index(i), 0))


def delta_intra_fwd(q, k, v, gates, *, name, riders=()):
    t = q.shape[1]
    rows = min(INTRA_CHUNKS, t // CHUNK) * CHUNK

    def body(q_ref, k_ref, v_ref, g_ref, u_ref, w_ref, qk_ref):
        for ref, val in zip((u_ref, w_ref, qk_ref), _intra_of_gates(q_ref[...], k_ref[...], v_ref[...], g_ref[...])):
            ref[...] = val

    x_spec, qk_spec = (_head_major(rows, w, lambda i: i) for w in (HEAD_DIM, CHUNK))
    g_spec = pl.BlockSpec((rows, LANES), lambda i: (i, 0))
    return hosted_call(
        riders, body, name=name, grid=(t // rows,), in_specs=[x_spec] * 3 + [g_spec], out_specs=[x_spec, x_spec, qk_spec],
        out_shape=[jax.ShapeDtypeStruct((N_HEADS, t, HEAD_DIM), F32)] * 2 + [jax.ShapeDtypeStruct((N_HEADS, t, CHUNK), F32)],
        compiler_params=_params(("parallel",)),
    )(q, k, v, gates)


def delta_seq_fwd(q, k, gates, u, w, qk, *, name, riders=()):
    t = q.shape[1]
    nc = t // CHUNK

    def body(q_ref, k_ref, g_ref, u_ref, w_ref, qk_ref, o_ref, s0_ref, s_ref):
        @pl.when(pl.program_id(0) == 0)
        def _():
            s_ref[...] = jnp.zeros_like(s_ref)

        s = s_ref[...]
        s0_ref[...] = s
        o, s_new = _step_of_gates(s, q_ref[...], k_ref[...], g_ref[...], u_ref[...], w_ref[...], qk_ref[...])
        o_ref[...] = o
        s_ref[...] = s_new

    x_spec, qk_spec = (_head_major(CHUNK, w, lambda c: c) for w in (HEAD_DIM, CHUNK))
    g_spec = pl.BlockSpec((CHUNK, LANES), lambda c: (c, 0))
    s_spec = pl.BlockSpec((N_HEADS, None, HEAD_DIM, HEAD_DIM), lambda c: (0, c, 0, 0))
    return hosted_call(
        riders, body, name=name, grid=(nc,), in_specs=[x_spec, x_spec, g_spec, x_spec, x_spec, qk_spec], out_specs=[x_spec, s_spec],
        out_shape=[jax.ShapeDtypeStruct((N_HEADS, t, HEAD_DIM), F32),
                   jax.ShapeDtypeStruct((N_HEADS, nc, HEAD_DIM, HEAD_DIM), F32)],
        scratch_shapes=[pltpu.VMEM((N_HEADS, HEAD_DIM, HEAD_DIM), F32)],
        compiler_params=_params(("arbitrary",)),
    )(q, k, gates, u, w, qk)


def delta_seq_bwd(q, k, gates, u, w, qk, s0, do, *, name, riders=()):
    t = q.shape[1]
    nc = t // CHUNK

    def body(q_ref, k_ref, g_ref, u_ref, w_ref, qk_ref, s0_ref, do_ref,
             dq_ref, dk_ref, dg_ref, du_ref, dw_ref, dqk_ref, ds_ref):
        @pl.when(pl.program_id(0) == 0)
        def _():
            ds_ref[...] = jnp.zeros_like(ds_ref)

        _, vjp = jax.vjp(_step_of_gates, s0_ref[...], q_ref[...], k_ref[...], g_ref[...], u_ref[...], w_ref[...], qk_ref[...])
        ds, dq, dk, dg, du, dw, dqk = vjp((do_ref[...], ds_ref[...]))
        for ref, val in zip((ds_ref, dq_ref, dk_ref, dg_ref, du_ref, dw_ref, dqk_ref), (ds, dq, dk, dg, du, dw, dqk)):
            ref[...] = val

    x_spec, qk_spec = (_head_major(CHUNK, w, lambda c: nc - 1 - c) for w in (HEAD_DIM, CHUNK))
    g_spec = pl.BlockSpec((CHUNK, LANES), lambda c: (nc - 1 - c, 0))
    s_spec = pl.BlockSpec((N_HEADS, None, HEAD_DIM, HEAD_DIM), lambda c: (0, nc - 1 - c, 0, 0))
    head_shape = [jax.ShapeDtypeStruct((N_HEADS, t, w_), F32) for w_ in (HEAD_DIM, HEAD_DIM, HEAD_DIM, HEAD_DIM, CHUNK)]
    return hosted_call(
        riders, body, name=name, grid=(nc,), in_specs=[x_spec, x_spec, g_spec, x_spec, x_spec, qk_spec, s_spec, x_spec],
        out_specs=[x_spec, x_spec, g_spec, x_spec, x_spec, qk_spec],
        out_shape=head_shape[:2] + [jax.ShapeDtypeStruct((t, LANES), F32)] + head_shape[2:],
        scratch_shapes=[pltpu.VMEM((N_HEADS, HEAD_DIM, HEAD_DIM), F32)],
        compiler_params=_params(("arbitrary",)),
    )(q, k, gates, u, w, qk, s0, do)


def delta_intra_bwd(q, k, v, gates, du, dw, dqk, dq_s, dk_s, dg_s, *, name, riders=()):
    t = q.shape[1]
    rows = min(INTRA_CHUNKS, t // CHUNK) * CHUNK

    def body(q_ref, k_ref, v_ref, g_ref, du_ref, dw_ref, dqk_ref, dqs_ref, dks_ref, dgs_ref, dq_ref, dk_ref, dv_ref, dg_ref):
        _, vjp = jax.vjp(_intra_of_gates, q_ref[...], k_ref[...], v_ref[...], g_ref[...])
        dq, dk, dv, dg = vjp((du_ref[...], dw_ref[...], dqk_ref[...]))
        dq_ref[...] = dq + dqs_ref[...]
        dk_ref[...] = dk + dks_ref[...]
        dv_ref[...] = dv
        dg_ref[...] = dg + dgs_ref[...]

    x_spec, qk_spec = (_head_major(rows, w, lambda i: i) for w in (HEAD_DIM, CHUNK))
    g_spec = pl.BlockSpec((rows, LANES), lambda i: (i, 0))
    return hosted_call(
        riders, body, name=name, grid=(t // rows,),
        in_specs=[x_spec] * 3 + [g_spec, x_spec, x_spec, qk_spec, x_spec, x_spec, g_spec],
        out_specs=[x_spec] * 3 + [g_spec],
        out_shape=[jax.ShapeDtypeStruct((N_HEADS, t, HEAD_DIM), F32)] * 3 + [jax.ShapeDtypeStruct((t, LANES), F32)],
        compiler_params=_params(("parallel",)),
    )(q, k, v, gates, du, dw, dqk, dq_s, dk_s, dg_s)


_V_BLOCK = 2 * N_HEADS
FOX_GROUPS = 8


def _fox_groups(t):
    nq = t // Q_BLOCK
    per = max(1, nq // FOX_GROUPS)
    return [(g0, per, (g0 + per) * Q_BLOCK) for g0 in range(0, nq, per)]


def fox_attn_fwd(q, k, proj, fq, fk, *, name, riders=()):
    t = q.shape[0]

    def body(q_ref, k_ref, v_ref, fq_ref, fk_ref, o_ref, kb_ref, vb_ref):
        head = pl.program_id(0)
        kb_ref[...] = k_ref[...].astype(BF16)
        vb_ref[...] = v_ref[...].astype(BF16)
        for g0, per, keys in _fox_groups(t):
            def block(j, carry, g0=g0, keys=keys):
                rows = pl.ds(pl.multiple_of((g0 + j) * Q_BLOCK, Q_BLOCK), Q_BLOCK)
                p = fox_probs(q_ref[rows, :].astype(BF16), kb_ref[0:keys, :], _lane_column(fq_ref[rows, :], head),
                              fk_ref[:, 0:keys], (g0 + j) * Q_BLOCK)
                o_ref[rows, :] = jnp.dot(p.astype(BF16), vb_ref[0:keys, :], preferred_element_type=F32)
                return carry
            lax.fori_loop(0, per, block, 0)

    x_spec = pl.BlockSpec((t, HEAD_DIM), lambda h: (0, h))
    v_spec = pl.BlockSpec((t, HEAD_DIM), lambda h: (0, _V_BLOCK + h))
    fq_spec = pl.BlockSpec((t, LANES), lambda h: (0, 0))
    fk_spec = pl.BlockSpec((None, 1, t), lambda h: (h, 0, 0))
    return hosted_call(
        riders, body, name=name, grid=(N_HEADS,), in_specs=[x_spec, x_spec, v_spec, fq_spec, fk_spec], out_specs=x_spec,
        out_shape=jax.ShapeDtypeStruct((t, D_MODEL), F32), scratch_shapes=[pltpu.VMEM((t, HEAD_DIM), BF16)] * 2,
        compiler_params=_params(("parallel",)),
    )(q, k, proj, fq, fk)


def fox_attn_bwd(q, k, proj, fq, fk, do, *, name, riders=()):
    t = q.shape[0]

    def body(q_ref, k_ref, v_ref, fq_ref, fk_ref, do_ref, dq_ref, dk_ref, dv_out_ref, dfq_ref, dfk_ref, kb_ref, vb_ref, dv_ref):
        head = pl.program_id(0)

        @pl.when(head == 0)
        def _():
            dfq_ref[...] = jnp.zeros_like(dfq_ref)

        kb_ref[...] = k_ref[...].astype(BF16)
        vb_ref[...] = v_ref[...].astype(BF16)
        dk_ref[...] = jnp.zeros_like(dk_ref)
        dv_ref[...] = jnp.zeros_like(dv_ref)
        dfk_ref[...] = jnp.zeros_like(dfk_ref)
        nt = (((1,), (1,)), ((), ()))
        tn = (((0,), (0,)), ((), ()))
        for g0, per, keys in _fox_groups(t):
            def block(j, carry, g0=g0, keys=keys):
                rows = pl.ds(pl.multiple_of((g0 + j) * Q_BLOCK, Q_BLOCK), Q_BLOCK)
                qb, dob = q_ref[rows, :].astype(BF16), do_ref[rows, :].astype(BF16)
                kb, vb = kb_ref[0:keys, :], vb_ref[0:keys, :]
                p = fox_probs(qb, kb, _lane_column(fq_ref[rows, :], head), fk_ref[:, 0:keys], (g0 + j) * Q_BLOCK)
                dp = lax.dot_general(dob, vb, nt, preferred_element_type=F32)
                dz = p * (dp - jnp.sum(dp * p, axis=-1, keepdims=True))
                pb, dzb = p.astype(BF16), dz.astype(BF16)
                dq_ref[rows, :] = jnp.dot(dzb, kb, preferred_element_type=F32)
                lane = lax.broadcasted_iota(jnp.int32, (Q_BLOCK, LANES), 1)
                dfq_ref[rows, :] += jnp.where(lane == head, jnp.sum(dz, axis=-1, keepdims=True), 0.0)
                dk_ref[0:keys, :] += lax.dot_general(dzb, qb, tn, preferred_element_type=F32)
                dv_ref[0:keys, :] += lax.dot_general(pb, dob, tn, preferred_element_type=F32)
                dfk_ref[:, 0:keys] -= jnp.sum(dz, axis=0, keepdims=True)
                return carry
            lax.fori_loop(0, per, block, 0)
        dv_out_ref[...] = dv_ref[...].astype(dv_out_ref.dtype)

    x_spec = pl.BlockSpec((t, HEAD_DIM), lambda h: (0, h))
    v_spec = pl.BlockSpec((t, HEAD_DIM), lambda h: (0, _V_BLOCK + h))
    fq_spec = pl.BlockSpec((t, LANES), lambda h: (0, 0))
    fk_spec = pl.BlockSpec((None, 1, t), lambda h: (h, 0, 0))
    return hosted_call(
        riders, body, name=name, grid=(N_HEADS,), in_specs=[x_spec, x_spec, v_spec, fq_spec, fk_spec, x_spec],
        out_specs=[x_spec, x_spec, x_spec, fq_spec, fk_spec],
        out_shape=[jax.ShapeDtypeStruct((t, D_MODEL), F32)] * 2 + [jax.ShapeDtypeStruct((t, D_MODEL), BF16)]
        + [jax.ShapeDtypeStruct((t, LANES), F32), jax.ShapeDtypeStruct((N_HEADS, 1, t), F32)],
        scratch_shapes=[pltpu.VMEM((t, HEAD_DIM), BF16)] * 2 + [pltpu.VMEM((t, HEAD_DIM), F32)],
        compiler_params=_params(("arbitrary",)),
    )(q, k, proj, fq, fk, do)


def memkv_fwd(mem, mnw, wkv, mknw, *, name):
    n = mem.shape[0]

    def body(mem_ref, mnw_ref, w_ref, mknw_ref, mk_ref, mv_ref):
        mk, mv = memkv_fn(mem_ref[...], mnw_ref[...], w_ref[...], mknw_ref[...])
        mk_ref[...] = mk
        mv_ref[...] = mv

    return pl.pallas_call(
        body, name=name, out_shape=[jax.ShapeDtypeStruct((n, MEM_WIDTH), F32)] * 2,
        compiler_params=pltpu.CompilerParams(vmem_limit_bytes=VMEM_LIMIT),
    )(mem, mnw, wkv, mknw)


def memkv_bwd(mem, mnw, wkv, mknw, dmk, dmv, *, name):
    def body(mem_ref, mnw_ref, w_ref, mknw_ref, dmk_ref, dmv_ref, dmnw_ref, dw_ref, dmknw_ref):
        f = functools.partial(memkv_fn, mem_ref[...])
        _, vjp = jax.vjp(f, mnw_ref[...], w_ref[...].astype(F32), mknw_ref[...])
        dmnw, dw, dmknw = vjp((dmk_ref[...], dmv_ref[...]))
        dmnw_ref[...] = dmnw
        dw_ref[...] = dw.astype(dw_ref.dtype)
        dmknw_ref[...] = dmknw

    return pl.pallas_call(
        body, name=name,
        out_shape=[jax.ShapeDtypeStruct(mnw.shape, F32), jax.ShapeDtypeStruct(wkv.shape, BF16), jax.ShapeDtypeStruct(mknw.shape, F32)],
        compiler_params=pltpu.CompilerParams(vmem_limit_bytes=VMEM_LIMIT),
    )(mem, mnw, wkv, mknw, dmk, dmv)


def _row(v, width=None):
    v = v.reshape(1, -1)
    if width is not None and v.shape[1] < width:
        v = jnp.pad(v, ((0, 0), (0, width - v.shape[1])))
    return v


def _norm_fwd(x, w, name, riders=()):
    return rows_call(lambda x, w: rms(x, w), [x], [w], [(D_MODEL, BF16)], [], tm=512, name=name, riders=riders)[0]


FF_PIECE = D_FF // N_DEV


def _add(r, x, *rows):
    return r + x


def _norm_rows(r, w):
    return rms(r, w)


def _norm_bwd_post(dh, x, dx_in, w):
    _, vjp = jax.vjp(rms, x, w)
    dx, dw = vjp(dh)
    return dx + dx_in, dw


def _piece(rows, cols, index):
    return pl.BlockSpec((None, rows, cols), lambda i, j, kk: (index(i, j, kk), 0, 0))


def _two_pieces(rows, cols, index):
    return pl.BlockSpec((2, rows, cols), lambda i, j, kk: (index(i, j, kk), 0, 0))


def _mlp_fwd(x, h2, w1, w2, layer, riders=(), next_norm_w=None):
    riders = list(riders) + [None, None]
    u, a1 = matmul(h2, w1, name=f"mlp1_fwd_{layer}", tiles=(None, FF_PIECE, D_MODEL),
                   extra_out=(lambda u: jnp.square(jnp.maximum(u, 0.0)), BF16),
                   b_view=(D_MODEL, D_FF, _piece(D_MODEL, FF_PIECE, lambda i, j, kk: j)), riders=riders[0])
    norm = dict(row_ins=[next_norm_w], extra_out=(_norm_rows, BF16)) if next_norm_w is not None else {}
    y = matmul(a1, w2, name=f"mlp2_fwd_{layer}", post=_add, post_ins=[x], tiles=(None, D_MODEL, 2 * FF_PIECE),
               b_view=(D_FF, D_MODEL, _two_pieces(FF_PIECE, D_MODEL, lambda i, j, kk: kk)), riders=riders[1], **norm)
    return y, (x, h2, u, a1)


def pair_sum(g, got, *, name):
    _, rows, cols = g.shape
    tile = _pick(rows, (512, 256, 128))
    c = lax.axis_index("c").astype(jnp.int32).reshape(1)

    def body(c_ref, a_ref, b_ref, o_ref):
        o_ref[...] = (a_ref[...].astype(F32) + b_ref[...].astype(F32)).astype(o_ref.dtype)

    grid_spec = pltpu.PrefetchScalarGridSpec(
        num_scalar_prefetch=1, grid=(4, rows // tile),
        in_specs=[pl.BlockSpec((None, tile, cols), lambda k, i, c_ref: (2 * k + c_ref[0], i, 0)),
                  pl.BlockSpec((None, tile, cols), lambda k, i, c_ref: (k, i, 0))],
        out_specs=pl.BlockSpec((None, tile, cols), lambda k, i, c_ref: (k, i, 0)))
    return pl.pallas_call(
        body, name=name, grid_spec=grid_spec, out_shape=jax.ShapeDtypeStruct((4, rows, cols), g.dtype),
        compiler_params=_params(("parallel", "parallel")),
    )(c, g, got)


def chip_sums(names, pieces, gots):
    return [pair_sum(a, got, name=f"grads_pair_sum_{n}") for n, a, got in zip(names, pieces, gots)]


def _mlp_bwd(dy, res, n2w, w1, w2, layer, riders=()):
    x, h2, u, a1 = res
    du = matmul(dy, w2, tb=True, name=f"mlp2_dx_{layer}", out_dtype=BF16, tiles=(None, 2 * FF_PIECE, D_MODEL),
                post=lambda r, u: r * (2.0 * jnp.maximum(u, 0.0)), post_ins=[u],
                b_view=(D_MODEL, D_FF, _two_pieces(FF_PIECE, D_MODEL, lambda i, j, kk: j)), riders=riders)
    dw2 = matmul(a1, dy, ta=True, name=f"mlp2_dw_{layer}", out_dtype=BF16, tiles=(FF_PIECE, D_MODEL, None), out_view=(
        w2.shape, _piece(FF_PIECE, D_MODEL, lambda i, j, kk: i)))
    sib2 = sibling_rider([dw2])
    dx, dn2w = matmul(du, w1, tb=True, name=f"mlp1_dx_{layer}", tiles=(None, D_MODEL, FF_PIECE),
                      b_view=(D_FF, D_MODEL, _piece(D_MODEL, FF_PIECE, lambda i, j, kk: kk)),
                      post=_norm_bwd_post, post_ins=[x, dy], row_ins=[n2w], acc=True, riders=[sib2])
    dw1 = matmul(h2, du, ta=True, name=f"mlp1_dw_{layer}", out_dtype=BF16, tiles=(D_MODEL, FF_PIECE, None), out_view=(
        w1.shape, _piece(D_MODEL, FF_PIECE, lambda i, j, kk: j)))
    return dx, dw1, dw2, dn2w, sibling_rider([dw1]), sib2


def _in_proj_bwd(h, dmain, dsmall, w_main, w_small, x, dx_in, n1w, tag):
    dh = matmul(dmain, w_main, tb=True, name=f"inproj_dx_main_{tag}")

    def post(r, dh_main, x, dx_in, w):
        return _norm_bwd_post(r + dh_main, x, dx_in, w)

    dx, dn1w = matmul(dsmall, w_small, tb=True, name=f"inproj_dx_small_{tag}", tiles=(None, D_MODEL, None),
                      post=post, post_ins=[dh, x, dx_in], row_ins=[n1w], acc=True)
    dw_main = matmul(h, dmain, ta=True, out_dtype=BF16, name=f"inproj_dw_main_{tag}")
    dw_small = matmul(h, dsmall, ta=True, out_dtype=BF16, name=f"inproj_dw_small_{tag}")
    return dx, dn1w, dw_main, dw_small


def local_step(x, mem, target, w, m, v):
    t = x.shape[0]
    n_mem = mem.shape[0]
    g = {}

    def wire(a):
        return a.astype(BF16)

    (dn_g,), = run_riders([gather_rider([wire(w["dn_w_in"][0])])], name="weights_gather_first")
    dn_main, dn_ab = in_proj_weights(dn_g, DN_IN, 2 * N_HEADS)
    fox_w = wire(w["fox_w_in"][0])
    ride_out = gather_rider([wire(w["w_out"][0]), wire(w["w_out"][1]), w["dn_conv_w"][0]])
    ride_kv = gather_rider([wire(w["w_mem_kv"])])
    ride_mlp1_0 = gather_rider([wire(w["w_mlp1"][0])])
    ride_mlp2_0 = gather_rider([wire(w["w_mlp2"][0])])
    ride_fox_a, ride_fox_b = gather_rider([fox_w[:D_MODEL // 2]]), gather_rider([fox_w[D_MODEL // 2:]])
    ride_mlp_1 = gather_rider([wire(w["w_mlp1"][1]), wire(w["w_mlp2"][1])])
    mnw, mknw = _row(w["mem_norm_w"]), _row(w["mem_k_norm_w"])

    n1w0, n2w0 = _row(w["norm1_w"][0]), _row(w["norm2_w"][0])
    n1w1, n2w1 = _row(w["norm1_w"][1]), _row(w["norm2_w"][1])
    alog, dtb = _row(w["dn_a_log"][0], LANES), _row(w["dn_dt_bias"][0], LANES)
    onw, mqw0 = _row(w["dn_o_norm_w"][0]), _row(w["memq_norm_w"][0])
    x0 = x
    h0 = _norm_fwd(x0, n1w0, "norm1_fwd_0")
    pm0 = matmul(h0, dn_main, name="inproj_main_0", riders=[ride_out])
    w_out0, w_out1 = (a.reshape(OUT_IN, D_MODEL) for a in ride_out.results[:2])
    conv_w = ride_out.results[2].transpose(1, 0, 2).reshape(CONV_WIDTH, 3 * D_MODEL)
    ps0 = matmul(h0, dn_ab, name="inproj_small_0")
    gates = rows_call(dn_gates_fn, [ps0], [alog, dtb], [(LANES, F32)], [], tm=512, name="dn_gates_fwd")[0]
    q0, k0, v0 = dn_prep_fwd(pm0, conv_w, name="dn_prep_fwd", riders=[ride_kv])
    w_kv = ride_kv.results[0].reshape(D_MODEL, D_MODEL)
    mk, mv = memkv_fwd(mem, mnw, w_kv, mknw, name="memkv_fwd")
    u0, w0, qk0 = delta_intra_fwd(q0, k0, v0, gates, name="delta_intra_fwd", riders=[ride_mlp1_0])
    o0, s_start = delta_seq_fwd(q0, k0, gates, u0, w0, qk0, name="delta_seq_fwd", riders=[ride_mlp2_0])
    cat0 = rows_call(dn_out_fn, [o0, (pm0, D_MODEL, 3), (pm0, MEM_WIDTH, 8)], [onw, mqw0, mk, mv],
                     [(D_MODEL + MEM_WIDTH, BF16)], [], tm=256, name="dn_out_fwd")[0]
    (w1_0,), (w2_0,) = ride_mlp1_0.results, ride_mlp2_0.results
    x1, h2_0 = matmul(cat0, w_out0, post=_add, post_ins=[x0], row_ins=[n2w0], extra_out=(_norm_rows, BF16),
                      tiles=(None, D_MODEL, None), name="wout_fwd_0")
    (x2, h1), mlp_res0 = _mlp_fwd(x1, h2_0, w1_0, w2_0, 0, riders=[[ride_fox_a], [ride_fox_b]], next_norm_w=n1w1)
    fox_main, fox_f = in_proj_weights(
        jnp.concatenate([ride_fox_a.results[0], ride_fox_b.results[0]], axis=1), FOX_IN, N_HEADS)

    fbias = _row(w["fox_f_bias"][0], LANES)
    qnw, knw, mqw1 = _row(w["fox_q_norm_w"][0]), _row(w["fox_k_norm_w"][0]), _row(w["memq_norm_w"][1])
    pm1 = matmul(h1, fox_main, name="inproj_main_1")
    ps1 = matmul(h1, fox_f, name="inproj_small_1")
    fq = rows_call(fox_fcum_fn, [ps1], [fbias], [(LANES, F32)], [], tm=t, name="fox_fcum_fwd")[0]
    fk = fq[:, :N_HEADS].T[:, None, :]
    q1, k1 = rows_call(fox_qk_fn, [(pm1, D_MODEL, 0), (pm1, D_MODEL, 1)], [qnw, knw], [(D_MODEL, F32)] * 2, [], tm=256,
                       name="fox_qk_fwd")
    o1 = fox_attn_fwd(q1, k1, pm1, fq, fk, name="fox_attn_fwd", riders=[ride_mlp_1])
    cat1 = rows_call(fox_out_fn, [o1, (pm1, D_MODEL, 3), (pm1, MEM_WIDTH, 8)], [mqw1, mk, mv],
                     [(D_MODEL + MEM_WIDTH, BF16)], [], tm=256, name="fox_out_fwd")[0]
    w1_1, w2_1 = ride_mlp_1.results
    x3, h2_1 = matmul(cat1, w_out1, post=_add, post_ins=[x2], row_ins=[n2w1], extra_out=(_norm_rows, BF16),
                      tiles=(None, D_MODEL, None), name="wout_fwd_1")
    y, mlp_res1 = _mlp_fwd(x3, h2_1, w1_1, w2_1, 1)

    def loss_fn(y, tgt):
        e = y - tgt
        return e * (1.0 / D_MODEL), jnp.sum(jnp.sum(e * e, axis=1, keepdims=True), axis=0, keepdims=True)
    dy, sq = rows_call(loss_fn, [y, target], [], [(D_MODEL, F32)], [(1, 1)], tm=512, name="loss")
    loss = sq[0, 0] * (0.5 / D_MODEL)

    dx3, dw1_1, dw2_1, dn2w1, sib1, sib2 = _mlp_bwd(dy, mlp_res1, n2w1, w1_1, w2_1, 1)
    dcat1 = matmul(dx3, w_out1, tb=True, name="wout_dx_1", riders=[sib1])
    dwo_1 = matmul(cat1, dx3, ta=True, out_dtype=BF16, name="wout_dw_1").reshape(N_DEV, OUT_IN // N_DEV, D_MODEL)
    sibo = sibling_rider([dwo_1])
    do1, dgate1, dqm1, dmqw1, dmk1, dmv1 = rows_call(
        functools.partial(vjp_rows(fox_out_fn, 3, (True, True, True)), n_row=3, n_ct=1),
        [o1, (pm1, D_MODEL, 3), (pm1, MEM_WIDTH, 8), dcat1], [mqw1, mk, mv],
        [(D_MODEL, F32), (D_MODEL, BF16), (MEM_WIDTH, BF16)], [(1, HEAD_DIM), (n_mem, MEM_WIDTH), (n_mem, MEM_WIDTH)],
        tm=256, name="fox_out_bwd", riders=[sibo])
    ride_l1 = chips_rider(chip_sums(["w_mlp2_1", "w_mlp1_1", "w_out_1"], [dw2_1, dw1_1, dwo_1],
                                    sib2.results + sib1.results + sibo.results))
    dq1, dk1, dv1, dfq, dfk = fox_attn_bwd(q1, k1, pm1, fq, fk, do1, name="fox_attn_bwd", riders=[ride_l1])
    dqraw1, dkraw1, dqnw, dknw = rows_call(
        functools.partial(vjp_rows(fox_qk_fn, 2, (True, True)), n_row=2, n_ct=2),
        [(pm1, D_MODEL, 0), (pm1, D_MODEL, 1), dq1, dk1], [qnw, knw],
        [(D_MODEL, BF16)] * 2, [(1, HEAD_DIM)] * 2, tm=256, name="fox_qk_bwd")
    dfcum = dfq + jnp.pad(dfk[:, 0, :].T, ((0, 0), (0, LANES - N_HEADS)))
    dps1, dfbias = rows_call(
        functools.partial(vjp_rows(fox_fcum_fn, 1, (True,)), n_row=1, n_ct=1),
        [ps1, dfcum], [fbias], [(LANES, F32)], [(1, LANES)], tm=t, name="fox_fcum_bwd")
    dpm1 = jnp.concatenate([dqraw1, dkraw1, dv1, dgate1, dqm1], axis=1)
    dx2, dn1w1, dwmain1, dwsmall1 = _in_proj_bwd(h1, dpm1, dps1, fox_main, fox_f, x2, dx3, n1w1, "1")
    g_fox = in_proj_pieces(dwmain1, dwsmall1, N_HEADS, FOX_IN)
    sibf = sibling_rider([g_fox])

    dx1, dw1_0, dw2_0, dn2w0, sib1, sib2 = _mlp_bwd(dx2, mlp_res0, n2w0, w1_0, w2_0, 0, riders=[sibf])
    ride_fox_g = chips_rider(chip_sums(["fox_w_in"], [g_fox], sibf.results))
    dcat0 = matmul(dx1, w_out0, tb=True, name="wout_dx_0", riders=[sib1])
    dwo_0 = matmul(cat0, dx1, ta=True, out_dtype=BF16, name="wout_dw_0").reshape(N_DEV, OUT_IN // N_DEV, D_MODEL)
    sibo = sibling_rider([dwo_0])
    do0, dz0, dqm0, donw, dmqw0, dmk0, dmv0 = rows_call(
        functools.partial(vjp_rows(dn_out_fn, 3, (True, True, True, True)), n_row=3, n_ct=1),
        [o0, (pm0, D_MODEL, 3), (pm0, MEM_WIDTH, 8), dcat0], [onw, mqw0, mk, mv],
        [((N_HEADS, HEAD_DIM), F32), (D_MODEL, BF16), (MEM_WIDTH, BF16)],
        [(1, HEAD_DIM), (1, HEAD_DIM), (n_mem, MEM_WIDTH), (n_mem, MEM_WIDTH)], tm=256, name="dn_out_bwd", riders=[sibo])
    h_l0 = chip_sums(["w_mlp2_0", "w_mlp1_0", "w_out_0"], [dw2_0, dw1_0, dwo_0], sib2.results + sib1.results + sibo.results)
    ride_l0_mlp, ride_l0_out = chips_rider(h_l0[:2]), chips_rider(h_l0[2:])
    dmnw, dwkv, dmknw = memkv_bwd(mem, mnw, w_kv, mknw, dmk0 + dmk1, dmv0 + dmv1, name="memkv_bwd")
    g_kv = dwkv.reshape(N_DEV, D_MODEL // N_DEV, D_MODEL)
    sibk = sibling_rider([g_kv])
    dq_s, dk_s, dg_s, du0, dw0, dqk0 = delta_seq_bwd(q0, k0, gates, u0, w0, qk0, s_start, do0, name="delta_seq_bwd",
                                                     riders=[ride_fox_g, sibk])
    ride_kv_g = chips_rider(chip_sums(["w_mem_kv"], [g_kv], sibk.results))
    dq0, dk0, dv0, dgates = delta_intra_bwd(q0, k0, v0, gates, du0, dw0, dqk0, dq_s, dk_s, dg_s,
                                            name="delta_intra_bwd", riders=[ride_l0_mlp, ride_kv_g])
    dxq, dxk, dxv, dcq, dck, dcv = dn_prep_bwd(pm0, conv_w, dq0, dk0, dv0, name="dn_prep_bwd", riders=[ride_l0_out])
    dconv = jnp.concatenate([dcq, dck, dcv], axis=1)
    dps0, dalog, ddtb = rows_call(
        functools.partial(vjp_rows(dn_gates_fn, 1, (True, True)), n_row=1, n_ct=1),
        [ps0, dgates], [alog, dtb], [(LANES, F32)], [(1, LANES)] * 2, tm=512, name="dn_gates_bwd")
    dpm0 = jnp.concatenate([dxq, dxk, dxv, dz0, dqm0], axis=1)
    grad_x, dn1w0, dwmain0, dwsmall0 = _in_proj_bwd(h0, dpm0, dps0, dn_main, dn_ab, x0, dx1, n1w0, "0")
    g_dn = in_proj_pieces(dwmain0, dwsmall0, 2 * N_HEADS, DN_IN)
    g_conv = dconv.reshape(CONV_WIDTH, N_DEV, -1).transpose(1, 0, 2).astype(BF16)

    g["mem_norm_w"] = dmnw[0]
    g["mem_k_norm_w"] = dmknw[0]
    g["norm1_w"] = jnp.concatenate([dn1w0, dn1w1], axis=0)
    g["dn_a_log"] = dalog[:, :N_HEADS]
    g["dn_dt_bias"] = ddtb[:, :N_HEADS]
    g["dn_o_norm_w"] = donw
    g["fox_f_bias"] = dfbias[:, :N_HEADS]
    g["fox_q_norm_w"] = dqnw
    g["fox_k_norm_w"] = dknw
    g["memq_norm_w"] = jnp.concatenate([dmqw0, dmqw1], axis=0)
    g["norm2_w"] = jnp.concatenate([dn2w0, dn2w1], axis=0)

    sibd = sibling_rider([g_dn, g_conv])
    run_riders([sibd], name="grads_to_sibling_last")
    ride_last = chips_rider(chip_sums(["dn_w_in", "dn_conv_w"], [g_dn, g_conv], sibd.results))
    ride_small = gather_rider([pack_small(g, last=loss)])
    run_riders([ride_last, ride_small], name="grads_to_chips_last")

    def layers(l0, l1):
        return jnp.stack([l0, l1], axis=1).reshape(4, -1, l0.shape[-1])

    parts = {
        "w_mlp1": layers(ride_l0_mlp.results[1], ride_l1.results[1]),
        "w_mlp2": layers(ride_l0_mlp.results[0], ride_l1.results[0]),
        "w_out": layers(ride_l0_out.results[0], ride_l1.results[2]),
        "fox_w_in": ride_fox_g.results[0], "w_mem_kv": ride_kv_g.results[0],
        "dn_w_in": ride_last.results[0], "dn_conv_w": ride_last.results[1],
    }
    out = {n: adamw(parts[n], w[n], m[n], v[n], name=f"adamw_{n}") for n, _, _ in BIG}
    small = adamw(ride_small.results[0], pack_small(w), pack_small(m), pack_small(v), name="adamw_small")
    loss = small[0][-1, -1]
    return loss, grad_x, out, small


WEIGHTS = ["mem_norm_w", "w_mem_kv", "mem_k_norm_w", "norm1_w", "dn_w_in", "dn_conv_w", "dn_a_log", "dn_dt_bias",
           "dn_o_norm_w", "fox_w_in", "fox_f_bias", "fox_q_norm_w", "fox_k_norm_w", "memq_norm_w", "w_out", "norm2_w",
           "w_mlp1", "w_mlp2"]
DN_IN = 4 * D_MODEL + 2 * N_HEADS + MEM_WIDTH
FOX_IN = 4 * D_MODEL + N_HEADS + MEM_WIDTH
GATE_END = 4 * D_MODEL
OUT_IN = D_MODEL + MEM_WIDTH
BIG = [("w_mem_kv", D_MODEL // N_DEV, D_MODEL), ("dn_w_in", D_MODEL, DN_IN // N_DEV), ("fox_w_in", D_MODEL, FOX_IN // N_DEV),
       ("dn_conv_w", CONV_WIDTH, 3 * D_MODEL // N_DEV), ("w_out", 2 * OUT_IN // N_DEV, D_MODEL),
       ("w_mlp1", 2 * D_MODEL, FF_PIECE), ("w_mlp2", 2 * FF_PIECE, D_MODEL)]
SMALL_TILE = 8 * LANES
SMALL = [(name, shape, -(-math.prod(shape) // SMALL_TILE) * SMALL_TILE) for name, shape in [
    ("mem_norm_w", (D_MODEL,)), ("mem_k_norm_w", (HEAD_DIM,)), ("norm1_w", (2, D_MODEL)), ("dn_a_log", (1, N_HEADS)),
    ("dn_dt_bias", (1, N_HEADS)), ("dn_o_norm_w", (1, HEAD_DIM)), ("fox_f_bias", (1, N_HEADS)),
    ("fox_q_norm_w", (1, HEAD_DIM)), ("fox_k_norm_w", (1, HEAD_DIM)), ("memq_norm_w", (2, HEAD_DIM)), ("norm2_w", (2, D_MODEL))]]
SMALL_ROWS = sum(ln for _, _, ln in SMALL) // LANES + 8


def pack_small(p, last=None):
    def rows(a, ln):
        a = a.reshape(-1)
        return (a if a.shape[0] == ln else jnp.pad(a, (0, ln - a.shape[0]))).reshape(-1, LANES)

    used = sum(ln for _, _, ln in SMALL) // LANES
    tail = jnp.zeros(((SMALL_ROWS - used) * LANES,), F32)
    if last is not None:
        tail = jnp.concatenate([tail[:-1], last.reshape(1)])
    return jnp.concatenate([rows(p[n], ln) for n, _, ln in SMALL] + [tail.reshape(-1, LANES)], axis=0)


def unpack_small(pk):
    row, out = 0, {}
    for n, sh, ln in SMALL:
        out[n] = pk[row:row + ln // LANES].reshape(-1)[:math.prod(sh)].reshape(sh)
        row += ln // LANES
    return out


def in_proj_weights(gathered, width, n_small):
    full = gathered.transpose(1, 0, 2).reshape(D_MODEL, width)
    main = jnp.concatenate([full[:, :GATE_END], full[:, GATE_END + n_small:]], axis=1)
    return main, jnp.pad(full[:, GATE_END:GATE_END + n_small], ((0, 0), (0, LANES - n_small)))


def in_proj_pieces(d_main, d_small, n_small, width):
    full = jnp.concatenate([d_main[:, :GATE_END], d_small[:, :n_small], d_main[:, GATE_END:]], axis=1)
    return full.reshape(D_MODEL, N_DEV, width // N_DEV).transpose(1, 0, 2)


def adamw(parts, w, m, v, *, name):
    n, _, cols = parts.shape
    layers = w.shape[0] if w.ndim == 3 else 1
    rows = w.shape[-2]
    tile = _pick(rows, (512, 256, 128))
    steps = rows // tile

    def body(p_ref, w_ref, m_ref, v_ref, g_ref, d_ref, mo_ref, vo_ref):
        g = p_ref[0].astype(F32)
        for i in range(1, n):
            g = g + p_ref[i].astype(F32)
        m_new = ADAM_B1 * m_ref[...] + (1.0 - ADAM_B1) * g
        v_new = ADAM_B2 * v_ref[...] + (1.0 - ADAM_B2) * jnp.square(g)
        m_hat = m_new / (1.0 - ADAM_B1 ** ADAM_STEP)
        v_hat = v_new / (1.0 - ADAM_B2 ** ADAM_STEP)
        g_ref[...] = g
        d_ref[...] = -ADAM_LR * (m_hat / (jnp.sqrt(v_hat) + ADAM_EPS) + ADAM_WD * w_ref[...])
        mo_ref[...] = m_new
        vo_ref[...] = v_new

    if w.ndim == 3:
        spec = pl.BlockSpec((None, tile, cols), lambda l, i: (l, i, 0))
    else:
        spec = pl.BlockSpec((tile, cols), lambda l, i: (i, 0))
    return pl.pallas_call(
        body, name=name, grid=(layers, steps),
        in_specs=[pl.BlockSpec((n, tile, cols), lambda l, i: (0, l * steps + i, 0)), spec, spec, spec], out_specs=[spec] * 4,
        out_shape=[jax.ShapeDtypeStruct(w.shape, F32)] * 4, compiler_params=_params(("parallel", "parallel")),
    )(parts, w, m, v)


def kernel(x, mem, mem_norm_w, w_mem_kv, mem_k_norm_w, norm1_w, dn_w_in, dn_conv_w, dn_a_log, dn_dt_bias, dn_o_norm_w, fox_w_in, fox_f_bias, fox_q_norm_w, fox_k_norm_w, memq_norm_w, w_out, norm2_w, w_mlp1, w_mlp2, loss_target, m_mem_norm_w, m_w_mem_kv, m_mem_k_norm_w, m_norm1_w, m_dn_w_in, m_dn_conv_w, m_dn_a_log, m_dn_dt_bias, m_dn_o_norm_w, m_fox_w_in, m_fox_f_bias, m_fox_q_norm_w, m_fox_k_norm_w, m_memq_norm_w, m_w_out, m_norm2_w, m_w_mlp1, m_w_mlp2, v_mem_norm_w, v_w_mem_kv, v_mem_k_norm_w, v_norm1_w, v_dn_w_in, v_dn_conv_w, v_dn_a_log, v_dn_dt_bias, v_dn_o_norm_w, v_fox_w_in, v_fox_f_bias, v_fox_q_norm_w, v_fox_k_norm_w, v_memq_norm_w, v_w_out, v_norm2_w, v_w_mlp1, v_w_mlp2):
    p = dict(mem_norm_w=mem_norm_w, w_mem_kv=w_mem_kv, mem_k_norm_w=mem_k_norm_w, norm1_w=norm1_w, dn_w_in=dn_w_in,
             dn_conv_w=dn_conv_w, dn_a_log=dn_a_log, dn_dt_bias=dn_dt_bias, dn_o_norm_w=dn_o_norm_w, fox_w_in=fox_w_in,
             fox_f_bias=fox_f_bias, fox_q_norm_w=fox_q_norm_w, fox_k_norm_w=fox_k_norm_w, memq_norm_w=memq_norm_w,
             w_out=w_out, norm2_w=norm2_w, w_mlp1=w_mlp1, w_mlp2=w_mlp2)
    pm = dict(mem_norm_w=m_mem_norm_w, w_mem_kv=m_w_mem_kv, mem_k_norm_w=m_mem_k_norm_w, norm1_w=m_norm1_w,
              dn_w_in=m_dn_w_in, dn_conv_w=m_dn_conv_w, dn_a_log=m_dn_a_log, dn_dt_bias=m_dn_dt_bias,
              dn_o_norm_w=m_dn_o_norm_w, fox_w_in=m_fox_w_in, fox_f_bias=m_fox_f_bias, fox_q_norm_w=m_fox_q_norm_w,
              fox_k_norm_w=m_fox_k_norm_w, memq_norm_w=m_memq_norm_w, w_out=m_w_out, norm2_w=m_norm2_w, w_mlp1=m_w_mlp1,
              w_mlp2=m_w_mlp2)
    pv = dict(mem_norm_w=v_mem_norm_w, w_mem_kv=v_w_mem_kv, mem_k_norm_w=v_mem_k_norm_w, norm1_w=v_norm1_w,
              dn_w_in=v_dn_w_in, dn_conv_w=v_dn_conv_w, dn_a_log=v_dn_a_log, dn_dt_bias=v_dn_dt_bias,
              dn_o_norm_w=v_dn_o_norm_w, fox_w_in=v_fox_w_in, fox_f_bias=v_fox_f_bias, fox_q_norm_w=v_fox_q_norm_w,
              fox_k_norm_w=v_fox_k_norm_w, memq_norm_w=v_memq_norm_w, w_out=v_w_out, norm2_w=v_norm2_w, w_mlp1=v_w_mlp1,
              w_mlp2=v_w_mlp2)

    loss, grad_x, results, small = local_step(x[0], mem[0], loss_target[0], p, pm, pv)
    small = [unpack_small(o) for o in small]
    groups = [{**small[i], **{n: r[i] for n, r in results.items()}} for i in range(4)]
    return (loss, grad_x[None], *[grp[n] for grp in groups for n in WEIGHTS])
```

```python
import functools
import math

import jax
import jax.numpy as jnp
from jax import lax
from jax.experimental import pallas as pl
from jax.experimental.pallas import tpu as pltpu

F32 = jnp.float32
BF16 = jnp.bfloat16
HIGHEST = lax.Precision.HIGHEST

D_MODEL = 1024
HEAD_DIM = 128
N_HEADS = 8
MEM_HEADS = 4
MEM_WIDTH = MEM_HEADS * HEAD_DIM
D_FF = 4 * D_MODEL
CONV_WIDTH = 4
CHUNK = 64
Q_BLOCK = 128
EPS = 1e-6
SCALE = HEAD_DIM ** -0.5
MAIN_WIDTH = 4 * D_MODEL + MEM_WIDTH
LANES = 128
N_DEV = 8

ADAM_LR = 0.001
ADAM_B1 = 0.9
ADAM_B2 = 0.999
ADAM_EPS = 1e-08
ADAM_WD = 0.01
ADAM_STEP = 10

VMEM_LIMIT = 56 * 2 ** 20
MESH = pl.DeviceIdType.MESH


def _bdot(a, b, dims):
    return lax.dot_general(a.astype(BF16), b.astype(BF16), (dims, ((), ())), preferred_element_type=F32)


@jax.custom_vjp
def mm(a, b):
    return _bdot(a, b, ((1,), (0,)))


@jax.custom_vjp
def mm_nt(a, b):
    return _bdot(a, b, ((1,), (1,)))


@jax.custom_vjp
def mm_tn(a, b):
    return _bdot(a, b, ((0,), (0,)))


mm.defvjp(lambda a, b: (mm(a, b), (a, b)), lambda r, g: (mm_nt(g, r[1]), mm_tn(r[0], g)))
mm_nt.defvjp(lambda a, b: (mm_nt(a, b), (a, b)), lambda r, g: (mm(g, r[1]), mm_tn(g, r[0])))
mm_tn.defvjp(lambda a, b: (mm_tn(a, b), (a, b)), lambda r, g: (mm_nt(r[1], g), mm(r[0], g)))


def hdot(a, b):
    return jnp.dot(a, b, precision=HIGHEST, preferred_element_type=F32)


def rms(x, w):
    return x * lax.rsqrt(jnp.mean(x * x, axis=-1, keepdims=True) + EPS) * w


def l2n(x):
    return x * lax.rsqrt(jnp.sum(x * x, axis=-1, keepdims=True) + EPS)


def _iota2(n, m):
    return lax.broadcasted_iota(jnp.int32, (n, m), 0), lax.broadcasted_iota(jnp.int32, (n, m), 1)


def _lower_ones(n):
    r, c = _iota2(n, n)
    return jnp.where(r >= c, 1.0, 0.0).astype(F32)


def _last_row(x):
    r = lax.broadcasted_iota(jnp.int32, x.shape, 0)
    return jnp.sum(jnp.where(r == x.shape[0] - 1, x, 0.0), axis=0, keepdims=True)


def _softmax_rows(z):
    m = lax.stop_gradient(jnp.max(z, axis=-1, keepdims=True))
    e = jnp.exp(z - m)
    return e * (1.0 / jnp.sum(e, axis=-1, keepdims=True))


_BNN = (((2,), (1,)), ((0,), (0,)))
_BNT = (((2,), (2,)), ((0,), (0,)))
_BTN = (((1,), (1,)), ((0,), (0,)))


def _bbdot(a, b, dims):
    return lax.dot_general(a.astype(BF16), b.astype(BF16), dims, preferred_element_type=F32)


@jax.custom_vjp
def bmm(a, b):
    return _bbdot(a, b, _BNN)


@jax.custom_vjp
def bmm_nt(a, b):
    return _bbdot(a, b, _BNT)


@jax.custom_vjp
def bmm_tn(a, b):
    return _bbdot(a, b, _BTN)


@jax.custom_vjp
def bmm_high(a, b):
    return lax.dot_general(a, b, _BNN, precision=lax.Precision.HIGH, preferred_element_type=F32)


bmm.defvjp(lambda a, b: (bmm(a, b), (a, b)), lambda r, g: (bmm_nt(g, r[1]), bmm_tn(r[0], g)))
bmm_nt.defvjp(lambda a, b: (bmm_nt(a, b), (a, b)), lambda r, g: (bmm(g, r[1]), bmm_tn(g, r[0])))
bmm_tn.defvjp(lambda a, b: (bmm_tn(a, b), (a, b)), lambda r, g: (bmm_nt(r[1], g), bmm(r[0], g)))
bmm_high.defvjp(lambda a, b: (bmm_high(a, b), (a, b)), lambda r, g: (bmm_nt(g, r[1]), bmm_tn(r[0], g)))

NEUMANN_HIGH_LEVELS = 2


@jax.custom_vjp
def inv_unit_lower(a):
    n = a.shape[-1]
    r, c = _iota2(n, n)
    p = jnp.where(r == c, 1.0, 0.0).astype(F32) - a
    ak = a
    for level in range(int(math.log2(n)) - 1):
        dot = bmm_high if level < NEUMANN_HIGH_LEVELS else bmm
        ak = dot(ak, ak)
        p = p + dot(p, ak)
    return p


def _inv_unit_lower_fwd(a):
    t = inv_unit_lower(a)
    return t, t


def _inv_unit_lower_bwd(t, g):
    return (-bmm_tn(t, bmm_nt(g, t)),)


inv_unit_lower.defvjp(_inv_unit_lower_fwd, _inv_unit_lower_bwd)


def delta_intra(q, k, v, gc, beta):
    b, c, _ = q.shape
    r, cc = _iota2(c, c)
    causal = r >= cc
    strict = r > cc
    gi = jnp.broadcast_to(gc, (b, c, c))
    gj = jnp.swapaxes(gi, 1, 2)
    decay = jnp.where(causal, jnp.exp(jnp.where(causal, gi - gj, 0.0)), 0.0)
    kb = k * beta
    a = jnp.where(strict, bmm_nt(kb, k) * decay, 0.0)
    t = inv_unit_lower(a)
    u = bmm(t, v * beta)
    w = bmm(t, kb * jnp.exp(gc))
    qk = jnp.where(causal, bmm_nt(q, k) * decay, 0.0)
    return u, w, qk


def delta_step(s, q, k, gc, u, w, qk):
    v_new = u - bmm(w, s)
    out = bmm(q * jnp.exp(gc), s) + bmm(qk, v_new)
    r = lax.broadcasted_iota(jnp.int32, gc.shape, 1)
    g_last = jnp.sum(jnp.where(r == gc.shape[1] - 1, gc, 0.0), axis=1, keepdims=True)
    k_dec = k * jnp.exp(g_last - gc)
    s_new = s * jnp.exp(g_last) + bmm_tn(k_dec, v_new)
    return out, s_new


def fox_probs(q, k, fq, fk, qpos0):
    s = lax.dot_general(q, k, (((1,), (1,)), ((), ())), preferred_element_type=F32)
    r, c = _iota2(s.shape[0], s.shape[1])
    return _softmax_rows(jnp.where(c <= (r + qpos0), s + (fq - fk), -jnp.inf))


def mem_head(qm, wq, mk, mv):
    p = _softmax_rows(mm_nt(rms(qm, wq) * SCALE, mk))
    return mm(p, mv)


def _heads(x, n):
    return [x[:, h * HEAD_DIM:(h + 1) * HEAD_DIM] for h in range(n)]


def memkv_fn(mem, mnw, wkv, mknw):
    kv = mm(rms(mem, mnw), wkv)
    mk = jnp.concatenate([rms(kh, mknw) for kh in _heads(kv[:, :MEM_WIDTH], MEM_HEADS)], axis=1)
    return mk, kv[:, MEM_WIDTH:]


def dn_gates_fn(ab, alog, dtb):
    g = -jnp.exp(alog) * jax.nn.softplus(ab + dtb)
    low = _lower_ones(CHUNK)
    gc = jnp.concatenate([hdot(low, g[i * CHUNK:(i + 1) * CHUNK]) for i in range(ab.shape[0] // CHUNK)], axis=0)
    lane = lax.broadcasted_iota(jnp.int32, ab.shape, 1)
    return jnp.where(lane < N_HEADS, gc, jax.nn.sigmoid(ab))


def fox_fcum_fn(fp, fbias):
    lf = jax.nn.log_sigmoid(fp + fbias)
    low = _lower_ones(LANES)
    carry = jnp.zeros((1, fp.shape[1]), F32)
    outs = []
    for i in range(fp.shape[0] // LANES):
        cs = hdot(low, lf[i * LANES:(i + 1) * LANES]) + carry
        carry = _last_row(cs)
        outs.append(cs)
    return jnp.concatenate(outs, axis=0)


def fox_qk_fn(qraw, kraw, qnw, knw):
    q = jnp.concatenate([rms(x, qnw) * SCALE for x in _heads(qraw, N_HEADS)], axis=1)
    k = jnp.concatenate([rms(x, knw) for x in _heads(kraw, N_HEADS)], axis=1)
    return q, k


def _mem_out(qm, mqw, mk, mv):
    return [mem_head(a, mqw, b, c) for a, b, c in zip(_heads(qm, MEM_HEADS), _heads(mk, MEM_HEADS), _heads(mv, MEM_HEADS))]


def dn_out_fn(o, z, qm, onw, mqw, mk, mv):
    mix = [rms(a, onw) * jax.nn.silu(b) for a, b in zip(o, _heads(z, N_HEADS))]
    return jnp.concatenate(mix + _mem_out(qm, mqw, mk, mv), axis=1)


def fox_out_fn(o, gate, qm, mqw, mk, mv):
    return jnp.concatenate([o * jax.nn.sigmoid(gate)] + _mem_out(qm, mqw, mk, mv), axis=1)


_HBM = pl.BlockSpec(memory_space=pltpu.HBM)


def _place():
    return lax.axis_index("x"), lax.axis_index("y"), lax.axis_index("c")


class Rider:
    def __init__(self, ins, out_shape, scratch, start, finish):
        self.ins, self.out_shape, self.scratch, self.start, self.finish = list(ins), list(out_shape), list(scratch), start, finish
        self.results = None


def gather_rider(xs):
    n = len(xs)

    def plan(x_refs, out_refs, sems):
        send_sems, recv_sems, local_sems = sems
        x, y, c = _place()
        me, sibling = (x, y, c), (x, y, 1 - c)
        chips = [(1 - x, y), (x, 1 - y), (1 - x, 1 - y)]

        def copy(a, k, block, to, src=None):
            px, py, pc = block
            dst = out_refs[a].at[4 * px + 2 * py + pc]
            return pltpu.make_async_remote_copy(
                src_ref=dst if src is None else src, dst_ref=dst,
                send_sem=send_sems.at[a, k], recv_sem=recv_sems.at[a, k], device_id=to, device_id_type=MESH)

        mine = [pltpu.make_async_copy(x_refs[a], out_refs[a].at[4 * x + 2 * y + c], local_sems.at[a]) for a in range(n)]
        first = [copy(a, 0, me, sibling, src=x_refs[a]) for a in range(n)]
        first += [copy(a, 1 + j, me, (*chip, c), src=x_refs[a]) for j, chip in enumerate(chips) for a in range(n)]
        return copy, me, sibling, chips, mine, first

    def start(x_refs, out_refs, sems):
        _, _, _, _, mine, first = plan(x_refs, out_refs, sems)
        for cp in mine + first:
            cp.start()

    def finish(x_refs, out_refs, sems):
        copy, me, sibling, chips, mine, first = plan(x_refs, out_refs, sems)
        _, _, c = me
        passed = []
        for j, chip in enumerate(chips):
            for a in range(n):
                copy(a, 1 + j, (*chip, c), me).wait_recv()
                passed.append(copy(a, 4 + j, (*chip, c), sibling))
                passed[-1].start()
        for a in range(n):
            copy(a, 0, sibling, me).wait_recv()
        for j, chip in enumerate(chips):
            for a in range(n):
                copy(a, 4 + j, (*chip, 1 - c), me).wait_recv()
        for cp in first + passed:
            cp.wait_send()
        for cp in mine:
            cp.wait()

    return Rider(xs, [jax.ShapeDtypeStruct((N_DEV,) + a.shape, a.dtype) for a in xs],
                 [pltpu.SemaphoreType.DMA((n, 7)), pltpu.SemaphoreType.DMA((n, 7)), pltpu.SemaphoreType.DMA((n,))], start, finish)


def sibling_rider(gs):
    n = len(gs)

    def plan(g_refs, out_refs, sems):
        send_sems, recv_sems = sems
        x, y, c = _place()
        return [pltpu.make_async_remote_copy(
            src_ref=g_refs[a].at[2 * k + 1 - c], dst_ref=out_refs[a].at[k], send_sem=send_sems.at[a, k],
            recv_sem=recv_sems.at[a, k], device_id=(x, y, 1 - c), device_id_type=MESH) for a in range(n) for k in range(4)]

    def start(g_refs, out_refs, sems):
        for cp in plan(g_refs, out_refs, sems):
            cp.start()

    def finish(g_refs, out_refs, sems):
        copies = plan(g_refs, out_refs, sems)
        for cp in copies:
            cp.wait_recv()
        for cp in copies:
            cp.wait_send()

    return Rider(gs, [jax.ShapeDtypeStruct((4,) + g.shape[1:], g.dtype) for g in gs],
                 [pltpu.SemaphoreType.DMA((n, 4)), pltpu.SemaphoreType.DMA((n, 4))], start, finish)


def chips_rider(hs):
    n = len(hs)

    def plan(h_refs, out_refs, sems):
        send_sems, recv_sems, local_sems = sems
        x, y, c = _place()
        mine = 2 * x + y
        chips = [(1 - x, y), (x, 1 - y), (1 - x, 1 - y)]
        keep = [pltpu.make_async_copy(h_refs[a].at[mine], out_refs[a].at[mine], local_sems.at[a]) for a in range(n)]
        sends = [pltpu.make_async_remote_copy(
            src_ref=h_refs[a].at[2 * qx + qy], dst_ref=out_refs[a].at[mine], send_sem=send_sems.at[a, j],
            recv_sem=recv_sems.at[a, j], device_id=(qx, qy, c), device_id_type=MESH)
            for j, (qx, qy) in enumerate(chips) for a in range(n)]
        recvs = [pltpu.make_async_remote_copy(
            src_ref=h_refs[a].at[mine], dst_ref=out_refs[a].at[2 * qx + qy], send_sem=send_sems.at[a, j],
            recv_sem=recv_sems.at[a, j], device_id=(qx, qy, c), device_id_type=MESH)
            for j, (qx, qy) in enumerate(chips) for a in range(n)]
        return keep, sends, recvs

    def start(h_refs, out_refs, sems):
        keep, sends, _ = plan(h_refs, out_refs, sems)
        for cp in keep + sends:
            cp.start()

    def finish(h_refs, out_refs, sems):
        keep, sends, recvs = plan(h_refs, out_refs, sems)
        for cp in recvs:
            cp.wait_recv()
        for cp in sends:
            cp.wait_send()
        for cp in keep:
            cp.wait()

    return Rider(hs, [jax.ShapeDtypeStruct(h.shape, h.dtype) for h in hs],
                 [pltpu.SemaphoreType.DMA((n, 3)), pltpu.SemaphoreType.DMA((n, 3)), pltpu.SemaphoreType.DMA((n,))], start, finish)


def hosted_call(riders, body, *, out_shape, in_specs, out_specs, grid=(), scratch_shapes=(), **kw):
    riders = tuple(riders or ())
    if not riders:
        return pl.pallas_call(body, out_shape=out_shape, in_specs=in_specs, out_specs=out_specs, grid=grid,
                              scratch_shapes=scratch_shapes, **kw)
    single = not isinstance(out_shape, (list, tuple))
    k_out_shape = [out_shape] if single else list(out_shape)
    k_out_specs = [out_specs] if single else list(out_specs)
    n_in, n_out, n_scr = len(in_specs), len(k_out_shape), len(scratch_shapes)
    r_ins = [a for r in riders for a in r.ins]
    r_outs = [s for r in riders for s in r.out_shape]
    r_scr = [s for r in riders for s in r.scratch]

    def full_body(*refs):
        ins = refs[:n_in + len(r_ins)]
        outs = refs[n_in + len(r_ins):n_in + len(r_ins) + n_out + len(r_outs)]
        scr = refs[n_in + len(r_ins) + n_out + len(r_outs):]
        ids = [pl.program_id(d) for d in range(len(grid))]
        first = functools.reduce(jnp.logical_and, [i == 0 for i in ids]) if ids else None
        last = functools.reduce(jnp.logical_and, [i == g - 1 for i, g in zip(ids, grid)]) if ids else None

        def each(method):
            i0, o0, s0 = n_in, n_out, n_scr
            for r in riders:
                getattr(r, method)(ins[i0:i0 + len(r.ins)], outs[o0:o0 + len(r.out_shape)], scr[s0:s0 + len(r.scratch)])
                i0, o0, s0 = i0 + len(r.ins), o0 + len(r.out_shape), s0 + len(r.scratch)

        if first is None:
            each("start")
        else:
            pl.when(first)(lambda: each("start"))
        body(*ins[:n_in], *outs[:n_out], *scr[:n_scr])
        if last is None:
            each("finish")
        else:
            pl.when(last)(lambda: each("finish"))

    call = pl.pallas_call(
        full_body, out_shape=k_out_shape + r_outs, in_specs=list(in_specs) + [_HBM] * len(r_ins),
        out_specs=k_out_specs + [_HBM] * len(r_outs), grid=grid, scratch_shapes=list(scratch_shapes) + r_scr, **kw)

    def run(*args):
        res = call(*args, *r_ins)
        o0 = n_out
        for r in riders:
            r.results = list(res[o0:o0 + len(r.out_shape)])
            o0 += len(r.out_shape)
        return res[0] if single else list(res[:n_out])

    return run


def run_riders(riders, *, name):
    hosted_call(riders, lambda: None, name=name, out_shape=[], in_specs=[], out_specs=[])()
    return [r.results for r in riders]


def _pick(n, cands):
    for c in cands:
        if n % c == 0:
            return c
    return n


def _params(sem):
    return pltpu.CompilerParams(dimension_semantics=sem, vmem_limit_bytes=VMEM_LIMIT)


MATMUL_VMEM_BUDGET = 40 * 2 ** 20


def _matmul_tiles(m, n, k, bytes_a, bytes_b, bytes_mn, fixed):
    fm, fn, fk = fixed if fixed is not None else (None, None, None)

    def options(given, size, cands):
        return [given] if given else ([c for c in cands if size % c == 0] or [size])

    best = None
    for tm in options(fm, m, (2048, 1024, 512, 256, 128)):
        for tn in options(fn, n, (512, 256, 128)):
            for tk in options(fk, k, (2048, 1536, 1024, 512, 256, 128)):
                if 2 * (tm * tk * bytes_a + tk * tn * bytes_b + tm * tn * bytes_mn) + tm * tn * 4 > MATMUL_VMEM_BUDGET:
                    continue
                key = ((m // tm) * (n // tn) * (k // tk), -tk)
                if best is None or key < best[0]:
                    best = (key, (tm, tn, tk))
    assert best is not None, (m, n, k, fixed)
    return best[1]


def matmul(a, b, *, name, ta=False, tb=False, post=None, post_ins=(), row_ins=(), acc=False, extra_out=None,
           out_dtype=F32, tiles=None, b_view=None, out_view=None, riders=()):
    (k, m) = a.shape if ta else a.shape[::-1]
    (kb, n) = b_view[:2] if b_view is not None else (b.shape[::-1] if tb else b.shape)
    assert k == kb, (a.shape, b.shape, ta, tb)
    bytes_mn = sum(p.dtype.itemsize for p in post_ins) + jnp.dtype(out_dtype).itemsize
    bytes_mn += jnp.dtype(extra_out[1]).itemsize if extra_out else 0
    tm, tn, tk = _matmul_tiles(m, n, k, a.dtype.itemsize, b.dtype.itemsize, bytes_mn, tiles)
    assert not acc or tn == n, (name, tn, n)
    nk = k // tk
    dims = ((0,) if ta else (1,), (1,) if tb else (0,))
    n_post, n_row = len(post_ins), len(row_ins)
    n_out = 1 + bool(extra_out) + bool(acc)

    def body(*refs):
        a_ref, b_ref = refs[:2]
        post_refs = refs[2:2 + n_post + n_row]
        o_refs, acc_ref = refs[-1 - n_out:-1], refs[-1]
        first_rows, kk = pl.program_id(0) == 0, pl.program_id(2)

        @pl.when(kk == 0)
        def _():
            acc_ref[...] = jnp.zeros_like(acc_ref)

        b_tile = b_ref[...]
        acc_ref[...] += _bdot(a_ref[...], b_tile.reshape(-1, b_tile.shape[-1]), dims)

        @pl.when(kk == nk - 1)
        def _():
            r = acc_ref[...]
            rows = [p[...] for p in post_refs[n_post:]]
            if post is not None:
                r = post(r, *[p[...] for p in post_refs[:n_post]], *rows)
            if acc:
                r, s = r
                sum_ref = o_refs[-1]

                @pl.when(first_rows)
                def _():
                    sum_ref[...] = s

                @pl.when(jnp.logical_not(first_rows))
                def _():
                    sum_ref[...] += s

            o_refs[0][...] = r.astype(out_dtype)
            if extra_out:
                o_refs[1][...] = extra_out[0](r, *rows).astype(extra_out[1])

    a_spec = pl.BlockSpec((tk, tm), lambda i, j, kk: (kk, i)) if ta else pl.BlockSpec((tm, tk), lambda i, j, kk: (i, kk))
    if b_view is not None:
        b_spec = b_view[2]
    else:
        b_spec = pl.BlockSpec((tn, tk), lambda i, j, kk: (j, kk)) if tb else pl.BlockSpec((tk, tn), lambda i, j, kk: (kk, j))
    mn_spec = pl.BlockSpec((tm, tn), lambda i, j, kk: (i, j))
    row_spec = pl.BlockSpec((1, tn), lambda i, j, kk: (0, j))
    o_shape, o_spec = ((m, n), mn_spec) if out_view is None else out_view
    out_shape = [jax.ShapeDtypeStruct(o_shape, out_dtype)]
    out_specs = [o_spec]
    if extra_out:
        out_shape.append(jax.ShapeDtypeStruct((m, n), extra_out[1]))
        out_specs.append(mn_spec)
    if acc:
        out_shape.append(jax.ShapeDtypeStruct((1, n), F32))
        out_specs.append(row_spec)
    res = hosted_call(
        riders, body, name=name, grid=(m // tm, n // tn, nk),
        in_specs=[a_spec, b_spec] + [mn_spec] * n_post + [row_spec] * n_row, out_specs=out_specs, out_shape=out_shape,
        scratch_shapes=[pltpu.VMEM((tm, tn), F32)],
        compiler_params=_params(("arbitrary" if acc else "parallel", "parallel", "arbitrary")),
    )(a, b, *post_ins, *row_ins)
    return res if n_out > 1 else res[0]


def rows_call(fn, row_ins, full_ins, row_outs, acc_outs, *, tm, name, riders=()):
    row_ins = [r if isinstance(r, tuple) else (r, r.shape[-1], 0) for r in row_ins]
    t = row_ins[0][0].shape[-2]
    tm = min(tm, t)
    n_in = len(row_ins) + len(full_ins)
    n_row = len(row_outs)

    def body(*refs):
        res = fn(*[[r[h] for h in range(r.shape[0])] if (i < len(row_ins) and len(r.shape) == 3) else r[...]
                   for i, r in enumerate(refs[:n_in])])
        res = res if isinstance(res, (tuple, list)) else (res,)
        outs = refs[n_in:]
        for ref, val in zip(outs[:n_row], res[:n_row]):
            if len(ref.shape) == 3:
                for h, vh in enumerate(val):
                    ref[h] = vh.astype(ref.dtype)
            else:
                ref[...] = val.astype(ref.dtype)
        first = pl.program_id(0) == 0
        for ref, val in zip(outs[n_row:], res[n_row:]):
            @pl.when(first)
            def _(ref=ref, val=val):
                ref[...] = val

            @pl.when(jnp.logical_not(first))
            def _(ref=ref, val=val):
                ref[...] += val

    def full_spec(shape):
        return pl.BlockSpec(shape, lambda i, nd=len(shape): (0,) * nd)

    def row_spec(lead, w, cb):
        if lead is None:
            return pl.BlockSpec((tm, w), lambda i: (i, cb))
        return pl.BlockSpec((lead, tm, w), lambda i: (0, i, cb))

    def lead_cols(c):
        return c if isinstance(c, tuple) else (None, c)

    in_specs = [row_spec(a.shape[0] if a.ndim == 3 else None, w, cb) for (a, w, cb) in row_ins]
    in_specs += [full_spec(f.shape) for f in full_ins]
    out_specs = [row_spec(*lead_cols(c), 0) for c, _ in row_outs] + [full_spec(s) for s in acc_outs]
    out_shape = [jax.ShapeDtypeStruct(tuple(d for d in (lead_cols(c)[0], t, lead_cols(c)[1]) if d is not None), dt)
                 for c, dt in row_outs] + [jax.ShapeDtypeStruct(s, F32) for s in acc_outs]
    res = hosted_call(
        riders, body, name=name, grid=(t // tm,), in_specs=in_specs, out_specs=out_specs, out_shape=out_shape,
        compiler_params=_params(("arbitrary",)),
    )(*[r[0] for r in row_ins], *full_ins)
    return res


def vjp_rows(fn, n_diff_row, row_diff_full):
    def bwd(*args, n_row, n_ct):
        prim_rows = args[:n_row]
        cts = args[n_row:n_row + n_ct]
        fulls = args[n_row + n_ct:]
        _, vjp = jax.vjp(fn, *prim_rows, *fulls)
        g = vjp(cts[0] if n_ct == 1 else tuple(cts))
        out = list(g[:n_diff_row])
        out += [gf for gf, d in zip(g[n_row:], row_diff_full) if d]
        return tuple(out)
    return bwd


def _shift_down(x, s):
    if s == 0:
        return x
    t = lax.broadcasted_iota(jnp.int32, x.shape, 0)
    return jnp.where(t >= s, pltpu.roll(x, s, 0), 0.0)


def _shift_up(x, s):
    if s == 0:
        return x
    n = x.shape[0]
    t = lax.broadcasted_iota(jnp.int32, x.shape, 0)
    return jnp.where(t < n - s, pltpu.roll(x, n - s, 0), 0.0)


def _conv(x, w_ref):
    return sum(w_ref[pl.ds(j, 1), :] * _shift_down(x, CONV_WIDTH - 1 - j) for j in range(CONV_WIDTH))


_DN_POST = (lambda c: l2n(jax.nn.silu(c)) * SCALE, lambda c: l2n(jax.nn.silu(c)), jax.nn.silu)


def dn_prep_fwd(proj, conv_w, *, name, riders=()):
    t = proj.shape[0]

    def body(xq, xk, xv, wq, wk, wv, oq, ok, ov):
        for x_ref, w_ref, o_ref, post in zip((xq, xk, xv), (wq, wk, wv), (oq, ok, ov), _DN_POST):
            o_ref[...] = post(_conv(x_ref[...], w_ref))

    x_specs = [pl.BlockSpec((t, HEAD_DIM), lambda h, g=g: (0, g * N_HEADS + h)) for g in range(3)]
    w_specs = [pl.BlockSpec((CONV_WIDTH, HEAD_DIM), lambda h, g=g: (0, g * N_HEADS + h)) for g in range(3)]
    o_spec = pl.BlockSpec((None, t, HEAD_DIM), lambda h: (h, 0, 0))
    return hosted_call(
        riders, body, name=name, grid=(N_HEADS,), in_specs=x_specs + w_specs, out_specs=[o_spec] * 3,
        out_shape=[jax.ShapeDtypeStruct((N_HEADS, t, HEAD_DIM), F32)] * 3, compiler_params=_params(("parallel",)),
    )(proj, proj, proj, conv_w, conv_w, conv_w)


def dn_prep_bwd(proj, conv_w, dq, dk, dv, *, name, riders=()):
    t = proj.shape[0]

    def body(xq, xk, xv, wq, wk, wv, gq, gk, gv, dxq, dxk, dxv, dwq, dwk, dwv):
        for x_ref, w_ref, g_ref, dx_ref, dw_ref, post in zip(
                (xq, xk, xv), (wq, wk, wv), (gq, gk, gv), (dxq, dxk, dxv), (dwq, dwk, dwv), _DN_POST):
            x = x_ref[...]
            _, vjp = jax.vjp(post, _conv(x, w_ref))
            dc, = vjp(g_ref[...])
            dx = sum(w_ref[pl.ds(j, 1), :] * _shift_up(dc, CONV_WIDTH - 1 - j) for j in range(CONV_WIDTH))
            dx_ref[...] = dx.astype(dx_ref.dtype)
            for j in range(CONV_WIDTH):
                dw_ref[pl.ds(j, 1), :] = jnp.sum(dc * _shift_down(x, CONV_WIDTH - 1 - j), axis=0, keepdims=True)

    x_specs = [pl.BlockSpec((t, HEAD_DIM), lambda h, g=g: (0, g * N_HEADS + h)) for g in range(3)]
    w_specs = [pl.BlockSpec((CONV_WIDTH, HEAD_DIM), lambda h, g=g: (0, g * N_HEADS + h)) for g in range(3)]
    g_spec = pl.BlockSpec((None, t, HEAD_DIM), lambda h: (h, 0, 0))
    dx_spec = pl.BlockSpec((t, HEAD_DIM), lambda h: (0, h))
    dw_spec = pl.BlockSpec((CONV_WIDTH, HEAD_DIM), lambda h: (0, h))
    return hosted_call(
        riders, body, name=name, grid=(N_HEADS,), in_specs=x_specs + w_specs + [g_spec] * 3, out_specs=[dx_spec] * 3 + [dw_spec] * 3,
        out_shape=[jax.ShapeDtypeStruct((t, D_MODEL), BF16)] * 3 + [jax.ShapeDtypeStruct((CONV_WIDTH, D_MODEL), F32)] * 3,
        compiler_params=_params(("parallel",)),
    )(proj, proj, proj, conv_w, conv_w, conv_w, dq, dk, dv)


INTRA_CHUNKS = 4


def _lane_column(x, lane_index):
    lane = lax.broadcasted_iota(jnp.int32, x.shape, 1)
    return jnp.sum(jnp.where(lane == lane_index, x, 0.0), axis=1, keepdims=True)


def _head_columns(g, first_lane):
    return jnp.concatenate([_lane_column(g, first_lane + h)[None] for h in range(N_HEADS)], axis=0)


def _intra_of_gates(q, k, v, gates):
    nb = N_HEADS * (gates.shape[0] // CHUNK)

    def chunks(x):
        return x.reshape(nb, CHUNK, x.shape[-1])

    res = delta_intra(chunks(q), chunks(k), chunks(v), chunks(_head_columns(gates, 0)), chunks(_head_columns(gates, N_HEADS)))
    return tuple(x.reshape(N_HEADS, -1, x.shape[-1]) for x in res)


def _step_of_gates(s, q, k, gates, u, w, qk):
    return delta_step(s, q, k, _head_columns(gates, 0), u, w, qk)


def _head_major(rows, w, index):
    return pl.BlockSpec((N_HEADS, rows, w), lambda i: (0, index(i), 0))


def delta_intra_fwd(q, k, v, gates, *, name, riders=()):
    t = q.shape[1]
    rows = min(INTRA_CHUNKS, t // CHUNK) * CHUNK

    def body(q_ref, k_ref, v_ref, g_ref, u_ref, w_ref, qk_ref):
        for ref, val in zip((u_ref, w_ref, qk_ref), _intra_of_gates(q_ref[...], k_ref[...], v_ref[...], g_ref[...])):
            ref[...] = val

    x_spec, qk_spec = (_head_major(rows, w, lambda i: i) for w in (HEAD_DIM, CHUNK))
    g_spec = pl.BlockSpec((rows, LANES), lambda i: (i, 0))
    return hosted_call(
        riders, body, name=name, grid=(t // rows,), in_specs=[x_spec] * 3 + [g_spec], out_specs=[x_spec, x_spec, qk_spec],
        out_shape=[jax.ShapeDtypeStruct((N_HEADS, t, HEAD_DIM), F32)] * 2 + [jax.ShapeDtypeStruct((N_HEADS, t, CHUNK), F32)],
        compiler_params=_params(("parallel",)),
    )(q, k, v, gates)


def delta_seq_fwd(q, k, gates, u, w, qk, *, name, riders=()):
    t = q.shape[1]
    nc = t // CHUNK

    def body(q_ref, k_ref, g_ref, u_ref, w_ref, qk_ref, o_ref, s0_ref, s_ref):
        @pl.when(pl.program_id(0) == 0)
        def _():
            s_ref[...] = jnp.zeros_like(s_ref)

        s = s_ref[...]
        s0_ref[...] = s
        o, s_new = _step_of_gates(s, q_ref[...], k_ref[...], g_ref[...], u_ref[...], w_ref[...], qk_ref[...])
        o_ref[...] = o
        s_ref[...] = s_new

    x_spec, qk_spec = (_head_major(CHUNK, w, lambda c: c) for w in (HEAD_DIM, CHUNK))
    g_spec = pl.BlockSpec((CHUNK, LANES), lambda c: (c, 0))
    s_spec = pl.BlockSpec((N_HEADS, None, HEAD_DIM, HEAD_DIM), lambda c: (0, c, 0, 0))
    return hosted_call(
        riders, body, name=name, grid=(nc,), in_specs=[x_spec, x_spec, g_spec, x_spec, x_spec, qk_spec], out_specs=[x_spec, s_spec],
        out_shape=[jax.ShapeDtypeStruct((N_HEADS, t, HEAD_DIM), F32),
                   jax.ShapeDtypeStruct((N_HEADS, nc, HEAD_DIM, HEAD_DIM), F32)],
        scratch_shapes=[pltpu.VMEM((N_HEADS, HEAD_DIM, HEAD_DIM), F32)],
        compiler_params=_params(("arbitrary",)),
    )(q, k, gates, u, w, qk)


def delta_seq_bwd(q, k, gates, u, w, qk, s0, do, *, name, riders=()):
    t = q.shape[1]
    nc = t // CHUNK

    def body(q_ref, k_ref, g_ref, u_ref, w_ref, qk_ref, s0_ref, do_ref,
             dq_ref, dk_ref, dg_ref, du_ref, dw_ref, dqk_ref, ds_ref):
        @pl.when(pl.program_id(0) == 0)
        def _():
            ds_ref[...] = jnp.zeros_like(ds_ref)

        _, vjp = jax.vjp(_step_of_gates, s0_ref[...], q_ref[...], k_ref[...], g_ref[...], u_ref[...], w_ref[...], qk_ref[...])
        ds, dq, dk, dg, du, dw, dqk = vjp((do_ref[...], ds_ref[...]))
        for ref, val in zip((ds_ref, dq_ref, dk_ref, dg_ref, du_ref, dw_ref, dqk_ref), (ds, dq, dk, dg, du, dw, dqk)):
            ref[...] = val

    x_spec, qk_spec = (_head_major(CHUNK, w, lambda c: nc - 1 - c) for w in (HEAD_DIM, CHUNK))
    g_spec = pl.BlockSpec((CHUNK, LANES), lambda c: (nc - 1 - c, 0))
    s_spec = pl.BlockSpec((N_HEADS, None, HEAD_DIM, HEAD_DIM), lambda c: (0, nc - 1 - c, 0, 0))
    head_shape = [jax.ShapeDtypeStruct((N_HEADS, t, w_), F32) for w_ in (HEAD_DIM, HEAD_DIM, HEAD_DIM, HEAD_DIM, CHUNK)]
    return hosted_call(
        riders, body, name=name, grid=(nc,), in_specs=[x_spec, x_spec, g_spec, x_spec, x_spec, qk_spec, s_spec, x_spec],
        out_specs=[x_spec, x_spec, g_spec, x_spec, x_spec, qk_spec],
        out_shape=head_shape[:2] + [jax.ShapeDtypeStruct((t, LANES), F32)] + head_shape[2:],
        scratch_shapes=[pltpu.VMEM((N_HEADS, HEAD_DIM, HEAD_DIM), F32)],
        compiler_params=_params(("arbitrary",)),
    )(q, k, gates, u, w, qk, s0, do)


def delta_intra_bwd(q, k, v, gates, du, dw, dqk, dq_s, dk_s, dg_s, *, name, riders=()):
    t = q.shape[1]
    rows = min(INTRA_CHUNKS, t // CHUNK) * CHUNK

    def body(q_ref, k_ref, v_ref, g_ref, du_ref, dw_ref, dqk_ref, dqs_ref, dks_ref, dgs_ref, dq_ref, dk_ref, dv_ref, dg_ref):
        _, vjp = jax.vjp(_intra_of_gates, q_ref[...], k_ref[...], v_ref[...], g_ref[...])
        dq, dk, dv, dg = vjp((du_ref[...], dw_ref[...], dqk_ref[...]))
        dq_ref[...] = dq + dqs_ref[...]
        dk_ref[...] = dk + dks_ref[...]
        dv_ref[...] = dv
        dg_ref[...] = dg + dgs_ref[...]

    x_spec, qk_spec = (_head_major(rows, w, lambda i: i) for w in (HEAD_DIM, CHUNK))
    g_spec = pl.BlockSpec((rows, LANES), lambda i: (i, 0))
    return hosted_call(
        riders, body, name=name, grid=(t // rows,),
        in_specs=[x_spec] * 3 + [g_spec, x_spec, x_spec, qk_spec, x_spec, x_spec, g_spec],
        out_specs=[x_spec] * 3 + [g_spec],
        out_shape=[jax.ShapeDtypeStruct((N_HEADS, t, HEAD_DIM), F32)] * 3 + [jax.ShapeDtypeStruct((t, LANES), F32)],
        compiler_params=_params(("parallel",)),
    )(q, k, v, gates, du, dw, dqk, dq_s, dk_s, dg_s)


_V_BLOCK = 2 * N_HEADS
FOX_GROUPS = 8


def _fox_groups(t):
    nq = t // Q_BLOCK
    per = max(1, nq // FOX_GROUPS)
    return [(g0, per, (g0 + per) * Q_BLOCK) for g0 in range(0, nq, per)]


def fox_attn_fwd(q, k, proj, fq, fk, *, name, riders=()):
    t = q.shape[0]

    def body(q_ref, k_ref, v_ref, fq_ref, fk_ref, o_ref, kb_ref, vb_ref):
        head = pl.program_id(0)
        kb_ref[...] = k_ref[...].astype(BF16)
        vb_ref[...] = v_ref[...].astype(BF16)
        for g0, per, keys in _fox_groups(t):
            def block(j, carry, g0=g0, keys=keys):
                rows = pl.ds(pl.multiple_of((g0 + j) * Q_BLOCK, Q_BLOCK), Q_BLOCK)
                p = fox_probs(q_ref[rows, :].astype(BF16), kb_ref[0:keys, :], _lane_column(fq_ref[rows, :], head),
                              fk_ref[:, 0:keys], (g0 + j) * Q_BLOCK)
                o_ref[rows, :] = jnp.dot(p.astype(BF16), vb_ref[0:keys, :], preferred_element_type=F32)
                return carry
            lax.fori_loop(0, per, block, 0)

    x_spec = pl.BlockSpec((t, HEAD_DIM), lambda h: (0, h))
    v_spec = pl.BlockSpec((t, HEAD_DIM), lambda h: (0, _V_BLOCK + h))
    fq_spec = pl.BlockSpec((t, LANES), lambda h: (0, 0))
    fk_spec = pl.BlockSpec((None, 1, t), lambda h: (h, 0, 0))
    return hosted_call(
        riders, body, name=name, grid=(N_HEADS,), in_specs=[x_spec, x_spec, v_spec, fq_spec, fk_spec], out_specs=x_spec,
        out_shape=jax.ShapeDtypeStruct((t, D_MODEL), F32), scratch_shapes=[pltpu.VMEM((t, HEAD_DIM), BF16)] * 2,
        compiler_params=_params(("parallel",)),
    )(q, k, proj, fq, fk)


def fox_attn_bwd(q, k, proj, fq, fk, do, *, name, riders=()):
    t = q.shape[0]

    def body(q_ref, k_ref, v_ref, fq_ref, fk_ref, do_ref, dq_ref, dk_ref, dv_out_ref, dfq_ref, dfk_ref, kb_ref, vb_ref, dv_ref):
        head = pl.program_id(0)

        @pl.when(head == 0)
        def _():
            dfq_ref[...] = jnp.zeros_like(dfq_ref)

        kb_ref[...] = k_ref[...].astype(BF16)
        vb_ref[...] = v_ref[...].astype(BF16)
        dk_ref[...] = jnp.zeros_like(dk_ref)
        dv_ref[...] = jnp.zeros_like(dv_ref)
        dfk_ref[...] = jnp.zeros_like(dfk_ref)
        nt = (((1,), (1,)), ((), ()))
        tn = (((0,), (0,)), ((), ()))
        for g0, per, keys in _fox_groups(t):
            def block(j, carry, g0=g0, keys=keys):
                rows = pl.ds(pl.multiple_of((g0 + j) * Q_BLOCK, Q_BLOCK), Q_BLOCK)
                qb, dob = q_ref[rows, :].astype(BF16), do_ref[rows, :].astype(BF16)
                kb, vb = kb_ref[0:keys, :], vb_ref[0:keys, :]
                p = fox_probs(qb, kb, _lane_column(fq_ref[rows, :], head), fk_ref[:, 0:keys], (g0 + j) * Q_BLOCK)
                dp = lax.dot_general(dob, vb, nt, preferred_element_type=F32)
                dz = p * (dp - jnp.sum(dp * p, axis=-1, keepdims=True))
                pb, dzb = p.astype(BF16), dz.astype(BF16)
                dq_ref[rows, :] = jnp.dot(dzb, kb, preferred_element_type=F32)
                lane = lax.broadcasted_iota(jnp.int32, (Q_BLOCK, LANES), 1)
                dfq_ref[rows, :] += jnp.where(lane == head, jnp.sum(dz, axis=-1, keepdims=True), 0.0)
                dk_ref[0:keys, :] += lax.dot_general(dzb, qb, tn, preferred_element_type=F32)
                dv_ref[0:keys, :] += lax.dot_general(pb, dob, tn, preferred_element_type=F32)
                dfk_ref[:, 0:keys] -= jnp.sum(dz, axis=0, keepdims=True)
                return carry
            lax.fori_loop(0, per, block, 0)
        dv_out_ref[...] = dv_ref[...].astype(dv_out_ref.dtype)

    x_spec = pl.BlockSpec((t, HEAD_DIM), lambda h: (0, h))
    v_spec = pl.BlockSpec((t, HEAD_DIM), lambda h: (0, _V_BLOCK + h))
    fq_spec = pl.BlockSpec((t, LANES), lambda h: (0, 0))
    fk_spec = pl.BlockSpec((None, 1, t), lambda h: (h, 0, 0))
    return hosted_call(
        riders, body, name=name, grid=(N_HEADS,), in_specs=[x_spec, x_spec, v_spec, fq_spec, fk_spec, x_spec],
        out_specs=[x_spec, x_spec, x_spec, fq_spec, fk_spec],
        out_shape=[jax.ShapeDtypeStruct((t, D_MODEL), F32)] * 2 + [jax.ShapeDtypeStruct((t, D_MODEL), BF16)]
        + [jax.ShapeDtypeStruct((t, LANES), F32), jax.ShapeDtypeStruct((N_HEADS, 1, t), F32)],
        scratch_shapes=[pltpu.VMEM((t, HEAD_DIM), BF16)] * 2 + [pltpu.VMEM((t, HEAD_DIM), F32)],
        compiler_params=_params(("arbitrary",)),
    )(q, k, proj, fq, fk, do)


def memkv_fwd(mem, mnw, wkv, mknw, *, name):
    n = mem.shape[0]

    def body(mem_ref, mnw_ref, w_ref, mknw_ref, mk_ref, mv_ref):
        mk, mv = memkv_fn(mem_ref[...], mnw_ref[...], w_ref[...], mknw_ref[...])
        mk_ref[...] = mk
        mv_ref[...] = mv

    return pl.pallas_call(
        body, name=name, out_shape=[jax.ShapeDtypeStruct((n, MEM_WIDTH), F32)] * 2,
        compiler_params=pltpu.CompilerParams(vmem_limit_bytes=VMEM_LIMIT),
    )(mem, mnw, wkv, mknw)


def memkv_bwd(mem, mnw, wkv, mknw, dmk, dmv, *, name):
    def body(mem_ref, mnw_ref, w_ref, mknw_ref, dmk_ref, dmv_ref, dmnw_ref, dw_ref, dmknw_ref):
        f = functools.partial(memkv_fn, mem_ref[...])
        _, vjp = jax.vjp(f, mnw_ref[...], w_ref[...].astype(F32), mknw_ref[...])
        dmnw, dw, dmknw = vjp((dmk_ref[...], dmv_ref[...]))
        dmnw_ref[...] = dmnw
        dw_ref[...] = dw.astype(dw_ref.dtype)
        dmknw_ref[...] = dmknw

    return pl.pallas_call(
        body, name=name,
        out_shape=[jax.ShapeDtypeStruct(mnw.shape, F32), jax.ShapeDtypeStruct(wkv.shape, BF16), jax.ShapeDtypeStruct(mknw.shape, F32)],
        compiler_params=pltpu.CompilerParams(vmem_limit_bytes=VMEM_LIMIT),
    )(mem, mnw, wkv, mknw, dmk, dmv)


def _row(v, width=None):
    v = v.reshape(1, -1)
    if width is not None and v.shape[1] < width:
        v = jnp.pad(v, ((0, 0), (0, width - v.shape[1])))
    return v


def _norm_fwd(x, w, name, riders=()):
    return rows_call(lambda x, w: rms(x, w), [x], [w], [(D_MODEL, BF16)], [], tm=512, name=name, riders=riders)[0]


FF_PIECE = D_FF // N_DEV


def _add(r, x, *rows):
    return r + x


def _norm_rows(r, w):
    return rms(r, w)


def _norm_bwd_post(dh, x, dx_in, w):
    _, vjp = jax.vjp(rms, x, w)
    dx, dw = vjp(dh)
    return dx + dx_in, dw


def _piece(rows, cols, index):
    return pl.BlockSpec((None, rows, cols), lambda i, j, kk: (index(i, j, kk), 0, 0))


def _two_pieces(rows, cols, index):
    return pl.BlockSpec((2, rows, cols), lambda i, j, kk: (index(i, j, kk), 0, 0))


def _mlp_fwd(x, h2, w1, w2, layer, riders=(), next_norm_w=None):
    riders = list(riders) + [None, None]
    u, a1 = matmul(h2, w1, name=f"mlp1_fwd_{layer}", tiles=(None, FF_PIECE, D_MODEL),
                   extra_out=(lambda u: jnp.square(jnp.maximum(u, 0.0)), BF16),
                   b_view=(D_MODEL, D_FF, _piece(D_MODEL, FF_PIECE, lambda i, j, kk: j)), riders=riders[0])
    norm = dict(row_ins=[next_norm_w], extra_out=(_norm_rows, BF16)) if next_norm_w is not None else {}
    y = matmul(a1, w2, name=f"mlp2_fwd_{layer}", post=_add, post_ins=[x], tiles=(None, D_MODEL, 2 * FF_PIECE),
               b_view=(D_FF, D_MODEL, _two_pieces(FF_PIECE, D_MODEL, lambda i, j, kk: kk)), riders=riders[1], **norm)
    return y, (x, h2, u, a1)


def pair_sum(g, got, *, name):
    _, rows, cols = g.shape
    tile = _pick(rows, (512, 256, 128))
    c = lax.axis_index("c").astype(jnp.int32).reshape(1)

    def body(c_ref, a_ref, b_ref, o_ref):
        o_ref[...] = (a_ref[...].astype(F32) + b_ref[...].astype(F32)).astype(o_ref.dtype)

    grid_spec = pltpu.PrefetchScalarGridSpec(
        num_scalar_prefetch=1, grid=(4, rows // tile),
        in_specs=[pl.BlockSpec((None, tile, cols), lambda k, i, c_ref: (2 * k + c_ref[0], i, 0)),
                  pl.BlockSpec((None, tile, cols), lambda k, i, c_ref: (k, i, 0))],
        out_specs=pl.BlockSpec((None, tile, cols), lambda k, i, c_ref: (k, i, 0)))
    return pl.pallas_call(
        body, name=name, grid_spec=grid_spec, out_shape=jax.ShapeDtypeStruct((4, rows, cols), g.dtype),
        compiler_params=_params(("parallel", "parallel")),
    )(c, g, got)


def chip_sums(names, pieces, gots):
    return [pair_sum(a, got, name=f"grads_pair_sum_{n}") for n, a, got in zip(names, pieces, gots)]


def _mlp_bwd(dy, res, n2w, w1, w2, layer, riders=()):
    x, h2, u, a1 = res
    du = matmul(dy, w2, tb=True, name=f"mlp2_dx_{layer}", out_dtype=BF16, tiles=(None, 2 * FF_PIECE, D_MODEL),
                post=lambda r, u: r * (2.0 * jnp.maximum(u, 0.0)), post_ins=[u],
                b_view=(D_MODEL, D_FF, _two_pieces(FF_PIECE, D_MODEL, lambda i, j, kk: j)), riders=riders)
    dw2 = matmul(a1, dy, ta=True, name=f"mlp2_dw_{layer}", out_dtype=BF16, tiles=(FF_PIECE, D_MODEL, None), out_view=(
        w2.shape, _piece(FF_PIECE, D_MODEL, lambda i, j, kk: i)))
    sib2 = sibling_rider([dw2])
    dx, dn2w = matmul(du, w1, tb=True, name=f"mlp1_dx_{layer}", tiles=(None, D_MODEL, FF_PIECE),
                      b_view=(D_FF, D_MODEL, _piece(D_MODEL, FF_PIECE, lambda i, j, kk: kk)),
                      post=_norm_bwd_post, post_ins=[x, dy], row_ins=[n2w], acc=True, riders=[sib2])
    dw1 = matmul(h2, du, ta=True, name=f"mlp1_dw_{layer}", out_dtype=BF16, tiles=(D_MODEL, FF_PIECE, None), out_view=(
        w1.shape, _piece(D_MODEL, FF_PIECE, lambda i, j, kk: j)))
    return dx, dw1, dw2, dn2w, sibling_rider([dw1]), sib2


def _in_proj_bwd(h, dmain, dsmall, w_main, w_small, x, dx_in, n1w, tag):
    dh = matmul(dmain, w_main, tb=True, name=f"inproj_dx_main_{tag}")

    def post(r, dh_main, x, dx_in, w):
        return _norm_bwd_post(r + dh_main, x, dx_in, w)

    dx, dn1w = matmul(dsmall, w_small, tb=True, name=f"inproj_dx_small_{tag}", tiles=(None, D_MODEL, None),
                      post=post, post_ins=[dh, x, dx_in], row_ins=[n1w], acc=True)
    dw_main = matmul(h, dmain, ta=True, out_dtype=BF16, name=f"inproj_dw_main_{tag}")
    dw_small = matmul(h, dsmall, ta=True, out_dtype=BF16, name=f"inproj_dw_small_{tag}")
    return dx, dn1w, dw_main, dw_small


def local_step(x, mem, target, w, m, v):
    t = x.shape[0]
    n_mem = mem.shape[0]
    g = {}

    def wire(a):
        return a.astype(BF16)

    (dn_g,), = run_riders([gather_rider([wire(w["dn_w_in"][0])])], name="weights_gather_first")
    dn_main, dn_ab = in_proj_weights(dn_g, DN_IN, 2 * N_HEADS)
    fox_w = wire(w["fox_w_in"][0])
    ride_out = gather_rider([wire(w["w_out"][0]), wire(w["w_out"][1]), w["dn_conv_w"][0]])
    ride_kv = gather_rider([wire(w["w_mem_kv"])])
    ride_mlp1_0 = gather_rider([wire(w["w_mlp1"][0])])
    ride_mlp2_0 = gather_rider([wire(w["w_mlp2"][0])])
    ride_fox_a, ride_fox_b = gather_rider([fox_w[:D_MODEL // 2]]), gather_rider([fox_w[D_MODEL // 2:]])
    ride_mlp_1 = gather_rider([wire(w["w_mlp1"][1]), wire(w["w_mlp2"][1])])
    mnw, mknw = _row(w["mem_norm_w"]), _row(w["mem_k_norm_w"])

    n1w0, n2w0 = _row(w["norm1_w"][0]), _row(w["norm2_w"][0])
    n1w1, n2w1 = _row(w["norm1_w"][1]), _row(w["norm2_w"][1])
    alog, dtb = _row(w["dn_a_log"][0], LANES), _row(w["dn_dt_bias"][0], LANES)
    onw, mqw0 = _row(w["dn_o_norm_w"][0]), _row(w["memq_norm_w"][0])
    x0 = x
    h0 = _norm_fwd(x0, n1w0, "norm1_fwd_0")
    pm0 = matmul(h0, dn_main, name="inproj_main_0", riders=[ride_out])
    w_out0, w_out1 = (a.reshape(OUT_IN, D_MODEL) for a in ride_out.results[:2])
    conv_w = ride_out.results[2].transpose(1, 0, 2).reshape(CONV_WIDTH, 3 * D_MODEL)
    ps0 = matmul(h0, dn_ab, name="inproj_small_0")
    gates = rows_call(dn_gates_fn, [ps0], [alog, dtb], [(LANES, F32)], [], tm=512, name="dn_gates_fwd")[0]
    q0, k0, v0 = dn_prep_fwd(pm0, conv_w, name="dn_prep_fwd", riders=[ride_kv])
    w_kv = ride_kv.results[0].reshape(D_MODEL, D_MODEL)
    mk, mv = memkv_fwd(mem, mnw, w_kv, mknw, name="memkv_fwd")
    u0, w0, qk0 = delta_intra_fwd(q0, k0, v0, gates, name="delta_intra_fwd", riders=[ride_mlp1_0])
    o0, s_start = delta_seq_fwd(q0, k0, gates, u0, w0, qk0, name="delta_seq_fwd", riders=[ride_mlp2_0])
    cat0 = rows_call(dn_out_fn, [o0, (pm0, D_MODEL, 3), (pm0, MEM_WIDTH, 8)], [onw, mqw0, mk, mv],
                     [(D_MODEL + MEM_WIDTH, BF16)], [], tm=256, name="dn_out_fwd")[0]
    (w1_0,), (w2_0,) = ride_mlp1_0.results, ride_mlp2_0.results
    x1, h2_0 = matmul(cat0, w_out0, post=_add, post_ins=[x0], row_ins=[n2w0], extra_out=(_norm_rows, BF16),
                      tiles=(None, D_MODEL, None), name="wout_fwd_0")
    (x2, h1), mlp_res0 = _mlp_fwd(x1, h2_0, w1_0, w2_0, 0, riders=[[ride_fox_a], [ride_fox_b]], next_norm_w=n1w1)
    fox_main, fox_f = in_proj_weights(
        jnp.concatenate([ride_fox_a.results[0], ride_fox_b.results[0]], axis=1), FOX_IN, N_HEADS)

    fbias = _row(w["fox_f_bias"][0], LANES)
    qnw, knw, mqw1 = _row(w["fox_q_norm_w"][0]), _row(w["fox_k_norm_w"][0]), _row(w["memq_norm_w"][1])
    pm1 = matmul(h1, fox_main, name="inproj_main_1")
    ps1 = matmul(h1, fox_f, name="inproj_small_1")
    fq = rows_call(fox_fcum_fn, [ps1], [fbias], [(LANES, F32)], [], tm=t, name="fox_fcum_fwd")[0]
    fk = fq[:, :N_HEADS].T[:, None, :]
    q1, k1 = rows_call(fox_qk_fn, [(pm1, D_MODEL, 0), (pm1, D_MODEL, 1)], [qnw, knw], [(D_MODEL, F32)] * 2, [], tm=256,
                       name="fox_qk_fwd")
    o1 = fox_attn_fwd(q1, k1, pm1, fq, fk, name="fox_attn_fwd", riders=[ride_mlp_1])
    cat1 = rows_call(fox_out_fn, [o1, (pm1, D_MODEL, 3), (pm1, MEM_WIDTH, 8)], [mqw1, mk, mv],
                     [(D_MODEL + MEM_WIDTH, BF16)], [], tm=256, name="fox_out_fwd")[0]
    w1_1, w2_1 = ride_mlp_1.results
    x3, h2_1 = matmul(cat1, w_out1, post=_add, post_ins=[x2], row_ins=[n2w1], extra_out=(_norm_rows, BF16),
                      tiles=(None, D_MODEL, None), name="wout_fwd_1")
    y, mlp_res1 = _mlp_fwd(x3, h2_1, w1_1, w2_1, 1)

    def loss_fn(y, tgt):
        e = y - tgt
        return e * (1.0 / D_MODEL), jnp.sum(jnp.sum(e * e, axis=1, keepdims=True), axis=0, keepdims=True)
    dy, sq = rows_call(loss_fn, [y, target], [], [(D_MODEL, F32)], [(1, 1)], tm=512, name="loss")
    loss = sq[0, 0] * (0.5 / D_MODEL)

    dx3, dw1_1, dw2_1, dn2w1, sib1, sib2 = _mlp_bwd(dy, mlp_res1, n2w1, w1_1, w2_1, 1)
    dcat1 = matmul(dx3, w_out1, tb=True, name="wout_dx_1", riders=[sib1])
    dwo_1 = matmul(cat1, dx3, ta=True, out_dtype=BF16, name="wout_dw_1").reshape(N_DEV, OUT_IN // N_DEV, D_MODEL)
    sibo = sibling_rider([dwo_1])
    do1, dgate1, dqm1, dmqw1, dmk1, dmv1 = rows_call(
        functools.partial(vjp_rows(fox_out_fn, 3, (True, True, True)), n_row=3, n_ct=1),
        [o1, (pm1, D_MODEL, 3), (pm1, MEM_WIDTH, 8), dcat1], [mqw1, mk, mv],
        [(D_MODEL, F32), (D_MODEL, BF16), (MEM_WIDTH, BF16)], [(1, HEAD_DIM), (n_mem, MEM_WIDTH), (n_mem, MEM_WIDTH)],
        tm=256, name="fox_out_bwd", riders=[sibo])
    ride_l1 = chips_rider(chip_sums(["w_mlp2_1", "w_mlp1_1", "w_out_1"], [dw2_1, dw1_1, dwo_1],
                                    sib2.results + sib1.results + sibo.results))
    dq1, dk1, dv1, dfq, dfk = fox_attn_bwd(q1, k1, pm1, fq, fk, do1, name="fox_attn_bwd", riders=[ride_l1])
    dqraw1, dkraw1, dqnw, dknw = rows_call(
        functools.partial(vjp_rows(fox_qk_fn, 2, (True, True)), n_row=2, n_ct=2),
        [(pm1, D_MODEL, 0), (pm1, D_MODEL, 1), dq1, dk1], [qnw, knw],
        [(D_MODEL, BF16)] * 2, [(1, HEAD_DIM)] * 2, tm=256, name="fox_qk_bwd")
    dfcum = dfq + jnp.pad(dfk[:, 0, :].T, ((0, 0), (0, LANES - N_HEADS)))
    dps1, dfbias = rows_call(
        functools.partial(vjp_rows(fox_fcum_fn, 1, (True,)), n_row=1, n_ct=1),
        [ps1, dfcum], [fbias], [(LANES, F32)], [(1, LANES)], tm=t, name="fox_fcum_bwd")
    dpm1 = jnp.concatenate([dqraw1, dkraw1, dv1, dgate1, dqm1], axis=1)
    dx2, dn1w1, dwmain1, dwsmall1 = _in_proj_bwd(h1, dpm1, dps1, fox_main, fox_f, x2, dx3, n1w1, "1")
    g_fox = in_proj_pieces(dwmain1, dwsmall1, N_HEADS, FOX_IN)
    sibf = sibling_rider([g_fox])

    dx1, dw1_0, dw2_0, dn2w0, sib1, sib2 = _mlp_bwd(dx2, mlp_res0, n2w0, w1_0, w2_0, 0, riders=[sibf])
    ride_fox_g = chips_rider(chip_sums(["fox_w_in"], [g_fox], sibf.results))
    dcat0 = matmul(dx1, w_out0, tb=True, name="wout_dx_0", riders=[sib1])
    dwo_0 = matmul(cat0, dx1, ta=True, out_dtype=BF16, name="wout_dw_0").reshape(N_DEV, OUT_IN // N_DEV, D_MODEL)
    sibo = sibling_rider([dwo_0])
    do0, dz0, dqm0, donw, dmqw0, dmk0, dmv0 = rows_call(
        functools.partial(vjp_rows(dn_out_fn, 3, (True, True, True, True)), n_row=3, n_ct=1),
        [o0, (pm0, D_MODEL, 3), (pm0, MEM_WIDTH, 8), dcat0], [onw, mqw0, mk, mv],
        [((N_HEADS, HEAD_DIM), F32), (D_MODEL, BF16), (MEM_WIDTH, BF16)],
        [(1, HEAD_DIM), (1, HEAD_DIM), (n_mem, MEM_WIDTH), (n_mem, MEM_WIDTH)], tm=256, name="dn_out_bwd", riders=[sibo])
    h_l0 = chip_sums(["w_mlp2_0", "w_mlp1_0", "w_out_0"], [dw2_0, dw1_0, dwo_0], sib2.results + sib1.results + sibo.results)
    ride_l0_mlp, ride_l0_out = chips_rider(h_l0[:2]), chips_rider(h_l0[2:])
    dmnw, dwkv, dmknw = memkv_bwd(mem, mnw, w_kv, mknw, dmk0 + dmk1, dmv0 + dmv1, name="memkv_bwd")
    g_kv = dwkv.reshape(N_DEV, D_MODEL // N_DEV, D_MODEL)
    sibk = sibling_rider([g_kv])
    dq_s, dk_s, dg_s, du0, dw0, dqk0 = delta_seq_bwd(q0, k0, gates, u0, w0, qk0, s_start, do0, name="delta_seq_bwd",
                                                     riders=[ride_fox_g, sibk])
    ride_kv_g = chips_rider(chip_sums(["w_mem_kv"], [g_kv], sibk.results))
    dq0, dk0, dv0, dgates = delta_intra_bwd(q0, k0, v0, gates, du0, dw0, dqk0, dq_s, dk_s, dg_s,
                                            name="delta_intra_bwd", riders=[ride_l0_mlp, ride_kv_g])
    dxq, dxk, dxv, dcq, dck, dcv = dn_prep_bwd(pm0, conv_w, dq0, dk0, dv0, name="dn_prep_bwd", riders=[ride_l0_out])
    dconv = jnp.concatenate([dcq, dck, dcv], axis=1)
    dps0, dalog, ddtb = rows_call(
        functools.partial(vjp_rows(dn_gates_fn, 1, (True, True)), n_row=1, n_ct=1),
        [ps0, dgates], [alog, dtb], [(LANES, F32)], [(1, LANES)] * 2, tm=512, name="dn_gates_bwd")
    dpm0 = jnp.concatenate([dxq, dxk, dxv, dz0, dqm0], axis=1)
    grad_x, dn1w0, dwmain0, dwsmall0 = _in_proj_bwd(h0, dpm0, dps0, dn_main, dn_ab, x0, dx1, n1w0, "0")
    g_dn = in_proj_pieces(dwmain0, dwsmall0, 2 * N_HEADS, DN_IN)
    g_conv = dconv.reshape(CONV_WIDTH, N_DEV, -1).transpose(1, 0, 2).astype(BF16)

    g["mem_norm_w"] = dmnw[0]
    g["mem_k_norm_w"] = dmknw[0]
    g["norm1_w"] = jnp.concatenate([dn1w0, dn1w1], axis=0)
    g["dn_a_log"] = dalog[:, :N_HEADS]
    g["dn_dt_bias"] = ddtb[:, :N_HEADS]
    g["dn_o_norm_w"] = donw
    g["fox_f_bias"] = dfbias[:, :N_HEADS]
    g["fox_q_norm_w"] = dqnw
    g["fox_k_norm_w"] = dknw
    g["memq_norm_w"] = jnp.concatenate([dmqw0, dmqw1], axis=0)
    g["norm2_w"] = jnp.concatenate([dn2w0, dn2w1], axis=0)

    sibd = sibling_rider([g_dn, g_conv])
    run_riders([sibd], name="grads_to_sibling_last")
    ride_last = chips_rider(chip_sums(["dn_w_in", "dn_conv_w"], [g_dn, g_conv], sibd.results))
    ride_small = gather_rider([pack_small(g, last=loss)])
    run_riders([ride_last, ride_small], name="grads_to_chips_last")

    def layers(l0, l1):
        return jnp.stack([l0, l1], axis=1).reshape(4, -1, l0.shape[-1])

    parts = {
        "w_mlp1": layers(ride_l0_mlp.results[1], ride_l1.results[1]),
        "w_mlp2": layers(ride_l0_mlp.results[0], ride_l1.results[0]),
        "w_out": layers(ride_l0_out.results[0], ride_l1.results[2]),
        "fox_w_in": ride_fox_g.results[0], "w_mem_kv": ride_kv_g.results[0],
        "dn_w_in": ride_last.results[0], "dn_conv_w": ride_last.results[1],
    }
    out = {n: adamw(parts[n], w[n], m[n], v[n], name=f"adamw_{n}") for n, _, _ in BIG}
    small = adamw(ride_small.results[0], pack_small(w), pack_small(m), pack_small(v), name="adamw_small")
    loss = small[0][-1, -1]
    return loss, grad_x, out, small


WEIGHTS = ["mem_norm_w", "w_mem_kv", "mem_k_norm_w", "norm1_w", "dn_w_in", "dn_conv_w", "dn_a_log", "dn_dt_bias",
           "dn_o_norm_w", "fox_w_in", "fox_f_bias", "fox_q_norm_w", "fox_k_norm_w", "memq_norm_w", "w_out", "norm2_w",
           "w_mlp1", "w_mlp2"]
DN_IN = 4 * D_MODEL + 2 * N_HEADS + MEM_WIDTH
FOX_IN = 4 * D_MODEL + N_HEADS + MEM_WIDTH
GATE_END = 4 * D_MODEL
OUT_IN = D_MODEL + MEM_WIDTH
BIG = [("w_mem_kv", D_MODEL // N_DEV, D_MODEL), ("dn_w_in", D_MODEL, DN_IN // N_DEV), ("fox_w_in", D_MODEL, FOX_IN // N_DEV),
       ("dn_conv_w", CONV_WIDTH, 3 * D_MODEL // N_DEV), ("w_out", 2 * OUT_IN // N_DEV, D_MODEL),
       ("w_mlp1", 2 * D_MODEL, FF_PIECE), ("w_mlp2", 2 * FF_PIECE, D_MODEL)]
SMALL_TILE = 8 * LANES
SMALL = [(name, shape, -(-math.prod(shape) // SMALL_TILE) * SMALL_TILE) for name, shape in [
    ("mem_norm_w", (D_MODEL,)), ("mem_k_norm_w", (HEAD_DIM,)), ("norm1_w", (2, D_MODEL)), ("dn_a_log", (1, N_HEADS)),
    ("dn_dt_bias", (1, N_HEADS)), ("dn_o_norm_w", (1, HEAD_DIM)), ("fox_f_bias", (1, N_HEADS)),
    ("fox_q_norm_w", (1, HEAD_DIM)), ("fox_k_norm_w", (1, HEAD_DIM)), ("memq_norm_w", (2, HEAD_DIM)), ("norm2_w", (2, D_MODEL))]]
SMALL_ROWS = sum(ln for _, _, ln in SMALL) // LANES + 8


def pack_small(p, last=None):
    def rows(a, ln):
        a = a.reshape(-1)
        return (a if a.shape[0] == ln else jnp.pad(a, (0, ln - a.shape[0]))).reshape(-1, LANES)

    used = sum(ln for _, _, ln in SMALL) // LANES
    tail = jnp.zeros(((SMALL_ROWS - used) * LANES,), F32)
    if last is not None:
        tail = jnp.concatenate([tail[:-1], last.reshape(1)])
    return jnp.concatenate([rows(p[n], ln) for n, _, ln in SMALL] + [tail.reshape(-1, LANES)], axis=0)


def unpack_small(pk):
    row, out = 0, {}
    for n, sh, ln in SMALL:
        out[n] = pk[row:row + ln // LANES].reshape(-1)[:math.prod(sh)].reshape(sh)
        row += ln // LANES
    return out


def in_proj_weights(gathered, width, n_small):
    full = gathered.transpose(1, 0, 2).reshape(D_MODEL, width)
    main = jnp.concatenate([full[:, :GATE_END], full[:, GATE_END + n_small:]], axis=1)
    return main, jnp.pad(full[:, GATE_END:GATE_END + n_small], ((0, 0), (0, LANES - n_small)))


def in_proj_pieces(d_main, d_small, n_small, width):
    full = jnp.concatenate([d_main[:, :GATE_END], d_small[:, :n_small], d_main[:, GATE_END:]], axis=1)
    return full.reshape(D_MODEL, N_DEV, width // N_DEV).transpose(1, 0, 2)


def adamw(parts, w, m, v, *, name):
    n, _, cols = parts.shape
    layers = w.shape[0] if w.ndim == 3 else 1
    rows = w.shape[-2]
    tile = _pick(rows, (512, 256, 128))
    steps = rows // tile

    def body(p_ref, w_ref, m_ref, v_ref, g_ref, d_ref, mo_ref, vo_ref):
        g = p_ref[0].astype(F32)
        for i in range(1, n):
            g = g + p_ref[i].astype(F32)
        m_new = ADAM_B1 * m_ref[...] + (1.0 - ADAM_B1) * g
        v_new = ADAM_B2 * v_ref[...] + (1.0 - ADAM_B2) * jnp.square(g)
        m_hat = m_new / (1.0 - ADAM_B1 ** ADAM_STEP)
        v_hat = v_new / (1.0 - ADAM_B2 ** ADAM_STEP)
        g_ref[...] = g
        d_ref[...] = -ADAM_LR * (m_hat / (jnp.sqrt(v_hat) + ADAM_EPS) + ADAM_WD * w_ref[...])
        mo_ref[...] = m_new
        vo_ref[...] = v_new

    if w.ndim == 3:
        spec = pl.BlockSpec((None, tile, cols), lambda l, i: (l, i, 0))
    else:
        spec = pl.BlockSpec((tile, cols), lambda l, i: (i, 0))
    return pl.pallas_call(
        body, name=name, grid=(layers, steps),
        in_specs=[pl.BlockSpec((n, tile, cols), lambda l, i: (0, l * steps + i, 0)), spec, spec, spec], out_specs=[spec] * 4,
        out_shape=[jax.ShapeDtypeStruct(w.shape, F32)] * 4, compiler_params=_params(("parallel", "parallel")),
    )(parts, w, m, v)


def kernel(x, mem, mem_norm_w, w_mem_kv, mem_k_norm_w, norm1_w, dn_w_in, dn_conv_w, dn_a_log, dn_dt_bias, dn_o_norm_w, fox_w_in, fox_f_bias, fox_q_norm_w, fox_k_norm_w, memq_norm_w, w_out, norm2_w, w_mlp1, w_mlp2, loss_target, m_mem_norm_w, m_w_mem_kv, m_mem_k_norm_w, m_norm1_w, m_dn_w_in, m_dn_conv_w, m_dn_a_log, m_dn_dt_bias, m_dn_o_norm_w, m_fox_w_in, m_fox_f_bias, m_fox_q_norm_w, m_fox_k_norm_w, m_memq_norm_w, m_w_out, m_norm2_w, m_w_mlp1, m_w_mlp2, v_mem_norm_w, v_w_mem_kv, v_mem_k_norm_w, v_norm1_w, v_dn_w_in, v_dn_conv_w, v_dn_a_log, v_dn_dt_bias, v_dn_o_norm_w, v_fox_w_in, v_fox_f_bias, v_fox_q_norm_w, v_fox_k_norm_w, v_memq_norm_w, v_w_out, v_norm2_w, v_w_mlp1, v_w_mlp2):
    p = dict(mem_norm_w=mem_norm_w, w_mem_kv=w_mem_kv, mem_k_norm_w=mem_k_norm_w, norm1_w=norm1_w, dn_w_in=dn_w_in,
             dn_conv_w=dn_conv_w, dn_a_log=dn_a_log, dn_dt_bias=dn_dt_bias, dn_o_norm_w=dn_o_norm_w, fox_w_in=fox_w_in,
             fox_f_bias=fox_f_bias, fox_q_norm_w=fox_q_norm_w, fox_k_norm_w=fox_k_norm_w, memq_norm_w=memq_norm_w,
             w_out=w_out, norm2_w=norm2_w, w_mlp1=w_mlp1, w_mlp2=w_mlp2)
    pm = dict(mem_norm_w=m_mem_norm_w, w_mem_kv=m_w_mem_kv, mem_k_norm_w=m_mem_k_norm_w, norm1_w=m_norm1_w,
              dn_w_in=m_dn_w_in, dn_conv_w=m_dn_conv_w, dn_a_log=m_dn_a_log, dn_dt_bias=m_dn_dt_bias,
              dn_o_norm_w=m_dn_o_norm_w, fox_w_in=m_fox_w_in, fox_f_bias=m_fox_f_bias, fox_q_norm_w=m_fox_q_norm_w,
              fox_k_norm_w=m_fox_k_norm_w, memq_norm_w=m_memq_norm_w, w_out=m_w_out, norm2_w=m_norm2_w, w_mlp1=m_w_mlp1,
              w_mlp2=m_w_mlp2)
    pv = dict(mem_norm_w=v_mem_norm_w, w_mem_kv=v_w_mem_kv, mem_k_norm_w=v_mem_k_norm_w, norm1_w=v_norm1_w,
              dn_w_in=v_dn_w_in, dn_conv_w=v_dn_conv_w, dn_a_log=v_dn_a_log, dn_dt_bias=v_dn_dt_bias,
              dn_o_norm_w=v_dn_o_norm_w, fox_w_in=v_fox_w_in, fox_f_bias=v_fox_f_bias, fox_q_norm_w=v_fox_q_norm_w,
              fox_k_norm_w=v_fox_k_norm_w, memq_norm_w=v_memq_norm_w, w_out=v_w_out, norm2_w=v_norm2_w, w_mlp1=v_w_mlp1,
              w_mlp2=v_w_mlp2)

    loss, grad_x, results, small = local_step(x[0], mem[0], loss_target[0], p, pm, pv)
    small = [unpack_small(o) for o in small]
    groups = [{**small[i], **{n: r[i] for n, r in results.items()}} for i in range(4)]
    return (loss, grad_x[None], *[grp[n] for grp in groups for n in WEIGHTS])
```

```python
import functools
import math

import jax
import jax.numpy as jnp
from jax import lax
from jax.experimental import pallas as pl
from jax.experimental.pallas import tpu as pltpu

F32 = jnp.float32
BF16 = jnp.bfloat16
HIGHEST = lax.Precision.HIGHEST

D_MODEL = 1024
HEAD_DIM = 128
N_HEADS = 8
MEM_HEADS = 4
MEM_WIDTH = MEM_HEADS * HEAD_DIM
D_FF = 4 * D_MODEL
CONV_WIDTH = 4
CHUNK = 64
Q_BLOCK = 128
EPS = 1e-6
SCALE = HEAD_DIM ** -0.5
MAIN_WIDTH = 4 * D_MODEL + MEM_WIDTH
LANES = 128
N_DEV = 8

ADAM_LR = 0.001
ADAM_B1 = 0.9
ADAM_B2 = 0.999
ADAM_EPS = 1e-08
ADAM_WD = 0.01
ADAM_STEP = 10

VMEM_LIMIT = 56 * 2 ** 20
MESH = pl.DeviceIdType.MESH


def _bdot(a, b, dims):
    return lax.dot_general(a.astype(BF16), b.astype(BF16), (dims, ((), ())), preferred_element_type=F32)


@jax.custom_vjp
def mm(a, b):
    return _bdot(a, b, ((1,), (0,)))


@jax.custom_vjp
def mm_nt(a, b):
    return _bdot(a, b, ((1,), (1,)))


@jax.custom_vjp
def mm_tn(a, b):
    return _bdot(a, b, ((0,), (0,)))


mm.defvjp(lambda a, b: (mm(a, b), (a, b)), lambda r, g: (mm_nt(g, r[1]), mm_tn(r[0], g)))
mm_nt.defvjp(lambda a, b: (mm_nt(a, b), (a, b)), lambda r, g: (mm(g, r[1]), mm_tn(g, r[0])))
mm_tn.defvjp(lambda a, b: (mm_tn(a, b), (a, b)), lambda r, g: (mm_nt(r[1], g), mm(r[0], g)))


def hdot(a, b):
    return jnp.dot(a, b, precision=HIGHEST, preferred_element_type=F32)


def rms(x, w):
    return x * lax.rsqrt(jnp.mean(x * x, axis=-1, keepdims=True) + EPS) * w


def l2n(x):
    return x * lax.rsqrt(jnp.sum(x * x, axis=-1, keepdims=True) + EPS)


def _iota2(n, m):
    return lax.broadcasted_iota(jnp.int32, (n, m), 0), lax.broadcasted_iota(jnp.int32, (n, m), 1)


def _lower_ones(n):
    r, c = _iota2(n, n)
    return jnp.where(r >= c, 1.0, 0.0).astype(F32)


def _last_row(x):
    r = lax.broadcasted_iota(jnp.int32, x.shape, 0)
    return jnp.sum(jnp.where(r == x.shape[0] - 1, x, 0.0), axis=0, keepdims=True)


def _softmax_rows(z):
    m = lax.stop_gradient(jnp.max(z, axis=-1, keepdims=True))
    e = jnp.exp(z - m)
    return e * (1.0 / jnp.sum(e, axis=-1, keepdims=True))


_BNN = (((2,), (1,)), ((0,), (0,)))
_BNT = (((2,), (2,)), ((0,), (0,)))
_BTN = (((1,), (1,)), ((0,), (0,)))


def _bbdot(a, b, dims):
    return lax.dot_general(a.astype(BF16), b.astype(BF16), dims, preferred_element_type=F32)


@jax.custom_vjp
def bmm(a, b):
    return _bbdot(a, b, _BNN)


@jax.custom_vjp
def bmm_nt(a, b):
    return _bbdot(a, b, _BNT)


@jax.custom_vjp
def bmm_tn(a, b):
    return _bbdot(a, b, _BTN)


@jax.custom_vjp
def bmm_high(a, b):
    return lax.dot_general(a, b, _BNN, precision=lax.Precision.HIGH, preferred_element_type=F32)


bmm.defvjp(lambda a, b: (bmm(a, b), (a, b)), lambda r, g: (bmm_nt(g, r[1]), bmm_tn(r[0], g)))
bmm_nt.defvjp(lambda a, b: (bmm_nt(a, b), (a, b)), lambda r, g: (bmm(g, r[1]), bmm_tn(g, r[0])))
bmm_tn.defvjp(lambda a, b: (bmm_tn(a, b), (a, b)), lambda r, g: (bmm_nt(r[1], g), bmm(r[0], g)))
bmm_high.defvjp(lambda a, b: (bmm_high(a, b), (a, b)), lambda r, g: (bmm_nt(g, r[1]), bmm_tn(r[0], g)))

NEUMANN_HIGH_LEVELS = 2


@jax.custom_vjp
def inv_unit_lower(a):
    n = a.shape[-1]
    r, c = _iota2(n, n)
    p = jnp.where(r == c, 1.0, 0.0).astype(F32) - a
    ak = a
    for level in range(int(math.log2(n)) - 1):
        dot = bmm_high if level < NEUMANN_HIGH_LEVELS else bmm
        ak = dot(ak, ak)
        p = p + dot(p, ak)
    return p


def _inv_unit_lower_fwd(a):
    t = inv_unit_lower(a)
    return t, t


def _inv_unit_lower_bwd(t, g):
    return (-bmm_tn(t, bmm_nt(g, t)),)


inv_unit_lower.defvjp(_inv_unit_lower_fwd, _inv_unit_lower_bwd)


def delta_intra(q, k, v, gc, beta):
    b, c, _ = q.shape
    r, cc = _iota2(c, c)
    causal = r >= cc
    strict = r > cc
    gi = jnp.broadcast_to(gc, (b, c, c))
    gj = jnp.swapaxes(gi, 1, 2)
    decay = jnp.where(causal, jnp.exp(jnp.where(causal, gi - gj, 0.0)), 0.0)
    kb = k * beta
    a = jnp.where(strict, bmm_nt(kb, k) * decay, 0.0)
    t = inv_unit_lower(a)
    u = bmm(t, v * beta)
    w = bmm(t, kb * jnp.exp(gc))
    qk = jnp.where(causal, bmm_nt(q, k) * decay, 0.0)
    return u, w, qk


def delta_step(s, q, k, gc, u, w, qk):
    v_new = u - bmm(w, s)
    out = bmm(q * jnp.exp(gc), s) + bmm(qk, v_new)
    r = lax.broadcasted_iota(jnp.int32, gc.shape, 1)
    g_last = jnp.sum(jnp.where(r == gc.shape[1] - 1, gc, 0.0), axis=1, keepdims=True)
    k_dec = k * jnp.exp(g_last - gc)
    s_new = s * jnp.exp(g_last) + bmm_tn(k_dec, v_new)
    return out, s_new


def fox_probs(q, k, fq, fk, qpos0):
    s = lax.dot_general(q, k, (((1,), (1,)), ((), ())), preferred_element_type=F32)
    r, c = _iota2(s.shape[0], s.shape[1])
    return _softmax_rows(jnp.where(c <= (r + qpos0), s + (fq - fk), -jnp.inf))


def mem_head(qm, wq, mk, mv):
    p = _softmax_rows(mm_nt(rms(qm, wq) * SCALE, mk))
    return mm(p, mv)


def _heads(x, n):
    return [x[:, h * HEAD_DIM:(h + 1) * HEAD_DIM] for h in range(n)]


def memkv_fn(mem, mnw, wkv, mknw):
    kv = mm(rms(mem, mnw), wkv)
    mk = jnp.concatenate([rms(kh, mknw) for kh in _heads(kv[:, :MEM_WIDTH], MEM_HEADS)], axis=1)
    return mk, kv[:, MEM_WIDTH:]


def dn_gates_fn(ab, alog, dtb):
    g = -jnp.exp(alog) * jax.nn.softplus(ab + dtb)
    low = _lower_ones(CHUNK)
    gc = jnp.concatenate([hdot(low, g[i * CHUNK:(i + 1) * CHUNK]) for i in range(ab.shape[0] // CHUNK)], axis=0)
    lane = lax.broadcasted_iota(jnp.int32, ab.shape, 1)
    return jnp.where(lane < N_HEADS, gc, jax.nn.sigmoid(ab))


def fox_fcum_fn(fp, fbias):
    lf = jax.nn.log_sigmoid(fp + fbias)
    low = _lower_ones(LANES)
    carry = jnp.zeros((1, fp.shape[1]), F32)
    outs = []
    for i in range(fp.shape[0] // LANES):
        cs = hdot(low, lf[i * LANES:(i + 1) * LANES]) + carry
        carry = _last_row(cs)
        outs.append(cs)
    return jnp.concatenate(outs, axis=0)


def fox_qk_fn(qraw, kraw, qnw, knw):
    q = jnp.concatenate([rms(x, qnw) * SCALE for x in _heads(qraw, N_HEADS)], axis=1)
    k = jnp.concatenate([rms(x, knw) for x in _heads(kraw, N_HEADS)], axis=1)
    return q, k


def _mem_out(qm, mqw, mk, mv):
    return [mem_head(a, mqw, b, c) for a, b, c in zip(_heads(qm, MEM_HEADS), _heads(mk, MEM_HEADS), _heads(mv, MEM_HEADS))]


def dn_out_fn(o, z, qm, onw, mqw, mk, mv):
    mix = [rms(a, onw) * jax.nn.silu(b) for a, b in zip(o, _heads(z, N_HEADS))]
    return jnp.concatenate(mix + _mem_out(qm, mqw, mk, mv), axis=1)


def fox_out_fn(o, gate, qm, mqw, mk, mv):
    return jnp.concatenate([o * jax.nn.sigmoid(gate)] + _mem_out(qm, mqw, mk, mv), axis=1)


_HBM = pl.BlockSpec(memory_space=pltpu.HBM)


def _place():
    return lax.axis_index("x"), lax.axis_index("y"), lax.axis_index("c")


class Rider:
    def __init__(self, ins, out_shape, scratch, start, finish, relay=None):
        self.ins, self.out_shape, self.scratch = list(ins), list(out_shape), list(scratch)
        self.start, self.finish, self.relay = start, finish, relay or (lambda *refs: None)
        self.results = None


def gather_rider(xs):
    n = len(xs)

    def plan(x_refs, out_refs, sems):
        send_sems, recv_sems, local_sems = sems
        x, y, c = _place()
        me, sibling = (x, y, c), (x, y, 1 - c)
        chips = [(1 - x, y), (x, 1 - y), (1 - x, 1 - y)]

        def copy(a, k, block, to, src=None):
            px, py, pc = block
            dst = out_refs[a].at[4 * px + 2 * py + pc]
            return pltpu.make_async_remote_copy(
                src_ref=dst if src is None else src, dst_ref=dst,
                send_sem=send_sems.at[a, k], recv_sem=recv_sems.at[a, k], device_id=to, device_id_type=MESH)

        mine = [pltpu.make_async_copy(x_refs[a], out_refs[a].at[4 * x + 2 * y + c], local_sems.at[a]) for a in range(n)]
        first = [copy(a, 0, me, sibling, src=x_refs[a]) for a in range(n)]
        first += [copy(a, 1 + j, me, (*chip, c), src=x_refs[a]) for j, chip in enumerate(chips) for a in range(n)]
        return copy, me, sibling, chips, mine, first

    def start(x_refs, out_refs, sems):
        _, _, _, _, mine, first = plan(x_refs, out_refs, sems)
        for cp in mine + first:
            cp.start()

    def relay(x_refs, out_refs, sems):
        copy, me, sibling, chips, _, _ = plan(x_refs, out_refs, sems)
        _, _, c = me
        for j, chip in enumerate(chips):
            for a in range(n):
                copy(a, 1 + j, (*chip, c), me).wait_recv()
                copy(a, 4 + j, (*chip, c), sibling).start()

    def finish(x_refs, out_refs, sems):
        copy, me, sibling, chips, mine, first = plan(x_refs, out_refs, sems)
        _, _, c = me
        passed = [copy(a, 4 + j, (*chip, c), sibling) for j, chip in enumerate(chips) for a in range(n)]
        for a in range(n):
            copy(a, 0, sibling, me).wait_recv()
        for j, chip in enumerate(chips):
            for a in range(n):
                copy(a, 4 + j, (*chip, 1 - c), me).wait_recv()
        for cp in first + passed:
            cp.wait_send()
        for cp in mine:
            cp.wait()

    return Rider(xs, [jax.ShapeDtypeStruct((N_DEV,) + a.shape, a.dtype) for a in xs],
                 [pltpu.SemaphoreType.DMA((n, 7)), pltpu.SemaphoreType.DMA((n, 7)), pltpu.SemaphoreType.DMA((n,))],
                 start, finish, relay)


def sibling_rider(gs):
    n = len(gs)

    def plan(g_refs, out_refs, sems):
        send_sems, recv_sems = sems
        x, y, c = _place()
        return [pltpu.make_async_remote_copy(
            src_ref=g_refs[a].at[2 * k + 1 - c], dst_ref=out_refs[a].at[k], send_sem=send_sems.at[a, k],
            recv_sem=recv_sems.at[a, k], device_id=(x, y, 1 - c), device_id_type=MESH) for a in range(n) for k in range(4)]

    def start(g_refs, out_refs, sems):
        for cp in plan(g_refs, out_refs, sems):
            cp.start()

    def finish(g_refs, out_refs, sems):
        copies = plan(g_refs, out_refs, sems)
        for cp in copies:
            cp.wait_recv()
        for cp in copies:
            cp.wait_send()

    return Rider(gs, [jax.ShapeDtypeStruct((4,) + g.shape[1:], g.dtype) for g in gs],
                 [pltpu.SemaphoreType.DMA((n, 4)), pltpu.SemaphoreType.DMA((n, 4))], start, finish)


def chips_rider(hs):
    n = len(hs)

    def plan(h_refs, out_refs, sems):
        send_sems, recv_sems, local_sems = sems
        x, y, c = _place()
        mine = 2 * x + y
        chips = [(1 - x, y), (x, 1 - y), (1 - x, 1 - y)]
        keep = [pltpu.make_async_copy(h_refs[a].at[mine], out_refs[a].at[mine], local_sems.at[a]) for a in range(n)]
        sends = [pltpu.make_async_remote_copy(
            src_ref=h_refs[a].at[2 * qx + qy], dst_ref=out_refs[a].at[mine], send_sem=send_sems.at[a, j],
            recv_sem=recv_sems.at[a, j], device_id=(qx, qy, c), device_id_type=MESH)
            for j, (qx, qy) in enumerate(chips) for a in range(n)]
        recvs = [pltpu.make_async_remote_copy(
            src_ref=h_refs[a].at[mine], dst_ref=out_refs[a].at[2 * qx + qy], send_sem=send_sems.at[a, j],
            recv_sem=recv_sems.at[a, j], device_id=(qx, qy, c), device_id_type=MESH)
            for j, (qx, qy) in enumerate(chips) for a in range(n)]
        return keep, sends, recvs

    def start(h_refs, out_refs, sems):
        keep, sends, _ = plan(h_refs, out_refs, sems)
        for cp in keep + sends:
            cp.start()

    def finish(h_refs, out_refs, sems):
        keep, sends, recvs = plan(h_refs, out_refs, sems)
        for cp in recvs:
            cp.wait_recv()
        for cp in sends:
            cp.wait_send()
        for cp in keep:
            cp.wait()

    return Rider(hs, [jax.ShapeDtypeStruct(h.shape, h.dtype) for h in hs],
                 [pltpu.SemaphoreType.DMA((n, 3)), pltpu.SemaphoreType.DMA((n, 3)), pltpu.SemaphoreType.DMA((n,))], start, finish)


def hosted_call(riders, body, *, out_shape, in_specs, out_specs, grid=(), scratch_shapes=(), **kw):
    riders = tuple(riders or ())
    if not riders:
        return pl.pallas_call(body, out_shape=out_shape, in_specs=in_specs, out_specs=out_specs, grid=grid,
                              scratch_shapes=scratch_shapes, **kw)
    single = not isinstance(out_shape, (list, tuple))
    k_out_shape = [out_shape] if single else list(out_shape)
    k_out_specs = [out_specs] if single else list(out_specs)
    n_in, n_out, n_scr = len(in_specs), len(k_out_shape), len(scratch_shapes)
    r_ins = [a for r in riders for a in r.ins]
    r_outs = [s for r in riders for s in r.out_shape]
    r_scr = [s for r in riders for s in r.scratch]

    def full_body(*refs):
        ins = refs[:n_in + len(r_ins)]
        outs = refs[n_in + len(r_ins):n_in + len(r_ins) + n_out + len(r_outs)]
        scr = refs[n_in + len(r_ins) + n_out + len(r_outs):]
        steps = math.prod(grid)
        step = 0
        for d, g in enumerate(grid):
            step = step * g + pl.program_id(d)

        def each(method):
            i0, o0, s0 = n_in, n_out, n_scr
            for r in riders:
                getattr(r, method)(ins[i0:i0 + len(r.ins)], outs[o0:o0 + len(r.out_shape)], scr[s0:s0 + len(r.scratch)])
                i0, o0, s0 = i0 + len(r.ins), o0 + len(r.out_shape), s0 + len(r.scratch)

        def end():
            each("relay")
            each("finish")

        if steps == 1:
            each("start")
            body(*ins[:n_in], *outs[:n_out], *scr[:n_scr])
            end()
        else:
            pl.when(step == 0)(lambda: each("start"))
            body(*ins[:n_in], *outs[:n_out], *scr[:n_scr])
            pl.when(step == steps - 1)(end)

    call = pl.pallas_call(
        full_body, out_shape=k_out_shape + r_outs, in_specs=list(in_specs) + [_HBM] * len(r_ins),
        out_specs=k_out_specs + [_HBM] * len(r_outs), grid=grid, scratch_shapes=list(scratch_shapes) + r_scr, **kw)

    def run(*args):
        res = call(*args, *r_ins)
        o0 = n_out
        for r in riders:
            r.results = list(res[o0:o0 + len(r.out_shape)])
            o0 += len(r.out_shape)
        return res[0] if single else list(res[:n_out])

    return run


def run_riders(riders, *, name):
    hosted_call(riders, lambda: None, name=name, out_shape=[], in_specs=[], out_specs=[])()
    return [r.results for r in riders]


def _pick(n, cands):
    for c in cands:
        if n % c == 0:
            return c
    return n


def _params(sem):
    return pltpu.CompilerParams(dimension_semantics=sem, vmem_limit_bytes=VMEM_LIMIT)


MATMUL_VMEM_BUDGET = 40 * 2 ** 20


def _matmul_tiles(m, n, k, bytes_a, bytes_b, bytes_mn, fixed):
    fm, fn, fk = fixed if fixed is not None else (None, None, None)

    def options(given, size, cands):
        return [given] if given else ([c for c in cands if size % c == 0] or [size])

    best = None
    for tm in options(fm, m, (2048, 1024, 512, 256, 128)):
        for tn in options(fn, n, (512, 256, 128)):
            for tk in options(fk, k, (2048, 1536, 1024, 512, 256, 128)):
                if 2 * (tm * tk * bytes_a + tk * tn * bytes_b + tm * tn * bytes_mn) + tm * tn * 4 > MATMUL_VMEM_BUDGET:
                    continue
                key = ((m // tm) * (n // tn) * (k // tk), -tk)
                if best is None or key < best[0]:
                    best = (key, (tm, tn, tk))
    assert best is not None, (m, n, k, fixed)
    return best[1]


def matmul(a, b, *, name, ta=False, tb=False, post=None, post_ins=(), row_ins=(), acc=False, extra_out=None,
           out_dtype=F32, tiles=None, b_view=None, out_view=None, riders=()):
    (k, m) = a.shape if ta else a.shape[::-1]
    (kb, n) = b_view[:2] if b_view is not None else (b.shape[::-1] if tb else b.shape)
    assert k == kb, (a.shape, b.shape, ta, tb)
    bytes_mn = sum(p.dtype.itemsize for p in post_ins) + jnp.dtype(out_dtype).itemsize
    bytes_mn += jnp.dtype(extra_out[1]).itemsize if extra_out else 0
    tm, tn, tk = _matmul_tiles(m, n, k, a.dtype.itemsize, b.dtype.itemsize, bytes_mn, tiles)
    assert not acc or tn == n, (name, tn, n)
    nk = k // tk
    dims = ((0,) if ta else (1,), (1,) if tb else (0,))
    n_post, n_row = len(post_ins), len(row_ins)
    n_out = 1 + bool(extra_out) + bool(acc)

    def body(*refs):
        a_ref, b_ref = refs[:2]
        post_refs = refs[2:2 + n_post + n_row]
        o_refs, acc_ref = refs[-1 - n_out:-1], refs[-1]
        first_rows, kk = pl.program_id(0) == 0, pl.program_id(2)

        @pl.when(kk == 0)
        def _():
            acc_ref[...] = jnp.zeros_like(acc_ref)

        b_tile = b_ref[...]
        acc_ref[...] += _bdot(a_ref[...], b_tile.reshape(-1, b_tile.shape[-1]), dims)

        @pl.when(kk == nk - 1)
        def _():
            r = acc_ref[...]
            rows = [p[...] for p in post_refs[n_post:]]
            if post is not None:
                r = post(r, *[p[...] for p in post_refs[:n_post]], *rows)
            if acc:
                r, s = r
                sum_ref = o_refs[-1]

                @pl.when(first_rows)
                def _():
                    sum_ref[...] = s

                @pl.when(jnp.logical_not(first_rows))
                def _():
                    sum_ref[...] += s

            o_refs[0][...] = r.astype(out_dtype)
            if extra_out:
                o_refs[1][...] = extra_out[0](r, *rows).astype(extra_out[1])

    a_spec = pl.BlockSpec((tk, tm), lambda i, j, kk: (kk, i)) if ta else pl.BlockSpec((tm, tk), lambda i, j, kk: (i, kk))
    if b_view is not None:
        b_spec = b_view[2]
    else:
        b_spec = pl.BlockSpec((tn, tk), lambda i, j, kk: (j, kk)) if tb else pl.BlockSpec((tk, tn), lambda i, j, kk: (kk, j))
    mn_spec = pl.BlockSpec((tm, tn), lambda i, j, kk: (i, j))
    row_spec = pl.BlockSpec((1, tn), lambda i, j, kk: (0, j))
    o_shape, o_spec = ((m, n), mn_spec) if out_view is None else out_view
    out_shape = [jax.ShapeDtypeStruct(o_shape, out_dtype)]
    out_specs = [o_spec]
    if extra_out:
        out_shape.append(jax.ShapeDtypeStruct((m, n), extra_out[1]))
        out_specs.append(mn_spec)
    if acc:
        out_shape.append(jax.ShapeDtypeStruct((1, n), F32))
        out_specs.append(row_spec)
    res = hosted_call(
        riders, body, name=name, grid=(m // tm, n // tn, nk),
        in_specs=[a_spec, b_spec] + [mn_spec] * n_post + [row_spec] * n_row, out_specs=out_specs, out_shape=out_shape,
        scratch_shapes=[pltpu.VMEM((tm, tn), F32)],
        compiler_params=_params(("arbitrary" if acc else "parallel", "parallel", "arbitrary")),
    )(a, b, *post_ins, *row_ins)
    return res if n_out > 1 else res[0]


def rows_call(fn, row_ins, full_ins, row_outs, acc_outs, *, tm, name, riders=()):
    row_ins = [r if isinstance(r, tuple) else (r, r.shape[-1], 0) for r in row_ins]
    t = row_ins[0][0].shape[-2]
    tm = min(tm, t)
    n_in = len(row_ins) + len(full_ins)
    n_row = len(row_outs)

    def body(*refs):
        res = fn(*[[r[h] for h in range(r.shape[0])] if (i < len(row_ins) and len(r.shape) == 3) else r[...]
                   for i, r in enumerate(refs[:n_in])])
        res = res if isinstance(res, (tuple, list)) else (res,)
        outs = refs[n_in:]
        for ref, val in zip(outs[:n_row], res[:n_row]):
            if len(ref.shape) == 3:
                for h, vh in enumerate(val):
                    ref[h] = vh.astype(ref.dtype)
            else:
                ref[...] = val.astype(ref.dtype)
        first = pl.program_id(0) == 0
        for ref, val in zip(outs[n_row:], res[n_row:]):
            @pl.when(first)
            def _(ref=ref, val=val):
                ref[...] = val

            @pl.when(jnp.logical_not(first))
            def _(ref=ref, val=val):
                ref[...] += val

    def full_spec(shape):
        return pl.BlockSpec(shape, lambda i, nd=len(shape): (0,) * nd)

    def row_spec(lead, w, cb):
        if lead is None:
            return pl.BlockSpec((tm, w), lambda i: (i, cb))
        return pl.BlockSpec((lead, tm, w), lambda i: (0, i, cb))

    def lead_cols(c):
        return c if isinstance(c, tuple) else (None, c)

    in_specs = [row_spec(a.shape[0] if a.ndim == 3 else None, w, cb) for (a, w, cb) in row_ins]
    in_specs += [full_spec(f.shape) for f in full_ins]
    out_specs = [row_spec(*lead_cols(c), 0) for c, _ in row_outs] + [full_spec(s) for s in acc_outs]
    out_shape = [jax.ShapeDtypeStruct(tuple(d for d in (lead_cols(c)[0], t, lead_cols(c)[1]) if d is not None), dt)
                 for c, dt in row_outs] + [jax.ShapeDtypeStruct(s, F32) for s in acc_outs]
    res = hosted_call(
        riders, body, name=name, grid=(t // tm,), in_specs=in_specs, out_specs=out_specs, out_shape=out_shape,
        compiler_params=_params(("arbitrary",)),
    )(*[r[0] for r in row_ins], *full_ins)
    return res


def vjp_rows(fn, n_diff_row, row_diff_full):
    def bwd(*args, n_row, n_ct):
        prim_rows = args[:n_row]
        cts = args[n_row:n_row + n_ct]
        fulls = args[n_row + n_ct:]
        _, vjp = jax.vjp(fn, *prim_rows, *fulls)
        g = vjp(cts[0] if n_ct == 1 else tuple(cts))
        out = list(g[:n_diff_row])
        out += [gf for gf, d in zip(g[n_row:], row_diff_full) if d]
        return tuple(out)
    return bwd


def _shift_down(x, s):
    if s == 0:
        return x
    t = lax.broadcasted_iota(jnp.int32, x.shape, 0)
    return jnp.where(t >= s, pltpu.roll(x, s, 0), 0.0)


def _shift_up(x, s):
    if s == 0:
        return x
    n = x.shape[0]
    t = lax.broadcasted_iota(jnp.int32, x.shape, 0)
    return jnp.where(t < n - s, pltpu.roll(x, n - s, 0), 0.0)


def _conv(x, w_ref):
    return sum(w_ref[pl.ds(j, 1), :] * _shift_down(x, CONV_WIDTH - 1 - j) for j in range(CONV_WIDTH))


_DN_POST = (lambda c: l2n(jax.nn.silu(c)) * SCALE, lambda c: l2n(jax.nn.silu(c)), jax.nn.silu)


def dn_prep_fwd(proj, conv_w, *, name, riders=()):
    t = proj.shape[0]

    def body(xq, xk, xv, wq, wk, wv, oq, ok, ov):
        for x_ref, w_ref, o_ref, post in zip((xq, xk, xv), (wq, wk, wv), (oq, ok, ov), _DN_POST):
            o_ref[...] = post(_conv(x_ref[...], w_ref))

    x_specs = [pl.BlockSpec((t, HEAD_DIM), lambda h, g=g: (0, g * N_HEADS + h)) for g in range(3)]
    w_specs = [pl.BlockSpec((CONV_WIDTH, HEAD_DIM), lambda h, g=g: (0, g * N_HEADS + h)) for g in range(3)]
    o_spec = pl.BlockSpec((None, t, HEAD_DIM), lambda h: (h, 0, 0))
    return hosted_call(
        riders, body, name=name, grid=(N_HEADS,), in_specs=x_specs + w_specs, out_specs=[o_spec] * 3,
        out_shape=[jax.ShapeDtypeStruct((N_HEADS, t, HEAD_DIM), F32)] * 3, compiler_params=_params(("parallel",)),
    )(proj, proj, proj, conv_w, conv_w, conv_w)


def dn_prep_bwd(proj, conv_w, dq, dk, dv, *, name, riders=()):
    t = proj.shape[0]

    def body(xq, xk, xv, wq, wk, wv, gq, gk, gv, dxq, dxk, dxv, dwq, dwk, dwv):
        for x_ref, w_ref, g_ref, dx_ref, dw_ref, post in zip(
                (xq, xk, xv), (wq, wk, wv), (gq, gk, gv), (dxq, dxk, dxv), (dwq, dwk, dwv), _DN_POST):
            x = x_ref[...]
            _, vjp = jax.vjp(post, _conv(x, w_ref))
            dc, = vjp(g_ref[...])
            dx = sum(w_ref[pl.ds(j, 1), :] * _shift_up(dc, CONV_WIDTH - 1 - j) for j in range(CONV_WIDTH))
            dx_ref[...] = dx.astype(dx_ref.dtype)
            for j in range(CONV_WIDTH):
                dw_ref[pl.ds(j, 1), :] = jnp.sum(dc * _shift_down(x, CONV_WIDTH - 1 - j), axis=0, keepdims=True)

    x_specs = [pl.BlockSpec((t, HEAD_DIM), lambda h, g=g: (0, g * N_HEADS + h)) for g in range(3)]
    w_specs = [pl.BlockSpec((CONV_WIDTH, HEAD_DIM), lambda h, g=g: (0, g * N_HEADS + h)) for g in range(3)]
    g_spec = pl.BlockSpec((None, t, HEAD_DIM), lambda h: (h, 0, 0))
    dx_spec = pl.BlockSpec((t, HEAD_DIM), lambda h: (0, h))
    dw_spec = pl.BlockSpec((CONV_WIDTH, HEAD_DIM), lambda h: (0, h))
    return hosted_call(
        riders, body, name=name, grid=(N_HEADS,), in_specs=x_specs + w_specs + [g_spec] * 3, out_specs=[dx_spec] * 3 + [dw_spec] * 3,
        out_shape=[jax.ShapeDtypeStruct((t, D_MODEL), BF16)] * 3 + [jax.ShapeDtypeStruct((CONV_WIDTH, D_MODEL), F32)] * 3,
        compiler_params=_params(("parallel",)),
    )(proj, proj, proj, conv_w, conv_w, conv_w, dq, dk, dv)


INTRA_CHUNKS = 4


def _lane_column(x, lane_index):
    lane = lax.broadcasted_iota(jnp.int32, x.shape, 1)
    return jnp.sum(jnp.where(lane == lane_index, x, 0.0), axis=1, keepdims=True)


def _head_columns(g, first_lane):
    return jnp.concatenate([_lane_column(g, first_lane + h)[None] for h in range(N_HEADS)], axis=0)


def _intra_of_gates(q, k, v, gates):
    nb = N_HEADS * (gates.shape[0] // CHUNK)

    def chunks(x):
        return x.reshape(nb, CHUNK, x.shape[-1])

    res = delta_intra(chunks(q), chunks(k), chunks(v), chunks(_head_columns(gates, 0)), chunks(_head_columns(gates, N_HEADS)))
    return tuple(x.reshape(N_HEADS, -1, x.shape[-1]) for x in res)


def _step_of_gates(s, q, k, gates, u, w, qk):
    return delta_step(s, q, k, _head_columns(gates, 0), u, w, qk)


def _head_major(rows, w, index):
    return pl.BlockSpec((N_HEADS, rows, w), lambda i: (0, index(i), 0))


def delta_intra_fwd(q, k, v, gates, *, name, riders=()):
    t = q.shape[1]
    rows = min(INTRA_CHUNKS, t // CHUNK) * CHUNK

    def body(q_ref, k_ref, v_ref, g_ref, u_ref, w_ref, qk_ref):
        for ref, val in zip((u_ref, w_ref, qk_ref), _intra_of_gates(q_ref[...], k_ref[...], v_ref[...], g_ref[...])):
            ref[...] = val

    x_spec, qk_spec = (_head_major(rows, w, lambda i: i) for w in (HEAD_DIM, CHUNK))
    g_spec = pl.BlockSpec((rows, LANES), lambda i: (i, 0))
    return hosted_call(
        riders, body, name=name, grid=(t // rows,), in_specs=[x_spec] * 3 + [g_spec], out_specs=[x_spec, x_spec, qk_spec],
        out_shape=[jax.ShapeDtypeStruct((N_HEADS, t, HEAD_DIM), F32)] * 2 + [jax.ShapeDtypeStruct((N_HEADS, t, CHUNK), F32)],
        compiler_params=_params(("parallel",)),
    )(q, k, v, gates)


def delta_seq_fwd(q, k, gates, u, w, qk, *, name, riders=()):
    t = q.shape[1]
    nc = t // CHUNK

    def body(q_ref, k_ref, g_ref, u_ref, w_ref, qk_ref, o_ref, s0_ref, s_ref):
        @pl.when(pl.program_id(0) == 0)
        def _():
            s_ref[...] = jnp.zeros_like(s_ref)

        s = s_ref[...]
        s0_ref[...] = s
        o, s_new = _step_of_gates(s, q_ref[...], k_ref[...], g_ref[...], u_ref[...], w_ref[...], qk_ref[...])
        o_ref[...] = o
        s_ref[...] = s_new

    x_spec, qk_spec = (_head_major(CHUNK, w, lambda c: c) for w in (HEAD_DIM, CHUNK))
    g_spec = pl.BlockSpec((CHUNK, LANES), lambda c: (c, 0))
    s_spec = pl.BlockSpec((N_HEADS, None, HEAD_DIM, HEAD_DIM), lambda c: (0, c, 0, 0))
    return hosted_call(
        riders, body, name=name, grid=(nc,), in_specs=[x_spec, x_spec, g_spec, x_spec, x_spec, qk_spec], out_specs=[x_spec, s_spec],
        out_shape=[jax.ShapeDtypeStruct((N_HEADS, t, HEAD_DIM), F32),
                   jax.ShapeDtypeStruct((N_HEADS, nc, HEAD_DIM, HEAD_DIM), F32)],
        scratch_shapes=[pltpu.VMEM((N_HEADS, HEAD_DIM, HEAD_DIM), F32)],
        compiler_params=_params(("arbitrary",)),
    )(q, k, gates, u, w, qk)


def delta_seq_bwd(q, k, gates, u, w, qk, s0, do, *, name, riders=()):
    t = q.shape[1]
    nc = t // CHUNK

    def body(q_ref, k_ref, g_ref, u_ref, w_ref, qk_ref, s0_ref, do_ref,
             dq_ref, dk_ref, dg_ref, du_ref, dw_ref, dqk_ref, ds_ref):
        @pl.when(pl.program_id(0) == 0)
        def _():
            ds_ref[...] = jnp.zeros_like(ds_ref)

        _, vjp = jax.vjp(_step_of_gates, s0_ref[...], q_ref[...], k_ref[...], g_ref[...], u_ref[...], w_ref[...], qk_ref[...])
        ds, dq, dk, dg, du, dw, dqk = vjp((do_ref[...], ds_ref[...]))
        for ref, val in zip((ds_ref, dq_ref, dk_ref, dg_ref, du_ref, dw_ref, dqk_ref), (ds, dq, dk, dg, du, dw, dqk)):
            ref[...] = val

    x_spec, qk_spec = (_head_major(CHUNK, w, lambda c: nc - 1 - c) for w in (HEAD_DIM, CHUNK))
    g_spec = pl.BlockSpec((CHUNK, LANES), lambda c: (nc - 1 - c, 0))
    s_spec = pl.BlockSpec((N_HEADS, None, HEAD_DIM, HEAD_DIM), lambda c: (0, nc - 1 - c, 0, 0))
    head_shape = [jax.ShapeDtypeStruct((N_HEADS, t, w_), F32) for w_ in (HEAD_DIM, HEAD_DIM, HEAD_DIM, HEAD_DIM, CHUNK)]
    return hosted_call(
        riders, body, name=name, grid=(nc,), in_specs=[x_spec, x_spec, g_spec, x_spec, x_spec, qk_spec, s_spec, x_spec],
        out_specs=[x_spec, x_spec, g_spec, x_spec, x_spec, qk_spec],
        out_shape=head_shape[:2] + [jax.ShapeDtypeStruct((t, LANES), F32)] + head_shape[2:],
        scratch_shapes=[pltpu.VMEM((N_HEADS, HEAD_DIM, HEAD_DIM), F32)],
        compiler_params=_params(("arbitrary",)),
    )(q, k, gates, u, w, qk, s0, do)


def delta_intra_bwd(q, k, v, gates, du, dw, dqk, dq_s, dk_s, dg_s, *, name, riders=()):
    t = q.shape[1]
    rows = min(INTRA_CHUNKS, t // CHUNK) * CHUNK

    def body(q_ref, k_ref, v_ref, g_ref, du_ref, dw_ref, dqk_ref, dqs_ref, dks_ref, dgs_ref, dq_ref, dk_ref, dv_ref, dg_ref):
        _, vjp = jax.vjp(_intra_of_gates, q_ref[...], k_ref[...], v_ref[...], g_ref[...])
        dq, dk, dv, dg = vjp((du_ref[...], dw_ref[...], dqk_ref[...]))
        dq_ref[...] = dq + dqs_ref[...]
        dk_ref[...] = dk + dks_ref[...]
        dv_ref[...] = dv
        dg_ref[...] = dg + dgs_ref[...]

    x_spec, qk_spec = (_head_major(rows, w, lambda i: i) for w in (HEAD_DIM, CHUNK))
    g_spec = pl.BlockSpec((rows, LANES), lambda i: (i, 0))
    return hosted_call(
        riders, body, name=name, grid=(t // rows,),
        in_specs=[x_spec] * 3 + [g_spec, x_spec, x_spec, qk_spec, x_spec, x_spec, g_spec],
        out_specs=[x_spec] * 3 + [g_spec],
        out_shape=[jax.ShapeDtypeStruct((N_HEADS, t, HEAD_DIM), F32)] * 3 + [jax.ShapeDtypeStruct((t, LANES), F32)],
        compiler_params=_params(("parallel",)),
    )(q, k, v, gates, du, dw, dqk, dq_s, dk_s, dg_s)


_V_BLOCK = 2 * N_HEADS
FOX_GROUPS = 8


def _fox_groups(t):
    nq = t // Q_BLOCK
    per = max(1, nq // FOX_GROUPS)
    return [(g0, per, (g0 + per) * Q_BLOCK) for g0 in range(0, nq, per)]


def fox_attn_fwd(q, k, proj, fq, fk, *, name, riders=()):
    t = q.shape[0]

    def body(q_ref, k_ref, v_ref, fq_ref, fk_ref, o_ref, kb_ref, vb_ref):
        head = pl.program_id(0)
        kb_ref[...] = k_ref[...].astype(BF16)
        vb_ref[...] = v_ref[...].astype(BF16)
        for g0, per, keys in _fox_groups(t):
            def block(j, carry, g0=g0, keys=keys):
                rows = pl.ds(pl.multiple_of((g0 + j) * Q_BLOCK, Q_BLOCK), Q_BLOCK)
                p = fox_probs(q_ref[rows, :].astype(BF16), kb_ref[0:keys, :], _lane_column(fq_ref[rows, :], head),
                              fk_ref[:, 0:keys], (g0 + j) * Q_BLOCK)
                o_ref[rows, :] = jnp.dot(p.astype(BF16), vb_ref[0:keys, :], preferred_element_type=F32)
                return carry
            lax.fori_loop(0, per, block, 0)

    x_spec = pl.BlockSpec((t, HEAD_DIM), lambda h: (0, h))
    v_spec = pl.BlockSpec((t, HEAD_DIM), lambda h: (0, _V_BLOCK + h))
    fq_spec = pl.BlockSpec((t, LANES), lambda h: (0, 0))
    fk_spec = pl.BlockSpec((None, 1, t), lambda h: (h, 0, 0))
    return hosted_call(
        riders, body, name=name, grid=(N_HEADS,), in_specs=[x_spec, x_spec, v_spec, fq_spec, fk_spec], out_specs=x_spec,
        out_shape=jax.ShapeDtypeStruct((t, D_MODEL), F32), scratch_shapes=[pltpu.VMEM((t, HEAD_DIM), BF16)] * 2,
        compiler_params=_params(("parallel",)),
    )(q, k, proj, fq, fk)


def fox_attn_bwd(q, k, proj, fq, fk, do, *, name, riders=()):
    t = q.shape[0]

    def body(q_ref, k_ref, v_ref, fq_ref, fk_ref, do_ref, dq_ref, dk_ref, dv_out_ref, dfq_ref, dfk_ref, kb_ref, vb_ref, dv_ref):
        head = pl.program_id(0)

        @pl.when(head == 0)
        def _():
            dfq_ref[...] = jnp.zeros_like(dfq_ref)

        kb_ref[...] = k_ref[...].astype(BF16)
        vb_ref[...] = v_ref[...].astype(BF16)
        dk_ref[...] = jnp.zeros_like(dk_ref)
        dv_ref[...] = jnp.zeros_like(dv_ref)
        dfk_ref[...] = jnp.zeros_like(dfk_ref)
        nt = (((1,), (1,)), ((), ()))
        tn = (((0,), (0,)), ((), ()))
        for g0, per, keys in _fox_groups(t):
            def block(j, carry, g0=g0, keys=keys):
                rows = pl.ds(pl.multiple_of((g0 + j) * Q_BLOCK, Q_BLOCK), Q_BLOCK)
                qb, dob = q_ref[rows, :].astype(BF16), do_ref[rows, :].astype(BF16)
                kb, vb = kb_ref[0:keys, :], vb_ref[0:keys, :]
                p = fox_probs(qb, kb, _lane_column(fq_ref[rows, :], head), fk_ref[:, 0:keys], (g0 + j) * Q_BLOCK)
                dp = lax.dot_general(dob, vb, nt, preferred_element_type=F32)
                dz = p * (dp - jnp.sum(dp * p, axis=-1, keepdims=True))
                pb, dzb = p.astype(BF16), dz.astype(BF16)
                dq_ref[rows, :] = jnp.dot(dzb, kb, preferred_element_type=F32)
                lane = lax.broadcasted_iota(jnp.int32, (Q_BLOCK, LANES), 1)
                dfq_ref[rows, :] += jnp.where(lane == head, jnp.sum(dz, axis=-1, keepdims=True), 0.0)
                dk_ref[0:keys, :] += lax.dot_general(dzb, qb, tn, preferred_element_type=F32)
                dv_ref[0:keys, :] += lax.dot_general(pb, dob, tn, preferred_element_type=F32)
                dfk_ref[:, 0:keys] -= jnp.sum(dz, axis=0, keepdims=True)
                return carry
            lax.fori_loop(0, per, block, 0)
        dv_out_ref[...] = dv_ref[...].astype(dv_out_ref.dtype)

    x_spec = pl.BlockSpec((t, HEAD_DIM), lambda h: (0, h))
    v_spec = pl.BlockSpec((t, HEAD_DIM), lambda h: (0, _V_BLOCK + h))
    fq_spec = pl.BlockSpec((t, LANES), lambda h: (0, 0))
    fk_spec = pl.BlockSpec((None, 1, t), lambda h: (h, 0, 0))
    return hosted_call(
        riders, body, name=name, grid=(N_HEADS,), in_specs=[x_spec, x_spec, v_spec, fq_spec, fk_spec, x_spec],
        out_specs=[x_spec, x_spec, x_spec, fq_spec, fk_spec],
        out_shape=[jax.ShapeDtypeStruct((t, D_MODEL), F32)] * 2 + [jax.ShapeDtypeStruct((t, D_MODEL), BF16)]
        + [jax.ShapeDtypeStruct((t, LANES), F32), jax.ShapeDtypeStruct((N_HEADS, 1, t), F32)],
        scratch_shapes=[pltpu.VMEM((t, HEAD_DIM), BF16)] * 2 + [pltpu.VMEM((t, HEAD_DIM), F32)],
        compiler_params=_params(("arbitrary",)),
    )(q, k, proj, fq, fk, do)


def memkv_fwd(mem, mnw, wkv, mknw, *, name):
    n = mem.shape[0]

    def body(mem_ref, mnw_ref, w_ref, mknw_ref, mk_ref, mv_ref):
        mk, mv = memkv_fn(mem_ref[...], mnw_ref[...], w_ref[...], mknw_ref[...])
        mk_ref[...] = mk
        mv_ref[...] = mv

    return pl.pallas_call(
        body, name=name, out_shape=[jax.ShapeDtypeStruct((n, MEM_WIDTH), F32)] * 2,
        compiler_params=pltpu.CompilerParams(vmem_limit_bytes=VMEM_LIMIT),
    )(mem, mnw, wkv, mknw)


def memkv_bwd(mem, mnw, wkv, mknw, dmk, dmv, *, name):
    def body(mem_ref, mnw_ref, w_ref, mknw_ref, dmk_ref, dmv_ref, dmnw_ref, dw_ref, dmknw_ref):
        f = functools.partial(memkv_fn, mem_ref[...])
        _, vjp = jax.vjp(f, mnw_ref[...], w_ref[...].astype(F32), mknw_ref[...])
        dmnw, dw, dmknw = vjp((dmk_ref[...], dmv_ref[...]))
        dmnw_ref[...] = dmnw
        dw_ref[...] = dw.astype(dw_ref.dtype)
        dmknw_ref[...] = dmknw

    return pl.pallas_call(
        body, name=name,
        out_shape=[jax.ShapeDtypeStruct(mnw.shape, F32), jax.ShapeDtypeStruct(wkv.shape, BF16), jax.ShapeDtypeStruct(mknw.shape, F32)],
        compiler_params=pltpu.CompilerParams(vmem_limit_bytes=VMEM_LIMIT),
    )(mem, mnw, wkv, mknw, dmk, dmv)


def _row(v, width=None):
    v = v.reshape(1, -1)
    if width is not None and v.shape[1] < width:
        v = jnp.pad(v, ((0, 0), (0, width - v.shape[1])))
    return v


def _norm_fwd(x, w, name, riders=()):
    return rows_call(lambda x, w: rms(x, w), [x], [w], [(D_MODEL, BF16)], [], tm=512, name=name, riders=riders)[0]


FF_PIECE = D_FF // N_DEV


def _add(r, x, *rows):
    return r + x


def _norm_rows(r, w):
    return rms(r, w)


def _norm_bwd_post(dh, x, dx_in, w):
    _, vjp = jax.vjp(rms, x, w)
    dx, dw = vjp(dh)
    return dx + dx_in, dw


def _piece(rows, cols, index):
    return pl.BlockSpec((None, rows, cols), lambda i, j, kk: (index(i, j, kk), 0, 0))


def _two_pieces(rows, cols, index):
    return pl.BlockSpec((2, rows, cols), lambda i, j, kk: (index(i, j, kk), 0, 0))


def _mlp_fwd(x, h2, w1, w2, layer, riders=(), next_norm_w=None):
    riders = list(riders) + [None, None]
    u, a1 = matmul(h2, w1, name=f"mlp1_fwd_{layer}", tiles=(None, FF_PIECE, D_MODEL),
                   extra_out=(lambda u: jnp.square(jnp.maximum(u, 0.0)), BF16),
                   b_view=(D_MODEL, D_FF, _piece(D_MODEL, FF_PIECE, lambda i, j, kk: j)), riders=riders[0])
    norm = dict(row_ins=[next_norm_w], extra_out=(_norm_rows, BF16)) if next_norm_w is not None else {}
    y = matmul(a1, w2, name=f"mlp2_fwd_{layer}", post=_add, post_ins=[x], tiles=(None, D_MODEL, 2 * FF_PIECE),
               b_view=(D_FF, D_MODEL, _two_pieces(FF_PIECE, D_MODEL, lambda i, j, kk: kk)), riders=riders[1], **norm)
    return y, (x, h2, u, a1)


def pair_sum(g, got, *, name):
    _, rows, cols = g.shape
    tile = _pick(rows, (512, 256, 128))
    c = lax.axis_index("c").astype(jnp.int32).reshape(1)

    def body(c_ref, a_ref, b_ref, o_ref):
        o_ref[...] = (a_ref[...].astype(F32) + b_ref[...].astype(F32)).astype(o_ref.dtype)

    grid_spec = pltpu.PrefetchScalarGridSpec(
        num_scalar_prefetch=1, grid=(4, rows // tile),
        in_specs=[pl.BlockSpec((None, tile, cols), lambda k, i, c_ref: (2 * k + c_ref[0], i, 0)),
                  pl.BlockSpec((None, tile, cols), lambda k, i, c_ref: (k, i, 0))],
        out_specs=pl.BlockSpec((None, tile, cols), lambda k, i, c_ref: (k, i, 0)))
    return pl.pallas_call(
        body, name=name, grid_spec=grid_spec, out_shape=jax.ShapeDtypeStruct((4, rows, cols), g.dtype),
        compiler_params=_params(("parallel", "parallel")),
    )(c, g, got)


def chip_sums(names, pieces, gots):
    return [pair_sum(a, got, name=f"grads_pair_sum_{n}") for n, a, got in zip(names, pieces, gots)]


def _mlp_bwd(dy, res, n2w, w1, w2, layer, riders=()):
    x, h2, u, a1 = res
    du = matmul(dy, w2, tb=True, name=f"mlp2_dx_{layer}", out_dtype=BF16, tiles=(None, 2 * FF_PIECE, D_MODEL),
                post=lambda r, u: r * (2.0 * jnp.maximum(u, 0.0)), post_ins=[u],
                b_view=(D_MODEL, D_FF, _two_pieces(FF_PIECE, D_MODEL, lambda i, j, kk: j)), riders=riders)
    dw2 = matmul(a1, dy, ta=True, name=f"mlp2_dw_{layer}", out_dtype=BF16, tiles=(FF_PIECE, D_MODEL, None), out_view=(
        w2.shape, _piece(FF_PIECE, D_MODEL, lambda i, j, kk: i)))
    sib2 = sibling_rider([dw2])
    dx, dn2w = matmul(du, w1, tb=True, name=f"mlp1_dx_{layer}", tiles=(None, D_MODEL, FF_PIECE),
                      b_view=(D_FF, D_MODEL, _piece(D_MODEL, FF_PIECE, lambda i, j, kk: kk)),
                      post=_norm_bwd_post, post_ins=[x, dy], row_ins=[n2w], acc=True, riders=[sib2])
    dw1 = matmul(h2, du, ta=True, name=f"mlp1_dw_{layer}", out_dtype=BF16, tiles=(D_MODEL, FF_PIECE, None), out_view=(
        w1.shape, _piece(D_MODEL, FF_PIECE, lambda i, j, kk: j)))
    return dx, dw1, dw2, dn2w, sibling_rider([dw1]), sib2


def _in_proj_bwd(h, dmain, dsmall, w_main, w_small, x, dx_in, n1w, tag):
    dh = matmul(dmain, w_main, tb=True, name=f"inproj_dx_main_{tag}")

    def post(r, dh_main, x, dx_in, w):
        return _norm_bwd_post(r + dh_main, x, dx_in, w)

    dx, dn1w = matmul(dsmall, w_small, tb=True, name=f"inproj_dx_small_{tag}", tiles=(None, D_MODEL, None),
                      post=post, post_ins=[dh, x, dx_in], row_ins=[n1w], acc=True)
    dw_main = matmul(h, dmain, ta=True, out_dtype=BF16, name=f"inproj_dw_main_{tag}")
    dw_small = matmul(h, dsmall, ta=True, out_dtype=BF16, name=f"inproj_dw_small_{tag}")
    return dx, dn1w, dw_main, dw_small


def local_step(x, mem, target, w, m, v):
    t = x.shape[0]
    n_mem = mem.shape[0]
    g = {}

    def wire(a):
        return a.astype(BF16)

    (dn_g,), = run_riders([gather_rider([wire(w["dn_w_in"][0])])], name="weights_gather_first")
    dn_main, dn_ab = in_proj_weights(dn_g, DN_IN, 2 * N_HEADS)
    fox_w = wire(w["fox_w_in"][0])
    ride_out = gather_rider([wire(w["w_out"][0]), wire(w["w_out"][1]), w["dn_conv_w"][0]])
    ride_kv = gather_rider([wire(w["w_mem_kv"])])
    ride_mlp1_0 = gather_rider([wire(w["w_mlp1"][0])])
    ride_mlp2_0 = gather_rider([wire(w["w_mlp2"][0])])
    ride_fox_a, ride_fox_b = gather_rider([fox_w[:D_MODEL // 2]]), gather_rider([fox_w[D_MODEL // 2:]])
    ride_mlp_1 = gather_rider([wire(w["w_mlp1"][1]), wire(w["w_mlp2"][1])])
    mnw, mknw = _row(w["mem_norm_w"]), _row(w["mem_k_norm_w"])

    n1w0, n2w0 = _row(w["norm1_w"][0]), _row(w["norm2_w"][0])
    n1w1, n2w1 = _row(w["norm1_w"][1]), _row(w["norm2_w"][1])
    alog, dtb = _row(w["dn_a_log"][0], LANES), _row(w["dn_dt_bias"][0], LANES)
    onw, mqw0 = _row(w["dn_o_norm_w"][0]), _row(w["memq_norm_w"][0])
    x0 = x
    h0 = _norm_fwd(x0, n1w0, "norm1_fwd_0")
    pm0 = matmul(h0, dn_main, name="inproj_main_0", riders=[ride_out])
    w_out0, w_out1 = (a.reshape(OUT_IN, D_MODEL) for a in ride_out.results[:2])
    conv_w = ride_out.results[2].transpose(1, 0, 2).reshape(CONV_WIDTH, 3 * D_MODEL)
    ps0 = matmul(h0, dn_ab, name="inproj_small_0")
    gates = rows_call(dn_gates_fn, [ps0], [alog, dtb], [(LANES, F32)], [], tm=512, name="dn_gates_fwd")[0]
    q0, k0, v0 = dn_prep_fwd(pm0, conv_w, name="dn_prep_fwd", riders=[ride_kv])
    w_kv = ride_kv.results[0].reshape(D_MODEL, D_MODEL)
    mk, mv = memkv_fwd(mem, mnw, w_kv, mknw, name="memkv_fwd")
    u0, w0, qk0 = delta_intra_fwd(q0, k0, v0, gates, name="delta_intra_fwd", riders=[ride_mlp1_0])
    o0, s_start = delta_seq_fwd(q0, k0, gates, u0, w0, qk0, name="delta_seq_fwd", riders=[ride_mlp2_0])
    cat0 = rows_call(dn_out_fn, [o0, (pm0, D_MODEL, 3), (pm0, MEM_WIDTH, 8)], [onw, mqw0, mk, mv],
                     [(D_MODEL + MEM_WIDTH, BF16)], [], tm=256, name="dn_out_fwd")[0]
    (w1_0,), (w2_0,) = ride_mlp1_0.results, ride_mlp2_0.results
    x1, h2_0 = matmul(cat0, w_out0, post=_add, post_ins=[x0], row_ins=[n2w0], extra_out=(_norm_rows, BF16),
                      tiles=(None, D_MODEL, None), name="wout_fwd_0")
    (x2, h1), mlp_res0 = _mlp_fwd(x1, h2_0, w1_0, w2_0, 0, riders=[[ride_fox_a], [ride_fox_b]], next_norm_w=n1w1)
    fox_main, fox_f = in_proj_weights(
        jnp.concatenate([ride_fox_a.results[0], ride_fox_b.results[0]], axis=1), FOX_IN, N_HEADS)

    fbias = _row(w["fox_f_bias"][0], LANES)
    qnw, knw, mqw1 = _row(w["fox_q_norm_w"][0]), _row(w["fox_k_norm_w"][0]), _row(w["memq_norm_w"][1])
    pm1 = matmul(h1, fox_main, name="inproj_main_1")
    ps1 = matmul(h1, fox_f, name="inproj_small_1")
    fq = rows_call(fox_fcum_fn, [ps1], [fbias], [(LANES, F32)], [], tm=t, name="fox_fcum_fwd")[0]
    fk = fq[:, :N_HEADS].T[:, None, :]
    q1, k1 = rows_call(fox_qk_fn, [(pm1, D_MODEL, 0), (pm1, D_MODEL, 1)], [qnw, knw], [(D_MODEL, F32)] * 2, [], tm=256,
                       name="fox_qk_fwd")
    o1 = fox_attn_fwd(q1, k1, pm1, fq, fk, name="fox_attn_fwd", riders=[ride_mlp_1])
    cat1 = rows_call(fox_out_fn, [o1, (pm1, D_MODEL, 3), (pm1, MEM_WIDTH, 8)], [mqw1, mk, mv],
                     [(D_MODEL + MEM_WIDTH, BF16)], [], tm=256, name="fox_out_fwd")[0]
    w1_1, w2_1 = ride_mlp_1.results
    x3, h2_1 = matmul(cat1, w_out1, post=_add, post_ins=[x2], row_ins=[n2w1], extra_out=(_norm_rows, BF16),
                      tiles=(None, D_MODEL, None), name="wout_fwd_1")
    y, mlp_res1 = _mlp_fwd(x3, h2_1, w1_1, w2_1, 1)

    def loss_fn(y, tgt):
        e = y - tgt
        return e * (1.0 / D_MODEL), jnp.sum(jnp.sum(e * e, axis=1, keepdims=True), axis=0, keepdims=True)
    dy, sq = rows_call(loss_fn, [y, target], [], [(D_MODEL, F32)], [(1, 1)], tm=512, name="loss")
    loss = sq[0, 0] * (0.5 / D_MODEL)

    dx3, dw1_1, dw2_1, dn2w1, sib1, sib2 = _mlp_bwd(dy, mlp_res1, n2w1, w1_1, w2_1, 1)
    dcat1 = matmul(dx3, w_out1, tb=True, name="wout_dx_1", riders=[sib1])
    dwo_1 = matmul(cat1, dx3, ta=True, out_dtype=BF16, name="wout_dw_1").reshape(N_DEV, OUT_IN // N_DEV, D_MODEL)
    sibo = sibling_rider([dwo_1])
    do1, dgate1, dqm1, dmqw1, dmk1, dmv1 = rows_call(
        functools.partial(vjp_rows(fox_out_fn, 3, (True, True, True)), n_row=3, n_ct=1),
        [o1, (pm1, D_MODEL, 3), (pm1, MEM_WIDTH, 8), dcat1], [mqw1, mk, mv],
        [(D_MODEL, F32), (D_MODEL, BF16), (MEM_WIDTH, BF16)], [(1, HEAD_DIM), (n_mem, MEM_WIDTH), (n_mem, MEM_WIDTH)],
        tm=256, name="fox_out_bwd", riders=[sibo])
    ride_l1 = chips_rider(chip_sums(["w_mlp2_1", "w_mlp1_1", "w_out_1"], [dw2_1, dw1_1, dwo_1],
                                    sib2.results + sib1.results + sibo.results))
    dq1, dk1, dv1, dfq, dfk = fox_attn_bwd(q1, k1, pm1, fq, fk, do1, name="fox_attn_bwd", riders=[ride_l1])
    dqraw1, dkraw1, dqnw, dknw = rows_call(
        functools.partial(vjp_rows(fox_qk_fn, 2, (True, True)), n_row=2, n_ct=2),
        [(pm1, D_MODEL, 0), (pm1, D_MODEL, 1), dq1, dk1], [qnw, knw],
        [(D_MODEL, BF16)] * 2, [(1, HEAD_DIM)] * 2, tm=256, name="fox_qk_bwd")
    dfcum = dfq + jnp.pad(dfk[:, 0, :].T, ((0, 0), (0, LANES - N_HEADS)))
    dps1, dfbias = rows_call(
        functools.partial(vjp_rows(fox_fcum_fn, 1, (True,)), n_row=1, n_ct=1),
        [ps1, dfcum], [fbias], [(LANES, F32)], [(1, LANES)], tm=t, name="fox_fcum_bwd")
    dpm1 = jnp.concatenate([dqraw1, dkraw1, dv1, dgate1, dqm1], axis=1)
    dx2, dn1w1, dwmain1, dwsmall1 = _in_proj_bwd(h1, dpm1, dps1, fox_main, fox_f, x2, dx3, n1w1, "1")
    g_fox = in_proj_pieces(dwmain1, dwsmall1, N_HEADS, FOX_IN)
    sibf = sibling_rider([g_fox])

    dx1, dw1_0, dw2_0, dn2w0, sib1, sib2 = _mlp_bwd(dx2, mlp_res0, n2w0, w1_0, w2_0, 0, riders=[sibf])
    ride_fox_g = chips_rider(chip_sums(["fox_w_in"], [g_fox], sibf.results))
    dcat0 = matmul(dx1, w_out0, tb=True, name="wout_dx_0", riders=[sib1])
    dwo_0 = matmul(cat0, dx1, ta=True, out_dtype=BF16, name="wout_dw_0").reshape(N_DEV, OUT_IN // N_DEV, D_MODEL)
    sibo = sibling_rider([dwo_0])
    do0, dz0, dqm0, donw, dmqw0, dmk0, dmv0 = rows_call(
        functools.partial(vjp_rows(dn_out_fn, 3, (True, True, True, True)), n_row=3, n_ct=1),
        [o0, (pm0, D_MODEL, 3), (pm0, MEM_WIDTH, 8), dcat0], [onw, mqw0, mk, mv],
        [((N_HEADS, HEAD_DIM), F32), (D_MODEL, BF16), (MEM_WIDTH, BF16)],
        [(1, HEAD_DIM), (1, HEAD_DIM), (n_mem, MEM_WIDTH), (n_mem, MEM_WIDTH)], tm=256, name="dn_out_bwd", riders=[sibo])
    h_l0 = chip_sums(["w_mlp2_0", "w_mlp1_0", "w_out_0"], [dw2_0, dw1_0, dwo_0], sib2.results + sib1.results + sibo.results)
    ride_l0_mlp2, ride_l0_rest = chips_rider(h_l0[:1]), chips_rider(h_l0[1:])
    dmnw, dwkv, dmknw = memkv_bwd(mem, mnw, w_kv, mknw, dmk0 + dmk1, dmv0 + dmv1, name="memkv_bwd")
    g_kv = dwkv.reshape(N_DEV, D_MODEL // N_DEV, D_MODEL)
    sibk = sibling_rider([g_kv])
    dq_s, dk_s, dg_s, du0, dw0, dqk0 = delta_seq_bwd(q0, k0, gates, u0, w0, qk0, s_start, do0, name="delta_seq_bwd",
                                                     riders=[ride_fox_g, sibk])
    ride_kv_g = chips_rider(chip_sums(["w_mem_kv"], [g_kv], sibk.results))
    dq0, dk0, dv0, dgates = delta_intra_bwd(q0, k0, v0, gates, du0, dw0, dqk0, dq_s, dk_s, dg_s,
                                            name="delta_intra_bwd", riders=[ride_l0_mlp2, ride_kv_g])
    dxq, dxk, dxv, dcq, dck, dcv = dn_prep_bwd(pm0, conv_w, dq0, dk0, dv0, name="dn_prep_bwd", riders=[ride_l0_rest])
    dconv = jnp.concatenate([dcq, dck, dcv], axis=1)
    dps0, dalog, ddtb = rows_call(
        functools.partial(vjp_rows(dn_gates_fn, 1, (True, True)), n_row=1, n_ct=1),
        [ps0, dgates], [alog, dtb], [(LANES, F32)], [(1, LANES)] * 2, tm=512, name="dn_gates_bwd")
    dpm0 = jnp.concatenate([dxq, dxk, dxv, dz0, dqm0], axis=1)
    grad_x, dn1w0, dwmain0, dwsmall0 = _in_proj_bwd(h0, dpm0, dps0, dn_main, dn_ab, x0, dx1, n1w0, "0")
    g_dn = in_proj_pieces(dwmain0, dwsmall0, 2 * N_HEADS, DN_IN)
    g_conv = dconv.reshape(CONV_WIDTH, N_DEV, -1).transpose(1, 0, 2).astype(BF16)

    g["mem_norm_w"] = dmnw[0]
    g["mem_k_norm_w"] = dmknw[0]
    g["norm1_w"] = jnp.concatenate([dn1w0, dn1w1], axis=0)
    g["dn_a_log"] = dalog[:, :N_HEADS]
    g["dn_dt_bias"] = ddtb[:, :N_HEADS]
    g["dn_o_norm_w"] = donw
    g["fox_f_bias"] = dfbias[:, :N_HEADS]
    g["fox_q_norm_w"] = dqnw
    g["fox_k_norm_w"] = dknw
    g["memq_norm_w"] = jnp.concatenate([dmqw0, dmqw1], axis=0)
    g["norm2_w"] = jnp.concatenate([dn2w0, dn2w1], axis=0)

    sibd = sibling_rider([g_dn, g_conv])
    run_riders([sibd], name="grads_to_sibling_last")
    ride_last = chips_rider(chip_sums(["dn_w_in", "dn_conv_w"], [g_dn, g_conv], sibd.results))
    ride_small = gather_rider([pack_small(g, last=loss)])
    run_riders([ride_last, ride_small], name="grads_to_chips_last")

    def layers(l0, l1):
        return jnp.stack([l0, l1], axis=1).reshape(4, -1, l0.shape[-1])

    parts = {
        "w_mlp1": layers(ride_l0_rest.results[0], ride_l1.results[1]),
        "w_mlp2": layers(ride_l0_mlp2.results[0], ride_l1.results[0]),
        "w_out": layers(ride_l0_rest.results[1], ride_l1.results[2]),
        "fox_w_in": ride_fox_g.results[0], "w_mem_kv": ride_kv_g.results[0],
        "dn_w_in": ride_last.results[0], "dn_conv_w": ride_last.results[1],
    }
    out = {n: adamw(parts[n], w[n], m[n], v[n], name=f"adamw_{n}") for n, _, _ in BIG}
    small = adamw(ride_small.results[0], pack_small(w), pack_small(m), pack_small(v), name="adamw_small")
    loss = small[0][-1, -1]
    return loss, grad_x, out, small


WEIGHTS = ["mem_norm_w", "w_mem_kv", "mem_k_norm_w", "norm1_w", "dn_w_in", "dn_conv_w", "dn_a_log", "dn_dt_bias",
           "dn_o_norm_w", "fox_w_in", "fox_f_bias", "fox_q_norm_w", "fox_k_norm_w", "memq_norm_w", "w_out", "norm2_w",
           "w_mlp1", "w_mlp2"]
DN_IN = 4 * D_MODEL + 2 * N_HEADS + MEM_WIDTH
FOX_IN = 4 * D_MODEL + N_HEADS + MEM_WIDTH
GATE_END = 4 * D_MODEL
OUT_IN = D_MODEL + MEM_WIDTH
BIG = [("w_mem_kv", D_MODEL // N_DEV, D_MODEL), ("dn_w_in", D_MODEL, DN_IN // N_DEV), ("fox_w_in", D_MODEL, FOX_IN // N_DEV),
       ("dn_conv_w", CONV_WIDTH, 3 * D_MODEL // N_DEV), ("w_out", 2 * OUT_IN // N_DEV, D_MODEL),
       ("w_mlp1", 2 * D_MODEL, FF_PIECE), ("w_mlp2", 2 * FF_PIECE, D_MODEL)]
SMALL_TILE = 8 * LANES
SMALL = [(name, shape, -(-math.prod(shape) // SMALL_TILE) * SMALL_TILE) for name, shape in [
    ("mem_norm_w", (D_MODEL,)), ("mem_k_norm_w", (HEAD_DIM,)), ("norm1_w", (2, D_MODEL)), ("dn_a_log", (1, N_HEADS)),
    ("dn_dt_bias", (1, N_HEADS)), ("dn_o_norm_w", (1, HEAD_DIM)), ("fox_f_bias", (1, N_HEADS)),
    ("fox_q_norm_w", (1, HEAD_DIM)), ("fox_k_norm_w", (1, HEAD_DIM)), ("memq_norm_w", (2, HEAD_DIM)), ("norm2_w", (2, D_MODEL))]]
SMALL_ROWS = sum(ln for _, _, ln in SMALL) // LANES + 8


def pack_small(p, last=None):
    def rows(a, ln):
        a = a.reshape(-1)
        return (a if a.shape[0] == ln else jnp.pad(a, (0, ln - a.shape[0]))).reshape(-1, LANES)

    used = sum(ln for _, _, ln in SMALL) // LANES
    tail = jnp.zeros(((SMALL_ROWS - used) * LANES,), F32)
    if last is not None:
        tail = jnp.concatenate([tail[:-1], last.reshape(1)])
    return jnp.concatenate([rows(p[n], ln) for n, _, ln in SMALL] + [tail.reshape(-1, LANES)], axis=0)


def unpack_small(pk):
    row, out = 0, {}
    for n, sh, ln in SMALL:
        out[n] = pk[row:row + ln // LANES].reshape(-1)[:math.prod(sh)].reshape(sh)
        row += ln // LANES
    return out


def in_proj_weights(gathered, width, n_small):
    full = gathered.transpose(1, 0, 2).reshape(D_MODEL, width)
    main = jnp.concatenate([full[:, :GATE_END], full[:, GATE_END + n_small:]], axis=1)
    return main, jnp.pad(full[:, GATE_END:GATE_END + n_small], ((0, 0), (0, LANES - n_small)))


def in_proj_pieces(d_main, d_small, n_small, width):
    full = jnp.concatenate([d_main[:, :GATE_END], d_small[:, :n_small], d_main[:, GATE_END:]], axis=1)
    return full.reshape(D_MODEL, N_DEV, width // N_DEV).transpose(1, 0, 2)


def adamw(parts, w, m, v, *, name):
    n, _, cols = parts.shape
    layers = w.shape[0] if w.ndim == 3 else 1
    rows = w.shape[-2]
    tile = _pick(rows, (512, 256, 128))
    steps = rows // tile

    def body(p_ref, w_ref, m_ref, v_ref, g_ref, d_ref, mo_ref, vo_ref):
        g = p_ref[0].astype(F32)
        for i in range(1, n):
            g = g + p_ref[i].astype(F32)
        m_new = ADAM_B1 * m_ref[...] + (1.0 - ADAM_B1) * g
        v_new = ADAM_B2 * v_ref[...] + (1.0 - ADAM_B2) * jnp.square(g)
        m_hat = m_new / (1.0 - ADAM_B1 ** ADAM_STEP)
        v_hat = v_new / (1.0 - ADAM_B2 ** ADAM_STEP)
        g_ref[...] = g
        d_ref[...] = -ADAM_LR * (m_hat / (jnp.sqrt(v_hat) + ADAM_EPS) + ADAM_WD * w_ref[...])
        mo_ref[...] = m_new
        vo_ref[...] = v_new

    if w.ndim == 3:
        spec = pl.BlockSpec((None, tile, cols), lambda l, i: (l, i, 0))
    else:
        spec = pl.BlockSpec((tile, cols), lambda l, i: (i, 0))
    return pl.pallas_call(
        body, name=name, grid=(layers, steps),
        in_specs=[pl.BlockSpec((n, tile, cols), lambda l, i: (0, l * steps + i, 0)), spec, spec, spec], out_specs=[spec] * 4,
        out_shape=[jax.ShapeDtypeStruct(w.shape, F32)] * 4, compiler_params=_params(("parallel", "parallel")),
    )(parts, w, m, v)


def kernel(x, mem, mem_norm_w, w_mem_kv, mem_k_norm_w, norm1_w, dn_w_in, dn_conv_w, dn_a_log, dn_dt_bias, dn_o_norm_w, fox_w_in, fox_f_bias, fox_q_norm_w, fox_k_norm_w, memq_norm_w, w_out, norm2_w, w_mlp1, w_mlp2, loss_target, m_mem_norm_w, m_w_mem_kv, m_mem_k_norm_w, m_norm1_w, m_dn_w_in, m_dn_conv_w, m_dn_a_log, m_dn_dt_bias, m_dn_o_norm_w, m_fox_w_in, m_fox_f_bias, m_fox_q_norm_w, m_fox_k_norm_w, m_memq_norm_w, m_w_out, m_norm2_w, m_w_mlp1, m_w_mlp2, v_mem_norm_w, v_w_mem_kv, v_mem_k_norm_w, v_norm1_w, v_dn_w_in, v_dn_conv_w, v_dn_a_log, v_dn_dt_bias, v_dn_o_norm_w, v_fox_w_in, v_fox_f_bias, v_fox_q_norm_w, v_fox_k_norm_w, v_memq_norm_w, v_w_out, v_norm2_w, v_w_mlp1, v_w_mlp2):
    p = dict(mem_norm_w=mem_norm_w, w_mem_kv=w_mem_kv, mem_k_norm_w=mem_k_norm_w, norm1_w=norm1_w, dn_w_in=dn_w_in,
             dn_conv_w=dn_conv_w, dn_a_log=dn_a_log, dn_dt_bias=dn_dt_bias, dn_o_norm_w=dn_o_norm_w, fox_w_in=fox_w_in,
             fox_f_bias=fox_f_bias, fox_q_norm_w=fox_q_norm_w, fox_k_norm_w=fox_k_norm_w, memq_norm_w=memq_norm_w,
             w_out=w_out, norm2_w=norm2_w, w_mlp1=w_mlp1, w_mlp2=w_mlp2)
    pm = dict(mem_norm_w=m_mem_norm_w, w_mem_kv=m_w_mem_kv, mem_k_norm_w=m_mem_k_norm_w, norm1_w=m_norm1_w,
              dn_w_in=m_dn_w_in, dn_conv_w=m_dn_conv_w, dn_a_log=m_dn_a_log, dn_dt_bias=m_dn_dt_bias,
              dn_o_norm_w=m_dn_o_norm_w, fox_w_in=m_fox_w_in, fox_f_bias=m_fox_f_bias, fox_q_norm_w=m_fox_q_norm_w,
              fox_k_norm_w=m_fox_k_norm_w, memq_norm_w=m_memq_norm_w, w_out=m_w_out, norm2_w=m_norm2_w, w_mlp1=m_w_mlp1,
              w_mlp2=m_w_mlp2)
    pv = dict(mem_norm_w=v_mem_norm_w, w_mem_kv=v_w_mem_kv, mem_k_norm_w=v_mem_k_norm_w, norm1_w=v_norm1_w,
              dn_w_in=v_dn_w_in, dn_conv_w=v_dn_conv_w, dn_a_log=v_dn_a_log, dn_dt_bias=v_dn_dt_bias,
              dn_o_norm_w=v_dn_o_norm_w, fox_w_in=v_fox_w_in, fox_f_bias=v_fox_f_bias, fox_q_norm_w=v_fox_q_norm_w,
              fox_k_norm_w=v_fox_k_norm_w, memq_norm_w=v_memq_norm_w, w_out=v_w_out, norm2_w=v_norm2_w, w_mlp1=v_w_mlp1,
              w_mlp2=v_w_mlp2)

    loss, grad_x, results, small = local_step(x[0], mem[0], loss_target[0], p, pm, pv)
    small = [unpack_small(o) for o in small]
    groups = [{**small[i], **{n: r[i] for n, r in results.items()}} for i in range(4)]
    return (loss, grad_x[None], *[grp[n] for grp in groups for n in WEIGHTS])
```

```python
import functools
import math

import jax
import jax.numpy as jnp
from jax import lax
from jax.experimental import pallas as pl
from jax.experimental.pallas import tpu as pltpu

F32 = jnp.float32
BF16 = jnp.bfloat16
HIGHEST = lax.Precision.HIGHEST

D_MODEL = 1024
HEAD_DIM = 128
N_HEADS = 8
MEM_HEADS = 4
MEM_WIDTH = MEM_HEADS * HEAD_DIM
D_FF = 4 * D_MODEL
CONV_WIDTH = 4
CHUNK = 64
Q_BLOCK = 128
EPS = 1e-6
SCALE = HEAD_DIM ** -0.5
MAIN_WIDTH = 4 * D_MODEL + MEM_WIDTH
LANES = 128
N_DEV = 8

ADAM_LR = 0.001
ADAM_B1 = 0.9
ADAM_B2 = 0.999
ADAM_EPS = 1e-08
ADAM_WD = 0.01
ADAM_STEP = 10

VMEM_LIMIT = 56 * 2 ** 20
MESH = pl.DeviceIdType.MESH


def _bdot(a, b, dims):
    return lax.dot_general(a.astype(BF16), b.astype(BF16), (dims, ((), ())), preferred_element_type=F32)


@jax.custom_vjp
def mm(a, b):
    return _bdot(a, b, ((1,), (0,)))


@jax.custom_vjp
def mm_nt(a, b):
    return _bdot(a, b, ((1,), (1,)))


@jax.custom_vjp
def mm_tn(a, b):
    return _bdot(a, b, ((0,), (0,)))


mm.defvjp(lambda a, b: (mm(a, b), (a, b)), lambda r, g: (mm_nt(g, r[1]), mm_tn(r[0], g)))
mm_nt.defvjp(lambda a, b: (mm_nt(a, b), (a, b)), lambda r, g: (mm(g, r[1]), mm_tn(g, r[0])))
mm_tn.defvjp(lambda a, b: (mm_tn(a, b), (a, b)), lambda r, g: (mm_nt(r[1], g), mm(r[0], g)))


def hdot(a, b):
    return jnp.dot(a, b, precision=HIGHEST, preferred_element_type=F32)


def rms(x, w):
    return x * lax.rsqrt(jnp.mean(x * x, axis=-1, keepdims=True) + EPS) * w


def l2n(x):
    return x * lax.rsqrt(jnp.sum(x * x, axis=-1, keepdims=True) + EPS)


def _iota2(n, m):
    return lax.broadcasted_iota(jnp.int32, (n, m), 0), lax.broadcasted_iota(jnp.int32, (n, m), 1)


def _lower_ones(n):
    r, c = _iota2(n, n)
    return jnp.where(r >= c, 1.0, 0.0).astype(F32)


def _last_row(x):
    r = lax.broadcasted_iota(jnp.int32, x.shape, 0)
    return jnp.sum(jnp.where(r == x.shape[0] - 1, x, 0.0), axis=0, keepdims=True)


def _softmax_rows(z):
    m = lax.stop_gradient(jnp.max(z, axis=-1, keepdims=True))
    e = jnp.exp(z - m)
    return e * (1.0 / jnp.sum(e, axis=-1, keepdims=True))


_BNN = (((2,), (1,)), ((0,), (0,)))
_BNT = (((2,), (2,)), ((0,), (0,)))
_BTN = (((1,), (1,)), ((0,), (0,)))


def _bbdot(a, b, dims):
    return lax.dot_general(a.astype(BF16), b.astype(BF16), dims, preferred_element_type=F32)


@jax.custom_vjp
def bmm(a, b):
    return _bbdot(a, b, _BNN)


@jax.custom_vjp
def bmm_nt(a, b):
    return _bbdot(a, b, _BNT)


@jax.custom_vjp
def bmm_tn(a, b):
    return _bbdot(a, b, _BTN)


@jax.custom_vjp
def bmm_high(a, b):
    return lax.dot_general(a, b, _BNN, precision=lax.Precision.HIGH, preferred_element_type=F32)


bmm.defvjp(lambda a, b: (bmm(a, b), (a, b)), lambda r, g: (bmm_nt(g, r[1]), bmm_tn(r[0], g)))
bmm_nt.defvjp(lambda a, b: (bmm_nt(a, b), (a, b)), lambda r, g: (bmm(g, r[1]), bmm_tn(g, r[0])))
bmm_tn.defvjp(lambda a, b: (bmm_tn(a, b), (a, b)), lambda r, g: (bmm_nt(r[1], g), bmm(r[0], g)))
bmm_high.defvjp(lambda a, b: (bmm_high(a, b), (a, b)), lambda r, g: (bmm_nt(g, r[1]), bmm_tn(r[0], g)))

NEUMANN_HIGH_LEVELS = 2


@jax.custom_vjp
def inv_unit_lower(a):
    n = a.shape[-1]
    r, c = _iota2(n, n)
    p = jnp.where(r == c, 1.0, 0.0).astype(F32) - a
    ak = a
    for level in range(int(math.log2(n)) - 1):
        dot = bmm_high if level < NEUMANN_HIGH_LEVELS else bmm
        ak = dot(ak, ak)
        p = p + dot(p, ak)
    return p


def _inv_unit_lower_fwd(a):
    t = inv_unit_lower(a)
    return t, t


def _inv_unit_lower_bwd(t, g):
    return (-bmm_tn(t, bmm_nt(g, t)),)


inv_unit_lower.defvjp(_inv_unit_lower_fwd, _inv_unit_lower_bwd)


def delta_intra(q, k, v, gc, beta):
    b, c, _ = q.shape
    r, cc = _iota2(c, c)
    causal = r >= cc
    strict = r > cc
    gi = jnp.broadcast_to(gc, (b, c, c))
    gj = jnp.swapaxes(gi, 1, 2)
    decay = jnp.where(causal, jnp.exp(jnp.where(causal, gi - gj, 0.0)), 0.0)
    kb = k * beta
    a = jnp.where(strict, bmm_nt(kb, k) * decay, 0.0)
    t = inv_unit_lower(a)
    u = bmm(t, v * beta)
    w = bmm(t, kb * jnp.exp(gc))
    qk = jnp.where(causal, bmm_nt(q, k) * decay, 0.0)
    return u, w, qk


def delta_step(s, q, k, gc, u, w, qk):
    v_new = u - bmm(w, s)
    out = bmm(q * jnp.exp(gc), s) + bmm(qk, v_new)
    r = lax.broadcasted_iota(jnp.int32, gc.shape, 1)
    g_last = jnp.sum(jnp.where(r == gc.shape[1] - 1, gc, 0.0), axis=1, keepdims=True)
    k_dec = k * jnp.exp(g_last - gc)
    s_new = s * jnp.exp(g_last) + bmm_tn(k_dec, v_new)
    return out, s_new


def fox_probs(q, k, fq, fk, qpos0):
    s = lax.dot_general(q, k, (((1,), (1,)), ((), ())), preferred_element_type=F32)
    r, c = _iota2(s.shape[0], s.shape[1])
    return _softmax_rows(jnp.where(c <= (r + qpos0), s + (fq - fk), -jnp.inf))


def mem_head(qm, wq, mk, mv):
    p = _softmax_rows(mm_nt(rms(qm, wq) * SCALE, mk))
    return mm(p, mv)


def _heads(x, n):
    return [x[:, h * HEAD_DIM:(h + 1) * HEAD_DIM] for h in range(n)]


def memkv_fn(mem, mnw, wkv, mknw):
    kv = mm(rms(mem, mnw), wkv)
    mk = jnp.concatenate([rms(kh, mknw) for kh in _heads(kv[:, :MEM_WIDTH], MEM_HEADS)], axis=1)
    return mk, kv[:, MEM_WIDTH:]


def dn_gates_fn(ab, alog, dtb):
    g = -jnp.exp(alog) * jax.nn.softplus(ab + dtb)
    low = _lower_ones(CHUNK)
    gc = jnp.concatenate([hdot(low, g[i * CHUNK:(i + 1) * CHUNK]) for i in range(ab.shape[0] // CHUNK)], axis=0)
    lane = lax.broadcasted_iota(jnp.int32, ab.shape, 1)
    return jnp.where(lane < N_HEADS, gc, jax.nn.sigmoid(ab))


def fox_fcum_fn(fp, fbias):
    lf = jax.nn.log_sigmoid(fp + fbias)
    low = _lower_ones(LANES)
    carry = jnp.zeros((1, fp.shape[1]), F32)
    outs = []
    for i in range(fp.shape[0] // LANES):
        cs = hdot(low, lf[i * LANES:(i + 1) * LANES]) + carry
        carry = _last_row(cs)
        outs.append(cs)
    return jnp.concatenate(outs, axis=0)


def fox_qk_fn(qraw, kraw, qnw, knw):
    q = jnp.concatenate([rms(x, qnw) * SCALE for x in _heads(qraw, N_HEADS)], axis=1)
    k = jnp.concatenate([rms(x, knw) for x in _heads(kraw, N_HEADS)], axis=1)
    return q, k


def _mem_out(qm, mqw, mk, mv):
    return [mem_head(a, mqw, b, c) for a, b, c in zip(_heads(qm, MEM_HEADS), _heads(mk, MEM_HEADS), _heads(mv, MEM_HEADS))]


def dn_out_fn(o, z, qm, onw, mqw, mk, mv):
    mix = [rms(a, onw) * jax.nn.silu(b) for a, b in zip(o, _heads(z, N_HEADS))]
    return jnp.concatenate(mix + _mem_out(qm, mqw, mk, mv), axis=1)


def fox_out_fn(o, gate, qm, mqw, mk, mv):
    return jnp.concatenate([o * jax.nn.sigmoid(gate)] + _mem_out(qm, mqw, mk, mv), axis=1)


_HBM = pl.BlockSpec(memory_space=pltpu.HBM)


def _place():
    return lax.axis_index("x"), lax.axis_index("y"), lax.axis_index("c")


class Rider:
    def __init__(self, ins, out_shape, scratch, start, finish, relay=None):
        self.ins, self.out_shape, self.scratch = list(ins), list(out_shape), list(scratch)
        self.start, self.finish, self.relay = start, finish, relay or (lambda *refs: None)
        self.results = None


def gather_rider(xs):
    n = len(xs)

    def plan(x_refs, out_refs, sems):
        send_sems, recv_sems, local_sems = sems
        x, y, c = _place()
        me, sibling = (x, y, c), (x, y, 1 - c)
        chips = [(1 - x, y), (x, 1 - y), (1 - x, 1 - y)]

        def copy(a, k, block, to, src=None):
            px, py, pc = block
            dst = out_refs[a].at[4 * px + 2 * py + pc]
            return pltpu.make_async_remote_copy(
                src_ref=dst if src is None else src, dst_ref=dst,
                send_sem=send_sems.at[a, k], recv_sem=recv_sems.at[a, k], device_id=to, device_id_type=MESH)

        mine = [pltpu.make_async_copy(x_refs[a], out_refs[a].at[4 * x + 2 * y + c], local_sems.at[a]) for a in range(n)]
        first = [copy(a, 0, me, sibling, src=x_refs[a]) for a in range(n)]
        first += [copy(a, 1 + j, me, (*chip, c), src=x_refs[a]) for j, chip in enumerate(chips) for a in range(n)]
        return copy, me, sibling, chips, mine, first

    def start(x_refs, out_refs, sems):
        _, _, _, _, mine, first = plan(x_refs, out_refs, sems)
        for cp in mine + first:
            cp.start()

    def relay(x_refs, out_refs, sems):
        copy, me, sibling, chips, _, _ = plan(x_refs, out_refs, sems)
        _, _, c = me
        for j, chip in enumerate(chips):
            for a in range(n):
                copy(a, 1 + j, (*chip, c), me).wait_recv()
                copy(a, 4 + j, (*chip, c), sibling).start()

    def finish(x_refs, out_refs, sems):
        copy, me, sibling, chips, mine, first = plan(x_refs, out_refs, sems)
        _, _, c = me
        passed = [copy(a, 4 + j, (*chip, c), sibling) for j, chip in enumerate(chips) for a in range(n)]
        for a in range(n):
            copy(a, 0, sibling, me).wait_recv()
        for j, chip in enumerate(chips):
            for a in range(n):
                copy(a, 4 + j, (*chip, 1 - c), me).wait_recv()
        for cp in first + passed:
            cp.wait_send()
        for cp in mine:
            cp.wait()

    return Rider(xs, [jax.ShapeDtypeStruct((N_DEV,) + a.shape, a.dtype) for a in xs],
                 [pltpu.SemaphoreType.DMA((n, 7)), pltpu.SemaphoreType.DMA((n, 7)), pltpu.SemaphoreType.DMA((n,))],
                 start, finish, relay)


def sibling_rider(gs):
    n = len(gs)

    def plan(g_refs, out_refs, sems):
        send_sems, recv_sems = sems
        x, y, c = _place()
        return [pltpu.make_async_remote_copy(
            src_ref=g_refs[a].at[2 * k + 1 - c], dst_ref=out_refs[a].at[k], send_sem=send_sems.at[a, k],
            recv_sem=recv_sems.at[a, k], device_id=(x, y, 1 - c), device_id_type=MESH) for a in range(n) for k in range(4)]

    def start(g_refs, out_refs, sems):
        for cp in plan(g_refs, out_refs, sems):
            cp.start()

    def finish(g_refs, out_refs, sems):
        copies = plan(g_refs, out_refs, sems)
        for cp in copies:
            cp.wait_recv()
        for cp in copies:
            cp.wait_send()

    return Rider(gs, [jax.ShapeDtypeStruct((4,) + g.shape[1:], g.dtype) for g in gs],
                 [pltpu.SemaphoreType.DMA((n, 4)), pltpu.SemaphoreType.DMA((n, 4))], start, finish)


def chips_rider(hs):
    n = len(hs)

    def plan(h_refs, out_refs, sems):
        send_sems, recv_sems, local_sems = sems
        x, y, c = _place()
        mine = 2 * x + y
        chips = [(1 - x, y), (x, 1 - y), (1 - x, 1 - y)]
        keep = [pltpu.make_async_copy(h_refs[a].at[mine], out_refs[a].at[mine], local_sems.at[a]) for a in range(n)]
        sends = [pltpu.make_async_remote_copy(
            src_ref=h_refs[a].at[2 * qx + qy], dst_ref=out_refs[a].at[mine], send_sem=send_sems.at[a, j],
            recv_sem=recv_sems.at[a, j], device_id=(qx, qy, c), device_id_type=MESH)
            for j, (qx, qy) in enumerate(chips) for a in range(n)]
        recvs = [pltpu.make_async_remote_copy(
            src_ref=h_refs[a].at[mine], dst_ref=out_refs[a].at[2 * qx + qy], send_sem=send_sems.at[a, j],
            recv_sem=recv_sems.at[a, j], device_id=(qx, qy, c), device_id_type=MESH)
            for j, (qx, qy) in enumerate(chips) for a in range(n)]
        return keep, sends, recvs

    def start(h_refs, out_refs, sems):
        keep, sends, _ = plan(h_refs, out_refs, sems)
        for cp in keep + sends:
            cp.start()

    def finish(h_refs, out_refs, sems):
        keep, sends, recvs = plan(h_refs, out_refs, sems)
        for cp in recvs:
            cp.wait_recv()
        for cp in sends:
            cp.wait_send()
        for cp in keep:
            cp.wait()

    return Rider(hs, [jax.ShapeDtypeStruct(h.shape, h.dtype) for h in hs],
                 [pltpu.SemaphoreType.DMA((n, 3)), pltpu.SemaphoreType.DMA((n, 3)), pltpu.SemaphoreType.DMA((n,))], start, finish)


def hosted_call(riders, body, *, out_shape, in_specs, out_specs, grid=(), scratch_shapes=(), **kw):
    riders = tuple(riders or ())
    if not riders:
        return pl.pallas_call(body, out_shape=out_shape, in_specs=in_specs, out_specs=out_specs, grid=grid,
                              scratch_shapes=scratch_shapes, **kw)
    single = not isinstance(out_shape, (list, tuple))
    k_out_shape = [out_shape] if single else list(out_shape)
    k_out_specs = [out_specs] if single else list(out_specs)
    n_in, n_out, n_scr = len(in_specs), len(k_out_shape), len(scratch_shapes)
    r_ins = [a for r in riders for a in r.ins]
    r_outs = [s for r in riders for s in r.out_shape]
    r_scr = [s for r in riders for s in r.scratch]

    def full_body(*refs):
        ins = refs[:n_in + len(r_ins)]
        outs = refs[n_in + len(r_ins):n_in + len(r_ins) + n_out + len(r_outs)]
        scr = refs[n_in + len(r_ins) + n_out + len(r_outs):]
        steps = math.prod(grid)
        step = 0
        for d, g in enumerate(grid):
            step = step * g + pl.program_id(d)

        def each(method):
            i0, o0, s0 = n_in, n_out, n_scr
            for r in riders:
                getattr(r, method)(ins[i0:i0 + len(r.ins)], outs[o0:o0 + len(r.out_shape)], scr[s0:s0 + len(r.scratch)])
                i0, o0, s0 = i0 + len(r.ins), o0 + len(r.out_shape), s0 + len(r.scratch)

        def end():
            each("relay")
            each("finish")

        if steps == 1:
            each("start")
            body(*ins[:n_in], *outs[:n_out], *scr[:n_scr])
            end()
        else:
            pl.when(step == 0)(lambda: each("start"))
            body(*ins[:n_in], *outs[:n_out], *scr[:n_scr])
            pl.when(step == steps - 1)(end)

    call = pl.pallas_call(
        full_body, out_shape=k_out_shape + r_outs, in_specs=list(in_specs) + [_HBM] * len(r_ins),
        out_specs=k_out_specs + [_HBM] * len(r_outs), grid=grid, scratch_shapes=list(scratch_shapes) + r_scr, **kw)

    def run(*args):
        res = call(*args, *r_ins)
        o0 = n_out
        for r in riders:
            r.results = list(res[o0:o0 + len(r.out_shape)])
            o0 += len(r.out_shape)
        return res[0] if single else list(res[:n_out])

    return run


def run_riders(riders, *, name):
    hosted_call(riders, lambda: None, name=name, out_shape=[], in_specs=[], out_specs=[])()
    return [r.results for r in riders]


def _pick(n, cands):
    for c in cands:
        if n % c == 0:
            return c
    return n


def _params(sem):
    return pltpu.CompilerParams(dimension_semantics=sem, vmem_limit_bytes=VMEM_LIMIT)


MATMUL_VMEM_BUDGET = 40 * 2 ** 20


def _matmul_tiles(m, n, k, bytes_a, bytes_b, bytes_mn, fixed):
    fm, fn, fk = fixed if fixed is not None else (None, None, None)

    def options(given, size, cands):
        return [given] if given else ([c for c in cands if size % c == 0] or [size])

    best = None
    for tm in options(fm, m, (2048, 1024, 512, 256, 128)):
        for tn in options(fn, n, (512, 256, 128)):
            for tk in options(fk, k, (2048, 1536, 1024, 512, 256, 128)):
                if 2 * (tm * tk * bytes_a + tk * tn * bytes_b + tm * tn * bytes_mn) + tm * tn * 4 > MATMUL_VMEM_BUDGET:
                    continue
                key = ((m // tm) * (n // tn) * (k // tk), -tk)
                if best is None or key < best[0]:
                    best = (key, (tm, tn, tk))
    assert best is not None, (m, n, k, fixed)
    return best[1]


def matmul(a, b, *, name, ta=False, tb=False, post=None, post_ins=(), row_ins=(), acc=False, extra_out=None,
           out_dtype=F32, tiles=None, b_view=None, out_view=None, riders=()):
    (k, m) = a.shape if ta else a.shape[::-1]
    (kb, n) = b_view[:2] if b_view is not None else (b.shape[::-1] if tb else b.shape)
    assert k == kb, (a.shape, b.shape, ta, tb)
    bytes_mn = sum(p.dtype.itemsize for p in post_ins) + jnp.dtype(out_dtype).itemsize
    bytes_mn += jnp.dtype(extra_out[1]).itemsize if extra_out else 0
    tm, tn, tk = _matmul_tiles(m, n, k, a.dtype.itemsize, b.dtype.itemsize, bytes_mn, tiles)
    assert not acc or tn == n, (name, tn, n)
    nk = k // tk
    dims = ((0,) if ta else (1,), (1,) if tb else (0,))
    n_post, n_row = len(post_ins), len(row_ins)
    n_out = 1 + bool(extra_out) + bool(acc)

    def body(*refs):
        a_ref, b_ref = refs[:2]
        post_refs = refs[2:2 + n_post + n_row]
        o_refs, acc_ref = refs[-1 - n_out:-1], refs[-1]
        first_rows, kk = pl.program_id(0) == 0, pl.program_id(2)

        @pl.when(kk == 0)
        def _():
            acc_ref[...] = jnp.zeros_like(acc_ref)

        b_tile = b_ref[...]
        acc_ref[...] += _bdot(a_ref[...], b_tile.reshape(-1, b_tile.shape[-1]), dims)

        @pl.when(kk == nk - 1)
        def _():
            r = acc_ref[...]
            rows = [p[...] for p in post_refs[n_post:]]
            if post is not None:
                r = post(r, *[p[...] for p in post_refs[:n_post]], *rows)
            if acc:
                r, s = r
                sum_ref = o_refs[-1]

                @pl.when(first_rows)
                def _():
                    sum_ref[...] = s

                @pl.when(jnp.logical_not(first_rows))
                def _():
                    sum_ref[...] += s

            o_refs[0][...] = r.astype(out_dtype)
            if extra_out:
                o_refs[1][...] = extra_out[0](r, *rows).astype(extra_out[1])

    a_spec = pl.BlockSpec((tk, tm), lambda i, j, kk: (kk, i)) if ta else pl.BlockSpec((tm, tk), lambda i, j, kk: (i, kk))
    if b_view is not None:
        b_spec = b_view[2]
    else:
        b_spec = pl.BlockSpec((tn, tk), lambda i, j, kk: (j, kk)) if tb else pl.BlockSpec((tk, tn), lambda i, j, kk: (kk, j))
    mn_spec = pl.BlockSpec((tm, tn), lambda i, j, kk: (i, j))
    row_spec = pl.BlockSpec((1, tn), lambda i, j, kk: (0, j))
    o_shape, o_spec = ((m, n), mn_spec) if out_view is None else out_view
    out_shape = [jax.ShapeDtypeStruct(o_shape, out_dtype)]
    out_specs = [o_spec]
    if extra_out:
        out_shape.append(jax.ShapeDtypeStruct((m, n), extra_out[1]))
        out_specs.append(mn_spec)
    if acc:
        out_shape.append(jax.ShapeDtypeStruct((1, n), F32))
        out_specs.append(row_spec)
    res = hosted_call(
        riders, body, name=name, grid=(m // tm, n // tn, nk),
        in_specs=[a_spec, b_spec] + [mn_spec] * n_post + [row_spec] * n_row, out_specs=out_specs, out_shape=out_shape,
        scratch_shapes=[pltpu.VMEM((tm, tn), F32)],
        compiler_params=_params(("arbitrary" if acc else "parallel", "parallel", "arbitrary")),
    )(a, b, *post_ins, *row_ins)
    return res if n_out > 1 else res[0]


def rows_call(fn, row_ins, full_ins, row_outs, acc_outs, *, tm, name, riders=()):
    row_ins = [r if isinstance(r, tuple) else (r, r.shape[-1], 0) for r in row_ins]
    t = row_ins[0][0].shape[-2]
    tm = min(tm, t)
    n_in = len(row_ins) + len(full_ins)
    n_row = len(row_outs)

    def body(*refs):
        res = fn(*[[r[h] for h in range(r.shape[0])] if (i < len(row_ins) and len(r.shape) == 3) else r[...]
                   for i, r in enumerate(refs[:n_in])])
        res = res if isinstance(res, (tuple, list)) else (res,)
        outs = refs[n_in:]
        for ref, val in zip(outs[:n_row], res[:n_row]):
            if len(ref.shape) == 3:
                for h, vh in enumerate(val):
                    ref[h] = vh.astype(ref.dtype)
            else:
                ref[...] = val.astype(ref.dtype)
        first = pl.program_id(0) == 0
        for ref, val in zip(outs[n_row:], res[n_row:]):
            @pl.when(first)
            def _(ref=ref, val=val):
                ref[...] = val

            @pl.when(jnp.logical_not(first))
            def _(ref=ref, val=val):
                ref[...] += val

    def full_spec(shape):
        return pl.BlockSpec(shape, lambda i, nd=len(shape): (0,) * nd)

    def row_spec(lead, w, cb):
        if lead is None:
            return pl.BlockSpec((tm, w), lambda i: (i, cb))
        return pl.BlockSpec((lead, tm, w), lambda i: (0, i, cb))

    def lead_cols(c):
        return c if isinstance(c, tuple) else (None, c)

    in_specs = [row_spec(a.shape[0] if a.ndim == 3 else None, w, cb) for (a, w, cb) in row_ins]
    in_specs += [full_spec(f.shape) for f in full_ins]
    out_specs = [row_spec(*lead_cols(c), 0) for c, _ in row_outs] + [full_spec(s) for s in acc_outs]
    out_shape = [jax.ShapeDtypeStruct(tuple(d for d in (lead_cols(c)[0], t, lead_cols(c)[1]) if d is not None), dt)
                 for c, dt in row_outs] + [jax.ShapeDtypeStruct(s, F32) for s in acc_outs]
    res = hosted_call(
        riders, body, name=name, grid=(t // tm,), in_specs=in_specs, out_specs=out_specs, out_shape=out_shape,
        compiler_params=_params(("arbitrary",)),
    )(*[r[0] for r in row_ins], *full_ins)
    return res


def vjp_rows(fn, n_diff_row, row_diff_full):
    def bwd(*args, n_row, n_ct):
        prim_rows = args[:n_row]
        cts = args[n_row:n_row + n_ct]
        fulls = args[n_row + n_ct:]
        _, vjp = jax.vjp(fn, *prim_rows, *fulls)
        g = vjp(cts[0] if n_ct == 1 else tuple(cts))
        out = list(g[:n_diff_row])
        out += [gf for gf, d in zip(g[n_row:], row_diff_full) if d]
        return tuple(out)
    return bwd


def _shift_down(x, s):
    if s == 0:
        return x
    t = lax.broadcasted_iota(jnp.int32, x.shape, 0)
    return jnp.where(t >= s, pltpu.roll(x, s, 0), 0.0)


def _shift_up(x, s):
    if s == 0:
        return x
    n = x.shape[0]
    t = lax.broadcasted_iota(jnp.int32, x.shape, 0)
    return jnp.where(t < n - s, pltpu.roll(x, n - s, 0), 0.0)


def _conv(x, w_ref):
    return sum(w_ref[pl.ds(j, 1), :] * _shift_down(x, CONV_WIDTH - 1 - j) for j in range(CONV_WIDTH))


_DN_POST = (lambda c: l2n(jax.nn.silu(c)) * SCALE, lambda c: l2n(jax.nn.silu(c)), jax.nn.silu)


def dn_prep_fwd(proj, conv_w, *, name, riders=()):
    t = proj.shape[0]

    def body(xq, xk, xv, wq, wk, wv, oq, ok, ov):
        for x_ref, w_ref, o_ref, post in zip((xq, xk, xv), (wq, wk, wv), (oq, ok, ov), _DN_POST):
            o_ref[...] = post(_conv(x_ref[...], w_ref))

    x_specs = [pl.BlockSpec((t, HEAD_DIM), lambda h, g=g: (0, g * N_HEADS + h)) for g in range(3)]
    w_specs = [pl.BlockSpec((CONV_WIDTH, HEAD_DIM), lambda h, g=g: (0, g * N_HEADS + h)) for g in range(3)]
    o_spec = pl.BlockSpec((None, t, HEAD_DIM), lambda h: (h, 0, 0))
    return hosted_call(
        riders, body, name=name, grid=(N_HEADS,), in_specs=x_specs + w_specs, out_specs=[o_spec] * 3,
        out_shape=[jax.ShapeDtypeStruct((N_HEADS, t, HEAD_DIM), F32)] * 3, compiler_params=_params(("parallel",)),
    )(proj, proj, proj, conv_w, conv_w, conv_w)


def dn_prep_bwd(proj, conv_w, dq, dk, dv, *, name, riders=()):
    t = proj.shape[0]

    def body(xq, xk, xv, wq, wk, wv, gq, gk, gv, dxq, dxk, dxv, dwq, dwk, dwv):
        for x_ref, w_ref, g_ref, dx_ref, dw_ref, post in zip(
                (xq, xk, xv), (wq, wk, wv), (gq, gk, gv), (dxq, dxk, dxv), (dwq, dwk, dwv), _DN_POST):
            x = x_ref[...]
            _, vjp = jax.vjp(post, _conv(x, w_ref))
            dc, = vjp(g_ref[...])
            dx = sum(w_ref[pl.ds(j, 1), :] * _shift_up(dc, CONV_WIDTH - 1 - j) for j in range(CONV_WIDTH))
            dx_ref[...] = dx.astype(dx_ref.dtype)
            for j in range(CONV_WIDTH):
                dw_ref[pl.ds(j, 1), :] = jnp.sum(dc * _shift_down(x, CONV_WIDTH - 1 - j), axis=0, keepdims=True)

    x_specs = [pl.BlockSpec((t, HEAD_DIM), lambda h, g=g: (0, g * N_HEADS + h)) for g in range(3)]
    w_specs = [pl.BlockSpec((CONV_WIDTH, HEAD_DIM), lambda h, g=g: (0, g * N_HEADS + h)) for g in range(3)]
    g_spec = pl.BlockSpec((None, t, HEAD_DIM), lambda h: (h, 0, 0))
    dx_spec = pl.BlockSpec((t, HEAD_DIM), lambda h: (0, h))
    dw_spec = pl.BlockSpec((CONV_WIDTH, HEAD_DIM), lambda h: (0, h))
    return hosted_call(
        riders, body, name=name, grid=(N_HEADS,), in_specs=x_specs + w_specs + [g_spec] * 3, out_specs=[dx_spec] * 3 + [dw_spec] * 3,
        out_shape=[jax.ShapeDtypeStruct((t, D_MODEL), BF16)] * 3 + [jax.ShapeDtypeStruct((CONV_WIDTH, D_MODEL), F32)] * 3,
        compiler_params=_params(("parallel",)),
    )(proj, proj, proj, conv_w, conv_w, conv_w, dq, dk, dv)


INTRA_CHUNKS = 4


def _lane_column(x, lane_index):
    lane = lax.broadcasted_iota(jnp.int32, x.shape, 1)
    return jnp.sum(jnp.where(lane == lane_index, x, 0.0), axis=1, keepdims=True)


def _head_columns(g, first_lane):
    return jnp.concatenate([_lane_column(g, first_lane + h)[None] for h in range(N_HEADS)], axis=0)


def _intra_of_gates(q, k, v, gates):
    nb = N_HEADS * (gates.shape[0] // CHUNK)

    def chunks(x):
        return x.reshape(nb, CHUNK, x.shape[-1])

    res = delta_intra(chunks(q), chunks(k), chunks(v), chunks(_head_columns(gates, 0)), chunks(_head_columns(gates, N_HEADS)))
    return tuple(x.reshape(N_HEADS, -1, x.shape[-1]) for x in res)


def _step_of_gates(s, q, k, gates, u, w, qk):
    return delta_step(s, q, k, _head_columns(gates, 0), u, w, qk)


def _head_major(rows, w, index):
    return pl.BlockSpec((N_HEADS, rows, w), lambda i: (0, index(i), 0))


def delta_intra_fwd(q, k, v, gates, *, name, riders=()):
    t = q.shape[1]
    rows = min(INTRA_CHUNKS, t // CHUNK) * CHUNK

    def body(q_ref, k_ref, v_ref, g_ref, u_ref, w_ref, qk_ref):
        for ref, val in zip((u_ref, w_ref, qk_ref), _intra_of_gates(q_ref[...], k_ref[...], v_ref[...], g_ref[...])):
            ref[...] = val

    x_spec, qk_spec = (_head_major(rows, w, lambda i: i) for w in (HEAD_DIM, CHUNK))
    g_spec = pl.BlockSpec((rows, LANES), lambda i: (i, 0))
    return hosted_call(
        riders, body, name=name, grid=(t // rows,), in_specs=[x_spec] * 3 + [g_spec], out_specs=[x_spec, x_spec, qk_spec],
        out_shape=[jax.ShapeDtypeStruct((N_HEADS, t, HEAD_DIM), F32)] * 2 + [jax.ShapeDtypeStruct((N_HEADS, t, CHUNK), F32)],
        compiler_params=_params(("parallel",)),
    )(q, k, v, gates)


def delta_seq_fwd(q, k, gates, u, w, qk, *, name, riders=()):
    t = q.shape[1]
    nc = t // CHUNK

    def body(q_ref, k_ref, g_ref, u_ref, w_ref, qk_ref, o_ref, s0_ref, s_ref):
        @pl.when(pl.program_id(0) == 0)
        def _():
            s_ref[...] = jnp.zeros_like(s_ref)

        s = s_ref[...]
        s0_ref[...] = s
        o, s_new = _step_of_gates(s, q_ref[...], k_ref[...], g_ref[...], u_ref[...], w_ref[...], qk_ref[...])
        o_ref[...] = o
        s_ref[...] = s_new

    x_spec, qk_spec = (_head_major(CHUNK, w, lambda c: c) for w in (HEAD_DIM, CHUNK))
    g_spec = pl.BlockSpec((CHUNK, LANES), lambda c: (c, 0))
    s_spec = pl.BlockSpec((N_HEADS, None, HEAD_DIM, HEAD_DIM), lambda c: (0, c, 0, 0))
    return hosted_call(
        riders, body, name=name, grid=(nc,), in_specs=[x_spec, x_spec, g_spec, x_spec, x_spec, qk_spec], out_specs=[x_spec, s_spec],
        out_shape=[jax.ShapeDtypeStruct((N_HEADS, t, HEAD_DIM), F32),
                   jax.ShapeDtypeStruct((N_HEADS, nc, HEAD_DIM, HEAD_DIM), F32)],
        scratch_shapes=[pltpu.VMEM((N_HEADS, HEAD_DIM, HEAD_DIM), F32)],
        compiler_params=_params(("arbitrary",)),
    )(q, k, gates, u, w, qk)


def delta_seq_bwd(q, k, gates, u, w, qk, s0, do, *, name, riders=()):
    t = q.shape[1]
    nc = t // CHUNK

    def body(q_ref, k_ref, g_ref, u_ref, w_ref, qk_ref, s0_ref, do_ref,
             dq_ref, dk_ref, dg_ref, du_ref, dw_ref, dqk_ref, ds_ref):
        @pl.when(pl.program_id(0) == 0)
        def _():
            ds_ref[...] = jnp.zeros_like(ds_ref)

        _, vjp = jax.vjp(_step_of_gates, s0_ref[...], q_ref[...], k_ref[...], g_ref[...], u_ref[...], w_ref[...], qk_ref[...])
        ds, dq, dk, dg, du, dw, dqk = vjp((do_ref[...], ds_ref[...]))
        for ref, val in zip((ds_ref, dq_ref, dk_ref, dg_ref, du_ref, dw_ref, dqk_ref), (ds, dq, dk, dg, du, dw, dqk)):
            ref[...] = val

    x_spec, qk_spec = (_head_major(CHUNK, w, lambda c: nc - 1 - c) for w in (HEAD_DIM, CHUNK))
    g_spec = pl.BlockSpec((CHUNK, LANES), lambda c: (nc - 1 - c, 0))
    s_spec = pl.BlockSpec((N_HEADS, None, HEAD_DIM, HEAD_DIM), lambda c: (0, nc - 1 - c, 0, 0))
    head_shape = [jax.ShapeDtypeStruct((N_HEADS, t, w_), F32) for w_ in (HEAD_DIM, HEAD_DIM, HEAD_DIM, HEAD_DIM, CHUNK)]
    return hosted_call(
        riders, body, name=name, grid=(nc,), in_specs=[x_spec, x_spec, g_spec, x_spec, x_spec, qk_spec, s_spec, x_spec],
        out_specs=[x_spec, x_spec, g_spec, x_spec, x_spec, qk_spec],
        out_shape=head_shape[:2] + [jax.ShapeDtypeStruct((t, LANES), F32)] + head_shape[2:],
        scratch_shapes=[pltpu.VMEM((N_HEADS, HEAD_DIM, HEAD_DIM), F32)],
        compiler_params=_params(("arbitrary",)),
    )(q, k, gates, u, w, qk, s0, do)


def delta_intra_bwd(q, k, v, gates, du, dw, dqk, dq_s, dk_s, dg_s, *, name, riders=()):
    t = q.shape[1]
    rows = min(INTRA_CHUNKS, t // CHUNK) * CHUNK

    def body(q_ref, k_ref, v_ref, g_ref, du_ref, dw_ref, dqk_ref, dqs_ref, dks_ref, dgs_ref, dq_ref, dk_ref, dv_ref, dg_ref):
        _, vjp = jax.vjp(_intra_of_gates, q_ref[...], k_ref[...], v_ref[...], g_ref[...])
        dq, dk, dv, dg = vjp((du_ref[...], dw_ref[...], dqk_ref[...]))
        dq_ref[...] = dq + dqs_ref[...]
        dk_ref[...] = dk + dks_ref[...]
        dv_ref[...] = dv
        dg_ref[...] = dg + dgs_ref[...]

    x_spec, qk_spec = (_head_major(rows, w, lambda i: i) for w in (HEAD_DIM, CHUNK))
    g_spec = pl.BlockSpec((rows, LANES), lambda i: (i, 0))
    return hosted_call(
        riders, body, name=name, grid=(t // rows,),
        in_specs=[x_spec] * 3 + [g_spec, x_spec, x_spec, qk_spec, x_spec, x_spec, g_spec],
        out_specs=[x_spec] * 3 + [g_spec],
        out_shape=[jax.ShapeDtypeStruct((N_HEADS, t, HEAD_DIM), F32)] * 3 + [jax.ShapeDtypeStruct((t, LANES), F32)],
        compiler_params=_params(("parallel",)),
    )(q, k, v, gates, du, dw, dqk, dq_s, dk_s, dg_s)


_V_BLOCK = 2 * N_HEADS
FOX_GROUPS = 16


def _fox_groups(t):
    nq = t // Q_BLOCK
    per = max(1, nq // FOX_GROUPS)
    return [(g0, per, (g0 + per) * Q_BLOCK) for g0 in range(0, nq, per)]


def fox_attn_fwd(q, k, proj, fq, fk, *, name, riders=()):
    t = q.shape[0]

    def body(q_ref, k_ref, v_ref, fq_ref, fk_ref, o_ref, kb_ref, vb_ref):
        head = pl.program_id(0)
        kb_ref[...] = k_ref[...].astype(BF16)
        vb_ref[...] = v_ref[...].astype(BF16)
        for g0, per, keys in _fox_groups(t):
            def block(j, carry, g0=g0, keys=keys):
                rows = pl.ds((g0 + j) * Q_BLOCK, Q_BLOCK)
                p = fox_probs(q_ref[rows, :].astype(BF16), kb_ref[0:keys, :], _lane_column(fq_ref[rows, :], head),
                              fk_ref[:, 0:keys], (g0 + j) * Q_BLOCK)
                o_ref[rows, :] = jnp.dot(p.astype(BF16), vb_ref[0:keys, :], preferred_element_type=F32)
                return carry
            for j in range(per):
                block(j, 0)

    x_spec = pl.BlockSpec((t, HEAD_DIM), lambda h: (0, h))
    v_spec = pl.BlockSpec((t, HEAD_DIM), lambda h: (0, _V_BLOCK + h))
    fq_spec = pl.BlockSpec((t, LANES), lambda h: (0, 0))
    fk_spec = pl.BlockSpec((None, 1, t), lambda h: (h, 0, 0))
    return hosted_call(
        riders, body, name=name, grid=(N_HEADS,), in_specs=[x_spec, x_spec, v_spec, fq_spec, fk_spec], out_specs=x_spec,
        out_shape=jax.ShapeDtypeStruct((t, D_MODEL), F32), scratch_shapes=[pltpu.VMEM((t, HEAD_DIM), BF16)] * 2,
        compiler_params=_params(("parallel",)),
    )(q, k, proj, fq, fk)


def fox_attn_bwd(q, k, proj, fq, fk, do, *, name, riders=()):
    t = q.shape[0]

    def body(q_ref, k_ref, v_ref, fq_ref, fk_ref, do_ref, dq_ref, dk_ref, dv_out_ref, dfq_ref, dfk_ref, kb_ref, vb_ref, dv_ref):
        head = pl.program_id(0)

        @pl.when(head == 0)
        def _():
            dfq_ref[...] = jnp.zeros_like(dfq_ref)

        kb_ref[...] = k_ref[...].astype(BF16)
        vb_ref[...] = v_ref[...].astype(BF16)
        dk_ref[...] = jnp.zeros_like(dk_ref)
        dv_ref[...] = jnp.zeros_like(dv_ref)
        dfk_ref[...] = jnp.zeros_like(dfk_ref)
        nt = (((1,), (1,)), ((), ()))
        tn = (((0,), (0,)), ((), ()))
        for g0, per, keys in _fox_groups(t):
            def block(j, carry, g0=g0, keys=keys):
                rows = pl.ds((g0 + j) * Q_BLOCK, Q_BLOCK)
                qb, dob = q_ref[rows, :].astype(BF16), do_ref[rows, :].astype(BF16)
                kb, vb = kb_ref[0:keys, :], vb_ref[0:keys, :]
                p = fox_probs(qb, kb, _lane_column(fq_ref[rows, :], head), fk_ref[:, 0:keys], (g0 + j) * Q_BLOCK)
                dp = lax.dot_general(dob, vb, nt, preferred_element_type=F32)
                dz = p * (dp - jnp.sum(dp * p, axis=-1, keepdims=True))
                pb, dzb = p.astype(BF16), dz.astype(BF16)
                dq_ref[rows, :] = jnp.dot(dzb, kb, preferred_element_type=F32)
                lane = lax.broadcasted_iota(jnp.int32, (Q_BLOCK, LANES), 1)
                dfq_ref[rows, :] += jnp.where(lane == head, jnp.sum(dz, axis=-1, keepdims=True), 0.0)
                dk_ref[0:keys, :] += lax.dot_general(dzb, qb, tn, preferred_element_type=F32)
                dv_ref[0:keys, :] += lax.dot_general(pb, dob, tn, preferred_element_type=F32)
                dfk_ref[:, 0:keys] -= jnp.sum(dz, axis=0, keepdims=True)
                return carry
            for j in range(per):
                block(j, 0)
        dv_out_ref[...] = dv_ref[...].astype(dv_out_ref.dtype)

    x_spec = pl.BlockSpec((t, HEAD_DIM), lambda h: (0, h))
    v_spec = pl.BlockSpec((t, HEAD_DIM), lambda h: (0, _V_BLOCK + h))
    fq_spec = pl.BlockSpec((t, LANES), lambda h: (0, 0))
    fk_spec = pl.BlockSpec((None, 1, t), lambda h: (h, 0, 0))
    return hosted_call(
        riders, body, name=name, grid=(N_HEADS,), in_specs=[x_spec, x_spec, v_spec, fq_spec, fk_spec, x_spec],
        out_specs=[x_spec, x_spec, x_spec, fq_spec, fk_spec],
        out_shape=[jax.ShapeDtypeStruct((t, D_MODEL), F32)] * 2 + [jax.ShapeDtypeStruct((t, D_MODEL), BF16)]
        + [jax.ShapeDtypeStruct((t, LANES), F32), jax.ShapeDtypeStruct((N_HEADS, 1, t), F32)],
        scratch_shapes=[pltpu.VMEM((t, HEAD_DIM), BF16)] * 2 + [pltpu.VMEM((t, HEAD_DIM), F32)],
        compiler_params=_params(("arbitrary",)),
    )(q, k, proj, fq, fk, do)


def memkv_fwd(mem, mnw, wkv, mknw, *, name):
    n = mem.shape[0]

    def body(mem_ref, mnw_ref, w_ref, mknw_ref, mk_ref, mv_ref):
        mk, mv = memkv_fn(mem_ref[...], mnw_ref[...], w_ref[...], mknw_ref[...])
        mk_ref[...] = mk
        mv_ref[...] = mv

    return pl.pallas_call(
        body, name=name, out_shape=[jax.ShapeDtypeStruct((n, MEM_WIDTH), F32)] * 2,
        compiler_params=pltpu.CompilerParams(vmem_limit_bytes=VMEM_LIMIT),
    )(mem, mnw, wkv, mknw)


def memkv_bwd(mem, mnw, wkv, mknw, dmk, dmv, *, name):
    def body(mem_ref, mnw_ref, w_ref, mknw_ref, dmk_ref, dmv_ref, dmnw_ref, dw_ref, dmknw_ref):
        f = functools.partial(memkv_fn, mem_ref[...])
        _, vjp = jax.vjp(f, mnw_ref[...], w_ref[...].astype(F32), mknw_ref[...])
        dmnw, dw, dmknw = vjp((dmk_ref[...], dmv_ref[...]))
        dmnw_ref[...] = dmnw
        dw_ref[...] = dw.astype(dw_ref.dtype)
        dmknw_ref[...] = dmknw

    return pl.pallas_call(
        body, name=name,
        out_shape=[jax.ShapeDtypeStruct(mnw.shape, F32), jax.ShapeDtypeStruct(wkv.shape, BF16), jax.ShapeDtypeStruct(mknw.shape, F32)],
        compiler_params=pltpu.CompilerParams(vmem_limit_bytes=VMEM_LIMIT),
    )(mem, mnw, wkv, mknw, dmk, dmv)


def _row(v, width=None):
    v = v.reshape(1, -1)
    if width is not None and v.shape[1] < width:
        v = jnp.pad(v, ((0, 0), (0, width - v.shape[1])))
    return v


def _norm_fwd(x, w, name, riders=()):
    return rows_call(lambda x, w: rms(x, w), [x], [w], [(D_MODEL, BF16)], [], tm=512, name=name, riders=riders)[0]


FF_PIECE = D_FF // N_DEV


def _add(r, x, *rows):
    return r + x


def _norm_rows(r, w):
    return rms(r, w)


def _norm_bwd_post(dh, x, dx_in, w):
    _, vjp = jax.vjp(rms, x, w)
    dx, dw = vjp(dh)
    return dx + dx_in, dw


def _piece(rows, cols, index):
    return pl.BlockSpec((None, rows, cols), lambda i, j, kk: (index(i, j, kk), 0, 0))


def _two_pieces(rows, cols, index):
    return pl.BlockSpec((2, rows, cols), lambda i, j, kk: (index(i, j, kk), 0, 0))


def _mlp_fwd(x, h2, w1, w2, layer, riders=(), next_norm_w=None):
    riders = list(riders) + [None, None]
    u, a1 = matmul(h2, w1, name=f"mlp1_fwd_{layer}", tiles=(None, FF_PIECE, D_MODEL),
                   extra_out=(lambda u: jnp.square(jnp.maximum(u, 0.0)), BF16),
                   b_view=(D_MODEL, D_FF, _piece(D_MODEL, FF_PIECE, lambda i, j, kk: j)), riders=riders[0])
    norm = dict(row_ins=[next_norm_w], extra_out=(_norm_rows, BF16)) if next_norm_w is not None else {}
    y = matmul(a1, w2, name=f"mlp2_fwd_{layer}", post=_add, post_ins=[x], tiles=(None, D_MODEL, 2 * FF_PIECE),
               b_view=(D_FF, D_MODEL, _two_pieces(FF_PIECE, D_MODEL, lambda i, j, kk: kk)), riders=riders[1], **norm)
    return y, (x, h2, u, a1)


def pair_sum(g, got, *, name):
    _, rows, cols = g.shape
    tile = _pick(rows, (512, 256, 128))
    c = lax.axis_index("c").astype(jnp.int32).reshape(1)

    def body(c_ref, a_ref, b_ref, o_ref):
        o_ref[...] = (a_ref[...].astype(F32) + b_ref[...].astype(F32)).astype(o_ref.dtype)

    grid_spec = pltpu.PrefetchScalarGridSpec(
        num_scalar_prefetch=1, grid=(4, rows // tile),
        in_specs=[pl.BlockSpec((None, tile, cols), lambda k, i, c_ref: (2 * k + c_ref[0], i, 0)),
                  pl.BlockSpec((None, tile, cols), lambda k, i, c_ref: (k, i, 0))],
        out_specs=pl.BlockSpec((None, tile, cols), lambda k, i, c_ref: (k, i, 0)))
    return pl.pallas_call(
        body, name=name, grid_spec=grid_spec, out_shape=jax.ShapeDtypeStruct((4, rows, cols), g.dtype),
        compiler_params=_params(("parallel", "parallel")),
    )(c, g, got)


def chip_sums(names, pieces, gots):
    return [pair_sum(a, got, name=f"grads_pair_sum_{n}") for n, a, got in zip(names, pieces, gots)]


def _mlp_bwd(dy, res, n2w, w1, w2, layer, riders=()):
    x, h2, u, a1 = res
    du = matmul(dy, w2, tb=True, name=f"mlp2_dx_{layer}", out_dtype=BF16, tiles=(None, 2 * FF_PIECE, D_MODEL),
                post=lambda r, u: r * (2.0 * jnp.maximum(u, 0.0)), post_ins=[u],
                b_view=(D_MODEL, D_FF, _two_pieces(FF_PIECE, D_MODEL, lambda i, j, kk: j)), riders=riders)
    dw2 = matmul(a1, dy, ta=True, name=f"mlp2_dw_{layer}", out_dtype=BF16, tiles=(FF_PIECE, D_MODEL, None), out_view=(
        w2.shape, _piece(FF_PIECE, D_MODEL, lambda i, j, kk: i)))
    sib2 = sibling_rider([dw2])
    dx, dn2w = matmul(du, w1, tb=True, name=f"mlp1_dx_{layer}", tiles=(None, D_MODEL, FF_PIECE),
                      b_view=(D_FF, D_MODEL, _piece(D_MODEL, FF_PIECE, lambda i, j, kk: kk)),
                      post=_norm_bwd_post, post_ins=[x, dy], row_ins=[n2w], acc=True, riders=[sib2])
    dw1 = matmul(h2, du, ta=True, name=f"mlp1_dw_{layer}", out_dtype=BF16, tiles=(D_MODEL, FF_PIECE, None), out_view=(
        w1.shape, _piece(D_MODEL, FF_PIECE, lambda i, j, kk: j)))
    return dx, dw1, dw2, dn2w, sibling_rider([dw1]), sib2


def _in_proj_bwd(h, dmain, dsmall, w_main, w_small, x, dx_in, n1w, tag):
    dh = matmul(dmain, w_main, tb=True, name=f"inproj_dx_main_{tag}")

    def post(r, dh_main, x, dx_in, w):
        return _norm_bwd_post(r + dh_main, x, dx_in, w)

    dx, dn1w = matmul(dsmall, w_small, tb=True, name=f"inproj_dx_small_{tag}", tiles=(None, D_MODEL, None),
                      post=post, post_ins=[dh, x, dx_in], row_ins=[n1w], acc=True)
    dw_main = matmul(h, dmain, ta=True, out_dtype=BF16, name=f"inproj_dw_main_{tag}")
    dw_small = matmul(h, dsmall, ta=True, out_dtype=BF16, name=f"inproj_dw_small_{tag}")
    return dx, dn1w, dw_main, dw_small


def local_step(x, mem, target, w, m, v):
    t = x.shape[0]
    n_mem = mem.shape[0]
    g = {}

    def wire(a):
        return a.astype(BF16)

    (dn_g,), = run_riders([gather_rider([wire(w["dn_w_in"][0])])], name="weights_gather_first")
    dn_main, dn_ab = in_proj_weights(dn_g, DN_IN, 2 * N_HEADS)
    fox_w = wire(w["fox_w_in"][0])
    ride_out = gather_rider([wire(w["w_out"][0]), wire(w["w_out"][1]), w["dn_conv_w"][0]])
    ride_kv = gather_rider([wire(w["w_mem_kv"])])
    ride_mlp1_0 = gather_rider([wire(w["w_mlp1"][0])])
    ride_mlp2_0 = gather_rider([wire(w["w_mlp2"][0])])
    ride_fox_a, ride_fox_b = gather_rider([fox_w[:D_MODEL // 2]]), gather_rider([fox_w[D_MODEL // 2:]])
    ride_mlp_1 = gather_rider([wire(w["w_mlp1"][1]), wire(w["w_mlp2"][1])])
    mnw, mknw = _row(w["mem_norm_w"]), _row(w["mem_k_norm_w"])

    n1w0, n2w0 = _row(w["norm1_w"][0]), _row(w["norm2_w"][0])
    n1w1, n2w1 = _row(w["norm1_w"][1]), _row(w["norm2_w"][1])
    alog, dtb = _row(w["dn_a_log"][0], LANES), _row(w["dn_dt_bias"][0], LANES)
    onw, mqw0 = _row(w["dn_o_norm_w"][0]), _row(w["memq_norm_w"][0])
    x0 = x
    h0 = _norm_fwd(x0, n1w0, "norm1_fwd_0")
    pm0 = matmul(h0, dn_main, name="inproj_main_0", riders=[ride_out])
    w_out0, w_out1 = (a.reshape(OUT_IN, D_MODEL) for a in ride_out.results[:2])
    conv_w = ride_out.results[2].transpose(1, 0, 2).reshape(CONV_WIDTH, 3 * D_MODEL)
    ps0 = matmul(h0, dn_ab, name="inproj_small_0")
    gates = rows_call(dn_gates_fn, [ps0], [alog, dtb], [(LANES, F32)], [], tm=512, name="dn_gates_fwd")[0]
    q0, k0, v0 = dn_prep_fwd(pm0, conv_w, name="dn_prep_fwd", riders=[ride_kv])
    w_kv = ride_kv.results[0].reshape(D_MODEL, D_MODEL)
    mk, mv = memkv_fwd(mem, mnw, w_kv, mknw, name="memkv_fwd")
    u0, w0, qk0 = delta_intra_fwd(q0, k0, v0, gates, name="delta_intra_fwd", riders=[ride_mlp1_0])
    o0, s_start = delta_seq_fwd(q0, k0, gates, u0, w0, qk0, name="delta_seq_fwd", riders=[ride_mlp2_0])
    cat0 = rows_call(dn_out_fn, [o0, (pm0, D_MODEL, 3), (pm0, MEM_WIDTH, 8)], [onw, mqw0, mk, mv],
                     [(D_MODEL + MEM_WIDTH, BF16)], [], tm=256, name="dn_out_fwd")[0]
    (w1_0,), (w2_0,) = ride_mlp1_0.results, ride_mlp2_0.results
    x1, h2_0 = matmul(cat0, w_out0, post=_add, post_ins=[x0], row_ins=[n2w0], extra_out=(_norm_rows, BF16),
                      tiles=(None, D_MODEL, None), name="wout_fwd_0")
    (x2, h1), mlp_res0 = _mlp_fwd(x1, h2_0, w1_0, w2_0, 0, riders=[[ride_fox_a], [ride_fox_b]], next_norm_w=n1w1)
    fox_main, fox_f = in_proj_weights(
        jnp.concatenate([ride_fox_a.results[0], ride_fox_b.results[0]], axis=1), FOX_IN, N_HEADS)

    fbias = _row(w["fox_f_bias"][0], LANES)
    qnw, knw, mqw1 = _row(w["fox_q_norm_w"][0]), _row(w["fox_k_norm_w"][0]), _row(w["memq_norm_w"][1])
    pm1 = matmul(h1, fox_main, name="inproj_main_1")
    ps1 = matmul(h1, fox_f, name="inproj_small_1")
    fq = rows_call(fox_fcum_fn, [ps1], [fbias], [(LANES, F32)], [], tm=t, name="fox_fcum_fwd")[0]
    fk = fq[:, :N_HEADS].T[:, None, :]
    q1, k1 = rows_call(fox_qk_fn, [(pm1, D_MODEL, 0), (pm1, D_MODEL, 1)], [qnw, knw], [(D_MODEL, F32)] * 2, [], tm=256,
                       name="fox_qk_fwd")
    o1 = fox_attn_fwd(q1, k1, pm1, fq, fk, name="fox_attn_fwd", riders=[ride_mlp_1])
    cat1 = rows_call(fox_out_fn, [o1, (pm1, D_MODEL, 3), (pm1, MEM_WIDTH, 8)], [mqw1, mk, mv],
                     [(D_MODEL + MEM_WIDTH, BF16)], [], tm=256, name="fox_out_fwd")[0]
    w1_1, w2_1 = ride_mlp_1.results
    x3, h2_1 = matmul(cat1, w_out1, post=_add, post_ins=[x2], row_ins=[n2w1], extra_out=(_norm_rows, BF16),
                      tiles=(None, D_MODEL, None), name="wout_fwd_1")
    y, mlp_res1 = _mlp_fwd(x3, h2_1, w1_1, w2_1, 1)

    def loss_fn(y, tgt):
        e = y - tgt
        return e * (1.0 / D_MODEL), jnp.sum(jnp.sum(e * e, axis=1, keepdims=True), axis=0, keepdims=True)
    dy, sq = rows_call(loss_fn, [y, target], [], [(D_MODEL, F32)], [(1, 1)], tm=512, name="loss")
    loss = sq[0, 0] * (0.5 / D_MODEL)

    dx3, dw1_1, dw2_1, dn2w1, sib1, sib2 = _mlp_bwd(dy, mlp_res1, n2w1, w1_1, w2_1, 1)
    dcat1 = matmul(dx3, w_out1, tb=True, name="wout_dx_1", riders=[sib1])
    dwo_1 = matmul(cat1, dx3, ta=True, out_dtype=BF16, name="wout_dw_1").reshape(N_DEV, OUT_IN // N_DEV, D_MODEL)
    sibo = sibling_rider([dwo_1])
    do1, dgate1, dqm1, dmqw1, dmk1, dmv1 = rows_call(
        functools.partial(vjp_rows(fox_out_fn, 3, (True, True, True)), n_row=3, n_ct=1),
        [o1, (pm1, D_MODEL, 3), (pm1, MEM_WIDTH, 8), dcat1], [mqw1, mk, mv],
        [(D_MODEL, F32), (D_MODEL, BF16), (MEM_WIDTH, BF16)], [(1, HEAD_DIM), (n_mem, MEM_WIDTH), (n_mem, MEM_WIDTH)],
        tm=256, name="fox_out_bwd", riders=[sibo])
    ride_l1 = chips_rider(chip_sums(["w_mlp2_1", "w_mlp1_1", "w_out_1"], [dw2_1, dw1_1, dwo_1],
                                    sib2.results + sib1.results + sibo.results))
    dq1, dk1, dv1, dfq, dfk = fox_attn_bwd(q1, k1, pm1, fq, fk, do1, name="fox_attn_bwd", riders=[ride_l1])
    dqraw1, dkraw1, dqnw, dknw = rows_call(
        functools.partial(vjp_rows(fox_qk_fn, 2, (True, True)), n_row=2, n_ct=2),
        [(pm1, D_MODEL, 0), (pm1, D_MODEL, 1), dq1, dk1], [qnw, knw],
        [(D_MODEL, BF16)] * 2, [(1, HEAD_DIM)] * 2, tm=256, name="fox_qk_bwd")
    dfcum = dfq + jnp.pad(dfk[:, 0, :].T, ((0, 0), (0, LANES - N_HEADS)))
    dps1, dfbias = rows_call(
        functools.partial(vjp_rows(fox_fcum_fn, 1, (True,)), n_row=1, n_ct=1),
        [ps1, dfcum], [fbias], [(LANES, F32)], [(1, LANES)], tm=t, name="fox_fcum_bwd")
    dpm1 = jnp.concatenate([dqraw1, dkraw1, dv1, dgate1, dqm1], axis=1)
    dx2, dn1w1, dwmain1, dwsmall1 = _in_proj_bwd(h1, dpm1, dps1, fox_main, fox_f, x2, dx3, n1w1, "1")
    g_fox = in_proj_pieces(dwmain1, dwsmall1, N_HEADS, FOX_IN)
    sibf = sibling_rider([g_fox])

    dx1, dw1_0, dw2_0, dn2w0, sib1, sib2 = _mlp_bwd(dx2, mlp_res0, n2w0, w1_0, w2_0, 0, riders=[sibf])
    ride_fox_g = chips_rider(chip_sums(["fox_w_in"], [g_fox], sibf.results))
    dcat0 = matmul(dx1, w_out0, tb=True, name="wout_dx_0", riders=[sib1])
    dwo_0 = matmul(cat0, dx1, ta=True, out_dtype=BF16, name="wout_dw_0").reshape(N_DEV, OUT_IN // N_DEV, D_MODEL)
    sibo = sibling_rider([dwo_0])
    do0, dz0, dqm0, donw, dmqw0, dmk0, dmv0 = rows_call(
        functools.partial(vjp_rows(dn_out_fn, 3, (True, True, True, True)), n_row=3, n_ct=1),
        [o0, (pm0, D_MODEL, 3), (pm0, MEM_WIDTH, 8), dcat0], [onw, mqw0, mk, mv],
        [((N_HEADS, HEAD_DIM), F32), (D_MODEL, BF16), (MEM_WIDTH, BF16)],
        [(1, HEAD_DIM), (1, HEAD_DIM), (n_mem, MEM_WIDTH), (n_mem, MEM_WIDTH)], tm=256, name="dn_out_bwd", riders=[sibo])
    h_l0 = chip_sums(["w_mlp2_0", "w_mlp1_0", "w_out_0"], [dw2_0, dw1_0, dwo_0], sib2.results + sib1.results + sibo.results)
    ride_l0_mlp2, ride_l0_rest = chips_rider(h_l0[:1]), chips_rider(h_l0[1:])
    dmnw, dwkv, dmknw = memkv_bwd(mem, mnw, w_kv, mknw, dmk0 + dmk1, dmv0 + dmv1, name="memkv_bwd")
    g_kv = dwkv.reshape(N_DEV, D_MODEL // N_DEV, D_MODEL)
    sibk = sibling_rider([g_kv])
    dq_s, dk_s, dg_s, du0, dw0, dqk0 = delta_seq_bwd(q0, k0, gates, u0, w0, qk0, s_start, do0, name="delta_seq_bwd",
                                                     riders=[ride_fox_g, sibk])
    ride_kv_g = chips_rider(chip_sums(["w_mem_kv"], [g_kv], sibk.results))
    dq0, dk0, dv0, dgates = delta_intra_bwd(q0, k0, v0, gates, du0, dw0, dqk0, dq_s, dk_s, dg_s,
                                            name="delta_intra_bwd", riders=[ride_l0_mlp2, ride_kv_g])
    dxq, dxk, dxv, dcq, dck, dcv = dn_prep_bwd(pm0, conv_w, dq0, dk0, dv0, name="dn_prep_bwd", riders=[ride_l0_rest])
    dconv = jnp.concatenate([dcq, dck, dcv], axis=1)
    dps0, dalog, ddtb = rows_call(
        functools.partial(vjp_rows(dn_gates_fn, 1, (True, True)), n_row=1, n_ct=1),
        [ps0, dgates], [alog, dtb], [(LANES, F32)], [(1, LANES)] * 2, tm=512, name="dn_gates_bwd")
    dpm0 = jnp.concatenate([dxq, dxk, dxv, dz0, dqm0], axis=1)
    grad_x, dn1w0, dwmain0, dwsmall0 = _in_proj_bwd(h0, dpm0, dps0, dn_main, dn_ab, x0, dx1, n1w0, "0")
    g_dn = in_proj_pieces(dwmain0, dwsmall0, 2 * N_HEADS, DN_IN)
    g_conv = dconv.reshape(CONV_WIDTH, N_DEV, -1).transpose(1, 0, 2).astype(BF16)

    g["mem_norm_w"] = dmnw[0]
    g["mem_k_norm_w"] = dmknw[0]
    g["norm1_w"] = jnp.concatenate([dn1w0, dn1w1], axis=0)
    g["dn_a_log"] = dalog[:, :N_HEADS]
    g["dn_dt_bias"] = ddtb[:, :N_HEADS]
    g["dn_o_norm_w"] = donw
    g["fox_f_bias"] = dfbias[:, :N_HEADS]
    g["fox_q_norm_w"] = dqnw
    g["fox_k_norm_w"] = dknw
    g["memq_norm_w"] = jnp.concatenate([dmqw0, dmqw1], axis=0)
    g["norm2_w"] = jnp.concatenate([dn2w0, dn2w1], axis=0)

    sibd = sibling_rider([g_dn, g_conv])
    run_riders([sibd], name="grads_to_sibling_last")
    ride_last = chips_rider(chip_sums(["dn_w_in", "dn_conv_w"], [g_dn, g_conv], sibd.results))
    ride_small = gather_rider([pack_small(g, last=loss)])
    run_riders([ride_last, ride_small], name="grads_to_chips_last")

    def layers(l0, l1):
        return jnp.stack([l0, l1], axis=1).reshape(4, -1, l0.shape[-1])

    parts = {
        "w_mlp1": layers(ride_l0_rest.results[0], ride_l1.results[1]),
        "w_mlp2": layers(ride_l0_mlp2.results[0], ride_l1.results[0]),
        "w_out": layers(ride_l0_rest.results[1], ride_l1.results[2]),
        "fox_w_in": ride_fox_g.results[0], "w_mem_kv": ride_kv_g.results[0],
        "dn_w_in": ride_last.results[0], "dn_conv_w": ride_last.results[1],
    }
    out = {n: adamw(parts[n], w[n], m[n], v[n], name=f"adamw_{n}") for n, _, _ in BIG}
    small = adamw(ride_small.results[0], pack_small(w), pack_small(m), pack_small(v), name="adamw_small")
    loss = small[0][-1, -1]
    return loss, grad_x, out, small


WEIGHTS = ["mem_norm_w", "w_mem_kv", "mem_k_norm_w", "norm1_w", "dn_w_in", "dn_conv_w", "dn_a_log", "dn_dt_bias",
           "dn_o_norm_w", "fox_w_in", "fox_f_bias", "fox_q_norm_w", "fox_k_norm_w", "memq_norm_w", "w_out", "norm2_w",
           "w_mlp1", "w_mlp2"]
DN_IN = 4 * D_MODEL + 2 * N_HEADS + MEM_WIDTH
FOX_IN = 4 * D_MODEL + N_HEADS + MEM_WIDTH
GATE_END = 4 * D_MODEL
OUT_IN = D_MODEL + MEM_WIDTH
BIG = [("w_mem_kv", D_MODEL // N_DEV, D_MODEL), ("dn_w_in", D_MODEL, DN_IN // N_DEV), ("fox_w_in", D_MODEL, FOX_IN // N_DEV),
       ("dn_conv_w", CONV_WIDTH, 3 * D_MODEL // N_DEV), ("w_out", 2 * OUT_IN // N_DEV, D_MODEL),
       ("w_mlp1", 2 * D_MODEL, FF_PIECE), ("w_mlp2", 2 * FF_PIECE, D_MODEL)]
SMALL_TILE = 8 * LANES
SMALL = [(name, shape, -(-math.prod(shape) // SMALL_TILE) * SMALL_TILE) for name, shape in [
    ("mem_norm_w", (D_MODEL,)), ("mem_k_norm_w", (HEAD_DIM,)), ("norm1_w", (2, D_MODEL)), ("dn_a_log", (1, N_HEADS)),
    ("dn_dt_bias", (1, N_HEADS)), ("dn_o_norm_w", (1, HEAD_DIM)), ("fox_f_bias", (1, N_HEADS)),
    ("fox_q_norm_w", (1, HEAD_DIM)), ("fox_k_norm_w", (1, HEAD_DIM)), ("memq_norm_w", (2, HEAD_DIM)), ("norm2_w", (2, D_MODEL))]]
SMALL_ROWS = sum(ln for _, _, ln in SMALL) // LANES + 8


def pack_small(p, last=None):
    def rows(a, ln):
        a = a.reshape(-1)
        return (a if a.shape[0] == ln else jnp.pad(a, (0, ln - a.shape[0]))).reshape(-1, LANES)

    used = sum(ln for _, _, ln in SMALL) // LANES
    tail = jnp.zeros(((SMALL_ROWS - used) * LANES,), F32)
    if last is not None:
        tail = jnp.concatenate([tail[:-1], last.reshape(1)])
    return jnp.concatenate([rows(p[n], ln) for n, _, ln in SMALL] + [tail.reshape(-1, LANES)], axis=0)


def unpack_small(pk):
    row, out = 0, {}
    for n, sh, ln in SMALL:
        out[n] = pk[row:row + ln // LANES].reshape(-1)[:math.prod(sh)].reshape(sh)
        row += ln // LANES
    return out


def in_proj_weights(gathered, width, n_small):
    full = gathered.transpose(1, 0, 2).reshape(D_MODEL, width)
    main = jnp.concatenate([full[:, :GATE_END], full[:, GATE_END + n_small:]], axis=1)
    return main, jnp.pad(full[:, GATE_END:GATE_END + n_small], ((0, 0), (0, LANES - n_small)))


def in_proj_pieces(d_main, d_small, n_small, width):
    full = jnp.concatenate([d_main[:, :GATE_END], d_small[:, :n_small], d_main[:, GATE_END:]], axis=1)
    return full.reshape(D_MODEL, N_DEV, width // N_DEV).transpose(1, 0, 2)


def adamw(parts, w, m, v, *, name):
    n, _, cols = parts.shape
    layers = w.shape[0] if w.ndim == 3 else 1
    rows = w.shape[-2]
    tile = _pick(rows, (512, 256, 128))
    steps = rows // tile

    def body(p_ref, w_ref, m_ref, v_ref, g_ref, d_ref, mo_ref, vo_ref):
        g = p_ref[0].astype(F32)
        for i in range(1, n):
            g = g + p_ref[i].astype(F32)
        m_new = ADAM_B1 * m_ref[...] + (1.0 - ADAM_B1) * g
        v_new = ADAM_B2 * v_ref[...] + (1.0 - ADAM_B2) * jnp.square(g)
        m_hat = m_new / (1.0 - ADAM_B1 ** ADAM_STEP)
        v_hat = v_new / (1.0 - ADAM_B2 ** ADAM_STEP)
        g_ref[...] = g
        d_ref[...] = -ADAM_LR * (m_hat / (jnp.sqrt(v_hat) + ADAM_EPS) + ADAM_WD * w_ref[...])
        mo_ref[...] = m_new
        vo_ref[...] = v_new

    if w.ndim == 3:
        spec = pl.BlockSpec((None, tile, cols), lambda l, i: (l, i, 0))
    else:
        spec = pl.BlockSpec((tile, cols), lambda l, i: (i, 0))
    return pl.pallas_call(
        body, name=name, grid=(layers, steps),
        in_specs=[pl.BlockSpec((n, tile, cols), lambda l, i: (0, l * steps + i, 0)), spec, spec, spec], out_specs=[spec] * 4,
        out_shape=[jax.ShapeDtypeStruct(w.shape, F32)] * 4, compiler_params=_params(("parallel", "parallel")),
    )(parts, w, m, v)


def kernel(x, mem, mem_norm_w, w_mem_kv, mem_k_norm_w, norm1_w, dn_w_in, dn_conv_w, dn_a_log, dn_dt_bias, dn_o_norm_w, fox_w_in, fox_f_bias, fox_q_norm_w, fox_k_norm_w, memq_norm_w, w_out, norm2_w, w_mlp1, w_mlp2, loss_target, m_mem_norm_w, m_w_mem_kv, m_mem_k_norm_w, m_norm1_w, m_dn_w_in, m_dn_conv_w, m_dn_a_log, m_dn_dt_bias, m_dn_o_norm_w, m_fox_w_in, m_fox_f_bias, m_fox_q_norm_w, m_fox_k_norm_w, m_memq_norm_w, m_w_out, m_norm2_w, m_w_mlp1, m_w_mlp2, v_mem_norm_w, v_w_mem_kv, v_mem_k_norm_w, v_norm1_w, v_dn_w_in, v_dn_conv_w, v_dn_a_log, v_dn_dt_bias, v_dn_o_norm_w, v_fox_w_in, v_fox_f_bias, v_fox_q_norm_w, v_fox_k_norm_w, v_memq_norm_w, v_w_out, v_norm2_w, v_w_mlp1, v_w_mlp2):
    p = dict(mem_norm_w=mem_norm_w, w_mem_kv=w_mem_kv, mem_k_norm_w=mem_k_norm_w, norm1_w=norm1_w, dn_w_in=dn_w_in,
             dn_conv_w=dn_conv_w, dn_a_log=dn_a_log, dn_dt_bias=dn_dt_bias, dn_o_norm_w=dn_o_norm_w, fox_w_in=fox_w_in,
             fox_f_bias=fox_f_bias, fox_q_norm_w=fox_q_norm_w, fox_k_norm_w=fox_k_norm_w, memq_norm_w=memq_norm_w,
             w_out=w_out, norm2_w=norm2_w, w_mlp1=w_mlp1, w_mlp2=w_mlp2)
    pm = dict(mem_norm_w=m_mem_norm_w, w_mem_kv=m_w_mem_kv, mem_k_norm_w=m_mem_k_norm_w, norm1_w=m_norm1_w,
              dn_w_in=m_dn_w_in, dn_conv_w=m_dn_conv_w, dn_a_log=m_dn_a_log, dn_dt_bias=m_dn_dt_bias,
              dn_o_norm_w=m_dn_o_norm_w, fox_w_in=m_fox_w_in, fox_f_bias=m_fox_f_bias, fox_q_norm_w=m_fox_q_norm_w,
              fox_k_norm_w=m_fox_k_norm_w, memq_norm_w=m_memq_norm_w, w_out=m_w_out, norm2_w=m_norm2_w, w_mlp1=m_w_mlp1,
              w_mlp2=m_w_mlp2)
    pv = dict(mem_norm_w=v_mem_norm_w, w_mem_kv=v_w_mem_kv, mem_k_norm_w=v_mem_k_norm_w, norm1_w=v_norm1_w,
              dn_w_in=v_dn_w_in, dn_conv_w=v_dn_conv_w, dn_a_log=v_dn_a_log, dn_dt_bias=v_dn_dt_bias,
              dn_o_norm_w=v_dn_o_norm_w, fox_w_in=v_fox_w_in, fox_f_bias=v_fox_f_bias, fox_q_norm_w=v_fox_q_norm_w,
              fox_k_norm_w=v_fox_k_norm_w, memq_norm_w=v_memq_norm_w, w_out=v_w_out, norm2_w=v_norm2_w, w_mlp1=v_w_mlp1,
              w_mlp2=v_w_mlp2)

    loss, grad_x, results, small = local_step(x[0], mem[0], loss_target[0], p, pm, pv)
    small = [unpack_small(o) for o in small]
    groups = [{**small[i], **{n: r[i] for n, r in results.items()}} for i in range(4)]
    return (loss, grad_x[None], *[grp[n] for grp in groups for n in WEIGHTS])
```

```python
import functools
import math

import jax
import jax.numpy as jnp
from jax import lax
from jax.experimental import pallas as pl
from jax.experimental.pallas import tpu as pltpu

F32 = jnp.float32
BF16 = jnp.bfloat16
HIGHEST = lax.Precision.HIGHEST

D_MODEL = 1024
HEAD_DIM = 128
N_HEADS = 8
MEM_HEADS = 4
MEM_WIDTH = MEM_HEADS * HEAD_DIM
D_FF = 4 * D_MODEL
CONV_WIDTH = 4
CHUNK = 64
Q_BLOCK = 128
EPS = 1e-6
SCALE = HEAD_DIM ** -0.5
MAIN_WIDTH = 4 * D_MODEL + MEM_WIDTH
LANES = 128
N_DEV = 8

ADAM_LR = 0.001
ADAM_B1 = 0.9
ADAM_B2 = 0.999
ADAM_EPS = 1e-08
ADAM_WD = 0.01
ADAM_STEP = 10

VMEM_LIMIT = 56 * 2 ** 20
MESH = pl.DeviceIdType.MESH


def _bdot(a, b, dims):
    return lax.dot_general(a.astype(BF16), b.astype(BF16), (dims, ((), ())), preferred_element_type=F32)


@jax.custom_vjp
def mm(a, b):
    return _bdot(a, b, ((1,), (0,)))


@jax.custom_vjp
def mm_nt(a, b):
    return _bdot(a, b, ((1,), (1,)))


@jax.custom_vjp
def mm_tn(a, b):
    return _bdot(a, b, ((0,), (0,)))


mm.defvjp(lambda a, b: (mm(a, b), (a, b)), lambda r, g: (mm_nt(g, r[1]), mm_tn(r[0], g)))
mm_nt.defvjp(lambda a, b: (mm_nt(a, b), (a, b)), lambda r, g: (mm(g, r[1]), mm_tn(g, r[0])))
mm_tn.defvjp(lambda a, b: (mm_tn(a, b), (a, b)), lambda r, g: (mm_nt(r[1], g), mm(r[0], g)))


def hdot(a, b):
    return jnp.dot(a, b, precision=HIGHEST, preferred_element_type=F32)


def rms(x, w):
    return x * lax.rsqrt(jnp.mean(x * x, axis=-1, keepdims=True) + EPS) * w


def l2n(x):
    return x * lax.rsqrt(jnp.sum(x * x, axis=-1, keepdims=True) + EPS)


def _iota2(n, m):
    return lax.broadcasted_iota(jnp.int32, (n, m), 0), lax.broadcasted_iota(jnp.int32, (n, m), 1)


def _lower_ones(n):
    r, c = _iota2(n, n)
    return jnp.where(r >= c, 1.0, 0.0).astype(F32)


def _last_row(x):
    r = lax.broadcasted_iota(jnp.int32, x.shape, 0)
    return jnp.sum(jnp.where(r == x.shape[0] - 1, x, 0.0), axis=0, keepdims=True)


def _softmax_rows(z):
    m = lax.stop_gradient(jnp.max(z, axis=-1, keepdims=True))
    e = jnp.exp(z - m)
    return e * (1.0 / jnp.sum(e, axis=-1, keepdims=True))


_BNN = (((2,), (1,)), ((0,), (0,)))
_BNT = (((2,), (2,)), ((0,), (0,)))
_BTN = (((1,), (1,)), ((0,), (0,)))


def _bbdot(a, b, dims):
    return lax.dot_general(a.astype(BF16), b.astype(BF16), dims, preferred_element_type=F32)


@jax.custom_vjp
def bmm(a, b):
    return _bbdot(a, b, _BNN)


@jax.custom_vjp
def bmm_nt(a, b):
    return _bbdot(a, b, _BNT)


@jax.custom_vjp
def bmm_tn(a, b):
    return _bbdot(a, b, _BTN)


@jax.custom_vjp
def bmm_high(a, b):
    return lax.dot_general(a, b, _BNN, precision=lax.Precision.HIGH, preferred_element_type=F32)


bmm.defvjp(lambda a, b: (bmm(a, b), (a, b)), lambda r, g: (bmm_nt(g, r[1]), bmm_tn(r[0], g)))
bmm_nt.defvjp(lambda a, b: (bmm_nt(a, b), (a, b)), lambda r, g: (bmm(g, r[1]), bmm_tn(g, r[0])))
bmm_tn.defvjp(lambda a, b: (bmm_tn(a, b), (a, b)), lambda r, g: (bmm_nt(r[1], g), bmm(r[0], g)))
bmm_high.defvjp(lambda a, b: (bmm_high(a, b), (a, b)), lambda r, g: (bmm_nt(g, r[1]), bmm_tn(r[0], g)))

NEUMANN_HIGH_LEVELS = 2


@jax.custom_vjp
def inv_unit_lower(a):
    n = a.shape[-1]
    r, c = _iota2(n, n)
    p = jnp.where(r == c, 1.0, 0.0).astype(F32) - a
    ak = a
    for level in range(int(math.log2(n)) - 1):
        dot = bmm_high if level < NEUMANN_HIGH_LEVELS else bmm
        ak = dot(ak, ak)
        p = p + dot(p, ak)
    return p


def _inv_unit_lower_fwd(a):
    t = inv_unit_lower(a)
    return t, t


def _inv_unit_lower_bwd(t, g):
    return (-bmm_tn(t, bmm_nt(g, t)),)


inv_unit_lower.defvjp(_inv_unit_lower_fwd, _inv_unit_lower_bwd)


def delta_intra(q, k, v, gc, beta):
    b, c, _ = q.shape
    r, cc = _iota2(c, c)
    causal = r >= cc
    strict = r > cc
    gi = jnp.broadcast_to(gc, (b, c, c))
    gj = jnp.swapaxes(gi, 1, 2)
    decay = jnp.where(causal, jnp.exp(jnp.where(causal, gi - gj, 0.0)), 0.0)
    kb = k * beta
    a = jnp.where(strict, bmm_nt(kb, k) * decay, 0.0)
    t = inv_unit_lower(a)
    u = bmm(t, v * beta)
    w = bmm(t, kb * jnp.exp(gc))
    qk = jnp.where(causal, bmm_nt(q, k) * decay, 0.0)
    return u, w, qk


def delta_step(s, q, k, gc, u, w, qk):
    v_new = u - bmm(w, s)
    out = bmm(q * jnp.exp(gc), s) + bmm(qk, v_new)
    r = lax.broadcasted_iota(jnp.int32, gc.shape, 1)
    g_last = jnp.sum(jnp.where(r == gc.shape[1] - 1, gc, 0.0), axis=1, keepdims=True)
    k_dec = k * jnp.exp(g_last - gc)
    s_new = s * jnp.exp(g_last) + bmm_tn(k_dec, v_new)
    return out, s_new


def fox_probs(q, k, fq, fk, qpos0):
    s = lax.dot_general(q, k, (((1,), (1,)), ((), ())), preferred_element_type=F32)
    r, c = _iota2(s.shape[0], s.shape[1])
    return _softmax_rows(jnp.where(c <= (r + qpos0), s + (fq - fk), -jnp.inf))


def mem_head(qm, wq, mk, mv):
    p = _softmax_rows(mm_nt(rms(qm, wq) * SCALE, mk))
    return mm(p, mv)


def _heads(x, n):
    return [x[:, h * HEAD_DIM:(h + 1) * HEAD_DIM] for h in range(n)]


def memkv_fn(mem, mnw, wkv, mknw):
    kv = mm(rms(mem, mnw), wkv)
    mk = jnp.concatenate([rms(kh, mknw) for kh in _heads(kv[:, :MEM_WIDTH], MEM_HEADS)], axis=1)
    return mk, kv[:, MEM_WIDTH:]


def dn_gates_fn(ab, alog, dtb):
    g = -jnp.exp(alog) * jax.nn.softplus(ab + dtb)
    low = _lower_ones(CHUNK)
    gc = jnp.concatenate([hdot(low, g[i * CHUNK:(i + 1) * CHUNK]) for i in range(ab.shape[0] // CHUNK)], axis=0)
    lane = lax.broadcasted_iota(jnp.int32, ab.shape, 1)
    return jnp.where(lane < N_HEADS, gc, jax.nn.sigmoid(ab))


def fox_fcum_fn(fp, fbias):
    lf = jax.nn.log_sigmoid(fp + fbias)
    low = _lower_ones(LANES)
    carry = jnp.zeros((1, fp.shape[1]), F32)
    outs = []
    for i in range(fp.shape[0] // LANES):
        cs = hdot(low, lf[i * LANES:(i + 1) * LANES]) + carry
        carry = _last_row(cs)
        outs.append(cs)
    return jnp.concatenate(outs, axis=0)


def fox_qk_fn(qraw, kraw, qnw, knw):
    q = jnp.concatenate([rms(x, qnw) * SCALE for x in _heads(qraw, N_HEADS)], axis=1)
    k = jnp.concatenate([rms(x, knw) for x in _heads(kraw, N_HEADS)], axis=1)
    return q, k


def _mem_out(qm, mqw, mk, mv):
    return [mem_head(a, mqw, b, c) for a, b, c in zip(_heads(qm, MEM_HEADS), _heads(mk, MEM_HEADS), _heads(mv, MEM_HEADS))]


def dn_out_fn(o, z, qm, onw, mqw, mk, mv):
    mix = [rms(a, onw) * jax.nn.silu(b) for a, b in zip(o, _heads(z, N_HEADS))]
    return jnp.concatenate(mix + _mem_out(qm, mqw, mk, mv), axis=1)


def fox_out_fn(o, gate, qm, mqw, mk, mv):
    return jnp.concatenate([o * jax.nn.sigmoid(gate)] + _mem_out(qm, mqw, mk, mv), axis=1)


_HBM = pl.BlockSpec(memory_space=pltpu.HBM)


def _place():
    return lax.axis_index("x"), lax.axis_index("y"), lax.axis_index("c")


class Rider:
    def __init__(self, ins, out_shape, scratch, start, finish, relay=None):
        self.ins, self.out_shape, self.scratch = list(ins), list(out_shape), list(scratch)
        self.start, self.finish, self.relay = start, finish, relay or (lambda *refs: None)
        self.results = None


def gather_rider(xs):
    n = len(xs)

    def plan(x_refs, out_refs, sems):
        send_sems, recv_sems, local_sems = sems
        x, y, c = _place()
        me, sibling = (x, y, c), (x, y, 1 - c)
        chips = [(1 - x, y), (x, 1 - y), (1 - x, 1 - y)]

        def copy(a, k, block, to, src=None):
            px, py, pc = block
            dst = out_refs[a].at[4 * px + 2 * py + pc]
            return pltpu.make_async_remote_copy(
                src_ref=dst if src is None else src, dst_ref=dst,
                send_sem=send_sems.at[a, k], recv_sem=recv_sems.at[a, k], device_id=to, device_id_type=MESH)

        mine = [pltpu.make_async_copy(x_refs[a], out_refs[a].at[4 * x + 2 * y + c], local_sems.at[a]) for a in range(n)]
        first = [copy(a, 0, me, sibling, src=x_refs[a]) for a in range(n)]
        first += [copy(a, 1 + j, me, (*chip, c), src=x_refs[a]) for j, chip in enumerate(chips) for a in range(n)]
        return copy, me, sibling, chips, mine, first

    def start(x_refs, out_refs, sems):
        _, _, _, _, mine, first = plan(x_refs, out_refs, sems)
        for cp in mine + first:
            cp.start()

    def relay(x_refs, out_refs, sems):
        copy, me, sibling, chips, _, _ = plan(x_refs, out_refs, sems)
        _, _, c = me
        for j, chip in enumerate(chips):
            for a in range(n):
                copy(a, 1 + j, (*chip, c), me).wait_recv()
                copy(a, 4 + j, (*chip, c), sibling).start()

    def finish(x_refs, out_refs, sems):
        copy, me, sibling, chips, mine, first = plan(x_refs, out_refs, sems)
        _, _, c = me
        passed = [copy(a, 4 + j, (*chip, c), sibling) for j, chip in enumerate(chips) for a in range(n)]
        for a in range(n):
            copy(a, 0, sibling, me).wait_recv()
        for j, chip in enumerate(chips):
            for a in range(n):
                copy(a, 4 + j, (*chip, 1 - c), me).wait_recv()
        for cp in first + passed:
            cp.wait_send()
        for cp in mine:
            cp.wait()

    return Rider(xs, [jax.ShapeDtypeStruct((N_DEV,) + a.shape, a.dtype) for a in xs],
                 [pltpu.SemaphoreType.DMA((n, 7)), pltpu.SemaphoreType.DMA((n, 7)), pltpu.SemaphoreType.DMA((n,))],
                 start, finish, relay)


def sibling_rider(gs):
    n = len(gs)

    def plan(g_refs, out_refs, sems):
        send_sems, recv_sems = sems
        x, y, c = _place()
        return [pltpu.make_async_remote_copy(
            src_ref=g_refs[a].at[2 * k + 1 - c], dst_ref=out_refs[a].at[k], send_sem=send_sems.at[a, k],
            recv_sem=recv_sems.at[a, k], device_id=(x, y, 1 - c), device_id_type=MESH) for a in range(n) for k in range(4)]

    def start(g_refs, out_refs, sems):
        for cp in plan(g_refs, out_refs, sems):
            cp.start()

    def finish(g_refs, out_refs, sems):
        copies = plan(g_refs, out_refs, sems)
        for cp in copies:
            cp.wait_recv()
        for cp in copies:
            cp.wait_send()

    return Rider(gs, [jax.ShapeDtypeStruct((4,) + g.shape[1:], g.dtype) for g in gs],
                 [pltpu.SemaphoreType.DMA((n, 4)), pltpu.SemaphoreType.DMA((n, 4))], start, finish)


def chips_rider(hs):
    n = len(hs)

    def plan(h_refs, out_refs, sems):
        send_sems, recv_sems, local_sems = sems
        x, y, c = _place()
        mine = 2 * x + y
        chips = [(1 - x, y), (x, 1 - y), (1 - x, 1 - y)]
        keep = [pltpu.make_async_copy(h_refs[a].at[mine], out_refs[a].at[mine], local_sems.at[a]) for a in range(n)]
        sends = [pltpu.make_async_remote_copy(
            src_ref=h_refs[a].at[2 * qx + qy], dst_ref=out_refs[a].at[mine], send_sem=send_sems.at[a, j],
            recv_sem=recv_sems.at[a, j], device_id=(qx, qy, c), device_id_type=MESH)
            for j, (qx, qy) in enumerate(chips) for a in range(n)]
        recvs = [pltpu.make_async_remote_copy(
            src_ref=h_refs[a].at[mine], dst_ref=out_refs[a].at[2 * qx + qy], send_sem=send_sems.at[a, j],
            recv_sem=recv_sems.at[a, j], device_id=(qx, qy, c), device_id_type=MESH)
            for j, (qx, qy) in enumerate(chips) for a in range(n)]
        return keep, sends, recvs

    def start(h_refs, out_refs, sems):
        keep, sends, _ = plan(h_refs, out_refs, sems)
        for cp in keep + sends:
            cp.start()

    def finish(h_refs, out_refs, sems):
        keep, sends, recvs = plan(h_refs, out_refs, sems)
        for cp in recvs:
            cp.wait_recv()
        for cp in sends:
            cp.wait_send()
        for cp in keep:
            cp.wait()

    return Rider(hs, [jax.ShapeDtypeStruct(h.shape, h.dtype) for h in hs],
                 [pltpu.SemaphoreType.DMA((n, 3)), pltpu.SemaphoreType.DMA((n, 3)), pltpu.SemaphoreType.DMA((n,))], start, finish)


def hosted_call(riders, body, *, out_shape, in_specs, out_specs, grid=(), scratch_shapes=(), **kw):
    riders = tuple(riders or ())
    if not riders:
        return pl.pallas_call(body, out_shape=out_shape, in_specs=in_specs, out_specs=out_specs, grid=grid,
                              scratch_shapes=scratch_shapes, **kw)
    single = not isinstance(out_shape, (list, tuple))
    k_out_shape = [out_shape] if single else list(out_shape)
    k_out_specs = [out_specs] if single else list(out_specs)
    n_in, n_out, n_scr = len(in_specs), len(k_out_shape), len(scratch_shapes)
    r_ins = [a for r in riders for a in r.ins]
    r_outs = [s for r in riders for s in r.out_shape]
    r_scr = [s for r in riders for s in r.scratch]

    def full_body(*refs):
        ins = refs[:n_in + len(r_ins)]
        outs = refs[n_in + len(r_ins):n_in + len(r_ins) + n_out + len(r_outs)]
        scr = refs[n_in + len(r_ins) + n_out + len(r_outs):]
        steps = math.prod(grid)
        step = 0
        for d, g in enumerate(grid):
            step = step * g + pl.program_id(d)

        def each(method):
            i0, o0, s0 = n_in, n_out, n_scr
            for r in riders:
                getattr(r, method)(ins[i0:i0 + len(r.ins)], outs[o0:o0 + len(r.out_shape)], scr[s0:s0 + len(r.scratch)])
                i0, o0, s0 = i0 + len(r.ins), o0 + len(r.out_shape), s0 + len(r.scratch)

        def end():
            each("relay")
            each("finish")

        if steps == 1:
            each("start")
            body(*ins[:n_in], *outs[:n_out], *scr[:n_scr])
            end()
        else:
            pl.when(step == 0)(lambda: each("start"))
            body(*ins[:n_in], *outs[:n_out], *scr[:n_scr])
            pl.when(step == steps - 1)(end)

    call = pl.pallas_call(
        full_body, out_shape=k_out_shape + r_outs, in_specs=list(in_specs) + [_HBM] * len(r_ins),
        out_specs=k_out_specs + [_HBM] * len(r_outs), grid=grid, scratch_shapes=list(scratch_shapes) + r_scr, **kw)

    def run(*args):
        res = call(*args, *r_ins)
        o0 = n_out
        for r in riders:
            r.results = list(res[o0:o0 + len(r.out_shape)])
            o0 += len(r.out_shape)
        return res[0] if single else list(res[:n_out])

    return run


def run_riders(riders, *, name):
    hosted_call(riders, lambda: None, name=name, out_shape=[], in_specs=[], out_specs=[])()
    return [r.results for r in riders]


def _pick(n, cands):
    for c in cands:
        if n % c == 0:
            return c
    return n


def _params(sem):
    return pltpu.CompilerParams(dimension_semantics=sem, vmem_limit_bytes=VMEM_LIMIT)


MATMUL_VMEM_BUDGET = 40 * 2 ** 20


def _matmul_tiles(m, n, k, bytes_a, bytes_b, bytes_mn, fixed):
    fm, fn, fk = fixed if fixed is not None else (None, None, None)

    def options(given, size, cands):
        return [given] if given else ([c for c in cands if size % c == 0] or [size])

    best = None
    for tm in options(fm, m, (2048, 1024, 512, 256, 128)):
        for tn in options(fn, n, (512, 256, 128)):
            for tk in options(fk, k, (2048, 1536, 1024, 512, 256, 128)):
                if 2 * (tm * tk * bytes_a + tk * tn * bytes_b + tm * tn * bytes_mn) + tm * tn * 4 > MATMUL_VMEM_BUDGET:
                    continue
                key = ((m // tm) * (n // tn) * (k // tk), -tk)
                if best is None or key < best[0]:
                    best = (key, (tm, tn, tk))
    assert best is not None, (m, n, k, fixed)
    return best[1]


def matmul(a, b, *, name, ta=False, tb=False, post=None, post_ins=(), row_ins=(), acc=False, extra_out=None,
           out_dtype=F32, tiles=None, b_view=None, out_view=None, riders=()):
    (k, m) = a.shape if ta else a.shape[::-1]
    (kb, n) = b_view[:2] if b_view is not None else (b.shape[::-1] if tb else b.shape)
    assert k == kb, (a.shape, b.shape, ta, tb)
    bytes_mn = sum(p.dtype.itemsize for p in post_ins) + jnp.dtype(out_dtype).itemsize
    bytes_mn += jnp.dtype(extra_out[1]).itemsize if extra_out else 0
    tm, tn, tk = _matmul_tiles(m, n, k, a.dtype.itemsize, b.dtype.itemsize, bytes_mn, tiles)
    assert not acc or tn == n, (name, tn, n)
    nk = k // tk
    dims = ((0,) if ta else (1,), (1,) if tb else (0,))
    n_post, n_row = len(post_ins), len(row_ins)
    n_out = 1 + bool(extra_out) + bool(acc)

    def body(*refs):
        a_ref, b_ref = refs[:2]
        post_refs = refs[2:2 + n_post + n_row]
        o_refs, acc_ref = refs[-1 - n_out:-1], refs[-1]
        first_rows, kk = pl.program_id(0) == 0, pl.program_id(2)

        @pl.when(kk == 0)
        def _():
            acc_ref[...] = jnp.zeros_like(acc_ref)

        b_tile = b_ref[...]
        acc_ref[...] += _bdot(a_ref[...], b_tile.reshape(-1, b_tile.shape[-1]), dims)

        @pl.when(kk == nk - 1)
        def _():
            r = acc_ref[...]
            rows = [p[...] for p in post_refs[n_post:]]
            if post is not None:
                r = post(r, *[p[...] for p in post_refs[:n_post]], *rows)
            if acc:
                r, s = r
                sum_ref = o_refs[-1]

                @pl.when(first_rows)
                def _():
                    sum_ref[...] = s

                @pl.when(jnp.logical_not(first_rows))
                def _():
                    sum_ref[...] += s

            o_refs[0][...] = r.astype(out_dtype)
            if extra_out:
                o_refs[1][...] = extra_out[0](r, *rows).astype(extra_out[1])

    a_spec = pl.BlockSpec((tk, tm), lambda i, j, kk: (kk, i)) if ta else pl.BlockSpec((tm, tk), lambda i, j, kk: (i, kk))
    if b_view is not None:
        b_spec = b_view[2]
    else:
        b_spec = pl.BlockSpec((tn, tk), lambda i, j, kk: (j, kk)) if tb else pl.BlockSpec((tk, tn), lambda i, j, kk: (kk, j))
    mn_spec = pl.BlockSpec((tm, tn), lambda i, j, kk: (i, j))
    row_spec = pl.BlockSpec((1, tn), lambda i, j, kk: (0, j))
    o_shape, o_spec = ((m, n), mn_spec) if out_view is None else out_view
    out_shape = [jax.ShapeDtypeStruct(o_shape, out_dtype)]
    out_specs = [o_spec]
    if extra_out:
        out_shape.append(jax.ShapeDtypeStruct((m, n), extra_out[1]))
        out_specs.append(mn_spec)
    if acc:
        out_shape.append(jax.ShapeDtypeStruct((1, n), F32))
        out_specs.append(row_spec)
    res = hosted_call(
        riders, body, name=name, grid=(m // tm, n // tn, nk),
        in_specs=[a_spec, b_spec] + [mn_spec] * n_post + [row_spec] * n_row, out_specs=out_specs, out_shape=out_shape,
        scratch_shapes=[pltpu.VMEM((tm, tn), F32)],
        compiler_params=_params(("arbitrary" if acc else "parallel", "parallel", "arbitrary")),
    )(a, b, *post_ins, *row_ins)
    return res if n_out > 1 else res[0]


def rows_call(fn, row_ins, full_ins, row_outs, acc_outs, *, tm, name, riders=()):
    row_ins = [r if isinstance(r, tuple) else (r, r.shape[-1], 0) for r in row_ins]
    t = row_ins[0][0].shape[-2]
    tm = min(tm, t)
    n_in = len(row_ins) + len(full_ins)
    n_row = len(row_outs)

    def body(*refs):
        res = fn(*[[r[h] for h in range(r.shape[0])] if (i < len(row_ins) and len(r.shape) == 3) else r[...]
                   for i, r in enumerate(refs[:n_in])])
        res = res if isinstance(res, (tuple, list)) else (res,)
        outs = refs[n_in:]
        for ref, val in zip(outs[:n_row], res[:n_row]):
            if len(ref.shape) == 3:
                for h, vh in enumerate(val):
                    ref[h] = vh.astype(ref.dtype)
            else:
                ref[...] = val.astype(ref.dtype)
        first = pl.program_id(0) == 0
        for ref, val in zip(outs[n_row:], res[n_row:]):
            @pl.when(first)
            def _(ref=ref, val=val):
                ref[...] = val

            @pl.when(jnp.logical_not(first))
            def _(ref=ref, val=val):
                ref[...] += val

    def full_spec(shape):
        return pl.BlockSpec(shape, lambda i, nd=len(shape): (0,) * nd)

    def row_spec(lead, w, cb):
        if lead is None:
            return pl.BlockSpec((tm, w), lambda i: (i, cb))
        return pl.BlockSpec((lead, tm, w), lambda i: (0, i, cb))

    def lead_cols(c):
        return c if isinstance(c, tuple) else (None, c)

    in_specs = [row_spec(a.shape[0] if a.ndim == 3 else None, w, cb) for (a, w, cb) in row_ins]
    in_specs += [full_spec(f.shape) for f in full_ins]
    out_specs = [row_spec(*lead_cols(c), 0) for c, _ in row_outs] + [full_spec(s) for s in acc_outs]
    out_shape = [jax.ShapeDtypeStruct(tuple(d for d in (lead_cols(c)[0], t, lead_cols(c)[1]) if d is not None), dt)
                 for c, dt in row_outs] + [jax.ShapeDtypeStruct(s, F32) for s in acc_outs]
    res = hosted_call(
        riders, body, name=name, grid=(t // tm,), in_specs=in_specs, out_specs=out_specs, out_shape=out_shape,
        compiler_params=_params(("arbitrary",)),
    )(*[r[0] for r in row_ins], *full_ins)
    return res


def vjp_rows(fn, n_diff_row, row_diff_full):
    def bwd(*args, n_row, n_ct):
        prim_rows = args[:n_row]
        cts = args[n_row:n_row + n_ct]
        fulls = args[n_row + n_ct:]
        _, vjp = jax.vjp(fn, *prim_rows, *fulls)
        g = vjp(cts[0] if n_ct == 1 else tuple(cts))
        out = list(g[:n_diff_row])
        out += [gf for gf, d in zip(g[n_row:], row_diff_full) if d]
        return tuple(out)
    return bwd


def _shift_down(x, s):
    if s == 0:
        return x
    t = lax.broadcasted_iota(jnp.int32, x.shape, 0)
    return jnp.where(t >= s, pltpu.roll(x, s, 0), 0.0)


def _shift_up(x, s):
    if s == 0:
        return x
    n = x.shape[0]
    t = lax.broadcasted_iota(jnp.int32, x.shape, 0)
    return jnp.where(t < n - s, pltpu.roll(x, n - s, 0), 0.0)


def _conv(x, w_ref):
    return sum(w_ref[pl.ds(j, 1), :] * _shift_down(x, CONV_WIDTH - 1 - j) for j in range(CONV_WIDTH))


_DN_POST = (lambda c: l2n(jax.nn.silu(c)) * SCALE, lambda c: l2n(jax.nn.silu(c)), jax.nn.silu)


def dn_prep_fwd(proj, conv_w, *, name, riders=()):
    t = proj.shape[0]

    def body(xq, xk, xv, wq, wk, wv, oq, ok, ov):
        for x_ref, w_ref, o_ref, post in zip((xq, xk, xv), (wq, wk, wv), (oq, ok, ov), _DN_POST):
            o_ref[...] = post(_conv(x_ref[...], w_ref))

    x_specs = [pl.BlockSpec((t, HEAD_DIM), lambda h, g=g: (0, g * N_HEADS + h)) for g in range(3)]
    w_specs = [pl.BlockSpec((CONV_WIDTH, HEAD_DIM), lambda h, g=g: (0, g * N_HEADS + h)) for g in range(3)]
    o_spec = pl.BlockSpec((None, t, HEAD_DIM), lambda h: (h, 0, 0))
    return hosted_call(
        riders, body, name=name, grid=(N_HEADS,), in_specs=x_specs + w_specs, out_specs=[o_spec] * 3,
        out_shape=[jax.ShapeDtypeStruct((N_HEADS, t, HEAD_DIM), F32)] * 3, compiler_params=_params(("parallel",)),
    )(proj, proj, proj, conv_w, conv_w, conv_w)


def dn_prep_bwd(proj, conv_w, dq, dk, dv, *, name, riders=()):
    t = proj.shape[0]

    def body(xq, xk, xv, wq, wk, wv, gq, gk, gv, dxq, dxk, dxv, dwq, dwk, dwv):
        for x_ref, w_ref, g_ref, dx_ref, dw_ref, post in zip(
                (xq, xk, xv), (wq, wk, wv), (gq, gk, gv), (dxq, dxk, dxv), (dwq, dwk, dwv), _DN_POST):
            x = x_ref[...]
            _, vjp = jax.vjp(post, _conv(x, w_ref))
            dc, = vjp(g_ref[...])
            dx = sum(w_ref[pl.ds(j, 1), :] * _shift_up(dc, CONV_WIDTH - 1 - j) for j in range(CONV_WIDTH))
            dx_ref[...] = dx.astype(dx_ref.dtype)
            for j in range(CONV_WIDTH):
                dw_ref[pl.ds(j, 1), :] = jnp.sum(dc * _shift_down(x, CONV_WIDTH - 1 - j), axis=0, keepdims=True)

    x_specs = [pl.BlockSpec((t, HEAD_DIM), lambda h, g=g: (0, g * N_HEADS + h)) for g in range(3)]
    w_specs = [pl.BlockSpec((CONV_WIDTH, HEAD_DIM), lambda h, g=g: (0, g * N_HEADS + h)) for g in range(3)]
    g_spec = pl.BlockSpec((None, t, HEAD_DIM), lambda h: (h, 0, 0))
    dx_spec = pl.BlockSpec((t, HEAD_DIM), lambda h: (0, h))
    dw_spec = pl.BlockSpec((CONV_WIDTH, HEAD_DIM), lambda h: (0, h))
    return hosted_call(
        riders, body, name=name, grid=(N_HEADS,), in_specs=x_specs + w_specs + [g_spec] * 3, out_specs=[dx_spec] * 3 + [dw_spec] * 3,
        out_shape=[jax.ShapeDtypeStruct((t, D_MODEL), BF16)] * 3 + [jax.ShapeDtypeStruct((CONV_WIDTH, D_MODEL), F32)] * 3,
        compiler_params=_params(("parallel",)),
    )(proj, proj, proj, conv_w, conv_w, conv_w, dq, dk, dv)


INTRA_CHUNKS = 4


def _lane_column(x, lane_index):
    lane = lax.broadcasted_iota(jnp.int32, x.shape, 1)
    return jnp.sum(jnp.where(lane == lane_index, x, 0.0), axis=1, keepdims=True)


def _head_columns(g, first_lane):
    return jnp.concatenate([_lane_column(g, first_lane + h)[None] for h in range(N_HEADS)], axis=0)


def _intra_of_gates(q, k, v, gates):
    nb = N_HEADS * (gates.shape[0] // CHUNK)

    def chunks(x):
        return x.reshape(nb, CHUNK, x.shape[-1])

    res = delta_intra(chunks(q), chunks(k), chunks(v), chunks(_head_columns(gates, 0)), chunks(_head_columns(gates, N_HEADS)))
    return tuple(x.reshape(N_HEADS, -1, x.shape[-1]) for x in res)


def _step_of_gates(s, q, k, gates, u, w, qk):
    return delta_step(s, q, k, _head_columns(gates, 0), u, w, qk)


def _head_major(rows, w, index):
    return pl.BlockSpec((N_HEADS, rows, w), lambda i: (0, index(i), 0))


def delta_intra_fwd(q, k, v, gates, *, name, riders=()):
    t = q.shape[1]
    rows = min(INTRA_CHUNKS, t // CHUNK) * CHUNK

    def body(q_ref, k_ref, v_ref, g_ref, u_ref, w_ref, qk_ref):
        for ref, val in zip((u_ref, w_ref, qk_ref), _intra_of_gates(q_ref[...], k_ref[...], v_ref[...], g_ref[...])):
            ref[...] = val

    x_spec, qk_spec = (_head_major(rows, w, lambda i: i) for w in (HEAD_DIM, CHUNK))
    g_spec = pl.BlockSpec((rows, LANES), lambda i: (i, 0))
    return hosted_call(
        riders, body, name=name, grid=(t // rows,), in_specs=[x_spec] * 3 + [g_spec], out_specs=[x_spec, x_spec, qk_spec],
        out_shape=[jax.ShapeDtypeStruct((N_HEADS, t, HEAD_DIM), F32)] * 2 + [jax.ShapeDtypeStruct((N_HEADS, t, CHUNK), F32)],
        compiler_params=_params(("parallel",)),
    )(q, k, v, gates)


def delta_seq_fwd(q, k, gates, u, w, qk, *, name, riders=()):
    t = q.shape[1]
    nc = t // CHUNK

    def body(q_ref, k_ref, g_ref, u_ref, w_ref, qk_ref, o_ref, s0_ref, s_ref):
        @pl.when(pl.program_id(0) == 0)
        def _():
            s_ref[...] = jnp.zeros_like(s_ref)

        s = s_ref[...]
        s0_ref[...] = s
        o, s_new = _step_of_gates(s, q_ref[...], k_ref[...], g_ref[...], u_ref[...], w_ref[...], qk_ref[...])
        o_ref[...] = o
        s_ref[...] = s_new

    x_spec, qk_spec = (_head_major(CHUNK, w, lambda c: c) for w in (HEAD_DIM, CHUNK))
    g_spec = pl.BlockSpec((CHUNK, LANES), lambda c: (c, 0))
    s_spec = pl.BlockSpec((N_HEADS, None, HEAD_DIM, HEAD_DIM), lambda c: (0, c, 0, 0))
    return hosted_call(
        riders, body, name=name, grid=(nc,), in_specs=[x_spec, x_spec, g_spec, x_spec, x_spec, qk_spec], out_specs=[x_spec, s_spec],
        out_shape=[jax.ShapeDtypeStruct((N_HEADS, t, HEAD_DIM), F32),
                   jax.ShapeDtypeStruct((N_HEADS, nc, HEAD_DIM, HEAD_DIM), F32)],
        scratch_shapes=[pltpu.VMEM((N_HEADS, HEAD_DIM, HEAD_DIM), F32)],
        compiler_params=_params(("arbitrary",)),
    )(q, k, gates, u, w, qk)


def delta_seq_bwd(q, k, gates, u, w, qk, s0, do, *, name, riders=()):
    t = q.shape[1]
    nc = t // CHUNK

    def body(q_ref, k_ref, g_ref, u_ref, w_ref, qk_ref, s0_ref, do_ref,
             dq_ref, dk_ref, dg_ref, du_ref, dw_ref, dqk_ref, ds_ref):
        @pl.when(pl.program_id(0) == 0)
        def _():
            ds_ref[...] = jnp.zeros_like(ds_ref)

        _, vjp = jax.vjp(_step_of_gates, s0_ref[...], q_ref[...], k_ref[...], g_ref[...], u_ref[...], w_ref[...], qk_ref[...])
        ds, dq, dk, dg, du, dw, dqk = vjp((do_ref[...], ds_ref[...]))
        for ref, val in zip((ds_ref, dq_ref, dk_ref, dg_ref, du_ref, dw_ref, dqk_ref), (ds, dq, dk, dg, du, dw, dqk)):
            ref[...] = val

    x_spec, qk_spec = (_head_major(CHUNK, w, lambda c: nc - 1 - c) for w in (HEAD_DIM, CHUNK))
    g_spec = pl.BlockSpec((CHUNK, LANES), lambda c: (nc - 1 - c, 0))
    s_spec = pl.BlockSpec((N_HEADS, None, HEAD_DIM, HEAD_DIM), lambda c: (0, nc - 1 - c, 0, 0))
    head_shape = [jax.ShapeDtypeStruct((N_HEADS, t, w_), F32) for w_ in (HEAD_DIM, HEAD_DIM, HEAD_DIM, HEAD_DIM, CHUNK)]
    return hosted_call(
        riders, body, name=name, grid=(nc,), in_specs=[x_spec, x_spec, g_spec, x_spec, x_spec, qk_spec, s_spec, x_spec],
        out_specs=[x_spec, x_spec, g_spec, x_spec, x_spec, qk_spec],
        out_shape=head_shape[:2] + [jax.ShapeDtypeStruct((t, LANES), F32)] + head_shape[2:],
        scratch_shapes=[pltpu.VMEM((N_HEADS, HEAD_DIM, HEAD_DIM), F32)],
        compiler_params=_params(("arbitrary",)),
    )(q, k, gates, u, w, qk, s0, do)


def delta_intra_bwd(q, k, v, gates, du, dw, dqk, dq_s, dk_s, dg_s, *, name, riders=()):
    t = q.shape[1]
    rows = min(INTRA_CHUNKS, t // CHUNK) * CHUNK

    def body(q_ref, k_ref, v_ref, g_ref, du_ref, dw_ref, dqk_ref, dqs_ref, dks_ref, dgs_ref, dq_ref, dk_ref, dv_ref, dg_ref):
        _, vjp = jax.vjp(_intra_of_gates, q_ref[...], k_ref[...], v_ref[...], g_ref[...])
        dq, dk, dv, dg = vjp((du_ref[...], dw_ref[...], dqk_ref[...]))
        dq_ref[...] = dq + dqs_ref[...]
        dk_ref[...] = dk + dks_ref[...]
        dv_ref[...] = dv
        dg_ref[...] = dg + dgs_ref[...]

    x_spec, qk_spec = (_head_major(rows, w, lambda i: i) for w in (HEAD_DIM, CHUNK))
    g_spec = pl.BlockSpec((rows, LANES), lambda i: (i, 0))
    return hosted_call(
        riders, body, name=name, grid=(t // rows,),
        in_specs=[x_spec] * 3 + [g_spec, x_spec, x_spec, qk_spec, x_spec, x_spec, g_spec],
        out_specs=[x_spec] * 3 + [g_spec],
        out_shape=[jax.ShapeDtypeStruct((N_HEADS, t, HEAD_DIM), F32)] * 3 + [jax.ShapeDtypeStruct((t, LANES), F32)],
        compiler_params=_params(("parallel",)),
    )(q, k, v, gates, du, dw, dqk, dq_s, dk_s, dg_s)


_V_BLOCK = 2 * N_HEADS
FOX_GROUPS = 16


def _fox_groups(t):
    nq = t // Q_BLOCK
    per = max(1, nq // FOX_GROUPS)
    return [(g0, per, (g0 + per) * Q_BLOCK) for g0 in range(0, nq, per)]


def fox_attn_fwd(q, k, proj, fq, fk, *, name, riders=()):
    t = q.shape[0]

    def body(q_ref, k_ref, v_ref, fq_ref, fk_ref, o_ref, kb_ref, vb_ref):
        head = pl.program_id(0)
        kb_ref[...] = k_ref[...].astype(BF16)
        vb_ref[...] = v_ref[...].astype(BF16)
        for g0, per, keys in _fox_groups(t):
            def block(j, carry, g0=g0, keys=keys):
                rows = pl.ds((g0 + j) * Q_BLOCK, Q_BLOCK)
                p = fox_probs(q_ref[rows, :].astype(BF16), kb_ref[0:keys, :], _lane_column(fq_ref[rows, :], head),
                              fk_ref[:, 0:keys], (g0 + j) * Q_BLOCK)
                o_ref[rows, :] = jnp.dot(p.astype(BF16), vb_ref[0:keys, :], preferred_element_type=F32)
                return carry
            for j in range(per):
                block(j, 0)

    x_spec = pl.BlockSpec((t, HEAD_DIM), lambda h: (0, h))
    v_spec = pl.BlockSpec((t, HEAD_DIM), lambda h: (0, _V_BLOCK + h))
    fq_spec = pl.BlockSpec((t, LANES), lambda h: (0, 0))
    fk_spec = pl.BlockSpec((None, 1, t), lambda h: (h, 0, 0))
    return hosted_call(
        riders, body, name=name, grid=(N_HEADS,), in_specs=[x_spec, x_spec, v_spec, fq_spec, fk_spec], out_specs=x_spec,
        out_shape=jax.ShapeDtypeStruct((t, D_MODEL), F32), scratch_shapes=[pltpu.VMEM((t, HEAD_DIM), BF16)] * 2,
        compiler_params=_params(("parallel",)),
    )(q, k, proj, fq, fk)


def fox_attn_bwd(q, k, proj, fq, fk, do, *, name, riders=()):
    t = q.shape[0]

    def body(q_ref, k_ref, v_ref, fq_ref, fk_ref, do_ref, dq_ref, dk_ref, dv_out_ref, dfq_ref, dfk_ref, kb_ref, vb_ref, dv_ref):
        head = pl.program_id(0)

        @pl.when(head == 0)
        def _():
            dfq_ref[...] = jnp.zeros_like(dfq_ref)

        kb_ref[...] = k_ref[...].astype(BF16)
        vb_ref[...] = v_ref[...].astype(BF16)
        dk_ref[...] = jnp.zeros_like(dk_ref)
        dv_ref[...] = jnp.zeros_like(dv_ref)
        dfk_ref[...] = jnp.zeros_like(dfk_ref)
        nt = (((1,), (1,)), ((), ()))
        tn = (((0,), (0,)), ((), ()))
        for g0, per, keys in _fox_groups(t):
            def block(j, carry, g0=g0, keys=keys):
                rows = pl.ds((g0 + j) * Q_BLOCK, Q_BLOCK)
                qb, dob = q_ref[rows, :].astype(BF16), do_ref[rows, :].astype(BF16)
                kb, vb = kb_ref[0:keys, :], vb_ref[0:keys, :]
                p = fox_probs(qb, kb, _lane_column(fq_ref[rows, :], head), fk_ref[:, 0:keys], (g0 + j) * Q_BLOCK)
                dp = lax.dot_general(dob, vb, nt, preferred_element_type=F32)
                dz = p * (dp - jnp.sum(dp * p, axis=-1, keepdims=True))
                pb, dzb = p.astype(BF16), dz.astype(BF16)
                dq_ref[rows, :] = jnp.dot(dzb, kb, preferred_element_type=F32)
                lane = lax.broadcasted_iota(jnp.int32, (Q_BLOCK, LANES), 1)
                dfq_ref[rows, :] += jnp.where(lane == head, jnp.sum(dz, axis=-1, keepdims=True), 0.0)
                dk_ref[0:keys, :] += lax.dot_general(dzb, qb, tn, preferred_element_type=F32)
                dv_ref[0:keys, :] += lax.dot_general(pb, dob, tn, preferred_element_type=F32)
                dfk_ref[:, 0:keys] -= jnp.sum(dz, axis=0, keepdims=True)
                return carry
            for j in range(per):
                block(j, 0)
        dv_out_ref[...] = dv_ref[...].astype(dv_out_ref.dtype)

    x_spec = pl.BlockSpec((t, HEAD_DIM), lambda h: (0, h))
    v_spec = pl.BlockSpec((t, HEAD_DIM), lambda h: (0, _V_BLOCK + h))
    fq_spec = pl.BlockSpec((t, LANES), lambda h: (0, 0))
    fk_spec = pl.BlockSpec((None, 1, t), lambda h: (h, 0, 0))
    return hosted_call(
        riders, body, name=name, grid=(N_HEADS,), in_specs=[x_spec, x_spec, v_spec, fq_spec, fk_spec, x_spec],
        out_specs=[x_spec, x_spec, x_spec, fq_spec, fk_spec],
        out_shape=[jax.ShapeDtypeStruct((t, D_MODEL), F32)] * 2 + [jax.ShapeDtypeStruct((t, D_MODEL), BF16)]
        + [jax.ShapeDtypeStruct((t, LANES), F32), jax.ShapeDtypeStruct((N_HEADS, 1, t), F32)],
        scratch_shapes=[pltpu.VMEM((t, HEAD_DIM), BF16)] * 2 + [pltpu.VMEM((t, HEAD_DIM), F32)],
        compiler_params=_params(("arbitrary",)),
    )(q, k, proj, fq, fk, do)


def memkv_fwd(mem, mnw, wkv, mknw, *, name):
    n = mem.shape[0]

    def body(mem_ref, mnw_ref, w_ref, mknw_ref, mk_ref, mv_ref):
        mk, mv = memkv_fn(mem_ref[...], mnw_ref[...], w_ref[...], mknw_ref[...])
        mk_ref[...] = mk
        mv_ref[...] = mv

    return pl.pallas_call(
        body, name=name, out_shape=[jax.ShapeDtypeStruct((n, MEM_WIDTH), F32)] * 2,
        compiler_params=pltpu.CompilerParams(vmem_limit_bytes=VMEM_LIMIT),
    )(mem, mnw, wkv, mknw)


def memkv_bwd(mem, mnw, wkv, mknw, dmk, dmv, *, name):
    def body(mem_ref, mnw_ref, w_ref, mknw_ref, dmk_ref, dmv_ref, dmnw_ref, dw_ref, dmknw_ref):
        f = functools.partial(memkv_fn, mem_ref[...])
        _, vjp = jax.vjp(f, mnw_ref[...], w_ref[...].astype(F32), mknw_ref[...])
        dmnw, dw, dmknw = vjp((dmk_ref[...], dmv_ref[...]))
        dmnw_ref[...] = dmnw
        dw_ref[...] = dw.astype(dw_ref.dtype)
        dmknw_ref[...] = dmknw

    return pl.pallas_call(
        body, name=name,
        out_shape=[jax.ShapeDtypeStruct(mnw.shape, F32), jax.ShapeDtypeStruct(wkv.shape, BF16), jax.ShapeDtypeStruct(mknw.shape, F32)],
        compiler_params=pltpu.CompilerParams(vmem_limit_bytes=VMEM_LIMIT),
    )(mem, mnw, wkv, mknw, dmk, dmv)


def _row(v, width=None):
    v = v.reshape(1, -1)
    if width is not None and v.shape[1] < width:
        v = jnp.pad(v, ((0, 0), (0, width - v.shape[1])))
    return v


def _norm_fwd(x, w, name, riders=()):
    return rows_call(lambda x, w: rms(x, w), [x], [w], [(D_MODEL, BF16)], [], tm=512, name=name, riders=riders)[0]


FF_PIECE = D_FF // N_DEV


def _add(r, x, *rows):
    return r + x


def _norm_rows(r, w):
    return rms(r, w)


def _norm_bwd_post(dh, x, dx_in, w):
    _, vjp = jax.vjp(rms, x, w)
    dx, dw = vjp(dh)
    return dx + dx_in, dw


def _piece(rows, cols, index):
    return pl.BlockSpec((None, rows, cols), lambda i, j, kk: (index(i, j, kk), 0, 0))


def _two_pieces(rows, cols, index):
    return pl.BlockSpec((2, rows, cols), lambda i, j, kk: (index(i, j, kk), 0, 0))


def _mlp_fwd(x, h2, w1, w2, layer, riders=(), next_norm_w=None):
    riders = list(riders) + [None, None]
    u, a1 = matmul(h2, w1, name=f"mlp1_fwd_{layer}", tiles=(None, FF_PIECE, D_MODEL),
                   extra_out=(lambda u: jnp.square(jnp.maximum(u, 0.0)), BF16),
                   b_view=(D_MODEL, D_FF, _piece(D_MODEL, FF_PIECE, lambda i, j, kk: j)), riders=riders[0])
    norm = dict(row_ins=[next_norm_w], extra_out=(_norm_rows, BF16)) if next_norm_w is not None else {}
    y = matmul(a1, w2, name=f"mlp2_fwd_{layer}", post=_add, post_ins=[x], tiles=(None, D_MODEL, 2 * FF_PIECE),
               b_view=(D_FF, D_MODEL, _two_pieces(FF_PIECE, D_MODEL, lambda i, j, kk: kk)), riders=riders[1], **norm)
    return y, (x, h2, u, a1)


def pair_sum(g, got, *, name):
    _, rows, cols = g.shape
    tile = _pick(rows, (512, 256, 128))
    c = lax.axis_index("c").astype(jnp.int32).reshape(1)

    def body(c_ref, a_ref, b_ref, o_ref):
        o_ref[...] = (a_ref[...].astype(F32) + b_ref[...].astype(F32)).astype(o_ref.dtype)

    grid_spec = pltpu.PrefetchScalarGridSpec(
        num_scalar_prefetch=1, grid=(4, rows // tile),
        in_specs=[pl.BlockSpec((None, tile, cols), lambda k, i, c_ref: (2 * k + c_ref[0], i, 0)),
                  pl.BlockSpec((None, tile, cols), lambda k, i, c_ref: (k, i, 0))],
        out_specs=pl.BlockSpec((None, tile, cols), lambda k, i, c_ref: (k, i, 0)))
    return pl.pallas_call(
        body, name=name, grid_spec=grid_spec, out_shape=jax.ShapeDtypeStruct((4, rows, cols), g.dtype),
        compiler_params=_params(("parallel", "parallel")),
    )(c, g, got)


def chip_sums(names, pieces, gots):
    return [pair_sum(a, got, name=f"grads_pair_sum_{n}") for n, a, got in zip(names, pieces, gots)]


def _mlp_bwd(dy, res, n2w, w1, w2, layer, riders=()):
    x, h2, u, a1 = res
    du = matmul(dy, w2, tb=True, name=f"mlp2_dx_{layer}", out_dtype=BF16, tiles=(None, 2 * FF_PIECE, D_MODEL),
                post=lambda r, u: r * (2.0 * jnp.maximum(u, 0.0)), post_ins=[u],
                b_view=(D_MODEL, D_FF, _two_pieces(FF_PIECE, D_MODEL, lambda i, j, kk: j)), riders=riders)
    dw2 = matmul(a1, dy, ta=True, name=f"mlp2_dw_{layer}", out_dtype=BF16, tiles=(FF_PIECE, D_MODEL, None), out_view=(
        w2.shape, _piece(FF_PIECE, D_MODEL, lambda i, j, kk: i)))
    sib2 = sibling_rider([dw2])
    dx, dn2w = matmul(du, w1, tb=True, name=f"mlp1_dx_{layer}", tiles=(None, D_MODEL, FF_PIECE),
                      b_view=(D_FF, D_MODEL, _piece(D_MODEL, FF_PIECE, lambda i, j, kk: kk)),
                      post=_norm_bwd_post, post_ins=[x, dy], row_ins=[n2w], acc=True, riders=[sib2])
    dw1 = matmul(h2, du, ta=True, name=f"mlp1_dw_{layer}", out_dtype=BF16, tiles=(D_MODEL, FF_PIECE, None), out_view=(
        w1.shape, _piece(D_MODEL, FF_PIECE, lambda i, j, kk: j)))
    return dx, dw1, dw2, dn2w, sibling_rider([dw1]), sib2


def _in_proj_bwd(h, dmain, dsmall, w_main, w_small, x, dx_in, n1w, tag, riders=()):
    dh = matmul(dmain, w_main, tb=True, name=f"inproj_dx_main_{tag}", riders=riders)

    def post(r, dh_main, x, dx_in, w):
        return _norm_bwd_post(r + dh_main, x, dx_in, w)

    dx, dn1w = matmul(dsmall, w_small, tb=True, name=f"inproj_dx_small_{tag}", tiles=(None, D_MODEL, None),
                      post=post, post_ins=[dh, x, dx_in], row_ins=[n1w], acc=True)
    dw_main = matmul(h, dmain, ta=True, out_dtype=BF16, name=f"inproj_dw_main_{tag}")
    dw_small = matmul(h, dsmall, ta=True, out_dtype=BF16, name=f"inproj_dw_small_{tag}")
    return dx, dn1w, dw_main, dw_small


def local_step(x, mem, target, w, m, v):
    t = x.shape[0]
    n_mem = mem.shape[0]
    g = {}

    def wire(a):
        return a.astype(BF16)

    (dn_g,), = run_riders([gather_rider([wire(w["dn_w_in"][0])])], name="weights_gather_first")
    dn_main, dn_ab = in_proj_weights(dn_g, DN_IN, 2 * N_HEADS)
    fox_w = wire(w["fox_w_in"][0])
    ride_out = gather_rider([wire(w["w_out"][0]), wire(w["w_out"][1]), w["dn_conv_w"][0]])
    ride_kv = gather_rider([wire(w["w_mem_kv"])])
    ride_mlp1_0 = gather_rider([wire(w["w_mlp1"][0])])
    ride_mlp2_0 = gather_rider([wire(w["w_mlp2"][0])])
    ride_fox_a, ride_fox_b = gather_rider([fox_w[:D_MODEL // 2]]), gather_rider([fox_w[D_MODEL // 2:]])
    ride_mlp_1 = gather_rider([wire(w["w_mlp1"][1]), wire(w["w_mlp2"][1])])
    mnw, mknw = _row(w["mem_norm_w"]), _row(w["mem_k_norm_w"])

    n1w0, n2w0 = _row(w["norm1_w"][0]), _row(w["norm2_w"][0])
    n1w1, n2w1 = _row(w["norm1_w"][1]), _row(w["norm2_w"][1])
    alog, dtb = _row(w["dn_a_log"][0], LANES), _row(w["dn_dt_bias"][0], LANES)
    onw, mqw0 = _row(w["dn_o_norm_w"][0]), _row(w["memq_norm_w"][0])
    x0 = x
    h0 = _norm_fwd(x0, n1w0, "norm1_fwd_0")
    pm0 = matmul(h0, dn_main, name="inproj_main_0", riders=[ride_out])
    w_out0, w_out1 = (a.reshape(OUT_IN, D_MODEL) for a in ride_out.results[:2])
    conv_w = ride_out.results[2].transpose(1, 0, 2).reshape(CONV_WIDTH, 3 * D_MODEL)
    ps0 = matmul(h0, dn_ab, name="inproj_small_0")
    gates = rows_call(dn_gates_fn, [ps0], [alog, dtb], [(LANES, F32)], [], tm=512, name="dn_gates_fwd")[0]
    q0, k0, v0 = dn_prep_fwd(pm0, conv_w, name="dn_prep_fwd", riders=[ride_kv])
    w_kv = ride_kv.results[0].reshape(D_MODEL, D_MODEL)
    mk, mv = memkv_fwd(mem, mnw, w_kv, mknw, name="memkv_fwd")
    u0, w0, qk0 = delta_intra_fwd(q0, k0, v0, gates, name="delta_intra_fwd", riders=[ride_mlp1_0])
    o0, s_start = delta_seq_fwd(q0, k0, gates, u0, w0, qk0, name="delta_seq_fwd", riders=[ride_mlp2_0])
    cat0 = rows_call(dn_out_fn, [o0, (pm0, D_MODEL, 3), (pm0, MEM_WIDTH, 8)], [onw, mqw0, mk, mv],
                     [(D_MODEL + MEM_WIDTH, BF16)], [], tm=256, name="dn_out_fwd")[0]
    (w1_0,), (w2_0,) = ride_mlp1_0.results, ride_mlp2_0.results
    x1, h2_0 = matmul(cat0, w_out0, post=_add, post_ins=[x0], row_ins=[n2w0], extra_out=(_norm_rows, BF16),
                      tiles=(None, D_MODEL, None), name="wout_fwd_0")
    (x2, h1), mlp_res0 = _mlp_fwd(x1, h2_0, w1_0, w2_0, 0, riders=[[ride_fox_a], [ride_fox_b]], next_norm_w=n1w1)
    fox_main, fox_f = in_proj_weights(
        jnp.concatenate([ride_fox_a.results[0], ride_fox_b.results[0]], axis=1), FOX_IN, N_HEADS)

    fbias = _row(w["fox_f_bias"][0], LANES)
    qnw, knw, mqw1 = _row(w["fox_q_norm_w"][0]), _row(w["fox_k_norm_w"][0]), _row(w["memq_norm_w"][1])
    pm1 = matmul(h1, fox_main, name="inproj_main_1")
    ps1 = matmul(h1, fox_f, name="inproj_small_1")
    fq = rows_call(fox_fcum_fn, [ps1], [fbias], [(LANES, F32)], [], tm=t, name="fox_fcum_fwd")[0]
    fk = fq[:, :N_HEADS].T[:, None, :]
    q1, k1 = rows_call(fox_qk_fn, [(pm1, D_MODEL, 0), (pm1, D_MODEL, 1)], [qnw, knw], [(D_MODEL, F32)] * 2, [], tm=256,
                       name="fox_qk_fwd")
    o1 = fox_attn_fwd(q1, k1, pm1, fq, fk, name="fox_attn_fwd", riders=[ride_mlp_1])
    cat1 = rows_call(fox_out_fn, [o1, (pm1, D_MODEL, 3), (pm1, MEM_WIDTH, 8)], [mqw1, mk, mv],
                     [(D_MODEL + MEM_WIDTH, BF16)], [], tm=256, name="fox_out_fwd")[0]
    w1_1, w2_1 = ride_mlp_1.results
    x3, h2_1 = matmul(cat1, w_out1, post=_add, post_ins=[x2], row_ins=[n2w1], extra_out=(_norm_rows, BF16),
                      tiles=(None, D_MODEL, None), name="wout_fwd_1")
    y, mlp_res1 = _mlp_fwd(x3, h2_1, w1_1, w2_1, 1)

    def loss_fn(y, tgt):
        e = y - tgt
        return e * (1.0 / D_MODEL), jnp.sum(jnp.sum(e * e, axis=1, keepdims=True), axis=0, keepdims=True)
    dy, sq = rows_call(loss_fn, [y, target], [], [(D_MODEL, F32)], [(1, 1)], tm=512, name="loss")
    loss = sq[0, 0] * (0.5 / D_MODEL)

    dx3, dw1_1, dw2_1, dn2w1, sib1, sib2 = _mlp_bwd(dy, mlp_res1, n2w1, w1_1, w2_1, 1)
    dcat1 = matmul(dx3, w_out1, tb=True, name="wout_dx_1", riders=[sib1])
    dwo_1 = matmul(cat1, dx3, ta=True, out_dtype=BF16, name="wout_dw_1").reshape(N_DEV, OUT_IN // N_DEV, D_MODEL)
    sibo = sibling_rider([dwo_1])
    do1, dgate1, dqm1, dmqw1, dmk1, dmv1 = rows_call(
        functools.partial(vjp_rows(fox_out_fn, 3, (True, True, True)), n_row=3, n_ct=1),
        [o1, (pm1, D_MODEL, 3), (pm1, MEM_WIDTH, 8), dcat1], [mqw1, mk, mv],
        [(D_MODEL, F32), (D_MODEL, BF16), (MEM_WIDTH, BF16)], [(1, HEAD_DIM), (n_mem, MEM_WIDTH), (n_mem, MEM_WIDTH)],
        tm=256, name="fox_out_bwd", riders=[sibo])
    h_l1 = chip_sums(["w_mlp2_1", "w_mlp1_1", "w_out_1"], [dw2_1, dw1_1, dwo_1], sib2.results + sib1.results + sibo.results)
    ride_l1, ride_l1_out = chips_rider(h_l1[:2]), chips_rider(h_l1[2:])
    dq1, dk1, dv1, dfq, dfk = fox_attn_bwd(q1, k1, pm1, fq, fk, do1, name="fox_attn_bwd", riders=[ride_l1])
    dqraw1, dkraw1, dqnw, dknw = rows_call(
        functools.partial(vjp_rows(fox_qk_fn, 2, (True, True)), n_row=2, n_ct=2),
        [(pm1, D_MODEL, 0), (pm1, D_MODEL, 1), dq1, dk1], [qnw, knw],
        [(D_MODEL, BF16)] * 2, [(1, HEAD_DIM)] * 2, tm=256, name="fox_qk_bwd")
    dfcum = dfq + jnp.pad(dfk[:, 0, :].T, ((0, 0), (0, LANES - N_HEADS)))
    dps1, dfbias = rows_call(
        functools.partial(vjp_rows(fox_fcum_fn, 1, (True,)), n_row=1, n_ct=1),
        [ps1, dfcum], [fbias], [(LANES, F32)], [(1, LANES)], tm=t, name="fox_fcum_bwd")
    dpm1 = jnp.concatenate([dqraw1, dkraw1, dv1, dgate1, dqm1], axis=1)
    dx2, dn1w1, dwmain1, dwsmall1 = _in_proj_bwd(h1, dpm1, dps1, fox_main, fox_f, x2, dx3, n1w1, "1", riders=[ride_l1_out])
    g_fox = in_proj_pieces(dwmain1, dwsmall1, N_HEADS, FOX_IN)
    sibf = sibling_rider([g_fox])

    dx1, dw1_0, dw2_0, dn2w0, sib1, sib2 = _mlp_bwd(dx2, mlp_res0, n2w0, w1_0, w2_0, 0, riders=[sibf])
    ride_fox_g = chips_rider(chip_sums(["fox_w_in"], [g_fox], sibf.results))
    dcat0 = matmul(dx1, w_out0, tb=True, name="wout_dx_0", riders=[sib1])
    dwo_0 = matmul(cat0, dx1, ta=True, out_dtype=BF16, name="wout_dw_0").reshape(N_DEV, OUT_IN // N_DEV, D_MODEL)
    sibo = sibling_rider([dwo_0])
    do0, dz0, dqm0, donw, dmqw0, dmk0, dmv0 = rows_call(
        functools.partial(vjp_rows(dn_out_fn, 3, (True, True, True, True)), n_row=3, n_ct=1),
        [o0, (pm0, D_MODEL, 3), (pm0, MEM_WIDTH, 8), dcat0], [onw, mqw0, mk, mv],
        [((N_HEADS, HEAD_DIM), F32), (D_MODEL, BF16), (MEM_WIDTH, BF16)],
        [(1, HEAD_DIM), (1, HEAD_DIM), (n_mem, MEM_WIDTH), (n_mem, MEM_WIDTH)], tm=256, name="dn_out_bwd", riders=[sibo])
    h_l0 = chip_sums(["w_mlp2_0", "w_mlp1_0", "w_out_0"], [dw2_0, dw1_0, dwo_0], sib2.results + sib1.results + sibo.results)
    ride_l0_mlp2, ride_l0_rest = chips_rider(h_l0[:1]), chips_rider(h_l0[1:])
    dmnw, dwkv, dmknw = memkv_bwd(mem, mnw, w_kv, mknw, dmk0 + dmk1, dmv0 + dmv1, name="memkv_bwd")
    g_kv = dwkv.reshape(N_DEV, D_MODEL // N_DEV, D_MODEL)
    sibk = sibling_rider([g_kv])
    dq_s, dk_s, dg_s, du0, dw0, dqk0 = delta_seq_bwd(q0, k0, gates, u0, w0, qk0, s_start, do0, name="delta_seq_bwd",
                                                     riders=[ride_fox_g, sibk])
    ride_kv_g = chips_rider(chip_sums(["w_mem_kv"], [g_kv], sibk.results))
    dq0, dk0, dv0, dgates = delta_intra_bwd(q0, k0, v0, gates, du0, dw0, dqk0, dq_s, dk_s, dg_s,
                                            name="delta_intra_bwd", riders=[ride_l0_mlp2, ride_kv_g])
    dxq, dxk, dxv, dcq, dck, dcv = dn_prep_bwd(pm0, conv_w, dq0, dk0, dv0, name="dn_prep_bwd", riders=[ride_l0_rest])
    dconv = jnp.concatenate([dcq, dck, dcv], axis=1)
    dps0, dalog, ddtb = rows_call(
        functools.partial(vjp_rows(dn_gates_fn, 1, (True, True)), n_row=1, n_ct=1),
        [ps0, dgates], [alog, dtb], [(LANES, F32)], [(1, LANES)] * 2, tm=512, name="dn_gates_bwd")
    dpm0 = jnp.concatenate([dxq, dxk, dxv, dz0, dqm0], axis=1)
    grad_x, dn1w0, dwmain0, dwsmall0 = _in_proj_bwd(h0, dpm0, dps0, dn_main, dn_ab, x0, dx1, n1w0, "0")
    g_dn = in_proj_pieces(dwmain0, dwsmall0, 2 * N_HEADS, DN_IN)
    g_conv = dconv.reshape(CONV_WIDTH, N_DEV, -1).transpose(1, 0, 2).astype(BF16)

    g["mem_norm_w"] = dmnw[0]
    g["mem_k_norm_w"] = dmknw[0]
    g["norm1_w"] = jnp.concatenate([dn1w0, dn1w1], axis=0)
    g["dn_a_log"] = dalog[:, :N_HEADS]
    g["dn_dt_bias"] = ddtb[:, :N_HEADS]
    g["dn_o_norm_w"] = donw
    g["fox_f_bias"] = dfbias[:, :N_HEADS]
    g["fox_q_norm_w"] = dqnw
    g["fox_k_norm_w"] = dknw
    g["memq_norm_w"] = jnp.concatenate([dmqw0, dmqw1], axis=0)
    g["norm2_w"] = jnp.concatenate([dn2w0, dn2w1], axis=0)

    sibd = sibling_rider([g_dn, g_conv])
    run_riders([sibd], name="grads_to_sibling_last")
    ride_last = chips_rider(chip_sums(["dn_w_in", "dn_conv_w"], [g_dn, g_conv], sibd.results))
    ride_small = gather_rider([pack_small(g, last=loss)])
    run_riders([ride_last, ride_small], name="grads_to_chips_last")

    def layers(l0, l1):
        return jnp.stack([l0, l1], axis=1).reshape(4, -1, l0.shape[-1])

    parts = {
        "w_mlp1": layers(ride_l0_rest.results[0], ride_l1.results[1]),
        "w_mlp2": layers(ride_l0_mlp2.results[0], ride_l1.results[0]),
        "w_out": layers(ride_l0_rest.results[1], ride_l1_out.results[0]),
        "fox_w_in": ride_fox_g.results[0], "w_mem_kv": ride_kv_g.results[0],
        "dn_w_in": ride_last.results[0], "dn_conv_w": ride_last.results[1],
    }
    out = {n: adamw(parts[n], w[n], m[n], v[n], name=f"adamw_{n}") for n, _, _ in BIG}
    small = adamw(ride_small.results[0], pack_small(w), pack_small(m), pack_small(v), name="adamw_small")
    loss = small[0][-1, -1]
    return loss, grad_x, out, small


WEIGHTS = ["mem_norm_w", "w_mem_kv", "mem_k_norm_w", "norm1_w", "dn_w_in", "dn_conv_w", "dn_a_log", "dn_dt_bias",
           "dn_o_norm_w", "fox_w_in", "fox_f_bias", "fox_q_norm_w", "fox_k_norm_w", "memq_norm_w", "w_out", "norm2_w",
           "w_mlp1", "w_mlp2"]
DN_IN = 4 * D_MODEL + 2 * N_HEADS + MEM_WIDTH
FOX_IN = 4 * D_MODEL + N_HEADS + MEM_WIDTH
GATE_END = 4 * D_MODEL
OUT_IN = D_MODEL + MEM_WIDTH
BIG = [("w_mem_kv", D_MODEL // N_DEV, D_MODEL), ("dn_w_in", D_MODEL, DN_IN // N_DEV), ("fox_w_in", D_MODEL, FOX_IN // N_DEV),
       ("dn_conv_w", CONV_WIDTH, 3 * D_MODEL // N_DEV), ("w_out", 2 * OUT_IN // N_DEV, D_MODEL),
       ("w_mlp1", 2 * D_MODEL, FF_PIECE), ("w_mlp2", 2 * FF_PIECE, D_MODEL)]
SMALL_TILE = 8 * LANES
SMALL = [(name, shape, -(-math.prod(shape) // SMALL_TILE) * SMALL_TILE) for name, shape in [
    ("mem_norm_w", (D_MODEL,)), ("mem_k_norm_w", (HEAD_DIM,)), ("norm1_w", (2, D_MODEL)), ("dn_a_log", (1, N_HEADS)),
    ("dn_dt_bias", (1, N_HEADS)), ("dn_o_norm_w", (1, HEAD_DIM)), ("fox_f_bias", (1, N_HEADS)),
    ("fox_q_norm_w", (1, HEAD_DIM)), ("fox_k_norm_w", (1, HEAD_DIM)), ("memq_norm_w", (2, HEAD_DIM)), ("norm2_w", (2, D_MODEL))]]
SMALL_ROWS = sum(ln for _, _, ln in SMALL) // LANES + 8


def pack_small(p, last=None):
    def rows(a, ln):
        a = a.reshape(-1)
        return (a if a.shape[0] == ln else jnp.pad(a, (0, ln - a.shape[0]))).reshape(-1, LANES)

    used = sum(ln for _, _, ln in SMALL) // LANES
    tail = jnp.zeros(((SMALL_ROWS - used) * LANES,), F32)
    if last is not None:
        tail = jnp.concatenate([tail[:-1], last.reshape(1)])
    return jnp.concatenate([rows(p[n], ln) for n, _, ln in SMALL] + [tail.reshape(-1, LANES)], axis=0)


def unpack_small(pk):
    row, out = 0, {}
    for n, sh, ln in SMALL:
        out[n] = pk[row:row + ln // LANES].reshape(-1)[:math.prod(sh)].reshape(sh)
        row += ln // LANES
    return out


def in_proj_weights(gathered, width, n_small):
    full = gathered.transpose(1, 0, 2).reshape(D_MODEL, width)
    main = jnp.concatenate([full[:, :GATE_END], full[:, GATE_END + n_small:]], axis=1)
    return main, jnp.pad(full[:, GATE_END:GATE_END + n_small], ((0, 0), (0, LANES - n_small)))


def in_proj_pieces(d_main, d_small, n_small, width):
    full = jnp.concatenate([d_main[:, :GATE_END], d_small[:, :n_small], d_main[:, GATE_END:]], axis=1)
    return full.reshape(D_MODEL, N_DEV, width // N_DEV).transpose(1, 0, 2)


def adamw(parts, w, m, v, *, name):
    n, _, cols = parts.shape
    layers = w.shape[0] if w.ndim == 3 else 1
    rows = w.shape[-2]
    tile = _pick(rows, (512, 256, 128))
    steps = rows // tile

    def body(p_ref, w_ref, m_ref, v_ref, g_ref, d_ref, mo_ref, vo_ref):
        g = p_ref[0].astype(F32)
        for i in range(1, n):
            g = g + p_ref[i].astype(F32)
        m_new = ADAM_B1 * m_ref[...] + (1.0 - ADAM_B1) * g
        v_new = ADAM_B2 * v_ref[...] + (1.0 - ADAM_B2) * jnp.square(g)
        m_hat = m_new / (1.0 - ADAM_B1 ** ADAM_STEP)
        v_hat = v_new / (1.0 - ADAM_B2 ** ADAM_STEP)
        g_ref[...] = g
        d_ref[...] = -ADAM_LR * (m_hat / (jnp.sqrt(v_hat) + ADAM_EPS) + ADAM_WD * w_ref[...])
        mo_ref[...] = m_new
        vo_ref[...] = v_new

    if w.ndim == 3:
        spec = pl.BlockSpec((None, tile, cols), lambda l, i: (l, i, 0))
    else:
        spec = pl.BlockSpec((tile, cols), lambda l, i: (i, 0))
    return pl.pallas_call(
        body, name=name, grid=(layers, steps),
        in_specs=[pl.BlockSpec((n, tile, cols), lambda l, i: (0, l * steps + i, 0)), spec, spec, spec], out_specs=[spec] * 4,
        out_shape=[jax.ShapeDtypeStruct(w.shape, F32)] * 4, compiler_params=_params(("parallel", "parallel")),
    )(parts, w, m, v)


def kernel(x, mem, mem_norm_w, w_mem_kv, mem_k_norm_w, norm1_w, dn_w_in, dn_conv_w, dn_a_log, dn_dt_bias, dn_o_norm_w, fox_w_in, fox_f_bias, fox_q_norm_w, fox_k_norm_w, memq_norm_w, w_out, norm2_w, w_mlp1, w_mlp2, loss_target, m_mem_norm_w, m_w_mem_kv, m_mem_k_norm_w, m_norm1_w, m_dn_w_in, m_dn_conv_w, m_dn_a_log, m_dn_dt_bias, m_dn_o_norm_w, m_fox_w_in, m_fox_f_bias, m_fox_q_norm_w, m_fox_k_norm_w, m_memq_norm_w, m_w_out, m_norm2_w, m_w_mlp1, m_w_mlp2, v_mem_norm_w, v_w_mem_kv, v_mem_k_norm_w, v_norm1_w, v_dn_w_in, v_dn_conv_w, v_dn_a_log, v_dn_dt_bias, v_dn_o_norm_w, v_fox_w_in, v_fox_f_bias, v_fox_q_norm_w, v_fox_k_norm_w, v_memq_norm_w, v_w_out, v_norm2_w, v_w_mlp1, v_w_mlp2):
    p = dict(mem_norm_w=mem_norm_w, w_mem_kv=w_mem_kv, mem_k_norm_w=mem_k_norm_w, norm1_w=norm1_w, dn_w_in=dn_w_in,
             dn_conv_w=dn_conv_w, dn_a_log=dn_a_log, dn_dt_bias=dn_dt_bias, dn_o_norm_w=dn_o_norm_w, fox_w_in=fox_w_in,
             fox_f_bias=fox_f_bias, fox_q_norm_w=fox_q_norm_w, fox_k_norm_w=fox_k_norm_w, memq_norm_w=memq_norm_w,
             w_out=w_out, norm2_w=norm2_w, w_mlp1=w_mlp1, w_mlp2=w_mlp2)
    pm = dict(mem_norm_w=m_mem_norm_w, w_mem_kv=m_w_mem_kv, mem_k_norm_w=m_mem_k_norm_w, norm1_w=m_norm1_w,
              dn_w_in=m_dn_w_in, dn_conv_w=m_dn_conv_w, dn_a_log=m_dn_a_log, dn_dt_bias=m_dn_dt_bias,
              dn_o_norm_w=m_dn_o_norm_w, fox_w_in=m_fox_w_in, fox_f_bias=m_fox_f_bias, fox_q_norm_w=m_fox_q_norm_w,
              fox_k_norm_w=m_fox_k_norm_w, memq_norm_w=m_memq_norm_w, w_out=m_w_out, norm2_w=m_norm2_w, w_mlp1=m_w_mlp1,
              w_mlp2=m_w_mlp2)
    pv = dict(mem_norm_w=v_mem_norm_w, w_mem_kv=v_w_mem_kv, mem_k_norm_w=v_mem_k_norm_w, norm1_w=v_norm1_w,
              dn_w_in=v_dn_w_in, dn_conv_w=v_dn_conv_w, dn_a_log=v_dn_a_log, dn_dt_bias=v_dn_dt_bias,
              dn_o_norm_w=v_dn_o_norm_w, fox_w_in=v_fox_w_in, fox_f_bias=v_fox_f_bias, fox_q_norm_w=v_fox_q_norm_w,
              fox_k_norm_w=v_fox_k_norm_w, memq_norm_w=v_memq_norm_w, w_out=v_w_out, norm2_w=v_norm2_w, w_mlp1=v_w_mlp1,
              w_mlp2=v_w_mlp2)

    loss, grad_x, results, small = local_step(x[0], mem[0], loss_target[0], p, pm, pv)
    small = [unpack_small(o) for o in small]
    groups = [{**small[i], **{n: r[i] for n, r in results.items()}} for i in range(4)]
    return (loss, grad_x[None], *[grp[n] for grp in groups for n in WEIGHTS])
```

```python
import functools
import math

import jax
import jax.numpy as jnp
from jax import lax
from jax.experimental import pallas as pl
from jax.experimental.pallas import tpu as pltpu

F32 = jnp.float32
BF16 = jnp.bfloat16
HIGHEST = lax.Precision.HIGHEST

D_MODEL = 1024
HEAD_DIM = 128
N_HEADS = 8
MEM_HEADS = 4
MEM_WIDTH = MEM_HEADS * HEAD_DIM
D_FF = 4 * D_MODEL
CONV_WIDTH = 4
CHUNK = 64
Q_BLOCK = 128
EPS = 1e-6
SCALE = HEAD_DIM ** -0.5
MAIN_WIDTH = 4 * D_MODEL + MEM_WIDTH
LANES = 128
N_DEV = 8

ADAM_LR = 0.001
ADAM_B1 = 0.9
ADAM_B2 = 0.999
ADAM_EPS = 1e-08
ADAM_WD = 0.01
ADAM_STEP = 10

VMEM_LIMIT = 56 * 2 ** 20
MESH = pl.DeviceIdType.MESH


def _bdot(a, b, dims):
    return lax.dot_general(a.astype(BF16), b.astype(BF16), (dims, ((), ())), preferred_element_type=F32)


@jax.custom_vjp
def mm(a, b):
    return _bdot(a, b, ((1,), (0,)))


@jax.custom_vjp
def mm_nt(a, b):
    return _bdot(a, b, ((1,), (1,)))


@jax.custom_vjp
def mm_tn(a, b):
    return _bdot(a, b, ((0,), (0,)))


mm.defvjp(lambda a, b: (mm(a, b), (a, b)), lambda r, g: (mm_nt(g, r[1]), mm_tn(r[0], g)))
mm_nt.defvjp(lambda a, b: (mm_nt(a, b), (a, b)), lambda r, g: (mm(g, r[1]), mm_tn(g, r[0])))
mm_tn.defvjp(lambda a, b: (mm_tn(a, b), (a, b)), lambda r, g: (mm_nt(r[1], g), mm(r[0], g)))


def hdot(a, b):
    return jnp.dot(a, b, precision=HIGHEST, preferred_element_type=F32)


def rms(x, w):
    return x * lax.rsqrt(jnp.mean(x * x, axis=-1, keepdims=True) + EPS) * w


def l2n(x):
    return x * lax.rsqrt(jnp.sum(x * x, axis=-1, keepdims=True) + EPS)


def _iota2(n, m):
    return lax.broadcasted_iota(jnp.int32, (n, m), 0), lax.broadcasted_iota(jnp.int32, (n, m), 1)


def _lower_ones(n):
    r, c = _iota2(n, n)
    return jnp.where(r >= c, 1.0, 0.0).astype(F32)


def _last_row(x):
    r = lax.broadcasted_iota(jnp.int32, x.shape, 0)
    return jnp.sum(jnp.where(r == x.shape[0] - 1, x, 0.0), axis=0, keepdims=True)


def _softmax_rows(z):
    m = lax.stop_gradient(jnp.max(z, axis=-1, keepdims=True))
    e = jnp.exp(z - m)
    return e * (1.0 / jnp.sum(e, axis=-1, keepdims=True))


_BNN = (((2,), (1,)), ((0,), (0,)))
_BNT = (((2,), (2,)), ((0,), (0,)))
_BTN = (((1,), (1,)), ((0,), (0,)))


def _bbdot(a, b, dims):
    return lax.dot_general(a.astype(BF16), b.astype(BF16), dims, preferred_element_type=F32)


@jax.custom_vjp
def bmm(a, b):
    return _bbdot(a, b, _BNN)


@jax.custom_vjp
def bmm_nt(a, b):
    return _bbdot(a, b, _BNT)


@jax.custom_vjp
def bmm_tn(a, b):
    return _bbdot(a, b, _BTN)


@jax.custom_vjp
def bmm_high(a, b):
    return lax.dot_general(a, b, _BNN, precision=lax.Precision.HIGH, preferred_element_type=F32)


bmm.defvjp(lambda a, b: (bmm(a, b), (a, b)), lambda r, g: (bmm_nt(g, r[1]), bmm_tn(r[0], g)))
bmm_nt.defvjp(lambda a, b: (bmm_nt(a, b), (a, b)), lambda r, g: (bmm(g, r[1]), bmm_tn(g, r[0])))
bmm_tn.defvjp(lambda a, b: (bmm_tn(a, b), (a, b)), lambda r, g: (bmm_nt(r[1], g), bmm(r[0], g)))
bmm_high.defvjp(lambda a, b: (bmm_high(a, b), (a, b)), lambda r, g: (bmm_nt(g, r[1]), bmm_tn(r[0], g)))

NEUMANN_HIGH_LEVELS = 2


@jax.custom_vjp
def inv_unit_lower(a):
    n = a.shape[-1]
    r, c = _iota2(n, n)
    p = jnp.where(r == c, 1.0, 0.0).astype(F32) - a
    ak = a
    for level in range(int(math.log2(n)) - 1):
        dot = bmm_high if level < NEUMANN_HIGH_LEVELS else bmm
        ak = dot(ak, ak)
        p = p + dot(p, ak)
    return p


def _inv_unit_lower_fwd(a):
    t = inv_unit_lower(a)
    return t, t


def _inv_unit_lower_bwd(t, g):
    return (-bmm_tn(t, bmm_nt(g, t)),)


inv_unit_lower.defvjp(_inv_unit_lower_fwd, _inv_unit_lower_bwd)


def delta_intra(q, k, v, gc, beta):
    b, c, _ = q.shape
    r, cc = _iota2(c, c)
    causal = r >= cc
    strict = r > cc
    gi = jnp.broadcast_to(gc, (b, c, c))
    gj = jnp.swapaxes(gi, 1, 2)
    decay = jnp.where(causal, jnp.exp(jnp.where(causal, gi - gj, 0.0)), 0.0)
    kb = k * beta
    a = jnp.where(strict, bmm_nt(kb, k) * decay, 0.0)
    t = inv_unit_lower(a)
    u = bmm(t, v * beta)
    w = bmm(t, kb * jnp.exp(gc))
    qk = jnp.where(causal, bmm_nt(q, k) * decay, 0.0)
    return u, w, qk


def delta_step(s, q, k, gc, u, w, qk):
    v_new = u - bmm(w, s)
    out = bmm(q * jnp.exp(gc), s) + bmm(qk, v_new)
    r = lax.broadcasted_iota(jnp.int32, gc.shape, 1)
    g_last = jnp.sum(jnp.where(r == gc.shape[1] - 1, gc, 0.0), axis=1, keepdims=True)
    k_dec = k * jnp.exp(g_last - gc)
    s_new = s * jnp.exp(g_last) + bmm_tn(k_dec, v_new)
    return out, s_new


def fox_probs(q, k, fq, fk, qpos0):
    s = lax.dot_general(q, k, (((1,), (1,)), ((), ())), preferred_element_type=F32)
    r, c = _iota2(s.shape[0], s.shape[1])
    return _softmax_rows(jnp.where(c <= (r + qpos0), s + (fq - fk), -jnp.inf))


def mem_head(qm, wq, mk, mv):
    p = _softmax_rows(mm_nt(rms(qm, wq) * SCALE, mk))
    return mm(p, mv)


def _heads(x, n):
    return [x[:, h * HEAD_DIM:(h + 1) * HEAD_DIM] for h in range(n)]


def memkv_fn(mem, mnw, wkv, mknw):
    kv = mm(rms(mem, mnw), wkv)
    mk = jnp.concatenate([rms(kh, mknw) for kh in _heads(kv[:, :MEM_WIDTH], MEM_HEADS)], axis=1)
    return mk, kv[:, MEM_WIDTH:]


def dn_gates_fn(ab, alog, dtb):
    g = -jnp.exp(alog) * jax.nn.softplus(ab + dtb)
    low = _lower_ones(CHUNK)
    gc = jnp.concatenate([hdot(low, g[i * CHUNK:(i + 1) * CHUNK]) for i in range(ab.shape[0] // CHUNK)], axis=0)
    lane = lax.broadcasted_iota(jnp.int32, ab.shape, 1)
    return jnp.where(lane < N_HEADS, gc, jax.nn.sigmoid(ab))


def fox_fcum_fn(fp, fbias):
    lf = jax.nn.log_sigmoid(fp + fbias)
    low = _lower_ones(LANES)
    carry = jnp.zeros((1, fp.shape[1]), F32)
    outs = []
    for i in range(fp.shape[0] // LANES):
        cs = hdot(low, lf[i * LANES:(i + 1) * LANES]) + carry
        carry = _last_row(cs)
        outs.append(cs)
    return jnp.concatenate(outs, axis=0)


def fox_qk_fn(qraw, kraw, qnw, knw):
    q = jnp.concatenate([rms(x, qnw) * SCALE for x in _heads(qraw, N_HEADS)], axis=1)
    k = jnp.concatenate([rms(x, knw) for x in _heads(kraw, N_HEADS)], axis=1)
    return q, k


def _mem_out(qm, mqw, mk, mv):
    return [mem_head(a, mqw, b, c) for a, b, c in zip(_heads(qm, MEM_HEADS), _heads(mk, MEM_HEADS), _heads(mv, MEM_HEADS))]


def dn_out_fn(o, z, qm, onw, mqw, mk, mv):
    mix = [rms(a, onw) * jax.nn.silu(b) for a, b in zip(o, _heads(z, N_HEADS))]
    return jnp.concatenate(mix + _mem_out(qm, mqw, mk, mv), axis=1)


def fox_out_fn(o, gate, qm, mqw, mk, mv):
    return jnp.concatenate([o * jax.nn.sigmoid(gate)] + _mem_out(qm, mqw, mk, mv), axis=1)


_HBM = pl.BlockSpec(memory_space=pltpu.HBM)


def _place():
    return lax.axis_index("x"), lax.axis_index("y"), lax.axis_index("c")


class Rider:
    def __init__(self, ins, out_shape, scratch, start, finish):
        self.ins, self.out_shape, self.scratch, self.start, self.finish = list(ins), list(out_shape), list(scratch), start, finish
        self.results = None


def gather_rider(xs):
    n = len(xs)

    def plan(x_refs, out_refs, sems):
        send_sems, recv_sems, local_sems = sems
        x, y, c = _place()
        me, sibling = (x, y, c), (x, y, 1 - c)
        chips = [(1 - x, y), (x, 1 - y), (1 - x, 1 - y)]

        def copy(a, k, block, to, src=None):
            px, py, pc = block
            dst = out_refs[a].at[4 * px + 2 * py + pc]
            return pltpu.make_async_remote_copy(
                src_ref=dst if src is None else src, dst_ref=dst,
                send_sem=send_sems.at[a, k], recv_sem=recv_sems.at[a, k], device_id=to, device_id_type=MESH)

        mine = [pltpu.make_async_copy(x_refs[a], out_refs[a].at[4 * x + 2 * y + c], local_sems.at[a]) for a in range(n)]
        first = [copy(a, 0, me, sibling, src=x_refs[a]) for a in range(n)]
        first += [copy(a, 1 + j, me, (*chip, c), src=x_refs[a]) for j, chip in enumerate(chips) for a in range(n)]
        return copy, me, sibling, chips, mine, first

    def start(x_refs, out_refs, sems):
        _, _, _, _, mine, first = plan(x_refs, out_refs, sems)
        for cp in mine + first:
            cp.start()

    def finish(x_refs, out_refs, sems):
        copy, me, sibling, chips, mine, first = plan(x_refs, out_refs, sems)
        _, _, c = me
        passed = []
        for j, chip in enumerate(chips):
            for a in range(n):
                copy(a, 1 + j, (*chip, c), me).wait_recv()
                passed.append(copy(a, 4 + j, (*chip, c), sibling))
                passed[-1].start()
        for a in range(n):
            copy(a, 0, sibling, me).wait_recv()
        for j, chip in enumerate(chips):
            for a in range(n):
                copy(a, 4 + j, (*chip, 1 - c), me).wait_recv()
        for cp in first + passed:
            cp.wait_send()
        for cp in mine:
            cp.wait()

    return Rider(xs, [jax.ShapeDtypeStruct((N_DEV,) + a.shape, a.dtype) for a in xs],
                 [pltpu.SemaphoreType.DMA((n, 7)), pltpu.SemaphoreType.DMA((n, 7)), pltpu.SemaphoreType.DMA((n,))], start, finish)


def sibling_rider(gs):
    n = len(gs)

    def plan(g_refs, out_refs, sems):
        send_sems, recv_sems = sems
        x, y, c = _place()
        return [pltpu.make_async_remote_copy(
            src_ref=g_refs[a].at[2 * k + 1 - c], dst_ref=out_refs[a].at[k], send_sem=send_sems.at[a, k],
            recv_sem=recv_sems.at[a, k], device_id=(x, y, 1 - c), device_id_type=MESH) for a in range(n) for k in range(4)]

    def start(g_refs, out_refs, sems):
        for cp in plan(g_refs, out_refs, sems):
            cp.start()

    def finish(g_refs, out_refs, sems):
        copies = plan(g_refs, out_refs, sems)
        for cp in copies:
            cp.wait_recv()
        for cp in copies:
            cp.wait_send()

    return Rider(gs, [jax.ShapeDtypeStruct((4,) + g.shape[1:], g.dtype) for g in gs],
                 [pltpu.SemaphoreType.DMA((n, 4)), pltpu.SemaphoreType.DMA((n, 4))], start, finish)


def chips_rider(hs):
    n = len(hs)

    def plan(h_refs, out_refs, sems):
        send_sems, recv_sems, local_sems = sems
        x, y, c = _place()
        mine = 2 * x + y
        chips = [(1 - x, y), (x, 1 - y), (1 - x, 1 - y)]
        keep = [pltpu.make_async_copy(h_refs[a].at[mine], out_refs[a].at[mine], local_sems.at[a]) for a in range(n)]
        sends = [pltpu.make_async_remote_copy(
            src_ref=h_refs[a].at[2 * qx + qy], dst_ref=out_refs[a].at[mine], send_sem=send_sems.at[a, j],
            recv_sem=recv_sems.at[a, j], device_id=(qx, qy, c), device_id_type=MESH)
            for j, (qx, qy) in enumerate(chips) for a in range(n)]
        recvs = [pltpu.make_async_remote_copy(
            src_ref=h_refs[a].at[mine], dst_ref=out_refs[a].at[2 * qx + qy], send_sem=send_sems.at[a, j],
            recv_sem=recv_sems.at[a, j], device_id=(qx, qy, c), device_id_type=MESH)
            for j, (qx, qy) in enumerate(chips) for a in range(n)]
        return keep, sends, recvs

    def start(h_refs, out_refs, sems):
        keep, sends, _ = plan(h_refs, out_refs, sems)
        for cp in keep + sends:
            cp.start()

    def finish(h_refs, out_refs, sems):
        keep, sends, recvs = plan(h_refs, out_refs, sems)
        for cp in recvs:
            cp.wait_recv()
        for cp in sends:
            cp.wait_send()
        for cp in keep:
            cp.wait()

    return Rider(hs, [jax.ShapeDtypeStruct(h.shape, h.dtype) for h in hs],
                 [pltpu.SemaphoreType.DMA((n, 3)), pltpu.SemaphoreType.DMA((n, 3)), pltpu.SemaphoreType.DMA((n,))], start, finish)


def hosted_call(riders, body, *, out_shape, in_specs, out_specs, grid=(), scratch_shapes=(), **kw):
    riders = tuple(riders or ())
    if not riders:
        return pl.pallas_call(body, out_shape=out_shape, in_specs=in_specs, out_specs=out_specs, grid=grid,
                              scratch_shapes=scratch_shapes, **kw)
    single = not isinstance(out_shape, (list, tuple))
    k_out_shape = [out_shape] if single else list(out_shape)
    k_out_specs = [out_specs] if single else list(out_specs)
    n_in, n_out, n_scr = len(in_specs), len(k_out_shape), len(scratch_shapes)
    r_ins = [a for r in riders for a in r.ins]
    r_outs = [s for r in riders for s in r.out_shape]
    r_scr = [s for r in riders for s in r.scratch]

    def full_body(*refs):
        ins = refs[:n_in + len(r_ins)]
        outs = refs[n_in + len(r_ins):n_in + len(r_ins) + n_out + len(r_outs)]
        scr = refs[n_in + len(r_ins) + n_out + len(r_outs):]
        steps = math.prod(grid)
        step = 0
        for d, g in enumerate(grid):
            step = step * g + pl.program_id(d)

        def each(method):
            i0, o0, s0 = n_in, n_out, n_scr
            for r in riders:
                getattr(r, method)(ins[i0:i0 + len(r.ins)], outs[o0:o0 + len(r.out_shape)], scr[s0:s0 + len(r.scratch)])
                i0, o0, s0 = i0 + len(r.ins), o0 + len(r.out_shape), s0 + len(r.scratch)

        if steps == 1:
            each("start")
            body(*ins[:n_in], *outs[:n_out], *scr[:n_scr])
            each("finish")
        else:
            pl.when(step == 0)(lambda: each("start"))
            body(*ins[:n_in], *outs[:n_out], *scr[:n_scr])
            pl.when(step == steps - 1)(lambda: each("finish"))

    call = pl.pallas_call(
        full_body, out_shape=k_out_shape + r_outs, in_specs=list(in_specs) + [_HBM] * len(r_ins),
        out_specs=k_out_specs + [_HBM] * len(r_outs), grid=grid, scratch_shapes=list(scratch_shapes) + r_scr, **kw)

    def run(*args):
        res = call(*args, *r_ins)
        o0 = n_out
        for r in riders:
            r.results = list(res[o0:o0 + len(r.out_shape)])
            o0 += len(r.out_shape)
        return res[0] if single else list(res[:n_out])

    return run


def run_riders(riders, *, name):
    hosted_call(riders, lambda: None, name=name, out_shape=[], in_specs=[], out_specs=[])()
    return [r.results for r in riders]


def _pick(n, cands):
    for c in cands:
        if n % c == 0:
            return c
    return n


def _params(sem):
    return pltpu.CompilerParams(dimension_semantics=sem, vmem_limit_bytes=VMEM_LIMIT)


MATMUL_VMEM_BUDGET = 40 * 2 ** 20


def _matmul_tiles(m, n, k, bytes_a, bytes_b, bytes_mn, fixed):
    fm, fn, fk = fixed if fixed is not None else (None, None, None)

    def options(given, size, cands):
        return [given] if given else ([c for c in cands if size % c == 0] or [size])

    best = None
    for tm in options(fm, m, (2048, 1024, 512, 256, 128)):
        for tn in options(fn, n, (512, 256, 128)):
            for tk in options(fk, k, (2048, 1536, 1024, 512, 256, 128)):
                if 2 * (tm * tk * bytes_a + tk * tn * bytes_b + tm * tn * bytes_mn) + tm * tn * 4 > MATMUL_VMEM_BUDGET:
                    continue
                key = ((m // tm) * (n // tn) * (k // tk), -tk)
                if best is None or key < best[0]:
                    best = (key, (tm, tn, tk))
    assert best is not None, (m, n, k, fixed)
    return best[1]


def matmul(a, b, *, name, ta=False, tb=False, post=None, post_ins=(), row_ins=(), acc=False, extra_out=None,
           out_dtype=F32, tiles=None, b_view=None, out_view=None, riders=()):
    (k, m) = a.shape if ta else a.shape[::-1]
    (kb, n) = b_view[:2] if b_view is not None else (b.shape[::-1] if tb else b.shape)
    assert k == kb, (a.shape, b.shape, ta, tb)
    bytes_mn = sum(p.dtype.itemsize for p in post_ins) + jnp.dtype(out_dtype).itemsize
    bytes_mn += jnp.dtype(extra_out[1]).itemsize if extra_out else 0
    tm, tn, tk = _matmul_tiles(m, n, k, a.dtype.itemsize, b.dtype.itemsize, bytes_mn, tiles)
    assert not acc or tn == n, (name, tn, n)
    nk = k // tk
    dims = ((0,) if ta else (1,), (1,) if tb else (0,))
    n_post, n_row = len(post_ins), len(row_ins)
    n_out = 1 + bool(extra_out) + bool(acc)

    def body(*refs):
        a_ref, b_ref = refs[:2]
        post_refs = refs[2:2 + n_post + n_row]
        o_refs, acc_ref = refs[-1 - n_out:-1], refs[-1]
        first_rows, kk = pl.program_id(0) == 0, pl.program_id(2)

        @pl.when(kk == 0)
        def _():
            acc_ref[...] = jnp.zeros_like(acc_ref)

        b_tile = b_ref[...]
        acc_ref[...] += _bdot(a_ref[...], b_tile.reshape(-1, b_tile.shape[-1]), dims)

        @pl.when(kk == nk - 1)
        def _():
            r = acc_ref[...]
            rows = [p[...] for p in post_refs[n_post:]]
            if post is not None:
                r = post(r, *[p[...] for p in post_refs[:n_post]], *rows)
            if acc:
                r, s = r
                sum_ref = o_refs[-1]

                @pl.when(first_rows)
                def _():
                    sum_ref[...] = s

                @pl.when(jnp.logical_not(first_rows))
                def _():
                    sum_ref[...] += s

            o_refs[0][...] = r.astype(out_dtype)
            if extra_out:
                o_refs[1][...] = extra_out[0](r, *rows).astype(extra_out[1])

    a_spec = pl.BlockSpec((tk, tm), lambda i, j, kk: (kk, i)) if ta else pl.BlockSpec((tm, tk), lambda i, j, kk: (i, kk))
    if b_view is not None:
        b_spec = b_view[2]
    else:
        b_spec = pl.BlockSpec((tn, tk), lambda i, j, kk: (j, kk)) if tb else pl.BlockSpec((tk, tn), lambda i, j, kk: (kk, j))
    mn_spec = pl.BlockSpec((tm, tn), lambda i, j, kk: (i, j))
    row_spec = pl.BlockSpec((1, tn), lambda i, j, kk: (0, j))
    o_shape, o_spec = ((m, n), mn_spec) if out_view is None else out_view
    out_shape = [jax.ShapeDtypeStruct(o_shape, out_dtype)]
    out_specs = [o_spec]
    if extra_out:
        out_shape.append(jax.ShapeDtypeStruct((m, n), extra_out[1]))
        out_specs.append(mn_spec)
    if acc:
        out_shape.append(jax.ShapeDtypeStruct((1, n), F32))
        out_specs.append(row_spec)
    res = hosted_call(
        riders, body, name=name, grid=(m // tm, n // tn, nk),
        in_specs=[a_spec, b_spec] + [mn_spec] * n_post + [row_spec] * n_row, out_specs=out_specs, out_shape=out_shape,
        scratch_shapes=[pltpu.VMEM((tm, tn), F32)],
        compiler_params=_params(("arbitrary" if acc else "parallel", "parallel", "arbitrary")),
    )(a, b, *post_ins, *row_ins)
    return res if n_out > 1 else res[0]


def rows_call(fn, row_ins, full_ins, row_outs, acc_outs, *, tm, name, riders=()):
    row_ins = [r if isinstance(r, tuple) else (r, r.shape[-1], 0) for r in row_ins]
    t = row_ins[0][0].shape[-2]
    tm = min(tm, t)
    n_in = len(row_ins) + len(full_ins)
    n_row = len(row_outs)

    def body(*refs):
        res = fn(*[[r[h] for h in range(r.shape[0])] if (i < len(row_ins) and len(r.shape) == 3) else r[...]
                   for i, r in enumerate(refs[:n_in])])
        res = res if isinstance(res, (tuple, list)) else (res,)
        outs = refs[n_in:]
        for ref, val in zip(outs[:n_row], res[:n_row]):
            if len(ref.shape) == 3:
                for h, vh in enumerate(val):
                    ref[h] = vh.astype(ref.dtype)
            else:
                ref[...] = val.astype(ref.dtype)
        first = pl.program_id(0) == 0
        for ref, val in zip(outs[n_row:], res[n_row:]):
            @pl.when(first)
            def _(ref=ref, val=val):
                ref[...] = val

            @pl.when(jnp.logical_not(first))
            def _(ref=ref, val=val):
                ref[...] += val

    def full_spec(shape):
        return pl.BlockSpec(shape, lambda i, nd=len(shape): (0,) * nd)

    def row_spec(lead, w, cb):
        if lead is None:
            return pl.BlockSpec((tm, w), lambda i: (i, cb))
        return pl.BlockSpec((lead, tm, w), lambda i: (0, i, cb))

    def lead_cols(c):
        return c if isinstance(c, tuple) else (None, c)

    in_specs = [row_spec(a.shape[0] if a.ndim == 3 else None, w, cb) for (a, w, cb) in row_ins]
    in_specs += [full_spec(f.shape) for f in full_ins]
    out_specs = [row_spec(*lead_cols(c), 0) for c, _ in row_outs] + [full_spec(s) for s in acc_outs]
    out_shape = [jax.ShapeDtypeStruct(tuple(d for d in (lead_cols(c)[0], t, lead_cols(c)[1]) if d is not None), dt)
                 for c, dt in row_outs] + [jax.ShapeDtypeStruct(s, F32) for s in acc_outs]
    res = hosted_call(
        riders, body, name=name, grid=(t // tm,), in_specs=in_specs, out_specs=out_specs, out_shape=out_shape,
        compiler_params=_params(("arbitrary",)),
    )(*[r[0] for r in row_ins], *full_ins)
    return res


def vjp_rows(fn, n_diff_row, row_diff_full):
    def bwd(*args, n_row, n_ct):
        prim_rows = args[:n_row]
        cts = args[n_row:n_row + n_ct]
        fulls = args[n_row + n_ct:]
        _, vjp = jax.vjp(fn, *prim_rows, *fulls)
        g = vjp(cts[0] if n_ct == 1 else tuple(cts))
        out = list(g[:n_diff_row])
        out += [gf for gf, d in zip(g[n_row:], row_diff_full) if d]
        return tuple(out)
    return bwd


def _shift_down(x, s):
    if s == 0:
        return x
    t = lax.broadcasted_iota(jnp.int32, x.shape, 0)
    return jnp.where(t >= s, pltpu.roll(x, s, 0), 0.0)


def _shift_up(x, s):
    if s == 0:
        return x
    n = x.shape[0]
    t = lax.broadcasted_iota(jnp.int32, x.shape, 0)
    return jnp.where(t < n - s, pltpu.roll(x, n - s, 0), 0.0)


def _conv(x, w_ref):
    return sum(w_ref[pl.ds(j, 1), :] * _shift_down(x, CONV_WIDTH - 1 - j) for j in range(CONV_WIDTH))


_DN_POST = (lambda c: l2n(jax.nn.silu(c)) * SCALE, lambda c: l2n(jax.nn.silu(c)), jax.nn.silu)


def dn_prep_fwd(proj, conv_w, *, name, riders=()):
    t = proj.shape[0]

    def body(xq, xk, xv, wq, wk, wv, oq, ok, ov):
        for x_ref, w_ref, o_ref, post in zip((xq, xk, xv), (wq, wk, wv), (oq, ok, ov), _DN_POST):
            o_ref[...] = post(_conv(x_ref[...], w_ref))

    x_specs = [pl.BlockSpec((t, HEAD_DIM), lambda h, g=g: (0, g * N_HEADS + h)) for g in range(3)]
    w_specs = [pl.BlockSpec((CONV_WIDTH, HEAD_DIM), lambda h, g=g: (0, g * N_HEADS + h)) for g in range(3)]
    o_spec = pl.BlockSpec((None, t, HEAD_DIM), lambda h: (h, 0, 0))
    return hosted_call(
        riders, body, name=name, grid=(N_HEADS,), in_specs=x_specs + w_specs, out_specs=[o_spec] * 3,
        out_shape=[jax.ShapeDtypeStruct((N_HEADS, t, HEAD_DIM), F32)] * 3, compiler_params=_params(("parallel",)),
    )(proj, proj, proj, conv_w, conv_w, conv_w)


def dn_prep_bwd(proj, conv_w, dq, dk, dv, *, name, riders=()):
    t = proj.shape[0]

    def body(xq, xk, xv, wq, wk, wv, gq, gk, gv, dxq, dxk, dxv, dwq, dwk, dwv):
        for x_ref, w_ref, g_ref, dx_ref, dw_ref, post in zip(
                (xq, xk, xv), (wq, wk, wv), (gq, gk, gv), (dxq, dxk, dxv), (dwq, dwk, dwv), _DN_POST):
            x = x_ref[...]
            _, vjp = jax.vjp(post, _conv(x, w_ref))
            dc, = vjp(g_ref[...])
            dx = sum(w_ref[pl.ds(j, 1), :] * _shift_up(dc, CONV_WIDTH - 1 - j) for j in range(CONV_WIDTH))
            dx_ref[...] = dx.astype(dx_ref.dtype)
            for j in range(CONV_WIDTH):
                dw_ref[pl.ds(j, 1), :] = jnp.sum(dc * _shift_down(x, CONV_WIDTH - 1 - j), axis=0, keepdims=True)

    x_specs = [pl.BlockSpec((t, HEAD_DIM), lambda h, g=g: (0, g * N_HEADS + h)) for g in range(3)]
    w_specs = [pl.BlockSpec((CONV_WIDTH, HEAD_DIM), lambda h, g=g: (0, g * N_HEADS + h)) for g in range(3)]
    g_spec = pl.BlockSpec((None, t, HEAD_DIM), lambda h: (h, 0, 0))
    dx_spec = pl.BlockSpec((t, HEAD_DIM), lambda h: (0, h))
    dw_spec = pl.BlockSpec((CONV_WIDTH, HEAD_DIM), lambda h: (0, h))
    return hosted_call(
        riders, body, name=name, grid=(N_HEADS,), in_specs=x_specs + w_specs + [g_spec] * 3, out_specs=[dx_spec] * 3 + [dw_spec] * 3,
        out_shape=[jax.ShapeDtypeStruct((t, D_MODEL), BF16)] * 3 + [jax.ShapeDtypeStruct((CONV_WIDTH, D_MODEL), F32)] * 3,
        compiler_params=_params(("parallel",)),
    )(proj, proj, proj, conv_w, conv_w, conv_w, dq, dk, dv)


INTRA_CHUNKS = 4


def _lane_column(x, lane_index):
    lane = lax.broadcasted_iota(jnp.int32, x.shape, 1)
    return jnp.sum(jnp.where(lane == lane_index, x, 0.0), axis=1, keepdims=True)


def _head_columns(g, first_lane):
    return jnp.concatenate([_lane_column(g, first_lane + h)[None] for h in range(N_HEADS)], axis=0)


def _intra_of_gates(q, k, v, gates):
    nb = N_HEADS * (gates.shape[0] // CHUNK)

    def chunks(x):
        return x.reshape(nb, CHUNK, x.shape[-1])

    res = delta_intra(chunks(q), chunks(k), chunks(v), chunks(_head_columns(gates, 0)), chunks(_head_columns(gates, N_HEADS)))
    return tuple(x.reshape(N_HEADS, -1, x.shape[-1]) for x in res)


def _step_of_gates(s, q, k, gates, u, w, qk):
    return delta_step(s, q, k, _head_columns(gates, 0), u, w, qk)


def _head_major(rows, w, index):
    return pl.BlockSpec((N_HEADS, rows, w), lambda i: (0, index(i), 0))


def delta_intra_fwd(q, k, v, gates, *, name, riders=()):
    t = q.shape[1]
    rows = min(INTRA_CHUNKS, t // CHUNK) * CHUNK

    def body(q_ref, k_ref, v_ref, g_ref, u_ref, w_ref, qk_ref):
        for ref, val in zip((u_ref, w_ref, qk_ref), _intra_of_gates(q_ref[...], k_ref[...], v_ref[...], g_ref[...])):
            ref[...] = val

    x_spec, qk_spec = (_head_major(rows, w, lambda i: i) for w in (HEAD_DIM, CHUNK))
    g_spec = pl.BlockSpec((rows, LANES), lambda i: (i, 0))
    return hosted_call(
        riders, body, name=name, grid=(t // rows,), in_specs=[x_spec] * 3 + [g_spec], out_specs=[x_spec, x_spec, qk_spec],
        out_shape=[jax.ShapeDtypeStruct((N_HEADS, t, HEAD_DIM), F32)] * 2 + [jax.ShapeDtypeStruct((N_HEADS, t, CHUNK), F32)],
        compiler_params=_params(("parallel",)),
    )(q, k, v, gates)


def delta_seq_fwd(q, k, gates, u, w, qk, *, name, riders=()):
    t = q.shape[1]
    nc = t // CHUNK

    def body(q_ref, k_ref, g_ref, u_ref, w_ref, qk_ref, o_ref, s0_ref, s_ref):
        @pl.when(pl.program_id(0) == 0)
        def _():
            s_ref[...] = jnp.zeros_like(s_ref)

        s = s_ref[...]
        s0_ref[...] = s
        o, s_new = _step_of_gates(s, q_ref[...], k_ref[...], g_ref[...], u_ref[...], w_ref[...], qk_ref[...])
        o_ref[...] = o
        s_ref[...] = s_new

    x_spec, qk_spec = (_head_major(CHUNK, w, lambda c: c) for w in (HEAD_DIM, CHUNK))
    g_spec = pl.BlockSpec((CHUNK, LANES), lambda c: (c, 0))
    s_spec = pl.BlockSpec((N_HEADS, None, HEAD_DIM, HEAD_DIM), lambda c: (0, c, 0, 0))
    return hosted_call(
        riders, body, name=name, grid=(nc,), in_specs=[x_spec, x_spec, g_spec, x_spec, x_spec, qk_spec], out_specs=[x_spec, s_spec],
        out_shape=[jax.ShapeDtypeStruct((N_HEADS, t, HEAD_DIM), F32),
                   jax.ShapeDtypeStruct((N_HEADS, nc, HEAD_DIM, HEAD_DIM), F32)],
        scratch_shapes=[pltpu.VMEM((N_HEADS, HEAD_DIM, HEAD_DIM), F32)],
        compiler_params=_params(("arbitrary",)),
    )(q, k, gates, u, w, qk)


def delta_seq_bwd(q, k, gates, u, w, qk, s0, do, *, name, riders=()):
    t = q.shape[1]
    nc = t // CHUNK

    def body(q_ref, k_ref, g_ref, u_ref, w_ref, qk_ref, s0_ref, do_ref,
             dq_ref, dk_ref, dg_ref, du_ref, dw_ref, dqk_ref, ds_ref):
        @pl.when(pl.program_id(0) == 0)
        def _():
            ds_ref[...] = jnp.zeros_like(ds_ref)

        _, vjp = jax.vjp(_step_of_gates, s0_ref[...], q_ref[...], k_ref[...], g_ref[...], u_ref[...], w_ref[...], qk_ref[...])
        ds, dq, dk, dg, du, dw, dqk = vjp((do_ref[...], ds_ref[...]))
        for ref, val in zip((ds_ref, dq_ref, dk_ref, dg_ref, du_ref, dw_ref, dqk_ref), (ds, dq, dk, dg, du, dw, dqk)):
            ref[...] = val

    x_spec, qk_spec = (_head_major(CHUNK, w, lambda c: nc - 1 - c) for w in (HEAD_DIM, CHUNK))
    g_spec = pl.BlockSpec((CHUNK, LANES), lambda c: (nc - 1 - c, 0))
    s_spec = pl.BlockSpec((N_HEADS, None, HEAD_DIM, HEAD_DIM), lambda c: (0, nc - 1 - c, 0, 0))
    head_shape = [jax.ShapeDtypeStruct((N_HEADS, t, w_), F32) for w_ in (HEAD_DIM, HEAD_DIM, HEAD_DIM, HEAD_DIM, CHUNK)]
    return hosted_call(
        riders, body, name=name, grid=(nc,), in_specs=[x_spec, x_spec, g_spec, x_spec, x_spec, qk_spec, s_spec, x_spec],
        out_specs=[x_spec, x_spec, g_spec, x_spec, x_spec, qk_spec],
        out_shape=head_shape[:2] + [jax.ShapeDtypeStruct((t, LANES), F32)] + head_shape[2:],
        scratch_shapes=[pltpu.VMEM((N_HEADS, HEAD_DIM, HEAD_DIM), F32)],
        compiler_params=_params(("arbitrary",)),
    )(q, k, gates, u, w, qk, s0, do)


def delta_intra_bwd(q, k, v, gates, du, dw, dqk, dq_s, dk_s, dg_s, *, name, riders=()):
    t = q.shape[1]
    rows = min(INTRA_CHUNKS, t // CHUNK) * CHUNK

    def body(q_ref, k_ref, v_ref, g_ref, du_ref, dw_ref, dqk_ref, dqs_ref, dks_ref, dgs_ref, dq_ref, dk_ref, dv_ref, dg_ref):
        _, vjp = jax.vjp(_intra_of_gates, q_ref[...], k_ref[...], v_ref[...], g_ref[...])
        dq, dk, dv, dg = vjp((du_ref[...], dw_ref[...], dqk_ref[...]))
        dq_ref[...] = dq + dqs_ref[...]
        dk_ref[...] = dk + dks_ref[...]
        dv_ref[...] = dv
        dg_ref[...] = dg + dgs_ref[...]

    x_spec, qk_spec = (_head_major(rows, w, lambda i: i) for w in (HEAD_DIM, CHUNK))
    g_spec = pl.BlockSpec((rows, LANES), lambda i: (i, 0))
    return hosted_call(
        riders, body, name=name, grid=(t // rows,),
        in_specs=[x_spec] * 3 + [g_spec, x_spec, x_spec, qk_spec, x_spec, x_spec, g_spec],
        out_specs=[x_spec] * 3 + [g_spec],
        out_shape=[jax.ShapeDtypeStruct((N_HEADS, t, HEAD_DIM), F32)] * 3 + [jax.ShapeDtypeStruct((t, LANES), F32)],
        compiler_params=_params(("parallel",)),
    )(q, k, v, gates, du, dw, dqk, dq_s, dk_s, dg_s)


_V_BLOCK = 2 * N_HEADS
FOX_GROUPS = 16


def _fox_groups(t):
    nq = t // Q_BLOCK
    per = max(1, nq // FOX_GROUPS)
    return [(g0, per, (g0 + per) * Q_BLOCK) for g0 in range(0, nq, per)]


def fox_attn_fwd(q, k, proj, fq, fk, *, name, riders=()):
    t = q.shape[0]

    def body(q_ref, k_ref, v_ref, fq_ref, fk_ref, o_ref, kb_ref, vb_ref):
        head = pl.program_id(0)
        kb_ref[...] = k_ref[...].astype(BF16)
        vb_ref[...] = v_ref[...].astype(BF16)
        for g0, per, keys in _fox_groups(t):
            def block(j, carry, g0=g0, keys=keys):
                rows = pl.ds((g0 + j) * Q_BLOCK, Q_BLOCK)
                p = fox_probs(q_ref[rows, :].astype(BF16), kb_ref[0:keys, :], _lane_column(fq_ref[rows, :], head),
                              fk_ref[:, 0:keys], (g0 + j) * Q_BLOCK)
                o_ref[rows, :] = jnp.dot(p.astype(BF16), vb_ref[0:keys, :], preferred_element_type=F32)
                return carry
            for j in range(per):
                block(j, 0)

    x_spec = pl.BlockSpec((t, HEAD_DIM), lambda h: (0, h))
    v_spec = pl.BlockSpec((t, HEAD_DIM), lambda h: (0, _V_BLOCK + h))
    fq_spec = pl.BlockSpec((t, LANES), lambda h: (0, 0))
    fk_spec = pl.BlockSpec((None, 1, t), lambda h: (h, 0, 0))
    return hosted_call(
        riders, body, name=name, grid=(N_HEADS,), in_specs=[x_spec, x_spec, v_spec, fq_spec, fk_spec], out_specs=x_spec,
        out_shape=jax.ShapeDtypeStruct((t, D_MODEL), F32), scratch_shapes=[pltpu.VMEM((t, HEAD_DIM), BF16)] * 2,
        compiler_params=_params(("parallel",)),
    )(q, k, proj, fq, fk)


def fox_attn_bwd(q, k, proj, fq, fk, do, *, name, riders=()):
    t = q.shape[0]

    def body(q_ref, k_ref, v_ref, fq_ref, fk_ref, do_ref, dq_ref, dk_ref, dv_out_ref, dfq_ref, dfk_ref, kb_ref, vb_ref, dv_ref):
        head = pl.program_id(0)

        @pl.when(head == 0)
        def _():
            dfq_ref[...] = jnp.zeros_like(dfq_ref)

        kb_ref[...] = k_ref[...].astype(BF16)
        vb_ref[...] = v_ref[...].astype(BF16)
        dk_ref[...] = jnp.zeros_like(dk_ref)
        dv_ref[...] = jnp.zeros_like(dv_ref)
        dfk_ref[...] = jnp.zeros_like(dfk_ref)
        nt = (((1,), (1,)), ((), ()))
        tn = (((0,), (0,)), ((), ()))
        for g0, per, keys in _fox_groups(t):
            def block(j, carry, g0=g0, keys=keys):
                rows = pl.ds((g0 + j) * Q_BLOCK, Q_BLOCK)
                qb, dob = q_ref[rows, :].astype(BF16), do_ref[rows, :].astype(BF16)
                kb, vb = kb_ref[0:keys, :], vb_ref[0:keys, :]
                p = fox_probs(qb, kb, _lane_column(fq_ref[rows, :], head), fk_ref[:, 0:keys], (g0 + j) * Q_BLOCK)
                dp = lax.dot_general(dob, vb, nt, preferred_element_type=F32)
                dz = p * (dp - jnp.sum(dp * p, axis=-1, keepdims=True))
                pb, dzb = p.astype(BF16), dz.astype(BF16)
                dq_ref[rows, :] = jnp.dot(dzb, kb, preferred_element_type=F32)
                lane = lax.broadcasted_iota(jnp.int32, (Q_BLOCK, LANES), 1)
                dfq_ref[rows, :] += jnp.where(lane == head, jnp.sum(dz, axis=-1, keepdims=True), 0.0)
                dk_ref[0:keys, :] += lax.dot_general(dzb, qb, tn, preferred_element_type=F32)
                dv_ref[0:keys, :] += lax.dot_general(pb, dob, tn, preferred_element_type=F32)
                dfk_ref[:, 0:keys] -= jnp.sum(dz, axis=0, keepdims=True)
                return carry
            for j in range(per):
                block(j, 0)
        dv_out_ref[...] = dv_ref[...].astype(dv_out_ref.dtype)

    x_spec = pl.BlockSpec((t, HEAD_DIM), lambda h: (0, h))
    v_spec = pl.BlockSpec((t, HEAD_DIM), lambda h: (0, _V_BLOCK + h))
    fq_spec = pl.BlockSpec((t, LANES), lambda h: (0, 0))
    fk_spec = pl.BlockSpec((None, 1, t), lambda h: (h, 0, 0))
    return hosted_call(
        riders, body, name=name, grid=(N_HEADS,), in_specs=[x_spec, x_spec, v_spec, fq_spec, fk_spec, x_spec],
        out_specs=[x_spec, x_spec, x_spec, fq_spec, fk_spec],
        out_shape=[jax.ShapeDtypeStruct((t, D_MODEL), F32)] * 2 + [jax.ShapeDtypeStruct((t, D_MODEL), BF16)]
        + [jax.ShapeDtypeStruct((t, LANES), F32), jax.ShapeDtypeStruct((N_HEADS, 1, t), F32)],
        scratch_shapes=[pltpu.VMEM((t, HEAD_DIM), BF16)] * 2 + [pltpu.VMEM((t, HEAD_DIM), F32)],
        compiler_params=_params(("arbitrary",)),
    )(q, k, proj, fq, fk, do)


def memkv_fwd(mem, mnw, wkv, mknw, *, name):
    n = mem.shape[0]

    def body(mem_ref, mnw_ref, w_ref, mknw_ref, mk_ref, mv_ref):
        mk, mv = memkv_fn(mem_ref[...], mnw_ref[...], w_ref[...], mknw_ref[...])
        mk_ref[...] = mk
        mv_ref[...] = mv

    return pl.pallas_call(
        body, name=name, out_shape=[jax.ShapeDtypeStruct((n, MEM_WIDTH), F32)] * 2,
        compiler_params=pltpu.CompilerParams(vmem_limit_bytes=VMEM_LIMIT),
    )(mem, mnw, wkv, mknw)


def memkv_bwd(mem, mnw, wkv, mknw, dmk, dmv, *, name):
    def body(mem_ref, mnw_ref, w_ref, mknw_ref, dmk_ref, dmv_ref, dmnw_ref, dw_ref, dmknw_ref):
        f = functools.partial(memkv_fn, mem_ref[...])
        _, vjp = jax.vjp(f, mnw_ref[...], w_ref[...].astype(F32), mknw_ref[...])
        dmnw, dw, dmknw = vjp((dmk_ref[...], dmv_ref[...]))
        dmnw_ref[...] = dmnw
        dw_ref[...] = dw.astype(dw_ref.dtype)
        dmknw_ref[...] = dmknw

    return pl.pallas_call(
        body, name=name,
        out_shape=[jax.ShapeDtypeStruct(mnw.shape, F32), jax.ShapeDtypeStruct(wkv.shape, BF16), jax.ShapeDtypeStruct(mknw.shape, F32)],
        compiler_params=pltpu.CompilerParams(vmem_limit_bytes=VMEM_LIMIT),
    )(mem, mnw, wkv, mknw, dmk, dmv)


def _row(v, width=None):
    v = v.reshape(1, -1)
    if width is not None and v.shape[1] < width:
        v = jnp.pad(v, ((0, 0), (0, width - v.shape[1])))
    return v


def _norm_fwd(x, w, name, riders=()):
    return rows_call(lambda x, w: rms(x, w), [x], [w], [(D_MODEL, BF16)], [], tm=512, name=name, riders=riders)[0]


FF_PIECE = D_FF // N_DEV


def _add(r, x, *rows):
    return r + x


def _norm_rows(r, w):
    return rms(r, w)


def _norm_bwd_post(dh, x, dx_in, w):
    _, vjp = jax.vjp(rms, x, w)
    dx, dw = vjp(dh)
    return dx + dx_in, dw


def _piece(rows, cols, index):
    return pl.BlockSpec((None, rows, cols), lambda i, j, kk: (index(i, j, kk), 0, 0))


def _two_pieces(rows, cols, index):
    return pl.BlockSpec((2, rows, cols), lambda i, j, kk: (index(i, j, kk), 0, 0))


def _mlp_fwd(x, h2, w1, w2, layer, riders=(), next_norm_w=None):
    riders = list(riders) + [None, None]
    u, a1 = matmul(h2, w1, name=f"mlp1_fwd_{layer}", tiles=(None, FF_PIECE, D_MODEL),
                   extra_out=(lambda u: jnp.square(jnp.maximum(u, 0.0)), BF16),
                   b_view=(D_MODEL, D_FF, _piece(D_MODEL, FF_PIECE, lambda i, j, kk: j)), riders=riders[0])
    norm = dict(row_ins=[next_norm_w], extra_out=(_norm_rows, BF16)) if next_norm_w is not None else {}
    y = matmul(a1, w2, name=f"mlp2_fwd_{layer}", post=_add, post_ins=[x], tiles=(None, D_MODEL, 2 * FF_PIECE),
               b_view=(D_FF, D_MODEL, _two_pieces(FF_PIECE, D_MODEL, lambda i, j, kk: kk)), riders=riders[1], **norm)
    return y, (x, h2, u, a1)


def pair_sum(g, got, *, name):
    _, rows, cols = g.shape
    tile = _pick(rows, (512, 256, 128))
    c = lax.axis_index("c").astype(jnp.int32).reshape(1)

    def body(c_ref, a_ref, b_ref, o_ref):
        o_ref[...] = (a_ref[...].astype(F32) + b_ref[...].astype(F32)).astype(o_ref.dtype)

    grid_spec = pltpu.PrefetchScalarGridSpec(
        num_scalar_prefetch=1, grid=(4, rows // tile),
        in_specs=[pl.BlockSpec((None, tile, cols), lambda k, i, c_ref: (2 * k + c_ref[0], i, 0)),
                  pl.BlockSpec((None, tile, cols), lambda k, i, c_ref: (k, i, 0))],
        out_specs=pl.BlockSpec((None, tile, cols), lambda k, i, c_ref: (k, i, 0)))
    return pl.pallas_call(
        body, name=name, grid_spec=grid_spec, out_shape=jax.ShapeDtypeStruct((4, rows, cols), g.dtype),
        compiler_params=_params(("parallel", "parallel")),
    )(c, g, got)


def chip_sums(names, pieces, gots):
    return [pair_sum(a, got, name=f"grads_pair_sum_{n}") for n, a, got in zip(names, pieces, gots)]


def _mlp_bwd(dy, res, n2w, w1, w2, layer, riders=()):
    x, h2, u, a1 = res
    du = matmul(dy, w2, tb=True, name=f"mlp2_dx_{layer}", out_dtype=BF16, tiles=(None, 2 * FF_PIECE, D_MODEL),
                post=lambda r, u: r * (2.0 * jnp.maximum(u, 0.0)), post_ins=[u],
                b_view=(D_MODEL, D_FF, _two_pieces(FF_PIECE, D_MODEL, lambda i, j, kk: j)), riders=riders)
    dw2 = matmul(a1, dy, ta=True, name=f"mlp2_dw_{layer}", out_dtype=BF16, tiles=(FF_PIECE, D_MODEL, None), out_view=(
        w2.shape, _piece(FF_PIECE, D_MODEL, lambda i, j, kk: i)))
    sib2 = sibling_rider([dw2])
    dx, dn2w = matmul(du, w1, tb=True, name=f"mlp1_dx_{layer}", tiles=(None, D_MODEL, FF_PIECE),
                      b_view=(D_FF, D_MODEL, _piece(D_MODEL, FF_PIECE, lambda i, j, kk: kk)),
                      post=_norm_bwd_post, post_ins=[x, dy], row_ins=[n2w], acc=True, riders=[sib2])
    dw1 = matmul(h2, du, ta=True, name=f"mlp1_dw_{layer}", out_dtype=BF16, tiles=(D_MODEL, FF_PIECE, None), out_view=(
        w1.shape, _piece(D_MODEL, FF_PIECE, lambda i, j, kk: j)))
    return dx, dw1, dw2, dn2w, sibling_rider([dw1]), sib2


def _in_proj_bwd(h, dmain, dsmall, w_main, w_small, x, dx_in, n1w, tag):
    dh = matmul(dmain, w_main, tb=True, name=f"inproj_dx_main_{tag}")

    def post(r, dh_main, x, dx_in, w):
        return _norm_bwd_post(r + dh_main, x, dx_in, w)

    dx, dn1w = matmul(dsmall, w_small, tb=True, name=f"inproj_dx_small_{tag}", tiles=(None, D_MODEL, None),
                      post=post, post_ins=[dh, x, dx_in], row_ins=[n1w], acc=True)
    dw_main = matmul(h, dmain, ta=True, out_dtype=BF16, name=f"inproj_dw_main_{tag}")
    dw_small = matmul(h, dsmall, ta=True, out_dtype=BF16, name=f"inproj_dw_small_{tag}")
    return dx, dn1w, dw_main, dw_small


def local_step(x, mem, target, w, m, v):
    t = x.shape[0]
    n_mem = mem.shape[0]
    g = {}

    def wire(a):
        return a.astype(BF16)

    (dn_g,), = run_riders([gather_rider([wire(w["dn_w_in"][0])])], name="weights_gather_first")
    dn_main, dn_ab = in_proj_weights(dn_g, DN_IN, 2 * N_HEADS)
    fox_w = wire(w["fox_w_in"][0])
    ride_out = gather_rider([wire(w["w_out"][0]), w["dn_conv_w"][0]])
    ride_out_1 = gather_rider([wire(w["w_out"][1])])
    ride_kv = gather_rider([wire(w["w_mem_kv"])])
    ride_mlp1_0 = gather_rider([wire(w["w_mlp1"][0])])
    ride_mlp2_0 = gather_rider([wire(w["w_mlp2"][0])])
    ride_fox_a, ride_fox_b = gather_rider([fox_w[:D_MODEL // 2]]), gather_rider([fox_w[D_MODEL // 2:]])
    ride_mlp_1 = gather_rider([wire(w["w_mlp1"][1]), wire(w["w_mlp2"][1])])
    mnw, mknw = _row(w["mem_norm_w"]), _row(w["mem_k_norm_w"])

    n1w0, n2w0 = _row(w["norm1_w"][0]), _row(w["norm2_w"][0])
    n1w1, n2w1 = _row(w["norm1_w"][1]), _row(w["norm2_w"][1])
    alog, dtb = _row(w["dn_a_log"][0], LANES), _row(w["dn_dt_bias"][0], LANES)
    onw, mqw0 = _row(w["dn_o_norm_w"][0]), _row(w["memq_norm_w"][0])
    x0 = x
    h0 = _norm_fwd(x0, n1w0, "norm1_fwd_0")
    pm0 = matmul(h0, dn_main, name="inproj_main_0", riders=[ride_out])
    w_out0 = ride_out.results[0].reshape(OUT_IN, D_MODEL)
    conv_w = ride_out.results[1].transpose(1, 0, 2).reshape(CONV_WIDTH, 3 * D_MODEL)
    ps0 = matmul(h0, dn_ab, name="inproj_small_0")
    gates = rows_call(dn_gates_fn, [ps0], [alog, dtb], [(LANES, F32)], [], tm=512, name="dn_gates_fwd")[0]
    q0, k0, v0 = dn_prep_fwd(pm0, conv_w, name="dn_prep_fwd", riders=[ride_kv])
    w_kv = ride_kv.results[0].reshape(D_MODEL, D_MODEL)
    mk, mv = memkv_fwd(mem, mnw, w_kv, mknw, name="memkv_fwd")
    u0, w0, qk0 = delta_intra_fwd(q0, k0, v0, gates, name="delta_intra_fwd", riders=[ride_mlp1_0])
    o0, s_start = delta_seq_fwd(q0, k0, gates, u0, w0, qk0, name="delta_seq_fwd", riders=[ride_mlp2_0])
    cat0 = rows_call(dn_out_fn, [o0, (pm0, D_MODEL, 3), (pm0, MEM_WIDTH, 8)], [onw, mqw0, mk, mv],
                     [(D_MODEL + MEM_WIDTH, BF16)], [], tm=256, name="dn_out_fwd")[0]
    (w1_0,), (w2_0,) = ride_mlp1_0.results, ride_mlp2_0.results
    x1, h2_0 = matmul(cat0, w_out0, post=_add, post_ins=[x0], row_ins=[n2w0], extra_out=(_norm_rows, BF16),
                      tiles=(None, D_MODEL, None), name="wout_fwd_0")
    (x2, h1), mlp_res0 = _mlp_fwd(x1, h2_0, w1_0, w2_0, 0, riders=[[ride_fox_a], [ride_fox_b]], next_norm_w=n1w1)
    fox_main, fox_f = in_proj_weights(
        jnp.concatenate([ride_fox_a.results[0], ride_fox_b.results[0]], axis=1), FOX_IN, N_HEADS)

    fbias = _row(w["fox_f_bias"][0], LANES)
    qnw, knw, mqw1 = _row(w["fox_q_norm_w"][0]), _row(w["fox_k_norm_w"][0]), _row(w["memq_norm_w"][1])
    pm1 = matmul(h1, fox_main, name="inproj_main_1", riders=[ride_out_1])
    w_out1 = ride_out_1.results[0].reshape(OUT_IN, D_MODEL)
    ps1 = matmul(h1, fox_f, name="inproj_small_1")
    fq = rows_call(fox_fcum_fn, [ps1], [fbias], [(LANES, F32)], [], tm=t, name="fox_fcum_fwd")[0]
    fk = fq[:, :N_HEADS].T[:, None, :]
    q1, k1 = rows_call(fox_qk_fn, [(pm1, D_MODEL, 0), (pm1, D_MODEL, 1)], [qnw, knw], [(D_MODEL, F32)] * 2, [], tm=256,
                       name="fox_qk_fwd")
    o1 = fox_attn_fwd(q1, k1, pm1, fq, fk, name="fox_attn_fwd", riders=[ride_mlp_1])
    cat1 = rows_call(fox_out_fn, [o1, (pm1, D_MODEL, 3), (pm1, MEM_WIDTH, 8)], [mqw1, mk, mv],
                     [(D_MODEL + MEM_WIDTH, BF16)], [], tm=256, name="fox_out_fwd")[0]
    w1_1, w2_1 = ride_mlp_1.results
    x3, h2_1 = matmul(cat1, w_out1, post=_add, post_ins=[x2], row_ins=[n2w1], extra_out=(_norm_rows, BF16),
                      tiles=(None, D_MODEL, None), name="wout_fwd_1")
    y, mlp_res1 = _mlp_fwd(x3, h2_1, w1_1, w2_1, 1)

    def loss_fn(y, tgt):
        e = y - tgt
        return e * (1.0 / D_MODEL), jnp.sum(jnp.sum(e * e, axis=1, keepdims=True), axis=0, keepdims=True)
    dy, sq = rows_call(loss_fn, [y, target], [], [(D_MODEL, F32)], [(1, 1)], tm=512, name="loss")
    loss = sq[0, 0] * (0.5 / D_MODEL)

    dx3, dw1_1, dw2_1, dn2w1, sib1, sib2 = _mlp_bwd(dy, mlp_res1, n2w1, w1_1, w2_1, 1)
    dcat1 = matmul(dx3, w_out1, tb=True, name="wout_dx_1", riders=[sib1])
    dwo_1 = matmul(cat1, dx3, ta=True, out_dtype=BF16, name="wout_dw_1").reshape(N_DEV, OUT_IN // N_DEV, D_MODEL)
    sibo = sibling_rider([dwo_1])
    do1, dgate1, dqm1, dmqw1, dmk1, dmv1 = rows_call(
        functools.partial(vjp_rows(fox_out_fn, 3, (True, True, True)), n_row=3, n_ct=1),
        [o1, (pm1, D_MODEL, 3), (pm1, MEM_WIDTH, 8), dcat1], [mqw1, mk, mv],
        [(D_MODEL, F32), (D_MODEL, BF16), (MEM_WIDTH, BF16)], [(1, HEAD_DIM), (n_mem, MEM_WIDTH), (n_mem, MEM_WIDTH)],
        tm=256, name="fox_out_bwd", riders=[sibo])
    ride_l1 = chips_rider(chip_sums(["w_mlp2_1", "w_mlp1_1", "w_out_1"], [dw2_1, dw1_1, dwo_1],
                                    sib2.results + sib1.results + sibo.results))
    dq1, dk1, dv1, dfq, dfk = fox_attn_bwd(q1, k1, pm1, fq, fk, do1, name="fox_attn_bwd", riders=[ride_l1])
    dqraw1, dkraw1, dqnw, dknw = rows_call(
        functools.partial(vjp_rows(fox_qk_fn, 2, (True, True)), n_row=2, n_ct=2),
        [(pm1, D_MODEL, 0), (pm1, D_MODEL, 1), dq1, dk1], [qnw, knw],
        [(D_MODEL, BF16)] * 2, [(1, HEAD_DIM)] * 2, tm=256, name="fox_qk_bwd")
    dfcum = dfq + jnp.pad(dfk[:, 0, :].T, ((0, 0), (0, LANES - N_HEADS)))
    dps1, dfbias = rows_call(
        functools.partial(vjp_rows(fox_fcum_fn, 1, (True,)), n_row=1, n_ct=1),
        [ps1, dfcum], [fbias], [(LANES, F32)], [(1, LANES)], tm=t, name="fox_fcum_bwd")
    dpm1 = jnp.concatenate([dqraw1, dkraw1, dv1, dgate1, dqm1], axis=1)
    dx2, dn1w1, dwmain1, dwsmall1 = _in_proj_bwd(h1, dpm1, dps1, fox_main, fox_f, x2, dx3, n1w1, "1")
    g_fox = in_proj_pieces(dwmain1, dwsmall1, N_HEADS, FOX_IN)
    sibf = sibling_rider([g_fox])

    dx1, dw1_0, dw2_0, dn2w0, sib1, sib2 = _mlp_bwd(dx2, mlp_res0, n2w0, w1_0, w2_0, 0, riders=[sibf])
    ride_fox_g = chips_rider(chip_sums(["fox_w_in"], [g_fox], sibf.results))
    dcat0 = matmul(dx1, w_out0, tb=True, name="wout_dx_0", riders=[sib1])
    dwo_0 = matmul(cat0, dx1, ta=True, out_dtype=BF16, name="wout_dw_0").reshape(N_DEV, OUT_IN // N_DEV, D_MODEL)
    sibo = sibling_rider([dwo_0])
    do0, dz0, dqm0, donw, dmqw0, dmk0, dmv0 = rows_call(
        functools.partial(vjp_rows(dn_out_fn, 3, (True, True, True, True)), n_row=3, n_ct=1),
        [o0, (pm0, D_MODEL, 3), (pm0, MEM_WIDTH, 8), dcat0], [onw, mqw0, mk, mv],
        [((N_HEADS, HEAD_DIM), F32), (D_MODEL, BF16), (MEM_WIDTH, BF16)],
        [(1, HEAD_DIM), (1, HEAD_DIM), (n_mem, MEM_WIDTH), (n_mem, MEM_WIDTH)], tm=256, name="dn_out_bwd", riders=[sibo])
    h_l0 = chip_sums(["w_mlp2_0", "w_mlp1_0", "w_out_0"], [dw2_0, dw1_0, dwo_0], sib2.results + sib1.results + sibo.results)
    ride_l0_mlp2, ride_l0_rest = chips_rider(h_l0[:1]), chips_rider(h_l0[1:])
    dmnw, dwkv, dmknw = memkv_bwd(mem, mnw, w_kv, mknw, dmk0 + dmk1, dmv0 + dmv1, name="memkv_bwd")
    g_kv = dwkv.reshape(N_DEV, D_MODEL // N_DEV, D_MODEL)
    sibk = sibling_rider([g_kv])
    dq_s, dk_s, dg_s, du0, dw0, dqk0 = delta_seq_bwd(q0, k0, gates, u0, w0, qk0, s_start, do0, name="delta_seq_bwd",
                                                     riders=[ride_fox_g, sibk])
    ride_kv_g = chips_rider(chip_sums(["w_mem_kv"], [g_kv], sibk.results))
    dq0, dk0, dv0, dgates = delta_intra_bwd(q0, k0, v0, gates, du0, dw0, dqk0, dq_s, dk_s, dg_s,
                                            name="delta_intra_bwd", riders=[ride_l0_mlp2, ride_kv_g])
    dxq, dxk, dxv, dcq, dck, dcv = dn_prep_bwd(pm0, conv_w, dq0, dk0, dv0, name="dn_prep_bwd", riders=[ride_l0_rest])
    dconv = jnp.concatenate([dcq, dck, dcv], axis=1)
    dps0, dalog, ddtb = rows_call(
        functools.partial(vjp_rows(dn_gates_fn, 1, (True, True)), n_row=1, n_ct=1),
        [ps0, dgates], [alog, dtb], [(LANES, F32)], [(1, LANES)] * 2, tm=512, name="dn_gates_bwd")
    dpm0 = jnp.concatenate([dxq, dxk, dxv, dz0, dqm0], axis=1)
    grad_x, dn1w0, dwmain0, dwsmall0 = _in_proj_bwd(h0, dpm0, dps0, dn_main, dn_ab, x0, dx1, n1w0, "0")
    g_dn = in_proj_pieces(dwmain0, dwsmall0, 2 * N_HEADS, DN_IN)
    g_conv = dconv.reshape(CONV_WIDTH, N_DEV, -1).transpose(1, 0, 2).astype(BF16)

    g["mem_norm_w"] = dmnw[0]
    g["mem_k_norm_w"] = dmknw[0]
    g["norm1_w"] = jnp.concatenate([dn1w0, dn1w1], axis=0)
    g["dn_a_log"] = dalog[:, :N_HEADS]
    g["dn_dt_bias"] = ddtb[:, :N_HEADS]
    g["dn_o_norm_w"] = donw
    g["fox_f_bias"] = dfbias[:, :N_HEADS]
    g["fox_q_norm_w"] = dqnw
    g["fox_k_norm_w"] = dknw
    g["memq_norm_w"] = jnp.concatenate([dmqw0, dmqw1], axis=0)
    g["norm2_w"] = jnp.concatenate([dn2w0, dn2w1], axis=0)

    sibd = sibling_rider([g_dn, g_conv])
    run_riders([sibd], name="grads_to_sibling_last")
    ride_last = chips_rider(chip_sums(["dn_w_in", "dn_conv_w"], [g_dn, g_conv], sibd.results))
    ride_small = gather_rider([pack_small(g, last=loss)])
    run_riders([ride_last, ride_small], name="grads_to_chips_last")

    def layers(l0, l1):
        return jnp.stack([l0, l1], axis=1).reshape(4, -1, l0.shape[-1])

    parts = {
        "w_mlp1": layers(ride_l0_rest.results[0], ride_l1.results[1]),
        "w_mlp2": layers(ride_l0_mlp2.results[0], ride_l1.results[0]),
        "w_out": layers(ride_l0_rest.results[1], ride_l1.results[2]),
        "fox_w_in": ride_fox_g.results[0], "w_mem_kv": ride_kv_g.results[0],
        "dn_w_in": ride_last.results[0], "dn_conv_w": ride_last.results[1],
    }
    out = {n: adamw(parts[n], w[n], m[n], v[n], name=f"adamw_{n}") for n, _, _ in BIG}
    small = adamw(ride_small.results[0], pack_small(w), pack_small(m), pack_small(v), name="adamw_small")
    loss = small[0][-1, -1]
    return loss, grad_x, out, small


WEIGHTS = ["mem_norm_w", "w_mem_kv", "mem_k_norm_w", "norm1_w", "dn_w_in", "dn_conv_w", "dn_a_log", "dn_dt_bias",
           "dn_o_norm_w", "fox_w_in", "fox_f_bias", "fox_q_norm_w", "fox_k_norm_w", "memq_norm_w", "w_out", "norm2_w",
           "w_mlp1", "w_mlp2"]
DN_IN = 4 * D_MODEL + 2 * N_HEADS + MEM_WIDTH
FOX_IN = 4 * D_MODEL + N_HEADS + MEM_WIDTH
GATE_END = 4 * D_MODEL
OUT_IN = D_MODEL + MEM_WIDTH
BIG = [("w_mem_kv", D_MODEL // N_DEV, D_MODEL), ("dn_w_in", D_MODEL, DN_IN // N_DEV), ("fox_w_in", D_MODEL, FOX_IN // N_DEV),
       ("dn_conv_w", CONV_WIDTH, 3 * D_MODEL // N_DEV), ("w_out", 2 * OUT_IN // N_DEV, D_MODEL),
       ("w_mlp1", 2 * D_MODEL, FF_PIECE), ("w_mlp2", 2 * FF_PIECE, D_MODEL)]
SMALL_TILE = 8 * LANES
SMALL = [(name, shape, -(-math.prod(shape) // SMALL_TILE) * SMALL_TILE) for name, shape in [
    ("mem_norm_w", (D_MODEL,)), ("mem_k_norm_w", (HEAD_DIM,)), ("norm1_w", (2, D_MODEL)), ("dn_a_log", (1, N_HEADS)),
    ("dn_dt_bias", (1, N_HEADS)), ("dn_o_norm_w", (1, HEAD_DIM)), ("fox_f_bias", (1, N_HEADS)),
    ("fox_q_norm_w", (1, HEAD_DIM)), ("fox_k_norm_w", (1, HEAD_DIM)), ("memq_norm_w", (2, HEAD_DIM)), ("norm2_w", (2, D_MODEL))]]
SMALL_ROWS = sum(ln for _, _, ln in SMALL) // LANES + 8


def pack_small(p, last=None):
    def rows(a, ln):
        a = a.reshape(-1)
        return (a if a.shape[0] == ln else jnp.pad(a, (0, ln - a.shape[0]))).reshape(-1, LANES)

    used = sum(ln for _, _, ln in SMALL) // LANES
    tail = jnp.zeros(((SMALL_ROWS - used) * LANES,), F32)
    if last is not None:
        tail = jnp.concatenate([tail[:-1], last.reshape(1)])
    return jnp.concatenate([rows(p[n], ln) for n, _, ln in SMALL] + [tail.reshape(-1, LANES)], axis=0)


def unpack_small(pk):
    row, out = 0, {}
    for n, sh, ln in SMALL:
        out[n] = pk[row:row + ln // LANES].reshape(-1)[:math.prod(sh)].reshape(sh)
        row += ln // LANES
    return out


def in_proj_weights(gathered, width, n_small):
    full = gathered.transpose(1, 0, 2).reshape(D_MODEL, width)
    main = jnp.concatenate([full[:, :GATE_END], full[:, GATE_END + n_small:]], axis=1)
    return main, jnp.pad(full[:, GATE_END:GATE_END + n_small], ((0, 0), (0, LANES - n_small)))


def in_proj_pieces(d_main, d_small, n_small, width):
    full = jnp.concatenate([d_main[:, :GATE_END], d_small[:, :n_small], d_main[:, GATE_END:]], axis=1)
    return full.reshape(D_MODEL, N_DEV, width // N_DEV).transpose(1, 0, 2)


def adamw(parts, w, m, v, *, name):
    n, _, cols = parts.shape
    layers = w.shape[0] if w.ndim == 3 else 1
    rows = w.shape[-2]
    tile = _pick(rows, (512, 256, 128))
    steps = rows // tile

    def body(p_ref, w_ref, m_ref, v_ref, g_ref, d_ref, mo_ref, vo_ref):
        g = p_ref[0].astype(F32)
        for i in range(1, n):
            g = g + p_ref[i].astype(F32)
        m_new = ADAM_B1 * m_ref[...] + (1.0 - ADAM_B1) * g
        v_new = ADAM_B2 * v_ref[...] + (1.0 - ADAM_B2) * jnp.square(g)
        m_hat = m_new / (1.0 - ADAM_B1 ** ADAM_STEP)
        v_hat = v_new / (1.0 - ADAM_B2 ** ADAM_STEP)
        g_ref[...] = g
        d_ref[...] = -ADAM_LR * (m_hat / (jnp.sqrt(v_hat) + ADAM_EPS) + ADAM_WD * w_ref[...])
        mo_ref[...] = m_new
        vo_ref[...] = v_new

    if w.ndim == 3:
        spec = pl.BlockSpec((None, tile, cols), lambda l, i: (l, i, 0))
    else:
        spec = pl.BlockSpec((tile, cols), lambda l, i: (i, 0))
    return pl.pallas_call(
        body, name=name, grid=(layers, steps),
        in_specs=[pl.BlockSpec((n, tile, cols), lambda l, i: (0, l * steps + i, 0)), spec, spec, spec], out_specs=[spec] * 4,
        out_shape=[jax.ShapeDtypeStruct(w.shape, F32)] * 4, compiler_params=_params(("parallel", "parallel")),
    )(parts, w, m, v)


def kernel(x, mem, mem_norm_w, w_mem_kv, mem_k_norm_w, norm1_w, dn_w_in, dn_conv_w, dn_a_log, dn_dt_bias, dn_o_norm_w, fox_w_in, fox_f_bias, fox_q_norm_w, fox_k_norm_w, memq_norm_w, w_out, norm2_w, w_mlp1, w_mlp2, loss_target, m_mem_norm_w, m_w_mem_kv, m_mem_k_norm_w, m_norm1_w, m_dn_w_in, m_dn_conv_w, m_dn_a_log, m_dn_dt_bias, m_dn_o_norm_w, m_fox_w_in, m_fox_f_bias, m_fox_q_norm_w, m_fox_k_norm_w, m_memq_norm_w, m_w_out, m_norm2_w, m_w_mlp1, m_w_mlp2, v_mem_norm_w, v_w_mem_kv, v_mem_k_norm_w, v_norm1_w, v_dn_w_in, v_dn_conv_w, v_dn_a_log, v_dn_dt_bias, v_dn_o_norm_w, v_fox_w_in, v_fox_f_bias, v_fox_q_norm_w, v_fox_k_norm_w, v_memq_norm_w, v_w_out, v_norm2_w, v_w_mlp1, v_w_mlp2):
    p = dict(mem_norm_w=mem_norm_w, w_mem_kv=w_mem_kv, mem_k_norm_w=mem_k_norm_w, norm1_w=norm1_w, dn_w_in=dn_w_in,
             dn_conv_w=dn_conv_w, dn_a_log=dn_a_log, dn_dt_bias=dn_dt_bias, dn_o_norm_w=dn_o_norm_w, fox_w_in=fox_w_in,
             fox_f_bias=fox_f_bias, fox_q_norm_w=fox_q_norm_w, fox_k_norm_w=fox_k_norm_w, memq_norm_w=memq_norm_w,
             w_out=w_out, norm2_w=norm2_w, w_mlp1=w_mlp1, w_mlp2=w_mlp2)
    pm = dict(mem_norm_w=m_mem_norm_w, w_mem_kv=m_w_mem_kv, mem_k_norm_w=m_mem_k_norm_w, norm1_w=m_norm1_w,
              dn_w_in=m_dn_w_in, dn_conv_w=m_dn_conv_w, dn_a_log=m_dn_a_log, dn_dt_bias=m_dn_dt_bias,
              dn_o_norm_w=m_dn_o_norm_w, fox_w_in=m_fox_w_in, fox_f_bias=m_fox_f_bias, fox_q_norm_w=m_fox_q_norm_w,
              fox_k_norm_w=m_fox_k_norm_w, memq_norm_w=m_memq_norm_w, w_out=m_w_out, norm2_w=m_norm2_w, w_mlp1=m_w_mlp1,
              w_mlp2=m_w_mlp2)
    pv = dict(mem_norm_w=v_mem_norm_w, w_mem_kv=v_w_mem_kv, mem_k_norm_w=v_mem_k_norm_w, norm1_w=v_norm1_w,
              dn_w_in=v_dn_w_in, dn_conv_w=v_dn_conv_w, dn_a_log=v_dn_a_log, dn_dt_bias=v_dn_dt_bias,
              dn_o_norm_w=v_dn_o_norm_w, fox_w_in=v_fox_w_in, fox_f_bias=v_fox_f_bias, fox_q_norm_w=v_fox_q_norm_w,
              fox_k_norm_w=v_fox_k_norm_w, memq_norm_w=v_memq_norm_w, w_out=v_w_out, norm2_w=v_norm2_w, w_mlp1=v_w_mlp1,
              w_mlp2=v_w_mlp2)

    loss, grad_x, results, small = local_step(x[0], mem[0], loss_target[0], p, pm, pv)
    small = [unpack_small(o) for o in small]
    groups = [{**small[i], **{n: r[i] for n, r in results.items()}} for i in range(4)]
    return (loss, grad_x[None], *[grp[n] for grp in groups for n in WEIGHTS])
```

```python
import functools
import math

import jax
import jax.numpy as jnp
from jax import lax
from jax.experimental import pallas as pl
from jax.experimental.pallas import tpu as pltpu

F32 = jnp.float32
BF16 = jnp.bfloat16
HIGHEST = lax.Precision.HIGHEST

D_MODEL = 1024
HEAD_DIM = 128
N_HEADS = 8
MEM_HEADS = 4
MEM_WIDTH = MEM_HEADS * HEAD_DIM
D_FF = 4 * D_MODEL
CONV_WIDTH = 4
CHUNK = 64
Q_BLOCK = 128
EPS = 1e-6
SCALE = HEAD_DIM ** -0.5
MAIN_WIDTH = 4 * D_MODEL + MEM_WIDTH
LANES = 128
N_DEV = 8

ADAM_LR = 0.001
ADAM_B1 = 0.9
ADAM_B2 = 0.999
ADAM_EPS = 1e-08
ADAM_WD = 0.01
ADAM_STEP = 10

VMEM_LIMIT = 56 * 2 ** 20
MESH = pl.DeviceIdType.MESH


def _bdot(a, b, dims):
    return lax.dot_general(a.astype(BF16), b.astype(BF16), (dims, ((), ())), preferred_element_type=F32)


@jax.custom_vjp
def mm(a, b):
    return _bdot(a, b, ((1,), (0,)))


@jax.custom_vjp
def mm_nt(a, b):
    return _bdot(a, b, ((1,), (1,)))


@jax.custom_vjp
def mm_tn(a, b):
    return _bdot(a, b, ((0,), (0,)))


mm.defvjp(lambda a, b: (mm(a, b), (a, b)), lambda r, g: (mm_nt(g, r[1]), mm_tn(r[0], g)))
mm_nt.defvjp(lambda a, b: (mm_nt(a, b), (a, b)), lambda r, g: (mm(g, r[1]), mm_tn(g, r[0])))
mm_tn.defvjp(lambda a, b: (mm_tn(a, b), (a, b)), lambda r, g: (mm_nt(r[1], g), mm(r[0], g)))


def hdot(a, b):
    return jnp.dot(a, b, precision=HIGHEST, preferred_element_type=F32)


def rms(x, w):
    return x * lax.rsqrt(jnp.mean(x * x, axis=-1, keepdims=True) + EPS) * w


def l2n(x):
    return x * lax.rsqrt(jnp.sum(x * x, axis=-1, keepdims=True) + EPS)


def _iota2(n, m):
    return lax.broadcasted_iota(jnp.int32, (n, m), 0), lax.broadcasted_iota(jnp.int32, (n, m), 1)


def _lower_ones(n):
    r, c = _iota2(n, n)
    return jnp.where(r >= c, 1.0, 0.0).astype(F32)


def _last_row(x):
    r = lax.broadcasted_iota(jnp.int32, x.shape, 0)
    return jnp.sum(jnp.where(r == x.shape[0] - 1, x, 0.0), axis=0, keepdims=True)


def _softmax_rows(z):
    m = lax.stop_gradient(jnp.max(z, axis=-1, keepdims=True))
    e = jnp.exp(z - m)
    return e * (1.0 / jnp.sum(e, axis=-1, keepdims=True))


_BNN = (((2,), (1,)), ((0,), (0,)))
_BNT = (((2,), (2,)), ((0,), (0,)))
_BTN = (((1,), (1,)), ((0,), (0,)))


def _bbdot(a, b, dims):
    return lax.dot_general(a.astype(BF16), b.astype(BF16), dims, preferred_element_type=F32)


@jax.custom_vjp
def bmm(a, b):
    return _bbdot(a, b, _BNN)


@jax.custom_vjp
def bmm_nt(a, b):
    return _bbdot(a, b, _BNT)


@jax.custom_vjp
def bmm_tn(a, b):
    return _bbdot(a, b, _BTN)


@jax.custom_vjp
def bmm_high(a, b):
    return lax.dot_general(a, b, _BNN, precision=lax.Precision.HIGH, preferred_element_type=F32)


bmm.defvjp(lambda a, b: (bmm(a, b), (a, b)), lambda r, g: (bmm_nt(g, r[1]), bmm_tn(r[0], g)))
bmm_nt.defvjp(lambda a, b: (bmm_nt(a, b), (a, b)), lambda r, g: (bmm(g, r[1]), bmm_tn(g, r[0])))
bmm_tn.defvjp(lambda a, b: (bmm_tn(a, b), (a, b)), lambda r, g: (bmm_nt(r[1], g), bmm(r[0], g)))
bmm_high.defvjp(lambda a, b: (bmm_high(a, b), (a, b)), lambda r, g: (bmm_nt(g, r[1]), bmm_tn(r[0], g)))

NEUMANN_HIGH_LEVELS = 2


@jax.custom_vjp
def inv_unit_lower(a):
    n = a.shape[-1]
    r, c = _iota2(n, n)
    p = jnp.where(r == c, 1.0, 0.0).astype(F32) - a
    ak = a
    for level in range(int(math.log2(n)) - 1):
        dot = bmm_high if level < NEUMANN_HIGH_LEVELS else bmm
        ak = dot(ak, ak)
        p = p + dot(p, ak)
    return p


def _inv_unit_lower_fwd(a):
    t = inv_unit_lower(a)
    return t, t


def _inv_unit_lower_bwd(t, g):
    return (-bmm_tn(t, bmm_nt(g, t)),)


inv_unit_lower.defvjp(_inv_unit_lower_fwd, _inv_unit_lower_bwd)


def delta_intra(q, k, v, gc, beta):
    b, c, _ = q.shape
    r, cc = _iota2(c, c)
    causal = r >= cc
    strict = r > cc
    gi = jnp.broadcast_to(gc, (b, c, c))
    gj = jnp.swapaxes(gi, 1, 2)
    decay = jnp.where(causal, jnp.exp(jnp.where(causal, gi - gj, 0.0)), 0.0)
    kb = k * beta
    a = jnp.where(strict, bmm_nt(kb, k) * decay, 0.0)
    t = inv_unit_lower(a)
    u = bmm(t, v * beta)
    w = bmm(t, kb * jnp.exp(gc))
    qk = jnp.where(causal, bmm_nt(q, k) * decay, 0.0)
    return u, w, qk


def delta_step(s, q, k, gc, u, w, qk):
    v_new = u - bmm(w, s)
    out = bmm(q * jnp.exp(gc), s) + bmm(qk, v_new)
    r = lax.broadcasted_iota(jnp.int32, gc.shape, 1)
    g_last = jnp.sum(jnp.where(r == gc.shape[1] - 1, gc, 0.0), axis=1, keepdims=True)
    k_dec = k * jnp.exp(g_last - gc)
    s_new = s * jnp.exp(g_last) + bmm_tn(k_dec, v_new)
    return out, s_new


def fox_probs(q, k, fq, fk, qpos0):
    s = lax.dot_general(q, k, (((1,), (1,)), ((), ())), preferred_element_type=F32)
    r, c = _iota2(s.shape[0], s.shape[1])
    return _softmax_rows(jnp.where(c <= (r + qpos0), s + (fq - fk), -jnp.inf))


def mem_head(qm, wq, mk, mv):
    p = _softmax_rows(mm_nt(rms(qm, wq) * SCALE, mk))
    return mm(p, mv)


def _heads(x, n):
    return [x[:, h * HEAD_DIM:(h + 1) * HEAD_DIM] for h in range(n)]


def memkv_fn(mem, mnw, wkv, mknw):
    kv = mm(rms(mem, mnw), wkv)
    mk = jnp.concatenate([rms(kh, mknw) for kh in _heads(kv[:, :MEM_WIDTH], MEM_HEADS)], axis=1)
    return mk, kv[:, MEM_WIDTH:]


def dn_gates_fn(ab, alog, dtb):
    g = -jnp.exp(alog) * jax.nn.softplus(ab + dtb)
    low = _lower_ones(CHUNK)
    gc = jnp.concatenate([hdot(low, g[i * CHUNK:(i + 1) * CHUNK]) for i in range(ab.shape[0] // CHUNK)], axis=0)
    lane = lax.broadcasted_iota(jnp.int32, ab.shape, 1)
    return jnp.where(lane < N_HEADS, gc, jax.nn.sigmoid(ab))


def fox_fcum_fn(fp, fbias):
    lf = jax.nn.log_sigmoid(fp + fbias)
    low = _lower_ones(LANES)
    carry = jnp.zeros((1, fp.shape[1]), F32)
    outs = []
    for i in range(fp.shape[0] // LANES):
        cs = hdot(low, lf[i * LANES:(i + 1) * LANES]) + carry
        carry = _last_row(cs)
        outs.append(cs)
    return jnp.concatenate(outs, axis=0)


def fox_qk_fn(qraw, kraw, qnw, knw):
    q = jnp.concatenate([rms(x, qnw) * SCALE for x in _heads(qraw, N_HEADS)], axis=1)
    k = jnp.concatenate([rms(x, knw) for x in _heads(kraw, N_HEADS)], axis=1)
    return q, k


def _mem_out(qm, mqw, mk, mv):
    return [mem_head(a, mqw, b, c) for a, b, c in zip(_heads(qm, MEM_HEADS), _heads(mk, MEM_HEADS), _heads(mv, MEM_HEADS))]


def dn_out_fn(o, z, qm, onw, mqw, mk, mv):
    mix = [rms(a, onw) * jax.nn.silu(b) for a, b in zip(o, _heads(z, N_HEADS))]
    return jnp.concatenate(mix + _mem_out(qm, mqw, mk, mv), axis=1)


def fox_out_fn(o, gate, qm, mqw, mk, mv):
    return jnp.concatenate([o * jax.nn.sigmoid(gate)] + _mem_out(qm, mqw, mk, mv), axis=1)


_HBM = pl.BlockSpec(memory_space=pltpu.HBM)


def _place():
    return lax.axis_index("x"), lax.axis_index("y"), lax.axis_index("c")


class Rider:
    def __init__(self, ins, out_shape, scratch, start, finish):
        self.ins, self.out_shape, self.scratch, self.start, self.finish = list(ins), list(out_shape), list(scratch), start, finish
        self.results = None


def gather_rider(xs):
    n = len(xs)

    def plan(x_refs, out_refs, sems):
        send_sems, recv_sems, local_sems = sems
        x, y, c = _place()
        me, sibling = (x, y, c), (x, y, 1 - c)
        chips = [(1 - x, y), (x, 1 - y), (1 - x, 1 - y)]

        def copy(a, k, block, to, src=None):
            px, py, pc = block
            dst = out_refs[a].at[4 * px + 2 * py + pc]
            return pltpu.make_async_remote_copy(
                src_ref=dst if src is None else src, dst_ref=dst,
                send_sem=send_sems.at[a, k], recv_sem=recv_sems.at[a, k], device_id=to, device_id_type=MESH)

        mine = [pltpu.make_async_copy(x_refs[a], out_refs[a].at[4 * x + 2 * y + c], local_sems.at[a]) for a in range(n)]
        first = [copy(a, 0, me, sibling, src=x_refs[a]) for a in range(n)]
        first += [copy(a, 1 + j, me, (*chip, c), src=x_refs[a]) for j, chip in enumerate(chips) for a in range(n)]
        return copy, me, sibling, chips, mine, first

    def start(x_refs, out_refs, sems):
        _, _, _, _, mine, first = plan(x_refs, out_refs, sems)
        for cp in mine + first:
            cp.start()

    def finish(x_refs, out_refs, sems):
        copy, me, sibling, chips, mine, first = plan(x_refs, out_refs, sems)
        _, _, c = me
        passed = []
        for j, chip in enumerate(chips):
            for a in range(n):
                copy(a, 1 + j, (*chip, c), me).wait_recv()
                passed.append(copy(a, 4 + j, (*chip, c), sibling))
                passed[-1].start()
        for a in range(n):
            copy(a, 0, sibling, me).wait_recv()
        for j, chip in enumerate(chips):
            for a in range(n):
                copy(a, 4 + j, (*chip, 1 - c), me).wait_recv()
        for cp in first + passed:
            cp.wait_send()
        for cp in mine:
            cp.wait()

    return Rider(xs, [jax.ShapeDtypeStruct((N_DEV,) + a.shape, a.dtype) for a in xs],
                 [pltpu.SemaphoreType.DMA((n, 7)), pltpu.SemaphoreType.DMA((n, 7)), pltpu.SemaphoreType.DMA((n,))], start, finish)


def sibling_rider(gs):
    n = len(gs)

    def plan(g_refs, out_refs, sems):
        send_sems, recv_sems = sems
        x, y, c = _place()
        return [pltpu.make_async_remote_copy(
            src_ref=g_refs[a].at[2 * k + 1 - c], dst_ref=out_refs[a].at[k], send_sem=send_sems.at[a, k],
            recv_sem=recv_sems.at[a, k], device_id=(x, y, 1 - c), device_id_type=MESH) for a in range(n) for k in range(4)]

    def start(g_refs, out_refs, sems):
        for cp in plan(g_refs, out_refs, sems):
            cp.start()

    def finish(g_refs, out_refs, sems):
        copies = plan(g_refs, out_refs, sems)
        for cp in copies:
            cp.wait_recv()
        for cp in copies:
            cp.wait_send()

    return Rider(gs, [jax.ShapeDtypeStruct((4,) + g.shape[1:], g.dtype) for g in gs],
                 [pltpu.SemaphoreType.DMA((n, 4)), pltpu.SemaphoreType.DMA((n, 4))], start, finish)


def chips_rider(hs):
    n = len(hs)

    def plan(h_refs, out_refs, sems):
        send_sems, recv_sems, local_sems = sems
        x, y, c = _place()
        mine = 2 * x + y
        chips = [(1 - x, y), (x, 1 - y), (1 - x, 1 - y)]
        keep = [pltpu.make_async_copy(h_refs[a].at[mine], out_refs[a].at[mine], local_sems.at[a]) for a in range(n)]
        sends = [pltpu.make_async_remote_copy(
            src_ref=h_refs[a].at[2 * qx + qy], dst_ref=out_refs[a].at[mine], send_sem=send_sems.at[a, j],
            recv_sem=recv_sems.at[a, j], device_id=(qx, qy, c), device_id_type=MESH)
            for j, (qx, qy) in enumerate(chips) for a in range(n)]
        recvs = [pltpu.make_async_remote_copy(
            src_ref=h_refs[a].at[mine], dst_ref=out_refs[a].at[2 * qx + qy], send_sem=send_sems.at[a, j],
            recv_sem=recv_sems.at[a, j], device_id=(qx, qy, c), device_id_type=MESH)
            for j, (qx, qy) in enumerate(chips) for a in range(n)]
        return keep, sends, recvs

    def start(h_refs, out_refs, sems):
        keep, sends, _ = plan(h_refs, out_refs, sems)
        for cp in keep + sends:
            cp.start()

    def finish(h_refs, out_refs, sems):
        keep, sends, recvs = plan(h_refs, out_refs, sems)
        for cp in recvs:
            cp.wait_recv()
        for cp in sends:
            cp.wait_send()
        for cp in keep:
            cp.wait()

    return Rider(hs, [jax.ShapeDtypeStruct(h.shape, h.dtype) for h in hs],
                 [pltpu.SemaphoreType.DMA((n, 3)), pltpu.SemaphoreType.DMA((n, 3)), pltpu.SemaphoreType.DMA((n,))], start, finish)


def hosted_call(riders, body, *, out_shape, in_specs, out_specs, grid=(), scratch_shapes=(), **kw):
    riders = tuple(riders or ())
    if not riders:
        return pl.pallas_call(body, out_shape=out_shape, in_specs=in_specs, out_specs=out_specs, grid=grid,
                              scratch_shapes=scratch_shapes, **kw)
    single = not isinstance(out_shape, (list, tuple))
    k_out_shape = [out_shape] if single else list(out_shape)
    k_out_specs = [out_specs] if single else list(out_specs)
    n_in, n_out, n_scr = len(in_specs), len(k_out_shape), len(scratch_shapes)
    r_ins = [a for r in riders for a in r.ins]
    r_outs = [s for r in riders for s in r.out_shape]
    r_scr = [s for r in riders for s in r.scratch]

    def full_body(*refs):
        ins = refs[:n_in + len(r_ins)]
        outs = refs[n_in + len(r_ins):n_in + len(r_ins) + n_out + len(r_outs)]
        scr = refs[n_in + len(r_ins) + n_out + len(r_outs):]
        steps = math.prod(grid)
        step = 0
        for d, g in enumerate(grid):
            step = step * g + pl.program_id(d)

        def each(method):
            i0, o0, s0 = n_in, n_out, n_scr
            for r in riders:
                getattr(r, method)(ins[i0:i0 + len(r.ins)], outs[o0:o0 + len(r.out_shape)], scr[s0:s0 + len(r.scratch)])
                i0, o0, s0 = i0 + len(r.ins), o0 + len(r.out_shape), s0 + len(r.scratch)

        if steps == 1:
            each("start")
            body(*ins[:n_in], *outs[:n_out], *scr[:n_scr])
            each("finish")
        else:
            pl.when(step == 0)(lambda: each("start"))
            body(*ins[:n_in], *outs[:n_out], *scr[:n_scr])
            pl.when(step == steps - 1)(lambda: each("finish"))

    call = pl.pallas_call(
        full_body, out_shape=k_out_shape + r_outs, in_specs=list(in_specs) + [_HBM] * len(r_ins),
        out_specs=k_out_specs + [_HBM] * len(r_outs), grid=grid, scratch_shapes=list(scratch_shapes) + r_scr, **kw)

    def run(*args):
        res = call(*args, *r_ins)
        o0 = n_out
        for r in riders:
            r.results = list(res[o0:o0 + len(r.out_shape)])
            o0 += len(r.out_shape)
        return res[0] if single else list(res[:n_out])

    return run


def run_riders(riders, *, name):
    hosted_call(riders, lambda: None, name=name, out_shape=[], in_specs=[], out_specs=[])()
    return [r.results for r in riders]


def _pick(n, cands):
    for c in cands:
        if n % c == 0:
            return c
    return n


def _params(sem):
    return pltpu.CompilerParams(dimension_semantics=sem, vmem_limit_bytes=VMEM_LIMIT)


MATMUL_VMEM_BUDGET = 40 * 2 ** 20


def _matmul_tiles(m, n, k, bytes_a, bytes_b, bytes_mn, fixed):
    fm, fn, fk = fixed if fixed is not None else (None, None, None)

    def options(given, size, cands):
        return [given] if given else ([c for c in cands if size % c == 0] or [size])

    best = None
    for tm in options(fm, m, (2048, 1024, 512, 256, 128)):
        for tn in options(fn, n, (512, 256, 128)):
            for tk in options(fk, k, (2048, 1536, 1024, 512, 256, 128)):
                if 2 * (tm * tk * bytes_a + tk * tn * bytes_b + tm * tn * bytes_mn) + tm * tn * 4 > MATMUL_VMEM_BUDGET:
                    continue
                key = ((m // tm) * (n // tn) * (k // tk), -tk)
                if best is None or key < best[0]:
                    best = (key, (tm, tn, tk))
    assert best is not None, (m, n, k, fixed)
    return best[1]


def matmul(a, b, *, name, ta=False, tb=False, post=None, post_ins=(), row_ins=(), acc=False, extra_out=None,
           out_dtype=F32, tiles=None, b_view=None, out_view=None, riders=()):
    (k, m) = a.shape if ta else a.shape[::-1]
    (kb, n) = b_view[:2] if b_view is not None else (b.shape[::-1] if tb else b.shape)
    assert k == kb, (a.shape, b.shape, ta, tb)
    bytes_mn = sum(p.dtype.itemsize for p in post_ins) + jnp.dtype(out_dtype).itemsize
    bytes_mn += jnp.dtype(extra_out[1]).itemsize if extra_out else 0
    tm, tn, tk = _matmul_tiles(m, n, k, a.dtype.itemsize, b.dtype.itemsize, bytes_mn, tiles)
    assert not acc or tn == n, (name, tn, n)
    nk = k // tk
    dims = ((0,) if ta else (1,), (1,) if tb else (0,))
    n_post, n_row = len(post_ins), len(row_ins)
    n_out = 1 + bool(extra_out) + bool(acc)

    def body(*refs):
        a_ref, b_ref = refs[:2]
        post_refs = refs[2:2 + n_post + n_row]
        o_refs, acc_ref = refs[-1 - n_out:-1], refs[-1]
        first_rows, kk = pl.program_id(0) == 0, pl.program_id(2)

        @pl.when(kk == 0)
        def _():
            acc_ref[...] = jnp.zeros_like(acc_ref)

        b_tile = b_ref[...]
        acc_ref[...] += _bdot(a_ref[...], b_tile.reshape(-1, b_tile.shape[-1]), dims)

        @pl.when(kk == nk - 1)
        def _():
            r = acc_ref[...]
            rows = [p[...] for p in post_refs[n_post:]]
            if post is not None:
                r = post(r, *[p[...] for p in post_refs[:n_post]], *rows)
            if acc:
                r, s = r
                sum_ref = o_refs[-1]

                @pl.when(first_rows)
                def _():
                    sum_ref[...] = s

                @pl.when(jnp.logical_not(first_rows))
                def _():
                    sum_ref[...] += s

            o_refs[0][...] = r.astype(out_dtype)
            if extra_out:
                o_refs[1][...] = extra_out[0](r, *rows).astype(extra_out[1])

    a_spec = pl.BlockSpec((tk, tm), lambda i, j, kk: (kk, i)) if ta else pl.BlockSpec((tm, tk), lambda i, j, kk: (i, kk))
    if b_view is not None:
        b_spec = b_view[2]
    else:
        b_spec = pl.BlockSpec((tn, tk), lambda i, j, kk: (j, kk)) if tb else pl.BlockSpec((tk, tn), lambda i, j, kk: (kk, j))
    mn_spec = pl.BlockSpec((tm, tn), lambda i, j, kk: (i, j))
    row_spec = pl.BlockSpec((1, tn), lambda i, j, kk: (0, j))
    o_shape, o_spec = ((m, n), mn_spec) if out_view is None else out_view
    out_shape = [jax.ShapeDtypeStruct(o_shape, out_dtype)]
    out_specs = [o_spec]
    if extra_out:
        out_shape.append(jax.ShapeDtypeStruct((m, n), extra_out[1]))
        out_specs.append(mn_spec)
    if acc:
        out_shape.append(jax.ShapeDtypeStruct((1, n), F32))
        out_specs.append(row_spec)
    res = hosted_call(
        riders, body, name=name, grid=(m // tm, n // tn, nk),
        in_specs=[a_spec, b_spec] + [mn_spec] * n_post + [row_spec] * n_row, out_specs=out_specs, out_shape=out_shape,
        scratch_shapes=[pltpu.VMEM((tm, tn), F32)],
        compiler_params=_params(("arbitrary" if acc else "parallel", "parallel", "arbitrary")),
    )(a, b, *post_ins, *row_ins)
    return res if n_out > 1 else res[0]


def rows_call(fn, row_ins, full_ins, row_outs, acc_outs, *, tm, name, riders=()):
    row_ins = [r if isinstance(r, tuple) else (r, r.shape[-1], 0) for r in row_ins]
    t = row_ins[0][0].shape[-2]
    tm = min(tm, t)
    n_in = len(row_ins) + len(full_ins)
    n_row = len(row_outs)

    def body(*refs):
        res = fn(*[[r[h] for h in range(r.shape[0])] if (i < len(row_ins) and len(r.shape) == 3) else r[...]
                   for i, r in enumerate(refs[:n_in])])
        res = res if isinstance(res, (tuple, list)) else (res,)
        outs = refs[n_in:]
        for ref, val in zip(outs[:n_row], res[:n_row]):
            if len(ref.shape) == 3:
                for h, vh in enumerate(val):
                    ref[h] = vh.astype(ref.dtype)
            else:
                ref[...] = val.astype(ref.dtype)
        first = pl.program_id(0) == 0
        for ref, val in zip(outs[n_row:], res[n_row:]):
            @pl.when(first)
            def _(ref=ref, val=val):
                ref[...] = val

            @pl.when(jnp.logical_not(first))
            def _(ref=ref, val=val):
                ref[...] += val

    def full_spec(shape):
        return pl.BlockSpec(shape, lambda i, nd=len(shape): (0,) * nd)

    def row_spec(lead, w, cb):
        if lead is None:
            return pl.BlockSpec((tm, w), lambda i: (i, cb))
        return pl.BlockSpec((lead, tm, w), lambda i: (0, i, cb))

    def lead_cols(c):
        return c if isinstance(c, tuple) else (None, c)

    in_specs = [row_spec(a.shape[0] if a.ndim == 3 else None, w, cb) for (a, w, cb) in row_ins]
    in_specs += [full_spec(f.shape) for f in full_ins]
    out_specs = [row_spec(*lead_cols(c), 0) for c, _ in row_outs] + [full_spec(s) for s in acc_outs]
    out_shape = [jax.ShapeDtypeStruct(tuple(d for d in (lead_cols(c)[0], t, lead_cols(c)[1]) if d is not None), dt)
                 for c, dt in row_outs] + [jax.ShapeDtypeStruct(s, F32) for s in acc_outs]
    res = hosted_call(
        riders, body, name=name, grid=(t // tm,), in_specs=in_specs, out_specs=out_specs, out_shape=out_shape,
        compiler_params=_params(("arbitrary",)),
    )(*[r[0] for r in row_ins], *full_ins)
    return res


def vjp_rows(fn, n_diff_row, row_diff_full):
    def bwd(*args, n_row, n_ct):
        prim_rows = args[:n_row]
        cts = args[n_row:n_row + n_ct]
        fulls = args[n_row + n_ct:]
        _, vjp = jax.vjp(fn, *prim_rows, *fulls)
        g = vjp(cts[0] if n_ct == 1 else tuple(cts))
        out = list(g[:n_diff_row])
        out += [gf for gf, d in zip(g[n_row:], row_diff_full) if d]
        return tuple(out)
    return bwd


def _shift_down(x, s):
    if s == 0:
        return x
    t = lax.broadcasted_iota(jnp.int32, x.shape, 0)
    return jnp.where(t >= s, pltpu.roll(x, s, 0), 0.0)


def _shift_up(x, s):
    if s == 0:
        return x
    n = x.shape[0]
    t = lax.broadcasted_iota(jnp.int32, x.shape, 0)
    return jnp.where(t < n - s, pltpu.roll(x, n - s, 0), 0.0)


def _conv(x, w_ref):
    return sum(w_ref[pl.ds(j, 1), :] * _shift_down(x, CONV_WIDTH - 1 - j) for j in range(CONV_WIDTH))


_DN_POST = (lambda c: l2n(jax.nn.silu(c)) * SCALE, lambda c: l2n(jax.nn.silu(c)), jax.nn.silu)


def dn_prep_fwd(proj, conv_w, *, name, riders=()):
    t = proj.shape[0]

    def body(xq, xk, xv, wq, wk, wv, oq, ok, ov):
        for x_ref, w_ref, o_ref, post in zip((xq, xk, xv), (wq, wk, wv), (oq, ok, ov), _DN_POST):
            o_ref[...] = post(_conv(x_ref[...], w_ref))

    x_specs = [pl.BlockSpec((t, HEAD_DIM), lambda h, g=g: (0, g * N_HEADS + h)) for g in range(3)]
    w_specs = [pl.BlockSpec((CONV_WIDTH, HEAD_DIM), lambda h, g=g: (0, g * N_HEADS + h)) for g in range(3)]
    o_spec = pl.BlockSpec((None, t, HEAD_DIM), lambda h: (h, 0, 0))
    return hosted_call(
        riders, body, name=name, grid=(N_HEADS,), in_specs=x_specs + w_specs, out_specs=[o_spec] * 3,
        out_shape=[jax.ShapeDtypeStruct((N_HEADS, t, HEAD_DIM), F32)] * 3, compiler_params=_params(("parallel",)),
    )(proj, proj, proj, conv_w, conv_w, conv_w)


def dn_prep_bwd(proj, conv_w, dq, dk, dv, *, name, riders=()):
    t = proj.shape[0]

    def body(xq, xk, xv, wq, wk, wv, gq, gk, gv, dxq, dxk, dxv, dwq, dwk, dwv):
        for x_ref, w_ref, g_ref, dx_ref, dw_ref, post in zip(
                (xq, xk, xv), (wq, wk, wv), (gq, gk, gv), (dxq, dxk, dxv), (dwq, dwk, dwv), _DN_POST):
            x = x_ref[...]
            _, vjp = jax.vjp(post, _conv(x, w_ref))
            dc, = vjp(g_ref[...])
            dx = sum(w_ref[pl.ds(j, 1), :] * _shift_up(dc, CONV_WIDTH - 1 - j) for j in range(CONV_WIDTH))
            dx_ref[...] = dx.astype(dx_ref.dtype)
            for j in range(CONV_WIDTH):
                dw_ref[pl.ds(j, 1), :] = jnp.sum(dc * _shift_down(x, CONV_WIDTH - 1 - j), axis=0, keepdims=True)

    x_specs = [pl.BlockSpec((t, HEAD_DIM), lambda h, g=g: (0, g * N_HEADS + h)) for g in range(3)]
    w_specs = [pl.BlockSpec((CONV_WIDTH, HEAD_DIM), lambda h, g=g: (0, g * N_HEADS + h)) for g in range(3)]
    g_spec = pl.BlockSpec((None, t, HEAD_DIM), lambda h: (h, 0, 0))
    dx_spec = pl.BlockSpec((t, HEAD_DIM), lambda h: (0, h))
    dw_spec = pl.BlockSpec((CONV_WIDTH, HEAD_DIM), lambda h: (0, h))
    return hosted_call(
        riders, body, name=name, grid=(N_HEADS,), in_specs=x_specs + w_specs + [g_spec] * 3, out_specs=[dx_spec] * 3 + [dw_spec] * 3,
        out_shape=[jax.ShapeDtypeStruct((t, D_MODEL), BF16)] * 3 + [jax.ShapeDtypeStruct((CONV_WIDTH, D_MODEL), F32)] * 3,
        compiler_params=_params(("parallel",)),
    )(proj, proj, proj, conv_w, conv_w, conv_w, dq, dk, dv)


INTRA_CHUNKS = 4


def _lane_column(x, lane_index):
    lane = lax.broadcasted_iota(jnp.int32, x.shape, 1)
    return jnp.sum(jnp.where(lane == lane_index, x, 0.0), axis=1, keepdims=True)


def _head_columns(g, first_lane):
    return jnp.concatenate([_lane_column(g, first_lane + h)[None] for h in range(N_HEADS)], axis=0)


def _intra_of_gates(q, k, v, gates):
    nb = N_HEADS * (gates.shape[0] // CHUNK)

    def chunks(x):
        return x.reshape(nb, CHUNK, x.shape[-1])

    res = delta_intra(chunks(q), chunks(k), chunks(v), chunks(_head_columns(gates, 0)), chunks(_head_columns(gates, N_HEADS)))
    return tuple(x.reshape(N_HEADS, -1, x.shape[-1]) for x in res)


def _step_of_gates(s, q, k, gates, u, w, qk):
    return delta_step(s, q, k, _head_columns(gates, 0), u, w, qk)


def _head_major(rows, w, index):
    return pl.BlockSpec((N_HEADS, rows, w), lambda i: (0, index(i), 0))


def delta_intra_fwd(q, k, v, gates, *, name, riders=()):
    t = q.shape[1]
    rows = min(INTRA_CHUNKS, t // CHUNK) * CHUNK

    def body(q_ref, k_ref, v_ref, g_ref, u_ref, w_ref, qk_ref):
        for ref, val in zip((u_ref, w_ref, qk_ref), _intra_of_gates(q_ref[...], k_ref[...], v_ref[...], g_ref[...])):
            ref[...] = val

    x_spec, qk_spec = (_head_major(rows, w, lambda i: i) for w in (HEAD_DIM, CHUNK))
    g_spec = pl.BlockSpec((rows, LANES), lambda i: (i, 0))
    return hosted_call(
        riders, body, name=name, grid=(t // rows,), in_specs=[x_spec] * 3 + [g_spec], out_specs=[x_spec, x_spec, qk_spec],
        out_shape=[jax.ShapeDtypeStruct((N_HEADS, t, HEAD_DIM), F32)] * 2 + [jax.ShapeDtypeStruct((N_HEADS, t, CHUNK), F32)],
        compiler_params=_params(("parallel",)),
    )(q, k, v, gates)


def delta_seq_fwd(q, k, gates, u, w, qk, *, name, riders=()):
    t = q.shape[1]
    nc = t // CHUNK

    def body(q_ref, k_ref, g_ref, u_ref, w_ref, qk_ref, o_ref, s0_ref, s_ref):
        @pl.when(pl.program_id(0) == 0)
        def _():
            s_ref[...] = jnp.zeros_like(s_ref)

        s = s_ref[...]
        s0_ref[...] = s
        o, s_new = _step_of_gates(s, q_ref[...], k_ref[...], g_ref[...], u_ref[...], w_ref[...], qk_ref[...])
        o_ref[...] = o
        s_ref[...] = s_new

    x_spec, qk_spec = (_head_major(CHUNK, w, lambda c: c) for w in (HEAD_DIM, CHUNK))
    g_spec = pl.BlockSpec((CHUNK, LANES), lambda c: (c, 0))
    s_spec = pl.BlockSpec((N_HEADS, None, HEAD_DIM, HEAD_DIM), lambda c: (0, c, 0, 0))
    return hosted_call(
        riders, body, name=name, grid=(nc,), in_specs=[x_spec, x_spec, g_spec, x_spec, x_spec, qk_spec], out_specs=[x_spec, s_spec],
        out_shape=[jax.ShapeDtypeStruct((N_HEADS, t, HEAD_DIM), F32),
                   jax.ShapeDtypeStruct((N_HEADS, nc, HEAD_DIM, HEAD_DIM), F32)],
        scratch_shapes=[pltpu.VMEM((N_HEADS, HEAD_DIM, HEAD_DIM), F32)],
        compiler_params=_params(("arbitrary",)),
    )(q, k, gates, u, w, qk)


def delta_seq_bwd(q, k, gates, u, w, qk, s0, do, *, name, riders=()):
    t = q.shape[1]
    nc = t // CHUNK

    def body(q_ref, k_ref, g_ref, u_ref, w_ref, qk_ref, s0_ref, do_ref,
             dq_ref, dk_ref, dg_ref, du_ref, dw_ref, dqk_ref, ds_ref):
        @pl.when(pl.program_id(0) == 0)
        def _():
            ds_ref[...] = jnp.zeros_like(ds_ref)

        _, vjp = jax.vjp(_step_of_gates, s0_ref[...], q_ref[...], k_ref[...], g_ref[...], u_ref[...], w_ref[...], qk_ref[...])
        ds, dq, dk, dg, du, dw, dqk = vjp((do_ref[...], ds_ref[...]))
        for ref, val in zip((ds_ref, dq_ref, dk_ref, dg_ref, du_ref, dw_ref, dqk_ref), (ds, dq, dk, dg, du, dw, dqk)):
            ref[...] = val

    x_spec, qk_spec = (_head_major(CHUNK, w, lambda c: nc - 1 - c) for w in (HEAD_DIM, CHUNK))
    g_spec = pl.BlockSpec((CHUNK, LANES), lambda c: (nc - 1 - c, 0))
    s_spec = pl.BlockSpec((N_HEADS, None, HEAD_DIM, HEAD_DIM), lambda c: (0, nc - 1 - c, 0, 0))
    head_shape = [jax.ShapeDtypeStruct((N_HEADS, t, w_), F32) for w_ in (HEAD_DIM, HEAD_DIM, HEAD_DIM, HEAD_DIM, CHUNK)]
    return hosted_call(
        riders, body, name=name, grid=(nc,), in_specs=[x_spec, x_spec, g_spec, x_spec, x_spec, qk_spec, s_spec, x_spec],
        out_specs=[x_spec, x_spec, g_spec, x_spec, x_spec, qk_spec],
        out_shape=head_shape[:2] + [jax.ShapeDtypeStruct((t, LANES), F32)] + head_shape[2:],
        scratch_shapes=[pltpu.VMEM((N_HEADS, HEAD_DIM, HEAD_DIM), F32)],
        compiler_params=_params(("arbitrary",)),
    )(q, k, gates, u, w, qk, s0, do)


def delta_intra_bwd(q, k, v, gates, du, dw, dqk, dq_s, dk_s, dg_s, *, name, riders=()):
    t = q.shape[1]
    rows = min(INTRA_CHUNKS, t // CHUNK) * CHUNK

    def body(q_ref, k_ref, v_ref, g_ref, du_ref, dw_ref, dqk_ref, dqs_ref, dks_ref, dgs_ref, dq_ref, dk_ref, dv_ref, dg_ref):
        _, vjp = jax.vjp(_intra_of_gates, q_ref[...], k_ref[...], v_ref[...], g_ref[...])
        dq, dk, dv, dg = vjp((du_ref[...], dw_ref[...], dqk_ref[...]))
        dq_ref[...] = dq + dqs_ref[...]
        dk_ref[...] = dk + dks_ref[...]
        dv_ref[...] = dv
        dg_ref[...] = dg + dgs_ref[...]

    x_spec, qk_spec = (_head_major(rows, w, lambda i: i) for w in (HEAD_DIM, CHUNK))
    g_spec = pl.BlockSpec((rows, LANES), lambda i: (i, 0))
    return hosted_call(
        riders, body, name=name, grid=(t // rows,),
        in_specs=[x_spec] * 3 + [g_spec, x_spec, x_spec, qk_spec, x_spec, x_spec, g_spec],
        out_specs=[x_spec] * 3 + [g_spec],
        out_shape=[jax.ShapeDtypeStruct((N_HEADS, t, HEAD_DIM), F32)] * 3 + [jax.ShapeDtypeStruct((t, LANES), F32)],
        compiler_params=_params(("parallel",)),
    )(q, k, v, gates, du, dw, dqk, dq_s, dk_s, dg_s)


_V_BLOCK = 2 * N_HEADS
FOX_GROUPS = 16


def _fox_groups(t):
    nq = t // Q_BLOCK
    per = max(1, nq // FOX_GROUPS)
    return [(g0, per, (g0 + per) * Q_BLOCK) for g0 in range(0, nq, per)]


def fox_attn_fwd(q, k, proj, fq, fk, *, name, riders=()):
    t = q.shape[0]

    def body(q_ref, k_ref, v_ref, fq_ref, fk_ref, o_ref, kb_ref, vb_ref):
        head = pl.program_id(0)
        kb_ref[...] = k_ref[...].astype(BF16)
        vb_ref[...] = v_ref[...].astype(BF16)
        for g0, per, keys in _fox_groups(t):
            def block(j, carry, g0=g0, keys=keys):
                rows = pl.ds((g0 + j) * Q_BLOCK, Q_BLOCK)
                p = fox_probs(q_ref[rows, :].astype(BF16), kb_ref[0:keys, :], _lane_column(fq_ref[rows, :], head),
                              fk_ref[:, 0:keys], (g0 + j) * Q_BLOCK)
                o_ref[rows, :] = jnp.dot(p.astype(BF16), vb_ref[0:keys, :], preferred_element_type=F32)
                return carry
            for j in range(per):
                block(j, 0)

    x_spec = pl.BlockSpec((t, HEAD_DIM), lambda h: (0, h))
    v_spec = pl.BlockSpec((t, HEAD_DIM), lambda h: (0, _V_BLOCK + h))
    fq_spec = pl.BlockSpec((t, LANES), lambda h: (0, 0))
    fk_spec = pl.BlockSpec((None, 1, t), lambda h: (h, 0, 0))
    return hosted_call(
        riders, body, name=name, grid=(N_HEADS,), in_specs=[x_spec, x_spec, v_spec, fq_spec, fk_spec], out_specs=x_spec,
        out_shape=jax.ShapeDtypeStruct((t, D_MODEL), F32), scratch_shapes=[pltpu.VMEM((t, HEAD_DIM), BF16)] * 2,
        compiler_params=_params(("parallel",)),
    )(q, k, proj, fq, fk)


def fox_attn_bwd(q, k, proj, fq, fk, do, *, name, riders=()):
    t = q.shape[0]

    def body(q_ref, k_ref, v_ref, fq_ref, fk_ref, do_ref, dq_ref, dk_ref, dv_out_ref, dfq_ref, dfk_ref, kb_ref, vb_ref, dv_ref):
        head = pl.program_id(0)

        @pl.when(head == 0)
        def _():
            dfq_ref[...] = jnp.zeros_like(dfq_ref)

        kb_ref[...] = k_ref[...].astype(BF16)
        vb_ref[...] = v_ref[...].astype(BF16)
        dk_ref[...] = jnp.zeros_like(dk_ref)
        dv_ref[...] = jnp.zeros_like(dv_ref)
        dfk_ref[...] = jnp.zeros_like(dfk_ref)
        nt = (((1,), (1,)), ((), ()))
        tn = (((0,), (0,)), ((), ()))
        for g0, per, keys in _fox_groups(t):
            def block(j, carry, g0=g0, keys=keys):
                rows = pl.ds((g0 + j) * Q_BLOCK, Q_BLOCK)
                qb, dob = q_ref[rows, :].astype(BF16), do_ref[rows, :].astype(BF16)
                kb, vb = kb_ref[0:keys, :], vb_ref[0:keys, :]
                p = fox_probs(qb, kb, _lane_column(fq_ref[rows, :], head), fk_ref[:, 0:keys], (g0 + j) * Q_BLOCK)
                dp = lax.dot_general(dob, vb, nt, preferred_element_type=F32)
                dz = p * (dp - jnp.sum(dp * p, axis=-1, keepdims=True))
                pb, dzb = p.astype(BF16), dz.astype(BF16)
                dq_ref[rows, :] = jnp.dot(dzb, kb, preferred_element_type=F32)
                lane = lax.broadcasted_iota(jnp.int32, (Q_BLOCK, LANES), 1)
                dfq_ref[rows, :] += jnp.where(lane == head, jnp.sum(dz, axis=-1, keepdims=True), 0.0)
                dk_ref[0:keys, :] += lax.dot_general(dzb, qb, tn, preferred_element_type=F32)
                dv_ref[0:keys, :] += lax.dot_general(pb, dob, tn, preferred_element_type=F32)
                dfk_ref[:, 0:keys] -= jnp.sum(dz, axis=0, keepdims=True)
                return carry
            for j in range(per):
                block(j, 0)
        dv_out_ref[...] = dv_ref[...].astype(dv_out_ref.dtype)

    x_spec = pl.BlockSpec((t, HEAD_DIM), lambda h: (0, h))
    v_spec = pl.BlockSpec((t, HEAD_DIM), lambda h: (0, _V_BLOCK + h))
    fq_spec = pl.BlockSpec((t, LANES), lambda h: (0, 0))
    fk_spec = pl.BlockSpec((None, 1, t), lambda h: (h, 0, 0))
    return hosted_call(
        riders, body, name=name, grid=(N_HEADS,), in_specs=[x_spec, x_spec, v_spec, fq_spec, fk_spec, x_spec],
        out_specs=[x_spec, x_spec, x_spec, fq_spec, fk_spec],
        out_shape=[jax.ShapeDtypeStruct((t, D_MODEL), F32)] * 2 + [jax.ShapeDtypeStruct((t, D_MODEL), BF16)]
        + [jax.ShapeDtypeStruct((t, LANES), F32), jax.ShapeDtypeStruct((N_HEADS, 1, t), F32)],
        scratch_shapes=[pltpu.VMEM((t, HEAD_DIM), BF16)] * 2 + [pltpu.VMEM((t, HEAD_DIM), F32)],
        compiler_params=_params(("arbitrary",)),
    )(q, k, proj, fq, fk, do)


def memkv_fwd(mem, mnw, wkv, mknw, *, name):
    n = mem.shape[0]

    def body(mem_ref, mnw_ref, w_ref, mknw_ref, mk_ref, mv_ref):
        mk, mv = memkv_fn(mem_ref[...], mnw_ref[...], w_ref[...], mknw_ref[...])
        mk_ref[...] = mk
        mv_ref[...] = mv

    return pl.pallas_call(
        body, name=name, out_shape=[jax.ShapeDtypeStruct((n, MEM_WIDTH), F32)] * 2,
        compiler_params=pltpu.CompilerParams(vmem_limit_bytes=VMEM_LIMIT),
    )(mem, mnw, wkv, mknw)


def memkv_bwd(mem, mnw, wkv, mknw, dmk, dmv, *, name):
    def body(mem_ref, mnw_ref, w_ref, mknw_ref, dmk_ref, dmv_ref, dmnw_ref, dw_ref, dmknw_ref):
        f = functools.partial(memkv_fn, mem_ref[...])
        _, vjp = jax.vjp(f, mnw_ref[...], w_ref[...].astype(F32), mknw_ref[...])
        dmnw, dw, dmknw = vjp((dmk_ref[...], dmv_ref[...]))
        dmnw_ref[...] = dmnw
        dw_ref[...] = dw.astype(dw_ref.dtype)
        dmknw_ref[...] = dmknw

    return pl.pallas_call(
        body, name=name,
        out_shape=[jax.ShapeDtypeStruct(mnw.shape, F32), jax.ShapeDtypeStruct(wkv.shape, BF16), jax.ShapeDtypeStruct(mknw.shape, F32)],
        compiler_params=pltpu.CompilerParams(vmem_limit_bytes=VMEM_LIMIT),
    )(mem, mnw, wkv, mknw, dmk, dmv)


def _row(v, width=None):
    v = v.reshape(1, -1)
    if width is not None and v.shape[1] < width:
        v = jnp.pad(v, ((0, 0), (0, width - v.shape[1])))
    return v


def _norm_fwd(x, w, name, riders=()):
    return rows_call(lambda x, w: rms(x, w), [x], [w], [(D_MODEL, BF16)], [], tm=512, name=name, riders=riders)[0]


FF_PIECE = D_FF // N_DEV


def _add(r, x, *rows):
    return r + x


def _norm_rows(r, w):
    return rms(r, w)


def _norm_bwd_post(dh, x, dx_in, w):
    _, vjp = jax.vjp(rms, x, w)
    dx, dw = vjp(dh)
    return dx + dx_in, dw


def _piece(rows, cols, index):
    return pl.BlockSpec((None, rows, cols), lambda i, j, kk: (index(i, j, kk), 0, 0))


def _two_pieces(rows, cols, index):
    return pl.BlockSpec((2, rows, cols), lambda i, j, kk: (index(i, j, kk), 0, 0))


def _mlp_fwd(x, h2, w1, w2, layer, riders=(), next_norm_w=None):
    riders = list(riders) + [None, None]
    u, a1 = matmul(h2, w1, name=f"mlp1_fwd_{layer}", tiles=(None, FF_PIECE, D_MODEL),
                   extra_out=(lambda u: jnp.square(jnp.maximum(u, 0.0)), BF16),
                   b_view=(D_MODEL, D_FF, _piece(D_MODEL, FF_PIECE, lambda i, j, kk: j)), riders=riders[0])
    norm = dict(row_ins=[next_norm_w], extra_out=(_norm_rows, BF16)) if next_norm_w is not None else {}
    y = matmul(a1, w2, name=f"mlp2_fwd_{layer}", post=_add, post_ins=[x], tiles=(None, D_MODEL, 2 * FF_PIECE),
               b_view=(D_FF, D_MODEL, _two_pieces(FF_PIECE, D_MODEL, lambda i, j, kk: kk)), riders=riders[1], **norm)
    return y, (x, h2, u, a1)


def pair_sum(g, got, *, name):
    _, rows, cols = g.shape
    tile = _pick(rows, (512, 256, 128))
    c = lax.axis_index("c").astype(jnp.int32).reshape(1)

    def body(c_ref, a_ref, b_ref, o_ref):
        o_ref[...] = (a_ref[...].astype(F32) + b_ref[...].astype(F32)).astype(o_ref.dtype)

    grid_spec = pltpu.PrefetchScalarGridSpec(
        num_scalar_prefetch=1, grid=(4, rows // tile),
        in_specs=[pl.BlockSpec((None, tile, cols), lambda k, i, c_ref: (2 * k + c_ref[0], i, 0)),
                  pl.BlockSpec((None, tile, cols), lambda k, i, c_ref: (k, i, 0))],
        out_specs=pl.BlockSpec((None, tile, cols), lambda k, i, c_ref: (k, i, 0)))
    return pl.pallas_call(
        body, name=name, grid_spec=grid_spec, out_shape=jax.ShapeDtypeStruct((4, rows, cols), g.dtype),
        compiler_params=_params(("parallel", "parallel")),
    )(c, g, got)


def chip_sums(names, pieces, gots):
    return [pair_sum(a, got, name=f"grads_pair_sum_{n}") for n, a, got in zip(names, pieces, gots)]


def _mlp_bwd(dy, res, n2w, w1, w2, layer, riders=()):
    x, h2, u, a1 = res
    du = matmul(dy, w2, tb=True, name=f"mlp2_dx_{layer}", out_dtype=BF16, tiles=(None, 2 * FF_PIECE, D_MODEL),
                post=lambda r, u: r * (2.0 * jnp.maximum(u, 0.0)), post_ins=[u],
                b_view=(D_MODEL, D_FF, _two_pieces(FF_PIECE, D_MODEL, lambda i, j, kk: j)), riders=riders)
    dw2 = matmul(a1, dy, ta=True, name=f"mlp2_dw_{layer}", out_dtype=BF16, tiles=(FF_PIECE, D_MODEL, None), out_view=(
        w2.shape, _piece(FF_PIECE, D_MODEL, lambda i, j, kk: i)))
    sib2 = sibling_rider([dw2])
    dx, dn2w = matmul(du, w1, tb=True, name=f"mlp1_dx_{layer}", tiles=(None, D_MODEL, FF_PIECE),
                      b_view=(D_FF, D_MODEL, _piece(D_MODEL, FF_PIECE, lambda i, j, kk: kk)),
                      post=_norm_bwd_post, post_ins=[x, dy], row_ins=[n2w], acc=True, riders=[sib2])
    dw1 = matmul(h2, du, ta=True, name=f"mlp1_dw_{layer}", out_dtype=BF16, tiles=(D_MODEL, FF_PIECE, None), out_view=(
        w1.shape, _piece(D_MODEL, FF_PIECE, lambda i, j, kk: j)))
    return dx, dw1, dw2, dn2w, sibling_rider([dw1]), sib2


def _in_proj_bwd(h, dmain, dsmall, w_main, w_small, x, dx_in, n1w, tag):
    dh = matmul(dmain, w_main, tb=True, name=f"inproj_dx_main_{tag}")

    def post(r, dh_main, x, dx_in, w):
        return _norm_bwd_post(r + dh_main, x, dx_in, w)

    dx, dn1w = matmul(dsmall, w_small, tb=True, name=f"inproj_dx_small_{tag}", tiles=(None, D_MODEL, None),
                      post=post, post_ins=[dh, x, dx_in], row_ins=[n1w], acc=True)
    dw_main = matmul(h, dmain, ta=True, out_dtype=BF16, name=f"inproj_dw_main_{tag}")
    dw_small = matmul(h, dsmall, ta=True, out_dtype=BF16, name=f"inproj_dw_small_{tag}")
    return dx, dn1w, dw_main, dw_small


def local_step(x, mem, target, w, m, v):
    t = x.shape[0]
    n_mem = mem.shape[0]
    g = {}

    def wire(a):
        return a.astype(BF16)

    (dn_g,), = run_riders([gather_rider([wire(w["dn_w_in"][0])])], name="weights_gather_first")
    dn_main, dn_ab = in_proj_weights(dn_g, DN_IN, 2 * N_HEADS)
    fox_w = wire(w["fox_w_in"][0])
    ride_out = gather_rider([wire(w["w_out"][0]), w["dn_conv_w"][0]])
    ride_out_1 = gather_rider([wire(w["w_out"][1])])
    ride_kv = gather_rider([wire(w["w_mem_kv"])])
    ride_mlp1_0 = gather_rider([wire(w["w_mlp1"][0])])
    ride_mlp2_0 = gather_rider([wire(w["w_mlp2"][0])])
    ride_fox_a, ride_fox_b = gather_rider([fox_w[:D_MODEL // 2]]), gather_rider([fox_w[D_MODEL // 2:]])
    ride_mlp_1 = gather_rider([wire(w["w_mlp1"][1]), wire(w["w_mlp2"][1])])
    mnw, mknw = _row(w["mem_norm_w"]), _row(w["mem_k_norm_w"])

    n1w0, n2w0 = _row(w["norm1_w"][0]), _row(w["norm2_w"][0])
    n1w1, n2w1 = _row(w["norm1_w"][1]), _row(w["norm2_w"][1])
    alog, dtb = _row(w["dn_a_log"][0], LANES), _row(w["dn_dt_bias"][0], LANES)
    onw, mqw0 = _row(w["dn_o_norm_w"][0]), _row(w["memq_norm_w"][0])
    x0 = x
    h0 = _norm_fwd(x0, n1w0, "norm1_fwd_0")
    pm0 = matmul(h0, dn_main, name="inproj_main_0", riders=[ride_out])
    w_out0 = ride_out.results[0].reshape(OUT_IN, D_MODEL)
    conv_w = ride_out.results[1].transpose(1, 0, 2).reshape(CONV_WIDTH, 3 * D_MODEL)
    ps0 = matmul(h0, dn_ab, name="inproj_small_0")
    gates = rows_call(dn_gates_fn, [ps0], [alog, dtb], [(LANES, F32)], [], tm=512, name="dn_gates_fwd")[0]
    q0, k0, v0 = dn_prep_fwd(pm0, conv_w, name="dn_prep_fwd", riders=[ride_kv])
    w_kv = ride_kv.results[0].reshape(D_MODEL, D_MODEL)
    mk, mv = memkv_fwd(mem, mnw, w_kv, mknw, name="memkv_fwd")
    u0, w0, qk0 = delta_intra_fwd(q0, k0, v0, gates, name="delta_intra_fwd", riders=[ride_mlp1_0])
    o0, s_start = delta_seq_fwd(q0, k0, gates, u0, w0, qk0, name="delta_seq_fwd", riders=[ride_mlp2_0])
    cat0 = rows_call(dn_out_fn, [o0, (pm0, D_MODEL, 3), (pm0, MEM_WIDTH, 8)], [onw, mqw0, mk, mv],
                     [(D_MODEL + MEM_WIDTH, BF16)], [], tm=256, name="dn_out_fwd")[0]
    (w1_0,), (w2_0,) = ride_mlp1_0.results, ride_mlp2_0.results
    x1, h2_0 = matmul(cat0, w_out0, post=_add, post_ins=[x0], row_ins=[n2w0], extra_out=(_norm_rows, BF16),
                      tiles=(None, D_MODEL, None), name="wout_fwd_0")
    (x2, h1), mlp_res0 = _mlp_fwd(x1, h2_0, w1_0, w2_0, 0, riders=[[ride_fox_a], [ride_fox_b]], next_norm_w=n1w1)
    fox_main, fox_f = in_proj_weights(
        jnp.concatenate([ride_fox_a.results[0], ride_fox_b.results[0]], axis=1), FOX_IN, N_HEADS)

    fbias = _row(w["fox_f_bias"][0], LANES)
    qnw, knw, mqw1 = _row(w["fox_q_norm_w"][0]), _row(w["fox_k_norm_w"][0]), _row(w["memq_norm_w"][1])
    pm1 = matmul(h1, fox_main, name="inproj_main_1", riders=[ride_out_1])
    w_out1 = ride_out_1.results[0].reshape(OUT_IN, D_MODEL)
    ps1 = matmul(h1, fox_f, name="inproj_small_1")
    fq = rows_call(fox_fcum_fn, [ps1], [fbias], [(LANES, F32)], [], tm=t, name="fox_fcum_fwd")[0]
    fk = fq[:, :N_HEADS].T[:, None, :]
    q1, k1 = rows_call(fox_qk_fn, [(pm1, D_MODEL, 0), (pm1, D_MODEL, 1)], [qnw, knw], [(D_MODEL, F32)] * 2, [], tm=256,
                       name="fox_qk_fwd")
    o1 = fox_attn_fwd(q1, k1, pm1, fq, fk, name="fox_attn_fwd", riders=[ride_mlp_1])
    cat1 = rows_call(fox_out_fn, [o1, (pm1, D_MODEL, 3), (pm1, MEM_WIDTH, 8)], [mqw1, mk, mv],
                     [(D_MODEL + MEM_WIDTH, BF16)], [], tm=256, name="fox_out_fwd")[0]
    w1_1, w2_1 = ride_mlp_1.results
    x3, h2_1 = matmul(cat1, w_out1, post=_add, post_ins=[x2], row_ins=[n2w1], extra_out=(_norm_rows, BF16),
                      tiles=(None, D_MODEL, None), name="wout_fwd_1")
    y, mlp_res1 = _mlp_fwd(x3, h2_1, w1_1, w2_1, 1)

    def loss_fn(y, tgt):
        e = y - tgt
        return e * (1.0 / D_MODEL), jnp.sum(jnp.sum(e * e, axis=1, keepdims=True), axis=0, keepdims=True)
    dy, sq = rows_call(loss_fn, [y, target], [], [(D_MODEL, F32)], [(1, 1)], tm=512, name="loss")
    loss = sq[0, 0] * (0.5 / D_MODEL)

    dx3, dw1_1, dw2_1, dn2w1, sib1, sib2 = _mlp_bwd(dy, mlp_res1, n2w1, w1_1, w2_1, 1)
    dcat1 = matmul(dx3, w_out1, tb=True, name="wout_dx_1", riders=[sib1])
    dwo_1 = matmul(cat1, dx3, ta=True, out_dtype=BF16, name="wout_dw_1").reshape(N_DEV, OUT_IN // N_DEV, D_MODEL)
    sibo = sibling_rider([dwo_1])
    do1, dgate1, dqm1, dmqw1, dmk1, dmv1 = rows_call(
        functools.partial(vjp_rows(fox_out_fn, 3, (True, True, True)), n_row=3, n_ct=1),
        [o1, (pm1, D_MODEL, 3), (pm1, MEM_WIDTH, 8), dcat1], [mqw1, mk, mv],
        [(D_MODEL, F32), (D_MODEL, BF16), (MEM_WIDTH, BF16)], [(1, HEAD_DIM), (n_mem, MEM_WIDTH), (n_mem, MEM_WIDTH)],
        tm=256, name="fox_out_bwd", riders=[sibo])
    ride_l1 = chips_rider(chip_sums(["w_mlp2_1", "w_mlp1_1", "w_out_1"], [dw2_1, dw1_1, dwo_1],
                                    sib2.results + sib1.results + sibo.results))
    dq1, dk1, dv1, dfq, dfk = fox_attn_bwd(q1, k1, pm1, fq, fk, do1, name="fox_attn_bwd", riders=[ride_l1])
    dqraw1, dkraw1, dqnw, dknw = rows_call(
        functools.partial(vjp_rows(fox_qk_fn, 2, (True, True)), n_row=2, n_ct=2),
        [(pm1, D_MODEL, 0), (pm1, D_MODEL, 1), dq1, dk1], [qnw, knw],
        [(D_MODEL, BF16)] * 2, [(1, HEAD_DIM)] * 2, tm=256, name="fox_qk_bwd")
    dfcum = dfq + jnp.pad(dfk[:, 0, :].T, ((0, 0), (0, LANES - N_HEADS)))
    dps1, dfbias = rows_call(
        functools.partial(vjp_rows(fox_fcum_fn, 1, (True,)), n_row=1, n_ct=1),
        [ps1, dfcum], [fbias], [(LANES, F32)], [(1, LANES)], tm=t, name="fox_fcum_bwd")
    dpm1 = jnp.concatenate([dqraw1, dkraw1, dv1, dgate1, dqm1], axis=1)
    dx2, dn1w1, dwmain1, dwsmall1 = _in_proj_bwd(h1, dpm1, dps1, fox_main, fox_f, x2, dx3, n1w1, "1")
    g_fox = in_proj_pieces(dwmain1, dwsmall1, N_HEADS, FOX_IN)
    sibf = sibling_rider([g_fox])

    dx1, dw1_0, dw2_0, dn2w0, sib1, sib2 = _mlp_bwd(dx2, mlp_res0, n2w0, w1_0, w2_0, 0, riders=[sibf])
    ride_fox_g = chips_rider(chip_sums(["fox_w_in"], [g_fox], sibf.results))
    dcat0 = matmul(dx1, w_out0, tb=True, name="wout_dx_0", riders=[sib1])
    dwo_0 = matmul(cat0, dx1, ta=True, out_dtype=BF16, name="wout_dw_0").reshape(N_DEV, OUT_IN // N_DEV, D_MODEL)
    sibo = sibling_rider([dwo_0])
    do0, dz0, dqm0, donw, dmqw0, dmk0, dmv0 = rows_call(
        functools.partial(vjp_rows(dn_out_fn, 3, (True, True, True, True)), n_row=3, n_ct=1),
        [o0, (pm0, D_MODEL, 3), (pm0, MEM_WIDTH, 8), dcat0], [onw, mqw0, mk, mv],
        [((N_HEADS, HEAD_DIM), F32), (D_MODEL, BF16), (MEM_WIDTH, BF16)],
        [(1, HEAD_DIM), (1, HEAD_DIM), (n_mem, MEM_WIDTH), (n_mem, MEM_WIDTH)], tm=256, name="dn_out_bwd", riders=[sibo])
    h_l0 = chip_sums(["w_mlp2_0", "w_mlp1_0", "w_out_0"], [dw2_0, dw1_0, dwo_0], sib2.results + sib1.results + sibo.results)
    ride_l0_mlp2, ride_l0_rest = chips_rider(h_l0[:1]), chips_rider(h_l0[1:])
    dmnw, dwkv, dmknw = memkv_bwd(mem, mnw, w_kv, mknw, dmk0 + dmk1, dmv0 + dmv1, name="memkv_bwd")
    g_kv = dwkv.reshape(N_DEV, D_MODEL // N_DEV, D_MODEL)
    sibk = sibling_rider([g_kv])
    dq_s, dk_s, dg_s, du0, dw0, dqk0 = delta_seq_bwd(q0, k0, gates, u0, w0, qk0, s_start, do0, name="delta_seq_bwd",
                                                     riders=[ride_fox_g, sibk])
    ride_kv_g = chips_rider(chip_sums(["w_mem_kv"], [g_kv], sibk.results))
    dq0, dk0, dv0, dgates = delta_intra_bwd(q0, k0, v0, gates, du0, dw0, dqk0, dq_s, dk_s, dg_s,
                                            name="delta_intra_bwd", riders=[ride_l0_mlp2, ride_kv_g])
    dxq, dxk, dxv, dcq, dck, dcv = dn_prep_bwd(pm0, conv_w, dq0, dk0, dv0, name="dn_prep_bwd", riders=[ride_l0_rest])
    dconv = jnp.concatenate([dcq, dck, dcv], axis=1)
    dps0, dalog, ddtb = rows_call(
        functools.partial(vjp_rows(dn_gates_fn, 1, (True, True)), n_row=1, n_ct=1),
        [ps0, dgates], [alog, dtb], [(LANES, F32)], [(1, LANES)] * 2, tm=512, name="dn_gates_bwd")
    dpm0 = jnp.concatenate([dxq, dxk, dxv, dz0, dqm0], axis=1)
    grad_x, dn1w0, dwmain0, dwsmall0 = _in_proj_bwd(h0, dpm0, dps0, dn_main, dn_ab, x0, dx1, n1w0, "0")
    g_dn = in_proj_pieces(dwmain0, dwsmall0, 2 * N_HEADS, DN_IN)
    g_conv = dconv.reshape(CONV_WIDTH, N_DEV, -1).transpose(1, 0, 2).astype(BF16)

    g["mem_norm_w"] = dmnw[0]
    g["mem_k_norm_w"] = dmknw[0]
    g["norm1_w"] = jnp.concatenate([dn1w0, dn1w1], axis=0)
    g["dn_a_log"] = dalog[:, :N_HEADS]
    g["dn_dt_bias"] = ddtb[:, :N_HEADS]
    g["dn_o_norm_w"] = donw
    g["fox_f_bias"] = dfbias[:, :N_HEADS]
    g["fox_q_norm_w"] = dqnw
    g["fox_k_norm_w"] = dknw
    g["memq_norm_w"] = jnp.concatenate([dmqw0, dmqw1], axis=0)
    g["norm2_w"] = jnp.concatenate([dn2w0, dn2w1], axis=0)

    sibd = sibling_rider([g_dn, g_conv])
    run_riders([sibd], name="grads_to_sibling_last")
    ride_last = chips_rider(chip_sums(["dn_w_in", "dn_conv_w"], [g_dn, g_conv], sibd.results))
    ride_small = gather_rider([pack_small(g, last=loss)])
    run_riders([ride_last, ride_small], name="grads_to_chips_last")

    def layers(l0, l1):
        return jnp.stack([l0, l1], axis=1).reshape(4, -1, l0.shape[-1])

    parts = {
        "w_mlp1": layers(ride_l0_rest.results[0], ride_l1.results[1]),
        "w_mlp2": layers(ride_l0_mlp2.results[0], ride_l1.results[0]),
        "w_out": layers(ride_l0_rest.results[1], ride_l1.results[2]),
        "fox_w_in": ride_fox_g.results[0], "w_mem_kv": ride_kv_g.results[0],
        "dn_w_in": ride_last.results[0], "dn_conv_w": ride_last.results[1],
    }
    out = {n: adamw(parts[n], w[n], m[n], v[n], name=f"adamw_{n}") for n, _, _ in BIG}
    small, loss = adamw_small(ride_small.results[0], w, m, v, name="adamw_small")
    return loss, grad_x, out, small


WEIGHTS = ["mem_norm_w", "w_mem_kv", "mem_k_norm_w", "norm1_w", "dn_w_in", "dn_conv_w", "dn_a_log", "dn_dt_bias",
           "dn_o_norm_w", "fox_w_in", "fox_f_bias", "fox_q_norm_w", "fox_k_norm_w", "memq_norm_w", "w_out", "norm2_w",
           "w_mlp1", "w_mlp2"]
DN_IN = 4 * D_MODEL + 2 * N_HEADS + MEM_WIDTH
FOX_IN = 4 * D_MODEL + N_HEADS + MEM_WIDTH
GATE_END = 4 * D_MODEL
OUT_IN = D_MODEL + MEM_WIDTH
BIG = [("w_mem_kv", D_MODEL // N_DEV, D_MODEL), ("dn_w_in", D_MODEL, DN_IN // N_DEV), ("fox_w_in", D_MODEL, FOX_IN // N_DEV),
       ("dn_conv_w", CONV_WIDTH, 3 * D_MODEL // N_DEV), ("w_out", 2 * OUT_IN // N_DEV, D_MODEL),
       ("w_mlp1", 2 * D_MODEL, FF_PIECE), ("w_mlp2", 2 * FF_PIECE, D_MODEL)]
SMALL_TILE = 8 * LANES
SMALL = [(name, shape, -(-math.prod(shape) // SMALL_TILE) * SMALL_TILE) for name, shape in [
    ("mem_norm_w", (D_MODEL,)), ("mem_k_norm_w", (HEAD_DIM,)), ("norm1_w", (2, D_MODEL)), ("dn_a_log", (1, N_HEADS)),
    ("dn_dt_bias", (1, N_HEADS)), ("dn_o_norm_w", (1, HEAD_DIM)), ("fox_f_bias", (1, N_HEADS)),
    ("fox_q_norm_w", (1, HEAD_DIM)), ("fox_k_norm_w", (1, HEAD_DIM)), ("memq_norm_w", (2, HEAD_DIM)), ("norm2_w", (2, D_MODEL))]]
SMALL_ROWS = sum(ln for _, _, ln in SMALL) // LANES + 8


def pack_small(p, last=None):
    def rows(a, ln):
        a = a.reshape(-1)
        return (a if a.shape[0] == ln else jnp.pad(a, (0, ln - a.shape[0]))).reshape(-1, LANES)

    used = sum(ln for _, _, ln in SMALL) // LANES
    tail = jnp.zeros(((SMALL_ROWS - used) * LANES,), F32)
    if last is not None:
        tail = jnp.concatenate([tail[:-1], last.reshape(1)])
    return jnp.concatenate([rows(p[n], ln) for n, _, ln in SMALL] + [tail.reshape(-1, LANES)], axis=0)


def in_proj_weights(gathered, width, n_small):
    full = gathered.transpose(1, 0, 2).reshape(D_MODEL, width)
    main = jnp.concatenate([full[:, :GATE_END], full[:, GATE_END + n_small:]], axis=1)
    return main, jnp.pad(full[:, GATE_END:GATE_END + n_small], ((0, 0), (0, LANES - n_small)))


def in_proj_pieces(d_main, d_small, n_small, width):
    full = jnp.concatenate([d_main[:, :GATE_END], d_small[:, :n_small], d_main[:, GATE_END:]], axis=1)
    return full.reshape(D_MODEL, N_DEV, width // N_DEV).transpose(1, 0, 2)


def _adamw_update(g, w, m, v):
    m_new = ADAM_B1 * m + (1.0 - ADAM_B1) * g
    v_new = ADAM_B2 * v + (1.0 - ADAM_B2) * jnp.square(g)
    m_hat = m_new / (1.0 - ADAM_B1 ** ADAM_STEP)
    v_hat = v_new / (1.0 - ADAM_B2 ** ADAM_STEP)
    return -ADAM_LR * (m_hat / (jnp.sqrt(v_hat) + ADAM_EPS) + ADAM_WD * w), m_new, v_new


def adamw(parts, w, m, v, *, name):
    n, _, cols = parts.shape
    layers = w.shape[0] if w.ndim == 3 else 1
    rows = w.shape[-2]
    tile = _pick(rows, (512, 256, 128))
    steps = rows // tile

    def body(p_ref, w_ref, m_ref, v_ref, g_ref, d_ref, mo_ref, vo_ref):
        g = p_ref[0].astype(F32)
        for i in range(1, n):
            g = g + p_ref[i].astype(F32)
        g_ref[...] = g
        d_ref[...], mo_ref[...], vo_ref[...] = _adamw_update(g, w_ref[...], m_ref[...], v_ref[...])

    if w.ndim == 3:
        spec = pl.BlockSpec((None, tile, cols), lambda l, i: (l, i, 0))
    else:
        spec = pl.BlockSpec((tile, cols), lambda l, i: (i, 0))
    return pl.pallas_call(
        body, name=name, grid=(layers, steps),
        in_specs=[pl.BlockSpec((n, tile, cols), lambda l, i: (0, l * steps + i, 0)), spec, spec, spec], out_specs=[spec] * 4,
        out_shape=[jax.ShapeDtypeStruct(w.shape, F32)] * 4, compiler_params=_params(("parallel", "parallel")),
    )(parts, w, m, v)


def adamw_small(parts, w, m, v, *, name):
    def view(a):
        return a.reshape(-1, LANES) if a.size % LANES == 0 else a.reshape(1, a.size)

    k = len(SMALL)
    ins = [view(d[n]) for d in (w, m, v) for n, _, _ in SMALL]

    def body(p_ref, *refs):
        w_refs, m_refs, v_refs, outs, g_ref = refs[:k], refs[k:2 * k], refs[2 * k:3 * k], refs[3 * k:-1], refs[-1]
        g_all = p_ref[0]
        for i in range(1, N_DEV):
            g_all = g_all + p_ref[i]
        g_ref[...] = g_all
        row = 0
        for i, (_, _, ln) in enumerate(SMALL):
            r, c = w_refs[i].shape
            g = g_ref[row:row + r, 0:c]
            outs[4 * i][...] = g
            outs[4 * i + 1][...], outs[4 * i + 2][...], outs[4 * i + 3][...] = _adamw_update(
                g, w_refs[i][...], m_refs[i][...], v_refs[i][...])
            row += ln // LANES
        outs[-1][...] = g_ref[SMALL_ROWS - 1:SMALL_ROWS, LANES - 1:LANES]

    out_shape = [jax.ShapeDtypeStruct(a.shape, F32) for a in ins[:k] for _ in range(4)] + [jax.ShapeDtypeStruct((1, 1), F32)]
    res = pl.pallas_call(body, name=name, out_shape=out_shape,
                         scratch_shapes=[pltpu.VMEM((SMALL_ROWS, LANES), F32)])(parts, *ins)
    small = {n: [o.reshape(sh) for o in res[4 * i:4 * i + 4]] for i, (n, sh, _) in enumerate(SMALL)}
    return small, res[-1][0, 0]


def kernel(x, mem, mem_norm_w, w_mem_kv, mem_k_norm_w, norm1_w, dn_w_in, dn_conv_w, dn_a_log, dn_dt_bias, dn_o_norm_w, fox_w_in, fox_f_bias, fox_q_norm_w, fox_k_norm_w, memq_norm_w, w_out, norm2_w, w_mlp1, w_mlp2, loss_target, m_mem_norm_w, m_w_mem_kv, m_mem_k_norm_w, m_norm1_w, m_dn_w_in, m_dn_conv_w, m_dn_a_log, m_dn_dt_bias, m_dn_o_norm_w, m_fox_w_in, m_fox_f_bias, m_fox_q_norm_w, m_fox_k_norm_w, m_memq_norm_w, m_w_out, m_norm2_w, m_w_mlp1, m_w_mlp2, v_mem_norm_w, v_w_mem_kv, v_mem_k_norm_w, v_norm1_w, v_dn_w_in, v_dn_conv_w, v_dn_a_log, v_dn_dt_bias, v_dn_o_norm_w, v_fox_w_in, v_fox_f_bias, v_fox_q_norm_w, v_fox_k_norm_w, v_memq_norm_w, v_w_out, v_norm2_w, v_w_mlp1, v_w_mlp2):
    p = dict(mem_norm_w=mem_norm_w, w_mem_kv=w_mem_kv, mem_k_norm_w=mem_k_norm_w, norm1_w=norm1_w, dn_w_in=dn_w_in,
             dn_conv_w=dn_conv_w, dn_a_log=dn_a_log, dn_dt_bias=dn_dt_bias, dn_o_norm_w=dn_o_norm_w, fox_w_in=fox_w_in,
             fox_f_bias=fox_f_bias, fox_q_norm_w=fox_q_norm_w, fox_k_norm_w=fox_k_norm_w, memq_norm_w=memq_norm_w,
             w_out=w_out, norm2_w=norm2_w, w_mlp1=w_mlp1, w_mlp2=w_mlp2)
    pm = dict(mem_norm_w=m_mem_norm_w, w_mem_kv=m_w_mem_kv, mem_k_norm_w=m_mem_k_norm_w, norm1_w=m_norm1_w,
              dn_w_in=m_dn_w_in, dn_conv_w=m_dn_conv_w, dn_a_log=m_dn_a_log, dn_dt_bias=m_dn_dt_bias,
              dn_o_norm_w=m_dn_o_norm_w, fox_w_in=m_fox_w_in, fox_f_bias=m_fox_f_bias, fox_q_norm_w=m_fox_q_norm_w,
              fox_k_norm_w=m_fox_k_norm_w, memq_norm_w=m_memq_norm_w, w_out=m_w_out, norm2_w=m_norm2_w, w_mlp1=m_w_mlp1,
              w_mlp2=m_w_mlp2)
    pv = dict(mem_norm_w=v_mem_norm_w, w_mem_kv=v_w_mem_kv, mem_k_norm_w=v_mem_k_norm_w, norm1_w=v_norm1_w,
              dn_w_in=v_dn_w_in, dn_conv_w=v_dn_conv_w, dn_a_log=v_dn_a_log, dn_dt_bias=v_dn_dt_bias,
              dn_o_norm_w=v_dn_o_norm_w, fox_w_in=v_fox_w_in, fox_f_bias=v_fox_f_bias, fox_q_norm_w=v_fox_q_norm_w,
              fox_k_norm_w=v_fox_k_norm_w, memq_norm_w=v_memq_norm_w, w_out=v_w_out, norm2_w=v_norm2_w, w_mlp1=v_w_mlp1,
              w_mlp2=v_w_mlp2)

    loss, grad_x, results, small = local_step(x[0], mem[0], loss_target[0], p, pm, pv)
    groups = [{n: r[i] for n, r in {**small, **results}.items()} for i in range(4)]
    return (loss, grad_x[None], *[grp[n] for grp in groups for n in WEIGHTS])
```

```python
import functools
import math

import jax
import jax.numpy as jnp
from jax import lax
from jax.experimental import pallas as pl
from jax.experimental.pallas import tpu as pltpu

F32 = jnp.float32
BF16 = jnp.bfloat16
HIGHEST = lax.Precision.HIGHEST

D_MODEL = 1024
HEAD_DIM = 128
N_HEADS = 8
MEM_HEADS = 4
MEM_WIDTH = MEM_HEADS * HEAD_DIM
D_FF = 4 * D_MODEL
CONV_WIDTH = 4
CHUNK = 64
Q_BLOCK = 128
EPS = 1e-6
SCALE = HEAD_DIM ** -0.5
MAIN_WIDTH = 4 * D_MODEL + MEM_WIDTH
LANES = 128
N_DEV = 8

ADAM_LR = 0.001
ADAM_B1 = 0.9
ADAM_B2 = 0.999
ADAM_EPS = 1e-08
ADAM_WD = 0.01
ADAM_STEP = 10

VMEM_LIMIT = 56 * 2 ** 20
MESH = pl.DeviceIdType.MESH


def _bdot(a, b, dims):
    return lax.dot_general(a.astype(BF16), b.astype(BF16), (dims, ((), ())), preferred_element_type=F32)


@jax.custom_vjp
def mm(a, b):
    return _bdot(a, b, ((1,), (0,)))


@jax.custom_vjp
def mm_nt(a, b):
    return _bdot(a, b, ((1,), (1,)))


@jax.custom_vjp
def mm_tn(a, b):
    return _bdot(a, b, ((0,), (0,)))


mm.defvjp(lambda a, b: (mm(a, b), (a, b)), lambda r, g: (mm_nt(g, r[1]), mm_tn(r[0], g)))
mm_nt.defvjp(lambda a, b: (mm_nt(a, b), (a, b)), lambda r, g: (mm(g, r[1]), mm_tn(g, r[0])))
mm_tn.defvjp(lambda a, b: (mm_tn(a, b), (a, b)), lambda r, g: (mm_nt(r[1], g), mm(r[0], g)))


def hdot(a, b):
    return jnp.dot(a, b, precision=HIGHEST, preferred_element_type=F32)


def rms(x, w):
    return x * lax.rsqrt(jnp.mean(x * x, axis=-1, keepdims=True) + EPS) * w


def l2n(x):
    return x * lax.rsqrt(jnp.sum(x * x, axis=-1, keepdims=True) + EPS)


def _iota2(n, m):
    return lax.broadcasted_iota(jnp.int32, (n, m), 0), lax.broadcasted_iota(jnp.int32, (n, m), 1)


def _lower_ones(n):
    r, c = _iota2(n, n)
    return jnp.where(r >= c, 1.0, 0.0).astype(F32)


def _last_row(x):
    r = lax.broadcasted_iota(jnp.int32, x.shape, 0)
    return jnp.sum(jnp.where(r == x.shape[0] - 1, x, 0.0), axis=0, keepdims=True)


def _softmax_rows(z):
    m = lax.stop_gradient(jnp.max(z, axis=-1, keepdims=True))
    e = jnp.exp(z - m)
    return e * (1.0 / jnp.sum(e, axis=-1, keepdims=True))


_BNN = (((2,), (1,)), ((0,), (0,)))
_BNT = (((2,), (2,)), ((0,), (0,)))
_BTN = (((1,), (1,)), ((0,), (0,)))


def _bbdot(a, b, dims):
    return lax.dot_general(a.astype(BF16), b.astype(BF16), dims, preferred_element_type=F32)


@jax.custom_vjp
def bmm(a, b):
    return _bbdot(a, b, _BNN)


@jax.custom_vjp
def bmm_nt(a, b):
    return _bbdot(a, b, _BNT)


@jax.custom_vjp
def bmm_tn(a, b):
    return _bbdot(a, b, _BTN)


@jax.custom_vjp
def bmm_high(a, b):
    return lax.dot_general(a, b, _BNN, precision=lax.Precision.HIGH, preferred_element_type=F32)


bmm.defvjp(lambda a, b: (bmm(a, b), (a, b)), lambda r, g: (bmm_nt(g, r[1]), bmm_tn(r[0], g)))
bmm_nt.defvjp(lambda a, b: (bmm_nt(a, b), (a, b)), lambda r, g: (bmm(g, r[1]), bmm_tn(g, r[0])))
bmm_tn.defvjp(lambda a, b: (bmm_tn(a, b), (a, b)), lambda r, g: (bmm_nt(r[1], g), bmm(r[0], g)))
bmm_high.defvjp(lambda a, b: (bmm_high(a, b), (a, b)), lambda r, g: (bmm_nt(g, r[1]), bmm_tn(r[0], g)))

NEUMANN_HIGH_LEVELS = 2


@jax.custom_vjp
def inv_unit_lower(a):
    n = a.shape[-1]
    r, c = _iota2(n, n)
    p = jnp.where(r == c, 1.0, 0.0).astype(F32) - a
    ak = a
    for level in range(int(math.log2(n)) - 1):
        dot = bmm_high if level < NEUMANN_HIGH_LEVELS else bmm
        ak = dot(ak, ak)
        p = p + dot(p, ak)
    return p


def _inv_unit_lower_fwd(a):
    t = inv_unit_lower(a)
    return t, t


def _inv_unit_lower_bwd(t, g):
    return (-bmm_tn(t, bmm_nt(g, t)),)


inv_unit_lower.defvjp(_inv_unit_lower_fwd, _inv_unit_lower_bwd)


def delta_intra(q, k, v, gc, beta):
    b, c, _ = q.shape
    r, cc = _iota2(c, c)
    causal = r >= cc
    strict = r > cc
    gi = jnp.broadcast_to(gc, (b, c, c))
    gj = jnp.swapaxes(gi, 1, 2)
    decay = jnp.where(causal, jnp.exp(jnp.where(causal, gi - gj, 0.0)), 0.0)
    kb = k * beta
    a = jnp.where(strict, bmm_nt(kb, k) * decay, 0.0)
    t = inv_unit_lower(a)
    u = bmm(t, v * beta)
    w = bmm(t, kb * jnp.exp(gc))
    qk = jnp.where(causal, bmm_nt(q, k) * decay, 0.0)
    return u, w, qk


def delta_step(s, q, k, gc, u, w, qk):
    v_new = u - bmm(w, s)
    out = bmm(q * jnp.exp(gc), s) + bmm(qk, v_new)
    r = lax.broadcasted_iota(jnp.int32, gc.shape, 1)
    g_last = jnp.sum(jnp.where(r == gc.shape[1] - 1, gc, 0.0), axis=1, keepdims=True)
    k_dec = k * jnp.exp(g_last - gc)
    s_new = s * jnp.exp(g_last) + bmm_tn(k_dec, v_new)
    return out, s_new


def fox_probs(q, k, fq, fk, qpos0):
    s = lax.dot_general(q, k, (((1,), (1,)), ((), ())), preferred_element_type=F32)
    r, c = _iota2(s.shape[0], s.shape[1])
    return _softmax_rows(jnp.where(c <= (r + qpos0), s + (fq - fk), -jnp.inf))


def mem_head(qm, wq, mk, mv):
    p = _softmax_rows(mm_nt(rms(qm, wq) * SCALE, mk))
    return mm(p, mv)


def _heads(x, n):
    return [x[:, h * HEAD_DIM:(h + 1) * HEAD_DIM] for h in range(n)]


def memkv_fn(mem, mnw, wkv, mknw):
    kv = mm(rms(mem, mnw), wkv)
    mk = jnp.concatenate([rms(kh, mknw) for kh in _heads(kv[:, :MEM_WIDTH], MEM_HEADS)], axis=1)
    return mk, kv[:, MEM_WIDTH:]


def dn_gates_fn(ab, alog, dtb):
    g = -jnp.exp(alog) * jax.nn.softplus(ab + dtb)
    low = _lower_ones(CHUNK)
    gc = jnp.concatenate([hdot(low, g[i * CHUNK:(i + 1) * CHUNK]) for i in range(ab.shape[0] // CHUNK)], axis=0)
    lane = lax.broadcasted_iota(jnp.int32, ab.shape, 1)
    return jnp.where(lane < N_HEADS, gc, jax.nn.sigmoid(ab))


def fox_fcum_fn(fp, fbias):
    lf = jax.nn.log_sigmoid(fp + fbias)
    low = _lower_ones(LANES)
    carry = jnp.zeros((1, fp.shape[1]), F32)
    outs = []
    for i in range(fp.shape[0] // LANES):
        cs = hdot(low, lf[i * LANES:(i + 1) * LANES]) + carry
        carry = _last_row(cs)
        outs.append(cs)
    return jnp.concatenate(outs, axis=0)


def fox_qk_fn(qraw, kraw, qnw, knw):
    q = jnp.concatenate([rms(x, qnw) * SCALE for x in _heads(qraw, N_HEADS)], axis=1)
    k = jnp.concatenate([rms(x, knw) for x in _heads(kraw, N_HEADS)], axis=1)
    return q, k


def _mem_out(qm, mqw, mk, mv):
    return [mem_head(a, mqw, b, c) for a, b, c in zip(_heads(qm, MEM_HEADS), _heads(mk, MEM_HEADS), _heads(mv, MEM_HEADS))]


def dn_out_fn(o, z, qm, onw, mqw, mk, mv):
    mix = [rms(a, onw) * jax.nn.silu(b) for a, b in zip(o, _heads(z, N_HEADS))]
    return jnp.concatenate(mix + _mem_out(qm, mqw, mk, mv), axis=1)


def fox_out_fn(o, gate, qm, mqw, mk, mv):
    return jnp.concatenate([o * jax.nn.sigmoid(gate)] + _mem_out(qm, mqw, mk, mv), axis=1)


_HBM = pl.BlockSpec(memory_space=pltpu.HBM)


def _place():
    return lax.axis_index("x"), lax.axis_index("y"), lax.axis_index("c")


class Rider:
    def __init__(self, ins, out_shape, scratch, start, finish):
        self.ins, self.out_shape, self.scratch, self.start, self.finish = list(ins), list(out_shape), list(scratch), start, finish
        self.results = None


def gather_rider(xs):
    n = len(xs)

    def plan(x_refs, out_refs, sems):
        send_sems, recv_sems, local_sems = sems
        x, y, c = _place()
        me, sibling = (x, y, c), (x, y, 1 - c)
        chips = [(1 - x, y), (x, 1 - y), (1 - x, 1 - y)]

        def copy(a, k, block, to, src=None):
            px, py, pc = block
            dst = out_refs[a].at[4 * px + 2 * py + pc]
            return pltpu.make_async_remote_copy(
                src_ref=dst if src is None else src, dst_ref=dst,
                send_sem=send_sems.at[a, k], recv_sem=recv_sems.at[a, k], device_id=to, device_id_type=MESH)

        mine = [pltpu.make_async_copy(x_refs[a], out_refs[a].at[4 * x + 2 * y + c], local_sems.at[a]) for a in range(n)]
        first = [copy(a, 0, me, sibling, src=x_refs[a]) for a in range(n)]
        first += [copy(a, 1 + j, me, (*chip, c), src=x_refs[a]) for j, chip in enumerate(chips) for a in range(n)]
        return copy, me, sibling, chips, mine, first

    def start(x_refs, out_refs, sems):
        _, _, _, _, mine, first = plan(x_refs, out_refs, sems)
        for cp in mine + first:
            cp.start()

    def finish(x_refs, out_refs, sems):
        copy, me, sibling, chips, mine, first = plan(x_refs, out_refs, sems)
        _, _, c = me
        passed = []
        for j, chip in enumerate(chips):
            for a in range(n):
                copy(a, 1 + j, (*chip, c), me).wait_recv()
                passed.append(copy(a, 4 + j, (*chip, c), sibling))
                passed[-1].start()
        for a in range(n):
            copy(a, 0, sibling, me).wait_recv()
        for j, chip in enumerate(chips):
            for a in range(n):
                copy(a, 4 + j, (*chip, 1 - c), me).wait_recv()
        for cp in first + passed:
            cp.wait_send()
        for cp in mine:
            cp.wait()

    return Rider(xs, [jax.ShapeDtypeStruct((N_DEV,) + a.shape, a.dtype) for a in xs],
                 [pltpu.SemaphoreType.DMA((n, 7)), pltpu.SemaphoreType.DMA((n, 7)), pltpu.SemaphoreType.DMA((n,))], start, finish)


def sibling_rider(gs):
    n = len(gs)

    def plan(g_refs, out_refs, sems):
        send_sems, recv_sems = sems
        x, y, c = _place()
        return [pltpu.make_async_remote_copy(
            src_ref=g_refs[a].at[2 * k + 1 - c], dst_ref=out_refs[a].at[k], send_sem=send_sems.at[a, k],
            recv_sem=recv_sems.at[a, k], device_id=(x, y, 1 - c), device_id_type=MESH) for a in range(n) for k in range(4)]

    def start(g_refs, out_refs, sems):
        for cp in plan(g_refs, out_refs, sems):
            cp.start()

    def finish(g_refs, out_refs, sems):
        copies = plan(g_refs, out_refs, sems)
        for cp in copies:
            cp.wait_recv()
        for cp in copies:
            cp.wait_send()

    return Rider(gs, [jax.ShapeDtypeStruct((4,) + g.shape[1:], g.dtype) for g in gs],
                 [pltpu.SemaphoreType.DMA((n, 4)), pltpu.SemaphoreType.DMA((n, 4))], start, finish)


def chips_rider(hs):
    n = len(hs)

    def plan(h_refs, out_refs, sems):
        send_sems, recv_sems, local_sems = sems
        x, y, c = _place()
        mine = 2 * x + y
        chips = [(1 - x, y), (x, 1 - y), (1 - x, 1 - y)]
        keep = [pltpu.make_async_copy(h_refs[a].at[mine], out_refs[a].at[mine], local_sems.at[a]) for a in range(n)]
        sends = [pltpu.make_async_remote_copy(
            src_ref=h_refs[a].at[2 * qx + qy], dst_ref=out_refs[a].at[mine], send_sem=send_sems.at[a, j],
            recv_sem=recv_sems.at[a, j], device_id=(qx, qy, c), device_id_type=MESH)
            for j, (qx, qy) in enumerate(chips) for a in range(n)]
        recvs = [pltpu.make_async_remote_copy(
            src_ref=h_refs[a].at[mine], dst_ref=out_refs[a].at[2 * qx + qy], send_sem=send_sems.at[a, j],
            recv_sem=recv_sems.at[a, j], device_id=(qx, qy, c), device_id_type=MESH)
            for j, (qx, qy) in enumerate(chips) for a in range(n)]
        return keep, sends, recvs

    def start(h_refs, out_refs, sems):
        keep, sends, _ = plan(h_refs, out_refs, sems)
        for cp in keep + sends:
            cp.start()

    def finish(h_refs, out_refs, sems):
        keep, sends, recvs = plan(h_refs, out_refs, sems)
        for cp in recvs:
            cp.wait_recv()
        for cp in sends:
            cp.wait_send()
        for cp in keep:
            cp.wait()

    return Rider(hs, [jax.ShapeDtypeStruct(h.shape, h.dtype) for h in hs],
                 [pltpu.SemaphoreType.DMA((n, 3)), pltpu.SemaphoreType.DMA((n, 3)), pltpu.SemaphoreType.DMA((n,))], start, finish)


def hosted_call(riders, body, *, out_shape, in_specs, out_specs, grid=(), scratch_shapes=(), **kw):
    riders = tuple(riders or ())
    if not riders:
        return pl.pallas_call(body, out_shape=out_shape, in_specs=in_specs, out_specs=out_specs, grid=grid,
                              scratch_shapes=scratch_shapes, **kw)
    single = not isinstance(out_shape, (list, tuple))
    k_out_shape = [out_shape] if single else list(out_shape)
    k_out_specs = [out_specs] if single else list(out_specs)
    n_in, n_out, n_scr = len(in_specs), len(k_out_shape), len(scratch_shapes)
    r_ins = [a for r in riders for a in r.ins]
    r_outs = [s for r in riders for s in r.out_shape]
    r_scr = [s for r in riders for s in r.scratch]

    def full_body(*refs):
        ins = refs[:n_in + len(r_ins)]
        outs = refs[n_in + len(r_ins):n_in + len(r_ins) + n_out + len(r_outs)]
        scr = refs[n_in + len(r_ins) + n_out + len(r_outs):]
        steps = math.prod(grid)
        step = 0
        for d, g in enumerate(grid):
            step = step * g + pl.program_id(d)

        def each(method):
            i0, o0, s0 = n_in, n_out, n_scr
            for r in riders:
                getattr(r, method)(ins[i0:i0 + len(r.ins)], outs[o0:o0 + len(r.out_shape)], scr[s0:s0 + len(r.scratch)])
                i0, o0, s0 = i0 + len(r.ins), o0 + len(r.out_shape), s0 + len(r.scratch)

        if steps == 1:
            each("start")
            body(*ins[:n_in], *outs[:n_out], *scr[:n_scr])
            each("finish")
        else:
            pl.when(step == 0)(lambda: each("start"))
            body(*ins[:n_in], *outs[:n_out], *scr[:n_scr])
            pl.when(step == steps - 1)(lambda: each("finish"))

    call = pl.pallas_call(
        full_body, out_shape=k_out_shape + r_outs, in_specs=list(in_specs) + [_HBM] * len(r_ins),
        out_specs=k_out_specs + [_HBM] * len(r_outs), grid=grid, scratch_shapes=list(scratch_shapes) + r_scr, **kw)

    def run(*args):
        res = call(*args, *r_ins)
        o0 = n_out
        for r in riders:
            r.results = list(res[o0:o0 + len(r.out_shape)])
            o0 += len(r.out_shape)
        return res[0] if single else list(res[:n_out])

    return run


def run_riders(riders, *, name):
    hosted_call(riders, lambda: None, name=name, out_shape=[], in_specs=[], out_specs=[])()
    return [r.results for r in riders]


def _pick(n, cands):
    for c in cands:
        if n % c == 0:
            return c
    return n


def _params(sem):
    return pltpu.CompilerParams(dimension_semantics=sem, vmem_limit_bytes=VMEM_LIMIT)


MATMUL_VMEM_BUDGET = 40 * 2 ** 20


def _matmul_tiles(m, n, k, bytes_a, bytes_b, bytes_mn, fixed):
    fm, fn, fk = fixed if fixed is not None else (None, None, None)

    def options(given, size, cands):
        return [given] if given else ([c for c in cands if size % c == 0] or [size])

    best = None
    for tm in options(fm, m, (2048, 1024, 512, 256, 128)):
        for tn in options(fn, n, (512, 256, 128)):
            for tk in options(fk, k, (2048, 1536, 1024, 512, 256, 128)):
                if 2 * (tm * tk * bytes_a + tk * tn * bytes_b + tm * tn * bytes_mn) + tm * tn * 4 > MATMUL_VMEM_BUDGET:
                    continue
                key = ((m // tm) * (n // tn) * (k // tk), -tk)
                if best is None or key < best[0]:
                    best = (key, (tm, tn, tk))
    assert best is not None, (m, n, k, fixed)
    return best[1]


def matmul(a, b, *, name, ta=False, tb=False, post=None, post_ins=(), row_ins=(), acc=False, extra_out=None,
           out_dtype=F32, tiles=None, b_view=None, out_view=None, riders=()):
    (k, m) = a.shape if ta else a.shape[::-1]
    (kb, n) = b_view[:2] if b_view is not None else (b.shape[::-1] if tb else b.shape)
    assert k == kb, (a.shape, b.shape, ta, tb)
    bytes_mn = sum(p.dtype.itemsize for p in post_ins) + jnp.dtype(out_dtype).itemsize
    bytes_mn += jnp.dtype(extra_out[1]).itemsize if extra_out else 0
    tm, tn, tk = _matmul_tiles(m, n, k, a.dtype.itemsize, b.dtype.itemsize, bytes_mn, tiles)
    assert not acc or tn == n, (name, tn, n)
    nk = k // tk
    dims = ((0,) if ta else (1,), (1,) if tb else (0,))
    n_post, n_row = len(post_ins), len(row_ins)
    n_out = 1 + bool(extra_out) + bool(acc)

    def body(*refs):
        a_ref, b_ref = refs[:2]
        post_refs = refs[2:2 + n_post + n_row]
        o_refs, acc_ref = refs[-1 - n_out:-1], refs[-1]
        first_rows, kk = pl.program_id(0) == 0, pl.program_id(2)

        @pl.when(kk == 0)
        def _():
            acc_ref[...] = jnp.zeros_like(acc_ref)

        b_tile = b_ref[...]
        acc_ref[...] += _bdot(a_ref[...], b_tile.reshape(-1, b_tile.shape[-1]), dims)

        @pl.when(kk == nk - 1)
        def _():
            r = acc_ref[...]
            rows = [p[...] for p in post_refs[n_post:]]
            if post is not None:
                r = post(r, *[p[...] for p in post_refs[:n_post]], *rows)
            if acc:
                r, s = r
                sum_ref = o_refs[-1]

                @pl.when(first_rows)
                def _():
                    sum_ref[...] = s

                @pl.when(jnp.logical_not(first_rows))
                def _():
                    sum_ref[...] += s

            o_refs[0][...] = r.astype(out_dtype)
            if extra_out:
                o_refs[1][...] = extra_out[0](r, *rows).astype(extra_out[1])

    a_spec = pl.BlockSpec((tk, tm), lambda i, j, kk: (kk, i)) if ta else pl.BlockSpec((tm, tk), lambda i, j, kk: (i, kk))
    if b_view is not None:
        b_spec = b_view[2]
    else:
        b_spec = pl.BlockSpec((tn, tk), lambda i, j, kk: (j, kk)) if tb else pl.BlockSpec((tk, tn), lambda i, j, kk: (kk, j))
    mn_spec = pl.BlockSpec((tm, tn), lambda i, j, kk: (i, j))
    row_spec = pl.BlockSpec((1, tn), lambda i, j, kk: (0, j))
    o_shape, o_spec = ((m, n), mn_spec) if out_view is None else out_view
    out_shape = [jax.ShapeDtypeStruct(o_shape, out_dtype)]
    out_specs = [o_spec]
    if extra_out:
        out_shape.append(jax.ShapeDtypeStruct((m, n), extra_out[1]))
        out_specs.append(mn_spec)
    if acc:
        out_shape.append(jax.ShapeDtypeStruct((1, n), F32))
        out_specs.append(row_spec)
    res = hosted_call(
        riders, body, name=name, grid=(m // tm, n // tn, nk),
        in_specs=[a_spec, b_spec] + [mn_spec] * n_post + [row_spec] * n_row, out_specs=out_specs, out_shape=out_shape,
        scratch_shapes=[pltpu.VMEM((tm, tn), F32)],
        compiler_params=_params(("arbitrary" if acc else "parallel", "parallel", "arbitrary")),
    )(a, b, *post_ins, *row_ins)
    return res if n_out > 1 else res[0]


def rows_call(fn, row_ins, full_ins, row_outs, acc_outs, *, tm, name, riders=()):
    row_ins = [r if isinstance(r, tuple) else (r, r.shape[-1], 0) for r in row_ins]
    t = row_ins[0][0].shape[-2]
    tm = min(tm, t)
    n_in = len(row_ins) + len(full_ins)
    n_row = len(row_outs)

    def body(*refs):
        res = fn(*[[r[h] for h in range(r.shape[0])] if (i < len(row_ins) and len(r.shape) == 3) else r[...]
                   for i, r in enumerate(refs[:n_in])])
        res = res if isinstance(res, (tuple, list)) else (res,)
        outs = refs[n_in:]
        for ref, val in zip(outs[:n_row], res[:n_row]):
            if len(ref.shape) == 3:
                for h, vh in enumerate(val):
                    ref[h] = vh.astype(ref.dtype)
            else:
                ref[...] = val.astype(ref.dtype)
        first = pl.program_id(0) == 0
        for ref, val in zip(outs[n_row:], res[n_row:]):
            @pl.when(first)
            def _(ref=ref, val=val):
                ref[...] = val

            @pl.when(jnp.logical_not(first))
            def _(ref=ref, val=val):
                ref[...] += val

    def full_spec(shape):
        return pl.BlockSpec(shape, lambda i, nd=len(shape): (0,) * nd)

    def row_spec(lead, w, cb):
        if lead is None:
            return pl.BlockSpec((tm, w), lambda i: (i, cb))
        return pl.BlockSpec((lead, tm, w), lambda i: (0, i, cb))

    def lead_cols(c):
        return c if isinstance(c, tuple) else (None, c)

    in_specs = [row_spec(a.shape[0] if a.ndim == 3 else None, w, cb) for (a, w, cb) in row_ins]
    in_specs += [full_spec(f.shape) for f in full_ins]
    out_specs = [row_spec(*lead_cols(c), 0) for c, _ in row_outs] + [full_spec(s) for s in acc_outs]
    out_shape = [jax.ShapeDtypeStruct(tuple(d for d in (lead_cols(c)[0], t, lead_cols(c)[1]) if d is not None), dt)
                 for c, dt in row_outs] + [jax.ShapeDtypeStruct(s, F32) for s in acc_outs]
    res = hosted_call(
        riders, body, name=name, grid=(t // tm,), in_specs=in_specs, out_specs=out_specs, out_shape=out_shape,
        compiler_params=_params(("arbitrary",)),
    )(*[r[0] for r in row_ins], *full_ins)
    return res


def vjp_rows(fn, n_diff_row, row_diff_full):
    def bwd(*args, n_row, n_ct):
        prim_rows = args[:n_row]
        cts = args[n_row:n_row + n_ct]
        fulls = args[n_row + n_ct:]
        _, vjp = jax.vjp(fn, *prim_rows, *fulls)
        g = vjp(cts[0] if n_ct == 1 else tuple(cts))
        out = list(g[:n_diff_row])
        out += [gf for gf, d in zip(g[n_row:], row_diff_full) if d]
        return tuple(out)
    return bwd


def _shift_down(x, s):
    if s == 0:
        return x
    t = lax.broadcasted_iota(jnp.int32, x.shape, 0)
    return jnp.where(t >= s, pltpu.roll(x, s, 0), 0.0)


def _shift_up(x, s):
    if s == 0:
        return x
    n = x.shape[0]
    t = lax.broadcasted_iota(jnp.int32, x.shape, 0)
    return jnp.where(t < n - s, pltpu.roll(x, n - s, 0), 0.0)


def _conv(x, w_ref):
    return sum(w_ref[pl.ds(j, 1), :] * _shift_down(x, CONV_WIDTH - 1 - j) for j in range(CONV_WIDTH))


_DN_POST = (lambda c: l2n(jax.nn.silu(c)) * SCALE, lambda c: l2n(jax.nn.silu(c)), jax.nn.silu)


def dn_prep_fwd(proj, conv_w, *, name, riders=()):
    t = proj.shape[0]

    def body(xq, xk, xv, wq, wk, wv, oq, ok, ov):
        for x_ref, w_ref, o_ref, post in zip((xq, xk, xv), (wq, wk, wv), (oq, ok, ov), _DN_POST):
            o_ref[...] = post(_conv(x_ref[...], w_ref))

    x_specs = [pl.BlockSpec((t, HEAD_DIM), lambda h, g=g: (0, g * N_HEADS + h)) for g in range(3)]
    w_specs = [pl.BlockSpec((CONV_WIDTH, HEAD_DIM), lambda h, g=g: (0, g * N_HEADS + h)) for g in range(3)]
    o_spec = pl.BlockSpec((None, t, HEAD_DIM), lambda h: (h, 0, 0))
    return hosted_call(
        riders, body, name=name, grid=(N_HEADS,), in_specs=x_specs + w_specs, out_specs=[o_spec] * 3,
        out_shape=[jax.ShapeDtypeStruct((N_HEADS, t, HEAD_DIM), F32)] * 3, compiler_params=_params(("parallel",)),
    )(proj, proj, proj, conv_w, conv_w, conv_w)


def dn_prep_bwd(proj, conv_w, dq, dk, dv, *, name, riders=()):
    t = proj.shape[0]

    def body(xq, xk, xv, wq, wk, wv, gq, gk, gv, dxq, dxk, dxv, dwq, dwk, dwv):
        for x_ref, w_ref, g_ref, dx_ref, dw_ref, post in zip(
                (xq, xk, xv), (wq, wk, wv), (gq, gk, gv), (dxq, dxk, dxv), (dwq, dwk, dwv), _DN_POST):
            x = x_ref[...]
            _, vjp = jax.vjp(post, _conv(x, w_ref))
            dc, = vjp(g_ref[...])
            dx = sum(w_ref[pl.ds(j, 1), :] * _shift_up(dc, CONV_WIDTH - 1 - j) for j in range(CONV_WIDTH))
            dx_ref[...] = dx.astype(dx_ref.dtype)
            for j in range(CONV_WIDTH):
                dw_ref[pl.ds(j, 1), :] = jnp.sum(dc * _shift_down(x, CONV_WIDTH - 1 - j), axis=0, keepdims=True)

    x_specs = [pl.BlockSpec((t, HEAD_DIM), lambda h, g=g: (0, g * N_HEADS + h)) for g in range(3)]
    w_specs = [pl.BlockSpec((CONV_WIDTH, HEAD_DIM), lambda h, g=g: (0, g * N_HEADS + h)) for g in range(3)]
    g_spec = pl.BlockSpec((None, t, HEAD_DIM), lambda h: (h, 0, 0))
    dx_spec = pl.BlockSpec((t, HEAD_DIM), lambda h: (0, h))
    dw_spec = pl.BlockSpec((CONV_WIDTH, HEAD_DIM), lambda h: (0, h))
    return hosted_call(
        riders, body, name=name, grid=(N_HEADS,), in_specs=x_specs + w_specs + [g_spec] * 3, out_specs=[dx_spec] * 3 + [dw_spec] * 3,
        out_shape=[jax.ShapeDtypeStruct((t, D_MODEL), BF16)] * 3 + [jax.ShapeDtypeStruct((CONV_WIDTH, D_MODEL), F32)] * 3,
        compiler_params=_params(("parallel",)),
    )(proj, proj, proj, conv_w, conv_w, conv_w, dq, dk, dv)


INTRA_CHUNKS = 4


def _lane_column(x, lane_index):
    lane = lax.broadcasted_iota(jnp.int32, x.shape, 1)
    return jnp.sum(jnp.where(lane == lane_index, x, 0.0), axis=1, keepdims=True)


def _head_columns(g, first_lane):
    return jnp.concatenate([_lane_column(g, first_lane + h)[None] for h in range(N_HEADS)], axis=0)


def _intra_of_gates(q, k, v, gates):
    nb = N_HEADS * (gates.shape[0] // CHUNK)

    def chunks(x):
        return x.reshape(nb, CHUNK, x.shape[-1])

    res = delta_intra(chunks(q), chunks(k), chunks(v), chunks(_head_columns(gates, 0)), chunks(_head_columns(gates, N_HEADS)))
    return tuple(x.reshape(N_HEADS, -1, x.shape[-1]) for x in res)


def _step_of_gates(s, q, k, gates, u, w, qk):
    return delta_step(s, q, k, _head_columns(gates, 0), u, w, qk)


def _head_major(rows, w, index):
    return pl.BlockSpec((N_HEADS, rows, w), lambda i: (0, index(i), 0))


def delta_intra_fwd(q, k, v, gates, *, name, riders=()):
    t = q.shape[1]
    rows = min(INTRA_CHUNKS, t // CHUNK) * CHUNK

    def body(q_ref, k_ref, v_ref, g_ref, u_ref, w_ref, qk_ref):
        for ref, val in zip((u_ref, w_ref, qk_ref), _intra_of_gates(q_ref[...], k_ref[...], v_ref[...], g_ref[...])):
            ref[...] = val

    x_spec, qk_spec = (_head_major(rows, w, lambda i: i) for w in (HEAD_DIM, CHUNK))
    g_spec = pl.BlockSpec((rows, LANES), lambda i: (i, 0))
    return hosted_call(
        riders, body, name=name, grid=(t // rows,), in_specs=[x_spec] * 3 + [g_spec], out_specs=[x_spec, x_spec, qk_spec],
        out_shape=[jax.ShapeDtypeStruct((N_HEADS, t, HEAD_DIM), F32)] * 2 + [jax.ShapeDtypeStruct((N_HEADS, t, CHUNK), F32)],
        compiler_params=_params(("parallel",)),
    )(q, k, v, gates)


def delta_seq_fwd(q, k, gates, u, w, qk, *, name, riders=()):
    t = q.shape[1]
    nc = t // CHUNK

    def body(q_ref, k_ref, g_ref, u_ref, w_ref, qk_ref, o_ref, s0_ref, s_ref):
        @pl.when(pl.program_id(0) == 0)
        def _():
            s_ref[...] = jnp.zeros_like(s_ref)

        s = s_ref[...]
        s0_ref[...] = s
        o, s_new = _step_of_gates(s, q_ref[...], k_ref[...], g_ref[...], u_ref[...], w_ref[...], qk_ref[...])
        o_ref[...] = o
        s_ref[...] = s_new

    x_spec, qk_spec = (_head_major(CHUNK, w, lambda c: c) for w in (HEAD_DIM, CHUNK))
    g_spec = pl.BlockSpec((CHUNK, LANES), lambda c: (c, 0))
    s_spec = pl.BlockSpec((N_HEADS, None, HEAD_DIM, HEAD_DIM), lambda c: (0, c, 0, 0))
    return hosted_call(
        riders, body, name=name, grid=(nc,), in_specs=[x_spec, x_spec, g_spec, x_spec, x_spec, qk_spec], out_specs=[x_spec, s_spec],
        out_shape=[jax.ShapeDtypeStruct((N_HEADS, t, HEAD_DIM), F32),
                   jax.ShapeDtypeStruct((N_HEADS, nc, HEAD_DIM, HEAD_DIM), F32)],
        scratch_shapes=[pltpu.VMEM((N_HEADS, HEAD_DIM, HEAD_DIM), F32)],
        compiler_params=_params(("arbitrary",)),
    )(q, k, gates, u, w, qk)


def delta_seq_bwd(q, k, gates, u, w, qk, s0, do, *, name, riders=()):
    t = q.shape[1]
    nc = t // CHUNK

    def body(q_ref, k_ref, g_ref, u_ref, w_ref, qk_ref, s0_ref, do_ref,
             dq_ref, dk_ref, dg_ref, du_ref, dw_ref, dqk_ref, ds_ref):
        @pl.when(pl.program_id(0) == 0)
        def _():
            ds_ref[...] = jnp.zeros_like(ds_ref)

        _, vjp = jax.vjp(_step_of_gates, s0_ref[...], q_ref[...], k_ref[...], g_ref[...], u_ref[...], w_ref[...], qk_ref[...])
        ds, dq, dk, dg, du, dw, dqk = vjp((do_ref[...], ds_ref[...]))
        for ref, val in zip((ds_ref, dq_ref, dk_ref, dg_ref, du_ref, dw_ref, dqk_ref), (ds, dq, dk, dg, du, dw, dqk)):
            ref[...] = val

    x_spec, qk_spec = (_head_major(CHUNK, w, lambda c: nc - 1 - c) for w in (HEAD_DIM, CHUNK))
    g_spec = pl.BlockSpec((CHUNK, LANES), lambda c: (nc - 1 - c, 0))
    s_spec = pl.BlockSpec((N_HEADS, None, HEAD_DIM, HEAD_DIM), lambda c: (0, nc - 1 - c, 0, 0))
    head_shape = [jax.ShapeDtypeStruct((N_HEADS, t, w_), F32) for w_ in (HEAD_DIM, HEAD_DIM, HEAD_DIM, HEAD_DIM, CHUNK)]
    return hosted_call(
        riders, body, name=name, grid=(nc,), in_specs=[x_spec, x_spec, g_spec, x_spec, x_spec, qk_spec, s_spec, x_spec],
        out_specs=[x_spec, x_spec, g_spec, x_spec, x_spec, qk_spec],
        out_shape=head_shape[:2] + [jax.ShapeDtypeStruct((t, LANES), F32)] + head_shape[2:],
        scratch_shapes=[pltpu.VMEM((N_HEADS, HEAD_DIM, HEAD_DIM), F32)],
        compiler_params=_params(("arbitrary",)),
    )(q, k, gates, u, w, qk, s0, do)


def delta_intra_bwd(q, k, v, gates, du, dw, dqk, dq_s, dk_s, dg_s, *, name, riders=()):
    t = q.shape[1]
    rows = min(INTRA_CHUNKS, t // CHUNK) * CHUNK

    def body(q_ref, k_ref, v_ref, g_ref, du_ref, dw_ref, dqk_ref, dqs_ref, dks_ref, dgs_ref, dq_ref, dk_ref, dv_ref, dg_ref):
        _, vjp = jax.vjp(_intra_of_gates, q_ref[...], k_ref[...], v_ref[...], g_ref[...])
        dq, dk, dv, dg = vjp((du_ref[...], dw_ref[...], dqk_ref[...]))
        dq_ref[...] = dq + dqs_ref[...]
        dk_ref[...] = dk + dks_ref[...]
        dv_ref[...] = dv
        dg_ref[...] = dg + dgs_ref[...]

    x_spec, qk_spec = (_head_major(rows, w, lambda i: i) for w in (HEAD_DIM, CHUNK))
    g_spec = pl.BlockSpec((rows, LANES), lambda i: (i, 0))
    return hosted_call(
        riders, body, name=name, grid=(t // rows,),
        in_specs=[x_spec] * 3 + [g_spec, x_spec, x_spec, qk_spec, x_spec, x_spec, g_spec],
        out_specs=[x_spec] * 3 + [g_spec],
        out_shape=[jax.ShapeDtypeStruct((N_HEADS, t, HEAD_DIM), F32)] * 3 + [jax.ShapeDtypeStruct((t, LANES), F32)],
        compiler_params=_params(("parallel",)),
    )(q, k, v, gates, du, dw, dqk, dq_s, dk_s, dg_s)


_V_BLOCK = 2 * N_HEADS
FOX_GROUPS = 16


def _fox_groups(t):
    nq = t // Q_BLOCK
    per = max(1, nq // FOX_GROUPS)
    return [(g0, per, (g0 + per) * Q_BLOCK) for g0 in range(0, nq, per)]


def fox_attn_fwd(q, k, proj, fq, fk, *, name, riders=()):
    t = q.shape[0]

    def body(q_ref, k_ref, v_ref, fq_ref, fk_ref, o_ref, kb_ref, vb_ref):
        head = pl.program_id(0)
        kb_ref[...] = k_ref[...].astype(BF16)
        vb_ref[...] = v_ref[...].astype(BF16)
        for g0, per, keys in _fox_groups(t):
            def block(j, carry, g0=g0, keys=keys):
                rows = pl.ds((g0 + j) * Q_BLOCK, Q_BLOCK)
                p = fox_probs(q_ref[rows, :].astype(BF16), kb_ref[0:keys, :], _lane_column(fq_ref[rows, :], head),
                              fk_ref[:, 0:keys], (g0 + j) * Q_BLOCK)
                o_ref[rows, :] = jnp.dot(p.astype(BF16), vb_ref[0:keys, :], preferred_element_type=F32)
                return carry
            for j in range(per):
                block(j, 0)

    x_spec = pl.BlockSpec((t, HEAD_DIM), lambda h: (0, h))
    v_spec = pl.BlockSpec((t, HEAD_DIM), lambda h: (0, _V_BLOCK + h))
    fq_spec = pl.BlockSpec((t, LANES), lambda h: (0, 0))
    fk_spec = pl.BlockSpec((None, 1, t), lambda h: (h, 0, 0))
    return hosted_call(
        riders, body, name=name, grid=(N_HEADS,), in_specs=[x_spec, x_spec, v_spec, fq_spec, fk_spec], out_specs=x_spec,
        out_shape=jax.ShapeDtypeStruct((t, D_MODEL), F32), scratch_shapes=[pltpu.VMEM((t, HEAD_DIM), BF16)] * 2,
        compiler_params=_params(("parallel",)),
    )(q, k, proj, fq, fk)


def fox_attn_bwd(q, k, proj, fq, fk, do, *, name, riders=()):
    t = q.shape[0]

    def body(q_ref, k_ref, v_ref, fq_ref, fk_ref, do_ref, dq_ref, dk_ref, dv_out_ref, dfq_ref, dfk_ref, kb_ref, vb_ref, dv_ref):
        head = pl.program_id(0)

        @pl.when(head == 0)
        def _():
            dfq_ref[...] = jnp.zeros_like(dfq_ref)

        kb_ref[...] = k_ref[...].astype(BF16)
        vb_ref[...] = v_ref[...].astype(BF16)
        dk_ref[...] = jnp.zeros_like(dk_ref)
        dv_ref[...] = jnp.zeros_like(dv_ref)
        dfk_ref[...] = jnp.zeros_like(dfk_ref)
        nt = (((1,), (1,)), ((), ()))
        tn = (((0,), (0,)), ((), ()))
        for g0, per, keys in _fox_groups(t):
            def block(j, carry, g0=g0, keys=keys):
                rows = pl.ds((g0 + j) * Q_BLOCK, Q_BLOCK)
                qb, dob = q_ref[rows, :].astype(BF16), do_ref[rows, :].astype(BF16)
                kb, vb = kb_ref[0:keys, :], vb_ref[0:keys, :]
                p = fox_probs(qb, kb, _lane_column(fq_ref[rows, :], head), fk_ref[:, 0:keys], (g0 + j) * Q_BLOCK)
                dp = lax.dot_general(dob, vb, nt, preferred_element_type=F32)
                dz = p * (dp - jnp.sum(dp * p, axis=-1, keepdims=True))
                pb, dzb = p.astype(BF16), dz.astype(BF16)
                dq_ref[rows, :] = jnp.dot(dzb, kb, preferred_element_type=F32)
                lane = lax.broadcasted_iota(jnp.int32, (Q_BLOCK, LANES), 1)
                dfq_ref[rows, :] += jnp.where(lane == head, jnp.sum(dz, axis=-1, keepdims=True), 0.0)
                dk_ref[0:keys, :] += lax.dot_general(dzb, qb, tn, preferred_element_type=F32)
                dv_ref[0:keys, :] += lax.dot_general(pb, dob, tn, preferred_element_type=F32)
                dfk_ref[:, 0:keys] -= jnp.sum(dz, axis=0, keepdims=True)
                return carry
            for j in range(per):
                block(j, 0)
        dv_out_ref[...] = dv_ref[...].astype(dv_out_ref.dtype)

    x_spec = pl.BlockSpec((t, HEAD_DIM), lambda h: (0, h))
    v_spec = pl.BlockSpec((t, HEAD_DIM), lambda h: (0, _V_BLOCK + h))
    fq_spec = pl.BlockSpec((t, LANES), lambda h: (0, 0))
    fk_spec = pl.BlockSpec((None, 1, t), lambda h: (h, 0, 0))
    return hosted_call(
        riders, body, name=name, grid=(N_HEADS,), in_specs=[x_spec, x_spec, v_spec, fq_spec, fk_spec, x_spec],
        out_specs=[x_spec, x_spec, x_spec, fq_spec, fk_spec],
        out_shape=[jax.ShapeDtypeStruct((t, D_MODEL), F32)] * 2 + [jax.ShapeDtypeStruct((t, D_MODEL), BF16)]
        + [jax.ShapeDtypeStruct((t, LANES), F32), jax.ShapeDtypeStruct((N_HEADS, 1, t), F32)],
        scratch_shapes=[pltpu.VMEM((t, HEAD_DIM), BF16)] * 2 + [pltpu.VMEM((t, HEAD_DIM), F32)],
        compiler_params=_params(("arbitrary",)),
    )(q, k, proj, fq, fk, do)


def memkv_fwd(mem, mnw, wkv, mknw, *, name):
    n = mem.shape[0]

    def body(mem_ref, mnw_ref, w_ref, mknw_ref, mk_ref, mv_ref):
        mk, mv = memkv_fn(mem_ref[...], mnw_ref[...], w_ref[...], mknw_ref[...])
        mk_ref[...] = mk
        mv_ref[...] = mv

    return pl.pallas_call(
        body, name=name, out_shape=[jax.ShapeDtypeStruct((n, MEM_WIDTH), F32)] * 2,
        compiler_params=pltpu.CompilerParams(vmem_limit_bytes=VMEM_LIMIT),
    )(mem, mnw, wkv, mknw)


def memkv_bwd(mem, mnw, wkv, mknw, dmk, dmv, *, name):
    def body(mem_ref, mnw_ref, w_ref, mknw_ref, dmk_ref, dmv_ref, dmnw_ref, dw_ref, dmknw_ref):
        f = functools.partial(memkv_fn, mem_ref[...])
        _, vjp = jax.vjp(f, mnw_ref[...], w_ref[...].astype(F32), mknw_ref[...])
        dmnw, dw, dmknw = vjp((dmk_ref[...], dmv_ref[...]))
        dmnw_ref[...] = dmnw
        dw_ref[...] = dw.astype(dw_ref.dtype)
        dmknw_ref[...] = dmknw

    return pl.pallas_call(
        body, name=name,
        out_shape=[jax.ShapeDtypeStruct(mnw.shape, F32), jax.ShapeDtypeStruct(wkv.shape, BF16), jax.ShapeDtypeStruct(mknw.shape, F32)],
        compiler_params=pltpu.CompilerParams(vmem_limit_bytes=VMEM_LIMIT),
    )(mem, mnw, wkv, mknw, dmk, dmv)


def _row(v, width=None):
    v = v.reshape(1, -1)
    if width is not None and v.shape[1] < width:
        v = jnp.pad(v, ((0, 0), (0, width - v.shape[1])))
    return v


def _norm_fwd(x, w, name, riders=()):
    return rows_call(lambda x, w: rms(x, w), [x], [w], [(D_MODEL, BF16)], [], tm=512, name=name, riders=riders)[0]


FF_PIECE = D_FF // N_DEV


def _add(r, x, *rows):
    return r + x


def _norm_rows(r, w):
    return rms(r, w)


def _norm_bwd_post(dh, x, dx_in, w):
    _, vjp = jax.vjp(rms, x, w)
    dx, dw = vjp(dh)
    return dx + dx_in, dw


def _piece(rows, cols, index):
    return pl.BlockSpec((None, rows, cols), lambda i, j, kk: (index(i, j, kk), 0, 0))


def _two_pieces(rows, cols, index):
    return pl.BlockSpec((2, rows, cols), lambda i, j, kk: (index(i, j, kk), 0, 0))


def _mlp_fwd(x, h2, w1, w2, layer, riders=(), next_norm_w=None):
    riders = list(riders) + [None, None]
    u, a1 = matmul(h2, w1, name=f"mlp1_fwd_{layer}", tiles=(None, FF_PIECE, D_MODEL),
                   extra_out=(lambda u: jnp.square(jnp.maximum(u, 0.0)), BF16),
                   b_view=(D_MODEL, D_FF, _piece(D_MODEL, FF_PIECE, lambda i, j, kk: j)), riders=riders[0])
    norm = dict(row_ins=[next_norm_w], extra_out=(_norm_rows, BF16)) if next_norm_w is not None else {}
    y = matmul(a1, w2, name=f"mlp2_fwd_{layer}", post=_add, post_ins=[x], tiles=(None, D_MODEL, 2 * FF_PIECE),
               b_view=(D_FF, D_MODEL, _two_pieces(FF_PIECE, D_MODEL, lambda i, j, kk: kk)), riders=riders[1], **norm)
    return y, (x, h2, u, a1)


def pair_sum(g, got, *, name):
    _, rows, cols = g.shape
    tile = _pick(rows, (512, 256, 128))
    c = lax.axis_index("c").astype(jnp.int32).reshape(1)

    def body(c_ref, a_ref, b_ref, o_ref):
        o_ref[...] = (a_ref[...].astype(F32) + b_ref[...].astype(F32)).astype(o_ref.dtype)

    grid_spec = pltpu.PrefetchScalarGridSpec(
        num_scalar_prefetch=1, grid=(4, rows // tile),
        in_specs=[pl.BlockSpec((None, tile, cols), lambda k, i, c_ref: (2 * k + c_ref[0], i, 0)),
                  pl.BlockSpec((None, tile, cols), lambda k, i, c_ref: (k, i, 0))],
        out_specs=pl.BlockSpec((None, tile, cols), lambda k, i, c_ref: (k, i, 0)))
    return pl.pallas_call(
        body, name=name, grid_spec=grid_spec, out_shape=jax.ShapeDtypeStruct((4, rows, cols), g.dtype),
        compiler_params=_params(("parallel", "parallel")),
    )(c, g, got)


def chip_sums(names, pieces, gots):
    return [pair_sum(a, got, name=f"grads_pair_sum_{n}") for n, a, got in zip(names, pieces, gots)]


def _mlp_bwd(dy, res, n2w, w1, w2, layer, riders=()):
    x, h2, u, a1 = res
    du = matmul(dy, w2, tb=True, name=f"mlp2_dx_{layer}", out_dtype=BF16, tiles=(None, 2 * FF_PIECE, D_MODEL),
                post=lambda r, u: r * (2.0 * jnp.maximum(u, 0.0)), post_ins=[u],
                b_view=(D_MODEL, D_FF, _two_pieces(FF_PIECE, D_MODEL, lambda i, j, kk: j)), riders=riders)
    dw2 = matmul(a1, dy, ta=True, name=f"mlp2_dw_{layer}", out_dtype=BF16, tiles=(FF_PIECE, D_MODEL, None), out_view=(
        w2.shape, _piece(FF_PIECE, D_MODEL, lambda i, j, kk: i)))
    sib2 = sibling_rider([dw2])
    dx, dn2w = matmul(du, w1, tb=True, name=f"mlp1_dx_{layer}", tiles=(None, D_MODEL, FF_PIECE),
                      b_view=(D_FF, D_MODEL, _piece(D_MODEL, FF_PIECE, lambda i, j, kk: kk)),
                      post=_norm_bwd_post, post_ins=[x, dy], row_ins=[n2w], acc=True, riders=[sib2])
    dw1 = matmul(h2, du, ta=True, name=f"mlp1_dw_{layer}", out_dtype=BF16, tiles=(D_MODEL, FF_PIECE, None), out_view=(
        w1.shape, _piece(D_MODEL, FF_PIECE, lambda i, j, kk: j)))
    return dx, dw1, dw2, dn2w, sibling_rider([dw1]), sib2


def _in_proj_bwd(h, dmain, dsmall, w_main, w_small, x, dx_in, n1w, tag):
    dh = matmul(dmain, w_main, tb=True, name=f"inproj_dx_main_{tag}")

    def post(r, dh_main, x, dx_in, w):
        return _norm_bwd_post(r + dh_main, x, dx_in, w)

    dx, dn1w = matmul(dsmall, w_small, tb=True, name=f"inproj_dx_small_{tag}", tiles=(None, D_MODEL, None),
                      post=post, post_ins=[dh, x, dx_in], row_ins=[n1w], acc=True)
    dw_main = matmul(h, dmain, ta=True, out_dtype=BF16, name=f"inproj_dw_main_{tag}")
    dw_small = matmul(h, dsmall, ta=True, out_dtype=BF16, name=f"inproj_dw_small_{tag}")
    return dx, dn1w, dw_main, dw_small


def local_step(x, mem, target, w, m, v):
    t = x.shape[0]
    n_mem = mem.shape[0]
    g = {}

    def wire(a):
        return a.astype(BF16)

    (dn_g,), = run_riders([gather_rider([wire(w["dn_w_in"][0])])], name="weights_gather_first")
    dn_main, dn_ab = in_proj_weights(dn_g, DN_IN, 2 * N_HEADS)
    fox_w = wire(w["fox_w_in"][0])
    ride_out = gather_rider([wire(w["w_out"][0]), w["dn_conv_w"][0]])
    ride_out_1 = gather_rider([wire(w["w_out"][1])])
    ride_kv = gather_rider([wire(w["w_mem_kv"])])
    ride_mlp1_0 = gather_rider([wire(w["w_mlp1"][0])])
    ride_mlp2_0 = gather_rider([wire(w["w_mlp2"][0])])
    ride_fox_a, ride_fox_b = gather_rider([fox_w[:D_MODEL // 2]]), gather_rider([fox_w[D_MODEL // 2:]])
    ride_mlp_1 = gather_rider([wire(w["w_mlp1"][1]), wire(w["w_mlp2"][1])])
    mnw, mknw = _row(w["mem_norm_w"]), _row(w["mem_k_norm_w"])

    n1w0, n2w0 = _row(w["norm1_w"][0]), _row(w["norm2_w"][0])
    n1w1, n2w1 = _row(w["norm1_w"][1]), _row(w["norm2_w"][1])
    alog, dtb = _row(w["dn_a_log"][0], LANES), _row(w["dn_dt_bias"][0], LANES)
    onw, mqw0 = _row(w["dn_o_norm_w"][0]), _row(w["memq_norm_w"][0])
    x0 = x
    h0 = _norm_fwd(x0, n1w0, "norm1_fwd_0")
    pm0 = matmul(h0, dn_main, name="inproj_main_0", riders=[ride_out])
    w_out0 = ride_out.results[0].reshape(OUT_IN, D_MODEL)
    conv_w = ride_out.results[1].transpose(1, 0, 2).reshape(CONV_WIDTH, 3 * D_MODEL)
    ps0 = matmul(h0, dn_ab, name="inproj_small_0")
    gates = rows_call(dn_gates_fn, [ps0], [alog, dtb], [(LANES, F32)], [], tm=512, name="dn_gates_fwd")[0]
    q0, k0, v0 = dn_prep_fwd(pm0, conv_w, name="dn_prep_fwd", riders=[ride_kv])
    w_kv = ride_kv.results[0].reshape(D_MODEL, D_MODEL)
    mk, mv = memkv_fwd(mem, mnw, w_kv, mknw, name="memkv_fwd")
    u0, w0, qk0 = delta_intra_fwd(q0, k0, v0, gates, name="delta_intra_fwd", riders=[ride_mlp1_0])
    o0, s_start = delta_seq_fwd(q0, k0, gates, u0, w0, qk0, name="delta_seq_fwd", riders=[ride_mlp2_0])
    cat0 = rows_call(dn_out_fn, [o0, (pm0, D_MODEL, 3), (pm0, MEM_WIDTH, 8)], [onw, mqw0, mk, mv],
                     [(D_MODEL + MEM_WIDTH, BF16)], [], tm=256, name="dn_out_fwd")[0]
    (w1_0,), (w2_0,) = ride_mlp1_0.results, ride_mlp2_0.results
    x1, h2_0 = matmul(cat0, w_out0, post=_add, post_ins=[x0], row_ins=[n2w0], extra_out=(_norm_rows, BF16),
                      tiles=(None, D_MODEL, None), name="wout_fwd_0")
    (x2, h1), mlp_res0 = _mlp_fwd(x1, h2_0, w1_0, w2_0, 0, riders=[[ride_fox_a], [ride_fox_b]], next_norm_w=n1w1)
    fox_main, fox_f = in_proj_weights(
        jnp.concatenate([ride_fox_a.results[0], ride_fox_b.results[0]], axis=1), FOX_IN, N_HEADS)

    fbias = _row(w["fox_f_bias"][0], LANES)
    qnw, knw, mqw1 = _row(w["fox_q_norm_w"][0]), _row(w["fox_k_norm_w"][0]), _row(w["memq_norm_w"][1])
    pm1 = matmul(h1, fox_main, name="inproj_main_1", riders=[ride_out_1])
    w_out1 = ride_out_1.results[0].reshape(OUT_IN, D_MODEL)
    ps1 = matmul(h1, fox_f, name="inproj_small_1")
    fq = rows_call(fox_fcum_fn, [ps1], [fbias], [(LANES, F32)], [], tm=t, name="fox_fcum_fwd")[0]
    fk = fq[:, :N_HEADS].T[:, None, :]
    q1, k1 = rows_call(fox_qk_fn, [(pm1, D_MODEL, 0), (pm1, D_MODEL, 1)], [qnw, knw], [(D_MODEL, F32)] * 2, [], tm=256,
                       name="fox_qk_fwd")
    o1 = fox_attn_fwd(q1, k1, pm1, fq, fk, name="fox_attn_fwd", riders=[ride_mlp_1])
    cat1 = rows_call(fox_out_fn, [o1, (pm1, D_MODEL, 3), (pm1, MEM_WIDTH, 8)], [mqw1, mk, mv],
                     [(D_MODEL + MEM_WIDTH, BF16)], [], tm=256, name="fox_out_fwd")[0]
    w1_1, w2_1 = ride_mlp_1.results
    x3, h2_1 = matmul(cat1, w_out1, post=_add, post_ins=[x2], row_ins=[n2w1], extra_out=(_norm_rows, BF16),
                      tiles=(None, D_MODEL, None), name="wout_fwd_1")
    y, mlp_res1 = _mlp_fwd(x3, h2_1, w1_1, w2_1, 1)

    def loss_fn(y, tgt):
        e = y - tgt
        return e * (1.0 / D_MODEL), jnp.sum(jnp.sum(e * e, axis=1, keepdims=True), axis=0, keepdims=True)
    dy, sq = rows_call(loss_fn, [y, target], [], [(D_MODEL, F32)], [(1, 1)], tm=512, name="loss")
    loss = sq * (0.5 / D_MODEL)

    dx3, dw1_1, dw2_1, dn2w1, sib1, sib2 = _mlp_bwd(dy, mlp_res1, n2w1, w1_1, w2_1, 1)
    dcat1 = matmul(dx3, w_out1, tb=True, name="wout_dx_1", riders=[sib1])
    dwo_1 = matmul(cat1, dx3, ta=True, out_dtype=BF16, name="wout_dw_1").reshape(N_DEV, OUT_IN // N_DEV, D_MODEL)
    sibo = sibling_rider([dwo_1])
    do1, dgate1, dqm1, dmqw1, dmk1, dmv1 = rows_call(
        functools.partial(vjp_rows(fox_out_fn, 3, (True, True, True)), n_row=3, n_ct=1),
        [o1, (pm1, D_MODEL, 3), (pm1, MEM_WIDTH, 8), dcat1], [mqw1, mk, mv],
        [(D_MODEL, F32), (D_MODEL, BF16), (MEM_WIDTH, BF16)], [(1, HEAD_DIM), (n_mem, MEM_WIDTH), (n_mem, MEM_WIDTH)],
        tm=256, name="fox_out_bwd", riders=[sibo])
    ride_l1 = chips_rider(chip_sums(["w_mlp2_1", "w_mlp1_1", "w_out_1"], [dw2_1, dw1_1, dwo_1],
                                    sib2.results + sib1.results + sibo.results))
    dq1, dk1, dv1, dfq, dfk = fox_attn_bwd(q1, k1, pm1, fq, fk, do1, name="fox_attn_bwd", riders=[ride_l1])
    dqraw1, dkraw1, dqnw, dknw = rows_call(
        functools.partial(vjp_rows(fox_qk_fn, 2, (True, True)), n_row=2, n_ct=2),
        [(pm1, D_MODEL, 0), (pm1, D_MODEL, 1), dq1, dk1], [qnw, knw],
        [(D_MODEL, BF16)] * 2, [(1, HEAD_DIM)] * 2, tm=256, name="fox_qk_bwd")
    dfcum = dfq + jnp.pad(dfk[:, 0, :].T, ((0, 0), (0, LANES - N_HEADS)))
    dps1, dfbias = rows_call(
        functools.partial(vjp_rows(fox_fcum_fn, 1, (True,)), n_row=1, n_ct=1),
        [ps1, dfcum], [fbias], [(LANES, F32)], [(1, LANES)], tm=t, name="fox_fcum_bwd")
    dpm1 = jnp.concatenate([dqraw1, dkraw1, dv1, dgate1, dqm1], axis=1)
    dx2, dn1w1, dwmain1, dwsmall1 = _in_proj_bwd(h1, dpm1, dps1, fox_main, fox_f, x2, dx3, n1w1, "1")
    g_fox = in_proj_pieces(dwmain1, dwsmall1, N_HEADS, FOX_IN)
    sibf = sibling_rider([g_fox])

    dx1, dw1_0, dw2_0, dn2w0, sib1, sib2 = _mlp_bwd(dx2, mlp_res0, n2w0, w1_0, w2_0, 0, riders=[sibf])
    ride_fox_g = chips_rider(chip_sums(["fox_w_in"], [g_fox], sibf.results))
    dcat0 = matmul(dx1, w_out0, tb=True, name="wout_dx_0", riders=[sib1])
    dwo_0 = matmul(cat0, dx1, ta=True, out_dtype=BF16, name="wout_dw_0").reshape(N_DEV, OUT_IN // N_DEV, D_MODEL)
    sibo = sibling_rider([dwo_0])
    do0, dz0, dqm0, donw, dmqw0, dmk0, dmv0 = rows_call(
        functools.partial(vjp_rows(dn_out_fn, 3, (True, True, True, True)), n_row=3, n_ct=1),
        [o0, (pm0, D_MODEL, 3), (pm0, MEM_WIDTH, 8), dcat0], [onw, mqw0, mk, mv],
        [((N_HEADS, HEAD_DIM), F32), (D_MODEL, BF16), (MEM_WIDTH, BF16)],
        [(1, HEAD_DIM), (1, HEAD_DIM), (n_mem, MEM_WIDTH), (n_mem, MEM_WIDTH)], tm=256, name="dn_out_bwd", riders=[sibo])
    h_l0 = chip_sums(["w_mlp2_0", "w_mlp1_0", "w_out_0"], [dw2_0, dw1_0, dwo_0], sib2.results + sib1.results + sibo.results)
    ride_l0_mlp2, ride_l0_rest = chips_rider(h_l0[:1]), chips_rider(h_l0[1:])
    dmnw, dwkv, dmknw = memkv_bwd(mem, mnw, w_kv, mknw, dmk0 + dmk1, dmv0 + dmv1, name="memkv_bwd")
    g_kv = dwkv.reshape(N_DEV, D_MODEL // N_DEV, D_MODEL)
    sibk = sibling_rider([g_kv])
    dq_s, dk_s, dg_s, du0, dw0, dqk0 = delta_seq_bwd(q0, k0, gates, u0, w0, qk0, s_start, do0, name="delta_seq_bwd",
                                                     riders=[ride_fox_g, sibk])
    ride_kv_g = chips_rider(chip_sums(["w_mem_kv"], [g_kv], sibk.results))
    dq0, dk0, dv0, dgates = delta_intra_bwd(q0, k0, v0, gates, du0, dw0, dqk0, dq_s, dk_s, dg_s,
                                            name="delta_intra_bwd", riders=[ride_l0_mlp2, ride_kv_g])
    dxq, dxk, dxv, dcq, dck, dcv = dn_prep_bwd(pm0, conv_w, dq0, dk0, dv0, name="dn_prep_bwd", riders=[ride_l0_rest])
    dconv = jnp.concatenate([dcq, dck, dcv], axis=1)
    dps0, dalog, ddtb = rows_call(
        functools.partial(vjp_rows(dn_gates_fn, 1, (True, True)), n_row=1, n_ct=1),
        [ps0, dgates], [alog, dtb], [(LANES, F32)], [(1, LANES)] * 2, tm=512, name="dn_gates_bwd")
    dpm0 = jnp.concatenate([dxq, dxk, dxv, dz0, dqm0], axis=1)
    grad_x, dn1w0, dwmain0, dwsmall0 = _in_proj_bwd(h0, dpm0, dps0, dn_main, dn_ab, x0, dx1, n1w0, "0")
    g_dn = in_proj_pieces(dwmain0, dwsmall0, 2 * N_HEADS, DN_IN)
    g_conv = dconv.reshape(CONV_WIDTH, N_DEV, -1).transpose(1, 0, 2).astype(BF16)

    g["mem_norm_w"] = dmnw[0]
    g["mem_k_norm_w"] = dmknw[0]
    g["norm1_w"] = jnp.concatenate([dn1w0, dn1w1], axis=0)
    g["dn_a_log"] = dalog[:, :N_HEADS]
    g["dn_dt_bias"] = ddtb[:, :N_HEADS]
    g["dn_o_norm_w"] = donw
    g["fox_f_bias"] = dfbias[:, :N_HEADS]
    g["fox_q_norm_w"] = dqnw
    g["fox_k_norm_w"] = dknw
    g["memq_norm_w"] = jnp.concatenate([dmqw0, dmqw1], axis=0)
    g["norm2_w"] = jnp.concatenate([dn2w0, dn2w1], axis=0)

    sibd = sibling_rider([g_dn, g_conv])
    run_riders([sibd], name="grads_to_sibling_last")
    ride_last = chips_rider(chip_sums(["dn_w_in", "dn_conv_w"], [g_dn, g_conv], sibd.results))
    ride_small = gather_rider([pack_small(g, loss, name="pack_small_grads")])
    run_riders([ride_last, ride_small], name="grads_to_chips_last")

    def layers(l0, l1):
        return jnp.stack([l0, l1], axis=1).reshape(4, -1, l0.shape[-1])

    parts = {
        "w_mlp1": layers(ride_l0_rest.results[0], ride_l1.results[1]),
        "w_mlp2": layers(ride_l0_mlp2.results[0], ride_l1.results[0]),
        "w_out": layers(ride_l0_rest.results[1], ride_l1.results[2]),
        "fox_w_in": ride_fox_g.results[0], "w_mem_kv": ride_kv_g.results[0],
        "dn_w_in": ride_last.results[0], "dn_conv_w": ride_last.results[1],
    }
    out = {n: adamw(parts[n], w[n], m[n], v[n], name=f"adamw_{n}") for n, _, _ in BIG}
    small, loss = adamw_small(ride_small.results[0], w, m, v, name="adamw_small")
    return loss, grad_x, out, small


WEIGHTS = ["mem_norm_w", "w_mem_kv", "mem_k_norm_w", "norm1_w", "dn_w_in", "dn_conv_w", "dn_a_log", "dn_dt_bias",
           "dn_o_norm_w", "fox_w_in", "fox_f_bias", "fox_q_norm_w", "fox_k_norm_w", "memq_norm_w", "w_out", "norm2_w",
           "w_mlp1", "w_mlp2"]
DN_IN = 4 * D_MODEL + 2 * N_HEADS + MEM_WIDTH
FOX_IN = 4 * D_MODEL + N_HEADS + MEM_WIDTH
GATE_END = 4 * D_MODEL
OUT_IN = D_MODEL + MEM_WIDTH
BIG = [("w_mem_kv", D_MODEL // N_DEV, D_MODEL), ("dn_w_in", D_MODEL, DN_IN // N_DEV), ("fox_w_in", D_MODEL, FOX_IN // N_DEV),
       ("dn_conv_w", CONV_WIDTH, 3 * D_MODEL // N_DEV), ("w_out", 2 * OUT_IN // N_DEV, D_MODEL),
       ("w_mlp1", 2 * D_MODEL, FF_PIECE), ("w_mlp2", 2 * FF_PIECE, D_MODEL)]
SMALL_TILE = 8 * LANES
SMALL = [(name, shape, -(-math.prod(shape) // SMALL_TILE) * SMALL_TILE) for name, shape in [
    ("mem_norm_w", (D_MODEL,)), ("mem_k_norm_w", (HEAD_DIM,)), ("norm1_w", (2, D_MODEL)), ("dn_a_log", (1, N_HEADS)),
    ("dn_dt_bias", (1, N_HEADS)), ("dn_o_norm_w", (1, HEAD_DIM)), ("fox_f_bias", (1, N_HEADS)),
    ("fox_q_norm_w", (1, HEAD_DIM)), ("fox_k_norm_w", (1, HEAD_DIM)), ("memq_norm_w", (2, HEAD_DIM)), ("norm2_w", (2, D_MODEL))]]
SMALL_ROWS = sum(ln for _, _, ln in SMALL) // LANES + 8


def _small_view(a):
    return a.reshape(-1, LANES) if a.size % LANES == 0 else a.reshape(1, a.size)


def pack_small(g, last, *, name):
    ins = [_small_view(g[n]) for n, _, _ in SMALL]

    def body(*refs):
        out = refs[-1]
        out[...] = jnp.zeros_like(out)
        row = 0
        for ref, (_, _, ln) in zip(refs, SMALL):
            r, c = ref.shape
            out[row:row + r, 0:c] = ref[...]
            row += ln // LANES
        out[SMALL_ROWS - 1:SMALL_ROWS, LANES - 1:LANES] = refs[-2][...]

    return pl.pallas_call(body, name=name, out_shape=jax.ShapeDtypeStruct((SMALL_ROWS, LANES), F32))(*ins, last)


def in_proj_weights(gathered, width, n_small):
    full = gathered.transpose(1, 0, 2).reshape(D_MODEL, width)
    main = jnp.concatenate([full[:, :GATE_END], full[:, GATE_END + n_small:]], axis=1)
    return main, jnp.pad(full[:, GATE_END:GATE_END + n_small], ((0, 0), (0, LANES - n_small)))


def in_proj_pieces(d_main, d_small, n_small, width):
    full = jnp.concatenate([d_main[:, :GATE_END], d_small[:, :n_small], d_main[:, GATE_END:]], axis=1)
    return full.reshape(D_MODEL, N_DEV, width // N_DEV).transpose(1, 0, 2)


def _adamw_update(g, w, m, v):
    m_new = ADAM_B1 * m + (1.0 - ADAM_B1) * g
    v_new = ADAM_B2 * v + (1.0 - ADAM_B2) * jnp.square(g)
    m_hat = m_new / (1.0 - ADAM_B1 ** ADAM_STEP)
    v_hat = v_new / (1.0 - ADAM_B2 ** ADAM_STEP)
    return -ADAM_LR * (m_hat / (jnp.sqrt(v_hat) + ADAM_EPS) + ADAM_WD * w), m_new, v_new


def adamw(parts, w, m, v, *, name):
    n, _, cols = parts.shape
    layers = w.shape[0] if w.ndim == 3 else 1
    rows = w.shape[-2]
    tile = _pick(rows, (512, 256, 128))
    steps = rows // tile

    def body(p_ref, w_ref, m_ref, v_ref, g_ref, d_ref, mo_ref, vo_ref):
        g = p_ref[0].astype(F32)
        for i in range(1, n):
            g = g + p_ref[i].astype(F32)
        g_ref[...] = g
        d_ref[...], mo_ref[...], vo_ref[...] = _adamw_update(g, w_ref[...], m_ref[...], v_ref[...])

    if w.ndim == 3:
        spec = pl.BlockSpec((None, tile, cols), lambda l, i: (l, i, 0))
    else:
        spec = pl.BlockSpec((tile, cols), lambda l, i: (i, 0))
    return pl.pallas_call(
        body, name=name, grid=(layers, steps),
        in_specs=[pl.BlockSpec((n, tile, cols), lambda l, i: (0, l * steps + i, 0)), spec, spec, spec], out_specs=[spec] * 4,
        out_shape=[jax.ShapeDtypeStruct(w.shape, F32)] * 4, compiler_params=_params(("parallel", "parallel")),
    )(parts, w, m, v)


def adamw_small(parts, w, m, v, *, name):
    k = len(SMALL)
    ins = [_small_view(d[n]) for d in (w, m, v) for n, _, _ in SMALL]

    def body(p_ref, *refs):
        w_refs, m_refs, v_refs, outs, g_ref = refs[:k], refs[k:2 * k], refs[2 * k:3 * k], refs[3 * k:-1], refs[-1]
        g_all = p_ref[0]
        for i in range(1, N_DEV):
            g_all = g_all + p_ref[i]
        g_ref[...] = g_all
        row = 0
        for i, (_, _, ln) in enumerate(SMALL):
            r, c = w_refs[i].shape
            g = g_ref[row:row + r, 0:c]
            outs[4 * i][...] = g
            outs[4 * i + 1][...], outs[4 * i + 2][...], outs[4 * i + 3][...] = _adamw_update(
                g, w_refs[i][...], m_refs[i][...], v_refs[i][...])
            row += ln // LANES
        outs[-1][...] = g_ref[SMALL_ROWS - 1:SMALL_ROWS, LANES - 1:LANES]

    out_shape = [jax.ShapeDtypeStruct(a.shape, F32) for a in ins[:k] for _ in range(4)] + [jax.ShapeDtypeStruct((1, 1), F32)]
    res = pl.pallas_call(body, name=name, out_shape=out_shape,
                         scratch_shapes=[pltpu.VMEM((SMALL_ROWS, LANES), F32)])(parts, *ins)
    small = {n: [o.reshape(sh) for o in res[4 * i:4 * i + 4]] for i, (n, sh, _) in enumerate(SMALL)}
    return small, res[-1][0, 0]


def kernel(x, mem, mem_norm_w, w_mem_kv, mem_k_norm_w, norm1_w, dn_w_in, dn_conv_w, dn_a_log, dn_dt_bias, dn_o_norm_w, fox_w_in, fox_f_bias, fox_q_norm_w, fox_k_norm_w, memq_norm_w, w_out, norm2_w, w_mlp1, w_mlp2, loss_target, m_mem_norm_w, m_w_mem_kv, m_mem_k_norm_w, m_norm1_w, m_dn_w_in, m_dn_conv_w, m_dn_a_log, m_dn_dt_bias, m_dn_o_norm_w, m_fox_w_in, m_fox_f_bias, m_fox_q_norm_w, m_fox_k_norm_w, m_memq_norm_w, m_w_out, m_norm2_w, m_w_mlp1, m_w_mlp2, v_mem_norm_w, v_w_mem_kv, v_mem_k_norm_w, v_norm1_w, v_dn_w_in, v_dn_conv_w, v_dn_a_log, v_dn_dt_bias, v_dn_o_norm_w, v_fox_w_in, v_fox_f_bias, v_fox_q_norm_w, v_fox_k_norm_w, v_memq_norm_w, v_w_out, v_norm2_w, v_w_mlp1, v_w_mlp2):
    p = dict(mem_norm_w=mem_norm_w, w_mem_kv=w_mem_kv, mem_k_norm_w=mem_k_norm_w, norm1_w=norm1_w, dn_w_in=dn_w_in,
             dn_conv_w=dn_conv_w, dn_a_log=dn_a_log, dn_dt_bias=dn_dt_bias, dn_o_norm_w=dn_o_norm_w, fox_w_in=fox_w_in,
             fox_f_bias=fox_f_bias, fox_q_norm_w=fox_q_norm_w, fox_k_norm_w=fox_k_norm_w, memq_norm_w=memq_norm_w,
             w_out=w_out, norm2_w=norm2_w, w_mlp1=w_mlp1, w_mlp2=w_mlp2)
    pm = dict(mem_norm_w=m_mem_norm_w, w_mem_kv=m_w_mem_kv, mem_k_norm_w=m_mem_k_norm_w, norm1_w=m_norm1_w,
              dn_w_in=m_dn_w_in, dn_conv_w=m_dn_conv_w, dn_a_log=m_dn_a_log, dn_dt_bias=m_dn_dt_bias,
              dn_o_norm_w=m_dn_o_norm_w, fox_w_in=m_fox_w_in, fox_f_bias=m_fox_f_bias, fox_q_norm_w=m_fox_q_norm_w,
              fox_k_norm_w=m_fox_k_norm_w, memq_norm_w=m_memq_norm_w, w_out=m_w_out, norm2_w=m_norm2_w, w_mlp1=m_w_mlp1,
              w_mlp2=m_w_mlp2)
    pv = dict(mem_norm_w=v_mem_norm_w, w_mem_kv=v_w_mem_kv, mem_k_norm_w=v_mem_k_norm_w, norm1_w=v_norm1_w,
              dn_w_in=v_dn_w_in, dn_conv_w=v_dn_conv_w, dn_a_log=v_dn_a_log, dn_dt_bias=v_dn_dt_bias,
              dn_o_norm_w=v_dn_o_norm_w, fox_w_in=v_fox_w_in, fox_f_bias=v_fox_f_bias, fox_q_norm_w=v_fox_q_norm_w,
              fox_k_norm_w=v_fox_k_norm_w, memq_norm_w=v_memq_norm_w, w_out=v_w_out, norm2_w=v_norm2_w, w_mlp1=v_w_mlp1,
              w_mlp2=v_w_mlp2)

    loss, grad_x, results, small = local_step(x[0], mem[0], loss_target[0], p, pm, pv)
    groups = [{n: r[i] for n, r in {**small, **results}.items()} for i in range(4)]
    return (loss, grad_x[None], *[grp[n] for grp in groups for n in WEIGHTS])
```

```python
import functools
import math

import jax
import jax.numpy as jnp
from jax import lax
from jax.experimental import pallas as pl
from jax.experimental.pallas import tpu as pltpu

F32 = jnp.float32
BF16 = jnp.bfloat16
HIGHEST = lax.Precision.HIGHEST

D_MODEL = 1024
HEAD_DIM = 128
N_HEADS = 8
MEM_HEADS = 4
MEM_WIDTH = MEM_HEADS * HEAD_DIM
D_FF = 4 * D_MODEL
CONV_WIDTH = 4
CHUNK = 64
Q_BLOCK = 128
EPS = 1e-6
SCALE = HEAD_DIM ** -0.5
MAIN_WIDTH = 4 * D_MODEL + MEM_WIDTH
LANES = 128
N_DEV = 8

ADAM_LR = 0.001
ADAM_B1 = 0.9
ADAM_B2 = 0.999
ADAM_EPS = 1e-08
ADAM_WD = 0.01
ADAM_STEP = 10

VMEM_LIMIT = 56 * 2 ** 20
MESH = pl.DeviceIdType.MESH


def _bdot(a, b, dims):
    return lax.dot_general(a.astype(BF16), b.astype(BF16), (dims, ((), ())), preferred_element_type=F32)


@jax.custom_vjp
def mm(a, b):
    return _bdot(a, b, ((1,), (0,)))


@jax.custom_vjp
def mm_nt(a, b):
    return _bdot(a, b, ((1,), (1,)))


@jax.custom_vjp
def mm_tn(a, b):
    return _bdot(a, b, ((0,), (0,)))


mm.defvjp(lambda a, b: (mm(a, b), (a, b)), lambda r, g: (mm_nt(g, r[1]), mm_tn(r[0], g)))
mm_nt.defvjp(lambda a, b: (mm_nt(a, b), (a, b)), lambda r, g: (mm(g, r[1]), mm_tn(g, r[0])))
mm_tn.defvjp(lambda a, b: (mm_tn(a, b), (a, b)), lambda r, g: (mm_nt(r[1], g), mm(r[0], g)))


def hdot(a, b):
    return jnp.dot(a, b, precision=HIGHEST, preferred_element_type=F32)


def rms(x, w):
    return x * lax.rsqrt(jnp.mean(x * x, axis=-1, keepdims=True) + EPS) * w


def l2n(x):
    return x * lax.rsqrt(jnp.sum(x * x, axis=-1, keepdims=True) + EPS)


def _iota2(n, m):
    return lax.broadcasted_iota(jnp.int32, (n, m), 0), lax.broadcasted_iota(jnp.int32, (n, m), 1)


def _lower_ones(n):
    r, c = _iota2(n, n)
    return jnp.where(r >= c, 1.0, 0.0).astype(F32)


def _last_row(x):
    r = lax.broadcasted_iota(jnp.int32, x.shape, 0)
    return jnp.sum(jnp.where(r == x.shape[0] - 1, x, 0.0), axis=0, keepdims=True)


def _softmax_rows(z):
    m = lax.stop_gradient(jnp.max(z, axis=-1, keepdims=True))
    e = jnp.exp(z - m)
    return e * (1.0 / jnp.sum(e, axis=-1, keepdims=True))


_BNN = (((2,), (1,)), ((0,), (0,)))
_BNT = (((2,), (2,)), ((0,), (0,)))
_BTN = (((1,), (1,)), ((0,), (0,)))


def _bbdot(a, b, dims):
    return lax.dot_general(a.astype(BF16), b.astype(BF16), dims, preferred_element_type=F32)


@jax.custom_vjp
def bmm(a, b):
    return _bbdot(a, b, _BNN)


@jax.custom_vjp
def bmm_nt(a, b):
    return _bbdot(a, b, _BNT)


@jax.custom_vjp
def bmm_tn(a, b):
    return _bbdot(a, b, _BTN)


@jax.custom_vjp
def bmm_high(a, b):
    return lax.dot_general(a, b, _BNN, precision=lax.Precision.HIGH, preferred_element_type=F32)


bmm.defvjp(lambda a, b: (bmm(a, b), (a, b)), lambda r, g: (bmm_nt(g, r[1]), bmm_tn(r[0], g)))
bmm_nt.defvjp(lambda a, b: (bmm_nt(a, b), (a, b)), lambda r, g: (bmm(g, r[1]), bmm_tn(g, r[0])))
bmm_tn.defvjp(lambda a, b: (bmm_tn(a, b), (a, b)), lambda r, g: (bmm_nt(r[1], g), bmm(r[0], g)))
bmm_high.defvjp(lambda a, b: (bmm_high(a, b), (a, b)), lambda r, g: (bmm_nt(g, r[1]), bmm_tn(r[0], g)))

NEUMANN_HIGH_LEVELS = 2


@jax.custom_vjp
def inv_unit_lower(a):
    n = a.shape[-1]
    r, c = _iota2(n, n)
    p = jnp.where(r == c, 1.0, 0.0).astype(F32) - a
    ak = a
    for level in range(int(math.log2(n)) - 1):
        dot = bmm_high if level < NEUMANN_HIGH_LEVELS else bmm
        ak = dot(ak, ak)
        p = p + dot(p, ak)
    return p


def _inv_unit_lower_fwd(a):
    t = inv_unit_lower(a)
    return t, t


def _inv_unit_lower_bwd(t, g):
    return (-bmm_tn(t, bmm_nt(g, t)),)


inv_unit_lower.defvjp(_inv_unit_lower_fwd, _inv_unit_lower_bwd)


def delta_intra(q, k, v, gc, beta):
    b, c, _ = q.shape
    r, cc = _iota2(c, c)
    causal = r >= cc
    strict = r > cc
    gi = jnp.broadcast_to(gc, (b, c, c))
    gj = jnp.swapaxes(gi, 1, 2)
    decay = jnp.where(causal, jnp.exp(jnp.where(causal, gi - gj, 0.0)), 0.0)
    kb = k * beta
    a = jnp.where(strict, bmm_nt(kb, k) * decay, 0.0)
    t = inv_unit_lower(a)
    u = bmm(t, v * beta)
    w = bmm(t, kb * jnp.exp(gc))
    qk = jnp.where(causal, bmm_nt(q, k) * decay, 0.0)
    return u, w, qk


def delta_step(s, q, k, gc, u, w, qk):
    v_new = u - bmm(w, s)
    out = bmm(q * jnp.exp(gc), s) + bmm(qk, v_new)
    r = lax.broadcasted_iota(jnp.int32, gc.shape, 1)
    g_last = jnp.sum(jnp.where(r == gc.shape[1] - 1, gc, 0.0), axis=1, keepdims=True)
    k_dec = k * jnp.exp(g_last - gc)
    s_new = s * jnp.exp(g_last) + bmm_tn(k_dec, v_new)
    return out, s_new


def fox_probs(q, k, fq, fk, qpos0):
    s = lax.dot_general(q, k, (((1,), (1,)), ((), ())), preferred_element_type=F32)
    r, c = _iota2(s.shape[0], s.shape[1])
    return _softmax_rows(jnp.where(c <= (r + qpos0), s + (fq - fk), -jnp.inf))


def mem_head(qm, wq, mk, mv):
    p = _softmax_rows(mm_nt(rms(qm, wq) * SCALE, mk))
    return mm(p, mv)


def _heads(x, n):
    return [x[:, h * HEAD_DIM:(h + 1) * HEAD_DIM] for h in range(n)]


def memkv_fn(mem, mnw, wkv, mknw):
    kv = mm(rms(mem, mnw), wkv)
    mk = jnp.concatenate([rms(kh, mknw) for kh in _heads(kv[:, :MEM_WIDTH], MEM_HEADS)], axis=1)
    return mk, kv[:, MEM_WIDTH:]


def dn_gates_fn(ab, alog, dtb):
    g = -jnp.exp(alog) * jax.nn.softplus(ab + dtb)
    low = _lower_ones(CHUNK)
    gc = jnp.concatenate([hdot(low, g[i * CHUNK:(i + 1) * CHUNK]) for i in range(ab.shape[0] // CHUNK)], axis=0)
    lane = lax.broadcasted_iota(jnp.int32, ab.shape, 1)
    return jnp.where(lane < N_HEADS, gc, jax.nn.sigmoid(ab))


def fox_fcum_fn(fp, fbias):
    lf = jax.nn.log_sigmoid(fp + fbias)
    low = _lower_ones(LANES)
    carry = jnp.zeros((1, fp.shape[1]), F32)
    outs = []
    for i in range(fp.shape[0] // LANES):
        cs = hdot(low, lf[i * LANES:(i + 1) * LANES]) + carry
        carry = _last_row(cs)
        outs.append(cs)
    return jnp.concatenate(outs, axis=0)


def fox_qk_fn(qraw, kraw, qnw, knw):
    q = jnp.concatenate([rms(x, qnw) * SCALE for x in _heads(qraw, N_HEADS)], axis=1)
    k = jnp.concatenate([rms(x, knw) for x in _heads(kraw, N_HEADS)], axis=1)
    return q, k


def _mem_out(qm, mqw, mk, mv):
    return [mem_head(a, mqw, b, c) for a, b, c in zip(_heads(qm, MEM_HEADS), _heads(mk, MEM_HEADS), _heads(mv, MEM_HEADS))]


def dn_out_fn(o, z, qm, onw, mqw, mk, mv):
    mix = [rms(a, onw) * jax.nn.silu(b) for a, b in zip(o, _heads(z, N_HEADS))]
    return jnp.concatenate(mix + _mem_out(qm, mqw, mk, mv), axis=1)


def fox_out_fn(o, gate, qm, mqw, mk, mv):
    return jnp.concatenate([o * jax.nn.sigmoid(gate)] + _mem_out(qm, mqw, mk, mv), axis=1)


_HBM = pl.BlockSpec(memory_space=pltpu.HBM)


def _place():
    return lax.axis_index("x"), lax.axis_index("y"), lax.axis_index("c")


class Rider:
    def __init__(self, ins, out_shape, scratch, start, finish):
        self.ins, self.out_shape, self.scratch, self.start, self.finish = list(ins), list(out_shape), list(scratch), start, finish
        self.results = None


def gather_rider(xs):
    n = len(xs)

    def plan(x_refs, out_refs, sems):
        send_sems, recv_sems, local_sems = sems
        x, y, c = _place()
        me, sibling = (x, y, c), (x, y, 1 - c)
        chips = [(1 - x, y), (x, 1 - y), (1 - x, 1 - y)]

        def copy(a, k, block, to, src=None):
            px, py, pc = block
            dst = out_refs[a].at[4 * px + 2 * py + pc]
            return pltpu.make_async_remote_copy(
                src_ref=dst if src is None else src, dst_ref=dst,
                send_sem=send_sems.at[a, k], recv_sem=recv_sems.at[a, k], device_id=to, device_id_type=MESH)

        mine = [pltpu.make_async_copy(x_refs[a], out_refs[a].at[4 * x + 2 * y + c], local_sems.at[a]) for a in range(n)]
        first = [copy(a, 0, me, sibling, src=x_refs[a]) for a in range(n)]
        first += [copy(a, 1 + j, me, (*chip, c), src=x_refs[a]) for j, chip in enumerate(chips) for a in range(n)]
        return copy, me, sibling, chips, mine, first

    def start(x_refs, out_refs, sems):
        _, _, _, _, mine, first = plan(x_refs, out_refs, sems)
        for cp in mine + first:
            cp.start()

    def finish(x_refs, out_refs, sems):
        copy, me, sibling, chips, mine, first = plan(x_refs, out_refs, sems)
        _, _, c = me
        passed = []
        for j, chip in enumerate(chips):
            for a in range(n):
                copy(a, 1 + j, (*chip, c), me).wait_recv()
                passed.append(copy(a, 4 + j, (*chip, c), sibling))
                passed[-1].start()
        for a in range(n):
            copy(a, 0, sibling, me).wait_recv()
        for j, chip in enumerate(chips):
            for a in range(n):
                copy(a, 4 + j, (*chip, 1 - c), me).wait_recv()
        for cp in first + passed:
            cp.wait_send()
        for cp in mine:
            cp.wait()

    return Rider(xs, [jax.ShapeDtypeStruct((N_DEV,) + a.shape, a.dtype) for a in xs],
                 [pltpu.SemaphoreType.DMA((n, 7)), pltpu.SemaphoreType.DMA((n, 7)), pltpu.SemaphoreType.DMA((n,))], start, finish)


def sibling_rider(gs):
    n = len(gs)

    def plan(g_refs, out_refs, sems):
        send_sems, recv_sems = sems
        x, y, c = _place()
        return [pltpu.make_async_remote_copy(
            src_ref=g_refs[a].at[2 * k + 1 - c], dst_ref=out_refs[a].at[k], send_sem=send_sems.at[a, k],
            recv_sem=recv_sems.at[a, k], device_id=(x, y, 1 - c), device_id_type=MESH) for a in range(n) for k in range(4)]

    def start(g_refs, out_refs, sems):
        for cp in plan(g_refs, out_refs, sems):
            cp.start()

    def finish(g_refs, out_refs, sems):
        copies = plan(g_refs, out_refs, sems)
        for cp in copies:
            cp.wait_recv()
        for cp in copies:
            cp.wait_send()

    return Rider(gs, [jax.ShapeDtypeStruct((4,) + g.shape[1:], g.dtype) for g in gs],
                 [pltpu.SemaphoreType.DMA((n, 4)), pltpu.SemaphoreType.DMA((n, 4))], start, finish)


def chips_rider(hs):
    n = len(hs)

    def plan(h_refs, out_refs, sems):
        send_sems, recv_sems, local_sems = sems
        x, y, c = _place()
        mine = 2 * x + y
        chips = [(1 - x, y), (x, 1 - y), (1 - x, 1 - y)]
        keep = [pltpu.make_async_copy(h_refs[a].at[mine], out_refs[a].at[mine], local_sems.at[a]) for a in range(n)]
        sends = [pltpu.make_async_remote_copy(
            src_ref=h_refs[a].at[2 * qx + qy], dst_ref=out_refs[a].at[mine], send_sem=send_sems.at[a, j],
            recv_sem=recv_sems.at[a, j], device_id=(qx, qy, c), device_id_type=MESH)
            for j, (qx, qy) in enumerate(chips) for a in range(n)]
        recvs = [pltpu.make_async_remote_copy(
            src_ref=h_refs[a].at[mine], dst_ref=out_refs[a].at[2 * qx + qy], send_sem=send_sems.at[a, j],
            recv_sem=recv_sems.at[a, j], device_id=(qx, qy, c), device_id_type=MESH)
            for j, (qx, qy) in enumerate(chips) for a in range(n)]
        return keep, sends, recvs

    def start(h_refs, out_refs, sems):
        keep, sends, _ = plan(h_refs, out_refs, sems)
        for cp in keep + sends:
            cp.start()

    def finish(h_refs, out_refs, sems):
        keep, sends, recvs = plan(h_refs, out_refs, sems)
        for cp in recvs:
            cp.wait_recv()
        for cp in sends:
            cp.wait_send()
        for cp in keep:
            cp.wait()

    return Rider(hs, [jax.ShapeDtypeStruct(h.shape, h.dtype) for h in hs],
                 [pltpu.SemaphoreType.DMA((n, 3)), pltpu.SemaphoreType.DMA((n, 3)), pltpu.SemaphoreType.DMA((n,))], start, finish)


def hosted_call(riders, body, *, out_shape, in_specs, out_specs, grid=(), scratch_shapes=(), **kw):
    riders = tuple(riders or ())
    if not riders:
        return pl.pallas_call(body, out_shape=out_shape, in_specs=in_specs, out_specs=out_specs, grid=grid,
                              scratch_shapes=scratch_shapes, **kw)
    single = not isinstance(out_shape, (list, tuple))
    k_out_shape = [out_shape] if single else list(out_shape)
    k_out_specs = [out_specs] if single else list(out_specs)
    n_in, n_out, n_scr = len(in_specs), len(k_out_shape), len(scratch_shapes)
    r_ins = [a for r in riders for a in r.ins]
    r_outs = [s for r in riders for s in r.out_shape]
    r_scr = [s for r in riders for s in r.scratch]

    def full_body(*refs):
        ins = refs[:n_in + len(r_ins)]
        outs = refs[n_in + len(r_ins):n_in + len(r_ins) + n_out + len(r_outs)]
        scr = refs[n_in + len(r_ins) + n_out + len(r_outs):]
        steps = math.prod(grid)
        step = 0
        for d, g in enumerate(grid):
            step = step * g + pl.program_id(d)

        def each(method):
            i0, o0, s0 = n_in, n_out, n_scr
            for r in riders:
                getattr(r, method)(ins[i0:i0 + len(r.ins)], outs[o0:o0 + len(r.out_shape)], scr[s0:s0 + len(r.scratch)])
                i0, o0, s0 = i0 + len(r.ins), o0 + len(r.out_shape), s0 + len(r.scratch)

        if steps == 1:
            each("start")
            body(*ins[:n_in], *outs[:n_out], *scr[:n_scr])
            each("finish")
        else:
            pl.when(step == 0)(lambda: each("start"))
            body(*ins[:n_in], *outs[:n_out], *scr[:n_scr])
            pl.when(step == steps - 1)(lambda: each("finish"))

    call = pl.pallas_call(
        full_body, out_shape=k_out_shape + r_outs, in_specs=list(in_specs) + [_HBM] * len(r_ins),
        out_specs=k_out_specs + [_HBM] * len(r_outs), grid=grid, scratch_shapes=list(scratch_shapes) + r_scr, **kw)

    def run(*args):
        res = call(*args, *r_ins)
        o0 = n_out
        for r in riders:
            r.results = list(res[o0:o0 + len(r.out_shape)])
            o0 += len(r.out_shape)
        return res[0] if single else list(res[:n_out])

    return run


def run_riders(riders, *, name):
    hosted_call(riders, lambda: None, name=name, out_shape=[], in_specs=[], out_specs=[])()
    return [r.results for r in riders]


def _pick(n, cands):
    for c in cands:
        if n % c == 0:
            return c
    return n


def _params(sem):
    return pltpu.CompilerParams(dimension_semantics=sem, vmem_limit_bytes=VMEM_LIMIT)


MATMUL_VMEM_BUDGET = 40 * 2 ** 20


def _matmul_tiles(m, n, k, bytes_a, bytes_b, bytes_mn, fixed):
    fm, fn, fk = fixed if fixed is not None else (None, None, None)

    def options(given, size, cands):
        return [given] if given else ([c for c in cands if size % c == 0] or [size])

    best = None
    for tm in options(fm, m, (2048, 1024, 512, 256, 128)):
        for tn in options(fn, n, (512, 256, 128)):
            for tk in options(fk, k, (2048, 1536, 1024, 512, 256, 128)):
                if 2 * (tm * tk * bytes_a + tk * tn * bytes_b + tm * tn * bytes_mn) + tm * tn * 4 > MATMUL_VMEM_BUDGET:
                    continue
                key = ((m // tm) * (n // tn) * (k // tk), -tk)
                if best is None or key < best[0]:
                    best = (key, (tm, tn, tk))
    assert best is not None, (m, n, k, fixed)
    return best[1]


def matmul(a, b, *, name, ta=False, tb=False, post=None, post_ins=(), row_ins=(), acc=False, extra_out=None,
           out_dtype=F32, tiles=None, b_view=None, out_view=None, riders=()):
    (k, m) = a.shape if ta else a.shape[::-1]
    (kb, n) = b_view[:2] if b_view is not None else (b.shape[::-1] if tb else b.shape)
    assert k == kb, (a.shape, b.shape, ta, tb)
    bytes_mn = sum(p.dtype.itemsize for p in post_ins) + jnp.dtype(out_dtype).itemsize
    bytes_mn += jnp.dtype(extra_out[1]).itemsize if extra_out else 0
    tm, tn, tk = _matmul_tiles(m, n, k, a.dtype.itemsize, b.dtype.itemsize, bytes_mn, tiles)
    assert not acc or tn == n, (name, tn, n)
    nk = k // tk
    dims = ((0,) if ta else (1,), (1,) if tb else (0,))
    n_post, n_row = len(post_ins), len(row_ins)
    n_out = 1 + bool(extra_out) + bool(acc)

    def body(*refs):
        a_ref, b_ref = refs[:2]
        post_refs = refs[2:2 + n_post + n_row]
        o_refs, acc_ref = refs[-1 - n_out:-1], refs[-1]
        first_rows, kk = pl.program_id(0) == 0, pl.program_id(2)

        @pl.when(kk == 0)
        def _():
            acc_ref[...] = jnp.zeros_like(acc_ref)

        b_tile = b_ref[...]
        acc_ref[...] += _bdot(a_ref[...], b_tile.reshape(-1, b_tile.shape[-1]), dims)

        @pl.when(kk == nk - 1)
        def _():
            r = acc_ref[...]
            rows = [p[...] for p in post_refs[n_post:]]
            if post is not None:
                r = post(r, *[p[...] for p in post_refs[:n_post]], *rows)
            if acc:
                r, s = r
                sum_ref = o_refs[-1]

                @pl.when(first_rows)
                def _():
                    sum_ref[...] = s

                @pl.when(jnp.logical_not(first_rows))
                def _():
                    sum_ref[...] += s

            o_refs[0][...] = r.astype(out_dtype)
            if extra_out:
                o_refs[1][...] = extra_out[0](r, *rows).astype(extra_out[1])

    a_spec = pl.BlockSpec((tk, tm), lambda i, j, kk: (kk, i)) if ta else pl.BlockSpec((tm, tk), lambda i, j, kk: (i, kk))
    if b_view is not None:
        b_spec = b_view[2]
    else:
        b_spec = pl.BlockSpec((tn, tk), lambda i, j, kk: (j, kk)) if tb else pl.BlockSpec((tk, tn), lambda i, j, kk: (kk, j))
    mn_spec = pl.BlockSpec((tm, tn), lambda i, j, kk: (i, j))
    row_spec = pl.BlockSpec((1, tn), lambda i, j, kk: (0, j))
    o_shape, o_spec = ((m, n), mn_spec) if out_view is None else out_view
    out_shape = [jax.ShapeDtypeStruct(o_shape, out_dtype)]
    out_specs = [o_spec]
    if extra_out:
        out_shape.append(jax.ShapeDtypeStruct((m, n), extra_out[1]))
        out_specs.append(mn_spec)
    if acc:
        out_shape.append(jax.ShapeDtypeStruct((1, n), F32))
        out_specs.append(row_spec)
    res = hosted_call(
        riders, body, name=name, grid=(m // tm, n // tn, nk),
        in_specs=[a_spec, b_spec] + [mn_spec] * n_post + [row_spec] * n_row, out_specs=out_specs, out_shape=out_shape,
        scratch_shapes=[pltpu.VMEM((tm, tn), F32)],
        compiler_params=_params(("arbitrary" if acc else "parallel", "parallel", "arbitrary")),
    )(a, b, *post_ins, *row_ins)
    return res if n_out > 1 else res[0]


def rows_call(fn, row_ins, full_ins, row_outs, acc_outs, *, tm, name, riders=()):
    row_ins = [r if isinstance(r, tuple) else (r, r.shape[-1], 0) for r in row_ins]
    t = row_ins[0][0].shape[-2]
    tm = min(tm, t)
    n_in = len(row_ins) + len(full_ins)
    n_row = len(row_outs)

    def body(*refs):
        res = fn(*[[r[h] for h in range(r.shape[0])] if (i < len(row_ins) and len(r.shape) == 3) else r[...]
                   for i, r in enumerate(refs[:n_in])])
        res = res if isinstance(res, (tuple, list)) else (res,)
        outs = refs[n_in:]
        for ref, val in zip(outs[:n_row], res[:n_row]):
            if len(ref.shape) == 3:
                for h, vh in enumerate(val):
                    ref[h] = vh.astype(ref.dtype)
            else:
                ref[...] = val.astype(ref.dtype)
        first = pl.program_id(0) == 0
        for ref, val in zip(outs[n_row:], res[n_row:]):
            @pl.when(first)
            def _(ref=ref, val=val):
                ref[...] = val

            @pl.when(jnp.logical_not(first))
            def _(ref=ref, val=val):
                ref[...] += val

    def full_spec(shape):
        return pl.BlockSpec(shape, lambda i, nd=len(shape): (0,) * nd)

    def row_spec(lead, w, cb):
        if lead is None:
            return pl.BlockSpec((tm, w), lambda i: (i, cb))
        return pl.BlockSpec((lead, tm, w), lambda i: (0, i, cb))

    def lead_cols(c):
        return c if isinstance(c, tuple) else (None, c)

    in_specs = [row_spec(a.shape[0] if a.ndim == 3 else None, w, cb) for (a, w, cb) in row_ins]
    in_specs += [full_spec(f.shape) for f in full_ins]
    out_specs = [row_spec(*lead_cols(c), 0) for c, _ in row_outs] + [full_spec(s) for s in acc_outs]
    out_shape = [jax.ShapeDtypeStruct(tuple(d for d in (lead_cols(c)[0], t, lead_cols(c)[1]) if d is not None), dt)
                 for c, dt in row_outs] + [jax.ShapeDtypeStruct(s, F32) for s in acc_outs]
    res = hosted_call(
        riders, body, name=name, grid=(t // tm,), in_specs=in_specs, out_specs=out_specs, out_shape=out_shape,
        compiler_params=_params(("arbitrary",)),
    )(*[r[0] for r in row_ins], *full_ins)
    return res


def vjp_rows(fn, n_diff_row, row_diff_full):
    def bwd(*args, n_row, n_ct):
        prim_rows = args[:n_row]
        cts = args[n_row:n_row + n_ct]
        fulls = args[n_row + n_ct:]
        _, vjp = jax.vjp(fn, *prim_rows, *fulls)
        g = vjp(cts[0] if n_ct == 1 else tuple(cts))
        out = list(g[:n_diff_row])
        out += [gf for gf, d in zip(g[n_row:], row_diff_full) if d]
        return tuple(out)
    return bwd


def _shift_down(x, s):
    if s == 0:
        return x
    t = lax.broadcasted_iota(jnp.int32, x.shape, 0)
    return jnp.where(t >= s, pltpu.roll(x, s, 0), 0.0)


def _shift_up(x, s):
    if s == 0:
        return x
    n = x.shape[0]
    t = lax.broadcasted_iota(jnp.int32, x.shape, 0)
    return jnp.where(t < n - s, pltpu.roll(x, n - s, 0), 0.0)


def _conv(x, w_ref):
    return sum(w_ref[pl.ds(j, 1), :] * _shift_down(x, CONV_WIDTH - 1 - j) for j in range(CONV_WIDTH))


_DN_POST = (lambda c: l2n(jax.nn.silu(c)) * SCALE, lambda c: l2n(jax.nn.silu(c)), jax.nn.silu)


def dn_prep_fwd(proj, conv_w, *, name, riders=()):
    t = proj.shape[0]

    def body(xq, xk, xv, wq, wk, wv, oq, ok, ov):
        for x_ref, w_ref, o_ref, post in zip((xq, xk, xv), (wq, wk, wv), (oq, ok, ov), _DN_POST):
            o_ref[...] = post(_conv(x_ref[...], w_ref))

    x_specs = [pl.BlockSpec((t, HEAD_DIM), lambda h, g=g: (0, g * N_HEADS + h)) for g in range(3)]
    w_specs = [pl.BlockSpec((CONV_WIDTH, HEAD_DIM), lambda h, g=g: (0, g * N_HEADS + h)) for g in range(3)]
    o_spec = pl.BlockSpec((None, t, HEAD_DIM), lambda h: (h, 0, 0))
    return hosted_call(
        riders, body, name=name, grid=(N_HEADS,), in_specs=x_specs + w_specs, out_specs=[o_spec] * 3,
        out_shape=[jax.ShapeDtypeStruct((N_HEADS, t, HEAD_DIM), F32)] * 3, compiler_params=_params(("parallel",)),
    )(proj, proj, proj, conv_w, conv_w, conv_w)


def dn_prep_bwd(proj, conv_w, dq, dk, dv, *, name, riders=()):
    t = proj.shape[0]

    def body(xq, xk, xv, wq, wk, wv, gq, gk, gv, dxq, dxk, dxv, dwq, dwk, dwv):
        for x_ref, w_ref, g_ref, dx_ref, dw_ref, post in zip(
                (xq, xk, xv), (wq, wk, wv), (gq, gk, gv), (dxq, dxk, dxv), (dwq, dwk, dwv), _DN_POST):
            x = x_ref[...]
            _, vjp = jax.vjp(post, _conv(x, w_ref))
            dc, = vjp(g_ref[...])
            dx = sum(w_ref[pl.ds(j, 1), :] * _shift_up(dc, CONV_WIDTH - 1 - j) for j in range(CONV_WIDTH))
            dx_ref[...] = dx.astype(dx_ref.dtype)
            for j in range(CONV_WIDTH):
                dw_ref[pl.ds(j, 1), :] = jnp.sum(dc * _shift_down(x, CONV_WIDTH - 1 - j), axis=0, keepdims=True)

    x_specs = [pl.BlockSpec((t, HEAD_DIM), lambda h, g=g: (0, g * N_HEADS + h)) for g in range(3)]
    w_specs = [pl.BlockSpec((CONV_WIDTH, HEAD_DIM), lambda h, g=g: (0, g * N_HEADS + h)) for g in range(3)]
    g_spec = pl.BlockSpec((None, t, HEAD_DIM), lambda h: (h, 0, 0))
    dx_spec = pl.BlockSpec((t, HEAD_DIM), lambda h: (0, h))
    dw_spec = pl.BlockSpec((CONV_WIDTH, HEAD_DIM), lambda h: (0, h))
    return hosted_call(
        riders, body, name=name, grid=(N_HEADS,), in_specs=x_specs + w_specs + [g_spec] * 3, out_specs=[dx_spec] * 3 + [dw_spec] * 3,
        out_shape=[jax.ShapeDtypeStruct((t, D_MODEL), BF16)] * 3 + [jax.ShapeDtypeStruct((CONV_WIDTH, D_MODEL), F32)] * 3,
        compiler_params=_params(("parallel",)),
    )(proj, proj, proj, conv_w, conv_w, conv_w, dq, dk, dv)


INTRA_CHUNKS = 4


def _lane_column(x, lane_index):
    lane = lax.broadcasted_iota(jnp.int32, x.shape, 1)
    return jnp.sum(jnp.where(lane == lane_index, x, 0.0), axis=1, keepdims=True)


def _head_columns(g, first_lane):
    return jnp.concatenate([_lane_column(g, first_lane + h)[None] for h in range(N_HEADS)], axis=0)


def _intra_of_gates(q, k, v, gates):
    nb = N_HEADS * (gates.shape[0] // CHUNK)

    def chunks(x):
        return x.reshape(nb, CHUNK, x.shape[-1])

    res = delta_intra(chunks(q), chunks(k), chunks(v), chunks(_head_columns(gates, 0)), chunks(_head_columns(gates, N_HEADS)))
    return tuple(x.reshape(N_HEADS, -1, x.shape[-1]) for x in res)


def _step_of_gates(s, q, k, gates, u, w, qk):
    return delta_step(s, q, k, _head_columns(gates, 0), u, w, qk)


def _head_major(rows, w, index):
    return pl.BlockSpec((N_HEADS, rows, w), lambda i: (0, index(i), 0))


def delta_intra_fwd(q, k, v, gates, *, name, riders=()):
    t = q.shape[1]
    rows = min(INTRA_CHUNKS, t // CHUNK) * CHUNK

    def body(q_ref, k_ref, v_ref, g_ref, u_ref, w_ref, qk_ref):
        for ref, val in zip((u_ref, w_ref, qk_ref), _intra_of_gates(q_ref[...], k_ref[...], v_ref[...], g_ref[...])):
            ref[...] = val

    x_spec, qk_spec = (_head_major(rows, w, lambda i: i) for w in (HEAD_DIM, CHUNK))
    g_spec = pl.BlockSpec((rows, LANES), lambda i: (i, 0))
    return hosted_call(
        riders, body, name=name, grid=(t // rows,), in_specs=[x_spec] * 3 + [g_spec], out_specs=[x_spec, x_spec, qk_spec],
        out_shape=[jax.ShapeDtypeStruct((N_HEADS, t, HEAD_DIM), F32)] * 2 + [jax.ShapeDtypeStruct((N_HEADS, t, CHUNK), F32)],
        compiler_params=_params(("parallel",)),
    )(q, k, v, gates)


def delta_seq_fwd(q, k, gates, u, w, qk, *, name, riders=()):
    t = q.shape[1]
    nc = t // CHUNK

    def body(q_ref, k_ref, g_ref, u_ref, w_ref, qk_ref, o_ref, s0_ref, s_ref):
        @pl.when(pl.program_id(0) == 0)
        def _():
            s_ref[...] = jnp.zeros_like(s_ref)

        s = s_ref[...]
        s0_ref[...] = s
        o, s_new = _step_of_gates(s, q_ref[...], k_ref[...], g_ref[...], u_ref[...], w_ref[...], qk_ref[...])
        o_ref[...] = o
        s_ref[...] = s_new

    x_spec, qk_spec = (_head_major(CHUNK, w, lambda c: c) for w in (HEAD_DIM, CHUNK))
    g_spec = pl.BlockSpec((CHUNK, LANES), lambda c: (c, 0))
    s_spec = pl.BlockSpec((N_HEADS, None, HEAD_DIM, HEAD_DIM), lambda c: (0, c, 0, 0))
    return hosted_call(
        riders, body, name=name, grid=(nc,), in_specs=[x_spec, x_spec, g_spec, x_spec, x_spec, qk_spec], out_specs=[x_spec, s_spec],
        out_shape=[jax.ShapeDtypeStruct((N_HEADS, t, HEAD_DIM), F32),
                   jax.ShapeDtypeStruct((N_HEADS, nc, HEAD_DIM, HEAD_DIM), F32)],
        scratch_shapes=[pltpu.VMEM((N_HEADS, HEAD_DIM, HEAD_DIM), F32)],
        compiler_params=_params(("arbitrary",)),
    )(q, k, gates, u, w, qk)


def delta_seq_bwd(q, k, gates, u, w, qk, s0, do, *, name, riders=()):
    t = q.shape[1]
    nc = t // CHUNK

    def body(q_ref, k_ref, g_ref, u_ref, w_ref, qk_ref, s0_ref, do_ref,
             dq_ref, dk_ref, dg_ref, du_ref, dw_ref, dqk_ref, ds_ref):
        @pl.when(pl.program_id(0) == 0)
        def _():
            ds_ref[...] = jnp.zeros_like(ds_ref)

        _, vjp = jax.vjp(_step_of_gates, s0_ref[...], q_ref[...], k_ref[...], g_ref[...], u_ref[...], w_ref[...], qk_ref[...])
        ds, dq, dk, dg, du, dw, dqk = vjp((do_ref[...], ds_ref[...]))
        for ref, val in zip((ds_ref, dq_ref, dk_ref, dg_ref, du_ref, dw_ref, dqk_ref), (ds, dq, dk, dg, du, dw, dqk)):
            ref[...] = val

    x_spec, qk_spec = (_head_major(CHUNK, w, lambda c: nc - 1 - c) for w in (HEAD_DIM, CHUNK))
    g_spec = pl.BlockSpec((CHUNK, LANES), lambda c: (nc - 1 - c, 0))
    s_spec = pl.BlockSpec((N_HEADS, None, HEAD_DIM, HEAD_DIM), lambda c: (0, nc - 1 - c, 0, 0))
    head_shape = [jax.ShapeDtypeStruct((N_HEADS, t, w_), F32) for w_ in (HEAD_DIM, HEAD_DIM, HEAD_DIM, HEAD_DIM, CHUNK)]
    return hosted_call(
        riders, body, name=name, grid=(nc,), in_specs=[x_spec, x_spec, g_spec, x_spec, x_spec, qk_spec, s_spec, x_spec],
        out_specs=[x_spec, x_spec, g_spec, x_spec, x_spec, qk_spec],
        out_shape=head_shape[:2] + [jax.ShapeDtypeStruct((t, LANES), F32)] + head_shape[2:],
        scratch_shapes=[pltpu.VMEM((N_HEADS, HEAD_DIM, HEAD_DIM), F32)],
        compiler_params=_params(("arbitrary",)),
    )(q, k, gates, u, w, qk, s0, do)


def delta_intra_bwd(q, k, v, gates, du, dw, dqk, dq_s, dk_s, dg_s, *, name, riders=()):
    t = q.shape[1]
    rows = min(INTRA_CHUNKS, t // CHUNK) * CHUNK

    def body(q_ref, k_ref, v_ref, g_ref, du_ref, dw_ref, dqk_ref, dqs_ref, dks_ref, dgs_ref, dq_ref, dk_ref, dv_ref, dg_ref):
        _, vjp = jax.vjp(_intra_of_gates, q_ref[...], k_ref[...], v_ref[...], g_ref[...])
        dq, dk, dv, dg = vjp((du_ref[...], dw_ref[...], dqk_ref[...]))
        dq_ref[...] = dq + dqs_ref[...]
        dk_ref[...] = dk + dks_ref[...]
        dv_ref[...] = dv
        dg_ref[...] = dg + dgs_ref[...]

    x_spec, qk_spec = (_head_major(rows, w, lambda i: i) for w in (HEAD_DIM, CHUNK))
    g_spec = pl.BlockSpec((rows, LANES), lambda i: (i, 0))
    return hosted_call(
        riders, body, name=name, grid=(t // rows,),
        in_specs=[x_spec] * 3 + [g_spec, x_spec, x_spec, qk_spec, x_spec, x_spec, g_spec],
        out_specs=[x_spec] * 3 + [g_spec],
        out_shape=[jax.ShapeDtypeStruct((N_HEADS, t, HEAD_DIM), F32)] * 3 + [jax.ShapeDtypeStruct((t, LANES), F32)],
        compiler_params=_params(("parallel",)),
    )(q, k, v, gates, du, dw, dqk, dq_s, dk_s, dg_s)


_V_BLOCK = 2 * N_HEADS
FOX_GROUPS = 16


def _fox_groups(t):
    nq = t // Q_BLOCK
    per = max(1, nq // FOX_GROUPS)
    return [(g0, per, (g0 + per) * Q_BLOCK) for g0 in range(0, nq, per)]


def fox_attn_fwd(q, k, proj, fq, fk, *, name, riders=()):
    t = q.shape[0]

    def body(q_ref, k_ref, v_ref, fq_ref, fk_ref, o_ref, kb_ref, vb_ref):
        head = pl.program_id(0)
        kb_ref[...] = k_ref[...].astype(BF16)
        vb_ref[...] = v_ref[...].astype(BF16)
        for g0, per, keys in _fox_groups(t):
            def block(j, carry, g0=g0, keys=keys):
                rows = pl.ds((g0 + j) * Q_BLOCK, Q_BLOCK)
                p = fox_probs(q_ref[rows, :].astype(BF16), kb_ref[0:keys, :], _lane_column(fq_ref[rows, :], head),
                              fk_ref[:, 0:keys], (g0 + j) * Q_BLOCK)
                o_ref[rows, :] = jnp.dot(p.astype(BF16), vb_ref[0:keys, :], preferred_element_type=F32)
                return carry
            for j in range(per):
                block(j, 0)

    x_spec = pl.BlockSpec((t, HEAD_DIM), lambda h: (0, h))
    v_spec = pl.BlockSpec((t, HEAD_DIM), lambda h: (0, _V_BLOCK + h))
    fq_spec = pl.BlockSpec((t, LANES), lambda h: (0, 0))
    fk_spec = pl.BlockSpec((None, 1, t), lambda h: (h, 0, 0))
    return hosted_call(
        riders, body, name=name, grid=(N_HEADS,), in_specs=[x_spec, x_spec, v_spec, fq_spec, fk_spec], out_specs=x_spec,
        out_shape=jax.ShapeDtypeStruct((t, D_MODEL), F32), scratch_shapes=[pltpu.VMEM((t, HEAD_DIM), BF16)] * 2,
        compiler_params=_params(("parallel",)),
    )(q, k, proj, fq, fk)


def fox_attn_bwd(q, k, proj, fq, fk, do, *, name, riders=()):
    t = q.shape[0]

    def body(q_ref, k_ref, v_ref, fq_ref, fk_ref, do_ref, dq_ref, dk_ref, dv_out_ref, dfq_ref, dfk_ref, kb_ref, vb_ref, dv_ref):
        head = pl.program_id(0)

        @pl.when(head == 0)
        def _():
            dfq_ref[...] = jnp.zeros_like(dfq_ref)

        kb_ref[...] = k_ref[...].astype(BF16)
        vb_ref[...] = v_ref[...].astype(BF16)
        dk_ref[...] = jnp.zeros_like(dk_ref)
        dv_ref[...] = jnp.zeros_like(dv_ref)
        dfk_ref[...] = jnp.zeros_like(dfk_ref)
        nt = (((1,), (1,)), ((), ()))
        tn = (((0,), (0,)), ((), ()))
        for g0, per, keys in _fox_groups(t):
            def block(j, carry, g0=g0, keys=keys):
                rows = pl.ds((g0 + j) * Q_BLOCK, Q_BLOCK)
                qb, dob = q_ref[rows, :].astype(BF16), do_ref[rows, :].astype(BF16)
                kb, vb = kb_ref[0:keys, :], vb_ref[0:keys, :]
                p = fox_probs(qb, kb, _lane_column(fq_ref[rows, :], head), fk_ref[:, 0:keys], (g0 + j) * Q_BLOCK)
                dp = lax.dot_general(dob, vb, nt, preferred_element_type=F32)
                dz = p * (dp - jnp.sum(dp * p, axis=-1, keepdims=True))
                pb, dzb = p.astype(BF16), dz.astype(BF16)
                dq_ref[rows, :] = jnp.dot(dzb, kb, preferred_element_type=F32)
                lane = lax.broadcasted_iota(jnp.int32, (Q_BLOCK, LANES), 1)
                dfq_ref[rows, :] += jnp.where(lane == head, jnp.sum(dz, axis=-1, keepdims=True), 0.0)
                dk_ref[0:keys, :] += lax.dot_general(dzb, qb, tn, preferred_element_type=F32)
                dv_ref[0:keys, :] += lax.dot_general(pb, dob, tn, preferred_element_type=F32)
                dfk_ref[:, 0:keys] -= jnp.sum(dz, axis=0, keepdims=True)
                return carry
            for j in range(per):
                block(j, 0)
        dv_out_ref[...] = dv_ref[...].astype(dv_out_ref.dtype)

    x_spec = pl.BlockSpec((t, HEAD_DIM), lambda h: (0, h))
    v_spec = pl.BlockSpec((t, HEAD_DIM), lambda h: (0, _V_BLOCK + h))
    fq_spec = pl.BlockSpec((t, LANES), lambda h: (0, 0))
    fk_spec = pl.BlockSpec((None, 1, t), lambda h: (h, 0, 0))
    return hosted_call(
        riders, body, name=name, grid=(N_HEADS,), in_specs=[x_spec, x_spec, v_spec, fq_spec, fk_spec, x_spec],
        out_specs=[x_spec, x_spec, x_spec, fq_spec, fk_spec],
        out_shape=[jax.ShapeDtypeStruct((t, D_MODEL), F32)] * 2 + [jax.ShapeDtypeStruct((t, D_MODEL), BF16)]
        + [jax.ShapeDtypeStruct((t, LANES), F32), jax.ShapeDtypeStruct((N_HEADS, 1, t), F32)],
        scratch_shapes=[pltpu.VMEM((t, HEAD_DIM), BF16)] * 2 + [pltpu.VMEM((t, HEAD_DIM), F32)],
        compiler_params=_params(("arbitrary",)),
    )(q, k, proj, fq, fk, do)


def memkv_fwd(mem, mnw, wkv, mknw, *, name):
    n = mem.shape[0]

    def body(mem_ref, mnw_ref, w_ref, mknw_ref, mk_ref, mv_ref):
        mk, mv = memkv_fn(mem_ref[...], mnw_ref[...], w_ref[...], mknw_ref[...])
        mk_ref[...] = mk
        mv_ref[...] = mv

    return pl.pallas_call(
        body, name=name, out_shape=[jax.ShapeDtypeStruct((n, MEM_WIDTH), F32)] * 2,
        compiler_params=pltpu.CompilerParams(vmem_limit_bytes=VMEM_LIMIT),
    )(mem, mnw, wkv, mknw)


def memkv_bwd(mem, mnw, wkv, mknw, dmk, dmv, *, name):
    def body(mem_ref, mnw_ref, w_ref, mknw_ref, dmk_ref, dmv_ref, dmnw_ref, dw_ref, dmknw_ref):
        f = functools.partial(memkv_fn, mem_ref[...])
        _, vjp = jax.vjp(f, mnw_ref[...], w_ref[...].astype(F32), mknw_ref[...])
        dmnw, dw, dmknw = vjp((dmk_ref[...], dmv_ref[...]))
        dmnw_ref[...] = dmnw
        dw_ref[...] = dw.astype(dw_ref.dtype)
        dmknw_ref[...] = dmknw

    return pl.pallas_call(
        body, name=name,
        out_shape=[jax.ShapeDtypeStruct(mnw.shape, F32), jax.ShapeDtypeStruct(wkv.shape, BF16), jax.ShapeDtypeStruct(mknw.shape, F32)],
        compiler_params=pltpu.CompilerParams(vmem_limit_bytes=VMEM_LIMIT),
    )(mem, mnw, wkv, mknw, dmk, dmv)


def _row(v, width=None):
    v = v.reshape(1, -1)
    if width is not None and v.shape[1] < width:
        v = jnp.pad(v, ((0, 0), (0, width - v.shape[1])))
    return v


def _norm_fwd(x, w, name, riders=()):
    return rows_call(lambda x, w: rms(x, w), [x], [w], [(D_MODEL, BF16)], [], tm=512, name=name, riders=riders)[0]


FF_PIECE = D_FF // N_DEV


def _add(r, x, *rows):
    return r + x


def _norm_rows(r, w):
    return rms(r, w)


def _norm_bwd_post(dh, x, dx_in, w):
    _, vjp = jax.vjp(rms, x, w)
    dx, dw = vjp(dh)
    return dx + dx_in, dw


def _piece(rows, cols, index):
    return pl.BlockSpec((None, rows, cols), lambda i, j, kk: (index(i, j, kk), 0, 0))


def _two_pieces(rows, cols, index):
    return pl.BlockSpec((2, rows, cols), lambda i, j, kk: (index(i, j, kk), 0, 0))


def _mlp_fwd(x, h2, w1, w2, layer, riders=(), next_norm_w=None):
    riders = list(riders) + [None, None]
    u, a1 = matmul(h2, w1, name=f"mlp1_fwd_{layer}", tiles=(None, FF_PIECE, D_MODEL),
                   extra_out=(lambda u: jnp.square(jnp.maximum(u, 0.0)), BF16),
                   b_view=(D_MODEL, D_FF, _piece(D_MODEL, FF_PIECE, lambda i, j, kk: j)), riders=riders[0])
    norm = dict(row_ins=[next_norm_w], extra_out=(_norm_rows, BF16)) if next_norm_w is not None else {}
    y = matmul(a1, w2, name=f"mlp2_fwd_{layer}", post=_add, post_ins=[x], tiles=(None, D_MODEL, 2 * FF_PIECE),
               b_view=(D_FF, D_MODEL, _two_pieces(FF_PIECE, D_MODEL, lambda i, j, kk: kk)), riders=riders[1], **norm)
    return y, (x, h2, u, a1)


def pair_sum(g, got, *, name):
    _, rows, cols = g.shape
    tile = _pick(rows, (512, 256, 128))
    c = lax.axis_index("c").astype(jnp.int32).reshape(1)

    def body(c_ref, a_ref, b_ref, o_ref):
        o_ref[...] = (a_ref[...].astype(F32) + b_ref[...].astype(F32)).astype(o_ref.dtype)

    grid_spec = pltpu.PrefetchScalarGridSpec(
        num_scalar_prefetch=1, grid=(4, rows // tile),
        in_specs=[pl.BlockSpec((None, tile, cols), lambda k, i, c_ref: (2 * k + c_ref[0], i, 0)),
                  pl.BlockSpec((None, tile, cols), lambda k, i, c_ref: (k, i, 0))],
        out_specs=pl.BlockSpec((None, tile, cols), lambda k, i, c_ref: (k, i, 0)))
    return pl.pallas_call(
        body, name=name, grid_spec=grid_spec, out_shape=jax.ShapeDtypeStruct((4, rows, cols), g.dtype),
        compiler_params=_params(("parallel", "parallel")),
    )(c, g, got)


def chip_sums(names, pieces, gots):
    return [pair_sum(a, got, name=f"grads_pair_sum_{n}") for n, a, got in zip(names, pieces, gots)]


def _mlp_bwd(dy, res, n2w, w1, w2, layer, riders=()):
    x, h2, u, a1 = res
    du = matmul(dy, w2, tb=True, name=f"mlp2_dx_{layer}", out_dtype=BF16, tiles=(None, 2 * FF_PIECE, D_MODEL),
                post=lambda r, u: r * (2.0 * jnp.maximum(u, 0.0)), post_ins=[u],
                b_view=(D_MODEL, D_FF, _two_pieces(FF_PIECE, D_MODEL, lambda i, j, kk: j)), riders=riders)
    dw2 = matmul(a1, dy, ta=True, name=f"mlp2_dw_{layer}", out_dtype=BF16, tiles=(FF_PIECE, D_MODEL, None), out_view=(
        w2.shape, _piece(FF_PIECE, D_MODEL, lambda i, j, kk: i)))
    sib2 = sibling_rider([dw2])
    dx, dn2w = matmul(du, w1, tb=True, name=f"mlp1_dx_{layer}", tiles=(None, D_MODEL, FF_PIECE),
                      b_view=(D_FF, D_MODEL, _piece(D_MODEL, FF_PIECE, lambda i, j, kk: kk)),
                      post=_norm_bwd_post, post_ins=[x, dy], row_ins=[n2w], acc=True, riders=[sib2])
    dw1 = matmul(h2, du, ta=True, name=f"mlp1_dw_{layer}", out_dtype=BF16, tiles=(D_MODEL, FF_PIECE, None), out_view=(
        w1.shape, _piece(D_MODEL, FF_PIECE, lambda i, j, kk: j)))
    return dx, dw1, dw2, dn2w, sibling_rider([dw1]), sib2


def _in_proj_bwd(h, dmain, dsmall, w_main, w_small, x, dx_in, n1w, tag):
    dh = matmul(dmain, w_main, tb=True, name=f"inproj_dx_main_{tag}")

    def post(r, dh_main, x, dx_in, w):
        return _norm_bwd_post(r + dh_main, x, dx_in, w)

    dx, dn1w = matmul(dsmall, w_small, tb=True, name=f"inproj_dx_small_{tag}", tiles=(None, D_MODEL, None),
                      post=post, post_ins=[dh, x, dx_in], row_ins=[n1w], acc=True)
    t, piece = h.shape[0], MEM_WIDTH
    dw_main = [matmul(h, dmain, ta=True, out_dtype=BF16, name=f"inproj_dw_{part}_{tag}", tiles=(None, piece, t),
                      b_view=(t, cols, pl.BlockSpec((t, piece), lambda i, j, kk, first=first: (0, first + j))))
               for part, cols, first in (("main", GATE_END, 0), ("memq", MEM_WIDTH, GATE_END // piece))]
    dw_small = matmul(h, dsmall, ta=True, out_dtype=BF16, name=f"inproj_dw_small_{tag}")
    return dx, dn1w, dw_main, dw_small


def local_step(x, mem, target, w, m, v):
    t = x.shape[0]
    n_mem = mem.shape[0]
    g = {}

    def wire(a):
        return a.astype(BF16)

    (dn_g,), = run_riders([gather_rider([wire(w["dn_w_in"][0])])], name="weights_gather_first")
    dn_main, dn_ab = in_proj_weights(dn_g, DN_IN, 2 * N_HEADS)
    fox_w = wire(w["fox_w_in"][0])
    ride_out = gather_rider([wire(w["w_out"][0]), w["dn_conv_w"][0]])
    ride_out_1 = gather_rider([wire(w["w_out"][1])])
    ride_kv = gather_rider([wire(w["w_mem_kv"])])
    ride_mlp1_0 = gather_rider([wire(w["w_mlp1"][0])])
    ride_mlp2_0 = gather_rider([wire(w["w_mlp2"][0])])
    ride_fox_a, ride_fox_b = gather_rider([fox_w[:D_MODEL // 2]]), gather_rider([fox_w[D_MODEL // 2:]])
    ride_mlp_1 = gather_rider([wire(w["w_mlp1"][1]), wire(w["w_mlp2"][1])])
    mnw, mknw = _row(w["mem_norm_w"]), _row(w["mem_k_norm_w"])

    n1w0, n2w0 = _row(w["norm1_w"][0]), _row(w["norm2_w"][0])
    n1w1, n2w1 = _row(w["norm1_w"][1]), _row(w["norm2_w"][1])
    alog, dtb = _row(w["dn_a_log"][0], LANES), _row(w["dn_dt_bias"][0], LANES)
    onw, mqw0 = _row(w["dn_o_norm_w"][0]), _row(w["memq_norm_w"][0])
    x0 = x
    h0 = _norm_fwd(x0, n1w0, "norm1_fwd_0")
    pm0 = matmul(h0, dn_main, name="inproj_main_0", riders=[ride_out])
    w_out0 = ride_out.results[0].reshape(OUT_IN, D_MODEL)
    conv_w = ride_out.results[1].transpose(1, 0, 2).reshape(CONV_WIDTH, 3 * D_MODEL)
    ps0 = matmul(h0, dn_ab, name="inproj_small_0")
    gates = rows_call(dn_gates_fn, [ps0], [alog, dtb], [(LANES, F32)], [], tm=512, name="dn_gates_fwd")[0]
    q0, k0, v0 = dn_prep_fwd(pm0, conv_w, name="dn_prep_fwd", riders=[ride_kv])
    w_kv = ride_kv.results[0].reshape(D_MODEL, D_MODEL)
    mk, mv = memkv_fwd(mem, mnw, w_kv, mknw, name="memkv_fwd")
    u0, w0, qk0 = delta_intra_fwd(q0, k0, v0, gates, name="delta_intra_fwd", riders=[ride_mlp1_0])
    o0, s_start = delta_seq_fwd(q0, k0, gates, u0, w0, qk0, name="delta_seq_fwd", riders=[ride_mlp2_0])
    cat0 = rows_call(dn_out_fn, [o0, (pm0, D_MODEL, 3), (pm0, MEM_WIDTH, 8)], [onw, mqw0, mk, mv],
                     [(D_MODEL + MEM_WIDTH, BF16)], [], tm=256, name="dn_out_fwd")[0]
    (w1_0,), (w2_0,) = ride_mlp1_0.results, ride_mlp2_0.results
    x1, h2_0 = matmul(cat0, w_out0, post=_add, post_ins=[x0], row_ins=[n2w0], extra_out=(_norm_rows, BF16),
                      tiles=(None, D_MODEL, None), name="wout_fwd_0")
    (x2, h1), mlp_res0 = _mlp_fwd(x1, h2_0, w1_0, w2_0, 0, riders=[[ride_fox_a], [ride_fox_b]], next_norm_w=n1w1)
    fox_main, fox_f = in_proj_weights(
        jnp.concatenate([ride_fox_a.results[0], ride_fox_b.results[0]], axis=1), FOX_IN, N_HEADS)

    fbias = _row(w["fox_f_bias"][0], LANES)
    qnw, knw, mqw1 = _row(w["fox_q_norm_w"][0]), _row(w["fox_k_norm_w"][0]), _row(w["memq_norm_w"][1])
    pm1 = matmul(h1, fox_main, name="inproj_main_1", riders=[ride_out_1])
    w_out1 = ride_out_1.results[0].reshape(OUT_IN, D_MODEL)
    ps1 = matmul(h1, fox_f, name="inproj_small_1")
    fq = rows_call(fox_fcum_fn, [ps1], [fbias], [(LANES, F32)], [], tm=t, name="fox_fcum_fwd")[0]
    fk = fq[:, :N_HEADS].T[:, None, :]
    q1, k1 = rows_call(fox_qk_fn, [(pm1, D_MODEL, 0), (pm1, D_MODEL, 1)], [qnw, knw], [(D_MODEL, F32)] * 2, [], tm=256,
                       name="fox_qk_fwd")
    o1 = fox_attn_fwd(q1, k1, pm1, fq, fk, name="fox_attn_fwd", riders=[ride_mlp_1])
    cat1 = rows_call(fox_out_fn, [o1, (pm1, D_MODEL, 3), (pm1, MEM_WIDTH, 8)], [mqw1, mk, mv],
                     [(D_MODEL + MEM_WIDTH, BF16)], [], tm=256, name="fox_out_fwd")[0]
    w1_1, w2_1 = ride_mlp_1.results
    x3, h2_1 = matmul(cat1, w_out1, post=_add, post_ins=[x2], row_ins=[n2w1], extra_out=(_norm_rows, BF16),
                      tiles=(None, D_MODEL, None), name="wout_fwd_1")
    y, mlp_res1 = _mlp_fwd(x3, h2_1, w1_1, w2_1, 1)

    def loss_fn(y, tgt):
        e = y - tgt
        return e * (1.0 / D_MODEL), jnp.sum(jnp.sum(e * e, axis=1, keepdims=True), axis=0, keepdims=True)
    dy, sq = rows_call(loss_fn, [y, target], [], [(D_MODEL, F32)], [(1, 1)], tm=512, name="loss")
    loss = sq[0, 0] * (0.5 / D_MODEL)

    dx3, dw1_1, dw2_1, dn2w1, sib1, sib2 = _mlp_bwd(dy, mlp_res1, n2w1, w1_1, w2_1, 1)
    dcat1 = matmul(dx3, w_out1, tb=True, name="wout_dx_1", riders=[sib1])
    dwo_1 = matmul(cat1, dx3, ta=True, out_dtype=BF16, name="wout_dw_1").reshape(N_DEV, OUT_IN // N_DEV, D_MODEL)
    sibo = sibling_rider([dwo_1])
    do1, dgate1, dqm1, dmqw1, dmk1, dmv1 = rows_call(
        functools.partial(vjp_rows(fox_out_fn, 3, (True, True, True)), n_row=3, n_ct=1),
        [o1, (pm1, D_MODEL, 3), (pm1, MEM_WIDTH, 8), dcat1], [mqw1, mk, mv],
        [(D_MODEL, F32), (D_MODEL, BF16), (MEM_WIDTH, BF16)], [(1, HEAD_DIM), (n_mem, MEM_WIDTH), (n_mem, MEM_WIDTH)],
        tm=256, name="fox_out_bwd", riders=[sibo])
    ride_l1 = chips_rider(chip_sums(["w_mlp2_1", "w_mlp1_1", "w_out_1"], [dw2_1, dw1_1, dwo_1],
                                    sib2.results + sib1.results + sibo.results))
    dq1, dk1, dv1, dfq, dfk = fox_attn_bwd(q1, k1, pm1, fq, fk, do1, name="fox_attn_bwd", riders=[ride_l1])
    dqraw1, dkraw1, dqnw, dknw = rows_call(
        functools.partial(vjp_rows(fox_qk_fn, 2, (True, True)), n_row=2, n_ct=2),
        [(pm1, D_MODEL, 0), (pm1, D_MODEL, 1), dq1, dk1], [qnw, knw],
        [(D_MODEL, BF16)] * 2, [(1, HEAD_DIM)] * 2, tm=256, name="fox_qk_bwd")
    dfcum = dfq + jnp.pad(dfk[:, 0, :].T, ((0, 0), (0, LANES - N_HEADS)))
    dps1, dfbias = rows_call(
        functools.partial(vjp_rows(fox_fcum_fn, 1, (True,)), n_row=1, n_ct=1),
        [ps1, dfcum], [fbias], [(LANES, F32)], [(1, LANES)], tm=t, name="fox_fcum_bwd")
    dpm1 = jnp.concatenate([dqraw1, dkraw1, dv1, dgate1, dqm1], axis=1)
    dx2, dn1w1, dwmain1, dwsmall1 = _in_proj_bwd(h1, dpm1, dps1, fox_main, fox_f, x2, dx3, n1w1, "1")
    g_fox = in_proj_pieces(dwmain1, dwsmall1, N_HEADS, FOX_IN)
    sibf = sibling_rider([g_fox])

    dx1, dw1_0, dw2_0, dn2w0, sib1, sib2 = _mlp_bwd(dx2, mlp_res0, n2w0, w1_0, w2_0, 0, riders=[sibf])
    ride_fox_g = chips_rider(chip_sums(["fox_w_in"], [g_fox], sibf.results))
    dcat0 = matmul(dx1, w_out0, tb=True, name="wout_dx_0", riders=[sib1])
    dwo_0 = matmul(cat0, dx1, ta=True, out_dtype=BF16, name="wout_dw_0").reshape(N_DEV, OUT_IN // N_DEV, D_MODEL)
    sibo = sibling_rider([dwo_0])
    do0, dz0, dqm0, donw, dmqw0, dmk0, dmv0 = rows_call(
        functools.partial(vjp_rows(dn_out_fn, 3, (True, True, True, True)), n_row=3, n_ct=1),
        [o0, (pm0, D_MODEL, 3), (pm0, MEM_WIDTH, 8), dcat0], [onw, mqw0, mk, mv],
        [((N_HEADS, HEAD_DIM), F32), (D_MODEL, BF16), (MEM_WIDTH, BF16)],
        [(1, HEAD_DIM), (1, HEAD_DIM), (n_mem, MEM_WIDTH), (n_mem, MEM_WIDTH)], tm=256, name="dn_out_bwd", riders=[sibo])
    h_l0 = chip_sums(["w_mlp2_0", "w_mlp1_0", "w_out_0"], [dw2_0, dw1_0, dwo_0], sib2.results + sib1.results + sibo.results)
    ride_l0_mlp2, ride_l0_rest = chips_rider(h_l0[:1]), chips_rider(h_l0[1:])
    dmnw, dwkv, dmknw = memkv_bwd(mem, mnw, w_kv, mknw, dmk0 + dmk1, dmv0 + dmv1, name="memkv_bwd")
    g_kv = dwkv.reshape(N_DEV, D_MODEL // N_DEV, D_MODEL)
    sibk = sibling_rider([g_kv])
    dq_s, dk_s, dg_s, du0, dw0, dqk0 = delta_seq_bwd(q0, k0, gates, u0, w0, qk0, s_start, do0, name="delta_seq_bwd",
                                                     riders=[ride_fox_g, sibk])
    ride_kv_g = chips_rider(chip_sums(["w_mem_kv"], [g_kv], sibk.results))
    dq0, dk0, dv0, dgates = delta_intra_bwd(q0, k0, v0, gates, du0, dw0, dqk0, dq_s, dk_s, dg_s,
                                            name="delta_intra_bwd", riders=[ride_l0_mlp2, ride_kv_g])
    dxq, dxk, dxv, dcq, dck, dcv = dn_prep_bwd(pm0, conv_w, dq0, dk0, dv0, name="dn_prep_bwd", riders=[ride_l0_rest])
    dconv = jnp.concatenate([dcq, dck, dcv], axis=1)
    dps0, dalog, ddtb = rows_call(
        functools.partial(vjp_rows(dn_gates_fn, 1, (True, True)), n_row=1, n_ct=1),
        [ps0, dgates], [alog, dtb], [(LANES, F32)], [(1, LANES)] * 2, tm=512, name="dn_gates_bwd")
    dpm0 = jnp.concatenate([dxq, dxk, dxv, dz0, dqm0], axis=1)
    grad_x, dn1w0, dwmain0, dwsmall0 = _in_proj_bwd(h0, dpm0, dps0, dn_main, dn_ab, x0, dx1, n1w0, "0")
    g_dn = in_proj_pieces(dwmain0, dwsmall0, 2 * N_HEADS, DN_IN)
    g_conv = dconv.reshape(CONV_WIDTH, N_DEV, -1).transpose(1, 0, 2).astype(BF16)

    g["mem_norm_w"] = dmnw[0]
    g["mem_k_norm_w"] = dmknw[0]
    g["norm1_w"] = jnp.concatenate([dn1w0, dn1w1], axis=0)
    g["dn_a_log"] = dalog[:, :N_HEADS]
    g["dn_dt_bias"] = ddtb[:, :N_HEADS]
    g["dn_o_norm_w"] = donw
    g["fox_f_bias"] = dfbias[:, :N_HEADS]
    g["fox_q_norm_w"] = dqnw
    g["fox_k_norm_w"] = dknw
    g["memq_norm_w"] = jnp.concatenate([dmqw0, dmqw1], axis=0)
    g["norm2_w"] = jnp.concatenate([dn2w0, dn2w1], axis=0)

    sibd = sibling_rider([g_dn, g_conv])
    run_riders([sibd], name="grads_to_sibling_last")
    ride_last = chips_rider(chip_sums(["dn_w_in", "dn_conv_w"], [g_dn, g_conv], sibd.results))
    ride_small = gather_rider([pack_small(g, loss)])
    run_riders([ride_last, ride_small], name="grads_to_chips_last")

    def layers(l0, l1):
        return jnp.stack([l0, l1], axis=1).reshape(4, -1, l0.shape[-1])

    parts = {
        "w_mlp1": layers(ride_l0_rest.results[0], ride_l1.results[1]),
        "w_mlp2": layers(ride_l0_mlp2.results[0], ride_l1.results[0]),
        "w_out": layers(ride_l0_rest.results[1], ride_l1.results[2]),
        "fox_w_in": ride_fox_g.results[0], "w_mem_kv": ride_kv_g.results[0],
        "dn_w_in": ride_last.results[0], "dn_conv_w": ride_last.results[1],
    }
    out = {n: adamw(parts[n], w[n], m[n], v[n], name=f"adamw_{n}") for n, _, _ in BIG}
    small, loss = adamw_small(ride_small.results[0], w, m, v, name="adamw_small")
    return loss, grad_x, out, small


WEIGHTS = ["mem_norm_w", "w_mem_kv", "mem_k_norm_w", "norm1_w", "dn_w_in", "dn_conv_w", "dn_a_log", "dn_dt_bias",
           "dn_o_norm_w", "fox_w_in", "fox_f_bias", "fox_q_norm_w", "fox_k_norm_w", "memq_norm_w", "w_out", "norm2_w",
           "w_mlp1", "w_mlp2"]
DN_IN = 4 * D_MODEL + 2 * N_HEADS + MEM_WIDTH
FOX_IN = 4 * D_MODEL + N_HEADS + MEM_WIDTH
GATE_END = 4 * D_MODEL
OUT_IN = D_MODEL + MEM_WIDTH
BIG = [("w_mem_kv", D_MODEL // N_DEV, D_MODEL), ("dn_w_in", D_MODEL, DN_IN // N_DEV), ("fox_w_in", D_MODEL, FOX_IN // N_DEV),
       ("dn_conv_w", CONV_WIDTH, 3 * D_MODEL // N_DEV), ("w_out", 2 * OUT_IN // N_DEV, D_MODEL),
       ("w_mlp1", 2 * D_MODEL, FF_PIECE), ("w_mlp2", 2 * FF_PIECE, D_MODEL)]
SMALL_TILE = 8 * LANES
SMALL = [(name, shape, -(-math.prod(shape) // SMALL_TILE) * SMALL_TILE) for name, shape in [
    ("mem_norm_w", (D_MODEL,)), ("mem_k_norm_w", (HEAD_DIM,)), ("norm1_w", (2, D_MODEL)), ("dn_a_log", (1, N_HEADS)),
    ("dn_dt_bias", (1, N_HEADS)), ("dn_o_norm_w", (1, HEAD_DIM)), ("fox_f_bias", (1, N_HEADS)),
    ("fox_q_norm_w", (1, HEAD_DIM)), ("fox_k_norm_w", (1, HEAD_DIM)), ("memq_norm_w", (2, HEAD_DIM)), ("norm2_w", (2, D_MODEL))]]
SMALL_ROWS = sum(ln for _, _, ln in SMALL) // LANES + 8


def _small_view(a):
    return a.reshape(-1, LANES) if a.size % LANES == 0 else a.reshape(1, a.size)


def pack_small(g, last):
    def rows(a, ln):
        a = a.reshape(-1)
        return (a if a.shape[0] == ln else jnp.pad(a, (0, ln - a.shape[0]))).reshape(-1, LANES)

    used = sum(ln for _, _, ln in SMALL) // LANES
    tail = jnp.concatenate([jnp.zeros(((SMALL_ROWS - used) * LANES - 1,), F32), last.reshape(1)])
    return jnp.concatenate([rows(g[n], ln) for n, _, ln in SMALL] + [tail.reshape(-1, LANES)], axis=0)


def in_proj_weights(gathered, width, n_small):
    full = gathered.transpose(1, 0, 2).reshape(D_MODEL, width)
    main = jnp.concatenate([full[:, :GATE_END], full[:, GATE_END + n_small:]], axis=1)
    return main, jnp.pad(full[:, GATE_END:GATE_END + n_small], ((0, 0), (0, LANES - n_small)))


def in_proj_pieces(d_main, d_small, n_small, width):
    full = jnp.concatenate([d_main[0], d_small[:, :n_small], d_main[1]], axis=1)
    return full.reshape(D_MODEL, N_DEV, width // N_DEV).transpose(1, 0, 2)


def _adamw_update(g, w, m, v):
    m_new = ADAM_B1 * m + (1.0 - ADAM_B1) * g
    v_new = ADAM_B2 * v + (1.0 - ADAM_B2) * jnp.square(g)
    m_hat = m_new / (1.0 - ADAM_B1 ** ADAM_STEP)
    v_hat = v_new / (1.0 - ADAM_B2 ** ADAM_STEP)
    return -ADAM_LR * (m_hat / (jnp.sqrt(v_hat) + ADAM_EPS) + ADAM_WD * w), m_new, v_new


def adamw(parts, w, m, v, *, name):
    n, _, cols = parts.shape
    layers = w.shape[0] if w.ndim == 3 else 1
    rows = w.shape[-2]
    tile = _pick(rows, (512, 256, 128))
    steps = rows // tile

    def body(p_ref, w_ref, m_ref, v_ref, g_ref, d_ref, mo_ref, vo_ref):
        g = p_ref[0].astype(F32)
        for i in range(1, n):
            g = g + p_ref[i].astype(F32)
        g_ref[...] = g
        d_ref[...], mo_ref[...], vo_ref[...] = _adamw_update(g, w_ref[...], m_ref[...], v_ref[...])

    if w.ndim == 3:
        spec = pl.BlockSpec((None, tile, cols), lambda l, i: (l, i, 0))
    else:
        spec = pl.BlockSpec((tile, cols), lambda l, i: (i, 0))
    return pl.pallas_call(
        body, name=name, grid=(layers, steps),
        in_specs=[pl.BlockSpec((n, tile, cols), lambda l, i: (0, l * steps + i, 0)), spec, spec, spec], out_specs=[spec] * 4,
        out_shape=[jax.ShapeDtypeStruct(w.shape, F32)] * 4, compiler_params=_params(("parallel", "parallel")),
    )(parts, w, m, v)


def adamw_small(parts, w, m, v, *, name):
    k = len(SMALL)
    ins = [_small_view(d[n]) for d in (w, m, v) for n, _, _ in SMALL]

    def body(p_ref, *refs):
        w_refs, m_refs, v_refs, outs, g_ref = refs[:k], refs[k:2 * k], refs[2 * k:3 * k], refs[3 * k:-1], refs[-1]
        g_all = p_ref[0]
        for i in range(1, N_DEV):
            g_all = g_all + p_ref[i]
        g_ref[...] = g_all
        row = 0
        for i, (_, _, ln) in enumerate(SMALL):
            r, c = w_refs[i].shape
            g = g_ref[row:row + r, 0:c]
            outs[4 * i][...] = g
            outs[4 * i + 1][...], outs[4 * i + 2][...], outs[4 * i + 3][...] = _adamw_update(
                g, w_refs[i][...], m_refs[i][...], v_refs[i][...])
            row += ln // LANES
        outs[-1][...] = g_ref[SMALL_ROWS - 1:SMALL_ROWS, LANES - 1:LANES]

    out_shape = [jax.ShapeDtypeStruct(a.shape, F32) for a in ins[:k] for _ in range(4)] + [jax.ShapeDtypeStruct((1, 1), F32)]
    res = pl.pallas_call(body, name=name, out_shape=out_shape,
                         scratch_shapes=[pltpu.VMEM((SMALL_ROWS, LANES), F32)])(parts, *ins)
    small = {n: [o.reshape(sh) for o in res[4 * i:4 * i + 4]] for i, (n, sh, _) in enumerate(SMALL)}
    return small, res[-1][0, 0]


def kernel(x, mem, mem_norm_w, w_mem_kv, mem_k_norm_w, norm1_w, dn_w_in, dn_conv_w, dn_a_log, dn_dt_bias, dn_o_norm_w, fox_w_in, fox_f_bias, fox_q_norm_w, fox_k_norm_w, memq_norm_w, w_out, norm2_w, w_mlp1, w_mlp2, loss_target, m_mem_norm_w, m_w_mem_kv, m_mem_k_norm_w, m_norm1_w, m_dn_w_in, m_dn_conv_w, m_dn_a_log, m_dn_dt_bias, m_dn_o_norm_w, m_fox_w_in, m_fox_f_bias, m_fox_q_norm_w, m_fox_k_norm_w, m_memq_norm_w, m_w_out, m_norm2_w, m_w_mlp1, m_w_mlp2, v_mem_norm_w, v_w_mem_kv, v_mem_k_norm_w, v_norm1_w, v_dn_w_in, v_dn_conv_w, v_dn_a_log, v_dn_dt_bias, v_dn_o_norm_w, v_fox_w_in, v_fox_f_bias, v_fox_q_norm_w, v_fox_k_norm_w, v_memq_norm_w, v_w_out, v_norm2_w, v_w_mlp1, v_w_mlp2):
    p = dict(mem_norm_w=mem_norm_w, w_mem_kv=w_mem_kv, mem_k_norm_w=mem_k_norm_w, norm1_w=norm1_w, dn_w_in=dn_w_in,
             dn_conv_w=dn_conv_w, dn_a_log=dn_a_log, dn_dt_bias=dn_dt_bias, dn_o_norm_w=dn_o_norm_w, fox_w_in=fox_w_in,
             fox_f_bias=fox_f_bias, fox_q_norm_w=fox_q_norm_w, fox_k_norm_w=fox_k_norm_w, memq_norm_w=memq_norm_w,
             w_out=w_out, norm2_w=norm2_w, w_mlp1=w_mlp1, w_mlp2=w_mlp2)
    pm = dict(mem_norm_w=m_mem_norm_w, w_mem_kv=m_w_mem_kv, mem_k_norm_w=m_mem_k_norm_w, norm1_w=m_norm1_w,
              dn_w_in=m_dn_w_in, dn_conv_w=m_dn_conv_w, dn_a_log=m_dn_a_log, dn_dt_bias=m_dn_dt_bias,
              dn_o_norm_w=m_dn_o_norm_w, fox_w_in=m_fox_w_in, fox_f_bias=m_fox_f_bias, fox_q_norm_w=m_fox_q_norm_w,
              fox_k_norm_w=m_fox_k_norm_w, memq_norm_w=m_memq_norm_w, w_out=m_w_out, norm2_w=m_norm2_w, w_mlp1=m_w_mlp1,
              w_mlp2=m_w_mlp2)
    pv = dict(mem_norm_w=v_mem_norm_w, w_mem_kv=v_w_mem_kv, mem_k_norm_w=v_mem_k_norm_w, norm1_w=v_norm1_w,
              dn_w_in=v_dn_w_in, dn_conv_w=v_dn_conv_w, dn_a_log=v_dn_a_log, dn_dt_bias=v_dn_dt_bias,
              dn_o_norm_w=v_dn_o_norm_w, fox_w_in=v_fox_w_in, fox_f_bias=v_fox_f_bias, fox_q_norm_w=v_fox_q_norm_w,
              fox_k_norm_w=v_fox_k_norm_w, memq_norm_w=v_memq_norm_w, w_out=v_w_out, norm2_w=v_norm2_w, w_mlp1=v_w_mlp1,
              w_mlp2=v_w_mlp2)

    loss, grad_x, results, small = local_step(x[0], mem[0], loss_target[0], p, pm, pv)
    groups = [{n: r[i] for n, r in {**small, **results}.items()} for i in range(4)]
    return (loss, grad_x[None], *[grp[n] for grp in groups for n in WEIGHTS])
```

```python
import functools
import math

import jax
import jax.numpy as jnp
from jax import lax
from jax.experimental import pallas as pl
from jax.experimental.pallas import tpu as pltpu

F32 = jnp.float32
BF16 = jnp.bfloat16
HIGHEST = lax.Precision.HIGHEST

D_MODEL = 1024
HEAD_DIM = 128
N_HEADS = 8
MEM_HEADS = 4
MEM_WIDTH = MEM_HEADS * HEAD_DIM
D_FF = 4 * D_MODEL
CONV_WIDTH = 4
CHUNK = 64
Q_BLOCK = 128
EPS = 1e-6
SCALE = HEAD_DIM ** -0.5
MAIN_WIDTH = 4 * D_MODEL + MEM_WIDTH
LANES = 128
N_DEV = 8

ADAM_LR = 0.001
ADAM_B1 = 0.9
ADAM_B2 = 0.999
ADAM_EPS = 1e-08
ADAM_WD = 0.01
ADAM_STEP = 10

VMEM_LIMIT = 56 * 2 ** 20
MESH = pl.DeviceIdType.MESH


def _bdot(a, b, dims):
    return lax.dot_general(a.astype(BF16), b.astype(BF16), (dims, ((), ())), preferred_element_type=F32)


@jax.custom_vjp
def mm(a, b):
    return _bdot(a, b, ((1,), (0,)))


@jax.custom_vjp
def mm_nt(a, b):
    return _bdot(a, b, ((1,), (1,)))


@jax.custom_vjp
def mm_tn(a, b):
    return _bdot(a, b, ((0,), (0,)))


mm.defvjp(lambda a, b: (mm(a, b), (a, b)), lambda r, g: (mm_nt(g, r[1]), mm_tn(r[0], g)))
mm_nt.defvjp(lambda a, b: (mm_nt(a, b), (a, b)), lambda r, g: (mm(g, r[1]), mm_tn(g, r[0])))
mm_tn.defvjp(lambda a, b: (mm_tn(a, b), (a, b)), lambda r, g: (mm_nt(r[1], g), mm(r[0], g)))


def hdot(a, b):
    return jnp.dot(a, b, precision=HIGHEST, preferred_element_type=F32)


def rms(x, w):
    return x * lax.rsqrt(jnp.mean(x * x, axis=-1, keepdims=True) + EPS) * w


def l2n(x):
    return x * lax.rsqrt(jnp.sum(x * x, axis=-1, keepdims=True) + EPS)


def _iota2(n, m):
    return lax.broadcasted_iota(jnp.int32, (n, m), 0), lax.broadcasted_iota(jnp.int32, (n, m), 1)


def _lower_ones(n):
    r, c = _iota2(n, n)
    return jnp.where(r >= c, 1.0, 0.0).astype(F32)


def _last_row(x):
    r = lax.broadcasted_iota(jnp.int32, x.shape, 0)
    return jnp.sum(jnp.where(r == x.shape[0] - 1, x, 0.0), axis=0, keepdims=True)


def _softmax_rows(z):
    m = lax.stop_gradient(jnp.max(z, axis=-1, keepdims=True))
    e = jnp.exp(z - m)
    return e * (1.0 / jnp.sum(e, axis=-1, keepdims=True))


_BNN = (((2,), (1,)), ((0,), (0,)))
_BNT = (((2,), (2,)), ((0,), (0,)))
_BTN = (((1,), (1,)), ((0,), (0,)))


def _bbdot(a, b, dims):
    return lax.dot_general(a.astype(BF16), b.astype(BF16), dims, preferred_element_type=F32)


@jax.custom_vjp
def bmm(a, b):
    return _bbdot(a, b, _BNN)


@jax.custom_vjp
def bmm_nt(a, b):
    return _bbdot(a, b, _BNT)


@jax.custom_vjp
def bmm_tn(a, b):
    return _bbdot(a, b, _BTN)


@jax.custom_vjp
def bmm_high(a, b):
    return lax.dot_general(a, b, _BNN, precision=lax.Precision.HIGH, preferred_element_type=F32)


bmm.defvjp(lambda a, b: (bmm(a, b), (a, b)), lambda r, g: (bmm_nt(g, r[1]), bmm_tn(r[0], g)))
bmm_nt.defvjp(lambda a, b: (bmm_nt(a, b), (a, b)), lambda r, g: (bmm(g, r[1]), bmm_tn(g, r[0])))
bmm_tn.defvjp(lambda a, b: (bmm_tn(a, b), (a, b)), lambda r, g: (bmm_nt(r[1], g), bmm(r[0], g)))
bmm_high.defvjp(lambda a, b: (bmm_high(a, b), (a, b)), lambda r, g: (bmm_nt(g, r[1]), bmm_tn(r[0], g)))

NEUMANN_HIGH_LEVELS = 2


@jax.custom_vjp
def inv_unit_lower(a):
    n = a.shape[-1]
    r, c = _iota2(n, n)
    p = jnp.where(r == c, 1.0, 0.0).astype(F32) - a
    ak = a
    for level in range(int(math.log2(n)) - 1):
        dot = bmm_high if level < NEUMANN_HIGH_LEVELS else bmm
        ak = dot(ak, ak)
        p = p + dot(p, ak)
    return p


def _inv_unit_lower_fwd(a):
    t = inv_unit_lower(a)
    return t, t


def _inv_unit_lower_bwd(t, g):
    return (-bmm_tn(t, bmm_nt(g, t)),)


inv_unit_lower.defvjp(_inv_unit_lower_fwd, _inv_unit_lower_bwd)


def delta_intra(q, k, v, gc, beta):
    b, c, _ = q.shape
    r, cc = _iota2(c, c)
    causal = r >= cc
    strict = r > cc
    gi = jnp.broadcast_to(gc, (b, c, c))
    gj = jnp.swapaxes(gi, 1, 2)
    decay = jnp.where(causal, jnp.exp(jnp.where(causal, gi - gj, 0.0)), 0.0)
    kb = k * beta
    a = jnp.where(strict, bmm_nt(kb, k) * decay, 0.0)
    t = inv_unit_lower(a)
    u = bmm(t, v * beta)
    w = bmm(t, kb * jnp.exp(gc))
    qk = jnp.where(causal, bmm_nt(q, k) * decay, 0.0)
    return u, w, qk


def delta_step(s, q, k, gc, u, w, qk):
    v_new = u - bmm(w, s)
    out = bmm(q * jnp.exp(gc), s) + bmm(qk, v_new)
    r = lax.broadcasted_iota(jnp.int32, gc.shape, 1)
    g_last = jnp.sum(jnp.where(r == gc.shape[1] - 1, gc, 0.0), axis=1, keepdims=True)
    k_dec = k * jnp.exp(g_last - gc)
    s_new = s * jnp.exp(g_last) + bmm_tn(k_dec, v_new)
    return out, s_new


def fox_probs(q, k, fq, fk, qpos0):
    s = lax.dot_general(q, k, (((1,), (1,)), ((), ())), preferred_element_type=F32)
    r, c = _iota2(s.shape[0], s.shape[1])
    return _softmax_rows(jnp.where(c <= (r + qpos0), s + (fq - fk), -jnp.inf))


def mem_head(qm, wq, mk, mv):
    p = _softmax_rows(mm_nt(rms(qm, wq) * SCALE, mk))
    return mm(p, mv)


def _heads(x, n):
    return [x[:, h * HEAD_DIM:(h + 1) * HEAD_DIM] for h in range(n)]


def memkv_fn(mem, mnw, wkv, mknw):
    kv = mm(rms(mem, mnw), wkv)
    mk = jnp.concatenate([rms(kh, mknw) for kh in _heads(kv[:, :MEM_WIDTH], MEM_HEADS)], axis=1)
    return mk, kv[:, MEM_WIDTH:]


def dn_gates_fn(ab, alog, dtb):
    g = -jnp.exp(alog) * jax.nn.softplus(ab + dtb)
    low = _lower_ones(CHUNK)
    gc = jnp.concatenate([hdot(low, g[i * CHUNK:(i + 1) * CHUNK]) for i in range(ab.shape[0] // CHUNK)], axis=0)
    lane = lax.broadcasted_iota(jnp.int32, ab.shape, 1)
    return jnp.where(lane < N_HEADS, gc, jax.nn.sigmoid(ab))


def fox_fcum_fn(fp, fbias):
    lf = jax.nn.log_sigmoid(fp + fbias)
    low = _lower_ones(LANES)
    carry = jnp.zeros((1, fp.shape[1]), F32)
    outs = []
    for i in range(fp.shape[0] // LANES):
        cs = hdot(low, lf[i * LANES:(i + 1) * LANES]) + carry
        carry = _last_row(cs)
        outs.append(cs)
    return jnp.concatenate(outs, axis=0)


def fox_qk_fn(qraw, kraw, qnw, knw):
    q = jnp.concatenate([rms(x, qnw) * SCALE for x in _heads(qraw, N_HEADS)], axis=1)
    k = jnp.concatenate([rms(x, knw) for x in _heads(kraw, N_HEADS)], axis=1)
    return q, k


def _mem_out(qm, mqw, mk, mv):
    return [mem_head(a, mqw, b, c) for a, b, c in zip(_heads(qm, MEM_HEADS), _heads(mk, MEM_HEADS), _heads(mv, MEM_HEADS))]


def dn_out_fn(o, z, qm, onw, mqw, mk, mv):
    mix = [rms(a, onw) * jax.nn.silu(b) for a, b in zip(o, _heads(z, N_HEADS))]
    return jnp.concatenate(mix + _mem_out(qm, mqw, mk, mv), axis=1)


def fox_out_fn(o, gate, qm, mqw, mk, mv):
    return jnp.concatenate([o * jax.nn.sigmoid(gate)] + _mem_out(qm, mqw, mk, mv), axis=1)


_HBM = pl.BlockSpec(memory_space=pltpu.HBM)


def _place():
    return lax.axis_index("x"), lax.axis_index("y"), lax.axis_index("c")


class Rider:
    def __init__(self, ins, out_shape, scratch, start, finish):
        self.ins, self.out_shape, self.scratch, self.start, self.finish = list(ins), list(out_shape), list(scratch), start, finish
        self.results = None


def gather_rider(xs):
    n = len(xs)

    def plan(x_refs, out_refs, sems):
        send_sems, recv_sems, local_sems = sems
        x, y, c = _place()
        me, sibling = (x, y, c), (x, y, 1 - c)
        chips = [(1 - x, y), (x, 1 - y), (1 - x, 1 - y)]

        def copy(a, k, block, to, src=None):
            px, py, pc = block
            dst = out_refs[a].at[4 * px + 2 * py + pc]
            return pltpu.make_async_remote_copy(
                src_ref=dst if src is None else src, dst_ref=dst,
                send_sem=send_sems.at[a, k], recv_sem=recv_sems.at[a, k], device_id=to, device_id_type=MESH)

        mine = [pltpu.make_async_copy(x_refs[a], out_refs[a].at[4 * x + 2 * y + c], local_sems.at[a]) for a in range(n)]
        first = [copy(a, 0, me, sibling, src=x_refs[a]) for a in range(n)]
        first += [copy(a, 1 + j, me, (*chip, c), src=x_refs[a]) for j, chip in enumerate(chips) for a in range(n)]
        return copy, me, sibling, chips, mine, first

    def start(x_refs, out_refs, sems):
        _, _, _, _, mine, first = plan(x_refs, out_refs, sems)
        for cp in mine + first:
            cp.start()

    def finish(x_refs, out_refs, sems):
        copy, me, sibling, chips, mine, first = plan(x_refs, out_refs, sems)
        _, _, c = me
        passed = []
        for j, chip in enumerate(chips):
            for a in range(n):
                copy(a, 1 + j, (*chip, c), me).wait_recv()
                passed.append(copy(a, 4 + j, (*chip, c), sibling))
                passed[-1].start()
        for a in range(n):
            copy(a, 0, sibling, me).wait_recv()
        for j, chip in enumerate(chips):
            for a in range(n):
                copy(a, 4 + j, (*chip, 1 - c), me).wait_recv()
        for cp in first + passed:
            cp.wait_send()
        for cp in mine:
            cp.wait()

    return Rider(xs, [jax.ShapeDtypeStruct((N_DEV,) + a.shape, a.dtype) for a in xs],
                 [pltpu.SemaphoreType.DMA((n, 7)), pltpu.SemaphoreType.DMA((n, 7)), pltpu.SemaphoreType.DMA((n,))], start, finish)


def sibling_rider(gs):
    n = len(gs)

    def plan(g_refs, out_refs, sems):
        send_sems, recv_sems = sems
        x, y, c = _place()
        return [pltpu.make_async_remote_copy(
            src_ref=g_refs[a].at[2 * k + 1 - c], dst_ref=out_refs[a].at[k], send_sem=send_sems.at[a, k],
            recv_sem=recv_sems.at[a, k], device_id=(x, y, 1 - c), device_id_type=MESH) for a in range(n) for k in range(4)]

    def start(g_refs, out_refs, sems):
        for cp in plan(g_refs, out_refs, sems):
            cp.start()

    def finish(g_refs, out_refs, sems):
        copies = plan(g_refs, out_refs, sems)
        for cp in copies:
            cp.wait_recv()
        for cp in copies:
            cp.wait_send()

    return Rider(gs, [jax.ShapeDtypeStruct((4,) + g.shape[1:], g.dtype) for g in gs],
                 [pltpu.SemaphoreType.DMA((n, 4)), pltpu.SemaphoreType.DMA((n, 4))], start, finish)


def chips_rider(hs):
    n = len(hs)

    def plan(h_refs, out_refs, sems):
        send_sems, recv_sems, local_sems = sems
        x, y, c = _place()
        mine = 2 * x + y
        chips = [(1 - x, y), (x, 1 - y), (1 - x, 1 - y)]
        keep = [pltpu.make_async_copy(h_refs[a].at[mine], out_refs[a].at[mine], local_sems.at[a]) for a in range(n)]
        sends = [pltpu.make_async_remote_copy(
            src_ref=h_refs[a].at[2 * qx + qy], dst_ref=out_refs[a].at[mine], send_sem=send_sems.at[a, j],
            recv_sem=recv_sems.at[a, j], device_id=(qx, qy, c), device_id_type=MESH)
            for j, (qx, qy) in enumerate(chips) for a in range(n)]
        recvs = [pltpu.make_async_remote_copy(
            src_ref=h_refs[a].at[mine], dst_ref=out_refs[a].at[2 * qx + qy], send_sem=send_sems.at[a, j],
            recv_sem=recv_sems.at[a, j], device_id=(qx, qy, c), device_id_type=MESH)
            for j, (qx, qy) in enumerate(chips) for a in range(n)]
        return keep, sends, recvs

    def start(h_refs, out_refs, sems):
        keep, sends, _ = plan(h_refs, out_refs, sems)
        for cp in keep + sends:
            cp.start()

    def finish(h_refs, out_refs, sems):
        keep, sends, recvs = plan(h_refs, out_refs, sems)
        for cp in recvs:
            cp.wait_recv()
        for cp in sends:
            cp.wait_send()
        for cp in keep:
            cp.wait()

    return Rider(hs, [jax.ShapeDtypeStruct(h.shape, h.dtype) for h in hs],
                 [pltpu.SemaphoreType.DMA((n, 3)), pltpu.SemaphoreType.DMA((n, 3)), pltpu.SemaphoreType.DMA((n,))], start, finish)


def hosted_call(riders, body, *, out_shape, in_specs, out_specs, grid=(), scratch_shapes=(), **kw):
    riders = tuple(riders or ())
    if not riders:
        return pl.pallas_call(body, out_shape=out_shape, in_specs=in_specs, out_specs=out_specs, grid=grid,
                              scratch_shapes=scratch_shapes, **kw)
    single = not isinstance(out_shape, (list, tuple))
    k_out_shape = [out_shape] if single else list(out_shape)
    k_out_specs = [out_specs] if single else list(out_specs)
    n_in, n_out, n_scr = len(in_specs), len(k_out_shape), len(scratch_shapes)
    r_ins = [a for r in riders for a in r.ins]
    r_outs = [s for r in riders for s in r.out_shape]
    r_scr = [s for r in riders for s in r.scratch]

    def full_body(*refs):
        ins = refs[:n_in + len(r_ins)]
        outs = refs[n_in + len(r_ins):n_in + len(r_ins) + n_out + len(r_outs)]
        scr = refs[n_in + len(r_ins) + n_out + len(r_outs):]
        steps = math.prod(grid)
        step = 0
        for d, g in enumerate(grid):
            step = step * g + pl.program_id(d)

        def each(method):
            i0, o0, s0 = n_in, n_out, n_scr
            for r in riders:
                getattr(r, method)(ins[i0:i0 + len(r.ins)], outs[o0:o0 + len(r.out_shape)], scr[s0:s0 + len(r.scratch)])
                i0, o0, s0 = i0 + len(r.ins), o0 + len(r.out_shape), s0 + len(r.scratch)

        if steps == 1:
            each("start")
            body(*ins[:n_in], *outs[:n_out], *scr[:n_scr])
            each("finish")
        else:
            pl.when(step == 0)(lambda: each("start"))
            body(*ins[:n_in], *outs[:n_out], *scr[:n_scr])
            pl.when(step == steps - 1)(lambda: each("finish"))

    call = pl.pallas_call(
        full_body, out_shape=k_out_shape + r_outs, in_specs=list(in_specs) + [_HBM] * len(r_ins),
        out_specs=k_out_specs + [_HBM] * len(r_outs), grid=grid, scratch_shapes=list(scratch_shapes) + r_scr, **kw)

    def run(*args):
        res = call(*args, *r_ins)
        o0 = n_out
        for r in riders:
            r.results = list(res[o0:o0 + len(r.out_shape)])
            o0 += len(r.out_shape)
        return res[0] if single else list(res[:n_out])

    return run


def run_riders(riders, *, name):
    hosted_call(riders, lambda: None, name=name, out_shape=[], in_specs=[], out_specs=[])()
    return [r.results for r in riders]


def _pick(n, cands):
    for c in cands:
        if n % c == 0:
            return c
    return n


def _params(sem):
    return pltpu.CompilerParams(dimension_semantics=sem, vmem_limit_bytes=VMEM_LIMIT)


MATMUL_VMEM_BUDGET = 40 * 2 ** 20


def _matmul_tiles(m, n, k, bytes_a, bytes_b, bytes_mn, fixed):
    fm, fn, fk = fixed if fixed is not None else (None, None, None)

    def options(given, size, cands):
        return [given] if given else ([c for c in cands if size % c == 0] or [size])

    best = None
    for tm in options(fm, m, (2048, 1024, 512, 256, 128)):
        for tn in options(fn, n, (512, 256, 128)):
            for tk in options(fk, k, (2048, 1536, 1024, 512, 256, 128)):
                if 2 * (tm * tk * bytes_a + tk * tn * bytes_b + tm * tn * bytes_mn) + tm * tn * 4 > MATMUL_VMEM_BUDGET:
                    continue
                key = ((m // tm) * (n // tn) * (k // tk), -tk)
                if best is None or key < best[0]:
                    best = (key, (tm, tn, tk))
    assert best is not None, (m, n, k, fixed)
    return best[1]


def matmul(a, b, *, name, ta=False, tb=False, post=None, post_ins=(), row_ins=(), acc=False, extra_out=None,
           out_dtype=F32, tiles=None, b_view=None, out_view=None, riders=()):
    (k, m) = a.shape if ta else a.shape[::-1]
    (kb, n) = b_view[:2] if b_view is not None else (b.shape[::-1] if tb else b.shape)
    assert k == kb, (a.shape, b.shape, ta, tb)
    bytes_mn = sum(p.dtype.itemsize for p in post_ins) + jnp.dtype(out_dtype).itemsize
    bytes_mn += jnp.dtype(extra_out[1]).itemsize if extra_out else 0
    tm, tn, tk = _matmul_tiles(m, n, k, a.dtype.itemsize, b.dtype.itemsize, bytes_mn, tiles)
    assert not acc or tn == n, (name, tn, n)
    nk = k // tk
    dims = ((0,) if ta else (1,), (1,) if tb else (0,))
    n_post, n_row = len(post_ins), len(row_ins)
    n_out = 1 + bool(extra_out) + bool(acc)

    def body(*refs):
        a_ref, b_ref = refs[:2]
        post_refs = refs[2:2 + n_post + n_row]
        o_refs, acc_ref = refs[-1 - n_out:-1], refs[-1]
        first_rows, kk = pl.program_id(0) == 0, pl.program_id(2)

        @pl.when(kk == 0)
        def _():
            acc_ref[...] = jnp.zeros_like(acc_ref)

        b_tile = b_ref[...]
        acc_ref[...] += _bdot(a_ref[...], b_tile.reshape(-1, b_tile.shape[-1]), dims)

        @pl.when(kk == nk - 1)
        def _():
            r = acc_ref[...]
            rows = [p[...] for p in post_refs[n_post:]]
            if post is not None:
                r = post(r, *[p[...] for p in post_refs[:n_post]], *rows)
            if acc:
                r, s = r
                sum_ref = o_refs[-1]

                @pl.when(first_rows)
                def _():
                    sum_ref[...] = s

                @pl.when(jnp.logical_not(first_rows))
                def _():
                    sum_ref[...] += s

            o_refs[0][...] = r.astype(out_dtype)
            if extra_out:
                o_refs[1][...] = extra_out[0](r, *rows).astype(extra_out[1])

    a_spec = pl.BlockSpec((tk, tm), lambda i, j, kk: (kk, i)) if ta else pl.BlockSpec((tm, tk), lambda i, j, kk: (i, kk))
    if b_view is not None:
        b_spec = b_view[2]
    else:
        b_spec = pl.BlockSpec((tn, tk), lambda i, j, kk: (j, kk)) if tb else pl.BlockSpec((tk, tn), lambda i, j, kk: (kk, j))
    mn_spec = pl.BlockSpec((tm, tn), lambda i, j, kk: (i, j))
    row_spec = pl.BlockSpec((1, tn), lambda i, j, kk: (0, j))
    o_shape, o_spec = ((m, n), mn_spec) if out_view is None else out_view
    out_shape = [jax.ShapeDtypeStruct(o_shape, out_dtype)]
    out_specs = [o_spec]
    if extra_out:
        out_shape.append(jax.ShapeDtypeStruct((m, n), extra_out[1]))
        out_specs.append(mn_spec)
    if acc:
        out_shape.append(jax.ShapeDtypeStruct((1, n), F32))
        out_specs.append(row_spec)
    res = hosted_call(
        riders, body, name=name, grid=(m // tm, n // tn, nk),
        in_specs=[a_spec, b_spec] + [mn_spec] * n_post + [row_spec] * n_row, out_specs=out_specs, out_shape=out_shape,
        scratch_shapes=[pltpu.VMEM((tm, tn), F32)],
        compiler_params=_params(("arbitrary" if acc else "parallel", "parallel", "arbitrary")),
    )(a, b, *post_ins, *row_ins)
    return res if n_out > 1 else res[0]


def rows_call(fn, row_ins, full_ins, row_outs, acc_outs, *, tm, name, riders=()):
    row_ins = [r if isinstance(r, tuple) else (r, r.shape[-1], 0) for r in row_ins]
    t = row_ins[0][0].shape[-2]
    tm = min(tm, t)
    n_in = len(row_ins) + len(full_ins)
    n_row = len(row_outs)

    def body(*refs):
        res = fn(*[[r[h] for h in range(r.shape[0])] if (i < len(row_ins) and len(r.shape) == 3) else r[...]
                   for i, r in enumerate(refs[:n_in])])
        res = res if isinstance(res, (tuple, list)) else (res,)
        outs = refs[n_in:]
        for ref, val in zip(outs[:n_row], res[:n_row]):
            if len(ref.shape) == 3:
                for h, vh in enumerate(val):
                    ref[h] = vh.astype(ref.dtype)
            else:
                ref[...] = val.astype(ref.dtype)
        first = pl.program_id(0) == 0
        for ref, val in zip(outs[n_row:], res[n_row:]):
            @pl.when(first)
            def _(ref=ref, val=val):
                ref[...] = val

            @pl.when(jnp.logical_not(first))
            def _(ref=ref, val=val):
                ref[...] += val

    def full_spec(shape):
        return pl.BlockSpec(shape, lambda i, nd=len(shape): (0,) * nd)

    def row_spec(lead, w, cb):
        if lead is None:
            return pl.BlockSpec((tm, w), lambda i: (i, cb))
        return pl.BlockSpec((lead, tm, w), lambda i: (0, i, cb))

    def lead_cols(c):
        return c if isinstance(c, tuple) else (None, c)

    in_specs = [row_spec(a.shape[0] if a.ndim == 3 else None, w, cb) for (a, w, cb) in row_ins]
    in_specs += [full_spec(f.shape) for f in full_ins]
    out_specs = [row_spec(*lead_cols(c), 0) for c, _ in row_outs] + [full_spec(s) for s in acc_outs]
    out_shape = [jax.ShapeDtypeStruct(tuple(d for d in (lead_cols(c)[0], t, lead_cols(c)[1]) if d is not None), dt)
                 for c, dt in row_outs] + [jax.ShapeDtypeStruct(s, F32) for s in acc_outs]
    res = hosted_call(
        riders, body, name=name, grid=(t // tm,), in_specs=in_specs, out_specs=out_specs, out_shape=out_shape,
        compiler_params=_params(("arbitrary",)),
    )(*[r[0] for r in row_ins], *full_ins)
    return res


def vjp_rows(fn, n_diff_row, row_diff_full):
    def bwd(*args, n_row, n_ct):
        prim_rows = args[:n_row]
        cts = args[n_row:n_row + n_ct]
        fulls = args[n_row + n_ct:]
        _, vjp = jax.vjp(fn, *prim_rows, *fulls)
        g = vjp(cts[0] if n_ct == 1 else tuple(cts))
        out = list(g[:n_diff_row])
        out += [gf for gf, d in zip(g[n_row:], row_diff_full) if d]
        return tuple(out)
    return bwd


def _shift_down(x, s):
    if s == 0:
        return x
    t = lax.broadcasted_iota(jnp.int32, x.shape, 0)
    return jnp.where(t >= s, pltpu.roll(x, s, 0), 0.0)


def _shift_up(x, s):
    if s == 0:
        return x
    n = x.shape[0]
    t = lax.broadcasted_iota(jnp.int32, x.shape, 0)
    return jnp.where(t < n - s, pltpu.roll(x, n - s, 0), 0.0)


def _conv(x, w_ref):
    return sum(w_ref[pl.ds(j, 1), :] * _shift_down(x, CONV_WIDTH - 1 - j) for j in range(CONV_WIDTH))


_DN_POST = (lambda c: l2n(jax.nn.silu(c)) * SCALE, lambda c: l2n(jax.nn.silu(c)), jax.nn.silu)


def dn_prep_fwd(proj, conv_w, *, name, riders=()):
    t = proj.shape[0]

    def body(xq, xk, xv, wq, wk, wv, oq, ok, ov):
        for x_ref, w_ref, o_ref, post in zip((xq, xk, xv), (wq, wk, wv), (oq, ok, ov), _DN_POST):
            o_ref[...] = post(_conv(x_ref[...], w_ref))

    x_specs = [pl.BlockSpec((t, HEAD_DIM), lambda h, g=g: (0, g * N_HEADS + h)) for g in range(3)]
    w_specs = [pl.BlockSpec((CONV_WIDTH, HEAD_DIM), lambda h, g=g: (0, g * N_HEADS + h)) for g in range(3)]
    o_spec = pl.BlockSpec((None, t, HEAD_DIM), lambda h: (h, 0, 0))
    return hosted_call(
        riders, body, name=name, grid=(N_HEADS,), in_specs=x_specs + w_specs, out_specs=[o_spec] * 3,
        out_shape=[jax.ShapeDtypeStruct((N_HEADS, t, HEAD_DIM), F32)] * 3, compiler_params=_params(("parallel",)),
    )(proj, proj, proj, conv_w, conv_w, conv_w)


def dn_prep_bwd(proj, conv_w, dq, dk, dv, *, name, riders=()):
    t = proj.shape[0]

    def body(xq, xk, xv, wq, wk, wv, gq, gk, gv, dxq, dxk, dxv, dwq, dwk, dwv):
        for x_ref, w_ref, g_ref, dx_ref, dw_ref, post in zip(
                (xq, xk, xv), (wq, wk, wv), (gq, gk, gv), (dxq, dxk, dxv), (dwq, dwk, dwv), _DN_POST):
            x = x_ref[...]
            _, vjp = jax.vjp(post, _conv(x, w_ref))
            dc, = vjp(g_ref[...])
            dx = sum(w_ref[pl.ds(j, 1), :] * _shift_up(dc, CONV_WIDTH - 1 - j) for j in range(CONV_WIDTH))
            dx_ref[...] = dx.astype(dx_ref.dtype)
            for j in range(CONV_WIDTH):
                dw_ref[pl.ds(j, 1), :] = jnp.sum(dc * _shift_down(x, CONV_WIDTH - 1 - j), axis=0, keepdims=True)

    x_specs = [pl.BlockSpec((t, HEAD_DIM), lambda h, g=g: (0, g * N_HEADS + h)) for g in range(3)]
    w_specs = [pl.BlockSpec((CONV_WIDTH, HEAD_DIM), lambda h, g=g: (0, g * N_HEADS + h)) for g in range(3)]
    g_spec = pl.BlockSpec((None, t, HEAD_DIM), lambda h: (h, 0, 0))
    dx_spec = pl.BlockSpec((t, HEAD_DIM), lambda h: (0, h))
    dw_spec = pl.BlockSpec((CONV_WIDTH, HEAD_DIM), lambda h: (0, h))
    return hosted_call(
        riders, body, name=name, grid=(N_HEADS,), in_specs=x_specs + w_specs + [g_spec] * 3, out_specs=[dx_spec] * 3 + [dw_spec] * 3,
        out_shape=[jax.ShapeDtypeStruct((t, D_MODEL), BF16)] * 3 + [jax.ShapeDtypeStruct((CONV_WIDTH, D_MODEL), F32)] * 3,
        compiler_params=_params(("parallel",)),
    )(proj, proj, proj, conv_w, conv_w, conv_w, dq, dk, dv)


INTRA_CHUNKS = 4


def _lane_column(x, lane_index):
    lane = lax.broadcasted_iota(jnp.int32, x.shape, 1)
    return jnp.sum(jnp.where(lane == lane_index, x, 0.0), axis=1, keepdims=True)


def _head_columns(g, first_lane):
    return jnp.concatenate([_lane_column(g, first_lane + h)[None] for h in range(N_HEADS)], axis=0)


def _intra_of_gates(q, k, v, gates):
    nb = N_HEADS * (gates.shape[0] // CHUNK)

    def chunks(x):
        return x.reshape(nb, CHUNK, x.shape[-1])

    res = delta_intra(chunks(q), chunks(k), chunks(v), chunks(_head_columns(gates, 0)), chunks(_head_columns(gates, N_HEADS)))
    return tuple(x.reshape(N_HEADS, -1, x.shape[-1]) for x in res)


def _step_of_gates(s, q, k, gates, u, w, qk):
    return delta_step(s, q, k, _head_columns(gates, 0), u, w, qk)


def _head_major(rows, w, index):
    return pl.BlockSpec((N_HEADS, rows, w), lambda i: (0, index(i), 0))


def delta_intra_fwd(q, k, v, gates, *, name, riders=()):
    t = q.shape[1]
    rows = min(INTRA_CHUNKS, t // CHUNK) * CHUNK

    def body(q_ref, k_ref, v_ref, g_ref, u_ref, w_ref, qk_ref):
        for ref, val in zip((u_ref, w_ref, qk_ref), _intra_of_gates(q_ref[...], k_ref[...], v_ref[...], g_ref[...])):
            ref[...] = val

    x_spec, qk_spec = (_head_major(rows, w, lambda i: i) for w in (HEAD_DIM, CHUNK))
    g_spec = pl.BlockSpec((rows, LANES), lambda i: (i, 0))
    return hosted_call(
        riders, body, name=name, grid=(t // rows,), in_specs=[x_spec] * 3 + [g_spec], out_specs=[x_spec, x_spec, qk_spec],
        out_shape=[jax.ShapeDtypeStruct((N_HEADS, t, HEAD_DIM), F32)] * 2 + [jax.ShapeDtypeStruct((N_HEADS, t, CHUNK), F32)],
        compiler_params=_params(("parallel",)),
    )(q, k, v, gates)


def delta_seq_fwd(q, k, gates, u, w, qk, *, name, riders=()):
    t = q.shape[1]
    nc = t // CHUNK

    def body(q_ref, k_ref, g_ref, u_ref, w_ref, qk_ref, o_ref, s0_ref, s_ref):
        @pl.when(pl.program_id(0) == 0)
        def _():
            s_ref[...] = jnp.zeros_like(s_ref)

        s = s_ref[...]
        s0_ref[...] = s
        o, s_new = _step_of_gates(s, q_ref[...], k_ref[...], g_ref[...], u_ref[...], w_ref[...], qk_ref[...])
        o_ref[...] = o
        s_ref[...] = s_new

    x_spec, qk_spec = (_head_major(CHUNK, w, lambda c: c) for w in (HEAD_DIM, CHUNK))
    g_spec = pl.BlockSpec((CHUNK, LANES), lambda c: (c, 0))
    s_spec = pl.BlockSpec((N_HEADS, None, HEAD_DIM, HEAD_DIM), lambda c: (0, c, 0, 0))
    return hosted_call(
        riders, body, name=name, grid=(nc,), in_specs=[x_spec, x_spec, g_spec, x_spec, x_spec, qk_spec], out_specs=[x_spec, s_spec],
        out_shape=[jax.ShapeDtypeStruct((N_HEADS, t, HEAD_DIM), F32),
                   jax.ShapeDtypeStruct((N_HEADS, nc, HEAD_DIM, HEAD_DIM), F32)],
        scratch_shapes=[pltpu.VMEM((N_HEADS, HEAD_DIM, HEAD_DIM), F32)],
        compiler_params=_params(("arbitrary",)),
    )(q, k, gates, u, w, qk)


def delta_seq_bwd(q, k, gates, u, w, qk, s0, do, *, name, riders=()):
    t = q.shape[1]
    nc = t // CHUNK

    def body(q_ref, k_ref, g_ref, u_ref, w_ref, qk_ref, s0_ref, do_ref,
             dq_ref, dk_ref, dg_ref, du_ref, dw_ref, dqk_ref, ds_ref):
        @pl.when(pl.program_id(0) == 0)
        def _():
            ds_ref[...] = jnp.zeros_like(ds_ref)

        _, vjp = jax.vjp(_step_of_gates, s0_ref[...], q_ref[...], k_ref[...], g_ref[...], u_ref[...], w_ref[...], qk_ref[...])
        ds, dq, dk, dg, du, dw, dqk = vjp((do_ref[...], ds_ref[...]))
        for ref, val in zip((ds_ref, dq_ref, dk_ref, dg_ref, du_ref, dw_ref, dqk_ref), (ds, dq, dk, dg, du, dw, dqk)):
            ref[...] = val

    x_spec, qk_spec = (_head_major(CHUNK, w, lambda c: nc - 1 - c) for w in (HEAD_DIM, CHUNK))
    g_spec = pl.BlockSpec((CHUNK, LANES), lambda c: (nc - 1 - c, 0))
    s_spec = pl.BlockSpec((N_HEADS, None, HEAD_DIM, HEAD_DIM), lambda c: (0, nc - 1 - c, 0, 0))
    head_shape = [jax.ShapeDtypeStruct((N_HEADS, t, w_), F32) for w_ in (HEAD_DIM, HEAD_DIM, HEAD_DIM, HEAD_DIM, CHUNK)]
    return hosted_call(
        riders, body, name=name, grid=(nc,), in_specs=[x_spec, x_spec, g_spec, x_spec, x_spec, qk_spec, s_spec, x_spec],
        out_specs=[x_spec, x_spec, g_spec, x_spec, x_spec, qk_spec],
        out_shape=head_shape[:2] + [jax.ShapeDtypeStruct((t, LANES), F32)] + head_shape[2:],
        scratch_shapes=[pltpu.VMEM((N_HEADS, HEAD_DIM, HEAD_DIM), F32)],
        compiler_params=_params(("arbitrary",)),
    )(q, k, gates, u, w, qk, s0, do)


def delta_intra_bwd(q, k, v, gates, du, dw, dqk, dq_s, dk_s, dg_s, *, name, riders=()):
    t = q.shape[1]
    rows = min(INTRA_CHUNKS, t // CHUNK) * CHUNK

    def body(q_ref, k_ref, v_ref, g_ref, du_ref, dw_ref, dqk_ref, dqs_ref, dks_ref, dgs_ref, dq_ref, dk_ref, dv_ref, dg_ref):
        _, vjp = jax.vjp(_intra_of_gates, q_ref[...], k_ref[...], v_ref[...], g_ref[...])
        dq, dk, dv, dg = vjp((du_ref[...], dw_ref[...], dqk_ref[...]))
        dq_ref[...] = dq + dqs_ref[...]
        dk_ref[...] = dk + dks_ref[...]
        dv_ref[...] = dv
        dg_ref[...] = dg + dgs_ref[...]

    x_spec, qk_spec = (_head_major(rows, w, lambda i: i) for w in (HEAD_DIM, CHUNK))
    g_spec = pl.BlockSpec((rows, LANES), lambda i: (i, 0))
    return hosted_call(
        riders, body, name=name, grid=(t // rows,),
        in_specs=[x_spec] * 3 + [g_spec, x_spec, x_spec, qk_spec, x_spec, x_spec, g_spec],
        out_specs=[x_spec] * 3 + [g_spec],
        out_shape=[jax.ShapeDtypeStruct((N_HEADS, t, HEAD_DIM), F32)] * 3 + [jax.ShapeDtypeStruct((t, LANES), F32)],
        compiler_params=_params(("parallel",)),
    )(q, k, v, gates, du, dw, dqk, dq_s, dk_s, dg_s)


_V_BLOCK = 2 * N_HEADS
FOX_GROUPS = 16


def _fox_groups(t):
    nq = t // Q_BLOCK
    per = max(1, nq // FOX_GROUPS)
    return [(g0, per, (g0 + per) * Q_BLOCK) for g0 in range(0, nq, per)]


def fox_attn_fwd(q, k, proj, fq, fk, *, name, riders=()):
    t = q.shape[0]

    def body(q_ref, k_ref, v_ref, fq_ref, fk_ref, o_ref, kb_ref, vb_ref):
        head = pl.program_id(0)
        kb_ref[...] = k_ref[...].astype(BF16)
        vb_ref[...] = v_ref[...].astype(BF16)
        for g0, per, keys in _fox_groups(t):
            def block(j, carry, g0=g0, keys=keys):
                rows = pl.ds((g0 + j) * Q_BLOCK, Q_BLOCK)
                p = fox_probs(q_ref[rows, :].astype(BF16), kb_ref[0:keys, :], _lane_column(fq_ref[rows, :], head),
                              fk_ref[:, 0:keys], (g0 + j) * Q_BLOCK)
                o_ref[rows, :] = jnp.dot(p.astype(BF16), vb_ref[0:keys, :], preferred_element_type=F32)
                return carry
            for j in range(per):
                block(j, 0)

    x_spec = pl.BlockSpec((t, HEAD_DIM), lambda h: (0, h))
    v_spec = pl.BlockSpec((t, HEAD_DIM), lambda h: (0, _V_BLOCK + h))
    fq_spec = pl.BlockSpec((t, LANES), lambda h: (0, 0))
    fk_spec = pl.BlockSpec((None, 1, t), lambda h: (h, 0, 0))
    return hosted_call(
        riders, body, name=name, grid=(N_HEADS,), in_specs=[x_spec, x_spec, v_spec, fq_spec, fk_spec], out_specs=x_spec,
        out_shape=jax.ShapeDtypeStruct((t, D_MODEL), F32), scratch_shapes=[pltpu.VMEM((t, HEAD_DIM), BF16)] * 2,
        compiler_params=_params(("parallel",)),
    )(q, k, proj, fq, fk)


def fox_attn_bwd(q, k, proj, fq, fk, do, *, name, riders=()):
    t = q.shape[0]

    def body(q_ref, k_ref, v_ref, fq_ref, fk_ref, do_ref, dq_ref, dk_ref, dv_out_ref, dfq_ref, dfk_ref, kb_ref, vb_ref, dv_ref):
        head = pl.program_id(0)

        @pl.when(head == 0)
        def _():
            dfq_ref[...] = jnp.zeros_like(dfq_ref)

        kb_ref[...] = k_ref[...].astype(BF16)
        vb_ref[...] = v_ref[...].astype(BF16)
        dk_ref[...] = jnp.zeros_like(dk_ref)
        dv_ref[...] = jnp.zeros_like(dv_ref)
        dfk_ref[...] = jnp.zeros_like(dfk_ref)
        nt = (((1,), (1,)), ((), ()))
        tn = (((0,), (0,)), ((), ()))
        for g0, per, keys in _fox_groups(t):
            def block(j, carry, g0=g0, keys=keys):
                rows = pl.ds((g0 + j) * Q_BLOCK, Q_BLOCK)
                qb, dob = q_ref[rows, :].astype(BF16), do_ref[rows, :].astype(BF16)
                kb, vb = kb_ref[0:keys, :], vb_ref[0:keys, :]
                p = fox_probs(qb, kb, _lane_column(fq_ref[rows, :], head), fk_ref[:, 0:keys], (g0 + j) * Q_BLOCK)
                dp = lax.dot_general(dob, vb, nt, preferred_element_type=F32)
                dz = p * (dp - jnp.sum(dp * p, axis=-1, keepdims=True))
                pb, dzb = p.astype(BF16), dz.astype(BF16)
                dq_ref[rows, :] = jnp.dot(dzb, kb, preferred_element_type=F32)
                lane = lax.broadcasted_iota(jnp.int32, (Q_BLOCK, LANES), 1)
                dfq_ref[rows, :] += jnp.where(lane == head, jnp.sum(dz, axis=-1, keepdims=True), 0.0)
                dk_ref[0:keys, :] += lax.dot_general(dzb, qb, tn, preferred_element_type=F32)
                dv_ref[0:keys, :] += lax.dot_general(pb, dob, tn, preferred_element_type=F32)
                dfk_ref[:, 0:keys] -= jnp.sum(dz, axis=0, keepdims=True)
                return carry
            for j in range(per):
                block(j, 0)
        dv_out_ref[...] = dv_ref[...].astype(dv_out_ref.dtype)

    x_spec = pl.BlockSpec((t, HEAD_DIM), lambda h: (0, h))
    v_spec = pl.BlockSpec((t, HEAD_DIM), lambda h: (0, _V_BLOCK + h))
    fq_spec = pl.BlockSpec((t, LANES), lambda h: (0, 0))
    fk_spec = pl.BlockSpec((None, 1, t), lambda h: (h, 0, 0))
    return hosted_call(
        riders, body, name=name, grid=(N_HEADS,), in_specs=[x_spec, x_spec, v_spec, fq_spec, fk_spec, x_spec],
        out_specs=[x_spec, x_spec, x_spec, fq_spec, fk_spec],
        out_shape=[jax.ShapeDtypeStruct((t, D_MODEL), F32)] * 2 + [jax.ShapeDtypeStruct((t, D_MODEL), BF16)]
        + [jax.ShapeDtypeStruct((t, LANES), F32), jax.ShapeDtypeStruct((N_HEADS, 1, t), F32)],
        scratch_shapes=[pltpu.VMEM((t, HEAD_DIM), BF16)] * 2 + [pltpu.VMEM((t, HEAD_DIM), F32)],
        compiler_params=_params(("arbitrary",)),
    )(q, k, proj, fq, fk, do)


def memkv_fwd(mem, mnw, wkv, mknw, *, name):
    n = mem.shape[0]

    def body(mem_ref, mnw_ref, w_ref, mknw_ref, mk_ref, mv_ref):
        mk, mv = memkv_fn(mem_ref[...], mnw_ref[...], w_ref[...], mknw_ref[...])
        mk_ref[...] = mk
        mv_ref[...] = mv

    return pl.pallas_call(
        body, name=name, out_shape=[jax.ShapeDtypeStruct((n, MEM_WIDTH), F32)] * 2,
        compiler_params=pltpu.CompilerParams(vmem_limit_bytes=VMEM_LIMIT),
    )(mem, mnw, wkv, mknw)


def memkv_bwd(mem, mnw, wkv, mknw, dmk, dmv, *, name):
    def body(mem_ref, mnw_ref, w_ref, mknw_ref, dmk_ref, dmv_ref, dmnw_ref, dw_ref, dmknw_ref):
        f = functools.partial(memkv_fn, mem_ref[...])
        _, vjp = jax.vjp(f, mnw_ref[...], w_ref[...].astype(F32), mknw_ref[...])
        dmnw, dw, dmknw = vjp((dmk_ref[...], dmv_ref[...]))
        dmnw_ref[...] = dmnw
        dw_ref[...] = dw.astype(dw_ref.dtype)
        dmknw_ref[...] = dmknw

    return pl.pallas_call(
        body, name=name,
        out_shape=[jax.ShapeDtypeStruct(mnw.shape, F32), jax.ShapeDtypeStruct(wkv.shape, BF16), jax.ShapeDtypeStruct(mknw.shape, F32)],
        compiler_params=pltpu.CompilerParams(vmem_limit_bytes=VMEM_LIMIT),
    )(mem, mnw, wkv, mknw, dmk, dmv)


def _row(v, width=None):
    v = v.reshape(1, -1)
    if width is not None and v.shape[1] < width:
        v = jnp.pad(v, ((0, 0), (0, width - v.shape[1])))
    return v


def _norm_fwd(x, w, name, riders=()):
    return rows_call(lambda x, w: rms(x, w), [x], [w], [(D_MODEL, BF16)], [], tm=512, name=name, riders=riders)[0]


FF_PIECE = D_FF // N_DEV


def _add(r, x, *rows):
    return r + x


def _norm_rows(r, w):
    return rms(r, w)


def _norm_bwd_post(dh, x, dx_in, w):
    _, vjp = jax.vjp(rms, x, w)
    dx, dw = vjp(dh)
    return dx + dx_in, dw


def _piece(rows, cols, index):
    return pl.BlockSpec((None, rows, cols), lambda i, j, kk: (index(i, j, kk), 0, 0))


def _two_pieces(rows, cols, index):
    return pl.BlockSpec((2, rows, cols), lambda i, j, kk: (index(i, j, kk), 0, 0))


def _loss_post(r, x, tgt):
    e = r + x - tgt
    return e * (1.0 / D_MODEL), jnp.sum(e * e, axis=0, keepdims=True)


def _mlp_fwd(x, h2, w1, w2, layer, riders=(), next_norm_w=None, loss_target=None):
    riders = list(riders) + [None, None]
    u, a1 = matmul(h2, w1, name=f"mlp1_fwd_{layer}", tiles=(None, FF_PIECE, D_MODEL),
                   extra_out=(lambda u: jnp.square(jnp.maximum(u, 0.0)), BF16),
                   b_view=(D_MODEL, D_FF, _piece(D_MODEL, FF_PIECE, lambda i, j, kk: j)), riders=riders[0])
    if loss_target is not None:
        tail = dict(post=_loss_post, post_ins=[x, loss_target], acc=True)
    elif next_norm_w is not None:
        tail = dict(post=_add, post_ins=[x], row_ins=[next_norm_w], extra_out=(_norm_rows, BF16))
    else:
        tail = dict(post=_add, post_ins=[x])
    y = matmul(a1, w2, name=f"mlp2_fwd_{layer}", tiles=(None, D_MODEL, 2 * FF_PIECE),
               b_view=(D_FF, D_MODEL, _two_pieces(FF_PIECE, D_MODEL, lambda i, j, kk: kk)), riders=riders[1], **tail)
    return y, (x, h2, u, a1)


def pair_sum(g, got, *, name):
    _, rows, cols = g.shape
    tile = _pick(rows, (512, 256, 128))
    c = lax.axis_index("c").astype(jnp.int32).reshape(1)

    def body(c_ref, a_ref, b_ref, o_ref):
        o_ref[...] = (a_ref[...].astype(F32) + b_ref[...].astype(F32)).astype(o_ref.dtype)

    grid_spec = pltpu.PrefetchScalarGridSpec(
        num_scalar_prefetch=1, grid=(4, rows // tile),
        in_specs=[pl.BlockSpec((None, tile, cols), lambda k, i, c_ref: (2 * k + c_ref[0], i, 0)),
                  pl.BlockSpec((None, tile, cols), lambda k, i, c_ref: (k, i, 0))],
        out_specs=pl.BlockSpec((None, tile, cols), lambda k, i, c_ref: (k, i, 0)))
    return pl.pallas_call(
        body, name=name, grid_spec=grid_spec, out_shape=jax.ShapeDtypeStruct((4, rows, cols), g.dtype),
        compiler_params=_params(("parallel", "parallel")),
    )(c, g, got)


def chip_sums(names, pieces, gots):
    return [pair_sum(a, got, name=f"grads_pair_sum_{n}") for n, a, got in zip(names, pieces, gots)]


def _mlp_bwd(dy, res, n2w, w1, w2, layer, riders=()):
    x, h2, u, a1 = res
    du = matmul(dy, w2, tb=True, name=f"mlp2_dx_{layer}", out_dtype=BF16, tiles=(None, 2 * FF_PIECE, D_MODEL),
                post=lambda r, u: r * (2.0 * jnp.maximum(u, 0.0)), post_ins=[u],
                b_view=(D_MODEL, D_FF, _two_pieces(FF_PIECE, D_MODEL, lambda i, j, kk: j)), riders=riders)
    dw2 = matmul(a1, dy, ta=True, name=f"mlp2_dw_{layer}", out_dtype=BF16, tiles=(FF_PIECE, D_MODEL, None), out_view=(
        w2.shape, _piece(FF_PIECE, D_MODEL, lambda i, j, kk: i)))
    sib2 = sibling_rider([dw2])
    dx, dn2w = matmul(du, w1, tb=True, name=f"mlp1_dx_{layer}", tiles=(None, D_MODEL, FF_PIECE),
                      b_view=(D_FF, D_MODEL, _piece(D_MODEL, FF_PIECE, lambda i, j, kk: kk)),
                      post=_norm_bwd_post, post_ins=[x, dy], row_ins=[n2w], acc=True, riders=[sib2])
    dw1 = matmul(h2, du, ta=True, name=f"mlp1_dw_{layer}", out_dtype=BF16, tiles=(D_MODEL, FF_PIECE, None), out_view=(
        w1.shape, _piece(D_MODEL, FF_PIECE, lambda i, j, kk: j)))
    return dx, dw1, dw2, dn2w, sibling_rider([dw1]), sib2


def _in_proj_bwd(h, dmain, dsmall, w_main, w_small, x, dx_in, n1w, tag):
    dh = matmul(dmain, w_main, tb=True, name=f"inproj_dx_main_{tag}")

    def post(r, dh_main, x, dx_in, w):
        return _norm_bwd_post(r + dh_main, x, dx_in, w)

    dx, dn1w = matmul(dsmall, w_small, tb=True, name=f"inproj_dx_small_{tag}", tiles=(None, D_MODEL, None),
                      post=post, post_ins=[dh, x, dx_in], row_ins=[n1w], acc=True)
    dw_main = matmul(h, dmain, ta=True, out_dtype=BF16, name=f"inproj_dw_main_{tag}")
    dw_small = matmul(h, dsmall, ta=True, out_dtype=BF16, name=f"inproj_dw_small_{tag}")
    return dx, dn1w, dw_main, dw_small


def local_step(x, mem, target, w, m, v):
    t = x.shape[0]
    n_mem = mem.shape[0]
    g = {}

    def wire(a):
        return a.astype(BF16)

    ride_first = gather_rider([wire(w["dn_w_in"][0])])
    fox_w = wire(w["fox_w_in"][0])
    ride_out = gather_rider([wire(w["w_out"][0]), w["dn_conv_w"][0]])
    ride_out_1 = gather_rider([wire(w["w_out"][1])])
    ride_kv = gather_rider([wire(w["w_mem_kv"])])
    ride_mlp1_0 = gather_rider([wire(w["w_mlp1"][0])])
    ride_mlp2_0 = gather_rider([wire(w["w_mlp2"][0])])
    ride_fox_a, ride_fox_b = gather_rider([fox_w[:D_MODEL // 2]]), gather_rider([fox_w[D_MODEL // 2:]])
    ride_mlp_1 = gather_rider([wire(w["w_mlp1"][1]), wire(w["w_mlp2"][1])])
    mnw, mknw = _row(w["mem_norm_w"]), _row(w["mem_k_norm_w"])

    n1w0, n2w0 = _row(w["norm1_w"][0]), _row(w["norm2_w"][0])
    n1w1, n2w1 = _row(w["norm1_w"][1]), _row(w["norm2_w"][1])
    alog, dtb = _row(w["dn_a_log"][0], LANES), _row(w["dn_dt_bias"][0], LANES)
    onw, mqw0 = _row(w["dn_o_norm_w"][0]), _row(w["memq_norm_w"][0])
    x0 = x
    h0 = _norm_fwd(x0, n1w0, "norm1_fwd_0", riders=[ride_first])
    dn_main, dn_ab = in_proj_weights(ride_first.results[0], DN_IN, 2 * N_HEADS)
    pm0 = matmul(h0, dn_main, name="inproj_main_0", riders=[ride_out])
    w_out0 = ride_out.results[0].reshape(OUT_IN, D_MODEL)
    conv_w = ride_out.results[1].transpose(1, 0, 2).reshape(CONV_WIDTH, 3 * D_MODEL)
    ps0 = matmul(h0, dn_ab, name="inproj_small_0")
    gates = rows_call(dn_gates_fn, [ps0], [alog, dtb], [(LANES, F32)], [], tm=512, name="dn_gates_fwd")[0]
    q0, k0, v0 = dn_prep_fwd(pm0, conv_w, name="dn_prep_fwd", riders=[ride_kv])
    w_kv = ride_kv.results[0].reshape(D_MODEL, D_MODEL)
    mk, mv = memkv_fwd(mem, mnw, w_kv, mknw, name="memkv_fwd")
    u0, w0, qk0 = delta_intra_fwd(q0, k0, v0, gates, name="delta_intra_fwd", riders=[ride_mlp1_0])
    o0, s_start = delta_seq_fwd(q0, k0, gates, u0, w0, qk0, name="delta_seq_fwd", riders=[ride_mlp2_0])
    cat0 = rows_call(dn_out_fn, [o0, (pm0, D_MODEL, 3), (pm0, MEM_WIDTH, 8)], [onw, mqw0, mk, mv],
                     [(D_MODEL + MEM_WIDTH, BF16)], [], tm=256, name="dn_out_fwd")[0]
    (w1_0,), (w2_0,) = ride_mlp1_0.results, ride_mlp2_0.results
    x1, h2_0 = matmul(cat0, w_out0, post=_add, post_ins=[x0], row_ins=[n2w0], extra_out=(_norm_rows, BF16),
                      tiles=(None, D_MODEL, None), name="wout_fwd_0")
    (x2, h1), mlp_res0 = _mlp_fwd(x1, h2_0, w1_0, w2_0, 0, riders=[[ride_fox_a], [ride_fox_b]], next_norm_w=n1w1)
    fox_main, fox_f = in_proj_weights(
        jnp.concatenate([ride_fox_a.results[0], ride_fox_b.results[0]], axis=1), FOX_IN, N_HEADS)

    fbias = _row(w["fox_f_bias"][0], LANES)
    qnw, knw, mqw1 = _row(w["fox_q_norm_w"][0]), _row(w["fox_k_norm_w"][0]), _row(w["memq_norm_w"][1])
    pm1 = matmul(h1, fox_main, name="inproj_main_1", riders=[ride_out_1])
    w_out1 = ride_out_1.results[0].reshape(OUT_IN, D_MODEL)
    ps1 = matmul(h1, fox_f, name="inproj_small_1")
    fq = rows_call(fox_fcum_fn, [ps1], [fbias], [(LANES, F32)], [], tm=t, name="fox_fcum_fwd")[0]
    fk = fq[:, :N_HEADS].T[:, None, :]
    q1, k1 = rows_call(fox_qk_fn, [(pm1, D_MODEL, 0), (pm1, D_MODEL, 1)], [qnw, knw], [(D_MODEL, F32)] * 2, [], tm=256,
                       name="fox_qk_fwd")
    o1 = fox_attn_fwd(q1, k1, pm1, fq, fk, name="fox_attn_fwd", riders=[ride_mlp_1])
    cat1 = rows_call(fox_out_fn, [o1, (pm1, D_MODEL, 3), (pm1, MEM_WIDTH, 8)], [mqw1, mk, mv],
                     [(D_MODEL + MEM_WIDTH, BF16)], [], tm=256, name="fox_out_fwd")[0]
    w1_1, w2_1 = ride_mlp_1.results
    x3, h2_1 = matmul(cat1, w_out1, post=_add, post_ins=[x2], row_ins=[n2w1], extra_out=(_norm_rows, BF16),
                      tiles=(None, D_MODEL, None), name="wout_fwd_1")
    (dy, sq), mlp_res1 = _mlp_fwd(x3, h2_1, w1_1, w2_1, 1, loss_target=target)
    loss = jnp.sum(sq) * (0.5 / D_MODEL)

    dx3, dw1_1, dw2_1, dn2w1, sib1, sib2 = _mlp_bwd(dy, mlp_res1, n2w1, w1_1, w2_1, 1)
    dcat1 = matmul(dx3, w_out1, tb=True, name="wout_dx_1", riders=[sib1])
    dwo_1 = matmul(cat1, dx3, ta=True, out_dtype=BF16, name="wout_dw_1").reshape(N_DEV, OUT_IN // N_DEV, D_MODEL)
    sibo = sibling_rider([dwo_1])
    do1, dgate1, dqm1, dmqw1, dmk1, dmv1 = rows_call(
        functools.partial(vjp_rows(fox_out_fn, 3, (True, True, True)), n_row=3, n_ct=1),
        [o1, (pm1, D_MODEL, 3), (pm1, MEM_WIDTH, 8), dcat1], [mqw1, mk, mv],
        [(D_MODEL, F32), (D_MODEL, BF16), (MEM_WIDTH, BF16)], [(1, HEAD_DIM), (n_mem, MEM_WIDTH), (n_mem, MEM_WIDTH)],
        tm=256, name="fox_out_bwd", riders=[sibo])
    ride_l1 = chips_rider(chip_sums(["w_mlp2_1", "w_mlp1_1", "w_out_1"], [dw2_1, dw1_1, dwo_1],
                                    sib2.results + sib1.results + sibo.results))
    dq1, dk1, dv1, dfq, dfk = fox_attn_bwd(q1, k1, pm1, fq, fk, do1, name="fox_attn_bwd", riders=[ride_l1])
    dqraw1, dkraw1, dqnw, dknw = rows_call(
        functools.partial(vjp_rows(fox_qk_fn, 2, (True, True)), n_row=2, n_ct=2),
        [(pm1, D_MODEL, 0), (pm1, D_MODEL, 1), dq1, dk1], [qnw, knw],
        [(D_MODEL, BF16)] * 2, [(1, HEAD_DIM)] * 2, tm=256, name="fox_qk_bwd")
    dfcum = dfq + jnp.pad(dfk[:, 0, :].T, ((0, 0), (0, LANES - N_HEADS)))
    dps1, dfbias = rows_call(
        functools.partial(vjp_rows(fox_fcum_fn, 1, (True,)), n_row=1, n_ct=1),
        [ps1, dfcum], [fbias], [(LANES, F32)], [(1, LANES)], tm=t, name="fox_fcum_bwd")
    dpm1 = jnp.concatenate([dqraw1, dkraw1, dv1, dgate1, dqm1], axis=1)
    dx2, dn1w1, dwmain1, dwsmall1 = _in_proj_bwd(h1, dpm1, dps1, fox_main, fox_f, x2, dx3, n1w1, "1")
    g_fox = in_proj_pieces(dwmain1, dwsmall1, N_HEADS, FOX_IN)
    sibf = sibling_rider([g_fox])

    dx1, dw1_0, dw2_0, dn2w0, sib1, sib2 = _mlp_bwd(dx2, mlp_res0, n2w0, w1_0, w2_0, 0, riders=[sibf])
    ride_fox_g = chips_rider(chip_sums(["fox_w_in"], [g_fox], sibf.results))
    dcat0 = matmul(dx1, w_out0, tb=True, name="wout_dx_0", riders=[sib1])
    dwo_0 = matmul(cat0, dx1, ta=True, out_dtype=BF16, name="wout_dw_0").reshape(N_DEV, OUT_IN // N_DEV, D_MODEL)
    sibo = sibling_rider([dwo_0])
    do0, dz0, dqm0, donw, dmqw0, dmk0, dmv0 = rows_call(
        functools.partial(vjp_rows(dn_out_fn, 3, (True, True, True, True)), n_row=3, n_ct=1),
        [o0, (pm0, D_MODEL, 3), (pm0, MEM_WIDTH, 8), dcat0], [onw, mqw0, mk, mv],
        [((N_HEADS, HEAD_DIM), F32), (D_MODEL, BF16), (MEM_WIDTH, BF16)],
        [(1, HEAD_DIM), (1, HEAD_DIM), (n_mem, MEM_WIDTH), (n_mem, MEM_WIDTH)], tm=256, name="dn_out_bwd", riders=[sibo])
    h_l0 = chip_sums(["w_mlp2_0", "w_mlp1_0", "w_out_0"], [dw2_0, dw1_0, dwo_0], sib2.results + sib1.results + sibo.results)
    ride_l0_mlp2, ride_l0_rest = chips_rider(h_l0[:1]), chips_rider(h_l0[1:])
    dmnw, dwkv, dmknw = memkv_bwd(mem, mnw, w_kv, mknw, dmk0 + dmk1, dmv0 + dmv1, name="memkv_bwd")
    g_kv = dwkv.reshape(N_DEV, D_MODEL // N_DEV, D_MODEL)
    sibk = sibling_rider([g_kv])
    dq_s, dk_s, dg_s, du0, dw0, dqk0 = delta_seq_bwd(q0, k0, gates, u0, w0, qk0, s_start, do0, name="delta_seq_bwd",
                                                     riders=[ride_fox_g, sibk])
    ride_kv_g = chips_rider(chip_sums(["w_mem_kv"], [g_kv], sibk.results))
    dq0, dk0, dv0, dgates = delta_intra_bwd(q0, k0, v0, gates, du0, dw0, dqk0, dq_s, dk_s, dg_s,
                                            name="delta_intra_bwd", riders=[ride_l0_mlp2, ride_kv_g])
    dxq, dxk, dxv, dcq, dck, dcv = dn_prep_bwd(pm0, conv_w, dq0, dk0, dv0, name="dn_prep_bwd", riders=[ride_l0_rest])
    dconv = jnp.concatenate([dcq, dck, dcv], axis=1)
    dps0, dalog, ddtb = rows_call(
        functools.partial(vjp_rows(dn_gates_fn, 1, (True, True)), n_row=1, n_ct=1),
        [ps0, dgates], [alog, dtb], [(LANES, F32)], [(1, LANES)] * 2, tm=512, name="dn_gates_bwd")
    dpm0 = jnp.concatenate([dxq, dxk, dxv, dz0, dqm0], axis=1)
    grad_x, dn1w0, dwmain0, dwsmall0 = _in_proj_bwd(h0, dpm0, dps0, dn_main, dn_ab, x0, dx1, n1w0, "0")
    g_dn = in_proj_pieces(dwmain0, dwsmall0, 2 * N_HEADS, DN_IN)
    g_conv = dconv.reshape(CONV_WIDTH, N_DEV, -1).transpose(1, 0, 2).astype(BF16)

    g["mem_norm_w"] = dmnw[0]
    g["mem_k_norm_w"] = dmknw[0]
    g["norm1_w"] = jnp.concatenate([dn1w0, dn1w1], axis=0)
    g["dn_a_log"] = dalog[:, :N_HEADS]
    g["dn_dt_bias"] = ddtb[:, :N_HEADS]
    g["dn_o_norm_w"] = donw
    g["fox_f_bias"] = dfbias[:, :N_HEADS]
    g["fox_q_norm_w"] = dqnw
    g["fox_k_norm_w"] = dknw
    g["memq_norm_w"] = jnp.concatenate([dmqw0, dmqw1], axis=0)
    g["norm2_w"] = jnp.concatenate([dn2w0, dn2w1], axis=0)

    sibd = sibling_rider([g_dn, g_conv])
    run_riders([sibd], name="grads_to_sibling_last")
    ride_last = chips_rider(chip_sums(["dn_w_in", "dn_conv_w"], [g_dn, g_conv], sibd.results))
    ride_small = gather_rider([pack_small(g, last=loss)])
    run_riders([ride_last, ride_small], name="grads_to_chips_last")

    def layers(l0, l1):
        return jnp.stack([l0, l1], axis=1).reshape(4, -1, l0.shape[-1])

    parts = {
        "w_mlp1": layers(ride_l0_rest.results[0], ride_l1.results[1]),
        "w_mlp2": layers(ride_l0_mlp2.results[0], ride_l1.results[0]),
        "w_out": layers(ride_l0_rest.results[1], ride_l1.results[2]),
        "fox_w_in": ride_fox_g.results[0], "w_mem_kv": ride_kv_g.results[0],
        "dn_w_in": ride_last.results[0], "dn_conv_w": ride_last.results[1],
    }
    out = {n: adamw(parts[n], w[n], m[n], v[n], name=f"adamw_{n}") for n, _, _ in BIG}
    small, loss = adamw_small(ride_small.results[0], w, m, v, name="adamw_small")
    return loss, grad_x, out, small


WEIGHTS = ["mem_norm_w", "w_mem_kv", "mem_k_norm_w", "norm1_w", "dn_w_in", "dn_conv_w", "dn_a_log", "dn_dt_bias",
           "dn_o_norm_w", "fox_w_in", "fox_f_bias", "fox_q_norm_w", "fox_k_norm_w", "memq_norm_w", "w_out", "norm2_w",
           "w_mlp1", "w_mlp2"]
DN_IN = 4 * D_MODEL + 2 * N_HEADS + MEM_WIDTH
FOX_IN = 4 * D_MODEL + N_HEADS + MEM_WIDTH
GATE_END = 4 * D_MODEL
OUT_IN = D_MODEL + MEM_WIDTH
BIG = [("w_mem_kv", D_MODEL // N_DEV, D_MODEL), ("dn_w_in", D_MODEL, DN_IN // N_DEV), ("fox_w_in", D_MODEL, FOX_IN // N_DEV),
       ("dn_conv_w", CONV_WIDTH, 3 * D_MODEL // N_DEV), ("w_out", 2 * OUT_IN // N_DEV, D_MODEL),
       ("w_mlp1", 2 * D_MODEL, FF_PIECE), ("w_mlp2", 2 * FF_PIECE, D_MODEL)]
SMALL_TILE = 8 * LANES
SMALL = [(name, shape, -(-math.prod(shape) // SMALL_TILE) * SMALL_TILE) for name, shape in [
    ("mem_norm_w", (D_MODEL,)), ("mem_k_norm_w", (HEAD_DIM,)), ("norm1_w", (2, D_MODEL)), ("dn_a_log", (1, N_HEADS)),
    ("dn_dt_bias", (1, N_HEADS)), ("dn_o_norm_w", (1, HEAD_DIM)), ("fox_f_bias", (1, N_HEADS)),
    ("fox_q_norm_w", (1, HEAD_DIM)), ("fox_k_norm_w", (1, HEAD_DIM)), ("memq_norm_w", (2, HEAD_DIM)), ("norm2_w", (2, D_MODEL))]]
SMALL_ROWS = sum(ln for _, _, ln in SMALL) // LANES + 8


def pack_small(p, last=None):
    def rows(a, ln):
        a = a.reshape(-1)
        return (a if a.shape[0] == ln else jnp.pad(a, (0, ln - a.shape[0]))).reshape(-1, LANES)

    used = sum(ln for _, _, ln in SMALL) // LANES
    tail = jnp.zeros(((SMALL_ROWS - used) * LANES,), F32)
    if last is not None:
        tail = jnp.concatenate([tail[:-1], last.reshape(1)])
    return jnp.concatenate([rows(p[n], ln) for n, _, ln in SMALL] + [tail.reshape(-1, LANES)], axis=0)


def in_proj_weights(gathered, width, n_small):
    full = gathered.transpose(1, 0, 2).reshape(D_MODEL, width)
    main = jnp.concatenate([full[:, :GATE_END], full[:, GATE_END + n_small:]], axis=1)
    return main, jnp.pad(full[:, GATE_END:GATE_END + n_small], ((0, 0), (0, LANES - n_small)))


def in_proj_pieces(d_main, d_small, n_small, width):
    full = jnp.concatenate([d_main[:, :GATE_END], d_small[:, :n_small], d_main[:, GATE_END:]], axis=1)
    return full.reshape(D_MODEL, N_DEV, width // N_DEV).transpose(1, 0, 2)


def _adamw_update(g, w, m, v):
    m_new = ADAM_B1 * m + (1.0 - ADAM_B1) * g
    v_new = ADAM_B2 * v + (1.0 - ADAM_B2) * jnp.square(g)
    m_hat = m_new / (1.0 - ADAM_B1 ** ADAM_STEP)
    v_hat = v_new / (1.0 - ADAM_B2 ** ADAM_STEP)
    return -ADAM_LR * (m_hat / (jnp.sqrt(v_hat) + ADAM_EPS) + ADAM_WD * w), m_new, v_new


def adamw(parts, w, m, v, *, name):
    n, _, cols = parts.shape
    layers = w.shape[0] if w.ndim == 3 else 1
    rows = w.shape[-2]
    tile = _pick(rows, (512, 256, 128))
    steps = rows // tile

    def body(p_ref, w_ref, m_ref, v_ref, g_ref, d_ref, mo_ref, vo_ref):
        g = p_ref[0].astype(F32)
        for i in range(1, n):
            g = g + p_ref[i].astype(F32)
        g_ref[...] = g
        d_ref[...], mo_ref[...], vo_ref[...] = _adamw_update(g, w_ref[...], m_ref[...], v_ref[...])

    if w.ndim == 3:
        spec = pl.BlockSpec((None, tile, cols), lambda l, i: (l, i, 0))
    else:
        spec = pl.BlockSpec((tile, cols), lambda l, i: (i, 0))
    return pl.pallas_call(
        body, name=name, grid=(layers, steps),
        in_specs=[pl.BlockSpec((n, tile, cols), lambda l, i: (0, l * steps + i, 0)), spec, spec, spec], out_specs=[spec] * 4,
        out_shape=[jax.ShapeDtypeStruct(w.shape, F32)] * 4, compiler_params=_params(("parallel", "parallel")),
    )(parts, w, m, v)


def adamw_small(parts, w, m, v, *, name):
    def view(a):
        return a.reshape(-1, LANES) if a.size % LANES == 0 else a.reshape(1, a.size)

    k = len(SMALL)
    ins = [view(d[n]) for d in (w, m, v) for n, _, _ in SMALL]

    def body(p_ref, *refs):
        w_refs, m_refs, v_refs, outs, g_ref = refs[:k], refs[k:2 * k], refs[2 * k:3 * k], refs[3 * k:-1], refs[-1]
        g_all = p_ref[0]
        for i in range(1, N_DEV):
            g_all = g_all + p_ref[i]
        g_ref[...] = g_all
        row = 0
        for i, (_, _, ln) in enumerate(SMALL):
            r, c = w_refs[i].shape
            g = g_ref[row:row + r, 0:c]
            outs[4 * i][...] = g
            outs[4 * i + 1][...], outs[4 * i + 2][...], outs[4 * i + 3][...] = _adamw_update(
                g, w_refs[i][...], m_refs[i][...], v_refs[i][...])
            row += ln // LANES
        outs[-1][...] = g_ref[SMALL_ROWS - 1:SMALL_ROWS, LANES - 1:LANES]

    out_shape = [jax.ShapeDtypeStruct(a.shape, F32) for a in ins[:k] for _ in range(4)] + [jax.ShapeDtypeStruct((1, 1), F32)]
    res = pl.pallas_call(body, name=name, out_shape=out_shape,
                         scratch_shapes=[pltpu.VMEM((SMALL_ROWS, LANES), F32)])(parts, *ins)
    small = {n: [o.reshape(sh) for o in res[4 * i:4 * i + 4]] for i, (n, sh, _) in enumerate(SMALL)}
    return small, res[-1][0, 0]


def kernel(x, mem, mem_norm_w, w_mem_kv, mem_k_norm_w, norm1_w, dn_w_in, dn_conv_w, dn_a_log, dn_dt_bias, dn_o_norm_w, fox_w_in, fox_f_bias, fox_q_norm_w, fox_k_norm_w, memq_norm_w, w_out, norm2_w, w_mlp1, w_mlp2, loss_target, m_mem_norm_w, m_w_mem_kv, m_mem_k_norm_w, m_norm1_w, m_dn_w_in, m_dn_conv_w, m_dn_a_log, m_dn_dt_bias, m_dn_o_norm_w, m_fox_w_in, m_fox_f_bias, m_fox_q_norm_w, m_fox_k_norm_w, m_memq_norm_w, m_w_out, m_norm2_w, m_w_mlp1, m_w_mlp2, v_mem_norm_w, v_w_mem_kv, v_mem_k_norm_w, v_norm1_w, v_dn_w_in, v_dn_conv_w, v_dn_a_log, v_dn_dt_bias, v_dn_o_norm_w, v_fox_w_in, v_fox_f_bias, v_fox_q_norm_w, v_fox_k_norm_w, v_memq_norm_w, v_w_out, v_norm2_w, v_w_mlp1, v_w_mlp2):
    p = dict(mem_norm_w=mem_norm_w, w_mem_kv=w_mem_kv, mem_k_norm_w=mem_k_norm_w, norm1_w=norm1_w, dn_w_in=dn_w_in,
             dn_conv_w=dn_conv_w, dn_a_log=dn_a_log, dn_dt_bias=dn_dt_bias, dn_o_norm_w=dn_o_norm_w, fox_w_in=fox_w_in,
             fox_f_bias=fox_f_bias, fox_q_norm_w=fox_q_norm_w, fox_k_norm_w=fox_k_norm_w, memq_norm_w=memq_norm_w,
             w_out=w_out, norm2_w=norm2_w, w_mlp1=w_mlp1, w_mlp2=w_mlp2)
    pm = dict(mem_norm_w=m_mem_norm_w, w_mem_kv=m_w_mem_kv, mem_k_norm_w=m_mem_k_norm_w, norm1_w=m_norm1_w,
              dn_w_in=m_dn_w_in, dn_conv_w=m_dn_conv_w, dn_a_log=m_dn_a_log, dn_dt_bias=m_dn_dt_bias,
              dn_o_norm_w=m_dn_o_norm_w, fox_w_in=m_fox_w_in, fox_f_bias=m_fox_f_bias, fox_q_norm_w=m_fox_q_norm_w,
              fox_k_norm_w=m_fox_k_norm_w, memq_norm_w=m_memq_norm_w, w_out=m_w_out, norm2_w=m_norm2_w, w_mlp1=m_w_mlp1,
              w_mlp2=m_w_mlp2)
    pv = dict(mem_norm_w=v_mem_norm_w, w_mem_kv=v_w_mem_kv, mem_k_norm_w=v_mem_k_norm_w, norm1_w=v_norm1_w,
              dn_w_in=v_dn_w_in, dn_conv_w=v_dn_conv_w, dn_a_log=v_dn_a_log, dn_dt_bias=v_dn_dt_bias,
              dn_o_norm_w=v_dn_o_norm_w, fox_w_in=v_fox_w_in, fox_f_bias=v_fox_f_bias, fox_q_norm_w=v_fox_q_norm_w,
              fox_k_norm_w=v_fox_k_norm_w, memq_norm_w=v_memq_norm_w, w_out=v_w_out, norm2_w=v_norm2_w, w_mlp1=v_w_mlp1,
              w_mlp2=v_w_mlp2)

    loss, grad_x, results, small = local_step(x[0], mem[0], loss_target[0], p, pm, pv)
    groups = [{n: r[i] for n, r in {**small, **results}.items()} for i in range(4)]
    return (loss, grad_x[None], *[grp[n] for grp in groups for n in WEIGHTS])
```

```python
import functools
import math

import jax
import jax.numpy as jnp
from jax import lax
from jax.experimental import pallas as pl
from jax.experimental.pallas import tpu as pltpu

F32 = jnp.float32
BF16 = jnp.bfloat16
HIGHEST = lax.Precision.HIGHEST

D_MODEL = 1024
HEAD_DIM = 128
N_HEADS = 8
MEM_HEADS = 4
MEM_WIDTH = MEM_HEADS * HEAD_DIM
D_FF = 4 * D_MODEL
CONV_WIDTH = 4
CHUNK = 64
Q_BLOCK = 128
EPS = 1e-6
SCALE = HEAD_DIM ** -0.5
MAIN_WIDTH = 4 * D_MODEL + MEM_WIDTH
LANES = 128
N_DEV = 8

ADAM_LR = 0.001
ADAM_B1 = 0.9
ADAM_B2 = 0.999
ADAM_EPS = 1e-08
ADAM_WD = 0.01
ADAM_STEP = 10

VMEM_LIMIT = 56 * 2 ** 20
MESH = pl.DeviceIdType.MESH


def _bdot(a, b, dims):
    return lax.dot_general(a.astype(BF16), b.astype(BF16), (dims, ((), ())), preferred_element_type=F32)


@jax.custom_vjp
def mm(a, b):
    return _bdot(a, b, ((1,), (0,)))


@jax.custom_vjp
def mm_nt(a, b):
    return _bdot(a, b, ((1,), (1,)))


@jax.custom_vjp
def mm_tn(a, b):
    return _bdot(a, b, ((0,), (0,)))


mm.defvjp(lambda a, b: (mm(a, b), (a, b)), lambda r, g: (mm_nt(g, r[1]), mm_tn(r[0], g)))
mm_nt.defvjp(lambda a, b: (mm_nt(a, b), (a, b)), lambda r, g: (mm(g, r[1]), mm_tn(g, r[0])))
mm_tn.defvjp(lambda a, b: (mm_tn(a, b), (a, b)), lambda r, g: (mm_nt(r[1], g), mm(r[0], g)))


def hdot(a, b):
    return jnp.dot(a, b, precision=HIGHEST, preferred_element_type=F32)


def rms(x, w):
    return x * lax.rsqrt(jnp.mean(x * x, axis=-1, keepdims=True) + EPS) * w


def l2n(x):
    return x * lax.rsqrt(jnp.sum(x * x, axis=-1, keepdims=True) + EPS)


def _iota2(n, m):
    return lax.broadcasted_iota(jnp.int32, (n, m), 0), lax.broadcasted_iota(jnp.int32, (n, m), 1)


def _lower_ones(n):
    r, c = _iota2(n, n)
    return jnp.where(r >= c, 1.0, 0.0).astype(F32)


def _last_row(x):
    r = lax.broadcasted_iota(jnp.int32, x.shape, 0)
    return jnp.sum(jnp.where(r == x.shape[0] - 1, x, 0.0), axis=0, keepdims=True)


def _softmax_rows(z):
    m = lax.stop_gradient(jnp.max(z, axis=-1, keepdims=True))
    e = jnp.exp(z - m)
    return e * (1.0 / jnp.sum(e, axis=-1, keepdims=True))


_BNN = (((2,), (1,)), ((0,), (0,)))
_BNT = (((2,), (2,)), ((0,), (0,)))
_BTN = (((1,), (1,)), ((0,), (0,)))


def _bbdot(a, b, dims):
    return lax.dot_general(a.astype(BF16), b.astype(BF16), dims, preferred_element_type=F32)


@jax.custom_vjp
def bmm(a, b):
    return _bbdot(a, b, _BNN)


@jax.custom_vjp
def bmm_nt(a, b):
    return _bbdot(a, b, _BNT)


@jax.custom_vjp
def bmm_tn(a, b):
    return _bbdot(a, b, _BTN)


@jax.custom_vjp
def bmm_high(a, b):
    return lax.dot_general(a, b, _BNN, precision=lax.Precision.HIGH, preferred_element_type=F32)


bmm.defvjp(lambda a, b: (bmm(a, b), (a, b)), lambda r, g: (bmm_nt(g, r[1]), bmm_tn(r[0], g)))
bmm_nt.defvjp(lambda a, b: (bmm_nt(a, b), (a, b)), lambda r, g: (bmm(g, r[1]), bmm_tn(g, r[0])))
bmm_tn.defvjp(lambda a, b: (bmm_tn(a, b), (a, b)), lambda r, g: (bmm_nt(r[1], g), bmm(r[0], g)))
bmm_high.defvjp(lambda a, b: (bmm_high(a, b), (a, b)), lambda r, g: (bmm_nt(g, r[1]), bmm_tn(r[0], g)))

NEUMANN_HIGH_LEVELS = 2


@jax.custom_vjp
def inv_unit_lower(a):
    n = a.shape[-1]
    r, c = _iota2(n, n)
    p = jnp.where(r == c, 1.0, 0.0).astype(F32) - a
    ak = a
    for level in range(int(math.log2(n)) - 1):
        dot = bmm_high if level < NEUMANN_HIGH_LEVELS else bmm
        ak = dot(ak, ak)
        p = p + dot(p, ak)
    return p


def _inv_unit_lower_fwd(a):
    t = inv_unit_lower(a)
    return t, t


def _inv_unit_lower_bwd(t, g):
    return (-bmm_tn(t, bmm_nt(g, t)),)


inv_unit_lower.defvjp(_inv_unit_lower_fwd, _inv_unit_lower_bwd)


def delta_intra(q, k, v, gc, beta):
    b, c, _ = q.shape
    r, cc = _iota2(c, c)
    causal = r >= cc
    strict = r > cc
    gi = jnp.broadcast_to(gc, (b, c, c))
    gj = jnp.swapaxes(gi, 1, 2)
    decay = jnp.where(causal, jnp.exp(jnp.where(causal, gi - gj, 0.0)), 0.0)
    kb = k * beta
    a = jnp.where(strict, bmm_nt(kb, k) * decay, 0.0)
    t = inv_unit_lower(a)
    u = bmm(t, v * beta)
    w = bmm(t, kb * jnp.exp(gc))
    qk = jnp.where(causal, bmm_nt(q, k) * decay, 0.0)
    return u, w, qk


def delta_step(s, q, k, gc, u, w, qk):
    v_new = u - bmm(w, s)
    out = bmm(q * jnp.exp(gc), s) + bmm(qk, v_new)
    r = lax.broadcasted_iota(jnp.int32, gc.shape, 1)
    g_last = jnp.sum(jnp.where(r == gc.shape[1] - 1, gc, 0.0), axis=1, keepdims=True)
    k_dec = k * jnp.exp(g_last - gc)
    s_new = s * jnp.exp(g_last) + bmm_tn(k_dec, v_new)
    return out, s_new


def fox_probs(q, k, fq, fk, qpos0):
    s = lax.dot_general(q, k, (((1,), (1,)), ((), ())), preferred_element_type=F32)
    r, c = _iota2(s.shape[0], s.shape[1])
    return _softmax_rows(jnp.where(c <= (r + qpos0), s + (fq - fk), -jnp.inf))


def mem_head(qm, wq, mk, mv):
    p = _softmax_rows(mm_nt(rms(qm, wq) * SCALE, mk))
    return mm(p, mv)


def _heads(x, n):
    return [x[:, h * HEAD_DIM:(h + 1) * HEAD_DIM] for h in range(n)]


def memkv_fn(mem, mnw, wkv, mknw):
    kv = mm(rms(mem, mnw), wkv)
    mk = jnp.concatenate([rms(kh, mknw) for kh in _heads(kv[:, :MEM_WIDTH], MEM_HEADS)], axis=1)
    return mk, kv[:, MEM_WIDTH:]


def dn_gates_fn(ab, alog, dtb):
    g = -jnp.exp(alog) * jax.nn.softplus(ab + dtb)
    low = _lower_ones(CHUNK)
    gc = jnp.concatenate([hdot(low, g[i * CHUNK:(i + 1) * CHUNK]) for i in range(ab.shape[0] // CHUNK)], axis=0)
    lane = lax.broadcasted_iota(jnp.int32, ab.shape, 1)
    return jnp.where(lane < N_HEADS, gc, jax.nn.sigmoid(ab))


def fox_fcum_fn(fp, fbias):
    lf = jax.nn.log_sigmoid(fp + fbias)
    low = _lower_ones(LANES)
    carry = jnp.zeros((1, fp.shape[1]), F32)
    outs = []
    for i in range(fp.shape[0] // LANES):
        cs = hdot(low, lf[i * LANES:(i + 1) * LANES]) + carry
        carry = _last_row(cs)
        outs.append(cs)
    return jnp.concatenate(outs, axis=0)


def fox_qk_fn(qraw, kraw, qnw, knw):
    q = jnp.concatenate([rms(x, qnw) * SCALE for x in _heads(qraw, N_HEADS)], axis=1)
    k = jnp.concatenate([rms(x, knw) for x in _heads(kraw, N_HEADS)], axis=1)
    return q, k


def _mem_out(qm, mqw, mk, mv):
    return [mem_head(a, mqw, b, c) for a, b, c in zip(_heads(qm, MEM_HEADS), _heads(mk, MEM_HEADS), _heads(mv, MEM_HEADS))]


def dn_out_fn(o, z, qm, onw, mqw, mk, mv):
    mix = [rms(a, onw) * jax.nn.silu(b) for a, b in zip(o, _heads(z, N_HEADS))]
    return jnp.concatenate(mix + _mem_out(qm, mqw, mk, mv), axis=1)


def fox_out_fn(o, gate, qm, mqw, mk, mv):
    return jnp.concatenate([o * jax.nn.sigmoid(gate)] + _mem_out(qm, mqw, mk, mv), axis=1)


_HBM = pl.BlockSpec(memory_space=pltpu.HBM)


def _place():
    return lax.axis_index("x"), lax.axis_index("y"), lax.axis_index("c")


class Rider:
    def __init__(self, ins, out_shape, scratch, start, finish):
        self.ins, self.out_shape, self.scratch, self.start, self.finish = list(ins), list(out_shape), list(scratch), start, finish
        self.results = None


def gather_rider(xs):
    n = len(xs)

    def plan(x_refs, out_refs, sems):
        send_sems, recv_sems, local_sems = sems
        x, y, c = _place()
        me, sibling = (x, y, c), (x, y, 1 - c)
        chips = [(1 - x, y), (x, 1 - y), (1 - x, 1 - y)]

        def copy(a, k, block, to, src=None):
            px, py, pc = block
            dst = out_refs[a].at[4 * px + 2 * py + pc]
            return pltpu.make_async_remote_copy(
                src_ref=dst if src is None else src, dst_ref=dst,
                send_sem=send_sems.at[a, k], recv_sem=recv_sems.at[a, k], device_id=to, device_id_type=MESH)

        mine = [pltpu.make_async_copy(x_refs[a], out_refs[a].at[4 * x + 2 * y + c], local_sems.at[a]) for a in range(n)]
        first = [copy(a, 0, me, sibling, src=x_refs[a]) for a in range(n)]
        first += [copy(a, 1 + j, me, (*chip, c), src=x_refs[a]) for j, chip in enumerate(chips) for a in range(n)]
        return copy, me, sibling, chips, mine, first

    def start(x_refs, out_refs, sems):
        _, _, _, _, mine, first = plan(x_refs, out_refs, sems)
        for cp in mine + first:
            cp.start()

    def finish(x_refs, out_refs, sems):
        copy, me, sibling, chips, mine, first = plan(x_refs, out_refs, sems)
        _, _, c = me
        passed = []
        for j, chip in enumerate(chips):
            for a in range(n):
                copy(a, 1 + j, (*chip, c), me).wait_recv()
                passed.append(copy(a, 4 + j, (*chip, c), sibling))
                passed[-1].start()
        for a in range(n):
            copy(a, 0, sibling, me).wait_recv()
        for j, chip in enumerate(chips):
            for a in range(n):
                copy(a, 4 + j, (*chip, 1 - c), me).wait_recv()
        for cp in first + passed:
            cp.wait_send()
        for cp in mine:
            cp.wait()

    return Rider(xs, [jax.ShapeDtypeStruct((N_DEV,) + a.shape, a.dtype) for a in xs],
                 [pltpu.SemaphoreType.DMA((n, 7)), pltpu.SemaphoreType.DMA((n, 7)), pltpu.SemaphoreType.DMA((n,))], start, finish)


def sibling_rider(gs):
    n = len(gs)

    def plan(g_refs, out_refs, sems):
        send_sems, recv_sems = sems
        x, y, c = _place()
        return [pltpu.make_async_remote_copy(
            src_ref=g_refs[a].at[2 * k + 1 - c], dst_ref=out_refs[a].at[k], send_sem=send_sems.at[a, k],
            recv_sem=recv_sems.at[a, k], device_id=(x, y, 1 - c), device_id_type=MESH) for a in range(n) for k in range(4)]

    def start(g_refs, out_refs, sems):
        for cp in plan(g_refs, out_refs, sems):
            cp.start()

    def finish(g_refs, out_refs, sems):
        copies = plan(g_refs, out_refs, sems)
        for cp in copies:
            cp.wait_recv()
        for cp in copies:
            cp.wait_send()

    return Rider(gs, [jax.ShapeDtypeStruct((4,) + g.shape[1:], g.dtype) for g in gs],
                 [pltpu.SemaphoreType.DMA((n, 4)), pltpu.SemaphoreType.DMA((n, 4))], start, finish)


def chips_rider(hs):
    n = len(hs)

    def plan(h_refs, out_refs, sems):
        send_sems, recv_sems, local_sems = sems
        x, y, c = _place()
        mine = 2 * x + y
        chips = [(1 - x, y), (x, 1 - y), (1 - x, 1 - y)]
        keep = [pltpu.make_async_copy(h_refs[a].at[mine], out_refs[a].at[mine], local_sems.at[a]) for a in range(n)]
        sends = [pltpu.make_async_remote_copy(
            src_ref=h_refs[a].at[2 * qx + qy], dst_ref=out_refs[a].at[mine], send_sem=send_sems.at[a, j],
            recv_sem=recv_sems.at[a, j], device_id=(qx, qy, c), device_id_type=MESH)
            for j, (qx, qy) in enumerate(chips) for a in range(n)]
        recvs = [pltpu.make_async_remote_copy(
            src_ref=h_refs[a].at[mine], dst_ref=out_refs[a].at[2 * qx + qy], send_sem=send_sems.at[a, j],
            recv_sem=recv_sems.at[a, j], device_id=(qx, qy, c), device_id_type=MESH)
            for j, (qx, qy) in enumerate(chips) for a in range(n)]
        return keep, sends, recvs

    def start(h_refs, out_refs, sems):
        keep, sends, _ = plan(h_refs, out_refs, sems)
        for cp in keep + sends:
            cp.start()

    def finish(h_refs, out_refs, sems):
        keep, sends, recvs = plan(h_refs, out_refs, sems)
        for cp in recvs:
            cp.wait_recv()
        for cp in sends:
            cp.wait_send()
        for cp in keep:
            cp.wait()

    return Rider(hs, [jax.ShapeDtypeStruct(h.shape, h.dtype) for h in hs],
                 [pltpu.SemaphoreType.DMA((n, 3)), pltpu.SemaphoreType.DMA((n, 3)), pltpu.SemaphoreType.DMA((n,))], start, finish)


def hosted_call(riders, body, *, out_shape, in_specs, out_specs, grid=(), scratch_shapes=(), **kw):
    riders = tuple(riders or ())
    if not riders:
        return pl.pallas_call(body, out_shape=out_shape, in_specs=in_specs, out_specs=out_specs, grid=grid,
                              scratch_shapes=scratch_shapes, **kw)
    single = not isinstance(out_shape, (list, tuple))
    k_out_shape = [out_shape] if single else list(out_shape)
    k_out_specs = [out_specs] if single else list(out_specs)
    n_in, n_out, n_scr = len(in_specs), len(k_out_shape), len(scratch_shapes)
    r_ins = [a for r in riders for a in r.ins]
    r_outs = [s for r in riders for s in r.out_shape]
    r_scr = [s for r in riders for s in r.scratch]

    def full_body(*refs):
        ins = refs[:n_in + len(r_ins)]
        outs = refs[n_in + len(r_ins):n_in + len(r_ins) + n_out + len(r_outs)]
        scr = refs[n_in + len(r_ins) + n_out + len(r_outs):]
        steps = math.prod(grid)
        step = 0
        for d, g in enumerate(grid):
            step = step * g + pl.program_id(d)

        def each(method):
            i0, o0, s0 = n_in, n_out, n_scr
            for r in riders:
                getattr(r, method)(ins[i0:i0 + len(r.ins)], outs[o0:o0 + len(r.out_shape)], scr[s0:s0 + len(r.scratch)])
                i0, o0, s0 = i0 + len(r.ins), o0 + len(r.out_shape), s0 + len(r.scratch)

        if steps == 1:
            each("start")
            body(*ins[:n_in], *outs[:n_out], *scr[:n_scr])
            each("finish")
        else:
            pl.when(step == 0)(lambda: each("start"))
            body(*ins[:n_in], *outs[:n_out], *scr[:n_scr])
            pl.when(step == steps - 1)(lambda: each("finish"))

    call = pl.pallas_call(
        full_body, out_shape=k_out_shape + r_outs, in_specs=list(in_specs) + [_HBM] * len(r_ins),
        out_specs=k_out_specs + [_HBM] * len(r_outs), grid=grid, scratch_shapes=list(scratch_shapes) + r_scr, **kw)

    def run(*args):
        res = call(*args, *r_ins)
        o0 = n_out
        for r in riders:
            r.results = list(res[o0:o0 + len(r.out_shape)])
            o0 += len(r.out_shape)
        return res[0] if single else list(res[:n_out])

    return run


def run_riders(riders, *, name):
    hosted_call(riders, lambda: None, name=name, out_shape=[], in_specs=[], out_specs=[])()
    return [r.results for r in riders]


def _pick(n, cands):
    for c in cands:
        if n % c == 0:
            return c
    return n


def _params(sem):
    return pltpu.CompilerParams(dimension_semantics=sem, vmem_limit_bytes=VMEM_LIMIT)


MATMUL_VMEM_BUDGET = 40 * 2 ** 20


def _matmul_tiles(m, n, k, bytes_a, bytes_b, bytes_mn, fixed):
    fm, fn, fk = fixed if fixed is not None else (None, None, None)

    def options(given, size, cands):
        return [given] if given else ([c for c in cands if size % c == 0] or [size])

    best = None
    for tm in options(fm, m, (2048, 1024, 512, 256, 128)):
        for tn in options(fn, n, (512, 256, 128)):
            for tk in options(fk, k, (2048, 1536, 1024, 512, 256, 128)):
                if 2 * (tm * tk * bytes_a + tk * tn * bytes_b + tm * tn * bytes_mn) + tm * tn * 4 > MATMUL_VMEM_BUDGET:
                    continue
                key = ((m // tm) * (n // tn) * (k // tk), -tk)
                if best is None or key < best[0]:
                    best = (key, (tm, tn, tk))
    assert best is not None, (m, n, k, fixed)
    return best[1]


def matmul(a, b, *, name, ta=False, tb=False, post=None, post_ins=(), row_ins=(), acc=False, extra_out=None,
           out_dtype=F32, tiles=None, b_view=None, out_view=None, riders=()):
    (k, m) = a.shape if ta else a.shape[::-1]
    (kb, n) = b_view[:2] if b_view is not None else (b.shape[::-1] if tb else b.shape)
    assert k == kb, (a.shape, b.shape, ta, tb)
    bytes_mn = sum(p.dtype.itemsize for p in post_ins) + jnp.dtype(out_dtype).itemsize
    bytes_mn += jnp.dtype(extra_out[1]).itemsize if extra_out else 0
    tm, tn, tk = _matmul_tiles(m, n, k, a.dtype.itemsize, b.dtype.itemsize, bytes_mn, tiles)
    assert not acc or tn == n, (name, tn, n)
    nk = k // tk
    dims = ((0,) if ta else (1,), (1,) if tb else (0,))
    n_post, n_row = len(post_ins), len(row_ins)
    n_out = 1 + bool(extra_out) + bool(acc)

    def body(*refs):
        a_ref, b_ref = refs[:2]
        post_refs = refs[2:2 + n_post + n_row]
        o_refs, acc_ref = refs[-1 - n_out:-1], refs[-1]
        first_rows, kk = pl.program_id(0) == 0, pl.program_id(2)

        @pl.when(kk == 0)
        def _():
            acc_ref[...] = jnp.zeros_like(acc_ref)

        b_tile = b_ref[...]
        acc_ref[...] += _bdot(a_ref[...], b_tile.reshape(-1, b_tile.shape[-1]), dims)

        @pl.when(kk == nk - 1)
        def _():
            r = acc_ref[...]
            rows = [p[...] for p in post_refs[n_post:]]
            if post is not None:
                r = post(r, *[p[...] for p in post_refs[:n_post]], *rows)
            if acc:
                r, s = r
                sum_ref = o_refs[-1]

                @pl.when(first_rows)
                def _():
                    sum_ref[...] = s

                @pl.when(jnp.logical_not(first_rows))
                def _():
                    sum_ref[...] += s

            o_refs[0][...] = r.astype(out_dtype)
            if extra_out:
                o_refs[1][...] = extra_out[0](r, *rows).astype(extra_out[1])

    a_spec = pl.BlockSpec((tk, tm), lambda i, j, kk: (kk, i)) if ta else pl.BlockSpec((tm, tk), lambda i, j, kk: (i, kk))
    if b_view is not None:
        b_spec = b_view[2]
    else:
        b_spec = pl.BlockSpec((tn, tk), lambda i, j, kk: (j, kk)) if tb else pl.BlockSpec((tk, tn), lambda i, j, kk: (kk, j))
    mn_spec = pl.BlockSpec((tm, tn), lambda i, j, kk: (i, j))
    row_spec = pl.BlockSpec((1, tn), lambda i, j, kk: (0, j))
    o_shape, o_spec = ((m, n), mn_spec) if out_view is None else out_view
    out_shape = [jax.ShapeDtypeStruct(o_shape, out_dtype)]
    out_specs = [o_spec]
    if extra_out:
        out_shape.append(jax.ShapeDtypeStruct((m, n), extra_out[1]))
        out_specs.append(mn_spec)
    if acc:
        out_shape.append(jax.ShapeDtypeStruct((1, n), F32))
        out_specs.append(row_spec)
    res = hosted_call(
        riders, body, name=name, grid=(m // tm, n // tn, nk),
        in_specs=[a_spec, b_spec] + [mn_spec] * n_post + [row_spec] * n_row, out_specs=out_specs, out_shape=out_shape,
        scratch_shapes=[pltpu.VMEM((tm, tn), F32)],
        compiler_params=_params(("arbitrary" if acc else "parallel", "parallel", "arbitrary")),
    )(a, b, *post_ins, *row_ins)
    return res if n_out > 1 else res[0]


def rows_call(fn, row_ins, full_ins, row_outs, acc_outs, *, tm, name, riders=()):
    row_ins = [r if isinstance(r, tuple) else (r, r.shape[-1], 0) for r in row_ins]
    t = row_ins[0][0].shape[-2]
    tm = min(tm, t)
    n_in = len(row_ins) + len(full_ins)
    n_row = len(row_outs)

    def body(*refs):
        res = fn(*[[r[h] for h in range(r.shape[0])] if (i < len(row_ins) and len(r.shape) == 3) else r[...]
                   for i, r in enumerate(refs[:n_in])])
        res = res if isinstance(res, (tuple, list)) else (res,)
        outs = refs[n_in:]
        for ref, val in zip(outs[:n_row], res[:n_row]):
            if len(ref.shape) == 3:
                for h, vh in enumerate(val):
                    ref[h] = vh.astype(ref.dtype)
            else:
                ref[...] = val.astype(ref.dtype)
        first = pl.program_id(0) == 0
        for ref, val in zip(outs[n_row:], res[n_row:]):
            @pl.when(first)
            def _(ref=ref, val=val):
                ref[...] = val

            @pl.when(jnp.logical_not(first))
            def _(ref=ref, val=val):
                ref[...] += val

    def full_spec(shape):
        return pl.BlockSpec(shape, lambda i, nd=len(shape): (0,) * nd)

    def row_spec(lead, w, cb):
        if lead is None:
            return pl.BlockSpec((tm, w), lambda i: (i, cb))
        return pl.BlockSpec((lead, tm, w), lambda i: (0, i, cb))

    def lead_cols(c):
        return c if isinstance(c, tuple) else (None, c)

    in_specs = [row_spec(a.shape[0] if a.ndim == 3 else None, w, cb) for (a, w, cb) in row_ins]
    in_specs += [full_spec(f.shape) for f in full_ins]
    out_specs = [row_spec(*lead_cols(c), 0) for c, _ in row_outs] + [full_spec(s) for s in acc_outs]
    out_shape = [jax.ShapeDtypeStruct(tuple(d for d in (lead_cols(c)[0], t, lead_cols(c)[1]) if d is not None), dt)
                 for c, dt in row_outs] + [jax.ShapeDtypeStruct(s, F32) for s in acc_outs]
    res = hosted_call(
        riders, body, name=name, grid=(t // tm,), in_specs=in_specs, out_specs=out_specs, out_shape=out_shape,
        compiler_params=_params(("arbitrary",)),
    )(*[r[0] for r in row_ins], *full_ins)
    return res


def vjp_rows(fn, n_diff_row, row_diff_full):
    def bwd(*args, n_row, n_ct):
        prim_rows = args[:n_row]
        cts = args[n_row:n_row + n_ct]
        fulls = args[n_row + n_ct:]
        _, vjp = jax.vjp(fn, *prim_rows, *fulls)
        g = vjp(cts[0] if n_ct == 1 else tuple(cts))
        out = list(g[:n_diff_row])
        out += [gf for gf, d in zip(g[n_row:], row_diff_full) if d]
        return tuple(out)
    return bwd


def _shift_down(x, s):
    if s == 0:
        return x
    t = lax.broadcasted_iota(jnp.int32, x.shape, 0)
    return jnp.where(t >= s, pltpu.roll(x, s, 0), 0.0)


def _shift_up(x, s):
    if s == 0:
        return x
    n = x.shape[0]
    t = lax.broadcasted_iota(jnp.int32, x.shape, 0)
    return jnp.where(t < n - s, pltpu.roll(x, n - s, 0), 0.0)


def _conv(x, w_ref):
    return sum(w_ref[pl.ds(j, 1), :] * _shift_down(x, CONV_WIDTH - 1 - j) for j in range(CONV_WIDTH))


_DN_POST = (lambda c: l2n(jax.nn.silu(c)) * SCALE, lambda c: l2n(jax.nn.silu(c)), jax.nn.silu)


def dn_prep_fwd(proj, conv_w, *, name, riders=()):
    t = proj.shape[0]

    def body(xq, xk, xv, wq, wk, wv, oq, ok, ov):
        for x_ref, w_ref, o_ref, post in zip((xq, xk, xv), (wq, wk, wv), (oq, ok, ov), _DN_POST):
            o_ref[...] = post(_conv(x_ref[...], w_ref))

    x_specs = [pl.BlockSpec((t, HEAD_DIM), lambda h, g=g: (0, g * N_HEADS + h)) for g in range(3)]
    w_specs = [pl.BlockSpec((CONV_WIDTH, HEAD_DIM), lambda h, g=g: (0, g * N_HEADS + h)) for g in range(3)]
    o_spec = pl.BlockSpec((None, t, HEAD_DIM), lambda h: (h, 0, 0))
    return hosted_call(
        riders, body, name=name, grid=(N_HEADS,), in_specs=x_specs + w_specs, out_specs=[o_spec] * 3,
        out_shape=[jax.ShapeDtypeStruct((N_HEADS, t, HEAD_DIM), F32)] * 3, compiler_params=_params(("parallel",)),
    )(proj, proj, proj, conv_w, conv_w, conv_w)


def dn_prep_bwd(proj, conv_w, dq, dk, dv, *, name, riders=()):
    t = proj.shape[0]

    def body(xq, xk, xv, wq, wk, wv, gq, gk, gv, dxq, dxk, dxv, dwq, dwk, dwv):
        for x_ref, w_ref, g_ref, dx_ref, dw_ref, post in zip(
                (xq, xk, xv), (wq, wk, wv), (gq, gk, gv), (dxq, dxk, dxv), (dwq, dwk, dwv), _DN_POST):
            x = x_ref[...]
            _, vjp = jax.vjp(post, _conv(x, w_ref))
            dc, = vjp(g_ref[...])
            dx = sum(w_ref[pl.ds(j, 1), :] * _shift_up(dc, CONV_WIDTH - 1 - j) for j in range(CONV_WIDTH))
            dx_ref[...] = dx.astype(dx_ref.dtype)
            for j in range(CONV_WIDTH):
                dw_ref[pl.ds(j, 1), :] = jnp.sum(dc * _shift_down(x, CONV_WIDTH - 1 - j), axis=0, keepdims=True)

    x_specs = [pl.BlockSpec((t, HEAD_DIM), lambda h, g=g: (0, g * N_HEADS + h)) for g in range(3)]
    w_specs = [pl.BlockSpec((CONV_WIDTH, HEAD_DIM), lambda h, g=g: (0, g * N_HEADS + h)) for g in range(3)]
    g_spec = pl.BlockSpec((None, t, HEAD_DIM), lambda h: (h, 0, 0))
    dx_spec = pl.BlockSpec((t, HEAD_DIM), lambda h: (0, h))
    dw_spec = pl.BlockSpec((CONV_WIDTH, HEAD_DIM), lambda h: (0, h))
    return hosted_call(
        riders, body, name=name, grid=(N_HEADS,), in_specs=x_specs + w_specs + [g_spec] * 3, out_specs=[dx_spec] * 3 + [dw_spec] * 3,
        out_shape=[jax.ShapeDtypeStruct((t, D_MODEL), BF16)] * 3 + [jax.ShapeDtypeStruct((CONV_WIDTH, D_MODEL), F32)] * 3,
        compiler_params=_params(("parallel",)),
    )(proj, proj, proj, conv_w, conv_w, conv_w, dq, dk, dv)


INTRA_CHUNKS = 4


def _lane_column(x, lane_index):
    lane = lax.broadcasted_iota(jnp.int32, x.shape, 1)
    return jnp.sum(jnp.where(lane == lane_index, x, 0.0), axis=1, keepdims=True)


def _head_columns(g, first_lane):
    return jnp.concatenate([_lane_column(g, first_lane + h)[None] for h in range(N_HEADS)], axis=0)


def _intra_of_gates(q, k, v, gates):
    nb = N_HEADS * (gates.shape[0] // CHUNK)

    def chunks(x):
        return x.reshape(nb, CHUNK, x.shape[-1])

    res = delta_intra(chunks(q), chunks(k), chunks(v), chunks(_head_columns(gates, 0)), chunks(_head_columns(gates, N_HEADS)))
    return tuple(x.reshape(N_HEADS, -1, x.shape[-1]) for x in res)


def _step_of_gates(s, q, k, gates, u, w, qk):
    return delta_step(s, q, k, _head_columns(gates, 0), u, w, qk)


def _head_major(rows, w, index):
    return pl.BlockSpec((N_HEADS, rows, w), lambda i: (0, index(i), 0))


def delta_intra_fwd(q, k, v, gates, *, name, riders=()):
    t = q.shape[1]
    rows = min(INTRA_CHUNKS, t // CHUNK) * CHUNK

    def body(q_ref, k_ref, v_ref, g_ref, u_ref, w_ref, qk_ref):
        for ref, val in zip((u_ref, w_ref, qk_ref), _intra_of_gates(q_ref[...], k_ref[...], v_ref[...], g_ref[...])):
            ref[...] = val

    x_spec, qk_spec = (_head_major(rows, w, lambda i: i) for w in (HEAD_DIM, CHUNK))
    g_spec = pl.BlockSpec((rows, LANES), lambda i: (i, 0))
    return hosted_call(
        riders, body, name=name, grid=(t // rows,), in_specs=[x_spec] * 3 + [g_spec], out_specs=[x_spec, x_spec, qk_spec],
        out_shape=[jax.ShapeDtypeStruct((N_HEADS, t, HEAD_DIM), F32)] * 2 + [jax.ShapeDtypeStruct((N_HEADS, t, CHUNK), F32)],
        compiler_params=_params(("parallel",)),
    )(q, k, v, gates)


def delta_seq_fwd(q, k, gates, u, w, qk, *, name, riders=()):
    t = q.shape[1]
    nc = t // CHUNK

    def body(q_ref, k_ref, g_ref, u_ref, w_ref, qk_ref, o_ref, s0_ref, s_ref):
        @pl.when(pl.program_id(0) == 0)
        def _():
            s_ref[...] = jnp.zeros_like(s_ref)

        s = s_ref[...]
        s0_ref[...] = s
        o, s_new = _step_of_gates(s, q_ref[...], k_ref[...], g_ref[...], u_ref[...], w_ref[...], qk_ref[...])
        o_ref[...] = o
        s_ref[...] = s_new

    x_spec, qk_spec = (_head_major(CHUNK, w, lambda c: c) for w in (HEAD_DIM, CHUNK))
    g_spec = pl.BlockSpec((CHUNK, LANES), lambda c: (c, 0))
    s_spec = pl.BlockSpec((N_HEADS, None, HEAD_DIM, HEAD_DIM), lambda c: (0, c, 0, 0))
    return hosted_call(
        riders, body, name=name, grid=(nc,), in_specs=[x_spec, x_spec, g_spec, x_spec, x_spec, qk_spec], out_specs=[x_spec, s_spec],
        out_shape=[jax.ShapeDtypeStruct((N_HEADS, t, HEAD_DIM), F32),
                   jax.ShapeDtypeStruct((N_HEADS, nc, HEAD_DIM, HEAD_DIM), F32)],
        scratch_shapes=[pltpu.VMEM((N_HEADS, HEAD_DIM, HEAD_DIM), F32)],
        compiler_params=_params(("arbitrary",)),
    )(q, k, gates, u, w, qk)


def delta_seq_bwd(q, k, gates, u, w, qk, s0, do, *, name, riders=()):
    t = q.shape[1]
    nc = t // CHUNK

    def body(q_ref, k_ref, g_ref, u_ref, w_ref, qk_ref, s0_ref, do_ref,
             dq_ref, dk_ref, dg_ref, du_ref, dw_ref, dqk_ref, ds_ref):
        @pl.when(pl.program_id(0) == 0)
        def _():
            ds_ref[...] = jnp.zeros_like(ds_ref)

        _, vjp = jax.vjp(_step_of_gates, s0_ref[...], q_ref[...], k_ref[...], g_ref[...], u_ref[...], w_ref[...], qk_ref[...])
        ds, dq, dk, dg, du, dw, dqk = vjp((do_ref[...], ds_ref[...]))
        for ref, val in zip((ds_ref, dq_ref, dk_ref, dg_ref, du_ref, dw_ref, dqk_ref), (ds, dq, dk, dg, du, dw, dqk)):
            ref[...] = val

    x_spec, qk_spec = (_head_major(CHUNK, w, lambda c: nc - 1 - c) for w in (HEAD_DIM, CHUNK))
    g_spec = pl.BlockSpec((CHUNK, LANES), lambda c: (nc - 1 - c, 0))
    s_spec = pl.BlockSpec((N_HEADS, None, HEAD_DIM, HEAD_DIM), lambda c: (0, nc - 1 - c, 0, 0))
    head_shape = [jax.ShapeDtypeStruct((N_HEADS, t, w_), F32) for w_ in (HEAD_DIM, HEAD_DIM, HEAD_DIM, HEAD_DIM, CHUNK)]
    return hosted_call(
        riders, body, name=name, grid=(nc,), in_specs=[x_spec, x_spec, g_spec, x_spec, x_spec, qk_spec, s_spec, x_spec],
        out_specs=[x_spec, x_spec, g_spec, x_spec, x_spec, qk_spec],
        out_shape=head_shape[:2] + [jax.ShapeDtypeStruct((t, LANES), F32)] + head_shape[2:],
        scratch_shapes=[pltpu.VMEM((N_HEADS, HEAD_DIM, HEAD_DIM), F32)],
        compiler_params=_params(("arbitrary",)),
    )(q, k, gates, u, w, qk, s0, do)


def delta_intra_bwd(q, k, v, gates, du, dw, dqk, dq_s, dk_s, dg_s, *, name, riders=()):
    t = q.shape[1]
    rows = min(INTRA_CHUNKS, t // CHUNK) * CHUNK

    def body(q_ref, k_ref, v_ref, g_ref, du_ref, dw_ref, dqk_ref, dqs_ref, dks_ref, dgs_ref, dq_ref, dk_ref, dv_ref, dg_ref):
        _, vjp = jax.vjp(_intra_of_gates, q_ref[...], k_ref[...], v_ref[...], g_ref[...])
        dq, dk, dv, dg = vjp((du_ref[...], dw_ref[...], dqk_ref[...]))
        dq_ref[...] = dq + dqs_ref[...]
        dk_ref[...] = dk + dks_ref[...]
        dv_ref[...] = dv
        dg_ref[...] = dg + dgs_ref[...]

    x_spec, qk_spec = (_head_major(rows, w, lambda i: i) for w in (HEAD_DIM, CHUNK))
    g_spec = pl.BlockSpec((rows, LANES), lambda i: (i, 0))
    return hosted_call(
        riders, body, name=name, grid=(t // rows,),
        in_specs=[x_spec] * 3 + [g_spec, x_spec, x_spec, qk_spec, x_spec, x_spec, g_spec],
        out_specs=[x_spec] * 3 + [g_spec],
        out_shape=[jax.ShapeDtypeStruct((N_HEADS, t, HEAD_DIM), F32)] * 3 + [jax.ShapeDtypeStruct((t, LANES), F32)],
        compiler_params=_params(("parallel",)),
    )(q, k, v, gates, du, dw, dqk, dq_s, dk_s, dg_s)


_V_BLOCK = 2 * N_HEADS
FOX_GROUPS = 16


def _fox_groups(t):
    nq = t // Q_BLOCK
    per = max(1, nq // FOX_GROUPS)
    return [(g0, per, (g0 + per) * Q_BLOCK) for g0 in range(0, nq, per)]


def fox_attn_fwd(q, k, proj, fq, fk, *, name, riders=()):
    t = q.shape[0]

    def body(q_ref, k_ref, v_ref, fq_ref, fk_ref, o_ref, kb_ref, vb_ref):
        head = pl.program_id(0)
        kb_ref[...] = k_ref[...].astype(BF16)
        vb_ref[...] = v_ref[...].astype(BF16)
        for g0, per, keys in _fox_groups(t):
            def block(j, carry, g0=g0, keys=keys):
                rows = pl.ds((g0 + j) * Q_BLOCK, Q_BLOCK)
                p = fox_probs(q_ref[rows, :].astype(BF16), kb_ref[0:keys, :], _lane_column(fq_ref[rows, :], head),
                              fk_ref[:, 0:keys], (g0 + j) * Q_BLOCK)
                o_ref[rows, :] = jnp.dot(p.astype(BF16), vb_ref[0:keys, :], preferred_element_type=F32)
                return carry
            for j in range(per):
                block(j, 0)

    x_spec = pl.BlockSpec((t, HEAD_DIM), lambda h: (0, h))
    v_spec = pl.BlockSpec((t, HEAD_DIM), lambda h: (0, _V_BLOCK + h))
    fq_spec = pl.BlockSpec((t, LANES), lambda h: (0, 0))
    fk_spec = pl.BlockSpec((None, 1, t), lambda h: (h, 0, 0))
    return hosted_call(
        riders, body, name=name, grid=(N_HEADS,), in_specs=[x_spec, x_spec, v_spec, fq_spec, fk_spec], out_specs=x_spec,
        out_shape=jax.ShapeDtypeStruct((t, D_MODEL), F32), scratch_shapes=[pltpu.VMEM((t, HEAD_DIM), BF16)] * 2,
        compiler_params=_params(("parallel",)),
    )(q, k, proj, fq, fk)


def fox_attn_bwd(q, k, proj, fq, fk, do, *, name, riders=()):
    t = q.shape[0]

    def body(q_ref, k_ref, v_ref, fq_ref, fk_ref, do_ref, dq_ref, dk_ref, dv_out_ref, dfq_ref, dfk_ref, kb_ref, vb_ref, dv_ref):
        head = pl.program_id(0)

        @pl.when(head == 0)
        def _():
            dfq_ref[...] = jnp.zeros_like(dfq_ref)

        kb_ref[...] = k_ref[...].astype(BF16)
        vb_ref[...] = v_ref[...].astype(BF16)
        dk_ref[...] = jnp.zeros_like(dk_ref)
        dv_ref[...] = jnp.zeros_like(dv_ref)
        dfk_ref[...] = jnp.zeros_like(dfk_ref)
        nt = (((1,), (1,)), ((), ()))
        tn = (((0,), (0,)), ((), ()))
        for g0, per, keys in _fox_groups(t):
            def block(j, carry, g0=g0, keys=keys):
                rows = pl.ds((g0 + j) * Q_BLOCK, Q_BLOCK)
                qb, dob = q_ref[rows, :].astype(BF16), do_ref[rows, :].astype(BF16)
                kb, vb = kb_ref[0:keys, :], vb_ref[0:keys, :]
                p = fox_probs(qb, kb, _lane_column(fq_ref[rows, :], head), fk_ref[:, 0:keys], (g0 + j) * Q_BLOCK)
                dp = lax.dot_general(dob, vb, nt, preferred_element_type=F32)
                dz = p * (dp - jnp.sum(dp * p, axis=-1, keepdims=True))
                pb, dzb = p.astype(BF16), dz.astype(BF16)
                dq_ref[rows, :] = jnp.dot(dzb, kb, preferred_element_type=F32)
                lane = lax.broadcasted_iota(jnp.int32, (Q_BLOCK, LANES), 1)
                dfq_ref[rows, :] += jnp.where(lane == head, jnp.sum(dz, axis=-1, keepdims=True), 0.0)
                dk_ref[0:keys, :] += lax.dot_general(dzb, qb, tn, preferred_element_type=F32)
                dv_ref[0:keys, :] += lax.dot_general(pb, dob, tn, preferred_element_type=F32)
                dfk_ref[:, 0:keys] -= jnp.sum(dz, axis=0, keepdims=True)
                return carry
            for j in range(per):
                block(j, 0)
        dv_out_ref[...] = dv_ref[...].astype(dv_out_ref.dtype)

    x_spec = pl.BlockSpec((t, HEAD_DIM), lambda h: (0, h))
    v_spec = pl.BlockSpec((t, HEAD_DIM), lambda h: (0, _V_BLOCK + h))
    fq_spec = pl.BlockSpec((t, LANES), lambda h: (0, 0))
    fk_spec = pl.BlockSpec((None, 1, t), lambda h: (h, 0, 0))
    return hosted_call(
        riders, body, name=name, grid=(N_HEADS,), in_specs=[x_spec, x_spec, v_spec, fq_spec, fk_spec, x_spec],
        out_specs=[x_spec, x_spec, x_spec, fq_spec, fk_spec],
        out_shape=[jax.ShapeDtypeStruct((t, D_MODEL), F32)] * 2 + [jax.ShapeDtypeStruct((t, D_MODEL), BF16)]
        + [jax.ShapeDtypeStruct((t, LANES), F32), jax.ShapeDtypeStruct((N_HEADS, 1, t), F32)],
        scratch_shapes=[pltpu.VMEM((t, HEAD_DIM), BF16)] * 2 + [pltpu.VMEM((t, HEAD_DIM), F32)],
        compiler_params=_params(("arbitrary",)),
    )(q, k, proj, fq, fk, do)


def memkv_fwd(mem, mnw, wkv, mknw, *, name):
    n = mem.shape[0]

    def body(mem_ref, mnw_ref, w_ref, mknw_ref, mk_ref, mv_ref):
        mk, mv = memkv_fn(mem_ref[...], mnw_ref[...], w_ref[...], mknw_ref[...])
        mk_ref[...] = mk
        mv_ref[...] = mv

    return pl.pallas_call(
        body, name=name, out_shape=[jax.ShapeDtypeStruct((n, MEM_WIDTH), F32)] * 2,
        compiler_params=pltpu.CompilerParams(vmem_limit_bytes=VMEM_LIMIT),
    )(mem, mnw, wkv, mknw)


def memkv_bwd(mem, mnw, wkv, mknw, dmk, dmv, *, name):
    def body(mem_ref, mnw_ref, w_ref, mknw_ref, dmk_ref, dmv_ref, dmnw_ref, dw_ref, dmknw_ref):
        f = functools.partial(memkv_fn, mem_ref[...])
        _, vjp = jax.vjp(f, mnw_ref[...], w_ref[...].astype(F32), mknw_ref[...])
        dmnw, dw, dmknw = vjp((dmk_ref[...], dmv_ref[...]))
        dmnw_ref[...] = dmnw
        dw_ref[...] = dw.astype(dw_ref.dtype)
        dmknw_ref[...] = dmknw

    return pl.pallas_call(
        body, name=name,
        out_shape=[jax.ShapeDtypeStruct(mnw.shape, F32), jax.ShapeDtypeStruct(wkv.shape, BF16), jax.ShapeDtypeStruct(mknw.shape, F32)],
        compiler_params=pltpu.CompilerParams(vmem_limit_bytes=VMEM_LIMIT),
    )(mem, mnw, wkv, mknw, dmk, dmv)


def _row(v, width=None):
    v = v.reshape(1, -1)
    if width is not None and v.shape[1] < width:
        v = jnp.pad(v, ((0, 0), (0, width - v.shape[1])))
    return v


def _norm_fwd(x, w, name, riders=()):
    return rows_call(lambda x, w: rms(x, w), [x], [w], [(D_MODEL, BF16)], [], tm=512, name=name, riders=riders)[0]


FF_PIECE = D_FF // N_DEV


def _add(r, x, *rows):
    return r + x


def _norm_rows(r, w):
    return rms(r, w)


def _norm_bwd_post(dh, x, dx_in, w):
    _, vjp = jax.vjp(rms, x, w)
    dx, dw = vjp(dh)
    return dx + dx_in, dw


def _piece(rows, cols, index):
    return pl.BlockSpec((None, rows, cols), lambda i, j, kk: (index(i, j, kk), 0, 0))


def _two_pieces(rows, cols, index):
    return pl.BlockSpec((2, rows, cols), lambda i, j, kk: (index(i, j, kk), 0, 0))


def _loss_post(r, x, tgt):
    e = r + x - tgt
    return e * (1.0 / D_MODEL), jnp.sum(e * e, axis=0, keepdims=True)


def _mlp_fwd(x, h2, w1, w2, layer, riders=(), next_norm_w=None, loss_target=None):
    riders = list(riders) + [None, None]
    u, a1 = matmul(h2, w1, name=f"mlp1_fwd_{layer}", tiles=(None, FF_PIECE, D_MODEL),
                   extra_out=(lambda u: jnp.square(jnp.maximum(u, 0.0)), BF16),
                   b_view=(D_MODEL, D_FF, _piece(D_MODEL, FF_PIECE, lambda i, j, kk: j)), riders=riders[0])
    if loss_target is not None:
        tail = dict(post=_loss_post, post_ins=[x, loss_target], acc=True)
    elif next_norm_w is not None:
        tail = dict(post=_add, post_ins=[x], row_ins=[next_norm_w], extra_out=(_norm_rows, BF16))
    else:
        tail = dict(post=_add, post_ins=[x])
    y = matmul(a1, w2, name=f"mlp2_fwd_{layer}", tiles=(None, D_MODEL, 2 * FF_PIECE),
               b_view=(D_FF, D_MODEL, _two_pieces(FF_PIECE, D_MODEL, lambda i, j, kk: kk)), riders=riders[1], **tail)
    return y, (x, h2, u, a1)


def pair_sum(g, got, *, name):
    _, rows, cols = g.shape
    tile = _pick(rows, (512, 256, 128))
    c = lax.axis_index("c").astype(jnp.int32).reshape(1)

    def body(c_ref, a_ref, b_ref, o_ref):
        o_ref[...] = (a_ref[...].astype(F32) + b_ref[...].astype(F32)).astype(o_ref.dtype)

    grid_spec = pltpu.PrefetchScalarGridSpec(
        num_scalar_prefetch=1, grid=(4, rows // tile),
        in_specs=[pl.BlockSpec((None, tile, cols), lambda k, i, c_ref: (2 * k + c_ref[0], i, 0)),
                  pl.BlockSpec((None, tile, cols), lambda k, i, c_ref: (k, i, 0))],
        out_specs=pl.BlockSpec((None, tile, cols), lambda k, i, c_ref: (k, i, 0)))
    return pl.pallas_call(
        body, name=name, grid_spec=grid_spec, out_shape=jax.ShapeDtypeStruct((4, rows, cols), g.dtype),
        compiler_params=_params(("parallel", "parallel")),
    )(c, g, got)


def chip_sums(names, pieces, gots):
    return [pair_sum(a, got, name=f"grads_pair_sum_{n}") for n, a, got in zip(names, pieces, gots)]


def _mlp_bwd(dy, res, n2w, w1, w2, layer, riders=()):
    x, h2, u, a1 = res
    du = matmul(dy, w2, tb=True, name=f"mlp2_dx_{layer}", out_dtype=BF16, tiles=(None, 2 * FF_PIECE, D_MODEL),
                post=lambda r, u: r * (2.0 * jnp.maximum(u, 0.0)), post_ins=[u],
                b_view=(D_MODEL, D_FF, _two_pieces(FF_PIECE, D_MODEL, lambda i, j, kk: j)), riders=riders)
    dw2 = matmul(a1, dy, ta=True, name=f"mlp2_dw_{layer}", out_dtype=BF16, tiles=(FF_PIECE, D_MODEL, None), out_view=(
        w2.shape, _piece(FF_PIECE, D_MODEL, lambda i, j, kk: i)))
    sib2 = sibling_rider([dw2])
    dx, dn2w = matmul(du, w1, tb=True, name=f"mlp1_dx_{layer}", tiles=(None, D_MODEL, FF_PIECE),
                      b_view=(D_FF, D_MODEL, _piece(D_MODEL, FF_PIECE, lambda i, j, kk: kk)),
                      post=_norm_bwd_post, post_ins=[x, dy], row_ins=[n2w], acc=True, riders=[sib2])
    dw1 = matmul(h2, du, ta=True, name=f"mlp1_dw_{layer}", out_dtype=BF16, tiles=(D_MODEL, FF_PIECE, None), out_view=(
        w1.shape, _piece(D_MODEL, FF_PIECE, lambda i, j, kk: j)))
    return dx, dw1, dw2, dn2w, sibling_rider([dw1]), sib2


def _in_proj_bwd(h, dmain, dsmall, w_main, w_small, x, dx_in, n1w, tag, riders=()):
    dh = matmul(dmain, w_main, tb=True, name=f"inproj_dx_main_{tag}", riders=riders)

    def post(r, dh_main, x, dx_in, w):
        return _norm_bwd_post(r + dh_main, x, dx_in, w)

    dx, dn1w = matmul(dsmall, w_small, tb=True, name=f"inproj_dx_small_{tag}", tiles=(None, D_MODEL, None),
                      post=post, post_ins=[dh, x, dx_in], row_ins=[n1w], acc=True)
    dw_main = matmul(h, dmain, ta=True, out_dtype=BF16, name=f"inproj_dw_main_{tag}")
    dw_small = matmul(h, dsmall, ta=True, out_dtype=BF16, name=f"inproj_dw_small_{tag}")
    return dx, dn1w, dw_main, dw_small


def local_step(x, mem, target, w, m, v):
    t = x.shape[0]
    n_mem = mem.shape[0]
    g = {}

    def wire(a):
        return a.astype(BF16)

    ride_first = gather_rider([wire(w["dn_w_in"][0])])
    fox_w = wire(w["fox_w_in"][0])
    ride_out = gather_rider([wire(w["w_out"][0]), w["dn_conv_w"][0]])
    ride_out_1 = gather_rider([wire(w["w_out"][1])])
    ride_kv = gather_rider([wire(w["w_mem_kv"])])
    ride_mlp1_0 = gather_rider([wire(w["w_mlp1"][0])])
    ride_mlp2_0 = gather_rider([wire(w["w_mlp2"][0])])
    ride_fox_a, ride_fox_b = gather_rider([fox_w[:D_MODEL // 2]]), gather_rider([fox_w[D_MODEL // 2:]])
    ride_mlp_1 = gather_rider([wire(w["w_mlp1"][1]), wire(w["w_mlp2"][1])])
    mnw, mknw = _row(w["mem_norm_w"]), _row(w["mem_k_norm_w"])

    n1w0, n2w0 = _row(w["norm1_w"][0]), _row(w["norm2_w"][0])
    n1w1, n2w1 = _row(w["norm1_w"][1]), _row(w["norm2_w"][1])
    alog, dtb = _row(w["dn_a_log"][0], LANES), _row(w["dn_dt_bias"][0], LANES)
    onw, mqw0 = _row(w["dn_o_norm_w"][0]), _row(w["memq_norm_w"][0])
    x0 = x
    h0 = _norm_fwd(x0, n1w0, "norm1_fwd_0", riders=[ride_first])
    dn_main, dn_ab = in_proj_weights(ride_first.results[0], DN_IN, 2 * N_HEADS)
    pm0 = matmul(h0, dn_main, name="inproj_main_0", riders=[ride_out])
    w_out0 = ride_out.results[0].reshape(OUT_IN, D_MODEL)
    conv_w = ride_out.results[1].transpose(1, 0, 2).reshape(CONV_WIDTH, 3 * D_MODEL)
    ps0 = matmul(h0, dn_ab, name="inproj_small_0")
    gates = rows_call(dn_gates_fn, [ps0], [alog, dtb], [(LANES, F32)], [], tm=512, name="dn_gates_fwd")[0]
    q0, k0, v0 = dn_prep_fwd(pm0, conv_w, name="dn_prep_fwd", riders=[ride_kv])
    w_kv = ride_kv.results[0].reshape(D_MODEL, D_MODEL)
    mk, mv = memkv_fwd(mem, mnw, w_kv, mknw, name="memkv_fwd")
    u0, w0, qk0 = delta_intra_fwd(q0, k0, v0, gates, name="delta_intra_fwd", riders=[ride_mlp1_0])
    o0, s_start = delta_seq_fwd(q0, k0, gates, u0, w0, qk0, name="delta_seq_fwd", riders=[ride_mlp2_0])
    cat0 = rows_call(dn_out_fn, [o0, (pm0, D_MODEL, 3), (pm0, MEM_WIDTH, 8)], [onw, mqw0, mk, mv],
                     [(D_MODEL + MEM_WIDTH, BF16)], [], tm=256, name="dn_out_fwd")[0]
    (w1_0,), (w2_0,) = ride_mlp1_0.results, ride_mlp2_0.results
    x1, h2_0 = matmul(cat0, w_out0, post=_add, post_ins=[x0], row_ins=[n2w0], extra_out=(_norm_rows, BF16),
                      tiles=(None, D_MODEL, None), name="wout_fwd_0")
    (x2, h1), mlp_res0 = _mlp_fwd(x1, h2_0, w1_0, w2_0, 0, riders=[[ride_fox_a], [ride_fox_b]], next_norm_w=n1w1)
    fox_main, fox_f = in_proj_weights(
        jnp.concatenate([ride_fox_a.results[0], ride_fox_b.results[0]], axis=1), FOX_IN, N_HEADS)

    fbias = _row(w["fox_f_bias"][0], LANES)
    qnw, knw, mqw1 = _row(w["fox_q_norm_w"][0]), _row(w["fox_k_norm_w"][0]), _row(w["memq_norm_w"][1])
    pm1 = matmul(h1, fox_main, name="inproj_main_1", riders=[ride_out_1])
    w_out1 = ride_out_1.results[0].reshape(OUT_IN, D_MODEL)
    ps1 = matmul(h1, fox_f, name="inproj_small_1")
    fq = rows_call(fox_fcum_fn, [ps1], [fbias], [(LANES, F32)], [], tm=t, name="fox_fcum_fwd")[0]
    fk = fq[:, :N_HEADS].T[:, None, :]
    q1, k1 = rows_call(fox_qk_fn, [(pm1, D_MODEL, 0), (pm1, D_MODEL, 1)], [qnw, knw], [(D_MODEL, F32)] * 2, [], tm=256,
                       name="fox_qk_fwd")
    o1 = fox_attn_fwd(q1, k1, pm1, fq, fk, name="fox_attn_fwd", riders=[ride_mlp_1])
    cat1 = rows_call(fox_out_fn, [o1, (pm1, D_MODEL, 3), (pm1, MEM_WIDTH, 8)], [mqw1, mk, mv],
                     [(D_MODEL + MEM_WIDTH, BF16)], [], tm=256, name="fox_out_fwd")[0]
    w1_1, w2_1 = ride_mlp_1.results
    x3, h2_1 = matmul(cat1, w_out1, post=_add, post_ins=[x2], row_ins=[n2w1], extra_out=(_norm_rows, BF16),
                      tiles=(None, D_MODEL, None), name="wout_fwd_1")
    (dy, sq), mlp_res1 = _mlp_fwd(x3, h2_1, w1_1, w2_1, 1, loss_target=target)
    loss = jnp.sum(sq) * (0.5 / D_MODEL)

    dx3, dw1_1, dw2_1, dn2w1, sib1, sib2 = _mlp_bwd(dy, mlp_res1, n2w1, w1_1, w2_1, 1)
    dcat1 = matmul(dx3, w_out1, tb=True, name="wout_dx_1", riders=[sib1])
    dwo_1 = matmul(cat1, dx3, ta=True, out_dtype=BF16, name="wout_dw_1").reshape(N_DEV, OUT_IN // N_DEV, D_MODEL)
    sibo = sibling_rider([dwo_1])
    do1, dgate1, dqm1, dmqw1, dmk1, dmv1 = rows_call(
        functools.partial(vjp_rows(fox_out_fn, 3, (True, True, True)), n_row=3, n_ct=1),
        [o1, (pm1, D_MODEL, 3), (pm1, MEM_WIDTH, 8), dcat1], [mqw1, mk, mv],
        [(D_MODEL, F32), (D_MODEL, BF16), (MEM_WIDTH, BF16)], [(1, HEAD_DIM), (n_mem, MEM_WIDTH), (n_mem, MEM_WIDTH)],
        tm=256, name="fox_out_bwd", riders=[sibo])
    ride_l1 = chips_rider(chip_sums(["w_mlp2_1", "w_mlp1_1", "w_out_1"], [dw2_1, dw1_1, dwo_1],
                                    sib2.results + sib1.results + sibo.results))
    dq1, dk1, dv1, dfq, dfk = fox_attn_bwd(q1, k1, pm1, fq, fk, do1, name="fox_attn_bwd", riders=[ride_l1])
    dqraw1, dkraw1, dqnw, dknw = rows_call(
        functools.partial(vjp_rows(fox_qk_fn, 2, (True, True)), n_row=2, n_ct=2),
        [(pm1, D_MODEL, 0), (pm1, D_MODEL, 1), dq1, dk1], [qnw, knw],
        [(D_MODEL, BF16)] * 2, [(1, HEAD_DIM)] * 2, tm=256, name="fox_qk_bwd")
    dfcum = dfq + jnp.pad(dfk[:, 0, :].T, ((0, 0), (0, LANES - N_HEADS)))
    dps1, dfbias = rows_call(
        functools.partial(vjp_rows(fox_fcum_fn, 1, (True,)), n_row=1, n_ct=1),
        [ps1, dfcum], [fbias], [(LANES, F32)], [(1, LANES)], tm=t, name="fox_fcum_bwd")
    dpm1 = jnp.concatenate([dqraw1, dkraw1, dv1, dgate1, dqm1], axis=1)
    dx2, dn1w1, dwmain1, dwsmall1 = _in_proj_bwd(h1, dpm1, dps1, fox_main, fox_f, x2, dx3, n1w1, "1")
    g_fox = in_proj_pieces(dwmain1, dwsmall1, N_HEADS, FOX_IN)
    sibf = sibling_rider([g_fox])

    dx1, dw1_0, dw2_0, dn2w0, sib1, sib2 = _mlp_bwd(dx2, mlp_res0, n2w0, w1_0, w2_0, 0, riders=[sibf])
    ride_fox_g = chips_rider(chip_sums(["fox_w_in"], [g_fox], sibf.results))
    dcat0 = matmul(dx1, w_out0, tb=True, name="wout_dx_0", riders=[sib1])
    dwo_0 = matmul(cat0, dx1, ta=True, out_dtype=BF16, name="wout_dw_0").reshape(N_DEV, OUT_IN // N_DEV, D_MODEL)
    sibo = sibling_rider([dwo_0])
    do0, dz0, dqm0, donw, dmqw0, dmk0, dmv0 = rows_call(
        functools.partial(vjp_rows(dn_out_fn, 3, (True, True, True, True)), n_row=3, n_ct=1),
        [o0, (pm0, D_MODEL, 3), (pm0, MEM_WIDTH, 8), dcat0], [onw, mqw0, mk, mv],
        [((N_HEADS, HEAD_DIM), F32), (D_MODEL, BF16), (MEM_WIDTH, BF16)],
        [(1, HEAD_DIM), (1, HEAD_DIM), (n_mem, MEM_WIDTH), (n_mem, MEM_WIDTH)], tm=256, name="dn_out_bwd", riders=[sibo])
    h_l0 = chip_sums(["w_mlp2_0", "w_mlp1_0", "w_out_0"], [dw2_0, dw1_0, dwo_0], sib2.results + sib1.results + sibo.results)
    ride_l0_mlp2, ride_l0_mlp1, ride_l0_out = (chips_rider([h]) for h in h_l0)
    dmnw, dwkv, dmknw = memkv_bwd(mem, mnw, w_kv, mknw, dmk0 + dmk1, dmv0 + dmv1, name="memkv_bwd")
    g_kv = dwkv.reshape(N_DEV, D_MODEL // N_DEV, D_MODEL)
    sibk = sibling_rider([g_kv])
    dq_s, dk_s, dg_s, du0, dw0, dqk0 = delta_seq_bwd(q0, k0, gates, u0, w0, qk0, s_start, do0, name="delta_seq_bwd",
                                                     riders=[ride_fox_g, sibk])
    ride_kv_g = chips_rider(chip_sums(["w_mem_kv"], [g_kv], sibk.results))
    dq0, dk0, dv0, dgates = delta_intra_bwd(q0, k0, v0, gates, du0, dw0, dqk0, dq_s, dk_s, dg_s,
                                            name="delta_intra_bwd", riders=[ride_l0_mlp2, ride_kv_g])
    dxq, dxk, dxv, dcq, dck, dcv = dn_prep_bwd(pm0, conv_w, dq0, dk0, dv0, name="dn_prep_bwd", riders=[ride_l0_mlp1])
    dconv = jnp.concatenate([dcq, dck, dcv], axis=1)
    dps0, dalog, ddtb = rows_call(
        functools.partial(vjp_rows(dn_gates_fn, 1, (True, True)), n_row=1, n_ct=1),
        [ps0, dgates], [alog, dtb], [(LANES, F32)], [(1, LANES)] * 2, tm=512, name="dn_gates_bwd")
    dpm0 = jnp.concatenate([dxq, dxk, dxv, dz0, dqm0], axis=1)
    grad_x, dn1w0, dwmain0, dwsmall0 = _in_proj_bwd(h0, dpm0, dps0, dn_main, dn_ab, x0, dx1, n1w0, "0", riders=[ride_l0_out])
    g_dn = in_proj_pieces(dwmain0, dwsmall0, 2 * N_HEADS, DN_IN)
    g_conv = dconv.reshape(CONV_WIDTH, N_DEV, -1).transpose(1, 0, 2).astype(BF16)

    g["mem_norm_w"] = dmnw[0]
    g["mem_k_norm_w"] = dmknw[0]
    g["norm1_w"] = jnp.concatenate([dn1w0, dn1w1], axis=0)
    g["dn_a_log"] = dalog[:, :N_HEADS]
    g["dn_dt_bias"] = ddtb[:, :N_HEADS]
    g["dn_o_norm_w"] = donw
    g["fox_f_bias"] = dfbias[:, :N_HEADS]
    g["fox_q_norm_w"] = dqnw
    g["fox_k_norm_w"] = dknw
    g["memq_norm_w"] = jnp.concatenate([dmqw0, dmqw1], axis=0)
    g["norm2_w"] = jnp.concatenate([dn2w0, dn2w1], axis=0)

    sibd = sibling_rider([g_dn, g_conv])
    run_riders([sibd], name="grads_to_sibling_last")
    ride_last = chips_rider(chip_sums(["dn_w_in", "dn_conv_w"], [g_dn, g_conv], sibd.results))
    ride_small = gather_rider([pack_small(g, last=loss)])
    run_riders([ride_last, ride_small], name="grads_to_chips_last")

    def layers(l0, l1):
        return jnp.stack([l0, l1], axis=1).reshape(4, -1, l0.shape[-1])

    parts = {
        "w_mlp1": layers(ride_l0_mlp1.results[0], ride_l1.results[1]),
        "w_mlp2": layers(ride_l0_mlp2.results[0], ride_l1.results[0]),
        "w_out": layers(ride_l0_out.results[0], ride_l1.results[2]),
        "fox_w_in": ride_fox_g.results[0], "w_mem_kv": ride_kv_g.results[0],
        "dn_w_in": ride_last.results[0], "dn_conv_w": ride_last.results[1],
    }
    out = {n: adamw(parts[n], w[n], m[n], v[n], name=f"adamw_{n}") for n, _, _ in BIG}
    small, loss = adamw_small(ride_small.results[0], w, m, v, name="adamw_small")
    return loss, grad_x, out, small


WEIGHTS = ["mem_norm_w", "w_mem_kv", "mem_k_norm_w", "norm1_w", "dn_w_in", "dn_conv_w", "dn_a_log", "dn_dt_bias",
           "dn_o_norm_w", "fox_w_in", "fox_f_bias", "fox_q_norm_w", "fox_k_norm_w", "memq_norm_w", "w_out", "norm2_w",
           "w_mlp1", "w_mlp2"]
DN_IN = 4 * D_MODEL + 2 * N_HEADS + MEM_WIDTH
FOX_IN = 4 * D_MODEL + N_HEADS + MEM_WIDTH
GATE_END = 4 * D_MODEL
OUT_IN = D_MODEL + MEM_WIDTH
BIG = [("w_mem_kv", D_MODEL // N_DEV, D_MODEL), ("dn_w_in", D_MODEL, DN_IN // N_DEV), ("fox_w_in", D_MODEL, FOX_IN // N_DEV),
       ("dn_conv_w", CONV_WIDTH, 3 * D_MODEL // N_DEV), ("w_out", 2 * OUT_IN // N_DEV, D_MODEL),
       ("w_mlp1", 2 * D_MODEL, FF_PIECE), ("w_mlp2", 2 * FF_PIECE, D_MODEL)]
SMALL_TILE = 8 * LANES
SMALL = [(name, shape, -(-math.prod(shape) // SMALL_TILE) * SMALL_TILE) for name, shape in [
    ("mem_norm_w", (D_MODEL,)), ("mem_k_norm_w", (HEAD_DIM,)), ("norm1_w", (2, D_MODEL)), ("dn_a_log", (1, N_HEADS)),
    ("dn_dt_bias", (1, N_HEADS)), ("dn_o_norm_w", (1, HEAD_DIM)), ("fox_f_bias", (1, N_HEADS)),
    ("fox_q_norm_w", (1, HEAD_DIM)), ("fox_k_norm_w", (1, HEAD_DIM)), ("memq_norm_w", (2, HEAD_DIM)), ("norm2_w", (2, D_MODEL))]]
SMALL_ROWS = sum(ln for _, _, ln in SMALL) // LANES + 8


def pack_small(p, last=None):
    def rows(a, ln):
        a = a.reshape(-1)
        return (a if a.shape[0] == ln else jnp.pad(a, (0, ln - a.shape[0]))).reshape(-1, LANES)

    used = sum(ln for _, _, ln in SMALL) // LANES
    tail = jnp.zeros(((SMALL_ROWS - used) * LANES,), F32)
    if last is not None:
        tail = jnp.concatenate([tail[:-1], last.reshape(1)])
    return jnp.concatenate([rows(p[n], ln) for n, _, ln in SMALL] + [tail.reshape(-1, LANES)], axis=0)


def in_proj_weights(gathered, width, n_small):
    full = gathered.transpose(1, 0, 2).reshape(D_MODEL, width)
    main = jnp.concatenate([full[:, :GATE_END], full[:, GATE_END + n_small:]], axis=1)
    return main, jnp.pad(full[:, GATE_END:GATE_END + n_small], ((0, 0), (0, LANES - n_small)))


def in_proj_pieces(d_main, d_small, n_small, width):
    full = jnp.concatenate([d_main[:, :GATE_END], d_small[:, :n_small], d_main[:, GATE_END:]], axis=1)
    return full.reshape(D_MODEL, N_DEV, width // N_DEV).transpose(1, 0, 2)


def _adamw_update(g, w, m, v):
    m_new = ADAM_B1 * m + (1.0 - ADAM_B1) * g
    v_new = ADAM_B2 * v + (1.0 - ADAM_B2) * jnp.square(g)
    m_hat = m_new / (1.0 - ADAM_B1 ** ADAM_STEP)
    v_hat = v_new / (1.0 - ADAM_B2 ** ADAM_STEP)
    return -ADAM_LR * (m_hat / (jnp.sqrt(v_hat) + ADAM_EPS) + ADAM_WD * w), m_new, v_new


def adamw(parts, w, m, v, *, name):
    n, _, cols = parts.shape
    layers = w.shape[0] if w.ndim == 3 else 1
    rows = w.shape[-2]
    tile = _pick(rows, (512, 256, 128))
    steps = rows // tile

    def body(p_ref, w_ref, m_ref, v_ref, g_ref, d_ref, mo_ref, vo_ref):
        g = p_ref[0].astype(F32)
        for i in range(1, n):
            g = g + p_ref[i].astype(F32)
        g_ref[...] = g
        d_ref[...], mo_ref[...], vo_ref[...] = _adamw_update(g, w_ref[...], m_ref[...], v_ref[...])

    if w.ndim == 3:
        spec = pl.BlockSpec((None, tile, cols), lambda l, i: (l, i, 0))
    else:
        spec = pl.BlockSpec((tile, cols), lambda l, i: (i, 0))
    return pl.pallas_call(
        body, name=name, grid=(layers, steps),
        in_specs=[pl.BlockSpec((n, tile, cols), lambda l, i: (0, l * steps + i, 0)), spec, spec, spec], out_specs=[spec] * 4,
        out_shape=[jax.ShapeDtypeStruct(w.shape, F32)] * 4, compiler_params=_params(("parallel", "parallel")),
    )(parts, w, m, v)


def adamw_small(parts, w, m, v, *, name):
    def view(a):
        return a.reshape(-1, LANES) if a.size % LANES == 0 else a.reshape(1, a.size)

    k = len(SMALL)
    ins = [view(d[n]) for d in (w, m, v) for n, _, _ in SMALL]

    def body(p_ref, *refs):
        w_refs, m_refs, v_refs, outs, g_ref = refs[:k], refs[k:2 * k], refs[2 * k:3 * k], refs[3 * k:-1], refs[-1]
        g_all = p_ref[0]
        for i in range(1, N_DEV):
            g_all = g_all + p_ref[i]
        g_ref[...] = g_all
        row = 0
        for i, (_, _, ln) in enumerate(SMALL):
            r, c = w_refs[i].shape
            g = g_ref[row:row + r, 0:c]
            outs[4 * i][...] = g
            outs[4 * i + 1][...], outs[4 * i + 2][...], outs[4 * i + 3][...] = _adamw_update(
                g, w_refs[i][...], m_refs[i][...], v_refs[i][...])
            row += ln // LANES
        outs[-1][...] = g_ref[SMALL_ROWS - 1:SMALL_ROWS, LANES - 1:LANES]

    out_shape = [jax.ShapeDtypeStruct(a.shape, F32) for a in ins[:k] for _ in range(4)] + [jax.ShapeDtypeStruct((1, 1), F32)]
    res = pl.pallas_call(body, name=name, out_shape=out_shape,
                         scratch_shapes=[pltpu.VMEM((SMALL_ROWS, LANES), F32)])(parts, *ins)
    small = {n: [o.reshape(sh) for o in res[4 * i:4 * i + 4]] for i, (n, sh, _) in enumerate(SMALL)}
    return small, res[-1][0, 0]


def kernel(x, mem, mem_norm_w, w_mem_kv, mem_k_norm_w, norm1_w, dn_w_in, dn_conv_w, dn_a_log, dn_dt_bias, dn_o_norm_w, fox_w_in, fox_f_bias, fox_q_norm_w, fox_k_norm_w, memq_norm_w, w_out, norm2_w, w_mlp1, w_mlp2, loss_target, m_mem_norm_w, m_w_mem_kv, m_mem_k_norm_w, m_norm1_w, m_dn_w_in, m_dn_conv_w, m_dn_a_log, m_dn_dt_bias, m_dn_o_norm_w, m_fox_w_in, m_fox_f_bias, m_fox_q_norm_w, m_fox_k_norm_w, m_memq_norm_w, m_w_out, m_norm2_w, m_w_mlp1, m_w_mlp2, v_mem_norm_w, v_w_mem_kv, v_mem_k_norm_w, v_norm1_w, v_dn_w_in, v_dn_conv_w, v_dn_a_log, v_dn_dt_bias, v_dn_o_norm_w, v_fox_w_in, v_fox_f_bias, v_fox_q_norm_w, v_fox_k_norm_w, v_memq_norm_w, v_w_out, v_norm2_w, v_w_mlp1, v_w_mlp2):
    p = dict(mem_norm_w=mem_norm_w, w_mem_kv=w_mem_kv, mem_k_norm_w=mem_k_norm_w, norm1_w=norm1_w, dn_w_in=dn_w_in,
             dn_conv_w=dn_conv_w, dn_a_log=dn_a_log, dn_dt_bias=dn_dt_bias, dn_o_norm_w=dn_o_norm_w, fox_w_in=fox_w_in,
             fox_f_bias=fox_f_bias, fox_q_norm_w=fox_q_norm_w, fox_k_norm_w=fox_k_norm_w, memq_norm_w=memq_norm_w,
             w_out=w_out, norm2_w=norm2_w, w_mlp1=w_mlp1, w_mlp2=w_mlp2)
    pm = dict(mem_norm_w=m_mem_norm_w, w_mem_kv=m_w_mem_kv, mem_k_norm_w=m_mem_k_norm_w, norm1_w=m_norm1_w,
              dn_w_in=m_dn_w_in, dn_conv_w=m_dn_conv_w, dn_a_log=m_dn_a_log, dn_dt_bias=m_dn_dt_bias,
              dn_o_norm_w=m_dn_o_norm_w, fox_w_in=m_fox_w_in, fox_f_bias=m_fox_f_bias, fox_q_norm_w=m_fox_q_norm_w,
              fox_k_norm_w=m_fox_k_norm_w, memq_norm_w=m_memq_norm_w, w_out=m_w_out, norm2_w=m_norm2_w, w_mlp1=m_w_mlp1,
              w_mlp2=m_w_mlp2)
    pv = dict(mem_norm_w=v_mem_norm_w, w_mem_kv=v_w_mem_kv, mem_k_norm_w=v_mem_k_norm_w, norm1_w=v_norm1_w,
              dn_w_in=v_dn_w_in, dn_conv_w=v_dn_conv_w, dn_a_log=v_dn_a_log, dn_dt_bias=v_dn_dt_bias,
              dn_o_norm_w=v_dn_o_norm_w, fox_w_in=v_fox_w_in, fox_f_bias=v_fox_f_bias, fox_q_norm_w=v_fox_q_norm_w,
              fox_k_norm_w=v_fox_k_norm_w, memq_norm_w=v_memq_norm_w, w_out=v_w_out, norm2_w=v_norm2_w, w_mlp1=v_w_mlp1,
              w_mlp2=v_w_mlp2)

    loss, grad_x, results, small = local_step(x[0], mem[0], loss_target[0], p, pm, pv)
    groups = [{n: r[i] for n, r in {**small, **results}.items()} for i in range(4)]
    return (loss, grad_x[None], *[grp[n] for grp in groups for n in WEIGHTS])
```

```python
import functools
import math

import jax
import jax.numpy as jnp
from jax import lax
from jax.experimental import pallas as pl
from jax.experimental.pallas import tpu as pltpu

F32 = jnp.float32
BF16 = jnp.bfloat16
HIGHEST = lax.Precision.HIGHEST

D_MODEL = 1024
HEAD_DIM = 128
N_HEADS = 8
MEM_HEADS = 4
MEM_WIDTH = MEM_HEADS * HEAD_DIM
D_FF = 4 * D_MODEL
CONV_WIDTH = 4
CHUNK = 64
Q_BLOCK = 128
EPS = 1e-6
SCALE = HEAD_DIM ** -0.5
MAIN_WIDTH = 4 * D_MODEL + MEM_WIDTH
LANES = 128
N_DEV = 8

ADAM_LR = 0.001
ADAM_B1 = 0.9
ADAM_B2 = 0.999
ADAM_EPS = 1e-08
ADAM_WD = 0.01
ADAM_STEP = 10

VMEM_LIMIT = 56 * 2 ** 20
MESH = pl.DeviceIdType.MESH


def _bdot(a, b, dims):
    return lax.dot_general(a.astype(BF16), b.astype(BF16), (dims, ((), ())), preferred_element_type=F32)


@jax.custom_vjp
def mm(a, b):
    return _bdot(a, b, ((1,), (0,)))


@jax.custom_vjp
def mm_nt(a, b):
    return _bdot(a, b, ((1,), (1,)))


@jax.custom_vjp
def mm_tn(a, b):
    return _bdot(a, b, ((0,), (0,)))


mm.defvjp(lambda a, b: (mm(a, b), (a, b)), lambda r, g: (mm_nt(g, r[1]), mm_tn(r[0], g)))
mm_nt.defvjp(lambda a, b: (mm_nt(a, b), (a, b)), lambda r, g: (mm(g, r[1]), mm_tn(g, r[0])))
mm_tn.defvjp(lambda a, b: (mm_tn(a, b), (a, b)), lambda r, g: (mm_nt(r[1], g), mm(r[0], g)))


def hdot(a, b):
    return jnp.dot(a, b, precision=HIGHEST, preferred_element_type=F32)


def rms(x, w):
    return x * lax.rsqrt(jnp.mean(x * x, axis=-1, keepdims=True) + EPS) * w


def l2n(x):
    return x * lax.rsqrt(jnp.sum(x * x, axis=-1, keepdims=True) + EPS)


def _iota2(n, m):
    return lax.broadcasted_iota(jnp.int32, (n, m), 0), lax.broadcasted_iota(jnp.int32, (n, m), 1)


def _lower_ones(n):
    r, c = _iota2(n, n)
    return jnp.where(r >= c, 1.0, 0.0).astype(F32)


def _last_row(x):
    r = lax.broadcasted_iota(jnp.int32, x.shape, 0)
    return jnp.sum(jnp.where(r == x.shape[0] - 1, x, 0.0), axis=0, keepdims=True)


def _softmax_rows(z):
    m = lax.stop_gradient(jnp.max(z, axis=-1, keepdims=True))
    e = jnp.exp(z - m)
    return e * (1.0 / jnp.sum(e, axis=-1, keepdims=True))


_BNN = (((2,), (1,)), ((0,), (0,)))
_BNT = (((2,), (2,)), ((0,), (0,)))
_BTN = (((1,), (1,)), ((0,), (0,)))


def _bbdot(a, b, dims):
    return lax.dot_general(a.astype(BF16), b.astype(BF16), dims, preferred_element_type=F32)


@jax.custom_vjp
def bmm(a, b):
    return _bbdot(a, b, _BNN)


@jax.custom_vjp
def bmm_nt(a, b):
    return _bbdot(a, b, _BNT)


@jax.custom_vjp
def bmm_tn(a, b):
    return _bbdot(a, b, _BTN)


@jax.custom_vjp
def bmm_high(a, b):
    return lax.dot_general(a, b, _BNN, precision=lax.Precision.HIGH, preferred_element_type=F32)


bmm.defvjp(lambda a, b: (bmm(a, b), (a, b)), lambda r, g: (bmm_nt(g, r[1]), bmm_tn(r[0], g)))
bmm_nt.defvjp(lambda a, b: (bmm_nt(a, b), (a, b)), lambda r, g: (bmm(g, r[1]), bmm_tn(g, r[0])))
bmm_tn.defvjp(lambda a, b: (bmm_tn(a, b), (a, b)), lambda r, g: (bmm_nt(r[1], g), bmm(r[0], g)))
bmm_high.defvjp(lambda a, b: (bmm_high(a, b), (a, b)), lambda r, g: (bmm_nt(g, r[1]), bmm_tn(r[0], g)))

NEUMANN_HIGH_LEVELS = 2


@jax.custom_vjp
def inv_unit_lower(a):
    n = a.shape[-1]
    r, c = _iota2(n, n)
    p = jnp.where(r == c, 1.0, 0.0).astype(F32) - a
    ak = a
    for level in range(int(math.log2(n)) - 1):
        dot = bmm_high if level < NEUMANN_HIGH_LEVELS else bmm
        ak = dot(ak, ak)
        p = p + dot(p, ak)
    return p


def _inv_unit_lower_fwd(a):
    t = inv_unit_lower(a)
    return t, t


def _inv_unit_lower_bwd(t, g):
    return (-bmm_tn(t, bmm_nt(g, t)),)


inv_unit_lower.defvjp(_inv_unit_lower_fwd, _inv_unit_lower_bwd)


def delta_intra(q, k, v, gc, beta):
    b, c, _ = q.shape
    r, cc = _iota2(c, c)
    causal = r >= cc
    strict = r > cc
    gi = jnp.broadcast_to(gc, (b, c, c))
    gj = jnp.swapaxes(gi, 1, 2)
    decay = jnp.where(causal, jnp.exp(jnp.where(causal, gi - gj, 0.0)), 0.0)
    kb = k * beta
    a = jnp.where(strict, bmm_nt(kb, k) * decay, 0.0)
    t = inv_unit_lower(a)
    u = bmm(t, v * beta)
    w = bmm(t, kb * jnp.exp(gc))
    qk = jnp.where(causal, bmm_nt(q, k) * decay, 0.0)
    return u, w, qk


def delta_step(s, q, k, gc, u, w, qk):
    v_new = u - bmm(w, s)
    out = bmm(q * jnp.exp(gc), s) + bmm(qk, v_new)
    r = lax.broadcasted_iota(jnp.int32, gc.shape, 1)
    g_last = jnp.sum(jnp.where(r == gc.shape[1] - 1, gc, 0.0), axis=1, keepdims=True)
    k_dec = k * jnp.exp(g_last - gc)
    s_new = s * jnp.exp(g_last) + bmm_tn(k_dec, v_new)
    return out, s_new


def fox_probs(q, k, fq, fk, qpos0):
    s = lax.dot_general(q, k, (((1,), (1,)), ((), ())), preferred_element_type=F32)
    z = s + (fq - fk)
    nq, nk = s.shape
    if isinstance(qpos0, int) and nk == qpos0 + nq:
        r, c = _iota2(nq, nq)
        diag = jnp.where(c <= r, z[:, nk - nq:], -jnp.inf)
        z = diag if nk == nq else jnp.concatenate([z[:, :nk - nq], diag], axis=1)
    else:
        r, c = _iota2(nq, nk)
        z = jnp.where(c <= (r + qpos0), z, -jnp.inf)
    return _softmax_rows(z)


def mem_head(qm, wq, mk, mv):
    p = _softmax_rows(mm_nt(rms(qm, wq) * SCALE, mk))
    return mm(p, mv)


def _heads(x, n):
    return [x[:, h * HEAD_DIM:(h + 1) * HEAD_DIM] for h in range(n)]


def memkv_fn(mem, mnw, wkv, mknw):
    kv = mm(rms(mem, mnw), wkv)
    mk = jnp.concatenate([rms(kh, mknw) for kh in _heads(kv[:, :MEM_WIDTH], MEM_HEADS)], axis=1)
    return mk, kv[:, MEM_WIDTH:]


def dn_gates_fn(ab, alog, dtb):
    g = -jnp.exp(alog) * jax.nn.softplus(ab + dtb)
    low = _lower_ones(CHUNK)
    gc = jnp.concatenate([hdot(low, g[i * CHUNK:(i + 1) * CHUNK]) for i in range(ab.shape[0] // CHUNK)], axis=0)
    lane = lax.broadcasted_iota(jnp.int32, ab.shape, 1)
    return jnp.where(lane < N_HEADS, gc, jax.nn.sigmoid(ab))


def fox_fcum_fn(fp, fbias):
    lf = jax.nn.log_sigmoid(fp + fbias)
    low = _lower_ones(LANES)
    carry = jnp.zeros((1, fp.shape[1]), F32)
    outs = []
    for i in range(fp.shape[0] // LANES):
        cs = hdot(low, lf[i * LANES:(i + 1) * LANES]) + carry
        carry = _last_row(cs)
        outs.append(cs)
    return jnp.concatenate(outs, axis=0)


def fox_qk_fn(qraw, kraw, qnw, knw):
    q = jnp.concatenate([rms(x, qnw) * SCALE for x in _heads(qraw, N_HEADS)], axis=1)
    k = jnp.concatenate([rms(x, knw) for x in _heads(kraw, N_HEADS)], axis=1)
    return q, k


def _mem_out(qm, mqw, mk, mv):
    return [mem_head(a, mqw, b, c) for a, b, c in zip(_heads(qm, MEM_HEADS), _heads(mk, MEM_HEADS), _heads(mv, MEM_HEADS))]


def dn_out_fn(o, z, qm, onw, mqw, mk, mv):
    mix = [rms(a, onw) * jax.nn.silu(b) for a, b in zip(o, _heads(z, N_HEADS))]
    return jnp.concatenate(mix + _mem_out(qm, mqw, mk, mv), axis=1)


def fox_out_fn(o, gate, qm, mqw, mk, mv):
    return jnp.concatenate([o * jax.nn.sigmoid(gate)] + _mem_out(qm, mqw, mk, mv), axis=1)


_HBM = pl.BlockSpec(memory_space=pltpu.HBM)


def _place():
    return lax.axis_index("x"), lax.axis_index("y"), lax.axis_index("c")


class Rider:
    def __init__(self, ins, out_shape, scratch, start, finish):
        self.ins, self.out_shape, self.scratch, self.start, self.finish = list(ins), list(out_shape), list(scratch), start, finish
        self.results = None


def gather_rider(xs):
    n = len(xs)

    def plan(x_refs, out_refs, sems):
        send_sems, recv_sems, local_sems = sems
        x, y, c = _place()
        me, sibling = (x, y, c), (x, y, 1 - c)
        chips = [(1 - x, y), (x, 1 - y), (1 - x, 1 - y)]

        def copy(a, k, block, to, src=None):
            px, py, pc = block
            dst = out_refs[a].at[4 * px + 2 * py + pc]
            return pltpu.make_async_remote_copy(
                src_ref=dst if src is None else src, dst_ref=dst,
                send_sem=send_sems.at[a, k], recv_sem=recv_sems.at[a, k], device_id=to, device_id_type=MESH)

        mine = [pltpu.make_async_copy(x_refs[a], out_refs[a].at[4 * x + 2 * y + c], local_sems.at[a]) for a in range(n)]
        first = [copy(a, 0, me, sibling, src=x_refs[a]) for a in range(n)]
        first += [copy(a, 1 + j, me, (*chip, c), src=x_refs[a]) for j, chip in enumerate(chips) for a in range(n)]
        return copy, me, sibling, chips, mine, first

    def start(x_refs, out_refs, sems):
        _, _, _, _, mine, first = plan(x_refs, out_refs, sems)
        for cp in mine + first:
            cp.start()

    def finish(x_refs, out_refs, sems):
        copy, me, sibling, chips, mine, first = plan(x_refs, out_refs, sems)
        _, _, c = me
        passed = []
        for j, chip in enumerate(chips):
            for a in range(n):
                copy(a, 1 + j, (*chip, c), me).wait_recv()
                passed.append(copy(a, 4 + j, (*chip, c), sibling))
                passed[-1].start()
        for a in range(n):
            copy(a, 0, sibling, me).wait_recv()
        for j, chip in enumerate(chips):
            for a in range(n):
                copy(a, 4 + j, (*chip, 1 - c), me).wait_recv()
        for cp in first + passed:
            cp.wait_send()
        for cp in mine:
            cp.wait()

    return Rider(xs, [jax.ShapeDtypeStruct((N_DEV,) + a.shape, a.dtype) for a in xs],
                 [pltpu.SemaphoreType.DMA((n, 7)), pltpu.SemaphoreType.DMA((n, 7)), pltpu.SemaphoreType.DMA((n,))], start, finish)


def sibling_rider(gs):
    n = len(gs)

    def plan(g_refs, out_refs, sems):
        send_sems, recv_sems = sems
        x, y, c = _place()
        return [pltpu.make_async_remote_copy(
            src_ref=g_refs[a].at[2 * k + 1 - c], dst_ref=out_refs[a].at[k], send_sem=send_sems.at[a, k],
            recv_sem=recv_sems.at[a, k], device_id=(x, y, 1 - c), device_id_type=MESH) for a in range(n) for k in range(4)]

    def start(g_refs, out_refs, sems):
        for cp in plan(g_refs, out_refs, sems):
            cp.start()

    def finish(g_refs, out_refs, sems):
        copies = plan(g_refs, out_refs, sems)
        for cp in copies:
            cp.wait_recv()
        for cp in copies:
            cp.wait_send()

    return Rider(gs, [jax.ShapeDtypeStruct((4,) + g.shape[1:], g.dtype) for g in gs],
                 [pltpu.SemaphoreType.DMA((n, 4)), pltpu.SemaphoreType.DMA((n, 4))], start, finish)


def chips_rider(hs):
    n = len(hs)

    def plan(h_refs, out_refs, sems):
        send_sems, recv_sems, local_sems = sems
        x, y, c = _place()
        mine = 2 * x + y
        chips = [(1 - x, y), (x, 1 - y), (1 - x, 1 - y)]
        keep = [pltpu.make_async_copy(h_refs[a].at[mine], out_refs[a].at[mine], local_sems.at[a]) for a in range(n)]
        sends = [pltpu.make_async_remote_copy(
            src_ref=h_refs[a].at[2 * qx + qy], dst_ref=out_refs[a].at[mine], send_sem=send_sems.at[a, j],
            recv_sem=recv_sems.at[a, j], device_id=(qx, qy, c), device_id_type=MESH)
            for j, (qx, qy) in enumerate(chips) for a in range(n)]
        recvs = [pltpu.make_async_remote_copy(
            src_ref=h_refs[a].at[mine], dst_ref=out_refs[a].at[2 * qx + qy], send_sem=send_sems.at[a, j],
            recv_sem=recv_sems.at[a, j], device_id=(qx, qy, c), device_id_type=MESH)
            for j, (qx, qy) in enumerate(chips) for a in range(n)]
        return keep, sends, recvs

    def start(h_refs, out_refs, sems):
        keep, sends, _ = plan(h_refs, out_refs, sems)
        for cp in keep + sends:
            cp.start()

    def finish(h_refs, out_refs, sems):
        keep, sends, recvs = plan(h_refs, out_refs, sems)
        for cp in recvs:
            cp.wait_recv()
        for cp in sends:
            cp.wait_send()
        for cp in keep:
            cp.wait()

    return Rider(hs, [jax.ShapeDtypeStruct(h.shape, h.dtype) for h in hs],
                 [pltpu.SemaphoreType.DMA((n, 3)), pltpu.SemaphoreType.DMA((n, 3)), pltpu.SemaphoreType.DMA((n,))], start, finish)


def hosted_call(riders, body, *, out_shape, in_specs, out_specs, grid=(), scratch_shapes=(), **kw):
    riders = tuple(riders or ())
    if not riders:
        return pl.pallas_call(body, out_shape=out_shape, in_specs=in_specs, out_specs=out_specs, grid=grid,
                              scratch_shapes=scratch_shapes, **kw)
    single = not isinstance(out_shape, (list, tuple))
    k_out_shape = [out_shape] if single else list(out_shape)
    k_out_specs = [out_specs] if single else list(out_specs)
    n_in, n_out, n_scr = len(in_specs), len(k_out_shape), len(scratch_shapes)
    r_ins = [a for r in riders for a in r.ins]
    r_outs = [s for r in riders for s in r.out_shape]
    r_scr = [s for r in riders for s in r.scratch]

    def full_body(*refs):
        ins = refs[:n_in + len(r_ins)]
        outs = refs[n_in + len(r_ins):n_in + len(r_ins) + n_out + len(r_outs)]
        scr = refs[n_in + len(r_ins) + n_out + len(r_outs):]
        steps = math.prod(grid)
        step = 0
        for d, g in enumerate(grid):
            step = step * g + pl.program_id(d)

        def each(method):
            i0, o0, s0 = n_in, n_out, n_scr
            for r in riders:
                getattr(r, method)(ins[i0:i0 + len(r.ins)], outs[o0:o0 + len(r.out_shape)], scr[s0:s0 + len(r.scratch)])
                i0, o0, s0 = i0 + len(r.ins), o0 + len(r.out_shape), s0 + len(r.scratch)

        if steps == 1:
            each("start")
            body(*ins[:n_in], *outs[:n_out], *scr[:n_scr])
            each("finish")
        else:
            pl.when(step == 0)(lambda: each("start"))
            body(*ins[:n_in], *outs[:n_out], *scr[:n_scr])
            pl.when(step == steps - 1)(lambda: each("finish"))

    call = pl.pallas_call(
        full_body, out_shape=k_out_shape + r_outs, in_specs=list(in_specs) + [_HBM] * len(r_ins),
        out_specs=k_out_specs + [_HBM] * len(r_outs), grid=grid, scratch_shapes=list(scratch_shapes) + r_scr, **kw)

    def run(*args):
        res = call(*args, *r_ins)
        o0 = n_out
        for r in riders:
            r.results = list(res[o0:o0 + len(r.out_shape)])
            o0 += len(r.out_shape)
        return res[0] if single else list(res[:n_out])

    return run


def run_riders(riders, *, name):
    hosted_call(riders, lambda: None, name=name, out_shape=[], in_specs=[], out_specs=[])()
    return [r.results for r in riders]


def _pick(n, cands):
    for c in cands:
        if n % c == 0:
            return c
    return n


def _params(sem):
    return pltpu.CompilerParams(dimension_semantics=sem, vmem_limit_bytes=VMEM_LIMIT)


MATMUL_VMEM_BUDGET = 40 * 2 ** 20


def _matmul_tiles(m, n, k, bytes_a, bytes_b, bytes_mn, fixed):
    fm, fn, fk = fixed if fixed is not None else (None, None, None)

    def options(given, size, cands):
        return [given] if given else ([c for c in cands if size % c == 0] or [size])

    best = None
    for tm in options(fm, m, (2048, 1024, 512, 256, 128)):
        for tn in options(fn, n, (512, 256, 128)):
            for tk in options(fk, k, (2048, 1536, 1024, 512, 256, 128)):
                if 2 * (tm * tk * bytes_a + tk * tn * bytes_b + tm * tn * bytes_mn) + tm * tn * 4 > MATMUL_VMEM_BUDGET:
                    continue
                key = ((m // tm) * (n // tn) * (k // tk), -tk)
                if best is None or key < best[0]:
                    best = (key, (tm, tn, tk))
    assert best is not None, (m, n, k, fixed)
    return best[1]


def matmul(a, b, *, name, ta=False, tb=False, post=None, post_ins=(), row_ins=(), acc=False, extra_out=None,
           out_dtype=F32, tiles=None, b_view=None, out_view=None, riders=()):
    (k, m) = a.shape if ta else a.shape[::-1]
    (kb, n) = b_view[:2] if b_view is not None else (b.shape[::-1] if tb else b.shape)
    assert k == kb, (a.shape, b.shape, ta, tb)
    bytes_mn = sum(p.dtype.itemsize for p in post_ins) + jnp.dtype(out_dtype).itemsize
    bytes_mn += jnp.dtype(extra_out[1]).itemsize if extra_out else 0
    tm, tn, tk = _matmul_tiles(m, n, k, a.dtype.itemsize, b.dtype.itemsize, bytes_mn, tiles)
    assert not acc or tn == n, (name, tn, n)
    nk = k // tk
    dims = ((0,) if ta else (1,), (1,) if tb else (0,))
    n_post, n_row = len(post_ins), len(row_ins)
    n_out = 1 + bool(extra_out) + bool(acc)

    def body(*refs):
        a_ref, b_ref = refs[:2]
        post_refs = refs[2:2 + n_post + n_row]
        o_refs, acc_ref = refs[-1 - n_out:-1], refs[-1]
        first_rows, kk = pl.program_id(0) == 0, pl.program_id(2)

        @pl.when(kk == 0)
        def _():
            acc_ref[...] = jnp.zeros_like(acc_ref)

        b_tile = b_ref[...]
        acc_ref[...] += _bdot(a_ref[...], b_tile.reshape(-1, b_tile.shape[-1]), dims)

        @pl.when(kk == nk - 1)
        def _():
            r = acc_ref[...]
            rows = [p[...] for p in post_refs[n_post:]]
            if post is not None:
                r = post(r, *[p[...] for p in post_refs[:n_post]], *rows)
            if acc:
                r, s = r
                sum_ref = o_refs[-1]

                @pl.when(first_rows)
                def _():
                    sum_ref[...] = s

                @pl.when(jnp.logical_not(first_rows))
                def _():
                    sum_ref[...] += s

            o_refs[0][...] = r.astype(out_dtype)
            if extra_out:
                o_refs[1][...] = extra_out[0](r, *rows).astype(extra_out[1])

    a_spec = pl.BlockSpec((tk, tm), lambda i, j, kk: (kk, i)) if ta else pl.BlockSpec((tm, tk), lambda i, j, kk: (i, kk))
    if b_view is not None:
        b_spec = b_view[2]
    else:
        b_spec = pl.BlockSpec((tn, tk), lambda i, j, kk: (j, kk)) if tb else pl.BlockSpec((tk, tn), lambda i, j, kk: (kk, j))
    mn_spec = pl.BlockSpec((tm, tn), lambda i, j, kk: (i, j))
    row_spec = pl.BlockSpec((1, tn), lambda i, j, kk: (0, j))
    o_shape, o_spec = ((m, n), mn_spec) if out_view is None else out_view
    out_shape = [jax.ShapeDtypeStruct(o_shape, out_dtype)]
    out_specs = [o_spec]
    if extra_out:
        out_shape.append(jax.ShapeDtypeStruct((m, n), extra_out[1]))
        out_specs.append(mn_spec)
    if acc:
        out_shape.append(jax.ShapeDtypeStruct((1, n), F32))
        out_specs.append(row_spec)
    res = hosted_call(
        riders, body, name=name, grid=(m // tm, n // tn, nk),
        in_specs=[a_spec, b_spec] + [mn_spec] * n_post + [row_spec] * n_row, out_specs=out_specs, out_shape=out_shape,
        scratch_shapes=[pltpu.VMEM((tm, tn), F32)],
        compiler_params=_params(("arbitrary" if acc else "parallel", "parallel", "arbitrary")),
    )(a, b, *post_ins, *row_ins)
    return res if n_out > 1 else res[0]


def rows_call(fn, row_ins, full_ins, row_outs, acc_outs, *, tm, name, riders=()):
    row_ins = [r if isinstance(r, tuple) else (r, r.shape[-1], 0) for r in row_ins]
    t = row_ins[0][0].shape[-2]
    tm = min(tm, t)
    n_in = len(row_ins) + len(full_ins)
    n_row = len(row_outs)

    def body(*refs):
        res = fn(*[[r[h] for h in range(r.shape[0])] if (i < len(row_ins) and len(r.shape) == 3) else r[...]
                   for i, r in enumerate(refs[:n_in])])
        res = res if isinstance(res, (tuple, list)) else (res,)
        outs = refs[n_in:]
        for ref, val in zip(outs[:n_row], res[:n_row]):
            if len(ref.shape) == 3:
                for h, vh in enumerate(val):
                    ref[h] = vh.astype(ref.dtype)
            else:
                ref[...] = val.astype(ref.dtype)
        first = pl.program_id(0) == 0
        for ref, val in zip(outs[n_row:], res[n_row:]):
            @pl.when(first)
            def _(ref=ref, val=val):
                ref[...] = val

            @pl.when(jnp.logical_not(first))
            def _(ref=ref, val=val):
                ref[...] += val

    def full_spec(shape):
        return pl.BlockSpec(shape, lambda i, nd=len(shape): (0,) * nd)

    def row_spec(lead, w, cb):
        if lead is None:
            return pl.BlockSpec((tm, w), lambda i: (i, cb))
        return pl.BlockSpec((lead, tm, w), lambda i: (0, i, cb))

    def lead_cols(c):
        return c if isinstance(c, tuple) else (None, c)

    in_specs = [row_spec(a.shape[0] if a.ndim == 3 else None, w, cb) for (a, w, cb) in row_ins]
    in_specs += [full_spec(f.shape) for f in full_ins]
    out_specs = [row_spec(*lead_cols(c), 0) for c, _ in row_outs] + [full_spec(s) for s in acc_outs]
    out_shape = [jax.ShapeDtypeStruct(tuple(d for d in (lead_cols(c)[0], t, lead_cols(c)[1]) if d is not None), dt)
                 for c, dt in row_outs] + [jax.ShapeDtypeStruct(s, F32) for s in acc_outs]
    res = hosted_call(
        riders, body, name=name, grid=(t // tm,), in_specs=in_specs, out_specs=out_specs, out_shape=out_shape,
        compiler_params=_params(("arbitrary",)),
    )(*[r[0] for r in row_ins], *full_ins)
    return res


def vjp_rows(fn, n_diff_row, row_diff_full):
    def bwd(*args, n_row, n_ct):
        prim_rows = args[:n_row]
        cts = args[n_row:n_row + n_ct]
        fulls = args[n_row + n_ct:]
        _, vjp = jax.vjp(fn, *prim_rows, *fulls)
        g = vjp(cts[0] if n_ct == 1 else tuple(cts))
        out = list(g[:n_diff_row])
        out += [gf for gf, d in zip(g[n_row:], row_diff_full) if d]
        return tuple(out)
    return bwd


def _shift_down(x, s):
    if s == 0:
        return x
    t = lax.broadcasted_iota(jnp.int32, x.shape, 0)
    return jnp.where(t >= s, pltpu.roll(x, s, 0), 0.0)


def _shift_up(x, s):
    if s == 0:
        return x
    n = x.shape[0]
    t = lax.broadcasted_iota(jnp.int32, x.shape, 0)
    return jnp.where(t < n - s, pltpu.roll(x, n - s, 0), 0.0)


def _conv(x, w_ref):
    return sum(w_ref[pl.ds(j, 1), :] * _shift_down(x, CONV_WIDTH - 1 - j) for j in range(CONV_WIDTH))


_DN_POST = (lambda c: l2n(jax.nn.silu(c)) * SCALE, lambda c: l2n(jax.nn.silu(c)), jax.nn.silu)


def dn_prep_fwd(proj, conv_w, *, name, riders=()):
    t = proj.shape[0]

    def body(xq, xk, xv, wq, wk, wv, oq, ok, ov):
        for x_ref, w_ref, o_ref, post in zip((xq, xk, xv), (wq, wk, wv), (oq, ok, ov), _DN_POST):
            o_ref[...] = post(_conv(x_ref[...], w_ref))

    x_specs = [pl.BlockSpec((t, HEAD_DIM), lambda h, g=g: (0, g * N_HEADS + h)) for g in range(3)]
    w_specs = [pl.BlockSpec((CONV_WIDTH, HEAD_DIM), lambda h, g=g: (0, g * N_HEADS + h)) for g in range(3)]
    o_spec = pl.BlockSpec((None, t, HEAD_DIM), lambda h: (h, 0, 0))
    return hosted_call(
        riders, body, name=name, grid=(N_HEADS,), in_specs=x_specs + w_specs, out_specs=[o_spec] * 3,
        out_shape=[jax.ShapeDtypeStruct((N_HEADS, t, HEAD_DIM), F32)] * 3, compiler_params=_params(("parallel",)),
    )(proj, proj, proj, conv_w, conv_w, conv_w)


def dn_prep_bwd(proj, conv_w, dq, dk, dv, *, name, riders=()):
    t = proj.shape[0]

    def body(xq, xk, xv, wq, wk, wv, gq, gk, gv, dxq, dxk, dxv, dwq, dwk, dwv):
        for x_ref, w_ref, g_ref, dx_ref, dw_ref, post in zip(
                (xq, xk, xv), (wq, wk, wv), (gq, gk, gv), (dxq, dxk, dxv), (dwq, dwk, dwv), _DN_POST):
            x = x_ref[...]
            _, vjp = jax.vjp(post, _conv(x, w_ref))
            dc, = vjp(g_ref[...])
            dx = sum(w_ref[pl.ds(j, 1), :] * _shift_up(dc, CONV_WIDTH - 1 - j) for j in range(CONV_WIDTH))
            dx_ref[...] = dx.astype(dx_ref.dtype)
            for j in range(CONV_WIDTH):
                dw_ref[pl.ds(j, 1), :] = jnp.sum(dc * _shift_down(x, CONV_WIDTH - 1 - j), axis=0, keepdims=True)

    x_specs = [pl.BlockSpec((t, HEAD_DIM), lambda h, g=g: (0, g * N_HEADS + h)) for g in range(3)]
    w_specs = [pl.BlockSpec((CONV_WIDTH, HEAD_DIM), lambda h, g=g: (0, g * N_HEADS + h)) for g in range(3)]
    g_spec = pl.BlockSpec((None, t, HEAD_DIM), lambda h: (h, 0, 0))
    dx_spec = pl.BlockSpec((t, HEAD_DIM), lambda h: (0, h))
    dw_spec = pl.BlockSpec((CONV_WIDTH, HEAD_DIM), lambda h: (0, h))
    return hosted_call(
        riders, body, name=name, grid=(N_HEADS,), in_specs=x_specs + w_specs + [g_spec] * 3, out_specs=[dx_spec] * 3 + [dw_spec] * 3,
        out_shape=[jax.ShapeDtypeStruct((t, D_MODEL), BF16)] * 3 + [jax.ShapeDtypeStruct((CONV_WIDTH, D_MODEL), F32)] * 3,
        compiler_params=_params(("parallel",)),
    )(proj, proj, proj, conv_w, conv_w, conv_w, dq, dk, dv)


INTRA_CHUNKS = 4


def _lane_column(x, lane_index):
    lane = lax.broadcasted_iota(jnp.int32, x.shape, 1)
    return jnp.sum(jnp.where(lane == lane_index, x, 0.0), axis=1, keepdims=True)


def _head_columns(g, first_lane):
    return jnp.concatenate([_lane_column(g, first_lane + h)[None] for h in range(N_HEADS)], axis=0)


def _intra_of_gates(q, k, v, gates):
    nb = N_HEADS * (gates.shape[0] // CHUNK)

    def chunks(x):
        return x.reshape(nb, CHUNK, x.shape[-1])

    res = delta_intra(chunks(q), chunks(k), chunks(v), chunks(_head_columns(gates, 0)), chunks(_head_columns(gates, N_HEADS)))
    return tuple(x.reshape(N_HEADS, -1, x.shape[-1]) for x in res)


def _step_of_gates(s, q, k, gates, u, w, qk):
    return delta_step(s, q, k, _head_columns(gates, 0), u, w, qk)


def _head_major(rows, w, index):
    return pl.BlockSpec((N_HEADS, rows, w), lambda i: (0, index(i), 0))


def delta_intra_fwd(q, k, v, gates, *, name, riders=()):
    t = q.shape[1]
    rows = min(INTRA_CHUNKS, t // CHUNK) * CHUNK

    def body(q_ref, k_ref, v_ref, g_ref, u_ref, w_ref, qk_ref):
        for ref, val in zip((u_ref, w_ref, qk_ref), _intra_of_gates(q_ref[...], k_ref[...], v_ref[...], g_ref[...])):
            ref[...] = val

    x_spec, qk_spec = (_head_major(rows, w, lambda i: i) for w in (HEAD_DIM, CHUNK))
    g_spec = pl.BlockSpec((rows, LANES), lambda i: (i, 0))
    return hosted_call(
        riders, body, name=name, grid=(t // rows,), in_specs=[x_spec] * 3 + [g_spec], out_specs=[x_spec, x_spec, qk_spec],
        out_shape=[jax.ShapeDtypeStruct((N_HEADS, t, HEAD_DIM), F32)] * 2 + [jax.ShapeDtypeStruct((N_HEADS, t, CHUNK), F32)],
        compiler_params=_params(("parallel",)),
    )(q, k, v, gates)


def delta_seq_fwd(q, k, gates, u, w, qk, *, name, riders=()):
    t = q.shape[1]
    nc = t // CHUNK

    def body(q_ref, k_ref, g_ref, u_ref, w_ref, qk_ref, o_ref, s0_ref, s_ref):
        @pl.when(pl.program_id(0) == 0)
        def _():
            s_ref[...] = jnp.zeros_like(s_ref)

        s = s_ref[...]
        s0_ref[...] = s
        o, s_new = _step_of_gates(s, q_ref[...], k_ref[...], g_ref[...], u_ref[...], w_ref[...], qk_ref[...])
        o_ref[...] = o
        s_ref[...] = s_new

    x_spec, qk_spec = (_head_major(CHUNK, w, lambda c: c) for w in (HEAD_DIM, CHUNK))
    g_spec = pl.BlockSpec((CHUNK, LANES), lambda c: (c, 0))
    s_spec = pl.BlockSpec((N_HEADS, None, HEAD_DIM, HEAD_DIM), lambda c: (0, c, 0, 0))
    return hosted_call(
        riders, body, name=name, grid=(nc,), in_specs=[x_spec, x_spec, g_spec, x_spec, x_spec, qk_spec], out_specs=[x_spec, s_spec],
        out_shape=[jax.ShapeDtypeStruct((N_HEADS, t, HEAD_DIM), F32),
                   jax.ShapeDtypeStruct((N_HEADS, nc, HEAD_DIM, HEAD_DIM), F32)],
        scratch_shapes=[pltpu.VMEM((N_HEADS, HEAD_DIM, HEAD_DIM), F32)],
        compiler_params=_params(("arbitrary",)),
    )(q, k, gates, u, w, qk)


def delta_seq_bwd(q, k, gates, u, w, qk, s0, do, *, name, riders=()):
    t = q.shape[1]
    nc = t // CHUNK

    def body(q_ref, k_ref, g_ref, u_ref, w_ref, qk_ref, s0_ref, do_ref,
             dq_ref, dk_ref, dg_ref, du_ref, dw_ref, dqk_ref, ds_ref):
        @pl.when(pl.program_id(0) == 0)
        def _():
            ds_ref[...] = jnp.zeros_like(ds_ref)

        _, vjp = jax.vjp(_step_of_gates, s0_ref[...], q_ref[...], k_ref[...], g_ref[...], u_ref[...], w_ref[...], qk_ref[...])
        ds, dq, dk, dg, du, dw, dqk = vjp((do_ref[...], ds_ref[...]))
        for ref, val in zip((ds_ref, dq_ref, dk_ref, dg_ref, du_ref, dw_ref, dqk_ref), (ds, dq, dk, dg, du, dw, dqk)):
            ref[...] = val

    x_spec, qk_spec = (_head_major(CHUNK, w, lambda c: nc - 1 - c) for w in (HEAD_DIM, CHUNK))
    g_spec = pl.BlockSpec((CHUNK, LANES), lambda c: (nc - 1 - c, 0))
    s_spec = pl.BlockSpec((N_HEADS, None, HEAD_DIM, HEAD_DIM), lambda c: (0, nc - 1 - c, 0, 0))
    head_shape = [jax.ShapeDtypeStruct((N_HEADS, t, w_), F32) for w_ in (HEAD_DIM, HEAD_DIM, HEAD_DIM, HEAD_DIM, CHUNK)]
    return hosted_call(
        riders, body, name=name, grid=(nc,), in_specs=[x_spec, x_spec, g_spec, x_spec, x_spec, qk_spec, s_spec, x_spec],
        out_specs=[x_spec, x_spec, g_spec, x_spec, x_spec, qk_spec],
        out_shape=head_shape[:2] + [jax.ShapeDtypeStruct((t, LANES), F32)] + head_shape[2:],
        scratch_shapes=[pltpu.VMEM((N_HEADS, HEAD_DIM, HEAD_DIM), F32)],
        compiler_params=_params(("arbitrary",)),
    )(q, k, gates, u, w, qk, s0, do)


def delta_intra_bwd(q, k, v, gates, du, dw, dqk, dq_s, dk_s, dg_s, *, name, riders=()):
    t = q.shape[1]
    rows = min(INTRA_CHUNKS, t // CHUNK) * CHUNK

    def body(q_ref, k_ref, v_ref, g_ref, du_ref, dw_ref, dqk_ref, dqs_ref, dks_ref, dgs_ref, dq_ref, dk_ref, dv_ref, dg_ref):
        _, vjp = jax.vjp(_intra_of_gates, q_ref[...], k_ref[...], v_ref[...], g_ref[...])
        dq, dk, dv, dg = vjp((du_ref[...], dw_ref[...], dqk_ref[...]))
        dq_ref[...] = dq + dqs_ref[...]
        dk_ref[...] = dk + dks_ref[...]
        dv_ref[...] = dv
        dg_ref[...] = dg + dgs_ref[...]

    x_spec, qk_spec = (_head_major(rows, w, lambda i: i) for w in (HEAD_DIM, CHUNK))
    g_spec = pl.BlockSpec((rows, LANES), lambda i: (i, 0))
    return hosted_call(
        riders, body, name=name, grid=(t // rows,),
        in_specs=[x_spec] * 3 + [g_spec, x_spec, x_spec, qk_spec, x_spec, x_spec, g_spec],
        out_specs=[x_spec] * 3 + [g_spec],
        out_shape=[jax.ShapeDtypeStruct((N_HEADS, t, HEAD_DIM), F32)] * 3 + [jax.ShapeDtypeStruct((t, LANES), F32)],
        compiler_params=_params(("parallel",)),
    )(q, k, v, gates, du, dw, dqk, dq_s, dk_s, dg_s)


_V_BLOCK = 2 * N_HEADS
FOX_GROUPS = 16


def _fox_groups(t):
    nq = t // Q_BLOCK
    per = max(1, nq // FOX_GROUPS)
    return [(g0, per, (g0 + per) * Q_BLOCK) for g0 in range(0, nq, per)]


def fox_attn_fwd(q, k, proj, fq, fk, *, name, riders=()):
    t = q.shape[0]

    def body(q_ref, k_ref, v_ref, fq_ref, fk_ref, o_ref, kb_ref, vb_ref):
        head = pl.program_id(0)
        kb_ref[...] = k_ref[...].astype(BF16)
        vb_ref[...] = v_ref[...].astype(BF16)
        for g0, per, keys in _fox_groups(t):
            def block(j, carry, g0=g0, keys=keys):
                rows = pl.ds((g0 + j) * Q_BLOCK, Q_BLOCK)
                p = fox_probs(q_ref[rows, :].astype(BF16), kb_ref[0:keys, :], _lane_column(fq_ref[rows, :], head),
                              fk_ref[:, 0:keys], (g0 + j) * Q_BLOCK)
                o_ref[rows, :] = jnp.dot(p.astype(BF16), vb_ref[0:keys, :], preferred_element_type=F32)
                return carry
            for j in range(per):
                block(j, 0)

    x_spec = pl.BlockSpec((t, HEAD_DIM), lambda h: (0, h))
    v_spec = pl.BlockSpec((t, HEAD_DIM), lambda h: (0, _V_BLOCK + h))
    fq_spec = pl.BlockSpec((t, LANES), lambda h: (0, 0))
    fk_spec = pl.BlockSpec((None, 1, t), lambda h: (h, 0, 0))
    return hosted_call(
        riders, body, name=name, grid=(N_HEADS,), in_specs=[x_spec, x_spec, v_spec, fq_spec, fk_spec], out_specs=x_spec,
        out_shape=jax.ShapeDtypeStruct((t, D_MODEL), F32), scratch_shapes=[pltpu.VMEM((t, HEAD_DIM), BF16)] * 2,
        compiler_params=_params(("parallel",)),
    )(q, k, proj, fq, fk)


def fox_attn_bwd(q, k, proj, fq, fk, do, *, name, riders=()):
    t = q.shape[0]

    def body(q_ref, k_ref, v_ref, fq_ref, fk_ref, do_ref, dq_ref, dk_ref, dv_out_ref, dfq_ref, dfk_ref, kb_ref, vb_ref, dv_ref):
        head = pl.program_id(0)

        @pl.when(head == 0)
        def _():
            dfq_ref[...] = jnp.zeros_like(dfq_ref)

        kb_ref[...] = k_ref[...].astype(BF16)
        vb_ref[...] = v_ref[...].astype(BF16)
        dk_ref[...] = jnp.zeros_like(dk_ref)
        dv_ref[...] = jnp.zeros_like(dv_ref)
        dfk_ref[...] = jnp.zeros_like(dfk_ref)
        nt = (((1,), (1,)), ((), ()))
        tn = (((0,), (0,)), ((), ()))
        for g0, per, keys in _fox_groups(t):
            def block(j, carry, g0=g0, keys=keys):
                rows = pl.ds((g0 + j) * Q_BLOCK, Q_BLOCK)
                qb, dob = q_ref[rows, :].astype(BF16), do_ref[rows, :].astype(BF16)
                kb, vb = kb_ref[0:keys, :], vb_ref[0:keys, :]
                p = fox_probs(qb, kb, _lane_column(fq_ref[rows, :], head), fk_ref[:, 0:keys], (g0 + j) * Q_BLOCK)
                dp = lax.dot_general(dob, vb, nt, preferred_element_type=F32)
                dz = p * (dp - jnp.sum(dp * p, axis=-1, keepdims=True))
                pb, dzb = p.astype(BF16), dz.astype(BF16)
                dq_ref[rows, :] = jnp.dot(dzb, kb, preferred_element_type=F32)
                lane = lax.broadcasted_iota(jnp.int32, (Q_BLOCK, LANES), 1)
                dfq_ref[rows, :] += jnp.where(lane == head, jnp.sum(dz, axis=-1, keepdims=True), 0.0)
                dk_ref[0:keys, :] += lax.dot_general(dzb, qb, tn, preferred_element_type=F32)
                dv_ref[0:keys, :] += lax.dot_general(pb, dob, tn, preferred_element_type=F32)
                dfk_ref[:, 0:keys] -= jnp.sum(dz, axis=0, keepdims=True)
                return carry
            for j in range(per):
                block(j, 0)
        dv_out_ref[...] = dv_ref[...].astype(dv_out_ref.dtype)

    x_spec = pl.BlockSpec((t, HEAD_DIM), lambda h: (0, h))
    v_spec = pl.BlockSpec((t, HEAD_DIM), lambda h: (0, _V_BLOCK + h))
    fq_spec = pl.BlockSpec((t, LANES), lambda h: (0, 0))
    fk_spec = pl.BlockSpec((None, 1, t), lambda h: (h, 0, 0))
    return hosted_call(
        riders, body, name=name, grid=(N_HEADS,), in_specs=[x_spec, x_spec, v_spec, fq_spec, fk_spec, x_spec],
        out_specs=[x_spec, x_spec, x_spec, fq_spec, fk_spec],
        out_shape=[jax.ShapeDtypeStruct((t, D_MODEL), F32)] * 2 + [jax.ShapeDtypeStruct((t, D_MODEL), BF16)]
        + [jax.ShapeDtypeStruct((t, LANES), F32), jax.ShapeDtypeStruct((N_HEADS, 1, t), F32)],
        scratch_shapes=[pltpu.VMEM((t, HEAD_DIM), BF16)] * 2 + [pltpu.VMEM((t, HEAD_DIM), F32)],
        compiler_params=_params(("arbitrary",)),
    )(q, k, proj, fq, fk, do)


def memkv_fwd(mem, mnw, wkv, mknw, *, name):
    n = mem.shape[0]

    def body(mem_ref, mnw_ref, w_ref, mknw_ref, mk_ref, mv_ref):
        mk, mv = memkv_fn(mem_ref[...], mnw_ref[...], w_ref[...], mknw_ref[...])
        mk_ref[...] = mk
        mv_ref[...] = mv

    return pl.pallas_call(
        body, name=name, out_shape=[jax.ShapeDtypeStruct((n, MEM_WIDTH), F32)] * 2,
        compiler_params=pltpu.CompilerParams(vmem_limit_bytes=VMEM_LIMIT),
    )(mem, mnw, wkv, mknw)


def memkv_bwd(mem, mnw, wkv, mknw, dmk, dmv, *, name):
    def body(mem_ref, mnw_ref, w_ref, mknw_ref, dmk_ref, dmv_ref, dmnw_ref, dw_ref, dmknw_ref):
        f = functools.partial(memkv_fn, mem_ref[...])
        _, vjp = jax.vjp(f, mnw_ref[...], w_ref[...].astype(F32), mknw_ref[...])
        dmnw, dw, dmknw = vjp((dmk_ref[...], dmv_ref[...]))
        dmnw_ref[...] = dmnw
        dw_ref[...] = dw.astype(dw_ref.dtype)
        dmknw_ref[...] = dmknw

    return pl.pallas_call(
        body, name=name,
        out_shape=[jax.ShapeDtypeStruct(mnw.shape, F32), jax.ShapeDtypeStruct(wkv.shape, BF16), jax.ShapeDtypeStruct(mknw.shape, F32)],
        compiler_params=pltpu.CompilerParams(vmem_limit_bytes=VMEM_LIMIT),
    )(mem, mnw, wkv, mknw, dmk, dmv)


def _row(v, width=None):
    v = v.reshape(1, -1)
    if width is not None and v.shape[1] < width:
        v = jnp.pad(v, ((0, 0), (0, width - v.shape[1])))
    return v


def _norm_fwd(x, w, name, riders=()):
    return rows_call(lambda x, w: rms(x, w), [x], [w], [(D_MODEL, BF16)], [], tm=512, name=name, riders=riders)[0]


FF_PIECE = D_FF // N_DEV


def _add(r, x, *rows):
    return r + x


def _norm_rows(r, w):
    return rms(r, w)


def _norm_bwd_post(dh, x, dx_in, w):
    _, vjp = jax.vjp(rms, x, w)
    dx, dw = vjp(dh)
    return dx + dx_in, dw


def _piece(rows, cols, index):
    return pl.BlockSpec((None, rows, cols), lambda i, j, kk: (index(i, j, kk), 0, 0))


def _two_pieces(rows, cols, index):
    return pl.BlockSpec((2, rows, cols), lambda i, j, kk: (index(i, j, kk), 0, 0))


def _loss_post(r, x, tgt):
    e = r + x - tgt
    return e * (1.0 / D_MODEL), jnp.sum(e * e, axis=0, keepdims=True)


def _mlp_fwd(x, h2, w1, w2, layer, riders=(), next_norm_w=None, loss_target=None):
    riders = list(riders) + [None, None]
    u, a1 = matmul(h2, w1, name=f"mlp1_fwd_{layer}", tiles=(None, FF_PIECE, D_MODEL),
                   extra_out=(lambda u: jnp.square(jnp.maximum(u, 0.0)), BF16),
                   b_view=(D_MODEL, D_FF, _piece(D_MODEL, FF_PIECE, lambda i, j, kk: j)), riders=riders[0])
    if loss_target is not None:
        tail = dict(post=_loss_post, post_ins=[x, loss_target], acc=True)
    elif next_norm_w is not None:
        tail = dict(post=_add, post_ins=[x], row_ins=[next_norm_w], extra_out=(_norm_rows, BF16))
    else:
        tail = dict(post=_add, post_ins=[x])
    y = matmul(a1, w2, name=f"mlp2_fwd_{layer}", tiles=(None, D_MODEL, 2 * FF_PIECE),
               b_view=(D_FF, D_MODEL, _two_pieces(FF_PIECE, D_MODEL, lambda i, j, kk: kk)), riders=riders[1], **tail)
    return y, (x, h2, u, a1)


def pair_sum(g, got, *, name):
    _, rows, cols = g.shape
    tile = _pick(rows, (512, 256, 128))
    c = lax.axis_index("c").astype(jnp.int32).reshape(1)

    def body(c_ref, a_ref, b_ref, o_ref):
        o_ref[...] = (a_ref[...].astype(F32) + b_ref[...].astype(F32)).astype(o_ref.dtype)

    grid_spec = pltpu.PrefetchScalarGridSpec(
        num_scalar_prefetch=1, grid=(4, rows // tile),
        in_specs=[pl.BlockSpec((None, tile, cols), lambda k, i, c_ref: (2 * k + c_ref[0], i, 0)),
                  pl.BlockSpec((None, tile, cols), lambda k, i, c_ref: (k, i, 0))],
        out_specs=pl.BlockSpec((None, tile, cols), lambda k, i, c_ref: (k, i, 0)))
    return pl.pallas_call(
        body, name=name, grid_spec=grid_spec, out_shape=jax.ShapeDtypeStruct((4, rows, cols), g.dtype),
        compiler_params=_params(("parallel", "parallel")),
    )(c, g, got)


def chip_sums(names, pieces, gots):
    return [pair_sum(a, got, name=f"grads_pair_sum_{n}") for n, a, got in zip(names, pieces, gots)]


def _mlp_bwd(dy, res, n2w, w1, w2, layer, riders=()):
    x, h2, u, a1 = res
    du = matmul(dy, w2, tb=True, name=f"mlp2_dx_{layer}", out_dtype=BF16, tiles=(None, 2 * FF_PIECE, D_MODEL),
                post=lambda r, u: r * (2.0 * jnp.maximum(u, 0.0)), post_ins=[u],
                b_view=(D_MODEL, D_FF, _two_pieces(FF_PIECE, D_MODEL, lambda i, j, kk: j)), riders=riders)
    dw2 = matmul(a1, dy, ta=True, name=f"mlp2_dw_{layer}", out_dtype=BF16, tiles=(FF_PIECE, D_MODEL, None), out_view=(
        w2.shape, _piece(FF_PIECE, D_MODEL, lambda i, j, kk: i)))
    sib2 = sibling_rider([dw2])
    dx, dn2w = matmul(du, w1, tb=True, name=f"mlp1_dx_{layer}", tiles=(None, D_MODEL, FF_PIECE),
                      b_view=(D_FF, D_MODEL, _piece(D_MODEL, FF_PIECE, lambda i, j, kk: kk)),
                      post=_norm_bwd_post, post_ins=[x, dy], row_ins=[n2w], acc=True, riders=[sib2])
    dw1 = matmul(h2, du, ta=True, name=f"mlp1_dw_{layer}", out_dtype=BF16, tiles=(D_MODEL, FF_PIECE, None), out_view=(
        w1.shape, _piece(D_MODEL, FF_PIECE, lambda i, j, kk: j)))
    return dx, dw1, dw2, dn2w, sibling_rider([dw1]), sib2


def _in_proj_bwd(h, dmain, dsmall, w_main, w_small, x, dx_in, n1w, tag):
    dh = matmul(dmain, w_main, tb=True, name=f"inproj_dx_main_{tag}")

    def post(r, dh_main, x, dx_in, w):
        return _norm_bwd_post(r + dh_main, x, dx_in, w)

    dx, dn1w = matmul(dsmall, w_small, tb=True, name=f"inproj_dx_small_{tag}", tiles=(None, D_MODEL, None),
                      post=post, post_ins=[dh, x, dx_in], row_ins=[n1w], acc=True)
    dw_main = matmul(h, dmain, ta=True, out_dtype=BF16, name=f"inproj_dw_main_{tag}")
    dw_small = matmul(h, dsmall, ta=True, out_dtype=BF16, name=f"inproj_dw_small_{tag}")
    return dx, dn1w, dw_main, dw_small


def local_step(x, mem, target, w, m, v):
    t = x.shape[0]
    n_mem = mem.shape[0]
    g = {}

    def wire(a):
        return a.astype(BF16)

    ride_first = gather_rider([wire(w["dn_w_in"][0])])
    fox_w = wire(w["fox_w_in"][0])
    ride_out = gather_rider([wire(w["w_out"][0]), w["dn_conv_w"][0]])
    ride_out_1 = gather_rider([wire(w["w_out"][1])])
    ride_kv = gather_rider([wire(w["w_mem_kv"])])
    ride_mlp1_0 = gather_rider([wire(w["w_mlp1"][0])])
    ride_mlp2_0 = gather_rider([wire(w["w_mlp2"][0])])
    ride_fox_a, ride_fox_b = gather_rider([fox_w[:D_MODEL // 2]]), gather_rider([fox_w[D_MODEL // 2:]])
    ride_mlp_1 = gather_rider([wire(w["w_mlp1"][1]), wire(w["w_mlp2"][1])])
    mnw, mknw = _row(w["mem_norm_w"]), _row(w["mem_k_norm_w"])

    n1w0, n2w0 = _row(w["norm1_w"][0]), _row(w["norm2_w"][0])
    n1w1, n2w1 = _row(w["norm1_w"][1]), _row(w["norm2_w"][1])
    alog, dtb = _row(w["dn_a_log"][0], LANES), _row(w["dn_dt_bias"][0], LANES)
    onw, mqw0 = _row(w["dn_o_norm_w"][0]), _row(w["memq_norm_w"][0])
    x0 = x
    h0 = _norm_fwd(x0, n1w0, "norm1_fwd_0", riders=[ride_first])
    dn_main, dn_ab = in_proj_weights(ride_first.results[0], DN_IN, 2 * N_HEADS)
    pm0 = matmul(h0, dn_main, name="inproj_main_0", riders=[ride_out])
    w_out0 = ride_out.results[0].reshape(OUT_IN, D_MODEL)
    conv_w = ride_out.results[1].transpose(1, 0, 2).reshape(CONV_WIDTH, 3 * D_MODEL)
    ps0 = matmul(h0, dn_ab, name="inproj_small_0")
    gates = rows_call(dn_gates_fn, [ps0], [alog, dtb], [(LANES, F32)], [], tm=512, name="dn_gates_fwd")[0]
    q0, k0, v0 = dn_prep_fwd(pm0, conv_w, name="dn_prep_fwd", riders=[ride_kv])
    w_kv = ride_kv.results[0].reshape(D_MODEL, D_MODEL)
    mk, mv = memkv_fwd(mem, mnw, w_kv, mknw, name="memkv_fwd")
    u0, w0, qk0 = delta_intra_fwd(q0, k0, v0, gates, name="delta_intra_fwd", riders=[ride_mlp1_0])
    o0, s_start = delta_seq_fwd(q0, k0, gates, u0, w0, qk0, name="delta_seq_fwd", riders=[ride_mlp2_0])
    cat0 = rows_call(dn_out_fn, [o0, (pm0, D_MODEL, 3), (pm0, MEM_WIDTH, 8)], [onw, mqw0, mk, mv],
                     [(D_MODEL + MEM_WIDTH, BF16)], [], tm=256, name="dn_out_fwd")[0]
    (w1_0,), (w2_0,) = ride_mlp1_0.results, ride_mlp2_0.results
    x1, h2_0 = matmul(cat0, w_out0, post=_add, post_ins=[x0], row_ins=[n2w0], extra_out=(_norm_rows, BF16),
                      tiles=(None, D_MODEL, None), name="wout_fwd_0")
    (x2, h1), mlp_res0 = _mlp_fwd(x1, h2_0, w1_0, w2_0, 0, riders=[[ride_fox_a], [ride_fox_b]], next_norm_w=n1w1)
    fox_main, fox_f = in_proj_weights(
        jnp.concatenate([ride_fox_a.results[0], ride_fox_b.results[0]], axis=1), FOX_IN, N_HEADS)

    fbias = _row(w["fox_f_bias"][0], LANES)
    qnw, knw, mqw1 = _row(w["fox_q_norm_w"][0]), _row(w["fox_k_norm_w"][0]), _row(w["memq_norm_w"][1])
    pm1 = matmul(h1, fox_main, name="inproj_main_1", riders=[ride_out_1])
    w_out1 = ride_out_1.results[0].reshape(OUT_IN, D_MODEL)
    ps1 = matmul(h1, fox_f, name="inproj_small_1")
    fq = rows_call(fox_fcum_fn, [ps1], [fbias], [(LANES, F32)], [], tm=t, name="fox_fcum_fwd")[0]
    fk = fq[:, :N_HEADS].T[:, None, :]
    q1, k1 = rows_call(fox_qk_fn, [(pm1, D_MODEL, 0), (pm1, D_MODEL, 1)], [qnw, knw], [(D_MODEL, F32)] * 2, [], tm=256,
                       name="fox_qk_fwd")
    o1 = fox_attn_fwd(q1, k1, pm1, fq, fk, name="fox_attn_fwd", riders=[ride_mlp_1])
    cat1 = rows_call(fox_out_fn, [o1, (pm1, D_MODEL, 3), (pm1, MEM_WIDTH, 8)], [mqw1, mk, mv],
                     [(D_MODEL + MEM_WIDTH, BF16)], [], tm=256, name="fox_out_fwd")[0]
    w1_1, w2_1 = ride_mlp_1.results
    x3, h2_1 = matmul(cat1, w_out1, post=_add, post_ins=[x2], row_ins=[n2w1], extra_out=(_norm_rows, BF16),
                      tiles=(None, D_MODEL, None), name="wout_fwd_1")
    (dy, sq), mlp_res1 = _mlp_fwd(x3, h2_1, w1_1, w2_1, 1, loss_target=target)
    loss = jnp.sum(sq) * (0.5 / D_MODEL)

    dx3, dw1_1, dw2_1, dn2w1, sib1, sib2 = _mlp_bwd(dy, mlp_res1, n2w1, w1_1, w2_1, 1)
    dcat1 = matmul(dx3, w_out1, tb=True, name="wout_dx_1", riders=[sib1])
    dwo_1 = matmul(cat1, dx3, ta=True, out_dtype=BF16, name="wout_dw_1").reshape(N_DEV, OUT_IN // N_DEV, D_MODEL)
    sibo = sibling_rider([dwo_1])
    do1, dgate1, dqm1, dmqw1, dmk1, dmv1 = rows_call(
        functools.partial(vjp_rows(fox_out_fn, 3, (True, True, True)), n_row=3, n_ct=1),
        [o1, (pm1, D_MODEL, 3), (pm1, MEM_WIDTH, 8), dcat1], [mqw1, mk, mv],
        [(D_MODEL, F32), (D_MODEL, BF16), (MEM_WIDTH, BF16)], [(1, HEAD_DIM), (n_mem, MEM_WIDTH), (n_mem, MEM_WIDTH)],
        tm=256, name="fox_out_bwd", riders=[sibo])
    ride_l1 = chips_rider(chip_sums(["w_mlp2_1", "w_mlp1_1", "w_out_1"], [dw2_1, dw1_1, dwo_1],
                                    sib2.results + sib1.results + sibo.results))
    dq1, dk1, dv1, dfq, dfk = fox_attn_bwd(q1, k1, pm1, fq, fk, do1, name="fox_attn_bwd", riders=[ride_l1])
    dqraw1, dkraw1, dqnw, dknw = rows_call(
        functools.partial(vjp_rows(fox_qk_fn, 2, (True, True)), n_row=2, n_ct=2),
        [(pm1, D_MODEL, 0), (pm1, D_MODEL, 1), dq1, dk1], [qnw, knw],
        [(D_MODEL, BF16)] * 2, [(1, HEAD_DIM)] * 2, tm=256, name="fox_qk_bwd")
    dfcum = dfq + jnp.pad(dfk[:, 0, :].T, ((0, 0), (0, LANES - N_HEADS)))
    dps1, dfbias = rows_call(
        functools.partial(vjp_rows(fox_fcum_fn, 1, (True,)), n_row=1, n_ct=1),
        [ps1, dfcum], [fbias], [(LANES, F32)], [(1, LANES)], tm=t, name="fox_fcum_bwd")
    dpm1 = jnp.concatenate([dqraw1, dkraw1, dv1, dgate1, dqm1], axis=1)
    dx2, dn1w1, dwmain1, dwsmall1 = _in_proj_bwd(h1, dpm1, dps1, fox_main, fox_f, x2, dx3, n1w1, "1")
    g_fox = in_proj_pieces(dwmain1, dwsmall1, N_HEADS, FOX_IN)
    sibf = sibling_rider([g_fox])

    dx1, dw1_0, dw2_0, dn2w0, sib1, sib2 = _mlp_bwd(dx2, mlp_res0, n2w0, w1_0, w2_0, 0, riders=[sibf])
    ride_fox_g = chips_rider(chip_sums(["fox_w_in"], [g_fox], sibf.results))
    dcat0 = matmul(dx1, w_out0, tb=True, name="wout_dx_0", riders=[sib1])
    dwo_0 = matmul(cat0, dx1, ta=True, out_dtype=BF16, name="wout_dw_0").reshape(N_DEV, OUT_IN // N_DEV, D_MODEL)
    sibo = sibling_rider([dwo_0])
    do0, dz0, dqm0, donw, dmqw0, dmk0, dmv0 = rows_call(
        functools.partial(vjp_rows(dn_out_fn, 3, (True, True, True, True)), n_row=3, n_ct=1),
        [o0, (pm0, D_MODEL, 3), (pm0, MEM_WIDTH, 8), dcat0], [onw, mqw0, mk, mv],
        [((N_HEADS, HEAD_DIM), F32), (D_MODEL, BF16), (MEM_WIDTH, BF16)],
        [(1, HEAD_DIM), (1, HEAD_DIM), (n_mem, MEM_WIDTH), (n_mem, MEM_WIDTH)], tm=256, name="dn_out_bwd", riders=[sibo])
    h_l0 = chip_sums(["w_mlp2_0", "w_mlp1_0", "w_out_0"], [dw2_0, dw1_0, dwo_0], sib2.results + sib1.results + sibo.results)
    ride_l0_mlp2, ride_l0_rest = chips_rider(h_l0[:1]), chips_rider(h_l0[1:])
    dmnw, dwkv, dmknw = memkv_bwd(mem, mnw, w_kv, mknw, dmk0 + dmk1, dmv0 + dmv1, name="memkv_bwd")
    g_kv = dwkv.reshape(N_DEV, D_MODEL // N_DEV, D_MODEL)
    sibk = sibling_rider([g_kv])
    dq_s, dk_s, dg_s, du0, dw0, dqk0 = delta_seq_bwd(q0, k0, gates, u0, w0, qk0, s_start, do0, name="delta_seq_bwd",
                                                     riders=[ride_fox_g, sibk])
    ride_kv_g = chips_rider(chip_sums(["w_mem_kv"], [g_kv], sibk.results))
    dq0, dk0, dv0, dgates = delta_intra_bwd(q0, k0, v0, gates, du0, dw0, dqk0, dq_s, dk_s, dg_s,
                                            name="delta_intra_bwd", riders=[ride_l0_mlp2, ride_kv_g])
    dxq, dxk, dxv, dcq, dck, dcv = dn_prep_bwd(pm0, conv_w, dq0, dk0, dv0, name="dn_prep_bwd", riders=[ride_l0_rest])
    dconv = jnp.concatenate([dcq, dck, dcv], axis=1)
    dps0, dalog, ddtb = rows_call(
        functools.partial(vjp_rows(dn_gates_fn, 1, (True, True)), n_row=1, n_ct=1),
        [ps0, dgates], [alog, dtb], [(LANES, F32)], [(1, LANES)] * 2, tm=512, name="dn_gates_bwd")
    dpm0 = jnp.concatenate([dxq, dxk, dxv, dz0, dqm0], axis=1)
    grad_x, dn1w0, dwmain0, dwsmall0 = _in_proj_bwd(h0, dpm0, dps0, dn_main, dn_ab, x0, dx1, n1w0, "0")
    g_dn = in_proj_pieces(dwmain0, dwsmall0, 2 * N_HEADS, DN_IN)
    g_conv = dconv.reshape(CONV_WIDTH, N_DEV, -1).transpose(1, 0, 2).astype(BF16)

    g["mem_norm_w"] = dmnw[0]
    g["mem_k_norm_w"] = dmknw[0]
    g["norm1_w"] = jnp.concatenate([dn1w0, dn1w1], axis=0)
    g["dn_a_log"] = dalog[:, :N_HEADS]
    g["dn_dt_bias"] = ddtb[:, :N_HEADS]
    g["dn_o_norm_w"] = donw
    g["fox_f_bias"] = dfbias[:, :N_HEADS]
    g["fox_q_norm_w"] = dqnw
    g["fox_k_norm_w"] = dknw
    g["memq_norm_w"] = jnp.concatenate([dmqw0, dmqw1], axis=0)
    g["norm2_w"] = jnp.concatenate([dn2w0, dn2w1], axis=0)

    sibd = sibling_rider([g_dn, g_conv])
    run_riders([sibd], name="grads_to_sibling_last")
    ride_last = chips_rider(chip_sums(["dn_w_in", "dn_conv_w"], [g_dn, g_conv], sibd.results))
    ride_small = gather_rider([pack_small(g, last=loss)])
    run_riders([ride_last, ride_small], name="grads_to_chips_last")

    def layers(l0, l1):
        return jnp.stack([l0, l1], axis=1).reshape(4, -1, l0.shape[-1])

    parts = {
        "w_mlp1": layers(ride_l0_rest.results[0], ride_l1.results[1]),
        "w_mlp2": layers(ride_l0_mlp2.results[0], ride_l1.results[0]),
        "w_out": layers(ride_l0_rest.results[1], ride_l1.results[2]),
        "fox_w_in": ride_fox_g.results[0], "w_mem_kv": ride_kv_g.results[0],
        "dn_w_in": ride_last.results[0], "dn_conv_w": ride_last.results[1],
    }
    out = {n: adamw(parts[n], w[n], m[n], v[n], name=f"adamw_{n}") for n, _, _ in BIG}
    small, loss = adamw_small(ride_small.results[0], w, m, v, name="adamw_small")
    return loss, grad_x, out, small


WEIGHTS = ["mem_norm_w", "w_mem_kv", "mem_k_norm_w", "norm1_w", "dn_w_in", "dn_conv_w", "dn_a_log", "dn_dt_bias",
           "dn_o_norm_w", "fox_w_in", "fox_f_bias", "fox_q_norm_w", "fox_k_norm_w", "memq_norm_w", "w_out", "norm2_w",
           "w_mlp1", "w_mlp2"]
DN_IN = 4 * D_MODEL + 2 * N_HEADS + MEM_WIDTH
FOX_IN = 4 * D_MODEL + N_HEADS + MEM_WIDTH
GATE_END = 4 * D_MODEL
OUT_IN = D_MODEL + MEM_WIDTH
BIG = [("w_mem_kv", D_MODEL // N_DEV, D_MODEL), ("dn_w_in", D_MODEL, DN_IN // N_DEV), ("fox_w_in", D_MODEL, FOX_IN // N_DEV),
       ("dn_conv_w", CONV_WIDTH, 3 * D_MODEL // N_DEV), ("w_out", 2 * OUT_IN // N_DEV, D_MODEL),
       ("w_mlp1", 2 * D_MODEL, FF_PIECE), ("w_mlp2", 2 * FF_PIECE, D_MODEL)]
SMALL_TILE = 8 * LANES
SMALL = [(name, shape, -(-math.prod(shape) // SMALL_TILE) * SMALL_TILE) for name, shape in [
    ("mem_norm_w", (D_MODEL,)), ("mem_k_norm_w", (HEAD_DIM,)), ("norm1_w", (2, D_MODEL)), ("dn_a_log", (1, N_HEADS)),
    ("dn_dt_bias", (1, N_HEADS)), ("dn_o_norm_w", (1, HEAD_DIM)), ("fox_f_bias", (1, N_HEADS)),
    ("fox_q_norm_w", (1, HEAD_DIM)), ("fox_k_norm_w", (1, HEAD_DIM)), ("memq_norm_w", (2, HEAD_DIM)), ("norm2_w", (2, D_MODEL))]]
SMALL_ROWS = sum(ln for _, _, ln in SMALL) // LANES + 8


def pack_small(p, last=None):
    def rows(a, ln):
        a = a.reshape(-1)
        return (a if a.shape[0] == ln else jnp.pad(a, (0, ln - a.shape[0]))).reshape(-1, LANES)

    used = sum(ln for _, _, ln in SMALL) // LANES
    tail = jnp.zeros(((SMALL_ROWS - used) * LANES,), F32)
    if last is not None:
        tail = jnp.concatenate([tail[:-1], last.reshape(1)])
    return jnp.concatenate([rows(p[n], ln) for n, _, ln in SMALL] + [tail.reshape(-1, LANES)], axis=0)


def in_proj_weights(gathered, width, n_small):
    full = gathered.transpose(1, 0, 2).reshape(D_MODEL, width)
    main = jnp.concatenate([full[:, :GATE_END], full[:, GATE_END + n_small:]], axis=1)
    return main, jnp.pad(full[:, GATE_END:GATE_END + n_small], ((0, 0), (0, LANES - n_small)))


def in_proj_pieces(d_main, d_small, n_small, width):
    full = jnp.concatenate([d_main[:, :GATE_END], d_small[:, :n_small], d_main[:, GATE_END:]], axis=1)
    return full.reshape(D_MODEL, N_DEV, width // N_DEV).transpose(1, 0, 2)


def _adamw_update(g, w, m, v):
    m_new = ADAM_B1 * m + (1.0 - ADAM_B1) * g
    v_new = ADAM_B2 * v + (1.0 - ADAM_B2) * jnp.square(g)
    m_hat = m_new / (1.0 - ADAM_B1 ** ADAM_STEP)
    v_hat = v_new / (1.0 - ADAM_B2 ** ADAM_STEP)
    return -ADAM_LR * (m_hat / (jnp.sqrt(v_hat) + ADAM_EPS) + ADAM_WD * w), m_new, v_new


def adamw(parts, w, m, v, *, name):
    n, _, cols = parts.shape
    layers = w.shape[0] if w.ndim == 3 else 1
    rows = w.shape[-2]
    tile = _pick(rows, (512, 256, 128))
    steps = rows // tile

    def body(p_ref, w_ref, m_ref, v_ref, g_ref, d_ref, mo_ref, vo_ref):
        g = p_ref[0].astype(F32)
        for i in range(1, n):
            g = g + p_ref[i].astype(F32)
        g_ref[...] = g
        d_ref[...], mo_ref[...], vo_ref[...] = _adamw_update(g, w_ref[...], m_ref[...], v_ref[...])

    if w.ndim == 3:
        spec = pl.BlockSpec((None, tile, cols), lambda l, i: (l, i, 0))
    else:
        spec = pl.BlockSpec((tile, cols), lambda l, i: (i, 0))
    return pl.pallas_call(
        body, name=name, grid=(layers, steps),
        in_specs=[pl.BlockSpec((n, tile, cols), lambda l, i: (0, l * steps + i, 0)), spec, spec, spec], out_specs=[spec] * 4,
        out_shape=[jax.ShapeDtypeStruct(w.shape, F32)] * 4, compiler_params=_params(("parallel", "parallel")),
    )(parts, w, m, v)


def adamw_small(parts, w, m, v, *, name):
    def view(a):
        return a.reshape(-1, LANES) if a.size % LANES == 0 else a.reshape(1, a.size)

    k = len(SMALL)
    ins = [view(d[n]) for d in (w, m, v) for n, _, _ in SMALL]

    def body(p_ref, *refs):
        w_refs, m_refs, v_refs, outs, g_ref = refs[:k], refs[k:2 * k], refs[2 * k:3 * k], refs[3 * k:-1], refs[-1]
        g_all = p_ref[0]
        for i in range(1, N_DEV):
            g_all = g_all + p_ref[i]
        g_ref[...] = g_all
        row = 0
        for i, (_, _, ln) in enumerate(SMALL):
            r, c = w_refs[i].shape
            g = g_ref[row:row + r, 0:c]
            outs[4 * i][...] = g
            outs[4 * i + 1][...], outs[4 * i + 2][...], outs[4 * i + 3][...] = _adamw_update(
                g, w_refs[i][...], m_refs[i][...], v_refs[i][...])
            row += ln // LANES
        outs[-1][...] = g_ref[SMALL_ROWS - 1:SMALL_ROWS, LANES - 1:LANES]

    out_shape = [jax.ShapeDtypeStruct(a.shape, F32) for a in ins[:k] for _ in range(4)] + [jax.ShapeDtypeStruct((1, 1), F32)]
    res = pl.pallas_call(body, name=name, out_shape=out_shape,
                         scratch_shapes=[pltpu.VMEM((SMALL_ROWS, LANES), F32)])(parts, *ins)
    small = {n: [o.reshape(sh) for o in res[4 * i:4 * i + 4]] for i, (n, sh, _) in enumerate(SMALL)}
    return small, res[-1][0, 0]


def kernel(x, mem, mem_norm_w, w_mem_kv, mem_k_norm_w, norm1_w, dn_w_in, dn_conv_w, dn_a_log, dn_dt_bias, dn_o_norm_w, fox_w_in, fox_f_bias, fox_q_norm_w, fox_k_norm_w, memq_norm_w, w_out, norm2_w, w_mlp1, w_mlp2, loss_target, m_mem_norm_w, m_w_mem_kv, m_mem_k_norm_w, m_norm1_w, m_dn_w_in, m_dn_conv_w, m_dn_a_log, m_dn_dt_bias, m_dn_o_norm_w, m_fox_w_in, m_fox_f_bias, m_fox_q_norm_w, m_fox_k_norm_w, m_memq_norm_w, m_w_out, m_norm2_w, m_w_mlp1, m_w_mlp2, v_mem_norm_w, v_w_mem_kv, v_mem_k_norm_w, v_norm1_w, v_dn_w_in, v_dn_conv_w, v_dn_a_log, v_dn_dt_bias, v_dn_o_norm_w, v_fox_w_in, v_fox_f_bias, v_fox_q_norm_w, v_fox_k_norm_w, v_memq_norm_w, v_w_out, v_norm2_w, v_w_mlp1, v_w_mlp2):
    p = dict(mem_norm_w=mem_norm_w, w_mem_kv=w_mem_kv, mem_k_norm_w=mem_k_norm_w, norm1_w=norm1_w, dn_w_in=dn_w_in,
             dn_conv_w=dn_conv_w, dn_a_log=dn_a_log, dn_dt_bias=dn_dt_bias, dn_o_norm_w=dn_o_norm_w, fox_w_in=fox_w_in,
             fox_f_bias=fox_f_bias, fox_q_norm_w=fox_q_norm_w, fox_k_norm_w=fox_k_norm_w, memq_norm_w=memq_norm_w,
             w_out=w_out, norm2_w=norm2_w, w_mlp1=w_mlp1, w_mlp2=w_mlp2)
    pm = dict(mem_norm_w=m_mem_norm_w, w_mem_kv=m_w_mem_kv, mem_k_norm_w=m_mem_k_norm_w, norm1_w=m_norm1_w,
              dn_w_in=m_dn_w_in, dn_conv_w=m_dn_conv_w, dn_a_log=m_dn_a_log, dn_dt_bias=m_dn_dt_bias,
              dn_o_norm_w=m_dn_o_norm_w, fox_w_in=m_fox_w_in, fox_f_bias=m_fox_f_bias, fox_q_norm_w=m_fox_q_norm_w,
              fox_k_norm_w=m_fox_k_norm_w, memq_norm_w=m_memq_norm_w, w_out=m_w_out, norm2_w=m_norm2_w, w_mlp1=m_w_mlp1,
              w_mlp2=m_w_mlp2)
    pv = dict(mem_norm_w=v_mem_norm_w, w_mem_kv=v_w_mem_kv, mem_k_norm_w=v_mem_k_norm_w, norm1_w=v_norm1_w,
              dn_w_in=v_dn_w_in, dn_conv_w=v_dn_conv_w, dn_a_log=v_dn_a_log, dn_dt_bias=v_dn_dt_bias,
              dn_o_norm_w=v_dn_o_norm_w, fox_w_in=v_fox_w_in, fox_f_bias=v_fox_f_bias, fox_q_norm_w=v_fox_q_norm_w,
              fox_k_norm_w=v_fox_k_norm_w, memq_norm_w=v_memq_norm_w, w_out=v_w_out, norm2_w=v_norm2_w, w_mlp1=v_w_mlp1,
              w_mlp2=v_w_mlp2)

    loss, grad_x, results, small = local_step(x[0], mem[0], loss_target[0], p, pm, pv)
    groups = [{n: r[i] for n, r in {**small, **results}.items()} for i in range(4)]
    return (loss, grad_x[None], *[grp[n] for grp in groups for n in WEIGHTS])
```

```python
import functools
import math

import jax
import jax.numpy as jnp
from jax import lax
from jax.experimental import pallas as pl
from jax.experimental.pallas import tpu as pltpu

F32 = jnp.float32
BF16 = jnp.bfloat16
HIGHEST = lax.Precision.HIGHEST

D_MODEL = 1024
HEAD_DIM = 128
N_HEADS = 8
MEM_HEADS = 4
MEM_WIDTH = MEM_HEADS * HEAD_DIM
D_FF = 4 * D_MODEL
CONV_WIDTH = 4
CHUNK = 64
Q_BLOCK = 128
EPS = 1e-6
SCALE = HEAD_DIM ** -0.5
MAIN_WIDTH = 4 * D_MODEL + MEM_WIDTH
LANES = 128
N_DEV = 8

ADAM_LR = 0.001
ADAM_B1 = 0.9
ADAM_B2 = 0.999
ADAM_EPS = 1e-08
ADAM_WD = 0.01
ADAM_STEP = 10

VMEM_LIMIT = 56 * 2 ** 20
MESH = pl.DeviceIdType.MESH


def _bdot(a, b, dims):
    return lax.dot_general(a.astype(BF16), b.astype(BF16), (dims, ((), ())), preferred_element_type=F32)


@jax.custom_vjp
def mm(a, b):
    return _bdot(a, b, ((1,), (0,)))


@jax.custom_vjp
def mm_nt(a, b):
    return _bdot(a, b, ((1,), (1,)))


@jax.custom_vjp
def mm_tn(a, b):
    return _bdot(a, b, ((0,), (0,)))


mm.defvjp(lambda a, b: (mm(a, b), (a, b)), lambda r, g: (mm_nt(g, r[1]), mm_tn(r[0], g)))
mm_nt.defvjp(lambda a, b: (mm_nt(a, b), (a, b)), lambda r, g: (mm(g, r[1]), mm_tn(g, r[0])))
mm_tn.defvjp(lambda a, b: (mm_tn(a, b), (a, b)), lambda r, g: (mm_nt(r[1], g), mm(r[0], g)))


def hdot(a, b):
    return jnp.dot(a, b, precision=HIGHEST, preferred_element_type=F32)


def rms(x, w):
    return x * lax.rsqrt(jnp.mean(x * x, axis=-1, keepdims=True) + EPS) * w


def l2n(x):
    return x * lax.rsqrt(jnp.sum(x * x, axis=-1, keepdims=True) + EPS)


def _iota2(n, m):
    return lax.broadcasted_iota(jnp.int32, (n, m), 0), lax.broadcasted_iota(jnp.int32, (n, m), 1)


def _lower_ones(n):
    r, c = _iota2(n, n)
    return jnp.where(r >= c, 1.0, 0.0).astype(F32)


def _last_row(x):
    r = lax.broadcasted_iota(jnp.int32, x.shape, 0)
    return jnp.sum(jnp.where(r == x.shape[0] - 1, x, 0.0), axis=0, keepdims=True)


def _softmax_rows(z):
    m = lax.stop_gradient(jnp.max(z, axis=-1, keepdims=True))
    e = jnp.exp(z - m)
    return e * (1.0 / jnp.sum(e, axis=-1, keepdims=True))


_BNN = (((2,), (1,)), ((0,), (0,)))
_BNT = (((2,), (2,)), ((0,), (0,)))
_BTN = (((1,), (1,)), ((0,), (0,)))


def _bbdot(a, b, dims):
    return lax.dot_general(a.astype(BF16), b.astype(BF16), dims, preferred_element_type=F32)


@jax.custom_vjp
def bmm(a, b):
    return _bbdot(a, b, _BNN)


@jax.custom_vjp
def bmm_nt(a, b):
    return _bbdot(a, b, _BNT)


@jax.custom_vjp
def bmm_tn(a, b):
    return _bbdot(a, b, _BTN)


@jax.custom_vjp
def bmm_high(a, b):
    return lax.dot_general(a, b, _BNN, precision=lax.Precision.HIGH, preferred_element_type=F32)


bmm.defvjp(lambda a, b: (bmm(a, b), (a, b)), lambda r, g: (bmm_nt(g, r[1]), bmm_tn(r[0], g)))
bmm_nt.defvjp(lambda a, b: (bmm_nt(a, b), (a, b)), lambda r, g: (bmm(g, r[1]), bmm_tn(g, r[0])))
bmm_tn.defvjp(lambda a, b: (bmm_tn(a, b), (a, b)), lambda r, g: (bmm_nt(r[1], g), bmm(r[0], g)))
bmm_high.defvjp(lambda a, b: (bmm_high(a, b), (a, b)), lambda r, g: (bmm_nt(g, r[1]), bmm_tn(r[0], g)))

NEUMANN_HIGH_LEVELS = 2


@jax.custom_vjp
def inv_unit_lower(a):
    n = a.shape[-1]
    r, c = _iota2(n, n)
    p = jnp.where(r == c, 1.0, 0.0).astype(F32) - a
    ak = a
    for level in range(int(math.log2(n)) - 1):
        dot = bmm_high if level < NEUMANN_HIGH_LEVELS else bmm
        ak = dot(ak, ak)
        p = p + dot(p, ak)
    return p


def _inv_unit_lower_fwd(a):
    t = inv_unit_lower(a)
    return t, t


def _inv_unit_lower_bwd(t, g):
    return (-bmm_tn(t, bmm_nt(g, t)),)


inv_unit_lower.defvjp(_inv_unit_lower_fwd, _inv_unit_lower_bwd)


def delta_intra(q, k, v, gc, beta):
    b, c, _ = q.shape
    r, cc = _iota2(c, c)
    causal = r >= cc
    strict = r > cc
    gi = jnp.broadcast_to(gc, (b, c, c))
    gj = jnp.swapaxes(gi, 1, 2)
    decay = jnp.where(causal, jnp.exp(jnp.where(causal, gi - gj, 0.0)), 0.0)
    kb = k * beta
    a = jnp.where(strict, bmm_nt(kb, k) * decay, 0.0)
    t = inv_unit_lower(a)
    u = bmm(t, v * beta)
    w = bmm(t, kb * jnp.exp(gc))
    qk = jnp.where(causal, bmm_nt(q, k) * decay, 0.0)
    return u, w, qk


def delta_step(s, q, k, gc, u, w, qk):
    v_new = u - bmm(w, s)
    out = bmm(q * jnp.exp(gc), s) + bmm(qk, v_new)
    r = lax.broadcasted_iota(jnp.int32, gc.shape, 1)
    g_last = jnp.sum(jnp.where(r == gc.shape[1] - 1, gc, 0.0), axis=1, keepdims=True)
    k_dec = k * jnp.exp(g_last - gc)
    s_new = s * jnp.exp(g_last) + bmm_tn(k_dec, v_new)
    return out, s_new


def fox_probs(q, k, fq, fk, qpos0):
    s = lax.dot_general(q, k, (((1,), (1,)), ((), ())), preferred_element_type=F32)
    r, c = _iota2(s.shape[0], s.shape[1])
    return _softmax_rows(jnp.where(c <= (r + qpos0), s + (fq - fk), -jnp.inf))


def mem_head(qm, wq, mk, mv):
    p = _softmax_rows(mm_nt(rms(qm, wq) * SCALE, mk))
    return mm(p, mv)


def _heads(x, n):
    return [x[:, h * HEAD_DIM:(h + 1) * HEAD_DIM] for h in range(n)]


def memkv_fn(mem, mnw, wkv, mknw):
    kv = mm(rms(mem, mnw), wkv)
    mk = jnp.concatenate([rms(kh, mknw) for kh in _heads(kv[:, :MEM_WIDTH], MEM_HEADS)], axis=1)
    return mk, kv[:, MEM_WIDTH:]


def dn_gates_fn(ab, alog, dtb):
    g = -jnp.exp(alog) * jax.nn.softplus(ab + dtb)
    low = _lower_ones(CHUNK)
    gc = jnp.concatenate([hdot(low, g[i * CHUNK:(i + 1) * CHUNK]) for i in range(ab.shape[0] // CHUNK)], axis=0)
    lane = lax.broadcasted_iota(jnp.int32, ab.shape, 1)
    return jnp.where(lane < N_HEADS, gc, jax.nn.sigmoid(ab))


def fox_fcum_fn(fp, fbias):
    lf = jax.nn.log_sigmoid(fp + fbias)
    low = _lower_ones(LANES)
    carry = jnp.zeros((1, fp.shape[1]), F32)
    outs = []
    for i in range(fp.shape[0] // LANES):
        cs = hdot(low, lf[i * LANES:(i + 1) * LANES]) + carry
        carry = _last_row(cs)
        outs.append(cs)
    return jnp.concatenate(outs, axis=0)


def fox_qk_fn(qraw, kraw, qnw, knw):
    q = jnp.concatenate([rms(x, qnw) * SCALE for x in _heads(qraw, N_HEADS)], axis=1)
    k = jnp.concatenate([rms(x, knw) for x in _heads(kraw, N_HEADS)], axis=1)
    return q, k


def _mem_out(qm, mqw, mk, mv):
    return [mem_head(a, mqw, b, c) for a, b, c in zip(_heads(qm, MEM_HEADS), _heads(mk, MEM_HEADS), _heads(mv, MEM_HEADS))]


def dn_out_fn(o, z, qm, onw, mqw, mk, mv):
    mix = [rms(a, onw) * jax.nn.silu(b) for a, b in zip(o, _heads(z, N_HEADS))]
    return jnp.concatenate(mix + _mem_out(qm, mqw, mk, mv), axis=1)


def fox_out_fn(o, gate, qm, mqw, mk, mv):
    return jnp.concatenate([o * jax.nn.sigmoid(gate)] + _mem_out(qm, mqw, mk, mv), axis=1)


_HBM = pl.BlockSpec(memory_space=pltpu.HBM)


def _place():
    return lax.axis_index("x"), lax.axis_index("y"), lax.axis_index("c")


class Rider:
    def __init__(self, ins, out_shape, scratch, start, finish):
        self.ins, self.out_shape, self.scratch, self.start, self.finish = list(ins), list(out_shape), list(scratch), start, finish
        self.results = None


def gather_rider(xs):
    n = len(xs)

    def plan(x_refs, out_refs, sems):
        send_sems, recv_sems, local_sems = sems
        x, y, c = _place()
        me, sibling = (x, y, c), (x, y, 1 - c)
        chips = [(1 - x, y), (x, 1 - y), (1 - x, 1 - y)]

        def copy(a, k, block, to, src=None):
            px, py, pc = block
            dst = out_refs[a].at[4 * px + 2 * py + pc]
            return pltpu.make_async_remote_copy(
                src_ref=dst if src is None else src, dst_ref=dst,
                send_sem=send_sems.at[a, k], recv_sem=recv_sems.at[a, k], device_id=to, device_id_type=MESH)

        mine = [pltpu.make_async_copy(x_refs[a], out_refs[a].at[4 * x + 2 * y + c], local_sems.at[a]) for a in range(n)]
        first = [copy(a, 0, me, sibling, src=x_refs[a]) for a in range(n)]
        first += [copy(a, 1 + j, me, (*chip, c), src=x_refs[a]) for j, chip in enumerate(chips) for a in range(n)]
        return copy, me, sibling, chips, mine, first

    def start(x_refs, out_refs, sems):
        _, _, _, _, mine, first = plan(x_refs, out_refs, sems)
        for cp in mine + first:
            cp.start()

    def finish(x_refs, out_refs, sems):
        copy, me, sibling, chips, mine, first = plan(x_refs, out_refs, sems)
        _, _, c = me
        passed = []
        for j, chip in enumerate(chips):
            for a in range(n):
                copy(a, 1 + j, (*chip, c), me).wait_recv()
                passed.append(copy(a, 4 + j, (*chip, c), sibling))
                passed[-1].start()
        for a in range(n):
            copy(a, 0, sibling, me).wait_recv()
        for j, chip in enumerate(chips):
            for a in range(n):
                copy(a, 4 + j, (*chip, 1 - c), me).wait_recv()
        for cp in first + passed:
            cp.wait_send()
        for cp in mine:
            cp.wait()

    return Rider(xs, [jax.ShapeDtypeStruct((N_DEV,) + a.shape, a.dtype) for a in xs],
                 [pltpu.SemaphoreType.DMA((n, 7)), pltpu.SemaphoreType.DMA((n, 7)), pltpu.SemaphoreType.DMA((n,))], start, finish)


def sibling_rider(gs):
    n = len(gs)

    def plan(g_refs, out_refs, sems):
        send_sems, recv_sems = sems
        x, y, c = _place()
        return [pltpu.make_async_remote_copy(
            src_ref=g_refs[a].at[2 * k + 1 - c], dst_ref=out_refs[a].at[k], send_sem=send_sems.at[a, k],
            recv_sem=recv_sems.at[a, k], device_id=(x, y, 1 - c), device_id_type=MESH) for a in range(n) for k in range(4)]

    def start(g_refs, out_refs, sems):
        for cp in plan(g_refs, out_refs, sems):
            cp.start()

    def finish(g_refs, out_refs, sems):
        copies = plan(g_refs, out_refs, sems)
        for cp in copies:
            cp.wait_recv()
        for cp in copies:
            cp.wait_send()

    return Rider(gs, [jax.ShapeDtypeStruct((4,) + g.shape[1:], g.dtype) for g in gs],
                 [pltpu.SemaphoreType.DMA((n, 4)), pltpu.SemaphoreType.DMA((n, 4))], start, finish)


def chips_rider(hs):
    n = len(hs)

    def plan(h_refs, out_refs, sems):
        send_sems, recv_sems, local_sems = sems
        x, y, c = _place()
        mine = 2 * x + y
        chips = [(1 - x, y), (x, 1 - y), (1 - x, 1 - y)]
        keep = [pltpu.make_async_copy(h_refs[a].at[mine], out_refs[a].at[mine], local_sems.at[a]) for a in range(n)]
        sends = [pltpu.make_async_remote_copy(
            src_ref=h_refs[a].at[2 * qx + qy], dst_ref=out_refs[a].at[mine], send_sem=send_sems.at[a, j],
            recv_sem=recv_sems.at[a, j], device_id=(qx, qy, c), device_id_type=MESH)
            for j, (qx, qy) in enumerate(chips) for a in range(n)]
        recvs = [pltpu.make_async_remote_copy(
            src_ref=h_refs[a].at[mine], dst_ref=out_refs[a].at[2 * qx + qy], send_sem=send_sems.at[a, j],
            recv_sem=recv_sems.at[a, j], device_id=(qx, qy, c), device_id_type=MESH)
            for j, (qx, qy) in enumerate(chips) for a in range(n)]
        return keep, sends, recvs

    def start(h_refs, out_refs, sems):
        keep, sends, _ = plan(h_refs, out_refs, sems)
        for cp in keep + sends:
            cp.start()

    def finish(h_refs, out_refs, sems):
        keep, sends, recvs = plan(h_refs, out_refs, sems)
        for cp in recvs:
            cp.wait_recv()
        for cp in sends:
            cp.wait_send()
        for cp in keep:
            cp.wait()

    return Rider(hs, [jax.ShapeDtypeStruct(h.shape, h.dtype) for h in hs],
                 [pltpu.SemaphoreType.DMA((n, 3)), pltpu.SemaphoreType.DMA((n, 3)), pltpu.SemaphoreType.DMA((n,))], start, finish)


def hosted_call(riders, body, *, out_shape, in_specs, out_specs, grid=(), scratch_shapes=(), **kw):
    riders = tuple(riders or ())
    if not riders:
        return pl.pallas_call(body, out_shape=out_shape, in_specs=in_specs, out_specs=out_specs, grid=grid,
                              scratch_shapes=scratch_shapes, **kw)
    single = not isinstance(out_shape, (list, tuple))
    k_out_shape = [out_shape] if single else list(out_shape)
    k_out_specs = [out_specs] if single else list(out_specs)
    n_in, n_out, n_scr = len(in_specs), len(k_out_shape), len(scratch_shapes)
    r_ins = [a for r in riders for a in r.ins]
    r_outs = [s for r in riders for s in r.out_shape]
    r_scr = [s for r in riders for s in r.scratch]

    def full_body(*refs):
        ins = refs[:n_in + len(r_ins)]
        outs = refs[n_in + len(r_ins):n_in + len(r_ins) + n_out + len(r_outs)]
        scr = refs[n_in + len(r_ins) + n_out + len(r_outs):]
        steps = math.prod(grid)
        step = 0
        for d, g in enumerate(grid):
            step = step * g + pl.program_id(d)

        def each(method):
            i0, o0, s0 = n_in, n_out, n_scr
            for r in riders:
                getattr(r, method)(ins[i0:i0 + len(r.ins)], outs[o0:o0 + len(r.out_shape)], scr[s0:s0 + len(r.scratch)])
                i0, o0, s0 = i0 + len(r.ins), o0 + len(r.out_shape), s0 + len(r.scratch)

        if steps == 1:
            each("start")
            body(*ins[:n_in], *outs[:n_out], *scr[:n_scr])
            each("finish")
        else:
            pl.when(step == 0)(lambda: each("start"))
            body(*ins[:n_in], *outs[:n_out], *scr[:n_scr])
            pl.when(step == steps - 1)(lambda: each("finish"))

    call = pl.pallas_call(
        full_body, out_shape=k_out_shape + r_outs, in_specs=list(in_specs) + [_HBM] * len(r_ins),
        out_specs=k_out_specs + [_HBM] * len(r_outs), grid=grid, scratch_shapes=list(scratch_shapes) + r_scr, **kw)

    def run(*args):
        res = call(*args, *r_ins)
        o0 = n_out
        for r in riders:
            r.results = list(res[o0:o0 + len(r.out_shape)])
            o0 += len(r.out_shape)
        return res[0] if single else list(res[:n_out])

    return run


def run_riders(riders, *, name):
    hosted_call(riders, lambda: None, name=name, out_shape=[], in_specs=[], out_specs=[])()
    return [r.results for r in riders]


def _pick(n, cands):
    for c in cands:
        if n % c == 0:
            return c
    return n


def _params(sem):
    return pltpu.CompilerParams(dimension_semantics=sem, vmem_limit_bytes=VMEM_LIMIT)


MATMUL_VMEM_BUDGET = 40 * 2 ** 20


def _matmul_tiles(m, n, k, bytes_a, bytes_b, bytes_mn, fixed):
    fm, fn, fk = fixed if fixed is not None else (None, None, None)

    def options(given, size, cands):
        return [given] if given else ([c for c in cands if size % c == 0] or [size])

    best = None
    for tm in options(fm, m, (2048, 1024, 512, 256, 128)):
        for tn in options(fn, n, (512, 256, 128)):
            for tk in options(fk, k, (2048, 1536, 1024, 512, 256, 128)):
                if 2 * (tm * tk * bytes_a + tk * tn * bytes_b + tm * tn * bytes_mn) + tm * tn * 4 > MATMUL_VMEM_BUDGET:
                    continue
                key = ((m // tm) * (n // tn) * (k // tk), -tk)
                if best is None or key < best[0]:
                    best = (key, (tm, tn, tk))
    assert best is not None, (m, n, k, fixed)
    return best[1]


def matmul(a, b, *, name, ta=False, tb=False, post=None, post_ins=(), row_ins=(), acc=False, extra_out=None,
           out_dtype=F32, tiles=None, b_view=None, out_view=None, riders=()):
    (k, m) = a.shape if ta else a.shape[::-1]
    (kb, n) = b_view[:2] if b_view is not None else (b.shape[::-1] if tb else b.shape)
    assert k == kb, (a.shape, b.shape, ta, tb)
    bytes_mn = sum(p.dtype.itemsize for p in post_ins) + jnp.dtype(out_dtype).itemsize
    bytes_mn += jnp.dtype(extra_out[1]).itemsize if extra_out else 0
    tm, tn, tk = _matmul_tiles(m, n, k, a.dtype.itemsize, b.dtype.itemsize, bytes_mn, tiles)
    assert not acc or tn == n, (name, tn, n)
    nk = k // tk
    dims = ((0,) if ta else (1,), (1,) if tb else (0,))
    n_post, n_row = len(post_ins), len(row_ins)
    n_out = 1 + bool(extra_out) + bool(acc)

    def body(*refs):
        a_ref, b_ref = refs[:2]
        post_refs = refs[2:2 + n_post + n_row]
        o_refs, acc_ref = refs[-1 - n_out:-1], refs[-1]
        first_rows, kk = pl.program_id(0) == 0, pl.program_id(2)

        @pl.when(kk == 0)
        def _():
            acc_ref[...] = jnp.zeros_like(acc_ref)

        b_tile = b_ref[...]
        acc_ref[...] += _bdot(a_ref[...], b_tile.reshape(-1, b_tile.shape[-1]), dims)

        @pl.when(kk == nk - 1)
        def _():
            r = acc_ref[...]
            rows = [p[...] for p in post_refs[n_post:]]
            if post is not None:
                r = post(r, *[p[...] for p in post_refs[:n_post]], *rows)
            if acc:
                r, s = r
                sum_ref = o_refs[-1]

                @pl.when(first_rows)
                def _():
                    sum_ref[...] = s

                @pl.when(jnp.logical_not(first_rows))
                def _():
                    sum_ref[...] += s

            o_refs[0][...] = r.astype(out_dtype)
            if extra_out:
                o_refs[1][...] = extra_out[0](r, *rows).astype(extra_out[1])

    a_spec = pl.BlockSpec((tk, tm), lambda i, j, kk: (kk, i)) if ta else pl.BlockSpec((tm, tk), lambda i, j, kk: (i, kk))
    if b_view is not None:
        b_spec = b_view[2]
    else:
        b_spec = pl.BlockSpec((tn, tk), lambda i, j, kk: (j, kk)) if tb else pl.BlockSpec((tk, tn), lambda i, j, kk: (kk, j))
    mn_spec = pl.BlockSpec((tm, tn), lambda i, j, kk: (i, j))
    row_spec = pl.BlockSpec((1, tn), lambda i, j, kk: (0, j))
    o_shape, o_spec = ((m, n), mn_spec) if out_view is None else out_view
    out_shape = [jax.ShapeDtypeStruct(o_shape, out_dtype)]
    out_specs = [o_spec]
    if extra_out:
        out_shape.append(jax.ShapeDtypeStruct((m, n), extra_out[1]))
        out_specs.append(mn_spec)
    if acc:
        out_shape.append(jax.ShapeDtypeStruct((1, n), F32))
        out_specs.append(row_spec)
    res = hosted_call(
        riders, body, name=name, grid=(m // tm, n // tn, nk),
        in_specs=[a_spec, b_spec] + [mn_spec] * n_post + [row_spec] * n_row, out_specs=out_specs, out_shape=out_shape,
        scratch_shapes=[pltpu.VMEM((tm, tn), F32)],
        compiler_params=_params(("arbitrary" if acc else "parallel", "parallel", "arbitrary")),
    )(a, b, *post_ins, *row_ins)
    return res if n_out > 1 else res[0]


def rows_call(fn, row_ins, full_ins, row_outs, acc_outs, *, tm, name, riders=()):
    row_ins = [r if isinstance(r, tuple) else (r, r.shape[-1], 0) for r in row_ins]
    t = row_ins[0][0].shape[-2]
    tm = min(tm, t)
    n_in = len(row_ins) + len(full_ins)
    n_row = len(row_outs)

    def body(*refs):
        res = fn(*[[r[h] for h in range(r.shape[0])] if (i < len(row_ins) and len(r.shape) == 3) else r[...]
                   for i, r in enumerate(refs[:n_in])])
        res = res if isinstance(res, (tuple, list)) else (res,)
        outs = refs[n_in:]
        for ref, val in zip(outs[:n_row], res[:n_row]):
            if len(ref.shape) == 3:
                for h, vh in enumerate(val):
                    ref[h] = vh.astype(ref.dtype)
            else:
                ref[...] = val.astype(ref.dtype)
        first = pl.program_id(0) == 0
        for ref, val in zip(outs[n_row:], res[n_row:]):
            @pl.when(first)
            def _(ref=ref, val=val):
                ref[...] = val

            @pl.when(jnp.logical_not(first))
            def _(ref=ref, val=val):
                ref[...] += val

    def full_spec(shape):
        return pl.BlockSpec(shape, lambda i, nd=len(shape): (0,) * nd)

    def row_spec(lead, w, cb):
        if lead is None:
            return pl.BlockSpec((tm, w), lambda i: (i, cb))
        return pl.BlockSpec((lead, tm, w), lambda i: (0, i, cb))

    def lead_cols(c):
        return c if isinstance(c, tuple) else (None, c)

    in_specs = [row_spec(a.shape[0] if a.ndim == 3 else None, w, cb) for (a, w, cb) in row_ins]
    in_specs += [full_spec(f.shape) for f in full_ins]
    out_specs = [row_spec(*lead_cols(c), 0) for c, _ in row_outs] + [full_spec(s) for s in acc_outs]
    out_shape = [jax.ShapeDtypeStruct(tuple(d for d in (lead_cols(c)[0], t, lead_cols(c)[1]) if d is not None), dt)
                 for c, dt in row_outs] + [jax.ShapeDtypeStruct(s, F32) for s in acc_outs]
    res = hosted_call(
        riders, body, name=name, grid=(t // tm,), in_specs=in_specs, out_specs=out_specs, out_shape=out_shape,
        compiler_params=_params(("arbitrary",)),
    )(*[r[0] for r in row_ins], *full_ins)
    return res


def vjp_rows(fn, n_diff_row, row_diff_full):
    def bwd(*args, n_row, n_ct):
        prim_rows = args[:n_row]
        cts = args[n_row:n_row + n_ct]
        fulls = args[n_row + n_ct:]
        _, vjp = jax.vjp(fn, *prim_rows, *fulls)
        g = vjp(cts[0] if n_ct == 1 else tuple(cts))
        out = list(g[:n_diff_row])
        out += [gf for gf, d in zip(g[n_row:], row_diff_full) if d]
        return tuple(out)
    return bwd


def _shift_down(x, s):
    if s == 0:
        return x
    t = lax.broadcasted_iota(jnp.int32, x.shape, 0)
    return jnp.where(t >= s, pltpu.roll(x, s, 0), 0.0)


def _shift_up(x, s):
    if s == 0:
        return x
    n = x.shape[0]
    t = lax.broadcasted_iota(jnp.int32, x.shape, 0)
    return jnp.where(t < n - s, pltpu.roll(x, n - s, 0), 0.0)


def _conv(x, w_ref):
    return sum(w_ref[pl.ds(j, 1), :] * _shift_down(x, CONV_WIDTH - 1 - j) for j in range(CONV_WIDTH))


_DN_POST = (lambda c: l2n(jax.nn.silu(c)) * SCALE, lambda c: l2n(jax.nn.silu(c)), jax.nn.silu)


def dn_prep_fwd(proj, conv_w, *, name, riders=()):
    t = proj.shape[0]

    def body(xq, xk, xv, wq, wk, wv, oq, ok, ov):
        for x_ref, w_ref, o_ref, post in zip((xq, xk, xv), (wq, wk, wv), (oq, ok, ov), _DN_POST):
            o_ref[...] = post(_conv(x_ref[...], w_ref))

    x_specs = [pl.BlockSpec((t, HEAD_DIM), lambda h, g=g: (0, g * N_HEADS + h)) for g in range(3)]
    w_specs = [pl.BlockSpec((CONV_WIDTH, HEAD_DIM), lambda h, g=g: (0, g * N_HEADS + h)) for g in range(3)]
    o_spec = pl.BlockSpec((None, t, HEAD_DIM), lambda h: (h, 0, 0))
    return hosted_call(
        riders, body, name=name, grid=(N_HEADS,), in_specs=x_specs + w_specs, out_specs=[o_spec] * 3,
        out_shape=[jax.ShapeDtypeStruct((N_HEADS, t, HEAD_DIM), F32)] * 3, compiler_params=_params(("parallel",)),
    )(proj, proj, proj, conv_w, conv_w, conv_w)


def dn_prep_bwd(proj, conv_w, dq, dk, dv, *, name, riders=()):
    t = proj.shape[0]

    def body(xq, xk, xv, wq, wk, wv, gq, gk, gv, dxq, dxk, dxv, dwq, dwk, dwv):
        for x_ref, w_ref, g_ref, dx_ref, dw_ref, post in zip(
                (xq, xk, xv), (wq, wk, wv), (gq, gk, gv), (dxq, dxk, dxv), (dwq, dwk, dwv), _DN_POST):
            x = x_ref[...]
            _, vjp = jax.vjp(post, _conv(x, w_ref))
            dc, = vjp(g_ref[...])
            dx = sum(w_ref[pl.ds(j, 1), :] * _shift_up(dc, CONV_WIDTH - 1 - j) for j in range(CONV_WIDTH))
            dx_ref[...] = dx.astype(dx_ref.dtype)
            for j in range(CONV_WIDTH):
                dw_ref[pl.ds(j, 1), :] = jnp.sum(dc * _shift_down(x, CONV_WIDTH - 1 - j), axis=0, keepdims=True)

    x_specs = [pl.BlockSpec((t, HEAD_DIM), lambda h, g=g: (0, g * N_HEADS + h)) for g in range(3)]
    w_specs = [pl.BlockSpec((CONV_WIDTH, HEAD_DIM), lambda h, g=g: (0, g * N_HEADS + h)) for g in range(3)]
    g_spec = pl.BlockSpec((None, t, HEAD_DIM), lambda h: (h, 0, 0))
    dx_spec = pl.BlockSpec((t, HEAD_DIM), lambda h: (0, h))
    dw_spec = pl.BlockSpec((CONV_WIDTH, HEAD_DIM), lambda h: (0, h))
    return hosted_call(
        riders, body, name=name, grid=(N_HEADS,), in_specs=x_specs + w_specs + [g_spec] * 3, out_specs=[dx_spec] * 3 + [dw_spec] * 3,
        out_shape=[jax.ShapeDtypeStruct((t, D_MODEL), BF16)] * 3 + [jax.ShapeDtypeStruct((CONV_WIDTH, D_MODEL), F32)] * 3,
        compiler_params=_params(("parallel",)),
    )(proj, proj, proj, conv_w, conv_w, conv_w, dq, dk, dv)


INTRA_CHUNKS = 4


def _lane_column(x, lane_index):
    lane = lax.broadcasted_iota(jnp.int32, x.shape, 1)
    return jnp.sum(jnp.where(lane == lane_index, x, 0.0), axis=1, keepdims=True)


def _head_columns(g, first_lane):
    return jnp.concatenate([_lane_column(g, first_lane + h)[None] for h in range(N_HEADS)], axis=0)


def _intra_of_gates(q, k, v, gates):
    nb = N_HEADS * (gates.shape[0] // CHUNK)

    def chunks(x):
        return x.reshape(nb, CHUNK, x.shape[-1])

    res = delta_intra(chunks(q), chunks(k), chunks(v), chunks(_head_columns(gates, 0)), chunks(_head_columns(gates, N_HEADS)))
    return tuple(x.reshape(N_HEADS, -1, x.shape[-1]) for x in res)


def _step_of_gates(s, q, k, gates, u, w, qk):
    return delta_step(s, q, k, _head_columns(gates, 0), u, w, qk)


def _head_major(rows, w, index):
    return pl.BlockSpec((N_HEADS, rows, w), lambda i: (0, index(i), 0))


def delta_intra_fwd(q, k, v, gates, *, name, riders=()):
    t = q.shape[1]
    rows = min(INTRA_CHUNKS, t // CHUNK) * CHUNK

    def body(q_ref, k_ref, v_ref, g_ref, u_ref, w_ref, qk_ref):
        for ref, val in zip((u_ref, w_ref, qk_ref), _intra_of_gates(q_ref[...], k_ref[...], v_ref[...], g_ref[...])):
            ref[...] = val

    x_spec, qk_spec = (_head_major(rows, w, lambda i: i) for w in (HEAD_DIM, CHUNK))
    g_spec = pl.BlockSpec((rows, LANES), lambda i: (i, 0))
    return hosted_call(
        riders, body, name=name, grid=(t // rows,), in_specs=[x_spec] * 3 + [g_spec], out_specs=[x_spec, x_spec, qk_spec],
        out_shape=[jax.ShapeDtypeStruct((N_HEADS, t, HEAD_DIM), F32)] * 2 + [jax.ShapeDtypeStruct((N_HEADS, t, CHUNK), F32)],
        compiler_params=_params(("parallel",)),
    )(q, k, v, gates)


def delta_seq_fwd(q, k, gates, u, w, qk, *, name, riders=()):
    t = q.shape[1]
    nc = t // CHUNK

    def body(q_ref, k_ref, g_ref, u_ref, w_ref, qk_ref, o_ref, s0_ref, s_ref):
        @pl.when(pl.program_id(0) == 0)
        def _():
            s_ref[...] = jnp.zeros_like(s_ref)

        s = s_ref[...]
        s0_ref[...] = s
        o, s_new = _step_of_gates(s, q_ref[...], k_ref[...], g_ref[...], u_ref[...], w_ref[...], qk_ref[...])
        o_ref[...] = o
        s_ref[...] = s_new

    x_spec, qk_spec = (_head_major(CHUNK, w, lambda c: c) for w in (HEAD_DIM, CHUNK))
    g_spec = pl.BlockSpec((CHUNK, LANES), lambda c: (c, 0))
    s_spec = pl.BlockSpec((N_HEADS, None, HEAD_DIM, HEAD_DIM), lambda c: (0, c, 0, 0))
    return hosted_call(
        riders, body, name=name, grid=(nc,), in_specs=[x_spec, x_spec, g_spec, x_spec, x_spec, qk_spec], out_specs=[x_spec, s_spec],
        out_shape=[jax.ShapeDtypeStruct((N_HEADS, t, HEAD_DIM), F32),
                   jax.ShapeDtypeStruct((N_HEADS, nc, HEAD_DIM, HEAD_DIM), F32)],
        scratch_shapes=[pltpu.VMEM((N_HEADS, HEAD_DIM, HEAD_DIM), F32)],
        compiler_params=_params(("arbitrary",)),
    )(q, k, gates, u, w, qk)


def delta_seq_bwd(q, k, gates, u, w, qk, s0, do, *, name, riders=()):
    t = q.shape[1]
    nc = t // CHUNK

    def body(q_ref, k_ref, g_ref, u_ref, w_ref, qk_ref, s0_ref, do_ref,
             dq_ref, dk_ref, dg_ref, du_ref, dw_ref, dqk_ref, ds_ref):
        @pl.when(pl.program_id(0) == 0)
        def _():
            ds_ref[...] = jnp.zeros_like(ds_ref)

        _, vjp = jax.vjp(_step_of_gates, s0_ref[...], q_ref[...], k_ref[...], g_ref[...], u_ref[...], w_ref[...], qk_ref[...])
        ds, dq, dk, dg, du, dw, dqk = vjp((do_ref[...], ds_ref[...]))
        for ref, val in zip((ds_ref, dq_ref, dk_ref, dg_ref, du_ref, dw_ref, dqk_ref), (ds, dq, dk, dg, du, dw, dqk)):
            ref[...] = val

    x_spec, qk_spec = (_head_major(CHUNK, w, lambda c: nc - 1 - c) for w in (HEAD_DIM, CHUNK))
    g_spec = pl.BlockSpec((CHUNK, LANES), lambda c: (nc - 1 - c, 0))
    s_spec = pl.BlockSpec((N_HEADS, None, HEAD_DIM, HEAD_DIM), lambda c: (0, nc - 1 - c, 0, 0))
    head_shape = [jax.ShapeDtypeStruct((N_HEADS, t, w_), F32) for w_ in (HEAD_DIM, HEAD_DIM, HEAD_DIM, HEAD_DIM, CHUNK)]
    return hosted_call(
        riders, body, name=name, grid=(nc,), in_specs=[x_spec, x_spec, g_spec, x_spec, x_spec, qk_spec, s_spec, x_spec],
        out_specs=[x_spec, x_spec, g_spec, x_spec, x_spec, qk_spec],
        out_shape=head_shape[:2] + [jax.ShapeDtypeStruct((t, LANES), F32)] + head_shape[2:],
        scratch_shapes=[pltpu.VMEM((N_HEADS, HEAD_DIM, HEAD_DIM), F32)],
        compiler_params=_params(("arbitrary",)),
    )(q, k, gates, u, w, qk, s0, do)


def delta_intra_bwd(q, k, v, gates, du, dw, dqk, dq_s, dk_s, dg_s, *, name, riders=()):
    t = q.shape[1]
    rows = min(INTRA_CHUNKS, t // CHUNK) * CHUNK

    def body(q_ref, k_ref, v_ref, g_ref, du_ref, dw_ref, dqk_ref, dqs_ref, dks_ref, dgs_ref, dq_ref, dk_ref, dv_ref, dg_ref):
        _, vjp = jax.vjp(_intra_of_gates, q_ref[...], k_ref[...], v_ref[...], g_ref[...])
        dq, dk, dv, dg = vjp((du_ref[...], dw_ref[...], dqk_ref[...]))
        dq_ref[...] = dq + dqs_ref[...]
        dk_ref[...] = dk + dks_ref[...]
        dv_ref[...] = dv
        dg_ref[...] = dg + dgs_ref[...]

    x_spec, qk_spec = (_head_major(rows, w, lambda i: i) for w in (HEAD_DIM, CHUNK))
    g_spec = pl.BlockSpec((rows, LANES), lambda i: (i, 0))
    return hosted_call(
        riders, body, name=name, grid=(t // rows,),
        in_specs=[x_spec] * 3 + [g_spec, x_spec, x_spec, qk_spec, x_spec, x_spec, g_spec],
        out_specs=[x_spec] * 3 + [g_spec],
        out_shape=[jax.ShapeDtypeStruct((N_HEADS, t, HEAD_DIM), F32)] * 3 + [jax.ShapeDtypeStruct((t, LANES), F32)],
        compiler_params=_params(("parallel",)),
    )(q, k, v, gates, du, dw, dqk, dq_s, dk_s, dg_s)


_V_BLOCK = 2 * N_HEADS
FOX_GROUPS = 16


def _fox_groups(t):
    nq = t // Q_BLOCK
    per = max(1, nq // FOX_GROUPS)
    return [(g0, per, (g0 + per) * Q_BLOCK) for g0 in range(0, nq, per)]


def fox_attn_fwd(q, k, proj, fq, fk, *, name, riders=()):
    t = q.shape[0]

    def body(q_ref, k_ref, v_ref, fq_ref, fk_ref, o_ref, kb_ref, vb_ref):
        head = pl.program_id(0)
        kb_ref[...] = k_ref[...].astype(BF16)
        vb_ref[...] = v_ref[...].astype(BF16)
        for g0, per, keys in _fox_groups(t):
            def block(j, carry, g0=g0, keys=keys):
                rows = pl.ds((g0 + j) * Q_BLOCK, Q_BLOCK)
                p = fox_probs(q_ref[rows, :].astype(BF16), kb_ref[0:keys, :], _lane_column(fq_ref[rows, :], head),
                              fk_ref[:, 0:keys], (g0 + j) * Q_BLOCK)
                o_ref[rows, :] = jnp.dot(p.astype(BF16), vb_ref[0:keys, :], preferred_element_type=F32)
                return carry
            for j in range(per):
                block(j, 0)

    x_spec = pl.BlockSpec((t, HEAD_DIM), lambda h: (0, h))
    v_spec = pl.BlockSpec((t, HEAD_DIM), lambda h: (0, _V_BLOCK + h))
    fq_spec = pl.BlockSpec((t, LANES), lambda h: (0, 0))
    fk_spec = pl.BlockSpec((None, 1, t), lambda h: (h, 0, 0))
    return hosted_call(
        riders, body, name=name, grid=(N_HEADS,), in_specs=[x_spec, x_spec, v_spec, fq_spec, fk_spec], out_specs=x_spec,
        out_shape=jax.ShapeDtypeStruct((t, D_MODEL), F32), scratch_shapes=[pltpu.VMEM((t, HEAD_DIM), BF16)] * 2,
        compiler_params=_params(("parallel",)),
    )(q, k, proj, fq, fk)


def fox_attn_bwd(q, k, proj, fq, fk, do, *, name, riders=()):
    t = q.shape[0]

    def body(q_ref, k_ref, v_ref, fq_ref, fk_ref, do_ref, dq_ref, dk_ref, dv_out_ref, dfq_ref, dfk_ref, kb_ref, vb_ref, dv_ref):
        head = pl.program_id(0)

        @pl.when(head == 0)
        def _():
            dfq_ref[...] = jnp.zeros_like(dfq_ref)

        kb_ref[...] = k_ref[...].astype(BF16)
        vb_ref[...] = v_ref[...].astype(BF16)
        dk_ref[...] = jnp.zeros_like(dk_ref)
        dv_ref[...] = jnp.zeros_like(dv_ref)
        dfk_ref[...] = jnp.zeros_like(dfk_ref)
        nt = (((1,), (1,)), ((), ()))
        tn = (((0,), (0,)), ((), ()))
        for g0, per, keys in _fox_groups(t):
            def block(j, carry, g0=g0, keys=keys):
                rows = pl.ds((g0 + j) * Q_BLOCK, Q_BLOCK)
                qb, dob = q_ref[rows, :].astype(BF16), do_ref[rows, :].astype(BF16)
                kb, vb = kb_ref[0:keys, :], vb_ref[0:keys, :]
                p = fox_probs(qb, kb, _lane_column(fq_ref[rows, :], head), fk_ref[:, 0:keys], (g0 + j) * Q_BLOCK)
                dp = lax.dot_general(dob, vb, nt, preferred_element_type=F32)
                dz = p * (dp - jnp.sum(dp * p, axis=-1, keepdims=True))
                pb, dzb = p.astype(BF16), dz.astype(BF16)
                dq_ref[rows, :] = jnp.dot(dzb, kb, preferred_element_type=F32)
                lane = lax.broadcasted_iota(jnp.int32, (Q_BLOCK, LANES), 1)
                dfq_ref[rows, :] += jnp.where(lane == head, jnp.sum(dz, axis=-1, keepdims=True), 0.0)
                dk_ref[0:keys, :] += lax.dot_general(dzb, qb, tn, preferred_element_type=F32)
                dv_ref[0:keys, :] += lax.dot_general(pb, dob, tn, preferred_element_type=F32)
                dfk_ref[:, 0:keys] -= jnp.sum(dz, axis=0, keepdims=True)
                return carry
            for j in range(per):
                block(j, 0)
        dv_out_ref[...] = dv_ref[...].astype(dv_out_ref.dtype)

    x_spec = pl.BlockSpec((t, HEAD_DIM), lambda h: (0, h))
    v_spec = pl.BlockSpec((t, HEAD_DIM), lambda h: (0, _V_BLOCK + h))
    fq_spec = pl.BlockSpec((t, LANES), lambda h: (0, 0))
    fk_spec = pl.BlockSpec((None, 1, t), lambda h: (h, 0, 0))
    return hosted_call(
        riders, body, name=name, grid=(N_HEADS,), in_specs=[x_spec, x_spec, v_spec, fq_spec, fk_spec, x_spec],
        out_specs=[x_spec, x_spec, x_spec, fq_spec, fk_spec],
        out_shape=[jax.ShapeDtypeStruct((t, D_MODEL), F32)] * 2 + [jax.ShapeDtypeStruct((t, D_MODEL), BF16)]
        + [jax.ShapeDtypeStruct((t, LANES), F32), jax.ShapeDtypeStruct((N_HEADS, 1, t), F32)],
        scratch_shapes=[pltpu.VMEM((t, HEAD_DIM), BF16)] * 2 + [pltpu.VMEM((t, HEAD_DIM), F32)],
        compiler_params=_params(("arbitrary",)),
    )(q, k, proj, fq, fk, do)


def memkv_fwd(mem, mnw, wkv, mknw, *, name):
    n = mem.shape[0]

    def body(mem_ref, mnw_ref, w_ref, mknw_ref, mk_ref, mv_ref):
        mk, mv = memkv_fn(mem_ref[...], mnw_ref[...], w_ref[...], mknw_ref[...])
        mk_ref[...] = mk
        mv_ref[...] = mv

    return pl.pallas_call(
        body, name=name, out_shape=[jax.ShapeDtypeStruct((n, MEM_WIDTH), F32)] * 2,
        compiler_params=pltpu.CompilerParams(vmem_limit_bytes=VMEM_LIMIT),
    )(mem, mnw, wkv, mknw)


def memkv_bwd(mem, mnw, wkv, mknw, dmk, dmv, *, name):
    def body(mem_ref, mnw_ref, w_ref, mknw_ref, dmk_ref, dmv_ref, dmnw_ref, dw_ref, dmknw_ref):
        f = functools.partial(memkv_fn, mem_ref[...])
        _, vjp = jax.vjp(f, mnw_ref[...], w_ref[...].astype(F32), mknw_ref[...])
        dmnw, dw, dmknw = vjp((dmk_ref[...], dmv_ref[...]))
        dmnw_ref[...] = dmnw
        dw_ref[...] = dw.astype(dw_ref.dtype)
        dmknw_ref[...] = dmknw

    return pl.pallas_call(
        body, name=name,
        out_shape=[jax.ShapeDtypeStruct(mnw.shape, F32), jax.ShapeDtypeStruct(wkv.shape, BF16), jax.ShapeDtypeStruct(mknw.shape, F32)],
        compiler_params=pltpu.CompilerParams(vmem_limit_bytes=VMEM_LIMIT),
    )(mem, mnw, wkv, mknw, dmk, dmv)


def _row(v, width=None):
    v = v.reshape(1, -1)
    if width is not None and v.shape[1] < width:
        v = jnp.pad(v, ((0, 0), (0, width - v.shape[1])))
    return v


def _norm_fwd(x, w, name, riders=()):
    return rows_call(lambda x, w: rms(x, w), [x], [w], [(D_MODEL, BF16)], [], tm=512, name=name, riders=riders)[0]


FF_PIECE = D_FF // N_DEV


def _add(r, x, *rows):
    return r + x


def _norm_rows(r, w):
    return rms(r, w)


def _norm_bwd_post(dh, x, dx_in, w):
    _, vjp = jax.vjp(rms, x, w)
    dx, dw = vjp(dh)
    return dx + dx_in, dw


def _piece(rows, cols, index):
    return pl.BlockSpec((None, rows, cols), lambda i, j, kk: (index(i, j, kk), 0, 0))


def _two_pieces(rows, cols, index):
    return pl.BlockSpec((2, rows, cols), lambda i, j, kk: (index(i, j, kk), 0, 0))


def _loss_post(r, x, tgt):
    e = r + x - tgt
    return e * (1.0 / D_MODEL), jnp.sum(e * e, axis=0, keepdims=True)


def _mlp_fwd(x, h2, w1, w2, layer, riders=(), next_norm_w=None, loss_target=None):
    riders = list(riders) + [None, None]
    u, a1 = matmul(h2, w1, name=f"mlp1_fwd_{layer}", tiles=(None, FF_PIECE, D_MODEL),
                   extra_out=(lambda u: jnp.square(jnp.maximum(u, 0.0)), BF16),
                   b_view=(D_MODEL, D_FF, _piece(D_MODEL, FF_PIECE, lambda i, j, kk: j)), riders=riders[0])
    if loss_target is not None:
        tail = dict(post=_loss_post, post_ins=[x, loss_target], acc=True)
    elif next_norm_w is not None:
        tail = dict(post=_add, post_ins=[x], row_ins=[next_norm_w], extra_out=(_norm_rows, BF16))
    else:
        tail = dict(post=_add, post_ins=[x])
    y = matmul(a1, w2, name=f"mlp2_fwd_{layer}", tiles=(None, D_MODEL, 2 * FF_PIECE),
               b_view=(D_FF, D_MODEL, _two_pieces(FF_PIECE, D_MODEL, lambda i, j, kk: kk)), riders=riders[1], **tail)
    return y, (x, h2, u, a1)


def pair_sum(g, got, *, name):
    _, rows, cols = g.shape
    tile = _pick(rows, (512, 256, 128))
    c = lax.axis_index("c").astype(jnp.int32).reshape(1)

    def body(c_ref, a_ref, b_ref, o_ref):
        o_ref[...] = (a_ref[...].astype(F32) + b_ref[...].astype(F32)).astype(o_ref.dtype)

    grid_spec = pltpu.PrefetchScalarGridSpec(
        num_scalar_prefetch=1, grid=(4, rows // tile),
        in_specs=[pl.BlockSpec((None, tile, cols), lambda k, i, c_ref: (2 * k + c_ref[0], i, 0)),
                  pl.BlockSpec((None, tile, cols), lambda k, i, c_ref: (k, i, 0))],
        out_specs=pl.BlockSpec((None, tile, cols), lambda k, i, c_ref: (k, i, 0)))
    return pl.pallas_call(
        body, name=name, grid_spec=grid_spec, out_shape=jax.ShapeDtypeStruct((4, rows, cols), g.dtype),
        compiler_params=_params(("parallel", "parallel")),
    )(c, g, got)


def chip_sums(names, pieces, gots):
    return [pair_sum(a, got, name=f"grads_pair_sum_{n}") for n, a, got in zip(names, pieces, gots)]


def _mlp_bwd(dy, res, n2w, w1, w2, layer, riders=()):
    x, h2, u, a1 = res
    du = matmul(dy, w2, tb=True, name=f"mlp2_dx_{layer}", out_dtype=BF16, tiles=(None, 2 * FF_PIECE, D_MODEL),
                post=lambda r, u: r * (2.0 * jnp.maximum(u, 0.0)), post_ins=[u],
                b_view=(D_MODEL, D_FF, _two_pieces(FF_PIECE, D_MODEL, lambda i, j, kk: j)), riders=riders)
    dw2 = matmul(a1, dy, ta=True, name=f"mlp2_dw_{layer}", out_dtype=BF16, tiles=(FF_PIECE, D_MODEL, None), out_view=(
        w2.shape, _piece(FF_PIECE, D_MODEL, lambda i, j, kk: i)))
    sib2 = sibling_rider([dw2])
    dx, dn2w = matmul(du, w1, tb=True, name=f"mlp1_dx_{layer}", tiles=(None, D_MODEL, FF_PIECE),
                      b_view=(D_FF, D_MODEL, _piece(D_MODEL, FF_PIECE, lambda i, j, kk: kk)),
                      post=_norm_bwd_post, post_ins=[x, dy], row_ins=[n2w], acc=True, riders=[sib2])
    dw1 = matmul(h2, du, ta=True, name=f"mlp1_dw_{layer}", out_dtype=BF16, tiles=(D_MODEL, FF_PIECE, None), out_view=(
        w1.shape, _piece(D_MODEL, FF_PIECE, lambda i, j, kk: j)))
    return dx, dw1, dw2, dn2w, sibling_rider([dw1]), sib2


def _in_proj_dx(dmain, dsmall, w_main, w_small, x, dx_in, n1w, tag, riders=()):
    dh = matmul(dmain, w_main, tb=True, name=f"inproj_dx_main_{tag}", riders=riders)

    def post(r, dh_main, x, dx_in, w):
        return _norm_bwd_post(r + dh_main, x, dx_in, w)

    return matmul(dsmall, w_small, tb=True, name=f"inproj_dx_small_{tag}", tiles=(None, D_MODEL, None),
                  post=post, post_ins=[dh, x, dx_in], row_ins=[n1w], acc=True)


def _in_proj_dw(h, dmain, dsmall, tag):
    dw_main = matmul(h, dmain, ta=True, out_dtype=BF16, name=f"inproj_dw_main_{tag}")
    dw_small = matmul(h, dsmall, ta=True, out_dtype=BF16, name=f"inproj_dw_small_{tag}")
    return dw_main, dw_small


def local_step(x, mem, target, w, m, v):
    t = x.shape[0]
    n_mem = mem.shape[0]
    g = {}

    def wire(a):
        return a.astype(BF16)

    ride_first = gather_rider([wire(w["dn_w_in"][0])])
    fox_w = wire(w["fox_w_in"][0])
    ride_out = gather_rider([wire(w["w_out"][0]), w["dn_conv_w"][0]])
    ride_out_1 = gather_rider([wire(w["w_out"][1])])
    ride_kv = gather_rider([wire(w["w_mem_kv"])])
    ride_mlp1_0 = gather_rider([wire(w["w_mlp1"][0])])
    ride_mlp2_0 = gather_rider([wire(w["w_mlp2"][0])])
    ride_fox_a, ride_fox_b = gather_rider([fox_w[:D_MODEL // 2]]), gather_rider([fox_w[D_MODEL // 2:]])
    ride_mlp_1 = gather_rider([wire(w["w_mlp1"][1]), wire(w["w_mlp2"][1])])
    mnw, mknw = _row(w["mem_norm_w"]), _row(w["mem_k_norm_w"])

    n1w0, n2w0 = _row(w["norm1_w"][0]), _row(w["norm2_w"][0])
    n1w1, n2w1 = _row(w["norm1_w"][1]), _row(w["norm2_w"][1])
    alog, dtb = _row(w["dn_a_log"][0], LANES), _row(w["dn_dt_bias"][0], LANES)
    onw, mqw0 = _row(w["dn_o_norm_w"][0]), _row(w["memq_norm_w"][0])
    x0 = x
    h0 = _norm_fwd(x0, n1w0, "norm1_fwd_0", riders=[ride_first])
    dn_main, dn_ab = in_proj_weights(ride_first.results[0], DN_IN, 2 * N_HEADS)
    pm0 = matmul(h0, dn_main, name="inproj_main_0", riders=[ride_out])
    w_out0 = ride_out.results[0].reshape(OUT_IN, D_MODEL)
    conv_w = ride_out.results[1].transpose(1, 0, 2).reshape(CONV_WIDTH, 3 * D_MODEL)
    ps0 = matmul(h0, dn_ab, name="inproj_small_0")
    gates = rows_call(dn_gates_fn, [ps0], [alog, dtb], [(LANES, F32)], [], tm=512, name="dn_gates_fwd")[0]
    q0, k0, v0 = dn_prep_fwd(pm0, conv_w, name="dn_prep_fwd", riders=[ride_kv])
    w_kv = ride_kv.results[0].reshape(D_MODEL, D_MODEL)
    mk, mv = memkv_fwd(mem, mnw, w_kv, mknw, name="memkv_fwd")
    u0, w0, qk0 = delta_intra_fwd(q0, k0, v0, gates, name="delta_intra_fwd", riders=[ride_mlp1_0])
    o0, s_start = delta_seq_fwd(q0, k0, gates, u0, w0, qk0, name="delta_seq_fwd", riders=[ride_mlp2_0])
    cat0 = rows_call(dn_out_fn, [o0, (pm0, D_MODEL, 3), (pm0, MEM_WIDTH, 8)], [onw, mqw0, mk, mv],
                     [(D_MODEL + MEM_WIDTH, BF16)], [], tm=256, name="dn_out_fwd")[0]
    (w1_0,), (w2_0,) = ride_mlp1_0.results, ride_mlp2_0.results
    x1, h2_0 = matmul(cat0, w_out0, post=_add, post_ins=[x0], row_ins=[n2w0], extra_out=(_norm_rows, BF16),
                      tiles=(None, D_MODEL, None), name="wout_fwd_0")
    (x2, h1), mlp_res0 = _mlp_fwd(x1, h2_0, w1_0, w2_0, 0, riders=[[ride_fox_a], [ride_fox_b]], next_norm_w=n1w1)
    fox_main, fox_f = in_proj_weights(
        jnp.concatenate([ride_fox_a.results[0], ride_fox_b.results[0]], axis=1), FOX_IN, N_HEADS)

    fbias = _row(w["fox_f_bias"][0], LANES)
    qnw, knw, mqw1 = _row(w["fox_q_norm_w"][0]), _row(w["fox_k_norm_w"][0]), _row(w["memq_norm_w"][1])
    pm1 = matmul(h1, fox_main, name="inproj_main_1", riders=[ride_out_1])
    w_out1 = ride_out_1.results[0].reshape(OUT_IN, D_MODEL)
    ps1 = matmul(h1, fox_f, name="inproj_small_1")
    fq = rows_call(fox_fcum_fn, [ps1], [fbias], [(LANES, F32)], [], tm=t, name="fox_fcum_fwd")[0]
    fk = fq[:, :N_HEADS].T[:, None, :]
    q1, k1 = rows_call(fox_qk_fn, [(pm1, D_MODEL, 0), (pm1, D_MODEL, 1)], [qnw, knw], [(D_MODEL, F32)] * 2, [], tm=256,
                       name="fox_qk_fwd")
    o1 = fox_attn_fwd(q1, k1, pm1, fq, fk, name="fox_attn_fwd", riders=[ride_mlp_1])
    cat1 = rows_call(fox_out_fn, [o1, (pm1, D_MODEL, 3), (pm1, MEM_WIDTH, 8)], [mqw1, mk, mv],
                     [(D_MODEL + MEM_WIDTH, BF16)], [], tm=256, name="fox_out_fwd")[0]
    w1_1, w2_1 = ride_mlp_1.results
    x3, h2_1 = matmul(cat1, w_out1, post=_add, post_ins=[x2], row_ins=[n2w1], extra_out=(_norm_rows, BF16),
                      tiles=(None, D_MODEL, None), name="wout_fwd_1")
    (dy, sq), mlp_res1 = _mlp_fwd(x3, h2_1, w1_1, w2_1, 1, loss_target=target)
    loss = jnp.sum(sq) * (0.5 / D_MODEL)

    dx3, dw1_1, dw2_1, dn2w1, sib1, sib2 = _mlp_bwd(dy, mlp_res1, n2w1, w1_1, w2_1, 1)
    dcat1 = matmul(dx3, w_out1, tb=True, name="wout_dx_1", riders=[sib1])
    dwo_1 = matmul(cat1, dx3, ta=True, out_dtype=BF16, name="wout_dw_1").reshape(N_DEV, OUT_IN // N_DEV, D_MODEL)
    sibo = sibling_rider([dwo_1])
    do1, dgate1, dqm1, dmqw1, dmk1, dmv1 = rows_call(
        functools.partial(vjp_rows(fox_out_fn, 3, (True, True, True)), n_row=3, n_ct=1),
        [o1, (pm1, D_MODEL, 3), (pm1, MEM_WIDTH, 8), dcat1], [mqw1, mk, mv],
        [(D_MODEL, F32), (D_MODEL, BF16), (MEM_WIDTH, BF16)], [(1, HEAD_DIM), (n_mem, MEM_WIDTH), (n_mem, MEM_WIDTH)],
        tm=256, name="fox_out_bwd", riders=[sibo])
    ride_l1 = chips_rider(chip_sums(["w_mlp2_1", "w_mlp1_1", "w_out_1"], [dw2_1, dw1_1, dwo_1],
                                    sib2.results + sib1.results + sibo.results))
    dq1, dk1, dv1, dfq, dfk = fox_attn_bwd(q1, k1, pm1, fq, fk, do1, name="fox_attn_bwd", riders=[ride_l1])
    dqraw1, dkraw1, dqnw, dknw = rows_call(
        functools.partial(vjp_rows(fox_qk_fn, 2, (True, True)), n_row=2, n_ct=2),
        [(pm1, D_MODEL, 0), (pm1, D_MODEL, 1), dq1, dk1], [qnw, knw],
        [(D_MODEL, BF16)] * 2, [(1, HEAD_DIM)] * 2, tm=256, name="fox_qk_bwd")
    dfcum = dfq + jnp.pad(dfk[:, 0, :].T, ((0, 0), (0, LANES - N_HEADS)))
    dps1, dfbias = rows_call(
        functools.partial(vjp_rows(fox_fcum_fn, 1, (True,)), n_row=1, n_ct=1),
        [ps1, dfcum], [fbias], [(LANES, F32)], [(1, LANES)], tm=t, name="fox_fcum_bwd")
    dpm1 = jnp.concatenate([dqraw1, dkraw1, dv1, dgate1, dqm1], axis=1)
    dx2, dn1w1 = _in_proj_dx(dpm1, dps1, fox_main, fox_f, x2, dx3, n1w1, "1")
    dwmain1, dwsmall1 = _in_proj_dw(h1, dpm1, dps1, "1")
    g_fox = in_proj_pieces(dwmain1, dwsmall1, N_HEADS, FOX_IN)
    sibf = sibling_rider([g_fox])

    dx1, dw1_0, dw2_0, dn2w0, sib1, sib2 = _mlp_bwd(dx2, mlp_res0, n2w0, w1_0, w2_0, 0, riders=[sibf])
    ride_fox_g = chips_rider(chip_sums(["fox_w_in"], [g_fox], sibf.results))
    dcat0 = matmul(dx1, w_out0, tb=True, name="wout_dx_0", riders=[sib1])
    dwo_0 = matmul(cat0, dx1, ta=True, out_dtype=BF16, name="wout_dw_0").reshape(N_DEV, OUT_IN // N_DEV, D_MODEL)
    sibo = sibling_rider([dwo_0])
    do0, dz0, dqm0, donw, dmqw0, dmk0, dmv0 = rows_call(
        functools.partial(vjp_rows(dn_out_fn, 3, (True, True, True, True)), n_row=3, n_ct=1),
        [o0, (pm0, D_MODEL, 3), (pm0, MEM_WIDTH, 8), dcat0], [onw, mqw0, mk, mv],
        [((N_HEADS, HEAD_DIM), F32), (D_MODEL, BF16), (MEM_WIDTH, BF16)],
        [(1, HEAD_DIM), (1, HEAD_DIM), (n_mem, MEM_WIDTH), (n_mem, MEM_WIDTH)], tm=256, name="dn_out_bwd", riders=[sibo])
    h_l0 = chip_sums(["w_mlp2_0", "w_mlp1_0", "w_out_0"], [dw2_0, dw1_0, dwo_0], sib2.results + sib1.results + sibo.results)
    ride_l0_mlp2, ride_l0_rest = chips_rider(h_l0[:1]), chips_rider(h_l0[1:])
    dmnw, dwkv, dmknw = memkv_bwd(mem, mnw, w_kv, mknw, dmk0 + dmk1, dmv0 + dmv1, name="memkv_bwd")
    g_kv = dwkv.reshape(N_DEV, D_MODEL // N_DEV, D_MODEL)
    sibk = sibling_rider([g_kv])
    dq_s, dk_s, dg_s, du0, dw0, dqk0 = delta_seq_bwd(q0, k0, gates, u0, w0, qk0, s_start, do0, name="delta_seq_bwd",
                                                     riders=[ride_fox_g, sibk])
    ride_kv_g = chips_rider(chip_sums(["w_mem_kv"], [g_kv], sibk.results))
    dq0, dk0, dv0, dgates = delta_intra_bwd(q0, k0, v0, gates, du0, dw0, dqk0, dq_s, dk_s, dg_s,
                                            name="delta_intra_bwd", riders=[ride_l0_mlp2, ride_kv_g])
    dxq, dxk, dxv, dcq, dck, dcv = dn_prep_bwd(pm0, conv_w, dq0, dk0, dv0, name="dn_prep_bwd", riders=[ride_l0_rest])
    dconv = jnp.concatenate([dcq, dck, dcv], axis=1)
    dps0, dalog, ddtb = rows_call(
        functools.partial(vjp_rows(dn_gates_fn, 1, (True, True)), n_row=1, n_ct=1),
        [ps0, dgates], [alog, dtb], [(LANES, F32)], [(1, LANES)] * 2, tm=512, name="dn_gates_bwd")
    dpm0 = jnp.concatenate([dxq, dxk, dxv, dz0, dqm0], axis=1)
    dwmain0, dwsmall0 = _in_proj_dw(h0, dpm0, dps0, "0")
    g_dn = in_proj_pieces(dwmain0, dwsmall0, 2 * N_HEADS, DN_IN)
    g_conv = dconv.reshape(CONV_WIDTH, N_DEV, -1).transpose(1, 0, 2).astype(BF16)
    sibd = sibling_rider([g_dn, g_conv])
    grad_x, dn1w0 = _in_proj_dx(dpm0, dps0, dn_main, dn_ab, x0, dx1, n1w0, "0", riders=[sibd])

    g["mem_norm_w"] = dmnw[0]
    g["mem_k_norm_w"] = dmknw[0]
    g["norm1_w"] = jnp.concatenate([dn1w0, dn1w1], axis=0)
    g["dn_a_log"] = dalog[:, :N_HEADS]
    g["dn_dt_bias"] = ddtb[:, :N_HEADS]
    g["dn_o_norm_w"] = donw
    g["fox_f_bias"] = dfbias[:, :N_HEADS]
    g["fox_q_norm_w"] = dqnw
    g["fox_k_norm_w"] = dknw
    g["memq_norm_w"] = jnp.concatenate([dmqw0, dmqw1], axis=0)
    g["norm2_w"] = jnp.concatenate([dn2w0, dn2w1], axis=0)

    ride_last = chips_rider(chip_sums(["dn_w_in", "dn_conv_w"], [g_dn, g_conv], sibd.results))
    ride_small = gather_rider([pack_small(g, last=loss)])
    run_riders([ride_last, ride_small], name="grads_to_chips_last")

    def layers(l0, l1):
        return jnp.stack([l0, l1], axis=1).reshape(4, -1, l0.shape[-1])

    parts = {
        "w_mlp1": layers(ride_l0_rest.results[0], ride_l1.results[1]),
        "w_mlp2": layers(ride_l0_mlp2.results[0], ride_l1.results[0]),
        "w_out": layers(ride_l0_rest.results[1], ride_l1.results[2]),
        "fox_w_in": ride_fox_g.results[0], "w_mem_kv": ride_kv_g.results[0],
        "dn_w_in": ride_last.results[0], "dn_conv_w": ride_last.results[1],
    }
    out = {n: adamw(parts[n], w[n], m[n], v[n], name=f"adamw_{n}") for n, _, _ in BIG}
    small, loss = adamw_small(ride_small.results[0], w, m, v, name="adamw_small")
    return loss, grad_x, out, small


WEIGHTS = ["mem_norm_w", "w_mem_kv", "mem_k_norm_w", "norm1_w", "dn_w_in", "dn_conv_w", "dn_a_log", "dn_dt_bias",
           "dn_o_norm_w", "fox_w_in", "fox_f_bias", "fox_q_norm_w", "fox_k_norm_w", "memq_norm_w", "w_out", "norm2_w",
           "w_mlp1", "w_mlp2"]
DN_IN = 4 * D_MODEL + 2 * N_HEADS + MEM_WIDTH
FOX_IN = 4 * D_MODEL + N_HEADS + MEM_WIDTH
GATE_END = 4 * D_MODEL
OUT_IN = D_MODEL + MEM_WIDTH
BIG = [("w_mem_kv", D_MODEL // N_DEV, D_MODEL), ("dn_w_in", D_MODEL, DN_IN // N_DEV), ("fox_w_in", D_MODEL, FOX_IN // N_DEV),
       ("dn_conv_w", CONV_WIDTH, 3 * D_MODEL // N_DEV), ("w_out", 2 * OUT_IN // N_DEV, D_MODEL),
       ("w_mlp1", 2 * D_MODEL, FF_PIECE), ("w_mlp2", 2 * FF_PIECE, D_MODEL)]
SMALL_TILE = 8 * LANES
SMALL = [(name, shape, -(-math.prod(shape) // SMALL_TILE) * SMALL_TILE) for name, shape in [
    ("mem_norm_w", (D_MODEL,)), ("mem_k_norm_w", (HEAD_DIM,)), ("norm1_w", (2, D_MODEL)), ("dn_a_log", (1, N_HEADS)),
    ("dn_dt_bias", (1, N_HEADS)), ("dn_o_norm_w", (1, HEAD_DIM)), ("fox_f_bias", (1, N_HEADS)),
    ("fox_q_norm_w", (1, HEAD_DIM)), ("fox_k_norm_w", (1, HEAD_DIM)), ("memq_norm_w", (2, HEAD_DIM)), ("norm2_w", (2, D_MODEL))]]
SMALL_ROWS = sum(ln for _, _, ln in SMALL) // LANES + 8


def pack_small(p, last=None):
    def rows(a, ln):
        a = a.reshape(-1)
        return (a if a.shape[0] == ln else jnp.pad(a, (0, ln - a.shape[0]))).reshape(-1, LANES)

    used = sum(ln for _, _, ln in SMALL) // LANES
    tail = jnp.zeros(((SMALL_ROWS - used) * LANES,), F32)
    if last is not None:
        tail = jnp.concatenate([tail[:-1], last.reshape(1)])
    return jnp.concatenate([rows(p[n], ln) for n, _, ln in SMALL] + [tail.reshape(-1, LANES)], axis=0)


def in_proj_weights(gathered, width, n_small):
    full = gathered.transpose(1, 0, 2).reshape(D_MODEL, width)
    main = jnp.concatenate([full[:, :GATE_END], full[:, GATE_END + n_small:]], axis=1)
    return main, jnp.pad(full[:, GATE_END:GATE_END + n_small], ((0, 0), (0, LANES - n_small)))


def in_proj_pieces(d_main, d_small, n_small, width):
    full = jnp.concatenate([d_main[:, :GATE_END], d_small[:, :n_small], d_main[:, GATE_END:]], axis=1)
    return full.reshape(D_MODEL, N_DEV, width // N_DEV).transpose(1, 0, 2)


def _adamw_update(g, w, m, v):
    m_new = ADAM_B1 * m + (1.0 - ADAM_B1) * g
    v_new = ADAM_B2 * v + (1.0 - ADAM_B2) * jnp.square(g)
    m_hat = m_new / (1.0 - ADAM_B1 ** ADAM_STEP)
    v_hat = v_new / (1.0 - ADAM_B2 ** ADAM_STEP)
    return -ADAM_LR * (m_hat / (jnp.sqrt(v_hat) + ADAM_EPS) + ADAM_WD * w), m_new, v_new


def adamw(parts, w, m, v, *, name):
    n, _, cols = parts.shape
    layers = w.shape[0] if w.ndim == 3 else 1
    rows = w.shape[-2]
    tile = _pick(rows, (512, 256, 128))
    steps = rows // tile

    def body(p_ref, w_ref, m_ref, v_ref, g_ref, d_ref, mo_ref, vo_ref):
        g = p_ref[0].astype(F32)
        for i in range(1, n):
            g = g + p_ref[i].astype(F32)
        g_ref[...] = g
        d_ref[...], mo_ref[...], vo_ref[...] = _adamw_update(g, w_ref[...], m_ref[...], v_ref[...])

    if w.ndim == 3:
        spec = pl.BlockSpec((None, tile, cols), lambda l, i: (l, i, 0))
    else:
        spec = pl.BlockSpec((tile, cols), lambda l, i: (i, 0))
    return pl.pallas_call(
        body, name=name, grid=(layers, steps),
        in_specs=[pl.BlockSpec((n, tile, cols), lambda l, i: (0, l * steps + i, 0)), spec, spec, spec], out_specs=[spec] * 4,
        out_shape=[jax.ShapeDtypeStruct(w.shape, F32)] * 4, compiler_params=_params(("parallel", "parallel")),
    )(parts, w, m, v)


def adamw_small(parts, w, m, v, *, name):
    def view(a):
        return a.reshape(-1, LANES) if a.size % LANES == 0 else a.reshape(1, a.size)

    k = len(SMALL)
    ins = [view(d[n]) for d in (w, m, v) for n, _, _ in SMALL]

    def body(p_ref, *refs):
        w_refs, m_refs, v_refs, outs, g_ref = refs[:k], refs[k:2 * k], refs[2 * k:3 * k], refs[3 * k:-1], refs[-1]
        g_all = p_ref[0]
        for i in range(1, N_DEV):
            g_all = g_all + p_ref[i]
        g_ref[...] = g_all
        row = 0
        for i, (_, _, ln) in enumerate(SMALL):
            r, c = w_refs[i].shape
            g = g_ref[row:row + r, 0:c]
            outs[4 * i][...] = g
            outs[4 * i + 1][...], outs[4 * i + 2][...], outs[4 * i + 3][...] = _adamw_update(
                g, w_refs[i][...], m_refs[i][...], v_refs[i][...])
            row += ln // LANES
        outs[-1][...] = g_ref[SMALL_ROWS - 1:SMALL_ROWS, LANES - 1:LANES]

    out_shape = [jax.ShapeDtypeStruct(a.shape, F32) for a in ins[:k] for _ in range(4)] + [jax.ShapeDtypeStruct((1, 1), F32)]
    res = pl.pallas_call(body, name=name, out_shape=out_shape,
                         scratch_shapes=[pltpu.VMEM((SMALL_ROWS, LANES), F32)])(parts, *ins)
    small = {n: [o.reshape(sh) for o in res[4 * i:4 * i + 4]] for i, (n, sh, _) in enumerate(SMALL)}
    return small, res[-1][0, 0]


def kernel(x, mem, mem_norm_w, w_mem_kv, mem_k_norm_w, norm1_w, dn_w_in, dn_conv_w, dn_a_log, dn_dt_bias, dn_o_norm_w, fox_w_in, fox_f_bias, fox_q_norm_w, fox_k_norm_w, memq_norm_w, w_out, norm2_w, w_mlp1, w_mlp2, loss_target, m_mem_norm_w, m_w_mem_kv, m_mem_k_norm_w, m_norm1_w, m_dn_w_in, m_dn_conv_w, m_dn_a_log, m_dn_dt_bias, m_dn_o_norm_w, m_fox_w_in, m_fox_f_bias, m_fox_q_norm_w, m_fox_k_norm_w, m_memq_norm_w, m_w_out, m_norm2_w, m_w_mlp1, m_w_mlp2, v_mem_norm_w, v_w_mem_kv, v_mem_k_norm_w, v_norm1_w, v_dn_w_in, v_dn_conv_w, v_dn_a_log, v_dn_dt_bias, v_dn_o_norm_w, v_fox_w_in, v_fox_f_bias, v_fox_q_norm_w, v_fox_k_norm_w, v_memq_norm_w, v_w_out, v_norm2_w, v_w_mlp1, v_w_mlp2):
    p = dict(mem_norm_w=mem_norm_w, w_mem_kv=w_mem_kv, mem_k_norm_w=mem_k_norm_w, norm1_w=norm1_w, dn_w_in=dn_w_in,
             dn_conv_w=dn_conv_w, dn_a_log=dn_a_log, dn_dt_bias=dn_dt_bias, dn_o_norm_w=dn_o_norm_w, fox_w_in=fox_w_in,
             fox_f_bias=fox_f_bias, fox_q_norm_w=fox_q_norm_w, fox_k_norm_w=fox_k_norm_w, memq_norm_w=memq_norm_w,
             w_out=w_out, norm2_w=norm2_w, w_mlp1=w_mlp1, w_mlp2=w_mlp2)
    pm = dict(mem_norm_w=m_mem_norm_w, w_mem_kv=m_w_mem_kv, mem_k_norm_w=m_mem_k_norm_w, norm1_w=m_norm1_w,
              dn_w_in=m_dn_w_in, dn_conv_w=m_dn_conv_w, dn_a_log=m_dn_a_log, dn_dt_bias=m_dn_dt_bias,
              dn_o_norm_w=m_dn_o_norm_w, fox_w_in=m_fox_w_in, fox_f_bias=m_fox_f_bias, fox_q_norm_w=m_fox_q_norm_w,
              fox_k_norm_w=m_fox_k_norm_w, memq_norm_w=m_memq_norm_w, w_out=m_w_out, norm2_w=m_norm2_w, w_mlp1=m_w_mlp1,
              w_mlp2=m_w_mlp2)
    pv = dict(mem_norm_w=v_mem_norm_w, w_mem_kv=v_w_mem_kv, mem_k_norm_w=v_mem_k_norm_w, norm1_w=v_norm1_w,
              dn_w_in=v_dn_w_in, dn_conv_w=v_dn_conv_w, dn_a_log=v_dn_a_log, dn_dt_bias=v_dn_dt_bias,
              dn_o_norm_w=v_dn_o_norm_w, fox_w_in=v_fox_w_in, fox_f_bias=v_fox_f_bias, fox_q_norm_w=v_fox_q_norm_w,
              fox_k_norm_w=v_fox_k_norm_w, memq_norm_w=v_memq_norm_w, w_out=v_w_out, norm2_w=v_norm2_w, w_mlp1=v_w_mlp1,
              w_mlp2=v_w_mlp2)

    loss, grad_x, results, small = local_step(x[0], mem[0], loss_target[0], p, pm, pv)
    groups = [{n: r[i] for n, r in {**small, **results}.items()} for i in range(4)]
    return (loss, grad_x[None], *[grp[n] for grp in groups for n in WEIGHTS])
```

```python
import functools
import math

import jax
import jax.numpy as jnp
from jax import lax
from jax.experimental import pallas as pl
from jax.experimental.pallas import tpu as pltpu

F32 = jnp.float32
BF16 = jnp.bfloat16
HIGHEST = lax.Precision.HIGHEST

D_MODEL = 1024
HEAD_DIM = 128
N_HEADS = 8
MEM_HEADS = 4
MEM_WIDTH = MEM_HEADS * HEAD_DIM
D_FF = 4 * D_MODEL
CONV_WIDTH = 4
CHUNK = 64
Q_BLOCK = 128
EPS = 1e-6
SCALE = HEAD_DIM ** -0.5
MAIN_WIDTH = 4 * D_MODEL + MEM_WIDTH
LANES = 128
N_DEV = 8

ADAM_LR = 0.001
ADAM_B1 = 0.9
ADAM_B2 = 0.999
ADAM_EPS = 1e-08
ADAM_WD = 0.01
ADAM_STEP = 10

VMEM_LIMIT = 56 * 2 ** 20
MESH = pl.DeviceIdType.MESH


def _bdot(a, b, dims):
    return lax.dot_general(a.astype(BF16), b.astype(BF16), (dims, ((), ())), preferred_element_type=F32)


@jax.custom_vjp
def mm(a, b):
    return _bdot(a, b, ((1,), (0,)))


@jax.custom_vjp
def mm_nt(a, b):
    return _bdot(a, b, ((1,), (1,)))


@jax.custom_vjp
def mm_tn(a, b):
    return _bdot(a, b, ((0,), (0,)))


mm.defvjp(lambda a, b: (mm(a, b), (a, b)), lambda r, g: (mm_nt(g, r[1]), mm_tn(r[0], g)))
mm_nt.defvjp(lambda a, b: (mm_nt(a, b), (a, b)), lambda r, g: (mm(g, r[1]), mm_tn(g, r[0])))
mm_tn.defvjp(lambda a, b: (mm_tn(a, b), (a, b)), lambda r, g: (mm_nt(r[1], g), mm(r[0], g)))


def hdot(a, b):
    return jnp.dot(a, b, precision=HIGHEST, preferred_element_type=F32)


def rms(x, w):
    return x * lax.rsqrt(jnp.mean(x * x, axis=-1, keepdims=True) + EPS) * w


def l2n(x):
    return x * lax.rsqrt(jnp.sum(x * x, axis=-1, keepdims=True) + EPS)


def _iota2(n, m):
    return lax.broadcasted_iota(jnp.int32, (n, m), 0), lax.broadcasted_iota(jnp.int32, (n, m), 1)


def _lower_ones(n):
    r, c = _iota2(n, n)
    return jnp.where(r >= c, 1.0, 0.0).astype(F32)


def _last_row(x):
    r = lax.broadcasted_iota(jnp.int32, x.shape, 0)
    return jnp.sum(jnp.where(r == x.shape[0] - 1, x, 0.0), axis=0, keepdims=True)


def _softmax_rows(z):
    m = lax.stop_gradient(jnp.max(z, axis=-1, keepdims=True))
    e = jnp.exp(z - m)
    return e * (1.0 / jnp.sum(e, axis=-1, keepdims=True))


_BNN = (((2,), (1,)), ((0,), (0,)))
_BNT = (((2,), (2,)), ((0,), (0,)))
_BTN = (((1,), (1,)), ((0,), (0,)))


def _bbdot(a, b, dims):
    return lax.dot_general(a.astype(BF16), b.astype(BF16), dims, preferred_element_type=F32)


@jax.custom_vjp
def bmm(a, b):
    return _bbdot(a, b, _BNN)


@jax.custom_vjp
def bmm_nt(a, b):
    return _bbdot(a, b, _BNT)


@jax.custom_vjp
def bmm_tn(a, b):
    return _bbdot(a, b, _BTN)


@jax.custom_vjp
def bmm_high(a, b):
    return lax.dot_general(a, b, _BNN, precision=lax.Precision.HIGH, preferred_element_type=F32)


bmm.defvjp(lambda a, b: (bmm(a, b), (a, b)), lambda r, g: (bmm_nt(g, r[1]), bmm_tn(r[0], g)))
bmm_nt.defvjp(lambda a, b: (bmm_nt(a, b), (a, b)), lambda r, g: (bmm(g, r[1]), bmm_tn(g, r[0])))
bmm_tn.defvjp(lambda a, b: (bmm_tn(a, b), (a, b)), lambda r, g: (bmm_nt(r[1], g), bmm(r[0], g)))
bmm_high.defvjp(lambda a, b: (bmm_high(a, b), (a, b)), lambda r, g: (bmm_nt(g, r[1]), bmm_tn(r[0], g)))

NEUMANN_HIGH_LEVELS = 2


@jax.custom_vjp
def inv_unit_lower(a):
    n = a.shape[-1]
    r, c = _iota2(n, n)
    p = jnp.where(r == c, 1.0, 0.0).astype(F32) - a
    ak = a
    for level in range(int(math.log2(n)) - 1):
        dot = bmm_high if level < NEUMANN_HIGH_LEVELS else bmm
        ak = dot(ak, ak)
        p = p + dot(p, ak)
    return p


def _inv_unit_lower_fwd(a):
    t = inv_unit_lower(a)
    return t, t


def _inv_unit_lower_bwd(t, g):
    return (-bmm_tn(t, bmm_nt(g, t)),)


inv_unit_lower.defvjp(_inv_unit_lower_fwd, _inv_unit_lower_bwd)


def delta_intra(q, k, v, gc, beta):
    b, c, _ = q.shape
    r, cc = _iota2(c, c)
    causal = r >= cc
    strict = r > cc
    gi = jnp.broadcast_to(gc, (b, c, c))
    gj = jnp.swapaxes(gi, 1, 2)
    decay = jnp.where(causal, jnp.exp(jnp.where(causal, gi - gj, 0.0)), 0.0)
    kb = k * beta
    a = jnp.where(strict, bmm_nt(kb, k) * decay, 0.0)
    t = inv_unit_lower(a)
    u = bmm(t, v * beta)
    w = bmm(t, kb * jnp.exp(gc))
    qk = jnp.where(causal, bmm_nt(q, k) * decay, 0.0)
    return u, w, qk


def delta_step(s, q, k, gc, u, w, qk):
    v_new = u - bmm(w, s)
    out = bmm(q * jnp.exp(gc), s) + bmm(qk, v_new)
    r = lax.broadcasted_iota(jnp.int32, gc.shape, 1)
    g_last = jnp.sum(jnp.where(r == gc.shape[1] - 1, gc, 0.0), axis=1, keepdims=True)
    k_dec = k * jnp.exp(g_last - gc)
    s_new = s * jnp.exp(g_last) + bmm_tn(k_dec, v_new)
    return out, s_new


def fox_probs(q, k, fq, fk, qpos0):
    s = lax.dot_general(q, k, (((1,), (1,)), ((), ())), preferred_element_type=F32)
    r, c = _iota2(s.shape[0], s.shape[1])
    return _softmax_rows(jnp.where(c <= (r + qpos0), s + (fq - fk), -jnp.inf))


def mem_head(qm, wq, mk, mv):
    p = _softmax_rows(mm_nt(rms(qm, wq) * SCALE, mk))
    return mm(p, mv)


def _heads(x, n):
    return [x[:, h * HEAD_DIM:(h + 1) * HEAD_DIM] for h in range(n)]


def memkv_fn(mem, mnw, wkv, mknw):
    kv = mm(rms(mem, mnw), wkv)
    mk = jnp.concatenate([rms(kh, mknw) for kh in _heads(kv[:, :MEM_WIDTH], MEM_HEADS)], axis=1)
    return mk, kv[:, MEM_WIDTH:]


def dn_gates_fn(ab, alog, dtb):
    g = -jnp.exp(alog) * jax.nn.softplus(ab + dtb)
    low = _lower_ones(CHUNK)
    gc = jnp.concatenate([hdot(low, g[i * CHUNK:(i + 1) * CHUNK]) for i in range(ab.shape[0] // CHUNK)], axis=0)
    lane = lax.broadcasted_iota(jnp.int32, ab.shape, 1)
    return jnp.where(lane < N_HEADS, gc, jax.nn.sigmoid(ab))


def fox_fcum_fn(fp, fbias):
    lf = jax.nn.log_sigmoid(fp + fbias)
    low = _lower_ones(LANES)
    carry = jnp.zeros((1, fp.shape[1]), F32)
    outs = []
    for i in range(fp.shape[0] // LANES):
        cs = hdot(low, lf[i * LANES:(i + 1) * LANES]) + carry
        carry = _last_row(cs)
        outs.append(cs)
    return jnp.concatenate(outs, axis=0)


def fox_qk_fn(qraw, kraw, qnw, knw):
    q = jnp.concatenate([rms(x, qnw) * SCALE for x in _heads(qraw, N_HEADS)], axis=1)
    k = jnp.concatenate([rms(x, knw) for x in _heads(kraw, N_HEADS)], axis=1)
    return q, k


def _mem_out(qm, mqw, mk, mv):
    return [mem_head(a, mqw, b, c) for a, b, c in zip(_heads(qm, MEM_HEADS), _heads(mk, MEM_HEADS), _heads(mv, MEM_HEADS))]


def dn_out_fn(o, z, qm, onw, mqw, mk, mv):
    mix = [rms(a, onw) * jax.nn.silu(b) for a, b in zip(o, _heads(z, N_HEADS))]
    return jnp.concatenate(mix + _mem_out(qm, mqw, mk, mv), axis=1)


def fox_out_fn(o, gate, qm, mqw, mk, mv):
    return jnp.concatenate([o * jax.nn.sigmoid(gate)] + _mem_out(qm, mqw, mk, mv), axis=1)


_HBM = pl.BlockSpec(memory_space=pltpu.HBM)


def _place():
    return lax.axis_index("x"), lax.axis_index("y"), lax.axis_index("c")


class Rider:
    def __init__(self, ins, out_shape, scratch, start, finish):
        self.ins, self.out_shape, self.scratch, self.start, self.finish = list(ins), list(out_shape), list(scratch), start, finish
        self.results = None


def gather_rider(xs):
    n = len(xs)

    def plan(x_refs, out_refs, sems):
        send_sems, recv_sems, local_sems = sems
        x, y, c = _place()
        me, sibling = (x, y, c), (x, y, 1 - c)
        chips = [(1 - x, y), (x, 1 - y), (1 - x, 1 - y)]

        def copy(a, k, block, to, src=None):
            px, py, pc = block
            dst = out_refs[a].at[4 * px + 2 * py + pc]
            return pltpu.make_async_remote_copy(
                src_ref=dst if src is None else src, dst_ref=dst,
                send_sem=send_sems.at[a, k], recv_sem=recv_sems.at[a, k], device_id=to, device_id_type=MESH)

        mine = [pltpu.make_async_copy(x_refs[a], out_refs[a].at[4 * x + 2 * y + c], local_sems.at[a]) for a in range(n)]
        first = [copy(a, 0, me, sibling, src=x_refs[a]) for a in range(n)]
        first += [copy(a, 1 + j, me, (*chip, c), src=x_refs[a]) for j, chip in enumerate(chips) for a in range(n)]
        return copy, me, sibling, chips, mine, first

    def start(x_refs, out_refs, sems):
        _, _, _, _, mine, first = plan(x_refs, out_refs, sems)
        for cp in mine + first:
            cp.start()

    def finish(x_refs, out_refs, sems):
        copy, me, sibling, chips, mine, first = plan(x_refs, out_refs, sems)
        _, _, c = me
        passed = []
        for j, chip in enumerate(chips):
            for a in range(n):
                copy(a, 1 + j, (*chip, c), me).wait_recv()
                passed.append(copy(a, 4 + j, (*chip, c), sibling))
                passed[-1].start()
        for a in range(n):
            copy(a, 0, sibling, me).wait_recv()
        for j, chip in enumerate(chips):
            for a in range(n):
                copy(a, 4 + j, (*chip, 1 - c), me).wait_recv()
        for cp in first + passed:
            cp.wait_send()
        for cp in mine:
            cp.wait()

    return Rider(xs, [jax.ShapeDtypeStruct((N_DEV,) + a.shape, a.dtype) for a in xs],
                 [pltpu.SemaphoreType.DMA((n, 7)), pltpu.SemaphoreType.DMA((n, 7)), pltpu.SemaphoreType.DMA((n,))], start, finish)


def sibling_rider(gs):
    n = len(gs)

    def plan(g_refs, out_refs, sems):
        send_sems, recv_sems = sems
        x, y, c = _place()
        return [pltpu.make_async_remote_copy(
            src_ref=g_refs[a].at[2 * k + 1 - c], dst_ref=out_refs[a].at[k], send_sem=send_sems.at[a, k],
            recv_sem=recv_sems.at[a, k], device_id=(x, y, 1 - c), device_id_type=MESH) for a in range(n) for k in range(4)]

    def start(g_refs, out_refs, sems):
        for cp in plan(g_refs, out_refs, sems):
            cp.start()

    def finish(g_refs, out_refs, sems):
        copies = plan(g_refs, out_refs, sems)
        for cp in copies:
            cp.wait_recv()
        for cp in copies:
            cp.wait_send()

    return Rider(gs, [jax.ShapeDtypeStruct((4,) + g.shape[1:], g.dtype) for g in gs],
                 [pltpu.SemaphoreType.DMA((n, 4)), pltpu.SemaphoreType.DMA((n, 4))], start, finish)


def chips_rider(hs):
    n = len(hs)

    def plan(h_refs, out_refs, sems):
        send_sems, recv_sems, local_sems = sems
        x, y, c = _place()
        mine = 2 * x + y
        chips = [(1 - x, y), (x, 1 - y), (1 - x, 1 - y)]
        keep = [pltpu.make_async_copy(h_refs[a].at[mine], out_refs[a].at[mine], local_sems.at[a]) for a in range(n)]
        sends = [pltpu.make_async_remote_copy(
            src_ref=h_refs[a].at[2 * qx + qy], dst_ref=out_refs[a].at[mine], send_sem=send_sems.at[a, j],
            recv_sem=recv_sems.at[a, j], device_id=(qx, qy, c), device_id_type=MESH)
            for j, (qx, qy) in enumerate(chips) for a in range(n)]
        recvs = [pltpu.make_async_remote_copy(
            src_ref=h_refs[a].at[mine], dst_ref=out_refs[a].at[2 * qx + qy], send_sem=send_sems.at[a, j],
            recv_sem=recv_sems.at[a, j], device_id=(qx, qy, c), device_id_type=MESH)
            for j, (qx, qy) in enumerate(chips) for a in range(n)]
        return keep, sends, recvs

    def start(h_refs, out_refs, sems):
        keep, sends, _ = plan(h_refs, out_refs, sems)
        for cp in keep + sends:
            cp.start()

    def finish(h_refs, out_refs, sems):
        keep, sends, recvs = plan(h_refs, out_refs, sems)
        for cp in recvs:
            cp.wait_recv()
        for cp in sends:
            cp.wait_send()
        for cp in keep:
            cp.wait()

    return Rider(hs, [jax.ShapeDtypeStruct(h.shape, h.dtype) for h in hs],
                 [pltpu.SemaphoreType.DMA((n, 3)), pltpu.SemaphoreType.DMA((n, 3)), pltpu.SemaphoreType.DMA((n,))], start, finish)


def hosted_call(riders, body, *, out_shape, in_specs, out_specs, grid=(), scratch_shapes=(), **kw):
    riders = tuple(riders or ())
    if not riders:
        return pl.pallas_call(body, out_shape=out_shape, in_specs=in_specs, out_specs=out_specs, grid=grid,
                              scratch_shapes=scratch_shapes, **kw)
    single = not isinstance(out_shape, (list, tuple))
    k_out_shape = [out_shape] if single else list(out_shape)
    k_out_specs = [out_specs] if single else list(out_specs)
    n_in, n_out, n_scr = len(in_specs), len(k_out_shape), len(scratch_shapes)
    r_ins = [a for r in riders for a in r.ins]
    r_outs = [s for r in riders for s in r.out_shape]
    r_scr = [s for r in riders for s in r.scratch]

    def full_body(*refs):
        ins = refs[:n_in + len(r_ins)]
        outs = refs[n_in + len(r_ins):n_in + len(r_ins) + n_out + len(r_outs)]
        scr = refs[n_in + len(r_ins) + n_out + len(r_outs):]
        steps = math.prod(grid)
        step = 0
        for d, g in enumerate(grid):
            step = step * g + pl.program_id(d)

        def each(method):
            i0, o0, s0 = n_in, n_out, n_scr
            for r in riders:
                getattr(r, method)(ins[i0:i0 + len(r.ins)], outs[o0:o0 + len(r.out_shape)], scr[s0:s0 + len(r.scratch)])
                i0, o0, s0 = i0 + len(r.ins), o0 + len(r.out_shape), s0 + len(r.scratch)

        if steps == 1:
            each("start")
            body(*ins[:n_in], *outs[:n_out], *scr[:n_scr])
            each("finish")
        else:
            pl.when(step == 0)(lambda: each("start"))
            body(*ins[:n_in], *outs[:n_out], *scr[:n_scr])
            pl.when(step == steps - 1)(lambda: each("finish"))

    call = pl.pallas_call(
        full_body, out_shape=k_out_shape + r_outs, in_specs=list(in_specs) + [_HBM] * len(r_ins),
        out_specs=k_out_specs + [_HBM] * len(r_outs), grid=grid, scratch_shapes=list(scratch_shapes) + r_scr, **kw)

    def run(*args):
        res = call(*args, *r_ins)
        o0 = n_out
        for r in riders:
            r.results = list(res[o0:o0 + len(r.out_shape)])
            o0 += len(r.out_shape)
        return res[0] if single else list(res[:n_out])

    return run


def run_riders(riders, *, name):
    hosted_call(riders, lambda: None, name=name, out_shape=[], in_specs=[], out_specs=[])()
    return [r.results for r in riders]


def _pick(n, cands):
    for c in cands:
        if n % c == 0:
            return c
    return n


def _params(sem):
    return pltpu.CompilerParams(dimension_semantics=sem, vmem_limit_bytes=VMEM_LIMIT)


MATMUL_VMEM_BUDGET = 40 * 2 ** 20


def _matmul_tiles(m, n, k, bytes_a, bytes_b, bytes_mn, fixed):
    fm, fn, fk = fixed if fixed is not None else (None, None, None)

    def options(given, size, cands):
        return [given] if given else ([c for c in cands if size % c == 0] or [size])

    best = None
    for tm in options(fm, m, (2048, 1024, 512, 256, 128)):
        for tn in options(fn, n, (512, 256, 128)):
            for tk in options(fk, k, (2048, 1536, 1024, 512, 256, 128)):
                if 2 * (tm * tk * bytes_a + tk * tn * bytes_b + tm * tn * bytes_mn) + tm * tn * 4 > MATMUL_VMEM_BUDGET:
                    continue
                key = ((m // tm) * (n // tn) * (k // tk), -tk)
                if best is None or key < best[0]:
                    best = (key, (tm, tn, tk))
    assert best is not None, (m, n, k, fixed)
    return best[1]


def matmul(a, b, *, name, ta=False, tb=False, post=None, post_ins=(), row_ins=(), acc=False, extra_out=None,
           out_dtype=F32, tiles=None, b_view=None, out_view=None, riders=()):
    (k, m) = a.shape if ta else a.shape[::-1]
    (kb, n) = b_view[:2] if b_view is not None else (b.shape[::-1] if tb else b.shape)
    assert k == kb, (a.shape, b.shape, ta, tb)
    bytes_mn = sum(p.dtype.itemsize for p in post_ins) + jnp.dtype(out_dtype).itemsize
    bytes_mn += jnp.dtype(extra_out[1]).itemsize if extra_out else 0
    tm, tn, tk = _matmul_tiles(m, n, k, a.dtype.itemsize, b.dtype.itemsize, bytes_mn, tiles)
    assert not acc or tn == n, (name, tn, n)
    nk = k // tk
    dims = ((0,) if ta else (1,), (1,) if tb else (0,))
    n_post, n_row = len(post_ins), len(row_ins)
    n_out = 1 + bool(extra_out) + bool(acc)

    def body(*refs):
        a_ref, b_ref = refs[:2]
        post_refs = refs[2:2 + n_post + n_row]
        o_refs, acc_ref = refs[-1 - n_out:-1], refs[-1]
        first_rows, kk = pl.program_id(0) == 0, pl.program_id(2)

        @pl.when(kk == 0)
        def _():
            acc_ref[...] = jnp.zeros_like(acc_ref)

        b_tile = b_ref[...]
        acc_ref[...] += _bdot(a_ref[...], b_tile.reshape(-1, b_tile.shape[-1]), dims)

        @pl.when(kk == nk - 1)
        def _():
            r = acc_ref[...]
            rows = [p[...] for p in post_refs[n_post:]]
            if post is not None:
                r = post(r, *[p[...] for p in post_refs[:n_post]], *rows)
            if acc:
                r, s = r
                sum_ref = o_refs[-1]

                @pl.when(first_rows)
                def _():
                    sum_ref[...] = s

                @pl.when(jnp.logical_not(first_rows))
                def _():
                    sum_ref[...] += s

            o_refs[0][...] = r.astype(out_dtype)
            if extra_out:
                o_refs[1][...] = extra_out[0](r, *rows).astype(extra_out[1])

    a_spec = pl.BlockSpec((tk, tm), lambda i, j, kk: (kk, i)) if ta else pl.BlockSpec((tm, tk), lambda i, j, kk: (i, kk))
    if b_view is not None:
        b_spec = b_view[2]
    else:
        b_spec = pl.BlockSpec((tn, tk), lambda i, j, kk: (j, kk)) if tb else pl.BlockSpec((tk, tn), lambda i, j, kk: (kk, j))
    mn_spec = pl.BlockSpec((tm, tn), lambda i, j, kk: (i, j))
    row_spec = pl.BlockSpec((1, tn), lambda i, j, kk: (0, j))
    o_shape, o_spec = ((m, n), mn_spec) if out_view is None else out_view
    out_shape = [jax.ShapeDtypeStruct(o_shape, out_dtype)]
    out_specs = [o_spec]
    if extra_out:
        out_shape.append(jax.ShapeDtypeStruct((m, n), extra_out[1]))
        out_specs.append(mn_spec)
    if acc:
        out_shape.append(jax.ShapeDtypeStruct((1, n), F32))
        out_specs.append(row_spec)
    res = hosted_call(
        riders, body, name=name, grid=(m // tm, n // tn, nk),
        in_specs=[a_spec, b_spec] + [mn_spec] * n_post + [row_spec] * n_row, out_specs=out_specs, out_shape=out_shape,
        scratch_shapes=[pltpu.VMEM((tm, tn), F32)],
        compiler_params=_params(("arbitrary" if acc else "parallel", "parallel", "arbitrary")),
    )(a, b, *post_ins, *row_ins)
    return res if n_out > 1 else res[0]


def rows_call(fn, row_ins, full_ins, row_outs, acc_outs, *, tm, name, riders=()):
    row_ins = [r if isinstance(r, tuple) else (r, r.shape[-1], 0) for r in row_ins]
    t = row_ins[0][0].shape[-2]
    tm = min(tm, t)
    n_in = len(row_ins) + len(full_ins)
    n_row = len(row_outs)

    def body(*refs):
        res = fn(*[[r[h] for h in range(r.shape[0])] if (i < len(row_ins) and len(r.shape) == 3) else r[...]
                   for i, r in enumerate(refs[:n_in])])
        res = res if isinstance(res, (tuple, list)) else (res,)
        outs = refs[n_in:]
        for ref, val in zip(outs[:n_row], res[:n_row]):
            if len(ref.shape) == 3:
                for h, vh in enumerate(val):
                    ref[h] = vh.astype(ref.dtype)
            else:
                ref[...] = val.astype(ref.dtype)
        first = pl.program_id(0) == 0
        for ref, val in zip(outs[n_row:], res[n_row:]):
            @pl.when(first)
            def _(ref=ref, val=val):
                ref[...] = val

            @pl.when(jnp.logical_not(first))
            def _(ref=ref, val=val):
                ref[...] += val

    def full_spec(shape):
        return pl.BlockSpec(shape, lambda i, nd=len(shape): (0,) * nd)

    def row_spec(lead, w, cb):
        if lead is None:
            return pl.BlockSpec((tm, w), lambda i: (i, cb))
        return pl.BlockSpec((lead, tm, w), lambda i: (0, i, cb))

    def lead_cols(c):
        return c if isinstance(c, tuple) else (None, c)

    in_specs = [row_spec(a.shape[0] if a.ndim == 3 else None, w, cb) for (a, w, cb) in row_ins]
    in_specs += [full_spec(f.shape) for f in full_ins]
    out_specs = [row_spec(*lead_cols(c), 0) for c, _ in row_outs] + [full_spec(s) for s in acc_outs]
    out_shape = [jax.ShapeDtypeStruct(tuple(d for d in (lead_cols(c)[0], t, lead_cols(c)[1]) if d is not None), dt)
                 for c, dt in row_outs] + [jax.ShapeDtypeStruct(s, F32) for s in acc_outs]
    res = hosted_call(
        riders, body, name=name, grid=(t // tm,), in_specs=in_specs, out_specs=out_specs, out_shape=out_shape,
        compiler_params=_params(("arbitrary",)),
    )(*[r[0] for r in row_ins], *full_ins)
    return res


def vjp_rows(fn, n_diff_row, row_diff_full):
    def bwd(*args, n_row, n_ct):
        prim_rows = args[:n_row]
        cts = args[n_row:n_row + n_ct]
        fulls = args[n_row + n_ct:]
        _, vjp = jax.vjp(fn, *prim_rows, *fulls)
        g = vjp(cts[0] if n_ct == 1 else tuple(cts))
        out = list(g[:n_diff_row])
        out += [gf for gf, d in zip(g[n_row:], row_diff_full) if d]
        return tuple(out)
    return bwd


def _shift_down(x, s):
    if s == 0:
        return x
    t = lax.broadcasted_iota(jnp.int32, x.shape, 0)
    return jnp.where(t >= s, pltpu.roll(x, s, 0), 0.0)


def _shift_up(x, s):
    if s == 0:
        return x
    n = x.shape[0]
    t = lax.broadcasted_iota(jnp.int32, x.shape, 0)
    return jnp.where(t < n - s, pltpu.roll(x, n - s, 0), 0.0)


def _conv(x, w_ref):
    return sum(w_ref[pl.ds(j, 1), :] * _shift_down(x, CONV_WIDTH - 1 - j) for j in range(CONV_WIDTH))


_DN_POST = (lambda c: l2n(jax.nn.silu(c)) * SCALE, lambda c: l2n(jax.nn.silu(c)), jax.nn.silu)


def dn_prep_fwd(proj, conv_w, *, name, riders=()):
    t = proj.shape[0]

    def body(xq, xk, xv, wq, wk, wv, oq, ok, ov):
        for x_ref, w_ref, o_ref, post in zip((xq, xk, xv), (wq, wk, wv), (oq, ok, ov), _DN_POST):
            o_ref[...] = post(_conv(x_ref[...], w_ref))

    x_specs = [pl.BlockSpec((t, HEAD_DIM), lambda h, g=g: (0, g * N_HEADS + h)) for g in range(3)]
    w_specs = [pl.BlockSpec((CONV_WIDTH, HEAD_DIM), lambda h, g=g: (0, g * N_HEADS + h)) for g in range(3)]
    o_spec = pl.BlockSpec((None, t, HEAD_DIM), lambda h: (h, 0, 0))
    return hosted_call(
        riders, body, name=name, grid=(N_HEADS,), in_specs=x_specs + w_specs, out_specs=[o_spec] * 3,
        out_shape=[jax.ShapeDtypeStruct((N_HEADS, t, HEAD_DIM), F32)] * 3, compiler_params=_params(("parallel",)),
    )(proj, proj, proj, conv_w, conv_w, conv_w)


def dn_prep_bwd(proj, conv_w, dq, dk, dv, *, name, riders=()):
    t = proj.shape[0]

    def body(xq, xk, xv, wq, wk, wv, gq, gk, gv, dxq, dxk, dxv, dwq, dwk, dwv):
        for x_ref, w_ref, g_ref, dx_ref, dw_ref, post in zip(
                (xq, xk, xv), (wq, wk, wv), (gq, gk, gv), (dxq, dxk, dxv), (dwq, dwk, dwv), _DN_POST):
            x = x_ref[...]
            _, vjp = jax.vjp(post, _conv(x, w_ref))
            dc, = vjp(g_ref[...])
            dx = sum(w_ref[pl.ds(j, 1), :] * _shift_up(dc, CONV_WIDTH - 1 - j) for j in range(CONV_WIDTH))
            dx_ref[...] = dx.astype(dx_ref.dtype)
            for j in range(CONV_WIDTH):
                dw_ref[pl.ds(j, 1), :] = jnp.sum(dc * _shift_down(x, CONV_WIDTH - 1 - j), axis=0, keepdims=True)

    x_specs = [pl.BlockSpec((t, HEAD_DIM), lambda h, g=g: (0, g * N_HEADS + h)) for g in range(3)]
    w_specs = [pl.BlockSpec((CONV_WIDTH, HEAD_DIM), lambda h, g=g: (0, g * N_HEADS + h)) for g in range(3)]
    g_spec = pl.BlockSpec((None, t, HEAD_DIM), lambda h: (h, 0, 0))
    dx_spec = pl.BlockSpec((t, HEAD_DIM), lambda h: (0, h))
    dw_spec = pl.BlockSpec((CONV_WIDTH, HEAD_DIM), lambda h: (0, h))
    return hosted_call(
        riders, body, name=name, grid=(N_HEADS,), in_specs=x_specs + w_specs + [g_spec] * 3, out_specs=[dx_spec] * 3 + [dw_spec] * 3,
        out_shape=[jax.ShapeDtypeStruct((t, D_MODEL), BF16)] * 3 + [jax.ShapeDtypeStruct((CONV_WIDTH, D_MODEL), F32)] * 3,
        compiler_params=_params(("parallel",)),
    )(proj, proj, proj, conv_w, conv_w, conv_w, dq, dk, dv)


INTRA_CHUNKS = 4


def _lane_column(x, lane_index):
    lane = lax.broadcasted_iota(jnp.int32, x.shape, 1)
    return jnp.sum(jnp.where(lane == lane_index, x, 0.0), axis=1, keepdims=True)


def _head_columns(g, first_lane):
    return jnp.concatenate([_lane_column(g, first_lane + h)[None] for h in range(N_HEADS)], axis=0)


def _intra_of_gates(q, k, v, gates):
    nb = N_HEADS * (gates.shape[0] // CHUNK)

    def chunks(x):
        return x.reshape(nb, CHUNK, x.shape[-1])

    res = delta_intra(chunks(q), chunks(k), chunks(v), chunks(_head_columns(gates, 0)), chunks(_head_columns(gates, N_HEADS)))
    return tuple(x.reshape(N_HEADS, -1, x.shape[-1]) for x in res)


def _step_of_gates(s, q, k, gates, u, w, qk):
    return delta_step(s, q, k, _head_columns(gates, 0), u, w, qk)


def _head_major(rows, w, index):
    return pl.BlockSpec((N_HEADS, rows, w), lambda i: (0, index(i), 0))


def delta_intra_fwd(q, k, v, gates, *, name, riders=()):
    t = q.shape[1]
    rows = min(INTRA_CHUNKS, t // CHUNK) * CHUNK

    def body(q_ref, k_ref, v_ref, g_ref, u_ref, w_ref, qk_ref):
        for ref, val in zip((u_ref, w_ref, qk_ref), _intra_of_gates(q_ref[...], k_ref[...], v_ref[...], g_ref[...])):
            ref[...] = val

    x_spec, qk_spec = (_head_major(rows, w, lambda i: i) for w in (HEAD_DIM, CHUNK))
    g_spec = pl.BlockSpec((rows, LANES), lambda i: (i, 0))
    return hosted_call(
        riders, body, name=name, grid=(t // rows,), in_specs=[x_spec] * 3 + [g_spec], out_specs=[x_spec, x_spec, qk_spec],
        out_shape=[jax.ShapeDtypeStruct((N_HEADS, t, HEAD_DIM), F32)] * 2 + [jax.ShapeDtypeStruct((N_HEADS, t, CHUNK), F32)],
        compiler_params=_params(("parallel",)),
    )(q, k, v, gates)


def delta_seq_fwd(q, k, gates, u, w, qk, *, name, riders=()):
    t = q.shape[1]
    nc = t // CHUNK

    def body(q_ref, k_ref, g_ref, u_ref, w_ref, qk_ref, o_ref, s0_ref, s_ref):
        @pl.when(pl.program_id(0) == 0)
        def _():
            s_ref[...] = jnp.zeros_like(s_ref)

        s = s_ref[...]
        s0_ref[...] = s
        o, s_new = _step_of_gates(s, q_ref[...], k_ref[...], g_ref[...], u_ref[...], w_ref[...], qk_ref[...])
        o_ref[...] = o
        s_ref[...] = s_new

    x_spec, qk_spec = (_head_major(CHUNK, w, lambda c: c) for w in (HEAD_DIM, CHUNK))
    g_spec = pl.BlockSpec((CHUNK, LANES), lambda c: (c, 0))
    s_spec = pl.BlockSpec((N_HEADS, None, HEAD_DIM, HEAD_DIM), lambda c: (0, c, 0, 0))
    return hosted_call(
        riders, body, name=name, grid=(nc,), in_specs=[x_spec, x_spec, g_spec, x_spec, x_spec, qk_spec], out_specs=[x_spec, s_spec],
        out_shape=[jax.ShapeDtypeStruct((N_HEADS, t, HEAD_DIM), F32),
                   jax.ShapeDtypeStruct((N_HEADS, nc, HEAD_DIM, HEAD_DIM), F32)],
        scratch_shapes=[pltpu.VMEM((N_HEADS, HEAD_DIM, HEAD_DIM), F32)],
        compiler_params=_params(("arbitrary",)),
    )(q, k, gates, u, w, qk)


def delta_seq_bwd(q, k, gates, u, w, qk, s0, do, *, name, riders=()):
    t = q.shape[1]
    nc = t // CHUNK

    def body(q_ref, k_ref, g_ref, u_ref, w_ref, qk_ref, s0_ref, do_ref,
             dq_ref, dk_ref, dg_ref, du_ref, dw_ref, dqk_ref, ds_ref):
        @pl.when(pl.program_id(0) == 0)
        def _():
            ds_ref[...] = jnp.zeros_like(ds_ref)

        _, vjp = jax.vjp(_step_of_gates, s0_ref[...], q_ref[...], k_ref[...], g_ref[...], u_ref[...], w_ref[...], qk_ref[...])
        ds, dq, dk, dg, du, dw, dqk = vjp((do_ref[...], ds_ref[...]))
        for ref, val in zip((ds_ref, dq_ref, dk_ref, dg_ref, du_ref, dw_ref, dqk_ref), (ds, dq, dk, dg, du, dw, dqk)):
            ref[...] = val

    x_spec, qk_spec = (_head_major(CHUNK, w, lambda c: nc - 1 - c) for w in (HEAD_DIM, CHUNK))
    g_spec = pl.BlockSpec((CHUNK, LANES), lambda c: (nc - 1 - c, 0))
    s_spec = pl.BlockSpec((N_HEADS, None, HEAD_DIM, HEAD_DIM), lambda c: (0, nc - 1 - c, 0, 0))
    head_shape = [jax.ShapeDtypeStruct((N_HEADS, t, w_), F32) for w_ in (HEAD_DIM, HEAD_DIM, HEAD_DIM, HEAD_DIM, CHUNK)]
    return hosted_call(
        riders, body, name=name, grid=(nc,), in_specs=[x_spec, x_spec, g_spec, x_spec, x_spec, qk_spec, s_spec, x_spec],
        out_specs=[x_spec, x_spec, g_spec, x_spec, x_spec, qk_spec],
        out_shape=head_shape[:2] + [jax.ShapeDtypeStruct((t, LANES), F32)] + head_shape[2:],
        scratch_shapes=[pltpu.VMEM((N_HEADS, HEAD_DIM, HEAD_DIM), F32)],
        compiler_params=_params(("arbitrary",)),
    )(q, k, gates, u, w, qk, s0, do)


def delta_intra_bwd(q, k, v, gates, du, dw, dqk, dq_s, dk_s, dg_s, *, name, riders=()):
    t = q.shape[1]
    rows = min(INTRA_CHUNKS, t // CHUNK) * CHUNK

    def body(q_ref, k_ref, v_ref, g_ref, du_ref, dw_ref, dqk_ref, dqs_ref, dks_ref, dgs_ref, dq_ref, dk_ref, dv_ref, dg_ref):
        _, vjp = jax.vjp(_intra_of_gates, q_ref[...], k_ref[...], v_ref[...], g_ref[...])
        dq, dk, dv, dg = vjp((du_ref[...], dw_ref[...], dqk_ref[...]))
        dq_ref[...] = dq + dqs_ref[...]
        dk_ref[...] = dk + dks_ref[...]
        dv_ref[...] = dv
        dg_ref[...] = dg + dgs_ref[...]

    x_spec, qk_spec = (_head_major(rows, w, lambda i: i) for w in (HEAD_DIM, CHUNK))
    g_spec = pl.BlockSpec((rows, LANES), lambda i: (i, 0))
    return hosted_call(
        riders, body, name=name, grid=(t // rows,),
        in_specs=[x_spec] * 3 + [g_spec, x_spec, x_spec, qk_spec, x_spec, x_spec, g_spec],
        out_specs=[x_spec] * 3 + [g_spec],
        out_shape=[jax.ShapeDtypeStruct((N_HEADS, t, HEAD_DIM), F32)] * 3 + [jax.ShapeDtypeStruct((t, LANES), F32)],
        compiler_params=_params(("parallel",)),
    )(q, k, v, gates, du, dw, dqk, dq_s, dk_s, dg_s)


_V_BLOCK = 2 * N_HEADS
FOX_GROUPS = 16


def _fox_groups(t):
    nq = t // Q_BLOCK
    per = max(1, nq // FOX_GROUPS)
    return [(g0, per, (g0 + per) * Q_BLOCK) for g0 in range(0, nq, per)]


def fox_attn_fwd(q, k, proj, fq, fk, *, name, riders=()):
    t = q.shape[0]

    def body(q_ref, k_ref, v_ref, fq_ref, fk_ref, o_ref, kb_ref, vb_ref):
        head = pl.program_id(0)
        kb_ref[...] = k_ref[...].astype(BF16)
        vb_ref[...] = v_ref[...].astype(BF16)
        for g0, per, keys in _fox_groups(t):
            def block(j, carry, g0=g0, keys=keys):
                rows = pl.ds((g0 + j) * Q_BLOCK, Q_BLOCK)
                p = fox_probs(q_ref[rows, :].astype(BF16), kb_ref[0:keys, :], _lane_column(fq_ref[rows, :], head),
                              fk_ref[:, 0:keys], (g0 + j) * Q_BLOCK)
                o_ref[rows, :] = jnp.dot(p.astype(BF16), vb_ref[0:keys, :], preferred_element_type=F32)
                return carry
            for j in range(per):
                block(j, 0)

    x_spec = pl.BlockSpec((t, HEAD_DIM), lambda h: (0, h))
    v_spec = pl.BlockSpec((t, HEAD_DIM), lambda h: (0, _V_BLOCK + h))
    fq_spec = pl.BlockSpec((t, LANES), lambda h: (0, 0))
    fk_spec = pl.BlockSpec((None, 1, t), lambda h: (h, 0, 0))
    return hosted_call(
        riders, body, name=name, grid=(N_HEADS,), in_specs=[x_spec, x_spec, v_spec, fq_spec, fk_spec], out_specs=x_spec,
        out_shape=jax.ShapeDtypeStruct((t, D_MODEL), F32), scratch_shapes=[pltpu.VMEM((t, HEAD_DIM), BF16)] * 2,
        compiler_params=_params(("parallel",)),
    )(q, k, proj, fq, fk)


def fox_attn_bwd(q, k, proj, fq, fk, do, *, name, riders=()):
    t = q.shape[0]

    def body(q_ref, k_ref, v_ref, fq_ref, fk_ref, do_ref, dq_ref, dk_ref, dv_out_ref, dfq_ref, dfk_ref, kb_ref, vb_ref, dv_ref):
        head = pl.program_id(0)

        @pl.when(head == 0)
        def _():
            dfq_ref[...] = jnp.zeros_like(dfq_ref)

        kb_ref[...] = k_ref[...].astype(BF16)
        vb_ref[...] = v_ref[...].astype(BF16)
        dk_ref[...] = jnp.zeros_like(dk_ref)
        dv_ref[...] = jnp.zeros_like(dv_ref)
        dfk_ref[...] = jnp.zeros_like(dfk_ref)
        nt = (((1,), (1,)), ((), ()))
        tn = (((0,), (0,)), ((), ()))
        for g0, per, keys in _fox_groups(t):
            def block(j, carry, g0=g0, keys=keys):
                rows = pl.ds((g0 + j) * Q_BLOCK, Q_BLOCK)
                qb, dob = q_ref[rows, :].astype(BF16), do_ref[rows, :].astype(BF16)
                kb, vb = kb_ref[0:keys, :], vb_ref[0:keys, :]
                p = fox_probs(qb, kb, _lane_column(fq_ref[rows, :], head), fk_ref[:, 0:keys], (g0 + j) * Q_BLOCK)
                dp = lax.dot_general(dob, vb, nt, preferred_element_type=F32)
                dz = p * (dp - jnp.sum(dp * p, axis=-1, keepdims=True))
                pb, dzb = p.astype(BF16), dz.astype(BF16)
                dq_ref[rows, :] = jnp.dot(dzb, kb, preferred_element_type=F32)
                lane = lax.broadcasted_iota(jnp.int32, (Q_BLOCK, LANES), 1)
                dfq_ref[rows, :] += jnp.where(lane == head, jnp.sum(dz, axis=-1, keepdims=True), 0.0)
                dk_ref[0:keys, :] += lax.dot_general(dzb, qb, tn, preferred_element_type=F32)
                dv_ref[0:keys, :] += lax.dot_general(pb, dob, tn, preferred_element_type=F32)
                dfk_ref[:, 0:keys] -= jnp.sum(dz, axis=0, keepdims=True)
                return carry
            for j in range(per):
                block(j, 0)
        dv_out_ref[...] = dv_ref[...].astype(dv_out_ref.dtype)

    x_spec = pl.BlockSpec((t, HEAD_DIM), lambda h: (0, h))
    v_spec = pl.BlockSpec((t, HEAD_DIM), lambda h: (0, _V_BLOCK + h))
    fq_spec = pl.BlockSpec((t, LANES), lambda h: (0, 0))
    fk_spec = pl.BlockSpec((None, 1, t), lambda h: (h, 0, 0))
    return hosted_call(
        riders, body, name=name, grid=(N_HEADS,), in_specs=[x_spec, x_spec, v_spec, fq_spec, fk_spec, x_spec],
        out_specs=[x_spec, x_spec, x_spec, fq_spec, fk_spec],
        out_shape=[jax.ShapeDtypeStruct((t, D_MODEL), F32)] * 2 + [jax.ShapeDtypeStruct((t, D_MODEL), BF16)]
        + [jax.ShapeDtypeStruct((t, LANES), F32), jax.ShapeDtypeStruct((N_HEADS, 1, t), F32)],
        scratch_shapes=[pltpu.VMEM((t, HEAD_DIM), BF16)] * 2 + [pltpu.VMEM((t, HEAD_DIM), F32)],
        compiler_params=_params(("arbitrary",)),
    )(q, k, proj, fq, fk, do)


def memkv_fwd(mem, mnw, wkv, mknw, *, name):
    n = mem.shape[0]

    def body(mem_ref, mnw_ref, w_ref, mknw_ref, mk_ref, mv_ref):
        mk, mv = memkv_fn(mem_ref[...], mnw_ref[...], w_ref[...], mknw_ref[...])
        mk_ref[...] = mk
        mv_ref[...] = mv

    return pl.pallas_call(
        body, name=name, out_shape=[jax.ShapeDtypeStruct((n, MEM_WIDTH), F32)] * 2,
        compiler_params=pltpu.CompilerParams(vmem_limit_bytes=VMEM_LIMIT),
    )(mem, mnw, wkv, mknw)


def memkv_bwd(mem, mnw, wkv, mknw, dmk, dmv, *, name):
    def body(mem_ref, mnw_ref, w_ref, mknw_ref, dmk_ref, dmv_ref, dmnw_ref, dw_ref, dmknw_ref):
        f = functools.partial(memkv_fn, mem_ref[...])
        _, vjp = jax.vjp(f, mnw_ref[...], w_ref[...].astype(F32), mknw_ref[...])
        dmnw, dw, dmknw = vjp((dmk_ref[...], dmv_ref[...]))
        dmnw_ref[...] = dmnw
        dw_ref[...] = dw.astype(dw_ref.dtype)
        dmknw_ref[...] = dmknw

    return pl.pallas_call(
        body, name=name,
        out_shape=[jax.ShapeDtypeStruct(mnw.shape, F32), jax.ShapeDtypeStruct(wkv.shape, BF16), jax.ShapeDtypeStruct(mknw.shape, F32)],
        compiler_params=pltpu.CompilerParams(vmem_limit_bytes=VMEM_LIMIT),
    )(mem, mnw, wkv, mknw, dmk, dmv)


def _row(v, width=None):
    v = v.reshape(1, -1)
    if width is not None and v.shape[1] < width:
        v = jnp.pad(v, ((0, 0), (0, width - v.shape[1])))
    return v


def _norm_fwd(x, w, name, riders=()):
    return rows_call(lambda x, w: rms(x, w), [x], [w], [(D_MODEL, BF16)], [], tm=512, name=name, riders=riders)[0]


FF_PIECE = D_FF // N_DEV


def _add(r, x, *rows):
    return r + x


def _norm_rows(r, w):
    return rms(r, w)


def _norm_bwd_post(dh, x, dx_in, w):
    _, vjp = jax.vjp(rms, x, w)
    dx, dw = vjp(dh)
    return dx + dx_in, dw


def _piece(rows, cols, index):
    return pl.BlockSpec((None, rows, cols), lambda i, j, kk: (index(i, j, kk), 0, 0))


def _two_pieces(rows, cols, index):
    return pl.BlockSpec((2, rows, cols), lambda i, j, kk: (index(i, j, kk), 0, 0))


def _loss_post(r, x, tgt):
    e = r + x - tgt
    return e * (1.0 / D_MODEL), jnp.sum(e * e, axis=0, keepdims=True)


def _mlp_fwd(x, h2, w1, w2, layer, riders=(), next_norm_w=None, loss_target=None):
    riders = list(riders) + [None, None]
    u, a1 = matmul(h2, w1, name=f"mlp1_fwd_{layer}", tiles=(None, FF_PIECE, D_MODEL),
                   extra_out=(lambda u: jnp.square(jnp.maximum(u, 0.0)), BF16),
                   b_view=(D_MODEL, D_FF, _piece(D_MODEL, FF_PIECE, lambda i, j, kk: j)), riders=riders[0])
    if loss_target is not None:
        tail = dict(post=_loss_post, post_ins=[x, loss_target], acc=True)
    elif next_norm_w is not None:
        tail = dict(post=_add, post_ins=[x], row_ins=[next_norm_w], extra_out=(_norm_rows, BF16))
    else:
        tail = dict(post=_add, post_ins=[x])
    y = matmul(a1, w2, name=f"mlp2_fwd_{layer}", tiles=(None, D_MODEL, 2 * FF_PIECE),
               b_view=(D_FF, D_MODEL, _two_pieces(FF_PIECE, D_MODEL, lambda i, j, kk: kk)), riders=riders[1], **tail)
    return y, (x, h2, u, a1)


PAIR_SLABS = 2


def pair_sum(g, got, *, name):
    _, rows, cols = g.shape
    tile = _pick(rows, (512, 256, 128))
    slabs = PAIR_SLABS if tile % (PAIR_SLABS * 16) == 0 else 1
    slab = tile // slabs
    c = lax.axis_index("c").astype(jnp.int32).reshape(1)

    def body(c_ref, *refs):
        o_ref = refs[-1]
        for s in range(slabs):
            a, b = refs[s][...].astype(F32), refs[slabs + s][...].astype(F32)
            o_ref[s * slab:(s + 1) * slab] = (a + b).astype(o_ref.dtype)

    def slab_specs(block):
        return [pl.BlockSpec((None, slab, cols), lambda k, i, c_ref, s=s: (block(k, c_ref), i * slabs + s, 0))
                for s in range(slabs)]

    grid_spec = pltpu.PrefetchScalarGridSpec(
        num_scalar_prefetch=1, grid=(4, rows // tile),
        in_specs=slab_specs(lambda k, c_ref: 2 * k + c_ref[0]) + slab_specs(lambda k, c_ref: k),
        out_specs=pl.BlockSpec((None, tile, cols), lambda k, i, c_ref: (k, i, 0)))
    return pl.pallas_call(
        body, name=name, grid_spec=grid_spec, out_shape=jax.ShapeDtypeStruct((4, rows, cols), g.dtype),
        compiler_params=_params(("parallel", "parallel")),
    )(c, *[g] * slabs, *[got] * slabs)


def chip_sums(names, pieces, gots):
    return [pair_sum(a, got, name=f"grads_pair_sum_{n}") for n, a, got in zip(names, pieces, gots)]


def _mlp_bwd(dy, res, n2w, w1, w2, layer, riders=()):
    x, h2, u, a1 = res
    du = matmul(dy, w2, tb=True, name=f"mlp2_dx_{layer}", out_dtype=BF16, tiles=(None, 2 * FF_PIECE, D_MODEL),
                post=lambda r, u: r * (2.0 * jnp.maximum(u, 0.0)), post_ins=[u],
                b_view=(D_MODEL, D_FF, _two_pieces(FF_PIECE, D_MODEL, lambda i, j, kk: j)), riders=riders)
    dw2 = matmul(a1, dy, ta=True, name=f"mlp2_dw_{layer}", out_dtype=BF16, tiles=(FF_PIECE, D_MODEL, None), out_view=(
        w2.shape, _piece(FF_PIECE, D_MODEL, lambda i, j, kk: i)))
    sib2 = sibling_rider([dw2])
    dx, dn2w = matmul(du, w1, tb=True, name=f"mlp1_dx_{layer}", tiles=(None, D_MODEL, FF_PIECE),
                      b_view=(D_FF, D_MODEL, _piece(D_MODEL, FF_PIECE, lambda i, j, kk: kk)),
                      post=_norm_bwd_post, post_ins=[x, dy], row_ins=[n2w], acc=True, riders=[sib2])
    dw1 = matmul(h2, du, ta=True, name=f"mlp1_dw_{layer}", out_dtype=BF16, tiles=(D_MODEL, FF_PIECE, None), out_view=(
        w1.shape, _piece(D_MODEL, FF_PIECE, lambda i, j, kk: j)))
    return dx, dw1, dw2, dn2w, sibling_rider([dw1]), sib2


def _in_proj_dx(dmain, dsmall, w_main, w_small, x, dx_in, n1w, tag, riders=()):
    dh = matmul(dmain, w_main, tb=True, name=f"inproj_dx_main_{tag}", riders=riders)

    def post(r, dh_main, x, dx_in, w):
        return _norm_bwd_post(r + dh_main, x, dx_in, w)

    return matmul(dsmall, w_small, tb=True, name=f"inproj_dx_small_{tag}", tiles=(None, D_MODEL, None),
                  post=post, post_ins=[dh, x, dx_in], row_ins=[n1w], acc=True)


def _in_proj_dw(h, dmain, dsmall, tag):
    dw_main = matmul(h, dmain, ta=True, out_dtype=BF16, name=f"inproj_dw_main_{tag}")
    dw_small = matmul(h, dsmall, ta=True, out_dtype=BF16, name=f"inproj_dw_small_{tag}")
    return dw_main, dw_small


def local_step(x, mem, target, w, m, v):
    t = x.shape[0]
    n_mem = mem.shape[0]
    g = {}

    def wire(a):
        return a.astype(BF16)

    ride_first = gather_rider([wire(w["dn_w_in"][0])])
    fox_w = wire(w["fox_w_in"][0])
    ride_out = gather_rider([wire(w["w_out"][0]), w["dn_conv_w"][0]])
    ride_out_1 = gather_rider([wire(w["w_out"][1])])
    ride_kv = gather_rider([wire(w["w_mem_kv"])])
    ride_mlp1_0 = gather_rider([wire(w["w_mlp1"][0])])
    ride_mlp2_0 = gather_rider([wire(w["w_mlp2"][0])])
    ride_fox_a, ride_fox_b = gather_rider([fox_w[:D_MODEL // 2]]), gather_rider([fox_w[D_MODEL // 2:]])
    ride_mlp_1 = gather_rider([wire(w["w_mlp1"][1]), wire(w["w_mlp2"][1])])
    mnw, mknw = _row(w["mem_norm_w"]), _row(w["mem_k_norm_w"])

    n1w0, n2w0 = _row(w["norm1_w"][0]), _row(w["norm2_w"][0])
    n1w1, n2w1 = _row(w["norm1_w"][1]), _row(w["norm2_w"][1])
    alog, dtb = _row(w["dn_a_log"][0], LANES), _row(w["dn_dt_bias"][0], LANES)
    onw, mqw0 = _row(w["dn_o_norm_w"][0]), _row(w["memq_norm_w"][0])
    x0 = x
    h0 = _norm_fwd(x0, n1w0, "norm1_fwd_0", riders=[ride_first])
    dn_main, dn_ab = in_proj_weights(ride_first.results[0], DN_IN, 2 * N_HEADS)
    pm0 = matmul(h0, dn_main, name="inproj_main_0", riders=[ride_out])
    w_out0 = ride_out.results[0].reshape(OUT_IN, D_MODEL)
    conv_w = ride_out.results[1].transpose(1, 0, 2).reshape(CONV_WIDTH, 3 * D_MODEL)
    ps0 = matmul(h0, dn_ab, name="inproj_small_0")
    gates = rows_call(dn_gates_fn, [ps0], [alog, dtb], [(LANES, F32)], [], tm=512, name="dn_gates_fwd")[0]
    q0, k0, v0 = dn_prep_fwd(pm0, conv_w, name="dn_prep_fwd", riders=[ride_kv])
    w_kv = ride_kv.results[0].reshape(D_MODEL, D_MODEL)
    mk, mv = memkv_fwd(mem, mnw, w_kv, mknw, name="memkv_fwd")
    u0, w0, qk0 = delta_intra_fwd(q0, k0, v0, gates, name="delta_intra_fwd", riders=[ride_mlp1_0])
    o0, s_start = delta_seq_fwd(q0, k0, gates, u0, w0, qk0, name="delta_seq_fwd", riders=[ride_mlp2_0])
    cat0 = rows_call(dn_out_fn, [o0, (pm0, D_MODEL, 3), (pm0, MEM_WIDTH, 8)], [onw, mqw0, mk, mv],
                     [(D_MODEL + MEM_WIDTH, BF16)], [], tm=256, name="dn_out_fwd")[0]
    (w1_0,), (w2_0,) = ride_mlp1_0.results, ride_mlp2_0.results
    x1, h2_0 = matmul(cat0, w_out0, post=_add, post_ins=[x0], row_ins=[n2w0], extra_out=(_norm_rows, BF16),
                      tiles=(None, D_MODEL, None), name="wout_fwd_0")
    (x2, h1), mlp_res0 = _mlp_fwd(x1, h2_0, w1_0, w2_0, 0, riders=[[ride_fox_a], [ride_fox_b]], next_norm_w=n1w1)
    fox_main, fox_f = in_proj_weights(
        jnp.concatenate([ride_fox_a.results[0], ride_fox_b.results[0]], axis=1), FOX_IN, N_HEADS)

    fbias = _row(w["fox_f_bias"][0], LANES)
    qnw, knw, mqw1 = _row(w["fox_q_norm_w"][0]), _row(w["fox_k_norm_w"][0]), _row(w["memq_norm_w"][1])
    pm1 = matmul(h1, fox_main, name="inproj_main_1", riders=[ride_out_1])
    w_out1 = ride_out_1.results[0].reshape(OUT_IN, D_MODEL)
    ps1 = matmul(h1, fox_f, name="inproj_small_1")
    fq = rows_call(fox_fcum_fn, [ps1], [fbias], [(LANES, F32)], [], tm=t, name="fox_fcum_fwd")[0]
    fk = fq[:, :N_HEADS].T[:, None, :]
    q1, k1 = rows_call(fox_qk_fn, [(pm1, D_MODEL, 0), (pm1, D_MODEL, 1)], [qnw, knw], [(D_MODEL, F32)] * 2, [], tm=256,
                       name="fox_qk_fwd")
    o1 = fox_attn_fwd(q1, k1, pm1, fq, fk, name="fox_attn_fwd", riders=[ride_mlp_1])
    cat1 = rows_call(fox_out_fn, [o1, (pm1, D_MODEL, 3), (pm1, MEM_WIDTH, 8)], [mqw1, mk, mv],
                     [(D_MODEL + MEM_WIDTH, BF16)], [], tm=256, name="fox_out_fwd")[0]
    w1_1, w2_1 = ride_mlp_1.results
    x3, h2_1 = matmul(cat1, w_out1, post=_add, post_ins=[x2], row_ins=[n2w1], extra_out=(_norm_rows, BF16),
                      tiles=(None, D_MODEL, None), name="wout_fwd_1")
    (dy, sq), mlp_res1 = _mlp_fwd(x3, h2_1, w1_1, w2_1, 1, loss_target=target)
    loss = jnp.sum(sq) * (0.5 / D_MODEL)

    dx3, dw1_1, dw2_1, dn2w1, sib1, sib2 = _mlp_bwd(dy, mlp_res1, n2w1, w1_1, w2_1, 1)
    dcat1 = matmul(dx3, w_out1, tb=True, name="wout_dx_1", riders=[sib1])
    dwo_1 = matmul(cat1, dx3, ta=True, out_dtype=BF16, name="wout_dw_1").reshape(N_DEV, OUT_IN // N_DEV, D_MODEL)
    sibo = sibling_rider([dwo_1])
    do1, dgate1, dqm1, dmqw1, dmk1, dmv1 = rows_call(
        functools.partial(vjp_rows(fox_out_fn, 3, (True, True, True)), n_row=3, n_ct=1),
        [o1, (pm1, D_MODEL, 3), (pm1, MEM_WIDTH, 8), dcat1], [mqw1, mk, mv],
        [(D_MODEL, F32), (D_MODEL, BF16), (MEM_WIDTH, BF16)], [(1, HEAD_DIM), (n_mem, MEM_WIDTH), (n_mem, MEM_WIDTH)],
        tm=256, name="fox_out_bwd", riders=[sibo])
    ride_l1 = chips_rider(chip_sums(["w_mlp2_1", "w_mlp1_1", "w_out_1"], [dw2_1, dw1_1, dwo_1],
                                    sib2.results + sib1.results + sibo.results))
    dq1, dk1, dv1, dfq, dfk = fox_attn_bwd(q1, k1, pm1, fq, fk, do1, name="fox_attn_bwd", riders=[ride_l1])
    dqraw1, dkraw1, dqnw, dknw = rows_call(
        functools.partial(vjp_rows(fox_qk_fn, 2, (True, True)), n_row=2, n_ct=2),
        [(pm1, D_MODEL, 0), (pm1, D_MODEL, 1), dq1, dk1], [qnw, knw],
        [(D_MODEL, BF16)] * 2, [(1, HEAD_DIM)] * 2, tm=256, name="fox_qk_bwd")
    dfcum = dfq + jnp.pad(dfk[:, 0, :].T, ((0, 0), (0, LANES - N_HEADS)))
    dps1, dfbias = rows_call(
        functools.partial(vjp_rows(fox_fcum_fn, 1, (True,)), n_row=1, n_ct=1),
        [ps1, dfcum], [fbias], [(LANES, F32)], [(1, LANES)], tm=t, name="fox_fcum_bwd")
    dpm1 = jnp.concatenate([dqraw1, dkraw1, dv1, dgate1, dqm1], axis=1)
    dx2, dn1w1 = _in_proj_dx(dpm1, dps1, fox_main, fox_f, x2, dx3, n1w1, "1")
    dwmain1, dwsmall1 = _in_proj_dw(h1, dpm1, dps1, "1")
    g_fox = in_proj_pieces(dwmain1, dwsmall1, N_HEADS, FOX_IN)
    sibf = sibling_rider([g_fox])

    dx1, dw1_0, dw2_0, dn2w0, sib1, sib2 = _mlp_bwd(dx2, mlp_res0, n2w0, w1_0, w2_0, 0, riders=[sibf])
    ride_fox_g = chips_rider(chip_sums(["fox_w_in"], [g_fox], sibf.results))
    dcat0 = matmul(dx1, w_out0, tb=True, name="wout_dx_0", riders=[sib1])
    dwo_0 = matmul(cat0, dx1, ta=True, out_dtype=BF16, name="wout_dw_0").reshape(N_DEV, OUT_IN // N_DEV, D_MODEL)
    sibo = sibling_rider([dwo_0])
    do0, dz0, dqm0, donw, dmqw0, dmk0, dmv0 = rows_call(
        functools.partial(vjp_rows(dn_out_fn, 3, (True, True, True, True)), n_row=3, n_ct=1),
        [o0, (pm0, D_MODEL, 3), (pm0, MEM_WIDTH, 8), dcat0], [onw, mqw0, mk, mv],
        [((N_HEADS, HEAD_DIM), F32), (D_MODEL, BF16), (MEM_WIDTH, BF16)],
        [(1, HEAD_DIM), (1, HEAD_DIM), (n_mem, MEM_WIDTH), (n_mem, MEM_WIDTH)], tm=256, name="dn_out_bwd", riders=[sibo])
    h_l0 = chip_sums(["w_mlp2_0", "w_mlp1_0", "w_out_0"], [dw2_0, dw1_0, dwo_0], sib2.results + sib1.results + sibo.results)
    ride_l0_mlp2, ride_l0_rest = chips_rider(h_l0[:1]), chips_rider(h_l0[1:])
    dmnw, dwkv, dmknw = memkv_bwd(mem, mnw, w_kv, mknw, dmk0 + dmk1, dmv0 + dmv1, name="memkv_bwd")
    g_kv = dwkv.reshape(N_DEV, D_MODEL // N_DEV, D_MODEL)
    sibk = sibling_rider([g_kv])
    dq_s, dk_s, dg_s, du0, dw0, dqk0 = delta_seq_bwd(q0, k0, gates, u0, w0, qk0, s_start, do0, name="delta_seq_bwd",
                                                     riders=[ride_fox_g, sibk])
    ride_kv_g = chips_rider(chip_sums(["w_mem_kv"], [g_kv], sibk.results))
    dq0, dk0, dv0, dgates = delta_intra_bwd(q0, k0, v0, gates, du0, dw0, dqk0, dq_s, dk_s, dg_s,
                                            name="delta_intra_bwd", riders=[ride_l0_mlp2, ride_kv_g])
    dxq, dxk, dxv, dcq, dck, dcv = dn_prep_bwd(pm0, conv_w, dq0, dk0, dv0, name="dn_prep_bwd", riders=[ride_l0_rest])
    dconv = jnp.concatenate([dcq, dck, dcv], axis=1)
    dps0, dalog, ddtb = rows_call(
        functools.partial(vjp_rows(dn_gates_fn, 1, (True, True)), n_row=1, n_ct=1),
        [ps0, dgates], [alog, dtb], [(LANES, F32)], [(1, LANES)] * 2, tm=512, name="dn_gates_bwd")
    dpm0 = jnp.concatenate([dxq, dxk, dxv, dz0, dqm0], axis=1)
    dwmain0, dwsmall0 = _in_proj_dw(h0, dpm0, dps0, "0")
    g_dn = in_proj_pieces(dwmain0, dwsmall0, 2 * N_HEADS, DN_IN)
    g_conv = dconv.reshape(CONV_WIDTH, N_DEV, -1).transpose(1, 0, 2).astype(BF16)
    sibd = sibling_rider([g_dn, g_conv])
    grad_x, dn1w0 = _in_proj_dx(dpm0, dps0, dn_main, dn_ab, x0, dx1, n1w0, "0", riders=[sibd])

    g["mem_norm_w"] = dmnw[0]
    g["mem_k_norm_w"] = dmknw[0]
    g["norm1_w"] = jnp.concatenate([dn1w0, dn1w1], axis=0)
    g["dn_a_log"] = dalog[:, :N_HEADS]
    g["dn_dt_bias"] = ddtb[:, :N_HEADS]
    g["dn_o_norm_w"] = donw
    g["fox_f_bias"] = dfbias[:, :N_HEADS]
    g["fox_q_norm_w"] = dqnw
    g["fox_k_norm_w"] = dknw
    g["memq_norm_w"] = jnp.concatenate([dmqw0, dmqw1], axis=0)
    g["norm2_w"] = jnp.concatenate([dn2w0, dn2w1], axis=0)

    ride_last = chips_rider(chip_sums(["dn_w_in", "dn_conv_w"], [g_dn, g_conv], sibd.results))
    ride_small = gather_rider([pack_small(g, last=loss)])
    run_riders([ride_last, ride_small], name="grads_to_chips_last")

    def layers(l0, l1):
        return jnp.stack([l0, l1], axis=1).reshape(4, -1, l0.shape[-1])

    parts = {
        "w_mlp1": layers(ride_l0_rest.results[0], ride_l1.results[1]),
        "w_mlp2": layers(ride_l0_mlp2.results[0], ride_l1.results[0]),
        "w_out": layers(ride_l0_rest.results[1], ride_l1.results[2]),
        "fox_w_in": ride_fox_g.results[0], "w_mem_kv": ride_kv_g.results[0],
        "dn_w_in": ride_last.results[0], "dn_conv_w": ride_last.results[1],
    }
    out = {n: adamw(parts[n], w[n], m[n], v[n], name=f"adamw_{n}") for n, _, _ in BIG}
    small, loss = adamw_small(ride_small.results[0], w, m, v, name="adamw_small")
    return loss, grad_x, out, small


WEIGHTS = ["mem_norm_w", "w_mem_kv", "mem_k_norm_w", "norm1_w", "dn_w_in", "dn_conv_w", "dn_a_log", "dn_dt_bias",
           "dn_o_norm_w", "fox_w_in", "fox_f_bias", "fox_q_norm_w", "fox_k_norm_w", "memq_norm_w", "w_out", "norm2_w",
           "w_mlp1", "w_mlp2"]
DN_IN = 4 * D_MODEL + 2 * N_HEADS + MEM_WIDTH
FOX_IN = 4 * D_MODEL + N_HEADS + MEM_WIDTH
GATE_END = 4 * D_MODEL
OUT_IN = D_MODEL + MEM_WIDTH
BIG = [("w_mem_kv", D_MODEL // N_DEV, D_MODEL), ("dn_w_in", D_MODEL, DN_IN // N_DEV), ("fox_w_in", D_MODEL, FOX_IN // N_DEV),
       ("dn_conv_w", CONV_WIDTH, 3 * D_MODEL // N_DEV), ("w_out", 2 * OUT_IN // N_DEV, D_MODEL),
       ("w_mlp1", 2 * D_MODEL, FF_PIECE), ("w_mlp2", 2 * FF_PIECE, D_MODEL)]
SMALL_TILE = 8 * LANES
SMALL = [(name, shape, -(-math.prod(shape) // SMALL_TILE) * SMALL_TILE) for name, shape in [
    ("mem_norm_w", (D_MODEL,)), ("mem_k_norm_w", (HEAD_DIM,)), ("norm1_w", (2, D_MODEL)), ("dn_a_log", (1, N_HEADS)),
    ("dn_dt_bias", (1, N_HEADS)), ("dn_o_norm_w", (1, HEAD_DIM)), ("fox_f_bias", (1, N_HEADS)),
    ("fox_q_norm_w", (1, HEAD_DIM)), ("fox_k_norm_w", (1, HEAD_DIM)), ("memq_norm_w", (2, HEAD_DIM)), ("norm2_w", (2, D_MODEL))]]
SMALL_ROWS = sum(ln for _, _, ln in SMALL) // LANES + 8


def pack_small(p, last=None):
    def rows(a, ln):
        a = a.reshape(-1)
        return (a if a.shape[0] == ln else jnp.pad(a, (0, ln - a.shape[0]))).reshape(-1, LANES)

    used = sum(ln for _, _, ln in SMALL) // LANES
    tail = jnp.zeros(((SMALL_ROWS - used) * LANES,), F32)
    if last is not None:
        tail = jnp.concatenate([tail[:-1], last.reshape(1)])
    return jnp.concatenate([rows(p[n], ln) for n, _, ln in SMALL] + [tail.reshape(-1, LANES)], axis=0)


def in_proj_weights(gathered, width, n_small):
    full = gathered.transpose(1, 0, 2).reshape(D_MODEL, width)
    main = jnp.concatenate([full[:, :GATE_END], full[:, GATE_END + n_small:]], axis=1)
    return main, jnp.pad(full[:, GATE_END:GATE_END + n_small], ((0, 0), (0, LANES - n_small)))


def in_proj_pieces(d_main, d_small, n_small, width):
    full = jnp.concatenate([d_main[:, :GATE_END], d_small[:, :n_small], d_main[:, GATE_END:]], axis=1)
    return full.reshape(D_MODEL, N_DEV, width // N_DEV).transpose(1, 0, 2)


def _adamw_update(g, w, m, v):
    m_new = ADAM_B1 * m + (1.0 - ADAM_B1) * g
    v_new = ADAM_B2 * v + (1.0 - ADAM_B2) * jnp.square(g)
    m_hat = m_new / (1.0 - ADAM_B1 ** ADAM_STEP)
    v_hat = v_new / (1.0 - ADAM_B2 ** ADAM_STEP)
    return -ADAM_LR * (m_hat / (jnp.sqrt(v_hat) + ADAM_EPS) + ADAM_WD * w), m_new, v_new


def adamw(parts, w, m, v, *, name):
    n, _, cols = parts.shape
    layers = w.shape[0] if w.ndim == 3 else 1
    rows = w.shape[-2]
    tile = _pick(rows, (512, 256, 128))
    steps = rows // tile

    def body(p_ref, w_ref, m_ref, v_ref, g_ref, d_ref, mo_ref, vo_ref):
        g = p_ref[0].astype(F32)
        for i in range(1, n):
            g = g + p_ref[i].astype(F32)
        g_ref[...] = g
        d_ref[...], mo_ref[...], vo_ref[...] = _adamw_update(g, w_ref[...], m_ref[...], v_ref[...])

    if w.ndim == 3:
        spec = pl.BlockSpec((None, tile, cols), lambda l, i: (l, i, 0))
    else:
        spec = pl.BlockSpec((tile, cols), lambda l, i: (i, 0))
    return pl.pallas_call(
        body, name=name, grid=(layers, steps),
        in_specs=[pl.BlockSpec((n, tile, cols), lambda l, i: (0, l * steps + i, 0)), spec, spec, spec], out_specs=[spec] * 4,
        out_shape=[jax.ShapeDtypeStruct(w.shape, F32)] * 4, compiler_params=_params(("parallel", "parallel")),
    )(parts, w, m, v)


def adamw_small(parts, w, m, v, *, name):
    def view(a):
        return a.reshape(-1, LANES) if a.size % LANES == 0 else a.reshape(1, a.size)

    k = len(SMALL)
    ins = [view(d[n]) for d in (w, m, v) for n, _, _ in SMALL]

    def body(p_ref, *refs):
        w_refs, m_refs, v_refs, outs, g_ref = refs[:k], refs[k:2 * k], refs[2 * k:3 * k], refs[3 * k:-1], refs[-1]
        g_all = p_ref[0]
        for i in range(1, N_DEV):
            g_all = g_all + p_ref[i]
        g_ref[...] = g_all
        row = 0
        for i, (_, _, ln) in enumerate(SMALL):
            r, c = w_refs[i].shape
            g = g_ref[row:row + r, 0:c]
            outs[4 * i][...] = g
            outs[4 * i + 1][...], outs[4 * i + 2][...], outs[4 * i + 3][...] = _adamw_update(
                g, w_refs[i][...], m_refs[i][...], v_refs[i][...])
            row += ln // LANES
        outs[-1][...] = g_ref[SMALL_ROWS - 1:SMALL_ROWS, LANES - 1:LANES]

    out_shape = [jax.ShapeDtypeStruct(a.shape, F32) for a in ins[:k] for _ in range(4)] + [jax.ShapeDtypeStruct((1, 1), F32)]
    res = pl.pallas_call(body, name=name, out_shape=out_shape,
                         scratch_shapes=[pltpu.VMEM((SMALL_ROWS, LANES), F32)])(parts, *ins)
    small = {n: [o.reshape(sh) for o in res[4 * i:4 * i + 4]] for i, (n, sh, _) in enumerate(SMALL)}
    return small, res[-1][0, 0]


def kernel(x, mem, mem_norm_w, w_mem_kv, mem_k_norm_w, norm1_w, dn_w_in, dn_conv_w, dn_a_log, dn_dt_bias, dn_o_norm_w, fox_w_in, fox_f_bias, fox_q_norm_w, fox_k_norm_w, memq_norm_w, w_out, norm2_w, w_mlp1, w_mlp2, loss_target, m_mem_norm_w, m_w_mem_kv, m_mem_k_norm_w, m_norm1_w, m_dn_w_in, m_dn_conv_w, m_dn_a_log, m_dn_dt_bias, m_dn_o_norm_w, m_fox_w_in, m_fox_f_bias, m_fox_q_norm_w, m_fox_k_norm_w, m_memq_norm_w, m_w_out, m_norm2_w, m_w_mlp1, m_w_mlp2, v_mem_norm_w, v_w_mem_kv, v_mem_k_norm_w, v_norm1_w, v_dn_w_in, v_dn_conv_w, v_dn_a_log, v_dn_dt_bias, v_dn_o_norm_w, v_fox_w_in, v_fox_f_bias, v_fox_q_norm_w, v_fox_k_norm_w, v_memq_norm_w, v_w_out, v_norm2_w, v_w_mlp1, v_w_mlp2):
    p = dict(mem_norm_w=mem_norm_w, w_mem_kv=w_mem_kv, mem_k_norm_w=mem_k_norm_w, norm1_w=norm1_w, dn_w_in=dn_w_in,
             dn_conv_w=dn_conv_w, dn_a_log=dn_a_log, dn_dt_bias=dn_dt_bias, dn_o_norm_w=dn_o_norm_w, fox_w_in=fox_w_in,
             fox_f_bias=fox_f_bias, fox_q_norm_w=fox_q_norm_w, fox_k_norm_w=fox_k_norm_w, memq_norm_w=memq_norm_w,
             w_out=w_out, norm2_w=norm2_w, w_mlp1=w_mlp1, w_mlp2=w_mlp2)
    pm = dict(mem_norm_w=m_mem_norm_w, w_mem_kv=m_w_mem_kv, mem_k_norm_w=m_mem_k_norm_w, norm1_w=m_norm1_w,
              dn_w_in=m_dn_w_in, dn_conv_w=m_dn_conv_w, dn_a_log=m_dn_a_log, dn_dt_bias=m_dn_dt_bias,
              dn_o_norm_w=m_dn_o_norm_w, fox_w_in=m_fox_w_in, fox_f_bias=m_fox_f_bias, fox_q_norm_w=m_fox_q_norm_w,
              fox_k_norm_w=m_fox_k_norm_w, memq_norm_w=m_memq_norm_w, w_out=m_w_out, norm2_w=m_norm2_w, w_mlp1=m_w_mlp1,
              w_mlp2=m_w_mlp2)
    pv = dict(mem_norm_w=v_mem_norm_w, w_mem_kv=v_w_mem_kv, mem_k_norm_w=v_mem_k_norm_w, norm1_w=v_norm1_w,
              dn_w_in=v_dn_w_in, dn_conv_w=v_dn_conv_w, dn_a_log=v_dn_a_log, dn_dt_bias=v_dn_dt_bias,
              dn_o_norm_w=v_dn_o_norm_w, fox_w_in=v_fox_w_in, fox_f_bias=v_fox_f_bias, fox_q_norm_w=v_fox_q_norm_w,
              fox_k_norm_w=v_fox_k_norm_w, memq_norm_w=v_memq_norm_w, w_out=v_w_out, norm2_w=v_norm2_w, w_mlp1=v_w_mlp1,
              w_mlp2=v_w_mlp2)

    loss, grad_x, results, small = local_step(x[0], mem[0], loss_target[0], p, pm, pv)
    groups = [{n: r[i] for n, r in {**small, **results}.items()} for i in range(4)]
    return (loss, grad_x[None], *[grp[n] for grp in groups for n in WEIGHTS])
```

```python
import functools
import math

import jax
import jax.numpy as jnp
from jax import lax
from jax.experimental import pallas as pl
from jax.experimental.pallas import tpu as pltpu

F32 = jnp.float32
BF16 = jnp.bfloat16
HIGHEST = lax.Precision.HIGHEST

D_MODEL = 1024
HEAD_DIM = 128
N_HEADS = 8
MEM_HEADS = 4
MEM_WIDTH = MEM_HEADS * HEAD_DIM
D_FF = 4 * D_MODEL
CONV_WIDTH = 4
CHUNK = 64
Q_BLOCK = 128
EPS = 1e-6
SCALE = HEAD_DIM ** -0.5
MAIN_WIDTH = 4 * D_MODEL + MEM_WIDTH
LANES = 128
N_DEV = 8

ADAM_LR = 0.001
ADAM_B1 = 0.9
ADAM_B2 = 0.999
ADAM_EPS = 1e-08
ADAM_WD = 0.01
ADAM_STEP = 10

VMEM_LIMIT = 56 * 2 ** 20
MESH = pl.DeviceIdType.MESH


def _bdot(a, b, dims):
    return lax.dot_general(a.astype(BF16), b.astype(BF16), (dims, ((), ())), preferred_element_type=F32)


@jax.custom_vjp
def mm(a, b):
    return _bdot(a, b, ((1,), (0,)))


@jax.custom_vjp
def mm_nt(a, b):
    return _bdot(a, b, ((1,), (1,)))


@jax.custom_vjp
def mm_tn(a, b):
    return _bdot(a, b, ((0,), (0,)))


mm.defvjp(lambda a, b: (mm(a, b), (a, b)), lambda r, g: (mm_nt(g, r[1]), mm_tn(r[0], g)))
mm_nt.defvjp(lambda a, b: (mm_nt(a, b), (a, b)), lambda r, g: (mm(g, r[1]), mm_tn(g, r[0])))
mm_tn.defvjp(lambda a, b: (mm_tn(a, b), (a, b)), lambda r, g: (mm_nt(r[1], g), mm(r[0], g)))


def hdot(a, b):
    return jnp.dot(a, b, precision=HIGHEST, preferred_element_type=F32)


def rms(x, w):
    return x * lax.rsqrt(jnp.mean(x * x, axis=-1, keepdims=True) + EPS) * w


def l2n(x):
    return x * lax.rsqrt(jnp.sum(x * x, axis=-1, keepdims=True) + EPS)


def _iota2(n, m):
    return lax.broadcasted_iota(jnp.int32, (n, m), 0), lax.broadcasted_iota(jnp.int32, (n, m), 1)


def _lower_ones(n):
    r, c = _iota2(n, n)
    return jnp.where(r >= c, 1.0, 0.0).astype(F32)


def _last_row(x):
    r = lax.broadcasted_iota(jnp.int32, x.shape, 0)
    return jnp.sum(jnp.where(r == x.shape[0] - 1, x, 0.0), axis=0, keepdims=True)


def _softmax_rows(z):
    m = lax.stop_gradient(jnp.max(z, axis=-1, keepdims=True))
    e = jnp.exp(z - m)
    return e * (1.0 / jnp.sum(e, axis=-1, keepdims=True))


_BNN = (((2,), (1,)), ((0,), (0,)))
_BNT = (((2,), (2,)), ((0,), (0,)))
_BTN = (((1,), (1,)), ((0,), (0,)))


def _bbdot(a, b, dims):
    return lax.dot_general(a.astype(BF16), b.astype(BF16), dims, preferred_element_type=F32)


@jax.custom_vjp
def bmm(a, b):
    return _bbdot(a, b, _BNN)


@jax.custom_vjp
def bmm_nt(a, b):
    return _bbdot(a, b, _BNT)


@jax.custom_vjp
def bmm_tn(a, b):
    return _bbdot(a, b, _BTN)


@jax.custom_vjp
def bmm_high(a, b):
    return lax.dot_general(a, b, _BNN, precision=lax.Precision.HIGH, preferred_element_type=F32)


bmm.defvjp(lambda a, b: (bmm(a, b), (a, b)), lambda r, g: (bmm_nt(g, r[1]), bmm_tn(r[0], g)))
bmm_nt.defvjp(lambda a, b: (bmm_nt(a, b), (a, b)), lambda r, g: (bmm(g, r[1]), bmm_tn(g, r[0])))
bmm_tn.defvjp(lambda a, b: (bmm_tn(a, b), (a, b)), lambda r, g: (bmm_nt(r[1], g), bmm(r[0], g)))
bmm_high.defvjp(lambda a, b: (bmm_high(a, b), (a, b)), lambda r, g: (bmm_nt(g, r[1]), bmm_tn(r[0], g)))

NEUMANN_HIGH_LEVELS = 2


@jax.custom_vjp
def inv_unit_lower(a):
    n = a.shape[-1]
    r, c = _iota2(n, n)
    p = jnp.where(r == c, 1.0, 0.0).astype(F32) - a
    ak = a
    for level in range(int(math.log2(n)) - 1):
        dot = bmm_high if level < NEUMANN_HIGH_LEVELS else bmm
        ak = dot(ak, ak)
        p = p + dot(p, ak)
    return p


def _inv_unit_lower_fwd(a):
    t = inv_unit_lower(a)
    return t, t


def _inv_unit_lower_bwd(t, g):
    return (-bmm_tn(t, bmm_nt(g, t)),)


inv_unit_lower.defvjp(_inv_unit_lower_fwd, _inv_unit_lower_bwd)


def delta_intra(q, k, v, gc, beta):
    b, c, _ = q.shape
    r, cc = _iota2(c, c)
    causal = r >= cc
    strict = r > cc
    gi = jnp.broadcast_to(gc, (b, c, c))
    gj = jnp.swapaxes(gi, 1, 2)
    decay = jnp.where(causal, jnp.exp(jnp.where(causal, gi - gj, 0.0)), 0.0)
    kb = k * beta
    a = jnp.where(strict, bmm_nt(kb, k) * decay, 0.0)
    t = inv_unit_lower(a)
    u = bmm(t, v * beta)
    w = bmm(t, kb * jnp.exp(gc))
    qk = jnp.where(causal, bmm_nt(q, k) * decay, 0.0)
    return u, w, qk


def delta_step(s, q, k, gc, u, w, qk):
    v_new = u - bmm(w, s)
    out = bmm(q * jnp.exp(gc), s) + bmm(qk, v_new)
    r = lax.broadcasted_iota(jnp.int32, gc.shape, 1)
    g_last = jnp.sum(jnp.where(r == gc.shape[1] - 1, gc, 0.0), axis=1, keepdims=True)
    k_dec = k * jnp.exp(g_last - gc)
    s_new = s * jnp.exp(g_last) + bmm_tn(k_dec, v_new)
    return out, s_new


def fox_probs(q, k, fq, fk, qpos0):
    s = lax.dot_general(q, k, (((1,), (1,)), ((), ())), preferred_element_type=F32)
    r, c = _iota2(s.shape[0], s.shape[1])
    return _softmax_rows(jnp.where(c <= (r + qpos0), s + (fq - fk), -jnp.inf))


def mem_head(qm, wq, mk, mv):
    p = _softmax_rows(mm_nt(rms(qm, wq) * SCALE, mk))
    return mm(p, mv)


def _heads(x, n):
    return [x[:, h * HEAD_DIM:(h + 1) * HEAD_DIM] for h in range(n)]


def memkv_fn(mem, mnw, wkv, mknw):
    kv = mm(rms(mem, mnw), wkv)
    mk = jnp.concatenate([rms(kh, mknw) for kh in _heads(kv[:, :MEM_WIDTH], MEM_HEADS)], axis=1)
    return mk, kv[:, MEM_WIDTH:]


def dn_gates_fn(ab, alog, dtb):
    g = -jnp.exp(alog) * jax.nn.softplus(ab + dtb)
    low = _lower_ones(CHUNK)
    gc = jnp.concatenate([hdot(low, g[i * CHUNK:(i + 1) * CHUNK]) for i in range(ab.shape[0] // CHUNK)], axis=0)
    lane = lax.broadcasted_iota(jnp.int32, ab.shape, 1)
    return jnp.where(lane < N_HEADS, gc, jax.nn.sigmoid(ab))


def fox_fcum_fn(fp, fbias):
    lf = jax.nn.log_sigmoid(fp + fbias)
    low = _lower_ones(LANES)
    carry = jnp.zeros((1, fp.shape[1]), F32)
    outs = []
    for i in range(fp.shape[0] // LANES):
        cs = hdot(low, lf[i * LANES:(i + 1) * LANES]) + carry
        carry = _last_row(cs)
        outs.append(cs)
    return jnp.concatenate(outs, axis=0)


def fox_qk_fn(qraw, kraw, qnw, knw):
    q = jnp.concatenate([rms(x, qnw) * SCALE for x in _heads(qraw, N_HEADS)], axis=1)
    k = jnp.concatenate([rms(x, knw) for x in _heads(kraw, N_HEADS)], axis=1)
    return q, k


def _mem_out(qm, mqw, mk, mv):
    return [mem_head(a, mqw, b, c) for a, b, c in zip(_heads(qm, MEM_HEADS), _heads(mk, MEM_HEADS), _heads(mv, MEM_HEADS))]


def dn_out_fn(o, z, qm, onw, mqw, mk, mv):
    mix = [rms(a, onw) * jax.nn.silu(b) for a, b in zip(o, _heads(z, N_HEADS))]
    return jnp.concatenate(mix + _mem_out(qm, mqw, mk, mv), axis=1)


def fox_out_fn(o, gate, qm, mqw, mk, mv):
    return jnp.concatenate([o * jax.nn.sigmoid(gate)] + _mem_out(qm, mqw, mk, mv), axis=1)


_HBM = pl.BlockSpec(memory_space=pltpu.HBM)


def _place():
    return lax.axis_index("x"), lax.axis_index("y"), lax.axis_index("c")


class Rider:
    def __init__(self, ins, out_shape, scratch, start, finish):
        self.ins, self.out_shape, self.scratch, self.start, self.finish = list(ins), list(out_shape), list(scratch), start, finish
        self.results = None


def gather_rider(xs):
    n = len(xs)

    def plan(x_refs, out_refs, sems):
        send_sems, recv_sems, local_sems = sems
        x, y, c = _place()
        me, sibling = (x, y, c), (x, y, 1 - c)
        chips = [(1 - x, y), (x, 1 - y), (1 - x, 1 - y)]

        def copy(a, k, block, to, src=None):
            px, py, pc = block
            dst = out_refs[a].at[4 * px + 2 * py + pc]
            return pltpu.make_async_remote_copy(
                src_ref=dst if src is None else src, dst_ref=dst,
                send_sem=send_sems.at[a, k], recv_sem=recv_sems.at[a, k], device_id=to, device_id_type=MESH)

        mine = [pltpu.make_async_copy(x_refs[a], out_refs[a].at[4 * x + 2 * y + c], local_sems.at[a]) for a in range(n)]
        first = [copy(a, 0, me, sibling, src=x_refs[a]) for a in range(n)]
        first += [copy(a, 1 + j, me, (*chip, c), src=x_refs[a]) for j, chip in enumerate(chips) for a in range(n)]
        return copy, me, sibling, chips, mine, first

    def start(x_refs, out_refs, sems):
        _, _, _, _, mine, first = plan(x_refs, out_refs, sems)
        for cp in mine + first:
            cp.start()

    def finish(x_refs, out_refs, sems):
        copy, me, sibling, chips, mine, first = plan(x_refs, out_refs, sems)
        _, _, c = me
        passed = []
        for j, chip in enumerate(chips):
            for a in range(n):
                copy(a, 1 + j, (*chip, c), me).wait_recv()
                passed.append(copy(a, 4 + j, (*chip, c), sibling))
                passed[-1].start()
        for a in range(n):
            copy(a, 0, sibling, me).wait_recv()
        for j, chip in enumerate(chips):
            for a in range(n):
                copy(a, 4 + j, (*chip, 1 - c), me).wait_recv()
        for cp in first + passed:
            cp.wait_send()
        for cp in mine:
            cp.wait()

    return Rider(xs, [jax.ShapeDtypeStruct((N_DEV,) + a.shape, a.dtype) for a in xs],
                 [pltpu.SemaphoreType.DMA((n, 7)), pltpu.SemaphoreType.DMA((n, 7)), pltpu.SemaphoreType.DMA((n,))], start, finish)


def sibling_rider(gs):
    n = len(gs)

    def plan(g_refs, out_refs, sems):
        send_sems, recv_sems = sems
        x, y, c = _place()
        return [pltpu.make_async_remote_copy(
            src_ref=g_refs[a].at[2 * k + 1 - c], dst_ref=out_refs[a].at[k], send_sem=send_sems.at[a, k],
            recv_sem=recv_sems.at[a, k], device_id=(x, y, 1 - c), device_id_type=MESH) for a in range(n) for k in range(4)]

    def start(g_refs, out_refs, sems):
        for cp in plan(g_refs, out_refs, sems):
            cp.start()

    def finish(g_refs, out_refs, sems):
        copies = plan(g_refs, out_refs, sems)
        for cp in copies:
            cp.wait_recv()
        for cp in copies:
            cp.wait_send()

    return Rider(gs, [jax.ShapeDtypeStruct((4,) + g.shape[1:], g.dtype) for g in gs],
                 [pltpu.SemaphoreType.DMA((n, 4)), pltpu.SemaphoreType.DMA((n, 4))], start, finish)


def chips_rider(hs):
    n = len(hs)

    def plan(h_refs, out_refs, sems):
        send_sems, recv_sems, local_sems = sems
        x, y, c = _place()
        mine = 2 * x + y
        chips = [(1 - x, y), (x, 1 - y), (1 - x, 1 - y)]
        keep = [pltpu.make_async_copy(h_refs[a].at[mine], out_refs[a].at[mine], local_sems.at[a]) for a in range(n)]
        sends = [pltpu.make_async_remote_copy(
            src_ref=h_refs[a].at[2 * qx + qy], dst_ref=out_refs[a].at[mine], send_sem=send_sems.at[a, j],
            recv_sem=recv_sems.at[a, j], device_id=(qx, qy, c), device_id_type=MESH)
            for j, (qx, qy) in enumerate(chips) for a in range(n)]
        recvs = [pltpu.make_async_remote_copy(
            src_ref=h_refs[a].at[mine], dst_ref=out_refs[a].at[2 * qx + qy], send_sem=send_sems.at[a, j],
            recv_sem=recv_sems.at[a, j], device_id=(qx, qy, c), device_id_type=MESH)
            for j, (qx, qy) in enumerate(chips) for a in range(n)]
        return keep, sends, recvs

    def start(h_refs, out_refs, sems):
        keep, sends, _ = plan(h_refs, out_refs, sems)
        for cp in keep + sends:
            cp.start()

    def finish(h_refs, out_refs, sems):
        keep, sends, recvs = plan(h_refs, out_refs, sems)
        for cp in recvs:
            cp.wait_recv()
        for cp in sends:
            cp.wait_send()
        for cp in keep:
            cp.wait()

    return Rider(hs, [jax.ShapeDtypeStruct(h.shape, h.dtype) for h in hs],
                 [pltpu.SemaphoreType.DMA((n, 3)), pltpu.SemaphoreType.DMA((n, 3)), pltpu.SemaphoreType.DMA((n,))], start, finish)


def hosted_call(riders, body, *, out_shape, in_specs, out_specs, grid=(), scratch_shapes=(), **kw):
    riders = tuple(riders or ())
    if not riders:
        return pl.pallas_call(body, out_shape=out_shape, in_specs=in_specs, out_specs=out_specs, grid=grid,
                              scratch_shapes=scratch_shapes, **kw)
    single = not isinstance(out_shape, (list, tuple))
    k_out_shape = [out_shape] if single else list(out_shape)
    k_out_specs = [out_specs] if single else list(out_specs)
    n_in, n_out, n_scr = len(in_specs), len(k_out_shape), len(scratch_shapes)
    r_ins = [a for r in riders for a in r.ins]
    r_outs = [s for r in riders for s in r.out_shape]
    r_scr = [s for r in riders for s in r.scratch]

    def full_body(*refs):
        ins = refs[:n_in + len(r_ins)]
        outs = refs[n_in + len(r_ins):n_in + len(r_ins) + n_out + len(r_outs)]
        scr = refs[n_in + len(r_ins) + n_out + len(r_outs):]
        steps = math.prod(grid)
        step = 0
        for d, g in enumerate(grid):
            step = step * g + pl.program_id(d)

        def each(method):
            i0, o0, s0 = n_in, n_out, n_scr
            for r in riders:
                getattr(r, method)(ins[i0:i0 + len(r.ins)], outs[o0:o0 + len(r.out_shape)], scr[s0:s0 + len(r.scratch)])
                i0, o0, s0 = i0 + len(r.ins), o0 + len(r.out_shape), s0 + len(r.scratch)

        if steps == 1:
            each("start")
            body(*ins[:n_in], *outs[:n_out], *scr[:n_scr])
            each("finish")
        else:
            pl.when(step == 0)(lambda: each("start"))
            body(*ins[:n_in], *outs[:n_out], *scr[:n_scr])
            pl.when(step == steps - 1)(lambda: each("finish"))

    call = pl.pallas_call(
        full_body, out_shape=k_out_shape + r_outs, in_specs=list(in_specs) + [_HBM] * len(r_ins),
        out_specs=k_out_specs + [_HBM] * len(r_outs), grid=grid, scratch_shapes=list(scratch_shapes) + r_scr, **kw)

    def run(*args):
        res = call(*args, *r_ins)
        o0 = n_out
        for r in riders:
            r.results = list(res[o0:o0 + len(r.out_shape)])
            o0 += len(r.out_shape)
        return res[0] if single else list(res[:n_out])

    return run


def run_riders(riders, *, name):
    hosted_call(riders, lambda: None, name=name, out_shape=[], in_specs=[], out_specs=[])()
    return [r.results for r in riders]


def _pick(n, cands):
    for c in cands:
        if n % c == 0:
            return c
    return n


def _params(sem):
    return pltpu.CompilerParams(dimension_semantics=sem, vmem_limit_bytes=VMEM_LIMIT)


MATMUL_VMEM_BUDGET = 40 * 2 ** 20


def _matmul_tiles(m, n, k, bytes_a, bytes_b, bytes_mn, fixed):
    fm, fn, fk = fixed if fixed is not None else (None, None, None)

    def options(given, size, cands):
        return [given] if given else ([c for c in cands if size % c == 0] or [size])

    best = None
    for tm in options(fm, m, (2048, 1024, 512, 256, 128)):
        for tn in options(fn, n, (512, 256, 128)):
            for tk in options(fk, k, (2048, 1536, 1024, 512, 256, 128)):
                if 2 * (tm * tk * bytes_a + tk * tn * bytes_b + tm * tn * bytes_mn) + tm * tn * 4 > MATMUL_VMEM_BUDGET:
                    continue
                key = ((m // tm) * (n // tn) * (k // tk), -tk)
                if best is None or key < best[0]:
                    best = (key, (tm, tn, tk))
    assert best is not None, (m, n, k, fixed)
    return best[1]


def matmul(a, b, *, name, ta=False, tb=False, post=None, post_ins=(), row_ins=(), acc=False, extra_out=None,
           out_dtype=F32, tiles=None, b_view=None, out_view=None, riders=()):
    (k, m) = a.shape if ta else a.shape[::-1]
    (kb, n) = b_view[:2] if b_view is not None else (b.shape[::-1] if tb else b.shape)
    assert k == kb, (a.shape, b.shape, ta, tb)
    bytes_mn = sum(p.dtype.itemsize for p in post_ins) + jnp.dtype(out_dtype).itemsize
    bytes_mn += jnp.dtype(extra_out[1]).itemsize if extra_out else 0
    tm, tn, tk = _matmul_tiles(m, n, k, a.dtype.itemsize, b.dtype.itemsize, bytes_mn, tiles)
    assert not acc or tn == n, (name, tn, n)
    nk = k // tk
    dims = ((0,) if ta else (1,), (1,) if tb else (0,))
    n_post, n_row = len(post_ins), len(row_ins)
    n_out = 1 + bool(extra_out) + bool(acc)

    def body(*refs):
        a_ref, b_ref = refs[:2]
        post_refs = refs[2:2 + n_post + n_row]
        o_refs, acc_ref = refs[-1 - n_out:-1], refs[-1]
        first_rows, kk = pl.program_id(0) == 0, pl.program_id(2)

        @pl.when(kk == 0)
        def _():
            acc_ref[...] = jnp.zeros_like(acc_ref)

        b_tile = b_ref[...]
        acc_ref[...] += _bdot(a_ref[...], b_tile.reshape(-1, b_tile.shape[-1]), dims)

        @pl.when(kk == nk - 1)
        def _():
            r = acc_ref[...]
            rows = [p[...] for p in post_refs[n_post:]]
            if post is not None:
                r = post(r, *[p[...] for p in post_refs[:n_post]], *rows)
            if acc:
                r, s = r
                sum_ref = o_refs[-1]

                @pl.when(first_rows)
                def _():
                    sum_ref[...] = s

                @pl.when(jnp.logical_not(first_rows))
                def _():
                    sum_ref[...] += s

            o_refs[0][...] = r.astype(out_dtype)
            if extra_out:
                o_refs[1][...] = extra_out[0](r, *rows).astype(extra_out[1])

    a_spec = pl.BlockSpec((tk, tm), lambda i, j, kk: (kk, i)) if ta else pl.BlockSpec((tm, tk), lambda i, j, kk: (i, kk))
    if b_view is not None:
        b_spec = b_view[2]
    else:
        b_spec = pl.BlockSpec((tn, tk), lambda i, j, kk: (j, kk)) if tb else pl.BlockSpec((tk, tn), lambda i, j, kk: (kk, j))
    mn_spec = pl.BlockSpec((tm, tn), lambda i, j, kk: (i, j))
    row_spec = pl.BlockSpec((1, tn), lambda i, j, kk: (0, j))
    o_shape, o_spec = ((m, n), mn_spec) if out_view is None else out_view
    out_shape = [jax.ShapeDtypeStruct(o_shape, out_dtype)]
    out_specs = [o_spec]
    if extra_out:
        out_shape.append(jax.ShapeDtypeStruct((m, n), extra_out[1]))
        out_specs.append(mn_spec)
    if acc:
        out_shape.append(jax.ShapeDtypeStruct((1, n), F32))
        out_specs.append(row_spec)
    res = hosted_call(
        riders, body, name=name, grid=(m // tm, n // tn, nk),
        in_specs=[a_spec, b_spec] + [mn_spec] * n_post + [row_spec] * n_row, out_specs=out_specs, out_shape=out_shape,
        scratch_shapes=[pltpu.VMEM((tm, tn), F32)],
        compiler_params=_params(("arbitrary" if acc else "parallel", "parallel", "arbitrary")),
    )(a, b, *post_ins, *row_ins)
    return res if n_out > 1 else res[0]


def rows_call(fn, row_ins, full_ins, row_outs, acc_outs, *, tm, name, riders=()):
    row_ins = [r if isinstance(r, tuple) else (r, r.shape[-1], 0) for r in row_ins]
    t = row_ins[0][0].shape[-2]
    tm = min(tm, t)
    n_in = len(row_ins) + len(full_ins)
    n_row = len(row_outs)

    def body(*refs):
        res = fn(*[[r[h] for h in range(r.shape[0])] if (i < len(row_ins) and len(r.shape) == 3) else r[...]
                   for i, r in enumerate(refs[:n_in])])
        res = res if isinstance(res, (tuple, list)) else (res,)
        outs = refs[n_in:]
        for ref, val in zip(outs[:n_row], res[:n_row]):
            if len(ref.shape) == 3:
                for h, vh in enumerate(val):
                    ref[h] = vh.astype(ref.dtype)
            else:
                ref[...] = val.astype(ref.dtype)
        first = pl.program_id(0) == 0
        for ref, val in zip(outs[n_row:], res[n_row:]):
            @pl.when(first)
            def _(ref=ref, val=val):
                ref[...] = val

            @pl.when(jnp.logical_not(first))
            def _(ref=ref, val=val):
                ref[...] += val

    def full_spec(shape):
        return pl.BlockSpec(shape, lambda i, nd=len(shape): (0,) * nd)

    def row_spec(lead, w, cb):
        if lead is None:
            return pl.BlockSpec((tm, w), lambda i: (i, cb))
        return pl.BlockSpec((lead, tm, w), lambda i: (0, i, cb))

    def lead_cols(c):
        return c if isinstance(c, tuple) else (None, c)

    in_specs = [row_spec(a.shape[0] if a.ndim == 3 else None, w, cb) for (a, w, cb) in row_ins]
    in_specs += [full_spec(f.shape) for f in full_ins]
    out_specs = [row_spec(*lead_cols(c), 0) for c, _ in row_outs] + [full_spec(s) for s in acc_outs]
    out_shape = [jax.ShapeDtypeStruct(tuple(d for d in (lead_cols(c)[0], t, lead_cols(c)[1]) if d is not None), dt)
                 for c, dt in row_outs] + [jax.ShapeDtypeStruct(s, F32) for s in acc_outs]
    res = hosted_call(
        riders, body, name=name, grid=(t // tm,), in_specs=in_specs, out_specs=out_specs, out_shape=out_shape,
        compiler_params=_params(("arbitrary",)),
    )(*[r[0] for r in row_ins], *full_ins)
    return res


def vjp_rows(fn, n_diff_row, row_diff_full):
    def bwd(*args, n_row, n_ct):
        prim_rows = args[:n_row]
        cts = args[n_row:n_row + n_ct]
        fulls = args[n_row + n_ct:]
        _, vjp = jax.vjp(fn, *prim_rows, *fulls)
        g = vjp(cts[0] if n_ct == 1 else tuple(cts))
        out = list(g[:n_diff_row])
        out += [gf for gf, d in zip(g[n_row:], row_diff_full) if d]
        return tuple(out)
    return bwd


def _shift_down(x, s):
    if s == 0:
        return x
    t = lax.broadcasted_iota(jnp.int32, x.shape, 0)
    return jnp.where(t >= s, pltpu.roll(x, s, 0), 0.0)


def _shift_up(x, s):
    if s == 0:
        return x
    n = x.shape[0]
    t = lax.broadcasted_iota(jnp.int32, x.shape, 0)
    return jnp.where(t < n - s, pltpu.roll(x, n - s, 0), 0.0)


def _conv(x, w_ref):
    return sum(w_ref[pl.ds(j, 1), :] * _shift_down(x, CONV_WIDTH - 1 - j) for j in range(CONV_WIDTH))


_DN_POST = (lambda c: l2n(jax.nn.silu(c)) * SCALE, lambda c: l2n(jax.nn.silu(c)), jax.nn.silu)


def dn_prep_fwd(proj, conv_w, *, name, riders=()):
    t = proj.shape[0]

    def body(xq, xk, xv, wq, wk, wv, oq, ok, ov):
        for x_ref, w_ref, o_ref, post in zip((xq, xk, xv), (wq, wk, wv), (oq, ok, ov), _DN_POST):
            o_ref[...] = post(_conv(x_ref[...], w_ref))

    x_specs = [pl.BlockSpec((t, HEAD_DIM), lambda h, g=g: (0, g * N_HEADS + h)) for g in range(3)]
    w_specs = [pl.BlockSpec((CONV_WIDTH, HEAD_DIM), lambda h, g=g: (0, g * N_HEADS + h)) for g in range(3)]
    o_spec = pl.BlockSpec((None, t, HEAD_DIM), lambda h: (h, 0, 0))
    return hosted_call(
        riders, body, name=name, grid=(N_HEADS,), in_specs=x_specs + w_specs, out_specs=[o_spec] * 3,
        out_shape=[jax.ShapeDtypeStruct((N_HEADS, t, HEAD_DIM), F32)] * 3, compiler_params=_params(("parallel",)),
    )(proj, proj, proj, conv_w, conv_w, conv_w)


def dn_prep_bwd(proj, conv_w, dq, dk, dv, *, name, riders=()):
    t = proj.shape[0]

    def body(xq, xk, xv, wq, wk, wv, gq, gk, gv, dxq, dxk, dxv, dwq, dwk, dwv):
        for x_ref, w_ref, g_ref, dx_ref, dw_ref, post in zip(
                (xq, xk, xv), (wq, wk, wv), (gq, gk, gv), (dxq, dxk, dxv), (dwq, dwk, dwv), _DN_POST):
            x = x_ref[...]
            _, vjp = jax.vjp(post, _conv(x, w_ref))
            dc, = vjp(g_ref[...])
            dx = sum(w_ref[pl.ds(j, 1), :] * _shift_up(dc, CONV_WIDTH - 1 - j) for j in range(CONV_WIDTH))
            dx_ref[...] = dx.astype(dx_ref.dtype)
            for j in range(CONV_WIDTH):
                dw_ref[pl.ds(j, 1), :] = jnp.sum(dc * _shift_down(x, CONV_WIDTH - 1 - j), axis=0, keepdims=True)

    x_specs = [pl.BlockSpec((t, HEAD_DIM), lambda h, g=g: (0, g * N_HEADS + h)) for g in range(3)]
    w_specs = [pl.BlockSpec((CONV_WIDTH, HEAD_DIM), lambda h, g=g: (0, g * N_HEADS + h)) for g in range(3)]
    g_spec = pl.BlockSpec((None, t, HEAD_DIM), lambda h: (h, 0, 0))
    dx_spec = pl.BlockSpec((t, HEAD_DIM), lambda h: (0, h))
    dw_spec = pl.BlockSpec((CONV_WIDTH, HEAD_DIM), lambda h: (0, h))
    return hosted_call(
        riders, body, name=name, grid=(N_HEADS,), in_specs=x_specs + w_specs + [g_spec] * 3, out_specs=[dx_spec] * 3 + [dw_spec] * 3,
        out_shape=[jax.ShapeDtypeStruct((t, D_MODEL), BF16)] * 3 + [jax.ShapeDtypeStruct((CONV_WIDTH, D_MODEL), F32)] * 3,
        compiler_params=_params(("parallel",)),
    )(proj, proj, proj, conv_w, conv_w, conv_w, dq, dk, dv)


INTRA_CHUNKS = 4


def _lane_column(x, lane_index):
    lane = lax.broadcasted_iota(jnp.int32, x.shape, 1)
    return jnp.sum(jnp.where(lane == lane_index, x, 0.0), axis=1, keepdims=True)


def _head_columns(g, first_lane):
    return jnp.concatenate([_lane_column(g, first_lane + h)[None] for h in range(N_HEADS)], axis=0)


def _intra_of_gates(q, k, v, gates):
    nb = N_HEADS * (gates.shape[0] // CHUNK)

    def chunks(x):
        return x.reshape(nb, CHUNK, x.shape[-1])

    res = delta_intra(chunks(q), chunks(k), chunks(v), chunks(_head_columns(gates, 0)), chunks(_head_columns(gates, N_HEADS)))
    return tuple(x.reshape(N_HEADS, -1, x.shape[-1]) for x in res)


def _step_of_gates(s, q, k, gates, u, w, qk):
    return delta_step(s, q, k, _head_columns(gates, 0), u, w, qk)


def _head_major(rows, w, index):
    return pl.BlockSpec((N_HEADS, rows, w), lambda i: (0, index(i), 0))


def delta_intra_fwd(q, k, v, gates, *, name, riders=()):
    t = q.shape[1]
    rows = min(INTRA_CHUNKS, t // CHUNK) * CHUNK

    def body(q_ref, k_ref, v_ref, g_ref, u_ref, w_ref, qk_ref):
        for ref, val in zip((u_ref, w_ref, qk_ref), _intra_of_gates(q_ref[...], k_ref[...], v_ref[...], g_ref[...])):
            ref[...] = val

    x_spec, qk_spec = (_head_major(rows, w, lambda i: i) for w in (HEAD_DIM, CHUNK))
    g_spec = pl.BlockSpec((rows, LANES), lambda i: (i, 0))
    return hosted_call(
        riders, body, name=name, grid=(t // rows,), in_specs=[x_spec] * 3 + [g_spec], out_specs=[x_spec, x_spec, qk_spec],
        out_shape=[jax.ShapeDtypeStruct((N_HEADS, t, HEAD_DIM), F32)] * 2 + [jax.ShapeDtypeStruct((N_HEADS, t, CHUNK), F32)],
        compiler_params=_params(("parallel",)),
    )(q, k, v, gates)


def delta_seq_fwd(q, k, gates, u, w, qk, *, name, riders=()):
    t = q.shape[1]
    nc = t // CHUNK

    def body(q_ref, k_ref, g_ref, u_ref, w_ref, qk_ref, o_ref, s0_ref, s_ref):
        @pl.when(pl.program_id(0) == 0)
        def _():
            s_ref[...] = jnp.zeros_like(s_ref)

        s = s_ref[...]
        s0_ref[...] = s
        o, s_new = _step_of_gates(s, q_ref[...], k_ref[...], g_ref[...], u_ref[...], w_ref[...], qk_ref[...])
        o_ref[...] = o
        s_ref[...] = s_new

    x_spec, qk_spec = (_head_major(CHUNK, w, lambda c: c) for w in (HEAD_DIM, CHUNK))
    g_spec = pl.BlockSpec((CHUNK, LANES), lambda c: (c, 0))
    s_spec = pl.BlockSpec((N_HEADS, None, HEAD_DIM, HEAD_DIM), lambda c: (0, c, 0, 0))
    return hosted_call(
        riders, body, name=name, grid=(nc,), in_specs=[x_spec, x_spec, g_spec, x_spec, x_spec, qk_spec], out_specs=[x_spec, s_spec],
        out_shape=[jax.ShapeDtypeStruct((N_HEADS, t, HEAD_DIM), F32),
                   jax.ShapeDtypeStruct((N_HEADS, nc, HEAD_DIM, HEAD_DIM), F32)],
        scratch_shapes=[pltpu.VMEM((N_HEADS, HEAD_DIM, HEAD_DIM), F32)],
        compiler_params=_params(("arbitrary",)),
    )(q, k, gates, u, w, qk)


def delta_seq_bwd(q, k, gates, u, w, qk, s0, do, *, name, riders=()):
    t = q.shape[1]
    nc = t // CHUNK

    def body(q_ref, k_ref, g_ref, u_ref, w_ref, qk_ref, s0_ref, do_ref,
             dq_ref, dk_ref, dg_ref, du_ref, dw_ref, dqk_ref, ds_ref):
        @pl.when(pl.program_id(0) == 0)
        def _():
            ds_ref[...] = jnp.zeros_like(ds_ref)

        _, vjp = jax.vjp(_step_of_gates, s0_ref[...], q_ref[...], k_ref[...], g_ref[...], u_ref[...], w_ref[...], qk_ref[...])
        ds, dq, dk, dg, du, dw, dqk = vjp((do_ref[...], ds_ref[...]))
        for ref, val in zip((ds_ref, dq_ref, dk_ref, dg_ref, du_ref, dw_ref, dqk_ref), (ds, dq, dk, dg, du, dw, dqk)):
            ref[...] = val

    x_spec, qk_spec = (_head_major(CHUNK, w, lambda c: nc - 1 - c) for w in (HEAD_DIM, CHUNK))
    g_spec = pl.BlockSpec((CHUNK, LANES), lambda c: (nc - 1 - c, 0))
    s_spec = pl.BlockSpec((N_HEADS, None, HEAD_DIM, HEAD_DIM), lambda c: (0, nc - 1 - c, 0, 0))
    head_shape = [jax.ShapeDtypeStruct((N_HEADS, t, w_), F32) for w_ in (HEAD_DIM, HEAD_DIM, HEAD_DIM, HEAD_DIM, CHUNK)]
    return hosted_call(
        riders, body, name=name, grid=(nc,), in_specs=[x_spec, x_spec, g_spec, x_spec, x_spec, qk_spec, s_spec, x_spec],
        out_specs=[x_spec, x_spec, g_spec, x_spec, x_spec, qk_spec],
        out_shape=head_shape[:2] + [jax.ShapeDtypeStruct((t, LANES), F32)] + head_shape[2:],
        scratch_shapes=[pltpu.VMEM((N_HEADS, HEAD_DIM, HEAD_DIM), F32)],
        compiler_params=_params(("arbitrary",)),
    )(q, k, gates, u, w, qk, s0, do)


def delta_intra_bwd(q, k, v, gates, du, dw, dqk, dq_s, dk_s, dg_s, *, name, riders=()):
    t = q.shape[1]
    rows = min(INTRA_CHUNKS, t // CHUNK) * CHUNK

    def body(q_ref, k_ref, v_ref, g_ref, du_ref, dw_ref, dqk_ref, dqs_ref, dks_ref, dgs_ref, dq_ref, dk_ref, dv_ref, dg_ref):
        _, vjp = jax.vjp(_intra_of_gates, q_ref[...], k_ref[...], v_ref[...], g_ref[...])
        dq, dk, dv, dg = vjp((du_ref[...], dw_ref[...], dqk_ref[...]))
        dq_ref[...] = dq + dqs_ref[...]
        dk_ref[...] = dk + dks_ref[...]
        dv_ref[...] = dv
        dg_ref[...] = dg + dgs_ref[...]

    x_spec, qk_spec = (_head_major(rows, w, lambda i: i) for w in (HEAD_DIM, CHUNK))
    g_spec = pl.BlockSpec((rows, LANES), lambda i: (i, 0))
    return hosted_call(
        riders, body, name=name, grid=(t // rows,),
        in_specs=[x_spec] * 3 + [g_spec, x_spec, x_spec, qk_spec, x_spec, x_spec, g_spec],
        out_specs=[x_spec] * 3 + [g_spec],
        out_shape=[jax.ShapeDtypeStruct((N_HEADS, t, HEAD_DIM), F32)] * 3 + [jax.ShapeDtypeStruct((t, LANES), F32)],
        compiler_params=_params(("parallel",)),
    )(q, k, v, gates, du, dw, dqk, dq_s, dk_s, dg_s)


_V_BLOCK = 2 * N_HEADS
FOX_GROUPS = 16


def _fox_groups(t):
    nq = t // Q_BLOCK
    per = max(1, nq // FOX_GROUPS)
    return [(g0, per, (g0 + per) * Q_BLOCK) for g0 in range(0, nq, per)]


def fox_attn_fwd(q, k, proj, fq, fk, *, name, riders=()):
    t = q.shape[0]

    def body(q_ref, k_ref, v_ref, fq_ref, fk_ref, o_ref, kb_ref, vb_ref):
        head = pl.program_id(0)
        kb_ref[...] = k_ref[...].astype(BF16)
        vb_ref[...] = v_ref[...].astype(BF16)
        for g0, per, keys in _fox_groups(t):
            def block(j, carry, g0=g0, keys=keys):
                rows = pl.ds((g0 + j) * Q_BLOCK, Q_BLOCK)
                p = fox_probs(q_ref[rows, :].astype(BF16), kb_ref[0:keys, :], _lane_column(fq_ref[rows, :], head),
                              fk_ref[:, 0:keys], (g0 + j) * Q_BLOCK)
                o_ref[rows, :] = jnp.dot(p.astype(BF16), vb_ref[0:keys, :], preferred_element_type=F32)
                return carry
            for j in range(per):
                block(j, 0)

    x_spec = pl.BlockSpec((t, HEAD_DIM), lambda h: (0, h))
    v_spec = pl.BlockSpec((t, HEAD_DIM), lambda h: (0, _V_BLOCK + h))
    fq_spec = pl.BlockSpec((t, LANES), lambda h: (0, 0))
    fk_spec = pl.BlockSpec((None, 1, t), lambda h: (h, 0, 0))
    return hosted_call(
        riders, body, name=name, grid=(N_HEADS,), in_specs=[x_spec, x_spec, v_spec, fq_spec, fk_spec], out_specs=x_spec,
        out_shape=jax.ShapeDtypeStruct((t, D_MODEL), F32), scratch_shapes=[pltpu.VMEM((t, HEAD_DIM), BF16)] * 2,
        compiler_params=_params(("parallel",)),
    )(q, k, proj, fq, fk)


def fox_attn_bwd(q, k, proj, fq, fk, do, *, name, riders=()):
    t = q.shape[0]

    def body(q_ref, k_ref, v_ref, fq_ref, fk_ref, do_ref, dq_ref, dk_ref, dv_out_ref, dfq_ref, dfk_ref, kb_ref, vb_ref, dv_ref):
        head = pl.program_id(0)

        @pl.when(head == 0)
        def _():
            dfq_ref[...] = jnp.zeros_like(dfq_ref)

        kb_ref[...] = k_ref[...].astype(BF16)
        vb_ref[...] = v_ref[...].astype(BF16)
        dk_ref[...] = jnp.zeros_like(dk_ref)
        dv_ref[...] = jnp.zeros_like(dv_ref)
        dfk_ref[...] = jnp.zeros_like(dfk_ref)
        nt = (((1,), (1,)), ((), ()))
        tn = (((0,), (0,)), ((), ()))
        for g0, per, keys in _fox_groups(t):
            def block(j, carry, g0=g0, keys=keys):
                rows = pl.ds((g0 + j) * Q_BLOCK, Q_BLOCK)
                qb, dob = q_ref[rows, :].astype(BF16), do_ref[rows, :].astype(BF16)
                kb, vb = kb_ref[0:keys, :], vb_ref[0:keys, :]
                p = fox_probs(qb, kb, _lane_column(fq_ref[rows, :], head), fk_ref[:, 0:keys], (g0 + j) * Q_BLOCK)
                dp = lax.dot_general(dob, vb, nt, preferred_element_type=F32)
                dz = p * (dp - jnp.sum(dp * p, axis=-1, keepdims=True))
                pb, dzb = p.astype(BF16), dz.astype(BF16)
                dq_ref[rows, :] = jnp.dot(dzb, kb, preferred_element_type=F32)
                lane = lax.broadcasted_iota(jnp.int32, (Q_BLOCK, LANES), 1)
                dfq_ref[rows, :] += jnp.where(lane == head, jnp.sum(dz, axis=-1, keepdims=True), 0.0)
                dk_ref[0:keys, :] += lax.dot_general(dzb, qb, tn, preferred_element_type=F32)
                dv_ref[0:keys, :] += lax.dot_general(pb, dob, tn, preferred_element_type=F32)
                dfk_ref[:, 0:keys] -= jnp.sum(dz, axis=0, keepdims=True)
                return carry
            for j in range(per):
                block(j, 0)
        dv_out_ref[...] = dv_ref[...].astype(dv_out_ref.dtype)

    x_spec = pl.BlockSpec((t, HEAD_DIM), lambda h: (0, h))
    v_spec = pl.BlockSpec((t, HEAD_DIM), lambda h: (0, _V_BLOCK + h))
    fq_spec = pl.BlockSpec((t, LANES), lambda h: (0, 0))
    fk_spec = pl.BlockSpec((None, 1, t), lambda h: (h, 0, 0))
    return hosted_call(
        riders, body, name=name, grid=(N_HEADS,), in_specs=[x_spec, x_spec, v_spec, fq_spec, fk_spec, x_spec],
        out_specs=[x_spec, x_spec, x_spec, fq_spec, fk_spec],
        out_shape=[jax.ShapeDtypeStruct((t, D_MODEL), F32)] * 2 + [jax.ShapeDtypeStruct((t, D_MODEL), BF16)]
        + [jax.ShapeDtypeStruct((t, LANES), F32), jax.ShapeDtypeStruct((N_HEADS, 1, t), F32)],
        scratch_shapes=[pltpu.VMEM((t, HEAD_DIM), BF16)] * 2 + [pltpu.VMEM((t, HEAD_DIM), F32)],
        compiler_params=_params(("arbitrary",)),
    )(q, k, proj, fq, fk, do)


def memkv_fwd(mem, mnw, wkv, mknw, *, name):
    n = mem.shape[0]

    def body(mem_ref, mnw_ref, w_ref, mknw_ref, mk_ref, mv_ref):
        mk, mv = memkv_fn(mem_ref[...], mnw_ref[...], w_ref[...], mknw_ref[...])
        mk_ref[...] = mk
        mv_ref[...] = mv

    return pl.pallas_call(
        body, name=name, out_shape=[jax.ShapeDtypeStruct((n, MEM_WIDTH), F32)] * 2,
        compiler_params=pltpu.CompilerParams(vmem_limit_bytes=VMEM_LIMIT),
    )(mem, mnw, wkv, mknw)


def memkv_bwd(mem, mnw, wkv, mknw, dmk, dmv, *, name):
    def body(mem_ref, mnw_ref, w_ref, mknw_ref, dmk_ref, dmv_ref, dmnw_ref, dw_ref, dmknw_ref):
        f = functools.partial(memkv_fn, mem_ref[...])
        _, vjp = jax.vjp(f, mnw_ref[...], w_ref[...].astype(F32), mknw_ref[...])
        dmnw, dw, dmknw = vjp((dmk_ref[...], dmv_ref[...]))
        dmnw_ref[...] = dmnw
        dw_ref[...] = dw.astype(dw_ref.dtype)
        dmknw_ref[...] = dmknw

    return pl.pallas_call(
        body, name=name,
        out_shape=[jax.ShapeDtypeStruct(mnw.shape, F32), jax.ShapeDtypeStruct(wkv.shape, BF16), jax.ShapeDtypeStruct(mknw.shape, F32)],
        compiler_params=pltpu.CompilerParams(vmem_limit_bytes=VMEM_LIMIT),
    )(mem, mnw, wkv, mknw, dmk, dmv)


def _row(v, width=None):
    v = v.reshape(1, -1)
    if width is not None and v.shape[1] < width:
        v = jnp.pad(v, ((0, 0), (0, width - v.shape[1])))
    return v


def _norm_fwd(x, w, name, riders=()):
    return rows_call(lambda x, w: rms(x, w), [x], [w], [(D_MODEL, BF16)], [], tm=512, name=name, riders=riders)[0]


FF_PIECE = D_FF // N_DEV


def _add(r, x, *rows):
    return r + x


def _norm_rows(r, w):
    return rms(r, w)


def _norm_bwd_post(dh, x, dx_in, w):
    _, vjp = jax.vjp(rms, x, w)
    dx, dw = vjp(dh)
    return dx + dx_in, dw


def _piece(rows, cols, index):
    return pl.BlockSpec((None, rows, cols), lambda i, j, kk: (index(i, j, kk), 0, 0))


def _two_pieces(rows, cols, index):
    return pl.BlockSpec((2, rows, cols), lambda i, j, kk: (index(i, j, kk), 0, 0))


def _loss_post(r, x, tgt):
    e = r + x - tgt
    return e * (1.0 / D_MODEL), jnp.sum(e * e, axis=0, keepdims=True)


def _mlp_fwd(x, h2, w1, w2, layer, riders=(), next_norm_w=None, loss_target=None):
    riders = list(riders) + [None, None]
    u, a1 = matmul(h2, w1, name=f"mlp1_fwd_{layer}", tiles=(None, FF_PIECE, D_MODEL),
                   extra_out=(lambda u: jnp.square(jnp.maximum(u, 0.0)), BF16),
                   b_view=(D_MODEL, D_FF, _piece(D_MODEL, FF_PIECE, lambda i, j, kk: j)), riders=riders[0])
    if loss_target is not None:
        tail = dict(post=_loss_post, post_ins=[x, loss_target], acc=True)
    elif next_norm_w is not None:
        tail = dict(post=_add, post_ins=[x], row_ins=[next_norm_w], extra_out=(_norm_rows, BF16))
    else:
        tail = dict(post=_add, post_ins=[x])
    y = matmul(a1, w2, name=f"mlp2_fwd_{layer}", tiles=(None, D_MODEL, 2 * FF_PIECE),
               b_view=(D_FF, D_MODEL, _two_pieces(FF_PIECE, D_MODEL, lambda i, j, kk: kk)), riders=riders[1], **tail)
    return y, (x, h2, u, a1)


def chip_sums(names, pieces, gots):
    n = len(pieces)
    c = lax.axis_index("c").astype(jnp.int32).reshape(1)

    def body(c_ref, *refs):
        for a_ref, b_ref, o_ref in zip(refs[:n], refs[n:2 * n], refs[2 * n:]):
            o_ref[...] = (a_ref[...].astype(F32) + b_ref[...].astype(F32)).astype(o_ref.dtype)

    def slot(a):
        return (None,) + a.shape[1:]

    grid_spec = pltpu.PrefetchScalarGridSpec(
        num_scalar_prefetch=1, grid=(4,),
        in_specs=[pl.BlockSpec(slot(a), lambda k, c_ref: (2 * k + c_ref[0], 0, 0)) for a in pieces]
        + [pl.BlockSpec(slot(a), lambda k, c_ref: (k, 0, 0)) for a in pieces],
        out_specs=[pl.BlockSpec(slot(a), lambda k, c_ref: (k, 0, 0)) for a in pieces])
    return pl.pallas_call(
        body, name="grads_pair_sum_" + "_".join(names), grid_spec=grid_spec,
        out_shape=[jax.ShapeDtypeStruct((4,) + a.shape[1:], a.dtype) for a in pieces],
        compiler_params=_params(("parallel",)),
    )(c, *pieces, *gots)


def _mlp_bwd(dy, res, n2w, w1, w2, layer, riders=()):
    x, h2, u, a1 = res
    du = matmul(dy, w2, tb=True, name=f"mlp2_dx_{layer}", out_dtype=BF16, tiles=(None, 2 * FF_PIECE, D_MODEL),
                post=lambda r, u: r * (2.0 * jnp.maximum(u, 0.0)), post_ins=[u],
                b_view=(D_MODEL, D_FF, _two_pieces(FF_PIECE, D_MODEL, lambda i, j, kk: j)), riders=riders)
    dw2 = matmul(a1, dy, ta=True, name=f"mlp2_dw_{layer}", out_dtype=BF16, tiles=(FF_PIECE, D_MODEL, None), out_view=(
        w2.shape, _piece(FF_PIECE, D_MODEL, lambda i, j, kk: i)))
    sib2 = sibling_rider([dw2])
    dx, dn2w = matmul(du, w1, tb=True, name=f"mlp1_dx_{layer}", tiles=(None, D_MODEL, FF_PIECE),
                      b_view=(D_FF, D_MODEL, _piece(D_MODEL, FF_PIECE, lambda i, j, kk: kk)),
                      post=_norm_bwd_post, post_ins=[x, dy], row_ins=[n2w], acc=True, riders=[sib2])
    dw1 = matmul(h2, du, ta=True, name=f"mlp1_dw_{layer}", out_dtype=BF16, tiles=(D_MODEL, FF_PIECE, None), out_view=(
        w1.shape, _piece(D_MODEL, FF_PIECE, lambda i, j, kk: j)))
    return dx, dw1, dw2, dn2w, sibling_rider([dw1]), sib2


def _in_proj_dx(dmain, dsmall, w_main, w_small, x, dx_in, n1w, tag, riders=()):
    dh = matmul(dmain, w_main, tb=True, name=f"inproj_dx_main_{tag}", riders=riders)

    def post(r, dh_main, x, dx_in, w):
        return _norm_bwd_post(r + dh_main, x, dx_in, w)

    return matmul(dsmall, w_small, tb=True, name=f"inproj_dx_small_{tag}", tiles=(None, D_MODEL, None),
                  post=post, post_ins=[dh, x, dx_in], row_ins=[n1w], acc=True)


def _in_proj_dw(h, dmain, dsmall, tag):
    dw_main = matmul(h, dmain, ta=True, out_dtype=BF16, name=f"inproj_dw_main_{tag}")
    dw_small = matmul(h, dsmall, ta=True, out_dtype=BF16, name=f"inproj_dw_small_{tag}")
    return dw_main, dw_small


def local_step(x, mem, target, w, m, v):
    t = x.shape[0]
    n_mem = mem.shape[0]
    g = {}

    def wire(a):
        return a.astype(BF16)

    ride_first = gather_rider([wire(w["dn_w_in"][0])])
    fox_w = wire(w["fox_w_in"][0])
    ride_out = gather_rider([wire(w["w_out"][0]), w["dn_conv_w"][0]])
    ride_out_1 = gather_rider([wire(w["w_out"][1])])
    ride_kv = gather_rider([wire(w["w_mem_kv"])])
    ride_mlp1_0 = gather_rider([wire(w["w_mlp1"][0])])
    ride_mlp2_0 = gather_rider([wire(w["w_mlp2"][0])])
    ride_fox_a, ride_fox_b = gather_rider([fox_w[:D_MODEL // 2]]), gather_rider([fox_w[D_MODEL // 2:]])
    ride_mlp_1 = gather_rider([wire(w["w_mlp1"][1]), wire(w["w_mlp2"][1])])
    mnw, mknw = _row(w["mem_norm_w"]), _row(w["mem_k_norm_w"])

    n1w0, n2w0 = _row(w["norm1_w"][0]), _row(w["norm2_w"][0])
    n1w1, n2w1 = _row(w["norm1_w"][1]), _row(w["norm2_w"][1])
    alog, dtb = _row(w["dn_a_log"][0], LANES), _row(w["dn_dt_bias"][0], LANES)
    onw, mqw0 = _row(w["dn_o_norm_w"][0]), _row(w["memq_norm_w"][0])
    x0 = x
    h0 = _norm_fwd(x0, n1w0, "norm1_fwd_0", riders=[ride_first])
    dn_main, dn_ab = in_proj_weights(ride_first.results[0], DN_IN, 2 * N_HEADS)
    pm0 = matmul(h0, dn_main, name="inproj_main_0", riders=[ride_out])
    w_out0 = ride_out.results[0].reshape(OUT_IN, D_MODEL)
    conv_w = ride_out.results[1].transpose(1, 0, 2).reshape(CONV_WIDTH, 3 * D_MODEL)
    ps0 = matmul(h0, dn_ab, name="inproj_small_0")
    gates = rows_call(dn_gates_fn, [ps0], [alog, dtb], [(LANES, F32)], [], tm=512, name="dn_gates_fwd")[0]
    q0, k0, v0 = dn_prep_fwd(pm0, conv_w, name="dn_prep_fwd", riders=[ride_kv])
    w_kv = ride_kv.results[0].reshape(D_MODEL, D_MODEL)
    mk, mv = memkv_fwd(mem, mnw, w_kv, mknw, name="memkv_fwd")
    u0, w0, qk0 = delta_intra_fwd(q0, k0, v0, gates, name="delta_intra_fwd", riders=[ride_mlp1_0])
    o0, s_start = delta_seq_fwd(q0, k0, gates, u0, w0, qk0, name="delta_seq_fwd", riders=[ride_mlp2_0])
    cat0 = rows_call(dn_out_fn, [o0, (pm0, D_MODEL, 3), (pm0, MEM_WIDTH, 8)], [onw, mqw0, mk, mv],
                     [(D_MODEL + MEM_WIDTH, BF16)], [], tm=256, name="dn_out_fwd")[0]
    (w1_0,), (w2_0,) = ride_mlp1_0.results, ride_mlp2_0.results
    x1, h2_0 = matmul(cat0, w_out0, post=_add, post_ins=[x0], row_ins=[n2w0], extra_out=(_norm_rows, BF16),
                      tiles=(None, D_MODEL, None), name="wout_fwd_0")
    (x2, h1), mlp_res0 = _mlp_fwd(x1, h2_0, w1_0, w2_0, 0, riders=[[ride_fox_a], [ride_fox_b]], next_norm_w=n1w1)
    fox_main, fox_f = in_proj_weights(
        jnp.concatenate([ride_fox_a.results[0], ride_fox_b.results[0]], axis=1), FOX_IN, N_HEADS)

    fbias = _row(w["fox_f_bias"][0], LANES)
    qnw, knw, mqw1 = _row(w["fox_q_norm_w"][0]), _row(w["fox_k_norm_w"][0]), _row(w["memq_norm_w"][1])
    pm1 = matmul(h1, fox_main, name="inproj_main_1", riders=[ride_out_1])
    w_out1 = ride_out_1.results[0].reshape(OUT_IN, D_MODEL)
    ps1 = matmul(h1, fox_f, name="inproj_small_1")
    fq = rows_call(fox_fcum_fn, [ps1], [fbias], [(LANES, F32)], [], tm=t, name="fox_fcum_fwd")[0]
    fk = fq[:, :N_HEADS].T[:, None, :]
    q1, k1 = rows_call(fox_qk_fn, [(pm1, D_MODEL, 0), (pm1, D_MODEL, 1)], [qnw, knw], [(D_MODEL, F32)] * 2, [], tm=256,
                       name="fox_qk_fwd")
    o1 = fox_attn_fwd(q1, k1, pm1, fq, fk, name="fox_attn_fwd", riders=[ride_mlp_1])
    cat1 = rows_call(fox_out_fn, [o1, (pm1, D_MODEL, 3), (pm1, MEM_WIDTH, 8)], [mqw1, mk, mv],
                     [(D_MODEL + MEM_WIDTH, BF16)], [], tm=256, name="fox_out_fwd")[0]
    w1_1, w2_1 = ride_mlp_1.results
    x3, h2_1 = matmul(cat1, w_out1, post=_add, post_ins=[x2], row_ins=[n2w1], extra_out=(_norm_rows, BF16),
                      tiles=(None, D_MODEL, None), name="wout_fwd_1")
    (dy, sq), mlp_res1 = _mlp_fwd(x3, h2_1, w1_1, w2_1, 1, loss_target=target)
    loss = jnp.sum(sq) * (0.5 / D_MODEL)

    dx3, dw1_1, dw2_1, dn2w1, sib1, sib2 = _mlp_bwd(dy, mlp_res1, n2w1, w1_1, w2_1, 1)
    dcat1 = matmul(dx3, w_out1, tb=True, name="wout_dx_1", riders=[sib1])
    dwo_1 = matmul(cat1, dx3, ta=True, out_dtype=BF16, name="wout_dw_1").reshape(N_DEV, OUT_IN // N_DEV, D_MODEL)
    sibo = sibling_rider([dwo_1])
    do1, dgate1, dqm1, dmqw1, dmk1, dmv1 = rows_call(
        functools.partial(vjp_rows(fox_out_fn, 3, (True, True, True)), n_row=3, n_ct=1),
        [o1, (pm1, D_MODEL, 3), (pm1, MEM_WIDTH, 8), dcat1], [mqw1, mk, mv],
        [(D_MODEL, F32), (D_MODEL, BF16), (MEM_WIDTH, BF16)], [(1, HEAD_DIM), (n_mem, MEM_WIDTH), (n_mem, MEM_WIDTH)],
        tm=256, name="fox_out_bwd", riders=[sibo])
    ride_l1 = chips_rider(chip_sums(["w_mlp2_1", "w_mlp1_1", "w_out_1"], [dw2_1, dw1_1, dwo_1],
                                    sib2.results + sib1.results + sibo.results))
    dq1, dk1, dv1, dfq, dfk = fox_attn_bwd(q1, k1, pm1, fq, fk, do1, name="fox_attn_bwd", riders=[ride_l1])
    dqraw1, dkraw1, dqnw, dknw = rows_call(
        functools.partial(vjp_rows(fox_qk_fn, 2, (True, True)), n_row=2, n_ct=2),
        [(pm1, D_MODEL, 0), (pm1, D_MODEL, 1), dq1, dk1], [qnw, knw],
        [(D_MODEL, BF16)] * 2, [(1, HEAD_DIM)] * 2, tm=256, name="fox_qk_bwd")
    dfcum = dfq + jnp.pad(dfk[:, 0, :].T, ((0, 0), (0, LANES - N_HEADS)))
    dps1, dfbias = rows_call(
        functools.partial(vjp_rows(fox_fcum_fn, 1, (True,)), n_row=1, n_ct=1),
        [ps1, dfcum], [fbias], [(LANES, F32)], [(1, LANES)], tm=t, name="fox_fcum_bwd")
    dpm1 = jnp.concatenate([dqraw1, dkraw1, dv1, dgate1, dqm1], axis=1)
    dx2, dn1w1 = _in_proj_dx(dpm1, dps1, fox_main, fox_f, x2, dx3, n1w1, "1")
    dwmain1, dwsmall1 = _in_proj_dw(h1, dpm1, dps1, "1")
    g_fox = in_proj_pieces(dwmain1, dwsmall1, N_HEADS, FOX_IN)
    sibf = sibling_rider([g_fox])

    dx1, dw1_0, dw2_0, dn2w0, sib1, sib2 = _mlp_bwd(dx2, mlp_res0, n2w0, w1_0, w2_0, 0, riders=[sibf])
    ride_fox_g = chips_rider(chip_sums(["fox_w_in"], [g_fox], sibf.results))
    dcat0 = matmul(dx1, w_out0, tb=True, name="wout_dx_0", riders=[sib1])
    dwo_0 = matmul(cat0, dx1, ta=True, out_dtype=BF16, name="wout_dw_0").reshape(N_DEV, OUT_IN // N_DEV, D_MODEL)
    sibo = sibling_rider([dwo_0])
    do0, dz0, dqm0, donw, dmqw0, dmk0, dmv0 = rows_call(
        functools.partial(vjp_rows(dn_out_fn, 3, (True, True, True, True)), n_row=3, n_ct=1),
        [o0, (pm0, D_MODEL, 3), (pm0, MEM_WIDTH, 8), dcat0], [onw, mqw0, mk, mv],
        [((N_HEADS, HEAD_DIM), F32), (D_MODEL, BF16), (MEM_WIDTH, BF16)],
        [(1, HEAD_DIM), (1, HEAD_DIM), (n_mem, MEM_WIDTH), (n_mem, MEM_WIDTH)], tm=256, name="dn_out_bwd", riders=[sibo])
    h_l0 = chip_sums(["w_mlp2_0", "w_mlp1_0", "w_out_0"], [dw2_0, dw1_0, dwo_0], sib2.results + sib1.results + sibo.results)
    ride_l0_mlp2, ride_l0_rest = chips_rider(h_l0[:1]), chips_rider(h_l0[1:])
    dmnw, dwkv, dmknw = memkv_bwd(mem, mnw, w_kv, mknw, dmk0 + dmk1, dmv0 + dmv1, name="memkv_bwd")
    g_kv = dwkv.reshape(N_DEV, D_MODEL // N_DEV, D_MODEL)
    sibk = sibling_rider([g_kv])
    dq_s, dk_s, dg_s, du0, dw0, dqk0 = delta_seq_bwd(q0, k0, gates, u0, w0, qk0, s_start, do0, name="delta_seq_bwd",
                                                     riders=[ride_fox_g, sibk])
    ride_kv_g = chips_rider(chip_sums(["w_mem_kv"], [g_kv], sibk.results))
    dq0, dk0, dv0, dgates = delta_intra_bwd(q0, k0, v0, gates, du0, dw0, dqk0, dq_s, dk_s, dg_s,
                                            name="delta_intra_bwd", riders=[ride_l0_mlp2, ride_kv_g])
    dxq, dxk, dxv, dcq, dck, dcv = dn_prep_bwd(pm0, conv_w, dq0, dk0, dv0, name="dn_prep_bwd", riders=[ride_l0_rest])
    dconv = jnp.concatenate([dcq, dck, dcv], axis=1)
    dps0, dalog, ddtb = rows_call(
        functools.partial(vjp_rows(dn_gates_fn, 1, (True, True)), n_row=1, n_ct=1),
        [ps0, dgates], [alog, dtb], [(LANES, F32)], [(1, LANES)] * 2, tm=512, name="dn_gates_bwd")
    dpm0 = jnp.concatenate([dxq, dxk, dxv, dz0, dqm0], axis=1)
    dwmain0, dwsmall0 = _in_proj_dw(h0, dpm0, dps0, "0")
    g_dn = in_proj_pieces(dwmain0, dwsmall0, 2 * N_HEADS, DN_IN)
    g_conv = dconv.reshape(CONV_WIDTH, N_DEV, -1).transpose(1, 0, 2).astype(BF16)
    sibd = sibling_rider([g_dn, g_conv])
    grad_x, dn1w0 = _in_proj_dx(dpm0, dps0, dn_main, dn_ab, x0, dx1, n1w0, "0", riders=[sibd])

    g["mem_norm_w"] = dmnw[0]
    g["mem_k_norm_w"] = dmknw[0]
    g["norm1_w"] = jnp.concatenate([dn1w0, dn1w1], axis=0)
    g["dn_a_log"] = dalog[:, :N_HEADS]
    g["dn_dt_bias"] = ddtb[:, :N_HEADS]
    g["dn_o_norm_w"] = donw
    g["fox_f_bias"] = dfbias[:, :N_HEADS]
    g["fox_q_norm_w"] = dqnw
    g["fox_k_norm_w"] = dknw
    g["memq_norm_w"] = jnp.concatenate([dmqw0, dmqw1], axis=0)
    g["norm2_w"] = jnp.concatenate([dn2w0, dn2w1], axis=0)

    ride_last = chips_rider(chip_sums(["dn_w_in", "dn_conv_w"], [g_dn, g_conv], sibd.results))
    ride_small = gather_rider([pack_small(g, last=loss)])
    run_riders([ride_last, ride_small], name="grads_to_chips_last")

    def layers(l0, l1):
        return jnp.stack([l0, l1], axis=1).reshape(4, -1, l0.shape[-1])

    parts = {
        "w_mlp1": layers(ride_l0_rest.results[0], ride_l1.results[1]),
        "w_mlp2": layers(ride_l0_mlp2.results[0], ride_l1.results[0]),
        "w_out": layers(ride_l0_rest.results[1], ride_l1.results[2]),
        "fox_w_in": ride_fox_g.results[0], "w_mem_kv": ride_kv_g.results[0],
        "dn_w_in": ride_last.results[0], "dn_conv_w": ride_last.results[1],
    }
    out = {n: adamw(parts[n], w[n], m[n], v[n], name=f"adamw_{n}") for n, _, _ in BIG}
    small, loss = adamw_small(ride_small.results[0], w, m, v, name="adamw_small")
    return loss, grad_x, out, small


WEIGHTS = ["mem_norm_w", "w_mem_kv", "mem_k_norm_w", "norm1_w", "dn_w_in", "dn_conv_w", "dn_a_log", "dn_dt_bias",
           "dn_o_norm_w", "fox_w_in", "fox_f_bias", "fox_q_norm_w", "fox_k_norm_w", "memq_norm_w", "w_out", "norm2_w",
           "w_mlp1", "w_mlp2"]
DN_IN = 4 * D_MODEL + 2 * N_HEADS + MEM_WIDTH
FOX_IN = 4 * D_MODEL + N_HEADS + MEM_WIDTH
GATE_END = 4 * D_MODEL
OUT_IN = D_MODEL + MEM_WIDTH
BIG = [("w_mem_kv", D_MODEL // N_DEV, D_MODEL), ("dn_w_in", D_MODEL, DN_IN // N_DEV), ("fox_w_in", D_MODEL, FOX_IN // N_DEV),
       ("dn_conv_w", CONV_WIDTH, 3 * D_MODEL // N_DEV), ("w_out", 2 * OUT_IN // N_DEV, D_MODEL),
       ("w_mlp1", 2 * D_MODEL, FF_PIECE), ("w_mlp2", 2 * FF_PIECE, D_MODEL)]
SMALL_TILE = 8 * LANES
SMALL = [(name, shape, -(-math.prod(shape) // SMALL_TILE) * SMALL_TILE) for name, shape in [
    ("mem_norm_w", (D_MODEL,)), ("mem_k_norm_w", (HEAD_DIM,)), ("norm1_w", (2, D_MODEL)), ("dn_a_log", (1, N_HEADS)),
    ("dn_dt_bias", (1, N_HEADS)), ("dn_o_norm_w", (1, HEAD_DIM)), ("fox_f_bias", (1, N_HEADS)),
    ("fox_q_norm_w", (1, HEAD_DIM)), ("fox_k_norm_w", (1, HEAD_DIM)), ("memq_norm_w", (2, HEAD_DIM)), ("norm2_w", (2, D_MODEL))]]
SMALL_ROWS = sum(ln for _, _, ln in SMALL) // LANES + 8


def pack_small(p, last=None):
    def rows(a, ln):
        a = a.reshape(-1)
        return (a if a.shape[0] == ln else jnp.pad(a, (0, ln - a.shape[0]))).reshape(-1, LANES)

    used = sum(ln for _, _, ln in SMALL) // LANES
    tail = jnp.zeros(((SMALL_ROWS - used) * LANES,), F32)
    if last is not None:
        tail = jnp.concatenate([tail[:-1], last.reshape(1)])
    return jnp.concatenate([rows(p[n], ln) for n, _, ln in SMALL] + [tail.reshape(-1, LANES)], axis=0)


def in_proj_weights(gathered, width, n_small):
    full = gathered.transpose(1, 0, 2).reshape(D_MODEL, width)
    main = jnp.concatenate([full[:, :GATE_END], full[:, GATE_END + n_small:]], axis=1)
    return main, jnp.pad(full[:, GATE_END:GATE_END + n_small], ((0, 0), (0, LANES - n_small)))


def in_proj_pieces(d_main, d_small, n_small, width):
    full = jnp.concatenate([d_main[:, :GATE_END], d_small[:, :n_small], d_main[:, GATE_END:]], axis=1)
    return full.reshape(D_MODEL, N_DEV, width // N_DEV).transpose(1, 0, 2)


def _adamw_update(g, w, m, v):
    m_new = ADAM_B1 * m + (1.0 - ADAM_B1) * g
    v_new = ADAM_B2 * v + (1.0 - ADAM_B2) * jnp.square(g)
    m_hat = m_new / (1.0 - ADAM_B1 ** ADAM_STEP)
    v_hat = v_new / (1.0 - ADAM_B2 ** ADAM_STEP)
    return -ADAM_LR * (m_hat / (jnp.sqrt(v_hat) + ADAM_EPS) + ADAM_WD * w), m_new, v_new


def adamw(parts, w, m, v, *, name):
    n, _, cols = parts.shape
    layers = w.shape[0] if w.ndim == 3 else 1
    rows = w.shape[-2]
    tile = _pick(rows, (512, 256, 128))
    steps = rows // tile

    def body(p_ref, w_ref, m_ref, v_ref, g_ref, d_ref, mo_ref, vo_ref):
        g = p_ref[0].astype(F32)
        for i in range(1, n):
            g = g + p_ref[i].astype(F32)
        g_ref[...] = g
        d_ref[...], mo_ref[...], vo_ref[...] = _adamw_update(g, w_ref[...], m_ref[...], v_ref[...])

    if w.ndim == 3:
        spec = pl.BlockSpec((None, tile, cols), lambda l, i: (l, i, 0))
    else:
        spec = pl.BlockSpec((tile, cols), lambda l, i: (i, 0))
    return pl.pallas_call(
        body, name=name, grid=(layers, steps),
        in_specs=[pl.BlockSpec((n, tile, cols), lambda l, i: (0, l * steps + i, 0)), spec, spec, spec], out_specs=[spec] * 4,
        out_shape=[jax.ShapeDtypeStruct(w.shape, F32)] * 4, compiler_params=_params(("parallel", "parallel")),
    )(parts, w, m, v)


def adamw_small(parts, w, m, v, *, name):
    def view(a):
        return a.reshape(-1, LANES) if a.size % LANES == 0 else a.reshape(1, a.size)

    k = len(SMALL)
    ins = [view(d[n]) for d in (w, m, v) for n, _, _ in SMALL]

    def body(p_ref, *refs):
        w_refs, m_refs, v_refs, outs, g_ref = refs[:k], refs[k:2 * k], refs[2 * k:3 * k], refs[3 * k:-1], refs[-1]
        g_all = p_ref[0]
        for i in range(1, N_DEV):
            g_all = g_all + p_ref[i]
        g_ref[...] = g_all
        row = 0
        for i, (_, _, ln) in enumerate(SMALL):
            r, c = w_refs[i].shape
            g = g_ref[row:row + r, 0:c]
            outs[4 * i][...] = g
            outs[4 * i + 1][...], outs[4 * i + 2][...], outs[4 * i + 3][...] = _adamw_update(
                g, w_refs[i][...], m_refs[i][...], v_refs[i][...])
            row += ln // LANES
        outs[-1][...] = g_ref[SMALL_ROWS - 1:SMALL_ROWS, LANES - 1:LANES]

    out_shape = [jax.ShapeDtypeStruct(a.shape, F32) for a in ins[:k] for _ in range(4)] + [jax.ShapeDtypeStruct((1, 1), F32)]
    res = pl.pallas_call(body, name=name, out_shape=out_shape,
                         scratch_shapes=[pltpu.VMEM((SMALL_ROWS, LANES), F32)])(parts, *ins)
    small = {n: [o.reshape(sh) for o in res[4 * i:4 * i + 4]] for i, (n, sh, _) in enumerate(SMALL)}
    return small, res[-1][0, 0]


def kernel(x, mem, mem_norm_w, w_mem_kv, mem_k_norm_w, norm1_w, dn_w_in, dn_conv_w, dn_a_log, dn_dt_bias, dn_o_norm_w, fox_w_in, fox_f_bias, fox_q_norm_w, fox_k_norm_w, memq_norm_w, w_out, norm2_w, w_mlp1, w_mlp2, loss_target, m_mem_norm_w, m_w_mem_kv, m_mem_k_norm_w, m_norm1_w, m_dn_w_in, m_dn_conv_w, m_dn_a_log, m_dn_dt_bias, m_dn_o_norm_w, m_fox_w_in, m_fox_f_bias, m_fox_q_norm_w, m_fox_k_norm_w, m_memq_norm_w, m_w_out, m_norm2_w, m_w_mlp1, m_w_mlp2, v_mem_norm_w, v_w_mem_kv, v_mem_k_norm_w, v_norm1_w, v_dn_w_in, v_dn_conv_w, v_dn_a_log, v_dn_dt_bias, v_dn_o_norm_w, v_fox_w_in, v_fox_f_bias, v_fox_q_norm_w, v_fox_k_norm_w, v_memq_norm_w, v_w_out, v_norm2_w, v_w_mlp1, v_w_mlp2):
    p = dict(mem_norm_w=mem_norm_w, w_mem_kv=w_mem_kv, mem_k_norm_w=mem_k_norm_w, norm1_w=norm1_w, dn_w_in=dn_w_in,
             dn_conv_w=dn_conv_w, dn_a_log=dn_a_log, dn_dt_bias=dn_dt_bias, dn_o_norm_w=dn_o_norm_w, fox_w_in=fox_w_in,
             fox_f_bias=fox_f_bias, fox_q_norm_w=fox_q_norm_w, fox_k_norm_w=fox_k_norm_w, memq_norm_w=memq_norm_w,
             w_out=w_out, norm2_w=norm2_w, w_mlp1=w_mlp1, w_mlp2=w_mlp2)
    pm = dict(mem_norm_w=m_mem_norm_w, w_mem_kv=m_w_mem_kv, mem_k_norm_w=m_mem_k_norm_w, norm1_w=m_norm1_w,
              dn_w_in=m_dn_w_in, dn_conv_w=m_dn_conv_w, dn_a_log=m_dn_a_log, dn_dt_bias=m_dn_dt_bias,
              dn_o_norm_w=m_dn_o_norm_w, fox_w_in=m_fox_w_in, fox_f_bias=m_fox_f_bias, fox_q_norm_w=m_fox_q_norm_w,
              fox_k_norm_w=m_fox_k_norm_w, memq_norm_w=m_memq_norm_w, w_out=m_w_out, norm2_w=m_norm2_w, w_mlp1=m_w_mlp1,
              w_mlp2=m_w_mlp2)
    pv = dict(mem_norm_w=v_mem_norm_w, w_mem_kv=v_w_mem_kv, mem_k_norm_w=v_mem_k_norm_w, norm1_w=v_norm1_w,
              dn_w_in=v_dn_w_in, dn_conv_w=v_dn_conv_w, dn_a_log=v_dn_a_log, dn_dt_bias=v_dn_dt_bias,
              dn_o_norm_w=v_dn_o_norm_w, fox_w_in=v_fox_w_in, fox_f_bias=v_fox_f_bias, fox_q_norm_w=v_fox_q_norm_w,
              fox_k_norm_w=v_fox_k_norm_w, memq_norm_w=v_memq_norm_w, w_out=v_w_out, norm2_w=v_norm2_w, w_mlp1=v_w_mlp1,
              w_mlp2=v_w_mlp2)

    loss, grad_x, results, small = local_step(x[0], mem[0], loss_target[0], p, pm, pv)
    groups = [{n: r[i] for n, r in {**small, **results}.items()} for i in range(4)]
    return (loss, grad_x[None], *[grp[n] for grp in groups for n in WEIGHTS])
```

```python
import functools
import math

import jax
import jax.numpy as jnp
from jax import lax
from jax.experimental import pallas as pl
from jax.experimental.pallas import tpu as pltpu

F32 = jnp.float32
BF16 = jnp.bfloat16
HIGHEST = lax.Precision.HIGHEST

D_MODEL = 1024
HEAD_DIM = 128
N_HEADS = 8
MEM_HEADS = 4
MEM_WIDTH = MEM_HEADS * HEAD_DIM
D_FF = 4 * D_MODEL
CONV_WIDTH = 4
CHUNK = 64
Q_BLOCK = 128
EPS = 1e-6
SCALE = HEAD_DIM ** -0.5
MAIN_WIDTH = 4 * D_MODEL + MEM_WIDTH
LANES = 128
N_DEV = 8

ADAM_LR = 0.001
ADAM_B1 = 0.9
ADAM_B2 = 0.999
ADAM_EPS = 1e-08
ADAM_WD = 0.01
ADAM_STEP = 10

VMEM_LIMIT = 56 * 2 ** 20
MESH = pl.DeviceIdType.MESH


def _bdot(a, b, dims):
    return lax.dot_general(a.astype(BF16), b.astype(BF16), (dims, ((), ())), preferred_element_type=F32)


@jax.custom_vjp
def mm(a, b):
    return _bdot(a, b, ((1,), (0,)))


@jax.custom_vjp
def mm_nt(a, b):
    return _bdot(a, b, ((1,), (1,)))


@jax.custom_vjp
def mm_tn(a, b):
    return _bdot(a, b, ((0,), (0,)))


mm.defvjp(lambda a, b: (mm(a, b), (a, b)), lambda r, g: (mm_nt(g, r[1]), mm_tn(r[0], g)))
mm_nt.defvjp(lambda a, b: (mm_nt(a, b), (a, b)), lambda r, g: (mm(g, r[1]), mm_tn(g, r[0])))
mm_tn.defvjp(lambda a, b: (mm_tn(a, b), (a, b)), lambda r, g: (mm_nt(r[1], g), mm(r[0], g)))


def hdot(a, b):
    return jnp.dot(a, b, precision=HIGHEST, preferred_element_type=F32)


def rms(x, w):
    return x * lax.rsqrt(jnp.mean(x * x, axis=-1, keepdims=True) + EPS) * w


def l2n(x):
    return x * lax.rsqrt(jnp.sum(x * x, axis=-1, keepdims=True) + EPS)


def _iota2(n, m):
    return lax.broadcasted_iota(jnp.int32, (n, m), 0), lax.broadcasted_iota(jnp.int32, (n, m), 1)


def _lower_ones(n):
    r, c = _iota2(n, n)
    return jnp.where(r >= c, 1.0, 0.0).astype(F32)


def _last_row(x):
    r = lax.broadcasted_iota(jnp.int32, x.shape, 0)
    return jnp.sum(jnp.where(r == x.shape[0] - 1, x, 0.0), axis=0, keepdims=True)


def _softmax_rows(z):
    m = lax.stop_gradient(jnp.max(z, axis=-1, keepdims=True))
    e = jnp.exp(z - m)
    return e * (1.0 / jnp.sum(e, axis=-1, keepdims=True))


_BNN = (((2,), (1,)), ((0,), (0,)))
_BNT = (((2,), (2,)), ((0,), (0,)))
_BTN = (((1,), (1,)), ((0,), (0,)))


def _bbdot(a, b, dims):
    return lax.dot_general(a.astype(BF16), b.astype(BF16), dims, preferred_element_type=F32)


@jax.custom_vjp
def bmm(a, b):
    return _bbdot(a, b, _BNN)


@jax.custom_vjp
def bmm_nt(a, b):
    return _bbdot(a, b, _BNT)


@jax.custom_vjp
def bmm_tn(a, b):
    return _bbdot(a, b, _BTN)


@jax.custom_vjp
def bmm_high(a, b):
    return lax.dot_general(a, b, _BNN, precision=lax.Precision.HIGH, preferred_element_type=F32)


bmm.defvjp(lambda a, b: (bmm(a, b), (a, b)), lambda r, g: (bmm_nt(g, r[1]), bmm_tn(r[0], g)))
bmm_nt.defvjp(lambda a, b: (bmm_nt(a, b), (a, b)), lambda r, g: (bmm(g, r[1]), bmm_tn(g, r[0])))
bmm_tn.defvjp(lambda a, b: (bmm_tn(a, b), (a, b)), lambda r, g: (bmm_nt(r[1], g), bmm(r[0], g)))
bmm_high.defvjp(lambda a, b: (bmm_high(a, b), (a, b)), lambda r, g: (bmm_nt(g, r[1]), bmm_tn(r[0], g)))

NEUMANN_HIGH_LEVELS = 2


@jax.custom_vjp
def inv_unit_lower(a):
    n = a.shape[-1]
    r, c = _iota2(n, n)
    p = jnp.where(r == c, 1.0, 0.0).astype(F32) - a
    ak = a
    for level in range(int(math.log2(n)) - 1):
        dot = bmm_high if level < NEUMANN_HIGH_LEVELS else bmm
        ak = dot(ak, ak)
        p = p + dot(p, ak)
    return p


def _inv_unit_lower_fwd(a):
    t = inv_unit_lower(a)
    return t, t


def _inv_unit_lower_bwd(t, g):
    return (-bmm_tn(t, bmm_nt(g, t)),)


inv_unit_lower.defvjp(_inv_unit_lower_fwd, _inv_unit_lower_bwd)


def delta_intra(q, k, v, gc, beta):
    b, c, _ = q.shape
    r, cc = _iota2(c, c)
    causal = r >= cc
    strict = r > cc
    gi = jnp.broadcast_to(gc, (b, c, c))
    gj = jnp.swapaxes(gi, 1, 2)
    decay = jnp.where(causal, jnp.exp(jnp.where(causal, gi - gj, 0.0)), 0.0)
    kb = k * beta
    a = jnp.where(strict, bmm_nt(kb, k) * decay, 0.0)
    t = inv_unit_lower(a)
    u = bmm(t, v * beta)
    w = bmm(t, kb * jnp.exp(gc))
    qk = jnp.where(causal, bmm_nt(q, k) * decay, 0.0)
    return u, w, qk


def delta_step(s, q, k, gc, u, w, qk):
    v_new = u - bmm(w, s)
    out = bmm(q * jnp.exp(gc), s) + bmm(qk, v_new)
    r = lax.broadcasted_iota(jnp.int32, gc.shape, 1)
    g_last = jnp.sum(jnp.where(r == gc.shape[1] - 1, gc, 0.0), axis=1, keepdims=True)
    k_dec = k * jnp.exp(g_last - gc)
    s_new = s * jnp.exp(g_last) + bmm_tn(k_dec, v_new)
    return out, s_new


def fox_probs(q, k, fq, fk, qpos0):
    s = lax.dot_general(q, k, (((1,), (1,)), ((), ())), preferred_element_type=F32)
    r, c = _iota2(s.shape[0], s.shape[1])
    return _softmax_rows(jnp.where(c <= (r + qpos0), s + (fq - fk), -jnp.inf))


def mem_head(qm, wq, mk, mv):
    p = _softmax_rows(mm_nt(rms(qm, wq) * SCALE, mk))
    return mm(p, mv)


def _heads(x, n):
    return [x[:, h * HEAD_DIM:(h + 1) * HEAD_DIM] for h in range(n)]


def memkv_fn(mem, mnw, wkv, mknw):
    kv = mm(rms(mem, mnw), wkv)
    mk = jnp.concatenate([rms(kh, mknw) for kh in _heads(kv[:, :MEM_WIDTH], MEM_HEADS)], axis=1)
    return mk, kv[:, MEM_WIDTH:]


def dn_gates_fn(ab, alog, dtb):
    g = -jnp.exp(alog) * jax.nn.softplus(ab + dtb)
    low = _lower_ones(CHUNK)
    gc = jnp.concatenate([hdot(low, g[i * CHUNK:(i + 1) * CHUNK]) for i in range(ab.shape[0] // CHUNK)], axis=0)
    lane = lax.broadcasted_iota(jnp.int32, ab.shape, 1)
    return jnp.where(lane < N_HEADS, gc, jax.nn.sigmoid(ab))


def fox_fcum_fn(fp, fbias):
    lf = jax.nn.log_sigmoid(fp + fbias)
    low = _lower_ones(LANES)
    carry = jnp.zeros((1, fp.shape[1]), F32)
    outs = []
    for i in range(fp.shape[0] // LANES):
        cs = hdot(low, lf[i * LANES:(i + 1) * LANES]) + carry
        carry = _last_row(cs)
        outs.append(cs)
    return jnp.concatenate(outs, axis=0)


def fox_qk_fn(qraw, kraw, qnw, knw):
    q = jnp.concatenate([rms(x, qnw) * SCALE for x in _heads(qraw, N_HEADS)], axis=1)
    k = jnp.concatenate([rms(x, knw) for x in _heads(kraw, N_HEADS)], axis=1)
    return q, k


def _mem_out(qm, mqw, mk, mv):
    return [mem_head(a, mqw, b, c) for a, b, c in zip(_heads(qm, MEM_HEADS), _heads(mk, MEM_HEADS), _heads(mv, MEM_HEADS))]


def dn_out_fn(o, z, qm, onw, mqw, mk, mv):
    mix = [rms(a, onw) * jax.nn.silu(b) for a, b in zip(o, _heads(z, N_HEADS))]
    return jnp.concatenate(mix + _mem_out(qm, mqw, mk, mv), axis=1)


def fox_out_fn(o, gate, qm, mqw, mk, mv):
    return jnp.concatenate([o * jax.nn.sigmoid(gate)] + _mem_out(qm, mqw, mk, mv), axis=1)


_HBM = pl.BlockSpec(memory_space=pltpu.HBM)


def _place():
    return lax.axis_index("x"), lax.axis_index("y"), lax.axis_index("c")


class Rider:
    def __init__(self, ins, out_shape, scratch, start, finish):
        self.ins, self.out_shape, self.scratch, self.start, self.finish = list(ins), list(out_shape), list(scratch), start, finish
        self.results = None


def gather_rider(xs):
    n = len(xs)

    def plan(x_refs, out_refs, sems):
        send_sems, recv_sems, local_sems = sems
        x, y, c = _place()
        me, sibling = (x, y, c), (x, y, 1 - c)
        chips = [(1 - x, y), (x, 1 - y), (1 - x, 1 - y)]

        def copy(a, k, block, to, src=None):
            px, py, pc = block
            dst = out_refs[a].at[4 * px + 2 * py + pc]
            return pltpu.make_async_remote_copy(
                src_ref=dst if src is None else src, dst_ref=dst,
                send_sem=send_sems.at[a, k], recv_sem=recv_sems.at[a, k], device_id=to, device_id_type=MESH)

        mine = [pltpu.make_async_copy(x_refs[a], out_refs[a].at[4 * x + 2 * y + c], local_sems.at[a]) for a in range(n)]
        first = [copy(a, 0, me, sibling, src=x_refs[a]) for a in range(n)]
        first += [copy(a, 1 + j, me, (*chip, c), src=x_refs[a]) for j, chip in enumerate(chips) for a in range(n)]
        return copy, me, sibling, chips, mine, first

    def start(x_refs, out_refs, sems):
        _, _, _, _, mine, first = plan(x_refs, out_refs, sems)
        for cp in mine + first:
            cp.start()

    def finish(x_refs, out_refs, sems):
        copy, me, sibling, chips, mine, first = plan(x_refs, out_refs, sems)
        _, _, c = me
        passed = []
        for j, chip in enumerate(chips):
            for a in range(n):
                copy(a, 1 + j, (*chip, c), me).wait_recv()
                passed.append(copy(a, 4 + j, (*chip, c), sibling))
                passed[-1].start()
        for a in range(n):
            copy(a, 0, sibling, me).wait_recv()
        for j, chip in enumerate(chips):
            for a in range(n):
                copy(a, 4 + j, (*chip, 1 - c), me).wait_recv()
        for cp in first + passed:
            cp.wait_send()
        for cp in mine:
            cp.wait()

    return Rider(xs, [jax.ShapeDtypeStruct((N_DEV,) + a.shape, a.dtype) for a in xs],
                 [pltpu.SemaphoreType.DMA((n, 7)), pltpu.SemaphoreType.DMA((n, 7)), pltpu.SemaphoreType.DMA((n,))], start, finish)


def sibling_rider(gs):
    n = len(gs)

    def plan(g_refs, out_refs, sems):
        send_sems, recv_sems = sems
        x, y, c = _place()
        return [pltpu.make_async_remote_copy(
            src_ref=g_refs[a].at[2 * k + 1 - c], dst_ref=out_refs[a].at[k], send_sem=send_sems.at[a, k],
            recv_sem=recv_sems.at[a, k], device_id=(x, y, 1 - c), device_id_type=MESH) for a in range(n) for k in range(4)]

    def start(g_refs, out_refs, sems):
        for cp in plan(g_refs, out_refs, sems):
            cp.start()

    def finish(g_refs, out_refs, sems):
        copies = plan(g_refs, out_refs, sems)
        for cp in copies:
            cp.wait_recv()
        for cp in copies:
            cp.wait_send()

    return Rider(gs, [jax.ShapeDtypeStruct((4,) + g.shape[1:], g.dtype) for g in gs],
                 [pltpu.SemaphoreType.DMA((n, 4)), pltpu.SemaphoreType.DMA((n, 4))], start, finish)


def chips_rider(hs):
    n = len(hs)

    def plan(h_refs, out_refs, sems):
        send_sems, recv_sems, local_sems = sems
        x, y, c = _place()
        mine = 2 * x + y
        chips = [(1 - x, y), (x, 1 - y), (1 - x, 1 - y)]
        keep = [pltpu.make_async_copy(h_refs[a].at[mine], out_refs[a].at[mine], local_sems.at[a]) for a in range(n)]
        sends = [pltpu.make_async_remote_copy(
            src_ref=h_refs[a].at[2 * qx + qy], dst_ref=out_refs[a].at[mine], send_sem=send_sems.at[a, j],
            recv_sem=recv_sems.at[a, j], device_id=(qx, qy, c), device_id_type=MESH)
            for j, (qx, qy) in enumerate(chips) for a in range(n)]
        recvs = [pltpu.make_async_remote_copy(
            src_ref=h_refs[a].at[mine], dst_ref=out_refs[a].at[2 * qx + qy], send_sem=send_sems.at[a, j],
            recv_sem=recv_sems.at[a, j], device_id=(qx, qy, c), device_id_type=MESH)
            for j, (qx, qy) in enumerate(chips) for a in range(n)]
        return keep, sends, recvs

    def start(h_refs, out_refs, sems):
        keep, sends, _ = plan(h_refs, out_refs, sems)
        for cp in keep + sends:
            cp.start()

    def finish(h_refs, out_refs, sems):
        keep, sends, recvs = plan(h_refs, out_refs, sems)
        for cp in recvs:
            cp.wait_recv()
        for cp in sends:
            cp.wait_send()
        for cp in keep:
            cp.wait()

    return Rider(hs, [jax.ShapeDtypeStruct(h.shape, h.dtype) for h in hs],
                 [pltpu.SemaphoreType.DMA((n, 3)), pltpu.SemaphoreType.DMA((n, 3)), pltpu.SemaphoreType.DMA((n,))], start, finish)


def hosted_call(riders, body, *, out_shape, in_specs, out_specs, grid=(), scratch_shapes=(), **kw):
    riders = tuple(riders or ())
    if not riders:
        return pl.pallas_call(body, out_shape=out_shape, in_specs=in_specs, out_specs=out_specs, grid=grid,
                              scratch_shapes=scratch_shapes, **kw)
    single = not isinstance(out_shape, (list, tuple))
    k_out_shape = [out_shape] if single else list(out_shape)
    k_out_specs = [out_specs] if single else list(out_specs)
    n_in, n_out, n_scr = len(in_specs), len(k_out_shape), len(scratch_shapes)
    r_ins = [a for r in riders for a in r.ins]
    r_outs = [s for r in riders for s in r.out_shape]
    r_scr = [s for r in riders for s in r.scratch]

    def full_body(*refs):
        ins = refs[:n_in + len(r_ins)]
        outs = refs[n_in + len(r_ins):n_in + len(r_ins) + n_out + len(r_outs)]
        scr = refs[n_in + len(r_ins) + n_out + len(r_outs):]
        steps = math.prod(grid)
        step = 0
        for d, g in enumerate(grid):
            step = step * g + pl.program_id(d)

        def each(method):
            i0, o0, s0 = n_in, n_out, n_scr
            for r in riders:
                getattr(r, method)(ins[i0:i0 + len(r.ins)], outs[o0:o0 + len(r.out_shape)], scr[s0:s0 + len(r.scratch)])
                i0, o0, s0 = i0 + len(r.ins), o0 + len(r.out_shape), s0 + len(r.scratch)

        if steps == 1:
            each("start")
            body(*ins[:n_in], *outs[:n_out], *scr[:n_scr])
            each("finish")
        else:
            pl.when(step == 0)(lambda: each("start"))
            body(*ins[:n_in], *outs[:n_out], *scr[:n_scr])
            pl.when(step == steps - 1)(lambda: each("finish"))

    call = pl.pallas_call(
        full_body, out_shape=k_out_shape + r_outs, in_specs=list(in_specs) + [_HBM] * len(r_ins),
        out_specs=k_out_specs + [_HBM] * len(r_outs), grid=grid, scratch_shapes=list(scratch_shapes) + r_scr, **kw)

    def run(*args):
        res = call(*args, *r_ins)
        o0 = n_out
        for r in riders:
            r.results = list(res[o0:o0 + len(r.out_shape)])
            o0 += len(r.out_shape)
        return res[0] if single else list(res[:n_out])

    return run


def run_riders(riders, *, name):
    hosted_call(riders, lambda: None, name=name, out_shape=[], in_specs=[], out_specs=[])()
    return [r.results for r in riders]


def _pick(n, cands):
    for c in cands:
        if n % c == 0:
            return c
    return n


def _params(sem):
    return pltpu.CompilerParams(dimension_semantics=sem, vmem_limit_bytes=VMEM_LIMIT)


MATMUL_VMEM_BUDGET = 40 * 2 ** 20


def _matmul_tiles(m, n, k, bytes_a, bytes_b, bytes_mn, fixed):
    fm, fn, fk = fixed if fixed is not None else (None, None, None)

    def options(given, size, cands):
        return [given] if given else ([c for c in cands if size % c == 0] or [size])

    best = None
    for tm in options(fm, m, (2048, 1024, 512, 256, 128)):
        for tn in options(fn, n, (512, 256, 128)):
            for tk in options(fk, k, (2048, 1536, 1024, 512, 256, 128)):
                if 2 * (tm * tk * bytes_a + tk * tn * bytes_b + tm * tn * bytes_mn) + tm * tn * 4 > MATMUL_VMEM_BUDGET:
                    continue
                key = ((m // tm) * (n // tn) * (k // tk), -tk)
                if best is None or key < best[0]:
                    best = (key, (tm, tn, tk))
    assert best is not None, (m, n, k, fixed)
    return best[1]


def matmul(a, b, *, name, ta=False, tb=False, post=None, post_ins=(), row_ins=(), acc=False, extra_out=None,
           out_dtype=F32, tiles=None, b_view=None, out_view=None, riders=()):
    (k, m) = a.shape if ta else a.shape[::-1]
    (kb, n) = b_view[:2] if b_view is not None else (b.shape[::-1] if tb else b.shape)
    assert k == kb, (a.shape, b.shape, ta, tb)
    bytes_mn = sum(p.dtype.itemsize for p in post_ins) + jnp.dtype(out_dtype).itemsize
    bytes_mn += jnp.dtype(extra_out[1]).itemsize if extra_out else 0
    tm, tn, tk = _matmul_tiles(m, n, k, a.dtype.itemsize, b.dtype.itemsize, bytes_mn, tiles)
    assert not acc or tn == n, (name, tn, n)
    nk = k // tk
    dims = ((0,) if ta else (1,), (1,) if tb else (0,))
    n_post, n_row = len(post_ins), len(row_ins)
    n_out = 1 + bool(extra_out) + bool(acc)

    def body(*refs):
        a_ref, b_ref = refs[:2]
        post_refs = refs[2:2 + n_post + n_row]
        o_refs, acc_ref = refs[-1 - n_out:-1], refs[-1]
        first_rows, kk = pl.program_id(0) == 0, pl.program_id(2)

        @pl.when(kk == 0)
        def _():
            acc_ref[...] = jnp.zeros_like(acc_ref)

        b_tile = b_ref[...]
        acc_ref[...] += _bdot(a_ref[...], b_tile.reshape(-1, b_tile.shape[-1]), dims)

        @pl.when(kk == nk - 1)
        def _():
            r = acc_ref[...]
            rows = [p[...] for p in post_refs[n_post:]]
            if post is not None:
                r = post(r, *[p[...] for p in post_refs[:n_post]], *rows)
            if acc:
                r, s = r
                sum_ref = o_refs[-1]

                @pl.when(first_rows)
                def _():
                    sum_ref[...] = s

                @pl.when(jnp.logical_not(first_rows))
                def _():
                    sum_ref[...] += s

            o_refs[0][...] = r.astype(out_dtype)
            if extra_out:
                o_refs[1][...] = extra_out[0](r, *rows).astype(extra_out[1])

    a_spec = pl.BlockSpec((tk, tm), lambda i, j, kk: (kk, i)) if ta else pl.BlockSpec((tm, tk), lambda i, j, kk: (i, kk))
    if b_view is not None:
        b_spec = b_view[2]
    else:
        b_spec = pl.BlockSpec((tn, tk), lambda i, j, kk: (j, kk)) if tb else pl.BlockSpec((tk, tn), lambda i, j, kk: (kk, j))
    mn_spec = pl.BlockSpec((tm, tn), lambda i, j, kk: (i, j))
    row_spec = pl.BlockSpec((1, tn), lambda i, j, kk: (0, j))
    o_shape, o_spec = ((m, n), mn_spec) if out_view is None else out_view
    out_shape = [jax.ShapeDtypeStruct(o_shape, out_dtype)]
    out_specs = [o_spec]
    if extra_out:
        out_shape.append(jax.ShapeDtypeStruct((m, n), extra_out[1]))
        out_specs.append(mn_spec)
    if acc:
        out_shape.append(jax.ShapeDtypeStruct((1, n), F32))
        out_specs.append(row_spec)
    res = hosted_call(
        riders, body, name=name, grid=(m // tm, n // tn, nk),
        in_specs=[a_spec, b_spec] + [mn_spec] * n_post + [row_spec] * n_row, out_specs=out_specs, out_shape=out_shape,
        scratch_shapes=[pltpu.VMEM((tm, tn), F32)],
        compiler_params=_params(("arbitrary" if acc else "parallel", "parallel", "arbitrary")),
    )(a, b, *post_ins, *row_ins)
    return res if n_out > 1 else res[0]


def rows_call(fn, row_ins, full_ins, row_outs, acc_outs, *, tm, name, riders=()):
    row_ins = [r if isinstance(r, tuple) else (r, r.shape[-1], 0) for r in row_ins]
    t = row_ins[0][0].shape[-2]
    tm = min(tm, t)
    n_in = len(row_ins) + len(full_ins)
    n_row = len(row_outs)

    def body(*refs):
        res = fn(*[[r[h] for h in range(r.shape[0])] if (i < len(row_ins) and len(r.shape) == 3) else r[...]
                   for i, r in enumerate(refs[:n_in])])
        res = res if isinstance(res, (tuple, list)) else (res,)
        outs = refs[n_in:]
        for ref, val in zip(outs[:n_row], res[:n_row]):
            if len(ref.shape) == 3:
                for h, vh in enumerate(val):
                    ref[h] = vh.astype(ref.dtype)
            else:
                ref[...] = val.astype(ref.dtype)
        first = pl.program_id(0) == 0
        for ref, val in zip(outs[n_row:], res[n_row:]):
            @pl.when(first)
            def _(ref=ref, val=val):
                ref[...] = val

            @pl.when(jnp.logical_not(first))
            def _(ref=ref, val=val):
                ref[...] += val

    def full_spec(shape):
        return pl.BlockSpec(shape, lambda i, nd=len(shape): (0,) * nd)

    def row_spec(lead, w, cb):
        if lead is None:
            return pl.BlockSpec((tm, w), lambda i: (i, cb))
        return pl.BlockSpec((lead, tm, w), lambda i: (0, i, cb))

    def lead_cols(c):
        return c if isinstance(c, tuple) else (None, c)

    in_specs = [row_spec(a.shape[0] if a.ndim == 3 else None, w, cb) for (a, w, cb) in row_ins]
    in_specs += [full_spec(f.shape) for f in full_ins]
    out_specs = [row_spec(*lead_cols(c), 0) for c, _ in row_outs] + [full_spec(s) for s in acc_outs]
    out_shape = [jax.ShapeDtypeStruct(tuple(d for d in (lead_cols(c)[0], t, lead_cols(c)[1]) if d is not None), dt)
                 for c, dt in row_outs] + [jax.ShapeDtypeStruct(s, F32) for s in acc_outs]
    res = hosted_call(
        riders, body, name=name, grid=(t // tm,), in_specs=in_specs, out_specs=out_specs, out_shape=out_shape,
        compiler_params=_params(("arbitrary",)),
    )(*[r[0] for r in row_ins], *full_ins)
    return res


def vjp_rows(fn, n_diff_row, row_diff_full):
    def bwd(*args, n_row, n_ct):
        prim_rows = args[:n_row]
        cts = args[n_row:n_row + n_ct]
        fulls = args[n_row + n_ct:]
        _, vjp = jax.vjp(fn, *prim_rows, *fulls)
        g = vjp(cts[0] if n_ct == 1 else tuple(cts))
        out = list(g[:n_diff_row])
        out += [gf for gf, d in zip(g[n_row:], row_diff_full) if d]
        return tuple(out)
    return bwd


def _shift_down(x, s):
    if s == 0:
        return x
    t = lax.broadcasted_iota(jnp.int32, x.shape, 0)
    return jnp.where(t >= s, pltpu.roll(x, s, 0), 0.0)


def _shift_up(x, s):
    if s == 0:
        return x
    n = x.shape[0]
    t = lax.broadcasted_iota(jnp.int32, x.shape, 0)
    return jnp.where(t < n - s, pltpu.roll(x, n - s, 0), 0.0)


def _conv(x, w_ref):
    return sum(w_ref[pl.ds(j, 1), :] * _shift_down(x, CONV_WIDTH - 1 - j) for j in range(CONV_WIDTH))


_DN_POST = (lambda c: l2n(jax.nn.silu(c)) * SCALE, lambda c: l2n(jax.nn.silu(c)), jax.nn.silu)


def dn_prep_fwd(proj, conv_w, *, name, riders=()):
    t = proj.shape[0]

    def body(xq, xk, xv, wq, wk, wv, oq, ok, ov):
        for x_ref, w_ref, o_ref, post in zip((xq, xk, xv), (wq, wk, wv), (oq, ok, ov), _DN_POST):
            o_ref[...] = post(_conv(x_ref[...], w_ref))

    x_specs = [pl.BlockSpec((t, HEAD_DIM), lambda h, g=g: (0, g * N_HEADS + h)) for g in range(3)]
    w_specs = [pl.BlockSpec((CONV_WIDTH, HEAD_DIM), lambda h, g=g: (0, g * N_HEADS + h)) for g in range(3)]
    o_spec = pl.BlockSpec((None, t, HEAD_DIM), lambda h: (h, 0, 0))
    return hosted_call(
        riders, body, name=name, grid=(N_HEADS,), in_specs=x_specs + w_specs, out_specs=[o_spec] * 3,
        out_shape=[jax.ShapeDtypeStruct((N_HEADS, t, HEAD_DIM), F32)] * 3, compiler_params=_params(("parallel",)),
    )(proj, proj, proj, conv_w, conv_w, conv_w)


def dn_prep_bwd(proj, conv_w, dq, dk, dv, *, name, riders=()):
    t = proj.shape[0]

    def body(xq, xk, xv, wq, wk, wv, gq, gk, gv, dxq, dxk, dxv, dwq, dwk, dwv):
        for x_ref, w_ref, g_ref, dx_ref, dw_ref, post in zip(
                (xq, xk, xv), (wq, wk, wv), (gq, gk, gv), (dxq, dxk, dxv), (dwq, dwk, dwv), _DN_POST):
            x = x_ref[...]
            _, vjp = jax.vjp(post, _conv(x, w_ref))
            dc, = vjp(g_ref[...])
            dx = sum(w_ref[pl.ds(j, 1), :] * _shift_up(dc, CONV_WIDTH - 1 - j) for j in range(CONV_WIDTH))
            dx_ref[...] = dx.astype(dx_ref.dtype)
            for j in range(CONV_WIDTH):
                dw_ref[pl.ds(j, 1), :] = jnp.sum(dc * _shift_down(x, CONV_WIDTH - 1 - j), axis=0, keepdims=True)

    x_specs = [pl.BlockSpec((t, HEAD_DIM), lambda h, g=g: (0, g * N_HEADS + h)) for g in range(3)]
    w_specs = [pl.BlockSpec((CONV_WIDTH, HEAD_DIM), lambda h, g=g: (0, g * N_HEADS + h)) for g in range(3)]
    g_spec = pl.BlockSpec((None, t, HEAD_DIM), lambda h: (h, 0, 0))
    dx_spec = pl.BlockSpec((t, HEAD_DIM), lambda h: (0, h))
    dw_spec = pl.BlockSpec((CONV_WIDTH, HEAD_DIM), lambda h: (0, h))
    return hosted_call(
        riders, body, name=name, grid=(N_HEADS,), in_specs=x_specs + w_specs + [g_spec] * 3, out_specs=[dx_spec] * 3 + [dw_spec] * 3,
        out_shape=[jax.ShapeDtypeStruct((t, D_MODEL), BF16)] * 3 + [jax.ShapeDtypeStruct((CONV_WIDTH, D_MODEL), F32)] * 3,
        compiler_params=_params(("parallel",)),
    )(proj, proj, proj, conv_w, conv_w, conv_w, dq, dk, dv)


INTRA_CHUNKS = 4


def _lane_column(x, lane_index):
    lane = lax.broadcasted_iota(jnp.int32, x.shape, 1)
    return jnp.sum(jnp.where(lane == lane_index, x, 0.0), axis=1, keepdims=True)


def _head_columns(g, first_lane):
    return jnp.concatenate([_lane_column(g, first_lane + h)[None] for h in range(N_HEADS)], axis=0)


def _intra_of_gates(q, k, v, gates):
    nb = N_HEADS * (gates.shape[0] // CHUNK)

    def chunks(x):
        return x.reshape(nb, CHUNK, x.shape[-1])

    res = delta_intra(chunks(q), chunks(k), chunks(v), chunks(_head_columns(gates, 0)), chunks(_head_columns(gates, N_HEADS)))
    return tuple(x.reshape(N_HEADS, -1, x.shape[-1]) for x in res)


def _step_of_gates(s, q, k, gates, u, w, qk):
    return delta_step(s, q, k, _head_columns(gates, 0), u, w, qk)


def _head_major(rows, w, index):
    return pl.BlockSpec((N_HEADS, rows, w), lambda i: (0, index(i), 0))


def delta_intra_fwd(q, k, v, gates, *, name, riders=()):
    t = q.shape[1]
    rows = min(INTRA_CHUNKS, t // CHUNK) * CHUNK

    def body(q_ref, k_ref, v_ref, g_ref, u_ref, w_ref, qk_ref):
        for ref, val in zip((u_ref, w_ref, qk_ref), _intra_of_gates(q_ref[...], k_ref[...], v_ref[...], g_ref[...])):
            ref[...] = val

    x_spec, qk_spec = (_head_major(rows, w, lambda i: i) for w in (HEAD_DIM, CHUNK))
    g_spec = pl.BlockSpec((rows, LANES), lambda i: (i, 0))
    return hosted_call(
        riders, body, name=name, grid=(t // rows,), in_specs=[x_spec] * 3 + [g_spec], out_specs=[x_spec, x_spec, qk_spec],
        out_shape=[jax.ShapeDtypeStruct((N_HEADS, t, HEAD_DIM), F32)] * 2 + [jax.ShapeDtypeStruct((N_HEADS, t, CHUNK), F32)],
        compiler_params=_params(("parallel",)),
    )(q, k, v, gates)


def delta_seq_fwd(q, k, gates, u, w, qk, *, name, riders=()):
    t = q.shape[1]
    nc = t // CHUNK

    def body(q_ref, k_ref, g_ref, u_ref, w_ref, qk_ref, o_ref, s0_ref, s_ref):
        @pl.when(pl.program_id(0) == 0)
        def _():
            s_ref[...] = jnp.zeros_like(s_ref)

        s = s_ref[...]
        s0_ref[...] = s
        o, s_new = _step_of_gates(s, q_ref[...], k_ref[...], g_ref[...], u_ref[...], w_ref[...], qk_ref[...])
        o_ref[...] = o
        s_ref[...] = s_new

    x_spec, qk_spec = (_head_major(CHUNK, w, lambda c: c) for w in (HEAD_DIM, CHUNK))
    g_spec = pl.BlockSpec((CHUNK, LANES), lambda c: (c, 0))
    s_spec = pl.BlockSpec((N_HEADS, None, HEAD_DIM, HEAD_DIM), lambda c: (0, c, 0, 0))
    return hosted_call(
        riders, body, name=name, grid=(nc,), in_specs=[x_spec, x_spec, g_spec, x_spec, x_spec, qk_spec], out_specs=[x_spec, s_spec],
        out_shape=[jax.ShapeDtypeStruct((N_HEADS, t, HEAD_DIM), F32),
                   jax.ShapeDtypeStruct((N_HEADS, nc, HEAD_DIM, HEAD_DIM), F32)],
        scratch_shapes=[pltpu.VMEM((N_HEADS, HEAD_DIM, HEAD_DIM), F32)],
        compiler_params=_params(("arbitrary",)),
    )(q, k, gates, u, w, qk)


def delta_seq_bwd(q, k, gates, u, w, qk, s0, do, *, name, riders=()):
    t = q.shape[1]
    nc = t // CHUNK

    def body(q_ref, k_ref, g_ref, u_ref, w_ref, qk_ref, s0_ref, do_ref,
             dq_ref, dk_ref, dg_ref, du_ref, dw_ref, dqk_ref, ds_ref):
        @pl.when(pl.program_id(0) == 0)
        def _():
            ds_ref[...] = jnp.zeros_like(ds_ref)

        _, vjp = jax.vjp(_step_of_gates, s0_ref[...], q_ref[...], k_ref[...], g_ref[...], u_ref[...], w_ref[...], qk_ref[...])
        ds, dq, dk, dg, du, dw, dqk = vjp((do_ref[...], ds_ref[...]))
        for ref, val in zip((ds_ref, dq_ref, dk_ref, dg_ref, du_ref, dw_ref, dqk_ref), (ds, dq, dk, dg, du, dw, dqk)):
            ref[...] = val

    x_spec, qk_spec = (_head_major(CHUNK, w, lambda c: nc - 1 - c) for w in (HEAD_DIM, CHUNK))
    g_spec = pl.BlockSpec((CHUNK, LANES), lambda c: (nc - 1 - c, 0))
    s_spec = pl.BlockSpec((N_HEADS, None, HEAD_DIM, HEAD_DIM), lambda c: (0, nc - 1 - c, 0, 0))
    head_shape = [jax.ShapeDtypeStruct((N_HEADS, t, w_), F32) for w_ in (HEAD_DIM, HEAD_DIM, HEAD_DIM, HEAD_DIM, CHUNK)]
    return hosted_call(
        riders, body, name=name, grid=(nc,), in_specs=[x_spec, x_spec, g_spec, x_spec, x_spec, qk_spec, s_spec, x_spec],
        out_specs=[x_spec, x_spec, g_spec, x_spec, x_spec, qk_spec],
        out_shape=head_shape[:2] + [jax.ShapeDtypeStruct((t, LANES), F32)] + head_shape[2:],
        scratch_shapes=[pltpu.VMEM((N_HEADS, HEAD_DIM, HEAD_DIM), F32)],
        compiler_params=_params(("arbitrary",)),
    )(q, k, gates, u, w, qk, s0, do)


def delta_intra_bwd(q, k, v, gates, du, dw, dqk, dq_s, dk_s, dg_s, *, name, riders=()):
    t = q.shape[1]
    rows = min(INTRA_CHUNKS, t // CHUNK) * CHUNK

    def body(q_ref, k_ref, v_ref, g_ref, du_ref, dw_ref, dqk_ref, dqs_ref, dks_ref, dgs_ref, dq_ref, dk_ref, dv_ref, dg_ref):
        _, vjp = jax.vjp(_intra_of_gates, q_ref[...], k_ref[...], v_ref[...], g_ref[...])
        dq, dk, dv, dg = vjp((du_ref[...], dw_ref[...], dqk_ref[...]))
        dq_ref[...] = dq + dqs_ref[...]
        dk_ref[...] = dk + dks_ref[...]
        dv_ref[...] = dv
        dg_ref[...] = dg + dgs_ref[...]

    x_spec, qk_spec = (_head_major(rows, w, lambda i: i) for w in (HEAD_DIM, CHUNK))
    g_spec = pl.BlockSpec((rows, LANES), lambda i: (i, 0))
    return hosted_call(
        riders, body, name=name, grid=(t // rows,),
        in_specs=[x_spec] * 3 + [g_spec, x_spec, x_spec, qk_spec, x_spec, x_spec, g_spec],
        out_specs=[x_spec] * 3 + [g_spec],
        out_shape=[jax.ShapeDtypeStruct((N_HEADS, t, HEAD_DIM), F32)] * 3 + [jax.ShapeDtypeStruct((t, LANES), F32)],
        compiler_params=_params(("parallel",)),
    )(q, k, v, gates, du, dw, dqk, dq_s, dk_s, dg_s)


_V_BLOCK = 2 * N_HEADS
FOX_GROUPS = 16


def _fox_groups(t):
    nq = t // Q_BLOCK
    per = max(1, nq // FOX_GROUPS)
    return [(g0, per, (g0 + per) * Q_BLOCK) for g0 in range(0, nq, per)]


def fox_attn_fwd(q, k, proj, fq, fk, *, name, riders=()):
    t = q.shape[0]

    def body(q_ref, k_ref, v_ref, fq_ref, fk_ref, o_ref, kb_ref, vb_ref):
        head = pl.program_id(0)
        kb_ref[...] = k_ref[...].astype(BF16)
        vb_ref[...] = v_ref[...].astype(BF16)
        for g0, per, keys in _fox_groups(t):
            def block(j, carry, g0=g0, keys=keys):
                rows = pl.ds((g0 + j) * Q_BLOCK, Q_BLOCK)
                p = fox_probs(q_ref[rows, :].astype(BF16), kb_ref[0:keys, :], _lane_column(fq_ref[rows, :], head),
                              fk_ref[:, 0:keys], (g0 + j) * Q_BLOCK)
                o_ref[rows, :] = jnp.dot(p.astype(BF16), vb_ref[0:keys, :], preferred_element_type=F32)
                return carry
            for j in range(per):
                block(j, 0)

    x_spec = pl.BlockSpec((t, HEAD_DIM), lambda h: (0, h))
    v_spec = pl.BlockSpec((t, HEAD_DIM), lambda h: (0, _V_BLOCK + h))
    fq_spec = pl.BlockSpec((t, LANES), lambda h: (0, 0))
    fk_spec = pl.BlockSpec((None, 1, t), lambda h: (h, 0, 0))
    return hosted_call(
        riders, body, name=name, grid=(N_HEADS,), in_specs=[x_spec, x_spec, v_spec, fq_spec, fk_spec], out_specs=x_spec,
        out_shape=jax.ShapeDtypeStruct((t, D_MODEL), F32), scratch_shapes=[pltpu.VMEM((t, HEAD_DIM), BF16)] * 2,
        compiler_params=_params(("parallel",)),
    )(q, k, proj, fq, fk)


def fox_attn_bwd(q, k, proj, fq, fk, do, *, name, riders=()):
    t = q.shape[0]

    def body(q_ref, k_ref, v_ref, fq_ref, fk_ref, do_ref, dq_ref, dk_ref, dv_out_ref, dfq_ref, dfk_ref, kb_ref, vb_ref, dv_ref):
        head = pl.program_id(0)

        @pl.when(head == 0)
        def _():
            dfq_ref[...] = jnp.zeros_like(dfq_ref)

        kb_ref[...] = k_ref[...].astype(BF16)
        vb_ref[...] = v_ref[...].astype(BF16)
        dk_ref[...] = jnp.zeros_like(dk_ref)
        dv_ref[...] = jnp.zeros_like(dv_ref)
        dfk_ref[...] = jnp.zeros_like(dfk_ref)
        nt = (((1,), (1,)), ((), ()))
        tn = (((0,), (0,)), ((), ()))
        for g0, per, keys in _fox_groups(t):
            def block(j, carry, g0=g0, keys=keys):
                rows = pl.ds((g0 + j) * Q_BLOCK, Q_BLOCK)
                qb, dob = q_ref[rows, :].astype(BF16), do_ref[rows, :].astype(BF16)
                kb, vb = kb_ref[0:keys, :], vb_ref[0:keys, :]
                p = fox_probs(qb, kb, _lane_column(fq_ref[rows, :], head), fk_ref[:, 0:keys], (g0 + j) * Q_BLOCK)
                dp = lax.dot_general(dob, vb, nt, preferred_element_type=F32)
                dz = p * (dp - jnp.sum(dp * p, axis=-1, keepdims=True))
                pb, dzb = p.astype(BF16), dz.astype(BF16)
                dq_ref[rows, :] = jnp.dot(dzb, kb, preferred_element_type=F32)
                lane = lax.broadcasted_iota(jnp.int32, (Q_BLOCK, LANES), 1)
                dfq_ref[rows, :] += jnp.where(lane == head, jnp.sum(dz, axis=-1, keepdims=True), 0.0)
                dk_ref[0:keys, :] += lax.dot_general(dzb, qb, tn, preferred_element_type=F32)
                dv_ref[0:keys, :] += lax.dot_general(pb, dob, tn, preferred_element_type=F32)
                dfk_ref[:, 0:keys] -= jnp.sum(dz, axis=0, keepdims=True)
                return carry
            for j in range(per):
                block(j, 0)
        dv_out_ref[...] = dv_ref[...].astype(dv_out_ref.dtype)

    x_spec = pl.BlockSpec((t, HEAD_DIM), lambda h: (0, h))
    v_spec = pl.BlockSpec((t, HEAD_DIM), lambda h: (0, _V_BLOCK + h))
    fq_spec = pl.BlockSpec((t, LANES), lambda h: (0, 0))
    fk_spec = pl.BlockSpec((None, 1, t), lambda h: (h, 0, 0))
    return hosted_call(
        riders, body, name=name, grid=(N_HEADS,), in_specs=[x_spec, x_spec, v_spec, fq_spec, fk_spec, x_spec],
        out_specs=[x_spec, x_spec, x_spec, fq_spec, fk_spec],
        out_shape=[jax.ShapeDtypeStruct((t, D_MODEL), F32)] * 2 + [jax.ShapeDtypeStruct((t, D_MODEL), BF16)]
        + [jax.ShapeDtypeStruct((t, LANES), F32), jax.ShapeDtypeStruct((N_HEADS, 1, t), F32)],
        scratch_shapes=[pltpu.VMEM((t, HEAD_DIM), BF16)] * 2 + [pltpu.VMEM((t, HEAD_DIM), F32)],
        compiler_params=_params(("arbitrary",)),
    )(q, k, proj, fq, fk, do)


def memkv_fwd(mem, mnw, wkv, mknw, *, name):
    n = mem.shape[0]

    def body(mem_ref, mnw_ref, w_ref, mknw_ref, mk_ref, mv_ref):
        mk, mv = memkv_fn(mem_ref[...], mnw_ref[...], w_ref[...], mknw_ref[...])
        mk_ref[...] = mk
        mv_ref[...] = mv

    return pl.pallas_call(
        body, name=name, out_shape=[jax.ShapeDtypeStruct((n, MEM_WIDTH), F32)] * 2,
        compiler_params=pltpu.CompilerParams(vmem_limit_bytes=VMEM_LIMIT),
    )(mem, mnw, wkv, mknw)


def memkv_bwd(mem, mnw, wkv, mknw, dmk, dmv, *, name):
    def body(mem_ref, mnw_ref, w_ref, mknw_ref, dmk_ref, dmv_ref, dmnw_ref, dw_ref, dmknw_ref):
        f = functools.partial(memkv_fn, mem_ref[...])
        _, vjp = jax.vjp(f, mnw_ref[...], w_ref[...].astype(F32), mknw_ref[...])
        dmnw, dw, dmknw = vjp((dmk_ref[...], dmv_ref[...]))
        dmnw_ref[...] = dmnw
        dw_ref[...] = dw.astype(dw_ref.dtype)
        dmknw_ref[...] = dmknw

    return pl.pallas_call(
        body, name=name,
        out_shape=[jax.ShapeDtypeStruct(mnw.shape, F32), jax.ShapeDtypeStruct(wkv.shape, BF16), jax.ShapeDtypeStruct(mknw.shape, F32)],
        compiler_params=pltpu.CompilerParams(vmem_limit_bytes=VMEM_LIMIT),
    )(mem, mnw, wkv, mknw, dmk, dmv)


def _row(v, width=None):
    v = v.reshape(1, -1)
    if width is not None and v.shape[1] < width:
        v = jnp.pad(v, ((0, 0), (0, width - v.shape[1])))
    return v


def _norm_fwd(x, w, name, riders=()):
    return rows_call(lambda x, w: rms(x, w), [x], [w], [(D_MODEL, BF16)], [], tm=512, name=name, riders=riders)[0]


FF_PIECE = D_FF // N_DEV


def _add(r, x, *rows):
    return r + x


def _norm_rows(r, w):
    return rms(r, w)


def _norm_bwd_post(dh, x, dx_in, w):
    _, vjp = jax.vjp(rms, x, w)
    dx, dw = vjp(dh)
    return dx + dx_in, dw


def _piece(rows, cols, index):
    return pl.BlockSpec((None, rows, cols), lambda i, j, kk: (index(i, j, kk), 0, 0))


def _two_pieces(rows, cols, index):
    return pl.BlockSpec((2, rows, cols), lambda i, j, kk: (index(i, j, kk), 0, 0))


def _loss_post(r, x, tgt):
    e = r + x - tgt
    return e * (1.0 / D_MODEL), jnp.sum(e * e, axis=0, keepdims=True)


def _mlp_fwd(x, h2, w1, w2, layer, riders=(), next_norm_w=None, loss_target=None):
    riders = list(riders) + [None, None]
    u, a1 = matmul(h2, w1, name=f"mlp1_fwd_{layer}", tiles=(None, FF_PIECE, D_MODEL),
                   extra_out=(lambda u: jnp.square(jnp.maximum(u, 0.0)), BF16),
                   b_view=(D_MODEL, D_FF, _piece(D_MODEL, FF_PIECE, lambda i, j, kk: j)), riders=riders[0])
    if loss_target is not None:
        tail = dict(post=_loss_post, post_ins=[x, loss_target], acc=True)
    elif next_norm_w is not None:
        tail = dict(post=_add, post_ins=[x], row_ins=[next_norm_w], extra_out=(_norm_rows, BF16))
    else:
        tail = dict(post=_add, post_ins=[x])
    y = matmul(a1, w2, name=f"mlp2_fwd_{layer}", tiles=(None, D_MODEL, 2 * FF_PIECE),
               b_view=(D_FF, D_MODEL, _two_pieces(FF_PIECE, D_MODEL, lambda i, j, kk: kk)), riders=riders[1], **tail)
    return y, (x, h2, u, a1)


def chip_sums(names, pieces, gots):
    n = len(pieces)
    c = lax.axis_index("c").astype(jnp.int32).reshape(1)

    def body(c_ref, *refs):
        for a_ref, b_ref, o_ref in zip(refs[:n], refs[n:2 * n], refs[2 * n:]):
            o_ref[...] = (a_ref[...].astype(F32) + b_ref[...].astype(F32)).astype(o_ref.dtype)

    def slot(a):
        return (None,) + a.shape[1:]

    grid_spec = pltpu.PrefetchScalarGridSpec(
        num_scalar_prefetch=1, grid=(4,),
        in_specs=[pl.BlockSpec(slot(a), lambda k, c_ref: (2 * k + c_ref[0], 0, 0)) for a in pieces]
        + [pl.BlockSpec(slot(a), lambda k, c_ref: (k, 0, 0)) for a in pieces],
        out_specs=[pl.BlockSpec(slot(a), lambda k, c_ref: (k, 0, 0)) for a in pieces])
    return pl.pallas_call(
        body, name="grads_pair_sum_" + "_".join(names), grid_spec=grid_spec,
        out_shape=[jax.ShapeDtypeStruct((4,) + a.shape[1:], a.dtype) for a in pieces],
        compiler_params=_params(("parallel",)),
    )(c, *pieces, *gots)


def _mlp_bwd(dy, res, n2w, w1, w2, layer, riders=()):
    x, h2, u, a1 = res
    du = matmul(dy, w2, tb=True, name=f"mlp2_dx_{layer}", out_dtype=BF16, tiles=(None, 2 * FF_PIECE, D_MODEL),
                post=lambda r, u: r * (2.0 * jnp.maximum(u, 0.0)), post_ins=[u],
                b_view=(D_MODEL, D_FF, _two_pieces(FF_PIECE, D_MODEL, lambda i, j, kk: j)), riders=riders)
    dw2 = matmul(a1, dy, ta=True, name=f"mlp2_dw_{layer}", out_dtype=BF16, tiles=(FF_PIECE, D_MODEL, None), out_view=(
        w2.shape, _piece(FF_PIECE, D_MODEL, lambda i, j, kk: i)))
    sib2 = sibling_rider([dw2])
    dx, dn2w = matmul(du, w1, tb=True, name=f"mlp1_dx_{layer}", tiles=(None, D_MODEL, FF_PIECE),
                      b_view=(D_FF, D_MODEL, _piece(D_MODEL, FF_PIECE, lambda i, j, kk: kk)),
                      post=_norm_bwd_post, post_ins=[x, dy], row_ins=[n2w], acc=True, riders=[sib2])
    dw1 = matmul(h2, du, ta=True, name=f"mlp1_dw_{layer}", out_dtype=BF16, tiles=(D_MODEL, FF_PIECE, None), out_view=(
        w1.shape, _piece(D_MODEL, FF_PIECE, lambda i, j, kk: j)))
    return dx, dw1, dw2, dn2w, sibling_rider([dw1]), sib2


def _in_proj_dx(dmain, dsmall, w_main, w_small, x, dx_in, n1w, tag, riders=()):
    dh = matmul(dmain, w_main, tb=True, name=f"inproj_dx_main_{tag}", riders=riders)

    def post(r, dh_main, x, dx_in, w):
        return _norm_bwd_post(r + dh_main, x, dx_in, w)

    return matmul(dsmall, w_small, tb=True, name=f"inproj_dx_small_{tag}", tiles=(None, D_MODEL, None),
                  post=post, post_ins=[dh, x, dx_in], row_ins=[n1w], acc=True)


def _in_proj_dw(h, dmain, dsmall, tag):
    dw_main = matmul(h, dmain, ta=True, out_dtype=BF16, name=f"inproj_dw_main_{tag}")
    dw_small = matmul(h, dsmall, ta=True, out_dtype=BF16, name=f"inproj_dw_small_{tag}")
    return dw_main, dw_small


def local_step(x, mem, target, w, m, v):
    t = x.shape[0]
    n_mem = mem.shape[0]
    g = {}

    def wire(a):
        return a.astype(BF16)

    ride_first = gather_rider([wire(w["dn_w_in"][0])])
    fox_w = wire(w["fox_w_in"][0])
    ride_out = gather_rider([wire(w["w_out"][0]), w["dn_conv_w"][0]])
    ride_out_1 = gather_rider([wire(w["w_out"][1])])
    ride_kv = gather_rider([wire(w["w_mem_kv"])])
    ride_mlp1_0 = gather_rider([wire(w["w_mlp1"][0])])
    ride_mlp2_0 = gather_rider([wire(w["w_mlp2"][0])])
    ride_fox_a, ride_fox_b = gather_rider([fox_w[:D_MODEL // 2]]), gather_rider([fox_w[D_MODEL // 2:]])
    ride_mlp_1 = gather_rider([wire(w["w_mlp1"][1]), wire(w["w_mlp2"][1])])
    mnw, mknw = _row(w["mem_norm_w"]), _row(w["mem_k_norm_w"])

    n1w0, n2w0 = _row(w["norm1_w"][0]), _row(w["norm2_w"][0])
    n1w1, n2w1 = _row(w["norm1_w"][1]), _row(w["norm2_w"][1])
    alog, dtb = _row(w["dn_a_log"][0], LANES), _row(w["dn_dt_bias"][0], LANES)
    onw, mqw0 = _row(w["dn_o_norm_w"][0]), _row(w["memq_norm_w"][0])
    x0 = x
    h0 = _norm_fwd(x0, n1w0, "norm1_fwd_0", riders=[ride_first])
    dn_main, dn_ab = in_proj_weights(ride_first.results[0], DN_IN, 2 * N_HEADS)
    pm0 = matmul(h0, dn_main, name="inproj_main_0", riders=[ride_out])
    w_out0 = ride_out.results[0].reshape(OUT_IN, D_MODEL)
    conv_w = ride_out.results[1].transpose(1, 0, 2).reshape(CONV_WIDTH, 3 * D_MODEL)
    ps0 = matmul(h0, dn_ab, name="inproj_small_0")
    gates = rows_call(dn_gates_fn, [ps0], [alog, dtb], [(LANES, F32)], [], tm=512, name="dn_gates_fwd")[0]
    q0, k0, v0 = dn_prep_fwd(pm0, conv_w, name="dn_prep_fwd", riders=[ride_kv])
    w_kv = ride_kv.results[0].reshape(D_MODEL, D_MODEL)
    mk, mv = memkv_fwd(mem, mnw, w_kv, mknw, name="memkv_fwd")
    u0, w0, qk0 = delta_intra_fwd(q0, k0, v0, gates, name="delta_intra_fwd", riders=[ride_mlp1_0])
    o0, s_start = delta_seq_fwd(q0, k0, gates, u0, w0, qk0, name="delta_seq_fwd", riders=[ride_mlp2_0])
    cat0 = rows_call(dn_out_fn, [o0, (pm0, D_MODEL, 3), (pm0, MEM_WIDTH, 8)], [onw, mqw0, mk, mv],
                     [(D_MODEL + MEM_WIDTH, BF16)], [], tm=256, name="dn_out_fwd")[0]
    (w1_0,), (w2_0,) = ride_mlp1_0.results, ride_mlp2_0.results
    x1, h2_0 = matmul(cat0, w_out0, post=_add, post_ins=[x0], row_ins=[n2w0], extra_out=(_norm_rows, BF16),
                      tiles=(None, D_MODEL, None), name="wout_fwd_0")
    (x2, h1), mlp_res0 = _mlp_fwd(x1, h2_0, w1_0, w2_0, 0, riders=[[ride_fox_a], [ride_fox_b]], next_norm_w=n1w1)
    fox_main, fox_f = in_proj_weights(
        jnp.concatenate([ride_fox_a.results[0], ride_fox_b.results[0]], axis=1), FOX_IN, N_HEADS)

    fbias = _row(w["fox_f_bias"][0], LANES)
    qnw, knw, mqw1 = _row(w["fox_q_norm_w"][0]), _row(w["fox_k_norm_w"][0]), _row(w["memq_norm_w"][1])
    pm1 = matmul(h1, fox_main, name="inproj_main_1", riders=[ride_out_1])
    w_out1 = ride_out_1.results[0].reshape(OUT_IN, D_MODEL)
    ps1 = matmul(h1, fox_f, name="inproj_small_1")
    fq = rows_call(fox_fcum_fn, [ps1], [fbias], [(LANES, F32)], [], tm=t, name="fox_fcum_fwd")[0]
    fk = fq[:, :N_HEADS].T[:, None, :]
    q1, k1 = rows_call(fox_qk_fn, [(pm1, D_MODEL, 0), (pm1, D_MODEL, 1)], [qnw, knw], [(D_MODEL, F32)] * 2, [], tm=256,
                       name="fox_qk_fwd")
    o1 = fox_attn_fwd(q1, k1, pm1, fq, fk, name="fox_attn_fwd", riders=[ride_mlp_1])
    cat1 = rows_call(fox_out_fn, [o1, (pm1, D_MODEL, 3), (pm1, MEM_WIDTH, 8)], [mqw1, mk, mv],
                     [(D_MODEL + MEM_WIDTH, BF16)], [], tm=256, name="fox_out_fwd")[0]
    w1_1, w2_1 = ride_mlp_1.results
    x3, h2_1 = matmul(cat1, w_out1, post=_add, post_ins=[x2], row_ins=[n2w1], extra_out=(_norm_rows, BF16),
                      tiles=(None, D_MODEL, None), name="wout_fwd_1")
    (dy, sq), mlp_res1 = _mlp_fwd(x3, h2_1, w1_1, w2_1, 1, loss_target=target)
    loss = jnp.sum(sq) * (0.5 / D_MODEL)

    dx3, dw1_1, dw2_1, dn2w1, sib1, sib2 = _mlp_bwd(dy, mlp_res1, n2w1, w1_1, w2_1, 1)
    dcat1 = matmul(dx3, w_out1, tb=True, name="wout_dx_1", riders=[sib1])
    dwo_1 = matmul(cat1, dx3, ta=True, out_dtype=BF16, name="wout_dw_1").reshape(N_DEV, OUT_IN // N_DEV, D_MODEL)
    sibo = sibling_rider([dwo_1])
    do1, dgate1, dqm1, dmqw1, dmk1, dmv1 = rows_call(
        functools.partial(vjp_rows(fox_out_fn, 3, (True, True, True)), n_row=3, n_ct=1),
        [o1, (pm1, D_MODEL, 3), (pm1, MEM_WIDTH, 8), dcat1], [mqw1, mk, mv],
        [(D_MODEL, F32), (D_MODEL, BF16), (MEM_WIDTH, BF16)], [(1, HEAD_DIM), (n_mem, MEM_WIDTH), (n_mem, MEM_WIDTH)],
        tm=256, name="fox_out_bwd", riders=[sibo])
    ride_l1 = chips_rider(chip_sums(["w_mlp2_1", "w_mlp1_1", "w_out_1"], [dw2_1, dw1_1, dwo_1],
                                    sib2.results + sib1.results + sibo.results))
    dq1, dk1, dv1, dfq, dfk = fox_attn_bwd(q1, k1, pm1, fq, fk, do1, name="fox_attn_bwd", riders=[ride_l1])
    dqraw1, dkraw1, dqnw, dknw = rows_call(
        functools.partial(vjp_rows(fox_qk_fn, 2, (True, True)), n_row=2, n_ct=2),
        [(pm1, D_MODEL, 0), (pm1, D_MODEL, 1), dq1, dk1], [qnw, knw],
        [(D_MODEL, BF16)] * 2, [(1, HEAD_DIM)] * 2, tm=256, name="fox_qk_bwd")
    dfcum = dfq + jnp.pad(dfk[:, 0, :].T, ((0, 0), (0, LANES - N_HEADS)))
    dps1, dfbias = rows_call(
        functools.partial(vjp_rows(fox_fcum_fn, 1, (True,)), n_row=1, n_ct=1),
        [ps1, dfcum], [fbias], [(LANES, F32)], [(1, LANES)], tm=t, name="fox_fcum_bwd")
    dpm1 = jnp.concatenate([dqraw1, dkraw1, dv1, dgate1, dqm1], axis=1)
    dx2, dn1w1 = _in_proj_dx(dpm1, dps1, fox_main, fox_f, x2, dx3, n1w1, "1")
    dwmain1, dwsmall1 = _in_proj_dw(h1, dpm1, dps1, "1")
    g_fox = in_proj_pieces(dwmain1, dwsmall1, N_HEADS, FOX_IN)
    sibf = sibling_rider([g_fox])

    dx1, dw1_0, dw2_0, dn2w0, sib1, sib2 = _mlp_bwd(dx2, mlp_res0, n2w0, w1_0, w2_0, 0, riders=[sibf])
    ride_fox_g = chips_rider(chip_sums(["fox_w_in"], [g_fox], sibf.results))
    dcat0 = matmul(dx1, w_out0, tb=True, name="wout_dx_0", riders=[sib1])
    dwo_0 = matmul(cat0, dx1, ta=True, out_dtype=BF16, name="wout_dw_0").reshape(N_DEV, OUT_IN // N_DEV, D_MODEL)
    sibo = sibling_rider([dwo_0])
    do0, dz0, dqm0, donw, dmqw0, dmk0, dmv0 = rows_call(
        functools.partial(vjp_rows(dn_out_fn, 3, (True, True, True, True)), n_row=3, n_ct=1),
        [o0, (pm0, D_MODEL, 3), (pm0, MEM_WIDTH, 8), dcat0], [onw, mqw0, mk, mv],
        [((N_HEADS, HEAD_DIM), F32), (D_MODEL, BF16), (MEM_WIDTH, BF16)],
        [(1, HEAD_DIM), (1, HEAD_DIM), (n_mem, MEM_WIDTH), (n_mem, MEM_WIDTH)], tm=256, name="dn_out_bwd", riders=[sibo])
    h_l0 = chip_sums(["w_mlp2_0", "w_mlp1_0", "w_out_0"], [dw2_0, dw1_0, dwo_0], sib2.results + sib1.results + sibo.results)
    ride_l0_mlp2, ride_l0_rest = chips_rider(h_l0[:1]), chips_rider(h_l0[1:])
    dmnw, dwkv, dmknw = memkv_bwd(mem, mnw, w_kv, mknw, dmk0 + dmk1, dmv0 + dmv1, name="memkv_bwd")
    g_kv = dwkv.reshape(N_DEV, D_MODEL // N_DEV, D_MODEL)
    sibk = sibling_rider([g_kv])
    dq_s, dk_s, dg_s, du0, dw0, dqk0 = delta_seq_bwd(q0, k0, gates, u0, w0, qk0, s_start, do0, name="delta_seq_bwd",
                                                     riders=[ride_fox_g, sibk])
    ride_kv_g = chips_rider(chip_sums(["w_mem_kv"], [g_kv], sibk.results))
    dq0, dk0, dv0, dgates = delta_intra_bwd(q0, k0, v0, gates, du0, dw0, dqk0, dq_s, dk_s, dg_s,
                                            name="delta_intra_bwd", riders=[ride_l0_mlp2, ride_kv_g])
    dxq, dxk, dxv, dcq, dck, dcv = dn_prep_bwd(pm0, conv_w, dq0, dk0, dv0, name="dn_prep_bwd", riders=[ride_l0_rest])
    dconv = jnp.concatenate([dcq, dck, dcv], axis=1)
    dps0, dalog, ddtb = rows_call(
        functools.partial(vjp_rows(dn_gates_fn, 1, (True, True)), n_row=1, n_ct=1),
        [ps0, dgates], [alog, dtb], [(LANES, F32)], [(1, LANES)] * 2, tm=512, name="dn_gates_bwd")
    dpm0 = jnp.concatenate([dxq, dxk, dxv, dz0, dqm0], axis=1)
    dwmain0, dwsmall0 = _in_proj_dw(h0, dpm0, dps0, "0")
    g_dn = in_proj_pieces(dwmain0, dwsmall0, 2 * N_HEADS, DN_IN)
    g_conv = dconv.reshape(CONV_WIDTH, N_DEV, -1).transpose(1, 0, 2).astype(BF16)
    sibd = sibling_rider([g_dn, g_conv])
    grad_x, dn1w0 = _in_proj_dx(dpm0, dps0, dn_main, dn_ab, x0, dx1, n1w0, "0", riders=[sibd])

    g["mem_norm_w"] = dmnw[0]
    g["mem_k_norm_w"] = dmknw[0]
    g["norm1_w"] = jnp.concatenate([dn1w0, dn1w1], axis=0)
    g["dn_a_log"] = dalog[:, :N_HEADS]
    g["dn_dt_bias"] = ddtb[:, :N_HEADS]
    g["dn_o_norm_w"] = donw
    g["fox_f_bias"] = dfbias[:, :N_HEADS]
    g["fox_q_norm_w"] = dqnw
    g["fox_k_norm_w"] = dknw
    g["memq_norm_w"] = jnp.concatenate([dmqw0, dmqw1], axis=0)
    g["norm2_w"] = jnp.concatenate([dn2w0, dn2w1], axis=0)

    ride_last = chips_rider(chip_sums(["dn_w_in", "dn_conv_w"], [g_dn, g_conv], sibd.results))
    ride_small = gather_rider([pack_small(g, last=loss)])
    run_riders([ride_last, ride_small], name="grads_to_chips_last")

    parts = {
        "w_mlp1": [ride_l0_rest.results[0], ride_l1.results[1]],
        "w_mlp2": [ride_l0_mlp2.results[0], ride_l1.results[0]],
        "w_out": [ride_l0_rest.results[1], ride_l1.results[2]],
        "fox_w_in": [ride_fox_g.results[0]], "w_mem_kv": [ride_kv_g.results[0]],
        "dn_w_in": [ride_last.results[0]], "dn_conv_w": [ride_last.results[1]],
    }
    out = {n: adamw(parts[n], w[n], m[n], v[n], name=f"adamw_{n}") for n, _, _ in BIG}
    small, loss = adamw_small(ride_small.results[0], w, m, v, name="adamw_small")
    return loss, grad_x, out, small


WEIGHTS = ["mem_norm_w", "w_mem_kv", "mem_k_norm_w", "norm1_w", "dn_w_in", "dn_conv_w", "dn_a_log", "dn_dt_bias",
           "dn_o_norm_w", "fox_w_in", "fox_f_bias", "fox_q_norm_w", "fox_k_norm_w", "memq_norm_w", "w_out", "norm2_w",
           "w_mlp1", "w_mlp2"]
DN_IN = 4 * D_MODEL + 2 * N_HEADS + MEM_WIDTH
FOX_IN = 4 * D_MODEL + N_HEADS + MEM_WIDTH
GATE_END = 4 * D_MODEL
OUT_IN = D_MODEL + MEM_WIDTH
BIG = [("w_mem_kv", D_MODEL // N_DEV, D_MODEL), ("dn_w_in", D_MODEL, DN_IN // N_DEV), ("fox_w_in", D_MODEL, FOX_IN // N_DEV),
       ("dn_conv_w", CONV_WIDTH, 3 * D_MODEL // N_DEV), ("w_out", 2 * OUT_IN // N_DEV, D_MODEL),
       ("w_mlp1", 2 * D_MODEL, FF_PIECE), ("w_mlp2", 2 * FF_PIECE, D_MODEL)]
SMALL_TILE = 8 * LANES
SMALL = [(name, shape, -(-math.prod(shape) // SMALL_TILE) * SMALL_TILE) for name, shape in [
    ("mem_norm_w", (D_MODEL,)), ("mem_k_norm_w", (HEAD_DIM,)), ("norm1_w", (2, D_MODEL)), ("dn_a_log", (1, N_HEADS)),
    ("dn_dt_bias", (1, N_HEADS)), ("dn_o_norm_w", (1, HEAD_DIM)), ("fox_f_bias", (1, N_HEADS)),
    ("fox_q_norm_w", (1, HEAD_DIM)), ("fox_k_norm_w", (1, HEAD_DIM)), ("memq_norm_w", (2, HEAD_DIM)), ("norm2_w", (2, D_MODEL))]]
SMALL_ROWS = sum(ln for _, _, ln in SMALL) // LANES + 8


def pack_small(p, last=None):
    def rows(a, ln):
        a = a.reshape(-1)
        return (a if a.shape[0] == ln else jnp.pad(a, (0, ln - a.shape[0]))).reshape(-1, LANES)

    used = sum(ln for _, _, ln in SMALL) // LANES
    tail = jnp.zeros(((SMALL_ROWS - used) * LANES,), F32)
    if last is not None:
        tail = jnp.concatenate([tail[:-1], last.reshape(1)])
    return jnp.concatenate([rows(p[n], ln) for n, _, ln in SMALL] + [tail.reshape(-1, LANES)], axis=0)


def in_proj_weights(gathered, width, n_small):
    full = gathered.transpose(1, 0, 2).reshape(D_MODEL, width)
    main = jnp.concatenate([full[:, :GATE_END], full[:, GATE_END + n_small:]], axis=1)
    return main, jnp.pad(full[:, GATE_END:GATE_END + n_small], ((0, 0), (0, LANES - n_small)))


def in_proj_pieces(d_main, d_small, n_small, width):
    full = jnp.concatenate([d_main[:, :GATE_END], d_small[:, :n_small], d_main[:, GATE_END:]], axis=1)
    return full.reshape(D_MODEL, N_DEV, width // N_DEV).transpose(1, 0, 2)


def _adamw_update(g, w, m, v):
    m_new = ADAM_B1 * m + (1.0 - ADAM_B1) * g
    v_new = ADAM_B2 * v + (1.0 - ADAM_B2) * jnp.square(g)
    m_hat = m_new / (1.0 - ADAM_B1 ** ADAM_STEP)
    v_hat = v_new / (1.0 - ADAM_B2 ** ADAM_STEP)
    return -ADAM_LR * (m_hat / (jnp.sqrt(v_hat) + ADAM_EPS) + ADAM_WD * w), m_new, v_new


def adamw(parts, w, m, v, *, name):
    layers = len(parts)
    assert layers == (w.shape[0] if w.ndim == 3 else 1)
    n, _, cols = parts[0].shape
    rows = w.shape[-2]
    tile = _pick(rows, (512, 256, 128))
    steps = rows // tile

    def total(p_ref):
        g = p_ref[0].astype(F32)
        for i in range(1, n):
            g = g + p_ref[i].astype(F32)
        return g

    def body(*refs):
        p_refs, (w_ref, m_ref, v_ref, g_ref, d_ref, mo_ref, vo_ref) = refs[:layers], refs[layers:]
        g = total(p_refs[0])
        for j in range(1, layers):
            g = jnp.where(pl.program_id(0) == j, total(p_refs[j]), g)
        g_ref[...] = g
        d_ref[...], mo_ref[...], vo_ref[...] = _adamw_update(g, w_ref[...], m_ref[...], v_ref[...])

    if w.ndim == 3:
        spec = pl.BlockSpec((None, tile, cols), lambda l, i: (l, i, 0))
    else:
        spec = pl.BlockSpec((tile, cols), lambda l, i: (i, 0))
    def parts_spec(j):
        return pl.BlockSpec((n, tile, cols), lambda l, i: (0, jnp.where(l == j, i, jnp.where(l > j, steps - 1, 0)), 0))

    return pl.pallas_call(
        body, name=name, grid=(layers, steps),
        in_specs=[parts_spec(j) for j in range(layers)] + [spec, spec, spec], out_specs=[spec] * 4,
        out_shape=[jax.ShapeDtypeStruct(w.shape, F32)] * 4, compiler_params=_params(("parallel", "parallel")),
    )(*parts, w, m, v)


def adamw_small(parts, w, m, v, *, name):
    def view(a):
        return a.reshape(-1, LANES) if a.size % LANES == 0 else a.reshape(1, a.size)

    k = len(SMALL)
    ins = [view(d[n]) for d in (w, m, v) for n, _, _ in SMALL]

    def body(p_ref, *refs):
        w_refs, m_refs, v_refs, outs, g_ref = refs[:k], refs[k:2 * k], refs[2 * k:3 * k], refs[3 * k:-1], refs[-1]
        g_all = p_ref[0]
        for i in range(1, N_DEV):
            g_all = g_all + p_ref[i]
        g_ref[...] = g_all
        row = 0
        for i, (_, _, ln) in enumerate(SMALL):
            r, c = w_refs[i].shape
            g = g_ref[row:row + r, 0:c]
            outs[4 * i][...] = g
            outs[4 * i + 1][...], outs[4 * i + 2][...], outs[4 * i + 3][...] = _adamw_update(
                g, w_refs[i][...], m_refs[i][...], v_refs[i][...])
            row += ln // LANES
        outs[-1][...] = g_ref[SMALL_ROWS - 1:SMALL_ROWS, LANES - 1:LANES]

    out_shape = [jax.ShapeDtypeStruct(a.shape, F32) for a in ins[:k] for _ in range(4)] + [jax.ShapeDtypeStruct((1, 1), F32)]
    res = pl.pallas_call(body, name=name, out_shape=out_shape,
                         scratch_shapes=[pltpu.VMEM((SMALL_ROWS, LANES), F32)])(parts, *ins)
    small = {n: [o.reshape(sh) for o in res[4 * i:4 * i + 4]] for i, (n, sh, _) in enumerate(SMALL)}
    return small, res[-1][0, 0]


def kernel(x, mem, mem_norm_w, w_mem_kv, mem_k_norm_w, norm1_w, dn_w_in, dn_conv_w, dn_a_log, dn_dt_bias, dn_o_norm_w, fox_w_in, fox_f_bias, fox_q_norm_w, fox_k_norm_w, memq_norm_w, w_out, norm2_w, w_mlp1, w_mlp2, loss_target, m_mem_norm_w, m_w_mem_kv, m_mem_k_norm_w, m_norm1_w, m_dn_w_in, m_dn_conv_w, m_dn_a_log, m_dn_dt_bias, m_dn_o_norm_w, m_fox_w_in, m_fox_f_bias, m_fox_q_norm_w, m_fox_k_norm_w, m_memq_norm_w, m_w_out, m_norm2_w, m_w_mlp1, m_w_mlp2, v_mem_norm_w, v_w_mem_kv, v_mem_k_norm_w, v_norm1_w, v_dn_w_in, v_dn_conv_w, v_dn_a_log, v_dn_dt_bias, v_dn_o_norm_w, v_fox_w_in, v_fox_f_bias, v_fox_q_norm_w, v_fox_k_norm_w, v_memq_norm_w, v_w_out, v_norm2_w, v_w_mlp1, v_w_mlp2):
    p = dict(mem_norm_w=mem_norm_w, w_mem_kv=w_mem_kv, mem_k_norm_w=mem_k_norm_w, norm1_w=norm1_w, dn_w_in=dn_w_in,
             dn_conv_w=dn_conv_w, dn_a_log=dn_a_log, dn_dt_bias=dn_dt_bias, dn_o_norm_w=dn_o_norm_w, fox_w_in=fox_w_in,
             fox_f_bias=fox_f_bias, fox_q_norm_w=fox_q_norm_w, fox_k_norm_w=fox_k_norm_w, memq_norm_w=memq_norm_w,
             w_out=w_out, norm2_w=norm2_w, w_mlp1=w_mlp1, w_mlp2=w_mlp2)
    pm = dict(mem_norm_w=m_mem_norm_w, w_mem_kv=m_w_mem_kv, mem_k_norm_w=m_mem_k_norm_w, norm1_w=m_norm1_w,
              dn_w_in=m_dn_w_in, dn_conv_w=m_dn_conv_w, dn_a_log=m_dn_a_log, dn_dt_bias=m_dn_dt_bias,
              dn_o_norm_w=m_dn_o_norm_w, fox_w_in=m_fox_w_in, fox_f_bias=m_fox_f_bias, fox_q_norm_w=m_fox_q_norm_w,
              fox_k_norm_w=m_fox_k_norm_w, memq_norm_w=m_memq_norm_w, w_out=m_w_out, norm2_w=m_norm2_w, w_mlp1=m_w_mlp1,
              w_mlp2=m_w_mlp2)
    pv = dict(mem_norm_w=v_mem_norm_w, w_mem_kv=v_w_mem_kv, mem_k_norm_w=v_mem_k_norm_w, norm1_w=v_norm1_w,
              dn_w_in=v_dn_w_in, dn_conv_w=v_dn_conv_w, dn_a_log=v_dn_a_log, dn_dt_bias=v_dn_dt_bias,
              dn_o_norm_w=v_dn_o_norm_w, fox_w_in=v_fox_w_in, fox_f_bias=v_fox_f_bias, fox_q_norm_w=v_fox_q_norm_w,
              fox_k_norm_w=v_fox_k_norm_w, memq_norm_w=v_memq_norm_w, w_out=v_w_out, norm2_w=v_norm2_w, w_mlp1=v_w_mlp1,
              w_mlp2=v_w_mlp2)

    loss, grad_x, results, small = local_step(x[0], mem[0], loss_target[0], p, pm, pv)
    groups = [{n: r[i] for n, r in {**small, **results}.items()} for i in range(4)]
    return (loss, grad_x[None], *[grp[n] for grp in groups for n in WEIGHTS])
```

```python
import functools
import math

import jax
import jax.numpy as jnp
from jax import lax
from jax.experimental import pallas as pl
from jax.experimental.pallas import tpu as pltpu

F32 = jnp.float32
BF16 = jnp.bfloat16
HIGHEST = lax.Precision.HIGHEST

D_MODEL = 1024
HEAD_DIM = 128
N_HEADS = 8
MEM_HEADS = 4
MEM_WIDTH = MEM_HEADS * HEAD_DIM
D_FF = 4 * D_MODEL
CONV_WIDTH = 4
CHUNK = 64
Q_BLOCK = 128
EPS = 1e-6
SCALE = HEAD_DIM ** -0.5
MAIN_WIDTH = 4 * D_MODEL + MEM_WIDTH
LANES = 128
N_DEV = 8

ADAM_LR = 0.001
ADAM_B1 = 0.9
ADAM_B2 = 0.999
ADAM_EPS = 1e-08
ADAM_WD = 0.01
ADAM_STEP = 10

VMEM_LIMIT = 56 * 2 ** 20
MESH = pl.DeviceIdType.MESH


def _bdot(a, b, dims):
    return lax.dot_general(a.astype(BF16), b.astype(BF16), (dims, ((), ())), preferred_element_type=F32)


@jax.custom_vjp
def mm(a, b):
    return _bdot(a, b, ((1,), (0,)))


@jax.custom_vjp
def mm_nt(a, b):
    return _bdot(a, b, ((1,), (1,)))


@jax.custom_vjp
def mm_tn(a, b):
    return _bdot(a, b, ((0,), (0,)))


mm.defvjp(lambda a, b: (mm(a, b), (a, b)), lambda r, g: (mm_nt(g, r[1]), mm_tn(r[0], g)))
mm_nt.defvjp(lambda a, b: (mm_nt(a, b), (a, b)), lambda r, g: (mm(g, r[1]), mm_tn(g, r[0])))
mm_tn.defvjp(lambda a, b: (mm_tn(a, b), (a, b)), lambda r, g: (mm_nt(r[1], g), mm(r[0], g)))


def hdot(a, b):
    return jnp.dot(a, b, precision=HIGHEST, preferred_element_type=F32)


def rms(x, w):
    return x * lax.rsqrt(jnp.mean(x * x, axis=-1, keepdims=True) + EPS) * w


def l2n(x):
    return x * lax.rsqrt(jnp.sum(x * x, axis=-1, keepdims=True) + EPS)


def _iota2(n, m):
    return lax.broadcasted_iota(jnp.int32, (n, m), 0), lax.broadcasted_iota(jnp.int32, (n, m), 1)


def _lower_ones(n):
    r, c = _iota2(n, n)
    return jnp.where(r >= c, 1.0, 0.0).astype(F32)


def _last_row(x):
    r = lax.broadcasted_iota(jnp.int32, x.shape, 0)
    return jnp.sum(jnp.where(r == x.shape[0] - 1, x, 0.0), axis=0, keepdims=True)


def _softmax_rows(z):
    m = lax.stop_gradient(jnp.max(z, axis=-1, keepdims=True))
    e = jnp.exp(z - m)
    return e * (1.0 / jnp.sum(e, axis=-1, keepdims=True))


_BNN = (((2,), (1,)), ((0,), (0,)))
_BNT = (((2,), (2,)), ((0,), (0,)))
_BTN = (((1,), (1,)), ((0,), (0,)))


def _bbdot(a, b, dims):
    return lax.dot_general(a.astype(BF16), b.astype(BF16), dims, preferred_element_type=F32)


@jax.custom_vjp
def bmm(a, b):
    return _bbdot(a, b, _BNN)


@jax.custom_vjp
def bmm_nt(a, b):
    return _bbdot(a, b, _BNT)


@jax.custom_vjp
def bmm_tn(a, b):
    return _bbdot(a, b, _BTN)


@jax.custom_vjp
def bmm_high(a, b):
    return lax.dot_general(a, b, _BNN, precision=lax.Precision.HIGH, preferred_element_type=F32)


bmm.defvjp(lambda a, b: (bmm(a, b), (a, b)), lambda r, g: (bmm_nt(g, r[1]), bmm_tn(r[0], g)))
bmm_nt.defvjp(lambda a, b: (bmm_nt(a, b), (a, b)), lambda r, g: (bmm(g, r[1]), bmm_tn(g, r[0])))
bmm_tn.defvjp(lambda a, b: (bmm_tn(a, b), (a, b)), lambda r, g: (bmm_nt(r[1], g), bmm(r[0], g)))
bmm_high.defvjp(lambda a, b: (bmm_high(a, b), (a, b)), lambda r, g: (bmm_nt(g, r[1]), bmm_tn(r[0], g)))

NEUMANN_HIGH_LEVELS = 2


@jax.custom_vjp
def inv_unit_lower(a):
    n = a.shape[-1]
    r, c = _iota2(n, n)
    p = jnp.where(r == c, 1.0, 0.0).astype(F32) - a
    ak = a
    for level in range(int(math.log2(n)) - 1):
        dot = bmm_high if level < NEUMANN_HIGH_LEVELS else bmm
        ak = dot(ak, ak)
        p = p + dot(p, ak)
    return p


def _inv_unit_lower_fwd(a):
    t = inv_unit_lower(a)
    return t, t


def _inv_unit_lower_bwd(t, g):
    return (-bmm_tn(t, bmm_nt(g, t)),)


inv_unit_lower.defvjp(_inv_unit_lower_fwd, _inv_unit_lower_bwd)


def delta_intra(q, k, v, gc, beta):
    b, c, _ = q.shape
    r, cc = _iota2(c, c)
    causal = r >= cc
    strict = r > cc
    gi = jnp.broadcast_to(gc, (b, c, c))
    gj = jnp.swapaxes(gi, 1, 2)
    decay = jnp.where(causal, jnp.exp(jnp.where(causal, gi - gj, 0.0)), 0.0)
    kb = k * beta
    a = jnp.where(strict, bmm_nt(kb, k) * decay, 0.0)
    t = inv_unit_lower(a)
    u = bmm(t, v * beta)
    w = bmm(t, kb * jnp.exp(gc))
    qk = jnp.where(causal, bmm_nt(q, k) * decay, 0.0)
    return u, w, qk


def delta_step(s, q, k, gc, u, w, qk):
    v_new = u - bmm(w, s)
    out = bmm(q * jnp.exp(gc), s) + bmm(qk, v_new)
    r = lax.broadcasted_iota(jnp.int32, gc.shape, 1)
    g_last = jnp.sum(jnp.where(r == gc.shape[1] - 1, gc, 0.0), axis=1, keepdims=True)
    k_dec = k * jnp.exp(g_last - gc)
    s_new = s * jnp.exp(g_last) + bmm_tn(k_dec, v_new)
    return out, s_new


def fox_probs(q, k, fq, fk, qpos0):
    s = lax.dot_general(q, k, (((1,), (1,)), ((), ())), preferred_element_type=F32)
    r, c = _iota2(s.shape[0], s.shape[1])
    return _softmax_rows(jnp.where(c <= (r + qpos0), s + (fq - fk), -jnp.inf))


def mem_head(qm, wq, mk, mv):
    p = _softmax_rows(mm_nt(rms(qm, wq) * SCALE, mk))
    return mm(p, mv)


def _heads(x, n):
    return [x[:, h * HEAD_DIM:(h + 1) * HEAD_DIM] for h in range(n)]


def memkv_fn(mem, mnw, wkv, mknw):
    kv = mm(rms(mem, mnw), wkv)
    mk = jnp.concatenate([rms(kh, mknw) for kh in _heads(kv[:, :MEM_WIDTH], MEM_HEADS)], axis=1)
    return mk, kv[:, MEM_WIDTH:]


def dn_gates_fn(ab, alog, dtb):
    g = -jnp.exp(alog) * jax.nn.softplus(ab + dtb)
    low = _lower_ones(CHUNK)
    gc = jnp.concatenate([hdot(low, g[i * CHUNK:(i + 1) * CHUNK]) for i in range(ab.shape[0] // CHUNK)], axis=0)
    lane = lax.broadcasted_iota(jnp.int32, ab.shape, 1)
    return jnp.where(lane < N_HEADS, gc, jax.nn.sigmoid(ab))


def fox_fcum_fn(fp, fbias):
    lf = jax.nn.log_sigmoid(fp + fbias)
    low = _lower_ones(LANES)
    carry = jnp.zeros((1, fp.shape[1]), F32)
    outs = []
    for i in range(fp.shape[0] // LANES):
        cs = hdot(low, lf[i * LANES:(i + 1) * LANES]) + carry
        carry = _last_row(cs)
        outs.append(cs)
    return jnp.concatenate(outs, axis=0)


def fox_qk_fn(qraw, kraw, qnw, knw):
    q = jnp.concatenate([rms(x, qnw) * SCALE for x in _heads(qraw, N_HEADS)], axis=1)
    k = jnp.concatenate([rms(x, knw) for x in _heads(kraw, N_HEADS)], axis=1)
    return q, k


def _mem_out(qm, mqw, mk, mv):
    return [mem_head(a, mqw, b, c) for a, b, c in zip(_heads(qm, MEM_HEADS), _heads(mk, MEM_HEADS), _heads(mv, MEM_HEADS))]


def dn_out_fn(o, z, qm, onw, mqw, mk, mv):
    mix = [rms(a, onw) * jax.nn.silu(b) for a, b in zip(o, _heads(z, N_HEADS))]
    return jnp.concatenate(mix + _mem_out(qm, mqw, mk, mv), axis=1)


def fox_out_fn(o, gate, qm, mqw, mk, mv):
    return jnp.concatenate([o * jax.nn.sigmoid(gate)] + _mem_out(qm, mqw, mk, mv), axis=1)


_HBM = pl.BlockSpec(memory_space=pltpu.HBM)


def _place():
    return lax.axis_index("x"), lax.axis_index("y"), lax.axis_index("c")


class Rider:
    def __init__(self, ins, out_shape, scratch, start, finish):
        self.ins, self.out_shape, self.scratch, self.start, self.finish = list(ins), list(out_shape), list(scratch), start, finish
        self.results = None


def gather_rider(xs):
    n = len(xs)

    def plan(x_refs, out_refs, sems):
        send_sems, recv_sems, local_sems = sems
        x, y, c = _place()
        me, sibling = (x, y, c), (x, y, 1 - c)
        chips = [(1 - x, y), (x, 1 - y), (1 - x, 1 - y)]

        def copy(a, k, block, to, src=None):
            px, py, pc = block
            dst = out_refs[a].at[4 * px + 2 * py + pc]
            return pltpu.make_async_remote_copy(
                src_ref=dst if src is None else src, dst_ref=dst,
                send_sem=send_sems.at[a, k], recv_sem=recv_sems.at[a, k], device_id=to, device_id_type=MESH)

        mine = [pltpu.make_async_copy(x_refs[a], out_refs[a].at[4 * x + 2 * y + c], local_sems.at[a]) for a in range(n)]
        first = [copy(a, 0, me, sibling, src=x_refs[a]) for a in range(n)]
        first += [copy(a, 1 + j, me, (*chip, c), src=x_refs[a]) for j, chip in enumerate(chips) for a in range(n)]
        return copy, me, sibling, chips, mine, first

    def start(x_refs, out_refs, sems):
        _, _, _, _, mine, first = plan(x_refs, out_refs, sems)
        for cp in mine + first:
            cp.start()

    def finish(x_refs, out_refs, sems):
        copy, me, sibling, chips, mine, first = plan(x_refs, out_refs, sems)
        _, _, c = me
        passed = []
        for j, chip in enumerate(chips):
            for a in range(n):
                copy(a, 1 + j, (*chip, c), me).wait_recv()
                passed.append(copy(a, 4 + j, (*chip, c), sibling))
                passed[-1].start()
        for a in range(n):
            copy(a, 0, sibling, me).wait_recv()
        for j, chip in enumerate(chips):
            for a in range(n):
                copy(a, 4 + j, (*chip, 1 - c), me).wait_recv()
        for cp in first + passed:
            cp.wait_send()
        for cp in mine:
            cp.wait()

    return Rider(xs, [jax.ShapeDtypeStruct((N_DEV,) + a.shape, a.dtype) for a in xs],
                 [pltpu.SemaphoreType.DMA((n, 7)), pltpu.SemaphoreType.DMA((n, 7)), pltpu.SemaphoreType.DMA((n,))], start, finish)


def sibling_rider(gs):
    n = len(gs)

    def plan(g_refs, out_refs, sems):
        send_sems, recv_sems = sems
        x, y, c = _place()
        return [pltpu.make_async_remote_copy(
            src_ref=g_refs[a].at[2 * k + 1 - c], dst_ref=out_refs[a].at[k], send_sem=send_sems.at[a, k],
            recv_sem=recv_sems.at[a, k], device_id=(x, y, 1 - c), device_id_type=MESH) for a in range(n) for k in range(4)]

    def start(g_refs, out_refs, sems):
        for cp in plan(g_refs, out_refs, sems):
            cp.start()

    def finish(g_refs, out_refs, sems):
        copies = plan(g_refs, out_refs, sems)
        for cp in copies:
            cp.wait_recv()
        for cp in copies:
            cp.wait_send()

    return Rider(gs, [jax.ShapeDtypeStruct((4,) + g.shape[1:], g.dtype) for g in gs],
                 [pltpu.SemaphoreType.DMA((n, 4)), pltpu.SemaphoreType.DMA((n, 4))], start, finish)


def chips_rider(hs):
    n = len(hs)

    def plan(h_refs, out_refs, sems):
        send_sems, recv_sems, local_sems = sems
        x, y, c = _place()
        mine = 2 * x + y
        chips = [(1 - x, y), (x, 1 - y), (1 - x, 1 - y)]
        keep = [pltpu.make_async_copy(h_refs[a].at[mine], out_refs[a].at[mine], local_sems.at[a]) for a in range(n)]
        sends = [pltpu.make_async_remote_copy(
            src_ref=h_refs[a].at[2 * qx + qy], dst_ref=out_refs[a].at[mine], send_sem=send_sems.at[a, j],
            recv_sem=recv_sems.at[a, j], device_id=(qx, qy, c), device_id_type=MESH)
            for j, (qx, qy) in enumerate(chips) for a in range(n)]
        recvs = [pltpu.make_async_remote_copy(
            src_ref=h_refs[a].at[mine], dst_ref=out_refs[a].at[2 * qx + qy], send_sem=send_sems.at[a, j],
            recv_sem=recv_sems.at[a, j], device_id=(qx, qy, c), device_id_type=MESH)
            for j, (qx, qy) in enumerate(chips) for a in range(n)]
        return keep, sends, recvs

    def start(h_refs, out_refs, sems):
        keep, sends, _ = plan(h_refs, out_refs, sems)
        for cp in keep + sends:
            cp.start()

    def finish(h_refs, out_refs, sems):
        keep, sends, recvs = plan(h_refs, out_refs, sems)
        for cp in recvs:
            cp.wait_recv()
        for cp in sends:
            cp.wait_send()
        for cp in keep:
            cp.wait()

    return Rider(hs, [jax.ShapeDtypeStruct(h.shape, h.dtype) for h in hs],
                 [pltpu.SemaphoreType.DMA((n, 3)), pltpu.SemaphoreType.DMA((n, 3)), pltpu.SemaphoreType.DMA((n,))], start, finish)


def hosted_call(riders, body, *, out_shape, in_specs, out_specs, grid=(), scratch_shapes=(), **kw):
    riders = tuple(riders or ())
    if not riders:
        return pl.pallas_call(body, out_shape=out_shape, in_specs=in_specs, out_specs=out_specs, grid=grid,
                              scratch_shapes=scratch_shapes, **kw)
    single = not isinstance(out_shape, (list, tuple))
    k_out_shape = [out_shape] if single else list(out_shape)
    k_out_specs = [out_specs] if single else list(out_specs)
    n_in, n_out, n_scr = len(in_specs), len(k_out_shape), len(scratch_shapes)
    r_ins = [a for r in riders for a in r.ins]
    r_outs = [s for r in riders for s in r.out_shape]
    r_scr = [s for r in riders for s in r.scratch]

    def full_body(*refs):
        ins = refs[:n_in + len(r_ins)]
        outs = refs[n_in + len(r_ins):n_in + len(r_ins) + n_out + len(r_outs)]
        scr = refs[n_in + len(r_ins) + n_out + len(r_outs):]
        steps = math.prod(grid)
        step = 0
        for d, g in enumerate(grid):
            step = step * g + pl.program_id(d)

        def each(method):
            i0, o0, s0 = n_in, n_out, n_scr
            for r in riders:
                getattr(r, method)(ins[i0:i0 + len(r.ins)], outs[o0:o0 + len(r.out_shape)], scr[s0:s0 + len(r.scratch)])
                i0, o0, s0 = i0 + len(r.ins), o0 + len(r.out_shape), s0 + len(r.scratch)

        if steps == 1:
            each("start")
            body(*ins[:n_in], *outs[:n_out], *scr[:n_scr])
            each("finish")
        else:
            pl.when(step == 0)(lambda: each("start"))
            body(*ins[:n_in], *outs[:n_out], *scr[:n_scr])
            pl.when(step == steps - 1)(lambda: each("finish"))

    call = pl.pallas_call(
        full_body, out_shape=k_out_shape + r_outs, in_specs=list(in_specs) + [_HBM] * len(r_ins),
        out_specs=k_out_specs + [_HBM] * len(r_outs), grid=grid, scratch_shapes=list(scratch_shapes) + r_scr, **kw)

    def run(*args):
        res = call(*args, *r_ins)
        o0 = n_out
        for r in riders:
            r.results = list(res[o0:o0 + len(r.out_shape)])
            o0 += len(r.out_shape)
        return res[0] if single else list(res[:n_out])

    return run


def run_riders(riders, *, name):
    hosted_call(riders, lambda: None, name=name, out_shape=[], in_specs=[], out_specs=[])()
    return [r.results for r in riders]


def _pick(n, cands):
    for c in cands:
        if n % c == 0:
            return c
    return n


def _params(sem):
    return pltpu.CompilerParams(dimension_semantics=sem, vmem_limit_bytes=VMEM_LIMIT)


MATMUL_VMEM_BUDGET = 40 * 2 ** 20


def _matmul_tiles(m, n, k, bytes_a, bytes_b, bytes_mn, fixed):
    fm, fn, fk = fixed if fixed is not None else (None, None, None)

    def options(given, size, cands):
        return [given] if given else ([c for c in cands if size % c == 0] or [size])

    best = None
    for tm in options(fm, m, (2048, 1024, 512, 256, 128)):
        for tn in options(fn, n, (512, 256, 128)):
            for tk in options(fk, k, (2048, 1536, 1024, 512, 256, 128)):
                if 2 * (tm * tk * bytes_a + tk * tn * bytes_b + tm * tn * bytes_mn) + tm * tn * 4 > MATMUL_VMEM_BUDGET:
                    continue
                key = ((m // tm) * (n // tn) * (k // tk), -tk)
                if best is None or key < best[0]:
                    best = (key, (tm, tn, tk))
    assert best is not None, (m, n, k, fixed)
    return best[1]


def matmul(a, b, *, name, ta=False, tb=False, post=None, post_ins=(), row_ins=(), acc=False, extra_out=None,
           out_dtype=F32, tiles=None, b_view=None, out_view=None, riders=()):
    (k, m) = a.shape if ta else a.shape[::-1]
    (kb, n) = b_view[:2] if b_view is not None else (b.shape[::-1] if tb else b.shape)
    assert k == kb, (a.shape, b.shape, ta, tb)
    bytes_mn = sum(p.dtype.itemsize for p in post_ins) + jnp.dtype(out_dtype).itemsize
    bytes_mn += jnp.dtype(extra_out[1]).itemsize if extra_out else 0
    tm, tn, tk = _matmul_tiles(m, n, k, a.dtype.itemsize, b.dtype.itemsize, bytes_mn, tiles)
    assert not acc or tn == n, (name, tn, n)
    nk = k // tk
    dims = ((0,) if ta else (1,), (1,) if tb else (0,))
    n_post, n_row = len(post_ins), len(row_ins)
    n_out = 1 + bool(extra_out) + bool(acc)

    def body(*refs):
        a_ref, b_ref = refs[:2]
        post_refs = refs[2:2 + n_post + n_row]
        o_refs, acc_ref = refs[-1 - n_out:-1], refs[-1]
        first_rows, kk = pl.program_id(0) == 0, pl.program_id(2)

        @pl.when(kk == 0)
        def _():
            acc_ref[...] = jnp.zeros_like(acc_ref)

        b_tile = b_ref[...]
        acc_ref[...] += _bdot(a_ref[...], b_tile.reshape(-1, b_tile.shape[-1]), dims)

        @pl.when(kk == nk - 1)
        def _():
            r = acc_ref[...]
            rows = [p[...] for p in post_refs[n_post:]]
            if post is not None:
                r = post(r, *[p[...] for p in post_refs[:n_post]], *rows)
            if acc:
                r, s = r
                sum_ref = o_refs[-1]

                @pl.when(first_rows)
                def _():
                    sum_ref[...] = s

                @pl.when(jnp.logical_not(first_rows))
                def _():
                    sum_ref[...] += s

            o_refs[0][...] = r.astype(out_dtype)
            if extra_out:
                o_refs[1][...] = extra_out[0](r, *rows).astype(extra_out[1])

    a_spec = pl.BlockSpec((tk, tm), lambda i, j, kk: (kk, i)) if ta else pl.BlockSpec((tm, tk), lambda i, j, kk: (i, kk))
    if b_view is not None:
        b_spec = b_view[2]
    else:
        b_spec = pl.BlockSpec((tn, tk), lambda i, j, kk: (j, kk)) if tb else pl.BlockSpec((tk, tn), lambda i, j, kk: (kk, j))
    mn_spec = pl.BlockSpec((tm, tn), lambda i, j, kk: (i, j))
    row_spec = pl.BlockSpec((1, tn), lambda i, j, kk: (0, j))
    o_shape, o_spec = ((m, n), mn_spec) if out_view is None else out_view
    out_shape = [jax.ShapeDtypeStruct(o_shape, out_dtype)]
    out_specs = [o_spec]
    if extra_out:
        out_shape.append(jax.ShapeDtypeStruct((m, n), extra_out[1]))
        out_specs.append(mn_spec)
    if acc:
        out_shape.append(jax.ShapeDtypeStruct((1, n), F32))
        out_specs.append(row_spec)
    res = hosted_call(
        riders, body, name=name, grid=(m // tm, n // tn, nk),
        in_specs=[a_spec, b_spec] + [mn_spec] * n_post + [row_spec] * n_row, out_specs=out_specs, out_shape=out_shape,
        scratch_shapes=[pltpu.VMEM((tm, tn), F32)],
        compiler_params=_params(("arbitrary" if acc else "parallel", "parallel", "arbitrary")),
    )(a, b, *post_ins, *row_ins)
    return res if n_out > 1 else res[0]


def rows_call(fn, row_ins, full_ins, row_outs, acc_outs, *, tm, name, riders=()):
    row_ins = [r if isinstance(r, tuple) else (r, r.shape[-1], 0) for r in row_ins]
    t = row_ins[0][0].shape[-2]
    tm = min(tm, t)
    n_in = len(row_ins) + len(full_ins)
    n_row = len(row_outs)

    def body(*refs):
        res = fn(*[[r[h] for h in range(r.shape[0])] if (i < len(row_ins) and len(r.shape) == 3) else r[...]
                   for i, r in enumerate(refs[:n_in])])
        res = res if isinstance(res, (tuple, list)) else (res,)
        outs = refs[n_in:]
        for ref, val in zip(outs[:n_row], res[:n_row]):
            if len(ref.shape) == 3:
                for h, vh in enumerate(val):
                    ref[h] = vh.astype(ref.dtype)
            else:
                ref[...] = val.astype(ref.dtype)
        first = pl.program_id(0) == 0
        for ref, val in zip(outs[n_row:], res[n_row:]):
            @pl.when(first)
            def _(ref=ref, val=val):
                ref[...] = val

            @pl.when(jnp.logical_not(first))
            def _(ref=ref, val=val):
                ref[...] += val

    def full_spec(shape):
        return pl.BlockSpec(shape, lambda i, nd=len(shape): (0,) * nd)

    def row_spec(lead, w, cb):
        if lead is None:
            return pl.BlockSpec((tm, w), lambda i: (i, cb))
        return pl.BlockSpec((lead, tm, w), lambda i: (0, i, cb))

    def lead_cols(c):
        return c if isinstance(c, tuple) else (None, c)

    in_specs = [row_spec(a.shape[0] if a.ndim == 3 else None, w, cb) for (a, w, cb) in row_ins]
    in_specs += [full_spec(f.shape) for f in full_ins]
    out_specs = [row_spec(*lead_cols(c), 0) for c, _ in row_outs] + [full_spec(s) for s in acc_outs]
    out_shape = [jax.ShapeDtypeStruct(tuple(d for d in (lead_cols(c)[0], t, lead_cols(c)[1]) if d is not None), dt)
                 for c, dt in row_outs] + [jax.ShapeDtypeStruct(s, F32) for s in acc_outs]
    res = hosted_call(
        riders, body, name=name, grid=(t // tm,), in_specs=in_specs, out_specs=out_specs, out_shape=out_shape,
        compiler_params=_params(("arbitrary",)),
    )(*[r[0] for r in row_ins], *full_ins)
    return res


def vjp_rows(fn, n_diff_row, row_diff_full):
    def bwd(*args, n_row, n_ct):
        prim_rows = args[:n_row]
        cts = args[n_row:n_row + n_ct]
        fulls = args[n_row + n_ct:]
        _, vjp = jax.vjp(fn, *prim_rows, *fulls)
        g = vjp(cts[0] if n_ct == 1 else tuple(cts))
        out = list(g[:n_diff_row])
        out += [gf for gf, d in zip(g[n_row:], row_diff_full) if d]
        return tuple(out)
    return bwd


def _shift_down(x, s):
    if s == 0:
        return x
    t = lax.broadcasted_iota(jnp.int32, x.shape, 0)
    return jnp.where(t >= s, pltpu.roll(x, s, 0), 0.0)


def _shift_up(x, s):
    if s == 0:
        return x
    n = x.shape[0]
    t = lax.broadcasted_iota(jnp.int32, x.shape, 0)
    return jnp.where(t < n - s, pltpu.roll(x, n - s, 0), 0.0)


def _conv(x, w_ref):
    return sum(w_ref[pl.ds(j, 1), :] * _shift_down(x, CONV_WIDTH - 1 - j) for j in range(CONV_WIDTH))


_DN_POST = (lambda c: l2n(jax.nn.silu(c)) * SCALE, lambda c: l2n(jax.nn.silu(c)), jax.nn.silu)


def dn_prep_fwd(proj, conv_w, *, name, riders=()):
    t = proj.shape[0]

    def body(xq, xk, xv, wq, wk, wv, oq, ok, ov):
        for x_ref, w_ref, o_ref, post in zip((xq, xk, xv), (wq, wk, wv), (oq, ok, ov), _DN_POST):
            o_ref[...] = post(_conv(x_ref[...], w_ref))

    x_specs = [pl.BlockSpec((t, HEAD_DIM), lambda h, g=g: (0, g * N_HEADS + h)) for g in range(3)]
    w_specs = [pl.BlockSpec((CONV_WIDTH, HEAD_DIM), lambda h, g=g: (0, g * N_HEADS + h)) for g in range(3)]
    o_spec = pl.BlockSpec((None, t, HEAD_DIM), lambda h: (h, 0, 0))
    return hosted_call(
        riders, body, name=name, grid=(N_HEADS,), in_specs=x_specs + w_specs, out_specs=[o_spec] * 3,
        out_shape=[jax.ShapeDtypeStruct((N_HEADS, t, HEAD_DIM), F32)] * 3, compiler_params=_params(("parallel",)),
    )(proj, proj, proj, conv_w, conv_w, conv_w)


def dn_prep_bwd(proj, conv_w, dq, dk, dv, *, name, riders=()):
    t = proj.shape[0]

    def body(xq, xk, xv, wq, wk, wv, gq, gk, gv, dxq, dxk, dxv, dwq, dwk, dwv):
        for x_ref, w_ref, g_ref, dx_ref, dw_ref, post in zip(
                (xq, xk, xv), (wq, wk, wv), (gq, gk, gv), (dxq, dxk, dxv), (dwq, dwk, dwv), _DN_POST):
            x = x_ref[...]
            _, vjp = jax.vjp(post, _conv(x, w_ref))
            dc, = vjp(g_ref[...])
            dx = sum(w_ref[pl.ds(j, 1), :] * _shift_up(dc, CONV_WIDTH - 1 - j) for j in range(CONV_WIDTH))
            dx_ref[...] = dx.astype(dx_ref.dtype)
            for j in range(CONV_WIDTH):
                dw_ref[pl.ds(j, 1), :] = jnp.sum(dc * _shift_down(x, CONV_WIDTH - 1 - j), axis=0, keepdims=True)

    x_specs = [pl.BlockSpec((t, HEAD_DIM), lambda h, g=g: (0, g * N_HEADS + h)) for g in range(3)]
    w_specs = [pl.BlockSpec((CONV_WIDTH, HEAD_DIM), lambda h, g=g: (0, g * N_HEADS + h)) for g in range(3)]
    g_spec = pl.BlockSpec((None, t, HEAD_DIM), lambda h: (h, 0, 0))
    dx_spec = pl.BlockSpec((t, HEAD_DIM), lambda h: (0, h))
    dw_spec = pl.BlockSpec((CONV_WIDTH, HEAD_DIM), lambda h: (0, h))
    return hosted_call(
        riders, body, name=name, grid=(N_HEADS,), in_specs=x_specs + w_specs + [g_spec] * 3, out_specs=[dx_spec] * 3 + [dw_spec] * 3,
        out_shape=[jax.ShapeDtypeStruct((t, D_MODEL), BF16)] * 3 + [jax.ShapeDtypeStruct((CONV_WIDTH, D_MODEL), F32)] * 3,
        compiler_params=_params(("parallel",)),
    )(proj, proj, proj, conv_w, conv_w, conv_w, dq, dk, dv)


INTRA_CHUNKS = 4


def _lane_column(x, lane_index):
    lane = lax.broadcasted_iota(jnp.int32, x.shape, 1)
    return jnp.sum(jnp.where(lane == lane_index, x, 0.0), axis=1, keepdims=True)


def _head_columns(g, first_lane):
    return jnp.concatenate([_lane_column(g, first_lane + h)[None] for h in range(N_HEADS)], axis=0)


def _intra_of_gates(q, k, v, gates):
    nb = N_HEADS * (gates.shape[0] // CHUNK)

    def chunks(x):
        return x.reshape(nb, CHUNK, x.shape[-1])

    res = delta_intra(chunks(q), chunks(k), chunks(v), chunks(_head_columns(gates, 0)), chunks(_head_columns(gates, N_HEADS)))
    return tuple(x.reshape(N_HEADS, -1, x.shape[-1]) for x in res)


def _step_of_gates(s, q, k, gates, u, w, qk):
    return delta_step(s, q, k, _head_columns(gates, 0), u, w, qk)


def _head_major(rows, w, index):
    return pl.BlockSpec((N_HEADS, rows, w), lambda i: (0, index(i), 0))


def delta_intra_fwd(q, k, v, gates, *, name, riders=()):
    t = q.shape[1]
    rows = min(INTRA_CHUNKS, t // CHUNK) * CHUNK

    def body(q_ref, k_ref, v_ref, g_ref, u_ref, w_ref, qk_ref):
        for ref, val in zip((u_ref, w_ref, qk_ref), _intra_of_gates(q_ref[...], k_ref[...], v_ref[...], g_ref[...])):
            ref[...] = val

    x_spec, qk_spec = (_head_major(rows, w, lambda i: i) for w in (HEAD_DIM, CHUNK))
    g_spec = pl.BlockSpec((rows, LANES), lambda i: (i, 0))
    return hosted_call(
        riders, body, name=name, grid=(t // rows,), in_specs=[x_spec] * 3 + [g_spec], out_specs=[x_spec, x_spec, qk_spec],
        out_shape=[jax.ShapeDtypeStruct((N_HEADS, t, HEAD_DIM), F32)] * 2 + [jax.ShapeDtypeStruct((N_HEADS, t, CHUNK), F32)],
        compiler_params=_params(("parallel",)),
    )(q, k, v, gates)


def delta_seq_fwd(q, k, gates, u, w, qk, *, name, riders=()):
    t = q.shape[1]
    nc = t // CHUNK

    def body(q_ref, k_ref, g_ref, u_ref, w_ref, qk_ref, o_ref, s0_ref, s_ref):
        @pl.when(pl.program_id(0) == 0)
        def _():
            s_ref[...] = jnp.zeros_like(s_ref)

        s = s_ref[...]
        s0_ref[...] = s
        o, s_new = _step_of_gates(s, q_ref[...], k_ref[...], g_ref[...], u_ref[...], w_ref[...], qk_ref[...])
        o_ref[...] = o
        s_ref[...] = s_new

    x_spec, qk_spec = (_head_major(CHUNK, w, lambda c: c) for w in (HEAD_DIM, CHUNK))
    g_spec = pl.BlockSpec((CHUNK, LANES), lambda c: (c, 0))
    s_spec = pl.BlockSpec((N_HEADS, None, HEAD_DIM, HEAD_DIM), lambda c: (0, c, 0, 0))
    return hosted_call(
        riders, body, name=name, grid=(nc,), in_specs=[x_spec, x_spec, g_spec, x_spec, x_spec, qk_spec], out_specs=[x_spec, s_spec],
        out_shape=[jax.ShapeDtypeStruct((N_HEADS, t, HEAD_DIM), F32),
                   jax.ShapeDtypeStruct((N_HEADS, nc, HEAD_DIM, HEAD_DIM), F32)],
        scratch_shapes=[pltpu.VMEM((N_HEADS, HEAD_DIM, HEAD_DIM), F32)],
        compiler_params=_params(("arbitrary",)),
    )(q, k, gates, u, w, qk)


def delta_seq_bwd(q, k, gates, u, w, qk, s0, do, *, name, riders=()):
    t = q.shape[1]
    nc = t // CHUNK

    def body(q_ref, k_ref, g_ref, u_ref, w_ref, qk_ref, s0_ref, do_ref,
             dq_ref, dk_ref, dg_ref, du_ref, dw_ref, dqk_ref, ds_ref):
        @pl.when(pl.program_id(0) == 0)
        def _():
            ds_ref[...] = jnp.zeros_like(ds_ref)

        _, vjp = jax.vjp(_step_of_gates, s0_ref[...], q_ref[...], k_ref[...], g_ref[...], u_ref[...], w_ref[...], qk_ref[...])
        ds, dq, dk, dg, du, dw, dqk = vjp((do_ref[...], ds_ref[...]))
        for ref, val in zip((ds_ref, dq_ref, dk_ref, dg_ref, du_ref, dw_ref, dqk_ref), (ds, dq, dk, dg, du, dw, dqk)):
            ref[...] = val

    x_spec, qk_spec = (_head_major(CHUNK, w, lambda c: nc - 1 - c) for w in (HEAD_DIM, CHUNK))
    g_spec = pl.BlockSpec((CHUNK, LANES), lambda c: (nc - 1 - c, 0))
    s_spec = pl.BlockSpec((N_HEADS, None, HEAD_DIM, HEAD_DIM), lambda c: (0, nc - 1 - c, 0, 0))
    head_shape = [jax.ShapeDtypeStruct((N_HEADS, t, w_), F32) for w_ in (HEAD_DIM, HEAD_DIM, HEAD_DIM, HEAD_DIM, CHUNK)]
    return hosted_call(
        riders, body, name=name, grid=(nc,), in_specs=[x_spec, x_spec, g_spec, x_spec, x_spec, qk_spec, s_spec, x_spec],
        out_specs=[x_spec, x_spec, g_spec, x_spec, x_spec, qk_spec],
        out_shape=head_shape[:2] + [jax.ShapeDtypeStruct((t, LANES), F32)] + head_shape[2:],
        scratch_shapes=[pltpu.VMEM((N_HEADS, HEAD_DIM, HEAD_DIM), F32)],
        compiler_params=_params(("arbitrary",)),
    )(q, k, gates, u, w, qk, s0, do)


def delta_intra_bwd(q, k, v, gates, du, dw, dqk, dq_s, dk_s, dg_s, *, name, riders=()):
    t = q.shape[1]
    rows = min(INTRA_CHUNKS, t // CHUNK) * CHUNK

    def body(q_ref, k_ref, v_ref, g_ref, du_ref, dw_ref, dqk_ref, dqs_ref, dks_ref, dgs_ref, dq_ref, dk_ref, dv_ref, dg_ref):
        _, vjp = jax.vjp(_intra_of_gates, q_ref[...], k_ref[...], v_ref[...], g_ref[...])
        dq, dk, dv, dg = vjp((du_ref[...], dw_ref[...], dqk_ref[...]))
        dq_ref[...] = dq + dqs_ref[...]
        dk_ref[...] = dk + dks_ref[...]
        dv_ref[...] = dv
        dg_ref[...] = dg + dgs_ref[...]

    x_spec, qk_spec = (_head_major(rows, w, lambda i: i) for w in (HEAD_DIM, CHUNK))
    g_spec = pl.BlockSpec((rows, LANES), lambda i: (i, 0))
    return hosted_call(
        riders, body, name=name, grid=(t // rows,),
        in_specs=[x_spec] * 3 + [g_spec, x_spec, x_spec, qk_spec, x_spec, x_spec, g_spec],
        out_specs=[x_spec] * 3 + [g_spec],
        out_shape=[jax.ShapeDtypeStruct((N_HEADS, t, HEAD_DIM), F32)] * 3 + [jax.ShapeDtypeStruct((t, LANES), F32)],
        compiler_params=_params(("parallel",)),
    )(q, k, v, gates, du, dw, dqk, dq_s, dk_s, dg_s)


_V_BLOCK = 2 * N_HEADS
FOX_GROUPS = 16


def _fox_groups(t):
    nq = t // Q_BLOCK
    per = max(1, nq // FOX_GROUPS)
    return [(g0, per, (g0 + per) * Q_BLOCK) for g0 in range(0, nq, per)]


def fox_attn_fwd(q, k, proj, fq, fk, *, name, riders=()):
    t = q.shape[0]

    def body(q_ref, k_ref, v_ref, fq_ref, fk_ref, o_ref, kb_ref, vb_ref):
        head = pl.program_id(0)
        kb_ref[...] = k_ref[...].astype(BF16)
        vb_ref[...] = v_ref[...].astype(BF16)
        for g0, per, keys in _fox_groups(t):
            def block(j, carry, g0=g0, keys=keys):
                rows = pl.ds((g0 + j) * Q_BLOCK, Q_BLOCK)
                p = fox_probs(q_ref[rows, :].astype(BF16), kb_ref[0:keys, :], _lane_column(fq_ref[rows, :], head),
                              fk_ref[:, 0:keys], (g0 + j) * Q_BLOCK)
                o_ref[rows, :] = jnp.dot(p.astype(BF16), vb_ref[0:keys, :], preferred_element_type=F32)
                return carry
            for j in range(per):
                block(j, 0)

    x_spec = pl.BlockSpec((t, HEAD_DIM), lambda h: (0, h))
    v_spec = pl.BlockSpec((t, HEAD_DIM), lambda h: (0, _V_BLOCK + h))
    fq_spec = pl.BlockSpec((t, LANES), lambda h: (0, 0))
    fk_spec = pl.BlockSpec((None, 1, t), lambda h: (h, 0, 0))
    return hosted_call(
        riders, body, name=name, grid=(N_HEADS,), in_specs=[x_spec, x_spec, v_spec, fq_spec, fk_spec], out_specs=x_spec,
        out_shape=jax.ShapeDtypeStruct((t, D_MODEL), F32), scratch_shapes=[pltpu.VMEM((t, HEAD_DIM), BF16)] * 2,
        compiler_params=_params(("parallel",)),
    )(q, k, proj, fq, fk)


def fox_attn_bwd(q, k, proj, fq, fk, do, *, name, riders=()):
    t = q.shape[0]

    def body(q_ref, k_ref, v_ref, fq_ref, fk_ref, do_ref, dq_ref, dk_ref, dv_out_ref, dfq_ref, dfk_ref, kb_ref, vb_ref, dv_ref):
        head = pl.program_id(0)

        @pl.when(head == 0)
        def _():
            dfq_ref[...] = jnp.zeros_like(dfq_ref)

        kb_ref[...] = k_ref[...].astype(BF16)
        vb_ref[...] = v_ref[...].astype(BF16)
        dk_ref[...] = jnp.zeros_like(dk_ref)
        dv_ref[...] = jnp.zeros_like(dv_ref)
        dfk_ref[...] = jnp.zeros_like(dfk_ref)
        nt = (((1,), (1,)), ((), ()))
        tn = (((0,), (0,)), ((), ()))
        for g0, per, keys in _fox_groups(t):
            def block(j, carry, g0=g0, keys=keys):
                rows = pl.ds((g0 + j) * Q_BLOCK, Q_BLOCK)
                qb, dob = q_ref[rows, :].astype(BF16), do_ref[rows, :].astype(BF16)
                kb, vb = kb_ref[0:keys, :], vb_ref[0:keys, :]
                p = fox_probs(qb, kb, _lane_column(fq_ref[rows, :], head), fk_ref[:, 0:keys], (g0 + j) * Q_BLOCK)
                dp = lax.dot_general(dob, vb, nt, preferred_element_type=F32)
                dz = p * (dp - jnp.sum(dp * p, axis=-1, keepdims=True))
                pb, dzb = p.astype(BF16), dz.astype(BF16)
                dq_ref[rows, :] = jnp.dot(dzb, kb, preferred_element_type=F32)
                lane = lax.broadcasted_iota(jnp.int32, (Q_BLOCK, LANES), 1)
                dfq_ref[rows, :] += jnp.where(lane == head, jnp.sum(dz, axis=-1, keepdims=True), 0.0)
                dk_ref[0:keys, :] += lax.dot_general(dzb, qb, tn, preferred_element_type=F32)
                dv_ref[0:keys, :] += lax.dot_general(pb, dob, tn, preferred_element_type=F32)
                dfk_ref[:, 0:keys] -= jnp.sum(dz, axis=0, keepdims=True)
                return carry
            for j in range(per):
                block(j, 0)
        dv_out_ref[...] = dv_ref[...].astype(dv_out_ref.dtype)

    x_spec = pl.BlockSpec((t, HEAD_DIM), lambda h: (0, h))
    v_spec = pl.BlockSpec((t, HEAD_DIM), lambda h: (0, _V_BLOCK + h))
    fq_spec = pl.BlockSpec((t, LANES), lambda h: (0, 0))
    fk_spec = pl.BlockSpec((None, 1, t), lambda h: (h, 0, 0))
    return hosted_call(
        riders, body, name=name, grid=(N_HEADS,), in_specs=[x_spec, x_spec, v_spec, fq_spec, fk_spec, x_spec],
        out_specs=[x_spec, x_spec, x_spec, fq_spec, fk_spec],
        out_shape=[jax.ShapeDtypeStruct((t, D_MODEL), F32)] * 2 + [jax.ShapeDtypeStruct((t, D_MODEL), BF16)]
        + [jax.ShapeDtypeStruct((t, LANES), F32), jax.ShapeDtypeStruct((N_HEADS, 1, t), F32)],
        scratch_shapes=[pltpu.VMEM((t, HEAD_DIM), BF16)] * 2 + [pltpu.VMEM((t, HEAD_DIM), F32)],
        compiler_params=_params(("arbitrary",)),
    )(q, k, proj, fq, fk, do)


def memkv_fwd(mem, mnw, wkv, mknw, *, name):
    n = mem.shape[0]

    def body(mem_ref, mnw_ref, w_ref, mknw_ref, mk_ref, mv_ref):
        mk, mv = memkv_fn(mem_ref[...], mnw_ref[...], w_ref[...], mknw_ref[...])
        mk_ref[...] = mk
        mv_ref[...] = mv

    return pl.pallas_call(
        body, name=name, out_shape=[jax.ShapeDtypeStruct((n, MEM_WIDTH), F32)] * 2,
        compiler_params=pltpu.CompilerParams(vmem_limit_bytes=VMEM_LIMIT),
    )(mem, mnw, wkv, mknw)


def memkv_bwd(mem, mnw, wkv, mknw, dmks, dmvs, *, name):
    def body(mem_ref, mnw_ref, w_ref, mknw_ref, dmk0_ref, dmk1_ref, dmv0_ref, dmv1_ref, dmnw_ref, dw_ref, dmknw_ref):
        f = functools.partial(memkv_fn, mem_ref[...])
        _, vjp = jax.vjp(f, mnw_ref[...], w_ref[...].astype(F32), mknw_ref[...])
        dmnw, dw, dmknw = vjp((dmk0_ref[...] + dmk1_ref[...], dmv0_ref[...] + dmv1_ref[...]))
        dmnw_ref[...] = dmnw
        dw_ref[...] = dw.astype(dw_ref.dtype)
        dmknw_ref[...] = dmknw

    return pl.pallas_call(
        body, name=name,
        out_shape=[jax.ShapeDtypeStruct(mnw.shape, F32), jax.ShapeDtypeStruct(wkv.shape, BF16), jax.ShapeDtypeStruct(mknw.shape, F32)],
        compiler_params=pltpu.CompilerParams(vmem_limit_bytes=VMEM_LIMIT),
    )(mem, mnw, wkv, mknw, *dmks, *dmvs)


def _row(v, width=None):
    v = v.reshape(1, -1)
    if width is not None and v.shape[1] < width:
        v = jnp.pad(v, ((0, 0), (0, width - v.shape[1])))
    return v


def _norm_fwd(x, w, name, riders=()):
    return rows_call(lambda x, w: rms(x, w), [x], [w], [(D_MODEL, BF16)], [], tm=512, name=name, riders=riders)[0]


FF_PIECE = D_FF // N_DEV


def _add(r, x, *rows):
    return r + x


def _norm_rows(r, w):
    return rms(r, w)


def _norm_bwd_post(dh, x, dx_in, w):
    _, vjp = jax.vjp(rms, x, w)
    dx, dw = vjp(dh)
    return dx + dx_in, dw


def _piece(rows, cols, index):
    return pl.BlockSpec((None, rows, cols), lambda i, j, kk: (index(i, j, kk), 0, 0))


def _two_pieces(rows, cols, index):
    return pl.BlockSpec((2, rows, cols), lambda i, j, kk: (index(i, j, kk), 0, 0))


def _loss_post(r, x, tgt):
    e = r + x - tgt
    return e * (1.0 / D_MODEL), jnp.sum(e * e, axis=0, keepdims=True)


def _mlp_fwd(x, h2, w1, w2, layer, riders=(), next_norm_w=None, loss_target=None):
    riders = list(riders) + [None, None]
    u, a1 = matmul(h2, w1, name=f"mlp1_fwd_{layer}", tiles=(None, FF_PIECE, D_MODEL),
                   extra_out=(lambda u: jnp.square(jnp.maximum(u, 0.0)), BF16),
                   b_view=(D_MODEL, D_FF, _piece(D_MODEL, FF_PIECE, lambda i, j, kk: j)), riders=riders[0])
    if loss_target is not None:
        tail = dict(post=_loss_post, post_ins=[x, loss_target], acc=True)
    elif next_norm_w is not None:
        tail = dict(post=_add, post_ins=[x], row_ins=[next_norm_w], extra_out=(_norm_rows, BF16))
    else:
        tail = dict(post=_add, post_ins=[x])
    y = matmul(a1, w2, name=f"mlp2_fwd_{layer}", tiles=(None, D_MODEL, 2 * FF_PIECE),
               b_view=(D_FF, D_MODEL, _two_pieces(FF_PIECE, D_MODEL, lambda i, j, kk: kk)), riders=riders[1], **tail)
    return y, (x, h2, u, a1)


def chip_sums(names, pieces, gots):
    n = len(pieces)
    c = lax.axis_index("c").astype(jnp.int32).reshape(1)

    def body(c_ref, *refs):
        for a_ref, b_ref, o_ref in zip(refs[:n], refs[n:2 * n], refs[2 * n:]):
            o_ref[...] = (a_ref[...].astype(F32) + b_ref[...].astype(F32)).astype(o_ref.dtype)

    def slot(a):
        return (None,) + a.shape[1:]

    grid_spec = pltpu.PrefetchScalarGridSpec(
        num_scalar_prefetch=1, grid=(4,),
        in_specs=[pl.BlockSpec(slot(a), lambda k, c_ref: (2 * k + c_ref[0], 0, 0)) for a in pieces]
        + [pl.BlockSpec(slot(a), lambda k, c_ref: (k, 0, 0)) for a in pieces],
        out_specs=[pl.BlockSpec(slot(a), lambda k, c_ref: (k, 0, 0)) for a in pieces])
    return pl.pallas_call(
        body, name="grads_pair_sum_" + "_".join(names), grid_spec=grid_spec,
        out_shape=[jax.ShapeDtypeStruct((4,) + a.shape[1:], a.dtype) for a in pieces],
        compiler_params=_params(("parallel",)),
    )(c, *pieces, *gots)


def _mlp_bwd(dy, res, n2w, w1, w2, layer, riders=()):
    x, h2, u, a1 = res
    du = matmul(dy, w2, tb=True, name=f"mlp2_dx_{layer}", out_dtype=BF16, tiles=(None, 2 * FF_PIECE, D_MODEL),
                post=lambda r, u: r * (2.0 * jnp.maximum(u, 0.0)), post_ins=[u],
                b_view=(D_MODEL, D_FF, _two_pieces(FF_PIECE, D_MODEL, lambda i, j, kk: j)), riders=riders)
    dw2 = matmul(a1, dy, ta=True, name=f"mlp2_dw_{layer}", out_dtype=BF16, tiles=(FF_PIECE, D_MODEL, None), out_view=(
        w2.shape, _piece(FF_PIECE, D_MODEL, lambda i, j, kk: i)))
    sib2 = sibling_rider([dw2])
    dx, dn2w = matmul(du, w1, tb=True, name=f"mlp1_dx_{layer}", tiles=(None, D_MODEL, FF_PIECE),
                      b_view=(D_FF, D_MODEL, _piece(D_MODEL, FF_PIECE, lambda i, j, kk: kk)),
                      post=_norm_bwd_post, post_ins=[x, dy], row_ins=[n2w], acc=True, riders=[sib2])
    dw1 = matmul(h2, du, ta=True, name=f"mlp1_dw_{layer}", out_dtype=BF16, tiles=(D_MODEL, FF_PIECE, None), out_view=(
        w1.shape, _piece(D_MODEL, FF_PIECE, lambda i, j, kk: j)))
    return dx, dw1, dw2, dn2w, sibling_rider([dw1]), sib2


def _in_proj_dx(dmain, dsmall, w_main, w_small, x, dx_in, n1w, tag, riders=()):
    dh = matmul(dmain, w_main, tb=True, name=f"inproj_dx_main_{tag}", riders=riders)

    def post(r, dh_main, x, dx_in, w):
        return _norm_bwd_post(r + dh_main, x, dx_in, w)

    return matmul(dsmall, w_small, tb=True, name=f"inproj_dx_small_{tag}", tiles=(None, D_MODEL, None),
                  post=post, post_ins=[dh, x, dx_in], row_ins=[n1w], acc=True)


def _in_proj_dw(h, dmain, dsmall, tag):
    dw_main = matmul(h, dmain, ta=True, out_dtype=BF16, name=f"inproj_dw_main_{tag}")
    dw_small = matmul(h, dsmall, ta=True, out_dtype=BF16, name=f"inproj_dw_small_{tag}")
    return dw_main, dw_small


def local_step(x, mem, target, w, m, v):
    t = x.shape[0]
    n_mem = mem.shape[0]
    g = {}

    def wire(a):
        return a.astype(BF16)

    ride_first = gather_rider([wire(w["dn_w_in"][0])])
    fox_w = wire(w["fox_w_in"][0])
    ride_out = gather_rider([wire(w["w_out"][0]), w["dn_conv_w"][0]])
    ride_out_1 = gather_rider([wire(w["w_out"][1])])
    ride_kv = gather_rider([wire(w["w_mem_kv"])])
    ride_mlp1_0 = gather_rider([wire(w["w_mlp1"][0])])
    ride_mlp2_0 = gather_rider([wire(w["w_mlp2"][0])])
    ride_fox_a, ride_fox_b = gather_rider([fox_w[:D_MODEL // 2]]), gather_rider([fox_w[D_MODEL // 2:]])
    ride_mlp_1 = gather_rider([wire(w["w_mlp1"][1]), wire(w["w_mlp2"][1])])
    mnw, mknw = _row(w["mem_norm_w"]), _row(w["mem_k_norm_w"])

    n1w0, n2w0 = _row(w["norm1_w"][0]), _row(w["norm2_w"][0])
    n1w1, n2w1 = _row(w["norm1_w"][1]), _row(w["norm2_w"][1])
    alog, dtb = _row(w["dn_a_log"][0], LANES), _row(w["dn_dt_bias"][0], LANES)
    onw, mqw0 = _row(w["dn_o_norm_w"][0]), _row(w["memq_norm_w"][0])
    x0 = x
    h0 = _norm_fwd(x0, n1w0, "norm1_fwd_0", riders=[ride_first])
    dn_main, dn_ab = in_proj_weights(ride_first.results[0], DN_IN, 2 * N_HEADS)
    pm0 = matmul(h0, dn_main, name="inproj_main_0", riders=[ride_out])
    w_out0 = ride_out.results[0].reshape(OUT_IN, D_MODEL)
    conv_w = ride_out.results[1].transpose(1, 0, 2).reshape(CONV_WIDTH, 3 * D_MODEL)
    ps0 = matmul(h0, dn_ab, name="inproj_small_0")
    gates = rows_call(dn_gates_fn, [ps0], [alog, dtb], [(LANES, F32)], [], tm=512, name="dn_gates_fwd")[0]
    q0, k0, v0 = dn_prep_fwd(pm0, conv_w, name="dn_prep_fwd", riders=[ride_kv])
    w_kv = ride_kv.results[0].reshape(D_MODEL, D_MODEL)
    mk, mv = memkv_fwd(mem, mnw, w_kv, mknw, name="memkv_fwd")
    u0, w0, qk0 = delta_intra_fwd(q0, k0, v0, gates, name="delta_intra_fwd", riders=[ride_mlp1_0])
    o0, s_start = delta_seq_fwd(q0, k0, gates, u0, w0, qk0, name="delta_seq_fwd", riders=[ride_mlp2_0])
    cat0 = rows_call(dn_out_fn, [o0, (pm0, D_MODEL, 3), (pm0, MEM_WIDTH, 8)], [onw, mqw0, mk, mv],
                     [(D_MODEL + MEM_WIDTH, BF16)], [], tm=256, name="dn_out_fwd")[0]
    (w1_0,), (w2_0,) = ride_mlp1_0.results, ride_mlp2_0.results
    x1, h2_0 = matmul(cat0, w_out0, post=_add, post_ins=[x0], row_ins=[n2w0], extra_out=(_norm_rows, BF16),
                      tiles=(None, D_MODEL, None), name="wout_fwd_0")
    (x2, h1), mlp_res0 = _mlp_fwd(x1, h2_0, w1_0, w2_0, 0, riders=[[ride_fox_a], [ride_fox_b]], next_norm_w=n1w1)
    fox_main, fox_f = in_proj_weights(
        jnp.concatenate([ride_fox_a.results[0], ride_fox_b.results[0]], axis=1), FOX_IN, N_HEADS)

    fbias = _row(w["fox_f_bias"][0], LANES)
    qnw, knw, mqw1 = _row(w["fox_q_norm_w"][0]), _row(w["fox_k_norm_w"][0]), _row(w["memq_norm_w"][1])
    pm1 = matmul(h1, fox_main, name="inproj_main_1", riders=[ride_out_1])
    w_out1 = ride_out_1.results[0].reshape(OUT_IN, D_MODEL)
    ps1 = matmul(h1, fox_f, name="inproj_small_1")
    fq = rows_call(fox_fcum_fn, [ps1], [fbias], [(LANES, F32)], [], tm=t, name="fox_fcum_fwd")[0]
    fk = fq[:, :N_HEADS].T[:, None, :]
    q1, k1 = rows_call(fox_qk_fn, [(pm1, D_MODEL, 0), (pm1, D_MODEL, 1)], [qnw, knw], [(D_MODEL, F32)] * 2, [], tm=256,
                       name="fox_qk_fwd")
    o1 = fox_attn_fwd(q1, k1, pm1, fq, fk, name="fox_attn_fwd", riders=[ride_mlp_1])
    cat1 = rows_call(fox_out_fn, [o1, (pm1, D_MODEL, 3), (pm1, MEM_WIDTH, 8)], [mqw1, mk, mv],
                     [(D_MODEL + MEM_WIDTH, BF16)], [], tm=256, name="fox_out_fwd")[0]
    w1_1, w2_1 = ride_mlp_1.results
    x3, h2_1 = matmul(cat1, w_out1, post=_add, post_ins=[x2], row_ins=[n2w1], extra_out=(_norm_rows, BF16),
                      tiles=(None, D_MODEL, None), name="wout_fwd_1")
    (dy, sq), mlp_res1 = _mlp_fwd(x3, h2_1, w1_1, w2_1, 1, loss_target=target)
    loss = jnp.sum(sq) * (0.5 / D_MODEL)

    dx3, dw1_1, dw2_1, dn2w1, sib1, sib2 = _mlp_bwd(dy, mlp_res1, n2w1, w1_1, w2_1, 1)
    dcat1 = matmul(dx3, w_out1, tb=True, name="wout_dx_1", riders=[sib1])
    dwo_1 = matmul(cat1, dx3, ta=True, out_dtype=BF16, name="wout_dw_1").reshape(N_DEV, OUT_IN // N_DEV, D_MODEL)
    sibo = sibling_rider([dwo_1])
    do1, dgate1, dqm1, dmqw1, dmk1, dmv1 = rows_call(
        functools.partial(vjp_rows(fox_out_fn, 3, (True, True, True)), n_row=3, n_ct=1),
        [o1, (pm1, D_MODEL, 3), (pm1, MEM_WIDTH, 8), dcat1], [mqw1, mk, mv],
        [(D_MODEL, F32), (D_MODEL, BF16), (MEM_WIDTH, BF16)], [(1, HEAD_DIM), (n_mem, MEM_WIDTH), (n_mem, MEM_WIDTH)],
        tm=256, name="fox_out_bwd", riders=[sibo])
    ride_l1 = chips_rider(chip_sums(["w_mlp2_1", "w_mlp1_1", "w_out_1"], [dw2_1, dw1_1, dwo_1],
                                    sib2.results + sib1.results + sibo.results))
    dq1, dk1, dv1, dfq, dfk = fox_attn_bwd(q1, k1, pm1, fq, fk, do1, name="fox_attn_bwd", riders=[ride_l1])
    dqraw1, dkraw1, dqnw, dknw = rows_call(
        functools.partial(vjp_rows(fox_qk_fn, 2, (True, True)), n_row=2, n_ct=2),
        [(pm1, D_MODEL, 0), (pm1, D_MODEL, 1), dq1, dk1], [qnw, knw],
        [(D_MODEL, BF16)] * 2, [(1, HEAD_DIM)] * 2, tm=256, name="fox_qk_bwd")
    dfcum = dfq + jnp.pad(dfk[:, 0, :].T, ((0, 0), (0, LANES - N_HEADS)))
    dps1, dfbias = rows_call(
        functools.partial(vjp_rows(fox_fcum_fn, 1, (True,)), n_row=1, n_ct=1),
        [ps1, dfcum], [fbias], [(LANES, F32)], [(1, LANES)], tm=t, name="fox_fcum_bwd")
    dpm1 = jnp.concatenate([dqraw1, dkraw1, dv1, dgate1, dqm1], axis=1)
    dx2, dn1w1 = _in_proj_dx(dpm1, dps1, fox_main, fox_f, x2, dx3, n1w1, "1")
    dwmain1, dwsmall1 = _in_proj_dw(h1, dpm1, dps1, "1")
    g_fox = in_proj_pieces(dwmain1, dwsmall1, N_HEADS, FOX_IN)
    sibf = sibling_rider([g_fox])

    dx1, dw1_0, dw2_0, dn2w0, sib1, sib2 = _mlp_bwd(dx2, mlp_res0, n2w0, w1_0, w2_0, 0, riders=[sibf])
    ride_fox_g = chips_rider(chip_sums(["fox_w_in"], [g_fox], sibf.results))
    dcat0 = matmul(dx1, w_out0, tb=True, name="wout_dx_0", riders=[sib1])
    dwo_0 = matmul(cat0, dx1, ta=True, out_dtype=BF16, name="wout_dw_0").reshape(N_DEV, OUT_IN // N_DEV, D_MODEL)
    sibo = sibling_rider([dwo_0])
    do0, dz0, dqm0, donw, dmqw0, dmk0, dmv0 = rows_call(
        functools.partial(vjp_rows(dn_out_fn, 3, (True, True, True, True)), n_row=3, n_ct=1),
        [o0, (pm0, D_MODEL, 3), (pm0, MEM_WIDTH, 8), dcat0], [onw, mqw0, mk, mv],
        [((N_HEADS, HEAD_DIM), F32), (D_MODEL, BF16), (MEM_WIDTH, BF16)],
        [(1, HEAD_DIM), (1, HEAD_DIM), (n_mem, MEM_WIDTH), (n_mem, MEM_WIDTH)], tm=256, name="dn_out_bwd", riders=[sibo])
    h_l0 = chip_sums(["w_mlp2_0", "w_mlp1_0", "w_out_0"], [dw2_0, dw1_0, dwo_0], sib2.results + sib1.results + sibo.results)
    ride_l0_mlp2, ride_l0_rest = chips_rider(h_l0[:1]), chips_rider(h_l0[1:])
    dmnw, dwkv, dmknw = memkv_bwd(mem, mnw, w_kv, mknw, [dmk0, dmk1], [dmv0, dmv1], name="memkv_bwd")
    g_kv = dwkv.reshape(N_DEV, D_MODEL // N_DEV, D_MODEL)
    sibk = sibling_rider([g_kv])
    dq_s, dk_s, dg_s, du0, dw0, dqk0 = delta_seq_bwd(q0, k0, gates, u0, w0, qk0, s_start, do0, name="delta_seq_bwd",
                                                     riders=[ride_fox_g, sibk])
    ride_kv_g = chips_rider(chip_sums(["w_mem_kv"], [g_kv], sibk.results))
    dq0, dk0, dv0, dgates = delta_intra_bwd(q0, k0, v0, gates, du0, dw0, dqk0, dq_s, dk_s, dg_s,
                                            name="delta_intra_bwd", riders=[ride_l0_mlp2, ride_kv_g])
    dxq, dxk, dxv, dcq, dck, dcv = dn_prep_bwd(pm0, conv_w, dq0, dk0, dv0, name="dn_prep_bwd", riders=[ride_l0_rest])
    dconv = jnp.concatenate([dcq, dck, dcv], axis=1)
    dps0, dalog, ddtb = rows_call(
        functools.partial(vjp_rows(dn_gates_fn, 1, (True, True)), n_row=1, n_ct=1),
        [ps0, dgates], [alog, dtb], [(LANES, F32)], [(1, LANES)] * 2, tm=512, name="dn_gates_bwd")
    dpm0 = jnp.concatenate([dxq, dxk, dxv, dz0, dqm0], axis=1)
    dwmain0, dwsmall0 = _in_proj_dw(h0, dpm0, dps0, "0")
    g_dn = in_proj_pieces(dwmain0, dwsmall0, 2 * N_HEADS, DN_IN)
    g_conv = dconv.reshape(CONV_WIDTH, N_DEV, -1).transpose(1, 0, 2).astype(BF16)
    sibd = sibling_rider([g_dn, g_conv])
    grad_x, dn1w0 = _in_proj_dx(dpm0, dps0, dn_main, dn_ab, x0, dx1, n1w0, "0", riders=[sibd])

    g["mem_norm_w"] = dmnw[0]
    g["mem_k_norm_w"] = dmknw[0]
    g["norm1_w"] = jnp.concatenate([dn1w0, dn1w1], axis=0)
    g["dn_a_log"] = dalog[:, :N_HEADS]
    g["dn_dt_bias"] = ddtb[:, :N_HEADS]
    g["dn_o_norm_w"] = donw
    g["fox_f_bias"] = dfbias[:, :N_HEADS]
    g["fox_q_norm_w"] = dqnw
    g["fox_k_norm_w"] = dknw
    g["memq_norm_w"] = jnp.concatenate([dmqw0, dmqw1], axis=0)
    g["norm2_w"] = jnp.concatenate([dn2w0, dn2w1], axis=0)

    ride_last = chips_rider(chip_sums(["dn_w_in", "dn_conv_w"], [g_dn, g_conv], sibd.results))
    ride_small = gather_rider([pack_small(g, last=loss)])
    run_riders([ride_last, ride_small], name="grads_to_chips_last")

    parts = {
        "w_mlp1": [ride_l0_rest.results[0], ride_l1.results[1]],
        "w_mlp2": [ride_l0_mlp2.results[0], ride_l1.results[0]],
        "w_out": [ride_l0_rest.results[1], ride_l1.results[2]],
        "fox_w_in": [ride_fox_g.results[0]], "w_mem_kv": [ride_kv_g.results[0]],
        "dn_w_in": [ride_last.results[0]], "dn_conv_w": [ride_last.results[1]],
    }
    out = {n: adamw(parts[n], w[n], m[n], v[n], name=f"adamw_{n}") for n, _, _ in BIG}
    small, loss = adamw_small(ride_small.results[0], w, m, v, name="adamw_small")
    return loss, grad_x, out, small


WEIGHTS = ["mem_norm_w", "w_mem_kv", "mem_k_norm_w", "norm1_w", "dn_w_in", "dn_conv_w", "dn_a_log", "dn_dt_bias",
           "dn_o_norm_w", "fox_w_in", "fox_f_bias", "fox_q_norm_w", "fox_k_norm_w", "memq_norm_w", "w_out", "norm2_w",
           "w_mlp1", "w_mlp2"]
DN_IN = 4 * D_MODEL + 2 * N_HEADS + MEM_WIDTH
FOX_IN = 4 * D_MODEL + N_HEADS + MEM_WIDTH
GATE_END = 4 * D_MODEL
OUT_IN = D_MODEL + MEM_WIDTH
BIG = [("w_mem_kv", D_MODEL // N_DEV, D_MODEL), ("dn_w_in", D_MODEL, DN_IN // N_DEV), ("fox_w_in", D_MODEL, FOX_IN // N_DEV),
       ("dn_conv_w", CONV_WIDTH, 3 * D_MODEL // N_DEV), ("w_out", 2 * OUT_IN // N_DEV, D_MODEL),
       ("w_mlp1", 2 * D_MODEL, FF_PIECE), ("w_mlp2", 2 * FF_PIECE, D_MODEL)]
SMALL_TILE = 8 * LANES
SMALL = [(name, shape, -(-math.prod(shape) // SMALL_TILE) * SMALL_TILE) for name, shape in [
    ("mem_norm_w", (D_MODEL,)), ("mem_k_norm_w", (HEAD_DIM,)), ("norm1_w", (2, D_MODEL)), ("dn_a_log", (1, N_HEADS)),
    ("dn_dt_bias", (1, N_HEADS)), ("dn_o_norm_w", (1, HEAD_DIM)), ("fox_f_bias", (1, N_HEADS)),
    ("fox_q_norm_w", (1, HEAD_DIM)), ("fox_k_norm_w", (1, HEAD_DIM)), ("memq_norm_w", (2, HEAD_DIM)), ("norm2_w", (2, D_MODEL))]]
SMALL_ROWS = sum(ln for _, _, ln in SMALL) // LANES + 8


def pack_small(p, last=None):
    def rows(a, ln):
        a = a.reshape(-1)
        return (a if a.shape[0] == ln else jnp.pad(a, (0, ln - a.shape[0]))).reshape(-1, LANES)

    used = sum(ln for _, _, ln in SMALL) // LANES
    tail = jnp.zeros(((SMALL_ROWS - used) * LANES,), F32)
    if last is not None:
        tail = jnp.concatenate([tail[:-1], last.reshape(1)])
    return jnp.concatenate([rows(p[n], ln) for n, _, ln in SMALL] + [tail.reshape(-1, LANES)], axis=0)


def in_proj_weights(gathered, width, n_small):
    full = gathered.transpose(1, 0, 2).reshape(D_MODEL, width)
    main = jnp.concatenate([full[:, :GATE_END], full[:, GATE_END + n_small:]], axis=1)
    return main, jnp.pad(full[:, GATE_END:GATE_END + n_small], ((0, 0), (0, LANES - n_small)))


def in_proj_pieces(d_main, d_small, n_small, width):
    full = jnp.concatenate([d_main[:, :GATE_END], d_small[:, :n_small], d_main[:, GATE_END:]], axis=1)
    return full.reshape(D_MODEL, N_DEV, width // N_DEV).transpose(1, 0, 2)


def _adamw_update(g, w, m, v):
    m_new = ADAM_B1 * m + (1.0 - ADAM_B1) * g
    v_new = ADAM_B2 * v + (1.0 - ADAM_B2) * jnp.square(g)
    m_hat = m_new / (1.0 - ADAM_B1 ** ADAM_STEP)
    v_hat = v_new / (1.0 - ADAM_B2 ** ADAM_STEP)
    return -ADAM_LR * (m_hat / (jnp.sqrt(v_hat) + ADAM_EPS) + ADAM_WD * w), m_new, v_new


def adamw(parts, w, m, v, *, name):
    layers = len(parts)
    assert layers == (w.shape[0] if w.ndim == 3 else 1)
    n, _, cols = parts[0].shape
    rows = w.shape[-2]
    tile = _pick(rows, (512, 256, 128))
    steps = rows // tile

    def total(p_ref):
        g = p_ref[0].astype(F32)
        for i in range(1, n):
            g = g + p_ref[i].astype(F32)
        return g

    def body(*refs):
        p_refs, (w_ref, m_ref, v_ref, g_ref, d_ref, mo_ref, vo_ref) = refs[:layers], refs[layers:]
        g = total(p_refs[0])
        for j in range(1, layers):
            g = jnp.where(pl.program_id(0) == j, total(p_refs[j]), g)
        g_ref[...] = g
        d_ref[...], mo_ref[...], vo_ref[...] = _adamw_update(g, w_ref[...], m_ref[...], v_ref[...])

    if w.ndim == 3:
        spec = pl.BlockSpec((None, tile, cols), lambda l, i: (l, i, 0))
    else:
        spec = pl.BlockSpec((tile, cols), lambda l, i: (i, 0))
    def parts_spec(j):
        return pl.BlockSpec((n, tile, cols), lambda l, i: (0, jnp.where(l == j, i, jnp.where(l > j, steps - 1, 0)), 0))

    return pl.pallas_call(
        body, name=name, grid=(layers, steps),
        in_specs=[parts_spec(j) for j in range(layers)] + [spec, spec, spec], out_specs=[spec] * 4,
        out_shape=[jax.ShapeDtypeStruct(w.shape, F32)] * 4, compiler_params=_params(("parallel", "parallel")),
    )(*parts, w, m, v)


def adamw_small(parts, w, m, v, *, name):
    def view(a):
        return a.reshape(-1, LANES) if a.size % LANES == 0 else a.reshape(1, a.size)

    k = len(SMALL)
    ins = [view(d[n]) for d in (w, m, v) for n, _, _ in SMALL]

    def body(p_ref, *refs):
        w_refs, m_refs, v_refs, outs, g_ref = refs[:k], refs[k:2 * k], refs[2 * k:3 * k], refs[3 * k:-1], refs[-1]
        g_all = p_ref[0]
        for i in range(1, N_DEV):
            g_all = g_all + p_ref[i]
        g_ref[...] = g_all
        row = 0
        for i, (_, _, ln) in enumerate(SMALL):
            r, c = w_refs[i].shape
            g = g_ref[row:row + r, 0:c]
            outs[4 * i][...] = g
            outs[4 * i + 1][...], outs[4 * i + 2][...], outs[4 * i + 3][...] = _adamw_update(
                g, w_refs[i][...], m_refs[i][...], v_refs[i][...])
            row += ln // LANES
        outs[-1][...] = g_ref[SMALL_ROWS - 1:SMALL_ROWS, LANES - 1:LANES]

    out_shape = [jax.ShapeDtypeStruct(a.shape, F32) for a in ins[:k] for _ in range(4)] + [jax.ShapeDtypeStruct((1, 1), F32)]
    res = pl.pallas_call(body, name=name, out_shape=out_shape,
                         scratch_shapes=[pltpu.VMEM((SMALL_ROWS, LANES), F32)])(parts, *ins)
    small = {n: [o.reshape(sh) for o in res[4 * i:4 * i + 4]] for i, (n, sh, _) in enumerate(SMALL)}
    return small, res[-1][0, 0]


def kernel(x, mem, mem_norm_w, w_mem_kv, mem_k_norm_w, norm1_w, dn_w_in, dn_conv_w, dn_a_log, dn_dt_bias, dn_o_norm_w, fox_w_in, fox_f_bias, fox_q_norm_w, fox_k_norm_w, memq_norm_w, w_out, norm2_w, w_mlp1, w_mlp2, loss_target, m_mem_norm_w, m_w_mem_kv, m_mem_k_norm_w, m_norm1_w, m_dn_w_in, m_dn_conv_w, m_dn_a_log, m_dn_dt_bias, m_dn_o_norm_w, m_fox_w_in, m_fox_f_bias, m_fox_q_norm_w, m_fox_k_norm_w, m_memq_norm_w, m_w_out, m_norm2_w, m_w_mlp1, m_w_mlp2, v_mem_norm_w, v_w_mem_kv, v_mem_k_norm_w, v_norm1_w, v_dn_w_in, v_dn_conv_w, v_dn_a_log, v_dn_dt_bias, v_dn_o_norm_w, v_fox_w_in, v_fox_f_bias, v_fox_q_norm_w, v_fox_k_norm_w, v_memq_norm_w, v_w_out, v_norm2_w, v_w_mlp1, v_w_mlp2):
    p = dict(mem_norm_w=mem_norm_w, w_mem_kv=w_mem_kv, mem_k_norm_w=mem_k_norm_w, norm1_w=norm1_w, dn_w_in=dn_w_in,
             dn_conv_w=dn_conv_w, dn_a_log=dn_a_log, dn_dt_bias=dn_dt_bias, dn_o_norm_w=dn_o_norm_w, fox_w_in=fox_w_in,
             fox_f_bias=fox_f_bias, fox_q_norm_w=fox_q_norm_w, fox_k_norm_w=fox_k_norm_w, memq_norm_w=memq_norm_w,
             w_out=w_out, norm2_w=norm2_w, w_mlp1=w_mlp1, w_mlp2=w_mlp2)
    pm = dict(mem_norm_w=m_mem_norm_w, w_mem_kv=m_w_mem_kv, mem_k_norm_w=m_mem_k_norm_w, norm1_w=m_norm1_w,
              dn_w_in=m_dn_w_in, dn_conv_w=m_dn_conv_w, dn_a_log=m_dn_a_log, dn_dt_bias=m_dn_dt_bias,
              dn_o_norm_w=m_dn_o_norm_w, fox_w_in=m_fox_w_in, fox_f_bias=m_fox_f_bias, fox_q_norm_w=m_fox_q_norm_w,
              fox_k_norm_w=m_fox_k_norm_w, memq_norm_w=m_memq_norm_w, w_out=m_w_out, norm2_w=m_norm2_w, w_mlp1=m_w_mlp1,
              w_mlp2=m_w_mlp2)
    pv = dict(mem_norm_w=v_mem_norm_w, w_mem_kv=v_w_mem_kv, mem_k_norm_w=v_mem_k_norm_w, norm1_w=v_norm1_w,
              dn_w_in=v_dn_w_in, dn_conv_w=v_dn_conv_w, dn_a_log=v_dn_a_log, dn_dt_bias=v_dn_dt_bias,
              dn_o_norm_w=v_dn_o_norm_w, fox_w_in=v_fox_w_in, fox_f_bias=v_fox_f_bias, fox_q_norm_w=v_fox_q_norm_w,
              fox_k_norm_w=v_fox_k_norm_w, memq_norm_w=v_memq_norm_w, w_out=v_w_out, norm2_w=v_norm2_w, w_mlp1=v_w_mlp1,
              w_mlp2=v_w_mlp2)

    loss, grad_x, results, small = local_step(x[0], mem[0], loss_target[0], p, pm, pv)
    groups = [{n: r[i] for n, r in {**small, **results}.items()} for i in range(4)]
    return (loss, grad_x[None], *[grp[n] for grp in groups for n in WEIGHTS])
```
